```python
import math
import jax, jax.numpy as jnp
from jax import lax
import numpy as np

D_MODEL = 2048
BATCH = 8
SEQ = 2048
DEPTH = 1

MIX_WIDTH = D_MODEL
GDN_WIDTH = MIX_WIDTH // 2
POOL_WIDTH = MIX_WIDTH - GDN_WIDTH
GDN_HEAD_DIM = 128
GDN_HEADS = GDN_WIDTH // GDN_HEAD_DIM
CONV_K = 4
CHUNK = 64
POOL_WINDOWS = (2, 4, 8, 16)
POOL_GROUPS = len(POOL_WINDOWS)
POOL_GROUP_DIM = POOL_WIDTH // POOL_GROUPS
MEM_LEN = 256
XATTN_HEADS = 4
XATTN_HEAD_DIM = D_MODEL // XATTN_HEADS
D_FF = 4 * D_MODEL
IN_COLS = 4 * GDN_WIDTH + 2 * GDN_HEADS + POOL_WIDTH
DEEPNORM_ALPHA = (2.0 * DEPTH) ** 0.25
DEEPNORM_BETA = (8.0 * DEPTH) ** -0.25
LN_EPS = 1e-5
NORM_EPS = 1e-6

kernel_name = "hybrid_gdn_pool_deepnorm_layer"


def layer_norm(x, g, b):
    xf = x.astype(jnp.float32)
    mu = jnp.mean(xf, axis=-1, keepdims=True)
    xc = xf - mu
    var = jnp.mean(xc * xc, axis=-1, keepdims=True)
    y = xc * lax.rsqrt(var + LN_EPS) * g.astype(jnp.float32) + b.astype(jnp.float32)
    return y.astype(x.dtype)


def l2norm(x):
    return x * lax.rsqrt(jnp.sum(x * x, axis=-1, keepdims=True) + NORM_EPS)


def causal_dwconv(x, w):
    c = x.shape[-1]
    return lax.conv_general_dilated(
        x, w.astype(x.dtype)[:, None, :], window_strides=(1,), padding=[(CONV_K - 1, 0)],
        dimension_numbers=("NWC", "WIO", "NWC"), feature_group_count=c)


def chunk_gated_delta_rule(q, k, v, g, beta):
    bsz, t_len, h, dk = q.shape
    dv = v.shape[-1]
    n = t_len // CHUNK

    def to_chunks(u):
        return u.reshape(bsz, n, CHUNK, h, u.shape[-1]).transpose(1, 0, 3, 2, 4)

    q = to_chunks(q * (dk ** -0.5))
    k = to_chunks(k)
    v = to_chunks(v)
    g = g.reshape(bsz, n, CHUNK, h).transpose(1, 0, 3, 2)
    beta = beta.reshape(bsz, n, CHUNK, h).transpose(1, 0, 3, 2)
    g = jnp.cumsum(g, axis=-1)

    idx = jnp.arange(CHUNK)
    lower_incl = idx[:, None] >= idx[None, :]
    strict = idx[:, None] > idx[None, :]
    diff = g[..., :, None] - g[..., None, :]
    decay = jnp.where(lower_incl, jnp.exp(jnp.where(lower_incl, diff, 0.0)), 0.0)

    k_beta = k * beta[..., None]
    v_beta = v * beta[..., None]
    L = jnp.where(strict, jnp.einsum("nbhcd,nbhmd->nbhcm", k_beta, k) * decay, 0.0)
    eye = jnp.eye(CHUNK, dtype=jnp.float32)
    rhs = jnp.concatenate([v_beta, k_beta * jnp.exp(g)[..., None]], axis=-1)
    sol = lax.linalg.triangular_solve(eye + L, rhs, left_side=True, lower=True, unit_diagonal=True)
    u, w = sol[..., :dv], sol[..., dv:]
    attn_intra = jnp.where(lower_incl, jnp.einsum("nbhcd,nbhmd->nbhcm", q, k) * decay, 0.0)

    def step(state, inp):
        q_c, k_c, u_c, w_c, g_c, a_c = inp
        v_new = u_c - jnp.einsum("bhck,bhkv->bhcv", w_c, state)
        o = (jnp.einsum("bhck,bhkv->bhcv", q_c * jnp.exp(g_c)[..., None], state)
             + jnp.einsum("bhcm,bhmv->bhcv", a_c, v_new))
        g_last = g_c[..., -1]
        k_dec = k_c * jnp.exp(g_last[..., None] - g_c)[..., None]
        state = state * jnp.exp(g_last)[..., None, None] + jnp.einsum("bhck,bhcv->bhkv", k_dec, v_new)
        return state, o

    s0 = jnp.zeros((bsz, h, dk, dv), jnp.float32)
    _, o = lax.scan(step, s0, (q, k, u, w, g, attn_intra))
    return o.transpose(1, 0, 3, 2, 4).reshape(bsz, t_len, h, dv)


def gated_deltanet(qkv, z, b, a, conv_w, a_log, dt_bias, norm_w):
    bsz, t_len, _ = qkv.shape
    qkv = jax.nn.silu(causal_dwconv(qkv, conv_w)).astype(jnp.float32)
    q, k, v = jnp.split(qkv, 3, axis=-1)
    shp = (bsz, t_len, GDN_HEADS, GDN_HEAD_DIM)
    q = l2norm(q.reshape(shp))
    k = l2norm(k.reshape(shp))
    v = v.reshape(shp)
    beta = jax.nn.sigmoid(b.astype(jnp.float32))
    g = -jnp.exp(a_log.astype(jnp.float32)) * jax.nn.softplus(
        a.astype(jnp.float32) + dt_bias.astype(jnp.float32))
    o = chunk_gated_delta_rule(q, k, v, g, beta)
    o = o * lax.rsqrt(jnp.mean(o * o, axis=-1, keepdims=True) + NORM_EPS) * norm_w.astype(jnp.float32)
    o = o * jax.nn.silu(z.astype(jnp.float32).reshape(shp))
    return o.reshape(bsz, t_len, GDN_WIDTH).astype(z.dtype)


def multiscale_pool(p, pool_w, pool_scale):
    bsz, t_len, _ = p.shape
    pg = p.astype(jnp.float32).reshape(bsz, t_len, POOL_GROUPS, POOL_GROUP_DIM)
    cs = jnp.cumsum(pg, axis=1)
    pos = jnp.arange(t_len)
    means = []
    for gi, win in enumerate(POOL_WINDOWS):
        c = cs[:, :, gi]
        lag = jnp.pad(c[:, : t_len - win], ((0, 0), (win, 0), (0, 0)))
        cnt = jnp.minimum(pos + 1, win).astype(jnp.float32)[None, :, None]
        means.append((c - lag) / cnt)
    pooled = jnp.stack(means, axis=2) - pg
    mixed = jnp.einsum("btgc,gcd->btgd", pooled.astype(p.dtype), pool_w)
    return mixed.reshape(bsz, t_len, POOL_WIDTH) * pool_scale


def memory_cross_attention(h, mem, wq, wk, wv, wo):
    bsz, t_len, _ = h.shape
    q = (h @ wq).reshape(bsz, t_len, XATTN_HEADS, XATTN_HEAD_DIM)
    k = (mem @ wk).reshape(bsz, mem.shape[1], XATTN_HEADS, XATTN_HEAD_DIM)
    v = (mem @ wv).reshape(bsz, mem.shape[1], XATTN_HEADS, XATTN_HEAD_DIM)
    s = jnp.einsum("bqhd,bmhd->bhqm", q, k).astype(jnp.float32) * (XATTN_HEAD_DIM ** -0.5)
    p = jax.nn.softmax(s, axis=-1).astype(v.dtype)
    o = jnp.einsum("bhqm,bmhd->bqhd", p, v).reshape(bsz, t_len, D_MODEL)
    return o @ wo


def _fwd_setup_inputs(seed: int = 0) -> dict:
    key = jax.random.key(seed)
    ks = jax.random.split(key, 24)
    f32 = jnp.float32
    nrm = lambda k, shape, scale: jax.random.normal(k, shape, f32) * scale
    x = nrm(ks[0], (BATCH, SEQ, D_MODEL), 1.0)
    mem = nrm(ks[1], (BATCH, MEM_LEN, D_MODEL), 1.0)
    w_in = nrm(ks[2], (DEPTH, D_MODEL, IN_COLS), D_MODEL ** -0.5)
    conv_w = nrm(ks[3], (DEPTH, CONV_K, 3 * GDN_WIDTH), CONV_K ** -0.5)
    a_log = jnp.log(jax.random.uniform(ks[4], (DEPTH, GDN_HEADS), f32, 1.0, 16.0))
    dt = jnp.exp(jax.random.uniform(ks[5], (DEPTH, GDN_HEADS), f32, math.log(1e-3), math.log(1e-1)))
    dt_bias = dt + jnp.log(-jnp.expm1(-dt))
    gdn_norm_w = 1.0 + nrm(ks[6], (DEPTH, GDN_HEAD_DIM), 0.02)
    pool_w = nrm(ks[7], (DEPTH, POOL_GROUPS, POOL_GROUP_DIM, POOL_GROUP_DIM), POOL_GROUP_DIM ** -0.5)
    pool_scale = 1.0 + nrm(ks[8], (DEPTH, POOL_WIDTH), 0.1)
    w_out = nrm(ks[9], (DEPTH, MIX_WIDTH, D_MODEL), MIX_WIDTH ** -0.5 * DEEPNORM_BETA)
    ln1_g = 1.0 + nrm(ks[10], (DEPTH, D_MODEL), 0.02)
    ln1_b = nrm(ks[11], (DEPTH, D_MODEL), 0.02)
    xq_w = nrm(ks[12], (DEPTH, D_MODEL, D_MODEL), D_MODEL ** -0.5)
    xk_w = nrm(ks[13], (DEPTH, D_MODEL, D_MODEL), D_MODEL ** -0.5)
    xv_w = nrm(ks[14], (DEPTH, D_MODEL, D_MODEL), D_MODEL ** -0.5)
    xo_w = nrm(ks[15], (DEPTH, D_MODEL, D_MODEL), D_MODEL ** -0.5 * DEEPNORM_BETA)
    ln2_g = 1.0 + nrm(ks[16], (DEPTH, D_MODEL), 0.02)
    ln2_b = nrm(ks[17], (DEPTH, D_MODEL), 0.02)
    w_up = nrm(ks[18], (DEPTH, D_MODEL, D_FF), D_MODEL ** -0.5)
    w_down = nrm(ks[19], (DEPTH, D_FF, D_MODEL), D_FF ** -0.5 * DEEPNORM_BETA)
    ln3_g = 1.0 + nrm(ks[20], (DEPTH, D_MODEL), 0.02)
    ln3_b = nrm(ks[21], (DEPTH, D_MODEL), 0.02)
    return {"x": x, "mem": mem, "w_in": w_in, "conv_w": conv_w, "a_log": a_log, "dt_bias": dt_bias,
            "gdn_norm_w": gdn_norm_w, "pool_w": pool_w, "pool_scale": pool_scale, "w_out": w_out,
            "ln1_g": ln1_g, "ln1_b": ln1_b, "xq_w": xq_w, "xk_w": xk_w, "xv_w": xv_w, "xo_w": xo_w,
            "ln2_g": ln2_g, "ln2_b": ln2_b, "w_up": w_up, "w_down": w_down, "ln3_g": ln3_g, "ln3_b": ln3_b}


def _fwd_reference(x, mem, w_in, conv_w, a_log, dt_bias, gdn_norm_w, pool_w, pool_scale, w_out,
              ln1_g, ln1_b, xq_w, xk_w, xv_w, xo_w, ln2_g, ln2_b, w_up, w_down, ln3_g, ln3_b):
    W, H = GDN_WIDTH, GDN_HEADS
    h = x
    for l in range(DEPTH):
        proj = h @ w_in[l]
        qkv = proj[..., : 3 * W]
        z = proj[..., 3 * W: 4 * W]
        b = proj[..., 4 * W: 4 * W + H]
        a = proj[..., 4 * W + H: 4 * W + 2 * H]
        p = proj[..., 4 * W + 2 * H:]
        o_gdn = gated_deltanet(qkv, z, b, a, conv_w[l], a_log[l], dt_bias[l], gdn_norm_w[l])
        o_pool = multiscale_pool(p, pool_w[l], pool_scale[l])
        mix = jnp.concatenate([o_gdn, o_pool], axis=-1) @ w_out[l]
        h = layer_norm(DEEPNORM_ALPHA * h + mix, ln1_g[l], ln1_b[l])
        xa = memory_cross_attention(h, mem, xq_w[l], xk_w[l], xv_w[l], xo_w[l])
        h = layer_norm(DEEPNORM_ALPHA * h + xa, ln2_g[l], ln2_b[l])
        ff = jnp.square(jax.nn.relu(h @ w_up[l])) @ w_down[l]
        h = layer_norm(DEEPNORM_ALPHA * h + ff, ln3_g[l], ln3_b[l])
    return h


import jax as _jax
import jax.numpy as _jnp

TWIN_FORMAT = 'train_step'
FWD_PARAMS = ['x', 'mem', 'w_in', 'conv_w', 'a_log', 'dt_bias', 'gdn_norm_w', 'pool_w', 'pool_scale', 'w_out', 'ln1_g', 'ln1_b', 'xq_w', 'xk_w', 'xv_w', 'xo_w', 'ln2_g', 'ln2_b', 'w_up', 'w_down', 'ln3_g', 'ln3_b']
TWIN_WEIGHTS = ['w_in', 'conv_w', 'a_log', 'dt_bias', 'gdn_norm_w', 'pool_w', 'pool_scale', 'w_out', 'ln1_g', 'ln1_b', 'xq_w', 'xk_w', 'xv_w', 'xo_w', 'ln2_g', 'ln2_b', 'w_up', 'w_down', 'ln3_g', 'ln3_b']
TWIN_DIFF_INPUT = 'x'
TWIN_INPUTS = ['x', 'mem', 'w_in', 'conv_w', 'a_log', 'dt_bias', 'gdn_norm_w', 'pool_w', 'pool_scale', 'w_out', 'ln1_g', 'ln1_b', 'xq_w', 'xk_w', 'xv_w', 'xo_w', 'ln2_g', 'ln2_b', 'w_up', 'w_down', 'ln3_g', 'ln3_b', 'loss_target', 'm_w_in', 'm_conv_w', 'm_a_log', 'm_dt_bias', 'm_gdn_norm_w', 'm_pool_w', 'm_pool_scale', 'm_w_out', 'm_ln1_g', 'm_ln1_b', 'm_xq_w', 'm_xk_w', 'm_xv_w', 'm_xo_w', 'm_ln2_g', 'm_ln2_b', 'm_w_up', 'm_w_down', 'm_ln3_g', 'm_ln3_b', 'v_w_in', 'v_conv_w', 'v_a_log', 'v_dt_bias', 'v_gdn_norm_w', 'v_pool_w', 'v_pool_scale', 'v_w_out', 'v_ln1_g', 'v_ln1_b', 'v_xq_w', 'v_xk_w', 'v_xv_w', 'v_xo_w', 'v_ln2_g', 'v_ln2_b', 'v_w_up', 'v_w_down', 'v_ln3_g', 'v_ln3_b']
TWIN_OUTPUTS = ['loss', 'grad_x', 'grad_w_in', 'grad_conv_w', 'grad_a_log', 'grad_dt_bias', 'grad_gdn_norm_w', 'grad_pool_w', 'grad_pool_scale', 'grad_w_out', 'grad_ln1_g', 'grad_ln1_b', 'grad_xq_w', 'grad_xk_w', 'grad_xv_w', 'grad_xo_w', 'grad_ln2_g', 'grad_ln2_b', 'grad_w_up', 'grad_w_down', 'grad_ln3_g', 'grad_ln3_b', 'delta_w_in', 'delta_conv_w', 'delta_a_log', 'delta_dt_bias', 'delta_gdn_norm_w', 'delta_pool_w', 'delta_pool_scale', 'delta_w_out', 'delta_ln1_g', 'delta_ln1_b', 'delta_xq_w', 'delta_xk_w', 'delta_xv_w', 'delta_xo_w', 'delta_ln2_g', 'delta_ln2_b', 'delta_w_up', 'delta_w_down', 'delta_ln3_g', 'delta_ln3_b', 'new_m_w_in', 'new_m_conv_w', 'new_m_a_log', 'new_m_dt_bias', 'new_m_gdn_norm_w', 'new_m_pool_w', 'new_m_pool_scale', 'new_m_w_out', 'new_m_ln1_g', 'new_m_ln1_b', 'new_m_xq_w', 'new_m_xk_w', 'new_m_xv_w', 'new_m_xo_w', 'new_m_ln2_g', 'new_m_ln2_b', 'new_m_w_up', 'new_m_w_down', 'new_m_ln3_g', 'new_m_ln3_b', 'new_v_w_in', 'new_v_conv_w', 'new_v_a_log', 'new_v_dt_bias', 'new_v_gdn_norm_w', 'new_v_pool_w', 'new_v_pool_scale', 'new_v_w_out', 'new_v_ln1_g', 'new_v_ln1_b', 'new_v_xq_w', 'new_v_xk_w', 'new_v_xv_w', 'new_v_xo_w', 'new_v_ln2_g', 'new_v_ln2_b', 'new_v_w_up', 'new_v_w_down', 'new_v_ln3_g', 'new_v_ln3_b']
TWIN_LEAF_KINDS = {'loss': 'loss', 'grad_x': 'grad_x', 'grad_w_in': 'grad_w', 'grad_conv_w': 'grad_w', 'grad_a_log': 'grad_w', 'grad_dt_bias': 'grad_w', 'grad_gdn_norm_w': 'grad_w', 'grad_pool_w': 'grad_w', 'grad_pool_scale': 'grad_w', 'grad_w_out': 'grad_w', 'grad_ln1_g': 'grad_w', 'grad_ln1_b': 'grad_w', 'grad_xq_w': 'grad_w', 'grad_xk_w': 'grad_w', 'grad_xv_w': 'grad_w', 'grad_xo_w': 'grad_w', 'grad_ln2_g': 'grad_w', 'grad_ln2_b': 'grad_w', 'grad_w_up': 'grad_w', 'grad_w_down': 'grad_w', 'grad_ln3_g': 'grad_w', 'grad_ln3_b': 'grad_w', 'delta_w_in': 'delta_w', 'delta_conv_w': 'delta_w', 'delta_a_log': 'delta_w', 'delta_dt_bias': 'delta_w', 'delta_gdn_norm_w': 'delta_w', 'delta_pool_w': 'delta_w', 'delta_pool_scale': 'delta_w', 'delta_w_out': 'delta_w', 'delta_ln1_g': 'delta_w', 'delta_ln1_b': 'delta_w', 'delta_xq_w': 'delta_w', 'delta_xk_w': 'delta_w', 'delta_xv_w': 'delta_w', 'delta_xo_w': 'delta_w', 'delta_ln2_g': 'delta_w', 'delta_ln2_b': 'delta_w', 'delta_w_up': 'delta_w', 'delta_w_down': 'delta_w', 'delta_ln3_g': 'delta_w', 'delta_ln3_b': 'delta_w', 'new_m_w_in': 'new_m', 'new_m_conv_w': 'new_m', 'new_m_a_log': 'new_m', 'new_m_dt_bias': 'new_m', 'new_m_gdn_norm_w': 'new_m', 'new_m_pool_w': 'new_m', 'new_m_pool_scale': 'new_m', 'new_m_w_out': 'new_m', 'new_m_ln1_g': 'new_m', 'new_m_ln1_b': 'new_m', 'new_m_xq_w': 'new_m', 'new_m_xk_w': 'new_m', 'new_m_xv_w': 'new_m', 'new_m_xo_w': 'new_m', 'new_m_ln2_g': 'new_m', 'new_m_ln2_b': 'new_m', 'new_m_w_up': 'new_m', 'new_m_w_down': 'new_m', 'new_m_ln3_g': 'new_m', 'new_m_ln3_b': 'new_m', 'new_v_w_in': 'new_v', 'new_v_conv_w': 'new_v', 'new_v_a_log': 'new_v', 'new_v_dt_bias': 'new_v', 'new_v_gdn_norm_w': 'new_v', 'new_v_pool_w': 'new_v', 'new_v_pool_scale': 'new_v', 'new_v_w_out': 'new_v', 'new_v_ln1_g': 'new_v', 'new_v_ln1_b': 'new_v', 'new_v_xq_w': 'new_v', 'new_v_xk_w': 'new_v', 'new_v_xv_w': 'new_v', 'new_v_xo_w': 'new_v', 'new_v_ln2_g': 'new_v', 'new_v_ln2_b': 'new_v', 'new_v_w_up': 'new_v', 'new_v_w_down': 'new_v', 'new_v_ln3_g': 'new_v', 'new_v_ln3_b': 'new_v'}


def _forward(args):
    return _fwd_reference(*[args[k] for k in FWD_PARAMS])


def _output_shape():
    out = _jax.eval_shape(lambda: _forward(_fwd_setup_inputs(0)))
    return out.shape, out.dtype

N_MICROBATCH = 1
ADAM_LR = 0.001
ADAM_B1 = 0.9
ADAM_B2 = 0.999
ADAM_EPS = 1e-08
ADAM_WD = 0.01
ADAM_STEP = 10
PER_EXAMPLE_BATCH_AXIS = {'x': 0, 'mem': 0, 'loss_target': 0}
SHARED_INPUTS = []
_WEIGHT_DTYPES = {'w_in': _jnp.float32, 'conv_w': _jnp.float32, 'a_log': _jnp.float32, 'dt_bias': _jnp.float32, 'gdn_norm_w': _jnp.float32, 'pool_w': _jnp.float32, 'pool_scale': _jnp.float32, 'w_out': _jnp.float32, 'ln1_g': _jnp.float32, 'ln1_b': _jnp.float32, 'xq_w': _jnp.float32, 'xk_w': _jnp.float32, 'xv_w': _jnp.float32, 'xo_w': _jnp.float32, 'ln2_g': _jnp.float32, 'ln2_b': _jnp.float32, 'w_up': _jnp.float32, 'w_down': _jnp.float32, 'ln3_g': _jnp.float32, 'ln3_b': _jnp.float32}
MOMENT_SCALE = {'w_in': 1.809243e-02, 'conv_w': 1.370532e-02, 'a_log': 5.530538e-02, 'dt_bias': 5.329453e-02, 'gdn_norm_w': 5.278787e-02, 'pool_w': 2.701032e-02, 'pool_scale': 2.704190e-02, 'w_out': 3.941709e-02, 'ln1_g': 2.050628e-01, 'ln1_b': 1.493540e-01, 'xq_w': 3.369184e-03, 'xk_w': 3.378759e-03, 'xv_w': 3.859174e-03, 'xo_w': 6.403223e-03, 'ln2_g': 2.057305e-01, 'ln2_b': 1.498628e-01, 'w_up': 1.896073e-02, 'w_down': 7.150828e-02, 'ln3_g': 8.025125e+00, 'ln3_b': 1.687652e+00}


def _to_microbatches(a, axis):
    t = _jnp.moveaxis(a, axis, 0)
    t = t.reshape((N_MICROBATCH, t.shape[0] // N_MICROBATCH) + t.shape[1:])
    return _jnp.moveaxis(t, 1, axis + 1)


def setup_inputs(seed: int = 0) -> dict:
    inp = _fwd_setup_inputs(seed)
    key = _jax.random.fold_in(_jax.random.key(seed), 7919)
    shape, _ = _output_shape()
    out = dict(inp)
    out["loss_target"] = _jax.random.normal(_jax.random.fold_in(key, 0), shape, _jnp.float32)
    for i, name in enumerate(TWIN_WEIGHTS):
        w = inp[name].astype(_jnp.float32)
        if MOMENT_SCALE is None:
            s = _jnp.sqrt(_jnp.mean(_jnp.square(w)) + 1e-30)
        else:
            s = MOMENT_SCALE[name]
        km, kv = _jax.random.split(_jax.random.fold_in(key, i + 1))
        out[name] = w
        out["m_" + name] = s * _jax.random.normal(km, w.shape, _jnp.float32)
        out["v_" + name] = (s * s) * _jax.random.uniform(kv, w.shape, _jnp.float32, 0.5, 1.5)
    if N_MICROBATCH > 1:
        for name, axis in PER_EXAMPLE_BATCH_AXIS.items():
            out[name] = _to_microbatches(out[name], axis)
    return {'x': out['x'], 'mem': out['mem'], 'w_in': out['w_in'], 'conv_w': out['conv_w'], 'a_log': out['a_log'], 'dt_bias': out['dt_bias'], 'gdn_norm_w': out['gdn_norm_w'], 'pool_w': out['pool_w'], 'pool_scale': out['pool_scale'], 'w_out': out['w_out'], 'ln1_g': out['ln1_g'], 'ln1_b': out['ln1_b'], 'xq_w': out['xq_w'], 'xk_w': out['xk_w'], 'xv_w': out['xv_w'], 'xo_w': out['xo_w'], 'ln2_g': out['ln2_g'], 'ln2_b': out['ln2_b'], 'w_up': out['w_up'], 'w_down': out['w_down'], 'ln3_g': out['ln3_g'], 'ln3_b': out['ln3_b'], 'loss_target': out['loss_target'], 'm_w_in': out['m_w_in'], 'm_conv_w': out['m_conv_w'], 'm_a_log': out['m_a_log'], 'm_dt_bias': out['m_dt_bias'], 'm_gdn_norm_w': out['m_gdn_norm_w'], 'm_pool_w': out['m_pool_w'], 'm_pool_scale': out['m_pool_scale'], 'm_w_out': out['m_w_out'], 'm_ln1_g': out['m_ln1_g'], 'm_ln1_b': out['m_ln1_b'], 'm_xq_w': out['m_xq_w'], 'm_xk_w': out['m_xk_w'], 'm_xv_w': out['m_xv_w'], 'm_xo_w': out['m_xo_w'], 'm_ln2_g': out['m_ln2_g'], 'm_ln2_b': out['m_ln2_b'], 'm_w_up': out['m_w_up'], 'm_w_down': out['m_w_down'], 'm_ln3_g': out['m_ln3_g'], 'm_ln3_b': out['m_ln3_b'], 'v_w_in': out['v_w_in'], 'v_conv_w': out['v_conv_w'], 'v_a_log': out['v_a_log'], 'v_dt_bias': out['v_dt_bias'], 'v_gdn_norm_w': out['v_gdn_norm_w'], 'v_pool_w': out['v_pool_w'], 'v_pool_scale': out['v_pool_scale'], 'v_w_out': out['v_w_out'], 'v_ln1_g': out['v_ln1_g'], 'v_ln1_b': out['v_ln1_b'], 'v_xq_w': out['v_xq_w'], 'v_xk_w': out['v_xk_w'], 'v_xv_w': out['v_xv_w'], 'v_xo_w': out['v_xo_w'], 'v_ln2_g': out['v_ln2_g'], 'v_ln2_b': out['v_ln2_b'], 'v_w_up': out['v_w_up'], 'v_w_down': out['v_w_down'], 'v_ln3_g': out['v_ln3_g'], 'v_ln3_b': out['v_ln3_b']}


def _loss(weights, diff, rest, loss_target):
    with _jax.named_scope("forward"):
        args = {**rest, TWIN_DIFF_INPUT: diff, **{k: w.astype(_WEIGHT_DTYPES[k]) for k, w in weights.items()}}
        y = _forward(args)
    with _jax.named_scope("loss_head"):
        err = _jnp.square(y.astype(_jnp.float32) - loss_target)
        return 0.5 * _jnp.sum(_jnp.mean(err, axis=-1)) if err.ndim else 0.5 * err


def _adamw(w, g, m, v):
    m = ADAM_B1 * m + (1.0 - ADAM_B1) * g
    v = ADAM_B2 * v + (1.0 - ADAM_B2) * _jnp.square(g)
    m_hat = m / (1.0 - ADAM_B1 ** ADAM_STEP)
    v_hat = v / (1.0 - ADAM_B2 ** ADAM_STEP)
    delta = -ADAM_LR * (m_hat / (_jnp.sqrt(v_hat) + ADAM_EPS) + ADAM_WD * w)
    return delta, m, v


def reference(x, mem, w_in, conv_w, a_log, dt_bias, gdn_norm_w, pool_w, pool_scale, w_out, ln1_g, ln1_b, xq_w, xk_w, xv_w, xo_w, ln2_g, ln2_b, w_up, w_down, ln3_g, ln3_b, loss_target, m_w_in, m_conv_w, m_a_log, m_dt_bias, m_gdn_norm_w, m_pool_w, m_pool_scale, m_w_out, m_ln1_g, m_ln1_b, m_xq_w, m_xk_w, m_xv_w, m_xo_w, m_ln2_g, m_ln2_b, m_w_up, m_w_down, m_ln3_g, m_ln3_b, v_w_in, v_conv_w, v_a_log, v_dt_bias, v_gdn_norm_w, v_pool_w, v_pool_scale, v_w_out, v_ln1_g, v_ln1_b, v_xq_w, v_xk_w, v_xv_w, v_xo_w, v_ln2_g, v_ln2_b, v_w_up, v_w_down, v_ln3_g, v_ln3_b):
    given = dict(x=x, mem=mem, w_in=w_in, conv_w=conv_w, a_log=a_log, dt_bias=dt_bias, gdn_norm_w=gdn_norm_w, pool_w=pool_w, pool_scale=pool_scale, w_out=w_out, ln1_g=ln1_g, ln1_b=ln1_b, xq_w=xq_w, xk_w=xk_w, xv_w=xv_w, xo_w=xo_w, ln2_g=ln2_g, ln2_b=ln2_b, w_up=w_up, w_down=w_down, ln3_g=ln3_g, ln3_b=ln3_b, loss_target=loss_target, m_w_in=m_w_in, m_conv_w=m_conv_w, m_a_log=m_a_log, m_dt_bias=m_dt_bias, m_gdn_norm_w=m_gdn_norm_w, m_pool_w=m_pool_w, m_pool_scale=m_pool_scale, m_w_out=m_w_out, m_ln1_g=m_ln1_g, m_ln1_b=m_ln1_b, m_xq_w=m_xq_w, m_xk_w=m_xk_w, m_xv_w=m_xv_w, m_xo_w=m_xo_w, m_ln2_g=m_ln2_g, m_ln2_b=m_ln2_b, m_w_up=m_w_up, m_w_down=m_w_down, m_ln3_g=m_ln3_g, m_ln3_b=m_ln3_b, v_w_in=v_w_in, v_conv_w=v_conv_w, v_a_log=v_a_log, v_dt_bias=v_dt_bias, v_gdn_norm_w=v_gdn_norm_w, v_pool_w=v_pool_w, v_pool_scale=v_pool_scale, v_w_out=v_w_out, v_ln1_g=v_ln1_g, v_ln1_b=v_ln1_b, v_xq_w=v_xq_w, v_xk_w=v_xk_w, v_xv_w=v_xv_w, v_xo_w=v_xo_w, v_ln2_g=v_ln2_g, v_ln2_b=v_ln2_b, v_w_up=v_w_up, v_w_down=v_w_down, v_ln3_g=v_ln3_g, v_ln3_b=v_ln3_b)
    weights = {n: given[n] for n in TWIN_WEIGHTS}
    shared = {n: given[n] for n in SHARED_INPUTS}
    per_example = {n: given[n] for n in ['x', 'mem']}
    grad_fn = _jax.value_and_grad(_loss, argnums=(0, 1))

    def one_microbatch(ex, loss_target):
        ex = dict(ex)
        diff = ex.pop(TWIN_DIFF_INPUT)
        return grad_fn(weights, diff, {**shared, **ex}, loss_target)

    if N_MICROBATCH == 1:
        loss, (grad_w, grad_x) = one_microbatch(per_example, given["loss_target"])
    else:
        def body(carry, xs):
            loss_sum, grad_sum = carry
            l_k, (gw_k, gx_k) = one_microbatch(xs[0], xs[1])
            with _jax.named_scope("update"):
                return (loss_sum + l_k, _jax.tree.map(_jnp.add, grad_sum, gw_k)), gx_k

        init = (_jnp.zeros((), _jnp.float32), _jax.tree.map(_jnp.zeros_like, weights))
        (loss, grad_w), grad_x = _jax.lax.scan(body, init, (per_example, given["loss_target"]))
    with _jax.named_scope("update"):
        delta_w, new_m, new_v = {}, {}, {}
        for n in TWIN_WEIGHTS:
            delta_w[n], new_m[n], new_v[n] = _adamw(weights[n], grad_w[n], given["m_" + n], given["v_" + n])
    return (loss, grad_x, *[grad_w[n] for n in TWIN_WEIGHTS], *[delta_w[n] for n in TWIN_WEIGHTS],
            *[new_m[n] for n in TWIN_WEIGHTS], *[new_v[n] for n in TWIN_WEIGHTS])
```

```python
import functools
import math

import jax
import jax.numpy as jnp
from jax import lax
from jax.experimental import pallas as pl
from jax.experimental.pallas import tpu as pltpu

F32 = jnp.float32
BF16 = jnp.bfloat16
HIGHEST = lax.Precision.HIGHEST
MESH = pl.DeviceIdType.MESH

HEAD_DIM = 128
CHUNK = 64
POOL_WINDOWS = (2, 4, 8, 16)
XATTN_HEADS = 4
ALPHA = 2.0 ** 0.25
LN_EPS = 1e-5
NORM_EPS = 1e-6
ADAM_LR, ADAM_B1, ADAM_B2, ADAM_EPS, ADAM_WD, ADAM_STEP = 0.001, 0.9, 0.999, 1e-08, 0.01, 10
N_SHARD = 4
VMEM_LIMIT = 56 * 1024 * 1024


def _params(*sem):
    return pltpu.CompilerParams(dimension_semantics=sem, vmem_limit_bytes=VMEM_LIMIT)


def _bdot(a, b, ta=False, tb=False):
    dims = (((0 if ta else 1,), (1 if tb else 0,)), ((), ()))
    return lax.dot_general(a.astype(BF16), b.astype(BF16), dims, preferred_element_type=F32)


def _hdot(a, b, ta=False, tb=False):
    dims = (((0 if ta else 1,), (1 if tb else 0,)), ((), ()))
    return lax.dot_general(a.astype(F32), b.astype(F32), dims, precision=HIGHEST, preferred_element_type=F32)


def _sigmoid(x):
    return 1.0 / (1.0 + jnp.exp(-x))


def _matmul(name, a, b, *, ta=False, tb=False, tm, tn, tk, extra=(), outs, epilogue, b_blocks=None,
            sequential=False):
    m, k_dim = (a.shape[1], a.shape[0]) if ta else a.shape
    if b_blocks and tb:
        n = b.shape[1]
        k_dim = b.shape[0] * b.shape[2]
        per = b.shape[2] // tk
        b_spec = pl.BlockSpec((None, tn, tk), lambda i, j, k: (k // per, j, k % per))
    elif b_blocks:
        n = b.shape[0] * b.shape[2]
        per = b.shape[2] // tn
        b_spec = pl.BlockSpec((None, tk, tn), lambda i, j, k: (j // per, k, j % per))
    elif tb:
        n = b.shape[0]
        b_spec = pl.BlockSpec((tn, tk), lambda i, j, k: (j, k))
    else:
        n = b.shape[1]
        b_spec = pl.BlockSpec((tk, tn), lambda i, j, k: (k, j))
    assert m % tm == 0 and n % tn == 0 and k_dim % tk == 0, (name, m, n, k_dim, tm, tn, tk)
    nk = k_dim // tk
    a_spec = pl.BlockSpec((tk, tm), lambda i, j, k: (k, i)) if ta else pl.BlockSpec((tm, tk), lambda i, j, k: (i, k))
    n_extra, n_out = len(extra), len(outs)

    def wrap(index_map):
        return lambda i, j, k: index_map(i, j)

    def body(*refs):
        a_ref, b_ref = refs[0], refs[1]
        ex = refs[2:2 + n_extra]
        out = refs[2 + n_extra:2 + n_extra + n_out]
        acc = refs[-1]
        i, k = pl.program_id(0), pl.program_id(2)

        @pl.when(k == 0)
        def _():
            acc[...] = jnp.zeros_like(acc)

        acc[...] += _bdot(a_ref[...], b_ref[...], ta, tb)

        @pl.when(k == nk - 1)
        def _():
            epilogue(acc[...], ex, out, i)

    sem = ("arbitrary",) * 3 if sequential else ("parallel", "parallel", "arbitrary")
    res = pl.pallas_call(
        body, name=name, grid=(m // tm, n // tn, nk),
        in_specs=[a_spec, b_spec] + [pl.BlockSpec(bs, wrap(im)) for _, bs, im in extra],
        out_specs=[pl.BlockSpec(bs, wrap(im)) for _, bs, im in outs],
        out_shape=[s for s, _, _ in outs],
        scratch_shapes=[pltpu.VMEM((tm, tn), F32)],
        compiler_params=_params(*sem),
    )(a, b, *[x for x, _, _ in extra])
    return res


def _tile(i, j):
    return (i, j)


def _plain(name, a, b, *, ta=False, tb=False, tm, tn, tk, out_dtype, b_blocks=None, out3=None):
    m = a.shape[1] if ta else a.shape[0]
    n = (b.shape[0] * b.shape[2]) if b_blocks else (b.shape[0] if tb else b.shape[1])

    def epi(acc, ex, out, i):
        out[0][...] = acc.astype(out_dtype)

    if out3:
        per = (n // out3) // tn
        spec = (jax.ShapeDtypeStruct((out3, m, n // out3), out_dtype), (None, tm, tn),
                lambda i, j: (j // per, i, j % per))
    else:
        spec = (jax.ShapeDtypeStruct((m, n), out_dtype), (tm, tn), _tile)
    return _matmul(name, a, b, ta=ta, tb=tb, tm=tm, tn=tn, tk=tk, outs=[spec], epilogue=epi,
                   b_blocks=b_blocks)[0]


def _ln_forward(name, a, b, res, gamma, beta, *, tm, tk, want_bf16=True):
    m, n = res.shape

    def epi(acc, ex, out, i):
        u = ALPHA * ex[0][...] + acc
        mu = jnp.mean(u, axis=-1, keepdims=True)
        xc = u - mu
        var = jnp.mean(xc * xc, axis=-1, keepdims=True)
        rstd = lax.rsqrt(var + LN_EPS)
        xhat = xc * rstd
        h = xhat * ex[1][...] + ex[2][...]
        out[0][...] = h
        out[1][...] = h.astype(BF16)
        out[2][...] = xhat
        out[3][...] = rstd

    row = lambda i, j: (i, 0)
    vec = lambda i, j: (0, 0)
    return _matmul(
        name, a, b, tm=tm, tn=n, tk=tk,
        extra=[(res, (tm, n), row), (gamma, (1, n), vec), (beta, (1, n), vec)],
        outs=[(jax.ShapeDtypeStruct((m, n), F32), (tm, n), row),
              (jax.ShapeDtypeStruct((m, n), BF16), (tm, n), row),
              (jax.ShapeDtypeStruct((m, n), F32), (tm, n), row),
              (jax.ShapeDtypeStruct((m, 1), F32), (tm, 1), row)],
        epilogue=epi)


def _ln_backward_math(dy, xhat, rstd, gamma):
    dxhat = dy * gamma
    m1 = jnp.mean(dxhat, axis=-1, keepdims=True)
    m2 = jnp.mean(dxhat * xhat, axis=-1, keepdims=True)
    du = rstd * (dxhat - m1 - xhat * m2)
    return du, jnp.sum(dy * xhat, axis=0, keepdims=True), jnp.sum(dy, axis=0, keepdims=True)


def _ln_backward(name, a, b, dres, xhat, rstd, gamma, *, tm, tk, b_blocks=None, tb=True):
    m, n = dres.shape

    def epi(acc, ex, out, i):
        dy = acc + ALPHA * ex[0][...]
        du, dg, db = _ln_backward_math(dy, ex[1][...], ex[2][...], ex[3][...])
        out[0][...] = du
        out[1][...] = du.astype(BF16)
        first = i == 0

        @pl.when(first)
        def _():
            out[2][...] = dg
            out[3][...] = db

        @pl.when(jnp.logical_not(first))
        def _():
            out[2][...] += dg
            out[3][...] += db

    row = lambda i, j: (i, 0)
    vec = lambda i, j: (0, 0)
    return _matmul(
        name, a, b, tb=tb, tm=tm, tn=n, tk=tk, b_blocks=b_blocks, sequential=True,
        extra=[(dres, (tm, n), row), (xhat, (tm, n), row), (rstd, (tm, 1), row), (gamma, (1, n), vec)],
        outs=[(jax.ShapeDtypeStruct((m, n), F32), (tm, n), row),
              (jax.ShapeDtypeStruct((m, n), BF16), (tm, n), row),
              (jax.ShapeDtypeStruct((1, n), F32), (1, n), vec),
              (jax.ShapeDtypeStruct((1, n), F32), (1, n), vec)],
        epilogue=epi)


def _shift_down(x, k):
    row = lax.broadcasted_iota(jnp.int32, x.shape, 0)
    return jnp.where(row >= k, pltpu.roll(x, k, axis=0), 0.0)


def _shift_up(x, k):
    t = x.shape[0]
    row = lax.broadcasted_iota(jnp.int32, x.shape, 0)
    return jnp.where(row < t - k, pltpu.roll(x, t - k, axis=0), 0.0)


def _conv_silu_norm(x, w, normalise):
    kk = w.shape[0]
    c = x * w[kk - 1:kk, :]
    for j in range(kk - 1):
        c = c + _shift_down(x, kk - 1 - j) * w[j:j + 1, :]
    sg = _sigmoid(c)
    s = c * sg
    r = lax.rsqrt(jnp.sum(s * s, axis=-1, keepdims=True) + NORM_EPS)
    y = jnp.where(normalise, s * r, s)
    return c, sg, s, r, y


def _gdn_pre(proj, conv_w, heads):
    t = proj.shape[0]
    kk = conv_w.shape[0]

    def body(x_ref, w_ref, o_ref):
        normalise = pl.program_id(0) < 2
        o_ref[...] = _conv_silu_norm(x_ref[...], w_ref[...], normalise)[4]

    col = lambda s, h: (0, s * heads + h)
    return pl.pallas_call(
        body, name="gdn_pre", grid=(3, heads),
        in_specs=[pl.BlockSpec((t, HEAD_DIM), col), pl.BlockSpec((kk, HEAD_DIM), col)],
        out_specs=pl.BlockSpec((t, HEAD_DIM), col),
        out_shape=jax.ShapeDtypeStruct((t, 3 * heads * HEAD_DIM), F32),
        compiler_params=_params("parallel", "parallel"),
    )(proj, conv_w)


def _gdn_pre_backward(proj, conv_w, dqkv, heads):
    t = proj.shape[0]
    kk = conv_w.shape[0]

    def body(x_ref, w_ref, dy_ref, dx_ref, dw_ref):
        normalise = pl.program_id(0) < 2
        x = x_ref[...]
        w = w_ref[...]
        dy = dy_ref[...]
        c, sg, s, r, y = _conv_silu_norm(x, w, normalise)
        ds_norm = r * (dy - y * jnp.sum(dy * y, axis=-1, keepdims=True))
        ds = jnp.where(normalise, ds_norm, dy)
        dc = ds * (sg * (1.0 + c * (1.0 - sg)))
        dx = dc * w[kk - 1:kk, :]
        rows = [None] * kk
        rows[kk - 1] = jnp.sum(dc * x, axis=0, keepdims=True)
        for j in range(kk - 1):
            lag = kk - 1 - j
            dx = dx + _shift_up(dc, lag) * w[j:j + 1, :]
            rows[j] = jnp.sum(dc * _shift_down(x, lag), axis=0, keepdims=True)
        dx_ref[...] = dx.astype(BF16)
        dw_ref[...] = jnp.concatenate(rows, axis=0)

    col = lambda s, h: (0, s * heads + h)
    return pl.pallas_call(
        body, name="gdn_pre_bwd", grid=(3, heads),
        in_specs=[pl.BlockSpec((t, HEAD_DIM), col), pl.BlockSpec((kk, HEAD_DIM), col),
                  pl.BlockSpec((t, HEAD_DIM), col)],
        out_specs=[pl.BlockSpec((t, HEAD_DIM), col), pl.BlockSpec((kk, HEAD_DIM), col)],
        out_shape=[jax.ShapeDtypeStruct((t, 3 * heads * HEAD_DIM), BF16),
                   jax.ShapeDtypeStruct((kk, 3 * heads * HEAD_DIM), F32)],
        compiler_params=_params("parallel", "parallel"),
    )(proj, conv_w, dqkv)


def _gate_vectors(a_log, dt_bias, heads):
    pad = lambda v: jnp.pad(v.astype(F32), ((0, 0), (heads, HEAD_DIM - 2 * heads)))
    return pad(jnp.exp(a_log.astype(F32))), pad(dt_bias)


def _softplus(x):
    return jnp.maximum(x, 0.0) + jnp.log(1.0 + jnp.exp(-jnp.abs(x)))


def _gates_epilogue(heads):
    def epi(acc, ex, out, i):
        lane = lax.broadcasted_iota(jnp.int32, acc.shape, 1)
        beta = _sigmoid(acc)
        g = -ex[0][...] * _softplus(acc + ex[1][...])
        out[0][...] = acc
        out[1][...] = jnp.where(lane < heads, beta, jnp.where(lane < 2 * heads, g, 0.0))
    return epi


def _gates_backward(ba, bg, dbg, ea, dtb, heads):
    t = ba.shape[0]

    def body(ba_ref, bg_ref, d_ref, ea_ref, dt_ref, dba_ref, dal_ref, ddt_ref):
        lane = lax.broadcasted_iota(jnp.int32, (t, HEAD_DIM), 1)
        bgv = bg_ref[...]
        d = d_ref[...]
        db = d * bgv * (1.0 - bgv)
        da = -d * ea_ref[...] * _sigmoid(ba_ref[...] + dt_ref[...])
        is_g = jnp.logical_and(lane >= heads, lane < 2 * heads)
        dba = jnp.where(lane < heads, db, jnp.where(is_g, da, 0.0))
        dba_ref[...] = dba.astype(BF16)
        dal_ref[...] = jnp.sum(jnp.where(is_g, d * bgv, 0.0), axis=0, keepdims=True)
        ddt_ref[...] = jnp.sum(jnp.where(is_g, da, 0.0), axis=0, keepdims=True)

    full = pl.BlockSpec((t, HEAD_DIM), lambda: (0, 0))
    vec = pl.BlockSpec((1, HEAD_DIM), lambda: (0, 0))
    return pl.pallas_call(
        body, name="gates_bwd", grid=(),
        in_specs=[full, full, full, vec, vec], out_specs=[full, vec, vec],
        out_shape=[jax.ShapeDtypeStruct((t, HEAD_DIM), BF16), jax.ShapeDtypeStruct((1, HEAD_DIM), F32),
                   jax.ShapeDtypeStruct((1, HEAD_DIM), F32)],
        compiler_params=pltpu.CompilerParams(vmem_limit_bytes=VMEM_LIMIT),
    )(ba, bg, dbg, ea, dtb)


class _Chunk:
    pass


def _chunk_local(q, k, v, bg, head, heads):
    c = _Chunk()
    n = CHUNK
    lane = lax.broadcasted_iota(jnp.int32, bg.shape, 1)
    beta = jnp.sum(jnp.where(lane == head, bg, 0.0), axis=1, keepdims=True)
    graw = jnp.sum(jnp.where(lane == head + heads, bg, 0.0), axis=1, keepdims=True)
    row = lax.broadcasted_iota(jnp.int32, (n, n), 0)
    col = lax.broadcasted_iota(jnp.int32, (n, n), 1)
    c.tri = row >= col
    c.strict = row > col
    c.trif = c.tri.astype(F32)
    eye = row == col
    c.gcb = _hdot(c.trif, jnp.broadcast_to(graw, (n, HEAD_DIM)))
    gcol = c.gcb[:, :n]
    grow = _hdot(jnp.ones((n, n), F32), jnp.where(eye, gcol, 0.0))
    c.decay = jnp.where(c.tri, jnp.exp(jnp.where(c.tri, gcol - grow, 0.0)), 0.0)
    c.eg = jnp.exp(c.gcb)
    glast = c.gcb[n - 1:n, :]
    c.egl = jnp.exp(glast)
    c.ekl = jnp.exp(glast - c.gcb)
    c.beta = beta
    c.q = q * (HEAD_DIM ** -0.5)
    c.k = k
    c.v = v
    c.kb = k * beta
    c.vb = v * beta
    c.kg = c.kb * c.eg
    c.m1 = _bdot(c.kb, k, tb=True)
    c.L = jnp.where(c.strict, c.m1 * c.decay, 0.0)
    x = -c.L
    tinv = eye.astype(F32) + x
    p = x
    for _ in range(int(math.log2(n)) - 1):
        p = _hdot(p, p)
        tinv = tinv + _hdot(tinv, p)
    c.T = tinv
    c.u = _hdot(tinv, c.vb)
    c.w = _hdot(tinv, c.kg)
    c.m2 = _bdot(c.q, k, tb=True)
    c.A = jnp.where(c.tri, c.m2 * c.decay, 0.0)
    c.qg = c.q * c.eg
    c.kdec = k * c.ekl
    return c


def _gdn_core(qkv, bg, heads):
    t = qkv.shape[0]
    nchunk = t // CHUNK

    def body(q_ref, k_ref, v_ref, bg_ref, o_ref, s_ref, state):
        head = pl.program_id(0)

        @pl.when(pl.program_id(1) == 0)
        def _():
            state[...] = jnp.zeros_like(state)

        c = _chunk_local(q_ref[...], k_ref[...], v_ref[...], bg_ref[...], head, heads)
        s0 = state[...]
        s_ref[0, 0] = s0
        v_new = c.u - _bdot(c.w, s0)
        o_ref[...] = _bdot(c.qg, s0) + _bdot(c.A, v_new)
        state[...] = s0 * c.egl + _bdot(c.kdec, v_new, ta=True)

    def sect(s):
        return pl.BlockSpec((CHUNK, HEAD_DIM), lambda h, n: (n, s * heads + h))

    return pl.pallas_call(
        body, name="gdn_core", grid=(heads, nchunk),
        in_specs=[sect(0), sect(1), sect(2), pl.BlockSpec((CHUNK, HEAD_DIM), lambda h, n: (n, 0))],
        out_specs=[pl.BlockSpec((CHUNK, HEAD_DIM), lambda h, n: (n, h)),
                   pl.BlockSpec((1, 1, HEAD_DIM, HEAD_DIM), lambda h, n: (h, n, 0, 0))],
        out_shape=[jax.ShapeDtypeStruct((t, heads * HEAD_DIM), F32),
                   jax.ShapeDtypeStruct((heads, nchunk, HEAD_DIM, HEAD_DIM), F32)],
        scratch_shapes=[pltpu.VMEM((HEAD_DIM, HEAD_DIM), F32)],
        compiler_params=_params("parallel", "arbitrary"),
    )(qkv, qkv, qkv, bg)


def _gdn_core_backward(qkv, bg, states, do, heads):
    t = qkv.shape[0]
    nchunk = t // CHUNK
    n = CHUNK

    def body(q_ref, k_ref, v_ref, bg_ref, s_ref, do_ref, dq_ref, dk_ref, dv_ref, dbg_ref, dstate):
        head = pl.program_id(0)

        @pl.when(pl.program_id(1) == 0)
        def _():
            dstate[...] = jnp.zeros_like(dstate)

        c = _chunk_local(q_ref[...], k_ref[...], v_ref[...], bg_ref[...], head, heads)
        s0 = s_ref[0, 0]
        d_out = do_ref[...]
        ds1 = dstate[...]
        v_new = c.u - _bdot(c.w, s0)
        dqg = _bdot(d_out, s0, tb=True)
        ds0 = _bdot(c.qg, d_out, ta=True) + ds1 * c.egl
        dA = jnp.where(c.tri, _bdot(d_out, v_new, tb=True), 0.0)
        dv_new = _bdot(c.A, d_out, ta=True) + _bdot(c.kdec, ds1)
        dkdec = _bdot(v_new, ds1, tb=True)
        dgl = jnp.sum(jnp.sum(ds1 * s0, axis=1, keepdims=True), axis=0, keepdims=True) * c.egl
        dw = -_bdot(dv_new, s0, tb=True)
        ds0 = ds0 - _bdot(c.w, dv_new, ta=True)
        dvb = _hdot(c.T, dv_new, ta=True)
        dkg = _hdot(c.T, dw, ta=True)
        dL = jnp.where(c.strict, -(_bdot(dvb, c.u, tb=True) + _bdot(dkg, c.w, tb=True)), 0.0)
        dm1 = dL * c.decay
        dkb = _bdot(dm1, c.k) + dkg * c.eg
        dk = _bdot(dm1, c.kb, ta=True)
        dm2 = dA * c.decay
        dq = _bdot(dm2, c.k) + dqg * c.eg
        dk = dk + _bdot(dm2, c.q, ta=True) + dkdec * c.ekl + dkb * c.beta
        pm = dL * c.L + dA * c.A
        ones = jnp.ones((n, HEAD_DIM), F32)
        rowsum = _hdot(pm, ones)
        colsum = _hdot(pm, ones, ta=True)
        tk_ = jnp.sum(dkdec * c.kdec, axis=1, keepdims=True)
        dgc = (rowsum - colsum
               + jnp.sum(dqg * c.qg, axis=1, keepdims=True)
               - tk_
               + jnp.sum(dkg * c.kg, axis=1, keepdims=True))
        dgl = dgl + jnp.sum(tk_, axis=0, keepdims=True)
        rowi = lax.broadcasted_iota(jnp.int32, (n, HEAD_DIM), 0)
        dgc = dgc + jnp.where(rowi == n - 1, dgl, 0.0)
        dgraw = _hdot(c.trif, dgc, ta=True)
        dbeta = jnp.sum(dkb * c.k, axis=1, keepdims=True) + jnp.sum(dvb * c.v, axis=1, keepdims=True)
        lane = lax.broadcasted_iota(jnp.int32, (n, HEAD_DIM), 1)
        dq_ref[...] = dq * (HEAD_DIM ** -0.5)
        dk_ref[...] = dk
        dv_ref[...] = dvb * c.beta
        dbg_ref[0] = jnp.where(lane == 0, dbeta, jnp.where(lane == 1, dgraw, 0.0))
        dstate[...] = ds0

    last = nchunk - 1

    def sect(s):
        return pl.BlockSpec((CHUNK, HEAD_DIM), lambda h, i: (last - i, s * heads + h))

    dq, dk, dv, dbg = pl.pallas_call(
        body, name="gdn_core_bwd", grid=(heads, nchunk),
        in_specs=[sect(0), sect(1), sect(2), pl.BlockSpec((CHUNK, HEAD_DIM), lambda h, i: (last - i, 0)),
                  pl.BlockSpec((1, 1, HEAD_DIM, HEAD_DIM), lambda h, i: (h, last - i, 0, 0)),
                  pl.BlockSpec((CHUNK, HEAD_DIM), lambda h, i: (last - i, h))],
        out_specs=[pl.BlockSpec((CHUNK, HEAD_DIM), lambda h, i: (last - i, h))] * 3
        + [pl.BlockSpec((1, CHUNK, HEAD_DIM), lambda h, i: (h, last - i, 0))],
        out_shape=[jax.ShapeDtypeStruct((t, heads * HEAD_DIM), F32)] * 3
        + [jax.ShapeDtypeStruct((heads, t, HEAD_DIM), F32)],
        scratch_shapes=[pltpu.VMEM((HEAD_DIM, HEAD_DIM), F32)],
        compiler_params=_params("parallel", "arbitrary"),
    )(qkv, qkv, qkv, bg, states, do)
    return jnp.concatenate([dq, dk, dv], axis=1), dbg


def _gdn_post(o, proj, z_col0, norm_w, heads, tt):
    t = o.shape[0]
    zb = z_col0 // HEAD_DIM

    def body(o_ref, z_ref, w_ref, out_ref):
        ov = o_ref[...]
        z = z_ref[...]
        rms = lax.rsqrt(jnp.mean(ov * ov, axis=-1, keepdims=True) + NORM_EPS)
        out_ref[...] = (ov * rms * w_ref[...] * (z * _sigmoid(z))).astype(BF16)

    return pl.pallas_call(
        body, name="gdn_post", grid=(t // tt, heads),
        in_specs=[pl.BlockSpec((tt, HEAD_DIM), lambda i, h: (i, h)),
                  pl.BlockSpec((tt, HEAD_DIM), lambda i, h: (i, zb + h)),
                  pl.BlockSpec((1, HEAD_DIM), lambda i, h: (0, 0))],
        out_specs=pl.BlockSpec((tt, HEAD_DIM), lambda i, h: (i, h)),
        out_shape=jax.ShapeDtypeStruct((t, heads * HEAD_DIM), BF16),
        compiler_params=_params("parallel", "parallel"),
    )(o, proj, norm_w)


def _gdn_post_backward(dcat, o, proj, z_col0, norm_w, heads, tt):
    t = o.shape[0]
    zb = z_col0 // HEAD_DIM

    def body(d_ref, o_ref, z_ref, w_ref, do_ref, dz_ref, dw_ref):
        d = d_ref[...]
        ov = o_ref[...]
        z = z_ref[...]
        w = w_ref[...]
        rms = lax.rsqrt(jnp.mean(ov * ov, axis=-1, keepdims=True) + NORM_EPS)
        ohat = ov * rms
        sg = _sigmoid(z)
        gate = z * sg
        dz_ref[...] = (d * ohat * w * (sg * (1.0 + z * (1.0 - sg)))).astype(BF16)
        don = d * gate
        dohat = don * w
        do_ref[...] = rms * (dohat - ohat * jnp.mean(dohat * ohat, axis=-1, keepdims=True))
        dw = jnp.sum(don * ohat, axis=0, keepdims=True)
        first = jnp.logical_and(pl.program_id(0) == 0, pl.program_id(1) == 0)

        @pl.when(first)
        def _():
            dw_ref[...] = dw

        @pl.when(jnp.logical_not(first))
        def _():
            dw_ref[...] += dw

    blk = pl.BlockSpec((tt, HEAD_DIM), lambda i, h: (i, h))
    return pl.pallas_call(
        body, name="gdn_post_bwd", grid=(t // tt, heads),
        in_specs=[blk, blk, pl.BlockSpec((tt, HEAD_DIM), lambda i, h: (i, zb + h)),
                  pl.BlockSpec((1, HEAD_DIM), lambda i, h: (0, 0))],
        out_specs=[blk, blk, pl.BlockSpec((1, HEAD_DIM), lambda i, h: (0, 0))],
        out_shape=[jax.ShapeDtypeStruct((t, heads * HEAD_DIM), F32),
                   jax.ShapeDtypeStruct((t, heads * HEAD_DIM), BF16),
                   jax.ShapeDtypeStruct((1, HEAD_DIM), F32)],
        compiler_params=_params("arbitrary", "arbitrary"),
    )(dcat, o, proj, norm_w)


def _pool_select(levels, group):
    out = levels[-1]
    for gi in range(len(levels) - 2, -1, -1):
        out = jnp.where(group == gi, levels[gi], out)
    return out


def _pool_counts(t, width, group):
    pos = lax.broadcasted_iota(jnp.int32, (t, width), 0)
    win = jnp.left_shift(2, group)
    return jnp.minimum(pos + 1, win).astype(F32)


def _pooled(p, group):
    levels, s, step = [], p, 1
    for _ in POOL_WINDOWS:
        s = s + _shift_down(s, step)
        levels.append(s)
        step *= 2
    cnt = _pool_counts(p.shape[0], p.shape[1], group)
    return _pool_select(levels, group) / cnt - p, cnt


def _pool_forward(proj, p_col0, pool_w, pool_scale):
    t = proj.shape[0]
    groups, cg, _ = pool_w.shape
    pb = p_col0 // cg

    def body(p_ref, w_ref, s_ref, o_ref):
        pooled, _ = _pooled(p_ref[...], pl.program_id(0))
        o_ref[...] = (_bdot(pooled, w_ref[0]) * s_ref[...]).astype(BF16)

    return pl.pallas_call(
        body, name="pool_fwd", grid=(groups,),
        in_specs=[pl.BlockSpec((t, cg), lambda g: (0, pb + g)), pl.BlockSpec((1, cg, cg), lambda g: (g, 0, 0)),
                  pl.BlockSpec((1, cg), lambda g: (0, g))],
        out_specs=pl.BlockSpec((t, cg), lambda g: (0, g)),
        out_shape=jax.ShapeDtypeStruct((t, groups * cg), BF16),
        compiler_params=_params("parallel"),
    )(proj, pool_w, pool_scale)


def _pool_backward(dcat, d_col0, proj, p_col0, pool_w, pool_scale):
    t = proj.shape[0]
    groups, cg, _ = pool_w.shape
    pb = p_col0 // cg
    db = d_col0 // cg

    def body(d_ref, p_ref, w_ref, s_ref, dp_ref, dw_ref, ds_ref):
        group = pl.program_id(0)
        pooled, cnt = _pooled(p_ref[...], group)
        w = w_ref[0]
        d = d_ref[...]
        mixed = _bdot(pooled, w)
        ds_ref[...] = jnp.sum(d * mixed, axis=0, keepdims=True)
        dmixed = d * s_ref[...]
        dw_ref[0] = _bdot(pooled, dmixed, ta=True)
        dpooled = _bdot(dmixed, w, tb=True)
        levels, s, step = [], dpooled / cnt, 1
        for _ in POOL_WINDOWS:
            s = s + _shift_up(s, step)
            levels.append(s)
            step *= 2
        dp_ref[...] = (_pool_select(levels, group) - dpooled).astype(BF16)

    return pl.pallas_call(
        body, name="pool_bwd", grid=(groups,),
        in_specs=[pl.BlockSpec((t, cg), lambda g: (0, db + g)), pl.BlockSpec((t, cg), lambda g: (0, pb + g)),
                  pl.BlockSpec((1, cg, cg), lambda g: (g, 0, 0)), pl.BlockSpec((1, cg), lambda g: (0, g))],
        out_specs=[pl.BlockSpec((t, cg), lambda g: (0, g)), pl.BlockSpec((1, cg, cg), lambda g: (g, 0, 0)),
                   pl.BlockSpec((1, cg), lambda g: (0, g))],
        out_shape=[jax.ShapeDtypeStruct((t, groups * cg), BF16), jax.ShapeDtypeStruct((groups, cg, cg), F32),
                   jax.ShapeDtypeStruct((1, groups * cg), F32)],
        compiler_params=_params("parallel"),
    )(dcat, proj, pool_w, pool_scale)


def _attention(q, k, v, tq):
    t, d = q.shape
    m = k.shape[0]
    dh = d // XATTN_HEADS
    scale = dh ** -0.5

    def body(q_ref, k_ref, v_ref, o_ref):
        s = _bdot(q_ref[...], k_ref[...], tb=True) * scale
        s = s - jnp.max(s, axis=-1, keepdims=True)
        e = jnp.exp(s)
        p = e / jnp.sum(e, axis=-1, keepdims=True)
        o_ref[...] = _bdot(p, v_ref[...]).astype(BF16)

    return pl.pallas_call(
        body, name="xattn_fwd", grid=(XATTN_HEADS, t // tq),
        in_specs=[pl.BlockSpec((tq, dh), lambda h, i: (i, h)), pl.BlockSpec((m, dh), lambda h, i: (0, h)),
                  pl.BlockSpec((m, dh), lambda h, i: (0, h))],
        out_specs=pl.BlockSpec((tq, dh), lambda h, i: (i, h)),
        out_shape=jax.ShapeDtypeStruct((t, d), BF16),
        compiler_params=_params("parallel", "parallel"),
    )(q, k, v)


def _attention_backward(q, k, v, do, tq):
    t, d = q.shape
    m = k.shape[0]
    dh = d // XATTN_HEADS
    scale = dh ** -0.5

    def body(q_ref, k_ref, v_ref, do_ref, dq_ref, dk_ref, dv_ref, dk_acc, dv_acc):
        i = pl.program_id(1)
        qv, kv, vv, dov = q_ref[...], k_ref[...], v_ref[...], do_ref[...]
        s = _bdot(qv, kv, tb=True) * scale
        s = s - jnp.max(s, axis=-1, keepdims=True)
        e = jnp.exp(s)
        p = e / jnp.sum(e, axis=-1, keepdims=True)
        dp = _bdot(dov, vv, tb=True)
        ds = p * (dp - jnp.sum(dp * p, axis=-1, keepdims=True)) * scale
        dq_ref[...] = _bdot(ds, kv).astype(BF16)
        dv_part = _bdot(p, dov, ta=True)
        dk_part = _bdot(ds, qv, ta=True)

        @pl.when(i == 0)
        def _():
            dk_acc[...] = dk_part
            dv_acc[...] = dv_part

        @pl.when(i > 0)
        def _():
            dk_acc[...] += dk_part
            dv_acc[...] += dv_part

        @pl.when(i == pl.num_programs(1) - 1)
        def _():
            dk_ref[...] = dk_acc[...].astype(BF16)
            dv_ref[...] = dv_acc[...].astype(BF16)

    qblk = pl.BlockSpec((tq, dh), lambda h, i: (i, h))
    kblk = pl.BlockSpec((m, dh), lambda h, i: (0, h))
    return pl.pallas_call(
        body, name="xattn_bwd", grid=(XATTN_HEADS, t // tq),
        in_specs=[qblk, kblk, kblk, qblk],
        out_specs=[qblk, kblk, kblk],
        out_shape=[jax.ShapeDtypeStruct((t, d), BF16), jax.ShapeDtypeStruct((m, d), BF16),
                   jax.ShapeDtypeStruct((m, d), BF16)],
        scratch_shapes=[pltpu.VMEM((m, dh), F32), pltpu.VMEM((m, dh), F32)],
        compiler_params=_params("parallel", "arbitrary"),
    )(q, k, v, do)


def _loss_and_ln_backward(xhat, rstd, gamma, beta, target, tm):
    t, d = xhat.shape

    def body(x_ref, r_ref, g_ref, b_ref, t_ref, du_ref, dub_ref, dg_ref, db_ref, loss_ref):
        xh = x_ref[...]
        g = g_ref[...]
        diff = xh * g + b_ref[...] - t_ref[...]
        part = jnp.sum(jnp.sum(diff * diff, axis=1, keepdims=True), axis=0, keepdims=True) * (0.5 / d)
        dy = diff * (1.0 / d)
        du, dg, db = _ln_backward_math(dy, xh, r_ref[...], g)
        du_ref[...] = du
        dub_ref[...] = du.astype(BF16)
        lossrow = jnp.broadcast_to(part, (1, HEAD_DIM))
        first = pl.program_id(0) == 0

        @pl.when(first)
        def _():
            dg_ref[...] = dg
            db_ref[...] = db
            loss_ref[...] = lossrow

        @pl.when(jnp.logical_not(first))
        def _():
            dg_ref[...] += dg
            db_ref[...] += db
            loss_ref[...] += lossrow

    row = pl.BlockSpec((tm, d), lambda i: (i, 0))
    vec = pl.BlockSpec((1, d), lambda i: (0, 0))
    return pl.pallas_call(
        body, name="loss_ln3_bwd", grid=(t // tm,),
        in_specs=[row, pl.BlockSpec((tm, 1), lambda i: (i, 0)), vec, vec, row],
        out_specs=[row, row, vec, vec, pl.BlockSpec((1, HEAD_DIM), lambda i: (0, 0))],
        out_shape=[jax.ShapeDtypeStruct((t, d), F32), jax.ShapeDtypeStruct((t, d), BF16),
                   jax.ShapeDtypeStruct((1, d), F32), jax.ShapeDtypeStruct((1, d), F32),
                   jax.ShapeDtypeStruct((1, HEAD_DIM), F32)],
        compiler_params=_params("arbitrary"),
    )(xhat, rstd, gamma, beta, target)


def _pick(n, prefs):
    for p in prefs:
        if n % p == 0:
            return p
    return n


def _local_step(x, mem, target, w):
    t, d = x.shape
    heads = w["a_log"].shape[1]
    gw = heads * HEAD_DIM
    groups, cg, _ = w["pool_w"].shape
    pw = groups * cg
    n_main = 4 * gw + pw
    in_cols = n_main + 2 * heads
    s_in = w["w_in3"].shape[0]

    tm = _pick(t, (512, 256, 128))
    tm_ln = _pick(t, (256, 128))
    tk = _pick(d, (512, 256, 128))

    w_in = jnp.concatenate([w["w_in3"][s] for s in range(s_in)], axis=1)
    w_main = jnp.concatenate([w_in[:, :4 * gw], w_in[:, 4 * gw + 2 * heads:]], axis=1)
    w_ba = jnp.pad(w_in[:, 4 * gw:4 * gw + 2 * heads], ((0, 0), (0, HEAD_DIM - 2 * heads)))
    x_bf = x.astype(BF16)
    mem_bf = mem.astype(BF16)

    proj = _plain("proj_main", x_bf, w_main, tm=tm, tn=_pick(n_main, (1024, 512, 256, 128)), tk=tk, out_dtype=F32)
    ea, dtb = _gate_vectors(w["a_log"], w["dt_bias"], heads)
    vec128 = lambda i, j: (0, 0)
    ba, bg = _matmul(
        "proj_gates", x_bf, w_ba, tm=tm, tn=HEAD_DIM, tk=tk,
        extra=[(ea, (1, HEAD_DIM), vec128), (dtb, (1, HEAD_DIM), vec128)],
        outs=[(jax.ShapeDtypeStruct((t, HEAD_DIM), F32), (tm, HEAD_DIM), _tile)] * 2,
        epilogue=_gates_epilogue(heads))
    qkv = _gdn_pre(proj, w["conv_w"], heads)
    o_gdn, states = _gdn_core(qkv, bg, heads)
    cat_g = _gdn_post(o_gdn, proj, 3 * gw, w["gdn_norm_w"], heads, tm)
    cat_p = _pool_forward(proj, 4 * gw, w["pool_w"], w["pool_scale"])
    cat = jnp.concatenate([cat_g, cat_p], axis=1)
    h1, h1_bf, xhat1, rstd1 = _ln_forward("mix_ln1", cat, w["w_out"], x, w["ln1_g"], w["ln1_b"], tm=tm_ln, tk=tk)

    tn_d = _pick(d, (1024, 512, 256, 128))
    q = _plain("xattn_q", h1_bf, w["xq_w"], tm=tm, tn=tn_d, tk=tk, out_dtype=BF16)
    mlen = mem.shape[0]
    tm_mem = _pick(mlen, (256, 128))
    k = _plain("xattn_k", mem_bf, w["xk_w"], tm=tm_mem, tn=tn_d, tk=tk, out_dtype=BF16)
    v = _plain("xattn_v", mem_bf, w["xv_w"], tm=tm_mem, tn=tn_d, tk=tk, out_dtype=BF16)
    att = _attention(q, k, v, tm)
    h2, h2_bf, xhat2, rstd2 = _ln_forward("xo_ln2", att, w["xo_w"], h1, w["ln2_g"], w["ln2_b"], tm=tm_ln, tk=tk)

    s_up = w["w_up3"].shape[0]
    ff = s_up * w["w_up3"].shape[2]
    tn_f = _pick(ff // s_up, (1024, 512, 256, 128))

    def up_epi(acc, ex, out, i):
        r = jnp.maximum(acc, 0.0)
        out[0][...] = (r * r).astype(BF16)
        out[1][...] = (2.0 * r).astype(BF16)

    act, act_grad = _matmul(
        "mlp_up", h2_bf, w["w_up3"], b_blocks=s_up, tm=tm, tn=tn_f, tk=tk,
        outs=[(jax.ShapeDtypeStruct((t, ff), BF16), (tm, tn_f), _tile)] * 2, epilogue=up_epi)
    tk_f = _pick(ff, (512, 256, 128))
    _, _, xhat3, rstd3 = _ln_forward("down_ln3", act, w["w_down"], h2, w["ln3_g"], w["ln3_b"], tm=tm_ln, tk=tk_f)

    grads = {}
    du3, du3_bf, grads["ln3_g"], grads["ln3_b"], loss = _loss_and_ln_backward(
        xhat3, rstd3, w["ln3_g"], w["ln3_b"], target, tm_ln)

    def dup_epi(acc, ex, out, i):
        out[0][...] = (acc * ex[0][...].astype(F32)).astype(BF16)

    dup = _matmul(
        "mlp_down_dx", du3_bf, w["w_down"], tb=True, tm=tm, tn=tn_f, tk=tk,
        extra=[(act_grad, (tm, tn_f), _tile)],
        outs=[(jax.ShapeDtypeStruct((t, ff), BF16), (tm, tn_f), _tile)], epilogue=dup_epi)[0]
    tk_t = _pick(t, (512, 256, 128))
    tm_w = _pick(d, (512, 256, 128))
    grads["w_down"] = _plain("mlp_down_dw", act, du3_bf, ta=True, tm=_pick(ff, (512, 256, 128)), tn=tn_d, tk=tk_t,
                             out_dtype=F32)
    grads["w_up3"] = _plain("mlp_up_dw", h2_bf, dup, ta=True, tm=tm_w, tn=tn_f, tk=tk_t, out_dtype=F32, out3=s_up)
    du2, du2_bf, grads["ln2_g"], grads["ln2_b"] = _ln_backward(
        "mlp_up_dx_ln2", dup, w["w_up3"], du3, xhat2, rstd2, w["ln2_g"], tm=tm_ln,
        tk=_pick(ff // s_up, (512, 256, 128)), b_blocks=s_up)

    grads["xo_w"] = _plain("xo_dw", att, du2_bf, ta=True, tm=tm_w, tn=tn_d, tk=tk_t, out_dtype=F32)
    datt = _plain("xo_dx", du2_bf, w["xo_w"], tb=True, tm=tm, tn=tn_d, tk=tk, out_dtype=BF16)
    dq, dk, dv = _attention_backward(q, k, v, datt, tm)
    tk_m = _pick(mlen, (256, 128))
    grads["xq_w"] = _plain("xq_dw", h1_bf, dq, ta=True, tm=tm_w, tn=tn_d, tk=tk_t, out_dtype=F32)
    grads["xk_w"] = _plain("xk_dw", mem_bf, dk, ta=True, tm=tm_w, tn=tn_d, tk=tk_m, out_dtype=F32)
    grads["xv_w"] = _plain("xv_dw", mem_bf, dv, ta=True, tm=tm_w, tn=tn_d, tk=tk_m, out_dtype=F32)
    du1, du1_bf, grads["ln1_g"], grads["ln1_b"] = _ln_backward(
        "xq_dx_ln1", dq, w["xq_w"], du2, xhat1, rstd1, w["ln1_g"], tm=tm_ln, tk=tk)

    grads["w_out"] = _plain("out_dw", cat, du1_bf, ta=True, tm=tm_w, tn=tn_d, tk=tk_t, out_dtype=F32)
    dcat = _plain("out_dx", du1_bf, w["w_out"], tb=True, tm=tm, tn=tn_d, tk=tk, out_dtype=F32)
    dp, grads["pool_w"], grads["pool_scale"] = _pool_backward(dcat, gw, proj, 4 * gw, w["pool_w"], w["pool_scale"])
    do_gdn, dz, grads["gdn_norm_w"] = _gdn_post_backward(dcat, o_gdn, proj, 3 * gw, w["gdn_norm_w"], heads, tm)
    dqkv, dbg_h = _gdn_core_backward(qkv, bg, states, do_gdn, heads)
    dqkv_pre, grads["conv_w"] = _gdn_pre_backward(proj, w["conv_w"], dqkv, heads)
    dbg = jnp.concatenate([dbg_h[:, :, 0].T, dbg_h[:, :, 1].T], axis=1)
    dbg = jnp.pad(dbg, ((0, 0), (0, HEAD_DIM - 2 * heads)))
    dba, dalog_row, ddt_row = _gates_backward(ba, bg, dbg, ea, dtb, heads)
    grads["a_log"] = dalog_row[:, heads:2 * heads]
    grads["dt_bias"] = ddt_row[:, heads:2 * heads]

    dproj = jnp.concatenate([dqkv_pre, dz, dp], axis=1)
    tn_main = _pick(n_main, (1024, 512, 256, 128))
    dw_main = _plain("proj_dw", x_bf, dproj, ta=True, tm=tm_w, tn=tn_main, tk=tk_t, out_dtype=F32)
    dw_ba = _plain("proj_gates_dw", x_bf, dba, ta=True, tm=tm_w, tn=HEAD_DIM, tk=tk_t, out_dtype=F32)
    dw_in = jnp.concatenate([dw_main[:, :4 * gw], dw_ba[:, :2 * heads], dw_main[:, 4 * gw:]], axis=1)
    per = in_cols // s_in
    grads["w_in3"] = jnp.stack([dw_in[:, s * per:(s + 1) * per] for s in range(s_in)], axis=0)

    def dx_epi(acc, ex, out, i):
        out[0][...] = acc + ALPHA * ex[0][...]

    dproj_all = jnp.concatenate([dproj, dba], axis=1)
    w_all = jnp.concatenate([w_main, w_ba], axis=1)
    tk_all = _pick(n_main + HEAD_DIM, (512, 256, 128))
    grad_x = _matmul(
        "proj_dx", dproj_all, w_all, tb=True, tm=tm, tn=tn_d, tk=tk_all,
        extra=[(du1, (tm, tn_d), _tile)],
        outs=[(jax.ShapeDtypeStruct((t, d), F32), (tm, tn_d), _tile)], epilogue=dx_epi)[0]
    return loss, grad_x, grads


def _adamw(name, w, g, m, v):
    r, c = w.shape
    tr = _pick(r, (256, 128, 64, 32, 16, 8))
    c1 = 1.0 - ADAM_B1 ** ADAM_STEP
    c2 = 1.0 - ADAM_B2 ** ADAM_STEP

    def body(w_ref, g_ref, m_ref, v_ref, d_ref, mo_ref, vo_ref):
        gv = g_ref[...]
        mn = ADAM_B1 * m_ref[...] + (1.0 - ADAM_B1) * gv
        vn = ADAM_B2 * v_ref[...] + (1.0 - ADAM_B2) * (gv * gv)
        d_ref[...] = -ADAM_LR * ((mn / c1) / (jnp.sqrt(vn / c2) + ADAM_EPS) + ADAM_WD * w_ref[...])
        mo_ref[...] = mn
        vo_ref[...] = vn

    blk = pl.BlockSpec((tr, c), lambda i: (i, 0))
    return pl.pallas_call(
        body, name=name, grid=(r // tr,), in_specs=[blk] * 4, out_specs=[blk] * 3,
        out_shape=[jax.ShapeDtypeStruct((r, c), F32)] * 3,
        compiler_params=_params("parallel"),
    )(w, g, m, v)


def _place():
    x, y, c = lax.axis_index("x"), lax.axis_index("y"), lax.axis_index("c")
    chips = [(1 - x, y), (x, 1 - y), (1 - x, 1 - y)]
    return x, y, c, chips


HBM = pl.BlockSpec(memory_space=pltpu.HBM)


def _gather_weights(shards):
    n = len(shards)

    def body(*refs):
        ins, outs = refs[:n], refs[n:2 * n]
        send_sems, recv_sems, local_sems = refs[2 * n:]
        x, y, c, chips = _place()
        me = 2 * x + y
        sibling = (x, y, 1 - c)

        def half(i, shard_index, which):
            rows = ins[i].shape[0] // 2
            return outs[i].at[shard_index, pl.ds(which * rows, rows)]

        def remote(i, slot, src, dst, to):
            return pltpu.make_async_remote_copy(src_ref=src, dst_ref=dst, send_sem=send_sems.at[6 * i + slot],
                                                recv_sem=recv_sems.at[6 * i + slot], device_id=to, device_id_type=MESH)

        local = [pltpu.make_async_copy(ins[i], outs[i].at[me], local_sems.at[i]) for i in range(n)]
        for cp in local:
            cp.start()
        started = []
        for i in range(n):
            rows = ins[i].shape[0] // 2
            for j, chip in enumerate(chips):
                cp = remote(i, j, ins[i].at[pl.ds(c * rows, rows)], half(i, me, c), (*chip, c))
                cp.start()
                started.append(cp)
        for i in range(n):
            for j, chip in enumerate(chips):
                theirs = 2 * chip[0] + chip[1]
                landed = half(i, theirs, c)
                remote(i, j, landed, landed, (*chip, c)).wait_recv()
                cp = remote(i, 3 + j, landed, landed, sibling)
                cp.start()
                started.append(cp)
        for i in range(n):
            for j, chip in enumerate(chips):
                theirs = 2 * chip[0] + chip[1]
                passed = half(i, theirs, 1 - c)
                remote(i, 3 + j, passed, passed, sibling).wait_recv()
        for cp in started:
            cp.wait_send()
        for cp in local:
            cp.wait()

    return pl.pallas_call(
        body, name="gather_weights",
        in_specs=[HBM] * n, out_specs=[HBM] * n,
        out_shape=[jax.ShapeDtypeStruct((N_SHARD,) + s.shape, s.dtype) for s in shards],
        scratch_shapes=[pltpu.SemaphoreType.DMA((6 * n,)), pltpu.SemaphoreType.DMA((6 * n,)),
                        pltpu.SemaphoreType.DMA((n,))],
    )(*shards)


def _all_reduce_small(slab):
    r, width = slab.shape
    ndev = 8

    def body(x_ref, out_ref, buf, send_sems, recv_sems):
        x, y, c, _ = _place()
        me = 4 * x + 2 * y + c
        buf[me] = x_ref[...]
        copies = []
        for k in range(1, ndev):
            peer = jnp.bitwise_xor(me, k)
            to = (peer // 4, (peer // 2) % 2, peer % 2)
            cp = pltpu.make_async_remote_copy(src_ref=x_ref, dst_ref=buf.at[me], send_sem=send_sems.at[k - 1],
                                              recv_sem=recv_sems.at[k - 1], device_id=to, device_id_type=MESH)
            cp.start()
            copies.append(cp)
        for k in range(1, ndev):
            peer = jnp.bitwise_xor(me, k)
            pltpu.make_async_remote_copy(src_ref=x_ref, dst_ref=buf.at[peer], send_sem=send_sems.at[k - 1],
                                         recv_sem=recv_sems.at[k - 1], device_id=(x, y, c),
                                         device_id_type=MESH).wait_recv()
        for cp in copies:
            cp.wait_send()
        total = buf[0]
        for d in range(1, ndev):
            total = total + buf[d]
        out_ref[...] = total

    return pl.pallas_call(
        body, name="all_reduce_small",
        in_specs=[pl.BlockSpec(memory_space=pltpu.VMEM)], out_specs=pl.BlockSpec(memory_space=pltpu.VMEM),
        out_shape=jax.ShapeDtypeStruct((r, width), F32),
        scratch_shapes=[pltpu.VMEM((ndev, r, width), F32), pltpu.SemaphoreType.DMA((ndev - 1,)),
                        pltpu.SemaphoreType.DMA((ndev - 1,))],
        compiler_params=pltpu.CompilerParams(vmem_limit_bytes=VMEM_LIMIT),
    )(slab)


def _swap_halves(grads):
    n = len(grads)

    def body(*refs):
        ins, outs = refs[:n], refs[n:2 * n]
        send_sems, recv_sems = refs[2 * n:]
        x, y, c, _ = _place()
        copies = []
        for i in range(n):
            rows = ins[i].shape[1] // 2
            for s in range(N_SHARD):
                cp = pltpu.make_async_remote_copy(
                    src_ref=ins[i].at[s, pl.ds((1 - c) * rows, rows)], dst_ref=outs[i].at[s],
                    send_sem=send_sems.at[N_SHARD * i + s], recv_sem=recv_sems.at[N_SHARD * i + s],
                    device_id=(x, y, 1 - c), device_id_type=MESH)
                cp.start()
                copies.append(cp)
        for cp in copies:
            cp.wait()

    return pl.pallas_call(
        body, name="grad_swap_halves",
        in_specs=[HBM] * n, out_specs=[HBM] * n,
        out_shape=[jax.ShapeDtypeStruct((N_SHARD, g.shape[1] // 2, g.shape[2]), F32) for g in grads],
        scratch_shapes=[pltpu.SemaphoreType.DMA((N_SHARD * n,)), pltpu.SemaphoreType.DMA((N_SHARD * n,))],
    )(*grads)


def _chip_partial(name, grad, other, core):
    s, r, cdim = grad.shape
    rows = r // 2
    tr = _pick(rows, (256, 128, 64, 32, 16))
    nb = rows // tr

    def body(core_ref, g_ref, o_ref, out_ref):
        out_ref[...] = (g_ref[...] + o_ref[...]).astype(BF16)

    return pl.pallas_call(
        body, name=name,
        grid_spec=pltpu.PrefetchScalarGridSpec(
            num_scalar_prefetch=1, grid=(s, nb),
            in_specs=[pl.BlockSpec((None, tr, cdim), lambda j, b, core_ref: (j, core_ref[0] * nb + b, 0)),
                      pl.BlockSpec((None, tr, cdim), lambda j, b, core_ref: (j, b, 0))],
            out_specs=pl.BlockSpec((None, tr, cdim), lambda j, b, core_ref: (j, b, 0))),
        out_shape=jax.ShapeDtypeStruct((s, rows, cdim), BF16),
        compiler_params=_params("parallel", "parallel"),
    )(core, grad, other)


def _send_partials(partials):
    n = len(partials)

    def body(*refs):
        ins, outs = refs[:n], refs[n:2 * n]
        send_sems, recv_sems = refs[2 * n:]
        x, y, c, chips = _place()
        copies = []
        for i in range(n):
            for j, chip in enumerate(chips):
                theirs = 2 * chip[0] + chip[1]
                cp = pltpu.make_async_remote_copy(
                    src_ref=ins[i].at[theirs], dst_ref=outs[i].at[j],
                    send_sem=send_sems.at[3 * i + j], recv_sem=recv_sems.at[3 * i + j],
                    device_id=(*chip, c), device_id_type=MESH)
                cp.start()
                copies.append(cp)
        for cp in copies:
            cp.wait()

    return pl.pallas_call(
        body, name="grad_send_partials",
        in_specs=[HBM] * n, out_specs=[HBM] * n,
        out_shape=[jax.ShapeDtypeStruct((3,) + p.shape[1:], BF16) for p in partials],
        scratch_shapes=[pltpu.SemaphoreType.DMA((3 * n,)), pltpu.SemaphoreType.DMA((3 * n,))],
    )(*partials)


def _reduce_own(name, grad, other, received, where):
    s, r, cdim = grad.shape
    rows = r // 2
    tr = _pick(rows, (256, 128, 64, 32, 16))
    nb = rows // tr

    def body(where_ref, g_ref, o_ref, r_ref, out_ref):
        total = g_ref[...] + o_ref[...]
        for j in range(3):
            total = total + r_ref[j].astype(F32)
        out_ref[...] = total

    return pl.pallas_call(
        body, name=name,
        grid_spec=pltpu.PrefetchScalarGridSpec(
            num_scalar_prefetch=1, grid=(nb,),
            in_specs=[pl.BlockSpec((None, tr, cdim), lambda b, w_ref: (w_ref[0], w_ref[1] * nb + b, 0)),
                      pl.BlockSpec((None, tr, cdim), lambda b, w_ref: (w_ref[0], b, 0)),
                      pl.BlockSpec((3, tr, cdim), lambda b, w_ref: (0, b, 0))],
            out_specs=pl.BlockSpec((tr, cdim), lambda b, w_ref: (b, 0))),
        out_shape=jax.ShapeDtypeStruct((rows, cdim), F32),
        compiler_params=_params("parallel"),
    )(where, grad, other, received)


def _join_halves(halves):
    n = len(halves)

    def body(*refs):
        ins, outs = refs[:n], refs[n:2 * n]
        send_sems, recv_sems, local_sems = refs[2 * n:]
        x, y, c, _ = _place()
        copies = []
        for i in range(n):
            rows = ins[i].shape[0]
            mine = outs[i].at[pl.ds(c * rows, rows)]
            lc = pltpu.make_async_copy(ins[i], mine, local_sems.at[i])
            lc.start()
            cp = pltpu.make_async_remote_copy(src_ref=ins[i], dst_ref=mine, send_sem=send_sems.at[i],
                                              recv_sem=recv_sems.at[i], device_id=(x, y, 1 - c), device_id_type=MESH)
            cp.start()
            copies.append((lc, cp))
        for i, (lc, cp) in enumerate(copies):
            rows = ins[i].shape[0]
            theirs = outs[i].at[pl.ds((1 - c) * rows, rows)]
            pltpu.make_async_remote_copy(src_ref=ins[i], dst_ref=theirs, send_sem=send_sems.at[i],
                                         recv_sem=recv_sems.at[i], device_id=(x, y, 1 - c),
                                         device_id_type=MESH).wait_recv()
            cp.wait_send()
            lc.wait()

    return pl.pallas_call(
        body, name="grad_join_halves",
        in_specs=[HBM] * n, out_specs=[HBM] * n,
        out_shape=[jax.ShapeDtypeStruct((2 * h.shape[0], h.shape[1]), F32) for h in halves],
        scratch_shapes=[pltpu.SemaphoreType.DMA((n,)), pltpu.SemaphoreType.DMA((n,)), pltpu.SemaphoreType.DMA((n,))],
    )(*halves)


BIG = ("w_in", "pool_w", "w_out", "xq_w", "xk_w", "xv_w", "xo_w", "w_up", "w_down")
SMALL = ("conv_w", "a_log", "dt_bias", "gdn_norm_w", "pool_scale", "ln1_g", "ln1_b", "ln2_g", "ln2_b", "ln3_g", "ln3_b")
ORDER = ("w_in", "conv_w", "a_log", "dt_bias", "gdn_norm_w", "pool_w", "pool_scale", "w_out", "ln1_g", "ln1_b",
         "xq_w", "xk_w", "xv_w", "xo_w", "ln2_g", "ln2_b", "w_up", "w_down", "ln3_g", "ln3_b")
LANES = 128


def _rows(flat_len):
    return -(-flat_len // LANES)


def _pack(pieces):
    out = []
    for p in pieces:
        flat = p.reshape(-1).astype(F32)
        out.append(jnp.pad(flat, (0, _rows(flat.shape[0]) * LANES - flat.shape[0])).reshape(-1, LANES))
    slab = jnp.concatenate(out, axis=0)
    return jnp.pad(slab, ((0, -slab.shape[0] % 8), (0, 0)))


def _unpack(slab, shapes):
    out, row = [], 0
    for shp in shapes:
        size = math.prod(shp)
        out.append(slab[row:row + _rows(size)].reshape(-1)[:size].reshape(shp))
        row += _rows(size)
    return out


def _as2d(a):
    a = a[0]
    return a.reshape(-1, a.shape[-1]) if a.ndim == 3 else a


def kernel(x, mem, w_in, conv_w, a_log, dt_bias, gdn_norm_w, pool_w, pool_scale, w_out, ln1_g, ln1_b, xq_w, xk_w, xv_w, xo_w, ln2_g, ln2_b, w_up, w_down, ln3_g, ln3_b, loss_target, m_w_in, m_conv_w, m_a_log, m_dt_bias, m_gdn_norm_w, m_pool_w, m_pool_scale, m_w_out, m_ln1_g, m_ln1_b, m_xq_w, m_xk_w, m_xv_w, m_xo_w, m_ln2_g, m_ln2_b, m_w_up, m_w_down, m_ln3_g, m_ln3_b, v_w_in, v_conv_w, v_a_log, v_dt_bias, v_gdn_norm_w, v_pool_w, v_pool_scale, v_w_out, v_ln1_g, v_ln1_b, v_xq_w, v_xk_w, v_xv_w, v_xo_w, v_ln2_g, v_ln2_b, v_w_up, v_w_down, v_ln3_g, v_ln3_b):
    given = dict(locals())
    cx, cy, cc = lax.axis_index("x"), lax.axis_index("y"), lax.axis_index("c")
    me = 2 * cx + cy
    groups = pool_w.shape[1]
    cs = pool_w.shape[2]
    kk, conv_cols = conv_w.shape[1], conv_w.shape[2]

    shards = [_as2d(given[n]).astype(BF16) for n in BIG]
    full = dict(zip(BIG, _gather_weights(shards)))
    wts = {n: full[n].reshape(-1, full[n].shape[-1]) for n in ("w_out", "xq_w", "xk_w", "xv_w", "xo_w", "w_down")}
    wts["w_in3"] = full["w_in"]
    wts["w_up3"] = full["w_up"]
    wts["pool_w"] = full["pool_w"].reshape(N_SHARD, groups, cs, -1).transpose(1, 0, 2, 3).reshape(groups, N_SHARD * cs, -1)
    conv_slab = jnp.zeros((kk, N_SHARD * conv_cols), F32)
    conv_slab = lax.dynamic_update_slice(conv_slab, conv_w[0] * (cc == 0).astype(F32), (0, me * conv_cols))
    wts["conv_w"] = _unpack(_all_reduce_small(_pack([conv_slab])), [conv_slab.shape])[0]
    for n in ("a_log", "dt_bias", "gdn_norm_w", "pool_scale", "ln1_g", "ln1_b", "ln2_g", "ln2_b", "ln3_g", "ln3_b"):
        wts[n] = given[n]

    loss_row, grad_x, g = _local_step(x[0], mem[0], loss_target[0], wts)

    small_names = ("a_log", "dt_bias", "gdn_norm_w", "pool_scale", "ln1_g", "ln1_b", "ln2_g", "ln2_b", "ln3_g", "ln3_b")
    pieces = [g["conv_w"]] + [g[n] for n in small_names] + [loss_row[:, :1]]
    shapes = [p.shape for p in pieces]
    summed = _unpack(_all_reduce_small(_pack(pieces)), shapes)
    gsmall = dict(zip(small_names, summed[1:-1]))
    gsmall["conv_w"] = lax.dynamic_slice(summed[0], (0, me * conv_cols), (kk, conv_cols))
    loss = summed[-1][0, 0]

    gpool = g["pool_w"].reshape(groups, N_SHARD, cs, -1).transpose(1, 0, 2, 3).reshape(N_SHARD, groups * cs, -1)
    blocks = {"w_in": g["w_in3"], "w_up": g["w_up3"], "pool_w": gpool}
    for n in ("w_out", "xq_w", "xk_w", "xv_w", "xo_w", "w_down"):
        blocks[n] = g[n].reshape(N_SHARD, -1, g[n].shape[-1])
    big = [blocks[n] for n in BIG]
    others = _swap_halves(big)
    core = cc.astype(jnp.int32).reshape(1)
    where = jnp.stack([me, cc]).astype(jnp.int32)
    partials = [_chip_partial("chip_partial_" + n, gb, ob, core) for n, gb, ob in zip(BIG, big, others)]
    received = _send_partials(partials)
    halves = [_reduce_own("reduce_own_" + n, gb, ob, rb, where) for n, gb, ob, rb in zip(BIG, big, others, received)]
    gbig = dict(zip(BIG, _join_halves(halves)))

    grad, delta, new_m, new_v = {}, {}, {}, {}
    for n in BIG:
        shp = given[n].shape
        d2, m2, v2 = _adamw("adamw_" + n, _as2d(given[n]), gbig[n], _as2d(given["m_" + n]), _as2d(given["v_" + n]))
        grad[n], delta[n], new_m[n], new_v[n] = (a.reshape(shp) for a in (gbig[n], d2, m2, v2))
    sshapes = [given[n].shape for n in SMALL]
    slabs = [_pack([given[p + n] for n in SMALL]) for p in ("", "m_", "v_")]
    gslab = _pack([gsmall[n] for n in SMALL])
    outs = _adamw("adamw_small", slabs[0], gslab, slabs[1], slabs[2])
    for dst, slab in zip((delta, new_m, new_v), outs):
        dst.update(zip(SMALL, _unpack(slab, sshapes)))
    for n in SMALL:
        grad[n] = gsmall[n].reshape(given[n].shape)

    return (loss, grad_x[None], *[grad[n] for n in ORDER], *[delta[n] for n in ORDER],
            *[new_m[n] for n in ORDER], *[new_v[n] for n in ORDER])
```

```python
import functools
import math

import jax
import jax.numpy as jnp
from jax import lax
from jax.experimental import pallas as pl
from jax.experimental.pallas import tpu as pltpu

F32 = jnp.float32
BF16 = jnp.bfloat16
HIGHEST = lax.Precision.HIGHEST
MESH = pl.DeviceIdType.MESH

HEAD_DIM = 128
CHUNK = 64
POOL_WINDOWS = (2, 4, 8, 16)
XATTN_HEADS = 4
ALPHA = 2.0 ** 0.25
LN_EPS = 1e-5
NORM_EPS = 1e-6
ADAM_LR, ADAM_B1, ADAM_B2, ADAM_EPS, ADAM_WD, ADAM_STEP = 0.001, 0.9, 0.999, 1e-08, 0.01, 10
N_SHARD = 4
VMEM_LIMIT = 56 * 1024 * 1024


def _params(*sem):
    return pltpu.CompilerParams(dimension_semantics=sem, vmem_limit_bytes=VMEM_LIMIT)


def _bdot(a, b, ta=False, tb=False):
    dims = (((0 if ta else 1,), (1 if tb else 0,)), ((), ()))
    return lax.dot_general(a.astype(BF16), b.astype(BF16), dims, preferred_element_type=F32)


def _hdot(a, b, ta=False, tb=False):
    dims = (((0 if ta else 1,), (1 if tb else 0,)), ((), ()))
    return lax.dot_general(a.astype(F32), b.astype(F32), dims, precision=HIGHEST, preferred_element_type=F32)


def _sigmoid(x):
    return 1.0 / (1.0 + jnp.exp(-x))


def _matmul(name, a, b, *, ta=False, tb=False, tm, tn, tk, extra=(), outs, epilogue, b_blocks=None,
            sequential=False):
    m, k_dim = (a.shape[1], a.shape[0]) if ta else a.shape
    if b_blocks and tb:
        n = b.shape[1]
        k_dim = b.shape[0] * b.shape[2]
        per = b.shape[2] // tk
        b_spec = pl.BlockSpec((None, tn, tk), lambda i, j, k: (k // per, j, k % per))
    elif b_blocks:
        n = b.shape[0] * b.shape[2]
        per = b.shape[2] // tn
        b_spec = pl.BlockSpec((None, tk, tn), lambda i, j, k: (j // per, k, j % per))
    elif tb:
        n = b.shape[0]
        b_spec = pl.BlockSpec((tn, tk), lambda i, j, k: (j, k))
    else:
        n = b.shape[1]
        b_spec = pl.BlockSpec((tk, tn), lambda i, j, k: (k, j))
    assert m % tm == 0 and n % tn == 0 and k_dim % tk == 0, (name, m, n, k_dim, tm, tn, tk)
    nk = k_dim // tk
    a_spec = pl.BlockSpec((tk, tm), lambda i, j, k: (k, i)) if ta else pl.BlockSpec((tm, tk), lambda i, j, k: (i, k))
    n_extra, n_out = len(extra), len(outs)

    def wrap(index_map):
        return lambda i, j, k: index_map(i, j)

    def body(*refs):
        a_ref, b_ref = refs[0], refs[1]
        ex = refs[2:2 + n_extra]
        out = refs[2 + n_extra:2 + n_extra + n_out]
        acc = refs[-1]
        i, k = pl.program_id(0), pl.program_id(2)

        @pl.when(k == 0)
        def _():
            acc[...] = jnp.zeros_like(acc)

        acc[...] += _bdot(a_ref[...], b_ref[...], ta, tb)

        @pl.when(k == nk - 1)
        def _():
            epilogue(acc[...], ex, out, i)

    sem = ("arbitrary",) * 3 if sequential else ("parallel", "parallel", "arbitrary")
    res = pl.pallas_call(
        body, name=name, grid=(m // tm, n // tn, nk),
        in_specs=[a_spec, b_spec] + [pl.BlockSpec(bs, wrap(im)) for _, bs, im in extra],
        out_specs=[pl.BlockSpec(bs, wrap(im)) for _, bs, im in outs],
        out_shape=[s for s, _, _ in outs],
        scratch_shapes=[pltpu.VMEM((tm, tn), F32)],
        compiler_params=_params(*sem),
    )(a, b, *[x for x, _, _ in extra])
    return res


def _tile(i, j):
    return (i, j)


def _plain(name, a, b, *, ta=False, tb=False, tm, tn, tk, out_dtype, b_blocks=None, out3=None):
    m = a.shape[1] if ta else a.shape[0]
    n = (b.shape[0] * b.shape[2]) if b_blocks else (b.shape[0] if tb else b.shape[1])

    def epi(acc, ex, out, i):
        out[0][...] = acc.astype(out_dtype)

    if out3:
        per = (n // out3) // tn
        spec = (jax.ShapeDtypeStruct((out3, m, n // out3), out_dtype), (None, tm, tn),
                lambda i, j: (j // per, i, j % per))
    else:
        spec = (jax.ShapeDtypeStruct((m, n), out_dtype), (tm, tn), _tile)
    return _matmul(name, a, b, ta=ta, tb=tb, tm=tm, tn=tn, tk=tk, outs=[spec], epilogue=epi,
                   b_blocks=b_blocks)[0]


def _ln_forward(name, a, b, res, gamma, beta, *, tm, tk, want_bf16=True):
    m, n = res.shape

    def epi(acc, ex, out, i):
        u = ALPHA * ex[0][...] + acc
        mu = jnp.mean(u, axis=-1, keepdims=True)
        xc = u - mu
        var = jnp.mean(xc * xc, axis=-1, keepdims=True)
        rstd = lax.rsqrt(var + LN_EPS)
        xhat = xc * rstd
        h = xhat * ex[1][...] + ex[2][...]
        out[0][...] = h
        out[1][...] = h.astype(BF16)
        out[2][...] = xhat
        out[3][...] = rstd

    row = lambda i, j: (i, 0)
    vec = lambda i, j: (0, 0)
    return _matmul(
        name, a, b, tm=tm, tn=n, tk=tk,
        extra=[(res, (tm, n), row), (gamma, (1, n), vec), (beta, (1, n), vec)],
        outs=[(jax.ShapeDtypeStruct((m, n), F32), (tm, n), row),
              (jax.ShapeDtypeStruct((m, n), BF16), (tm, n), row),
              (jax.ShapeDtypeStruct((m, n), F32), (tm, n), row),
              (jax.ShapeDtypeStruct((m, 1), F32), (tm, 1), row)],
        epilogue=epi)


def _ln_backward_math(dy, xhat, rstd, gamma):
    dxhat = dy * gamma
    m1 = jnp.mean(dxhat, axis=-1, keepdims=True)
    m2 = jnp.mean(dxhat * xhat, axis=-1, keepdims=True)
    du = rstd * (dxhat - m1 - xhat * m2)
    return du, jnp.sum(dy * xhat, axis=0, keepdims=True), jnp.sum(dy, axis=0, keepdims=True)


def _ln_backward(name, a, b, dres, xhat, rstd, gamma, *, tm, tk, b_blocks=None, tb=True):
    m, n = dres.shape

    def epi(acc, ex, out, i):
        dy = acc + ALPHA * ex[0][...]
        du, dg, db = _ln_backward_math(dy, ex[1][...], ex[2][...], ex[3][...])
        out[0][...] = du
        out[1][...] = du.astype(BF16)
        first = i == 0

        @pl.when(first)
        def _():
            out[2][...] = dg
            out[3][...] = db

        @pl.when(jnp.logical_not(first))
        def _():
            out[2][...] += dg
            out[3][...] += db

    row = lambda i, j: (i, 0)
    vec = lambda i, j: (0, 0)
    return _matmul(
        name, a, b, tb=tb, tm=tm, tn=n, tk=tk, b_blocks=b_blocks, sequential=True,
        extra=[(dres, (tm, n), row), (xhat, (tm, n), row), (rstd, (tm, 1), row), (gamma, (1, n), vec)],
        outs=[(jax.ShapeDtypeStruct((m, n), F32), (tm, n), row),
              (jax.ShapeDtypeStruct((m, n), BF16), (tm, n), row),
              (jax.ShapeDtypeStruct((1, n), F32), (1, n), vec),
              (jax.ShapeDtypeStruct((1, n), F32), (1, n), vec)],
        epilogue=epi)


def _shift_down(x, k):
    row = lax.broadcasted_iota(jnp.int32, x.shape, 0)
    return jnp.where(row >= k, pltpu.roll(x, k, axis=0), 0.0)


def _shift_up(x, k):
    t = x.shape[0]
    row = lax.broadcasted_iota(jnp.int32, x.shape, 0)
    return jnp.where(row < t - k, pltpu.roll(x, t - k, axis=0), 0.0)


def _conv_silu_norm(x, w, normalise):
    kk = w.shape[0]
    c = x * w[kk - 1:kk, :]
    for j in range(kk - 1):
        c = c + _shift_down(x, kk - 1 - j) * w[j:j + 1, :]
    sg = _sigmoid(c)
    s = c * sg
    r = lax.rsqrt(jnp.sum(s * s, axis=-1, keepdims=True) + NORM_EPS)
    y = jnp.where(normalise, s * r, s)
    return c, sg, s, r, y


def _gdn_pre(proj, conv_w, heads):
    t = proj.shape[0]
    kk = conv_w.shape[0]

    def body(x_ref, w_ref, o_ref):
        normalise = pl.program_id(0) < 2
        o_ref[...] = _conv_silu_norm(x_ref[...], w_ref[...], normalise)[4]

    col = lambda s, h: (0, s * heads + h)
    return pl.pallas_call(
        body, name="gdn_pre", grid=(3, heads),
        in_specs=[pl.BlockSpec((t, HEAD_DIM), col), pl.BlockSpec((kk, HEAD_DIM), col)],
        out_specs=pl.BlockSpec((t, HEAD_DIM), col),
        out_shape=jax.ShapeDtypeStruct((t, 3 * heads * HEAD_DIM), F32),
        compiler_params=_params("parallel", "parallel"),
    )(proj, conv_w)


def _gdn_pre_backward(proj, conv_w, dqkv, heads):
    t = proj.shape[0]
    kk = conv_w.shape[0]

    def body(x_ref, w_ref, dy_ref, dx_ref, dw_ref):
        normalise = pl.program_id(0) < 2
        x = x_ref[...]
        w = w_ref[...]
        dy = dy_ref[...]
        c, sg, s, r, y = _conv_silu_norm(x, w, normalise)
        ds_norm = r * (dy - y * jnp.sum(dy * y, axis=-1, keepdims=True))
        ds = jnp.where(normalise, ds_norm, dy)
        dc = ds * (sg * (1.0 + c * (1.0 - sg)))
        dx = dc * w[kk - 1:kk, :]
        rows = [None] * kk
        rows[kk - 1] = jnp.sum(dc * x, axis=0, keepdims=True)
        for j in range(kk - 1):
            lag = kk - 1 - j
            dx = dx + _shift_up(dc, lag) * w[j:j + 1, :]
            rows[j] = jnp.sum(dc * _shift_down(x, lag), axis=0, keepdims=True)
        dx_ref[...] = dx.astype(BF16)
        dw_ref[...] = jnp.concatenate(rows, axis=0)

    col = lambda s, h: (0, s * heads + h)
    return pl.pallas_call(
        body, name="gdn_pre_bwd", grid=(3, heads),
        in_specs=[pl.BlockSpec((t, HEAD_DIM), col), pl.BlockSpec((kk, HEAD_DIM), col),
                  pl.BlockSpec((t, HEAD_DIM), col)],
        out_specs=[pl.BlockSpec((t, HEAD_DIM), col), pl.BlockSpec((kk, HEAD_DIM), col)],
        out_shape=[jax.ShapeDtypeStruct((t, 3 * heads * HEAD_DIM), BF16),
                   jax.ShapeDtypeStruct((kk, 3 * heads * HEAD_DIM), F32)],
        compiler_params=_params("parallel", "parallel"),
    )(proj, conv_w, dqkv)


def _gate_vectors(a_log, dt_bias, heads):
    pad = lambda v: jnp.pad(v.astype(F32), ((0, 0), (heads, HEAD_DIM - 2 * heads)))
    return pad(jnp.exp(a_log.astype(F32))), pad(dt_bias)


def _softplus(x):
    return jnp.maximum(x, 0.0) + jnp.log(1.0 + jnp.exp(-jnp.abs(x)))


def _gates_epilogue(heads):
    def epi(acc, ex, out, i):
        lane = lax.broadcasted_iota(jnp.int32, acc.shape, 1)
        beta = _sigmoid(acc)
        g = -ex[0][...] * _softplus(acc + ex[1][...])
        out[0][...] = acc
        out[1][...] = jnp.where(lane < heads, beta, jnp.where(lane < 2 * heads, g, 0.0))
    return epi


def _gates_backward(ba, bg, dbg, ea, dtb, heads):
    t = ba.shape[0]

    def body(ba_ref, bg_ref, d_ref, ea_ref, dt_ref, dba_ref, dal_ref, ddt_ref):
        lane = lax.broadcasted_iota(jnp.int32, (t, HEAD_DIM), 1)
        bgv = bg_ref[...]
        d = d_ref[...]
        db = d * bgv * (1.0 - bgv)
        da = -d * ea_ref[...] * _sigmoid(ba_ref[...] + dt_ref[...])
        is_g = jnp.logical_and(lane >= heads, lane < 2 * heads)
        dba = jnp.where(lane < heads, db, jnp.where(is_g, da, 0.0))
        dba_ref[...] = dba.astype(BF16)
        dal_ref[...] = jnp.sum(jnp.where(is_g, d * bgv, 0.0), axis=0, keepdims=True)
        ddt_ref[...] = jnp.sum(jnp.where(is_g, da, 0.0), axis=0, keepdims=True)

    full = pl.BlockSpec((t, HEAD_DIM), lambda: (0, 0))
    vec = pl.BlockSpec((1, HEAD_DIM), lambda: (0, 0))
    return pl.pallas_call(
        body, name="gates_bwd", grid=(),
        in_specs=[full, full, full, vec, vec], out_specs=[full, vec, vec],
        out_shape=[jax.ShapeDtypeStruct((t, HEAD_DIM), BF16), jax.ShapeDtypeStruct((1, HEAD_DIM), F32),
                   jax.ShapeDtypeStruct((1, HEAD_DIM), F32)],
        compiler_params=pltpu.CompilerParams(vmem_limit_bytes=VMEM_LIMIT),
    )(ba, bg, dbg, ea, dtb)


class _Chunk:
    pass


def _chunk_local(q, k, v, bg, head, heads):
    c = _Chunk()
    n = CHUNK
    lane = lax.broadcasted_iota(jnp.int32, bg.shape, 1)
    beta = jnp.sum(jnp.where(lane == head, bg, 0.0), axis=1, keepdims=True)
    graw = jnp.sum(jnp.where(lane == head + heads, bg, 0.0), axis=1, keepdims=True)
    row = lax.broadcasted_iota(jnp.int32, (n, n), 0)
    col = lax.broadcasted_iota(jnp.int32, (n, n), 1)
    c.tri = row >= col
    c.strict = row > col
    c.trif = c.tri.astype(F32)
    eye = row == col
    c.gcb = _hdot(c.trif, jnp.broadcast_to(graw, (n, HEAD_DIM)))
    gcol = c.gcb[:, :n]
    grow = _hdot(jnp.ones((n, n), F32), jnp.where(eye, gcol, 0.0))
    c.decay = jnp.where(c.tri, jnp.exp(jnp.where(c.tri, gcol - grow, 0.0)), 0.0)
    c.eg = jnp.exp(c.gcb)
    glast = c.gcb[n - 1:n, :]
    c.egl = jnp.exp(glast)
    c.ekl = jnp.exp(glast - c.gcb)
    c.beta = beta
    c.q = q * (HEAD_DIM ** -0.5)
    c.k = k
    c.v = v
    c.kb = k * beta
    c.vb = v * beta
    c.kg = c.kb * c.eg
    c.m1 = _bdot(c.kb, k, tb=True)
    c.L = jnp.where(c.strict, c.m1 * c.decay, 0.0)
    x = -c.L
    tinv = eye.astype(F32) + x
    p = x
    for _ in range(int(math.log2(n)) - 1):
        p = _hdot(p, p)
        tinv = tinv + _hdot(tinv, p)
    c.T = tinv
    c.u = _hdot(tinv, c.vb)
    c.w = _hdot(tinv, c.kg)
    c.m2 = _bdot(c.q, k, tb=True)
    c.A = jnp.where(c.tri, c.m2 * c.decay, 0.0)
    c.qg = c.q * c.eg
    c.kdec = k * c.ekl
    return c


def _gdn_core(qkv, bg, heads):
    t = qkv.shape[0]
    nchunk = t // CHUNK

    gw = heads * HEAD_DIM

    def body(qkv_ref, bg_ref, o_ref, s_ref, state):
        @pl.when(pl.program_id(0) == 0)
        def _():
            state[...] = jnp.zeros_like(state)

        bg_v = bg_ref[...]
        for h in range(heads):
            col = lambda s: pl.ds(s * gw + h * HEAD_DIM, HEAD_DIM)
            c = _chunk_local(qkv_ref[:, col(0)], qkv_ref[:, col(1)], qkv_ref[:, col(2)], bg_v, h, heads)
            s0 = state[h]
            s_ref[h, 0] = s0
            v_new = c.u - _bdot(c.w, s0)
            o_ref[:, pl.ds(h * HEAD_DIM, HEAD_DIM)] = _bdot(c.qg, s0) + _bdot(c.A, v_new)
            state[h] = s0 * c.egl + _bdot(c.kdec, v_new, ta=True)

    return pl.pallas_call(
        body, name="gdn_core", grid=(nchunk,),
        in_specs=[pl.BlockSpec((CHUNK, 3 * gw), lambda n: (n, 0)), pl.BlockSpec((CHUNK, HEAD_DIM), lambda n: (n, 0))],
        out_specs=[pl.BlockSpec((CHUNK, gw), lambda n: (n, 0)),
                   pl.BlockSpec((heads, 1, HEAD_DIM, HEAD_DIM), lambda n: (0, n, 0, 0))],
        out_shape=[jax.ShapeDtypeStruct((t, gw), F32),
                   jax.ShapeDtypeStruct((heads, nchunk, HEAD_DIM, HEAD_DIM), F32)],
        scratch_shapes=[pltpu.VMEM((heads, HEAD_DIM, HEAD_DIM), F32)],
        compiler_params=_params("arbitrary"),
    )(qkv, bg)


def _gdn_core_backward(qkv, bg, states, do, heads):
    t = qkv.shape[0]
    nchunk = t // CHUNK
    n = CHUNK

    def one_head(c, s0, d_out, ds1):
        v_new = c.u - _bdot(c.w, s0)
        dqg = _bdot(d_out, s0, tb=True)
        ds0 = _bdot(c.qg, d_out, ta=True) + ds1 * c.egl
        dA = jnp.where(c.tri, _bdot(d_out, v_new, tb=True), 0.0)
        dv_new = _bdot(c.A, d_out, ta=True) + _bdot(c.kdec, ds1)
        dkdec = _bdot(v_new, ds1, tb=True)
        dgl = jnp.sum(jnp.sum(ds1 * s0, axis=1, keepdims=True), axis=0, keepdims=True) * c.egl
        dw = -_bdot(dv_new, s0, tb=True)
        ds0 = ds0 - _bdot(c.w, dv_new, ta=True)
        dvb = _hdot(c.T, dv_new, ta=True)
        dkg = _hdot(c.T, dw, ta=True)
        dL = jnp.where(c.strict, -(_bdot(dvb, c.u, tb=True) + _bdot(dkg, c.w, tb=True)), 0.0)
        dm1 = dL * c.decay
        dkb = _bdot(dm1, c.k) + dkg * c.eg
        dk = _bdot(dm1, c.kb, ta=True)
        dm2 = dA * c.decay
        dq = _bdot(dm2, c.k) + dqg * c.eg
        dk = dk + _bdot(dm2, c.q, ta=True) + dkdec * c.ekl + dkb * c.beta
        pm = dL * c.L + dA * c.A
        ones = jnp.ones((n, HEAD_DIM), F32)
        rowsum = _hdot(pm, ones)
        colsum = _hdot(pm, ones, ta=True)
        tk_ = jnp.sum(dkdec * c.kdec, axis=1, keepdims=True)
        dgc = (rowsum - colsum
               + jnp.sum(dqg * c.qg, axis=1, keepdims=True)
               - tk_
               + jnp.sum(dkg * c.kg, axis=1, keepdims=True))
        dgl = dgl + jnp.sum(tk_, axis=0, keepdims=True)
        rowi = lax.broadcasted_iota(jnp.int32, (n, HEAD_DIM), 0)
        dgc = dgc + jnp.where(rowi == n - 1, dgl, 0.0)
        dgraw = _hdot(c.trif, dgc, ta=True)
        dbeta = jnp.sum(dkb * c.k, axis=1, keepdims=True) + jnp.sum(dvb * c.v, axis=1, keepdims=True)
        return dq * (HEAD_DIM ** -0.5), dk, dvb * c.beta, dbeta, dgraw, ds0

    gw = heads * HEAD_DIM

    def body(qkv_ref, bg_ref, s_ref, do_ref, dqkv_ref, dbg_ref, dstate):
        @pl.when(pl.program_id(0) == 0)
        def _():
            dstate[...] = jnp.zeros_like(dstate)

        bg_v = bg_ref[...]
        lane = lax.broadcasted_iota(jnp.int32, (n, HEAD_DIM), 1)
        for h in range(heads):
            col = lambda s: pl.ds(s * gw + h * HEAD_DIM, HEAD_DIM)
            c = _chunk_local(qkv_ref[:, col(0)], qkv_ref[:, col(1)], qkv_ref[:, col(2)], bg_v, h, heads)
            dq, dk, dv, dbeta, dgraw, ds0 = one_head(c, s_ref[h, 0], do_ref[:, pl.ds(h * HEAD_DIM, HEAD_DIM)], dstate[h])
            dqkv_ref[:, col(0)] = dq
            dqkv_ref[:, col(1)] = dk
            dqkv_ref[:, col(2)] = dv
            dbg_ref[h] = jnp.where(lane == 0, dbeta, jnp.where(lane == 1, dgraw, 0.0))
            dstate[h] = ds0

    last = nchunk - 1
    return pl.pallas_call(
        body, name="gdn_core_bwd", grid=(nchunk,),
        in_specs=[pl.BlockSpec((CHUNK, 3 * gw), lambda i: (last - i, 0)),
                  pl.BlockSpec((CHUNK, HEAD_DIM), lambda i: (last - i, 0)),
                  pl.BlockSpec((heads, 1, HEAD_DIM, HEAD_DIM), lambda i: (0, last - i, 0, 0)),
                  pl.BlockSpec((CHUNK, gw), lambda i: (last - i, 0))],
        out_specs=[pl.BlockSpec((CHUNK, 3 * gw), lambda i: (last - i, 0)),
                   pl.BlockSpec((heads, CHUNK, HEAD_DIM), lambda i: (0, last - i, 0))],
        out_shape=[jax.ShapeDtypeStruct((t, 3 * gw), F32), jax.ShapeDtypeStruct((heads, t, HEAD_DIM), F32)],
        scratch_shapes=[pltpu.VMEM((heads, HEAD_DIM, HEAD_DIM), F32)],
        compiler_params=_params("arbitrary"),
    )(qkv, bg, states, do)


def _gdn_post(o, proj, z_col0, norm_w, heads, tt):
    t = o.shape[0]
    zb = z_col0 // HEAD_DIM

    def body(o_ref, z_ref, w_ref, out_ref):
        ov = o_ref[...]
        z = z_ref[...]
        rms = lax.rsqrt(jnp.mean(ov * ov, axis=-1, keepdims=True) + NORM_EPS)
        out_ref[...] = (ov * rms * w_ref[...] * (z * _sigmoid(z))).astype(BF16)

    return pl.pallas_call(
        body, name="gdn_post", grid=(t // tt, heads),
        in_specs=[pl.BlockSpec((tt, HEAD_DIM), lambda i, h: (i, h)),
                  pl.BlockSpec((tt, HEAD_DIM), lambda i, h: (i, zb + h)),
                  pl.BlockSpec((1, HEAD_DIM), lambda i, h: (0, 0))],
        out_specs=pl.BlockSpec((tt, HEAD_DIM), lambda i, h: (i, h)),
        out_shape=jax.ShapeDtypeStruct((t, heads * HEAD_DIM), BF16),
        compiler_params=_params("parallel", "parallel"),
    )(o, proj, norm_w)


def _gdn_post_backward(dcat, o, proj, z_col0, norm_w, heads, tt):
    t = o.shape[0]
    zb = z_col0 // HEAD_DIM

    def body(d_ref, o_ref, z_ref, w_ref, do_ref, dz_ref, dw_ref):
        d = d_ref[...]
        ov = o_ref[...]
        z = z_ref[...]
        w = w_ref[...]
        rms = lax.rsqrt(jnp.mean(ov * ov, axis=-1, keepdims=True) + NORM_EPS)
        ohat = ov * rms
        sg = _sigmoid(z)
        gate = z * sg
        dz_ref[...] = (d * ohat * w * (sg * (1.0 + z * (1.0 - sg)))).astype(BF16)
        don = d * gate
        dohat = don * w
        do_ref[...] = rms * (dohat - ohat * jnp.mean(dohat * ohat, axis=-1, keepdims=True))
        dw = jnp.sum(don * ohat, axis=0, keepdims=True)
        first = jnp.logical_and(pl.program_id(0) == 0, pl.program_id(1) == 0)

        @pl.when(first)
        def _():
            dw_ref[...] = dw

        @pl.when(jnp.logical_not(first))
        def _():
            dw_ref[...] += dw

    blk = pl.BlockSpec((tt, HEAD_DIM), lambda i, h: (i, h))
    return pl.pallas_call(
        body, name="gdn_post_bwd", grid=(t // tt, heads),
        in_specs=[blk, blk, pl.BlockSpec((tt, HEAD_DIM), lambda i, h: (i, zb + h)),
                  pl.BlockSpec((1, HEAD_DIM), lambda i, h: (0, 0))],
        out_specs=[blk, blk, pl.BlockSpec((1, HEAD_DIM), lambda i, h: (0, 0))],
        out_shape=[jax.ShapeDtypeStruct((t, heads * HEAD_DIM), F32),
                   jax.ShapeDtypeStruct((t, heads * HEAD_DIM), BF16),
                   jax.ShapeDtypeStruct((1, HEAD_DIM), F32)],
        compiler_params=_params("arbitrary", "arbitrary"),
    )(dcat, o, proj, norm_w)


def _pool_select(levels, group):
    out = levels[-1]
    for gi in range(len(levels) - 2, -1, -1):
        out = jnp.where(group == gi, levels[gi], out)
    return out


def _pool_counts(t, width, group):
    pos = lax.broadcasted_iota(jnp.int32, (t, width), 0)
    win = jnp.left_shift(2, group)
    return jnp.minimum(pos + 1, win).astype(F32)


def _pooled(p, group):
    levels, s, step = [], p, 1
    for _ in POOL_WINDOWS:
        s = s + _shift_down(s, step)
        levels.append(s)
        step *= 2
    cnt = _pool_counts(p.shape[0], p.shape[1], group)
    return _pool_select(levels, group) / cnt - p, cnt


def _pool_forward(proj, p_col0, pool_w, pool_scale):
    t = proj.shape[0]
    groups, cg, _ = pool_w.shape
    pb = p_col0 // cg

    def body(p_ref, w_ref, s_ref, o_ref):
        pooled, _ = _pooled(p_ref[...], pl.program_id(0))
        o_ref[...] = (_bdot(pooled, w_ref[0]) * s_ref[...]).astype(BF16)

    return pl.pallas_call(
        body, name="pool_fwd", grid=(groups,),
        in_specs=[pl.BlockSpec((t, cg), lambda g: (0, pb + g)), pl.BlockSpec((1, cg, cg), lambda g: (g, 0, 0)),
                  pl.BlockSpec((1, cg), lambda g: (0, g))],
        out_specs=pl.BlockSpec((t, cg), lambda g: (0, g)),
        out_shape=jax.ShapeDtypeStruct((t, groups * cg), BF16),
        compiler_params=_params("parallel"),
    )(proj, pool_w, pool_scale)


def _pool_backward(dcat, d_col0, proj, p_col0, pool_w, pool_scale):
    t = proj.shape[0]
    groups, cg, _ = pool_w.shape
    pb = p_col0 // cg
    db = d_col0 // cg

    def body(d_ref, p_ref, w_ref, s_ref, dp_ref, dw_ref, ds_ref):
        group = pl.program_id(0)
        pooled, cnt = _pooled(p_ref[...], group)
        w = w_ref[0]
        d = d_ref[...]
        mixed = _bdot(pooled, w)
        ds_ref[...] = jnp.sum(d * mixed, axis=0, keepdims=True)
        dmixed = d * s_ref[...]
        dw_ref[0] = _bdot(pooled, dmixed, ta=True)
        dpooled = _bdot(dmixed, w, tb=True)
        levels, s, step = [], dpooled / cnt, 1
        for _ in POOL_WINDOWS:
            s = s + _shift_up(s, step)
            levels.append(s)
            step *= 2
        dp_ref[...] = (_pool_select(levels, group) - dpooled).astype(BF16)

    return pl.pallas_call(
        body, name="pool_bwd", grid=(groups,),
        in_specs=[pl.BlockSpec((t, cg), lambda g: (0, db + g)), pl.BlockSpec((t, cg), lambda g: (0, pb + g)),
                  pl.BlockSpec((1, cg, cg), lambda g: (g, 0, 0)), pl.BlockSpec((1, cg), lambda g: (0, g))],
        out_specs=[pl.BlockSpec((t, cg), lambda g: (0, g)), pl.BlockSpec((1, cg, cg), lambda g: (g, 0, 0)),
                   pl.BlockSpec((1, cg), lambda g: (0, g))],
        out_shape=[jax.ShapeDtypeStruct((t, groups * cg), BF16), jax.ShapeDtypeStruct((groups, cg, cg), F32),
                   jax.ShapeDtypeStruct((1, groups * cg), F32)],
        compiler_params=_params("parallel"),
    )(dcat, proj, pool_w, pool_scale)


def _attention(q, k, v, tq):
    t, d = q.shape
    m = k.shape[0]
    dh = d // XATTN_HEADS
    scale = dh ** -0.5

    def body(q_ref, k_ref, v_ref, o_ref):
        s = _bdot(q_ref[...], k_ref[...], tb=True) * scale
        s = s - jnp.max(s, axis=-1, keepdims=True)
        e = jnp.exp(s)
        p = e / jnp.sum(e, axis=-1, keepdims=True)
        o_ref[...] = _bdot(p, v_ref[...]).astype(BF16)

    return pl.pallas_call(
        body, name="xattn_fwd", grid=(XATTN_HEADS, t // tq),
        in_specs=[pl.BlockSpec((tq, dh), lambda h, i: (i, h)), pl.BlockSpec((m, dh), lambda h, i: (0, h)),
                  pl.BlockSpec((m, dh), lambda h, i: (0, h))],
        out_specs=pl.BlockSpec((tq, dh), lambda h, i: (i, h)),
        out_shape=jax.ShapeDtypeStruct((t, d), BF16),
        compiler_params=_params("parallel", "parallel"),
    )(q, k, v)


def _attention_backward(q, k, v, do, tq):
    t, d = q.shape
    m = k.shape[0]
    dh = d // XATTN_HEADS
    scale = dh ** -0.5

    def body(q_ref, k_ref, v_ref, do_ref, dq_ref, dk_ref, dv_ref, dk_acc, dv_acc):
        i = pl.program_id(1)
        qv, kv, vv, dov = q_ref[...], k_ref[...], v_ref[...], do_ref[...]
        s = _bdot(qv, kv, tb=True) * scale
        s = s - jnp.max(s, axis=-1, keepdims=True)
        e = jnp.exp(s)
        p = e / jnp.sum(e, axis=-1, keepdims=True)
        dp = _bdot(dov, vv, tb=True)
        ds = p * (dp - jnp.sum(dp * p, axis=-1, keepdims=True)) * scale
        dq_ref[...] = _bdot(ds, kv).astype(BF16)
        dv_part = _bdot(p, dov, ta=True)
        dk_part = _bdot(ds, qv, ta=True)

        @pl.when(i == 0)
        def _():
            dk_acc[...] = dk_part
            dv_acc[...] = dv_part

        @pl.when(i > 0)
        def _():
            dk_acc[...] += dk_part
            dv_acc[...] += dv_part

        @pl.when(i == pl.num_programs(1) - 1)
        def _():
            dk_ref[...] = dk_acc[...].astype(BF16)
            dv_ref[...] = dv_acc[...].astype(BF16)

    qblk = pl.BlockSpec((tq, dh), lambda h, i: (i, h))
    kblk = pl.BlockSpec((m, dh), lambda h, i: (0, h))
    return pl.pallas_call(
        body, name="xattn_bwd", grid=(XATTN_HEADS, t // tq),
        in_specs=[qblk, kblk, kblk, qblk],
        out_specs=[qblk, kblk, kblk],
        out_shape=[jax.ShapeDtypeStruct((t, d), BF16), jax.ShapeDtypeStruct((m, d), BF16),
                   jax.ShapeDtypeStruct((m, d), BF16)],
        scratch_shapes=[pltpu.VMEM((m, dh), F32), pltpu.VMEM((m, dh), F32)],
        compiler_params=_params("parallel", "arbitrary"),
    )(q, k, v, do)


def _loss_and_ln_backward(xhat, rstd, gamma, beta, target, tm):
    t, d = xhat.shape

    def body(x_ref, r_ref, g_ref, b_ref, t_ref, du_ref, dub_ref, dg_ref, db_ref, loss_ref):
        xh = x_ref[...]
        g = g_ref[...]
        diff = xh * g + b_ref[...] - t_ref[...]
        part = jnp.sum(jnp.sum(diff * diff, axis=1, keepdims=True), axis=0, keepdims=True) * (0.5 / d)
        dy = diff * (1.0 / d)
        du, dg, db = _ln_backward_math(dy, xh, r_ref[...], g)
        du_ref[...] = du
        dub_ref[...] = du.astype(BF16)
        lossrow = jnp.broadcast_to(part, (1, HEAD_DIM))
        first = pl.program_id(0) == 0

        @pl.when(first)
        def _():
            dg_ref[...] = dg
            db_ref[...] = db
            loss_ref[...] = lossrow

        @pl.when(jnp.logical_not(first))
        def _():
            dg_ref[...] += dg
            db_ref[...] += db
            loss_ref[...] += lossrow

    row = pl.BlockSpec((tm, d), lambda i: (i, 0))
    vec = pl.BlockSpec((1, d), lambda i: (0, 0))
    return pl.pallas_call(
        body, name="loss_ln3_bwd", grid=(t // tm,),
        in_specs=[row, pl.BlockSpec((tm, 1), lambda i: (i, 0)), vec, vec, row],
        out_specs=[row, row, vec, vec, pl.BlockSpec((1, HEAD_DIM), lambda i: (0, 0))],
        out_shape=[jax.ShapeDtypeStruct((t, d), F32), jax.ShapeDtypeStruct((t, d), BF16),
                   jax.ShapeDtypeStruct((1, d), F32), jax.ShapeDtypeStruct((1, d), F32),
                   jax.ShapeDtypeStruct((1, HEAD_DIM), F32)],
        compiler_params=_params("arbitrary"),
    )(xhat, rstd, gamma, beta, target)


def _pick(n, prefs):
    for p in prefs:
        if n % p == 0:
            return p
    return n


def _local_step(x, mem, target, w):
    t, d = x.shape
    heads = w["a_log"].shape[1]
    gw = heads * HEAD_DIM
    groups, cg, _ = w["pool_w"].shape
    pw = groups * cg
    n_main = 4 * gw + pw
    in_cols = n_main + 2 * heads
    s_in = w["w_in3"].shape[0]

    tm = _pick(t, (512, 256, 128))
    tm_ln = _pick(t, (256, 128))
    tk = _pick(d, (512, 256, 128))

    w_in = jnp.concatenate([w["w_in3"][s] for s in range(s_in)], axis=1)
    w_main = jnp.concatenate([w_in[:, :4 * gw], w_in[:, 4 * gw + 2 * heads:]], axis=1)
    w_ba = jnp.pad(w_in[:, 4 * gw:4 * gw + 2 * heads], ((0, 0), (0, HEAD_DIM - 2 * heads)))
    x_bf = x.astype(BF16)
    mem_bf = mem.astype(BF16)

    proj = _plain("proj_main", x_bf, w_main, tm=tm, tn=_pick(n_main, (1024, 512, 256, 128)), tk=tk, out_dtype=F32)
    ea, dtb = _gate_vectors(w["a_log"], w["dt_bias"], heads)
    vec128 = lambda i, j: (0, 0)
    ba, bg = _matmul(
        "proj_gates", x_bf, w_ba, tm=tm, tn=HEAD_DIM, tk=tk,
        extra=[(ea, (1, HEAD_DIM), vec128), (dtb, (1, HEAD_DIM), vec128)],
        outs=[(jax.ShapeDtypeStruct((t, HEAD_DIM), F32), (tm, HEAD_DIM), _tile)] * 2,
        epilogue=_gates_epilogue(heads))
    qkv = _gdn_pre(proj, w["conv_w"], heads)
    o_gdn, states = _gdn_core(qkv, bg, heads)
    cat_g = _gdn_post(o_gdn, proj, 3 * gw, w["gdn_norm_w"], heads, tm)
    cat_p = _pool_forward(proj, 4 * gw, w["pool_w"], w["pool_scale"])
    cat = jnp.concatenate([cat_g, cat_p], axis=1)
    h1, h1_bf, xhat1, rstd1 = _ln_forward("mix_ln1", cat, w["w_out"], x, w["ln1_g"], w["ln1_b"], tm=tm_ln, tk=tk)

    tn_d = _pick(d, (1024, 512, 256, 128))
    q = _plain("xattn_q", h1_bf, w["xq_w"], tm=tm, tn=tn_d, tk=tk, out_dtype=BF16)
    mlen = mem.shape[0]
    tm_mem = _pick(mlen, (256, 128))
    k = _plain("xattn_k", mem_bf, w["xk_w"], tm=tm_mem, tn=tn_d, tk=tk, out_dtype=BF16)
    v = _plain("xattn_v", mem_bf, w["xv_w"], tm=tm_mem, tn=tn_d, tk=tk, out_dtype=BF16)
    att = _attention(q, k, v, tm)
    h2, h2_bf, xhat2, rstd2 = _ln_forward("xo_ln2", att, w["xo_w"], h1, w["ln2_g"], w["ln2_b"], tm=tm_ln, tk=tk)

    s_up = w["w_up3"].shape[0]
    ff = s_up * w["w_up3"].shape[2]
    tn_f = _pick(ff // s_up, (1024, 512, 256, 128))

    def up_epi(acc, ex, out, i):
        r = jnp.maximum(acc, 0.0)
        out[0][...] = (r * r).astype(BF16)
        out[1][...] = (2.0 * r).astype(BF16)

    act, act_grad = _matmul(
        "mlp_up", h2_bf, w["w_up3"], b_blocks=s_up, tm=tm, tn=tn_f, tk=tk,
        outs=[(jax.ShapeDtypeStruct((t, ff), BF16), (tm, tn_f), _tile)] * 2, epilogue=up_epi)
    tk_f = _pick(ff, (512, 256, 128))
    _, _, xhat3, rstd3 = _ln_forward("down_ln3", act, w["w_down"], h2, w["ln3_g"], w["ln3_b"], tm=tm_ln, tk=tk_f)

    grads = {}
    du3, du3_bf, grads["ln3_g"], grads["ln3_b"], loss = _loss_and_ln_backward(
        xhat3, rstd3, w["ln3_g"], w["ln3_b"], target, tm_ln)

    def dup_epi(acc, ex, out, i):
        out[0][...] = (acc * ex[0][...].astype(F32)).astype(BF16)

    dup = _matmul(
        "mlp_down_dx", du3_bf, w["w_down"], tb=True, tm=tm, tn=tn_f, tk=tk,
        extra=[(act_grad, (tm, tn_f), _tile)],
        outs=[(jax.ShapeDtypeStruct((t, ff), BF16), (tm, tn_f), _tile)], epilogue=dup_epi)[0]
    tk_t = _pick(t, (512, 256, 128))
    tm_w = _pick(d, (512, 256, 128))
    grads["w_down"] = _plain("mlp_down_dw", act, du3_bf, ta=True, tm=_pick(ff, (512, 256, 128)), tn=tn_d, tk=tk_t,
                             out_dtype=F32)
    grads["w_up3"] = _plain("mlp_up_dw", h2_bf, dup, ta=True, tm=tm_w, tn=tn_f, tk=tk_t, out_dtype=F32, out3=s_up)
    du2, du2_bf, grads["ln2_g"], grads["ln2_b"] = _ln_backward(
        "mlp_up_dx_ln2", dup, w["w_up3"], du3, xhat2, rstd2, w["ln2_g"], tm=tm_ln,
        tk=_pick(ff // s_up, (512, 256, 128)), b_blocks=s_up)

    grads["xo_w"] = _plain("xo_dw", att, du2_bf, ta=True, tm=tm_w, tn=tn_d, tk=tk_t, out_dtype=F32)
    datt = _plain("xo_dx", du2_bf, w["xo_w"], tb=True, tm=tm, tn=tn_d, tk=tk, out_dtype=BF16)
    dq, dk, dv = _attention_backward(q, k, v, datt, tm)
    tk_m = _pick(mlen, (256, 128))
    grads["xq_w"] = _plain("xq_dw", h1_bf, dq, ta=True, tm=tm_w, tn=tn_d, tk=tk_t, out_dtype=F32)
    grads["xk_w"] = _plain("xk_dw", mem_bf, dk, ta=True, tm=tm_w, tn=tn_d, tk=tk_m, out_dtype=F32)
    grads["xv_w"] = _plain("xv_dw", mem_bf, dv, ta=True, tm=tm_w, tn=tn_d, tk=tk_m, out_dtype=F32)
    du1, du1_bf, grads["ln1_g"], grads["ln1_b"] = _ln_backward(
        "xq_dx_ln1", dq, w["xq_w"], du2, xhat1, rstd1, w["ln1_g"], tm=tm_ln, tk=tk)

    grads["w_out"] = _plain("out_dw", cat, du1_bf, ta=True, tm=tm_w, tn=tn_d, tk=tk_t, out_dtype=F32)
    dcat = _plain("out_dx", du1_bf, w["w_out"], tb=True, tm=tm, tn=tn_d, tk=tk, out_dtype=F32)
    dp, grads["pool_w"], grads["pool_scale"] = _pool_backward(dcat, gw, proj, 4 * gw, w["pool_w"], w["pool_scale"])
    do_gdn, dz, grads["gdn_norm_w"] = _gdn_post_backward(dcat, o_gdn, proj, 3 * gw, w["gdn_norm_w"], heads, tm)
    dqkv, dbg_h = _gdn_core_backward(qkv, bg, states, do_gdn, heads)
    dqkv_pre, grads["conv_w"] = _gdn_pre_backward(proj, w["conv_w"], dqkv, heads)
    dbg = jnp.concatenate([dbg_h[:, :, 0].T, dbg_h[:, :, 1].T], axis=1)
    dbg = jnp.pad(dbg, ((0, 0), (0, HEAD_DIM - 2 * heads)))
    dba, dalog_row, ddt_row = _gates_backward(ba, bg, dbg, ea, dtb, heads)
    grads["a_log"] = dalog_row[:, heads:2 * heads]
    grads["dt_bias"] = ddt_row[:, heads:2 * heads]

    dproj = jnp.concatenate([dqkv_pre, dz, dp], axis=1)
    tn_main = _pick(n_main, (1024, 512, 256, 128))
    dw_main = _plain("proj_dw", x_bf, dproj, ta=True, tm=tm_w, tn=tn_main, tk=tk_t, out_dtype=F32)
    dw_ba = _plain("proj_gates_dw", x_bf, dba, ta=True, tm=tm_w, tn=HEAD_DIM, tk=tk_t, out_dtype=F32)
    dw_in = jnp.concatenate([dw_main[:, :4 * gw], dw_ba[:, :2 * heads], dw_main[:, 4 * gw:]], axis=1)
    per = in_cols // s_in
    grads["w_in3"] = jnp.stack([dw_in[:, s * per:(s + 1) * per] for s in range(s_in)], axis=0)

    def dx_epi(acc, ex, out, i):
        out[0][...] = acc + ex[1][...] + ALPHA * ex[0][...]

    dx_gates = _plain("proj_gates_dx", dba, w_ba, tb=True, tm=tm, tn=tn_d, tk=HEAD_DIM, out_dtype=F32)
    grad_x = _matmul(
        "proj_dx", dproj, w_main, tb=True, tm=tm, tn=tn_d, tk=_pick(n_main, (512, 256, 128)),
        extra=[(du1, (tm, tn_d), _tile), (dx_gates, (tm, tn_d), _tile)],
        outs=[(jax.ShapeDtypeStruct((t, d), F32), (tm, tn_d), _tile)], epilogue=dx_epi)[0]
    return loss, grad_x, grads


def _adamw(name, w, g, m, v):
    r, c = w.shape
    tr = _pick(r, (256, 128, 64, 32, 16, 8))
    c1 = 1.0 - ADAM_B1 ** ADAM_STEP
    c2 = 1.0 - ADAM_B2 ** ADAM_STEP

    def body(w_ref, g_ref, m_ref, v_ref, d_ref, mo_ref, vo_ref):
        gv = g_ref[...]
        mn = ADAM_B1 * m_ref[...] + (1.0 - ADAM_B1) * gv
        vn = ADAM_B2 * v_ref[...] + (1.0 - ADAM_B2) * (gv * gv)
        d_ref[...] = -ADAM_LR * ((mn / c1) / (jnp.sqrt(vn / c2) + ADAM_EPS) + ADAM_WD * w_ref[...])
        mo_ref[...] = mn
        vo_ref[...] = vn

    blk = pl.BlockSpec((tr, c), lambda i: (i, 0))
    return pl.pallas_call(
        body, name=name, grid=(r // tr,), in_specs=[blk] * 4, out_specs=[blk] * 3,
        out_shape=[jax.ShapeDtypeStruct((r, c), F32)] * 3,
        compiler_params=_params("parallel"),
    )(w, g, m, v)


def _place():
    x, y, c = lax.axis_index("x"), lax.axis_index("y"), lax.axis_index("c")
    chips = [(1 - x, y), (x, 1 - y), (1 - x, 1 - y)]
    return x, y, c, chips


HBM = pl.BlockSpec(memory_space=pltpu.HBM)


def _gather_weights(shards):
    n = len(shards)
    slots = 7

    def body(*refs):
        ins, outs = refs[:n], refs[n:2 * n]
        send_sems, recv_sems = refs[2 * n:]
        x, y, c, chips = _place()
        me = 2 * x + y
        sibling = (x, y, 1 - c)

        def half(i, shard_index, which):
            rows = ins[i].shape[0] // 2
            return outs[i].at[shard_index, pl.ds(which * rows, rows)]

        def remote(i, slot, src, dst, to):
            return pltpu.make_async_remote_copy(src_ref=src, dst_ref=dst, send_sem=send_sems.at[slots * i + slot],
                                                recv_sem=recv_sems.at[slots * i + slot], device_id=to,
                                                device_id_type=MESH)

        started = []
        for i in range(n):
            cp = remote(i, 6, ins[i], outs[i].at[me], sibling)
            cp.start()
            started.append(cp)
        for i in range(n):
            rows = ins[i].shape[0] // 2
            for j, chip in enumerate(chips):
                cp = remote(i, j, ins[i].at[pl.ds(c * rows, rows)], half(i, me, c), (*chip, c))
                cp.start()
                started.append(cp)
        for i in range(n):
            for j, chip in enumerate(chips):
                theirs = 2 * chip[0] + chip[1]
                landed = half(i, theirs, c)
                remote(i, j, landed, landed, (*chip, c)).wait_recv()
                cp = remote(i, 3 + j, landed, landed, sibling)
                cp.start()
                started.append(cp)
        for i in range(n):
            for j, chip in enumerate(chips):
                theirs = 2 * chip[0] + chip[1]
                passed = half(i, theirs, 1 - c)
                remote(i, 3 + j, passed, passed, sibling).wait_recv()
        for i in range(n):
            remote(i, 6, ins[i], outs[i].at[me], sibling).wait_recv()
        for cp in started:
            cp.wait_send()

    return pl.pallas_call(
        body, name="gather_weights",
        in_specs=[HBM] * n, out_specs=[HBM] * n,
        out_shape=[jax.ShapeDtypeStruct((N_SHARD,) + s.shape, s.dtype) for s in shards],
        scratch_shapes=[pltpu.SemaphoreType.DMA((slots * n,)), pltpu.SemaphoreType.DMA((slots * n,))],
    )(*shards)


def _all_reduce_small(slab):
    r, width = slab.shape
    ndev = 8

    def body(x_ref, out_ref, buf, send_sems, recv_sems):
        x, y, c, _ = _place()
        me = 4 * x + 2 * y + c
        buf[me] = x_ref[...]
        copies = []
        for k in range(1, ndev):
            peer = jnp.bitwise_xor(me, k)
            to = (peer // 4, (peer // 2) % 2, peer % 2)
            cp = pltpu.make_async_remote_copy(src_ref=x_ref, dst_ref=buf.at[me], send_sem=send_sems.at[k - 1],
                                              recv_sem=recv_sems.at[k - 1], device_id=to, device_id_type=MESH)
            cp.start()
            copies.append(cp)
        for k in range(1, ndev):
            peer = jnp.bitwise_xor(me, k)
            pltpu.make_async_remote_copy(src_ref=x_ref, dst_ref=buf.at[peer], send_sem=send_sems.at[k - 1],
                                         recv_sem=recv_sems.at[k - 1], device_id=(x, y, c),
                                         device_id_type=MESH).wait_recv()
        for cp in copies:
            cp.wait_send()
        total = buf[0]
        for d in range(1, ndev):
            total = total + buf[d]
        out_ref[...] = total

    return pl.pallas_call(
        body, name="all_reduce_small",
        in_specs=[pl.BlockSpec(memory_space=pltpu.VMEM)], out_specs=pl.BlockSpec(memory_space=pltpu.VMEM),
        out_shape=jax.ShapeDtypeStruct((r, width), F32),
        scratch_shapes=[pltpu.VMEM((ndev, r, width), F32), pltpu.SemaphoreType.DMA((ndev - 1,)),
                        pltpu.SemaphoreType.DMA((ndev - 1,))],
        compiler_params=pltpu.CompilerParams(vmem_limit_bytes=VMEM_LIMIT),
    )(slab)


def _swap_halves(grads):
    n = len(grads)

    def body(*refs):
        ins, outs = refs[:n], refs[n:2 * n]
        send_sems, recv_sems = refs[2 * n:]
        x, y, c, _ = _place()
        copies = []
        for i in range(n):
            rows = ins[i].shape[1] // 2
            for s in range(N_SHARD):
                cp = pltpu.make_async_remote_copy(
                    src_ref=ins[i].at[s, pl.ds((1 - c) * rows, rows)], dst_ref=outs[i].at[s],
                    send_sem=send_sems.at[N_SHARD * i + s], recv_sem=recv_sems.at[N_SHARD * i + s],
                    device_id=(x, y, 1 - c), device_id_type=MESH)
                cp.start()
                copies.append(cp)
        for cp in copies:
            cp.wait()

    return pl.pallas_call(
        body, name="grad_swap_halves",
        in_specs=[HBM] * n, out_specs=[HBM] * n,
        out_shape=[jax.ShapeDtypeStruct((N_SHARD, g.shape[1] // 2, g.shape[2]), F32) for g in grads],
        scratch_shapes=[pltpu.SemaphoreType.DMA((N_SHARD * n,)), pltpu.SemaphoreType.DMA((N_SHARD * n,))],
    )(*grads)


def _chip_partial(name, grad, other, core):
    s, r, cdim = grad.shape
    rows = r // 2
    tr = _pick(rows, (256, 128, 64, 32, 16))
    nb = rows // tr

    def body(core_ref, g_ref, o_ref, out_ref):
        out_ref[...] = (g_ref[...] + o_ref[...]).astype(BF16)

    return pl.pallas_call(
        body, name=name,
        grid_spec=pltpu.PrefetchScalarGridSpec(
            num_scalar_prefetch=1, grid=(s, nb),
            in_specs=[pl.BlockSpec((None, tr, cdim), lambda j, b, core_ref: (j, core_ref[0] * nb + b, 0)),
                      pl.BlockSpec((None, tr, cdim), lambda j, b, core_ref: (j, b, 0))],
            out_specs=pl.BlockSpec((None, tr, cdim), lambda j, b, core_ref: (j, b, 0))),
        out_shape=jax.ShapeDtypeStruct((s, rows, cdim), BF16),
        compiler_params=_params("parallel", "parallel"),
    )(core, grad, other)


def _send_partials(partials):
    n = len(partials)

    def body(*refs):
        ins, outs = refs[:n], refs[n:2 * n]
        send_sems, recv_sems = refs[2 * n:]
        x, y, c, chips = _place()
        copies = []
        for i in range(n):
            for j, chip in enumerate(chips):
                theirs = 2 * chip[0] + chip[1]
                cp = pltpu.make_async_remote_copy(
                    src_ref=ins[i].at[theirs], dst_ref=outs[i].at[j],
                    send_sem=send_sems.at[3 * i + j], recv_sem=recv_sems.at[3 * i + j],
                    device_id=(*chip, c), device_id_type=MESH)
                cp.start()
                copies.append(cp)
        for cp in copies:
            cp.wait()

    return pl.pallas_call(
        body, name="grad_send_partials",
        in_specs=[HBM] * n, out_specs=[HBM] * n,
        out_shape=[jax.ShapeDtypeStruct((3,) + p.shape[1:], BF16) for p in partials],
        scratch_shapes=[pltpu.SemaphoreType.DMA((3 * n,)), pltpu.SemaphoreType.DMA((3 * n,))],
    )(*partials)


def _reduce_own(name, grad, other, received, where):
    s, r, cdim = grad.shape
    rows = r // 2
    tr = _pick(rows, (256, 128, 64, 32, 16))
    nb = rows // tr

    def body(where_ref, g_ref, o_ref, r_ref, out_ref):
        total = g_ref[...] + o_ref[...]
        for j in range(3):
            total = total + r_ref[j].astype(F32)
        out_ref[...] = total

    return pl.pallas_call(
        body, name=name,
        grid_spec=pltpu.PrefetchScalarGridSpec(
            num_scalar_prefetch=1, grid=(nb,),
            in_specs=[pl.BlockSpec((None, tr, cdim), lambda b, w_ref: (w_ref[0], w_ref[1] * nb + b, 0)),
                      pl.BlockSpec((None, tr, cdim), lambda b, w_ref: (w_ref[0], b, 0)),
                      pl.BlockSpec((3, tr, cdim), lambda b, w_ref: (0, b, 0))],
            out_specs=pl.BlockSpec((tr, cdim), lambda b, w_ref: (w_ref[1] * nb + b, 0))),
        out_shape=jax.ShapeDtypeStruct((r, cdim), F32),
        compiler_params=_params("parallel"),
    )(where, grad, other, received)


def _join_halves(halves):
    n = len(halves)

    def body(*refs):
        bufs = refs[n:2 * n]
        send_sems, recv_sems = refs[2 * n:]
        x, y, c, _ = _place()
        copies = []
        for i in range(n):
            rows = bufs[i].shape[0] // 2
            mine = bufs[i].at[pl.ds(c * rows, rows)]
            cp = pltpu.make_async_remote_copy(src_ref=mine, dst_ref=mine, send_sem=send_sems.at[i],
                                              recv_sem=recv_sems.at[i], device_id=(x, y, 1 - c), device_id_type=MESH)
            cp.start()
            copies.append(cp)
        for i, cp in enumerate(copies):
            rows = bufs[i].shape[0] // 2
            theirs = bufs[i].at[pl.ds((1 - c) * rows, rows)]
            pltpu.make_async_remote_copy(src_ref=theirs, dst_ref=theirs, send_sem=send_sems.at[i],
                                         recv_sem=recv_sems.at[i], device_id=(x, y, 1 - c),
                                         device_id_type=MESH).wait_recv()
            cp.wait_send()

    return pl.pallas_call(
        body, name="grad_join_halves",
        in_specs=[HBM] * n, out_specs=[HBM] * n,
        out_shape=[jax.ShapeDtypeStruct(h.shape, F32) for h in halves],
        input_output_aliases={i: i for i in range(n)},
        scratch_shapes=[pltpu.SemaphoreType.DMA((n,)), pltpu.SemaphoreType.DMA((n,))],
    )(*halves)


BIG = ("w_in", "pool_w", "w_out", "xq_w", "xk_w", "xv_w", "xo_w", "w_up", "w_down")
SMALL = ("conv_w", "a_log", "dt_bias", "gdn_norm_w", "pool_scale", "ln1_g", "ln1_b", "ln2_g", "ln2_b", "ln3_g", "ln3_b")
ORDER = ("w_in", "conv_w", "a_log", "dt_bias", "gdn_norm_w", "pool_w", "pool_scale", "w_out", "ln1_g", "ln1_b",
         "xq_w", "xk_w", "xv_w", "xo_w", "ln2_g", "ln2_b", "w_up", "w_down", "ln3_g", "ln3_b")
LANES = 128


def _rows(flat_len):
    return -(-flat_len // LANES)


def _pack(pieces):
    out = []
    for p in pieces:
        flat = p.reshape(-1).astype(F32)
        out.append(jnp.pad(flat, (0, _rows(flat.shape[0]) * LANES - flat.shape[0])).reshape(-1, LANES))
    slab = jnp.concatenate(out, axis=0)
    return jnp.pad(slab, ((0, -slab.shape[0] % 8), (0, 0)))


def _unpack(slab, shapes):
    out, row = [], 0
    for shp in shapes:
        size = math.prod(shp)
        out.append(slab[row:row + _rows(size)].reshape(-1)[:size].reshape(shp))
        row += _rows(size)
    return out


def _as2d(a):
    a = a[0]
    return a.reshape(-1, a.shape[-1]) if a.ndim == 3 else a


def kernel(x, mem, w_in, conv_w, a_log, dt_bias, gdn_norm_w, pool_w, pool_scale, w_out, ln1_g, ln1_b, xq_w, xk_w, xv_w, xo_w, ln2_g, ln2_b, w_up, w_down, ln3_g, ln3_b, loss_target, m_w_in, m_conv_w, m_a_log, m_dt_bias, m_gdn_norm_w, m_pool_w, m_pool_scale, m_w_out, m_ln1_g, m_ln1_b, m_xq_w, m_xk_w, m_xv_w, m_xo_w, m_ln2_g, m_ln2_b, m_w_up, m_w_down, m_ln3_g, m_ln3_b, v_w_in, v_conv_w, v_a_log, v_dt_bias, v_gdn_norm_w, v_pool_w, v_pool_scale, v_w_out, v_ln1_g, v_ln1_b, v_xq_w, v_xk_w, v_xv_w, v_xo_w, v_ln2_g, v_ln2_b, v_w_up, v_w_down, v_ln3_g, v_ln3_b):
    given = dict(locals())
    cx, cy, cc = lax.axis_index("x"), lax.axis_index("y"), lax.axis_index("c")
    me = 2 * cx + cy
    groups = pool_w.shape[1]
    cs = pool_w.shape[2]
    kk, conv_cols = conv_w.shape[1], conv_w.shape[2]

    shards = [_as2d(given[n]).astype(BF16) for n in BIG]
    full = dict(zip(BIG, _gather_weights(shards)))
    wts = {n: full[n].reshape(-1, full[n].shape[-1]) for n in ("w_out", "xq_w", "xk_w", "xv_w", "xo_w", "w_down")}
    wts["w_in3"] = full["w_in"]
    wts["w_up3"] = full["w_up"]
    wts["pool_w"] = full["pool_w"].reshape(N_SHARD, groups, cs, -1).transpose(1, 0, 2, 3).reshape(groups, N_SHARD * cs, -1)
    conv_slab = jnp.zeros((kk, N_SHARD * conv_cols), F32)
    conv_slab = lax.dynamic_update_slice(conv_slab, conv_w[0] * (cc == 0).astype(F32), (0, me * conv_cols))
    wts["conv_w"] = _unpack(_all_reduce_small(_pack([conv_slab])), [conv_slab.shape])[0]
    for n in ("a_log", "dt_bias", "gdn_norm_w", "pool_scale", "ln1_g", "ln1_b", "ln2_g", "ln2_b", "ln3_g", "ln3_b"):
        wts[n] = given[n]

    loss_row, grad_x, g = _local_step(x[0], mem[0], loss_target[0], wts)

    small_names = ("a_log", "dt_bias", "gdn_norm_w", "pool_scale", "ln1_g", "ln1_b", "ln2_g", "ln2_b", "ln3_g", "ln3_b")
    pieces = [g["conv_w"]] + [g[n] for n in small_names] + [loss_row[:, :1]]
    shapes = [p.shape for p in pieces]
    summed = _unpack(_all_reduce_small(_pack(pieces)), shapes)
    gsmall = dict(zip(small_names, summed[1:-1]))
    gsmall["conv_w"] = lax.dynamic_slice(summed[0], (0, me * conv_cols), (kk, conv_cols))
    loss = summed[-1][0, 0]

    gpool = g["pool_w"].reshape(groups, N_SHARD, cs, -1).transpose(1, 0, 2, 3).reshape(N_SHARD, groups * cs, -1)
    blocks = {"w_in": g["w_in3"], "w_up": g["w_up3"], "pool_w": gpool}
    for n in ("w_out", "xq_w", "xk_w", "xv_w", "xo_w", "w_down"):
        blocks[n] = g[n].reshape(N_SHARD, -1, g[n].shape[-1])
    big = [blocks[n] for n in BIG]
    others = _swap_halves(big)
    core = cc.astype(jnp.int32).reshape(1)
    where = jnp.stack([me, cc]).astype(jnp.int32)
    partials = [_chip_partial("chip_partial_" + n, gb, ob, core) for n, gb, ob in zip(BIG, big, others)]
    received = _send_partials(partials)
    halves = [_reduce_own("reduce_own_" + n, gb, ob, rb, where) for n, gb, ob, rb in zip(BIG, big, others, received)]
    gbig = dict(zip(BIG, _join_halves(halves)))

    grad, delta, new_m, new_v = {}, {}, {}, {}
    for n in BIG:
        shp = given[n].shape
        d2, m2, v2 = _adamw("adamw_" + n, _as2d(given[n]), gbig[n], _as2d(given["m_" + n]), _as2d(given["v_" + n]))
        grad[n], delta[n], new_m[n], new_v[n] = (a.reshape(shp) for a in (gbig[n], d2, m2, v2))
    sshapes = [given[n].shape for n in SMALL]
    slabs = [_pack([given[p + n] for n in SMALL]) for p in ("", "m_", "v_")]
    gslab = _pack([gsmall[n] for n in SMALL])
    outs = _adamw("adamw_small", slabs[0], gslab, slabs[1], slabs[2])
    for dst, slab in zip((delta, new_m, new_v), outs):
        dst.update(zip(SMALL, _unpack(slab, sshapes)))
    for n in SMALL:
        grad[n] = gsmall[n].reshape(given[n].shape)

    return (loss, grad_x[None], *[grad[n] for n in ORDER], *[delta[n] for n in ORDER],
            *[new_m[n] for n in ORDER], *[new_v[n] for n in ORDER])
```

```python
import functools
import math

import jax
import jax.numpy as jnp
from jax import lax
from jax.experimental import pallas as pl
from jax.experimental.pallas import tpu as pltpu

F32 = jnp.float32
BF16 = jnp.bfloat16
MESH = pl.DeviceIdType.MESH

HEAD_DIM = 128
CHUNK = 64
POOL_WINDOWS = (2, 4, 8, 16)
XATTN_HEADS = 4
ALPHA = 2.0 ** 0.25
LN_EPS = 1e-5
NORM_EPS = 1e-6
ADAM_LR, ADAM_B1, ADAM_B2, ADAM_EPS, ADAM_WD, ADAM_STEP = 0.001, 0.9, 0.999, 1e-08, 0.01, 10
N_SHARD = 4
VMEM_LIMIT = 56 * 1024 * 1024
K_STEPS = (2048, 1024, 512, 256, 128)


def _params(*sem):
    return pltpu.CompilerParams(dimension_semantics=sem, vmem_limit_bytes=VMEM_LIMIT)


def _bdot(a, b, ta=False, tb=False):
    dims = (((0 if ta else 1,), (1 if tb else 0,)), ((), ()))
    return lax.dot_general(a.astype(BF16), b.astype(BF16), dims, preferred_element_type=F32)


def _sigmoid(x):
    return 1.0 / (1.0 + jnp.exp(-x))


def _matmul(name, a, b, *, ta=False, tb=False, tm, tn, tk, extra=(), outs, epilogue, b_blocks=None,
            sequential=False):
    m, k_dim = (a.shape[1], a.shape[0]) if ta else a.shape
    if b_blocks and tb:
        n = b.shape[1]
        k_dim = b.shape[0] * b.shape[2]
        per = b.shape[2] // tk
        b_spec = pl.BlockSpec((None, tn, tk), lambda i, j, k: (k // per, j, k % per))
    elif b_blocks:
        n = b.shape[0] * b.shape[2]
        per = b.shape[2] // tn
        b_spec = pl.BlockSpec((None, tk, tn), lambda i, j, k: (j // per, k, j % per))
    elif tb:
        n = b.shape[0]
        b_spec = pl.BlockSpec((tn, tk), lambda i, j, k: (j, k))
    else:
        n = b.shape[1]
        b_spec = pl.BlockSpec((tk, tn), lambda i, j, k: (k, j))
    assert m % tm == 0 and n % tn == 0 and k_dim % tk == 0, (name, m, n, k_dim, tm, tn, tk)
    nk = k_dim // tk
    a_spec = pl.BlockSpec((tk, tm), lambda i, j, k: (k, i)) if ta else pl.BlockSpec((tm, tk), lambda i, j, k: (i, k))
    n_extra, n_out = len(extra), len(outs)

    def wrap(index_map):
        return lambda i, j, k: index_map(i, j)

    def body_one_step(*refs):
        ex = refs[2:2 + n_extra]
        out = refs[2 + n_extra:2 + n_extra + n_out]
        epilogue(_bdot(refs[0][...], refs[1][...], ta, tb), ex, out, pl.program_id(0))

    def body(*refs):
        a_ref, b_ref = refs[0], refs[1]
        ex = refs[2:2 + n_extra]
        out = refs[2 + n_extra:2 + n_extra + n_out]
        acc = refs[-1]
        i, k = pl.program_id(0), pl.program_id(2)
        part = _bdot(a_ref[...], b_ref[...], ta, tb)

        @pl.when(k == 0)
        def _():
            acc[...] = part

        @pl.when(jnp.logical_and(k > 0, k < nk - 1))
        def _():
            acc[...] += part

        @pl.when(k == nk - 1)
        def _():
            epilogue(acc[...] + part, ex, out, i)

    sem = ("arbitrary",) * 3 if sequential else ("parallel", "parallel", "arbitrary")
    res = pl.pallas_call(
        body_one_step if nk == 1 else body, name=name, grid=(m // tm, n // tn, nk),
        in_specs=[a_spec, b_spec] + [pl.BlockSpec(bs, wrap(im)) for _, bs, im in extra],
        out_specs=[pl.BlockSpec(bs, wrap(im)) for _, bs, im in outs],
        out_shape=[s for s, _, _ in outs],
        scratch_shapes=[] if nk == 1 else [pltpu.VMEM((tm, tn), F32)],
        compiler_params=_params(*sem),
    )(a, b, *[x for x, _, _ in extra])
    return res


def _tile(i, j):
    return (i, j)


def _plain(name, a, b, *, ta=False, tb=False, tm, tn, tk, out_dtype, b_blocks=None, out3=None):
    m = a.shape[1] if ta else a.shape[0]
    n = (b.shape[0] * b.shape[2]) if b_blocks else (b.shape[0] if tb else b.shape[1])

    def epi(acc, ex, out, i):
        out[0][...] = acc.astype(out_dtype)

    if out3:
        per = (n // out3) // tn
        spec = (jax.ShapeDtypeStruct((out3, m, n // out3), out_dtype), (None, tm, tn),
                lambda i, j: (j // per, i, j % per))
    else:
        spec = (jax.ShapeDtypeStruct((m, n), out_dtype), (tm, tn), _tile)
    return _matmul(name, a, b, ta=ta, tb=tb, tm=tm, tn=tn, tk=tk, outs=[spec], epilogue=epi,
                   b_blocks=b_blocks)[0]


def _ln_forward(name, a, b, res, gamma, beta, *, tm, tk, want_bf16=True):
    m, n = res.shape

    def epi(acc, ex, out, i):
        u = ALPHA * ex[0][...] + acc
        mu = jnp.mean(u, axis=-1, keepdims=True)
        xc = u - mu
        var = jnp.mean(xc * xc, axis=-1, keepdims=True)
        rstd = lax.rsqrt(var + LN_EPS)
        xhat = xc * rstd
        h = xhat * ex[1][...] + ex[2][...]
        out[0][...] = h
        out[1][...] = h.astype(BF16)
        out[2][...] = xhat
        out[3][...] = rstd

    row = lambda i, j: (i, 0)
    vec = lambda i, j: (0, 0)
    return _matmul(
        name, a, b, tm=tm, tn=n, tk=tk,
        extra=[(res, (tm, n), row), (gamma, (1, n), vec), (beta, (1, n), vec)],
        outs=[(jax.ShapeDtypeStruct((m, n), F32), (tm, n), row),
              (jax.ShapeDtypeStruct((m, n), BF16), (tm, n), row),
              (jax.ShapeDtypeStruct((m, n), F32), (tm, n), row),
              (jax.ShapeDtypeStruct((m, 1), F32), (tm, 1), row)],
        epilogue=epi)


def _ln_backward_math(dy, xhat, rstd, gamma):
    dxhat = dy * gamma
    m1 = jnp.mean(dxhat, axis=-1, keepdims=True)
    m2 = jnp.mean(dxhat * xhat, axis=-1, keepdims=True)
    du = rstd * (dxhat - m1 - xhat * m2)
    return du, jnp.sum(dy * xhat, axis=0, keepdims=True), jnp.sum(dy, axis=0, keepdims=True)


def _ln_backward(name, a, b, dres, xhat, rstd, gamma, *, tm, tk, b_blocks=None, tb=True):
    m, n = dres.shape

    def epi(acc, ex, out, i):
        dy = acc + ALPHA * ex[0][...]
        du, dg, db = _ln_backward_math(dy, ex[1][...], ex[2][...], ex[3][...])
        out[0][...] = du
        out[1][...] = du.astype(BF16)
        first = i == 0

        @pl.when(first)
        def _():
            out[2][...] = dg
            out[3][...] = db

        @pl.when(jnp.logical_not(first))
        def _():
            out[2][...] += dg
            out[3][...] += db

    row = lambda i, j: (i, 0)
    vec = lambda i, j: (0, 0)
    return _matmul(
        name, a, b, tb=tb, tm=tm, tn=n, tk=tk, b_blocks=b_blocks, sequential=True,
        extra=[(dres, (tm, n), row), (xhat, (tm, n), row), (rstd, (tm, 1), row), (gamma, (1, n), vec)],
        outs=[(jax.ShapeDtypeStruct((m, n), F32), (tm, n), row),
              (jax.ShapeDtypeStruct((m, n), BF16), (tm, n), row),
              (jax.ShapeDtypeStruct((1, n), F32), (1, n), vec),
              (jax.ShapeDtypeStruct((1, n), F32), (1, n), vec)],
        epilogue=epi)


def _shift_down(x, k):
    row = lax.broadcasted_iota(jnp.int32, x.shape, 0)
    return jnp.where(row >= k, pltpu.roll(x, k, axis=0), 0.0)


def _shift_up(x, k):
    t = x.shape[0]
    row = lax.broadcasted_iota(jnp.int32, x.shape, 0)
    return jnp.where(row < t - k, pltpu.roll(x, t - k, axis=0), 0.0)


def _conv_silu_norm(x, w, normalise):
    kk = w.shape[0]
    c = x * w[kk - 1:kk, :]
    for j in range(kk - 1):
        c = c + _shift_down(x, kk - 1 - j) * w[j:j + 1, :]
    sg = _sigmoid(c)
    s = c * sg
    r = lax.rsqrt(jnp.sum(s * s, axis=-1, keepdims=True) + NORM_EPS)
    y = jnp.where(normalise, s * r, s)
    return c, sg, s, r, y


def _gdn_pre(proj, conv_w, heads):
    t = proj.shape[0]
    kk = conv_w.shape[0]

    def body(x_ref, w_ref, o_ref):
        normalise = pl.program_id(0) < 2
        o_ref[...] = _conv_silu_norm(x_ref[...], w_ref[...], normalise)[4]

    col = lambda s, h: (0, s * heads + h)
    return pl.pallas_call(
        body, name="gdn_pre", grid=(3, heads),
        in_specs=[pl.BlockSpec((t, HEAD_DIM), col), pl.BlockSpec((kk, HEAD_DIM), col)],
        out_specs=pl.BlockSpec((t, HEAD_DIM), col),
        out_shape=jax.ShapeDtypeStruct((t, 3 * heads * HEAD_DIM), F32),
        compiler_params=_params("parallel", "parallel"),
    )(proj, conv_w)


def _gdn_pre_backward(proj, conv_w, dqkv, heads):
    t = proj.shape[0]
    kk = conv_w.shape[0]

    def body(x_ref, w_ref, dy_ref, dx_ref, dw_ref):
        normalise = pl.program_id(0) < 2
        x = x_ref[...]
        w = w_ref[...]
        dy = dy_ref[...]
        c, sg, s, r, y = _conv_silu_norm(x, w, normalise)
        ds_norm = r * (dy - y * jnp.sum(dy * y, axis=-1, keepdims=True))
        ds = jnp.where(normalise, ds_norm, dy)
        dc = ds * (sg * (1.0 + c * (1.0 - sg)))
        dx = dc * w[kk - 1:kk, :]
        rows = [None] * kk
        rows[kk - 1] = jnp.sum(dc * x, axis=0, keepdims=True)
        for j in range(kk - 1):
            lag = kk - 1 - j
            dx = dx + _shift_up(dc, lag) * w[j:j + 1, :]
            rows[j] = jnp.sum(dc * _shift_down(x, lag), axis=0, keepdims=True)
        dx_ref[...] = dx.astype(BF16)
        dw_ref[...] = jnp.concatenate(rows, axis=0)

    col = lambda s, h: (0, s * heads + h)
    return pl.pallas_call(
        body, name="gdn_pre_bwd", grid=(3, heads),
        in_specs=[pl.BlockSpec((t, HEAD_DIM), col), pl.BlockSpec((kk, HEAD_DIM), col),
                  pl.BlockSpec((t, HEAD_DIM), col)],
        out_specs=[pl.BlockSpec((t, HEAD_DIM), col), pl.BlockSpec((kk, HEAD_DIM), col)],
        out_shape=[jax.ShapeDtypeStruct((t, 3 * heads * HEAD_DIM), BF16),
                   jax.ShapeDtypeStruct((kk, 3 * heads * HEAD_DIM), F32)],
        compiler_params=_params("parallel", "parallel"),
    )(proj, conv_w, dqkv)


def _gate_vectors(a_log, dt_bias, heads):
    pad = lambda v: jnp.pad(v.astype(F32), ((0, 0), (heads, HEAD_DIM - 2 * heads)))
    return pad(jnp.exp(a_log.astype(F32))), pad(dt_bias)


def _softplus(x):
    return jnp.maximum(x, 0.0) + jnp.log(1.0 + jnp.exp(-jnp.abs(x)))


def _gates_epilogue(heads):
    def epi(acc, ex, out, i):
        lane = lax.broadcasted_iota(jnp.int32, acc.shape, 1)
        beta = _sigmoid(acc)
        g = -ex[0][...] * _softplus(acc + ex[1][...])
        out[0][...] = acc
        out[1][...] = jnp.where(lane < heads, beta, jnp.where(lane < 2 * heads, g, 0.0))
    return epi


def _gates_backward(ba, bg, dbg, ea, dtb, heads):
    t = ba.shape[0]

    def body(ba_ref, bg_ref, d_ref, ea_ref, dt_ref, dba_ref, dal_ref, ddt_ref):
        lane = lax.broadcasted_iota(jnp.int32, (t, HEAD_DIM), 1)
        bgv = bg_ref[...]
        d = d_ref[...]
        db = d * bgv * (1.0 - bgv)
        da = -d * ea_ref[...] * _sigmoid(ba_ref[...] + dt_ref[...])
        is_g = jnp.logical_and(lane >= heads, lane < 2 * heads)
        dba = jnp.where(lane < heads, db, jnp.where(is_g, da, 0.0))
        dba_ref[...] = dba.astype(BF16)
        dal_ref[...] = jnp.sum(jnp.where(is_g, d * bgv, 0.0), axis=0, keepdims=True)
        ddt_ref[...] = jnp.sum(jnp.where(is_g, da, 0.0), axis=0, keepdims=True)

    full = pl.BlockSpec((t, HEAD_DIM), lambda: (0, 0))
    vec = pl.BlockSpec((1, HEAD_DIM), lambda: (0, 0))
    return pl.pallas_call(
        body, name="gates_bwd", grid=(),
        in_specs=[full, full, full, vec, vec], out_specs=[full, vec, vec],
        out_shape=[jax.ShapeDtypeStruct((t, HEAD_DIM), BF16), jax.ShapeDtypeStruct((1, HEAD_DIM), F32),
                   jax.ShapeDtypeStruct((1, HEAD_DIM), F32)],
        compiler_params=pltpu.CompilerParams(vmem_limit_bytes=VMEM_LIMIT),
    )(ba, bg, dbg, ea, dtb)


class _Chunk:
    pass


def _split2(x):
    hi = x.astype(BF16)
    return hi, (x - hi.astype(F32)).astype(BF16)


def _split3(x):
    hi = x.astype(BF16)
    rest = x - hi.astype(F32)
    mid = rest.astype(BF16)
    return hi, mid, (rest - mid.astype(F32)).astype(BF16)


def _dot_mask(mask, x, ta=False):
    hi, mid, lo = _split3(x)
    return _bdot(mask, hi, ta=ta) + (_bdot(mask, mid, ta=ta) + _bdot(mask, lo, ta=ta))


def _transpose_by_identity(x):
    r = x.shape[0]
    eye = (lax.broadcasted_iota(jnp.int32, (r, r), 0) == lax.broadcasted_iota(jnp.int32, (r, r), 1)).astype(BF16)
    hi, mid, lo = _split3(x)
    return _bdot(hi, eye, ta=True) + (_bdot(mid, eye, ta=True) + _bdot(lo, eye, ta=True))


def _dot22(a, b, ta=False, tb=False):
    ah, al = _split2(a)
    bh, bl = _split2(b)
    return _bdot(ah, bh, ta, tb) + (_bdot(ah, bl, ta, tb) + _bdot(al, bh, ta, tb))


def _chunk_gates(bg, heads):
    n = CHUNK
    row = lax.broadcasted_iota(jnp.int32, (n, n), 0)
    col = lax.broadcasted_iota(jnp.int32, (n, n), 1)
    lane = lax.broadcasted_iota(jnp.int32, bg.shape, 1)
    graw = jnp.where(jnp.logical_and(lane >= heads, lane < 2 * heads), bg, 0.0)
    gc = _dot_mask((row >= col).astype(BF16), graw)
    return gc, _transpose_by_identity(gc)


def _in_lockstep(generators):
    results = [None] * len(generators)
    live = list(enumerate(generators))
    while live:
        still = []
        for i, gen in live:
            try:
                next(gen)
                still.append((i, gen))
            except StopIteration as stop:
                results[i] = stop.value
        live = still
    return results


def _chunk_local(q, k, v, beta, gc, grow):
    c = _Chunk()
    n = CHUNK
    row = lax.broadcasted_iota(jnp.int32, (n, n), 0)
    col = lax.broadcasted_iota(jnp.int32, (n, n), 1)
    c.tri = row >= col
    c.strict = row > col
    eye = row == col
    c.gcb = jnp.broadcast_to(gc, (n, HEAD_DIM))
    c.decay = jnp.where(c.tri, jnp.exp(jnp.where(c.tri, gc - grow, 0.0)), 0.0)
    c.eg = jnp.exp(c.gcb)
    glast = c.gcb[n - 1:n, :]
    c.egl = jnp.exp(glast)
    c.ekl = jnp.exp(glast - c.gcb)
    c.beta = beta
    c.q = q * (HEAD_DIM ** -0.5)
    c.k = k
    c.v = v
    c.kb = k * beta
    c.vb = v * beta
    c.kg = c.kb * c.eg
    both = _bdot(jnp.concatenate([c.kb, c.q], axis=0), k, tb=True)
    yield
    c.L = jnp.where(c.strict, both[:n] * c.decay, 0.0)
    c.A = jnp.where(c.tri, both[n:] * c.decay, 0.0)
    x = -c.L
    tinv = eye.astype(F32) + x
    p = _dot22(x, x)
    yield
    for _ in range(int(math.log2(n)) - 2):
        both = _dot22(jnp.concatenate([p, tinv], axis=0), p)
        yield
        p, tinv = both[:n], tinv + both[n:]
    c.T = tinv + _dot22(tinv, p)
    yield
    tinv = c.T
    uw = _dot22(tinv, jnp.concatenate([c.vb, c.kg], axis=1))
    yield
    c.u, c.w = uw[:, :HEAD_DIM], uw[:, HEAD_DIM:]
    c.qg = c.q * c.eg
    c.kdec = k * c.ekl
    return c


def _gdn_core(qkv, bg, heads):
    t = qkv.shape[0]
    nchunk = t // CHUNK

    gw = heads * HEAD_DIM

    def body(qkv_ref, bg_ref, o_ref, s_ref, state):
        @pl.when(pl.program_id(0) == 0)
        def _():
            state[...] = jnp.zeros_like(state)

        bg_v = bg_ref[...]
        gc_all, gc_rows = _chunk_gates(bg_v, heads)
        def one_head(h):
            col = lambda s: pl.ds(s * gw + h * HEAD_DIM, HEAD_DIM)
            c = yield from _chunk_local(qkv_ref[:, col(0)], qkv_ref[:, col(1)], qkv_ref[:, col(2)], bg_v[:, h:h + 1],
                                        gc_all[:, heads + h:heads + h + 1], gc_rows[heads + h:heads + h + 1, :])
            s0 = state[h]
            v_new = c.u - _bdot(c.w, s0)
            yield
            o = _bdot(c.qg, s0) + _bdot(c.A, v_new)
            return s0, o, s0 * c.egl + _bdot(c.kdec, v_new, ta=True)

        results = _in_lockstep([one_head(h) for h in range(heads)])
        for h, (s0, o, s1) in enumerate(results):
            s_ref[h, 0] = s0
            o_ref[:, pl.ds(h * HEAD_DIM, HEAD_DIM)] = o
            state[h] = s1

    return pl.pallas_call(
        body, name="gdn_core", grid=(nchunk,),
        in_specs=[pl.BlockSpec((CHUNK, 3 * gw), lambda n: (n, 0)), pl.BlockSpec((CHUNK, HEAD_DIM), lambda n: (n, 0))],
        out_specs=[pl.BlockSpec((CHUNK, gw), lambda n: (n, 0)),
                   pl.BlockSpec((heads, 1, HEAD_DIM, HEAD_DIM), lambda n: (0, n, 0, 0))],
        out_shape=[jax.ShapeDtypeStruct((t, gw), F32),
                   jax.ShapeDtypeStruct((heads, nchunk, HEAD_DIM, HEAD_DIM), F32)],
        scratch_shapes=[pltpu.VMEM((heads, HEAD_DIM, HEAD_DIM), F32)],
        compiler_params=_params("arbitrary"),
    )(qkv, bg)


def _gdn_core_backward(qkv, bg, states, do, heads):
    t = qkv.shape[0]
    nchunk = t // CHUNK
    n = CHUNK

    def one_head(chunk_local, s0, d_out, ds1):
        c = yield from chunk_local
        v_new = c.u - _bdot(c.w, s0)
        dqg = _bdot(d_out, s0, tb=True)
        ds0 = _bdot(c.qg, d_out, ta=True) + ds1 * c.egl
        dv_new = _bdot(c.A, d_out, ta=True) + _bdot(c.kdec, ds1)
        yield
        dA = jnp.where(c.tri, _bdot(d_out, v_new, tb=True), 0.0)
        dkdec = _bdot(v_new, ds1, tb=True)
        dgl = jnp.sum(jnp.sum(ds1 * s0, axis=1, keepdims=True), axis=0, keepdims=True) * c.egl
        dw = -_bdot(dv_new, s0, tb=True)
        ds0 = ds0 - _bdot(c.w, dv_new, ta=True)
        yield
        both = _dot22(c.T, jnp.concatenate([dv_new, dw], axis=1), ta=True)
        yield
        dvb, dkg = both[:, :HEAD_DIM], both[:, HEAD_DIM:]
        dL = jnp.where(c.strict, -(_bdot(dvb, c.u, tb=True) + _bdot(dkg, c.w, tb=True)), 0.0)
        yield
        dm1 = dL * c.decay
        dkb = _bdot(dm1, c.k) + dkg * c.eg
        dk = _bdot(dm1, c.kb, ta=True)
        dm2 = dA * c.decay
        dq = _bdot(dm2, c.k) + dqg * c.eg
        dk = dk + _bdot(dm2, c.q, ta=True) + dkdec * c.ekl + dkb * c.beta
        pm = dL * c.L + dA * c.A
        ones = jnp.ones((n, HEAD_DIM), BF16)
        pm_hi, pm_lo = _split2(pm)
        colsum = _bdot(pm_hi, ones, ta=True) + _bdot(pm_lo, ones, ta=True)
        tk_ = jnp.sum(dkdec * c.kdec, axis=1, keepdims=True)
        dgc = (jnp.sum(pm, axis=1, keepdims=True) - colsum
               + jnp.sum(dqg * c.qg, axis=1, keepdims=True)
               - tk_
               + jnp.sum(dkg * c.kg, axis=1, keepdims=True))
        dgl = dgl + jnp.sum(tk_, axis=0, keepdims=True)
        rowi = lax.broadcasted_iota(jnp.int32, (n, HEAD_DIM), 0)
        dgc = dgc + jnp.where(rowi == n - 1, dgl, 0.0)
        dbeta = jnp.sum(dkb * c.k, axis=1, keepdims=True) + jnp.sum(dvb * c.v, axis=1, keepdims=True)
        return dq * (HEAD_DIM ** -0.5), dk, dvb * c.beta, dbeta, dgc, ds0

    gw = heads * HEAD_DIM

    def body(qkv_ref, bg_ref, s_ref, do_ref, dqkv_ref, dbg_ref, dstate):
        @pl.when(pl.program_id(0) == 0)
        def _():
            dstate[...] = jnp.zeros_like(dstate)

        bg_v = bg_ref[...]
        gc_all, gc_rows = _chunk_gates(bg_v, heads)
        lane = lax.broadcasted_iota(jnp.int32, (n, HEAD_DIM), 1)
        dgates = jnp.zeros((n, HEAD_DIM), F32)
        chains = []
        for h in range(heads):
            col = lambda s: pl.ds(s * gw + h * HEAD_DIM, HEAD_DIM)
            c = _chunk_local(qkv_ref[:, col(0)], qkv_ref[:, col(1)], qkv_ref[:, col(2)], bg_v[:, h:h + 1],
                             gc_all[:, heads + h:heads + h + 1], gc_rows[heads + h:heads + h + 1, :])
            chains.append(one_head(c, s_ref[h, 0], do_ref[:, pl.ds(h * HEAD_DIM, HEAD_DIM)], dstate[h]))
        results = _in_lockstep(chains)
        for h, (dq, dk, dv, dbeta, dgc, ds0) in enumerate(results):
            dgates = jnp.where(lane == h, dbeta, jnp.where(lane == heads + h, dgc, dgates))
        for h, (dq, dk, dv, dbeta, dgc, ds0) in enumerate(results):
            dqkv_ref[:, pl.ds(h * HEAD_DIM, HEAD_DIM)] = dq
            dqkv_ref[:, pl.ds(gw + h * HEAD_DIM, HEAD_DIM)] = dk
            dqkv_ref[:, pl.ds(2 * gw + h * HEAD_DIM, HEAD_DIM)] = dv
            dstate[h] = ds0
        row = lax.broadcasted_iota(jnp.int32, (n, n), 0)
        colm = lax.broadcasted_iota(jnp.int32, (n, n), 1)
        draw = _dot_mask((row >= colm).astype(BF16), dgates, ta=True)
        dbg_ref[...] = jnp.where(lane < heads, dgates, draw)

    last = nchunk - 1
    return pl.pallas_call(
        body, name="gdn_core_bwd", grid=(nchunk,),
        in_specs=[pl.BlockSpec((CHUNK, 3 * gw), lambda i: (last - i, 0)),
                  pl.BlockSpec((CHUNK, HEAD_DIM), lambda i: (last - i, 0)),
                  pl.BlockSpec((heads, 1, HEAD_DIM, HEAD_DIM), lambda i: (0, last - i, 0, 0)),
                  pl.BlockSpec((CHUNK, gw), lambda i: (last - i, 0))],
        out_specs=[pl.BlockSpec((CHUNK, 3 * gw), lambda i: (last - i, 0)),
                   pl.BlockSpec((CHUNK, HEAD_DIM), lambda i: (last - i, 0))],
        out_shape=[jax.ShapeDtypeStruct((t, 3 * gw), F32), jax.ShapeDtypeStruct((t, HEAD_DIM), F32)],
        scratch_shapes=[pltpu.VMEM((heads, HEAD_DIM, HEAD_DIM), F32)],
        compiler_params=_params("arbitrary"),
    )(qkv, bg, states, do)


def _gdn_post(o, proj, z_col0, norm_w, heads, tt):
    t = o.shape[0]
    zb = z_col0 // HEAD_DIM

    def body(o_ref, z_ref, w_ref, out_ref):
        ov = o_ref[...]
        z = z_ref[...]
        rms = lax.rsqrt(jnp.mean(ov * ov, axis=-1, keepdims=True) + NORM_EPS)
        out_ref[...] = (ov * rms * w_ref[...] * (z * _sigmoid(z))).astype(BF16)

    return pl.pallas_call(
        body, name="gdn_post", grid=(t // tt, heads),
        in_specs=[pl.BlockSpec((tt, HEAD_DIM), lambda i, h: (i, h)),
                  pl.BlockSpec((tt, HEAD_DIM), lambda i, h: (i, zb + h)),
                  pl.BlockSpec((1, HEAD_DIM), lambda i, h: (0, 0))],
        out_specs=pl.BlockSpec((tt, HEAD_DIM), lambda i, h: (i, h)),
        out_shape=jax.ShapeDtypeStruct((t, heads * HEAD_DIM), BF16),
        compiler_params=_params("parallel", "parallel"),
    )(o, proj, norm_w)


def _gdn_post_backward(dcat, o, proj, z_col0, norm_w, heads, tt):
    t = o.shape[0]
    zb = z_col0 // HEAD_DIM

    def body(d_ref, o_ref, z_ref, w_ref, do_ref, dz_ref, dw_ref):
        d = d_ref[...]
        ov = o_ref[...]
        z = z_ref[...]
        w = w_ref[...]
        rms = lax.rsqrt(jnp.mean(ov * ov, axis=-1, keepdims=True) + NORM_EPS)
        ohat = ov * rms
        sg = _sigmoid(z)
        gate = z * sg
        dz_ref[...] = (d * ohat * w * (sg * (1.0 + z * (1.0 - sg)))).astype(BF16)
        don = d * gate
        dohat = don * w
        do_ref[...] = rms * (dohat - ohat * jnp.mean(dohat * ohat, axis=-1, keepdims=True))
        dw = jnp.sum(don * ohat, axis=0, keepdims=True)
        first = jnp.logical_and(pl.program_id(0) == 0, pl.program_id(1) == 0)

        @pl.when(first)
        def _():
            dw_ref[...] = dw

        @pl.when(jnp.logical_not(first))
        def _():
            dw_ref[...] += dw

    blk = pl.BlockSpec((tt, HEAD_DIM), lambda i, h: (i, h))
    return pl.pallas_call(
        body, name="gdn_post_bwd", grid=(t // tt, heads),
        in_specs=[blk, blk, pl.BlockSpec((tt, HEAD_DIM), lambda i, h: (i, zb + h)),
                  pl.BlockSpec((1, HEAD_DIM), lambda i, h: (0, 0))],
        out_specs=[blk, blk, pl.BlockSpec((1, HEAD_DIM), lambda i, h: (0, 0))],
        out_shape=[jax.ShapeDtypeStruct((t, heads * HEAD_DIM), F32),
                   jax.ShapeDtypeStruct((t, heads * HEAD_DIM), BF16),
                   jax.ShapeDtypeStruct((1, HEAD_DIM), F32)],
        compiler_params=_params("arbitrary", "arbitrary"),
    )(dcat, o, proj, norm_w)


def _pool_select(levels, group):
    out = levels[-1]
    for gi in range(len(levels) - 2, -1, -1):
        out = jnp.where(group == gi, levels[gi], out)
    return out


def _pool_counts(t, width, group):
    pos = lax.broadcasted_iota(jnp.int32, (t, width), 0)
    win = jnp.left_shift(2, group)
    return jnp.minimum(pos + 1, win).astype(F32)


def _pooled(p, group):
    levels, s, step = [], p, 1
    for _ in POOL_WINDOWS:
        s = s + _shift_down(s, step)
        levels.append(s)
        step *= 2
    cnt = _pool_counts(p.shape[0], p.shape[1], group)
    return _pool_select(levels, group) / cnt - p, cnt


def _pool_forward(proj, p_col0, pool_w, pool_scale):
    t = proj.shape[0]
    groups, cg, _ = pool_w.shape
    pb = p_col0 // cg

    def body(p_ref, w_ref, s_ref, o_ref):
        pooled, _ = _pooled(p_ref[...], pl.program_id(0))
        o_ref[...] = (_bdot(pooled, w_ref[0]) * s_ref[...]).astype(BF16)

    return pl.pallas_call(
        body, name="pool_fwd", grid=(groups,),
        in_specs=[pl.BlockSpec((t, cg), lambda g: (0, pb + g)), pl.BlockSpec((1, cg, cg), lambda g: (g, 0, 0)),
                  pl.BlockSpec((1, cg), lambda g: (0, g))],
        out_specs=pl.BlockSpec((t, cg), lambda g: (0, g)),
        out_shape=jax.ShapeDtypeStruct((t, groups * cg), BF16),
        compiler_params=_params("parallel"),
    )(proj, pool_w, pool_scale)


def _pool_backward(dcat, d_col0, proj, p_col0, pool_w, pool_scale):
    t = proj.shape[0]
    groups, cg, _ = pool_w.shape
    pb = p_col0 // cg
    db = d_col0 // cg

    def body(d_ref, p_ref, w_ref, s_ref, dp_ref, dw_ref, ds_ref):
        group = pl.program_id(0)
        pooled, cnt = _pooled(p_ref[...], group)
        w = w_ref[0]
        d = d_ref[...]
        mixed = _bdot(pooled, w)
        ds_ref[...] = jnp.sum(d * mixed, axis=0, keepdims=True)
        dmixed = d * s_ref[...]
        dw_ref[0] = _bdot(pooled, dmixed, ta=True)
        dpooled = _bdot(dmixed, w, tb=True)
        levels, s, step = [], dpooled / cnt, 1
        for _ in POOL_WINDOWS:
            s = s + _shift_up(s, step)
            levels.append(s)
            step *= 2
        dp_ref[...] = (_pool_select(levels, group) - dpooled).astype(BF16)

    return pl.pallas_call(
        body, name="pool_bwd", grid=(groups,),
        in_specs=[pl.BlockSpec((t, cg), lambda g: (0, db + g)), pl.BlockSpec((t, cg), lambda g: (0, pb + g)),
                  pl.BlockSpec((1, cg, cg), lambda g: (g, 0, 0)), pl.BlockSpec((1, cg), lambda g: (0, g))],
        out_specs=[pl.BlockSpec((t, cg), lambda g: (0, g)), pl.BlockSpec((1, cg, cg), lambda g: (g, 0, 0)),
                   pl.BlockSpec((1, cg), lambda g: (0, g))],
        out_shape=[jax.ShapeDtypeStruct((t, groups * cg), BF16), jax.ShapeDtypeStruct((groups, cg, cg), F32),
                   jax.ShapeDtypeStruct((1, groups * cg), F32)],
        compiler_params=_params("parallel"),
    )(dcat, proj, pool_w, pool_scale)


def _attention(q, k, v, tq):
    t, d = q.shape
    m = k.shape[0]
    dh = d // XATTN_HEADS
    scale = dh ** -0.5

    def body(q_ref, k_ref, v_ref, o_ref):
        s = _bdot(q_ref[...], k_ref[...], tb=True) * scale
        s = s - jnp.max(s, axis=-1, keepdims=True)
        e = jnp.exp(s)
        p = e / jnp.sum(e, axis=-1, keepdims=True)
        o_ref[...] = _bdot(p, v_ref[...]).astype(BF16)

    return pl.pallas_call(
        body, name="xattn_fwd", grid=(XATTN_HEADS, t // tq),
        in_specs=[pl.BlockSpec((tq, dh), lambda h, i: (i, h)), pl.BlockSpec((m, dh), lambda h, i: (0, h)),
                  pl.BlockSpec((m, dh), lambda h, i: (0, h))],
        out_specs=pl.BlockSpec((tq, dh), lambda h, i: (i, h)),
        out_shape=jax.ShapeDtypeStruct((t, d), BF16),
        compiler_params=_params("parallel", "parallel"),
    )(q, k, v)


def _attention_backward(q, k, v, do, tq):
    t, d = q.shape
    m = k.shape[0]
    dh = d // XATTN_HEADS
    scale = dh ** -0.5

    def body(q_ref, k_ref, v_ref, do_ref, dq_ref, dk_ref, dv_ref, dk_acc, dv_acc):
        i = pl.program_id(1)
        qv, kv, vv, dov = q_ref[...], k_ref[...], v_ref[...], do_ref[...]
        s = _bdot(qv, kv, tb=True) * scale
        s = s - jnp.max(s, axis=-1, keepdims=True)
        e = jnp.exp(s)
        p = e / jnp.sum(e, axis=-1, keepdims=True)
        dp = _bdot(dov, vv, tb=True)
        ds = p * (dp - jnp.sum(dp * p, axis=-1, keepdims=True)) * scale
        dq_ref[...] = _bdot(ds, kv).astype(BF16)
        dv_part = _bdot(p, dov, ta=True)
        dk_part = _bdot(ds, qv, ta=True)

        @pl.when(i == 0)
        def _():
            dk_acc[...] = dk_part
            dv_acc[...] = dv_part

        @pl.when(i > 0)
        def _():
            dk_acc[...] += dk_part
            dv_acc[...] += dv_part

        @pl.when(i == pl.num_programs(1) - 1)
        def _():
            dk_ref[...] = dk_acc[...].astype(BF16)
            dv_ref[...] = dv_acc[...].astype(BF16)

    qblk = pl.BlockSpec((tq, dh), lambda h, i: (i, h))
    kblk = pl.BlockSpec((m, dh), lambda h, i: (0, h))
    return pl.pallas_call(
        body, name="xattn_bwd", grid=(XATTN_HEADS, t // tq),
        in_specs=[qblk, kblk, kblk, qblk],
        out_specs=[qblk, kblk, kblk],
        out_shape=[jax.ShapeDtypeStruct((t, d), BF16), jax.ShapeDtypeStruct((m, d), BF16),
                   jax.ShapeDtypeStruct((m, d), BF16)],
        scratch_shapes=[pltpu.VMEM((m, dh), F32), pltpu.VMEM((m, dh), F32)],
        compiler_params=_params("parallel", "arbitrary"),
    )(q, k, v, do)


def _loss_and_ln_backward(xhat, rstd, gamma, beta, target, tm):
    t, d = xhat.shape

    def body(x_ref, r_ref, g_ref, b_ref, t_ref, du_ref, dub_ref, dg_ref, db_ref, loss_ref):
        xh = x_ref[...]
        g = g_ref[...]
        diff = xh * g + b_ref[...] - t_ref[...]
        part = jnp.sum(jnp.sum(diff * diff, axis=1, keepdims=True), axis=0, keepdims=True) * (0.5 / d)
        dy = diff * (1.0 / d)
        du, dg, db = _ln_backward_math(dy, xh, r_ref[...], g)
        du_ref[...] = du
        dub_ref[...] = du.astype(BF16)
        lossrow = jnp.broadcast_to(part, (1, HEAD_DIM))
        first = pl.program_id(0) == 0

        @pl.when(first)
        def _():
            dg_ref[...] = dg
            db_ref[...] = db
            loss_ref[...] = lossrow

        @pl.when(jnp.logical_not(first))
        def _():
            dg_ref[...] += dg
            db_ref[...] += db
            loss_ref[...] += lossrow

    row = pl.BlockSpec((tm, d), lambda i: (i, 0))
    vec = pl.BlockSpec((1, d), lambda i: (0, 0))
    return pl.pallas_call(
        body, name="loss_ln3_bwd", grid=(t // tm,),
        in_specs=[row, pl.BlockSpec((tm, 1), lambda i: (i, 0)), vec, vec, row],
        out_specs=[row, row, vec, vec, pl.BlockSpec((1, HEAD_DIM), lambda i: (0, 0))],
        out_shape=[jax.ShapeDtypeStruct((t, d), F32), jax.ShapeDtypeStruct((t, d), BF16),
                   jax.ShapeDtypeStruct((1, d), F32), jax.ShapeDtypeStruct((1, d), F32),
                   jax.ShapeDtypeStruct((1, HEAD_DIM), F32)],
        compiler_params=_params("arbitrary"),
    )(xhat, rstd, gamma, beta, target)


def _pick(n, prefs):
    for p in prefs:
        if n % p == 0:
            return p
    return n


def _local_step(x, mem, target, w):
    t, d = x.shape
    heads = w["a_log"].shape[1]
    gw = heads * HEAD_DIM
    groups, cg, _ = w["pool_w"].shape
    pw = groups * cg
    n_main = 4 * gw + pw
    in_cols = n_main + 2 * heads
    s_in = w["w_in3"].shape[0]

    tm = _pick(t, (512, 256, 128))
    tm_ln = _pick(t, (256, 128))
    tk = _pick(d, K_STEPS)

    w_in = jnp.concatenate([w["w_in3"][s] for s in range(s_in)], axis=1)
    w_main = jnp.concatenate([w_in[:, :4 * gw], w_in[:, 4 * gw + 2 * heads:]], axis=1)
    w_ba = jnp.pad(w_in[:, 4 * gw:4 * gw + 2 * heads], ((0, 0), (0, HEAD_DIM - 2 * heads)))
    x_bf = x.astype(BF16)
    mem_bf = mem.astype(BF16)

    proj = _plain("proj_main", x_bf, w_main, tm=tm, tn=_pick(n_main, (1024, 512, 256, 128)), tk=tk, out_dtype=F32)
    ea, dtb = _gate_vectors(w["a_log"], w["dt_bias"], heads)
    vec128 = lambda i, j: (0, 0)
    ba, bg = _matmul(
        "proj_gates", x_bf, w_ba, tm=tm, tn=HEAD_DIM, tk=tk,
        extra=[(ea, (1, HEAD_DIM), vec128), (dtb, (1, HEAD_DIM), vec128)],
        outs=[(jax.ShapeDtypeStruct((t, HEAD_DIM), F32), (tm, HEAD_DIM), _tile)] * 2,
        epilogue=_gates_epilogue(heads))
    qkv = _gdn_pre(proj, w["conv_w"], heads)
    o_gdn, states = _gdn_core(qkv, bg, heads)
    cat_g = _gdn_post(o_gdn, proj, 3 * gw, w["gdn_norm_w"], heads, tm)
    cat_p = _pool_forward(proj, 4 * gw, w["pool_w"], w["pool_scale"])
    cat = jnp.concatenate([cat_g, cat_p], axis=1)
    h1, h1_bf, xhat1, rstd1 = _ln_forward("mix_ln1", cat, w["w_out"], x, w["ln1_g"], w["ln1_b"], tm=tm_ln, tk=tk)

    tn_d = _pick(d, (1024, 512, 256, 128))
    q = _plain("xattn_q", h1_bf, w["xq_w"], tm=tm, tn=tn_d, tk=tk, out_dtype=BF16)
    mlen = mem.shape[0]
    tm_mem = _pick(mlen, (256, 128))
    k = _plain("xattn_k", mem_bf, w["xk_w"], tm=tm_mem, tn=tn_d, tk=tk, out_dtype=BF16)
    v = _plain("xattn_v", mem_bf, w["xv_w"], tm=tm_mem, tn=tn_d, tk=tk, out_dtype=BF16)
    att = _attention(q, k, v, tm)
    h2, h2_bf, xhat2, rstd2 = _ln_forward("xo_ln2", att, w["xo_w"], h1, w["ln2_g"], w["ln2_b"], tm=tm_ln, tk=tk)

    s_up = w["w_up3"].shape[0]
    ff = s_up * w["w_up3"].shape[2]
    tn_f = _pick(ff // s_up, (1024, 512, 256, 128))

    def up_epi(acc, ex, out, i):
        r = jnp.maximum(acc, 0.0)
        out[0][...] = (r * r).astype(BF16)
        out[1][...] = (2.0 * r).astype(BF16)

    act, act_grad = _matmul(
        "mlp_up", h2_bf, w["w_up3"], b_blocks=s_up, tm=tm, tn=tn_f, tk=tk,
        outs=[(jax.ShapeDtypeStruct((t, ff), BF16), (tm, tn_f), _tile)] * 2, epilogue=up_epi)
    tk_f = _pick(ff, K_STEPS)
    _, _, xhat3, rstd3 = _ln_forward("down_ln3", act, w["w_down"], h2, w["ln3_g"], w["ln3_b"], tm=tm_ln, tk=tk_f)

    grads = {}
    du3, du3_bf, grads["ln3_g"], grads["ln3_b"], loss = _loss_and_ln_backward(
        xhat3, rstd3, w["ln3_g"], w["ln3_b"], target, tm_ln)

    def dup_epi(acc, ex, out, i):
        out[0][...] = (acc * ex[0][...].astype(F32)).astype(BF16)

    dup = _matmul(
        "mlp_down_dx", du3_bf, w["w_down"], tb=True, tm=tm, tn=tn_f, tk=tk,
        extra=[(act_grad, (tm, tn_f), _tile)],
        outs=[(jax.ShapeDtypeStruct((t, ff), BF16), (tm, tn_f), _tile)], epilogue=dup_epi)[0]
    tk_t = _pick(t, K_STEPS)
    tm_w = _pick(d, (512, 256, 128))
    grads["w_down"] = _plain("mlp_down_dw", act, du3_bf, ta=True, tm=_pick(ff, (512, 256, 128)), tn=tn_d, tk=tk_t,
                             out_dtype=F32)
    grads["w_up3"] = _plain("mlp_up_dw", h2_bf, dup, ta=True, tm=tm_w, tn=tn_f, tk=tk_t, out_dtype=F32, out3=s_up)
    du2, du2_bf, grads["ln2_g"], grads["ln2_b"] = _ln_backward(
        "mlp_up_dx_ln2", dup, w["w_up3"], du3, xhat2, rstd2, w["ln2_g"], tm=tm_ln,
        tk=_pick(ff // s_up, K_STEPS), b_blocks=s_up)

    grads["xo_w"] = _plain("xo_dw", att, du2_bf, ta=True, tm=tm_w, tn=tn_d, tk=tk_t, out_dtype=F32)
    datt = _plain("xo_dx", du2_bf, w["xo_w"], tb=True, tm=tm, tn=tn_d, tk=tk, out_dtype=BF16)
    dq, dk, dv = _attention_backward(q, k, v, datt, tm)
    tk_m = _pick(mlen, (256, 128))
    grads["xq_w"] = _plain("xq_dw", h1_bf, dq, ta=True, tm=tm_w, tn=tn_d, tk=tk_t, out_dtype=F32)
    grads["xk_w"] = _plain("xk_dw", mem_bf, dk, ta=True, tm=tm_w, tn=tn_d, tk=tk_m, out_dtype=F32)
    grads["xv_w"] = _plain("xv_dw", mem_bf, dv, ta=True, tm=tm_w, tn=tn_d, tk=tk_m, out_dtype=F32)
    du1, du1_bf, grads["ln1_g"], grads["ln1_b"] = _ln_backward(
        "xq_dx_ln1", dq, w["xq_w"], du2, xhat1, rstd1, w["ln1_g"], tm=tm_ln, tk=tk)

    grads["w_out"] = _plain("out_dw", cat, du1_bf, ta=True, tm=tm_w, tn=tn_d, tk=tk_t, out_dtype=F32)
    dcat = _plain("out_dx", du1_bf, w["w_out"], tb=True, tm=tm, tn=tn_d, tk=tk, out_dtype=F32)
    dp, grads["pool_w"], grads["pool_scale"] = _pool_backward(dcat, gw, proj, 4 * gw, w["pool_w"], w["pool_scale"])
    do_gdn, dz, grads["gdn_norm_w"] = _gdn_post_backward(dcat, o_gdn, proj, 3 * gw, w["gdn_norm_w"], heads, tm)
    dqkv, dbg = _gdn_core_backward(qkv, bg, states, do_gdn, heads)
    dqkv_pre, grads["conv_w"] = _gdn_pre_backward(proj, w["conv_w"], dqkv, heads)
    dba, dalog_row, ddt_row = _gates_backward(ba, bg, dbg, ea, dtb, heads)
    grads["a_log"] = dalog_row[:, heads:2 * heads]
    grads["dt_bias"] = ddt_row[:, heads:2 * heads]

    dproj = jnp.concatenate([dqkv_pre, dz, dp], axis=1)
    tn_main = _pick(n_main, (1024, 512, 256, 128))
    dw_main = _plain("proj_dw", x_bf, dproj, ta=True, tm=tm_w, tn=tn_main, tk=tk_t, out_dtype=F32)
    dw_ba = _plain("proj_gates_dw", x_bf, dba, ta=True, tm=tm_w, tn=HEAD_DIM, tk=tk_t, out_dtype=F32)
    dw_in = jnp.concatenate([dw_main[:, :4 * gw], dw_ba[:, :2 * heads], dw_main[:, 4 * gw:]], axis=1)
    per = in_cols // s_in
    grads["w_in3"] = jnp.stack([dw_in[:, s * per:(s + 1) * per] for s in range(s_in)], axis=0)

    def dx_epi(acc, ex, out, i):
        out[0][...] = acc + ex[1][...] + ALPHA * ex[0][...]

    dx_gates = _plain("proj_gates_dx", dba, w_ba, tb=True, tm=tm, tn=tn_d, tk=HEAD_DIM, out_dtype=F32)
    grad_x = _matmul(
        "proj_dx", dproj, w_main, tb=True, tm=tm, tn=tn_d, tk=_pick(n_main, (2560,) + K_STEPS),
        extra=[(du1, (tm, tn_d), _tile), (dx_gates, (tm, tn_d), _tile)],
        outs=[(jax.ShapeDtypeStruct((t, d), F32), (tm, tn_d), _tile)], epilogue=dx_epi)[0]
    return loss, grad_x, grads


def _adamw(name, w, g, m, v):
    r, c = w.shape
    tr = _pick(r, (256, 128, 64, 32, 16, 8))
    c1 = 1.0 - ADAM_B1 ** ADAM_STEP
    c2 = 1.0 - ADAM_B2 ** ADAM_STEP

    def body(w_ref, g_ref, m_ref, v_ref, d_ref, mo_ref, vo_ref):
        gv = g_ref[...]
        mn = ADAM_B1 * m_ref[...] + (1.0 - ADAM_B1) * gv
        vn = ADAM_B2 * v_ref[...] + (1.0 - ADAM_B2) * (gv * gv)
        d_ref[...] = -ADAM_LR * ((mn / c1) / (jnp.sqrt(vn / c2) + ADAM_EPS) + ADAM_WD * w_ref[...])
        mo_ref[...] = mn
        vo_ref[...] = vn

    blk = pl.BlockSpec((tr, c), lambda i: (i, 0))
    return pl.pallas_call(
        body, name=name, grid=(r // tr,), in_specs=[blk] * 4, out_specs=[blk] * 3,
        out_shape=[jax.ShapeDtypeStruct((r, c), F32)] * 3,
        compiler_params=_params("parallel"),
    )(w, g, m, v)


def _place():
    x, y, c = lax.axis_index("x"), lax.axis_index("y"), lax.axis_index("c")
    chips = [(1 - x, y), (x, 1 - y), (1 - x, 1 - y)]
    return x, y, c, chips


HBM = pl.BlockSpec(memory_space=pltpu.HBM)


def _gather_weights(shards):
    n = len(shards)
    slots = 7

    def body(*refs):
        ins, outs = refs[:n], refs[n:2 * n]
        send_sems, recv_sems = refs[2 * n:]
        x, y, c, chips = _place()
        me = 2 * x + y
        sibling = (x, y, 1 - c)

        def half(i, shard_index, which):
            rows = ins[i].shape[0] // 2
            return outs[i].at[shard_index, pl.ds(which * rows, rows)]

        def remote(i, slot, src, dst, to):
            return pltpu.make_async_remote_copy(src_ref=src, dst_ref=dst, send_sem=send_sems.at[slots * i + slot],
                                                recv_sem=recv_sems.at[slots * i + slot], device_id=to,
                                                device_id_type=MESH)

        started = []
        for i in range(n):
            cp = remote(i, 6, ins[i], outs[i].at[me], sibling)
            cp.start()
            started.append(cp)
        for i in range(n):
            rows = ins[i].shape[0] // 2
            for j, chip in enumerate(chips):
                cp = remote(i, j, ins[i].at[pl.ds(c * rows, rows)], half(i, me, c), (*chip, c))
                cp.start()
                started.append(cp)
        for i in range(n):
            for j, chip in enumerate(chips):
                theirs = 2 * chip[0] + chip[1]
                landed = half(i, theirs, c)
                remote(i, j, landed, landed, (*chip, c)).wait_recv()
                cp = remote(i, 3 + j, landed, landed, sibling)
                cp.start()
                started.append(cp)
        for i in range(n):
            for j, chip in enumerate(chips):
                theirs = 2 * chip[0] + chip[1]
                passed = half(i, theirs, 1 - c)
                remote(i, 3 + j, passed, passed, sibling).wait_recv()
        for i in range(n):
            remote(i, 6, ins[i], outs[i].at[me], sibling).wait_recv()
        for cp in started:
            cp.wait_send()

    return pl.pallas_call(
        body, name="gather_weights",
        in_specs=[HBM] * n, out_specs=[HBM] * n,
        out_shape=[jax.ShapeDtypeStruct((N_SHARD,) + s.shape, s.dtype) for s in shards],
        scratch_shapes=[pltpu.SemaphoreType.DMA((slots * n,)), pltpu.SemaphoreType.DMA((slots * n,))],
    )(*shards)


def _all_reduce_small(slab):
    r, width = slab.shape
    ndev = 8

    def body(x_ref, out_ref, buf, send_sems, recv_sems):
        x, y, c, _ = _place()
        me = 4 * x + 2 * y + c
        buf[me] = x_ref[...]
        copies = []
        for k in range(1, ndev):
            peer = jnp.bitwise_xor(me, k)
            to = (peer // 4, (peer // 2) % 2, peer % 2)
            cp = pltpu.make_async_remote_copy(src_ref=x_ref, dst_ref=buf.at[me], send_sem=send_sems.at[k - 1],
                                              recv_sem=recv_sems.at[k - 1], device_id=to, device_id_type=MESH)
            cp.start()
            copies.append(cp)
        for k in range(1, ndev):
            peer = jnp.bitwise_xor(me, k)
            pltpu.make_async_remote_copy(src_ref=x_ref, dst_ref=buf.at[peer], send_sem=send_sems.at[k - 1],
                                         recv_sem=recv_sems.at[k - 1], device_id=(x, y, c),
                                         device_id_type=MESH).wait_recv()
        for cp in copies:
            cp.wait_send()
        total = buf[0]
        for d in range(1, ndev):
            total = total + buf[d]
        out_ref[...] = total

    return pl.pallas_call(
        body, name="all_reduce_small",
        in_specs=[pl.BlockSpec(memory_space=pltpu.VMEM)], out_specs=pl.BlockSpec(memory_space=pltpu.VMEM),
        out_shape=jax.ShapeDtypeStruct((r, width), F32),
        scratch_shapes=[pltpu.VMEM((ndev, r, width), F32), pltpu.SemaphoreType.DMA((ndev - 1,)),
                        pltpu.SemaphoreType.DMA((ndev - 1,))],
        compiler_params=pltpu.CompilerParams(vmem_limit_bytes=VMEM_LIMIT),
    )(slab)


def _swap_halves(grads):
    n = len(grads)

    def body(*refs):
        ins, outs = refs[:n], refs[n:2 * n]
        send_sems, recv_sems = refs[2 * n:]
        x, y, c, _ = _place()
        copies = []
        for i in range(n):
            rows = ins[i].shape[1] // 2
            for s in range(N_SHARD):
                cp = pltpu.make_async_remote_copy(
                    src_ref=ins[i].at[s, pl.ds((1 - c) * rows, rows)], dst_ref=outs[i].at[s],
                    send_sem=send_sems.at[N_SHARD * i + s], recv_sem=recv_sems.at[N_SHARD * i + s],
                    device_id=(x, y, 1 - c), device_id_type=MESH)
                cp.start()
                copies.append(cp)
        for cp in copies:
            cp.wait()

    return pl.pallas_call(
        body, name="grad_swap_halves",
        in_specs=[HBM] * n, out_specs=[HBM] * n,
        out_shape=[jax.ShapeDtypeStruct((N_SHARD, g.shape[1] // 2, g.shape[2]), F32) for g in grads],
        scratch_shapes=[pltpu.SemaphoreType.DMA((N_SHARD * n,)), pltpu.SemaphoreType.DMA((N_SHARD * n,))],
    )(*grads)


def _chip_partial(name, grad, other, core):
    s, r, cdim = grad.shape
    rows = r // 2
    tr = _pick(rows, (256, 128, 64, 32, 16))
    nb = rows // tr

    def body(core_ref, g_ref, o_ref, out_ref):
        out_ref[...] = (g_ref[...] + o_ref[...]).astype(BF16)

    return pl.pallas_call(
        body, name=name,
        grid_spec=pltpu.PrefetchScalarGridSpec(
            num_scalar_prefetch=1, grid=(s, nb),
            in_specs=[pl.BlockSpec((None, tr, cdim), lambda j, b, core_ref: (j, core_ref[0] * nb + b, 0)),
                      pl.BlockSpec((None, tr, cdim), lambda j, b, core_ref: (j, b, 0))],
            out_specs=pl.BlockSpec((None, tr, cdim), lambda j, b, core_ref: (j, b, 0))),
        out_shape=jax.ShapeDtypeStruct((s, rows, cdim), BF16),
        compiler_params=_params("parallel", "parallel"),
    )(core, grad, other)


def _send_partials(partials):
    n = len(partials)

    def body(*refs):
        ins, outs = refs[:n], refs[n:2 * n]
        send_sems, recv_sems = refs[2 * n:]
        x, y, c, chips = _place()
        copies = []
        for i in range(n):
            for j, chip in enumerate(chips):
                theirs = 2 * chip[0] + chip[1]
                cp = pltpu.make_async_remote_copy(
                    src_ref=ins[i].at[theirs], dst_ref=outs[i].at[j],
                    send_sem=send_sems.at[3 * i + j], recv_sem=recv_sems.at[3 * i + j],
                    device_id=(*chip, c), device_id_type=MESH)
                cp.start()
                copies.append(cp)
        for cp in copies:
            cp.wait()

    return pl.pallas_call(
        body, name="grad_send_partials",
        in_specs=[HBM] * n, out_specs=[HBM] * n,
        out_shape=[jax.ShapeDtypeStruct((3,) + p.shape[1:], BF16) for p in partials],
        scratch_shapes=[pltpu.SemaphoreType.DMA((3 * n,)), pltpu.SemaphoreType.DMA((3 * n,))],
    )(*partials)


def _reduce_own(name, grad, other, received, where):
    s, r, cdim = grad.shape
    rows = r // 2
    tr = _pick(rows, (256, 128, 64, 32, 16))
    nb = rows // tr

    def body(where_ref, g_ref, o_ref, r_ref, out_ref):
        total = g_ref[...] + o_ref[...]
        for j in range(3):
            total = total + r_ref[j].astype(F32)
        out_ref[...] = total

    return pl.pallas_call(
        body, name=name,
        grid_spec=pltpu.PrefetchScalarGridSpec(
            num_scalar_prefetch=1, grid=(nb,),
            in_specs=[pl.BlockSpec((None, tr, cdim), lambda b, w_ref: (w_ref[0], w_ref[1] * nb + b, 0)),
                      pl.BlockSpec((None, tr, cdim), lambda b, w_ref: (w_ref[0], b, 0)),
                      pl.BlockSpec((3, tr, cdim), lambda b, w_ref: (0, b, 0))],
            out_specs=pl.BlockSpec((tr, cdim), lambda b, w_ref: (w_ref[1] * nb + b, 0))),
        out_shape=jax.ShapeDtypeStruct((r, cdim), F32),
        compiler_params=_params("parallel"),
    )(where, grad, other, received)


def _join_halves(halves):
    n = len(halves)

    def body(*refs):
        bufs = refs[n:2 * n]
        send_sems, recv_sems = refs[2 * n:]
        x, y, c, _ = _place()
        copies = []
        for i in range(n):
            rows = bufs[i].shape[0] // 2
            mine = bufs[i].at[pl.ds(c * rows, rows)]
            cp = pltpu.make_async_remote_copy(src_ref=mine, dst_ref=mine, send_sem=send_sems.at[i],
                                              recv_sem=recv_sems.at[i], device_id=(x, y, 1 - c), device_id_type=MESH)
            cp.start()
            copies.append(cp)
        for i, cp in enumerate(copies):
            rows = bufs[i].shape[0] // 2
            theirs = bufs[i].at[pl.ds((1 - c) * rows, rows)]
            pltpu.make_async_remote_copy(src_ref=theirs, dst_ref=theirs, send_sem=send_sems.at[i],
                                         recv_sem=recv_sems.at[i], device_id=(x, y, 1 - c),
                                         device_id_type=MESH).wait_recv()
            cp.wait_send()

    return pl.pallas_call(
        body, name="grad_join_halves",
        in_specs=[HBM] * n, out_specs=[HBM] * n,
        out_shape=[jax.ShapeDtypeStruct(h.shape, F32) for h in halves],
        input_output_aliases={i: i for i in range(n)},
        scratch_shapes=[pltpu.SemaphoreType.DMA((n,)), pltpu.SemaphoreType.DMA((n,))],
    )(*halves)


BIG = ("w_in", "pool_w", "w_out", "xq_w", "xk_w", "xv_w", "xo_w", "w_up", "w_down")
SMALL = ("conv_w", "a_log", "dt_bias", "gdn_norm_w", "pool_scale", "ln1_g", "ln1_b", "ln2_g", "ln2_b", "ln3_g", "ln3_b")
ORDER = ("w_in", "conv_w", "a_log", "dt_bias", "gdn_norm_w", "pool_w", "pool_scale", "w_out", "ln1_g", "ln1_b",
         "xq_w", "xk_w", "xv_w", "xo_w", "ln2_g", "ln2_b", "w_up", "w_down", "ln3_g", "ln3_b")
LANES = 128


def _rows(flat_len):
    return -(-flat_len // LANES)


def _pack(pieces):
    out = []
    for p in pieces:
        flat = p.reshape(-1).astype(F32)
        out.append(jnp.pad(flat, (0, _rows(flat.shape[0]) * LANES - flat.shape[0])).reshape(-1, LANES))
    slab = jnp.concatenate(out, axis=0)
    return jnp.pad(slab, ((0, -slab.shape[0] % 8), (0, 0)))


def _unpack(slab, shapes):
    out, row = [], 0
    for shp in shapes:
        size = math.prod(shp)
        out.append(slab[row:row + _rows(size)].reshape(-1)[:size].reshape(shp))
        row += _rows(size)
    return out


def _as2d(a):
    a = a[0]
    return a.reshape(-1, a.shape[-1]) if a.ndim == 3 else a


def kernel(x, mem, w_in, conv_w, a_log, dt_bias, gdn_norm_w, pool_w, pool_scale, w_out, ln1_g, ln1_b, xq_w, xk_w, xv_w, xo_w, ln2_g, ln2_b, w_up, w_down, ln3_g, ln3_b, loss_target, m_w_in, m_conv_w, m_a_log, m_dt_bias, m_gdn_norm_w, m_pool_w, m_pool_scale, m_w_out, m_ln1_g, m_ln1_b, m_xq_w, m_xk_w, m_xv_w, m_xo_w, m_ln2_g, m_ln2_b, m_w_up, m_w_down, m_ln3_g, m_ln3_b, v_w_in, v_conv_w, v_a_log, v_dt_bias, v_gdn_norm_w, v_pool_w, v_pool_scale, v_w_out, v_ln1_g, v_ln1_b, v_xq_w, v_xk_w, v_xv_w, v_xo_w, v_ln2_g, v_ln2_b, v_w_up, v_w_down, v_ln3_g, v_ln3_b):
    given = dict(locals())
    cx, cy, cc = lax.axis_index("x"), lax.axis_index("y"), lax.axis_index("c")
    me = 2 * cx + cy
    groups = pool_w.shape[1]
    cs = pool_w.shape[2]
    kk, conv_cols = conv_w.shape[1], conv_w.shape[2]

    shards = [_as2d(given[n]).astype(BF16) for n in BIG]
    full = dict(zip(BIG, _gather_weights(shards)))
    wts = {n: full[n].reshape(-1, full[n].shape[-1]) for n in ("w_out", "xq_w", "xk_w", "xv_w", "xo_w", "w_down")}
    wts["w_in3"] = full["w_in"]
    wts["w_up3"] = full["w_up"]
    wts["pool_w"] = full["pool_w"].reshape(N_SHARD, groups, cs, -1).transpose(1, 0, 2, 3).reshape(groups, N_SHARD * cs, -1)
    conv_slab = jnp.zeros((kk, N_SHARD * conv_cols), F32)
    conv_slab = lax.dynamic_update_slice(conv_slab, conv_w[0] * (cc == 0).astype(F32), (0, me * conv_cols))
    wts["conv_w"] = _unpack(_all_reduce_small(_pack([conv_slab])), [conv_slab.shape])[0]
    for n in ("a_log", "dt_bias", "gdn_norm_w", "pool_scale", "ln1_g", "ln1_b", "ln2_g", "ln2_b", "ln3_g", "ln3_b"):
        wts[n] = given[n]

    loss_row, grad_x, g = _local_step(x[0], mem[0], loss_target[0], wts)

    small_names = ("a_log", "dt_bias", "gdn_norm_w", "pool_scale", "ln1_g", "ln1_b", "ln2_g", "ln2_b", "ln3_g", "ln3_b")
    pieces = [g["conv_w"]] + [g[n] for n in small_names] + [loss_row[:, :1]]
    shapes = [p.shape for p in pieces]
    summed = _unpack(_all_reduce_small(_pack(pieces)), shapes)
    gsmall = dict(zip(small_names, summed[1:-1]))
    gsmall["conv_w"] = lax.dynamic_slice(summed[0], (0, me * conv_cols), (kk, conv_cols))
    loss = summed[-1][0, 0]

    gpool = g["pool_w"].reshape(groups, N_SHARD, cs, -1).transpose(1, 0, 2, 3).reshape(N_SHARD, groups * cs, -1)
    blocks = {"w_in": g["w_in3"], "w_up": g["w_up3"], "pool_w": gpool}
    for n in ("w_out", "xq_w", "xk_w", "xv_w", "xo_w", "w_down"):
        blocks[n] = g[n].reshape(N_SHARD, -1, g[n].shape[-1])
    big = [blocks[n] for n in BIG]
    others = _swap_halves(big)
    core = cc.astype(jnp.int32).reshape(1)
    where = jnp.stack([me, cc]).astype(jnp.int32)
    partials = [_chip_partial("chip_partial_" + n, gb, ob, core) for n, gb, ob in zip(BIG, big, others)]
    received = _send_partials(partials)
    halves = [_reduce_own("reduce_own_" + n, gb, ob, rb, where) for n, gb, ob, rb in zip(BIG, big, others, received)]
    gbig = dict(zip(BIG, _join_halves(halves)))

    grad, delta, new_m, new_v = {}, {}, {}, {}
    for n in BIG:
        shp = given[n].shape
        d2, m2, v2 = _adamw("adamw_" + n, _as2d(given[n]), gbig[n], _as2d(given["m_" + n]), _as2d(given["v_" + n]))
        grad[n], delta[n], new_m[n], new_v[n] = (a.reshape(shp) for a in (gbig[n], d2, m2, v2))
    sshapes = [given[n].shape for n in SMALL]
    slabs = [_pack([given[p + n] for n in SMALL]) for p in ("", "m_", "v_")]
    gslab = _pack([gsmall[n] for n in SMALL])
    outs = _adamw("adamw_small", slabs[0], gslab, slabs[1], slabs[2])
    for dst, slab in zip((delta, new_m, new_v), outs):
        dst.update(zip(SMALL, _unpack(slab, sshapes)))
    for n in SMALL:
        grad[n] = gsmall[n].reshape(given[n].shape)

    return (loss, grad_x[None], *[grad[n] for n in ORDER], *[delta[n] for n in ORDER],
            *[new_m[n] for n in ORDER], *[new_v[n] for n in ORDER])
```

```python
import functools
import math

import jax
import jax.numpy as jnp
from jax import lax
from jax.experimental import pallas as pl
from jax.experimental.pallas import tpu as pltpu

F32 = jnp.float32
BF16 = jnp.bfloat16
MESH = pl.DeviceIdType.MESH

HEAD_DIM = 128
CHUNK = 64
POOL_WINDOWS = (2, 4, 8, 16)
XATTN_HEADS = 4
ALPHA = 2.0 ** 0.25
LN_EPS = 1e-5
NORM_EPS = 1e-6
ADAM_LR, ADAM_B1, ADAM_B2, ADAM_EPS, ADAM_WD, ADAM_STEP = 0.001, 0.9, 0.999, 1e-08, 0.01, 10
N_SHARD = 4
VMEM_LIMIT = 56 * 1024 * 1024
K_STEPS = (2048, 1024, 512, 256, 128)


def _params(*sem):
    return pltpu.CompilerParams(dimension_semantics=sem, vmem_limit_bytes=VMEM_LIMIT)


def _bdot(a, b, ta=False, tb=False):
    dims = (((0 if ta else 1,), (1 if tb else 0,)), ((), ()))
    return lax.dot_general(a.astype(BF16), b.astype(BF16), dims, preferred_element_type=F32)


def _sigmoid(x):
    return 1.0 / (1.0 + jnp.exp(-x))


def _matmul(name, a, b, *, ta=False, tb=False, tm, tn, tk, extra=(), outs, epilogue, b_blocks=None,
            sequential=False):
    m, k_dim = (a.shape[1], a.shape[0]) if ta else a.shape
    if b_blocks and tb:
        n = b.shape[1]
        k_dim = b.shape[0] * b.shape[2]
        per = b.shape[2] // tk
        b_spec = pl.BlockSpec((None, tn, tk), lambda i, j, k: (k // per, j, k % per))
    elif b_blocks:
        n = b.shape[0] * b.shape[2]
        per = b.shape[2] // tn
        b_spec = pl.BlockSpec((None, tk, tn), lambda i, j, k: (j // per, k, j % per))
    elif tb:
        n = b.shape[0]
        b_spec = pl.BlockSpec((tn, tk), lambda i, j, k: (j, k))
    else:
        n = b.shape[1]
        b_spec = pl.BlockSpec((tk, tn), lambda i, j, k: (k, j))
    assert m % tm == 0 and n % tn == 0 and k_dim % tk == 0, (name, m, n, k_dim, tm, tn, tk)
    nk = k_dim // tk
    a_spec = pl.BlockSpec((tk, tm), lambda i, j, k: (k, i)) if ta else pl.BlockSpec((tm, tk), lambda i, j, k: (i, k))
    n_extra, n_out = len(extra), len(outs)

    def wrap(index_map):
        return lambda i, j, k: index_map(i, j)

    def body_one_step(*refs):
        ex = refs[2:2 + n_extra]
        out = refs[2 + n_extra:2 + n_extra + n_out]
        epilogue(_bdot(refs[0][...], refs[1][...], ta, tb), ex, out, pl.program_id(0))

    def body(*refs):
        a_ref, b_ref = refs[0], refs[1]
        ex = refs[2:2 + n_extra]
        out = refs[2 + n_extra:2 + n_extra + n_out]
        acc = refs[-1]
        i, k = pl.program_id(0), pl.program_id(2)
        part = _bdot(a_ref[...], b_ref[...], ta, tb)

        @pl.when(k == 0)
        def _():
            acc[...] = part

        @pl.when(jnp.logical_and(k > 0, k < nk - 1))
        def _():
            acc[...] += part

        @pl.when(k == nk - 1)
        def _():
            epilogue(acc[...] + part, ex, out, i)

    sem = ("arbitrary",) * 3 if sequential else ("parallel", "parallel", "arbitrary")
    res = pl.pallas_call(
        body_one_step if nk == 1 else body, name=name, grid=(m // tm, n // tn, nk),
        in_specs=[a_spec, b_spec] + [pl.BlockSpec(bs, wrap(im)) for _, bs, im in extra],
        out_specs=[pl.BlockSpec(bs, wrap(im)) for _, bs, im in outs],
        out_shape=[s for s, _, _ in outs],
        scratch_shapes=[] if nk == 1 else [pltpu.VMEM((tm, tn), F32)],
        compiler_params=_params(*sem),
    )(a, b, *[x for x, _, _ in extra])
    return res


def _tile(i, j):
    return (i, j)


def _plain(name, a, b, *, ta=False, tb=False, tm, tn, tk, out_dtype, b_blocks=None, out3=None):
    m = a.shape[1] if ta else a.shape[0]
    n = (b.shape[0] * b.shape[2]) if b_blocks else (b.shape[0] if tb else b.shape[1])

    def epi(acc, ex, out, i):
        out[0][...] = acc.astype(out_dtype)

    if out3:
        per = (n // out3) // tn
        spec = (jax.ShapeDtypeStruct((out3, m, n // out3), out_dtype), (None, tm, tn),
                lambda i, j: (j // per, i, j % per))
    else:
        spec = (jax.ShapeDtypeStruct((m, n), out_dtype), (tm, tn), _tile)
    return _matmul(name, a, b, ta=ta, tb=tb, tm=tm, tn=tn, tk=tk, outs=[spec], epilogue=epi,
                   b_blocks=b_blocks)[0]


def _ln_forward(name, a, b, res, gamma, beta, *, tm, tk, want_bf16=True):
    m, n = res.shape

    def epi(acc, ex, out, i):
        u = ALPHA * ex[0][...] + acc
        mu = jnp.mean(u, axis=-1, keepdims=True)
        xc = u - mu
        var = jnp.mean(xc * xc, axis=-1, keepdims=True)
        rstd = lax.rsqrt(var + LN_EPS)
        xhat = xc * rstd
        h = xhat * ex[1][...] + ex[2][...]
        out[0][...] = h
        out[1][...] = h.astype(BF16)
        out[2][...] = xhat
        out[3][...] = rstd

    row = lambda i, j: (i, 0)
    vec = lambda i, j: (0, 0)
    return _matmul(
        name, a, b, tm=tm, tn=n, tk=tk,
        extra=[(res, (tm, n), row), (gamma, (1, n), vec), (beta, (1, n), vec)],
        outs=[(jax.ShapeDtypeStruct((m, n), F32), (tm, n), row),
              (jax.ShapeDtypeStruct((m, n), BF16), (tm, n), row),
              (jax.ShapeDtypeStruct((m, n), F32), (tm, n), row),
              (jax.ShapeDtypeStruct((m, 1), F32), (tm, 1), row)],
        epilogue=epi)


def _ln_backward_math(dy, xhat, rstd, gamma):
    dxhat = dy * gamma
    m1 = jnp.mean(dxhat, axis=-1, keepdims=True)
    m2 = jnp.mean(dxhat * xhat, axis=-1, keepdims=True)
    du = rstd * (dxhat - m1 - xhat * m2)
    return du, jnp.sum(dy * xhat, axis=0, keepdims=True), jnp.sum(dy, axis=0, keepdims=True)


def _ln_backward(name, a, b, dres, xhat, rstd, gamma, *, tm, tk, b_blocks=None, tb=True):
    m, n = dres.shape

    def epi(acc, ex, out, i):
        dy = acc + ALPHA * ex[0][...]
        du, dg, db = _ln_backward_math(dy, ex[1][...], ex[2][...], ex[3][...])
        out[0][...] = du
        out[1][...] = du.astype(BF16)
        first = i == 0

        @pl.when(first)
        def _():
            out[2][...] = dg
            out[3][...] = db

        @pl.when(jnp.logical_not(first))
        def _():
            out[2][...] += dg
            out[3][...] += db

    row = lambda i, j: (i, 0)
    vec = lambda i, j: (0, 0)
    return _matmul(
        name, a, b, tb=tb, tm=tm, tn=n, tk=tk, b_blocks=b_blocks, sequential=True,
        extra=[(dres, (tm, n), row), (xhat, (tm, n), row), (rstd, (tm, 1), row), (gamma, (1, n), vec)],
        outs=[(jax.ShapeDtypeStruct((m, n), F32), (tm, n), row),
              (jax.ShapeDtypeStruct((m, n), BF16), (tm, n), row),
              (jax.ShapeDtypeStruct((1, n), F32), (1, n), vec),
              (jax.ShapeDtypeStruct((1, n), F32), (1, n), vec)],
        epilogue=epi)


def _shift_down(x, k):
    row = lax.broadcasted_iota(jnp.int32, x.shape, 0)
    return jnp.where(row >= k, pltpu.roll(x, k, axis=0), 0.0)


def _shift_up(x, k):
    t = x.shape[0]
    row = lax.broadcasted_iota(jnp.int32, x.shape, 0)
    return jnp.where(row < t - k, pltpu.roll(x, t - k, axis=0), 0.0)


def _conv_silu_norm(x, w, normalise):
    kk = w.shape[0]
    c = x * w[kk - 1:kk, :]
    for j in range(kk - 1):
        c = c + _shift_down(x, kk - 1 - j) * w[j:j + 1, :]
    sg = _sigmoid(c)
    s = c * sg
    r = lax.rsqrt(jnp.sum(s * s, axis=-1, keepdims=True) + NORM_EPS)
    y = jnp.where(normalise, s * r, s)
    return c, sg, s, r, y


def _gdn_pre(proj, conv_w, heads):
    t = proj.shape[0]
    kk = conv_w.shape[0]

    def body(x_ref, w_ref, o_ref):
        normalise = pl.program_id(0) < 2
        o_ref[...] = _conv_silu_norm(x_ref[...], w_ref[...], normalise)[4]

    col = lambda s, h: (0, s * heads + h)
    return pl.pallas_call(
        body, name="gdn_pre", grid=(3, heads),
        in_specs=[pl.BlockSpec((t, HEAD_DIM), col), pl.BlockSpec((kk, HEAD_DIM), col)],
        out_specs=pl.BlockSpec((t, HEAD_DIM), col),
        out_shape=jax.ShapeDtypeStruct((t, 3 * heads * HEAD_DIM), F32),
        compiler_params=_params("parallel", "parallel"),
    )(proj, conv_w)


def _gdn_pre_backward(proj, conv_w, dqkv, heads):
    t = proj.shape[0]
    kk = conv_w.shape[0]

    def body(x_ref, w_ref, dy_ref, dx_ref, dw_ref):
        normalise = pl.program_id(0) < 2
        x = x_ref[...]
        w = w_ref[...]
        dy = dy_ref[...]
        c, sg, s, r, y = _conv_silu_norm(x, w, normalise)
        ds_norm = r * (dy - y * jnp.sum(dy * y, axis=-1, keepdims=True))
        ds = jnp.where(normalise, ds_norm, dy)
        dc = ds * (sg * (1.0 + c * (1.0 - sg)))
        dx = dc * w[kk - 1:kk, :]
        rows = [None] * kk
        rows[kk - 1] = jnp.sum(dc * x, axis=0, keepdims=True)
        for j in range(kk - 1):
            lag = kk - 1 - j
            dx = dx + _shift_up(dc, lag) * w[j:j + 1, :]
            rows[j] = jnp.sum(dc * _shift_down(x, lag), axis=0, keepdims=True)
        dx_ref[...] = dx.astype(BF16)
        dw_ref[...] = jnp.concatenate(rows, axis=0)

    col = lambda s, h: (0, s * heads + h)
    return pl.pallas_call(
        body, name="gdn_pre_bwd", grid=(3, heads),
        in_specs=[pl.BlockSpec((t, HEAD_DIM), col), pl.BlockSpec((kk, HEAD_DIM), col),
                  pl.BlockSpec((t, HEAD_DIM), col)],
        out_specs=[pl.BlockSpec((t, HEAD_DIM), col), pl.BlockSpec((kk, HEAD_DIM), col)],
        out_shape=[jax.ShapeDtypeStruct((t, 3 * heads * HEAD_DIM), BF16),
                   jax.ShapeDtypeStruct((kk, 3 * heads * HEAD_DIM), F32)],
        compiler_params=_params("parallel", "parallel"),
    )(proj, conv_w, dqkv)


def _gate_vectors(a_log, dt_bias, heads):
    pad = lambda v: jnp.pad(v.astype(F32), ((0, 0), (heads, HEAD_DIM - 2 * heads)))
    return pad(jnp.exp(a_log.astype(F32))), pad(dt_bias)


def _softplus(x):
    return jnp.maximum(x, 0.0) + jnp.log(1.0 + jnp.exp(-jnp.abs(x)))


def _gates_epilogue(heads):
    def epi(acc, ex, out, i):
        lane = lax.broadcasted_iota(jnp.int32, acc.shape, 1)
        beta = _sigmoid(acc)
        g = -ex[0][...] * _softplus(acc + ex[1][...])
        out[0][...] = acc
        out[1][...] = jnp.where(lane < heads, beta, jnp.where(lane < 2 * heads, g, 0.0))
    return epi


def _gates_backward(ba, bg, dbg, ea, dtb, heads):
    t = ba.shape[0]

    def body(ba_ref, bg_ref, d_ref, ea_ref, dt_ref, dba_ref, dal_ref, ddt_ref):
        lane = lax.broadcasted_iota(jnp.int32, (t, HEAD_DIM), 1)
        bgv = bg_ref[...]
        d = d_ref[...]
        db = d * bgv * (1.0 - bgv)
        da = -d * ea_ref[...] * _sigmoid(ba_ref[...] + dt_ref[...])
        is_g = jnp.logical_and(lane >= heads, lane < 2 * heads)
        dba = jnp.where(lane < heads, db, jnp.where(is_g, da, 0.0))
        dba_ref[...] = dba.astype(BF16)
        dal_ref[...] = jnp.sum(jnp.where(is_g, d * bgv, 0.0), axis=0, keepdims=True)
        ddt_ref[...] = jnp.sum(jnp.where(is_g, da, 0.0), axis=0, keepdims=True)

    full = pl.BlockSpec((t, HEAD_DIM), lambda: (0, 0))
    vec = pl.BlockSpec((1, HEAD_DIM), lambda: (0, 0))
    return pl.pallas_call(
        body, name="gates_bwd", grid=(),
        in_specs=[full, full, full, vec, vec], out_specs=[full, vec, vec],
        out_shape=[jax.ShapeDtypeStruct((t, HEAD_DIM), BF16), jax.ShapeDtypeStruct((1, HEAD_DIM), F32),
                   jax.ShapeDtypeStruct((1, HEAD_DIM), F32)],
        compiler_params=pltpu.CompilerParams(vmem_limit_bytes=VMEM_LIMIT),
    )(ba, bg, dbg, ea, dtb)


class _Chunk:
    pass


def _split2(x):
    hi = x.astype(BF16)
    return hi, (x - hi.astype(F32)).astype(BF16)


def _split3(x):
    hi = x.astype(BF16)
    rest = x - hi.astype(F32)
    mid = rest.astype(BF16)
    return hi, mid, (rest - mid.astype(F32)).astype(BF16)


def _dot_mask(mask, x, ta=False):
    hi, mid, lo = _split3(x)
    return _bdot(mask, hi, ta=ta) + (_bdot(mask, mid, ta=ta) + _bdot(mask, lo, ta=ta))


def _transpose_by_identity(x):
    r = x.shape[0]
    eye = (lax.broadcasted_iota(jnp.int32, (r, r), 0) == lax.broadcasted_iota(jnp.int32, (r, r), 1)).astype(BF16)
    hi, mid, lo = _split3(x)
    return _bdot(hi, eye, ta=True) + (_bdot(mid, eye, ta=True) + _bdot(lo, eye, ta=True))


def _dot22(a, b, ta=False, tb=False):
    ah, al = _split2(a)
    bh, bl = _split2(b)
    return _bdot(ah, bh, ta, tb) + (_bdot(ah, bl, ta, tb) + _bdot(al, bh, ta, tb))


def _chunk_gates(bg, heads):
    n = CHUNK
    row = lax.broadcasted_iota(jnp.int32, (n, n), 0)
    col = lax.broadcasted_iota(jnp.int32, (n, n), 1)
    lane = lax.broadcasted_iota(jnp.int32, bg.shape, 1)
    graw = jnp.where(jnp.logical_and(lane >= heads, lane < 2 * heads), bg, 0.0)
    gc = _dot_mask((row >= col).astype(BF16), graw)
    return gc, _transpose_by_identity(gc)


def _in_lockstep(generators):
    results = [None] * len(generators)
    live = list(enumerate(generators))
    while live:
        still = []
        for i, gen in live:
            try:
                next(gen)
                still.append((i, gen))
            except StopIteration as stop:
                results[i] = stop.value
        live = still
    return results


def _chunk_local(q, k, v, beta, gc, grow):
    c = _Chunk()
    n = CHUNK
    row = lax.broadcasted_iota(jnp.int32, (n, n), 0)
    col = lax.broadcasted_iota(jnp.int32, (n, n), 1)
    c.tri = row >= col
    c.strict = row > col
    eye = row == col
    c.gcb = jnp.broadcast_to(gc, (n, HEAD_DIM))
    c.decay = jnp.where(c.tri, jnp.exp(jnp.where(c.tri, gc - grow, 0.0)), 0.0)
    c.eg = jnp.exp(c.gcb)
    glast = c.gcb[n - 1:n, :]
    c.egl = jnp.exp(glast)
    c.ekl = jnp.exp(glast - c.gcb)
    c.beta = beta
    c.q = q * (HEAD_DIM ** -0.5)
    c.k = k
    c.v = v
    c.kb = k * beta
    c.vb = v * beta
    c.kg = c.kb * c.eg
    both = _bdot(jnp.concatenate([c.kb, c.q], axis=0), k, tb=True)
    yield
    c.L = jnp.where(c.strict, both[:n] * c.decay, 0.0)
    c.A = jnp.where(c.tri, both[n:] * c.decay, 0.0)
    x = -c.L
    tinv = eye.astype(F32) + x
    p = _dot22(x, x)
    yield
    for _ in range(int(math.log2(n)) - 2):
        both = _dot22(jnp.concatenate([p, tinv], axis=0), p)
        yield
        p, tinv = both[:n], tinv + both[n:]
    c.T = tinv + _dot22(tinv, p)
    yield
    tinv = c.T
    uw = _dot22(tinv, jnp.concatenate([c.vb, c.kg], axis=1))
    yield
    c.u, c.w = uw[:, :HEAD_DIM], uw[:, HEAD_DIM:]
    c.qg = c.q * c.eg
    c.kdec = k * c.ekl
    return c


def _gdn_core(qkv, bg, heads):
    t = qkv.shape[0]
    nchunk = t // CHUNK

    gw = heads * HEAD_DIM

    def body(qkv_ref, bg_ref, o_ref, s_ref, state):
        @pl.when(pl.program_id(0) == 0)
        def _():
            state[...] = jnp.zeros_like(state)

        bg_v = bg_ref[...]
        gc_all, gc_rows = _chunk_gates(bg_v, heads)
        def one_head(h):
            col = lambda s: pl.ds(s * gw + h * HEAD_DIM, HEAD_DIM)
            c = yield from _chunk_local(qkv_ref[:, col(0)], qkv_ref[:, col(1)], qkv_ref[:, col(2)], bg_v[:, h:h + 1],
                                        gc_all[:, heads + h:heads + h + 1], gc_rows[heads + h:heads + h + 1, :])
            s0 = state[h]
            v_new = c.u - _bdot(c.w, s0)
            yield
            o = _bdot(c.qg, s0) + _bdot(c.A, v_new)
            return s0, o, s0 * c.egl + _bdot(c.kdec, v_new, ta=True)

        results = _in_lockstep([one_head(h) for h in range(heads)])
        for h, (s0, o, s1) in enumerate(results):
            s_ref[h, 0] = s0
            o_ref[:, pl.ds(h * HEAD_DIM, HEAD_DIM)] = o
            state[h] = s1

    return pl.pallas_call(
        body, name="gdn_core", grid=(nchunk,),
        in_specs=[pl.BlockSpec((CHUNK, 3 * gw), lambda n: (n, 0)), pl.BlockSpec((CHUNK, HEAD_DIM), lambda n: (n, 0))],
        out_specs=[pl.BlockSpec((CHUNK, gw), lambda n: (n, 0)),
                   pl.BlockSpec((heads, 1, HEAD_DIM, HEAD_DIM), lambda n: (0, n, 0, 0))],
        out_shape=[jax.ShapeDtypeStruct((t, gw), F32),
                   jax.ShapeDtypeStruct((heads, nchunk, HEAD_DIM, HEAD_DIM), F32)],
        scratch_shapes=[pltpu.VMEM((heads, HEAD_DIM, HEAD_DIM), F32)],
        compiler_params=_params("arbitrary"),
    )(qkv, bg)


def _gdn_core_backward(qkv, bg, states, do, heads):
    t = qkv.shape[0]
    nchunk = t // CHUNK
    n = CHUNK

    def one_head(chunk_local, s0, d_out, ds1):
        c = yield from chunk_local
        v_new = c.u - _bdot(c.w, s0)
        dqg = _bdot(d_out, s0, tb=True)
        ds0 = _bdot(c.qg, d_out, ta=True) + ds1 * c.egl
        dv_new = _bdot(c.A, d_out, ta=True) + _bdot(c.kdec, ds1)
        yield
        dA = jnp.where(c.tri, _bdot(d_out, v_new, tb=True), 0.0)
        dkdec = _bdot(v_new, ds1, tb=True)
        dgl = jnp.sum(jnp.sum(ds1 * s0, axis=1, keepdims=True), axis=0, keepdims=True) * c.egl
        dw = -_bdot(dv_new, s0, tb=True)
        ds0 = ds0 - _bdot(c.w, dv_new, ta=True)
        yield
        both = _dot22(c.T, jnp.concatenate([dv_new, dw], axis=1), ta=True)
        yield
        dvb, dkg = both[:, :HEAD_DIM], both[:, HEAD_DIM:]
        dL = jnp.where(c.strict, -(_bdot(dvb, c.u, tb=True) + _bdot(dkg, c.w, tb=True)), 0.0)
        yield
        dm1 = dL * c.decay
        dkb = _bdot(dm1, c.k) + dkg * c.eg
        dk = _bdot(dm1, c.kb, ta=True)
        dm2 = dA * c.decay
        dq = _bdot(dm2, c.k) + dqg * c.eg
        dk = dk + _bdot(dm2, c.q, ta=True) + dkdec * c.ekl + dkb * c.beta
        pm = dL * c.L + dA * c.A
        ones = jnp.ones((n, HEAD_DIM), BF16)
        pm_hi, pm_lo = _split2(pm)
        colsum = _bdot(pm_hi, ones, ta=True) + _bdot(pm_lo, ones, ta=True)
        tk_ = jnp.sum(dkdec * c.kdec, axis=1, keepdims=True)
        dgc = (jnp.sum(pm, axis=1, keepdims=True) - colsum
               + jnp.sum(dqg * c.qg, axis=1, keepdims=True)
               - tk_
               + jnp.sum(dkg * c.kg, axis=1, keepdims=True))
        dgl = dgl + jnp.sum(tk_, axis=0, keepdims=True)
        rowi = lax.broadcasted_iota(jnp.int32, (n, HEAD_DIM), 0)
        dgc = dgc + jnp.where(rowi == n - 1, dgl, 0.0)
        dbeta = jnp.sum(dkb * c.k, axis=1, keepdims=True) + jnp.sum(dvb * c.v, axis=1, keepdims=True)
        return dq * (HEAD_DIM ** -0.5), dk, dvb * c.beta, dbeta, dgc, ds0

    gw = heads * HEAD_DIM

    def body(qkv_ref, bg_ref, s_ref, do_ref, dqkv_ref, dbg_ref, dstate):
        @pl.when(pl.program_id(0) == 0)
        def _():
            dstate[...] = jnp.zeros_like(dstate)

        bg_v = bg_ref[...]
        gc_all, gc_rows = _chunk_gates(bg_v, heads)
        lane = lax.broadcasted_iota(jnp.int32, (n, HEAD_DIM), 1)
        dgates = jnp.zeros((n, HEAD_DIM), F32)
        chains = []
        for h in range(heads):
            col = lambda s: pl.ds(s * gw + h * HEAD_DIM, HEAD_DIM)
            c = _chunk_local(qkv_ref[:, col(0)], qkv_ref[:, col(1)], qkv_ref[:, col(2)], bg_v[:, h:h + 1],
                             gc_all[:, heads + h:heads + h + 1], gc_rows[heads + h:heads + h + 1, :])
            chains.append(one_head(c, s_ref[h, 0], do_ref[:, pl.ds(h * HEAD_DIM, HEAD_DIM)], dstate[h]))
        results = _in_lockstep(chains)
        for h, (dq, dk, dv, dbeta, dgc, ds0) in enumerate(results):
            dgates = jnp.where(lane == h, dbeta, jnp.where(lane == heads + h, dgc, dgates))
        for h, (dq, dk, dv, dbeta, dgc, ds0) in enumerate(results):
            dqkv_ref[:, pl.ds(h * HEAD_DIM, HEAD_DIM)] = dq
            dqkv_ref[:, pl.ds(gw + h * HEAD_DIM, HEAD_DIM)] = dk
            dqkv_ref[:, pl.ds(2 * gw + h * HEAD_DIM, HEAD_DIM)] = dv
            dstate[h] = ds0
        row = lax.broadcasted_iota(jnp.int32, (n, n), 0)
        colm = lax.broadcasted_iota(jnp.int32, (n, n), 1)
        draw = _dot_mask((row >= colm).astype(BF16), dgates, ta=True)
        dbg_ref[...] = jnp.where(lane < heads, dgates, draw)

    last = nchunk - 1
    return pl.pallas_call(
        body, name="gdn_core_bwd", grid=(nchunk,),
        in_specs=[pl.BlockSpec((CHUNK, 3 * gw), lambda i: (last - i, 0)),
                  pl.BlockSpec((CHUNK, HEAD_DIM), lambda i: (last - i, 0)),
                  pl.BlockSpec((heads, 1, HEAD_DIM, HEAD_DIM), lambda i: (0, last - i, 0, 0)),
                  pl.BlockSpec((CHUNK, gw), lambda i: (last - i, 0))],
        out_specs=[pl.BlockSpec((CHUNK, 3 * gw), lambda i: (last - i, 0)),
                   pl.BlockSpec((CHUNK, HEAD_DIM), lambda i: (last - i, 0))],
        out_shape=[jax.ShapeDtypeStruct((t, 3 * gw), F32), jax.ShapeDtypeStruct((t, HEAD_DIM), F32)],
        scratch_shapes=[pltpu.VMEM((heads, HEAD_DIM, HEAD_DIM), F32)],
        compiler_params=_params("arbitrary"),
    )(qkv, bg, states, do)


def _gdn_post(o, proj, z_col0, norm_w, heads, tt):
    t = o.shape[0]
    zb = z_col0 // HEAD_DIM

    def body(o_ref, z_ref, w_ref, out_ref):
        ov = o_ref[...]
        z = z_ref[...]
        rms = lax.rsqrt(jnp.mean(ov * ov, axis=-1, keepdims=True) + NORM_EPS)
        out_ref[...] = (ov * rms * w_ref[...] * (z * _sigmoid(z))).astype(BF16)

    return pl.pallas_call(
        body, name="gdn_post", grid=(t // tt, heads),
        in_specs=[pl.BlockSpec((tt, HEAD_DIM), lambda i, h: (i, h)),
                  pl.BlockSpec((tt, HEAD_DIM), lambda i, h: (i, zb + h)),
                  pl.BlockSpec((1, HEAD_DIM), lambda i, h: (0, 0))],
        out_specs=pl.BlockSpec((tt, HEAD_DIM), lambda i, h: (i, h)),
        out_shape=jax.ShapeDtypeStruct((t, heads * HEAD_DIM), BF16),
        compiler_params=_params("parallel", "parallel"),
    )(o, proj, norm_w)


def _gdn_post_backward(dcat, o, proj, z_col0, norm_w, heads, tt):
    t = o.shape[0]
    zb = z_col0 // HEAD_DIM

    def body(d_ref, o_ref, z_ref, w_ref, do_ref, dz_ref, dw_ref):
        d = d_ref[...]
        ov = o_ref[...]
        z = z_ref[...]
        w = w_ref[...]
        rms = lax.rsqrt(jnp.mean(ov * ov, axis=-1, keepdims=True) + NORM_EPS)
        ohat = ov * rms
        sg = _sigmoid(z)
        gate = z * sg
        dz_ref[...] = (d * ohat * w * (sg * (1.0 + z * (1.0 - sg)))).astype(BF16)
        don = d * gate
        dohat = don * w
        do_ref[...] = rms * (dohat - ohat * jnp.mean(dohat * ohat, axis=-1, keepdims=True))
        dw = jnp.sum(don * ohat, axis=0, keepdims=True)
        first = jnp.logical_and(pl.program_id(0) == 0, pl.program_id(1) == 0)

        @pl.when(first)
        def _():
            dw_ref[...] = dw

        @pl.when(jnp.logical_not(first))
        def _():
            dw_ref[...] += dw

    blk = pl.BlockSpec((tt, HEAD_DIM), lambda i, h: (i, h))
    return pl.pallas_call(
        body, name="gdn_post_bwd", grid=(t // tt, heads),
        in_specs=[blk, blk, pl.BlockSpec((tt, HEAD_DIM), lambda i, h: (i, zb + h)),
                  pl.BlockSpec((1, HEAD_DIM), lambda i, h: (0, 0))],
        out_specs=[blk, blk, pl.BlockSpec((1, HEAD_DIM), lambda i, h: (0, 0))],
        out_shape=[jax.ShapeDtypeStruct((t, heads * HEAD_DIM), F32),
                   jax.ShapeDtypeStruct((t, heads * HEAD_DIM), BF16),
                   jax.ShapeDtypeStruct((1, HEAD_DIM), F32)],
        compiler_params=_params("arbitrary", "arbitrary"),
    )(dcat, o, proj, norm_w)


def _pool_select(levels, group):
    out = levels[-1]
    for gi in range(len(levels) - 2, -1, -1):
        out = jnp.where(group == gi, levels[gi], out)
    return out


def _pool_counts(t, width, group):
    pos = lax.broadcasted_iota(jnp.int32, (t, width), 0)
    win = jnp.left_shift(2, group)
    return jnp.minimum(pos + 1, win).astype(F32)


def _pooled(p, group):
    levels, s, step = [], p, 1
    for _ in POOL_WINDOWS:
        s = s + _shift_down(s, step)
        levels.append(s)
        step *= 2
    cnt = _pool_counts(p.shape[0], p.shape[1], group)
    return _pool_select(levels, group) / cnt - p, cnt


def _pool_forward(proj, p_col0, pool_w, pool_scale):
    t = proj.shape[0]
    groups, cg, _ = pool_w.shape
    pb = p_col0 // cg

    def body(p_ref, w_ref, s_ref, o_ref):
        pooled, _ = _pooled(p_ref[...], pl.program_id(0))
        o_ref[...] = (_bdot(pooled, w_ref[0]) * s_ref[...]).astype(BF16)

    return pl.pallas_call(
        body, name="pool_fwd", grid=(groups,),
        in_specs=[pl.BlockSpec((t, cg), lambda g: (0, pb + g)), pl.BlockSpec((1, cg, cg), lambda g: (g, 0, 0)),
                  pl.BlockSpec((1, cg), lambda g: (0, g))],
        out_specs=pl.BlockSpec((t, cg), lambda g: (0, g)),
        out_shape=jax.ShapeDtypeStruct((t, groups * cg), BF16),
        compiler_params=_params("parallel"),
    )(proj, pool_w, pool_scale)


def _pool_backward(dcat, d_col0, proj, p_col0, pool_w, pool_scale):
    t = proj.shape[0]
    groups, cg, _ = pool_w.shape
    pb = p_col0 // cg
    db = d_col0 // cg

    def body(d_ref, p_ref, w_ref, s_ref, dp_ref, dw_ref, ds_ref):
        group = pl.program_id(0)
        pooled, cnt = _pooled(p_ref[...], group)
        w = w_ref[0]
        d = d_ref[...]
        mixed = _bdot(pooled, w)
        ds_ref[...] = jnp.sum(d * mixed, axis=0, keepdims=True)
        dmixed = d * s_ref[...]
        dw_ref[0] = _bdot(pooled, dmixed, ta=True)
        dpooled = _bdot(dmixed, w, tb=True)
        levels, s, step = [], dpooled / cnt, 1
        for _ in POOL_WINDOWS:
            s = s + _shift_up(s, step)
            levels.append(s)
            step *= 2
        dp_ref[...] = (_pool_select(levels, group) - dpooled).astype(BF16)

    return pl.pallas_call(
        body, name="pool_bwd", grid=(groups,),
        in_specs=[pl.BlockSpec((t, cg), lambda g: (0, db + g)), pl.BlockSpec((t, cg), lambda g: (0, pb + g)),
                  pl.BlockSpec((1, cg, cg), lambda g: (g, 0, 0)), pl.BlockSpec((1, cg), lambda g: (0, g))],
        out_specs=[pl.BlockSpec((t, cg), lambda g: (0, g)), pl.BlockSpec((1, cg, cg), lambda g: (g, 0, 0)),
                   pl.BlockSpec((1, cg), lambda g: (0, g))],
        out_shape=[jax.ShapeDtypeStruct((t, groups * cg), BF16), jax.ShapeDtypeStruct((groups, cg, cg), F32),
                   jax.ShapeDtypeStruct((1, groups * cg), F32)],
        compiler_params=_params("parallel"),
    )(dcat, proj, pool_w, pool_scale)


def _attention(q, k, v, tq):
    t, d = q.shape
    m = k.shape[0]
    dh = d // XATTN_HEADS
    scale = dh ** -0.5

    def body(q_ref, k_ref, v_ref, o_ref):
        s = _bdot(q_ref[...], k_ref[...], tb=True) * scale
        s = s - jnp.max(s, axis=-1, keepdims=True)
        e = jnp.exp(s)
        p = e / jnp.sum(e, axis=-1, keepdims=True)
        o_ref[...] = _bdot(p, v_ref[...]).astype(BF16)

    return pl.pallas_call(
        body, name="xattn_fwd", grid=(XATTN_HEADS, t // tq),
        in_specs=[pl.BlockSpec((tq, dh), lambda h, i: (i, h)), pl.BlockSpec((m, dh), lambda h, i: (0, h)),
                  pl.BlockSpec((m, dh), lambda h, i: (0, h))],
        out_specs=pl.BlockSpec((tq, dh), lambda h, i: (i, h)),
        out_shape=jax.ShapeDtypeStruct((t, d), BF16),
        compiler_params=_params("parallel", "parallel"),
    )(q, k, v)


def _attention_backward(q, k, v, do, tq):
    t, d = q.shape
    m = k.shape[0]
    dh = d // XATTN_HEADS
    scale = dh ** -0.5

    def body(q_ref, k_ref, v_ref, do_ref, dq_ref, dk_ref, dv_ref, dk_acc, dv_acc):
        i = pl.program_id(1)
        qv, kv, vv, dov = q_ref[...], k_ref[...], v_ref[...], do_ref[...]
        s = _bdot(qv, kv, tb=True) * scale
        s = s - jnp.max(s, axis=-1, keepdims=True)
        e = jnp.exp(s)
        p = e / jnp.sum(e, axis=-1, keepdims=True)
        dp = _bdot(dov, vv, tb=True)
        ds = p * (dp - jnp.sum(dp * p, axis=-1, keepdims=True)) * scale
        dq_ref[...] = _bdot(ds, kv).astype(BF16)
        dv_part = _bdot(p, dov, ta=True)
        dk_part = _bdot(ds, qv, ta=True)

        @pl.when(i == 0)
        def _():
            dk_acc[...] = dk_part
            dv_acc[...] = dv_part

        @pl.when(i > 0)
        def _():
            dk_acc[...] += dk_part
            dv_acc[...] += dv_part

        @pl.when(i == pl.num_programs(1) - 1)
        def _():
            dk_ref[...] = dk_acc[...].astype(BF16)
            dv_ref[...] = dv_acc[...].astype(BF16)

    qblk = pl.BlockSpec((tq, dh), lambda h, i: (i, h))
    kblk = pl.BlockSpec((m, dh), lambda h, i: (0, h))
    return pl.pallas_call(
        body, name="xattn_bwd", grid=(XATTN_HEADS, t // tq),
        in_specs=[qblk, kblk, kblk, qblk],
        out_specs=[qblk, kblk, kblk],
        out_shape=[jax.ShapeDtypeStruct((t, d), BF16), jax.ShapeDtypeStruct((m, d), BF16),
                   jax.ShapeDtypeStruct((m, d), BF16)],
        scratch_shapes=[pltpu.VMEM((m, dh), F32), pltpu.VMEM((m, dh), F32)],
        compiler_params=_params("parallel", "arbitrary"),
    )(q, k, v, do)


def _loss_and_ln_backward(xhat, rstd, gamma, beta, target, tm):
    t, d = xhat.shape

    def body(x_ref, r_ref, g_ref, b_ref, t_ref, du_ref, dub_ref, dg_ref, db_ref, loss_ref):
        xh = x_ref[...]
        g = g_ref[...]
        diff = xh * g + b_ref[...] - t_ref[...]
        part = jnp.sum(jnp.sum(diff * diff, axis=1, keepdims=True), axis=0, keepdims=True) * (0.5 / d)
        dy = diff * (1.0 / d)
        du, dg, db = _ln_backward_math(dy, xh, r_ref[...], g)
        du_ref[...] = du
        dub_ref[...] = du.astype(BF16)
        lossrow = jnp.broadcast_to(part, (1, HEAD_DIM))
        first = pl.program_id(0) == 0

        @pl.when(first)
        def _():
            dg_ref[...] = dg
            db_ref[...] = db
            loss_ref[...] = lossrow

        @pl.when(jnp.logical_not(first))
        def _():
            dg_ref[...] += dg
            db_ref[...] += db
            loss_ref[...] += lossrow

    row = pl.BlockSpec((tm, d), lambda i: (i, 0))
    vec = pl.BlockSpec((1, d), lambda i: (0, 0))
    return pl.pallas_call(
        body, name="loss_ln3_bwd", grid=(t // tm,),
        in_specs=[row, pl.BlockSpec((tm, 1), lambda i: (i, 0)), vec, vec, row],
        out_specs=[row, row, vec, vec, pl.BlockSpec((1, HEAD_DIM), lambda i: (0, 0))],
        out_shape=[jax.ShapeDtypeStruct((t, d), F32), jax.ShapeDtypeStruct((t, d), BF16),
                   jax.ShapeDtypeStruct((1, d), F32), jax.ShapeDtypeStruct((1, d), F32),
                   jax.ShapeDtypeStruct((1, HEAD_DIM), F32)],
        compiler_params=_params("arbitrary"),
    )(xhat, rstd, gamma, beta, target)


def _pick(n, prefs):
    for p in prefs:
        if n % p == 0:
            return p
    return n


def _local_step(x, mem, target, w):
    t, d = x.shape
    heads = w["a_log"].shape[1]
    gw = heads * HEAD_DIM
    groups, cg, _ = w["pool_w"].shape
    pw = groups * cg
    n_main = 4 * gw + pw
    in_cols = n_main + 2 * heads
    s_in = w["w_in3"].shape[0]

    tm = _pick(t, (512, 256, 128))
    tm_ln = _pick(t, (256, 128))
    tk = _pick(d, K_STEPS)

    w_in = jnp.concatenate([w["w_in3"][s] for s in range(s_in)], axis=1)
    w_main = jnp.concatenate([w_in[:, :4 * gw], w_in[:, 4 * gw + 2 * heads:]], axis=1)
    w_ba = jnp.pad(w_in[:, 4 * gw:4 * gw + 2 * heads], ((0, 0), (0, HEAD_DIM - 2 * heads)))
    x_bf = x.astype(BF16)
    mem_bf = mem.astype(BF16)

    proj = _plain("proj_main", x_bf, w_main, tm=tm, tn=_pick(n_main, (1024, 512, 256, 128)), tk=tk, out_dtype=F32)
    ea, dtb = _gate_vectors(w["a_log"], w["dt_bias"], heads)
    vec128 = lambda i, j: (0, 0)
    ba, bg = _matmul(
        "proj_gates", x_bf, w_ba, tm=tm, tn=HEAD_DIM, tk=tk,
        extra=[(ea, (1, HEAD_DIM), vec128), (dtb, (1, HEAD_DIM), vec128)],
        outs=[(jax.ShapeDtypeStruct((t, HEAD_DIM), F32), (tm, HEAD_DIM), _tile)] * 2,
        epilogue=_gates_epilogue(heads))
    qkv = _gdn_pre(proj, w["conv_w"], heads)
    o_gdn, states = _gdn_core(qkv, bg, heads)
    cat_g = _gdn_post(o_gdn, proj, 3 * gw, w["gdn_norm_w"], heads, tm)
    cat_p = _pool_forward(proj, 4 * gw, w["pool_w"], w["pool_scale"])
    cat = jnp.concatenate([cat_g, cat_p], axis=1)
    w = {**w, **(yield ("weights", 1, cat))}
    h1, h1_bf, xhat1, rstd1 = _ln_forward("mix_ln1", cat, w["w_out"], x, w["ln1_g"], w["ln1_b"], tm=tm_ln, tk=tk)

    tn_d = _pick(d, (1024, 512, 256, 128))
    q = _plain("xattn_q", h1_bf, w["xq_w"], tm=tm, tn=tn_d, tk=tk, out_dtype=BF16)
    mlen = mem.shape[0]
    tm_mem = _pick(mlen, (256, 128))
    k = _plain("xattn_k", mem_bf, w["xk_w"], tm=tm_mem, tn=tn_d, tk=tk, out_dtype=BF16)
    v = _plain("xattn_v", mem_bf, w["xv_w"], tm=tm_mem, tn=tn_d, tk=tk, out_dtype=BF16)
    att = _attention(q, k, v, tm)
    h2, h2_bf, xhat2, rstd2 = _ln_forward("xo_ln2", att, w["xo_w"], h1, w["ln2_g"], w["ln2_b"], tm=tm_ln, tk=tk)

    w = {**w, **(yield ("weights", 2, h2_bf))}
    s_up = w["w_up3"].shape[0]
    ff = s_up * w["w_up3"].shape[2]
    tn_f = _pick(ff // s_up, (1024, 512, 256, 128))

    def up_epi(acc, ex, out, i):
        r = jnp.maximum(acc, 0.0)
        out[0][...] = (r * r).astype(BF16)
        out[1][...] = (2.0 * r).astype(BF16)

    act, act_grad = _matmul(
        "mlp_up", h2_bf, w["w_up3"], b_blocks=s_up, tm=tm, tn=tn_f, tk=tk,
        outs=[(jax.ShapeDtypeStruct((t, ff), BF16), (tm, tn_f), _tile)] * 2, epilogue=up_epi)
    tk_f = _pick(ff, K_STEPS)
    _, _, xhat3, rstd3 = _ln_forward("down_ln3", act, w["w_down"], h2, w["ln3_g"], w["ln3_b"], tm=tm_ln, tk=tk_f)

    grads = {}
    du3, du3_bf, grads["ln3_g"], grads["ln3_b"], loss = _loss_and_ln_backward(
        xhat3, rstd3, w["ln3_g"], w["ln3_b"], target, tm_ln)

    def dup_epi(acc, ex, out, i):
        out[0][...] = (acc * ex[0][...].astype(F32)).astype(BF16)

    dup = _matmul(
        "mlp_down_dx", du3_bf, w["w_down"], tb=True, tm=tm, tn=tn_f, tk=tk,
        extra=[(act_grad, (tm, tn_f), _tile)],
        outs=[(jax.ShapeDtypeStruct((t, ff), BF16), (tm, tn_f), _tile)], epilogue=dup_epi)[0]
    tk_t = _pick(t, K_STEPS)
    tm_w = _pick(d, (512, 256, 128))
    grads["w_down"] = _plain("mlp_down_dw", act, du3_bf, ta=True, tm=_pick(ff, (512, 256, 128)), tn=tn_d, tk=tk_t,
                             out_dtype=F32)
    grads["w_up3"] = _plain("mlp_up_dw", h2_bf, dup, ta=True, tm=tm_w, tn=tn_f, tk=tk_t, out_dtype=F32, out3=s_up)
    yield ("grads", 0, {n: grads.pop(n) for n in ("w_down", "w_up3")})
    du2, du2_bf, grads["ln2_g"], grads["ln2_b"] = _ln_backward(
        "mlp_up_dx_ln2", dup, w["w_up3"], du3, xhat2, rstd2, w["ln2_g"], tm=tm_ln,
        tk=_pick(ff // s_up, K_STEPS), b_blocks=s_up)

    grads["xo_w"] = _plain("xo_dw", att, du2_bf, ta=True, tm=tm_w, tn=tn_d, tk=tk_t, out_dtype=F32)
    datt = _plain("xo_dx", du2_bf, w["xo_w"], tb=True, tm=tm, tn=tn_d, tk=tk, out_dtype=BF16)
    dq, dk, dv = _attention_backward(q, k, v, datt, tm)
    tk_m = _pick(mlen, (256, 128))
    grads["xq_w"] = _plain("xq_dw", h1_bf, dq, ta=True, tm=tm_w, tn=tn_d, tk=tk_t, out_dtype=F32)
    grads["xk_w"] = _plain("xk_dw", mem_bf, dk, ta=True, tm=tm_w, tn=tn_d, tk=tk_m, out_dtype=F32)
    grads["xv_w"] = _plain("xv_dw", mem_bf, dv, ta=True, tm=tm_w, tn=tn_d, tk=tk_m, out_dtype=F32)
    du1, du1_bf, grads["ln1_g"], grads["ln1_b"] = _ln_backward(
        "xq_dx_ln1", dq, w["xq_w"], du2, xhat1, rstd1, w["ln1_g"], tm=tm_ln, tk=tk)

    grads["w_out"] = _plain("out_dw", cat, du1_bf, ta=True, tm=tm_w, tn=tn_d, tk=tk_t, out_dtype=F32)
    yield ("grads", 1, {n: grads.pop(n) for n in ("xo_w", "xq_w", "xk_w", "xv_w", "w_out")})
    dcat = _plain("out_dx", du1_bf, w["w_out"], tb=True, tm=tm, tn=tn_d, tk=tk, out_dtype=F32)
    dp, grads["pool_w"], grads["pool_scale"] = _pool_backward(dcat, gw, proj, 4 * gw, w["pool_w"], w["pool_scale"])
    do_gdn, dz, grads["gdn_norm_w"] = _gdn_post_backward(dcat, o_gdn, proj, 3 * gw, w["gdn_norm_w"], heads, tm)
    dqkv, dbg = _gdn_core_backward(qkv, bg, states, do_gdn, heads)
    dqkv_pre, grads["conv_w"] = _gdn_pre_backward(proj, w["conv_w"], dqkv, heads)
    dba, dalog_row, ddt_row = _gates_backward(ba, bg, dbg, ea, dtb, heads)
    grads["a_log"] = dalog_row[:, heads:2 * heads]
    grads["dt_bias"] = ddt_row[:, heads:2 * heads]

    dproj = jnp.concatenate([dqkv_pre, dz, dp], axis=1)
    tn_main = _pick(n_main, (1024, 512, 256, 128))
    dw_main = _plain("proj_dw", x_bf, dproj, ta=True, tm=tm_w, tn=tn_main, tk=tk_t, out_dtype=F32)
    dw_ba = _plain("proj_gates_dw", x_bf, dba, ta=True, tm=tm_w, tn=HEAD_DIM, tk=tk_t, out_dtype=F32)
    dw_in = jnp.concatenate([dw_main[:, :4 * gw], dw_ba[:, :2 * heads], dw_main[:, 4 * gw:]], axis=1)
    per = in_cols // s_in
    grads["w_in3"] = jnp.stack([dw_in[:, s * per:(s + 1) * per] for s in range(s_in)], axis=0)

    def dx_epi(acc, ex, out, i):
        out[0][...] = acc + ex[1][...] + ALPHA * ex[0][...]

    dx_gates = _plain("proj_gates_dx", dba, w_ba, tb=True, tm=tm, tn=tn_d, tk=HEAD_DIM, out_dtype=F32)
    grad_x = _matmul(
        "proj_dx", dproj, w_main, tb=True, tm=tm, tn=tn_d, tk=_pick(n_main, (2560,) + K_STEPS),
        extra=[(du1, (tm, tn_d), _tile), (dx_gates, (tm, tn_d), _tile)],
        outs=[(jax.ShapeDtypeStruct((t, d), F32), (tm, tn_d), _tile)], epilogue=dx_epi)[0]
    return loss, grad_x, grads


def _adamw(name, w, g, m, v):
    r, c = w.shape
    tr = _pick(r, (256, 128, 64, 32, 16, 8))
    c1 = 1.0 - ADAM_B1 ** ADAM_STEP
    c2 = 1.0 - ADAM_B2 ** ADAM_STEP

    def body(w_ref, g_ref, m_ref, v_ref, d_ref, mo_ref, vo_ref):
        gv = g_ref[...]
        mn = ADAM_B1 * m_ref[...] + (1.0 - ADAM_B1) * gv
        vn = ADAM_B2 * v_ref[...] + (1.0 - ADAM_B2) * (gv * gv)
        d_ref[...] = -ADAM_LR * ((mn / c1) / (jnp.sqrt(vn / c2) + ADAM_EPS) + ADAM_WD * w_ref[...])
        mo_ref[...] = mn
        vo_ref[...] = vn

    blk = pl.BlockSpec((tr, c), lambda i: (i, 0))
    return pl.pallas_call(
        body, name=name, grid=(r // tr,), in_specs=[blk] * 4, out_specs=[blk] * 3,
        out_shape=[jax.ShapeDtypeStruct((r, c), F32)] * 3,
        compiler_params=_params("parallel"),
    )(w, g, m, v)


def _place():
    x, y, c = lax.axis_index("x"), lax.axis_index("y"), lax.axis_index("c")
    chips = [(1 - x, y), (x, 1 - y), (1 - x, 1 - y)]
    return x, y, c, chips


HBM = pl.BlockSpec(memory_space=pltpu.HBM)


SEM = pl.BlockSpec(memory_space=pltpu.SEMAPHORE)
ANY = pl.BlockSpec(memory_space=pl.ANY)
EFFECT = pltpu.SideEffectType.DATAFLOW_SIDE_EFFECTING


def _in_hbm(a):
    return pltpu.with_memory_space_constraint(a, pltpu.HBM)


def _remote(src, dst, send_sem, recv_sem, to):
    return pltpu.make_async_remote_copy(src_ref=src, dst_ref=dst, send_sem=send_sem, recv_sem=recv_sem,
                                        device_id=to, device_id_type=MESH)


def _landed(lands, i, shard_index, which):
    rows = lands[i].shape[1] // 2
    return lands[i].at[shard_index, pl.ds(which * rows, rows)]


def _gather_start(shards):
    n = len(shards)
    lands = [lax.empty((N_SHARD,) + s.shape, s.dtype) for s in shards]

    def body(*refs):
        ins, zones = refs[:n], refs[n:2 * n]
        ici_send, ici_recv, own_send, own_recv = refs[2 * n:2 * n + 4]
        token = refs[-1]
        x, y, c, chips = _place()
        me = 2 * x + y
        for i in range(n):
            rows = ins[i].shape[0] // 2
            for j, chip in enumerate(chips):
                _remote(ins[i].at[pl.ds(c * rows, rows)], _landed(zones, i, me, c), ici_send.at[3 * i + j],
                        ici_recv.at[3 * i + j], (*chip, c)).start()
        for i in range(n):
            _remote(ins[i], zones[i].at[me], own_send.at[i], own_recv.at[i], (x, y, 1 - c)).start()
        token[...] = jnp.zeros_like(token)

    dma = pltpu.SemaphoreType.DMA
    outs = pl.pallas_call(
        body, name="gather_start",
        in_specs=[HBM] * (2 * n),
        out_shape=(dma((3 * n,)), dma((3 * n,)), dma((n,)), dma((n,)),
                   *[pltpu.HBM(a.shape, a.dtype) for a in shards + lands], jax.ShapeDtypeStruct((8, LANES), F32)),
        out_specs=(SEM, SEM, SEM, SEM, *[HBM] * (2 * n), pl.BlockSpec(memory_space=pltpu.VMEM)),
        input_output_aliases={k: 4 + k for k in range(2 * n)},
        compiler_params=pltpu.CompilerParams(has_side_effects=EFFECT),
    )(*[_in_hbm(a) for a in shards + lands])
    sems = dict(zip(("ici_send", "ici_recv", "own_send", "own_recv"), outs[:4]))
    return sems, list(outs[4:4 + n]), list(outs[4 + n:4 + 2 * n]), outs[-1]


def _gather_forward(name, idx, lands, sems, after):
    n = len(idx)

    def body(*refs):
        zones = refs[:n]
        ici_recv = refs[n]
        fwd_send, fwd_recv = refs[n + 2], refs[n + 3]
        x, y, c, chips = _place()
        for k, i in enumerate(idx):
            for j, chip in enumerate(chips):
                half = _landed(zones, k, 2 * chip[0] + chip[1], c)
                _remote(half, half, fwd_send.at[3 * k + j], ici_recv.at[3 * i + j], (*chip, c)).wait_recv()
                _remote(half, half, fwd_send.at[3 * k + j], fwd_recv.at[3 * k + j], (x, y, 1 - c)).start()

    dma = pltpu.SemaphoreType.DMA
    outs = pl.pallas_call(
        body, name=name,
        in_specs=[HBM] * n + [SEM, ANY],
        out_shape=(dma((3 * n,)), dma((3 * n,)), *[pltpu.HBM(a.shape, a.dtype) for a in lands]),
        out_specs=(SEM, SEM, *[HBM] * n),
        input_output_aliases={k: 2 + k for k in range(n)},
        compiler_params=pltpu.CompilerParams(has_side_effects=EFFECT),
    )(*lands, sems["ici_recv"], after)
    return (outs[0], outs[1]), list(outs[2:])


def _gather_wait(name, idx, shards, lands, sems, fwd):
    n = len(idx)

    def body(*refs):
        ins, zones = refs[:n], refs[n:2 * n]
        ici_send, own_send, own_recv, fwd_send, fwd_recv = refs[2 * n:2 * n + 5]
        x, y, c, chips = _place()
        me = 2 * x + y
        for k, i in enumerate(idx):
            rows = ins[k].shape[0] // 2
            mine = ins[k].at[pl.ds(c * rows, rows)]
            for j, chip in enumerate(chips):
                theirs = 2 * chip[0] + chip[1]
                _remote(mine, _landed(zones, k, me, c), ici_send.at[3 * i + j], fwd_recv.at[3 * k + j],
                        (*chip, c)).wait_send()
                sent = _landed(zones, k, theirs, c)
                _remote(sent, sent, fwd_send.at[3 * k + j], fwd_recv.at[3 * k + j], (x, y, 1 - c)).wait_send()
                passed = _landed(zones, k, theirs, 1 - c)
                _remote(passed, passed, fwd_send.at[3 * k + j], fwd_recv.at[3 * k + j], (x, y, 1 - c)).wait_recv()
            own = _remote(ins[k], zones[k].at[me], own_send.at[i], own_recv.at[i], (x, y, 1 - c))
            own.wait_send()
            own.wait_recv()

    outs = pl.pallas_call(
        body, name=name,
        in_specs=[HBM] * (2 * n) + [SEM] * 5,
        out_shape=tuple(pltpu.HBM(a.shape, a.dtype) for a in lands),
        out_specs=tuple([HBM] * n),
        input_output_aliases={n + k: k for k in range(n)},
        compiler_params=pltpu.CompilerParams(has_side_effects=EFFECT),
    )(*shards, *lands, sems["ici_send"], sems["own_send"], sems["own_recv"], fwd[0], fwd[1])
    return list(outs)


def _all_reduce_small(name, slab):
    r, width = slab.shape
    ndev = 8

    def body(x_ref, out_ref, buf, send_sems, recv_sems):
        x, y, c, _ = _place()
        me = 4 * x + 2 * y + c
        buf[me] = x_ref[...]
        copies = []
        for k in range(1, ndev):
            peer = jnp.bitwise_xor(me, k)
            to = (peer // 4, (peer // 2) % 2, peer % 2)
            cp = pltpu.make_async_remote_copy(src_ref=x_ref, dst_ref=buf.at[me], send_sem=send_sems.at[k - 1],
                                              recv_sem=recv_sems.at[k - 1], device_id=to, device_id_type=MESH)
            cp.start()
            copies.append(cp)
        for k in range(1, ndev):
            peer = jnp.bitwise_xor(me, k)
            pltpu.make_async_remote_copy(src_ref=x_ref, dst_ref=buf.at[peer], send_sem=send_sems.at[k - 1],
                                         recv_sem=recv_sems.at[k - 1], device_id=(x, y, c),
                                         device_id_type=MESH).wait_recv()
        for cp in copies:
            cp.wait_send()
        total = buf[0]
        for d in range(1, ndev):
            total = total + buf[d]
        out_ref[...] = total

    return pl.pallas_call(
        body, name=name,
        in_specs=[pl.BlockSpec(memory_space=pltpu.VMEM)], out_specs=pl.BlockSpec(memory_space=pltpu.VMEM),
        out_shape=jax.ShapeDtypeStruct((r, width), F32),
        scratch_shapes=[pltpu.VMEM((ndev, r, width), F32), pltpu.SemaphoreType.DMA((ndev - 1,)),
                        pltpu.SemaphoreType.DMA((ndev - 1,))],
        compiler_params=pltpu.CompilerParams(vmem_limit_bytes=VMEM_LIMIT),
    )(slab)


def _swap_halves(name, grads):
    n = len(grads)

    def body(*refs):
        ins, outs = refs[:n], refs[n:2 * n]
        send_sems, recv_sems = refs[2 * n:]
        x, y, c, _ = _place()
        copies = []
        for i in range(n):
            rows = ins[i].shape[1] // 2
            for s in range(N_SHARD):
                cp = pltpu.make_async_remote_copy(
                    src_ref=ins[i].at[s, pl.ds((1 - c) * rows, rows)], dst_ref=outs[i].at[s],
                    send_sem=send_sems.at[N_SHARD * i + s], recv_sem=recv_sems.at[N_SHARD * i + s],
                    device_id=(x, y, 1 - c), device_id_type=MESH)
                cp.start()
                copies.append(cp)
        for cp in copies:
            cp.wait()

    return pl.pallas_call(
        body, name=name,
        in_specs=[HBM] * n, out_specs=[HBM] * n,
        out_shape=[jax.ShapeDtypeStruct((N_SHARD, g.shape[1] // 2, g.shape[2]), F32) for g in grads],
        scratch_shapes=[pltpu.SemaphoreType.DMA((N_SHARD * n,)), pltpu.SemaphoreType.DMA((N_SHARD * n,))],
    )(*grads)


def _chip_partial(name, grad, other, core):
    s, r, cdim = grad.shape
    rows = r // 2
    tr = _pick(rows, (256, 128, 64, 32, 16))
    nb = rows // tr

    def body(core_ref, g_ref, o_ref, out_ref):
        out_ref[...] = (g_ref[...] + o_ref[...]).astype(BF16)

    return pl.pallas_call(
        body, name=name,
        grid_spec=pltpu.PrefetchScalarGridSpec(
            num_scalar_prefetch=1, grid=(s, nb),
            in_specs=[pl.BlockSpec((None, tr, cdim), lambda j, b, core_ref: (j, core_ref[0] * nb + b, 0)),
                      pl.BlockSpec((None, tr, cdim), lambda j, b, core_ref: (j, b, 0))],
            out_specs=pl.BlockSpec((None, tr, cdim), lambda j, b, core_ref: (j, b, 0))),
        out_shape=jax.ShapeDtypeStruct((s, rows, cdim), BF16),
        compiler_params=_params("parallel", "parallel"),
    )(core, grad, other)


def _partial_copies(ins, zones, send_sems, recv_sems):
    x, y, c, chips = _place()
    return [_remote(ins[i].at[2 * chip[0] + chip[1]], zones[i].at[j], send_sems.at[3 * i + j],
                    recv_sems.at[3 * i + j], (*chip, c))
            for i in range(len(ins)) for j, chip in enumerate(chips)]


def _send_partials_start(name, partials):
    n = len(partials)
    lands = [lax.empty((3,) + p.shape[1:], BF16) for p in partials]

    def body(*refs):
        for cp in _partial_copies(refs[:n], refs[n:2 * n], refs[2 * n], refs[2 * n + 1]):
            cp.start()

    dma = pltpu.SemaphoreType.DMA
    outs = pl.pallas_call(
        body, name=name,
        in_specs=[HBM] * (2 * n),
        out_shape=(dma((3 * n,)), dma((3 * n,)), *[pltpu.HBM(a.shape, a.dtype) for a in partials + lands]),
        out_specs=(SEM, SEM, *[HBM] * (2 * n)),
        input_output_aliases={k: 2 + k for k in range(2 * n)},
        compiler_params=pltpu.CompilerParams(has_side_effects=EFFECT),
    )(*[_in_hbm(a) for a in partials + lands])
    return (outs[0], outs[1]), list(outs[2:2 + n]), list(outs[2 + n:])


def _send_partials_wait(name, started, after):
    sems, partials, lands = started
    n = len(partials)

    def body(*refs):
        for cp in _partial_copies(refs[:n], refs[n:2 * n], refs[2 * n], refs[2 * n + 1]):
            cp.wait_send()
            cp.wait_recv()

    outs = pl.pallas_call(
        body, name=name,
        in_specs=[HBM] * (2 * n) + [SEM, SEM, ANY],
        out_shape=tuple(pltpu.HBM(a.shape, a.dtype) for a in lands),
        out_specs=tuple([HBM] * n),
        input_output_aliases={n + k: k for k in range(n)},
        compiler_params=pltpu.CompilerParams(has_side_effects=EFFECT),
    )(*partials, *lands, sems[0], sems[1], after)
    return list(outs)


def _reduce_own(name, grad, other, received, where):
    s, r, cdim = grad.shape
    rows = r // 2
    tr = _pick(rows, (256, 128, 64, 32, 16))
    nb = rows // tr

    def body(where_ref, g_ref, o_ref, r_ref, out_ref):
        total = g_ref[...] + o_ref[...]
        for j in range(3):
            total = total + r_ref[j].astype(F32)
        out_ref[...] = total

    return pl.pallas_call(
        body, name=name,
        grid_spec=pltpu.PrefetchScalarGridSpec(
            num_scalar_prefetch=1, grid=(nb,),
            in_specs=[pl.BlockSpec((None, tr, cdim), lambda b, w_ref: (w_ref[0], w_ref[1] * nb + b, 0)),
                      pl.BlockSpec((None, tr, cdim), lambda b, w_ref: (w_ref[0], b, 0)),
                      pl.BlockSpec((3, tr, cdim), lambda b, w_ref: (0, b, 0))],
            out_specs=pl.BlockSpec((tr, cdim), lambda b, w_ref: (w_ref[1] * nb + b, 0))),
        out_shape=jax.ShapeDtypeStruct((r, cdim), F32),
        compiler_params=_params("parallel"),
    )(where, grad, other, received)


def _join_halves(name, halves):
    n = len(halves)

    def body(*refs):
        bufs = refs[n:2 * n]
        send_sems, recv_sems = refs[2 * n:]
        x, y, c, _ = _place()
        copies = []
        for i in range(n):
            rows = bufs[i].shape[0] // 2
            mine = bufs[i].at[pl.ds(c * rows, rows)]
            cp = pltpu.make_async_remote_copy(src_ref=mine, dst_ref=mine, send_sem=send_sems.at[i],
                                              recv_sem=recv_sems.at[i], device_id=(x, y, 1 - c), device_id_type=MESH)
            cp.start()
            copies.append(cp)
        for i, cp in enumerate(copies):
            rows = bufs[i].shape[0] // 2
            theirs = bufs[i].at[pl.ds((1 - c) * rows, rows)]
            pltpu.make_async_remote_copy(src_ref=theirs, dst_ref=theirs, send_sem=send_sems.at[i],
                                         recv_sem=recv_sems.at[i], device_id=(x, y, 1 - c),
                                         device_id_type=MESH).wait_recv()
            cp.wait_send()

    return pl.pallas_call(
        body, name=name,
        in_specs=[HBM] * n, out_specs=[HBM] * n,
        out_shape=[jax.ShapeDtypeStruct(h.shape, F32) for h in halves],
        input_output_aliases={i: i for i in range(n)},
        scratch_shapes=[pltpu.SemaphoreType.DMA((n,)), pltpu.SemaphoreType.DMA((n,))],
    )(*halves)


BIG = ("w_in", "pool_w", "w_out", "xq_w", "xk_w", "xv_w", "xo_w", "w_up", "w_down")
GATHER_GROUPS = ((0, 1), (2, 3, 4, 5, 6), (7, 8))
SMALL = ("conv_w", "a_log", "dt_bias", "gdn_norm_w", "pool_scale", "ln1_g", "ln1_b", "ln2_g", "ln2_b", "ln3_g", "ln3_b")
ORDER = ("w_in", "conv_w", "a_log", "dt_bias", "gdn_norm_w", "pool_w", "pool_scale", "w_out", "ln1_g", "ln1_b",
         "xq_w", "xk_w", "xv_w", "xo_w", "ln2_g", "ln2_b", "w_up", "w_down", "ln3_g", "ln3_b")
LANES = 128


def _rows(flat_len):
    return -(-flat_len // LANES)


def _pack(pieces):
    out = []
    for p in pieces:
        flat = p.reshape(-1).astype(F32)
        out.append(jnp.pad(flat, (0, _rows(flat.shape[0]) * LANES - flat.shape[0])).reshape(-1, LANES))
    slab = jnp.concatenate(out, axis=0)
    return jnp.pad(slab, ((0, -slab.shape[0] % 8), (0, 0)))


def _unpack(slab, shapes):
    out, row = [], 0
    for shp in shapes:
        size = math.prod(shp)
        out.append(slab[row:row + _rows(size)].reshape(-1)[:size].reshape(shp))
        row += _rows(size)
    return out


def _as2d(a):
    a = a[0]
    return a.reshape(-1, a.shape[-1]) if a.ndim == 3 else a


def kernel(x, mem, w_in, conv_w, a_log, dt_bias, gdn_norm_w, pool_w, pool_scale, w_out, ln1_g, ln1_b, xq_w, xk_w, xv_w, xo_w, ln2_g, ln2_b, w_up, w_down, ln3_g, ln3_b, loss_target, m_w_in, m_conv_w, m_a_log, m_dt_bias, m_gdn_norm_w, m_pool_w, m_pool_scale, m_w_out, m_ln1_g, m_ln1_b, m_xq_w, m_xk_w, m_xv_w, m_xo_w, m_ln2_g, m_ln2_b, m_w_up, m_w_down, m_ln3_g, m_ln3_b, v_w_in, v_conv_w, v_a_log, v_dt_bias, v_gdn_norm_w, v_pool_w, v_pool_scale, v_w_out, v_ln1_g, v_ln1_b, v_xq_w, v_xk_w, v_xv_w, v_xo_w, v_ln2_g, v_ln2_b, v_w_up, v_w_down, v_ln3_g, v_ln3_b):
    given = dict(locals())
    cx, cy, cc = lax.axis_index("x"), lax.axis_index("y"), lax.axis_index("c")
    me = 2 * cx + cy
    groups = pool_w.shape[1]
    cs = pool_w.shape[2]
    kk, conv_cols = conv_w.shape[1], conv_w.shape[2]
    core = cc.astype(jnp.int32).reshape(1)
    where = jnp.stack([me, cc]).astype(jnp.int32)

    sems, shards, lands, token = _gather_start([_as2d(given[n]).astype(BF16) for n in BIG])

    def fetch(group, after):
        idx = GATHER_GROUPS[group]
        fwd, zones = _gather_forward(f"gather_forward_{group}", idx, [lands[i] for i in idx], sems, after)
        full = dict(zip([BIG[i] for i in idx],
                        _gather_wait(f"gather_wait_{group}", idx, [shards[i] for i in idx], zones, sems, fwd)))
        out = {}
        for n, a in full.items():
            if n == "w_in":
                out["w_in3"] = a
            elif n == "w_up":
                out["w_up3"] = a
            elif n == "pool_w":
                out[n] = a.reshape(N_SHARD, groups, cs, -1).transpose(1, 0, 2, 3).reshape(groups, N_SHARD * cs, -1)
            else:
                out[n] = a.reshape(-1, a.shape[-1])
        return out

    conv_slab = jnp.zeros((kk, N_SHARD * conv_cols), F32)
    conv_slab = lax.dynamic_update_slice(conv_slab, conv_w[0] * (cc == 0).astype(F32), (0, me * conv_cols))
    wts = {"conv_w": _unpack(_all_reduce_small("gather_conv_w", _pack([conv_slab])), [conv_slab.shape])[0]}
    for n in ("a_log", "dt_bias", "gdn_norm_w", "pool_scale", "ln1_g", "ln1_b", "ln2_g", "ln2_b", "ln3_g", "ln3_b"):
        wts[n] = given[n]
    wts.update(fetch(0, token))

    def start_reduce(group, grads):
        names, blocks = [], []
        for n, g in grads.items():
            if n == "pool_w":
                g = g.reshape(groups, N_SHARD, cs, -1).transpose(1, 0, 2, 3).reshape(N_SHARD, groups * cs, -1)
            elif g.ndim == 2:
                g = g.reshape(N_SHARD, -1, g.shape[-1])
            names.append({"w_in3": "w_in", "w_up3": "w_up"}.get(n, n))
            blocks.append(g)
        others = _swap_halves(f"grad_swap_{group}", blocks)
        partials = [_chip_partial("chip_partial_" + n, gb, ob, core) for n, gb, ob in zip(names, blocks, others)]
        return group, names, blocks, others, _send_partials_start(f"grad_send_start_{group}", partials)

    grad, delta, new_m, new_v = {}, {}, {}, {}

    def finish_reduce(state, after):
        group, names, blocks, others, started = state
        received = _send_partials_wait(f"grad_send_wait_{group}", started, after)
        halves = [_reduce_own("reduce_own_" + n, gb, ob, rb, where)
                  for n, gb, ob, rb in zip(names, blocks, others, received)]
        for n, g in zip(names, _join_halves(f"grad_join_{group}", halves)):
            shp = given[n].shape
            d2, m2, v2 = _adamw("adamw_" + n, _as2d(given[n]), g, _as2d(given["m_" + n]), _as2d(given["v_" + n]))
            grad[n], delta[n], new_m[n], new_v[n] = (a.reshape(shp) for a in (g, d2, m2, v2))
        return d2

    step = _local_step(x[0], mem[0], loss_target[0], wts)
    pending = []
    request = next(step)
    while True:
        try:
            if request[0] == "weights":
                request = step.send(fetch(request[1], request[2]))
            else:
                pending.append(start_reduce(request[1], request[2]))
                request = next(step)
        except StopIteration as stop:
            loss_row, grad_x, g = stop.value
            break
    pending.append(start_reduce(2, {n: g[n] for n in ("w_in3", "pool_w")}))

    small_names = ("a_log", "dt_bias", "gdn_norm_w", "pool_scale", "ln1_g", "ln1_b", "ln2_g", "ln2_b", "ln3_g", "ln3_b")
    pieces = [g["conv_w"]] + [g[n] for n in small_names] + [loss_row[:, :1]]
    shapes = [p.shape for p in pieces]
    summed = _unpack(_all_reduce_small("all_reduce_small", _pack(pieces)), shapes)
    gsmall = dict(zip(small_names, summed[1:-1]))
    gsmall["conv_w"] = lax.dynamic_slice(summed[0], (0, me * conv_cols), (kk, conv_cols))
    loss = summed[-1][0, 0]

    after = pending[-1][4][1][0]
    for state in pending:
        after = finish_reduce(state, after)

    sshapes = [given[n].shape for n in SMALL]
    slabs = [_pack([given[p + n] for n in SMALL]) for p in ("", "m_", "v_")]
    gslab = _pack([gsmall[n] for n in SMALL])
    outs = _adamw("adamw_small", slabs[0], gslab, slabs[1], slabs[2])
    for dst, slab in zip((delta, new_m, new_v), outs):
        dst.update(zip(SMALL, _unpack(slab, sshapes)))
    for n in SMALL:
        grad[n] = gsmall[n].reshape(given[n].shape)

    return (loss, grad_x[None], *[grad[n] for n in ORDER], *[delta[n] for n in ORDER],
            *[new_m[n] for n in ORDER], *[new_v[n] for n in ORDER])
```

```python
import functools
import math

import jax
import jax.numpy as jnp
from jax import lax
from jax.experimental import pallas as pl
from jax.experimental.pallas import tpu as pltpu

F32 = jnp.float32
BF16 = jnp.bfloat16
MESH = pl.DeviceIdType.MESH

HEAD_DIM = 128
CHUNK = 64
POOL_WINDOWS = (2, 4, 8, 16)
XATTN_HEADS = 4
ALPHA = 2.0 ** 0.25
LN_EPS = 1e-5
NORM_EPS = 1e-6
ADAM_LR, ADAM_B1, ADAM_B2, ADAM_EPS, ADAM_WD, ADAM_STEP = 0.001, 0.9, 0.999, 1e-08, 0.01, 10
N_SHARD = 4
VMEM_LIMIT = 56 * 1024 * 1024
K_STEPS = (2048, 1024, 512, 256, 128)


def _params(*sem):
    return pltpu.CompilerParams(dimension_semantics=sem, vmem_limit_bytes=VMEM_LIMIT)


def _bdot(a, b, ta=False, tb=False):
    dims = (((0 if ta else 1,), (1 if tb else 0,)), ((), ()))
    return lax.dot_general(a.astype(BF16), b.astype(BF16), dims, preferred_element_type=F32)


def _sigmoid(x):
    return 1.0 / (1.0 + jnp.exp(-x))


def _matmul(name, a, b, *, ta=False, tb=False, tm, tn, tk, extra=(), outs, epilogue, b_blocks=None,
            sequential=False):
    m, k_dim = (a.shape[1], a.shape[0]) if ta else a.shape
    if b_blocks and tb:
        n = b.shape[1]
        k_dim = b.shape[0] * b.shape[2]
        per = b.shape[2] // tk
        b_spec = pl.BlockSpec((None, tn, tk), lambda i, j, k: (k // per, j, k % per))
    elif b_blocks:
        n = b.shape[0] * b.shape[2]
        per = b.shape[2] // tn
        b_spec = pl.BlockSpec((None, tk, tn), lambda i, j, k: (j // per, k, j % per))
    elif tb:
        n = b.shape[0]
        b_spec = pl.BlockSpec((tn, tk), lambda i, j, k: (j, k))
    else:
        n = b.shape[1]
        b_spec = pl.BlockSpec((tk, tn), lambda i, j, k: (k, j))
    assert m % tm == 0 and n % tn == 0 and k_dim % tk == 0, (name, m, n, k_dim, tm, tn, tk)
    nk = k_dim // tk
    a_spec = pl.BlockSpec((tk, tm), lambda i, j, k: (k, i)) if ta else pl.BlockSpec((tm, tk), lambda i, j, k: (i, k))
    n_extra, n_out = len(extra), len(outs)

    def wrap(index_map):
        return lambda i, j, k: index_map(i, j)

    def body_one_step(*refs):
        ex = refs[2:2 + n_extra]
        out = refs[2 + n_extra:2 + n_extra + n_out]
        epilogue(_bdot(refs[0][...], refs[1][...], ta, tb), ex, out, pl.program_id(0))

    def body(*refs):
        a_ref, b_ref = refs[0], refs[1]
        ex = refs[2:2 + n_extra]
        out = refs[2 + n_extra:2 + n_extra + n_out]
        acc = refs[-1]
        i, k = pl.program_id(0), pl.program_id(2)
        part = _bdot(a_ref[...], b_ref[...], ta, tb)

        @pl.when(k == 0)
        def _():
            acc[...] = part

        @pl.when(jnp.logical_and(k > 0, k < nk - 1))
        def _():
            acc[...] += part

        @pl.when(k == nk - 1)
        def _():
            epilogue(acc[...] + part, ex, out, i)

    sem = ("arbitrary",) * 3 if sequential else ("parallel", "parallel", "arbitrary")
    res = pl.pallas_call(
        body_one_step if nk == 1 else body, name=name, grid=(m // tm, n // tn, nk),
        in_specs=[a_spec, b_spec] + [pl.BlockSpec(bs, wrap(im)) for _, bs, im in extra],
        out_specs=[pl.BlockSpec(bs, wrap(im)) for _, bs, im in outs],
        out_shape=[s for s, _, _ in outs],
        scratch_shapes=[] if nk == 1 else [pltpu.VMEM((tm, tn), F32)],
        compiler_params=_params(*sem),
    )(a, b, *[x for x, _, _ in extra])
    return res


def _tile(i, j):
    return (i, j)


def _plain(name, a, b, *, ta=False, tb=False, tm, tn, tk, out_dtype, b_blocks=None, out3=None):
    m = a.shape[1] if ta else a.shape[0]
    n = (b.shape[0] * b.shape[2]) if b_blocks else (b.shape[0] if tb else b.shape[1])

    def epi(acc, ex, out, i):
        out[0][...] = acc.astype(out_dtype)

    if out3:
        per = (n // out3) // tn
        spec = (jax.ShapeDtypeStruct((out3, m, n // out3), out_dtype), (None, tm, tn),
                lambda i, j: (j // per, i, j % per))
    else:
        spec = (jax.ShapeDtypeStruct((m, n), out_dtype), (tm, tn), _tile)
    return _matmul(name, a, b, ta=ta, tb=tb, tm=tm, tn=tn, tk=tk, outs=[spec], epilogue=epi,
                   b_blocks=b_blocks)[0]


def _ln_forward(name, a, b, res, gamma, beta, *, tm, tk, want_bf16=True):
    m, n = res.shape

    def epi(acc, ex, out, i):
        u = ALPHA * ex[0][...] + acc
        mu = jnp.mean(u, axis=-1, keepdims=True)
        xc = u - mu
        var = jnp.mean(xc * xc, axis=-1, keepdims=True)
        rstd = lax.rsqrt(var + LN_EPS)
        xhat = xc * rstd
        h = xhat * ex[1][...] + ex[2][...]
        out[0][...] = h
        out[1][...] = h.astype(BF16)
        out[2][...] = xhat
        out[3][...] = rstd

    row = lambda i, j: (i, 0)
    vec = lambda i, j: (0, 0)
    return _matmul(
        name, a, b, tm=tm, tn=n, tk=tk,
        extra=[(res, (tm, n), row), (gamma, (1, n), vec), (beta, (1, n), vec)],
        outs=[(jax.ShapeDtypeStruct((m, n), F32), (tm, n), row),
              (jax.ShapeDtypeStruct((m, n), BF16), (tm, n), row),
              (jax.ShapeDtypeStruct((m, n), F32), (tm, n), row),
              (jax.ShapeDtypeStruct((m, 1), F32), (tm, 1), row)],
        epilogue=epi)


def _ln_backward_math(dy, xhat, rstd, gamma):
    dxhat = dy * gamma
    m1 = jnp.mean(dxhat, axis=-1, keepdims=True)
    m2 = jnp.mean(dxhat * xhat, axis=-1, keepdims=True)
    du = rstd * (dxhat - m1 - xhat * m2)
    return du, jnp.sum(dy * xhat, axis=0, keepdims=True), jnp.sum(dy, axis=0, keepdims=True)


def _ln_backward(name, a, b, dres, xhat, rstd, gamma, *, tm, tk, b_blocks=None, tb=True):
    m, n = dres.shape

    def epi(acc, ex, out, i):
        dy = acc + ALPHA * ex[0][...]
        du, dg, db = _ln_backward_math(dy, ex[1][...], ex[2][...], ex[3][...])
        out[0][...] = du
        out[1][...] = du.astype(BF16)
        first = i == 0

        @pl.when(first)
        def _():
            out[2][...] = dg
            out[3][...] = db

        @pl.when(jnp.logical_not(first))
        def _():
            out[2][...] += dg
            out[3][...] += db

    row = lambda i, j: (i, 0)
    vec = lambda i, j: (0, 0)
    return _matmul(
        name, a, b, tb=tb, tm=tm, tn=n, tk=tk, b_blocks=b_blocks, sequential=True,
        extra=[(dres, (tm, n), row), (xhat, (tm, n), row), (rstd, (tm, 1), row), (gamma, (1, n), vec)],
        outs=[(jax.ShapeDtypeStruct((m, n), F32), (tm, n), row),
              (jax.ShapeDtypeStruct((m, n), BF16), (tm, n), row),
              (jax.ShapeDtypeStruct((1, n), F32), (1, n), vec),
              (jax.ShapeDtypeStruct((1, n), F32), (1, n), vec)],
        epilogue=epi)


def _shift_down(x, k):
    row = lax.broadcasted_iota(jnp.int32, x.shape, 0)
    return jnp.where(row >= k, pltpu.roll(x, k, axis=0), 0.0)


def _shift_up(x, k):
    t = x.shape[0]
    row = lax.broadcasted_iota(jnp.int32, x.shape, 0)
    return jnp.where(row < t - k, pltpu.roll(x, t - k, axis=0), 0.0)


def _conv_silu_norm(x, w, normalise):
    kk = w.shape[0]
    c = x * w[kk - 1:kk, :]
    for j in range(kk - 1):
        c = c + _shift_down(x, kk - 1 - j) * w[j:j + 1, :]
    sg = _sigmoid(c)
    s = c * sg
    r = lax.rsqrt(jnp.sum(s * s, axis=-1, keepdims=True) + NORM_EPS)
    y = jnp.where(normalise, s * r, s)
    return c, sg, s, r, y


def _gdn_pre(proj, conv_w, heads):
    t = proj.shape[0]
    kk = conv_w.shape[0]

    def body(x_ref, w_ref, o_ref):
        normalise = pl.program_id(0) < 2
        o_ref[...] = _conv_silu_norm(x_ref[...], w_ref[...], normalise)[4]

    col = lambda s, h: (0, s * heads + h)
    return pl.pallas_call(
        body, name="gdn_pre", grid=(3, heads),
        in_specs=[pl.BlockSpec((t, HEAD_DIM), col), pl.BlockSpec((kk, HEAD_DIM), col)],
        out_specs=pl.BlockSpec((t, HEAD_DIM), col),
        out_shape=jax.ShapeDtypeStruct((t, 3 * heads * HEAD_DIM), F32),
        compiler_params=_params("parallel", "parallel"),
    )(proj, conv_w)


def _gdn_pre_backward(proj, conv_w, dqkv, heads):
    t = proj.shape[0]
    kk = conv_w.shape[0]

    def body(x_ref, w_ref, dy_ref, dx_ref, dw_ref):
        normalise = pl.program_id(0) < 2
        x = x_ref[...]
        w = w_ref[...]
        dy = dy_ref[...]
        c, sg, s, r, y = _conv_silu_norm(x, w, normalise)
        ds_norm = r * (dy - y * jnp.sum(dy * y, axis=-1, keepdims=True))
        ds = jnp.where(normalise, ds_norm, dy)
        dc = ds * (sg * (1.0 + c * (1.0 - sg)))
        dx = dc * w[kk - 1:kk, :]
        rows = [None] * kk
        rows[kk - 1] = jnp.sum(dc * x, axis=0, keepdims=True)
        for j in range(kk - 1):
            lag = kk - 1 - j
            dx = dx + _shift_up(dc, lag) * w[j:j + 1, :]
            rows[j] = jnp.sum(dc * _shift_down(x, lag), axis=0, keepdims=True)
        dx_ref[...] = dx.astype(BF16)
        dw_ref[...] = jnp.concatenate(rows, axis=0)

    col = lambda s, h: (0, s * heads + h)
    return pl.pallas_call(
        body, name="gdn_pre_bwd", grid=(3, heads),
        in_specs=[pl.BlockSpec((t, HEAD_DIM), col), pl.BlockSpec((kk, HEAD_DIM), col),
                  pl.BlockSpec((t, HEAD_DIM), col)],
        out_specs=[pl.BlockSpec((t, HEAD_DIM), col), pl.BlockSpec((kk, HEAD_DIM), col)],
        out_shape=[jax.ShapeDtypeStruct((t, 3 * heads * HEAD_DIM), BF16),
                   jax.ShapeDtypeStruct((kk, 3 * heads * HEAD_DIM), F32)],
        compiler_params=_params("parallel", "parallel"),
    )(proj, conv_w, dqkv)


def _gate_vectors(a_log, dt_bias, heads):
    pad = lambda v: jnp.pad(v.astype(F32), ((0, 0), (heads, HEAD_DIM - 2 * heads)))
    return pad(jnp.exp(a_log.astype(F32))), pad(dt_bias)


def _softplus(x):
    return jnp.maximum(x, 0.0) + jnp.log(1.0 + jnp.exp(-jnp.abs(x)))


def _gates_epilogue(heads):
    def epi(acc, ex, out, i):
        lane = lax.broadcasted_iota(jnp.int32, acc.shape, 1)
        beta = _sigmoid(acc)
        g = -ex[0][...] * _softplus(acc + ex[1][...])
        out[0][...] = acc
        out[1][...] = jnp.where(lane < heads, beta, jnp.where(lane < 2 * heads, g, 0.0))
    return epi


def _gates_backward(ba, bg, dbg, ea, dtb, heads):
    t = ba.shape[0]

    def body(ba_ref, bg_ref, d_ref, ea_ref, dt_ref, dba_ref, dal_ref, ddt_ref):
        lane = lax.broadcasted_iota(jnp.int32, (t, HEAD_DIM), 1)
        bgv = bg_ref[...]
        d = d_ref[...]
        db = d * bgv * (1.0 - bgv)
        da = -d * ea_ref[...] * _sigmoid(ba_ref[...] + dt_ref[...])
        is_g = jnp.logical_and(lane >= heads, lane < 2 * heads)
        dba = jnp.where(lane < heads, db, jnp.where(is_g, da, 0.0))
        dba_ref[...] = dba.astype(BF16)
        dal_ref[...] = jnp.sum(jnp.where(is_g, d * bgv, 0.0), axis=0, keepdims=True)
        ddt_ref[...] = jnp.sum(jnp.where(is_g, da, 0.0), axis=0, keepdims=True)

    full = pl.BlockSpec((t, HEAD_DIM), lambda: (0, 0))
    vec = pl.BlockSpec((1, HEAD_DIM), lambda: (0, 0))
    return pl.pallas_call(
        body, name="gates_bwd", grid=(),
        in_specs=[full, full, full, vec, vec], out_specs=[full, vec, vec],
        out_shape=[jax.ShapeDtypeStruct((t, HEAD_DIM), BF16), jax.ShapeDtypeStruct((1, HEAD_DIM), F32),
                   jax.ShapeDtypeStruct((1, HEAD_DIM), F32)],
        compiler_params=pltpu.CompilerParams(vmem_limit_bytes=VMEM_LIMIT),
    )(ba, bg, dbg, ea, dtb)


class _Chunk:
    pass


def _split2(x):
    hi = x.astype(BF16)
    return hi, (x - hi.astype(F32)).astype(BF16)


def _split3(x):
    hi = x.astype(BF16)
    rest = x - hi.astype(F32)
    mid = rest.astype(BF16)
    return hi, mid, (rest - mid.astype(F32)).astype(BF16)


def _dot_mask(mask, x, ta=False):
    hi, mid, lo = _split3(x)
    return _bdot(mask, hi, ta=ta) + (_bdot(mask, mid, ta=ta) + _bdot(mask, lo, ta=ta))


def _transpose_by_identity(x):
    r = x.shape[0]
    eye = (lax.broadcasted_iota(jnp.int32, (r, r), 0) == lax.broadcasted_iota(jnp.int32, (r, r), 1)).astype(BF16)
    hi, mid, lo = _split3(x)
    return _bdot(hi, eye, ta=True) + (_bdot(mid, eye, ta=True) + _bdot(lo, eye, ta=True))


def _dot22(a, b, ta=False, tb=False):
    ah, al = _split2(a)
    bh, bl = _split2(b)
    return _bdot(ah, bh, ta, tb) + (_bdot(ah, bl, ta, tb) + _bdot(al, bh, ta, tb))


def _chunk_gates(bg, heads):
    n = CHUNK
    row = lax.broadcasted_iota(jnp.int32, (n, n), 0)
    col = lax.broadcasted_iota(jnp.int32, (n, n), 1)
    lane = lax.broadcasted_iota(jnp.int32, bg.shape, 1)
    graw = jnp.where(jnp.logical_and(lane >= heads, lane < 2 * heads), bg, 0.0)
    gc = _dot_mask((row >= col).astype(BF16), graw)
    return gc, _transpose_by_identity(gc)


def _in_lockstep(generators):
    results = [None] * len(generators)
    live = list(enumerate(generators))
    while live:
        still = []
        for i, gen in live:
            try:
                next(gen)
                still.append((i, gen))
            except StopIteration as stop:
                results[i] = stop.value
        live = still
    return results


def _chunk_local(q, k, v, beta, gc, grow):
    c = _Chunk()
    n = CHUNK
    row = lax.broadcasted_iota(jnp.int32, (n, n), 0)
    col = lax.broadcasted_iota(jnp.int32, (n, n), 1)
    c.tri = row >= col
    c.strict = row > col
    eye = row == col
    c.gcb = jnp.broadcast_to(gc, (n, HEAD_DIM))
    c.decay = jnp.where(c.tri, jnp.exp(jnp.where(c.tri, gc - grow, 0.0)), 0.0)
    c.eg = jnp.exp(c.gcb)
    glast = c.gcb[n - 1:n, :]
    c.egl = jnp.exp(glast)
    c.ekl = jnp.exp(glast - c.gcb)
    c.beta = beta
    c.q = q * (HEAD_DIM ** -0.5)
    c.k = k
    c.v = v
    c.kb = k * beta
    c.vb = v * beta
    c.kg = c.kb * c.eg
    both = _bdot(jnp.concatenate([c.kb, c.q], axis=0), k, tb=True)
    yield
    c.L = jnp.where(c.strict, both[:n] * c.decay, 0.0)
    c.A = jnp.where(c.tri, both[n:] * c.decay, 0.0)
    x = -c.L
    tinv = eye.astype(F32) + x
    p = _dot22(x, x)
    yield
    for _ in range(int(math.log2(n)) - 2):
        both = _dot22(jnp.concatenate([p, tinv], axis=0), p)
        yield
        p, tinv = both[:n], tinv + both[n:]
    c.T = tinv + _dot22(tinv, p)
    yield
    tinv = c.T
    uw = _dot22(tinv, jnp.concatenate([c.vb, c.kg], axis=1))
    yield
    c.u, c.w = uw[:, :HEAD_DIM], uw[:, HEAD_DIM:]
    c.qg = c.q * c.eg
    c.kdec = k * c.ekl
    return c


def _gdn_core(qkv, bg, heads):
    t = qkv.shape[0]
    nchunk = t // CHUNK

    gw = heads * HEAD_DIM

    def body(qkv_ref, bg_ref, o_ref, s_ref, state):
        @pl.when(pl.program_id(0) == 0)
        def _():
            state[...] = jnp.zeros_like(state)

        bg_v = bg_ref[...]
        gc_all, gc_rows = _chunk_gates(bg_v, heads)
        def one_head(h):
            col = lambda s: pl.ds(s * gw + h * HEAD_DIM, HEAD_DIM)
            c = yield from _chunk_local(qkv_ref[:, col(0)], qkv_ref[:, col(1)], qkv_ref[:, col(2)], bg_v[:, h:h + 1],
                                        gc_all[:, heads + h:heads + h + 1], gc_rows[heads + h:heads + h + 1, :])
            s0 = state[h]
            v_new = c.u - _bdot(c.w, s0)
            yield
            o = _bdot(c.qg, s0) + _bdot(c.A, v_new)
            return s0, o, s0 * c.egl + _bdot(c.kdec, v_new, ta=True)

        results = _in_lockstep([one_head(h) for h in range(heads)])
        for h, (s0, o, s1) in enumerate(results):
            s_ref[h, 0] = s0
            o_ref[:, pl.ds(h * HEAD_DIM, HEAD_DIM)] = o
            state[h] = s1

    return pl.pallas_call(
        body, name="gdn_core", grid=(nchunk,),
        in_specs=[pl.BlockSpec((CHUNK, 3 * gw), lambda n: (n, 0)), pl.BlockSpec((CHUNK, HEAD_DIM), lambda n: (n, 0))],
        out_specs=[pl.BlockSpec((CHUNK, gw), lambda n: (n, 0)),
                   pl.BlockSpec((heads, 1, HEAD_DIM, HEAD_DIM), lambda n: (0, n, 0, 0))],
        out_shape=[jax.ShapeDtypeStruct((t, gw), F32),
                   jax.ShapeDtypeStruct((heads, nchunk, HEAD_DIM, HEAD_DIM), F32)],
        scratch_shapes=[pltpu.VMEM((heads, HEAD_DIM, HEAD_DIM), F32)],
        compiler_params=_params("arbitrary"),
    )(qkv, bg)


def _gdn_core_backward(qkv, bg, states, do, heads):
    t = qkv.shape[0]
    nchunk = t // CHUNK
    n = CHUNK

    def one_head(chunk_local, s0, d_out, ds1):
        c = yield from chunk_local
        v_new = c.u - _bdot(c.w, s0)
        dqg = _bdot(d_out, s0, tb=True)
        ds0 = _bdot(c.qg, d_out, ta=True) + ds1 * c.egl
        dv_new = _bdot(c.A, d_out, ta=True) + _bdot(c.kdec, ds1)
        yield
        dA = jnp.where(c.tri, _bdot(d_out, v_new, tb=True), 0.0)
        dkdec = _bdot(v_new, ds1, tb=True)
        dgl = jnp.sum(jnp.sum(ds1 * s0, axis=1, keepdims=True), axis=0, keepdims=True) * c.egl
        dw = -_bdot(dv_new, s0, tb=True)
        ds0 = ds0 - _bdot(c.w, dv_new, ta=True)
        yield
        both = _dot22(c.T, jnp.concatenate([dv_new, dw], axis=1), ta=True)
        yield
        dvb, dkg = both[:, :HEAD_DIM], both[:, HEAD_DIM:]
        dL = jnp.where(c.strict, -(_bdot(dvb, c.u, tb=True) + _bdot(dkg, c.w, tb=True)), 0.0)
        yield
        dm1 = dL * c.decay
        dkb = _bdot(dm1, c.k) + dkg * c.eg
        dk = _bdot(dm1, c.kb, ta=True)
        dm2 = dA * c.decay
        dq = _bdot(dm2, c.k) + dqg * c.eg
        dk = dk + _bdot(dm2, c.q, ta=True) + dkdec * c.ekl + dkb * c.beta
        pm = dL * c.L + dA * c.A
        ones = jnp.ones((n, HEAD_DIM), BF16)
        pm_hi, pm_lo = _split2(pm)
        colsum = _bdot(pm_hi, ones, ta=True) + _bdot(pm_lo, ones, ta=True)
        tk_ = jnp.sum(dkdec * c.kdec, axis=1, keepdims=True)
        dgc = (jnp.sum(pm, axis=1, keepdims=True) - colsum
               + jnp.sum(dqg * c.qg, axis=1, keepdims=True)
               - tk_
               + jnp.sum(dkg * c.kg, axis=1, keepdims=True))
        dgl = dgl + jnp.sum(tk_, axis=0, keepdims=True)
        rowi = lax.broadcasted_iota(jnp.int32, (n, HEAD_DIM), 0)
        dgc = dgc + jnp.where(rowi == n - 1, dgl, 0.0)
        dbeta = jnp.sum(dkb * c.k, axis=1, keepdims=True) + jnp.sum(dvb * c.v, axis=1, keepdims=True)
        return dq * (HEAD_DIM ** -0.5), dk, dvb * c.beta, dbeta, dgc, ds0

    gw = heads * HEAD_DIM

    def body(qkv_ref, bg_ref, s_ref, do_ref, dqkv_ref, dbg_ref, dstate):
        @pl.when(pl.program_id(0) == 0)
        def _():
            dstate[...] = jnp.zeros_like(dstate)

        bg_v = bg_ref[...]
        gc_all, gc_rows = _chunk_gates(bg_v, heads)
        lane = lax.broadcasted_iota(jnp.int32, (n, HEAD_DIM), 1)
        dgates = jnp.zeros((n, HEAD_DIM), F32)
        chains = []
        for h in range(heads):
            col = lambda s: pl.ds(s * gw + h * HEAD_DIM, HEAD_DIM)
            c = _chunk_local(qkv_ref[:, col(0)], qkv_ref[:, col(1)], qkv_ref[:, col(2)], bg_v[:, h:h + 1],
                             gc_all[:, heads + h:heads + h + 1], gc_rows[heads + h:heads + h + 1, :])
            chains.append(one_head(c, s_ref[h, 0], do_ref[:, pl.ds(h * HEAD_DIM, HEAD_DIM)], dstate[h]))
        results = _in_lockstep(chains)
        for h, (dq, dk, dv, dbeta, dgc, ds0) in enumerate(results):
            dgates = jnp.where(lane == h, dbeta, jnp.where(lane == heads + h, dgc, dgates))
        for h, (dq, dk, dv, dbeta, dgc, ds0) in enumerate(results):
            dqkv_ref[:, pl.ds(h * HEAD_DIM, HEAD_DIM)] = dq
            dqkv_ref[:, pl.ds(gw + h * HEAD_DIM, HEAD_DIM)] = dk
            dqkv_ref[:, pl.ds(2 * gw + h * HEAD_DIM, HEAD_DIM)] = dv
            dstate[h] = ds0
        row = lax.broadcasted_iota(jnp.int32, (n, n), 0)
        colm = lax.broadcasted_iota(jnp.int32, (n, n), 1)
        draw = _dot_mask((row >= colm).astype(BF16), dgates, ta=True)
        dbg_ref[...] = jnp.where(lane < heads, dgates, draw)

    last = nchunk - 1
    return pl.pallas_call(
        body, name="gdn_core_bwd", grid=(nchunk,),
        in_specs=[pl.BlockSpec((CHUNK, 3 * gw), lambda i: (last - i, 0)),
                  pl.BlockSpec((CHUNK, HEAD_DIM), lambda i: (last - i, 0)),
                  pl.BlockSpec((heads, 1, HEAD_DIM, HEAD_DIM), lambda i: (0, last - i, 0, 0)),
                  pl.BlockSpec((CHUNK, gw), lambda i: (last - i, 0))],
        out_specs=[pl.BlockSpec((CHUNK, 3 * gw), lambda i: (last - i, 0)),
                   pl.BlockSpec((CHUNK, HEAD_DIM), lambda i: (last - i, 0))],
        out_shape=[jax.ShapeDtypeStruct((t, 3 * gw), F32), jax.ShapeDtypeStruct((t, HEAD_DIM), F32)],
        scratch_shapes=[pltpu.VMEM((heads, HEAD_DIM, HEAD_DIM), F32)],
        compiler_params=_params("arbitrary"),
    )(qkv, bg, states, do)


def _gdn_post(o, proj, z_col0, norm_w, heads, tt):
    t = o.shape[0]
    zb = z_col0 // HEAD_DIM

    def body(o_ref, z_ref, w_ref, out_ref):
        ov = o_ref[...]
        z = z_ref[...]
        rms = lax.rsqrt(jnp.mean(ov * ov, axis=-1, keepdims=True) + NORM_EPS)
        out_ref[...] = (ov * rms * w_ref[...] * (z * _sigmoid(z))).astype(BF16)

    return pl.pallas_call(
        body, name="gdn_post", grid=(t // tt, heads),
        in_specs=[pl.BlockSpec((tt, HEAD_DIM), lambda i, h: (i, h)),
                  pl.BlockSpec((tt, HEAD_DIM), lambda i, h: (i, zb + h)),
                  pl.BlockSpec((1, HEAD_DIM), lambda i, h: (0, 0))],
        out_specs=pl.BlockSpec((tt, HEAD_DIM), lambda i, h: (i, h)),
        out_shape=jax.ShapeDtypeStruct((t, heads * HEAD_DIM), BF16),
        compiler_params=_params("parallel", "parallel"),
    )(o, proj, norm_w)


def _gdn_post_backward(dcat, o, proj, z_col0, norm_w, heads, tt):
    t = o.shape[0]
    zb = z_col0 // HEAD_DIM

    def body(d_ref, o_ref, z_ref, w_ref, do_ref, dz_ref, dw_ref):
        d = d_ref[...]
        ov = o_ref[...]
        z = z_ref[...]
        w = w_ref[...]
        rms = lax.rsqrt(jnp.mean(ov * ov, axis=-1, keepdims=True) + NORM_EPS)
        ohat = ov * rms
        sg = _sigmoid(z)
        gate = z * sg
        dz_ref[...] = (d * ohat * w * (sg * (1.0 + z * (1.0 - sg)))).astype(BF16)
        don = d * gate
        dohat = don * w
        do_ref[...] = rms * (dohat - ohat * jnp.mean(dohat * ohat, axis=-1, keepdims=True))
        dw = jnp.sum(don * ohat, axis=0, keepdims=True)
        first = jnp.logical_and(pl.program_id(0) == 0, pl.program_id(1) == 0)

        @pl.when(first)
        def _():
            dw_ref[...] = dw

        @pl.when(jnp.logical_not(first))
        def _():
            dw_ref[...] += dw

    blk = pl.BlockSpec((tt, HEAD_DIM), lambda i, h: (i, h))
    return pl.pallas_call(
        body, name="gdn_post_bwd", grid=(t // tt, heads),
        in_specs=[blk, blk, pl.BlockSpec((tt, HEAD_DIM), lambda i, h: (i, zb + h)),
                  pl.BlockSpec((1, HEAD_DIM), lambda i, h: (0, 0))],
        out_specs=[blk, blk, pl.BlockSpec((1, HEAD_DIM), lambda i, h: (0, 0))],
        out_shape=[jax.ShapeDtypeStruct((t, heads * HEAD_DIM), F32),
                   jax.ShapeDtypeStruct((t, heads * HEAD_DIM), BF16),
                   jax.ShapeDtypeStruct((1, HEAD_DIM), F32)],
        compiler_params=_params("arbitrary", "arbitrary"),
    )(dcat, o, proj, norm_w)


def _pool_select(levels, group):
    out = levels[-1]
    for gi in range(len(levels) - 2, -1, -1):
        out = jnp.where(group == gi, levels[gi], out)
    return out


def _pool_counts(t, width, group):
    pos = lax.broadcasted_iota(jnp.int32, (t, width), 0)
    win = jnp.left_shift(2, group)
    return jnp.minimum(pos + 1, win).astype(F32)


def _pooled(p, group):
    levels, s, step = [], p, 1
    for _ in POOL_WINDOWS:
        s = s + _shift_down(s, step)
        levels.append(s)
        step *= 2
    cnt = _pool_counts(p.shape[0], p.shape[1], group)
    return _pool_select(levels, group) / cnt - p, cnt


def _pool_forward(proj, p_col0, pool_w, pool_scale):
    t = proj.shape[0]
    groups, cg, _ = pool_w.shape
    pb = p_col0 // cg

    def body(p_ref, w_ref, s_ref, o_ref):
        pooled, _ = _pooled(p_ref[...], pl.program_id(0))
        o_ref[...] = (_bdot(pooled, w_ref[0]) * s_ref[...]).astype(BF16)

    return pl.pallas_call(
        body, name="pool_fwd", grid=(groups,),
        in_specs=[pl.BlockSpec((t, cg), lambda g: (0, pb + g)), pl.BlockSpec((1, cg, cg), lambda g: (g, 0, 0)),
                  pl.BlockSpec((1, cg), lambda g: (0, g))],
        out_specs=pl.BlockSpec((t, cg), lambda g: (0, g)),
        out_shape=jax.ShapeDtypeStruct((t, groups * cg), BF16),
        compiler_params=_params("parallel"),
    )(proj, pool_w, pool_scale)


def _pool_backward(dcat, d_col0, proj, p_col0, pool_w, pool_scale):
    t = proj.shape[0]
    groups, cg, _ = pool_w.shape
    pb = p_col0 // cg
    db = d_col0 // cg

    def body(d_ref, p_ref, w_ref, s_ref, dp_ref, dw_ref, ds_ref):
        group = pl.program_id(0)
        pooled, cnt = _pooled(p_ref[...], group)
        w = w_ref[0]
        d = d_ref[...]
        mixed = _bdot(pooled, w)
        ds_ref[...] = jnp.sum(d * mixed, axis=0, keepdims=True)
        dmixed = d * s_ref[...]
        dw_ref[0] = _bdot(pooled, dmixed, ta=True)
        dpooled = _bdot(dmixed, w, tb=True)
        levels, s, step = [], dpooled / cnt, 1
        for _ in POOL_WINDOWS:
            s = s + _shift_up(s, step)
            levels.append(s)
            step *= 2
        dp_ref[...] = (_pool_select(levels, group) - dpooled).astype(BF16)

    return pl.pallas_call(
        body, name="pool_bwd", grid=(groups,),
        in_specs=[pl.BlockSpec((t, cg), lambda g: (0, db + g)), pl.BlockSpec((t, cg), lambda g: (0, pb + g)),
                  pl.BlockSpec((1, cg, cg), lambda g: (g, 0, 0)), pl.BlockSpec((1, cg), lambda g: (0, g))],
        out_specs=[pl.BlockSpec((t, cg), lambda g: (0, g)), pl.BlockSpec((1, cg, cg), lambda g: (g, 0, 0)),
                   pl.BlockSpec((1, cg), lambda g: (0, g))],
        out_shape=[jax.ShapeDtypeStruct((t, groups * cg), BF16), jax.ShapeDtypeStruct((groups, cg, cg), F32),
                   jax.ShapeDtypeStruct((1, groups * cg), F32)],
        compiler_params=_params("parallel"),
    )(dcat, proj, pool_w, pool_scale)


def _attention(q, k, v, tq):
    t, d = q.shape
    m = k.shape[0]
    dh = d // XATTN_HEADS
    scale = dh ** -0.5

    def body(q_ref, k_ref, v_ref, o_ref):
        s = _bdot(q_ref[...], k_ref[...], tb=True) * scale
        s = s - jnp.max(s, axis=-1, keepdims=True)
        e = jnp.exp(s)
        p = e / jnp.sum(e, axis=-1, keepdims=True)
        o_ref[...] = _bdot(p, v_ref[...]).astype(BF16)

    return pl.pallas_call(
        body, name="xattn_fwd", grid=(XATTN_HEADS, t // tq),
        in_specs=[pl.BlockSpec((tq, dh), lambda h, i: (i, h)), pl.BlockSpec((m, dh), lambda h, i: (0, h)),
                  pl.BlockSpec((m, dh), lambda h, i: (0, h))],
        out_specs=pl.BlockSpec((tq, dh), lambda h, i: (i, h)),
        out_shape=jax.ShapeDtypeStruct((t, d), BF16),
        compiler_params=_params("parallel", "parallel"),
    )(q, k, v)


def _attention_backward(q, k, v, do, tq):
    t, d = q.shape
    m = k.shape[0]
    dh = d // XATTN_HEADS
    scale = dh ** -0.5

    def body(q_ref, k_ref, v_ref, do_ref, dq_ref, dk_ref, dv_ref, dk_acc, dv_acc):
        i = pl.program_id(1)
        qv, kv, vv, dov = q_ref[...], k_ref[...], v_ref[...], do_ref[...]
        s = _bdot(qv, kv, tb=True) * scale
        s = s - jnp.max(s, axis=-1, keepdims=True)
        e = jnp.exp(s)
        p = e / jnp.sum(e, axis=-1, keepdims=True)
        dp = _bdot(dov, vv, tb=True)
        ds = p * (dp - jnp.sum(dp * p, axis=-1, keepdims=True)) * scale
        dq_ref[...] = _bdot(ds, kv).astype(BF16)
        dv_part = _bdot(p, dov, ta=True)
        dk_part = _bdot(ds, qv, ta=True)

        @pl.when(i == 0)
        def _():
            dk_acc[...] = dk_part
            dv_acc[...] = dv_part

        @pl.when(i > 0)
        def _():
            dk_acc[...] += dk_part
            dv_acc[...] += dv_part

        @pl.when(i == pl.num_programs(1) - 1)
        def _():
            dk_ref[...] = dk_acc[...].astype(BF16)
            dv_ref[...] = dv_acc[...].astype(BF16)

    qblk = pl.BlockSpec((tq, dh), lambda h, i: (i, h))
    kblk = pl.BlockSpec((m, dh), lambda h, i: (0, h))
    return pl.pallas_call(
        body, name="xattn_bwd", grid=(XATTN_HEADS, t // tq),
        in_specs=[qblk, kblk, kblk, qblk],
        out_specs=[qblk, kblk, kblk],
        out_shape=[jax.ShapeDtypeStruct((t, d), BF16), jax.ShapeDtypeStruct((m, d), BF16),
                   jax.ShapeDtypeStruct((m, d), BF16)],
        scratch_shapes=[pltpu.VMEM((m, dh), F32), pltpu.VMEM((m, dh), F32)],
        compiler_params=_params("parallel", "arbitrary"),
    )(q, k, v, do)


def _loss_and_ln_backward(xhat, rstd, gamma, beta, target, tm):
    t, d = xhat.shape

    def body(x_ref, r_ref, g_ref, b_ref, t_ref, du_ref, dub_ref, dg_ref, db_ref, loss_ref):
        xh = x_ref[...]
        g = g_ref[...]
        diff = xh * g + b_ref[...] - t_ref[...]
        part = jnp.sum(jnp.sum(diff * diff, axis=1, keepdims=True), axis=0, keepdims=True) * (0.5 / d)
        dy = diff * (1.0 / d)
        du, dg, db = _ln_backward_math(dy, xh, r_ref[...], g)
        du_ref[...] = du
        dub_ref[...] = du.astype(BF16)
        lossrow = jnp.broadcast_to(part, (1, HEAD_DIM))
        first = pl.program_id(0) == 0

        @pl.when(first)
        def _():
            dg_ref[...] = dg
            db_ref[...] = db
            loss_ref[...] = lossrow

        @pl.when(jnp.logical_not(first))
        def _():
            dg_ref[...] += dg
            db_ref[...] += db
            loss_ref[...] += lossrow

    row = pl.BlockSpec((tm, d), lambda i: (i, 0))
    vec = pl.BlockSpec((1, d), lambda i: (0, 0))
    return pl.pallas_call(
        body, name="loss_ln3_bwd", grid=(t // tm,),
        in_specs=[row, pl.BlockSpec((tm, 1), lambda i: (i, 0)), vec, vec, row],
        out_specs=[row, row, vec, vec, pl.BlockSpec((1, HEAD_DIM), lambda i: (0, 0))],
        out_shape=[jax.ShapeDtypeStruct((t, d), F32), jax.ShapeDtypeStruct((t, d), BF16),
                   jax.ShapeDtypeStruct((1, d), F32), jax.ShapeDtypeStruct((1, d), F32),
                   jax.ShapeDtypeStruct((1, HEAD_DIM), F32)],
        compiler_params=_params("arbitrary"),
    )(xhat, rstd, gamma, beta, target)


def _after(token, a):
    return a if token is None else a + token[:1, :1]


def _pick(n, prefs):
    for p in prefs:
        if n % p == 0:
            return p
    return n


def _local_step(x, mem, target, w):
    t, d = x.shape
    heads = w["a_log"].shape[1]
    gw = heads * HEAD_DIM
    groups, cg, _ = w["pool_w"].shape
    pw = groups * cg
    n_main = 4 * gw + pw
    in_cols = n_main + 2 * heads
    s_in = w["w_in3"].shape[0]

    tm = _pick(t, (512, 256, 128))
    tm_ln = _pick(t, (256, 128))
    tk = _pick(d, K_STEPS)

    w_in = jnp.concatenate([w["w_in3"][s] for s in range(s_in)], axis=1)
    w_main = jnp.concatenate([w_in[:, :4 * gw], w_in[:, 4 * gw + 2 * heads:]], axis=1)
    w_ba = jnp.pad(w_in[:, 4 * gw:4 * gw + 2 * heads], ((0, 0), (0, HEAD_DIM - 2 * heads)))
    x_bf = x.astype(BF16)
    mem_bf = mem.astype(BF16)

    proj = _plain("proj_main", x_bf, w_main, tm=tm, tn=_pick(n_main, (1024, 512, 256, 128)), tk=tk, out_dtype=F32)
    ea, dtb = _gate_vectors(w["a_log"], w["dt_bias"], heads)
    vec128 = lambda i, j: (0, 0)
    ba, bg = _matmul(
        "proj_gates", x_bf, w_ba, tm=tm, tn=HEAD_DIM, tk=tk,
        extra=[(ea, (1, HEAD_DIM), vec128), (dtb, (1, HEAD_DIM), vec128)],
        outs=[(jax.ShapeDtypeStruct((t, HEAD_DIM), F32), (tm, HEAD_DIM), _tile)] * 2,
        epilogue=_gates_epilogue(heads))
    qkv = _gdn_pre(proj, w["conv_w"], heads)
    o_gdn, states = _gdn_core(qkv, bg, heads)
    cat_g = _gdn_post(o_gdn, proj, 3 * gw, w["gdn_norm_w"], heads, tm)
    cat_p = _pool_forward(proj, 4 * gw, w["pool_w"], w["pool_scale"])
    cat = jnp.concatenate([cat_g, cat_p], axis=1)
    w = {**w, **(yield ("weights", 1, cat))}
    h1, h1_bf, xhat1, rstd1 = _ln_forward("mix_ln1", cat, w["w_out"], x, w["ln1_g"], w["ln1_b"], tm=tm_ln, tk=tk)

    tn_d = _pick(d, (1024, 512, 256, 128))
    q = _plain("xattn_q", h1_bf, w["xq_w"], tm=tm, tn=tn_d, tk=tk, out_dtype=BF16)
    mlen = mem.shape[0]
    tm_mem = _pick(mlen, (256, 128))
    k = _plain("xattn_k", mem_bf, w["xk_w"], tm=tm_mem, tn=tn_d, tk=tk, out_dtype=BF16)
    v = _plain("xattn_v", mem_bf, w["xv_w"], tm=tm_mem, tn=tn_d, tk=tk, out_dtype=BF16)
    att = _attention(q, k, v, tm)
    h2, h2_bf, xhat2, rstd2 = _ln_forward("xo_ln2", att, w["xo_w"], h1, w["ln2_g"], w["ln2_b"], tm=tm_ln, tk=tk)

    w = {**w, **(yield ("weights", 2, h2_bf))}
    s_up = w["w_up3"].shape[0]
    ff = s_up * w["w_up3"].shape[2]
    tn_f = _pick(ff // s_up, (1024, 512, 256, 128))

    def up_epi(acc, ex, out, i):
        r = jnp.maximum(acc, 0.0)
        out[0][...] = (r * r).astype(BF16)
        out[1][...] = (2.0 * r).astype(BF16)

    act, act_grad = _matmul(
        "mlp_up", h2_bf, w["w_up3"], b_blocks=s_up, tm=tm, tn=tn_f, tk=tk,
        outs=[(jax.ShapeDtypeStruct((t, ff), BF16), (tm, tn_f), _tile)] * 2, epilogue=up_epi)
    tk_f = _pick(ff, K_STEPS)
    _, _, xhat3, rstd3 = _ln_forward("down_ln3", act, w["w_down"], h2, w["ln3_g"], w["ln3_b"], tm=tm_ln, tk=tk_f)

    grads = {}
    du3, du3_bf, grads["ln3_g"], grads["ln3_b"], loss = _loss_and_ln_backward(
        xhat3, rstd3, w["ln3_g"], w["ln3_b"], target, tm_ln)

    def dup_epi(acc, ex, out, i):
        out[0][...] = (acc * ex[0][...].astype(F32)).astype(BF16)

    dup = _matmul(
        "mlp_down_dx", du3_bf, w["w_down"], tb=True, tm=tm, tn=tn_f, tk=tk,
        extra=[(act_grad, (tm, tn_f), _tile)],
        outs=[(jax.ShapeDtypeStruct((t, ff), BF16), (tm, tn_f), _tile)], epilogue=dup_epi)[0]
    tk_t = _pick(t, K_STEPS)
    tm_w = _pick(d, (512, 256, 128))
    grads["w_down"] = _plain("mlp_down_dw", act, du3_bf, ta=True, tm=_pick(ff, (512, 256, 128)), tn=tn_d, tk=tk_t,
                             out_dtype=F32)
    grads["w_up3"] = _plain("mlp_up_dw", h2_bf, dup, ta=True, tm=tm_w, tn=tn_f, tk=tk_t, out_dtype=F32, out3=s_up)
    token = yield ("grads", 0, {n: grads.pop(n) for n in ("w_down", "w_up3")})
    du2, du2_bf, grads["ln2_g"], grads["ln2_b"] = _ln_backward(
        "mlp_up_dx_ln2", dup, w["w_up3"], du3, xhat2, rstd2, _after(token, w["ln2_g"]), tm=tm_ln,
        tk=_pick(ff // s_up, K_STEPS), b_blocks=s_up)

    grads["xo_w"] = _plain("xo_dw", att, du2_bf, ta=True, tm=tm_w, tn=tn_d, tk=tk_t, out_dtype=F32)
    datt = _plain("xo_dx", du2_bf, w["xo_w"], tb=True, tm=tm, tn=tn_d, tk=tk, out_dtype=BF16)
    dq, dk, dv = _attention_backward(q, k, v, datt, tm)
    tk_m = _pick(mlen, (256, 128))
    grads["xq_w"] = _plain("xq_dw", h1_bf, dq, ta=True, tm=tm_w, tn=tn_d, tk=tk_t, out_dtype=F32)
    grads["xk_w"] = _plain("xk_dw", mem_bf, dk, ta=True, tm=tm_w, tn=tn_d, tk=tk_m, out_dtype=F32)
    grads["xv_w"] = _plain("xv_dw", mem_bf, dv, ta=True, tm=tm_w, tn=tn_d, tk=tk_m, out_dtype=F32)
    du1, du1_bf, grads["ln1_g"], grads["ln1_b"] = _ln_backward(
        "xq_dx_ln1", dq, w["xq_w"], du2, xhat1, rstd1, w["ln1_g"], tm=tm_ln, tk=tk)

    grads["w_out"] = _plain("out_dw", cat, du1_bf, ta=True, tm=tm_w, tn=tn_d, tk=tk_t, out_dtype=F32)
    token = yield ("grads", 1, {n: grads.pop(n) for n in ("xo_w", "xq_w", "xk_w", "xv_w", "w_out")})
    dcat = _plain("out_dx", du1_bf, w["w_out"], tb=True, tm=tm, tn=tn_d, tk=tk, out_dtype=F32)
    dp, grads["pool_w"], grads["pool_scale"] = _pool_backward(dcat, gw, proj, 4 * gw, w["pool_w"],
                                                              _after(token, w["pool_scale"]))
    do_gdn, dz, grads["gdn_norm_w"] = _gdn_post_backward(dcat, o_gdn, proj, 3 * gw, _after(token, w["gdn_norm_w"]),
                                                         heads, tm)
    dqkv, dbg = _gdn_core_backward(qkv, bg, states, do_gdn, heads)
    dqkv_pre, grads["conv_w"] = _gdn_pre_backward(proj, w["conv_w"], dqkv, heads)
    dba, dalog_row, ddt_row = _gates_backward(ba, bg, dbg, ea, dtb, heads)
    grads["a_log"] = dalog_row[:, heads:2 * heads]
    grads["dt_bias"] = ddt_row[:, heads:2 * heads]

    dproj = jnp.concatenate([dqkv_pre, dz, dp], axis=1)
    tn_main = _pick(n_main, (1024, 512, 256, 128))
    dw_main = _plain("proj_dw", x_bf, dproj, ta=True, tm=tm_w, tn=tn_main, tk=tk_t, out_dtype=F32)
    dw_ba = _plain("proj_gates_dw", x_bf, dba, ta=True, tm=tm_w, tn=HEAD_DIM, tk=tk_t, out_dtype=F32)
    dw_in = jnp.concatenate([dw_main[:, :4 * gw], dw_ba[:, :2 * heads], dw_main[:, 4 * gw:]], axis=1)
    per = in_cols // s_in
    grads["w_in3"] = jnp.stack([dw_in[:, s * per:(s + 1) * per] for s in range(s_in)], axis=0)

    def dx_epi(acc, ex, out, i):
        out[0][...] = acc + ex[1][...] + ALPHA * ex[0][...]

    dx_gates = _plain("proj_gates_dx", dba, w_ba, tb=True, tm=tm, tn=tn_d, tk=HEAD_DIM, out_dtype=F32)
    grad_x = _matmul(
        "proj_dx", dproj, w_main, tb=True, tm=tm, tn=tn_d, tk=_pick(n_main, (2560,) + K_STEPS),
        extra=[(du1, (tm, tn_d), _tile), (dx_gates, (tm, tn_d), _tile)],
        outs=[(jax.ShapeDtypeStruct((t, d), F32), (tm, tn_d), _tile)], epilogue=dx_epi)[0]
    return loss, grad_x, grads


def _adamw(name, w, g, m, v):
    r, c = w.shape
    tr = _pick(r, (256, 128, 64, 32, 16, 8))
    c1 = 1.0 - ADAM_B1 ** ADAM_STEP
    c2 = 1.0 - ADAM_B2 ** ADAM_STEP

    def body(w_ref, g_ref, m_ref, v_ref, d_ref, mo_ref, vo_ref):
        gv = g_ref[...]
        mn = ADAM_B1 * m_ref[...] + (1.0 - ADAM_B1) * gv
        vn = ADAM_B2 * v_ref[...] + (1.0 - ADAM_B2) * (gv * gv)
        d_ref[...] = -ADAM_LR * ((mn / c1) / (jnp.sqrt(vn / c2) + ADAM_EPS) + ADAM_WD * w_ref[...])
        mo_ref[...] = mn
        vo_ref[...] = vn

    blk = pl.BlockSpec((tr, c), lambda i: (i, 0))
    return pl.pallas_call(
        body, name=name, grid=(r // tr,), in_specs=[blk] * 4, out_specs=[blk] * 3,
        out_shape=[jax.ShapeDtypeStruct((r, c), F32)] * 3,
        compiler_params=_params("parallel"),
    )(w, g, m, v)


def _place():
    x, y, c = lax.axis_index("x"), lax.axis_index("y"), lax.axis_index("c")
    chips = [(1 - x, y), (x, 1 - y), (1 - x, 1 - y)]
    return x, y, c, chips


HBM = pl.BlockSpec(memory_space=pltpu.HBM)


SEM = pl.BlockSpec(memory_space=pltpu.SEMAPHORE)
ANY = pl.BlockSpec(memory_space=pl.ANY)
EFFECT = pltpu.SideEffectType.DATAFLOW_SIDE_EFFECTING


def _in_hbm(a):
    return pltpu.with_memory_space_constraint(a, pltpu.HBM)


def _remote(src, dst, send_sem, recv_sem, to):
    return pltpu.make_async_remote_copy(src_ref=src, dst_ref=dst, send_sem=send_sem, recv_sem=recv_sem,
                                        device_id=to, device_id_type=MESH)


def _landed(lands, i, shard_index, which):
    rows = lands[i].shape[1] // 2
    return lands[i].at[shard_index, pl.ds(which * rows, rows)]


def _gather_start(shards, after):
    n = len(shards)
    lands = [lax.empty((N_SHARD,) + s.shape, s.dtype) for s in shards]

    def body(*refs):
        ins, zones = refs[:n], refs[n:2 * n]
        ici_send, ici_recv, own_send, own_recv = refs[2 * n + 1:2 * n + 5]
        token = refs[-1]
        x, y, c, chips = _place()
        me = 2 * x + y
        for i in range(n):
            rows = ins[i].shape[0] // 2
            for j, chip in enumerate(chips):
                _remote(ins[i].at[pl.ds(c * rows, rows)], _landed(zones, i, me, c), ici_send.at[3 * i + j],
                        ici_recv.at[3 * i + j], (*chip, c)).start()
        for i in range(n):
            _remote(ins[i], zones[i].at[me], own_send.at[i], own_recv.at[i], (x, y, 1 - c)).start()
        token[...] = jnp.zeros_like(token)

    dma = pltpu.SemaphoreType.DMA
    outs = pl.pallas_call(
        body, name="gather_start",
        in_specs=[HBM] * (2 * n) + [ANY],
        out_shape=(dma((3 * n,)), dma((3 * n,)), dma((n,)), dma((n,)),
                   *[pltpu.HBM(a.shape, a.dtype) for a in shards + lands], jax.ShapeDtypeStruct((8, LANES), F32)),
        out_specs=(SEM, SEM, SEM, SEM, *[HBM] * (2 * n), pl.BlockSpec(memory_space=pltpu.VMEM)),
        input_output_aliases={k: 4 + k for k in range(2 * n)},
        compiler_params=pltpu.CompilerParams(has_side_effects=EFFECT),
    )(*[_in_hbm(a) for a in shards + lands], after)
    sems = dict(zip(("ici_send", "ici_recv", "own_send", "own_recv"), outs[:4]))
    return sems, list(outs[4:4 + n]), list(outs[4 + n:4 + 2 * n]), outs[-1]


def _gather_forward(name, idx, lands, sems, after):
    n = len(idx)

    def body(*refs):
        zones = refs[:n]
        ici_recv = refs[n]
        fwd_send, fwd_recv = refs[n + 2], refs[n + 3]
        x, y, c, chips = _place()
        for k, i in enumerate(idx):
            for j, chip in enumerate(chips):
                half = _landed(zones, k, 2 * chip[0] + chip[1], c)
                _remote(half, half, fwd_send.at[3 * k + j], ici_recv.at[3 * i + j], (*chip, c)).wait_recv()
                _remote(half, half, fwd_send.at[3 * k + j], fwd_recv.at[3 * k + j], (x, y, 1 - c)).start()

    dma = pltpu.SemaphoreType.DMA
    outs = pl.pallas_call(
        body, name=name,
        in_specs=[HBM] * n + [SEM, ANY],
        out_shape=(dma((3 * n,)), dma((3 * n,)), *[pltpu.HBM(a.shape, a.dtype) for a in lands]),
        out_specs=(SEM, SEM, *[HBM] * n),
        input_output_aliases={k: 2 + k for k in range(n)},
        compiler_params=pltpu.CompilerParams(has_side_effects=EFFECT),
    )(*lands, sems["ici_recv"], after)
    return (outs[0], outs[1]), list(outs[2:])


def _gather_wait(name, idx, shards, lands, sems, fwd):
    n = len(idx)

    def body(*refs):
        ins, zones = refs[:n], refs[n:2 * n]
        ici_send, own_send, own_recv, fwd_send, fwd_recv = refs[2 * n:2 * n + 5]
        x, y, c, chips = _place()
        me = 2 * x + y
        for k, i in enumerate(idx):
            rows = ins[k].shape[0] // 2
            mine = ins[k].at[pl.ds(c * rows, rows)]
            for j, chip in enumerate(chips):
                theirs = 2 * chip[0] + chip[1]
                _remote(mine, _landed(zones, k, me, c), ici_send.at[3 * i + j], fwd_recv.at[3 * k + j],
                        (*chip, c)).wait_send()
                sent = _landed(zones, k, theirs, c)
                _remote(sent, sent, fwd_send.at[3 * k + j], fwd_recv.at[3 * k + j], (x, y, 1 - c)).wait_send()
                passed = _landed(zones, k, theirs, 1 - c)
                _remote(passed, passed, fwd_send.at[3 * k + j], fwd_recv.at[3 * k + j], (x, y, 1 - c)).wait_recv()
            own = _remote(ins[k], zones[k].at[me], own_send.at[i], own_recv.at[i], (x, y, 1 - c))
            own.wait_send()
            own.wait_recv()

    outs = pl.pallas_call(
        body, name=name,
        in_specs=[HBM] * (2 * n) + [SEM] * 5,
        out_shape=tuple(pltpu.HBM(a.shape, a.dtype) for a in lands),
        out_specs=tuple([HBM] * n),
        input_output_aliases={n + k: k for k in range(n)},
        compiler_params=pltpu.CompilerParams(has_side_effects=EFFECT),
    )(*shards, *lands, sems["ici_send"], sems["own_send"], sems["own_recv"], fwd[0], fwd[1])
    return list(outs)


def _all_reduce_small(name, slab, after=None):
    r, width = slab.shape
    ndev = 8

    def body(x_ref, after_ref, out_ref, buf, send_sems, recv_sems):
        x, y, c, _ = _place()
        me = 4 * x + 2 * y + c
        buf[me] = x_ref[...]
        copies = []
        for k in range(1, ndev):
            peer = jnp.bitwise_xor(me, k)
            to = (peer // 4, (peer // 2) % 2, peer % 2)
            cp = pltpu.make_async_remote_copy(src_ref=x_ref, dst_ref=buf.at[me], send_sem=send_sems.at[k - 1],
                                              recv_sem=recv_sems.at[k - 1], device_id=to, device_id_type=MESH)
            cp.start()
            copies.append(cp)
        for k in range(1, ndev):
            peer = jnp.bitwise_xor(me, k)
            pltpu.make_async_remote_copy(src_ref=x_ref, dst_ref=buf.at[peer], send_sem=send_sems.at[k - 1],
                                         recv_sem=recv_sems.at[k - 1], device_id=(x, y, c),
                                         device_id_type=MESH).wait_recv()
        for cp in copies:
            cp.wait_send()
        total = buf[0]
        for d in range(1, ndev):
            total = total + buf[d]
        out_ref[...] = total

    return pl.pallas_call(
        body, name=name,
        in_specs=[pl.BlockSpec(memory_space=pltpu.VMEM), ANY], out_specs=pl.BlockSpec(memory_space=pltpu.VMEM),
        out_shape=jax.ShapeDtypeStruct((r, width), F32),
        scratch_shapes=[pltpu.VMEM((ndev, r, width), F32), pltpu.SemaphoreType.DMA((ndev - 1,)),
                        pltpu.SemaphoreType.DMA((ndev - 1,))],
        compiler_params=pltpu.CompilerParams(vmem_limit_bytes=VMEM_LIMIT),
    )(slab, slab if after is None else after)


def _swap_halves(name, grads):
    n = len(grads)

    def body(*refs):
        ins, outs = refs[:n], refs[n:2 * n]
        send_sems, recv_sems = refs[2 * n:]
        x, y, c, _ = _place()
        copies = []
        for i in range(n):
            rows = ins[i].shape[1] // 2
            for s in range(N_SHARD):
                cp = pltpu.make_async_remote_copy(
                    src_ref=ins[i].at[s, pl.ds((1 - c) * rows, rows)], dst_ref=outs[i].at[s],
                    send_sem=send_sems.at[N_SHARD * i + s], recv_sem=recv_sems.at[N_SHARD * i + s],
                    device_id=(x, y, 1 - c), device_id_type=MESH)
                cp.start()
                copies.append(cp)
        for cp in copies:
            cp.wait()

    return pl.pallas_call(
        body, name=name,
        in_specs=[HBM] * n, out_specs=[HBM] * n,
        out_shape=[jax.ShapeDtypeStruct((N_SHARD, g.shape[1] // 2, g.shape[2]), F32) for g in grads],
        scratch_shapes=[pltpu.SemaphoreType.DMA((N_SHARD * n,)), pltpu.SemaphoreType.DMA((N_SHARD * n,))],
    )(*grads)


def _chip_partial(name, grad, other, core):
    s, r, cdim = grad.shape
    rows = r // 2
    tr = _pick(rows, (256, 128, 64, 32, 16))
    nb = rows // tr

    def body(core_ref, g_ref, o_ref, out_ref):
        out_ref[...] = (g_ref[...] + o_ref[...]).astype(BF16)

    return pl.pallas_call(
        body, name=name,
        grid_spec=pltpu.PrefetchScalarGridSpec(
            num_scalar_prefetch=1, grid=(s, nb),
            in_specs=[pl.BlockSpec((None, tr, cdim), lambda j, b, core_ref: (j, core_ref[0] * nb + b, 0)),
                      pl.BlockSpec((None, tr, cdim), lambda j, b, core_ref: (j, b, 0))],
            out_specs=pl.BlockSpec((None, tr, cdim), lambda j, b, core_ref: (j, b, 0))),
        out_shape=jax.ShapeDtypeStruct((s, rows, cdim), BF16),
        compiler_params=_params("parallel", "parallel"),
    )(core, grad, other)


def _partial_copies(ins, zones, send_sems, recv_sems):
    x, y, c, chips = _place()
    return [_remote(ins[i].at[2 * chip[0] + chip[1]], zones[i].at[j], send_sems.at[3 * i + j],
                    recv_sems.at[3 * i + j], (*chip, c))
            for i in range(len(ins)) for j, chip in enumerate(chips)]


def _send_partials_start(name, partials):
    n = len(partials)
    lands = [lax.empty((3,) + p.shape[1:], BF16) for p in partials]

    def body(*refs):
        for cp in _partial_copies(refs[:n], refs[n:2 * n], refs[2 * n], refs[2 * n + 1]):
            cp.start()
        refs[-1][...] = jnp.zeros_like(refs[-1])

    dma = pltpu.SemaphoreType.DMA
    outs = pl.pallas_call(
        body, name=name,
        in_specs=[HBM] * (2 * n),
        out_shape=(dma((3 * n,)), dma((3 * n,)), *[pltpu.HBM(a.shape, a.dtype) for a in partials + lands],
                   jax.ShapeDtypeStruct((8, LANES), F32)),
        out_specs=(SEM, SEM, *[HBM] * (2 * n), pl.BlockSpec(memory_space=pltpu.VMEM)),
        input_output_aliases={k: 2 + k for k in range(2 * n)},
        compiler_params=pltpu.CompilerParams(has_side_effects=EFFECT),
    )(*[_in_hbm(a) for a in partials + lands])
    return (outs[0], outs[1]), list(outs[2:2 + n]), list(outs[2 + n:2 + 2 * n]), outs[-1]


def _send_partials_wait(name, started, after):
    sems, partials, lands, _ = started
    n = len(partials)

    def body(*refs):
        for cp in _partial_copies(refs[:n], refs[n:2 * n], refs[2 * n], refs[2 * n + 1]):
            cp.wait_send()
            cp.wait_recv()

    outs = pl.pallas_call(
        body, name=name,
        in_specs=[HBM] * (2 * n) + [SEM, SEM] + [ANY] * len(after),
        out_shape=tuple(pltpu.HBM(a.shape, a.dtype) for a in lands),
        out_specs=tuple([HBM] * n),
        input_output_aliases={n + k: k for k in range(n)},
        compiler_params=pltpu.CompilerParams(has_side_effects=EFFECT),
    )(*partials, *lands, sems[0], sems[1], *after)
    return list(outs)


def _reduce_own(name, grad, other, received, where):
    s, r, cdim = grad.shape
    rows = r // 2
    tr = _pick(rows, (256, 128, 64, 32, 16))
    nb = rows // tr

    def body(where_ref, g_ref, o_ref, r_ref, out_ref):
        total = g_ref[...] + o_ref[...]
        for j in range(3):
            total = total + r_ref[j].astype(F32)
        out_ref[...] = total

    return pl.pallas_call(
        body, name=name,
        grid_spec=pltpu.PrefetchScalarGridSpec(
            num_scalar_prefetch=1, grid=(nb,),
            in_specs=[pl.BlockSpec((None, tr, cdim), lambda b, w_ref: (w_ref[0], w_ref[1] * nb + b, 0)),
                      pl.BlockSpec((None, tr, cdim), lambda b, w_ref: (w_ref[0], b, 0)),
                      pl.BlockSpec((3, tr, cdim), lambda b, w_ref: (0, b, 0))],
            out_specs=pl.BlockSpec((tr, cdim), lambda b, w_ref: (w_ref[1] * nb + b, 0))),
        out_shape=jax.ShapeDtypeStruct((r, cdim), F32),
        compiler_params=_params("parallel"),
    )(where, grad, other, received)


def _join_halves(name, halves):
    n = len(halves)

    def body(*refs):
        bufs = refs[n:2 * n]
        send_sems, recv_sems = refs[2 * n:]
        x, y, c, _ = _place()
        copies = []
        for i in range(n):
            rows = bufs[i].shape[0] // 2
            mine = bufs[i].at[pl.ds(c * rows, rows)]
            cp = pltpu.make_async_remote_copy(src_ref=mine, dst_ref=mine, send_sem=send_sems.at[i],
                                              recv_sem=recv_sems.at[i], device_id=(x, y, 1 - c), device_id_type=MESH)
            cp.start()
            copies.append(cp)
        for i, cp in enumerate(copies):
            rows = bufs[i].shape[0] // 2
            theirs = bufs[i].at[pl.ds((1 - c) * rows, rows)]
            pltpu.make_async_remote_copy(src_ref=theirs, dst_ref=theirs, send_sem=send_sems.at[i],
                                         recv_sem=recv_sems.at[i], device_id=(x, y, 1 - c),
                                         device_id_type=MESH).wait_recv()
            cp.wait_send()

    return pl.pallas_call(
        body, name=name,
        in_specs=[HBM] * n, out_specs=[HBM] * n,
        out_shape=[jax.ShapeDtypeStruct(h.shape, F32) for h in halves],
        input_output_aliases={i: i for i in range(n)},
        scratch_shapes=[pltpu.SemaphoreType.DMA((n,)), pltpu.SemaphoreType.DMA((n,))],
    )(*halves)


BIG = ("w_in", "pool_w", "w_out", "xq_w", "xk_w", "xv_w", "xo_w", "w_up", "w_down")
GATHER_GROUPS = ((0, 1), (2, 3, 4, 5, 6), (7, 8))
SMALL = ("conv_w", "a_log", "dt_bias", "gdn_norm_w", "pool_scale", "ln1_g", "ln1_b", "ln2_g", "ln2_b", "ln3_g", "ln3_b")
ORDER = ("w_in", "conv_w", "a_log", "dt_bias", "gdn_norm_w", "pool_w", "pool_scale", "w_out", "ln1_g", "ln1_b",
         "xq_w", "xk_w", "xv_w", "xo_w", "ln2_g", "ln2_b", "w_up", "w_down", "ln3_g", "ln3_b")
LANES = 128


def _rows(flat_len):
    return -(-flat_len // LANES)


def _pack(pieces):
    out = []
    for p in pieces:
        flat = p.reshape(-1).astype(F32)
        out.append(jnp.pad(flat, (0, _rows(flat.shape[0]) * LANES - flat.shape[0])).reshape(-1, LANES))
    slab = jnp.concatenate(out, axis=0)
    return jnp.pad(slab, ((0, -slab.shape[0] % 8), (0, 0)))


def _unpack(slab, shapes):
    out, row = [], 0
    for shp in shapes:
        size = math.prod(shp)
        out.append(slab[row:row + _rows(size)].reshape(-1)[:size].reshape(shp))
        row += _rows(size)
    return out


def _as2d(a):
    a = a[0]
    return a.reshape(-1, a.shape[-1]) if a.ndim == 3 else a


def kernel(x, mem, w_in, conv_w, a_log, dt_bias, gdn_norm_w, pool_w, pool_scale, w_out, ln1_g, ln1_b, xq_w, xk_w, xv_w, xo_w, ln2_g, ln2_b, w_up, w_down, ln3_g, ln3_b, loss_target, m_w_in, m_conv_w, m_a_log, m_dt_bias, m_gdn_norm_w, m_pool_w, m_pool_scale, m_w_out, m_ln1_g, m_ln1_b, m_xq_w, m_xk_w, m_xv_w, m_xo_w, m_ln2_g, m_ln2_b, m_w_up, m_w_down, m_ln3_g, m_ln3_b, v_w_in, v_conv_w, v_a_log, v_dt_bias, v_gdn_norm_w, v_pool_w, v_pool_scale, v_w_out, v_ln1_g, v_ln1_b, v_xq_w, v_xk_w, v_xv_w, v_xo_w, v_ln2_g, v_ln2_b, v_w_up, v_w_down, v_ln3_g, v_ln3_b):
    given = dict(locals())
    cx, cy, cc = lax.axis_index("x"), lax.axis_index("y"), lax.axis_index("c")
    me = 2 * cx + cy
    groups = pool_w.shape[1]
    cs = pool_w.shape[2]
    kk, conv_cols = conv_w.shape[1], conv_w.shape[2]
    core = cc.astype(jnp.int32).reshape(1)
    where = jnp.stack([me, cc]).astype(jnp.int32)

    conv_slab = jnp.zeros((kk, N_SHARD * conv_cols), F32)
    conv_slab = lax.dynamic_update_slice(conv_slab, conv_w[0] * (cc == 0).astype(F32), (0, me * conv_cols))
    wts = {"conv_w": _unpack(_all_reduce_small("gather_conv_w", _pack([conv_slab])), [conv_slab.shape])[0]}

    sems, shards, lands, token = _gather_start([_as2d(given[n]).astype(BF16) for n in BIG], wts["conv_w"])

    def fetch(group, after):
        idx = GATHER_GROUPS[group]
        fwd, zones = _gather_forward(f"gather_forward_{group}", idx, [lands[i] for i in idx], sems, after)
        full = dict(zip([BIG[i] for i in idx],
                        _gather_wait(f"gather_wait_{group}", idx, [shards[i] for i in idx], zones, sems, fwd)))
        out = {}
        for n, a in full.items():
            if n == "w_in":
                out["w_in3"] = a
            elif n == "w_up":
                out["w_up3"] = a
            elif n == "pool_w":
                out[n] = a.reshape(N_SHARD, groups, cs, -1).transpose(1, 0, 2, 3).reshape(groups, N_SHARD * cs, -1)
            else:
                out[n] = a.reshape(-1, a.shape[-1])
        return out

    for n in ("a_log", "dt_bias", "gdn_norm_w", "pool_scale", "ln1_g", "ln1_b", "ln2_g", "ln2_b", "ln3_g", "ln3_b"):
        wts[n] = given[n]
    wts.update(fetch(0, token))

    def start_reduce(group, grads):
        names, blocks = [], []
        for n, g in grads.items():
            if n == "pool_w":
                g = g.reshape(groups, N_SHARD, cs, -1).transpose(1, 0, 2, 3).reshape(N_SHARD, groups * cs, -1)
            elif g.ndim == 2:
                g = g.reshape(N_SHARD, -1, g.shape[-1])
            names.append({"w_in3": "w_in", "w_up3": "w_up"}.get(n, n))
            blocks.append(g)
        others = _swap_halves(f"grad_swap_{group}", blocks)
        partials = [_chip_partial("chip_partial_" + n, gb, ob, core) for n, gb, ob in zip(names, blocks, others)]
        return group, names, blocks, others, _send_partials_start(f"grad_send_start_{group}", partials)

    grad, delta, new_m, new_v = {}, {}, {}, {}

    def finish_reduce(state, after):
        group, names, blocks, others, started = state
        received = _send_partials_wait(f"grad_send_wait_{group}", started, after)
        halves = [_reduce_own("reduce_own_" + n, gb, ob, rb, where)
                  for n, gb, ob, rb in zip(names, blocks, others, received)]
        for n, g in zip(names, _join_halves(f"grad_join_{group}", halves)):
            shp = given[n].shape
            d2, m2, v2 = _adamw("adamw_" + n, _as2d(given[n]), g, _as2d(given["m_" + n]), _as2d(given["v_" + n]))
            grad[n], delta[n], new_m[n], new_v[n] = (a.reshape(shp) for a in (g, d2, m2, v2))
        return d2

    step = _local_step(x[0], mem[0], loss_target[0], wts)
    pending = []
    request = next(step)
    while True:
        try:
            if request[0] == "weights":
                request = step.send(fetch(request[1], request[2]))
            else:
                pending.append(start_reduce(request[1], request[2]))
                request = step.send(pending[-1][4][3])
        except StopIteration as stop:
            loss_row, grad_x, g = stop.value
            break
    pending.append(start_reduce(2, {n: g[n] for n in ("w_in3", "pool_w")}))

    after = [pending[-1][4][3], grad_x]
    for state in pending:
        after = [finish_reduce(state, after)]

    small_names = ("a_log", "dt_bias", "gdn_norm_w", "pool_scale", "ln1_g", "ln1_b", "ln2_g", "ln2_b", "ln3_g", "ln3_b")
    pieces = [g["conv_w"]] + [g[n] for n in small_names] + [loss_row[:, :1]]
    shapes = [p.shape for p in pieces]
    summed = _unpack(_all_reduce_small("all_reduce_small", _pack(pieces), after[0]), shapes)
    gsmall = dict(zip(small_names, summed[1:-1]))
    gsmall["conv_w"] = lax.dynamic_slice(summed[0], (0, me * conv_cols), (kk, conv_cols))
    loss = summed[-1][0, 0]

    sshapes = [given[n].shape for n in SMALL]
    slabs = [_pack([given[p + n] for n in SMALL]) for p in ("", "m_", "v_")]
    gslab = _pack([gsmall[n] for n in SMALL])
    outs = _adamw("adamw_small", slabs[0], gslab, slabs[1], slabs[2])
    for dst, slab in zip((delta, new_m, new_v), outs):
        dst.update(zip(SMALL, _unpack(slab, sshapes)))
    for n in SMALL:
        grad[n] = gsmall[n].reshape(given[n].shape)

    return (loss, grad_x[None], *[grad[n] for n in ORDER], *[delta[n] for n in ORDER],
            *[new_m[n] for n in ORDER], *[new_v[n] for n in ORDER])
```

```python
import functools
import math

import jax
import jax.numpy as jnp
from jax import lax
from jax.experimental import pallas as pl
from jax.experimental.pallas import tpu as pltpu

F32 = jnp.float32
BF16 = jnp.bfloat16
MESH = pl.DeviceIdType.MESH

HEAD_DIM = 128
CHUNK = 64
POOL_WINDOWS = (2, 4, 8, 16)
XATTN_HEADS = 4
ALPHA = 2.0 ** 0.25
LN_EPS = 1e-5
NORM_EPS = 1e-6
ADAM_LR, ADAM_B1, ADAM_B2, ADAM_EPS, ADAM_WD, ADAM_STEP = 0.001, 0.9, 0.999, 1e-08, 0.01, 10
N_SHARD = 4
VMEM_LIMIT = 56 * 1024 * 1024
K_STEPS = (2048, 1024, 512, 256, 128)


def _params(*sem):
    return pltpu.CompilerParams(dimension_semantics=sem, vmem_limit_bytes=VMEM_LIMIT)


def _bdot(a, b, ta=False, tb=False):
    dims = (((0 if ta else 1,), (1 if tb else 0,)), ((), ()))
    return lax.dot_general(a.astype(BF16), b.astype(BF16), dims, preferred_element_type=F32)


def _sigmoid(x):
    return 1.0 / (1.0 + jnp.exp(-x))


def _matmul(name, a, b, *, ta=False, tb=False, tm, tn, tk, extra=(), outs, epilogue, b_blocks=None,
            sequential=False):
    m, k_dim = (a.shape[1], a.shape[0]) if ta else a.shape
    if b_blocks and tb:
        n = b.shape[1]
        k_dim = b.shape[0] * b.shape[2]
        per = b.shape[2] // tk
        b_spec = pl.BlockSpec((None, tn, tk), lambda i, j, k: (k // per, j, k % per))
    elif b_blocks:
        n = b.shape[0] * b.shape[2]
        per = b.shape[2] // tn
        b_spec = pl.BlockSpec((None, tk, tn), lambda i, j, k: (j // per, k, j % per))
    elif tb:
        n = b.shape[0]
        b_spec = pl.BlockSpec((tn, tk), lambda i, j, k: (j, k))
    else:
        n = b.shape[1]
        b_spec = pl.BlockSpec((tk, tn), lambda i, j, k: (k, j))
    assert m % tm == 0 and n % tn == 0 and k_dim % tk == 0, (name, m, n, k_dim, tm, tn, tk)
    nk = k_dim // tk
    a_spec = pl.BlockSpec((tk, tm), lambda i, j, k: (k, i)) if ta else pl.BlockSpec((tm, tk), lambda i, j, k: (i, k))
    n_extra, n_out = len(extra), len(outs)

    def wrap(index_map):
        return lambda i, j, k: index_map(i, j)

    def body_one_step(*refs):
        ex = refs[2:2 + n_extra]
        out = refs[2 + n_extra:2 + n_extra + n_out]
        epilogue(_bdot(refs[0][...], refs[1][...], ta, tb), ex, out, pl.program_id(0))

    def body(*refs):
        a_ref, b_ref = refs[0], refs[1]
        ex = refs[2:2 + n_extra]
        out = refs[2 + n_extra:2 + n_extra + n_out]
        acc = refs[-1]
        i, k = pl.program_id(0), pl.program_id(2)
        part = _bdot(a_ref[...], b_ref[...], ta, tb)

        @pl.when(k == 0)
        def _():
            acc[...] = part

        @pl.when(jnp.logical_and(k > 0, k < nk - 1))
        def _():
            acc[...] += part

        @pl.when(k == nk - 1)
        def _():
            epilogue(acc[...] + part, ex, out, i)

    sem = ("arbitrary",) * 3 if sequential else ("parallel", "parallel", "arbitrary")
    res = pl.pallas_call(
        body_one_step if nk == 1 else body, name=name, grid=(m // tm, n // tn, nk),
        in_specs=[a_spec, b_spec] + [pl.BlockSpec(bs, wrap(im)) for _, bs, im in extra],
        out_specs=[pl.BlockSpec(bs, wrap(im)) for _, bs, im in outs],
        out_shape=[s for s, _, _ in outs],
        scratch_shapes=[] if nk == 1 else [pltpu.VMEM((tm, tn), F32)],
        compiler_params=_params(*sem),
    )(a, b, *[x for x, _, _ in extra])
    return res


def _tile(i, j):
    return (i, j)


def _plain(name, a, b, *, ta=False, tb=False, tm, tn, tk, out_dtype, b_blocks=None, out3=None):
    m = a.shape[1] if ta else a.shape[0]
    n = (b.shape[0] * b.shape[2]) if b_blocks else (b.shape[0] if tb else b.shape[1])

    def epi(acc, ex, out, i):
        out[0][...] = acc.astype(out_dtype)

    if out3:
        per = (n // out3) // tn
        spec = (jax.ShapeDtypeStruct((out3, m, n // out3), out_dtype), (None, tm, tn),
                lambda i, j: (j // per, i, j % per))
    else:
        spec = (jax.ShapeDtypeStruct((m, n), out_dtype), (tm, tn), _tile)
    return _matmul(name, a, b, ta=ta, tb=tb, tm=tm, tn=tn, tk=tk, outs=[spec], epilogue=epi,
                   b_blocks=b_blocks)[0]


def _ln_forward(name, a, b, res, gamma, beta, *, tm, tk, want_h=True):
    m, n = res.shape

    def epi(acc, ex, out, i):
        u = ALPHA * ex[0][...] + acc
        mu = jnp.mean(u, axis=-1, keepdims=True)
        xc = u - mu
        var = jnp.mean(xc * xc, axis=-1, keepdims=True)
        rstd = lax.rsqrt(var + LN_EPS)
        xhat = xc * rstd
        out[-2][...] = xhat
        out[-1][...] = rstd
        if want_h:
            h = xhat * ex[1][...] + ex[2][...]
            out[0][...] = h
            out[1][...] = h.astype(BF16)

    row = lambda i, j: (i, 0)
    vec = lambda i, j: (0, 0)
    outs = [(jax.ShapeDtypeStruct((m, n), F32), (tm, n), row), (jax.ShapeDtypeStruct((m, n), BF16), (tm, n), row),
            (jax.ShapeDtypeStruct((m, n), F32), (tm, n), row), (jax.ShapeDtypeStruct((m, 1), F32), (tm, 1), row)]
    return _matmul(
        name, a, b, tm=tm, tn=n, tk=tk,
        extra=[(res, (tm, n), row), (gamma, (1, n), vec), (beta, (1, n), vec)],
        outs=outs if want_h else outs[2:], epilogue=epi)


def _ln_backward_math(dy, xhat, rstd, gamma):
    dxhat = dy * gamma
    m1 = jnp.mean(dxhat, axis=-1, keepdims=True)
    m2 = jnp.mean(dxhat * xhat, axis=-1, keepdims=True)
    du = rstd * (dxhat - m1 - xhat * m2)
    return du, jnp.sum(dy * xhat, axis=0, keepdims=True), jnp.sum(dy, axis=0, keepdims=True)


def _ln_backward(name, a, b, dres, xhat, rstd, gamma, *, tm, tk, b_blocks=None, tb=True):
    m, n = dres.shape

    def epi(acc, ex, out, i):
        dy = acc + ALPHA * ex[0][...]
        du, dg, db = _ln_backward_math(dy, ex[1][...], ex[2][...], ex[3][...])
        out[0][...] = du
        out[1][...] = du.astype(BF16)
        first = i == 0

        @pl.when(first)
        def _():
            out[2][...] = dg
            out[3][...] = db

        @pl.when(jnp.logical_not(first))
        def _():
            out[2][...] += dg
            out[3][...] += db

    row = lambda i, j: (i, 0)
    vec = lambda i, j: (0, 0)
    return _matmul(
        name, a, b, tb=tb, tm=tm, tn=n, tk=tk, b_blocks=b_blocks, sequential=True,
        extra=[(dres, (tm, n), row), (xhat, (tm, n), row), (rstd, (tm, 1), row), (gamma, (1, n), vec)],
        outs=[(jax.ShapeDtypeStruct((m, n), F32), (tm, n), row),
              (jax.ShapeDtypeStruct((m, n), BF16), (tm, n), row),
              (jax.ShapeDtypeStruct((1, n), F32), (1, n), vec),
              (jax.ShapeDtypeStruct((1, n), F32), (1, n), vec)],
        epilogue=epi)


def _shift_down(x, k):
    row = lax.broadcasted_iota(jnp.int32, x.shape, 0)
    return jnp.where(row >= k, pltpu.roll(x, k, axis=0), 0.0)


def _shift_up(x, k):
    t = x.shape[0]
    row = lax.broadcasted_iota(jnp.int32, x.shape, 0)
    return jnp.where(row < t - k, pltpu.roll(x, t - k, axis=0), 0.0)


def _conv_silu_norm(x, w, normalise):
    kk = w.shape[0]
    c = x * w[kk - 1:kk, :]
    for j in range(kk - 1):
        c = c + _shift_down(x, kk - 1 - j) * w[j:j + 1, :]
    sg = _sigmoid(c)
    s = c * sg
    r = lax.rsqrt(jnp.sum(s * s, axis=-1, keepdims=True) + NORM_EPS)
    y = jnp.where(normalise, s * r, s)
    return c, sg, s, r, y


def _gdn_pre(proj, conv_w, heads):
    t = proj.shape[0]
    kk = conv_w.shape[0]

    def body(x_ref, w_ref, o_ref):
        normalise = pl.program_id(0) < 2
        o_ref[...] = _conv_silu_norm(x_ref[...], w_ref[...], normalise)[4]

    col = lambda s, h: (0, s * heads + h)
    return pl.pallas_call(
        body, name="gdn_pre", grid=(3, heads),
        in_specs=[pl.BlockSpec((t, HEAD_DIM), col), pl.BlockSpec((kk, HEAD_DIM), col)],
        out_specs=pl.BlockSpec((t, HEAD_DIM), col),
        out_shape=jax.ShapeDtypeStruct((t, 3 * heads * HEAD_DIM), F32),
        compiler_params=_params("parallel", "parallel"),
    )(proj, conv_w)


def _gdn_pre_backward(proj, conv_w, dqkv, heads):
    t = proj.shape[0]
    kk = conv_w.shape[0]

    def body(x_ref, w_ref, dy_ref, dx_ref, dw_ref):
        normalise = pl.program_id(0) < 2
        x = x_ref[...]
        w = w_ref[...]
        dy = dy_ref[...]
        c, sg, s, r, y = _conv_silu_norm(x, w, normalise)
        ds_norm = r * (dy - y * jnp.sum(dy * y, axis=-1, keepdims=True))
        ds = jnp.where(normalise, ds_norm, dy)
        dc = ds * (sg * (1.0 + c * (1.0 - sg)))
        dx = dc * w[kk - 1:kk, :]
        rows = [None] * kk
        rows[kk - 1] = jnp.sum(dc * x, axis=0, keepdims=True)
        for j in range(kk - 1):
            lag = kk - 1 - j
            dx = dx + _shift_up(dc, lag) * w[j:j + 1, :]
            rows[j] = jnp.sum(dc * _shift_down(x, lag), axis=0, keepdims=True)
        dx_ref[...] = dx.astype(BF16)
        dw_ref[...] = jnp.concatenate(rows, axis=0)

    col = lambda s, h: (0, s * heads + h)
    return pl.pallas_call(
        body, name="gdn_pre_bwd", grid=(3, heads),
        in_specs=[pl.BlockSpec((t, HEAD_DIM), col), pl.BlockSpec((kk, HEAD_DIM), col),
                  pl.BlockSpec((t, HEAD_DIM), col)],
        out_specs=[pl.BlockSpec((t, HEAD_DIM), col), pl.BlockSpec((kk, HEAD_DIM), col)],
        out_shape=[jax.ShapeDtypeStruct((t, 3 * heads * HEAD_DIM), BF16),
                   jax.ShapeDtypeStruct((kk, 3 * heads * HEAD_DIM), F32)],
        compiler_params=_params("parallel", "parallel"),
    )(proj, conv_w, dqkv)


def _gate_vectors(a_log, dt_bias, heads):
    pad = lambda v: jnp.pad(v.astype(F32), ((0, 0), (heads, HEAD_DIM - 2 * heads)))
    return pad(jnp.exp(a_log.astype(F32))), pad(dt_bias)


def _softplus(x):
    return jnp.maximum(x, 0.0) + jnp.log(1.0 + jnp.exp(-jnp.abs(x)))


def _gates_epilogue(heads):
    def epi(acc, ex, out, i):
        lane = lax.broadcasted_iota(jnp.int32, acc.shape, 1)
        beta = _sigmoid(acc)
        g = -ex[0][...] * _softplus(acc + ex[1][...])
        out[0][...] = acc
        out[1][...] = jnp.where(lane < heads, beta, jnp.where(lane < 2 * heads, g, 0.0))
    return epi


def _gates_backward(ba, bg, dbg, ea, dtb, heads):
    t = ba.shape[0]

    def body(ba_ref, bg_ref, d_ref, ea_ref, dt_ref, dba_ref, dal_ref, ddt_ref):
        lane = lax.broadcasted_iota(jnp.int32, (t, HEAD_DIM), 1)
        bgv = bg_ref[...]
        d = d_ref[...]
        db = d * bgv * (1.0 - bgv)
        da = -d * ea_ref[...] * _sigmoid(ba_ref[...] + dt_ref[...])
        is_g = jnp.logical_and(lane >= heads, lane < 2 * heads)
        dba = jnp.where(lane < heads, db, jnp.where(is_g, da, 0.0))
        dba_ref[...] = dba.astype(BF16)
        dal_ref[...] = jnp.sum(jnp.where(is_g, d * bgv, 0.0), axis=0, keepdims=True)
        ddt_ref[...] = jnp.sum(jnp.where(is_g, da, 0.0), axis=0, keepdims=True)

    full = pl.BlockSpec((t, HEAD_DIM), lambda: (0, 0))
    vec = pl.BlockSpec((1, HEAD_DIM), lambda: (0, 0))
    return pl.pallas_call(
        body, name="gates_bwd", grid=(),
        in_specs=[full, full, full, vec, vec], out_specs=[full, vec, vec],
        out_shape=[jax.ShapeDtypeStruct((t, HEAD_DIM), BF16), jax.ShapeDtypeStruct((1, HEAD_DIM), F32),
                   jax.ShapeDtypeStruct((1, HEAD_DIM), F32)],
        compiler_params=pltpu.CompilerParams(vmem_limit_bytes=VMEM_LIMIT),
    )(ba, bg, dbg, ea, dtb)


class _Chunk:
    pass


def _split2(x):
    hi = x.astype(BF16)
    return hi, (x - hi.astype(F32)).astype(BF16)


def _split3(x):
    hi = x.astype(BF16)
    rest = x - hi.astype(F32)
    mid = rest.astype(BF16)
    return hi, mid, (rest - mid.astype(F32)).astype(BF16)


def _dot_mask(mask, x, ta=False):
    hi, mid, lo = _split3(x)
    return _bdot(mask, hi, ta=ta) + (_bdot(mask, mid, ta=ta) + _bdot(mask, lo, ta=ta))


def _transpose_by_identity(x):
    r = x.shape[0]
    eye = (lax.broadcasted_iota(jnp.int32, (r, r), 0) == lax.broadcasted_iota(jnp.int32, (r, r), 1)).astype(BF16)
    hi, mid, lo = _split3(x)
    return _bdot(hi, eye, ta=True) + (_bdot(mid, eye, ta=True) + _bdot(lo, eye, ta=True))


def _dot22(a, b, ta=False, tb=False):
    ah, al = _split2(a)
    bh, bl = _split2(b)
    return _bdot(ah, bh, ta, tb) + (_bdot(ah, bl, ta, tb) + _bdot(al, bh, ta, tb))


def _chunk_gates(bg, heads):
    n = CHUNK
    row = lax.broadcasted_iota(jnp.int32, (n, n), 0)
    col = lax.broadcasted_iota(jnp.int32, (n, n), 1)
    lane = lax.broadcasted_iota(jnp.int32, bg.shape, 1)
    graw = jnp.where(jnp.logical_and(lane >= heads, lane < 2 * heads), bg, 0.0)
    gc = _dot_mask((row >= col).astype(BF16), graw)
    return gc, _transpose_by_identity(gc)


def _in_lockstep(generators):
    results = [None] * len(generators)
    live = list(enumerate(generators))
    while live:
        still = []
        for i, gen in live:
            try:
                next(gen)
                still.append((i, gen))
            except StopIteration as stop:
                results[i] = stop.value
        live = still
    return results


def _chunk_local(q, k, v, beta, gc, grow):
    c = _Chunk()
    n = CHUNK
    row = lax.broadcasted_iota(jnp.int32, (n, n), 0)
    col = lax.broadcasted_iota(jnp.int32, (n, n), 1)
    c.tri = row >= col
    c.strict = row > col
    eye = row == col
    c.gcb = jnp.broadcast_to(gc, (n, HEAD_DIM))
    c.decay = jnp.where(c.tri, jnp.exp(jnp.where(c.tri, gc - grow, 0.0)), 0.0)
    c.eg = jnp.exp(c.gcb)
    glast = c.gcb[n - 1:n, :]
    c.egl = jnp.exp(glast)
    c.ekl = jnp.exp(glast - c.gcb)
    c.beta = beta
    c.q = q * (HEAD_DIM ** -0.5)
    c.k = k
    c.v = v
    c.kb = k * beta
    c.vb = v * beta
    c.kg = c.kb * c.eg
    both = _bdot(jnp.concatenate([c.kb, c.q], axis=0), k, tb=True)
    yield
    c.L = jnp.where(c.strict, both[:n] * c.decay, 0.0)
    c.A = jnp.where(c.tri, both[n:] * c.decay, 0.0)
    x = -c.L
    tinv = eye.astype(F32) + x
    p = _dot22(x, x)
    yield
    for _ in range(int(math.log2(n)) - 2):
        both = _dot22(jnp.concatenate([p, tinv], axis=0), p)
        yield
        p, tinv = both[:n], tinv + both[n:]
    c.T = tinv + _dot22(tinv, p)
    yield
    tinv = c.T
    uw = _dot22(tinv, jnp.concatenate([c.vb, c.kg], axis=1))
    yield
    c.u, c.w = uw[:, :HEAD_DIM], uw[:, HEAD_DIM:]
    c.qg = c.q * c.eg
    c.kdec = k * c.ekl
    return c


def _gdn_core(qkv, bg, heads):
    t = qkv.shape[0]
    nchunk = t // CHUNK

    gw = heads * HEAD_DIM

    def body(qkv_ref, bg_ref, o_ref, s_ref, state):
        @pl.when(pl.program_id(0) == 0)
        def _():
            state[...] = jnp.zeros_like(state)

        bg_v = bg_ref[...]
        gc_all, gc_rows = _chunk_gates(bg_v, heads)
        def one_head(h):
            col = lambda s: pl.ds(s * gw + h * HEAD_DIM, HEAD_DIM)
            c = yield from _chunk_local(qkv_ref[:, col(0)], qkv_ref[:, col(1)], qkv_ref[:, col(2)], bg_v[:, h:h + 1],
                                        gc_all[:, heads + h:heads + h + 1], gc_rows[heads + h:heads + h + 1, :])
            s0 = state[h]
            v_new = c.u - _bdot(c.w, s0)
            yield
            o = _bdot(c.qg, s0) + _bdot(c.A, v_new)
            return s0, o, s0 * c.egl + _bdot(c.kdec, v_new, ta=True)

        results = _in_lockstep([one_head(h) for h in range(heads)])
        for h, (s0, o, s1) in enumerate(results):
            s_ref[h, 0] = s0
            o_ref[:, pl.ds(h * HEAD_DIM, HEAD_DIM)] = o
            state[h] = s1

    return pl.pallas_call(
        body, name="gdn_core", grid=(nchunk,),
        in_specs=[pl.BlockSpec((CHUNK, 3 * gw), lambda n: (n, 0)), pl.BlockSpec((CHUNK, HEAD_DIM), lambda n: (n, 0))],
        out_specs=[pl.BlockSpec((CHUNK, gw), lambda n: (n, 0)),
                   pl.BlockSpec((heads, 1, HEAD_DIM, HEAD_DIM), lambda n: (0, n, 0, 0))],
        out_shape=[jax.ShapeDtypeStruct((t, gw), F32),
                   jax.ShapeDtypeStruct((heads, nchunk, HEAD_DIM, HEAD_DIM), F32)],
        scratch_shapes=[pltpu.VMEM((heads, HEAD_DIM, HEAD_DIM), F32)],
        compiler_params=_params("arbitrary"),
    )(qkv, bg)


def _gdn_core_backward(qkv, bg, states, do, heads):
    t = qkv.shape[0]
    nchunk = t // CHUNK
    n = CHUNK

    def one_head(chunk_local, s0, d_out, ds1):
        c = yield from chunk_local
        v_new = c.u - _bdot(c.w, s0)
        dqg = _bdot(d_out, s0, tb=True)
        ds0 = _bdot(c.qg, d_out, ta=True) + ds1 * c.egl
        dv_new = _bdot(c.A, d_out, ta=True) + _bdot(c.kdec, ds1)
        yield
        dA = jnp.where(c.tri, _bdot(d_out, v_new, tb=True), 0.0)
        dkdec = _bdot(v_new, ds1, tb=True)
        dgl = jnp.sum(jnp.sum(ds1 * s0, axis=1, keepdims=True), axis=0, keepdims=True) * c.egl
        dw = -_bdot(dv_new, s0, tb=True)
        ds0 = ds0 - _bdot(c.w, dv_new, ta=True)
        yield
        both = _dot22(c.T, jnp.concatenate([dv_new, dw], axis=1), ta=True)
        yield
        dvb, dkg = both[:, :HEAD_DIM], both[:, HEAD_DIM:]
        dL = jnp.where(c.strict, -(_bdot(dvb, c.u, tb=True) + _bdot(dkg, c.w, tb=True)), 0.0)
        yield
        dm1 = dL * c.decay
        dkb = _bdot(dm1, c.k) + dkg * c.eg
        dk = _bdot(dm1, c.kb, ta=True)
        dm2 = dA * c.decay
        dq = _bdot(dm2, c.k) + dqg * c.eg
        dk = dk + _bdot(dm2, c.q, ta=True) + dkdec * c.ekl + dkb * c.beta
        pm = dL * c.L + dA * c.A
        ones = jnp.ones((n, HEAD_DIM), BF16)
        pm_hi, pm_lo = _split2(pm)
        colsum = _bdot(pm_hi, ones, ta=True) + _bdot(pm_lo, ones, ta=True)
        tk_ = jnp.sum(dkdec * c.kdec, axis=1, keepdims=True)
        dgc = (jnp.sum(pm, axis=1, keepdims=True) - colsum
               + jnp.sum(dqg * c.qg, axis=1, keepdims=True)
               - tk_
               + jnp.sum(dkg * c.kg, axis=1, keepdims=True))
        dgl = dgl + jnp.sum(tk_, axis=0, keepdims=True)
        rowi = lax.broadcasted_iota(jnp.int32, (n, HEAD_DIM), 0)
        dgc = dgc + jnp.where(rowi == n - 1, dgl, 0.0)
        dbeta = jnp.sum(dkb * c.k, axis=1, keepdims=True) + jnp.sum(dvb * c.v, axis=1, keepdims=True)
        return dq * (HEAD_DIM ** -0.5), dk, dvb * c.beta, dbeta, dgc, ds0

    gw = heads * HEAD_DIM

    def body(qkv_ref, bg_ref, s_ref, do_ref, dqkv_ref, dbg_ref, dstate):
        @pl.when(pl.program_id(0) == 0)
        def _():
            dstate[...] = jnp.zeros_like(dstate)

        bg_v = bg_ref[...]
        gc_all, gc_rows = _chunk_gates(bg_v, heads)
        lane = lax.broadcasted_iota(jnp.int32, (n, HEAD_DIM), 1)
        dgates = jnp.zeros((n, HEAD_DIM), F32)
        chains = []
        for h in range(heads):
            col = lambda s: pl.ds(s * gw + h * HEAD_DIM, HEAD_DIM)
            c = _chunk_local(qkv_ref[:, col(0)], qkv_ref[:, col(1)], qkv_ref[:, col(2)], bg_v[:, h:h + 1],
                             gc_all[:, heads + h:heads + h + 1], gc_rows[heads + h:heads + h + 1, :])
            chains.append(one_head(c, s_ref[h, 0], do_ref[:, pl.ds(h * HEAD_DIM, HEAD_DIM)], dstate[h]))
        results = _in_lockstep(chains)
        for h, (dq, dk, dv, dbeta, dgc, ds0) in enumerate(results):
            dgates = jnp.where(lane == h, dbeta, jnp.where(lane == heads + h, dgc, dgates))
        for h, (dq, dk, dv, dbeta, dgc, ds0) in enumerate(results):
            dqkv_ref[:, pl.ds(h * HEAD_DIM, HEAD_DIM)] = dq
            dqkv_ref[:, pl.ds(gw + h * HEAD_DIM, HEAD_DIM)] = dk
            dqkv_ref[:, pl.ds(2 * gw + h * HEAD_DIM, HEAD_DIM)] = dv
            dstate[h] = ds0
        row = lax.broadcasted_iota(jnp.int32, (n, n), 0)
        colm = lax.broadcasted_iota(jnp.int32, (n, n), 1)
        draw = _dot_mask((row >= colm).astype(BF16), dgates, ta=True)
        dbg_ref[...] = jnp.where(lane < heads, dgates, draw)

    last = nchunk - 1
    return pl.pallas_call(
        body, name="gdn_core_bwd", grid=(nchunk,),
        in_specs=[pl.BlockSpec((CHUNK, 3 * gw), lambda i: (last - i, 0)),
                  pl.BlockSpec((CHUNK, HEAD_DIM), lambda i: (last - i, 0)),
                  pl.BlockSpec((heads, 1, HEAD_DIM, HEAD_DIM), lambda i: (0, last - i, 0, 0)),
                  pl.BlockSpec((CHUNK, gw), lambda i: (last - i, 0))],
        out_specs=[pl.BlockSpec((CHUNK, 3 * gw), lambda i: (last - i, 0)),
                   pl.BlockSpec((CHUNK, HEAD_DIM), lambda i: (last - i, 0))],
        out_shape=[jax.ShapeDtypeStruct((t, 3 * gw), F32), jax.ShapeDtypeStruct((t, HEAD_DIM), F32)],
        scratch_shapes=[pltpu.VMEM((heads, HEAD_DIM, HEAD_DIM), F32)],
        compiler_params=_params("arbitrary"),
    )(qkv, bg, states, do)


def _gdn_post(o, proj, z_col0, norm_w, heads, tt):
    t = o.shape[0]
    zb = z_col0 // HEAD_DIM

    def body(o_ref, z_ref, w_ref, out_ref):
        ov = o_ref[...]
        z = z_ref[...]
        rms = lax.rsqrt(jnp.mean(ov * ov, axis=-1, keepdims=True) + NORM_EPS)
        out_ref[...] = (ov * rms * w_ref[...] * (z * _sigmoid(z))).astype(BF16)

    return pl.pallas_call(
        body, name="gdn_post", grid=(t // tt, heads),
        in_specs=[pl.BlockSpec((tt, HEAD_DIM), lambda i, h: (i, h)),
                  pl.BlockSpec((tt, HEAD_DIM), lambda i, h: (i, zb + h)),
                  pl.BlockSpec((1, HEAD_DIM), lambda i, h: (0, 0))],
        out_specs=pl.BlockSpec((tt, HEAD_DIM), lambda i, h: (i, h)),
        out_shape=jax.ShapeDtypeStruct((t, heads * HEAD_DIM), BF16),
        compiler_params=_params("parallel", "parallel"),
    )(o, proj, norm_w)


def _gdn_post_backward(dcat, o, proj, z_col0, norm_w, heads, tt):
    t = o.shape[0]
    zb = z_col0 // HEAD_DIM

    def body(d_ref, o_ref, z_ref, w_ref, do_ref, dz_ref, dw_ref):
        d = d_ref[...]
        ov = o_ref[...]
        z = z_ref[...]
        w = w_ref[...]
        rms = lax.rsqrt(jnp.mean(ov * ov, axis=-1, keepdims=True) + NORM_EPS)
        ohat = ov * rms
        sg = _sigmoid(z)
        gate = z * sg
        dz_ref[...] = (d * ohat * w * (sg * (1.0 + z * (1.0 - sg)))).astype(BF16)
        don = d * gate
        dohat = don * w
        do_ref[...] = rms * (dohat - ohat * jnp.mean(dohat * ohat, axis=-1, keepdims=True))
        dw = jnp.sum(don * ohat, axis=0, keepdims=True)
        first = jnp.logical_and(pl.program_id(0) == 0, pl.program_id(1) == 0)

        @pl.when(first)
        def _():
            dw_ref[...] = dw

        @pl.when(jnp.logical_not(first))
        def _():
            dw_ref[...] += dw

    blk = pl.BlockSpec((tt, HEAD_DIM), lambda i, h: (i, h))
    return pl.pallas_call(
        body, name="gdn_post_bwd", grid=(t // tt, heads),
        in_specs=[blk, blk, pl.BlockSpec((tt, HEAD_DIM), lambda i, h: (i, zb + h)),
                  pl.BlockSpec((1, HEAD_DIM), lambda i, h: (0, 0))],
        out_specs=[blk, blk, pl.BlockSpec((1, HEAD_DIM), lambda i, h: (0, 0))],
        out_shape=[jax.ShapeDtypeStruct((t, heads * HEAD_DIM), F32),
                   jax.ShapeDtypeStruct((t, heads * HEAD_DIM), BF16),
                   jax.ShapeDtypeStruct((1, HEAD_DIM), F32)],
        compiler_params=_params("arbitrary", "arbitrary"),
    )(dcat, o, proj, norm_w)


def _pool_select(levels, group):
    out = levels[-1]
    for gi in range(len(levels) - 2, -1, -1):
        out = jnp.where(group == gi, levels[gi], out)
    return out


def _pool_counts(t, width, group):
    pos = lax.broadcasted_iota(jnp.int32, (t, width), 0)
    win = jnp.left_shift(2, group)
    return jnp.minimum(pos + 1, win).astype(F32)


def _pooled(p, group):
    levels, s, step = [], p, 1
    for _ in POOL_WINDOWS:
        s = s + _shift_down(s, step)
        levels.append(s)
        step *= 2
    cnt = _pool_counts(p.shape[0], p.shape[1], group)
    return _pool_select(levels, group) / cnt - p, cnt


def _pool_forward(proj, p_col0, pool_w, pool_scale):
    t = proj.shape[0]
    groups, cg, _ = pool_w.shape
    pb = p_col0 // cg

    def body(p_ref, w_ref, s_ref, o_ref):
        pooled, _ = _pooled(p_ref[...], pl.program_id(0))
        o_ref[...] = (_bdot(pooled, w_ref[0]) * s_ref[...]).astype(BF16)

    return pl.pallas_call(
        body, name="pool_fwd", grid=(groups,),
        in_specs=[pl.BlockSpec((t, cg), lambda g: (0, pb + g)), pl.BlockSpec((1, cg, cg), lambda g: (g, 0, 0)),
                  pl.BlockSpec((1, cg), lambda g: (0, g))],
        out_specs=pl.BlockSpec((t, cg), lambda g: (0, g)),
        out_shape=jax.ShapeDtypeStruct((t, groups * cg), BF16),
        compiler_params=_params("parallel"),
    )(proj, pool_w, pool_scale)


def _pool_backward(dcat, d_col0, proj, p_col0, pool_w, pool_scale):
    t = proj.shape[0]
    groups, cg, _ = pool_w.shape
    pb = p_col0 // cg
    db = d_col0 // cg

    def body(d_ref, p_ref, w_ref, s_ref, dp_ref, dw_ref, ds_ref):
        group = pl.program_id(0)
        pooled, cnt = _pooled(p_ref[...], group)
        w = w_ref[0]
        d = d_ref[...]
        mixed = _bdot(pooled, w)
        ds_ref[...] = jnp.sum(d * mixed, axis=0, keepdims=True)
        dmixed = d * s_ref[...]
        dw_ref[0] = _bdot(pooled, dmixed, ta=True)
        dpooled = _bdot(dmixed, w, tb=True)
        levels, s, step = [], dpooled / cnt, 1
        for _ in POOL_WINDOWS:
            s = s + _shift_up(s, step)
            levels.append(s)
            step *= 2
        dp_ref[...] = (_pool_select(levels, group) - dpooled).astype(BF16)

    return pl.pallas_call(
        body, name="pool_bwd", grid=(groups,),
        in_specs=[pl.BlockSpec((t, cg), lambda g: (0, db + g)), pl.BlockSpec((t, cg), lambda g: (0, pb + g)),
                  pl.BlockSpec((1, cg, cg), lambda g: (g, 0, 0)), pl.BlockSpec((1, cg), lambda g: (0, g))],
        out_specs=[pl.BlockSpec((t, cg), lambda g: (0, g)), pl.BlockSpec((1, cg, cg), lambda g: (g, 0, 0)),
                   pl.BlockSpec((1, cg), lambda g: (0, g))],
        out_shape=[jax.ShapeDtypeStruct((t, groups * cg), BF16), jax.ShapeDtypeStruct((groups, cg, cg), F32),
                   jax.ShapeDtypeStruct((1, groups * cg), F32)],
        compiler_params=_params("parallel"),
    )(dcat, proj, pool_w, pool_scale)


def _attention(q, k, v, tq):
    t, d = q.shape
    m = k.shape[0]
    dh = d // XATTN_HEADS
    scale = dh ** -0.5

    def body(q_ref, k_ref, v_ref, o_ref):
        s = _bdot(q_ref[...], k_ref[...], tb=True) * scale
        s = s - jnp.max(s, axis=-1, keepdims=True)
        e = jnp.exp(s)
        p = e / jnp.sum(e, axis=-1, keepdims=True)
        o_ref[...] = _bdot(p, v_ref[...]).astype(BF16)

    return pl.pallas_call(
        body, name="xattn_fwd", grid=(XATTN_HEADS, t // tq),
        in_specs=[pl.BlockSpec((tq, dh), lambda h, i: (i, h)), pl.BlockSpec((m, dh), lambda h, i: (0, h)),
                  pl.BlockSpec((m, dh), lambda h, i: (0, h))],
        out_specs=pl.BlockSpec((tq, dh), lambda h, i: (i, h)),
        out_shape=jax.ShapeDtypeStruct((t, d), BF16),
        compiler_params=_params("parallel", "parallel"),
    )(q, k, v)


def _attention_backward(q, k, v, do, tq):
    t, d = q.shape
    m = k.shape[0]
    dh = d // XATTN_HEADS
    scale = dh ** -0.5

    def body(q_ref, k_ref, v_ref, do_ref, dq_ref, dk_ref, dv_ref, dk_acc, dv_acc):
        i = pl.program_id(1)
        qv, kv, vv, dov = q_ref[...], k_ref[...], v_ref[...], do_ref[...]
        s = _bdot(qv, kv, tb=True) * scale
        s = s - jnp.max(s, axis=-1, keepdims=True)
        e = jnp.exp(s)
        p = e / jnp.sum(e, axis=-1, keepdims=True)
        dp = _bdot(dov, vv, tb=True)
        ds = p * (dp - jnp.sum(dp * p, axis=-1, keepdims=True)) * scale
        dq_ref[...] = _bdot(ds, kv).astype(BF16)
        dv_part = _bdot(p, dov, ta=True)
        dk_part = _bdot(ds, qv, ta=True)

        @pl.when(i == 0)
        def _():
            dk_acc[...] = dk_part
            dv_acc[...] = dv_part

        @pl.when(i > 0)
        def _():
            dk_acc[...] += dk_part
            dv_acc[...] += dv_part

        @pl.when(i == pl.num_programs(1) - 1)
        def _():
            dk_ref[...] = dk_acc[...].astype(BF16)
            dv_ref[...] = dv_acc[...].astype(BF16)

    qblk = pl.BlockSpec((tq, dh), lambda h, i: (i, h))
    kblk = pl.BlockSpec((m, dh), lambda h, i: (0, h))
    return pl.pallas_call(
        body, name="xattn_bwd", grid=(XATTN_HEADS, t // tq),
        in_specs=[qblk, kblk, kblk, qblk],
        out_specs=[qblk, kblk, kblk],
        out_shape=[jax.ShapeDtypeStruct((t, d), BF16), jax.ShapeDtypeStruct((m, d), BF16),
                   jax.ShapeDtypeStruct((m, d), BF16)],
        scratch_shapes=[pltpu.VMEM((m, dh), F32), pltpu.VMEM((m, dh), F32)],
        compiler_params=_params("parallel", "arbitrary"),
    )(q, k, v, do)


def _loss_and_ln_backward(xhat, rstd, gamma, beta, target, tm):
    t, d = xhat.shape

    def body(x_ref, r_ref, g_ref, b_ref, t_ref, du_ref, dub_ref, dg_ref, db_ref, loss_ref):
        xh = x_ref[...]
        g = g_ref[...]
        diff = xh * g + b_ref[...] - t_ref[...]
        part = jnp.sum(jnp.sum(diff * diff, axis=1, keepdims=True), axis=0, keepdims=True) * (0.5 / d)
        dy = diff * (1.0 / d)
        du, dg, db = _ln_backward_math(dy, xh, r_ref[...], g)
        du_ref[...] = du
        dub_ref[...] = du.astype(BF16)
        lossrow = jnp.broadcast_to(part, (1, HEAD_DIM))
        first = pl.program_id(0) == 0

        @pl.when(first)
        def _():
            dg_ref[...] = dg
            db_ref[...] = db
            loss_ref[...] = lossrow

        @pl.when(jnp.logical_not(first))
        def _():
            dg_ref[...] += dg
            db_ref[...] += db
            loss_ref[...] += lossrow

    row = pl.BlockSpec((tm, d), lambda i: (i, 0))
    vec = pl.BlockSpec((1, d), lambda i: (0, 0))
    return pl.pallas_call(
        body, name="loss_ln3_bwd", grid=(t // tm,),
        in_specs=[row, pl.BlockSpec((tm, 1), lambda i: (i, 0)), vec, vec, row],
        out_specs=[row, row, vec, vec, pl.BlockSpec((1, HEAD_DIM), lambda i: (0, 0))],
        out_shape=[jax.ShapeDtypeStruct((t, d), F32), jax.ShapeDtypeStruct((t, d), BF16),
                   jax.ShapeDtypeStruct((1, d), F32), jax.ShapeDtypeStruct((1, d), F32),
                   jax.ShapeDtypeStruct((1, HEAD_DIM), F32)],
        compiler_params=_params("arbitrary"),
    )(xhat, rstd, gamma, beta, target)


def _after(token, a):
    return a if token is None else a + token[:1, :1].astype(a.dtype)


def _pick(n, prefs):
    for p in prefs:
        if n % p == 0:
            return p
    return n


def _local_step(x, mem, target, w):
    t, d = x.shape
    heads = w["a_log"].shape[1]
    gw = heads * HEAD_DIM
    groups, cg, _ = w["pool_w"].shape
    pw = groups * cg
    n_main = 4 * gw + pw
    in_cols = n_main + 2 * heads
    s_in = w["w_in3"].shape[0]

    tm = _pick(t, (512, 256, 128))
    tm_ln = _pick(t, (256, 128))
    tm_big = _pick(t, (1024, 512, 256, 128))
    tk = _pick(d, K_STEPS)

    w_in = jnp.concatenate([w["w_in3"][s] for s in range(s_in)], axis=1)
    w_main = jnp.concatenate([w_in[:, :4 * gw], w_in[:, 4 * gw + 2 * heads:]], axis=1)
    w_ba = jnp.pad(w_in[:, 4 * gw:4 * gw + 2 * heads], ((0, 0), (0, HEAD_DIM - 2 * heads)))
    x_bf = x.astype(BF16)
    mem_bf = mem.astype(BF16)

    proj = _plain("proj_main", x_bf, w_main, tm=tm_big, tn=_pick(n_main, (1024, 512, 256, 128)), tk=tk, out_dtype=F32)
    ea, dtb = _gate_vectors(w["a_log"], w["dt_bias"], heads)
    vec128 = lambda i, j: (0, 0)
    ba, bg = _matmul(
        "proj_gates", x_bf, w_ba, tm=tm, tn=HEAD_DIM, tk=tk,
        extra=[(ea, (1, HEAD_DIM), vec128), (dtb, (1, HEAD_DIM), vec128)],
        outs=[(jax.ShapeDtypeStruct((t, HEAD_DIM), F32), (tm, HEAD_DIM), _tile)] * 2,
        epilogue=_gates_epilogue(heads))
    qkv = _gdn_pre(proj, w["conv_w"], heads)
    o_gdn, states = _gdn_core(qkv, bg, heads)
    cat_g = _gdn_post(o_gdn, proj, 3 * gw, w["gdn_norm_w"], heads, tm)
    cat_p = _pool_forward(proj, 4 * gw, w["pool_w"], w["pool_scale"])
    cat = jnp.concatenate([cat_g, cat_p], axis=1)
    w = {**w, **(yield ("weights", 1, cat))}
    h1, h1_bf, xhat1, rstd1 = _ln_forward("mix_ln1", cat, w["w_out"], x, w["ln1_g"], w["ln1_b"], tm=tm_ln, tk=tk)

    tn_d = _pick(d, (1024, 512, 256, 128))
    q = _plain("xattn_q", h1_bf, w["xq_w"], tm=tm, tn=tn_d, tk=tk, out_dtype=BF16)
    mlen = mem.shape[0]
    tm_mem = _pick(mlen, (256, 128))
    k = _plain("xattn_k", mem_bf, w["xk_w"], tm=tm_mem, tn=tn_d, tk=tk, out_dtype=BF16)
    v = _plain("xattn_v", mem_bf, w["xv_w"], tm=tm_mem, tn=tn_d, tk=tk, out_dtype=BF16)
    att = _attention(q, k, v, tm)
    h2, h2_bf, xhat2, rstd2 = _ln_forward("xo_ln2", att, w["xo_w"], h1, w["ln2_g"], w["ln2_b"], tm=tm_ln, tk=tk)

    w = {**w, **(yield ("weights", 2, h2_bf))}
    s_up = w["w_up3"].shape[0]
    ff = s_up * w["w_up3"].shape[2]
    tn_f = _pick(ff // s_up, (1024, 512, 256, 128))

    def up_epi(acc, ex, out, i):
        r = jnp.maximum(acc, 0.0)
        out[0][...] = (r * r).astype(BF16)
        out[1][...] = (2.0 * r).astype(BF16)

    act, act_grad = _matmul(
        "mlp_up", h2_bf, w["w_up3"], b_blocks=s_up, tm=tm_big, tn=tn_f, tk=tk,
        outs=[(jax.ShapeDtypeStruct((t, ff), BF16), (tm_big, tn_f), _tile)] * 2, epilogue=up_epi)
    tk_f = _pick(ff, K_STEPS)
    xhat3, rstd3 = _ln_forward("down_ln3", act, w["w_down"], h2, w["ln3_g"], w["ln3_b"], tm=tm, tk=tk_f, want_h=False)

    grads = {}
    du3, du3_bf, grads["ln3_g"], grads["ln3_b"], loss = _loss_and_ln_backward(
        xhat3, rstd3, w["ln3_g"], w["ln3_b"], target, tm_ln)

    def dup_epi(acc, ex, out, i):
        out[0][...] = (acc * ex[0][...].astype(F32)).astype(BF16)

    dup = _matmul(
        "mlp_down_dx", du3_bf, w["w_down"], tb=True, tm=tm_big, tn=tn_f, tk=tk,
        extra=[(act_grad, (tm_big, tn_f), _tile)],
        outs=[(jax.ShapeDtypeStruct((t, ff), BF16), (tm_big, tn_f), _tile)], epilogue=dup_epi)[0]
    tk_t = _pick(t, K_STEPS)
    tm_w = _pick(d, (512, 256, 128))
    grads["w_down"] = _plain("mlp_down_dw", act, du3_bf, ta=True, tm=_pick(ff, (512, 256, 128)), tn=tn_d, tk=tk_t,
                             out_dtype=F32)
    grads["w_up3"] = _plain("mlp_up_dw", h2_bf, dup, ta=True, tm=tm_w, tn=tn_f, tk=tk_t, out_dtype=F32, out3=s_up)
    token = yield ("grads", 0, {n: grads.pop(n) for n in ("w_down", "w_up3")})
    du2, du2_bf, grads["ln2_g"], grads["ln2_b"] = _ln_backward(
        "mlp_up_dx_ln2", dup, w["w_up3"], du3, xhat2, rstd2, _after(token, w["ln2_g"]), tm=tm,
        tk=_pick(ff // s_up, K_STEPS[1:]), b_blocks=s_up)
    token = yield ("poll", 0, du2_bf)

    grads["xo_w"] = _plain("xo_dw", att, du2_bf, ta=True, tm=tm_w, tn=tn_d, tk=tk_t, out_dtype=F32)
    datt = _plain("xo_dx", du2_bf, w["xo_w"], tb=True, tm=tm, tn=tn_d, tk=tk, out_dtype=BF16)
    dq, dk, dv = _attention_backward(q, k, v, datt, tm)
    tk_m = _pick(mlen, (256, 128))
    grads["xq_w"] = _plain("xq_dw", h1_bf, dq, ta=True, tm=tm_w, tn=tn_d, tk=tk_t, out_dtype=F32)
    grads["xk_w"] = _plain("xk_dw", mem_bf, dk, ta=True, tm=tm_w, tn=tn_d, tk=tk_m, out_dtype=F32)
    grads["xv_w"] = _plain("xv_dw", mem_bf, dv, ta=True, tm=tm_w, tn=tn_d, tk=tk_m, out_dtype=F32)
    du1, du1_bf, grads["ln1_g"], grads["ln1_b"] = _ln_backward(
        "xq_dx_ln1", dq, w["xq_w"], du2, xhat1, rstd1, _after(token, w["ln1_g"]), tm=tm_ln, tk=tk)

    grads["w_out"] = _plain("out_dw", cat, du1_bf, ta=True, tm=tm_w, tn=tn_d, tk=tk_t, out_dtype=F32)
    token = yield ("grads", 1, {n: grads.pop(n) for n in ("xo_w", "xq_w", "xk_w", "xv_w", "w_out")})
    dcat = _plain("out_dx", du1_bf, w["w_out"], tb=True, tm=tm, tn=tn_d, tk=tk, out_dtype=F32)
    dp, grads["pool_w"], grads["pool_scale"] = _pool_backward(dcat, gw, proj, 4 * gw, w["pool_w"],
                                                              _after(token, w["pool_scale"]))
    do_gdn, dz, grads["gdn_norm_w"] = _gdn_post_backward(dcat, o_gdn, proj, 3 * gw, _after(token, w["gdn_norm_w"]),
                                                         heads, tm)
    token = yield ("poll", 1, do_gdn)
    dqkv, dbg = _gdn_core_backward(qkv, _after(token, bg), states, do_gdn, heads)
    dqkv_pre, grads["conv_w"] = _gdn_pre_backward(proj, w["conv_w"], dqkv, heads)
    dba, dalog_row, ddt_row = _gates_backward(ba, bg, dbg, ea, dtb, heads)
    grads["a_log"] = dalog_row[:, heads:2 * heads]
    grads["dt_bias"] = ddt_row[:, heads:2 * heads]

    dproj = jnp.concatenate([dqkv_pre, dz, dp], axis=1)
    tn_main = _pick(n_main, (1024, 512, 256, 128))
    dw_main = _plain("proj_dw", x_bf, dproj, ta=True, tm=tm_w, tn=tn_main, tk=tk_t, out_dtype=F32)
    dw_ba = _plain("proj_gates_dw", x_bf, dba, ta=True, tm=tm_w, tn=HEAD_DIM, tk=tk_t, out_dtype=F32)
    dw_in = jnp.concatenate([dw_main[:, :4 * gw], dw_ba[:, :2 * heads], dw_main[:, 4 * gw:]], axis=1)
    per = in_cols // s_in
    grads["w_in3"] = jnp.stack([dw_in[:, s * per:(s + 1) * per] for s in range(s_in)], axis=0)

    def dx_epi(acc, ex, out, i):
        out[0][...] = acc + ex[1][...] + ALPHA * ex[0][...]

    token = yield ("grads", 2, {n: grads.pop(n) for n in ("w_in3", "pool_w")})
    dx_gates = _plain("proj_gates_dx", dba, _after(token, w_ba), tb=True, tm=tm, tn=tn_d, tk=HEAD_DIM, out_dtype=F32)
    grad_x = _matmul(
        "proj_dx", dproj, w_main, tb=True, tm=tm, tn=tn_d, tk=_pick(n_main, (2560,) + K_STEPS),
        extra=[(du1, (tm, tn_d), _tile), (dx_gates, (tm, tn_d), _tile)],
        outs=[(jax.ShapeDtypeStruct((t, d), F32), (tm, tn_d), _tile)], epilogue=dx_epi)[0]
    yield ("poll", 2, grad_x)
    return loss, grad_x, grads


def _adamw(name, w, g, m, v):
    r, c = w.shape
    tr = _pick(r, (256, 128, 64, 32, 16, 8))
    c1 = 1.0 - ADAM_B1 ** ADAM_STEP
    c2 = 1.0 - ADAM_B2 ** ADAM_STEP

    def body(w_ref, g_ref, m_ref, v_ref, d_ref, mo_ref, vo_ref):
        gv = g_ref[...]
        mn = ADAM_B1 * m_ref[...] + (1.0 - ADAM_B1) * gv
        vn = ADAM_B2 * v_ref[...] + (1.0 - ADAM_B2) * (gv * gv)
        d_ref[...] = -ADAM_LR * ((mn / c1) / (jnp.sqrt(vn / c2) + ADAM_EPS) + ADAM_WD * w_ref[...])
        mo_ref[...] = mn
        vo_ref[...] = vn

    blk = pl.BlockSpec((tr, c), lambda i: (i, 0))
    return pl.pallas_call(
        body, name=name, grid=(r // tr,), in_specs=[blk] * 4, out_specs=[blk] * 3,
        out_shape=[jax.ShapeDtypeStruct((r, c), F32)] * 3,
        compiler_params=_params("parallel"),
    )(w, g, m, v)


def _place():
    x, y, c = lax.axis_index("x"), lax.axis_index("y"), lax.axis_index("c")
    chips = [(1 - x, y), (x, 1 - y), (1 - x, 1 - y)]
    return x, y, c, chips


HBM = pl.BlockSpec(memory_space=pltpu.HBM)


SEM = pl.BlockSpec(memory_space=pltpu.SEMAPHORE)
ANY = pl.BlockSpec(memory_space=pl.ANY)
EFFECT = pltpu.SideEffectType.DATAFLOW_SIDE_EFFECTING


def _in_hbm(a):
    return pltpu.with_memory_space_constraint(a, pltpu.HBM)


def _remote(src, dst, send_sem, recv_sem, to):
    return pltpu.make_async_remote_copy(src_ref=src, dst_ref=dst, send_sem=send_sem, recv_sem=recv_sem,
                                        device_id=to, device_id_type=MESH)


def _landed(lands, i, shard_index, which):
    rows = lands[i].shape[1] // 2
    return lands[i].at[shard_index, pl.ds(which * rows, rows)]


def _gather_start(shards, after):
    n = len(shards)
    lands = [lax.empty((N_SHARD,) + s.shape, s.dtype) for s in shards]

    def body(*refs):
        ins, zones = refs[:n], refs[n:2 * n]
        ici_send, ici_recv, own_send, own_recv = refs[2 * n + 1:2 * n + 5]
        token = refs[-1]
        x, y, c, chips = _place()
        me = 2 * x + y
        for i in range(n):
            rows = ins[i].shape[0] // 2
            for j, chip in enumerate(chips):
                _remote(ins[i].at[pl.ds(c * rows, rows)], _landed(zones, i, me, c), ici_send.at[3 * i + j],
                        ici_recv.at[3 * i + j], (*chip, c)).start()
        for i in range(n):
            _remote(ins[i], zones[i].at[me], own_send.at[i], own_recv.at[i], (x, y, 1 - c)).start()
        token[...] = jnp.zeros_like(token)

    dma = pltpu.SemaphoreType.DMA
    outs = pl.pallas_call(
        body, name="gather_start",
        in_specs=[HBM] * (2 * n) + [ANY],
        out_shape=(dma((3 * n,)), dma((3 * n,)), dma((n,)), dma((n,)),
                   *[pltpu.HBM(a.shape, a.dtype) for a in shards + lands], jax.ShapeDtypeStruct((8, LANES), F32)),
        out_specs=(SEM, SEM, SEM, SEM, *[HBM] * (2 * n), pl.BlockSpec(memory_space=pltpu.VMEM)),
        input_output_aliases={k: 4 + k for k in range(2 * n)},
        compiler_params=pltpu.CompilerParams(has_side_effects=EFFECT),
    )(*[_in_hbm(a) for a in shards + lands], after)
    sems = dict(zip(("ici_send", "ici_recv", "own_send", "own_recv"), outs[:4]))
    return sems, list(outs[4:4 + n]), list(outs[4 + n:4 + 2 * n]), outs[-1]


def _gather_forward(name, idx, lands, sems, after):
    n = len(idx)

    def body(*refs):
        zones = refs[:n]
        ici_recv = refs[n]
        fwd_send, fwd_recv = refs[n + 2], refs[n + 3]
        x, y, c, chips = _place()
        for k, i in enumerate(idx):
            for j, chip in enumerate(chips):
                half = _landed(zones, k, 2 * chip[0] + chip[1], c)
                _remote(half, half, fwd_send.at[3 * k + j], ici_recv.at[3 * i + j], (*chip, c)).wait_recv()
                _remote(half, half, fwd_send.at[3 * k + j], fwd_recv.at[3 * k + j], (x, y, 1 - c)).start()

    dma = pltpu.SemaphoreType.DMA
    outs = pl.pallas_call(
        body, name=name,
        in_specs=[HBM] * n + [SEM, ANY],
        out_shape=(dma((3 * n,)), dma((3 * n,)), *[pltpu.HBM(a.shape, a.dtype) for a in lands]),
        out_specs=(SEM, SEM, *[HBM] * n),
        input_output_aliases={k: 2 + k for k in range(n)},
        compiler_params=pltpu.CompilerParams(has_side_effects=EFFECT),
    )(*lands, sems["ici_recv"], after)
    return (outs[0], outs[1]), list(outs[2:])


def _gather_wait(name, idx, shards, lands, sems, fwd):
    n = len(idx)

    def body(*refs):
        ins, zones = refs[:n], refs[n:2 * n]
        ici_send, own_send, own_recv, fwd_send, fwd_recv = refs[2 * n:2 * n + 5]
        x, y, c, chips = _place()
        me = 2 * x + y
        for k, i in enumerate(idx):
            rows = ins[k].shape[0] // 2
            mine = ins[k].at[pl.ds(c * rows, rows)]
            for j, chip in enumerate(chips):
                theirs = 2 * chip[0] + chip[1]
                _remote(mine, _landed(zones, k, me, c), ici_send.at[3 * i + j], fwd_recv.at[3 * k + j],
                        (*chip, c)).wait_send()
                sent = _landed(zones, k, theirs, c)
                _remote(sent, sent, fwd_send.at[3 * k + j], fwd_recv.at[3 * k + j], (x, y, 1 - c)).wait_send()
                passed = _landed(zones, k, theirs, 1 - c)
                _remote(passed, passed, fwd_send.at[3 * k + j], fwd_recv.at[3 * k + j], (x, y, 1 - c)).wait_recv()
            own = _remote(ins[k], zones[k].at[me], own_send.at[i], own_recv.at[i], (x, y, 1 - c))
            own.wait_send()
            own.wait_recv()

    outs = pl.pallas_call(
        body, name=name,
        in_specs=[HBM] * (2 * n) + [SEM] * 5,
        out_shape=tuple(pltpu.HBM(a.shape, a.dtype) for a in lands),
        out_specs=tuple([HBM] * n),
        input_output_aliases={n + k: k for k in range(n)},
        compiler_params=pltpu.CompilerParams(has_side_effects=EFFECT),
    )(*shards, *lands, sems["ici_send"], sems["own_send"], sems["own_recv"], fwd[0], fwd[1])
    return list(outs)


def _all_reduce_small(name, slab, after=None):
    r, width = slab.shape
    ndev = 8

    def body(x_ref, after_ref, out_ref, buf, send_sems, recv_sems):
        x, y, c, _ = _place()
        me = 4 * x + 2 * y + c
        buf[me] = x_ref[...]
        copies = []
        for k in range(1, ndev):
            peer = jnp.bitwise_xor(me, k)
            to = (peer // 4, (peer // 2) % 2, peer % 2)
            cp = pltpu.make_async_remote_copy(src_ref=x_ref, dst_ref=buf.at[me], send_sem=send_sems.at[k - 1],
                                              recv_sem=recv_sems.at[k - 1], device_id=to, device_id_type=MESH)
            cp.start()
            copies.append(cp)
        for k in range(1, ndev):
            peer = jnp.bitwise_xor(me, k)
            pltpu.make_async_remote_copy(src_ref=x_ref, dst_ref=buf.at[peer], send_sem=send_sems.at[k - 1],
                                         recv_sem=recv_sems.at[k - 1], device_id=(x, y, c),
                                         device_id_type=MESH).wait_recv()
        for cp in copies:
            cp.wait_send()
        total = buf[0]
        for d in range(1, ndev):
            total = total + buf[d]
        out_ref[...] = total

    return pl.pallas_call(
        body, name=name,
        in_specs=[pl.BlockSpec(memory_space=pltpu.VMEM), ANY], out_specs=pl.BlockSpec(memory_space=pltpu.VMEM),
        out_shape=jax.ShapeDtypeStruct((r, width), F32),
        scratch_shapes=[pltpu.VMEM((ndev, r, width), F32), pltpu.SemaphoreType.DMA((ndev - 1,)),
                        pltpu.SemaphoreType.DMA((ndev - 1,))],
        compiler_params=pltpu.CompilerParams(vmem_limit_bytes=VMEM_LIMIT),
    )(slab, slab if after is None else after)


def _chip_partial(name, grad, other, core):
    s, r, cdim = grad.shape
    rows = r // 2
    tr = _pick(rows, (256, 128, 64, 32, 16))
    nb = rows // tr

    def body(core_ref, g_ref, o_ref, out_ref):
        out_ref[...] = (g_ref[...] + o_ref[...]).astype(BF16)

    return pl.pallas_call(
        body, name=name,
        grid_spec=pltpu.PrefetchScalarGridSpec(
            num_scalar_prefetch=1, grid=(s, nb),
            in_specs=[pl.BlockSpec((None, tr, cdim), lambda j, b, core_ref: (j, core_ref[0] * nb + b, 0)),
                      pl.BlockSpec((None, tr, cdim), lambda j, b, core_ref: (j, b, 0))],
            out_specs=pl.BlockSpec((None, tr, cdim), lambda j, b, core_ref: (j, b, 0))),
        out_shape=jax.ShapeDtypeStruct((s, rows, cdim), BF16),
        compiler_params=_params("parallel", "parallel"),
    )(core, grad, other)


def _partial_copies(ins, zones, send_sems, recv_sems):
    x, y, c, chips = _place()
    return [_remote(ins[i].at[2 * chip[0] + chip[1]], zones[i].at[j], send_sems.at[3 * i + j],
                    recv_sems.at[3 * i + j], (*chip, c))
            for i in range(len(ins)) for j, chip in enumerate(chips)]


def _swap_copies(ins, zones, send_sems, recv_sems):
    x, y, c, _ = _place()
    copies = []
    for i in range(len(ins)):
        rows = ins[i].shape[1] // 2
        for s in range(N_SHARD):
            copies.append(_remote(ins[i].at[s, pl.ds((1 - c) * rows, rows)], zones[i].at[s],
                                  send_sems.at[N_SHARD * i + s], recv_sems.at[N_SHARD * i + s], (x, y, 1 - c)))
    return copies


def _exchange_start(name, plan, sources, lands, per_array):
    n = len(sources)
    lands = [lax.empty(shape, dtype) for shape, dtype in lands]

    def body(*refs):
        for cp in plan(refs[:n], refs[n:2 * n], refs[2 * n], refs[2 * n + 1]):
            cp.start()
        refs[-1][...] = jnp.zeros_like(refs[-1])

    dma = pltpu.SemaphoreType.DMA
    outs = pl.pallas_call(
        body, name=name,
        in_specs=[HBM] * (2 * n),
        out_shape=(dma((per_array * n,)), dma((per_array * n,)),
                   *[pltpu.HBM(a.shape, a.dtype) for a in list(sources) + lands], jax.ShapeDtypeStruct((8, LANES), F32)),
        out_specs=(SEM, SEM, *[HBM] * (2 * n), pl.BlockSpec(memory_space=pltpu.VMEM)),
        input_output_aliases={k: 2 + k for k in range(2 * n)},
        compiler_params=pltpu.CompilerParams(has_side_effects=EFFECT),
    )(*[_in_hbm(a) for a in list(sources) + lands])
    return (outs[0], outs[1]), list(outs[2:2 + n]), list(outs[2 + n:2 + 2 * n]), outs[-1]


def _exchange_wait(name, plan, started, after):
    sems, partials, lands, _ = started
    n = len(partials)

    def body(*refs):
        for cp in plan(refs[:n], refs[n:2 * n], refs[2 * n], refs[2 * n + 1]):
            cp.wait_send()
            cp.wait_recv()

    outs = pl.pallas_call(
        body, name=name,
        in_specs=[HBM] * (2 * n) + [SEM, SEM] + [ANY] * len(after),
        out_shape=tuple(pltpu.HBM(a.shape, a.dtype) for a in lands),
        out_specs=tuple([HBM] * n),
        input_output_aliases={n + k: k for k in range(n)},
        compiler_params=pltpu.CompilerParams(has_side_effects=EFFECT),
    )(*partials, *lands, sems[0], sems[1], *after)
    return list(outs)


def _reduce_own(name, grad, other, received, where):
    s, r, cdim = grad.shape
    rows = r // 2
    tr = _pick(rows, (256, 128, 64, 32, 16))
    nb = rows // tr

    def body(where_ref, g_ref, o_ref, r_ref, out_ref):
        total = g_ref[...] + o_ref[...]
        for j in range(3):
            total = total + r_ref[j].astype(F32)
        out_ref[...] = total

    return pl.pallas_call(
        body, name=name,
        grid_spec=pltpu.PrefetchScalarGridSpec(
            num_scalar_prefetch=1, grid=(nb,),
            in_specs=[pl.BlockSpec((None, tr, cdim), lambda b, w_ref: (w_ref[0], w_ref[1] * nb + b, 0)),
                      pl.BlockSpec((None, tr, cdim), lambda b, w_ref: (w_ref[0], b, 0)),
                      pl.BlockSpec((3, tr, cdim), lambda b, w_ref: (0, b, 0))],
            out_specs=pl.BlockSpec((tr, cdim), lambda b, w_ref: (w_ref[1] * nb + b, 0))),
        out_shape=jax.ShapeDtypeStruct((r, cdim), F32),
        compiler_params=_params("parallel"),
    )(where, grad, other, received)


def _join_halves(name, halves):
    n = len(halves)

    def body(*refs):
        bufs = refs[n:2 * n]
        send_sems, recv_sems = refs[2 * n:]
        x, y, c, _ = _place()
        copies = []
        for i in range(n):
            rows = bufs[i].shape[0] // 2
            mine = bufs[i].at[pl.ds(c * rows, rows)]
            cp = pltpu.make_async_remote_copy(src_ref=mine, dst_ref=mine, send_sem=send_sems.at[i],
                                              recv_sem=recv_sems.at[i], device_id=(x, y, 1 - c), device_id_type=MESH)
            cp.start()
            copies.append(cp)
        for i, cp in enumerate(copies):
            rows = bufs[i].shape[0] // 2
            theirs = bufs[i].at[pl.ds((1 - c) * rows, rows)]
            pltpu.make_async_remote_copy(src_ref=theirs, dst_ref=theirs, send_sem=send_sems.at[i],
                                         recv_sem=recv_sems.at[i], device_id=(x, y, 1 - c),
                                         device_id_type=MESH).wait_recv()
            cp.wait_send()

    return pl.pallas_call(
        body, name=name,
        in_specs=[HBM] * n, out_specs=[HBM] * n,
        out_shape=[jax.ShapeDtypeStruct(h.shape, F32) for h in halves],
        input_output_aliases={i: i for i in range(n)},
        scratch_shapes=[pltpu.SemaphoreType.DMA((n,)), pltpu.SemaphoreType.DMA((n,))],
    )(*halves)


BIG = ("w_in", "pool_w", "w_out", "xq_w", "xk_w", "xv_w", "xo_w", "w_up", "w_down")
GATHER_GROUPS = ((0, 1), (2, 3, 4, 5, 6), (7, 8))
SMALL = ("conv_w", "a_log", "dt_bias", "gdn_norm_w", "pool_scale", "ln1_g", "ln1_b", "ln2_g", "ln2_b", "ln3_g", "ln3_b")
ORDER = ("w_in", "conv_w", "a_log", "dt_bias", "gdn_norm_w", "pool_w", "pool_scale", "w_out", "ln1_g", "ln1_b",
         "xq_w", "xk_w", "xv_w", "xo_w", "ln2_g", "ln2_b", "w_up", "w_down", "ln3_g", "ln3_b")
LANES = 128


def _rows(flat_len):
    return -(-flat_len // LANES)


def _pack(pieces):
    out = []
    for p in pieces:
        flat = p.reshape(-1).astype(F32)
        out.append(jnp.pad(flat, (0, _rows(flat.shape[0]) * LANES - flat.shape[0])).reshape(-1, LANES))
    slab = jnp.concatenate(out, axis=0)
    return jnp.pad(slab, ((0, -slab.shape[0] % 8), (0, 0)))


def _unpack(slab, shapes):
    out, row = [], 0
    for shp in shapes:
        size = math.prod(shp)
        out.append(slab[row:row + _rows(size)].reshape(-1)[:size].reshape(shp))
        row += _rows(size)
    return out


def _as2d(a):
    a = a[0]
    return a.reshape(-1, a.shape[-1]) if a.ndim == 3 else a


def kernel(x, mem, w_in, conv_w, a_log, dt_bias, gdn_norm_w, pool_w, pool_scale, w_out, ln1_g, ln1_b, xq_w, xk_w, xv_w, xo_w, ln2_g, ln2_b, w_up, w_down, ln3_g, ln3_b, loss_target, m_w_in, m_conv_w, m_a_log, m_dt_bias, m_gdn_norm_w, m_pool_w, m_pool_scale, m_w_out, m_ln1_g, m_ln1_b, m_xq_w, m_xk_w, m_xv_w, m_xo_w, m_ln2_g, m_ln2_b, m_w_up, m_w_down, m_ln3_g, m_ln3_b, v_w_in, v_conv_w, v_a_log, v_dt_bias, v_gdn_norm_w, v_pool_w, v_pool_scale, v_w_out, v_ln1_g, v_ln1_b, v_xq_w, v_xk_w, v_xv_w, v_xo_w, v_ln2_g, v_ln2_b, v_w_up, v_w_down, v_ln3_g, v_ln3_b):
    given = dict(locals())
    cx, cy, cc = lax.axis_index("x"), lax.axis_index("y"), lax.axis_index("c")
    me = 2 * cx + cy
    groups = pool_w.shape[1]
    cs = pool_w.shape[2]
    kk, conv_cols = conv_w.shape[1], conv_w.shape[2]
    core = cc.astype(jnp.int32).reshape(1)
    where = jnp.stack([me, cc]).astype(jnp.int32)

    conv_slab = jnp.zeros((kk, N_SHARD * conv_cols), F32)
    conv_slab = lax.dynamic_update_slice(conv_slab, conv_w[0] * (cc == 0).astype(F32), (0, me * conv_cols))
    wts = {"conv_w": _unpack(_all_reduce_small("gather_conv_w", _pack([conv_slab])), [conv_slab.shape])[0]}

    sems, shards, lands, token = _gather_start([_as2d(given[n]).astype(BF16) for n in BIG], wts["conv_w"])

    def fetch(group, after):
        idx = GATHER_GROUPS[group]
        fwd, zones = _gather_forward(f"gather_forward_{group}", idx, [lands[i] for i in idx], sems, after)
        full = dict(zip([BIG[i] for i in idx],
                        _gather_wait(f"gather_wait_{group}", idx, [shards[i] for i in idx], zones, sems, fwd)))
        out = {}
        for n, a in full.items():
            if n == "w_in":
                out["w_in3"] = a
            elif n == "w_up":
                out["w_up3"] = a
            elif n == "pool_w":
                out[n] = a.reshape(N_SHARD, groups, cs, -1).transpose(1, 0, 2, 3).reshape(groups, N_SHARD * cs, -1)
            else:
                out[n] = a.reshape(-1, a.shape[-1])
        return out

    for n in ("a_log", "dt_bias", "gdn_norm_w", "pool_scale", "ln1_g", "ln1_b", "ln2_g", "ln2_b", "ln3_g", "ln3_b"):
        wts[n] = given[n]
    wts.update(fetch(0, token))

    def start_swap(group, grads):
        names, blocks = [], []
        for n, g in grads.items():
            if n == "pool_w":
                g = g.reshape(groups, N_SHARD, cs, -1).transpose(1, 0, 2, 3).reshape(N_SHARD, groups * cs, -1)
            elif g.ndim == 2:
                g = g.reshape(N_SHARD, -1, g.shape[-1])
            names.append({"w_in3": "w_in", "w_up3": "w_up"}.get(n, n))
            blocks.append(g)
        zones = [((N_SHARD, b.shape[1] // 2, b.shape[2]), F32) for b in blocks]
        swap = _exchange_start(f"grad_swap_start_{group}", _swap_copies, blocks, zones, N_SHARD)
        return {"group": group, "names": names, "swap": swap, "token": swap[3]}

    def start_send(state, after):
        group, names = state["group"], state["names"]
        state["blocks"] = state["swap"][1]
        state["others"] = _exchange_wait(f"grad_swap_wait_{group}", _swap_copies, state["swap"], after)
        partials = [_chip_partial("chip_partial_" + n, gb, ob, core)
                    for n, gb, ob in zip(names, state["blocks"], state["others"])]
        zones = [((3,) + p.shape[1:], BF16) for p in partials]
        state["send"] = _exchange_start(f"grad_send_start_{group}", _partial_copies, partials, zones, 3)
        state["token"] = state["send"][3]

    grad, delta, new_m, new_v = {}, {}, {}, {}

    def finish_reduce(state, after):
        group, names = state["group"], state["names"]
        received = _exchange_wait(f"grad_send_wait_{group}", _partial_copies, state["send"], after)
        halves = [_reduce_own("reduce_own_" + n, gb, ob, rb, where)
                  for n, gb, ob, rb in zip(names, state["blocks"], state["others"], received)]
        for n, g in zip(names, _join_halves(f"grad_join_{group}", halves)):
            shp = given[n].shape
            d2, m2, v2 = _adamw("adamw_" + n, _as2d(given[n]), g, _as2d(given["m_" + n]), _as2d(given["v_" + n]))
            grad[n], delta[n], new_m[n], new_v[n] = (a.reshape(shp) for a in (g, d2, m2, v2))
        return d2

    step = _local_step(x[0], mem[0], loss_target[0], wts)
    pending = {}
    request = next(step)
    while True:
        try:
            kind, group, payload = request
            if kind == "weights":
                request = step.send(fetch(group, payload))
            elif kind == "grads":
                pending[group] = start_swap(group, payload)
                request = step.send(pending[group]["token"])
            else:
                start_send(pending[group], [payload])
                request = step.send(pending[group]["token"])
        except StopIteration as stop:
            loss_row, grad_x, g = stop.value
            break

    after = [pending[2]["token"], grad_x]
    for group in sorted(pending):
        after = [finish_reduce(pending[group], after)]

    small_names = ("a_log", "dt_bias", "gdn_norm_w", "pool_scale", "ln1_g", "ln1_b", "ln2_g", "ln2_b", "ln3_g", "ln3_b")
    pieces = [g["conv_w"]] + [g[n] for n in small_names] + [loss_row[:, :1]]
    shapes = [p.shape for p in pieces]
    summed = _unpack(_all_reduce_small("all_reduce_small", _pack(pieces), after[0]), shapes)
    gsmall = dict(zip(small_names, summed[1:-1]))
    gsmall["conv_w"] = lax.dynamic_slice(summed[0], (0, me * conv_cols), (kk, conv_cols))
    loss = summed[-1][0, 0]

    sshapes = [given[n].shape for n in SMALL]
    slabs = [_pack([given[p + n] for n in SMALL]) for p in ("", "m_", "v_")]
    gslab = _pack([gsmall[n] for n in SMALL])
    outs = _adamw("adamw_small", slabs[0], gslab, slabs[1], slabs[2])
    for dst, slab in zip((delta, new_m, new_v), outs):
        dst.update(zip(SMALL, _unpack(slab, sshapes)))
    for n in SMALL:
        grad[n] = gsmall[n].reshape(given[n].shape)

    return (loss, grad_x[None], *[grad[n] for n in ORDER], *[delta[n] for n in ORDER],
            *[new_m[n] for n in ORDER], *[new_v[n] for n in ORDER])
```

```python
import functools
import math

import jax
import jax.numpy as jnp
from jax import lax
from jax.experimental import pallas as pl
from jax.experimental.pallas import tpu as pltpu

F32 = jnp.float32
BF16 = jnp.bfloat16
MESH = pl.DeviceIdType.MESH

HEAD_DIM = 128
CHUNK = 64
POOL_WINDOWS = (2, 4, 8, 16)
XATTN_HEADS = 4
ALPHA = 2.0 ** 0.25
LN_EPS = 1e-5
NORM_EPS = 1e-6
ADAM_LR, ADAM_B1, ADAM_B2, ADAM_EPS, ADAM_WD, ADAM_STEP = 0.001, 0.9, 0.999, 1e-08, 0.01, 10
N_SHARD = 4
VMEM_LIMIT = 56 * 1024 * 1024
K_STEPS = (2048, 1024, 512, 256, 128)


def _params(*sem):
    return pltpu.CompilerParams(dimension_semantics=sem, vmem_limit_bytes=VMEM_LIMIT)


def _bdot(a, b, ta=False, tb=False):
    dims = (((0 if ta else 1,), (1 if tb else 0,)), ((), ()))
    return lax.dot_general(a.astype(BF16), b.astype(BF16), dims, preferred_element_type=F32)


def _sigmoid(x):
    return 1.0 / (1.0 + jnp.exp(-x))


def _matmul(name, a, b, *, ta=False, tb=False, tm, tn, tk, extra=(), outs, epilogue, b_blocks=None,
            sequential=False, n_used=None, k_used=None):
    m, k_dim = (a.shape[1], a.shape[0]) if ta else a.shape
    if b_blocks and tb:
        n = b.shape[1]
        k_dim = b.shape[0] * b.shape[2]
        per = b.shape[2] // tk
        b_spec = pl.BlockSpec((None, tn, tk), lambda i, j, k: (k // per, j, k % per))
    elif b_blocks:
        n = b.shape[0] * b.shape[2]
        per = b.shape[2] // tn
        b_spec = pl.BlockSpec((None, tk, tn), lambda i, j, k: (j // per, k, j % per))
    elif tb:
        n = b.shape[0]
        b_spec = pl.BlockSpec((tn, tk), lambda i, j, k: (j, k))
    else:
        n = b.shape[1]
        b_spec = pl.BlockSpec((tk, tn), lambda i, j, k: (k, j))
    n, k_dim = n_used or n, k_used or k_dim
    assert m % tm == 0 and n % tn == 0 and k_dim % tk == 0, (name, m, n, k_dim, tm, tn, tk)
    nk = k_dim // tk
    a_spec = pl.BlockSpec((tk, tm), lambda i, j, k: (k, i)) if ta else pl.BlockSpec((tm, tk), lambda i, j, k: (i, k))
    n_extra, n_out = len(extra), len(outs)

    def wrap(index_map):
        return lambda i, j, k: index_map(i, j)

    def body_one_step(*refs):
        ex = refs[2:2 + n_extra]
        out = refs[2 + n_extra:2 + n_extra + n_out]
        epilogue(_bdot(refs[0][...], refs[1][...], ta, tb), ex, out, pl.program_id(0))

    def body(*refs):
        a_ref, b_ref = refs[0], refs[1]
        ex = refs[2:2 + n_extra]
        out = refs[2 + n_extra:2 + n_extra + n_out]
        acc = refs[-1]
        i, k = pl.program_id(0), pl.program_id(2)
        part = _bdot(a_ref[...], b_ref[...], ta, tb)

        @pl.when(k == 0)
        def _():
            acc[...] = part

        @pl.when(jnp.logical_and(k > 0, k < nk - 1))
        def _():
            acc[...] += part

        @pl.when(k == nk - 1)
        def _():
            epilogue(acc[...] + part, ex, out, i)

    sem = ("arbitrary",) * 3 if sequential else ("parallel", "parallel", "arbitrary")
    res = pl.pallas_call(
        body_one_step if nk == 1 else body, name=name, grid=(m // tm, n // tn, nk),
        in_specs=[a_spec, b_spec] + [pl.BlockSpec(bs, wrap(im)) for _, bs, im in extra],
        out_specs=[pl.BlockSpec(bs, wrap(im)) for _, bs, im in outs],
        out_shape=[s for s, _, _ in outs],
        scratch_shapes=[] if nk == 1 else [pltpu.VMEM((tm, tn), F32)],
        compiler_params=_params(*sem),
    )(a, b, *[x for x, _, _ in extra])
    return res


def _tile(i, j):
    return (i, j)


def _plain(name, a, b, *, ta=False, tb=False, tm, tn, tk, out_dtype, b_blocks=None, out3=None, n_used=None):
    m = a.shape[1] if ta else a.shape[0]
    n = n_used or ((b.shape[0] * b.shape[2]) if b_blocks else (b.shape[0] if tb else b.shape[1]))

    def epi(acc, ex, out, i):
        out[0][...] = acc.astype(out_dtype)

    if out3:
        per = (n // out3) // tn
        spec = (jax.ShapeDtypeStruct((out3, m, n // out3), out_dtype), (None, tm, tn),
                lambda i, j: (j // per, i, j % per))
    else:
        spec = (jax.ShapeDtypeStruct((m, n), out_dtype), (tm, tn), _tile)
    return _matmul(name, a, b, ta=ta, tb=tb, tm=tm, tn=tn, tk=tk, outs=[spec], epilogue=epi,
                   b_blocks=b_blocks, n_used=n_used)[0]


def _ln_forward(name, a, b, res, gamma, beta, *, tm, tk, want_h=True):
    m, n = res.shape

    def epi(acc, ex, out, i):
        u = ALPHA * ex[0][...] + acc
        mu = jnp.mean(u, axis=-1, keepdims=True)
        xc = u - mu
        var = jnp.mean(xc * xc, axis=-1, keepdims=True)
        rstd = lax.rsqrt(var + LN_EPS)
        xhat = xc * rstd
        out[-2][...] = xhat
        out[-1][...] = rstd
        if want_h:
            h = xhat * ex[1][...] + ex[2][...]
            out[0][...] = h
            out[1][...] = h.astype(BF16)

    row = lambda i, j: (i, 0)
    vec = lambda i, j: (0, 0)
    outs = [(jax.ShapeDtypeStruct((m, n), F32), (tm, n), row), (jax.ShapeDtypeStruct((m, n), BF16), (tm, n), row),
            (jax.ShapeDtypeStruct((m, n), F32), (tm, n), row), (jax.ShapeDtypeStruct((m, 1), F32), (tm, 1), row)]
    return _matmul(
        name, a, b, tm=tm, tn=n, tk=tk,
        extra=[(res, (tm, n), row), (gamma, (1, n), vec), (beta, (1, n), vec)],
        outs=outs if want_h else outs[2:], epilogue=epi)


def _ln_backward_math(dy, xhat, rstd, gamma):
    dxhat = dy * gamma
    m1 = jnp.mean(dxhat, axis=-1, keepdims=True)
    m2 = jnp.mean(dxhat * xhat, axis=-1, keepdims=True)
    du = rstd * (dxhat - m1 - xhat * m2)
    return du, jnp.sum(dy * xhat, axis=0, keepdims=True), jnp.sum(dy, axis=0, keepdims=True)


def _ln_backward(name, a, b, dres, xhat, rstd, gamma, *, tm, tk, b_blocks=None, tb=True):
    m, n = dres.shape

    def epi(acc, ex, out, i):
        dy = acc + ALPHA * ex[0][...]
        du, dg, db = _ln_backward_math(dy, ex[1][...], ex[2][...], ex[3][...])
        out[0][...] = du
        out[1][...] = du.astype(BF16)
        first = i == 0

        @pl.when(first)
        def _():
            out[2][...] = dg
            out[3][...] = db

        @pl.when(jnp.logical_not(first))
        def _():
            out[2][...] += dg
            out[3][...] += db

    row = lambda i, j: (i, 0)
    vec = lambda i, j: (0, 0)
    return _matmul(
        name, a, b, tb=tb, tm=tm, tn=n, tk=tk, b_blocks=b_blocks, sequential=True,
        extra=[(dres, (tm, n), row), (xhat, (tm, n), row), (rstd, (tm, 1), row), (gamma, (1, n), vec)],
        outs=[(jax.ShapeDtypeStruct((m, n), F32), (tm, n), row),
              (jax.ShapeDtypeStruct((m, n), BF16), (tm, n), row),
              (jax.ShapeDtypeStruct((1, n), F32), (1, n), vec),
              (jax.ShapeDtypeStruct((1, n), F32), (1, n), vec)],
        epilogue=epi)


def _shift_down(x, k):
    row = lax.broadcasted_iota(jnp.int32, x.shape, 0)
    return jnp.where(row >= k, pltpu.roll(x, k, axis=0), 0.0)


def _shift_up(x, k):
    t = x.shape[0]
    row = lax.broadcasted_iota(jnp.int32, x.shape, 0)
    return jnp.where(row < t - k, pltpu.roll(x, t - k, axis=0), 0.0)


def _conv_silu_norm(x, w, normalise):
    kk = w.shape[0]
    c = x * w[kk - 1:kk, :]
    for j in range(kk - 1):
        c = c + _shift_down(x, kk - 1 - j) * w[j:j + 1, :]
    sg = _sigmoid(c)
    s = c * sg
    r = lax.rsqrt(jnp.sum(s * s, axis=-1, keepdims=True) + NORM_EPS)
    y = jnp.where(normalise, s * r, s)
    return c, sg, s, r, y


def _gdn_pre(proj, conv_w, heads):
    t = proj.shape[0]
    kk = conv_w.shape[0]

    def body(x_ref, w_ref, o_ref):
        normalise = pl.program_id(0) < 2
        o_ref[...] = _conv_silu_norm(x_ref[...], w_ref[...], normalise)[4]

    col = lambda s, h: (0, s * heads + h)
    return pl.pallas_call(
        body, name="gdn_pre", grid=(3, heads),
        in_specs=[pl.BlockSpec((t, HEAD_DIM), col), pl.BlockSpec((kk, HEAD_DIM), col)],
        out_specs=pl.BlockSpec((t, HEAD_DIM), col),
        out_shape=jax.ShapeDtypeStruct((t, 3 * heads * HEAD_DIM), F32),
        compiler_params=_params("parallel", "parallel"),
    )(proj, conv_w)


def _gdn_pre_backward(proj, conv_w, dqkv, heads):
    t = proj.shape[0]
    kk = conv_w.shape[0]

    def body(x_ref, w_ref, dy_ref, dx_ref, dw_ref):
        normalise = pl.program_id(0) < 2
        x = x_ref[...]
        w = w_ref[...]
        dy = dy_ref[...]
        c, sg, s, r, y = _conv_silu_norm(x, w, normalise)
        ds_norm = r * (dy - y * jnp.sum(dy * y, axis=-1, keepdims=True))
        ds = jnp.where(normalise, ds_norm, dy)
        dc = ds * (sg * (1.0 + c * (1.0 - sg)))
        dx = dc * w[kk - 1:kk, :]
        rows = [None] * kk
        rows[kk - 1] = jnp.sum(dc * x, axis=0, keepdims=True)
        for j in range(kk - 1):
            lag = kk - 1 - j
            dx = dx + _shift_up(dc, lag) * w[j:j + 1, :]
            rows[j] = jnp.sum(dc * _shift_down(x, lag), axis=0, keepdims=True)
        dx_ref[...] = dx.astype(BF16)
        dw_ref[...] = jnp.concatenate(rows, axis=0)

    col = lambda s, h: (0, s * heads + h)
    return pl.pallas_call(
        body, name="gdn_pre_bwd", grid=(3, heads),
        in_specs=[pl.BlockSpec((t, HEAD_DIM), col), pl.BlockSpec((kk, HEAD_DIM), col),
                  pl.BlockSpec((t, HEAD_DIM), col)],
        out_specs=[pl.BlockSpec((t, HEAD_DIM), col), pl.BlockSpec((kk, HEAD_DIM), col)],
        out_shape=[jax.ShapeDtypeStruct((t, 3 * heads * HEAD_DIM), BF16),
                   jax.ShapeDtypeStruct((kk, 3 * heads * HEAD_DIM), F32)],
        compiler_params=_params("parallel", "parallel"),
    )(proj, conv_w, dqkv)


def _gate_vectors(a_log, dt_bias, heads):
    pad = lambda v: jnp.pad(v.astype(F32), ((0, 0), (heads, HEAD_DIM - 2 * heads)))
    return pad(jnp.exp(a_log.astype(F32))), pad(dt_bias)


def _softplus(x):
    return jnp.maximum(x, 0.0) + jnp.log(1.0 + jnp.exp(-jnp.abs(x)))


def _gates_epilogue(heads):
    def epi(acc, ex, out, i):
        lane = lax.broadcasted_iota(jnp.int32, acc.shape, 1)
        beta = _sigmoid(acc)
        g = -ex[0][...] * _softplus(acc + ex[1][...])
        out[0][...] = acc
        out[1][...] = jnp.where(lane < heads, beta, jnp.where(lane < 2 * heads, g, 0.0))
    return epi


def _gates_backward(ba, bg, dbg, ea, dtb, heads):
    t = ba.shape[0]

    def body(ba_ref, bg_ref, d_ref, ea_ref, dt_ref, dba_ref, dal_ref, ddt_ref):
        lane = lax.broadcasted_iota(jnp.int32, (t, HEAD_DIM), 1)
        bgv = bg_ref[...]
        d = d_ref[...]
        db = d * bgv * (1.0 - bgv)
        da = -d * ea_ref[...] * _sigmoid(ba_ref[...] + dt_ref[...])
        is_g = jnp.logical_and(lane >= heads, lane < 2 * heads)
        dba = jnp.where(lane < heads, db, jnp.where(is_g, da, 0.0))
        dba_ref[...] = dba.astype(BF16)
        dal_ref[...] = jnp.sum(jnp.where(is_g, d * bgv, 0.0), axis=0, keepdims=True)
        ddt_ref[...] = jnp.sum(jnp.where(is_g, da, 0.0), axis=0, keepdims=True)

    full = pl.BlockSpec((t, HEAD_DIM), lambda: (0, 0))
    vec = pl.BlockSpec((1, HEAD_DIM), lambda: (0, 0))
    return pl.pallas_call(
        body, name="gates_bwd", grid=(),
        in_specs=[full, full, full, vec, vec], out_specs=[full, vec, vec],
        out_shape=[jax.ShapeDtypeStruct((t, HEAD_DIM), BF16), jax.ShapeDtypeStruct((1, HEAD_DIM), F32),
                   jax.ShapeDtypeStruct((1, HEAD_DIM), F32)],
        compiler_params=pltpu.CompilerParams(vmem_limit_bytes=VMEM_LIMIT),
    )(ba, bg, dbg, ea, dtb)


class _Chunk:
    pass


def _split2(x):
    hi = x.astype(BF16)
    return hi, (x - hi.astype(F32)).astype(BF16)


def _split3(x):
    hi = x.astype(BF16)
    rest = x - hi.astype(F32)
    mid = rest.astype(BF16)
    return hi, mid, (rest - mid.astype(F32)).astype(BF16)


def _dot_mask(mask, x, ta=False):
    hi, mid, lo = _split3(x)
    return _bdot(mask, hi, ta=ta) + (_bdot(mask, mid, ta=ta) + _bdot(mask, lo, ta=ta))


def _transpose_by_identity(x):
    r = x.shape[0]
    eye = (lax.broadcasted_iota(jnp.int32, (r, r), 0) == lax.broadcasted_iota(jnp.int32, (r, r), 1)).astype(BF16)
    hi, mid, lo = _split3(x)
    return _bdot(hi, eye, ta=True) + (_bdot(mid, eye, ta=True) + _bdot(lo, eye, ta=True))


def _dot22(a, b, ta=False, tb=False):
    ah, al = _split2(a)
    bh, bl = _split2(b)
    return _bdot(ah, bh, ta, tb) + (_bdot(ah, bl, ta, tb) + _bdot(al, bh, ta, tb))


def _chunk_gates(bg, heads):
    n = CHUNK
    row = lax.broadcasted_iota(jnp.int32, (n, n), 0)
    col = lax.broadcasted_iota(jnp.int32, (n, n), 1)
    lane = lax.broadcasted_iota(jnp.int32, bg.shape, 1)
    graw = jnp.where(jnp.logical_and(lane >= heads, lane < 2 * heads), bg, 0.0)
    gc = _dot_mask((row >= col).astype(BF16), graw)
    return gc, _transpose_by_identity(gc)


def _in_lockstep(generators):
    results = [None] * len(generators)
    live = list(enumerate(generators))
    while live:
        still = []
        for i, gen in live:
            try:
                next(gen)
                still.append((i, gen))
            except StopIteration as stop:
                results[i] = stop.value
        live = still
    return results


def _chunk_local(q, k, v, beta, gc, grow):
    c = _Chunk()
    n = CHUNK
    row = lax.broadcasted_iota(jnp.int32, (n, n), 0)
    col = lax.broadcasted_iota(jnp.int32, (n, n), 1)
    c.tri = row >= col
    c.strict = row > col
    eye = row == col
    c.gcb = jnp.broadcast_to(gc, (n, HEAD_DIM))
    c.decay = jnp.where(c.tri, jnp.exp(jnp.where(c.tri, gc - grow, 0.0)), 0.0)
    c.eg = jnp.exp(c.gcb)
    glast = c.gcb[n - 1:n, :]
    c.egl = jnp.exp(glast)
    c.ekl = jnp.exp(glast - c.gcb)
    c.beta = beta
    c.q = q * (HEAD_DIM ** -0.5)
    c.k = k
    c.v = v
    c.kb = k * beta
    c.vb = v * beta
    c.kg = c.kb * c.eg
    both = _bdot(jnp.concatenate([c.kb, c.q], axis=0), k, tb=True)
    yield
    c.L = jnp.where(c.strict, both[:n] * c.decay, 0.0)
    c.A = jnp.where(c.tri, both[n:] * c.decay, 0.0)
    x = -c.L
    tinv = eye.astype(F32) + x
    p = _dot22(x, x)
    yield
    for _ in range(int(math.log2(n)) - 2):
        both = _dot22(jnp.concatenate([p, tinv], axis=0), p)
        yield
        p, tinv = both[:n], tinv + both[n:]
    c.T = tinv + _dot22(tinv, p)
    yield
    tinv = c.T
    uw = _dot22(tinv, jnp.concatenate([c.vb, c.kg], axis=1))
    yield
    c.u, c.w = uw[:, :HEAD_DIM], uw[:, HEAD_DIM:]
    c.qg = c.q * c.eg
    c.kdec = k * c.ekl
    return c


def _gdn_core(qkv, bg, heads):
    t = qkv.shape[0]
    nchunk = t // CHUNK

    gw = heads * HEAD_DIM

    def body(qkv_ref, bg_ref, o_ref, s_ref, state):
        @pl.when(pl.program_id(0) == 0)
        def _():
            state[...] = jnp.zeros_like(state)

        bg_v = bg_ref[...]
        gc_all, gc_rows = _chunk_gates(bg_v, heads)
        def one_head(h):
            col = lambda s: pl.ds(s * gw + h * HEAD_DIM, HEAD_DIM)
            c = yield from _chunk_local(qkv_ref[:, col(0)], qkv_ref[:, col(1)], qkv_ref[:, col(2)], bg_v[:, h:h + 1],
                                        gc_all[:, heads + h:heads + h + 1], gc_rows[heads + h:heads + h + 1, :])
            s0 = state[h]
            v_new = c.u - _bdot(c.w, s0)
            yield
            o = _bdot(c.qg, s0) + _bdot(c.A, v_new)
            return s0, o, s0 * c.egl + _bdot(c.kdec, v_new, ta=True)

        results = _in_lockstep([one_head(h) for h in range(heads)])
        for h, (s0, o, s1) in enumerate(results):
            s_ref[h, 0] = s0
            o_ref[:, pl.ds(h * HEAD_DIM, HEAD_DIM)] = o
            state[h] = s1

    return pl.pallas_call(
        body, name="gdn_core", grid=(nchunk,),
        in_specs=[pl.BlockSpec((CHUNK, 3 * gw), lambda n: (n, 0)), pl.BlockSpec((CHUNK, HEAD_DIM), lambda n: (n, 0))],
        out_specs=[pl.BlockSpec((CHUNK, gw), lambda n: (n, 0)),
                   pl.BlockSpec((heads, 1, HEAD_DIM, HEAD_DIM), lambda n: (0, n, 0, 0))],
        out_shape=[jax.ShapeDtypeStruct((t, gw), F32),
                   jax.ShapeDtypeStruct((heads, nchunk, HEAD_DIM, HEAD_DIM), F32)],
        scratch_shapes=[pltpu.VMEM((heads, HEAD_DIM, HEAD_DIM), F32)],
        compiler_params=_params("arbitrary"),
    )(qkv, bg)


def _gdn_core_backward(qkv, bg, states, do, heads):
    t = qkv.shape[0]
    nchunk = t // CHUNK
    n = CHUNK

    def one_head(chunk_local, s0, d_out, ds1):
        c = yield from chunk_local
        v_new = c.u - _bdot(c.w, s0)
        dqg = _bdot(d_out, s0, tb=True)
        ds0 = _bdot(c.qg, d_out, ta=True) + ds1 * c.egl
        dv_new = _bdot(c.A, d_out, ta=True) + _bdot(c.kdec, ds1)
        yield
        dA = jnp.where(c.tri, _bdot(d_out, v_new, tb=True), 0.0)
        dkdec = _bdot(v_new, ds1, tb=True)
        dgl = jnp.sum(jnp.sum(ds1 * s0, axis=1, keepdims=True), axis=0, keepdims=True) * c.egl
        dw = -_bdot(dv_new, s0, tb=True)
        ds0 = ds0 - _bdot(c.w, dv_new, ta=True)
        yield
        both = _dot22(c.T, jnp.concatenate([dv_new, dw], axis=1), ta=True)
        yield
        dvb, dkg = both[:, :HEAD_DIM], both[:, HEAD_DIM:]
        dL = jnp.where(c.strict, -(_bdot(dvb, c.u, tb=True) + _bdot(dkg, c.w, tb=True)), 0.0)
        yield
        dm1 = dL * c.decay
        dkb = _bdot(dm1, c.k) + dkg * c.eg
        dk = _bdot(dm1, c.kb, ta=True)
        dm2 = dA * c.decay
        dq = _bdot(dm2, c.k) + dqg * c.eg
        dk = dk + _bdot(dm2, c.q, ta=True) + dkdec * c.ekl + dkb * c.beta
        pm = dL * c.L + dA * c.A
        ones = jnp.ones((n, HEAD_DIM), BF16)
        pm_hi, pm_lo = _split2(pm)
        colsum = _bdot(pm_hi, ones, ta=True) + _bdot(pm_lo, ones, ta=True)
        tk_ = jnp.sum(dkdec * c.kdec, axis=1, keepdims=True)
        dgc = (jnp.sum(pm, axis=1, keepdims=True) - colsum
               + jnp.sum(dqg * c.qg, axis=1, keepdims=True)
               - tk_
               + jnp.sum(dkg * c.kg, axis=1, keepdims=True))
        dgl = dgl + jnp.sum(tk_, axis=0, keepdims=True)
        rowi = lax.broadcasted_iota(jnp.int32, (n, HEAD_DIM), 0)
        dgc = dgc + jnp.where(rowi == n - 1, dgl, 0.0)
        dbeta = jnp.sum(dkb * c.k, axis=1, keepdims=True) + jnp.sum(dvb * c.v, axis=1, keepdims=True)
        return dq * (HEAD_DIM ** -0.5), dk, dvb * c.beta, dbeta, dgc, ds0

    gw = heads * HEAD_DIM

    def body(qkv_ref, bg_ref, s_ref, do_ref, dqkv_ref, dbg_ref, dstate):
        @pl.when(pl.program_id(0) == 0)
        def _():
            dstate[...] = jnp.zeros_like(dstate)

        bg_v = bg_ref[...]
        gc_all, gc_rows = _chunk_gates(bg_v, heads)
        lane = lax.broadcasted_iota(jnp.int32, (n, HEAD_DIM), 1)
        dgates = jnp.zeros((n, HEAD_DIM), F32)
        chains = []
        for h in range(heads):
            col = lambda s: pl.ds(s * gw + h * HEAD_DIM, HEAD_DIM)
            c = _chunk_local(qkv_ref[:, col(0)], qkv_ref[:, col(1)], qkv_ref[:, col(2)], bg_v[:, h:h + 1],
                             gc_all[:, heads + h:heads + h + 1], gc_rows[heads + h:heads + h + 1, :])
            chains.append(one_head(c, s_ref[h, 0], do_ref[:, pl.ds(h * HEAD_DIM, HEAD_DIM)], dstate[h]))
        results = _in_lockstep(chains)
        for h, (dq, dk, dv, dbeta, dgc, ds0) in enumerate(results):
            dgates = jnp.where(lane == h, dbeta, jnp.where(lane == heads + h, dgc, dgates))
        for h, (dq, dk, dv, dbeta, dgc, ds0) in enumerate(results):
            dqkv_ref[:, pl.ds(h * HEAD_DIM, HEAD_DIM)] = dq
            dqkv_ref[:, pl.ds(gw + h * HEAD_DIM, HEAD_DIM)] = dk
            dqkv_ref[:, pl.ds(2 * gw + h * HEAD_DIM, HEAD_DIM)] = dv
            dstate[h] = ds0
        row = lax.broadcasted_iota(jnp.int32, (n, n), 0)
        colm = lax.broadcasted_iota(jnp.int32, (n, n), 1)
        draw = _dot_mask((row >= colm).astype(BF16), dgates, ta=True)
        dbg_ref[...] = jnp.where(lane < heads, dgates, draw)

    last = nchunk - 1
    return pl.pallas_call(
        body, name="gdn_core_bwd", grid=(nchunk,),
        in_specs=[pl.BlockSpec((CHUNK, 3 * gw), lambda i: (last - i, 0)),
                  pl.BlockSpec((CHUNK, HEAD_DIM), lambda i: (last - i, 0)),
                  pl.BlockSpec((heads, 1, HEAD_DIM, HEAD_DIM), lambda i: (0, last - i, 0, 0)),
                  pl.BlockSpec((CHUNK, gw), lambda i: (last - i, 0))],
        out_specs=[pl.BlockSpec((CHUNK, 3 * gw), lambda i: (last - i, 0)),
                   pl.BlockSpec((CHUNK, HEAD_DIM), lambda i: (last - i, 0))],
        out_shape=[jax.ShapeDtypeStruct((t, 3 * gw), F32), jax.ShapeDtypeStruct((t, HEAD_DIM), F32)],
        scratch_shapes=[pltpu.VMEM((heads, HEAD_DIM, HEAD_DIM), F32)],
        compiler_params=_params("arbitrary"),
    )(qkv, bg, states, do)


def _gdn_post(o, proj, z_col0, norm_w, heads, tt):
    t = o.shape[0]
    zb = z_col0 // HEAD_DIM

    def body(o_ref, z_ref, w_ref, out_ref):
        ov = o_ref[...]
        z = z_ref[...]
        rms = lax.rsqrt(jnp.mean(ov * ov, axis=-1, keepdims=True) + NORM_EPS)
        out_ref[...] = (ov * rms * w_ref[...] * (z * _sigmoid(z))).astype(BF16)

    return pl.pallas_call(
        body, name="gdn_post", grid=(t // tt, heads),
        in_specs=[pl.BlockSpec((tt, HEAD_DIM), lambda i, h: (i, h)),
                  pl.BlockSpec((tt, HEAD_DIM), lambda i, h: (i, zb + h)),
                  pl.BlockSpec((1, HEAD_DIM), lambda i, h: (0, 0))],
        out_specs=pl.BlockSpec((tt, HEAD_DIM), lambda i, h: (i, h)),
        out_shape=jax.ShapeDtypeStruct((t, heads * HEAD_DIM), BF16),
        compiler_params=_params("parallel", "parallel"),
    )(o, proj, norm_w)


def _gdn_post_backward(dcat, o, proj, z_col0, norm_w, heads, tt):
    t = o.shape[0]
    zb = z_col0 // HEAD_DIM

    def body(d_ref, o_ref, z_ref, w_ref, do_ref, dz_ref, dw_ref):
        d = d_ref[...]
        ov = o_ref[...]
        z = z_ref[...]
        w = w_ref[...]
        rms = lax.rsqrt(jnp.mean(ov * ov, axis=-1, keepdims=True) + NORM_EPS)
        ohat = ov * rms
        sg = _sigmoid(z)
        gate = z * sg
        dz_ref[...] = (d * ohat * w * (sg * (1.0 + z * (1.0 - sg)))).astype(BF16)
        don = d * gate
        dohat = don * w
        do_ref[...] = rms * (dohat - ohat * jnp.mean(dohat * ohat, axis=-1, keepdims=True))
        dw = jnp.sum(don * ohat, axis=0, keepdims=True)
        first = jnp.logical_and(pl.program_id(0) == 0, pl.program_id(1) == 0)

        @pl.when(first)
        def _():
            dw_ref[...] = dw

        @pl.when(jnp.logical_not(first))
        def _():
            dw_ref[...] += dw

    blk = pl.BlockSpec((tt, HEAD_DIM), lambda i, h: (i, h))
    return pl.pallas_call(
        body, name="gdn_post_bwd", grid=(t // tt, heads),
        in_specs=[blk, blk, pl.BlockSpec((tt, HEAD_DIM), lambda i, h: (i, zb + h)),
                  pl.BlockSpec((1, HEAD_DIM), lambda i, h: (0, 0))],
        out_specs=[blk, blk, pl.BlockSpec((1, HEAD_DIM), lambda i, h: (0, 0))],
        out_shape=[jax.ShapeDtypeStruct((t, heads * HEAD_DIM), F32),
                   jax.ShapeDtypeStruct((t, heads * HEAD_DIM), BF16),
                   jax.ShapeDtypeStruct((1, HEAD_DIM), F32)],
        compiler_params=_params("arbitrary", "arbitrary"),
    )(dcat, o, proj, norm_w)


def _pool_select(levels, group):
    out = levels[-1]
    for gi in range(len(levels) - 2, -1, -1):
        out = jnp.where(group == gi, levels[gi], out)
    return out


def _pool_counts(t, width, group):
    pos = lax.broadcasted_iota(jnp.int32, (t, width), 0)
    win = jnp.left_shift(2, group)
    return jnp.minimum(pos + 1, win).astype(F32)


def _pooled(p, group):
    levels, s, step = [], p, 1
    for _ in POOL_WINDOWS:
        s = s + _shift_down(s, step)
        levels.append(s)
        step *= 2
    cnt = _pool_counts(p.shape[0], p.shape[1], group)
    return _pool_select(levels, group) / cnt - p, cnt


def _pool_forward(proj, p_col0, pool_w, pool_scale):
    t = proj.shape[0]
    groups, cg, _ = pool_w.shape
    pb = p_col0 // cg

    def body(p_ref, w_ref, s_ref, o_ref):
        pooled, _ = _pooled(p_ref[...], pl.program_id(0))
        o_ref[...] = (_bdot(pooled, w_ref[0]) * s_ref[...]).astype(BF16)

    return pl.pallas_call(
        body, name="pool_fwd", grid=(groups,),
        in_specs=[pl.BlockSpec((t, cg), lambda g: (0, pb + g)), pl.BlockSpec((1, cg, cg), lambda g: (g, 0, 0)),
                  pl.BlockSpec((1, cg), lambda g: (0, g))],
        out_specs=pl.BlockSpec((t, cg), lambda g: (0, g)),
        out_shape=jax.ShapeDtypeStruct((t, groups * cg), BF16),
        compiler_params=_params("parallel"),
    )(proj, pool_w, pool_scale)


def _pool_backward(dcat, d_col0, proj, p_col0, pool_w, pool_scale):
    t = proj.shape[0]
    groups, cg, _ = pool_w.shape
    pb = p_col0 // cg
    db = d_col0 // cg

    def body(d_ref, p_ref, w_ref, s_ref, dp_ref, dw_ref, ds_ref):
        group = pl.program_id(0)
        pooled, cnt = _pooled(p_ref[...], group)
        w = w_ref[0]
        d = d_ref[...]
        mixed = _bdot(pooled, w)
        ds_ref[...] = jnp.sum(d * mixed, axis=0, keepdims=True)
        dmixed = d * s_ref[...]
        dw_ref[0] = _bdot(pooled, dmixed, ta=True)
        dpooled = _bdot(dmixed, w, tb=True)
        levels, s, step = [], dpooled / cnt, 1
        for _ in POOL_WINDOWS:
            s = s + _shift_up(s, step)
            levels.append(s)
            step *= 2
        dp_ref[...] = (_pool_select(levels, group) - dpooled).astype(BF16)

    return pl.pallas_call(
        body, name="pool_bwd", grid=(groups,),
        in_specs=[pl.BlockSpec((t, cg), lambda g: (0, db + g)), pl.BlockSpec((t, cg), lambda g: (0, pb + g)),
                  pl.BlockSpec((1, cg, cg), lambda g: (g, 0, 0)), pl.BlockSpec((1, cg), lambda g: (0, g))],
        out_specs=[pl.BlockSpec((t, cg), lambda g: (0, g)), pl.BlockSpec((1, cg, cg), lambda g: (g, 0, 0)),
                   pl.BlockSpec((1, cg), lambda g: (0, g))],
        out_shape=[jax.ShapeDtypeStruct((t, groups * cg), BF16), jax.ShapeDtypeStruct((groups, cg, cg), F32),
                   jax.ShapeDtypeStruct((1, groups * cg), F32)],
        compiler_params=_params("parallel"),
    )(dcat, proj, pool_w, pool_scale)


def _attention(q, k, v, tq):
    t, d = q.shape
    m = k.shape[0]
    dh = d // XATTN_HEADS
    scale = dh ** -0.5

    def body(q_ref, k_ref, v_ref, o_ref):
        s = _bdot(q_ref[...], k_ref[...], tb=True) * scale
        s = s - jnp.max(s, axis=-1, keepdims=True)
        e = jnp.exp(s)
        p = e / jnp.sum(e, axis=-1, keepdims=True)
        o_ref[...] = _bdot(p, v_ref[...]).astype(BF16)

    return pl.pallas_call(
        body, name="xattn_fwd", grid=(XATTN_HEADS, t // tq),
        in_specs=[pl.BlockSpec((tq, dh), lambda h, i: (i, h)), pl.BlockSpec((m, dh), lambda h, i: (0, h)),
                  pl.BlockSpec((m, dh), lambda h, i: (0, h))],
        out_specs=pl.BlockSpec((tq, dh), lambda h, i: (i, h)),
        out_shape=jax.ShapeDtypeStruct((t, d), BF16),
        compiler_params=_params("parallel", "parallel"),
    )(q, k, v)


def _attention_backward(q, k, v, do, tq):
    t, d = q.shape
    m = k.shape[0]
    dh = d // XATTN_HEADS
    scale = dh ** -0.5

    def body(q_ref, k_ref, v_ref, do_ref, dq_ref, dk_ref, dv_ref, dk_acc, dv_acc):
        i = pl.program_id(1)
        qv, kv, vv, dov = q_ref[...], k_ref[...], v_ref[...], do_ref[...]
        s = _bdot(qv, kv, tb=True) * scale
        s = s - jnp.max(s, axis=-1, keepdims=True)
        e = jnp.exp(s)
        p = e / jnp.sum(e, axis=-1, keepdims=True)
        dp = _bdot(dov, vv, tb=True)
        ds = p * (dp - jnp.sum(dp * p, axis=-1, keepdims=True)) * scale
        dq_ref[...] = _bdot(ds, kv).astype(BF16)
        dv_part = _bdot(p, dov, ta=True)
        dk_part = _bdot(ds, qv, ta=True)

        @pl.when(i == 0)
        def _():
            dk_acc[...] = dk_part
            dv_acc[...] = dv_part

        @pl.when(i > 0)
        def _():
            dk_acc[...] += dk_part
            dv_acc[...] += dv_part

        @pl.when(i == pl.num_programs(1) - 1)
        def _():
            dk_ref[...] = dk_acc[...].astype(BF16)
            dv_ref[...] = dv_acc[...].astype(BF16)

    qblk = pl.BlockSpec((tq, dh), lambda h, i: (i, h))
    kblk = pl.BlockSpec((m, dh), lambda h, i: (0, h))
    return pl.pallas_call(
        body, name="xattn_bwd", grid=(XATTN_HEADS, t // tq),
        in_specs=[qblk, kblk, kblk, qblk],
        out_specs=[qblk, kblk, kblk],
        out_shape=[jax.ShapeDtypeStruct((t, d), BF16), jax.ShapeDtypeStruct((m, d), BF16),
                   jax.ShapeDtypeStruct((m, d), BF16)],
        scratch_shapes=[pltpu.VMEM((m, dh), F32), pltpu.VMEM((m, dh), F32)],
        compiler_params=_params("parallel", "arbitrary"),
    )(q, k, v, do)


def _loss_and_ln_backward(xhat, rstd, gamma, beta, target, tm):
    t, d = xhat.shape

    def body(x_ref, r_ref, g_ref, b_ref, t_ref, du_ref, dub_ref, dg_ref, db_ref, loss_ref):
        xh = x_ref[...]
        g = g_ref[...]
        diff = xh * g + b_ref[...] - t_ref[...]
        part = jnp.sum(jnp.sum(diff * diff, axis=1, keepdims=True), axis=0, keepdims=True) * (0.5 / d)
        dy = diff * (1.0 / d)
        du, dg, db = _ln_backward_math(dy, xh, r_ref[...], g)
        du_ref[...] = du
        dub_ref[...] = du.astype(BF16)
        lossrow = jnp.broadcast_to(part, (1, HEAD_DIM))
        first = pl.program_id(0) == 0

        @pl.when(first)
        def _():
            dg_ref[...] = dg
            db_ref[...] = db
            loss_ref[...] = lossrow

        @pl.when(jnp.logical_not(first))
        def _():
            dg_ref[...] += dg
            db_ref[...] += db
            loss_ref[...] += lossrow

    row = pl.BlockSpec((tm, d), lambda i: (i, 0))
    vec = pl.BlockSpec((1, d), lambda i: (0, 0))
    return pl.pallas_call(
        body, name="loss_ln3_bwd", grid=(t // tm,),
        in_specs=[row, pl.BlockSpec((tm, 1), lambda i: (i, 0)), vec, vec, row],
        out_specs=[row, row, vec, vec, pl.BlockSpec((1, HEAD_DIM), lambda i: (0, 0))],
        out_shape=[jax.ShapeDtypeStruct((t, d), F32), jax.ShapeDtypeStruct((t, d), BF16),
                   jax.ShapeDtypeStruct((1, d), F32), jax.ShapeDtypeStruct((1, d), F32),
                   jax.ShapeDtypeStruct((1, HEAD_DIM), F32)],
        compiler_params=_params("arbitrary"),
    )(xhat, rstd, gamma, beta, target)


def _after(token, a):
    return a if token is None else a + token[:1, :1].astype(a.dtype)


def _pick(n, prefs):
    for p in prefs:
        if n % p == 0:
            return p
    return n


def _local_step(x, mem, target, w):
    t, d = x.shape
    heads = w["a_log"].shape[1]
    gw = heads * HEAD_DIM
    groups, cg, _ = w["pool_w"].shape
    pw = groups * cg
    n_main = 4 * gw + pw
    in_cols = n_main + 2 * heads
    s_in = w["w_in_t"].shape[0]

    tm = _pick(t, (512, 256, 128))
    tm_ln = _pick(t, (256, 128))
    tm_big = _pick(t, (1024, 512, 256, 128))
    tk = _pick(d, K_STEPS)

    w_in_t = w["w_in_t"].reshape(in_cols, d)
    w_p_t = w_in_t[4 * gw + 2 * heads:]
    w_ba_t = jnp.pad(w_in_t[4 * gw:4 * gw + 2 * heads], ((0, HEAD_DIM - 2 * heads), (0, 0)))
    x_bf = x.astype(BF16)
    mem_bf = mem.astype(BF16)

    tn_d = _pick(d, (1024, 512, 256, 128))
    proj = _plain("proj_main", x_bf, w_in_t, tb=True, n_used=4 * gw, tm=tm_big, tn=_pick(4 * gw, (1024, 512, 256, 128)),
                  tk=tk, out_dtype=F32)
    pproj = _plain("proj_pool", x_bf, w_p_t, tb=True, tm=tm_big, tn=_pick(pw, (1024, 512, 256, 128)), tk=tk, out_dtype=F32)
    ea, dtb = _gate_vectors(w["a_log"], w["dt_bias"], heads)
    vec128 = lambda i, j: (0, 0)
    ba, bg = _matmul(
        "proj_gates", x_bf, w_ba_t, tb=True, tm=tm, tn=HEAD_DIM, tk=tk,
        extra=[(ea, (1, HEAD_DIM), vec128), (dtb, (1, HEAD_DIM), vec128)],
        outs=[(jax.ShapeDtypeStruct((t, HEAD_DIM), F32), (tm, HEAD_DIM), _tile)] * 2,
        epilogue=_gates_epilogue(heads))
    qkv = _gdn_pre(proj, w["conv_w"], heads)
    o_gdn, states = _gdn_core(qkv, bg, heads)
    cat_g = _gdn_post(o_gdn, proj, 3 * gw, w["gdn_norm_w"], heads, tm)
    cat_p = _pool_forward(pproj, 0, w["pool_w"], w["pool_scale"])
    cat = jnp.concatenate([cat_g, cat_p], axis=1)
    w = {**w, **(yield ("weights", 1, cat))}
    h1, h1_bf, xhat1, rstd1 = _ln_forward("mix_ln1", cat, w["w_out"], x, w["ln1_g"], w["ln1_b"], tm=tm_ln, tk=tk)

    q = _plain("xattn_q", h1_bf, w["xq_w"], tm=tm, tn=tn_d, tk=tk, out_dtype=BF16)
    mlen = mem.shape[0]
    tm_mem = _pick(mlen, (256, 128))
    k = _plain("xattn_k", mem_bf, w["xk_w"], tm=tm_mem, tn=tn_d, tk=tk, out_dtype=BF16)
    v = _plain("xattn_v", mem_bf, w["xv_w"], tm=tm_mem, tn=tn_d, tk=tk, out_dtype=BF16)
    att = _attention(q, k, v, tm)
    h2, h2_bf, xhat2, rstd2 = _ln_forward("xo_ln2", att, w["xo_w"], h1, w["ln2_g"], w["ln2_b"], tm=tm_ln, tk=tk)

    w = {**w, **(yield ("weights", 2, h2_bf))}
    s_up = w["w_up3"].shape[0]
    ff = s_up * w["w_up3"].shape[2]
    tn_f = _pick(ff // s_up, (1024, 512, 256, 128))

    def up_epi(acc, ex, out, i):
        r = jnp.maximum(acc, 0.0)
        out[0][...] = (r * r).astype(BF16)
        out[1][...] = (2.0 * r).astype(BF16)

    act, act_grad = _matmul(
        "mlp_up", h2_bf, w["w_up3"], b_blocks=s_up, tm=tm_big, tn=tn_f, tk=tk,
        outs=[(jax.ShapeDtypeStruct((t, ff), BF16), (tm_big, tn_f), _tile)] * 2, epilogue=up_epi)
    tk_f = _pick(ff, K_STEPS)
    xhat3, rstd3 = _ln_forward("down_ln3", act, w["w_down"], h2, w["ln3_g"], w["ln3_b"], tm=tm, tk=tk_f, want_h=False)

    grads = {}
    du3, du3_bf, grads["ln3_g"], grads["ln3_b"], loss = _loss_and_ln_backward(
        xhat3, rstd3, w["ln3_g"], w["ln3_b"], target, tm_ln)

    def dup_epi(acc, ex, out, i):
        out[0][...] = (acc * ex[0][...].astype(F32)).astype(BF16)

    dup = _matmul(
        "mlp_down_dx", du3_bf, w["w_down"], tb=True, tm=tm_big, tn=tn_f, tk=tk,
        extra=[(act_grad, (tm_big, tn_f), _tile)],
        outs=[(jax.ShapeDtypeStruct((t, ff), BF16), (tm_big, tn_f), _tile)], epilogue=dup_epi)[0]
    tk_t = _pick(t, K_STEPS)
    tm_w = _pick(d, (512, 256, 128))
    grads["w_down"] = _plain("mlp_down_dw", act, du3_bf, ta=True, tm=_pick(ff, (512, 256, 128)), tn=tn_d, tk=tk_t,
                             out_dtype=F32)
    grads["w_up3"] = _plain("mlp_up_dw", h2_bf, dup, ta=True, tm=tm_w, tn=tn_f, tk=tk_t, out_dtype=F32, out3=s_up)
    token = yield ("grads", 0, {n: grads.pop(n) for n in ("w_down", "w_up3")})
    du2, du2_bf, grads["ln2_g"], grads["ln2_b"] = _ln_backward(
        "mlp_up_dx_ln2", dup, w["w_up3"], du3, xhat2, rstd2, _after(token, w["ln2_g"]), tm=tm,
        tk=_pick(ff // s_up, K_STEPS[1:]), b_blocks=s_up)
    token = yield ("poll", 0, du2_bf)

    grads["xo_w"] = _plain("xo_dw", att, du2_bf, ta=True, tm=tm_w, tn=tn_d, tk=tk_t, out_dtype=F32)
    datt = _plain("xo_dx", du2_bf, w["xo_w"], tb=True, tm=tm, tn=tn_d, tk=tk, out_dtype=BF16)
    dq, dk, dv = _attention_backward(q, k, v, datt, tm)
    tk_m = _pick(mlen, (256, 128))
    grads["xq_w"] = _plain("xq_dw", h1_bf, dq, ta=True, tm=tm_w, tn=tn_d, tk=tk_t, out_dtype=F32)
    grads["xk_w"] = _plain("xk_dw", mem_bf, dk, ta=True, tm=tm_w, tn=tn_d, tk=tk_m, out_dtype=F32)
    grads["xv_w"] = _plain("xv_dw", mem_bf, dv, ta=True, tm=tm_w, tn=tn_d, tk=tk_m, out_dtype=F32)
    du1, du1_bf, grads["ln1_g"], grads["ln1_b"] = _ln_backward(
        "xq_dx_ln1", dq, w["xq_w"], du2, xhat1, rstd1, _after(token, w["ln1_g"]), tm=tm_ln, tk=tk)

    grads["w_out"] = _plain("out_dw", cat, du1_bf, ta=True, tm=tm_w, tn=tn_d, tk=tk_t, out_dtype=F32)
    token = yield ("grads", 1, {n: grads.pop(n) for n in ("xo_w", "xq_w", "xk_w", "xv_w", "w_out")})
    dcat = _plain("out_dx", du1_bf, w["w_out"], tb=True, tm=tm, tn=tn_d, tk=tk, out_dtype=F32)
    dp, grads["pool_w"], grads["pool_scale"] = _pool_backward(dcat, gw, pproj, 0, w["pool_w"],
                                                              _after(token, w["pool_scale"]))
    do_gdn, dz, grads["gdn_norm_w"] = _gdn_post_backward(dcat, o_gdn, proj, 3 * gw, _after(token, w["gdn_norm_w"]),
                                                         heads, tm)
    token = yield ("poll", 1, do_gdn)
    dqkv, dbg = _gdn_core_backward(qkv, _after(token, bg), states, do_gdn, heads)
    dqkv_pre, grads["conv_w"] = _gdn_pre_backward(proj, w["conv_w"], dqkv, heads)
    dba, dalog_row, ddt_row = _gates_backward(ba, bg, dbg, ea, dtb, heads)
    grads["a_log"] = dalog_row[:, heads:2 * heads]
    grads["dt_bias"] = ddt_row[:, heads:2 * heads]

    dproj = jnp.concatenate([dqkv_pre, dz, dp], axis=1)
    dw_main = _plain("proj_dw", dproj, x_bf, ta=True, tm=_pick(n_main, (512, 256, 128)), tn=tn_d, tk=tk_t, out_dtype=F32)
    dw_ba = _plain("proj_gates_dw", dba, x_bf, ta=True, tm=HEAD_DIM, tn=tn_d, tk=tk_t, out_dtype=F32)
    dw_in_t = jnp.concatenate([dw_main[:4 * gw], dw_ba[:2 * heads], dw_main[4 * gw:]], axis=0)
    grads["w_in_t"] = dw_in_t.reshape(s_in, in_cols // s_in, d)

    def dx_epi(acc, ex, out, i):
        out[0][...] = acc + ex[1][...] + ALPHA * ex[0][...]

    def add_epi(acc, ex, out, i):
        out[0][...] = acc + ex[0][...]

    token = yield ("grads", 2, {n: grads.pop(n) for n in ("w_in_t", "pool_w")})
    dx_gates = _plain("proj_gates_dx", dba, _after(token, w_ba_t), tm=tm, tn=tn_d, tk=HEAD_DIM, out_dtype=F32)
    out_tile = [(jax.ShapeDtypeStruct((t, d), F32), (tm, tn_d), _tile)]
    dx_pool = _matmul("proj_pool_dx", dp, w_p_t, tm=tm, tn=tn_d, tk=_pick(pw, K_STEPS),
                      extra=[(dx_gates, (tm, tn_d), _tile)], outs=out_tile, epilogue=add_epi)[0]
    grad_x = _matmul(
        "proj_dx", dproj, w_in_t, k_used=4 * gw, tm=tm, tn=tn_d, tk=_pick(4 * gw, K_STEPS),
        extra=[(du1, (tm, tn_d), _tile), (dx_pool, (tm, tn_d), _tile)], outs=out_tile, epilogue=dx_epi)[0]
    yield ("poll", 2, grad_x)
    return loss, grad_x, grads


def _adamw(name, w, g, m, v):
    r, c = w.shape
    if r % 8 == 0:
        tr = _pick(r, (256, 128, 64, 32, 16, 8))
        blk, steps = pl.BlockSpec((tr, c), lambda i: (i, 0)), r // tr
    else:
        tc = _pick(c, (256, 128))
        blk, steps = pl.BlockSpec((r, tc), lambda i: (0, i)), c // tc
    c1 = 1.0 - ADAM_B1 ** ADAM_STEP
    c2 = 1.0 - ADAM_B2 ** ADAM_STEP

    def body(w_ref, g_ref, m_ref, v_ref, d_ref, mo_ref, vo_ref):
        gv = g_ref[...]
        mn = ADAM_B1 * m_ref[...] + (1.0 - ADAM_B1) * gv
        vn = ADAM_B2 * v_ref[...] + (1.0 - ADAM_B2) * (gv * gv)
        d_ref[...] = -ADAM_LR * ((mn / c1) / (jnp.sqrt(vn / c2) + ADAM_EPS) + ADAM_WD * w_ref[...])
        mo_ref[...] = mn
        vo_ref[...] = vn

    return pl.pallas_call(
        body, name=name, grid=(steps,), in_specs=[blk] * 4, out_specs=[blk] * 3,
        out_shape=[jax.ShapeDtypeStruct((r, c), F32)] * 3,
        compiler_params=_params("parallel"),
    )(w, g, m, v)


def _place():
    x, y, c = lax.axis_index("x"), lax.axis_index("y"), lax.axis_index("c")
    chips = [(1 - x, y), (x, 1 - y), (1 - x, 1 - y)]
    return x, y, c, chips


HBM = pl.BlockSpec(memory_space=pltpu.HBM)


SEM = pl.BlockSpec(memory_space=pltpu.SEMAPHORE)
ANY = pl.BlockSpec(memory_space=pl.ANY)
EFFECT = pltpu.SideEffectType.DATAFLOW_SIDE_EFFECTING


def _in_hbm(a):
    return pltpu.with_memory_space_constraint(a, pltpu.HBM)


def _remote(src, dst, send_sem, recv_sem, to):
    return pltpu.make_async_remote_copy(src_ref=src, dst_ref=dst, send_sem=send_sem, recv_sem=recv_sem,
                                        device_id=to, device_id_type=MESH)


def _by_rows(rows):
    return rows % 32 == 0


def _half_shape(rows, cols):
    return (rows // 2, cols) if _by_rows(rows) else (rows, cols // 2)


def _half(ref, which, *lead):
    rows, cols = ref.shape[-2:]
    if _by_rows(rows):
        return ref.at[(*lead, pl.ds(which * (rows // 2), rows // 2))]
    return ref.at[(*lead, slice(None), pl.ds(which * (cols // 2), cols // 2))]


def _landed(lands, i, shard_index, which):
    return _half(lands[i], which, shard_index)


def _gather_start(shards, after):
    n = len(shards)
    lands = [lax.empty((N_SHARD,) + s.shape, s.dtype) for s in shards]

    def body(*refs):
        ins, zones = refs[:n], refs[n:2 * n]
        ici_send, ici_recv, own_send, own_recv = refs[2 * n + 1:2 * n + 5]
        token = refs[-1]
        x, y, c, chips = _place()
        me = 2 * x + y
        for i in range(n):
            for j, chip in enumerate(chips):
                _remote(_half(ins[i], c), _landed(zones, i, me, c), ici_send.at[3 * i + j],
                        ici_recv.at[3 * i + j], (*chip, c)).start()
        for i in range(n):
            _remote(ins[i], zones[i].at[me], own_send.at[i], own_recv.at[i], (x, y, 1 - c)).start()
        token[...] = jnp.zeros_like(token)

    dma = pltpu.SemaphoreType.DMA
    outs = pl.pallas_call(
        body, name="gather_start",
        in_specs=[HBM] * (2 * n) + [ANY],
        out_shape=(dma((3 * n,)), dma((3 * n,)), dma((n,)), dma((n,)),
                   *[pltpu.HBM(a.shape, a.dtype) for a in shards + lands], jax.ShapeDtypeStruct((8, LANES), F32)),
        out_specs=(SEM, SEM, SEM, SEM, *[HBM] * (2 * n), pl.BlockSpec(memory_space=pltpu.VMEM)),
        input_output_aliases={k: 4 + k for k in range(2 * n)},
        compiler_params=pltpu.CompilerParams(has_side_effects=EFFECT),
    )(*[_in_hbm(a) for a in shards + lands], after)
    sems = dict(zip(("ici_send", "ici_recv", "own_send", "own_recv"), outs[:4]))
    return sems, list(outs[4:4 + n]), list(outs[4 + n:4 + 2 * n]), outs[-1]


def _gather_forward(name, idx, lands, sems, after):
    n = len(idx)

    def body(*refs):
        zones = refs[:n]
        ici_recv = refs[n]
        fwd_send, fwd_recv = refs[n + 2], refs[n + 3]
        x, y, c, chips = _place()
        for k, i in enumerate(idx):
            for j, chip in enumerate(chips):
                half = _landed(zones, k, 2 * chip[0] + chip[1], c)
                _remote(half, half, fwd_send.at[3 * k + j], ici_recv.at[3 * i + j], (*chip, c)).wait_recv()
                _remote(half, half, fwd_send.at[3 * k + j], fwd_recv.at[3 * k + j], (x, y, 1 - c)).start()

    dma = pltpu.SemaphoreType.DMA
    outs = pl.pallas_call(
        body, name=name,
        in_specs=[HBM] * n + [SEM, ANY],
        out_shape=(dma((3 * n,)), dma((3 * n,)), *[pltpu.HBM(a.shape, a.dtype) for a in lands]),
        out_specs=(SEM, SEM, *[HBM] * n),
        input_output_aliases={k: 2 + k for k in range(n)},
        compiler_params=pltpu.CompilerParams(has_side_effects=EFFECT),
    )(*lands, sems["ici_recv"], after)
    return (outs[0], outs[1]), list(outs[2:])


def _gather_wait(name, idx, shards, lands, sems, fwd):
    n = len(idx)

    def body(*refs):
        ins, zones = refs[:n], refs[n:2 * n]
        ici_send, own_send, own_recv, fwd_send, fwd_recv = refs[2 * n:2 * n + 5]
        x, y, c, chips = _place()
        me = 2 * x + y
        for k, i in enumerate(idx):
            mine = _half(ins[k], c)
            for j, chip in enumerate(chips):
                theirs = 2 * chip[0] + chip[1]
                _remote(mine, _landed(zones, k, me, c), ici_send.at[3 * i + j], fwd_recv.at[3 * k + j],
                        (*chip, c)).wait_send()
                sent = _landed(zones, k, theirs, c)
                _remote(sent, sent, fwd_send.at[3 * k + j], fwd_recv.at[3 * k + j], (x, y, 1 - c)).wait_send()
                passed = _landed(zones, k, theirs, 1 - c)
                _remote(passed, passed, fwd_send.at[3 * k + j], fwd_recv.at[3 * k + j], (x, y, 1 - c)).wait_recv()
            own = _remote(ins[k], zones[k].at[me], own_send.at[i], own_recv.at[i], (x, y, 1 - c))
            own.wait_send()
            own.wait_recv()

    outs = pl.pallas_call(
        body, name=name,
        in_specs=[HBM] * (2 * n) + [SEM] * 5,
        out_shape=tuple(pltpu.HBM(a.shape, a.dtype) for a in lands),
        out_specs=tuple([HBM] * n),
        input_output_aliases={n + k: k for k in range(n)},
        compiler_params=pltpu.CompilerParams(has_side_effects=EFFECT),
    )(*shards, *lands, sems["ici_send"], sems["own_send"], sems["own_recv"], fwd[0], fwd[1])
    return list(outs)


def _all_reduce_small(name, slab, after=None):
    r, width = slab.shape
    ndev = 8

    def body(x_ref, after_ref, out_ref, buf, send_sems, recv_sems):
        x, y, c, _ = _place()
        me = 4 * x + 2 * y + c
        buf[me] = x_ref[...]
        copies = []
        for k in range(1, ndev):
            peer = jnp.bitwise_xor(me, k)
            to = (peer // 4, (peer // 2) % 2, peer % 2)
            cp = pltpu.make_async_remote_copy(src_ref=x_ref, dst_ref=buf.at[me], send_sem=send_sems.at[k - 1],
                                              recv_sem=recv_sems.at[k - 1], device_id=to, device_id_type=MESH)
            cp.start()
            copies.append(cp)
        for k in range(1, ndev):
            peer = jnp.bitwise_xor(me, k)
            pltpu.make_async_remote_copy(src_ref=x_ref, dst_ref=buf.at[peer], send_sem=send_sems.at[k - 1],
                                         recv_sem=recv_sems.at[k - 1], device_id=(x, y, c),
                                         device_id_type=MESH).wait_recv()
        for cp in copies:
            cp.wait_send()
        total = buf[0]
        for d in range(1, ndev):
            total = total + buf[d]
        out_ref[...] = total

    return pl.pallas_call(
        body, name=name,
        in_specs=[pl.BlockSpec(memory_space=pltpu.VMEM), ANY], out_specs=pl.BlockSpec(memory_space=pltpu.VMEM),
        out_shape=jax.ShapeDtypeStruct((r, width), F32),
        scratch_shapes=[pltpu.VMEM((ndev, r, width), F32), pltpu.SemaphoreType.DMA((ndev - 1,)),
                        pltpu.SemaphoreType.DMA((ndev - 1,))],
        compiler_params=pltpu.CompilerParams(vmem_limit_bytes=VMEM_LIMIT),
    )(slab, slab if after is None else after)


def _half_tiling(rows, cols):
    if _by_rows(rows):
        tr = _pick(rows // 2, (256, 128, 64, 32, 16))
        nb = (rows // 2) // tr
        return (tr, cols), nb, (lambda which, b: (which * nb + b, 0)), (lambda b: (b, 0))
    tc = _pick(cols // 2, (256, 128))
    nb = (cols // 2) // tc
    return (rows, tc), nb, (lambda which, b: (0, which * nb + b)), (lambda b: (0, b))


def _chip_partial(name, grad, other, core):
    s, r, cdim = grad.shape
    blk, nb, whole, within = _half_tiling(r, cdim)

    def body(core_ref, g_ref, o_ref, out_ref):
        out_ref[...] = (g_ref[...] + o_ref[...]).astype(BF16)

    return pl.pallas_call(
        body, name=name,
        grid_spec=pltpu.PrefetchScalarGridSpec(
            num_scalar_prefetch=1, grid=(s, nb),
            in_specs=[pl.BlockSpec((None,) + blk, lambda j, b, core_ref: (j,) + whole(core_ref[0], b)),
                      pl.BlockSpec((None,) + blk, lambda j, b, core_ref: (j,) + within(b))],
            out_specs=pl.BlockSpec((None,) + blk, lambda j, b, core_ref: (j,) + within(b))),
        out_shape=jax.ShapeDtypeStruct((s,) + _half_shape(r, cdim), BF16),
        compiler_params=_params("parallel", "parallel"),
    )(core, grad, other)


def _partial_copies(ins, zones, send_sems, recv_sems):
    x, y, c, chips = _place()
    return [_remote(ins[i].at[2 * chip[0] + chip[1]], zones[i].at[j], send_sems.at[3 * i + j],
                    recv_sems.at[3 * i + j], (*chip, c))
            for i in range(len(ins)) for j, chip in enumerate(chips)]


def _swap_copies(ins, zones, send_sems, recv_sems):
    x, y, c, _ = _place()
    copies = []
    for i in range(len(ins)):
        for s in range(N_SHARD):
            copies.append(_remote(_half(ins[i], 1 - c, s), zones[i].at[s],
                                  send_sems.at[N_SHARD * i + s], recv_sems.at[N_SHARD * i + s], (x, y, 1 - c)))
    return copies


def _exchange_start(name, plan, sources, lands, per_array):
    n = len(sources)
    lands = [lax.empty(shape, dtype) for shape, dtype in lands]

    def body(*refs):
        for cp in plan(refs[:n], refs[n:2 * n], refs[2 * n], refs[2 * n + 1]):
            cp.start()
        refs[-1][...] = jnp.zeros_like(refs[-1])

    dma = pltpu.SemaphoreType.DMA
    outs = pl.pallas_call(
        body, name=name,
        in_specs=[HBM] * (2 * n),
        out_shape=(dma((per_array * n,)), dma((per_array * n,)),
                   *[pltpu.HBM(a.shape, a.dtype) for a in list(sources) + lands], jax.ShapeDtypeStruct((8, LANES), F32)),
        out_specs=(SEM, SEM, *[HBM] * (2 * n), pl.BlockSpec(memory_space=pltpu.VMEM)),
        input_output_aliases={k: 2 + k for k in range(2 * n)},
        compiler_params=pltpu.CompilerParams(has_side_effects=EFFECT),
    )(*[_in_hbm(a) for a in list(sources) + lands])
    return (outs[0], outs[1]), list(outs[2:2 + n]), list(outs[2 + n:2 + 2 * n]), outs[-1]


def _exchange_wait(name, plan, started, after):
    sems, partials, lands, _ = started
    n = len(partials)

    def body(*refs):
        for cp in plan(refs[:n], refs[n:2 * n], refs[2 * n], refs[2 * n + 1]):
            cp.wait_send()
            cp.wait_recv()

    outs = pl.pallas_call(
        body, name=name,
        in_specs=[HBM] * (2 * n) + [SEM, SEM] + [ANY] * len(after),
        out_shape=tuple(pltpu.HBM(a.shape, a.dtype) for a in lands),
        out_specs=tuple([HBM] * n),
        input_output_aliases={n + k: k for k in range(n)},
        compiler_params=pltpu.CompilerParams(has_side_effects=EFFECT),
    )(*partials, *lands, sems[0], sems[1], *after)
    return list(outs)


def _reduce_own(name, grad, other, received, where):
    s, r, cdim = grad.shape
    blk, nb, whole, within = _half_tiling(r, cdim)

    def body(where_ref, g_ref, o_ref, r_ref, out_ref):
        total = g_ref[...] + o_ref[...]
        for j in range(3):
            total = total + r_ref[j].astype(F32)
        out_ref[...] = total

    return pl.pallas_call(
        body, name=name,
        grid_spec=pltpu.PrefetchScalarGridSpec(
            num_scalar_prefetch=1, grid=(nb,),
            in_specs=[pl.BlockSpec((None,) + blk, lambda b, w_ref: (w_ref[0],) + whole(w_ref[1], b)),
                      pl.BlockSpec((None,) + blk, lambda b, w_ref: (w_ref[0],) + within(b)),
                      pl.BlockSpec((3,) + blk, lambda b, w_ref: (0,) + within(b))],
            out_specs=pl.BlockSpec(blk, lambda b, w_ref: whole(w_ref[1], b))),
        out_shape=jax.ShapeDtypeStruct((r, cdim), F32),
        compiler_params=_params("parallel"),
    )(where, grad, other, received)


def _join_halves(name, halves):
    n = len(halves)

    def body(*refs):
        bufs = refs[n:2 * n]
        send_sems, recv_sems = refs[2 * n:]
        x, y, c, _ = _place()
        copies = []
        for i in range(n):
            mine = _half(bufs[i], c)
            cp = pltpu.make_async_remote_copy(src_ref=mine, dst_ref=mine, send_sem=send_sems.at[i],
                                              recv_sem=recv_sems.at[i], device_id=(x, y, 1 - c), device_id_type=MESH)
            cp.start()
            copies.append(cp)
        for i, cp in enumerate(copies):
            theirs = _half(bufs[i], 1 - c)
            pltpu.make_async_remote_copy(src_ref=theirs, dst_ref=theirs, send_sem=send_sems.at[i],
                                         recv_sem=recv_sems.at[i], device_id=(x, y, 1 - c),
                                         device_id_type=MESH).wait_recv()
            cp.wait_send()

    return pl.pallas_call(
        body, name=name,
        in_specs=[HBM] * n, out_specs=[HBM] * n,
        out_shape=[jax.ShapeDtypeStruct(h.shape, F32) for h in halves],
        input_output_aliases={i: i for i in range(n)},
        scratch_shapes=[pltpu.SemaphoreType.DMA((n,)), pltpu.SemaphoreType.DMA((n,))],
    )(*halves)


BIG = ("w_in", "pool_w", "w_out", "xq_w", "xk_w", "xv_w", "xo_w", "w_up", "w_down")
GATHER_GROUPS = ((0, 1), (2, 3, 4, 5, 6), (7, 8))
SMALL = ("conv_w", "a_log", "dt_bias", "gdn_norm_w", "pool_scale", "ln1_g", "ln1_b", "ln2_g", "ln2_b", "ln3_g", "ln3_b")
ORDER = ("w_in", "conv_w", "a_log", "dt_bias", "gdn_norm_w", "pool_w", "pool_scale", "w_out", "ln1_g", "ln1_b",
         "xq_w", "xk_w", "xv_w", "xo_w", "ln2_g", "ln2_b", "w_up", "w_down", "ln3_g", "ln3_b")
LANES = 128


def _rows(flat_len):
    return -(-flat_len // LANES)


def _pack(pieces):
    out = []
    for p in pieces:
        flat = p.reshape(-1).astype(F32)
        out.append(jnp.pad(flat, (0, _rows(flat.shape[0]) * LANES - flat.shape[0])).reshape(-1, LANES))
    slab = jnp.concatenate(out, axis=0)
    return jnp.pad(slab, ((0, -slab.shape[0] % 8), (0, 0)))


def _unpack(slab, shapes):
    out, row = [], 0
    for shp in shapes:
        size = math.prod(shp)
        out.append(slab[row:row + _rows(size)].reshape(-1)[:size].reshape(shp))
        row += _rows(size)
    return out


TRANSPOSED = ("w_in",)


def _as2d(name, a):
    a = a[0]
    if name in TRANSPOSED:
        return jnp.swapaxes(a, 0, 1)
    return a.reshape(-1, a.shape[-1]) if a.ndim == 3 else a


def _from2d(name, a, shape):
    return (jnp.swapaxes(a, 0, 1) if name in TRANSPOSED else a).reshape(shape)


def kernel(x, mem, w_in, conv_w, a_log, dt_bias, gdn_norm_w, pool_w, pool_scale, w_out, ln1_g, ln1_b, xq_w, xk_w, xv_w, xo_w, ln2_g, ln2_b, w_up, w_down, ln3_g, ln3_b, loss_target, m_w_in, m_conv_w, m_a_log, m_dt_bias, m_gdn_norm_w, m_pool_w, m_pool_scale, m_w_out, m_ln1_g, m_ln1_b, m_xq_w, m_xk_w, m_xv_w, m_xo_w, m_ln2_g, m_ln2_b, m_w_up, m_w_down, m_ln3_g, m_ln3_b, v_w_in, v_conv_w, v_a_log, v_dt_bias, v_gdn_norm_w, v_pool_w, v_pool_scale, v_w_out, v_ln1_g, v_ln1_b, v_xq_w, v_xk_w, v_xv_w, v_xo_w, v_ln2_g, v_ln2_b, v_w_up, v_w_down, v_ln3_g, v_ln3_b):
    given = dict(locals())
    cx, cy, cc = lax.axis_index("x"), lax.axis_index("y"), lax.axis_index("c")
    me = 2 * cx + cy
    groups = pool_w.shape[1]
    cs = pool_w.shape[2]
    kk, conv_cols = conv_w.shape[1], conv_w.shape[2]
    core = cc.astype(jnp.int32).reshape(1)
    where = jnp.stack([me, cc]).astype(jnp.int32)

    conv_slab = jnp.zeros((kk, N_SHARD * conv_cols), F32)
    conv_slab = lax.dynamic_update_slice(conv_slab, conv_w[0] * (cc == 0).astype(F32), (0, me * conv_cols))
    wts = {"conv_w": _unpack(_all_reduce_small("gather_conv_w", _pack([conv_slab])), [conv_slab.shape])[0]}

    sems, shards, lands, token = _gather_start([_as2d(n, given[n]).astype(BF16) for n in BIG], wts["conv_w"])

    def fetch(group, after):
        idx = GATHER_GROUPS[group]
        fwd, zones = _gather_forward(f"gather_forward_{group}", idx, [lands[i] for i in idx], sems, after)
        full = dict(zip([BIG[i] for i in idx],
                        _gather_wait(f"gather_wait_{group}", idx, [shards[i] for i in idx], zones, sems, fwd)))
        out = {}
        for n, a in full.items():
            if n == "w_in":
                out["w_in_t"] = a
            elif n == "w_up":
                out["w_up3"] = a
            elif n == "pool_w":
                out[n] = a.reshape(N_SHARD, groups, cs, -1).transpose(1, 0, 2, 3).reshape(groups, N_SHARD * cs, -1)
            else:
                out[n] = a.reshape(-1, a.shape[-1])
        return out

    for n in ("a_log", "dt_bias", "gdn_norm_w", "pool_scale", "ln1_g", "ln1_b", "ln2_g", "ln2_b", "ln3_g", "ln3_b"):
        wts[n] = given[n]
    wts.update(fetch(0, token))

    def start_swap(group, grads):
        names, blocks = [], []
        for n, g in grads.items():
            if n == "pool_w":
                g = g.reshape(groups, N_SHARD, cs, -1).transpose(1, 0, 2, 3).reshape(N_SHARD, groups * cs, -1)
            elif g.ndim == 2:
                g = g.reshape(N_SHARD, -1, g.shape[-1])
            names.append({"w_in_t": "w_in", "w_up3": "w_up"}.get(n, n))
            blocks.append(g)
        zones = [((N_SHARD,) + _half_shape(b.shape[1], b.shape[2]), F32) for b in blocks]
        swap = _exchange_start(f"grad_swap_start_{group}", _swap_copies, blocks, zones, N_SHARD)
        return {"group": group, "names": names, "swap": swap, "token": swap[3]}

    def start_send(state, after):
        group, names = state["group"], state["names"]
        state["blocks"] = state["swap"][1]
        state["others"] = _exchange_wait(f"grad_swap_wait_{group}", _swap_copies, state["swap"], after)
        partials = [_chip_partial("chip_partial_" + n, gb, ob, core)
                    for n, gb, ob in zip(names, state["blocks"], state["others"])]
        zones = [((3,) + p.shape[1:], BF16) for p in partials]
        state["send"] = _exchange_start(f"grad_send_start_{group}", _partial_copies, partials, zones, 3)
        state["token"] = state["send"][3]

    grad, delta, new_m, new_v = {}, {}, {}, {}

    def finish_reduce(state, after):
        group, names = state["group"], state["names"]
        received = _exchange_wait(f"grad_send_wait_{group}", _partial_copies, state["send"], after)
        halves = [_reduce_own("reduce_own_" + n, gb, ob, rb, where)
                  for n, gb, ob, rb in zip(names, state["blocks"], state["others"], received)]
        for n, g in zip(names, _join_halves(f"grad_join_{group}", halves)):
            shp = given[n].shape
            d2, m2, v2 = _adamw("adamw_" + n, _as2d(n, given[n]), g, _as2d(n, given["m_" + n]), _as2d(n, given["v_" + n]))
            grad[n], delta[n], new_m[n], new_v[n] = (_from2d(n, a, shp) for a in (g, d2, m2, v2))
        return d2

    step = _local_step(x[0], mem[0], loss_target[0], wts)
    pending = {}
    request = next(step)
    while True:
        try:
            kind, group, payload = request
            if kind == "weights":
                request = step.send(fetch(group, payload))
            elif kind == "grads":
                pending[group] = start_swap(group, payload)
                request = step.send(pending[group]["token"])
            else:
                start_send(pending[group], [payload])
                request = step.send(pending[group]["token"])
        except StopIteration as stop:
            loss_row, grad_x, g = stop.value
            break

    after = [pending[2]["token"], grad_x]
    for group in sorted(pending):
        after = [finish_reduce(pending[group], after)]

    small_names = ("a_log", "dt_bias", "gdn_norm_w", "pool_scale", "ln1_g", "ln1_b", "ln2_g", "ln2_b", "ln3_g", "ln3_b")
    pieces = [g["conv_w"]] + [g[n] for n in small_names] + [loss_row[:, :1]]
    shapes = [p.shape for p in pieces]
    summed = _unpack(_all_reduce_small("all_reduce_small", _pack(pieces), after[0]), shapes)
    gsmall = dict(zip(small_names, summed[1:-1]))
    gsmall["conv_w"] = lax.dynamic_slice(summed[0], (0, me * conv_cols), (kk, conv_cols))
    loss = summed[-1][0, 0]

    sshapes = [given[n].shape for n in SMALL]
    slabs = [_pack([given[p + n] for n in SMALL]) for p in ("", "m_", "v_")]
    gslab = _pack([gsmall[n] for n in SMALL])
    outs = _adamw("adamw_small", slabs[0], gslab, slabs[1], slabs[2])
    for dst, slab in zip((delta, new_m, new_v), outs):
        dst.update(zip(SMALL, _unpack(slab, sshapes)))
    for n in SMALL:
        grad[n] = gsmall[n].reshape(given[n].shape)

    return (loss, grad_x[None], *[grad[n] for n in ORDER], *[delta[n] for n in ORDER],
            *[new_m[n] for n in ORDER], *[new_v[n] for n in ORDER])
```

```python
import functools
import math

import jax
import jax.numpy as jnp
from jax import lax
from jax.experimental import pallas as pl
from jax.experimental.pallas import tpu as pltpu

F32 = jnp.float32
BF16 = jnp.bfloat16
MESH = pl.DeviceIdType.MESH

HEAD_DIM = 128
CHUNK = 64
POOL_WINDOWS = (2, 4, 8, 16)
XATTN_HEADS = 4
ALPHA = 2.0 ** 0.25
LN_EPS = 1e-5
NORM_EPS = 1e-6
ADAM_LR, ADAM_B1, ADAM_B2, ADAM_EPS, ADAM_WD, ADAM_STEP = 0.001, 0.9, 0.999, 1e-08, 0.01, 10
N_SHARD = 4
VMEM_LIMIT = 56 * 1024 * 1024
K_STEPS = (2048, 1024, 512, 256, 128)


def _params(*sem):
    return pltpu.CompilerParams(dimension_semantics=sem, vmem_limit_bytes=VMEM_LIMIT)


def _bdot(a, b, ta=False, tb=False):
    dims = (((0 if ta else 1,), (1 if tb else 0,)), ((), ()))
    return lax.dot_general(a.astype(BF16), b.astype(BF16), dims, preferred_element_type=F32)


def _sigmoid(x):
    return 1.0 / (1.0 + jnp.exp(-x))


def _matmul(name, a, b, *, ta=False, tb=False, tm, tn, tk, extra=(), outs, epilogue, b_blocks=None,
            sequential=False, n_used=None, k_used=None):
    m, k_dim = (a.shape[1], a.shape[0]) if ta else a.shape
    if b_blocks and tb:
        n = b.shape[1]
        k_dim = b.shape[0] * b.shape[2]
        per = b.shape[2] // tk
        b_spec = pl.BlockSpec((None, tn, tk), lambda i, j, k: (k // per, j, k % per))
    elif b_blocks:
        n = b.shape[0] * b.shape[2]
        per = b.shape[2] // tn
        b_spec = pl.BlockSpec((None, tk, tn), lambda i, j, k: (j // per, k, j % per))
    elif tb:
        n = b.shape[0]
        b_spec = pl.BlockSpec((tn, tk), lambda i, j, k: (j, k))
    else:
        n = b.shape[1]
        b_spec = pl.BlockSpec((tk, tn), lambda i, j, k: (k, j))
    n, k_dim = n_used or n, k_used or k_dim
    assert m % tm == 0 and n % tn == 0 and k_dim % tk == 0, (name, m, n, k_dim, tm, tn, tk)
    nk = k_dim // tk
    a_spec = pl.BlockSpec((tk, tm), lambda i, j, k: (k, i)) if ta else pl.BlockSpec((tm, tk), lambda i, j, k: (i, k))
    n_extra, n_out = len(extra), len(outs)

    def wrap(index_map):
        return lambda i, j, k: index_map(i, j)

    def body_one_step(*refs):
        ex = refs[2:2 + n_extra]
        out = refs[2 + n_extra:2 + n_extra + n_out]
        epilogue(_bdot(refs[0][...], refs[1][...], ta, tb), ex, out, pl.program_id(0))

    def body(*refs):
        a_ref, b_ref = refs[0], refs[1]
        ex = refs[2:2 + n_extra]
        out = refs[2 + n_extra:2 + n_extra + n_out]
        acc = refs[-1]
        i, k = pl.program_id(0), pl.program_id(2)
        part = _bdot(a_ref[...], b_ref[...], ta, tb)

        @pl.when(k == 0)
        def _():
            acc[...] = part

        @pl.when(jnp.logical_and(k > 0, k < nk - 1))
        def _():
            acc[...] += part

        @pl.when(k == nk - 1)
        def _():
            epilogue(acc[...] + part, ex, out, i)

    sem = ("arbitrary",) * 3 if sequential else ("parallel", "parallel", "arbitrary")
    res = pl.pallas_call(
        body_one_step if nk == 1 else body, name=name, grid=(m // tm, n // tn, nk),
        in_specs=[a_spec, b_spec] + [pl.BlockSpec(bs, wrap(im)) for _, bs, im in extra],
        out_specs=[pl.BlockSpec(bs, wrap(im)) for _, bs, im in outs],
        out_shape=[s for s, _, _ in outs],
        scratch_shapes=[] if nk == 1 else [pltpu.VMEM((tm, tn), F32)],
        compiler_params=_params(*sem),
    )(a, b, *[x for x, _, _ in extra])
    return res


def _tile(i, j):
    return (i, j)


def _plain(name, a, b, *, ta=False, tb=False, tm, tn, tk, out_dtype, b_blocks=None, out3=None, n_used=None):
    m = a.shape[1] if ta else a.shape[0]
    n = n_used or ((b.shape[0] * b.shape[2]) if b_blocks else (b.shape[0] if tb else b.shape[1]))

    def epi(acc, ex, out, i):
        out[0][...] = acc.astype(out_dtype)

    if out3:
        per = (n // out3) // tn
        spec = (jax.ShapeDtypeStruct((out3, m, n // out3), out_dtype), (None, tm, tn),
                lambda i, j: (j // per, i, j % per))
    else:
        spec = (jax.ShapeDtypeStruct((m, n), out_dtype), (tm, tn), _tile)
    return _matmul(name, a, b, ta=ta, tb=tb, tm=tm, tn=tn, tk=tk, outs=[spec], epilogue=epi,
                   b_blocks=b_blocks, n_used=n_used)[0]


def _ln_forward(name, a, b, res, gamma, beta, *, tm, tk, want_h=True):
    m, n = res.shape

    def epi(acc, ex, out, i):
        u = ALPHA * ex[0][...] + acc
        mu = jnp.mean(u, axis=-1, keepdims=True)
        xc = u - mu
        var = jnp.mean(xc * xc, axis=-1, keepdims=True)
        rstd = lax.rsqrt(var + LN_EPS)
        xhat = xc * rstd
        out[-2][...] = xhat
        out[-1][...] = rstd
        if want_h:
            h = xhat * ex[1][...] + ex[2][...]
            out[0][...] = h
            out[1][...] = h.astype(BF16)

    row = lambda i, j: (i, 0)
    vec = lambda i, j: (0, 0)
    outs = [(jax.ShapeDtypeStruct((m, n), F32), (tm, n), row), (jax.ShapeDtypeStruct((m, n), BF16), (tm, n), row),
            (jax.ShapeDtypeStruct((m, n), F32), (tm, n), row), (jax.ShapeDtypeStruct((m, 1), F32), (tm, 1), row)]
    return _matmul(
        name, a, b, tm=tm, tn=n, tk=tk,
        extra=[(res, (tm, n), row), (gamma, (1, n), vec), (beta, (1, n), vec)],
        outs=outs if want_h else outs[2:], epilogue=epi)


def _ln_backward_math(dy, xhat, rstd, gamma):
    dxhat = dy * gamma
    m1 = jnp.mean(dxhat, axis=-1, keepdims=True)
    m2 = jnp.mean(dxhat * xhat, axis=-1, keepdims=True)
    du = rstd * (dxhat - m1 - xhat * m2)
    return du, jnp.sum(dy * xhat, axis=0, keepdims=True), jnp.sum(dy, axis=0, keepdims=True)


def _ln_backward(name, a, b, dres, xhat, rstd, gamma, *, tm, tk, b_blocks=None, tb=True):
    m, n = dres.shape

    def epi(acc, ex, out, i):
        dy = acc + ALPHA * ex[0][...]
        du, dg, db = _ln_backward_math(dy, ex[1][...], ex[2][...], ex[3][...])
        out[0][...] = du
        out[1][...] = du.astype(BF16)
        first = i == 0

        @pl.when(first)
        def _():
            out[2][...] = dg
            out[3][...] = db

        @pl.when(jnp.logical_not(first))
        def _():
            out[2][...] += dg
            out[3][...] += db

    row = lambda i, j: (i, 0)
    vec = lambda i, j: (0, 0)
    return _matmul(
        name, a, b, tb=tb, tm=tm, tn=n, tk=tk, b_blocks=b_blocks, sequential=True,
        extra=[(dres, (tm, n), row), (xhat, (tm, n), row), (rstd, (tm, 1), row), (gamma, (1, n), vec)],
        outs=[(jax.ShapeDtypeStruct((m, n), F32), (tm, n), row),
              (jax.ShapeDtypeStruct((m, n), BF16), (tm, n), row),
              (jax.ShapeDtypeStruct((1, n), F32), (1, n), vec),
              (jax.ShapeDtypeStruct((1, n), F32), (1, n), vec)],
        epilogue=epi)


def _shift_down(x, k):
    row = lax.broadcasted_iota(jnp.int32, x.shape, 0)
    return jnp.where(row >= k, pltpu.roll(x, k, axis=0), 0.0)


def _shift_up(x, k):
    t = x.shape[0]
    row = lax.broadcasted_iota(jnp.int32, x.shape, 0)
    return jnp.where(row < t - k, pltpu.roll(x, t - k, axis=0), 0.0)


def _conv_silu_norm(x, w, normalise):
    kk = w.shape[0]
    c = x * w[kk - 1:kk, :]
    for j in range(kk - 1):
        c = c + _shift_down(x, kk - 1 - j) * w[j:j + 1, :]
    sg = _sigmoid(c)
    s = c * sg
    r = lax.rsqrt(jnp.sum(s * s, axis=-1, keepdims=True) + NORM_EPS)
    y = jnp.where(normalise, s * r, s)
    return c, sg, s, r, y


def _gdn_pre(proj, conv_w, heads):
    t = proj.shape[0]
    kk = conv_w.shape[0]

    def body(x_ref, w_ref, o_ref):
        normalise = pl.program_id(0) < 2
        o_ref[...] = _conv_silu_norm(x_ref[...], w_ref[...], normalise)[4]

    col = lambda s, h: (0, s * heads + h)
    return pl.pallas_call(
        body, name="gdn_pre", grid=(3, heads),
        in_specs=[pl.BlockSpec((t, HEAD_DIM), col), pl.BlockSpec((kk, HEAD_DIM), col)],
        out_specs=pl.BlockSpec((t, HEAD_DIM), col),
        out_shape=jax.ShapeDtypeStruct((t, 3 * heads * HEAD_DIM), F32),
        compiler_params=_params("parallel", "parallel"),
    )(proj, conv_w)


def _gdn_pre_backward(proj, conv_w, dqkv, heads):
    t = proj.shape[0]
    kk = conv_w.shape[0]

    def body(x_ref, w_ref, dy_ref, dx_ref, dw_ref):
        normalise = pl.program_id(0) < 2
        x = x_ref[...]
        w = w_ref[...]
        dy = dy_ref[...]
        c, sg, s, r, y = _conv_silu_norm(x, w, normalise)
        ds_norm = r * (dy - y * jnp.sum(dy * y, axis=-1, keepdims=True))
        ds = jnp.where(normalise, ds_norm, dy)
        dc = ds * (sg * (1.0 + c * (1.0 - sg)))
        dx = dc * w[kk - 1:kk, :]
        rows = [None] * kk
        rows[kk - 1] = jnp.sum(dc * x, axis=0, keepdims=True)
        for j in range(kk - 1):
            lag = kk - 1 - j
            dx = dx + _shift_up(dc, lag) * w[j:j + 1, :]
            rows[j] = jnp.sum(dc * _shift_down(x, lag), axis=0, keepdims=True)
        dx_ref[...] = dx.astype(BF16)
        dw_ref[...] = jnp.concatenate(rows, axis=0)

    col = lambda s, h: (0, s * heads + h)
    return pl.pallas_call(
        body, name="gdn_pre_bwd", grid=(3, heads),
        in_specs=[pl.BlockSpec((t, HEAD_DIM), col), pl.BlockSpec((kk, HEAD_DIM), col),
                  pl.BlockSpec((t, HEAD_DIM), col)],
        out_specs=[pl.BlockSpec((t, HEAD_DIM), col), pl.BlockSpec((kk, HEAD_DIM), col)],
        out_shape=[jax.ShapeDtypeStruct((t, 3 * heads * HEAD_DIM), BF16),
                   jax.ShapeDtypeStruct((kk, 3 * heads * HEAD_DIM), F32)],
        compiler_params=_params("parallel", "parallel"),
    )(proj, conv_w, dqkv)


def _gate_vectors(a_log, dt_bias, heads):
    pad = lambda v: jnp.pad(v.astype(F32), ((0, 0), (heads, HEAD_DIM - 2 * heads)))
    return pad(jnp.exp(a_log.astype(F32))), pad(dt_bias)


def _softplus(x):
    return jnp.maximum(x, 0.0) + jnp.log(1.0 + jnp.exp(-jnp.abs(x)))


def _gates_epilogue(heads):
    def epi(acc, ex, out, i):
        lane = lax.broadcasted_iota(jnp.int32, acc.shape, 1)
        beta = _sigmoid(acc)
        g = -ex[0][...] * _softplus(acc + ex[1][...])
        out[0][...] = acc
        out[1][...] = jnp.where(lane < heads, beta, jnp.where(lane < 2 * heads, g, 0.0))
    return epi


def _gates_backward(ba, bg, dbg, ea, dtb, heads):
    t = ba.shape[0]

    def body(ba_ref, bg_ref, d_ref, ea_ref, dt_ref, dba_ref, dal_ref, ddt_ref):
        lane = lax.broadcasted_iota(jnp.int32, (t, HEAD_DIM), 1)
        bgv = bg_ref[...]
        d = d_ref[...]
        db = d * bgv * (1.0 - bgv)
        da = -d * ea_ref[...] * _sigmoid(ba_ref[...] + dt_ref[...])
        is_g = jnp.logical_and(lane >= heads, lane < 2 * heads)
        dba = jnp.where(lane < heads, db, jnp.where(is_g, da, 0.0))
        dba_ref[...] = dba.astype(BF16)
        dal_ref[...] = jnp.sum(jnp.where(is_g, d * bgv, 0.0), axis=0, keepdims=True)
        ddt_ref[...] = jnp.sum(jnp.where(is_g, da, 0.0), axis=0, keepdims=True)

    full = pl.BlockSpec((t, HEAD_DIM), lambda: (0, 0))
    vec = pl.BlockSpec((1, HEAD_DIM), lambda: (0, 0))
    return pl.pallas_call(
        body, name="gates_bwd", grid=(),
        in_specs=[full, full, full, vec, vec], out_specs=[full, vec, vec],
        out_shape=[jax.ShapeDtypeStruct((t, HEAD_DIM), BF16), jax.ShapeDtypeStruct((1, HEAD_DIM), F32),
                   jax.ShapeDtypeStruct((1, HEAD_DIM), F32)],
        compiler_params=pltpu.CompilerParams(vmem_limit_bytes=VMEM_LIMIT),
    )(ba, bg, dbg, ea, dtb)


class _Chunk:
    pass


def _split2(x):
    hi = x.astype(BF16)
    return hi, (x - hi.astype(F32)).astype(BF16)


def _split3(x):
    hi = x.astype(BF16)
    rest = x - hi.astype(F32)
    mid = rest.astype(BF16)
    return hi, mid, (rest - mid.astype(F32)).astype(BF16)


def _dot_mask(mask, x, ta=False):
    hi, mid, lo = _split3(x)
    return _bdot(mask, hi, ta=ta) + (_bdot(mask, mid, ta=ta) + _bdot(mask, lo, ta=ta))


def _transpose_by_identity(x):
    r = x.shape[0]
    eye = (lax.broadcasted_iota(jnp.int32, (r, r), 0) == lax.broadcasted_iota(jnp.int32, (r, r), 1)).astype(BF16)
    hi, mid, lo = _split3(x)
    return _bdot(hi, eye, ta=True) + (_bdot(mid, eye, ta=True) + _bdot(lo, eye, ta=True))


def _dot22(a, b, ta=False, tb=False):
    ah, al = _split2(a)
    bh, bl = _split2(b)
    return _bdot(ah, bh, ta, tb) + (_bdot(ah, bl, ta, tb) + _bdot(al, bh, ta, tb))


def _chunk_gates(bg, heads):
    n = CHUNK
    row = lax.broadcasted_iota(jnp.int32, (n, n), 0)
    col = lax.broadcasted_iota(jnp.int32, (n, n), 1)
    lane = lax.broadcasted_iota(jnp.int32, bg.shape, 1)
    graw = jnp.where(jnp.logical_and(lane >= heads, lane < 2 * heads), bg, 0.0)
    gc = _dot_mask((row >= col).astype(BF16), graw)
    return gc, _transpose_by_identity(gc)


def _in_lockstep(generators):
    results = [None] * len(generators)
    live = list(enumerate(generators))
    while live:
        still = []
        for i, gen in live:
            try:
                next(gen)
                still.append((i, gen))
            except StopIteration as stop:
                results[i] = stop.value
        live = still
    return results


def _chunk_local(q, k, v, beta, gc, grow):
    c = _Chunk()
    n = CHUNK
    row = lax.broadcasted_iota(jnp.int32, (n, n), 0)
    col = lax.broadcasted_iota(jnp.int32, (n, n), 1)
    c.tri = row >= col
    c.strict = row > col
    eye = row == col
    c.gcb = jnp.broadcast_to(gc, (n, HEAD_DIM))
    c.decay = jnp.where(c.tri, jnp.exp(jnp.where(c.tri, gc - grow, 0.0)), 0.0)
    c.eg = jnp.exp(c.gcb)
    glast = c.gcb[n - 1:n, :]
    c.egl = jnp.exp(glast)
    c.ekl = jnp.exp(glast - c.gcb)
    c.beta = beta
    c.q = q * (HEAD_DIM ** -0.5)
    c.k = k
    c.v = v
    c.kb = k * beta
    c.vb = v * beta
    c.kg = c.kb * c.eg
    both = _bdot(jnp.concatenate([c.kb, c.q], axis=0), k, tb=True)
    yield
    c.L = jnp.where(c.strict, both[:n] * c.decay, 0.0)
    c.A = jnp.where(c.tri, both[n:] * c.decay, 0.0)
    x = -c.L
    tinv = eye.astype(F32) + x
    p = _dot22(x, x)
    yield
    for _ in range(int(math.log2(n)) - 2):
        both = _dot22(jnp.concatenate([p, tinv], axis=0), p)
        yield
        p, tinv = both[:n], tinv + both[n:]
    c.T = tinv + _dot22(tinv, p)
    yield
    tinv = c.T
    uw = _dot22(tinv, jnp.concatenate([c.vb, c.kg], axis=1))
    yield
    c.u, c.w = uw[:, :HEAD_DIM], uw[:, HEAD_DIM:]
    c.qg = c.q * c.eg
    c.kdec = k * c.ekl
    return c


def _gdn_core(qkv, bg, heads):
    t = qkv.shape[0]
    nchunk = t // CHUNK

    gw = heads * HEAD_DIM

    def body(qkv_ref, bg_ref, o_ref, s_ref, state):
        @pl.when(pl.program_id(0) == 0)
        def _():
            state[...] = jnp.zeros_like(state)

        bg_v = bg_ref[...]
        gc_all, gc_rows = _chunk_gates(bg_v, heads)
        def one_head(h):
            col = lambda s: pl.ds(s * gw + h * HEAD_DIM, HEAD_DIM)
            c = yield from _chunk_local(qkv_ref[:, col(0)], qkv_ref[:, col(1)], qkv_ref[:, col(2)], bg_v[:, h:h + 1],
                                        gc_all[:, heads + h:heads + h + 1], gc_rows[heads + h:heads + h + 1, :])
            s0 = state[h]
            v_new = c.u - _bdot(c.w, s0)
            yield
            o = _bdot(c.qg, s0) + _bdot(c.A, v_new)
            return s0, o, s0 * c.egl + _bdot(c.kdec, v_new, ta=True)

        results = _in_lockstep([one_head(h) for h in range(heads)])
        for h, (s0, o, s1) in enumerate(results):
            s_ref[h, 0] = s0
            o_ref[:, pl.ds(h * HEAD_DIM, HEAD_DIM)] = o
            state[h] = s1

    return pl.pallas_call(
        body, name="gdn_core", grid=(nchunk,),
        in_specs=[pl.BlockSpec((CHUNK, 3 * gw), lambda n: (n, 0)), pl.BlockSpec((CHUNK, HEAD_DIM), lambda n: (n, 0))],
        out_specs=[pl.BlockSpec((CHUNK, gw), lambda n: (n, 0)),
                   pl.BlockSpec((heads, 1, HEAD_DIM, HEAD_DIM), lambda n: (0, n, 0, 0))],
        out_shape=[jax.ShapeDtypeStruct((t, gw), F32),
                   jax.ShapeDtypeStruct((heads, nchunk, HEAD_DIM, HEAD_DIM), F32)],
        scratch_shapes=[pltpu.VMEM((heads, HEAD_DIM, HEAD_DIM), F32)],
        compiler_params=_params("arbitrary"),
    )(qkv, bg)


def _gdn_core_backward(qkv, bg, states, do, heads):
    t = qkv.shape[0]
    nchunk = t // CHUNK
    n = CHUNK

    def one_head(chunk_local, s0, d_out, ds1):
        c = yield from chunk_local
        v_new = c.u - _bdot(c.w, s0)
        dqg = _bdot(d_out, s0, tb=True)
        ds0 = _bdot(c.qg, d_out, ta=True) + ds1 * c.egl
        dv_new = _bdot(c.A, d_out, ta=True) + _bdot(c.kdec, ds1)
        yield
        dA = jnp.where(c.tri, _bdot(d_out, v_new, tb=True), 0.0)
        dkdec = _bdot(v_new, ds1, tb=True)
        dgl = jnp.sum(jnp.sum(ds1 * s0, axis=1, keepdims=True), axis=0, keepdims=True) * c.egl
        dw = -_bdot(dv_new, s0, tb=True)
        ds0 = ds0 - _bdot(c.w, dv_new, ta=True)
        yield
        both = _dot22(c.T, jnp.concatenate([dv_new, dw], axis=1), ta=True)
        yield
        dvb, dkg = both[:, :HEAD_DIM], both[:, HEAD_DIM:]
        dL = jnp.where(c.strict, -(_bdot(dvb, c.u, tb=True) + _bdot(dkg, c.w, tb=True)), 0.0)
        yield
        dm1 = dL * c.decay
        dkb = _bdot(dm1, c.k) + dkg * c.eg
        dk = _bdot(dm1, c.kb, ta=True)
        dm2 = dA * c.decay
        dq = _bdot(dm2, c.k) + dqg * c.eg
        dk = dk + _bdot(dm2, c.q, ta=True) + dkdec * c.ekl + dkb * c.beta
        pm = dL * c.L + dA * c.A
        ones = jnp.ones((n, HEAD_DIM), BF16)
        pm_hi, pm_lo = _split2(pm)
        colsum = _bdot(pm_hi, ones, ta=True) + _bdot(pm_lo, ones, ta=True)
        tk_ = jnp.sum(dkdec * c.kdec, axis=1, keepdims=True)
        dgc = (jnp.sum(pm, axis=1, keepdims=True) - colsum
               + jnp.sum(dqg * c.qg, axis=1, keepdims=True)
               - tk_
               + jnp.sum(dkg * c.kg, axis=1, keepdims=True))
        dgl = dgl + jnp.sum(tk_, axis=0, keepdims=True)
        rowi = lax.broadcasted_iota(jnp.int32, (n, HEAD_DIM), 0)
        dgc = dgc + jnp.where(rowi == n - 1, dgl, 0.0)
        dbeta = jnp.sum(dkb * c.k, axis=1, keepdims=True) + jnp.sum(dvb * c.v, axis=1, keepdims=True)
        return dq * (HEAD_DIM ** -0.5), dk, dvb * c.beta, dbeta, dgc, ds0

    gw = heads * HEAD_DIM

    def body(qkv_ref, bg_ref, s_ref, do_ref, dqkv_ref, dbg_ref, dstate):
        @pl.when(pl.program_id(0) == 0)
        def _():
            dstate[...] = jnp.zeros_like(dstate)

        bg_v = bg_ref[...]
        gc_all, gc_rows = _chunk_gates(bg_v, heads)
        lane = lax.broadcasted_iota(jnp.int32, (n, HEAD_DIM), 1)
        dgates = jnp.zeros((n, HEAD_DIM), F32)
        chains = []
        for h in range(heads):
            col = lambda s: pl.ds(s * gw + h * HEAD_DIM, HEAD_DIM)
            c = _chunk_local(qkv_ref[:, col(0)], qkv_ref[:, col(1)], qkv_ref[:, col(2)], bg_v[:, h:h + 1],
                             gc_all[:, heads + h:heads + h + 1], gc_rows[heads + h:heads + h + 1, :])
            chains.append(one_head(c, s_ref[h, 0], do_ref[:, pl.ds(h * HEAD_DIM, HEAD_DIM)], dstate[h]))
        results = _in_lockstep(chains)
        for h, (dq, dk, dv, dbeta, dgc, ds0) in enumerate(results):
            dgates = jnp.where(lane == h, dbeta, jnp.where(lane == heads + h, dgc, dgates))
        for h, (dq, dk, dv, dbeta, dgc, ds0) in enumerate(results):
            dqkv_ref[:, pl.ds(h * HEAD_DIM, HEAD_DIM)] = dq
            dqkv_ref[:, pl.ds(gw + h * HEAD_DIM, HEAD_DIM)] = dk
            dqkv_ref[:, pl.ds(2 * gw + h * HEAD_DIM, HEAD_DIM)] = dv
            dstate[h] = ds0
        row = lax.broadcasted_iota(jnp.int32, (n, n), 0)
        colm = lax.broadcasted_iota(jnp.int32, (n, n), 1)
        draw = _dot_mask((row >= colm).astype(BF16), dgates, ta=True)
        dbg_ref[...] = jnp.where(lane < heads, dgates, draw)

    last = nchunk - 1
    return pl.pallas_call(
        body, name="gdn_core_bwd", grid=(nchunk,),
        in_specs=[pl.BlockSpec((CHUNK, 3 * gw), lambda i: (last - i, 0)),
                  pl.BlockSpec((CHUNK, HEAD_DIM), lambda i: (last - i, 0)),
                  pl.BlockSpec((heads, 1, HEAD_DIM, HEAD_DIM), lambda i: (0, last - i, 0, 0)),
                  pl.BlockSpec((CHUNK, gw), lambda i: (last - i, 0))],
        out_specs=[pl.BlockSpec((CHUNK, 3 * gw), lambda i: (last - i, 0)),
                   pl.BlockSpec((CHUNK, HEAD_DIM), lambda i: (last - i, 0))],
        out_shape=[jax.ShapeDtypeStruct((t, 3 * gw), F32), jax.ShapeDtypeStruct((t, HEAD_DIM), F32)],
        scratch_shapes=[pltpu.VMEM((heads, HEAD_DIM, HEAD_DIM), F32)],
        compiler_params=_params("arbitrary"),
    )(qkv, bg, states, do)


def _gdn_post(o, proj, z_col0, norm_w, heads, tt):
    t = o.shape[0]
    zb = z_col0 // HEAD_DIM

    def body(o_ref, z_ref, w_ref, out_ref):
        ov = o_ref[...]
        z = z_ref[...]
        rms = lax.rsqrt(jnp.mean(ov * ov, axis=-1, keepdims=True) + NORM_EPS)
        out_ref[...] = (ov * rms * w_ref[...] * (z * _sigmoid(z))).astype(BF16)

    return pl.pallas_call(
        body, name="gdn_post", grid=(t // tt, heads),
        in_specs=[pl.BlockSpec((tt, HEAD_DIM), lambda i, h: (i, h)),
                  pl.BlockSpec((tt, HEAD_DIM), lambda i, h: (i, zb + h)),
                  pl.BlockSpec((1, HEAD_DIM), lambda i, h: (0, 0))],
        out_specs=pl.BlockSpec((tt, HEAD_DIM), lambda i, h: (i, h)),
        out_shape=jax.ShapeDtypeStruct((t, heads * HEAD_DIM), BF16),
        compiler_params=_params("parallel", "parallel"),
    )(o, proj, norm_w)


def _gdn_post_backward(dcat, o, proj, z_col0, norm_w, heads, tt):
    t = o.shape[0]
    zb = z_col0 // HEAD_DIM

    def body(d_ref, o_ref, z_ref, w_ref, do_ref, dz_ref, dw_ref):
        d = d_ref[...]
        ov = o_ref[...]
        z = z_ref[...]
        w = w_ref[...]
        rms = lax.rsqrt(jnp.mean(ov * ov, axis=-1, keepdims=True) + NORM_EPS)
        ohat = ov * rms
        sg = _sigmoid(z)
        gate = z * sg
        dz_ref[...] = (d * ohat * w * (sg * (1.0 + z * (1.0 - sg)))).astype(BF16)
        don = d * gate
        dohat = don * w
        do_ref[...] = rms * (dohat - ohat * jnp.mean(dohat * ohat, axis=-1, keepdims=True))
        dw = jnp.sum(don * ohat, axis=0, keepdims=True)
        first = jnp.logical_and(pl.program_id(0) == 0, pl.program_id(1) == 0)

        @pl.when(first)
        def _():
            dw_ref[...] = dw

        @pl.when(jnp.logical_not(first))
        def _():
            dw_ref[...] += dw

    blk = pl.BlockSpec((tt, HEAD_DIM), lambda i, h: (i, h))
    return pl.pallas_call(
        body, name="gdn_post_bwd", grid=(t // tt, heads),
        in_specs=[blk, blk, pl.BlockSpec((tt, HEAD_DIM), lambda i, h: (i, zb + h)),
                  pl.BlockSpec((1, HEAD_DIM), lambda i, h: (0, 0))],
        out_specs=[blk, blk, pl.BlockSpec((1, HEAD_DIM), lambda i, h: (0, 0))],
        out_shape=[jax.ShapeDtypeStruct((t, heads * HEAD_DIM), F32),
                   jax.ShapeDtypeStruct((t, heads * HEAD_DIM), BF16),
                   jax.ShapeDtypeStruct((1, HEAD_DIM), F32)],
        compiler_params=_params("arbitrary", "arbitrary"),
    )(dcat, o, proj, norm_w)


def _pool_select(levels, group):
    out = levels[-1]
    for gi in range(len(levels) - 2, -1, -1):
        out = jnp.where(group == gi, levels[gi], out)
    return out


def _pool_counts(t, width, group):
    pos = lax.broadcasted_iota(jnp.int32, (t, width), 0)
    win = jnp.left_shift(2, group)
    return jnp.minimum(pos + 1, win).astype(F32)


def _pooled(p, group):
    levels, s, step = [], p, 1
    for _ in POOL_WINDOWS:
        s = s + _shift_down(s, step)
        levels.append(s)
        step *= 2
    cnt = _pool_counts(p.shape[0], p.shape[1], group)
    return _pool_select(levels, group) / cnt - p, cnt


def _pool_forward(proj, p_col0, pool_w, pool_scale):
    t = proj.shape[0]
    groups, cg, _ = pool_w.shape
    pb = p_col0 // cg

    def body(p_ref, w_ref, s_ref, o_ref):
        pooled, _ = _pooled(p_ref[...], pl.program_id(0))
        o_ref[...] = (_bdot(pooled, w_ref[0]) * s_ref[...]).astype(BF16)

    return pl.pallas_call(
        body, name="pool_fwd", grid=(groups,),
        in_specs=[pl.BlockSpec((t, cg), lambda g: (0, pb + g)), pl.BlockSpec((1, cg, cg), lambda g: (g, 0, 0)),
                  pl.BlockSpec((1, cg), lambda g: (0, g))],
        out_specs=pl.BlockSpec((t, cg), lambda g: (0, g)),
        out_shape=jax.ShapeDtypeStruct((t, groups * cg), BF16),
        compiler_params=_params("parallel"),
    )(proj, pool_w, pool_scale)


def _pool_backward(dcat, d_col0, proj, p_col0, pool_w, pool_scale):
    t = proj.shape[0]
    groups, cg, _ = pool_w.shape
    pb = p_col0 // cg
    db = d_col0 // cg

    def body(d_ref, p_ref, w_ref, s_ref, dp_ref, dw_ref, ds_ref):
        group = pl.program_id(0)
        pooled, cnt = _pooled(p_ref[...], group)
        w = w_ref[0]
        d = d_ref[...]
        mixed = _bdot(pooled, w)
        ds_ref[...] = jnp.sum(d * mixed, axis=0, keepdims=True)
        dmixed = d * s_ref[...]
        dw_ref[0] = _bdot(pooled, dmixed, ta=True)
        dpooled = _bdot(dmixed, w, tb=True)
        levels, s, step = [], dpooled / cnt, 1
        for _ in POOL_WINDOWS:
            s = s + _shift_up(s, step)
            levels.append(s)
            step *= 2
        dp_ref[...] = (_pool_select(levels, group) - dpooled).astype(BF16)

    return pl.pallas_call(
        body, name="pool_bwd", grid=(groups,),
        in_specs=[pl.BlockSpec((t, cg), lambda g: (0, db + g)), pl.BlockSpec((t, cg), lambda g: (0, pb + g)),
                  pl.BlockSpec((1, cg, cg), lambda g: (g, 0, 0)), pl.BlockSpec((1, cg), lambda g: (0, g))],
        out_specs=[pl.BlockSpec((t, cg), lambda g: (0, g)), pl.BlockSpec((1, cg, cg), lambda g: (g, 0, 0)),
                   pl.BlockSpec((1, cg), lambda g: (0, g))],
        out_shape=[jax.ShapeDtypeStruct((t, groups * cg), BF16), jax.ShapeDtypeStruct((groups, cg, cg), F32),
                   jax.ShapeDtypeStruct((1, groups * cg), F32)],
        compiler_params=_params("parallel"),
    )(dcat, proj, pool_w, pool_scale)


def _attention(q, k, v, tq):
    t, d = q.shape
    m = k.shape[0]
    dh = d // XATTN_HEADS
    scale = dh ** -0.5

    def body(q_ref, k_ref, v_ref, o_ref):
        s = _bdot(q_ref[...], k_ref[...], tb=True) * scale
        s = s - jnp.max(s, axis=-1, keepdims=True)
        e = jnp.exp(s)
        p = e / jnp.sum(e, axis=-1, keepdims=True)
        o_ref[...] = _bdot(p, v_ref[...]).astype(BF16)

    return pl.pallas_call(
        body, name="xattn_fwd", grid=(XATTN_HEADS, t // tq),
        in_specs=[pl.BlockSpec((tq, dh), lambda h, i: (i, h)), pl.BlockSpec((m, dh), lambda h, i: (0, h)),
                  pl.BlockSpec((m, dh), lambda h, i: (0, h))],
        out_specs=pl.BlockSpec((tq, dh), lambda h, i: (i, h)),
        out_shape=jax.ShapeDtypeStruct((t, d), BF16),
        compiler_params=_params("parallel", "parallel"),
    )(q, k, v)


def _attention_backward(q, k, v, do, tq):
    t, d = q.shape
    m = k.shape[0]
    dh = d // XATTN_HEADS
    scale = dh ** -0.5

    def body(q_ref, k_ref, v_ref, do_ref, dq_ref, dk_ref, dv_ref, dk_acc, dv_acc):
        i = pl.program_id(1)
        qv, kv, vv, dov = q_ref[...], k_ref[...], v_ref[...], do_ref[...]
        s = _bdot(qv, kv, tb=True) * scale
        s = s - jnp.max(s, axis=-1, keepdims=True)
        e = jnp.exp(s)
        p = e / jnp.sum(e, axis=-1, keepdims=True)
        dp = _bdot(dov, vv, tb=True)
        ds = p * (dp - jnp.sum(dp * p, axis=-1, keepdims=True)) * scale
        dq_ref[...] = _bdot(ds, kv).astype(BF16)
        dv_part = _bdot(p, dov, ta=True)
        dk_part = _bdot(ds, qv, ta=True)

        @pl.when(i == 0)
        def _():
            dk_acc[...] = dk_part
            dv_acc[...] = dv_part

        @pl.when(i > 0)
        def _():
            dk_acc[...] += dk_part
            dv_acc[...] += dv_part

        @pl.when(i == pl.num_programs(1) - 1)
        def _():
            dk_ref[...] = dk_acc[...].astype(BF16)
            dv_ref[...] = dv_acc[...].astype(BF16)

    qblk = pl.BlockSpec((tq, dh), lambda h, i: (i, h))
    kblk = pl.BlockSpec((m, dh), lambda h, i: (0, h))
    return pl.pallas_call(
        body, name="xattn_bwd", grid=(XATTN_HEADS, t // tq),
        in_specs=[qblk, kblk, kblk, qblk],
        out_specs=[qblk, kblk, kblk],
        out_shape=[jax.ShapeDtypeStruct((t, d), BF16), jax.ShapeDtypeStruct((m, d), BF16),
                   jax.ShapeDtypeStruct((m, d), BF16)],
        scratch_shapes=[pltpu.VMEM((m, dh), F32), pltpu.VMEM((m, dh), F32)],
        compiler_params=_params("parallel", "arbitrary"),
    )(q, k, v, do)


def _loss_and_ln_backward(xhat, rstd, gamma, beta, target, tm):
    t, d = xhat.shape

    def body(x_ref, r_ref, g_ref, b_ref, t_ref, du_ref, dub_ref, dg_ref, db_ref, loss_ref):
        xh = x_ref[...]
        g = g_ref[...]
        diff = xh * g + b_ref[...] - t_ref[...]
        part = jnp.sum(jnp.sum(diff * diff, axis=1, keepdims=True), axis=0, keepdims=True) * (0.5 / d)
        dy = diff * (1.0 / d)
        du, dg, db = _ln_backward_math(dy, xh, r_ref[...], g)
        du_ref[...] = du
        dub_ref[...] = du.astype(BF16)
        lossrow = jnp.broadcast_to(part, (1, HEAD_DIM))
        first = pl.program_id(0) == 0

        @pl.when(first)
        def _():
            dg_ref[...] = dg
            db_ref[...] = db
            loss_ref[...] = lossrow

        @pl.when(jnp.logical_not(first))
        def _():
            dg_ref[...] += dg
            db_ref[...] += db
            loss_ref[...] += lossrow

    row = pl.BlockSpec((tm, d), lambda i: (i, 0))
    vec = pl.BlockSpec((1, d), lambda i: (0, 0))
    return pl.pallas_call(
        body, name="loss_ln3_bwd", grid=(t // tm,),
        in_specs=[row, pl.BlockSpec((tm, 1), lambda i: (i, 0)), vec, vec, row],
        out_specs=[row, row, vec, vec, pl.BlockSpec((1, HEAD_DIM), lambda i: (0, 0))],
        out_shape=[jax.ShapeDtypeStruct((t, d), F32), jax.ShapeDtypeStruct((t, d), BF16),
                   jax.ShapeDtypeStruct((1, d), F32), jax.ShapeDtypeStruct((1, d), F32),
                   jax.ShapeDtypeStruct((1, HEAD_DIM), F32)],
        compiler_params=_params("arbitrary"),
    )(xhat, rstd, gamma, beta, target)


def _after(token, a):
    return a if token is None else a + token[:1, :1].astype(a.dtype)


def _pick(n, prefs):
    for p in prefs:
        if n % p == 0:
            return p
    return n


def _local_step(x, mem, target, w):
    t, d = x.shape
    heads = w["a_log"].shape[1]
    gw = heads * HEAD_DIM
    groups, cg, _ = w["pool_w"].shape
    pw = groups * cg
    n_main = 4 * gw + pw
    in_cols = n_main + 2 * heads
    s_in = w["w_in_t"].shape[0]

    tm = _pick(t, (512, 256, 128))
    tm_ln = _pick(t, (256, 128))
    tm_big = _pick(t, (1024, 512, 256, 128))
    tk = _pick(d, K_STEPS)

    w_in_t = w["w_in_t"].reshape(in_cols, d)
    w_p_t = w_in_t[4 * gw + 2 * heads:]
    w_ba_t = jnp.pad(w_in_t[4 * gw:4 * gw + 2 * heads], ((0, HEAD_DIM - 2 * heads), (0, 0)))
    x_bf = x.astype(BF16)
    mem_bf = mem.astype(BF16)

    tn_d = _pick(d, (1024, 512, 256, 128))
    proj = _plain("proj_main", x_bf, w_in_t, tb=True, n_used=4 * gw, tm=tm_big, tn=_pick(4 * gw, (1024, 512, 256, 128)),
                  tk=tk, out_dtype=F32)
    pproj = _plain("proj_pool", x_bf, w_p_t, tb=True, tm=tm_big, tn=_pick(pw, (1024, 512, 256, 128)), tk=tk, out_dtype=F32)
    ea, dtb = _gate_vectors(w["a_log"], w["dt_bias"], heads)
    vec128 = lambda i, j: (0, 0)
    ba, bg = _matmul(
        "proj_gates", x_bf, w_ba_t, tb=True, tm=tm, tn=HEAD_DIM, tk=tk,
        extra=[(ea, (1, HEAD_DIM), vec128), (dtb, (1, HEAD_DIM), vec128)],
        outs=[(jax.ShapeDtypeStruct((t, HEAD_DIM), F32), (tm, HEAD_DIM), _tile)] * 2,
        epilogue=_gates_epilogue(heads))
    qkv = _gdn_pre(proj, w["conv_w"], heads)
    o_gdn, states = _gdn_core(qkv, bg, heads)
    cat_g = _gdn_post(o_gdn, proj, 3 * gw, w["gdn_norm_w"], heads, tm)
    cat_p = _pool_forward(pproj, 0, w["pool_w"], w["pool_scale"])
    cat = jnp.concatenate([cat_g, cat_p], axis=1)
    w = {**w, **(yield ("weights", 1, cat))}
    h1, h1_bf, xhat1, rstd1 = _ln_forward("mix_ln1", cat, w["w_out"], x, w["ln1_g"], w["ln1_b"], tm=tm_ln, tk=tk)

    q = _plain("xattn_q", h1_bf, w["xq_w"], tm=tm, tn=tn_d, tk=tk, out_dtype=BF16)
    mlen = mem.shape[0]
    tm_mem = _pick(mlen, (256, 128))
    k = _plain("xattn_k", mem_bf, w["xk_w"], tm=tm_mem, tn=tn_d, tk=tk, out_dtype=BF16)
    v = _plain("xattn_v", mem_bf, w["xv_w"], tm=tm_mem, tn=tn_d, tk=tk, out_dtype=BF16)
    att = _attention(q, k, v, tm)
    h2, h2_bf, xhat2, rstd2 = _ln_forward("xo_ln2", att, w["xo_w"], h1, w["ln2_g"], w["ln2_b"], tm=tm_ln, tk=tk)

    w = {**w, **(yield ("weights", 2, h2_bf))}
    s_up = w["w_up3"].shape[0]
    ff = s_up * w["w_up3"].shape[2]
    tn_f = _pick(ff // s_up, (1024, 512, 256, 128))

    def up_epi(acc, ex, out, i):
        r = jnp.maximum(acc, 0.0)
        out[0][...] = (r * r).astype(BF16)
        out[1][...] = (2.0 * r).astype(BF16)

    act, act_grad = _matmul(
        "mlp_up", h2_bf, w["w_up3"], b_blocks=s_up, tm=tm_big, tn=tn_f, tk=tk,
        outs=[(jax.ShapeDtypeStruct((t, ff), BF16), (tm_big, tn_f), _tile)] * 2, epilogue=up_epi)
    w = {**w, **(yield ("weights", 3, act))}
    tk_f = _pick(ff, K_STEPS)
    xhat3, rstd3 = _ln_forward("down_ln3", act, w["w_down"], h2, w["ln3_g"], w["ln3_b"], tm=tm, tk=tk_f, want_h=False)

    grads = {}
    du3, du3_bf, grads["ln3_g"], grads["ln3_b"], loss = _loss_and_ln_backward(
        xhat3, rstd3, w["ln3_g"], w["ln3_b"], target, tm_ln)

    def dup_epi(acc, ex, out, i):
        out[0][...] = (acc * ex[0][...].astype(F32)).astype(BF16)

    dup = _matmul(
        "mlp_down_dx", du3_bf, w["w_down"], tb=True, tm=tm_big, tn=tn_f, tk=tk,
        extra=[(act_grad, (tm_big, tn_f), _tile)],
        outs=[(jax.ShapeDtypeStruct((t, ff), BF16), (tm_big, tn_f), _tile)], epilogue=dup_epi)[0]
    tk_t = _pick(t, K_STEPS)
    tm_w = _pick(d, (512, 256, 128))
    grads["w_down"] = _plain("mlp_down_dw", act, du3_bf, ta=True, tm=_pick(ff, (512, 256, 128)), tn=tn_d, tk=tk_t,
                             out_dtype=F32)
    grads["w_up3"] = _plain("mlp_up_dw", h2_bf, dup, ta=True, tm=tm_w, tn=tn_f, tk=tk_t, out_dtype=F32, out3=s_up)
    token = yield ("grads", 0, {n: grads.pop(n) for n in ("w_down", "w_up3")})
    du2, du2_bf, grads["ln2_g"], grads["ln2_b"] = _ln_backward(
        "mlp_up_dx_ln2", dup, w["w_up3"], du3, xhat2, rstd2, _after(token, w["ln2_g"]), tm=tm,
        tk=_pick(ff // s_up, K_STEPS[1:]), b_blocks=s_up)
    token = yield ("poll", 0, du2_bf)

    grads["xo_w"] = _plain("xo_dw", att, du2_bf, ta=True, tm=tm_w, tn=tn_d, tk=tk_t, out_dtype=F32)
    datt = _plain("xo_dx", du2_bf, w["xo_w"], tb=True, tm=tm, tn=tn_d, tk=tk, out_dtype=BF16)
    dq, dk, dv = _attention_backward(q, k, v, datt, tm)
    tk_m = _pick(mlen, (256, 128))
    grads["xq_w"] = _plain("xq_dw", h1_bf, dq, ta=True, tm=tm_w, tn=tn_d, tk=tk_t, out_dtype=F32)
    grads["xk_w"] = _plain("xk_dw", mem_bf, dk, ta=True, tm=tm_w, tn=tn_d, tk=tk_m, out_dtype=F32)
    grads["xv_w"] = _plain("xv_dw", mem_bf, dv, ta=True, tm=tm_w, tn=tn_d, tk=tk_m, out_dtype=F32)
    du1, du1_bf, grads["ln1_g"], grads["ln1_b"] = _ln_backward(
        "xq_dx_ln1", dq, w["xq_w"], du2, xhat1, rstd1, _after(token, w["ln1_g"]), tm=tm_ln, tk=tk)

    grads["w_out"] = _plain("out_dw", cat, du1_bf, ta=True, tm=tm_w, tn=tn_d, tk=tk_t, out_dtype=F32)
    token = yield ("grads", 1, {n: grads.pop(n) for n in ("xo_w", "xq_w", "xk_w", "xv_w", "w_out")})
    dcat = _plain("out_dx", du1_bf, w["w_out"], tb=True, tm=tm, tn=tn_d, tk=tk, out_dtype=F32)
    dp, grads["pool_w"], grads["pool_scale"] = _pool_backward(dcat, gw, pproj, 0, w["pool_w"],
                                                              _after(token, w["pool_scale"]))
    do_gdn, dz, grads["gdn_norm_w"] = _gdn_post_backward(dcat, o_gdn, proj, 3 * gw, _after(token, w["gdn_norm_w"]),
                                                         heads, tm)
    token = yield ("poll", 1, do_gdn)
    dqkv, dbg = _gdn_core_backward(qkv, _after(token, bg), states, do_gdn, heads)
    dqkv_pre, grads["conv_w"] = _gdn_pre_backward(proj, w["conv_w"], dqkv, heads)
    dba, dalog_row, ddt_row = _gates_backward(ba, bg, dbg, ea, dtb, heads)
    grads["a_log"] = dalog_row[:, heads:2 * heads]
    grads["dt_bias"] = ddt_row[:, heads:2 * heads]

    dproj = jnp.concatenate([dqkv_pre, dz, dp], axis=1)
    dw_main = _plain("proj_dw", dproj, x_bf, ta=True, tm=_pick(n_main, (512, 256, 128)), tn=tn_d, tk=tk_t, out_dtype=F32)
    dw_ba = _plain("proj_gates_dw", dba, x_bf, ta=True, tm=HEAD_DIM, tn=tn_d, tk=tk_t, out_dtype=F32)
    dw_in_t = jnp.concatenate([dw_main[:4 * gw], dw_ba[:2 * heads], dw_main[4 * gw:]], axis=0)
    grads["w_in_t"] = dw_in_t.reshape(s_in, in_cols // s_in, d)

    def dx_epi(acc, ex, out, i):
        out[0][...] = acc + ex[1][...] + ALPHA * ex[0][...]

    def add_epi(acc, ex, out, i):
        out[0][...] = acc + ex[0][...]

    token = yield ("grads", 2, {n: grads.pop(n) for n in ("w_in_t", "pool_w")})
    dx_gates = _plain("proj_gates_dx", dba, _after(token, w_ba_t), tm=tm, tn=tn_d, tk=HEAD_DIM, out_dtype=F32)
    out_tile = [(jax.ShapeDtypeStruct((t, d), F32), (tm, tn_d), _tile)]
    dx_pool = _matmul("proj_pool_dx", dp, w_p_t, tm=tm, tn=tn_d, tk=_pick(pw, K_STEPS),
                      extra=[(dx_gates, (tm, tn_d), _tile)], outs=out_tile, epilogue=add_epi)[0]
    grad_x = _matmul(
        "proj_dx", dproj, w_in_t, k_used=4 * gw, tm=tm, tn=tn_d, tk=_pick(4 * gw, K_STEPS),
        extra=[(du1, (tm, tn_d), _tile), (dx_pool, (tm, tn_d), _tile)], outs=out_tile, epilogue=dx_epi)[0]
    yield ("poll", 2, grad_x)
    return loss, grad_x, grads


def _adamw(name, w, g, m, v):
    r, c = w.shape
    if r % 8 == 0:
        tr = _pick(r, (256, 128, 64, 32, 16, 8))
        blk, steps = pl.BlockSpec((tr, c), lambda i: (i, 0)), r // tr
    else:
        tc = _pick(c, (256, 128))
        blk, steps = pl.BlockSpec((r, tc), lambda i: (0, i)), c // tc
    c1 = 1.0 - ADAM_B1 ** ADAM_STEP
    c2 = 1.0 - ADAM_B2 ** ADAM_STEP

    def body(w_ref, g_ref, m_ref, v_ref, d_ref, mo_ref, vo_ref):
        gv = g_ref[...]
        mn = ADAM_B1 * m_ref[...] + (1.0 - ADAM_B1) * gv
        vn = ADAM_B2 * v_ref[...] + (1.0 - ADAM_B2) * (gv * gv)
        d_ref[...] = -ADAM_LR * ((mn / c1) / (jnp.sqrt(vn / c2) + ADAM_EPS) + ADAM_WD * w_ref[...])
        mo_ref[...] = mn
        vo_ref[...] = vn

    return pl.pallas_call(
        body, name=name, grid=(steps,), in_specs=[blk] * 4, out_specs=[blk] * 3,
        out_shape=[jax.ShapeDtypeStruct((r, c), F32)] * 3,
        compiler_params=_params("parallel"),
    )(w, g, m, v)


def _place():
    x, y, c = lax.axis_index("x"), lax.axis_index("y"), lax.axis_index("c")
    chips = [(1 - x, y), (x, 1 - y), (1 - x, 1 - y)]
    return x, y, c, chips


HBM = pl.BlockSpec(memory_space=pltpu.HBM)


SEM = pl.BlockSpec(memory_space=pltpu.SEMAPHORE)
ANY = pl.BlockSpec(memory_space=pl.ANY)
EFFECT = pltpu.SideEffectType.DATAFLOW_SIDE_EFFECTING


def _in_hbm(a):
    return pltpu.with_memory_space_constraint(a, pltpu.HBM)


def _remote(src, dst, send_sem, recv_sem, to):
    return pltpu.make_async_remote_copy(src_ref=src, dst_ref=dst, send_sem=send_sem, recv_sem=recv_sem,
                                        device_id=to, device_id_type=MESH)


def _by_rows(rows):
    return rows % 32 == 0


def _half_shape(rows, cols):
    return (rows // 2, cols) if _by_rows(rows) else (rows, cols // 2)


def _half(ref, which, *lead):
    rows, cols = ref.shape[-2:]
    if _by_rows(rows):
        return ref.at[(*lead, pl.ds(which * (rows // 2), rows // 2))]
    return ref.at[(*lead, slice(None), pl.ds(which * (cols // 2), cols // 2))]


def _landed(lands, i, shard_index, which):
    return _half(lands[i], which, shard_index)


def _gather_start(shards, after):
    n = len(shards)
    lands = [lax.empty((N_SHARD,) + s.shape, s.dtype) for s in shards]

    def body(*refs):
        ins, zones = refs[:n], refs[n:2 * n]
        ici_send, ici_recv, own_send, own_recv = refs[2 * n + 1:2 * n + 5]
        token = refs[-1]
        x, y, c, chips = _place()
        me = 2 * x + y
        for i in range(n):
            for j, chip in enumerate(chips):
                _remote(_half(ins[i], c), _landed(zones, i, me, c), ici_send.at[3 * i + j],
                        ici_recv.at[3 * i + j], (*chip, c)).start()
        for i in range(n):
            _remote(ins[i], zones[i].at[me], own_send.at[i], own_recv.at[i], (x, y, 1 - c)).start()
        token[...] = jnp.zeros_like(token)

    dma = pltpu.SemaphoreType.DMA
    outs = pl.pallas_call(
        body, name="gather_start",
        in_specs=[HBM] * (2 * n) + [ANY],
        out_shape=(dma((3 * n,)), dma((3 * n,)), dma((n,)), dma((n,)),
                   *[pltpu.HBM(a.shape, a.dtype) for a in shards + lands], jax.ShapeDtypeStruct((8, LANES), F32)),
        out_specs=(SEM, SEM, SEM, SEM, *[HBM] * (2 * n), pl.BlockSpec(memory_space=pltpu.VMEM)),
        input_output_aliases={k: 4 + k for k in range(2 * n)},
        compiler_params=pltpu.CompilerParams(has_side_effects=EFFECT),
    )(*[_in_hbm(a) for a in shards + lands], after)
    sems = dict(zip(("ici_send", "ici_recv", "own_send", "own_recv"), outs[:4]))
    return sems, list(outs[4:4 + n]), list(outs[4 + n:4 + 2 * n]), outs[-1]


def _gather_forward(name, idx, lands, sems, after):
    n = len(idx)

    def body(*refs):
        zones = refs[:n]
        ici_recv = refs[n]
        fwd_send, fwd_recv = refs[n + 2], refs[n + 3]
        x, y, c, chips = _place()
        for k, i in enumerate(idx):
            for j, chip in enumerate(chips):
                half = _landed(zones, k, 2 * chip[0] + chip[1], c)
                _remote(half, half, fwd_send.at[3 * k + j], ici_recv.at[3 * i + j], (*chip, c)).wait_recv()
                _remote(half, half, fwd_send.at[3 * k + j], fwd_recv.at[3 * k + j], (x, y, 1 - c)).start()

    dma = pltpu.SemaphoreType.DMA
    outs = pl.pallas_call(
        body, name=name,
        in_specs=[HBM] * n + [SEM, ANY],
        out_shape=(dma((3 * n,)), dma((3 * n,)), *[pltpu.HBM(a.shape, a.dtype) for a in lands]),
        out_specs=(SEM, SEM, *[HBM] * n),
        input_output_aliases={k: 2 + k for k in range(n)},
        compiler_params=pltpu.CompilerParams(has_side_effects=EFFECT),
    )(*lands, sems["ici_recv"], after)
    return (outs[0], outs[1]), list(outs[2:])


def _gather_wait(name, idx, shards, lands, sems, fwd):
    n = len(idx)

    def body(*refs):
        ins, zones = refs[:n], refs[n:2 * n]
        ici_send, own_send, own_recv, fwd_send, fwd_recv = refs[2 * n:2 * n + 5]
        x, y, c, chips = _place()
        me = 2 * x + y
        for k, i in enumerate(idx):
            mine = _half(ins[k], c)
            for j, chip in enumerate(chips):
                theirs = 2 * chip[0] + chip[1]
                _remote(mine, _landed(zones, k, me, c), ici_send.at[3 * i + j], fwd_recv.at[3 * k + j],
                        (*chip, c)).wait_send()
                sent = _landed(zones, k, theirs, c)
                _remote(sent, sent, fwd_send.at[3 * k + j], fwd_recv.at[3 * k + j], (x, y, 1 - c)).wait_send()
                passed = _landed(zones, k, theirs, 1 - c)
                _remote(passed, passed, fwd_send.at[3 * k + j], fwd_recv.at[3 * k + j], (x, y, 1 - c)).wait_recv()
            own = _remote(ins[k], zones[k].at[me], own_send.at[i], own_recv.at[i], (x, y, 1 - c))
            own.wait_send()
            own.wait_recv()

    outs = pl.pallas_call(
        body, name=name,
        in_specs=[HBM] * (2 * n) + [SEM] * 5,
        out_shape=tuple(pltpu.HBM(a.shape, a.dtype) for a in lands),
        out_specs=tuple([HBM] * n),
        input_output_aliases={n + k: k for k in range(n)},
        compiler_params=pltpu.CompilerParams(has_side_effects=EFFECT),
    )(*shards, *lands, sems["ici_send"], sems["own_send"], sems["own_recv"], fwd[0], fwd[1])
    return list(outs)


def _all_reduce_small(name, slab, after=None):
    r, width = slab.shape
    ndev = 8

    def body(x_ref, after_ref, out_ref, buf, send_sems, recv_sems):
        x, y, c, _ = _place()
        me = 4 * x + 2 * y + c
        buf[me] = x_ref[...]
        copies = []
        for k in range(1, ndev):
            peer = jnp.bitwise_xor(me, k)
            to = (peer // 4, (peer // 2) % 2, peer % 2)
            cp = pltpu.make_async_remote_copy(src_ref=x_ref, dst_ref=buf.at[me], send_sem=send_sems.at[k - 1],
                                              recv_sem=recv_sems.at[k - 1], device_id=to, device_id_type=MESH)
            cp.start()
            copies.append(cp)
        for k in range(1, ndev):
            peer = jnp.bitwise_xor(me, k)
            pltpu.make_async_remote_copy(src_ref=x_ref, dst_ref=buf.at[peer], send_sem=send_sems.at[k - 1],
                                         recv_sem=recv_sems.at[k - 1], device_id=(x, y, c),
                                         device_id_type=MESH).wait_recv()
        for cp in copies:
            cp.wait_send()
        total = buf[0]
        for d in range(1, ndev):
            total = total + buf[d]
        out_ref[...] = total

    return pl.pallas_call(
        body, name=name,
        in_specs=[pl.BlockSpec(memory_space=pltpu.VMEM), ANY], out_specs=pl.BlockSpec(memory_space=pltpu.VMEM),
        out_shape=jax.ShapeDtypeStruct((r, width), F32),
        scratch_shapes=[pltpu.VMEM((ndev, r, width), F32), pltpu.SemaphoreType.DMA((ndev - 1,)),
                        pltpu.SemaphoreType.DMA((ndev - 1,))],
        compiler_params=pltpu.CompilerParams(vmem_limit_bytes=VMEM_LIMIT),
    )(slab, slab if after is None else after)


def _half_tiling(rows, cols):
    if _by_rows(rows):
        tr = _pick(rows // 2, (256, 128, 64, 32, 16))
        nb = (rows // 2) // tr
        return (tr, cols), nb, (lambda which, b: (which * nb + b, 0)), (lambda b: (b, 0))
    tc = _pick(cols // 2, (256, 128))
    nb = (cols // 2) // tc
    return (rows, tc), nb, (lambda which, b: (0, which * nb + b)), (lambda b: (0, b))


def _chip_partial(name, grad, other, core):
    s, r, cdim = grad.shape
    blk, nb, whole, within = _half_tiling(r, cdim)

    def body(core_ref, g_ref, o_ref, out_ref):
        out_ref[...] = (g_ref[...] + o_ref[...]).astype(BF16)

    return pl.pallas_call(
        body, name=name,
        grid_spec=pltpu.PrefetchScalarGridSpec(
            num_scalar_prefetch=1, grid=(s, nb),
            in_specs=[pl.BlockSpec((None,) + blk, lambda j, b, core_ref: (j,) + whole(core_ref[0], b)),
                      pl.BlockSpec((None,) + blk, lambda j, b, core_ref: (j,) + within(b))],
            out_specs=pl.BlockSpec((None,) + blk, lambda j, b, core_ref: (j,) + within(b))),
        out_shape=jax.ShapeDtypeStruct((s,) + _half_shape(r, cdim), BF16),
        compiler_params=_params("parallel", "parallel"),
    )(core, grad, other)


def _partial_copies(ins, zones, send_sems, recv_sems):
    x, y, c, chips = _place()
    return [_remote(ins[i].at[2 * chip[0] + chip[1]], zones[i].at[j], send_sems.at[3 * i + j],
                    recv_sems.at[3 * i + j], (*chip, c))
            for i in range(len(ins)) for j, chip in enumerate(chips)]


def _swap_copies(ins, zones, send_sems, recv_sems):
    x, y, c, _ = _place()
    copies = []
    for i in range(len(ins)):
        for s in range(N_SHARD):
            copies.append(_remote(_half(ins[i], 1 - c, s), zones[i].at[s],
                                  send_sems.at[N_SHARD * i + s], recv_sems.at[N_SHARD * i + s], (x, y, 1 - c)))
    return copies


def _exchange_start(name, plan, sources, lands, per_array):
    n = len(sources)
    lands = [lax.empty(shape, dtype) for shape, dtype in lands]

    def body(*refs):
        for cp in plan(refs[:n], refs[n:2 * n], refs[2 * n], refs[2 * n + 1]):
            cp.start()
        refs[-1][...] = jnp.zeros_like(refs[-1])

    dma = pltpu.SemaphoreType.DMA
    outs = pl.pallas_call(
        body, name=name,
        in_specs=[HBM] * (2 * n),
        out_shape=(dma((per_array * n,)), dma((per_array * n,)),
                   *[pltpu.HBM(a.shape, a.dtype) for a in list(sources) + lands], jax.ShapeDtypeStruct((8, LANES), F32)),
        out_specs=(SEM, SEM, *[HBM] * (2 * n), pl.BlockSpec(memory_space=pltpu.VMEM)),
        input_output_aliases={k: 2 + k for k in range(2 * n)},
        compiler_params=pltpu.CompilerParams(has_side_effects=EFFECT),
    )(*[_in_hbm(a) for a in list(sources) + lands])
    return (outs[0], outs[1]), list(outs[2:2 + n]), list(outs[2 + n:2 + 2 * n]), outs[-1]


def _exchange_wait(name, plan, started, after):
    sems, partials, lands, _ = started
    n = len(partials)

    def body(*refs):
        for cp in plan(refs[:n], refs[n:2 * n], refs[2 * n], refs[2 * n + 1]):
            cp.wait_send()
            cp.wait_recv()

    outs = pl.pallas_call(
        body, name=name,
        in_specs=[HBM] * (2 * n) + [SEM, SEM] + [ANY] * len(after),
        out_shape=tuple(pltpu.HBM(a.shape, a.dtype) for a in lands),
        out_specs=tuple([HBM] * n),
        input_output_aliases={n + k: k for k in range(n)},
        compiler_params=pltpu.CompilerParams(has_side_effects=EFFECT),
    )(*partials, *lands, sems[0], sems[1], *after)
    return list(outs)


def _reduce_own(name, grad, other, received, where):
    s, r, cdim = grad.shape
    blk, nb, whole, within = _half_tiling(r, cdim)

    def body(where_ref, g_ref, o_ref, r_ref, out_ref):
        total = g_ref[...] + o_ref[...]
        for j in range(3):
            total = total + r_ref[j].astype(F32)
        out_ref[...] = total

    return pl.pallas_call(
        body, name=name,
        grid_spec=pltpu.PrefetchScalarGridSpec(
            num_scalar_prefetch=1, grid=(nb,),
            in_specs=[pl.BlockSpec((None,) + blk, lambda b, w_ref: (w_ref[0],) + whole(w_ref[1], b)),
                      pl.BlockSpec((None,) + blk, lambda b, w_ref: (w_ref[0],) + within(b)),
                      pl.BlockSpec((3,) + blk, lambda b, w_ref: (0,) + within(b))],
            out_specs=pl.BlockSpec(blk, lambda b, w_ref: whole(w_ref[1], b))),
        out_shape=jax.ShapeDtypeStruct((r, cdim), F32),
        compiler_params=_params("parallel"),
    )(where, grad, other, received)


def _join_start(name, halves):
    n = len(halves)

    def body(*refs):
        bufs, send_sems, recv_sems = refs[:n], refs[n], refs[n + 1]
        x, y, c, _ = _place()
        for i in range(n):
            mine = _half(bufs[i], c)
            _remote(mine, mine, send_sems.at[i], recv_sems.at[i], (x, y, 1 - c)).start()
        refs[-1][...] = jnp.zeros_like(refs[-1])

    dma = pltpu.SemaphoreType.DMA
    outs = pl.pallas_call(
        body, name=name,
        in_specs=[HBM] * n,
        out_shape=(dma((n,)), dma((n,)), *[pltpu.HBM(h.shape, F32) for h in halves], jax.ShapeDtypeStruct((8, LANES), F32)),
        out_specs=(SEM, SEM, *[HBM] * n, pl.BlockSpec(memory_space=pltpu.VMEM)),
        input_output_aliases={k: 2 + k for k in range(n)},
        compiler_params=pltpu.CompilerParams(has_side_effects=EFFECT),
    )(*[_in_hbm(h) for h in halves])
    return (outs[0], outs[1]), list(outs[2:2 + n]), outs[-1]


def _join_wait(name, started, after):
    sems, bufs, _ = started
    n = len(bufs)

    def body(*refs):
        bufs, send_sems, recv_sems = refs[:n], refs[n], refs[n + 1]
        x, y, c, _ = _place()
        for i in range(n):
            mine, theirs = _half(bufs[i], c), _half(bufs[i], 1 - c)
            _remote(mine, mine, send_sems.at[i], recv_sems.at[i], (x, y, 1 - c)).wait_send()
            _remote(theirs, theirs, send_sems.at[i], recv_sems.at[i], (x, y, 1 - c)).wait_recv()

    outs = pl.pallas_call(
        body, name=name,
        in_specs=[HBM] * n + [SEM, SEM] + [ANY] * len(after),
        out_shape=tuple(pltpu.HBM(b.shape, F32) for b in bufs),
        out_specs=tuple([HBM] * n),
        input_output_aliases={k: k for k in range(n)},
        compiler_params=pltpu.CompilerParams(has_side_effects=EFFECT),
    )(*bufs, sems[0], sems[1], *after)
    return list(outs)


BIG = ("w_in", "pool_w", "w_out", "xq_w", "xk_w", "xv_w", "xo_w", "w_up", "w_down")
GATHER_GROUPS = ((0, 1), (2, 3, 4, 5, 6), (7,), (8,))
SMALL = ("conv_w", "a_log", "dt_bias", "gdn_norm_w", "pool_scale", "ln1_g", "ln1_b", "ln2_g", "ln2_b", "ln3_g", "ln3_b")
ORDER = ("w_in", "conv_w", "a_log", "dt_bias", "gdn_norm_w", "pool_w", "pool_scale", "w_out", "ln1_g", "ln1_b",
         "xq_w", "xk_w", "xv_w", "xo_w", "ln2_g", "ln2_b", "w_up", "w_down", "ln3_g", "ln3_b")
LANES = 128


def _rows(flat_len):
    return -(-flat_len // LANES)


def _pack(pieces):
    out = []
    for p in pieces:
        flat = p.reshape(-1).astype(F32)
        out.append(jnp.pad(flat, (0, _rows(flat.shape[0]) * LANES - flat.shape[0])).reshape(-1, LANES))
    slab = jnp.concatenate(out, axis=0)
    return jnp.pad(slab, ((0, -slab.shape[0] % 8), (0, 0)))


def _unpack(slab, shapes):
    out, row = [], 0
    for shp in shapes:
        size = math.prod(shp)
        out.append(slab[row:row + _rows(size)].reshape(-1)[:size].reshape(shp))
        row += _rows(size)
    return out


TRANSPOSED = ("w_in",)


def _as2d(name, a):
    a = a[0]
    if name in TRANSPOSED:
        return jnp.swapaxes(a, 0, 1)
    return a.reshape(-1, a.shape[-1]) if a.ndim == 3 else a


def _from2d(name, a, shape):
    return (jnp.swapaxes(a, 0, 1) if name in TRANSPOSED else a).reshape(shape)


def kernel(x, mem, w_in, conv_w, a_log, dt_bias, gdn_norm_w, pool_w, pool_scale, w_out, ln1_g, ln1_b, xq_w, xk_w, xv_w, xo_w, ln2_g, ln2_b, w_up, w_down, ln3_g, ln3_b, loss_target, m_w_in, m_conv_w, m_a_log, m_dt_bias, m_gdn_norm_w, m_pool_w, m_pool_scale, m_w_out, m_ln1_g, m_ln1_b, m_xq_w, m_xk_w, m_xv_w, m_xo_w, m_ln2_g, m_ln2_b, m_w_up, m_w_down, m_ln3_g, m_ln3_b, v_w_in, v_conv_w, v_a_log, v_dt_bias, v_gdn_norm_w, v_pool_w, v_pool_scale, v_w_out, v_ln1_g, v_ln1_b, v_xq_w, v_xk_w, v_xv_w, v_xo_w, v_ln2_g, v_ln2_b, v_w_up, v_w_down, v_ln3_g, v_ln3_b):
    given = dict(locals())
    cx, cy, cc = lax.axis_index("x"), lax.axis_index("y"), lax.axis_index("c")
    me = 2 * cx + cy
    groups = pool_w.shape[1]
    cs = pool_w.shape[2]
    kk, conv_cols = conv_w.shape[1], conv_w.shape[2]
    core = cc.astype(jnp.int32).reshape(1)
    where = jnp.stack([me, cc]).astype(jnp.int32)

    conv_slab = jnp.zeros((kk, N_SHARD * conv_cols), F32)
    conv_slab = lax.dynamic_update_slice(conv_slab, conv_w[0] * (cc == 0).astype(F32), (0, me * conv_cols))
    wts = {"conv_w": _unpack(_all_reduce_small("gather_conv_w", _pack([conv_slab])), [conv_slab.shape])[0]}

    sems, shards, lands, token = _gather_start([_as2d(n, given[n]).astype(BF16) for n in BIG], wts["conv_w"])

    def fetch(group, after):
        idx = GATHER_GROUPS[group]
        fwd, zones = _gather_forward(f"gather_forward_{group}", idx, [lands[i] for i in idx], sems, after)
        full = dict(zip([BIG[i] for i in idx],
                        _gather_wait(f"gather_wait_{group}", idx, [shards[i] for i in idx], zones, sems, fwd)))
        out = {}
        for n, a in full.items():
            if n == "w_in":
                out["w_in_t"] = a
            elif n == "w_up":
                out["w_up3"] = a
            elif n == "pool_w":
                out[n] = a.reshape(N_SHARD, groups, cs, -1).transpose(1, 0, 2, 3).reshape(groups, N_SHARD * cs, -1)
            else:
                out[n] = a.reshape(-1, a.shape[-1])
        return out

    for n in ("a_log", "dt_bias", "gdn_norm_w", "pool_scale", "ln1_g", "ln1_b", "ln2_g", "ln2_b", "ln3_g", "ln3_b"):
        wts[n] = given[n]
    wts.update(fetch(0, token))

    def start_swap(group, grads):
        names, blocks = [], []
        for n, g in grads.items():
            if n == "pool_w":
                g = g.reshape(groups, N_SHARD, cs, -1).transpose(1, 0, 2, 3).reshape(N_SHARD, groups * cs, -1)
            elif g.ndim == 2:
                g = g.reshape(N_SHARD, -1, g.shape[-1])
            names.append({"w_in_t": "w_in", "w_up3": "w_up"}.get(n, n))
            blocks.append(g)
        zones = [((N_SHARD,) + _half_shape(b.shape[1], b.shape[2]), F32) for b in blocks]
        swap = _exchange_start(f"grad_swap_start_{group}", _swap_copies, blocks, zones, N_SHARD)
        return {"group": group, "names": names, "swap": swap, "token": swap[3]}

    def start_send(state, after):
        group, names = state["group"], state["names"]
        state["blocks"] = state["swap"][1]
        state["others"] = _exchange_wait(f"grad_swap_wait_{group}", _swap_copies, state["swap"], after)
        partials = [_chip_partial("chip_partial_" + n, gb, ob, core)
                    for n, gb, ob in zip(names, state["blocks"], state["others"])]
        zones = [((3,) + p.shape[1:], BF16) for p in partials]
        state["send"] = _exchange_start(f"grad_send_start_{group}", _partial_copies, partials, zones, 3)
        state["token"] = state["send"][3]

    grad, delta, new_m, new_v = {}, {}, {}, {}

    def start_join(state, after):
        group, names = state["group"], state["names"]
        received = _exchange_wait(f"grad_send_wait_{group}", _partial_copies, state["send"], after)
        halves = [_reduce_own("reduce_own_" + n, gb, ob, rb, where)
                  for n, gb, ob, rb in zip(names, state["blocks"], state["others"], received)]
        state["join"] = _join_start(f"grad_join_start_{group}", halves)
        return state["join"][2]

    def finish_reduce(state, after):
        group, names = state["group"], state["names"]
        for n, g in zip(names, _join_wait(f"grad_join_wait_{group}", state["join"], after)):
            shp = given[n].shape
            d2, m2, v2 = _adamw("adamw_" + n, _as2d(n, given[n]), g, _as2d(n, given["m_" + n]), _as2d(n, given["v_" + n]))
            grad[n], delta[n], new_m[n], new_v[n] = (_from2d(n, a, shp) for a in (g, d2, m2, v2))
        return d2

    step = _local_step(x[0], mem[0], loss_target[0], wts)
    pending = {}
    request = next(step)
    while True:
        try:
            kind, group, payload = request
            if kind == "weights":
                request = step.send(fetch(group, payload))
            elif kind == "grads":
                pending[group] = start_swap(group, payload)
                request = step.send(pending[group]["token"])
            else:
                start_send(pending[group], [payload])
                request = step.send(pending[group]["token"])
        except StopIteration as stop:
            loss_row, grad_x, g = stop.value
            break

    after = [pending[2]["token"], grad_x]
    for group in sorted(pending):
        after = [start_join(pending[group], after)]
    for group in sorted(pending):
        after = [finish_reduce(pending[group], after)]

    small_names = ("a_log", "dt_bias", "gdn_norm_w", "pool_scale", "ln1_g", "ln1_b", "ln2_g", "ln2_b", "ln3_g", "ln3_b")
    pieces = [g["conv_w"]] + [g[n] for n in small_names] + [loss_row[:, :1]]
    shapes = [p.shape for p in pieces]
    summed = _unpack(_all_reduce_small("all_reduce_small", _pack(pieces), after[0]), shapes)
    gsmall = dict(zip(small_names, summed[1:-1]))
    gsmall["conv_w"] = lax.dynamic_slice(summed[0], (0, me * conv_cols), (kk, conv_cols))
    loss = summed[-1][0, 0]

    sshapes = [given[n].shape for n in SMALL]
    slabs = [_pack([given[p + n] for n in SMALL]) for p in ("", "m_", "v_")]
    gslab = _pack([gsmall[n] for n in SMALL])
    outs = _adamw("adamw_small", slabs[0], gslab, slabs[1], slabs[2])
    for dst, slab in zip((delta, new_m, new_v), outs):
        dst.update(zip(SMALL, _unpack(slab, sshapes)))
    for n in SMALL:
        grad[n] = gsmall[n].reshape(given[n].shape)

    return (loss, grad_x[None], *[grad[n] for n in ORDER], *[delta[n] for n in ORDER],
            *[new_m[n] for n in ORDER], *[new_v[n] for n in ORDER])
```

```python
import functools
import math

import jax
import jax.numpy as jnp
from jax import lax
from jax.experimental import pallas as pl
from jax.experimental.pallas import tpu as pltpu

F32 = jnp.float32
BF16 = jnp.bfloat16
MESH = pl.DeviceIdType.MESH

HEAD_DIM = 128
CHUNK = 64
POOL_WINDOWS = (2, 4, 8, 16)
XATTN_HEADS = 4
ALPHA = 2.0 ** 0.25
LN_EPS = 1e-5
NORM_EPS = 1e-6
ADAM_LR, ADAM_B1, ADAM_B2, ADAM_EPS, ADAM_WD, ADAM_STEP = 0.001, 0.9, 0.999, 1e-08, 0.01, 10
N_SHARD = 4
VMEM_LIMIT = 56 * 1024 * 1024
K_STEPS = (2048, 1024, 512, 256, 128)


def _params(*sem):
    return pltpu.CompilerParams(dimension_semantics=sem, vmem_limit_bytes=VMEM_LIMIT)


def _bdot(a, b, ta=False, tb=False):
    dims = (((0 if ta else 1,), (1 if tb else 0,)), ((), ()))
    return lax.dot_general(a.astype(BF16), b.astype(BF16), dims, preferred_element_type=F32)


def _sigmoid(x):
    return 1.0 / (1.0 + jnp.exp(-x))


def _matmul(name, a, b, *, ta=False, tb=False, tm, tn, tk, extra=(), outs, epilogue, b_blocks=None,
            sequential=False, n_used=None, k_used=None, n_outer=False):
    m, k_dim = (a.shape[1], a.shape[0]) if ta else a.shape
    if b_blocks and tb:
        n = b.shape[1]
        k_dim = b.shape[0] * b.shape[2]
        per = b.shape[2] // tk
        b_spec = pl.BlockSpec((None, tn, tk), lambda i, j, k: (k // per, j, k % per))
    elif b_blocks:
        n = b.shape[0] * b.shape[2]
        per = b.shape[2] // tn
        b_spec = pl.BlockSpec((None, tk, tn), lambda i, j, k: (j // per, k, j % per))
    elif tb:
        n = b.shape[0]
        b_spec = pl.BlockSpec((tn, tk), lambda i, j, k: (j, k))
    else:
        n = b.shape[1]
        b_spec = pl.BlockSpec((tk, tn), lambda i, j, k: (k, j))
    n, k_dim = n_used or n, k_used or k_dim
    assert m % tm == 0 and n % tn == 0 and k_dim % tk == 0, (name, m, n, k_dim, tm, tn, tk)
    nk = k_dim // tk
    a_spec = pl.BlockSpec((tk, tm), lambda i, j, k: (k, i)) if ta else pl.BlockSpec((tm, tk), lambda i, j, k: (i, k))
    n_extra, n_out = len(extra), len(outs)

    def wrap(index_map):
        return lambda i, j, k: index_map(i, j)

    def spec(block, index_map):
        if n_outer:
            return pl.BlockSpec(block, lambda j, i, k: index_map(i, j, k))
        return pl.BlockSpec(block, index_map)

    row_axis = 1 if n_outer else 0

    def body_one_step(*refs):
        ex = refs[2:2 + n_extra]
        out = refs[2 + n_extra:2 + n_extra + n_out]
        epilogue(_bdot(refs[0][...], refs[1][...], ta, tb), ex, out, pl.program_id(row_axis))

    def body(*refs):
        a_ref, b_ref = refs[0], refs[1]
        ex = refs[2:2 + n_extra]
        out = refs[2 + n_extra:2 + n_extra + n_out]
        acc = refs[-1]
        i, k = pl.program_id(row_axis), pl.program_id(2)
        part = _bdot(a_ref[...], b_ref[...], ta, tb)

        @pl.when(k == 0)
        def _():
            acc[...] = part

        @pl.when(jnp.logical_and(k > 0, k < nk - 1))
        def _():
            acc[...] += part

        @pl.when(k == nk - 1)
        def _():
            epilogue(acc[...] + part, ex, out, i)

    sem = ("arbitrary",) * 3 if sequential else ("parallel", "parallel", "arbitrary")
    res = pl.pallas_call(
        body_one_step if nk == 1 else body, name=name,
        grid=(n // tn, m // tm, nk) if n_outer else (m // tm, n // tn, nk),
        in_specs=[spec(a_spec.block_shape, a_spec.index_map), spec(b_spec.block_shape, b_spec.index_map)]
        + [spec(bs, wrap(im)) for _, bs, im in extra],
        out_specs=[spec(bs, wrap(im)) for _, bs, im in outs],
        out_shape=[s for s, _, _ in outs],
        scratch_shapes=[] if nk == 1 else [pltpu.VMEM((tm, tn), F32)],
        compiler_params=_params(*sem),
    )(a, b, *[x for x, _, _ in extra])
    return res


def _tile(i, j):
    return (i, j)


def _plain(name, a, b, *, ta=False, tb=False, tm, tn, tk, out_dtype, b_blocks=None, out3=None, n_used=None,
           n_outer=False):
    m = a.shape[1] if ta else a.shape[0]
    n = n_used or ((b.shape[0] * b.shape[2]) if b_blocks else (b.shape[0] if tb else b.shape[1]))

    def epi(acc, ex, out, i):
        out[0][...] = acc.astype(out_dtype)

    if out3:
        per = (n // out3) // tn
        spec = (jax.ShapeDtypeStruct((out3, m, n // out3), out_dtype), (None, tm, tn),
                lambda i, j: (j // per, i, j % per))
    else:
        spec = (jax.ShapeDtypeStruct((m, n), out_dtype), (tm, tn), _tile)
    return _matmul(name, a, b, ta=ta, tb=tb, tm=tm, tn=tn, tk=tk, outs=[spec], epilogue=epi,
                   b_blocks=b_blocks, n_used=n_used, n_outer=n_outer)[0]


def _ln_forward(name, a, b, res, gamma, beta, *, tm, tk, want_h=True):
    m, n = res.shape

    def epi(acc, ex, out, i):
        u = ALPHA * ex[0][...] + acc
        mu = jnp.mean(u, axis=-1, keepdims=True)
        xc = u - mu
        var = jnp.mean(xc * xc, axis=-1, keepdims=True)
        rstd = lax.rsqrt(var + LN_EPS)
        xhat = xc * rstd
        out[-2][...] = xhat
        out[-1][...] = rstd
        if want_h:
            h = xhat * ex[1][...] + ex[2][...]
            out[0][...] = h
            out[1][...] = h.astype(BF16)

    row = lambda i, j: (i, 0)
    vec = lambda i, j: (0, 0)
    outs = [(jax.ShapeDtypeStruct((m, n), F32), (tm, n), row), (jax.ShapeDtypeStruct((m, n), BF16), (tm, n), row),
            (jax.ShapeDtypeStruct((m, n), F32), (tm, n), row), (jax.ShapeDtypeStruct((m, 1), F32), (tm, 1), row)]
    return _matmul(
        name, a, b, tm=tm, tn=n, tk=tk,
        extra=[(res, (tm, n), row), (gamma, (1, n), vec), (beta, (1, n), vec)],
        outs=outs if want_h else outs[2:], epilogue=epi)


def _ln_backward_math(dy, xhat, rstd, gamma):
    dxhat = dy * gamma
    m1 = jnp.mean(dxhat, axis=-1, keepdims=True)
    m2 = jnp.mean(dxhat * xhat, axis=-1, keepdims=True)
    du = rstd * (dxhat - m1 - xhat * m2)
    return du, jnp.sum(dy * xhat, axis=0, keepdims=True), jnp.sum(dy, axis=0, keepdims=True)


def _ln_backward(name, a, b, dres, xhat, rstd, gamma, *, tm, tk, b_blocks=None, tb=True):
    m, n = dres.shape

    def epi(acc, ex, out, i):
        dy = acc + ALPHA * ex[0][...]
        du, dg, db = _ln_backward_math(dy, ex[1][...], ex[2][...], ex[3][...])
        out[0][...] = du
        out[1][...] = du.astype(BF16)
        first = i == 0

        @pl.when(first)
        def _():
            out[2][...] = dg
            out[3][...] = db

        @pl.when(jnp.logical_not(first))
        def _():
            out[2][...] += dg
            out[3][...] += db

    row = lambda i, j: (i, 0)
    vec = lambda i, j: (0, 0)
    return _matmul(
        name, a, b, tb=tb, tm=tm, tn=n, tk=tk, b_blocks=b_blocks, sequential=True,
        extra=[(dres, (tm, n), row), (xhat, (tm, n), row), (rstd, (tm, 1), row), (gamma, (1, n), vec)],
        outs=[(jax.ShapeDtypeStruct((m, n), F32), (tm, n), row),
              (jax.ShapeDtypeStruct((m, n), BF16), (tm, n), row),
              (jax.ShapeDtypeStruct((1, n), F32), (1, n), vec),
              (jax.ShapeDtypeStruct((1, n), F32), (1, n), vec)],
        epilogue=epi)


def _shift_down(x, k):
    row = lax.broadcasted_iota(jnp.int32, x.shape, 0)
    return jnp.where(row >= k, pltpu.roll(x, k, axis=0), 0.0)


def _shift_up(x, k):
    t = x.shape[0]
    row = lax.broadcasted_iota(jnp.int32, x.shape, 0)
    return jnp.where(row < t - k, pltpu.roll(x, t - k, axis=0), 0.0)


def _conv_silu_norm(x, w, normalise):
    kk = w.shape[0]
    c = x * w[kk - 1:kk, :]
    for j in range(kk - 1):
        c = c + _shift_down(x, kk - 1 - j) * w[j:j + 1, :]
    sg = _sigmoid(c)
    s = c * sg
    r = lax.rsqrt(jnp.sum(s * s, axis=-1, keepdims=True) + NORM_EPS)
    y = jnp.where(normalise, s * r, s)
    return c, sg, s, r, y


def _gdn_pre(proj, conv_w, heads):
    t = proj.shape[0]
    kk = conv_w.shape[0]

    def body(x_ref, w_ref, o_ref):
        normalise = pl.program_id(0) < 2
        o_ref[...] = _conv_silu_norm(x_ref[...], w_ref[...], normalise)[4]

    col = lambda s, h: (0, s * heads + h)
    return pl.pallas_call(
        body, name="gdn_pre", grid=(3, heads),
        in_specs=[pl.BlockSpec((t, HEAD_DIM), col), pl.BlockSpec((kk, HEAD_DIM), col)],
        out_specs=pl.BlockSpec((t, HEAD_DIM), col),
        out_shape=jax.ShapeDtypeStruct((t, 3 * heads * HEAD_DIM), F32),
        compiler_params=_params("parallel", "parallel"),
    )(proj, conv_w)


def _gdn_pre_backward(proj, conv_w, dqkv, heads):
    t = proj.shape[0]
    kk = conv_w.shape[0]

    def body(x_ref, w_ref, dy_ref, dx_ref, dw_ref):
        normalise = pl.program_id(0) < 2
        x = x_ref[...]
        w = w_ref[...]
        dy = dy_ref[...]
        c, sg, s, r, y = _conv_silu_norm(x, w, normalise)
        ds_norm = r * (dy - y * jnp.sum(dy * y, axis=-1, keepdims=True))
        ds = jnp.where(normalise, ds_norm, dy)
        dc = ds * (sg * (1.0 + c * (1.0 - sg)))
        dx = dc * w[kk - 1:kk, :]
        rows = [None] * kk
        rows[kk - 1] = jnp.sum(dc * x, axis=0, keepdims=True)
        for j in range(kk - 1):
            lag = kk - 1 - j
            dx = dx + _shift_up(dc, lag) * w[j:j + 1, :]
            rows[j] = jnp.sum(dc * _shift_down(x, lag), axis=0, keepdims=True)
        dx_ref[...] = dx.astype(BF16)
        dw_ref[...] = jnp.concatenate(rows, axis=0)

    col = lambda s, h: (0, s * heads + h)
    return pl.pallas_call(
        body, name="gdn_pre_bwd", grid=(3, heads),
        in_specs=[pl.BlockSpec((t, HEAD_DIM), col), pl.BlockSpec((kk, HEAD_DIM), col),
                  pl.BlockSpec((t, HEAD_DIM), col)],
        out_specs=[pl.BlockSpec((t, HEAD_DIM), col), pl.BlockSpec((kk, HEAD_DIM), col)],
        out_shape=[jax.ShapeDtypeStruct((t, 3 * heads * HEAD_DIM), BF16),
                   jax.ShapeDtypeStruct((kk, 3 * heads * HEAD_DIM), F32)],
        compiler_params=_params("parallel", "parallel"),
    )(proj, conv_w, dqkv)


def _gate_vectors(a_log, dt_bias, heads):
    pad = lambda v: jnp.pad(v.astype(F32), ((0, 0), (heads, HEAD_DIM - 2 * heads)))
    return pad(jnp.exp(a_log.astype(F32))), pad(dt_bias)


def _softplus(x):
    return jnp.maximum(x, 0.0) + jnp.log(1.0 + jnp.exp(-jnp.abs(x)))


def _gates_epilogue(heads):
    def epi(acc, ex, out, i):
        lane = lax.broadcasted_iota(jnp.int32, acc.shape, 1)
        beta = _sigmoid(acc)
        g = -ex[0][...] * _softplus(acc + ex[1][...])
        out[0][...] = acc
        out[1][...] = jnp.where(lane < heads, beta, jnp.where(lane < 2 * heads, g, 0.0))
    return epi


def _gates_backward(ba, bg, dbg, ea, dtb, heads):
    t = ba.shape[0]

    def body(ba_ref, bg_ref, d_ref, ea_ref, dt_ref, dba_ref, dal_ref, ddt_ref):
        lane = lax.broadcasted_iota(jnp.int32, (t, HEAD_DIM), 1)
        bgv = bg_ref[...]
        d = d_ref[...]
        db = d * bgv * (1.0 - bgv)
        da = -d * ea_ref[...] * _sigmoid(ba_ref[...] + dt_ref[...])
        is_g = jnp.logical_and(lane >= heads, lane < 2 * heads)
        dba = jnp.where(lane < heads, db, jnp.where(is_g, da, 0.0))
        dba_ref[...] = dba.astype(BF16)
        dal_ref[...] = jnp.sum(jnp.where(is_g, d * bgv, 0.0), axis=0, keepdims=True)
        ddt_ref[...] = jnp.sum(jnp.where(is_g, da, 0.0), axis=0, keepdims=True)

    full = pl.BlockSpec((t, HEAD_DIM), lambda: (0, 0))
    vec = pl.BlockSpec((1, HEAD_DIM), lambda: (0, 0))
    return pl.pallas_call(
        body, name="gates_bwd", grid=(),
        in_specs=[full, full, full, vec, vec], out_specs=[full, vec, vec],
        out_shape=[jax.ShapeDtypeStruct((t, HEAD_DIM), BF16), jax.ShapeDtypeStruct((1, HEAD_DIM), F32),
                   jax.ShapeDtypeStruct((1, HEAD_DIM), F32)],
        compiler_params=pltpu.CompilerParams(vmem_limit_bytes=VMEM_LIMIT),
    )(ba, bg, dbg, ea, dtb)


class _Chunk:
    pass


def _split2(x):
    hi = x.astype(BF16)
    return hi, (x - hi.astype(F32)).astype(BF16)


def _split3(x):
    hi = x.astype(BF16)
    rest = x - hi.astype(F32)
    mid = rest.astype(BF16)
    return hi, mid, (rest - mid.astype(F32)).astype(BF16)


def _dot_mask(mask, x, ta=False):
    hi, mid, lo = _split3(x)
    return _bdot(mask, hi, ta=ta) + (_bdot(mask, mid, ta=ta) + _bdot(mask, lo, ta=ta))


def _transpose_by_identity(x):
    r = x.shape[0]
    eye = (lax.broadcasted_iota(jnp.int32, (r, r), 0) == lax.broadcasted_iota(jnp.int32, (r, r), 1)).astype(BF16)
    hi, mid, lo = _split3(x)
    return _bdot(hi, eye, ta=True) + (_bdot(mid, eye, ta=True) + _bdot(lo, eye, ta=True))


def _dot22(a, b, ta=False, tb=False):
    ah, al = _split2(a)
    bh, bl = _split2(b)
    return _bdot(ah, bh, ta, tb) + (_bdot(ah, bl, ta, tb) + _bdot(al, bh, ta, tb))


def _chunk_gates(bg, heads):
    n = CHUNK
    row = lax.broadcasted_iota(jnp.int32, (n, n), 0)
    col = lax.broadcasted_iota(jnp.int32, (n, n), 1)
    lane = lax.broadcasted_iota(jnp.int32, bg.shape, 1)
    graw = jnp.where(jnp.logical_and(lane >= heads, lane < 2 * heads), bg, 0.0)
    gc = _dot_mask((row >= col).astype(BF16), graw)
    return gc, _transpose_by_identity(gc)


def _in_lockstep(generators):
    results = [None] * len(generators)
    live = list(enumerate(generators))
    while live:
        still = []
        for i, gen in live:
            try:
                next(gen)
                still.append((i, gen))
            except StopIteration as stop:
                results[i] = stop.value
        live = still
    return results


def _chunk_local(q, k, v, beta, gc, grow):
    c = _Chunk()
    n = CHUNK
    row = lax.broadcasted_iota(jnp.int32, (n, n), 0)
    col = lax.broadcasted_iota(jnp.int32, (n, n), 1)
    c.tri = row >= col
    c.strict = row > col
    eye = row == col
    c.gcb = jnp.broadcast_to(gc, (n, HEAD_DIM))
    c.decay = jnp.where(c.tri, jnp.exp(jnp.where(c.tri, gc - grow, 0.0)), 0.0)
    c.eg = jnp.exp(c.gcb)
    glast = c.gcb[n - 1:n, :]
    c.egl = jnp.exp(glast)
    c.ekl = jnp.exp(glast - c.gcb)
    c.beta = beta
    c.q = q * (HEAD_DIM ** -0.5)
    c.k = k
    c.v = v
    c.kb = k * beta
    c.vb = v * beta
    c.kg = c.kb * c.eg
    both = _bdot(jnp.concatenate([c.kb, c.q], axis=0), k, tb=True)
    yield
    c.L = jnp.where(c.strict, both[:n] * c.decay, 0.0)
    c.A = jnp.where(c.tri, both[n:] * c.decay, 0.0)
    x = -c.L
    tinv = eye.astype(F32) + x
    p = _dot22(x, x)
    yield
    for _ in range(int(math.log2(n)) - 2):
        both = _dot22(jnp.concatenate([p, tinv], axis=0), p)
        yield
        p, tinv = both[:n], tinv + both[n:]
    c.T = tinv + _dot22(tinv, p)
    yield
    tinv = c.T
    uw = _dot22(tinv, jnp.concatenate([c.vb, c.kg], axis=1))
    yield
    c.u, c.w = uw[:, :HEAD_DIM], uw[:, HEAD_DIM:]
    c.qg = c.q * c.eg
    c.kdec = k * c.ekl
    return c


def _gdn_core(qkv, bg, heads):
    t = qkv.shape[0]
    nchunk = t // CHUNK

    gw = heads * HEAD_DIM

    def body(qkv_ref, bg_ref, o_ref, s_ref, state):
        @pl.when(pl.program_id(0) == 0)
        def _():
            state[...] = jnp.zeros_like(state)

        bg_v = bg_ref[...]
        gc_all, gc_rows = _chunk_gates(bg_v, heads)
        def one_head(h):
            col = lambda s: pl.ds(s * gw + h * HEAD_DIM, HEAD_DIM)
            c = yield from _chunk_local(qkv_ref[:, col(0)], qkv_ref[:, col(1)], qkv_ref[:, col(2)], bg_v[:, h:h + 1],
                                        gc_all[:, heads + h:heads + h + 1], gc_rows[heads + h:heads + h + 1, :])
            s0 = state[h]
            v_new = c.u - _bdot(c.w, s0)
            yield
            o = _bdot(c.qg, s0) + _bdot(c.A, v_new)
            return s0, o, s0 * c.egl + _bdot(c.kdec, v_new, ta=True)

        results = _in_lockstep([one_head(h) for h in range(heads)])
        for h, (s0, o, s1) in enumerate(results):
            s_ref[h, 0] = s0
            o_ref[:, pl.ds(h * HEAD_DIM, HEAD_DIM)] = o
            state[h] = s1

    return pl.pallas_call(
        body, name="gdn_core", grid=(nchunk,),
        in_specs=[pl.BlockSpec((CHUNK, 3 * gw), lambda n: (n, 0)), pl.BlockSpec((CHUNK, HEAD_DIM), lambda n: (n, 0))],
        out_specs=[pl.BlockSpec((CHUNK, gw), lambda n: (n, 0)),
                   pl.BlockSpec((heads, 1, HEAD_DIM, HEAD_DIM), lambda n: (0, n, 0, 0))],
        out_shape=[jax.ShapeDtypeStruct((t, gw), F32),
                   jax.ShapeDtypeStruct((heads, nchunk, HEAD_DIM, HEAD_DIM), F32)],
        scratch_shapes=[pltpu.VMEM((heads, HEAD_DIM, HEAD_DIM), F32)],
        compiler_params=_params("arbitrary"),
    )(qkv, bg)


def _gdn_core_backward(qkv, bg, states, do, heads):
    t = qkv.shape[0]
    nchunk = t // CHUNK
    n = CHUNK

    def one_head(chunk_local, s0, d_out, ds1):
        c = yield from chunk_local
        v_new = c.u - _bdot(c.w, s0)
        dqg = _bdot(d_out, s0, tb=True)
        ds0 = _bdot(c.qg, d_out, ta=True) + ds1 * c.egl
        dv_new = _bdot(c.A, d_out, ta=True) + _bdot(c.kdec, ds1)
        yield
        dA = jnp.where(c.tri, _bdot(d_out, v_new, tb=True), 0.0)
        dkdec = _bdot(v_new, ds1, tb=True)
        dgl = jnp.sum(jnp.sum(ds1 * s0, axis=1, keepdims=True), axis=0, keepdims=True) * c.egl
        dw = -_bdot(dv_new, s0, tb=True)
        ds0 = ds0 - _bdot(c.w, dv_new, ta=True)
        yield
        both = _dot22(c.T, jnp.concatenate([dv_new, dw], axis=1), ta=True)
        yield
        dvb, dkg = both[:, :HEAD_DIM], both[:, HEAD_DIM:]
        dL = jnp.where(c.strict, -(_bdot(dvb, c.u, tb=True) + _bdot(dkg, c.w, tb=True)), 0.0)
        yield
        dm1 = dL * c.decay
        dkb = _bdot(dm1, c.k) + dkg * c.eg
        dk = _bdot(dm1, c.kb, ta=True)
        dm2 = dA * c.decay
        dq = _bdot(dm2, c.k) + dqg * c.eg
        dk = dk + _bdot(dm2, c.q, ta=True) + dkdec * c.ekl + dkb * c.beta
        pm = dL * c.L + dA * c.A
        ones = jnp.ones((n, HEAD_DIM), BF16)
        pm_hi, pm_lo = _split2(pm)
        colsum = _bdot(pm_hi, ones, ta=True) + _bdot(pm_lo, ones, ta=True)
        tk_ = jnp.sum(dkdec * c.kdec, axis=1, keepdims=True)
        dgc = (jnp.sum(pm, axis=1, keepdims=True) - colsum
               + jnp.sum(dqg * c.qg, axis=1, keepdims=True)
               - tk_
               + jnp.sum(dkg * c.kg, axis=1, keepdims=True))
        dgl = dgl + jnp.sum(tk_, axis=0, keepdims=True)
        rowi = lax.broadcasted_iota(jnp.int32, (n, HEAD_DIM), 0)
        dgc = dgc + jnp.where(rowi == n - 1, dgl, 0.0)
        dbeta = jnp.sum(dkb * c.k, axis=1, keepdims=True) + jnp.sum(dvb * c.v, axis=1, keepdims=True)
        return dq * (HEAD_DIM ** -0.5), dk, dvb * c.beta, dbeta, dgc, ds0

    gw = heads * HEAD_DIM

    def body(qkv_ref, bg_ref, s_ref, do_ref, dqkv_ref, dbg_ref, dstate):
        @pl.when(pl.program_id(0) == 0)
        def _():
            dstate[...] = jnp.zeros_like(dstate)

        bg_v = bg_ref[...]
        gc_all, gc_rows = _chunk_gates(bg_v, heads)
        lane = lax.broadcasted_iota(jnp.int32, (n, HEAD_DIM), 1)
        dgates = jnp.zeros((n, HEAD_DIM), F32)
        chains = []
        for h in range(heads):
            col = lambda s: pl.ds(s * gw + h * HEAD_DIM, HEAD_DIM)
            c = _chunk_local(qkv_ref[:, col(0)], qkv_ref[:, col(1)], qkv_ref[:, col(2)], bg_v[:, h:h + 1],
                             gc_all[:, heads + h:heads + h + 1], gc_rows[heads + h:heads + h + 1, :])
            chains.append(one_head(c, s_ref[h, 0], do_ref[:, pl.ds(h * HEAD_DIM, HEAD_DIM)], dstate[h]))
        results = _in_lockstep(chains)
        for h, (dq, dk, dv, dbeta, dgc, ds0) in enumerate(results):
            dgates = jnp.where(lane == h, dbeta, jnp.where(lane == heads + h, dgc, dgates))
        for h, (dq, dk, dv, dbeta, dgc, ds0) in enumerate(results):
            dqkv_ref[:, pl.ds(h * HEAD_DIM, HEAD_DIM)] = dq
            dqkv_ref[:, pl.ds(gw + h * HEAD_DIM, HEAD_DIM)] = dk
            dqkv_ref[:, pl.ds(2 * gw + h * HEAD_DIM, HEAD_DIM)] = dv
            dstate[h] = ds0
        row = lax.broadcasted_iota(jnp.int32, (n, n), 0)
        colm = lax.broadcasted_iota(jnp.int32, (n, n), 1)
        draw = _dot_mask((row >= colm).astype(BF16), dgates, ta=True)
        dbg_ref[...] = jnp.where(lane < heads, dgates, draw)

    last = nchunk - 1
    return pl.pallas_call(
        body, name="gdn_core_bwd", grid=(nchunk,),
        in_specs=[pl.BlockSpec((CHUNK, 3 * gw), lambda i: (last - i, 0)),
                  pl.BlockSpec((CHUNK, HEAD_DIM), lambda i: (last - i, 0)),
                  pl.BlockSpec((heads, 1, HEAD_DIM, HEAD_DIM), lambda i: (0, last - i, 0, 0)),
                  pl.BlockSpec((CHUNK, gw), lambda i: (last - i, 0))],
        out_specs=[pl.BlockSpec((CHUNK, 3 * gw), lambda i: (last - i, 0)),
                   pl.BlockSpec((CHUNK, HEAD_DIM), lambda i: (last - i, 0))],
        out_shape=[jax.ShapeDtypeStruct((t, 3 * gw), F32), jax.ShapeDtypeStruct((t, HEAD_DIM), F32)],
        scratch_shapes=[pltpu.VMEM((heads, HEAD_DIM, HEAD_DIM), F32)],
        compiler_params=_params("arbitrary"),
    )(qkv, bg, states, do)


def _gdn_post(o, proj, z_col0, norm_w, heads, tt):
    t = o.shape[0]
    zb = z_col0 // HEAD_DIM

    def body(o_ref, z_ref, w_ref, out_ref):
        ov = o_ref[...]
        z = z_ref[...]
        rms = lax.rsqrt(jnp.mean(ov * ov, axis=-1, keepdims=True) + NORM_EPS)
        out_ref[...] = (ov * rms * w_ref[...] * (z * _sigmoid(z))).astype(BF16)

    return pl.pallas_call(
        body, name="gdn_post", grid=(t // tt, heads),
        in_specs=[pl.BlockSpec((tt, HEAD_DIM), lambda i, h: (i, h)),
                  pl.BlockSpec((tt, HEAD_DIM), lambda i, h: (i, zb + h)),
                  pl.BlockSpec((1, HEAD_DIM), lambda i, h: (0, 0))],
        out_specs=pl.BlockSpec((tt, HEAD_DIM), lambda i, h: (i, h)),
        out_shape=jax.ShapeDtypeStruct((t, heads * HEAD_DIM), BF16),
        compiler_params=_params("parallel", "parallel"),
    )(o, proj, norm_w)


def _gdn_post_backward(dcat, o, proj, z_col0, norm_w, heads, tt):
    t = o.shape[0]
    zb = z_col0 // HEAD_DIM

    def body(d_ref, o_ref, z_ref, w_ref, do_ref, dz_ref, dw_ref):
        d = d_ref[...]
        ov = o_ref[...]
        z = z_ref[...]
        w = w_ref[...]
        rms = lax.rsqrt(jnp.mean(ov * ov, axis=-1, keepdims=True) + NORM_EPS)
        ohat = ov * rms
        sg = _sigmoid(z)
        gate = z * sg
        dz_ref[...] = (d * ohat * w * (sg * (1.0 + z * (1.0 - sg)))).astype(BF16)
        don = d * gate
        dohat = don * w
        do_ref[...] = rms * (dohat - ohat * jnp.mean(dohat * ohat, axis=-1, keepdims=True))
        dw = jnp.sum(don * ohat, axis=0, keepdims=True)
        first = jnp.logical_and(pl.program_id(0) == 0, pl.program_id(1) == 0)

        @pl.when(first)
        def _():
            dw_ref[...] = dw

        @pl.when(jnp.logical_not(first))
        def _():
            dw_ref[...] += dw

    blk = pl.BlockSpec((tt, HEAD_DIM), lambda i, h: (i, h))
    return pl.pallas_call(
        body, name="gdn_post_bwd", grid=(t // tt, heads),
        in_specs=[blk, blk, pl.BlockSpec((tt, HEAD_DIM), lambda i, h: (i, zb + h)),
                  pl.BlockSpec((1, HEAD_DIM), lambda i, h: (0, 0))],
        out_specs=[blk, blk, pl.BlockSpec((1, HEAD_DIM), lambda i, h: (0, 0))],
        out_shape=[jax.ShapeDtypeStruct((t, heads * HEAD_DIM), F32),
                   jax.ShapeDtypeStruct((t, heads * HEAD_DIM), BF16),
                   jax.ShapeDtypeStruct((1, HEAD_DIM), F32)],
        compiler_params=_params("arbitrary", "arbitrary"),
    )(dcat, o, proj, norm_w)


def _pool_select(levels, group):
    out = levels[-1]
    for gi in range(len(levels) - 2, -1, -1):
        out = jnp.where(group == gi, levels[gi], out)
    return out


def _pool_counts(t, width, group):
    pos = lax.broadcasted_iota(jnp.int32, (t, width), 0)
    win = jnp.left_shift(2, group)
    return jnp.minimum(pos + 1, win).astype(F32)


def _pooled(p, group):
    levels, s, step = [], p, 1
    for _ in POOL_WINDOWS:
        s = s + _shift_down(s, step)
        levels.append(s)
        step *= 2
    cnt = _pool_counts(p.shape[0], p.shape[1], group)
    return _pool_select(levels, group) / cnt - p, cnt


def _pool_forward(proj, p_col0, pool_w, pool_scale):
    t = proj.shape[0]
    groups, cg, _ = pool_w.shape
    pb = p_col0 // cg

    def body(p_ref, w_ref, s_ref, o_ref):
        pooled, _ = _pooled(p_ref[...], pl.program_id(0))
        o_ref[...] = (_bdot(pooled, w_ref[0]) * s_ref[...]).astype(BF16)

    return pl.pallas_call(
        body, name="pool_fwd", grid=(groups,),
        in_specs=[pl.BlockSpec((t, cg), lambda g: (0, pb + g)), pl.BlockSpec((1, cg, cg), lambda g: (g, 0, 0)),
                  pl.BlockSpec((1, cg), lambda g: (0, g))],
        out_specs=pl.BlockSpec((t, cg), lambda g: (0, g)),
        out_shape=jax.ShapeDtypeStruct((t, groups * cg), BF16),
        compiler_params=_params("parallel"),
    )(proj, pool_w, pool_scale)


def _pool_backward(dcat, d_col0, proj, p_col0, pool_w, pool_scale):
    t = proj.shape[0]
    groups, cg, _ = pool_w.shape
    pb = p_col0 // cg
    db = d_col0 // cg

    def body(d_ref, p_ref, w_ref, s_ref, dp_ref, dw_ref, ds_ref):
        group = pl.program_id(0)
        pooled, cnt = _pooled(p_ref[...], group)
        w = w_ref[0]
        d = d_ref[...]
        mixed = _bdot(pooled, w)
        ds_ref[...] = jnp.sum(d * mixed, axis=0, keepdims=True)
        dmixed = d * s_ref[...]
        dw_ref[0] = _bdot(pooled, dmixed, ta=True)
        dpooled = _bdot(dmixed, w, tb=True)
        levels, s, step = [], dpooled / cnt, 1
        for _ in POOL_WINDOWS:
            s = s + _shift_up(s, step)
            levels.append(s)
            step *= 2
        dp_ref[...] = (_pool_select(levels, group) - dpooled).astype(BF16)

    return pl.pallas_call(
        body, name="pool_bwd", grid=(groups,),
        in_specs=[pl.BlockSpec((t, cg), lambda g: (0, db + g)), pl.BlockSpec((t, cg), lambda g: (0, pb + g)),
                  pl.BlockSpec((1, cg, cg), lambda g: (g, 0, 0)), pl.BlockSpec((1, cg), lambda g: (0, g))],
        out_specs=[pl.BlockSpec((t, cg), lambda g: (0, g)), pl.BlockSpec((1, cg, cg), lambda g: (g, 0, 0)),
                   pl.BlockSpec((1, cg), lambda g: (0, g))],
        out_shape=[jax.ShapeDtypeStruct((t, groups * cg), BF16), jax.ShapeDtypeStruct((groups, cg, cg), F32),
                   jax.ShapeDtypeStruct((1, groups * cg), F32)],
        compiler_params=_params("parallel"),
    )(dcat, proj, pool_w, pool_scale)


def _attention(q, k, v, tq):
    t, d = q.shape
    m = k.shape[0]
    dh = d // XATTN_HEADS
    scale = dh ** -0.5

    def body(q_ref, k_ref, v_ref, o_ref):
        s = _bdot(q_ref[...], k_ref[...], tb=True) * scale
        s = s - jnp.max(s, axis=-1, keepdims=True)
        e = jnp.exp(s)
        p = e / jnp.sum(e, axis=-1, keepdims=True)
        o_ref[...] = _bdot(p, v_ref[...]).astype(BF16)

    return pl.pallas_call(
        body, name="xattn_fwd", grid=(XATTN_HEADS, t // tq),
        in_specs=[pl.BlockSpec((tq, dh), lambda h, i: (i, h)), pl.BlockSpec((m, dh), lambda h, i: (0, h)),
                  pl.BlockSpec((m, dh), lambda h, i: (0, h))],
        out_specs=pl.BlockSpec((tq, dh), lambda h, i: (i, h)),
        out_shape=jax.ShapeDtypeStruct((t, d), BF16),
        compiler_params=_params("parallel", "parallel"),
    )(q, k, v)


def _attention_backward(q, k, v, do, tq):
    t, d = q.shape
    m = k.shape[0]
    dh = d // XATTN_HEADS
    scale = dh ** -0.5

    def body(q_ref, k_ref, v_ref, do_ref, dq_ref, dk_ref, dv_ref, dk_acc, dv_acc):
        i = pl.program_id(1)
        qv, kv, vv, dov = q_ref[...], k_ref[...], v_ref[...], do_ref[...]
        s = _bdot(qv, kv, tb=True) * scale
        s = s - jnp.max(s, axis=-1, keepdims=True)
        e = jnp.exp(s)
        p = e / jnp.sum(e, axis=-1, keepdims=True)
        dp = _bdot(dov, vv, tb=True)
        ds = p * (dp - jnp.sum(dp * p, axis=-1, keepdims=True)) * scale
        dq_ref[...] = _bdot(ds, kv).astype(BF16)
        dv_part = _bdot(p, dov, ta=True)
        dk_part = _bdot(ds, qv, ta=True)

        @pl.when(i == 0)
        def _():
            dk_acc[...] = dk_part
            dv_acc[...] = dv_part

        @pl.when(i > 0)
        def _():
            dk_acc[...] += dk_part
            dv_acc[...] += dv_part

        @pl.when(i == pl.num_programs(1) - 1)
        def _():
            dk_ref[...] = dk_acc[...].astype(BF16)
            dv_ref[...] = dv_acc[...].astype(BF16)

    qblk = pl.BlockSpec((tq, dh), lambda h, i: (i, h))
    kblk = pl.BlockSpec((m, dh), lambda h, i: (0, h))
    return pl.pallas_call(
        body, name="xattn_bwd", grid=(XATTN_HEADS, t // tq),
        in_specs=[qblk, kblk, kblk, qblk],
        out_specs=[qblk, kblk, kblk],
        out_shape=[jax.ShapeDtypeStruct((t, d), BF16), jax.ShapeDtypeStruct((m, d), BF16),
                   jax.ShapeDtypeStruct((m, d), BF16)],
        scratch_shapes=[pltpu.VMEM((m, dh), F32), pltpu.VMEM((m, dh), F32)],
        compiler_params=_params("parallel", "arbitrary"),
    )(q, k, v, do)


def _loss_and_ln_backward(xhat, rstd, gamma, beta, target, tm):
    t, d = xhat.shape

    def body(x_ref, r_ref, g_ref, b_ref, t_ref, du_ref, dub_ref, dg_ref, db_ref, loss_ref):
        xh = x_ref[...]
        g = g_ref[...]
        diff = xh * g + b_ref[...] - t_ref[...]
        part = jnp.sum(jnp.sum(diff * diff, axis=1, keepdims=True), axis=0, keepdims=True) * (0.5 / d)
        dy = diff * (1.0 / d)
        du, dg, db = _ln_backward_math(dy, xh, r_ref[...], g)
        du_ref[...] = du
        dub_ref[...] = du.astype(BF16)
        lossrow = jnp.broadcast_to(part, (1, HEAD_DIM))
        first = pl.program_id(0) == 0

        @pl.when(first)
        def _():
            dg_ref[...] = dg
            db_ref[...] = db
            loss_ref[...] = lossrow

        @pl.when(jnp.logical_not(first))
        def _():
            dg_ref[...] += dg
            db_ref[...] += db
            loss_ref[...] += lossrow

    row = pl.BlockSpec((tm, d), lambda i: (i, 0))
    vec = pl.BlockSpec((1, d), lambda i: (0, 0))
    return pl.pallas_call(
        body, name="loss_ln3_bwd", grid=(t // tm,),
        in_specs=[row, pl.BlockSpec((tm, 1), lambda i: (i, 0)), vec, vec, row],
        out_specs=[row, row, vec, vec, pl.BlockSpec((1, HEAD_DIM), lambda i: (0, 0))],
        out_shape=[jax.ShapeDtypeStruct((t, d), F32), jax.ShapeDtypeStruct((t, d), BF16),
                   jax.ShapeDtypeStruct((1, d), F32), jax.ShapeDtypeStruct((1, d), F32),
                   jax.ShapeDtypeStruct((1, HEAD_DIM), F32)],
        compiler_params=_params("arbitrary"),
    )(xhat, rstd, gamma, beta, target)


def _after(token, a):
    return a if token is None else a + token[:1, :1].astype(a.dtype)


def _pick(n, prefs):
    for p in prefs:
        if n % p == 0:
            return p
    return n


def _local_step(x, mem, target, w):
    t, d = x.shape
    heads = w["a_log"].shape[1]
    gw = heads * HEAD_DIM
    groups, cg, _ = w["pool_w"].shape
    pw = groups * cg
    n_main = 4 * gw + pw
    in_cols = n_main + 2 * heads
    s_in = w["w_in_t"].shape[0]

    tm = _pick(t, (512, 256, 128))
    tm_ln = _pick(t, (256, 128))
    tm_big = _pick(t, (1024, 512, 256, 128))
    tk = _pick(d, K_STEPS)

    w_in_t = w["w_in_t"].reshape(in_cols, d)
    w_p_t = w_in_t[4 * gw + 2 * heads:]
    w_ba_t = jnp.pad(w_in_t[4 * gw:4 * gw + 2 * heads], ((0, HEAD_DIM - 2 * heads), (0, 0)))
    x_bf = x.astype(BF16)
    mem_bf = mem.astype(BF16)

    tn_d = _pick(d, (1024, 512, 256, 128))
    proj = _plain("proj_main", x_bf, w_in_t, tb=True, n_used=4 * gw, tm=tm_big, tn=_pick(4 * gw, (1024, 512, 256, 128)),
                  tk=tk, out_dtype=F32)
    pproj = _plain("proj_pool", x_bf, w_p_t, tb=True, tm=tm_big, tn=_pick(pw, (1024, 512, 256, 128)), tk=tk, out_dtype=F32)
    ea, dtb = _gate_vectors(w["a_log"], w["dt_bias"], heads)
    vec128 = lambda i, j: (0, 0)
    ba, bg = _matmul(
        "proj_gates", x_bf, w_ba_t, tb=True, tm=tm, tn=HEAD_DIM, tk=tk,
        extra=[(ea, (1, HEAD_DIM), vec128), (dtb, (1, HEAD_DIM), vec128)],
        outs=[(jax.ShapeDtypeStruct((t, HEAD_DIM), F32), (tm, HEAD_DIM), _tile)] * 2,
        epilogue=_gates_epilogue(heads))
    qkv = _gdn_pre(proj, w["conv_w"], heads)
    o_gdn, states = _gdn_core(qkv, bg, heads)
    cat_g = _gdn_post(o_gdn, proj, 3 * gw, w["gdn_norm_w"], heads, tm)
    cat_p = _pool_forward(pproj, 0, w["pool_w"], w["pool_scale"])
    cat = jnp.concatenate([cat_g, cat_p], axis=1)
    w = {**w, **(yield ("weights", 1, cat))}
    h1, h1_bf, xhat1, rstd1 = _ln_forward("mix_ln1", cat, w["w_out"], x, w["ln1_g"], w["ln1_b"], tm=tm_ln, tk=tk)

    q = _plain("xattn_q", h1_bf, w["xq_w"], tm=tm, tn=tn_d, tk=tk, out_dtype=BF16)
    mlen = mem.shape[0]
    tm_mem = _pick(mlen, (256, 128))
    k = _plain("xattn_k", mem_bf, w["xk_w"], tm=tm_mem, tn=tn_d, tk=tk, out_dtype=BF16)
    v = _plain("xattn_v", mem_bf, w["xv_w"], tm=tm_mem, tn=tn_d, tk=tk, out_dtype=BF16)
    att = _attention(q, k, v, tm)
    h2, h2_bf, xhat2, rstd2 = _ln_forward("xo_ln2", att, w["xo_w"], h1, w["ln2_g"], w["ln2_b"], tm=tm_ln, tk=tk)

    w = {**w, **(yield ("weights", 2, h2_bf))}
    s_up = w["w_up3"].shape[0]
    ff = s_up * w["w_up3"].shape[2]
    tn_f = _pick(ff // s_up, (1024, 512, 256, 128))

    def up_epi(acc, ex, out, i):
        r = jnp.maximum(acc, 0.0)
        out[0][...] = (r * r).astype(BF16)
        out[1][...] = (2.0 * r).astype(BF16)

    act, act_grad = _matmul(
        "mlp_up", h2_bf, w["w_up3"], b_blocks=s_up, tm=tm_big, tn=tn_f, tk=tk,
        outs=[(jax.ShapeDtypeStruct((t, ff), BF16), (tm_big, tn_f), _tile)] * 2, epilogue=up_epi)
    w = {**w, **(yield ("weights", 3, act))}
    tk_f = _pick(ff, K_STEPS)
    xhat3, rstd3 = _ln_forward("down_ln3", act, w["w_down"], h2, w["ln3_g"], w["ln3_b"], tm=tm, tk=tk_f, want_h=False)

    grads = {}
    du3, du3_bf, grads["ln3_g"], grads["ln3_b"], loss = _loss_and_ln_backward(
        xhat3, rstd3, w["ln3_g"], w["ln3_b"], target, tm_ln)

    def dup_epi(acc, ex, out, i):
        out[0][...] = (acc * ex[0][...].astype(F32)).astype(BF16)

    dup = _matmul(
        "mlp_down_dx", du3_bf, w["w_down"], tb=True, tm=tm_big, tn=tn_f, tk=tk,
        extra=[(act_grad, (tm_big, tn_f), _tile)],
        outs=[(jax.ShapeDtypeStruct((t, ff), BF16), (tm_big, tn_f), _tile)], epilogue=dup_epi)[0]
    tk_t = _pick(t, K_STEPS)
    tm_w = _pick(d, (512, 256, 128))
    grads["w_down"] = _plain("mlp_down_dw", act, du3_bf, ta=True, tm=_pick(ff, (512, 256, 128)), tn=d, tk=tk_t,
                             out_dtype=F32)
    grads["w_up3"] = _plain("mlp_up_dw", h2_bf, dup, ta=True, tm=tm_w, tn=ff // s_up, tk=tk_t, out_dtype=F32, out3=s_up,
                            n_outer=True)
    token = yield ("grads", 0, {n: grads.pop(n) for n in ("w_down", "w_up3")})
    du2, du2_bf, grads["ln2_g"], grads["ln2_b"] = _ln_backward(
        "mlp_up_dx_ln2", dup, w["w_up3"], du3, xhat2, rstd2, _after(token, w["ln2_g"]), tm=tm,
        tk=_pick(ff // s_up, K_STEPS[1:]), b_blocks=s_up)
    token = yield ("poll", 0, du2_bf)

    grads["xo_w"] = _plain("xo_dw", att, du2_bf, ta=True, tm=tm_w, tn=d, tk=tk_t, out_dtype=F32)
    datt = _plain("xo_dx", du2_bf, w["xo_w"], tb=True, tm=tm, tn=tn_d, tk=tk, out_dtype=BF16)
    dq, dk, dv = _attention_backward(q, k, v, datt, tm)
    tk_m = _pick(mlen, (256, 128))
    grads["xq_w"] = _plain("xq_dw", h1_bf, dq, ta=True, tm=tm_w, tn=d, tk=tk_t, out_dtype=F32)
    grads["xk_w"] = _plain("xk_dw", mem_bf, dk, ta=True, tm=tm_w, tn=tn_d, tk=tk_m, out_dtype=F32)
    grads["xv_w"] = _plain("xv_dw", mem_bf, dv, ta=True, tm=tm_w, tn=tn_d, tk=tk_m, out_dtype=F32)
    du1, du1_bf, grads["ln1_g"], grads["ln1_b"] = _ln_backward(
        "xq_dx_ln1", dq, w["xq_w"], du2, xhat1, rstd1, _after(token, w["ln1_g"]), tm=tm_ln, tk=tk)

    grads["w_out"] = _plain("out_dw", cat, du1_bf, ta=True, tm=tm_w, tn=d, tk=tk_t, out_dtype=F32)
    token = yield ("grads", 1, {n: grads.pop(n) for n in ("xo_w", "xq_w", "xk_w", "xv_w", "w_out")})
    dcat = _plain("out_dx", du1_bf, w["w_out"], tb=True, tm=tm, tn=tn_d, tk=tk, out_dtype=F32)
    dp, grads["pool_w"], grads["pool_scale"] = _pool_backward(dcat, gw, pproj, 0, w["pool_w"],
                                                              _after(token, w["pool_scale"]))
    do_gdn, dz, grads["gdn_norm_w"] = _gdn_post_backward(dcat, o_gdn, proj, 3 * gw, _after(token, w["gdn_norm_w"]),
                                                         heads, tm)
    token = yield ("poll", 1, do_gdn)
    dqkv, dbg = _gdn_core_backward(qkv, _after(token, bg), states, do_gdn, heads)
    dqkv_pre, grads["conv_w"] = _gdn_pre_backward(proj, w["conv_w"], dqkv, heads)
    dba, dalog_row, ddt_row = _gates_backward(ba, bg, dbg, ea, dtb, heads)
    grads["a_log"] = dalog_row[:, heads:2 * heads]
    grads["dt_bias"] = ddt_row[:, heads:2 * heads]

    dproj = jnp.concatenate([dqkv_pre, dz, dp], axis=1)
    dw_main = _plain("proj_dw", dproj, x_bf, ta=True, tm=_pick(n_main, (512, 256, 128)), tn=d, tk=tk_t, out_dtype=F32)
    dw_ba = _plain("proj_gates_dw", dba, x_bf, ta=True, tm=HEAD_DIM, tn=tn_d, tk=tk_t, out_dtype=F32)
    dw_in_t = jnp.concatenate([dw_main[:4 * gw], dw_ba[:2 * heads], dw_main[4 * gw:]], axis=0)
    grads["w_in_t"] = dw_in_t.reshape(s_in, in_cols // s_in, d)

    def dx_epi(acc, ex, out, i):
        out[0][...] = acc + ex[1][...] + ALPHA * ex[0][...]

    def add_epi(acc, ex, out, i):
        out[0][...] = acc + ex[0][...]

    token = yield ("grads", 2, {n: grads.pop(n) for n in ("w_in_t", "pool_w")})
    dx_gates = _plain("proj_gates_dx", dba, _after(token, w_ba_t), tm=tm, tn=tn_d, tk=HEAD_DIM, out_dtype=F32)
    out_tile = [(jax.ShapeDtypeStruct((t, d), F32), (tm, tn_d), _tile)]
    dx_pool = _matmul("proj_pool_dx", dp, w_p_t, tm=tm, tn=tn_d, tk=_pick(pw, K_STEPS),
                      extra=[(dx_gates, (tm, tn_d), _tile)], outs=out_tile, epilogue=add_epi)[0]
    grad_x = _matmul(
        "proj_dx", dproj, w_in_t, k_used=4 * gw, tm=tm, tn=tn_d, tk=_pick(4 * gw, K_STEPS),
        extra=[(du1, (tm, tn_d), _tile), (dx_pool, (tm, tn_d), _tile)], outs=out_tile, epilogue=dx_epi)[0]
    yield ("poll", 2, grad_x)
    return loss, grad_x, grads


def _adamw(name, w, g, m, v):
    r, c = w.shape
    if r % 8 == 0:
        tr = _pick(r, (256, 128, 64, 32, 16, 8))
        blk, steps = pl.BlockSpec((tr, c), lambda i: (i, 0)), r // tr
    else:
        tc = _pick(c, (256, 128))
        blk, steps = pl.BlockSpec((r, tc), lambda i: (0, i)), c // tc
    c1 = 1.0 - ADAM_B1 ** ADAM_STEP
    c2 = 1.0 - ADAM_B2 ** ADAM_STEP

    def body(w_ref, g_ref, m_ref, v_ref, d_ref, mo_ref, vo_ref):
        gv = g_ref[...]
        mn = ADAM_B1 * m_ref[...] + (1.0 - ADAM_B1) * gv
        vn = ADAM_B2 * v_ref[...] + (1.0 - ADAM_B2) * (gv * gv)
        d_ref[...] = -ADAM_LR * ((mn / c1) / (jnp.sqrt(vn / c2) + ADAM_EPS) + ADAM_WD * w_ref[...])
        mo_ref[...] = mn
        vo_ref[...] = vn

    return pl.pallas_call(
        body, name=name, grid=(steps,), in_specs=[blk] * 4, out_specs=[blk] * 3,
        out_shape=[jax.ShapeDtypeStruct((r, c), F32)] * 3,
        compiler_params=_params("parallel"),
    )(w, g, m, v)


def _place():
    x, y, c = lax.axis_index("x"), lax.axis_index("y"), lax.axis_index("c")
    chips = [(1 - x, y), (x, 1 - y), (1 - x, 1 - y)]
    return x, y, c, chips


HBM = pl.BlockSpec(memory_space=pltpu.HBM)


SEM = pl.BlockSpec(memory_space=pltpu.SEMAPHORE)
ANY = pl.BlockSpec(memory_space=pl.ANY)
EFFECT = pltpu.SideEffectType.DATAFLOW_SIDE_EFFECTING


def _in_hbm(a):
    return pltpu.with_memory_space_constraint(a, pltpu.HBM)


def _remote(src, dst, send_sem, recv_sem, to):
    return pltpu.make_async_remote_copy(src_ref=src, dst_ref=dst, send_sem=send_sem, recv_sem=recv_sem,
                                        device_id=to, device_id_type=MESH)


def _by_rows(rows):
    return rows % 32 == 0


def _half_shape(rows, cols):
    return (rows // 2, cols) if _by_rows(rows) else (rows, cols // 2)


def _half(ref, which, *lead):
    rows, cols = ref.shape[-2:]
    if _by_rows(rows):
        return ref.at[(*lead, pl.ds(which * (rows // 2), rows // 2))]
    return ref.at[(*lead, slice(None), pl.ds(which * (cols // 2), cols // 2))]


def _landed(lands, i, shard_index, which):
    return _half(lands[i], which, shard_index)


def _gather_start(shards, after):
    n = len(shards)
    lands = [lax.empty((N_SHARD,) + s.shape, s.dtype) for s in shards]

    def body(*refs):
        ins, zones = refs[:n], refs[n:2 * n]
        ici_send, ici_recv, own_send, own_recv = refs[2 * n + 1:2 * n + 5]
        token = refs[-1]
        x, y, c, chips = _place()
        me = 2 * x + y
        for i in range(n):
            for j, chip in enumerate(chips):
                _remote(_half(ins[i], c), _landed(zones, i, me, c), ici_send.at[3 * i + j],
                        ici_recv.at[3 * i + j], (*chip, c)).start()
        for i in range(n):
            _remote(ins[i], zones[i].at[me], own_send.at[i], own_recv.at[i], (x, y, 1 - c)).start()
        token[...] = jnp.zeros_like(token)

    dma = pltpu.SemaphoreType.DMA
    outs = pl.pallas_call(
        body, name="gather_start",
        in_specs=[HBM] * (2 * n) + [ANY],
        out_shape=(dma((3 * n,)), dma((3 * n,)), dma((n,)), dma((n,)),
                   *[pltpu.HBM(a.shape, a.dtype) for a in shards + lands], jax.ShapeDtypeStruct((8, LANES), F32)),
        out_specs=(SEM, SEM, SEM, SEM, *[HBM] * (2 * n), pl.BlockSpec(memory_space=pltpu.VMEM)),
        input_output_aliases={k: 4 + k for k in range(2 * n)},
        compiler_params=pltpu.CompilerParams(has_side_effects=EFFECT),
    )(*[_in_hbm(a) for a in shards + lands], after)
    sems = dict(zip(("ici_send", "ici_recv", "own_send", "own_recv"), outs[:4]))
    return sems, list(outs[4:4 + n]), list(outs[4 + n:4 + 2 * n]), outs[-1]


def _gather_forward(name, idx, lands, sems, after):
    n = len(idx)

    def body(*refs):
        zones = refs[:n]
        ici_recv = refs[n]
        fwd_send, fwd_recv = refs[n + 2], refs[n + 3]
        x, y, c, chips = _place()
        for k, i in enumerate(idx):
            for j, chip in enumerate(chips):
                half = _landed(zones, k, 2 * chip[0] + chip[1], c)
                _remote(half, half, fwd_send.at[3 * k + j], ici_recv.at[3 * i + j], (*chip, c)).wait_recv()
                _remote(half, half, fwd_send.at[3 * k + j], fwd_recv.at[3 * k + j], (x, y, 1 - c)).start()

    dma = pltpu.SemaphoreType.DMA
    outs = pl.pallas_call(
        body, name=name,
        in_specs=[HBM] * n + [SEM, ANY],
        out_shape=(dma((3 * n,)), dma((3 * n,)), *[pltpu.HBM(a.shape, a.dtype) for a in lands]),
        out_specs=(SEM, SEM, *[HBM] * n),
        input_output_aliases={k: 2 + k for k in range(n)},
        compiler_params=pltpu.CompilerParams(has_side_effects=EFFECT),
    )(*lands, sems["ici_recv"], after)
    return (outs[0], outs[1]), list(outs[2:])


def _gather_wait(name, idx, shards, lands, sems, fwd):
    n = len(idx)

    def body(*refs):
        ins, zones = refs[:n], refs[n:2 * n]
        ici_send, own_send, own_recv, fwd_send, fwd_recv = refs[2 * n:2 * n + 5]
        x, y, c, chips = _place()
        me = 2 * x + y
        for k, i in enumerate(idx):
            mine = _half(ins[k], c)
            for j, chip in enumerate(chips):
                theirs = 2 * chip[0] + chip[1]
                _remote(mine, _landed(zones, k, me, c), ici_send.at[3 * i + j], fwd_recv.at[3 * k + j],
                        (*chip, c)).wait_send()
                sent = _landed(zones, k, theirs, c)
                _remote(sent, sent, fwd_send.at[3 * k + j], fwd_recv.at[3 * k + j], (x, y, 1 - c)).wait_send()
                passed = _landed(zones, k, theirs, 1 - c)
                _remote(passed, passed, fwd_send.at[3 * k + j], fwd_recv.at[3 * k + j], (x, y, 1 - c)).wait_recv()
            own = _remote(ins[k], zones[k].at[me], own_send.at[i], own_recv.at[i], (x, y, 1 - c))
            own.wait_send()
            own.wait_recv()

    outs = pl.pallas_call(
        body, name=name,
        in_specs=[HBM] * (2 * n) + [SEM] * 5,
        out_shape=tuple(pltpu.HBM(a.shape, a.dtype) for a in lands),
        out_specs=tuple([HBM] * n),
        input_output_aliases={n + k: k for k in range(n)},
        compiler_params=pltpu.CompilerParams(has_side_effects=EFFECT),
    )(*shards, *lands, sems["ici_send"], sems["own_send"], sems["own_recv"], fwd[0], fwd[1])
    return list(outs)


def _all_reduce_small(name, slab, after=None):
    r, width = slab.shape
    ndev = 8

    def body(x_ref, after_ref, out_ref, buf, send_sems, recv_sems):
        x, y, c, _ = _place()
        me = 4 * x + 2 * y + c
        buf[me] = x_ref[...]
        copies = []
        for k in range(1, ndev):
            peer = jnp.bitwise_xor(me, k)
            to = (peer // 4, (peer // 2) % 2, peer % 2)
            cp = pltpu.make_async_remote_copy(src_ref=x_ref, dst_ref=buf.at[me], send_sem=send_sems.at[k - 1],
                                              recv_sem=recv_sems.at[k - 1], device_id=to, device_id_type=MESH)
            cp.start()
            copies.append(cp)
        for k in range(1, ndev):
            peer = jnp.bitwise_xor(me, k)
            pltpu.make_async_remote_copy(src_ref=x_ref, dst_ref=buf.at[peer], send_sem=send_sems.at[k - 1],
                                         recv_sem=recv_sems.at[k - 1], device_id=(x, y, c),
                                         device_id_type=MESH).wait_recv()
        for cp in copies:
            cp.wait_send()
        total = buf[0]
        for d in range(1, ndev):
            total = total + buf[d]
        out_ref[...] = total

    return pl.pallas_call(
        body, name=name,
        in_specs=[pl.BlockSpec(memory_space=pltpu.VMEM), ANY], out_specs=pl.BlockSpec(memory_space=pltpu.VMEM),
        out_shape=jax.ShapeDtypeStruct((r, width), F32),
        scratch_shapes=[pltpu.VMEM((ndev, r, width), F32), pltpu.SemaphoreType.DMA((ndev - 1,)),
                        pltpu.SemaphoreType.DMA((ndev - 1,))],
        compiler_params=pltpu.CompilerParams(vmem_limit_bytes=VMEM_LIMIT),
    )(slab, slab if after is None else after)


def _half_tiling(rows, cols):
    if _by_rows(rows):
        tr = _pick(rows // 2, (256, 128, 64, 32, 16))
        nb = (rows // 2) // tr
        return (tr, cols), nb, (lambda which, b: (which * nb + b, 0)), (lambda b: (b, 0))
    tc = _pick(cols // 2, (256, 128))
    nb = (cols // 2) // tc
    return (rows, tc), nb, (lambda which, b: (0, which * nb + b)), (lambda b: (0, b))


def _chip_partial(name, grad, other, core):
    s, r, cdim = grad.shape
    blk, nb, whole, within = _half_tiling(r, cdim)

    def body(core_ref, g_ref, o_ref, out_ref):
        out_ref[...] = (g_ref[...] + o_ref[...]).astype(BF16)

    return pl.pallas_call(
        body, name=name,
        grid_spec=pltpu.PrefetchScalarGridSpec(
            num_scalar_prefetch=1, grid=(s, nb),
            in_specs=[pl.BlockSpec((None,) + blk, lambda j, b, core_ref: (j,) + whole(core_ref[0], b)),
                      pl.BlockSpec((None,) + blk, lambda j, b, core_ref: (j,) + within(b))],
            out_specs=pl.BlockSpec((None,) + blk, lambda j, b, core_ref: (j,) + within(b))),
        out_shape=jax.ShapeDtypeStruct((s,) + _half_shape(r, cdim), BF16),
        compiler_params=_params("parallel", "parallel"),
    )(core, grad, other)


def _partial_copies(ins, zones, send_sems, recv_sems):
    x, y, c, chips = _place()
    return [_remote(ins[i].at[2 * chip[0] + chip[1]], zones[i].at[j], send_sems.at[3 * i + j],
                    recv_sems.at[3 * i + j], (*chip, c))
            for i in range(len(ins)) for j, chip in enumerate(chips)]


def _swap_copies(ins, zones, send_sems, recv_sems):
    x, y, c, _ = _place()
    copies = []
    for i in range(len(ins)):
        for s in range(N_SHARD):
            copies.append(_remote(_half(ins[i], 1 - c, s), zones[i].at[s],
                                  send_sems.at[N_SHARD * i + s], recv_sems.at[N_SHARD * i + s], (x, y, 1 - c)))
    return copies


def _exchange_start(name, plan, sources, lands, per_array):
    n = len(sources)
    lands = [lax.empty(shape, dtype) for shape, dtype in lands]

    def body(*refs):
        for cp in plan(refs[:n], refs[n:2 * n], refs[2 * n], refs[2 * n + 1]):
            cp.start()
        refs[-1][...] = jnp.zeros_like(refs[-1])

    dma = pltpu.SemaphoreType.DMA
    outs = pl.pallas_call(
        body, name=name,
        in_specs=[HBM] * (2 * n),
        out_shape=(dma((per_array * n,)), dma((per_array * n,)),
                   *[pltpu.HBM(a.shape, a.dtype) for a in list(sources) + lands], jax.ShapeDtypeStruct((8, LANES), F32)),
        out_specs=(SEM, SEM, *[HBM] * (2 * n), pl.BlockSpec(memory_space=pltpu.VMEM)),
        input_output_aliases={k: 2 + k for k in range(2 * n)},
        compiler_params=pltpu.CompilerParams(has_side_effects=EFFECT),
    )(*[_in_hbm(a) for a in list(sources) + lands])
    return (outs[0], outs[1]), list(outs[2:2 + n]), list(outs[2 + n:2 + 2 * n]), outs[-1]


def _exchange_wait(name, plan, started, after):
    sems, partials, lands, _ = started
    n = len(partials)

    def body(*refs):
        for cp in plan(refs[:n], refs[n:2 * n], refs[2 * n], refs[2 * n + 1]):
            cp.wait_send()
            cp.wait_recv()

    outs = pl.pallas_call(
        body, name=name,
        in_specs=[HBM] * (2 * n) + [SEM, SEM] + [ANY] * len(after),
        out_shape=tuple(pltpu.HBM(a.shape, a.dtype) for a in lands),
        out_specs=tuple([HBM] * n),
        input_output_aliases={n + k: k for k in range(n)},
        compiler_params=pltpu.CompilerParams(has_side_effects=EFFECT),
    )(*partials, *lands, sems[0], sems[1], *after)
    return list(outs)


def _reduce_own(name, grad, other, received, where):
    s, r, cdim = grad.shape
    blk, nb, whole, within = _half_tiling(r, cdim)

    def body(where_ref, g_ref, o_ref, r_ref, out_ref):
        total = g_ref[...] + o_ref[...]
        for j in range(3):
            total = total + r_ref[j].astype(F32)
        out_ref[...] = total

    return pl.pallas_call(
        body, name=name,
        grid_spec=pltpu.PrefetchScalarGridSpec(
            num_scalar_prefetch=1, grid=(nb,),
            in_specs=[pl.BlockSpec((None,) + blk, lambda b, w_ref: (w_ref[0],) + whole(w_ref[1], b)),
                      pl.BlockSpec((None,) + blk, lambda b, w_ref: (w_ref[0],) + within(b)),
                      pl.BlockSpec((3,) + blk, lambda b, w_ref: (0,) + within(b))],
            out_specs=pl.BlockSpec(blk, lambda b, w_ref: whole(w_ref[1], b))),
        out_shape=jax.ShapeDtypeStruct((r, cdim), F32),
        compiler_params=_params("parallel"),
    )(where, grad, other, received)


def _join_start(name, halves):
    n = len(halves)

    def body(*refs):
        bufs, send_sems, recv_sems = refs[:n], refs[n], refs[n + 1]
        x, y, c, _ = _place()
        for i in range(n):
            mine = _half(bufs[i], c)
            _remote(mine, mine, send_sems.at[i], recv_sems.at[i], (x, y, 1 - c)).start()
        refs[-1][...] = jnp.zeros_like(refs[-1])

    dma = pltpu.SemaphoreType.DMA
    outs = pl.pallas_call(
        body, name=name,
        in_specs=[HBM] * n,
        out_shape=(dma((n,)), dma((n,)), *[pltpu.HBM(h.shape, F32) for h in halves], jax.ShapeDtypeStruct((8, LANES), F32)),
        out_specs=(SEM, SEM, *[HBM] * n, pl.BlockSpec(memory_space=pltpu.VMEM)),
        input_output_aliases={k: 2 + k for k in range(n)},
        compiler_params=pltpu.CompilerParams(has_side_effects=EFFECT),
    )(*[_in_hbm(h) for h in halves])
    return (outs[0], outs[1]), list(outs[2:2 + n]), outs[-1]


def _join_wait(name, started, after):
    sems, bufs, _ = started
    n = len(bufs)

    def body(*refs):
        bufs, send_sems, recv_sems = refs[:n], refs[n], refs[n + 1]
        x, y, c, _ = _place()
        for i in range(n):
            mine, theirs = _half(bufs[i], c), _half(bufs[i], 1 - c)
            _remote(mine, mine, send_sems.at[i], recv_sems.at[i], (x, y, 1 - c)).wait_send()
            _remote(theirs, theirs, send_sems.at[i], recv_sems.at[i], (x, y, 1 - c)).wait_recv()

    outs = pl.pallas_call(
        body, name=name,
        in_specs=[HBM] * n + [SEM, SEM] + [ANY] * len(after),
        out_shape=tuple(pltpu.HBM(b.shape, F32) for b in bufs),
        out_specs=tuple([HBM] * n),
        input_output_aliases={k: k for k in range(n)},
        compiler_params=pltpu.CompilerParams(has_side_effects=EFFECT),
    )(*bufs, sems[0], sems[1], *after)
    return list(outs)


BIG = ("w_in", "pool_w", "w_out", "xq_w", "xk_w", "xv_w", "xo_w", "w_up", "w_down")
GATHER_GROUPS = ((0, 1), (2, 3, 4, 5, 6), (7,), (8,))
SMALL = ("conv_w", "a_log", "dt_bias", "gdn_norm_w", "pool_scale", "ln1_g", "ln1_b", "ln2_g", "ln2_b", "ln3_g", "ln3_b")
ORDER = ("w_in", "conv_w", "a_log", "dt_bias", "gdn_norm_w", "pool_w", "pool_scale", "w_out", "ln1_g", "ln1_b",
         "xq_w", "xk_w", "xv_w", "xo_w", "ln2_g", "ln2_b", "w_up", "w_down", "ln3_g", "ln3_b")
LANES = 128


def _rows(flat_len):
    return -(-flat_len // LANES)


def _pack(pieces):
    out = []
    for p in pieces:
        flat = p.reshape(-1).astype(F32)
        out.append(jnp.pad(flat, (0, _rows(flat.shape[0]) * LANES - flat.shape[0])).reshape(-1, LANES))
    slab = jnp.concatenate(out, axis=0)
    return jnp.pad(slab, ((0, -slab.shape[0] % 8), (0, 0)))


def _unpack(slab, shapes):
    out, row = [], 0
    for shp in shapes:
        size = math.prod(shp)
        out.append(slab[row:row + _rows(size)].reshape(-1)[:size].reshape(shp))
        row += _rows(size)
    return out


TRANSPOSED = ("w_in",)


def _as2d(name, a):
    a = a[0]
    if name in TRANSPOSED:
        return jnp.swapaxes(a, 0, 1)
    return a.reshape(-1, a.shape[-1]) if a.ndim == 3 else a


def _from2d(name, a, shape):
    return (jnp.swapaxes(a, 0, 1) if name in TRANSPOSED else a).reshape(shape)


def kernel(x, mem, w_in, conv_w, a_log, dt_bias, gdn_norm_w, pool_w, pool_scale, w_out, ln1_g, ln1_b, xq_w, xk_w, xv_w, xo_w, ln2_g, ln2_b, w_up, w_down, ln3_g, ln3_b, loss_target, m_w_in, m_conv_w, m_a_log, m_dt_bias, m_gdn_norm_w, m_pool_w, m_pool_scale, m_w_out, m_ln1_g, m_ln1_b, m_xq_w, m_xk_w, m_xv_w, m_xo_w, m_ln2_g, m_ln2_b, m_w_up, m_w_down, m_ln3_g, m_ln3_b, v_w_in, v_conv_w, v_a_log, v_dt_bias, v_gdn_norm_w, v_pool_w, v_pool_scale, v_w_out, v_ln1_g, v_ln1_b, v_xq_w, v_xk_w, v_xv_w, v_xo_w, v_ln2_g, v_ln2_b, v_w_up, v_w_down, v_ln3_g, v_ln3_b):
    given = dict(locals())
    cx, cy, cc = lax.axis_index("x"), lax.axis_index("y"), lax.axis_index("c")
    me = 2 * cx + cy
    groups = pool_w.shape[1]
    cs = pool_w.shape[2]
    kk, conv_cols = conv_w.shape[1], conv_w.shape[2]
    core = cc.astype(jnp.int32).reshape(1)
    where = jnp.stack([me, cc]).astype(jnp.int32)

    conv_slab = jnp.zeros((kk, N_SHARD * conv_cols), F32)
    conv_slab = lax.dynamic_update_slice(conv_slab, conv_w[0] * (cc == 0).astype(F32), (0, me * conv_cols))
    wts = {"conv_w": _unpack(_all_reduce_small("gather_conv_w", _pack([conv_slab])), [conv_slab.shape])[0]}

    sems, shards, lands, token = _gather_start([_as2d(n, given[n]).astype(BF16) for n in BIG], wts["conv_w"])

    def fetch(group, after):
        idx = GATHER_GROUPS[group]
        fwd, zones = _gather_forward(f"gather_forward_{group}", idx, [lands[i] for i in idx], sems, after)
        full = dict(zip([BIG[i] for i in idx],
                        _gather_wait(f"gather_wait_{group}", idx, [shards[i] for i in idx], zones, sems, fwd)))
        out = {}
        for n, a in full.items():
            if n == "w_in":
                out["w_in_t"] = a
            elif n == "w_up":
                out["w_up3"] = a
            elif n == "pool_w":
                out[n] = a.reshape(N_SHARD, groups, cs, -1).transpose(1, 0, 2, 3).reshape(groups, N_SHARD * cs, -1)
            else:
                out[n] = a.reshape(-1, a.shape[-1])
        return out

    for n in ("a_log", "dt_bias", "gdn_norm_w", "pool_scale", "ln1_g", "ln1_b", "ln2_g", "ln2_b", "ln3_g", "ln3_b"):
        wts[n] = given[n]
    wts.update(fetch(0, token))

    def start_swap(group, grads):
        names, blocks = [], []
        for n, g in grads.items():
            if n == "pool_w":
                g = g.reshape(groups, N_SHARD, cs, -1).transpose(1, 0, 2, 3).reshape(N_SHARD, groups * cs, -1)
            elif g.ndim == 2:
                g = g.reshape(N_SHARD, -1, g.shape[-1])
            names.append({"w_in_t": "w_in", "w_up3": "w_up"}.get(n, n))
            blocks.append(g)
        zones = [((N_SHARD,) + _half_shape(b.shape[1], b.shape[2]), F32) for b in blocks]
        swap = _exchange_start(f"grad_swap_start_{group}", _swap_copies, blocks, zones, N_SHARD)
        return {"group": group, "names": names, "swap": swap, "token": swap[3]}

    def start_send(state, after):
        group, names = state["group"], state["names"]
        state["blocks"] = state["swap"][1]
        state["others"] = _exchange_wait(f"grad_swap_wait_{group}", _swap_copies, state["swap"], after)
        partials = [_chip_partial("chip_partial_" + n, gb, ob, core)
                    for n, gb, ob in zip(names, state["blocks"], state["others"])]
        zones = [((3,) + p.shape[1:], BF16) for p in partials]
        state["send"] = _exchange_start(f"grad_send_start_{group}", _partial_copies, partials, zones, 3)
        state["token"] = state["send"][3]

    grad, delta, new_m, new_v = {}, {}, {}, {}

    def start_join(state, after):
        group, names = state["group"], state["names"]
        received = _exchange_wait(f"grad_send_wait_{group}", _partial_copies, state["send"], after)
        halves = [_reduce_own("reduce_own_" + n, gb, ob, rb, where)
                  for n, gb, ob, rb in zip(names, state["blocks"], state["others"], received)]
        state["join"] = _join_start(f"grad_join_start_{group}", halves)
        return state["join"][2]

    def finish_reduce(state, after):
        group, names = state["group"], state["names"]
        for n, g in zip(names, _join_wait(f"grad_join_wait_{group}", state["join"], after)):
            shp = given[n].shape
            d2, m2, v2 = _adamw("adamw_" + n, _as2d(n, given[n]), g, _as2d(n, given["m_" + n]), _as2d(n, given["v_" + n]))
            grad[n], delta[n], new_m[n], new_v[n] = (_from2d(n, a, shp) for a in (g, d2, m2, v2))
        return d2

    step = _local_step(x[0], mem[0], loss_target[0], wts)
    pending = {}
    request = next(step)
    while True:
        try:
            kind, group, payload = request
            if kind == "weights":
                request = step.send(fetch(group, payload))
            elif kind == "grads":
                pending[group] = start_swap(group, payload)
                request = step.send(pending[group]["token"])
            else:
                start_send(pending[group], [payload])
                request = step.send(pending[group]["token"])
        except StopIteration as stop:
            loss_row, grad_x, g = stop.value
            break

    after = [pending[2]["token"], grad_x]
    for group in (0, 1):
        after = [start_join(pending[group], after)]
    for group in (0, 1):
        after = [finish_reduce(pending[group], after)]
    after = [finish_reduce(pending[2], [start_join(pending[2], after)])]

    small_names = ("a_log", "dt_bias", "gdn_norm_w", "pool_scale", "ln1_g", "ln1_b", "ln2_g", "ln2_b", "ln3_g", "ln3_b")
    pieces = [g["conv_w"]] + [g[n] for n in small_names] + [loss_row[:, :1]]
    shapes = [p.shape for p in pieces]
    summed = _unpack(_all_reduce_small("all_reduce_small", _pack(pieces), after[0]), shapes)
    gsmall = dict(zip(small_names, summed[1:-1]))
    gsmall["conv_w"] = lax.dynamic_slice(summed[0], (0, me * conv_cols), (kk, conv_cols))
    loss = summed[-1][0, 0]

    sshapes = [given[n].shape for n in SMALL]
    slabs = [_pack([given[p + n] for n in SMALL]) for p in ("", "m_", "v_")]
    gslab = _pack([gsmall[n] for n in SMALL])
    outs = _adamw("adamw_small", slabs[0], gslab, slabs[1], slabs[2])
    for dst, slab in zip((delta, new_m, new_v), outs):
        dst.update(zip(SMALL, _unpack(slab, sshapes)))
    for n in SMALL:
        grad[n] = gsmall[n].reshape(given[n].shape)

    return (loss, grad_x[None], *[grad[n] for n in ORDER], *[delta[n] for n in ORDER],
            *[new_m[n] for n in ORDER], *[new_v[n] for n in ORDER])
```

```python
import functools
import math

import jax
import jax.numpy as jnp
from jax import lax
from jax.experimental import pallas as pl
from jax.experimental.pallas import tpu as pltpu

F32 = jnp.float32
BF16 = jnp.bfloat16
MESH = pl.DeviceIdType.MESH

HEAD_DIM = 128
CHUNK = 64
POOL_WINDOWS = (2, 4, 8, 16)
XATTN_HEADS = 4
ALPHA = 2.0 ** 0.25
LN_EPS = 1e-5
NORM_EPS = 1e-6
ADAM_LR, ADAM_B1, ADAM_B2, ADAM_EPS, ADAM_WD, ADAM_STEP = 0.001, 0.9, 0.999, 1e-08, 0.01, 10
N_SHARD = 4
VMEM_LIMIT = 56 * 1024 * 1024
K_STEPS = (2048, 1024, 512, 256, 128)


def _params(*sem):
    return pltpu.CompilerParams(dimension_semantics=sem, vmem_limit_bytes=VMEM_LIMIT)


def _bdot(a, b, ta=False, tb=False):
    dims = (((0 if ta else 1,), (1 if tb else 0,)), ((), ()))
    return lax.dot_general(a.astype(BF16), b.astype(BF16), dims, preferred_element_type=F32)


def _sigmoid(x):
    return 1.0 / (1.0 + jnp.exp(-x))


def _matmul(name, a, b, *, ta=False, tb=False, tm, tn, tk, extra=(), outs, epilogue, b_blocks=None,
            sequential=False, n_used=None, k_used=None, n_outer=False):
    m, k_dim = (a.shape[1], a.shape[0]) if ta else a.shape
    if b_blocks and tb:
        n = b.shape[1]
        k_dim = b.shape[0] * b.shape[2]
        per = b.shape[2] // tk
        b_spec = pl.BlockSpec((None, tn, tk), lambda i, j, k: (k // per, j, k % per))
    elif b_blocks:
        n = b.shape[0] * b.shape[2]
        per = b.shape[2] // tn
        b_spec = pl.BlockSpec((None, tk, tn), lambda i, j, k: (j // per, k, j % per))
    elif tb:
        n = b.shape[0]
        b_spec = pl.BlockSpec((tn, tk), lambda i, j, k: (j, k))
    else:
        n = b.shape[1]
        b_spec = pl.BlockSpec((tk, tn), lambda i, j, k: (k, j))
    n, k_dim = n_used or n, k_used or k_dim
    assert m % tm == 0 and n % tn == 0 and k_dim % tk == 0, (name, m, n, k_dim, tm, tn, tk)
    nk = k_dim // tk
    a_spec = pl.BlockSpec((tk, tm), lambda i, j, k: (k, i)) if ta else pl.BlockSpec((tm, tk), lambda i, j, k: (i, k))
    n_extra, n_out = len(extra), len(outs)

    def wrap(index_map):
        return lambda i, j, k: index_map(i, j)

    def spec(block, index_map):
        if n_outer:
            return pl.BlockSpec(block, lambda j, i, k: index_map(i, j, k))
        return pl.BlockSpec(block, index_map)

    row_axis = 1 if n_outer else 0

    def body_one_step(*refs):
        ex = refs[2:2 + n_extra]
        out = refs[2 + n_extra:2 + n_extra + n_out]
        epilogue(_bdot(refs[0][...], refs[1][...], ta, tb), ex, out, pl.program_id(row_axis))

    def body(*refs):
        a_ref, b_ref = refs[0], refs[1]
        ex = refs[2:2 + n_extra]
        out = refs[2 + n_extra:2 + n_extra + n_out]
        acc = refs[-1]
        i, k = pl.program_id(row_axis), pl.program_id(2)
        part = _bdot(a_ref[...], b_ref[...], ta, tb)

        @pl.when(k == 0)
        def _():
            acc[...] = part

        @pl.when(jnp.logical_and(k > 0, k < nk - 1))
        def _():
            acc[...] += part

        @pl.when(k == nk - 1)
        def _():
            epilogue(acc[...] + part, ex, out, i)

    sem = ("arbitrary",) * 3 if sequential else ("parallel", "parallel", "arbitrary")
    res = pl.pallas_call(
        body_one_step if nk == 1 else body, name=name,
        grid=(n // tn, m // tm, nk) if n_outer else (m // tm, n // tn, nk),
        in_specs=[spec(a_spec.block_shape, a_spec.index_map), spec(b_spec.block_shape, b_spec.index_map)]
        + [spec(bs, wrap(im)) for _, bs, im in extra],
        out_specs=[spec(bs, wrap(im)) for _, bs, im in outs],
        out_shape=[s for s, _, _ in outs],
        scratch_shapes=[] if nk == 1 else [pltpu.VMEM((tm, tn), F32)],
        compiler_params=_params(*sem),
    )(a, b, *[x for x, _, _ in extra])
    return res


def _tile(i, j):
    return (i, j)


def _plain(name, a, b, *, ta=False, tb=False, tm, tn, tk, out_dtype, b_blocks=None, out3=None, n_used=None,
           n_outer=False):
    m = a.shape[1] if ta else a.shape[0]
    if b_blocks:
        n = b.shape[1] if tb else b.shape[0] * b.shape[2]
    else:
        n = n_used or (b.shape[0] if tb else b.shape[1])

    def epi(acc, ex, out, i):
        out[0][...] = acc.astype(out_dtype)

    if out3:
        per = (n // out3) // tn
        spec = (jax.ShapeDtypeStruct((out3, m, n // out3), out_dtype), (None, tm, tn),
                lambda i, j: (j // per, i, j % per))
    else:
        spec = (jax.ShapeDtypeStruct((m, n), out_dtype), (tm, tn), _tile)
    return _matmul(name, a, b, ta=ta, tb=tb, tm=tm, tn=tn, tk=tk, outs=[spec], epilogue=epi,
                   b_blocks=b_blocks, n_used=n_used, n_outer=n_outer)[0]


def _ln_forward(name, a, b, res, gamma, beta, *, tm, tk, want_h=True):
    m, n = res.shape

    def epi(acc, ex, out, i):
        u = ALPHA * ex[0][...] + acc
        mu = jnp.mean(u, axis=-1, keepdims=True)
        xc = u - mu
        var = jnp.mean(xc * xc, axis=-1, keepdims=True)
        rstd = lax.rsqrt(var + LN_EPS)
        xhat = xc * rstd
        out[-2][...] = xhat
        out[-1][...] = rstd
        if want_h:
            h = xhat * ex[1][...] + ex[2][...]
            out[0][...] = h
            out[1][...] = h.astype(BF16)

    row = lambda i, j: (i, 0)
    vec = lambda i, j: (0, 0)
    outs = [(jax.ShapeDtypeStruct((m, n), F32), (tm, n), row), (jax.ShapeDtypeStruct((m, n), BF16), (tm, n), row),
            (jax.ShapeDtypeStruct((m, n), F32), (tm, n), row), (jax.ShapeDtypeStruct((m, 1), F32), (tm, 1), row)]
    return _matmul(
        name, a, b, tm=tm, tn=n, tk=tk,
        extra=[(res, (tm, n), row), (gamma, (1, n), vec), (beta, (1, n), vec)],
        outs=outs if want_h else outs[2:], epilogue=epi)


def _ln_backward_math(dy, xhat, rstd, gamma):
    dxhat = dy * gamma
    m1 = jnp.mean(dxhat, axis=-1, keepdims=True)
    m2 = jnp.mean(dxhat * xhat, axis=-1, keepdims=True)
    du = rstd * (dxhat - m1 - xhat * m2)
    return du, jnp.sum(dy * xhat, axis=0, keepdims=True), jnp.sum(dy, axis=0, keepdims=True)


def _ln_backward(name, a, b, dres, xhat, rstd, gamma, *, tm, tk, b_blocks=None, tb=True):
    m, n = dres.shape

    def epi(acc, ex, out, i):
        dy = acc + ALPHA * ex[0][...]
        du, dg, db = _ln_backward_math(dy, ex[1][...], ex[2][...], ex[3][...])
        out[0][...] = du
        out[1][...] = du.astype(BF16)
        first = i == 0

        @pl.when(first)
        def _():
            out[2][...] = dg
            out[3][...] = db

        @pl.when(jnp.logical_not(first))
        def _():
            out[2][...] += dg
            out[3][...] += db

    row = lambda i, j: (i, 0)
    vec = lambda i, j: (0, 0)
    return _matmul(
        name, a, b, tb=tb, tm=tm, tn=n, tk=tk, b_blocks=b_blocks, sequential=True,
        extra=[(dres, (tm, n), row), (xhat, (tm, n), row), (rstd, (tm, 1), row), (gamma, (1, n), vec)],
        outs=[(jax.ShapeDtypeStruct((m, n), F32), (tm, n), row),
              (jax.ShapeDtypeStruct((m, n), BF16), (tm, n), row),
              (jax.ShapeDtypeStruct((1, n), F32), (1, n), vec),
              (jax.ShapeDtypeStruct((1, n), F32), (1, n), vec)],
        epilogue=epi)


def _shift_down(x, k):
    row = lax.broadcasted_iota(jnp.int32, x.shape, 0)
    return jnp.where(row >= k, pltpu.roll(x, k, axis=0), 0.0)


def _shift_up(x, k):
    t = x.shape[0]
    row = lax.broadcasted_iota(jnp.int32, x.shape, 0)
    return jnp.where(row < t - k, pltpu.roll(x, t - k, axis=0), 0.0)


def _conv_silu_norm(x, w, normalise):
    kk = w.shape[0]
    c = x * w[kk - 1:kk, :]
    for j in range(kk - 1):
        c = c + _shift_down(x, kk - 1 - j) * w[j:j + 1, :]
    sg = _sigmoid(c)
    s = c * sg
    r = lax.rsqrt(jnp.sum(s * s, axis=-1, keepdims=True) + NORM_EPS)
    y = jnp.where(normalise, s * r, s)
    return c, sg, s, r, y


def _gdn_pre(proj, conv_w, heads):
    t = proj.shape[0]
    kk = conv_w.shape[0]

    def body(x_ref, w_ref, o_ref):
        normalise = pl.program_id(0) < 2
        o_ref[...] = _conv_silu_norm(x_ref[...], w_ref[...], normalise)[4]

    col = lambda s, h: (0, s * heads + h)
    return pl.pallas_call(
        body, name="gdn_pre", grid=(3, heads),
        in_specs=[pl.BlockSpec((t, HEAD_DIM), col), pl.BlockSpec((kk, HEAD_DIM), col)],
        out_specs=pl.BlockSpec((t, HEAD_DIM), col),
        out_shape=jax.ShapeDtypeStruct((t, 3 * heads * HEAD_DIM), F32),
        compiler_params=_params("parallel", "parallel"),
    )(proj, conv_w)


def _gdn_pre_backward(proj, conv_w, dqkv, heads):
    t = proj.shape[0]
    kk = conv_w.shape[0]

    def body(x_ref, w_ref, dy_ref, dx_ref, dw_ref):
        normalise = pl.program_id(0) < 2
        x = x_ref[...]
        w = w_ref[...]
        dy = dy_ref[...]
        c, sg, s, r, y = _conv_silu_norm(x, w, normalise)
        ds_norm = r * (dy - y * jnp.sum(dy * y, axis=-1, keepdims=True))
        ds = jnp.where(normalise, ds_norm, dy)
        dc = ds * (sg * (1.0 + c * (1.0 - sg)))
        dx = dc * w[kk - 1:kk, :]
        rows = [None] * kk
        rows[kk - 1] = jnp.sum(dc * x, axis=0, keepdims=True)
        for j in range(kk - 1):
            lag = kk - 1 - j
            dx = dx + _shift_up(dc, lag) * w[j:j + 1, :]
            rows[j] = jnp.sum(dc * _shift_down(x, lag), axis=0, keepdims=True)
        dx_ref[...] = dx.astype(BF16)
        dw_ref[...] = jnp.concatenate(rows, axis=0)

    col = lambda s, h: (0, s * heads + h)
    return pl.pallas_call(
        body, name="gdn_pre_bwd", grid=(3, heads),
        in_specs=[pl.BlockSpec((t, HEAD_DIM), col), pl.BlockSpec((kk, HEAD_DIM), col),
                  pl.BlockSpec((t, HEAD_DIM), col)],
        out_specs=[pl.BlockSpec((t, HEAD_DIM), col), pl.BlockSpec((kk, HEAD_DIM), col)],
        out_shape=[jax.ShapeDtypeStruct((t, 3 * heads * HEAD_DIM), BF16),
                   jax.ShapeDtypeStruct((kk, 3 * heads * HEAD_DIM), F32)],
        compiler_params=_params("parallel", "parallel"),
    )(proj, conv_w, dqkv)


def _gate_vectors(a_log, dt_bias, heads):
    pad = lambda v: jnp.pad(v.astype(F32), ((0, 0), (heads, HEAD_DIM - 2 * heads)))
    return pad(jnp.exp(a_log.astype(F32))), pad(dt_bias)


def _softplus(x):
    return jnp.maximum(x, 0.0) + jnp.log(1.0 + jnp.exp(-jnp.abs(x)))


def _gates_epilogue(heads):
    def epi(acc, ex, out, i):
        lane = lax.broadcasted_iota(jnp.int32, acc.shape, 1)
        beta = _sigmoid(acc)
        g = -ex[0][...] * _softplus(acc + ex[1][...])
        out[0][...] = acc
        out[1][...] = jnp.where(lane < heads, beta, jnp.where(lane < 2 * heads, g, 0.0))
    return epi


def _gates_backward(ba, bg, dbg, ea, dtb, heads):
    t = ba.shape[0]

    def body(ba_ref, bg_ref, d_ref, ea_ref, dt_ref, dba_ref, dal_ref, ddt_ref):
        lane = lax.broadcasted_iota(jnp.int32, (t, HEAD_DIM), 1)
        bgv = bg_ref[...]
        d = d_ref[...]
        db = d * bgv * (1.0 - bgv)
        da = -d * ea_ref[...] * _sigmoid(ba_ref[...] + dt_ref[...])
        is_g = jnp.logical_and(lane >= heads, lane < 2 * heads)
        dba = jnp.where(lane < heads, db, jnp.where(is_g, da, 0.0))
        dba_ref[...] = dba.astype(BF16)
        dal_ref[...] = jnp.sum(jnp.where(is_g, d * bgv, 0.0), axis=0, keepdims=True)
        ddt_ref[...] = jnp.sum(jnp.where(is_g, da, 0.0), axis=0, keepdims=True)

    full = pl.BlockSpec((t, HEAD_DIM), lambda: (0, 0))
    vec = pl.BlockSpec((1, HEAD_DIM), lambda: (0, 0))
    return pl.pallas_call(
        body, name="gates_bwd", grid=(),
        in_specs=[full, full, full, vec, vec], out_specs=[full, vec, vec],
        out_shape=[jax.ShapeDtypeStruct((t, HEAD_DIM), BF16), jax.ShapeDtypeStruct((1, HEAD_DIM), F32),
                   jax.ShapeDtypeStruct((1, HEAD_DIM), F32)],
        compiler_params=pltpu.CompilerParams(vmem_limit_bytes=VMEM_LIMIT),
    )(ba, bg, dbg, ea, dtb)


class _Chunk:
    pass


def _split2(x):
    hi = x.astype(BF16)
    return hi, (x - hi.astype(F32)).astype(BF16)


def _split3(x):
    hi = x.astype(BF16)
    rest = x - hi.astype(F32)
    mid = rest.astype(BF16)
    return hi, mid, (rest - mid.astype(F32)).astype(BF16)


def _dot_mask(mask, x, ta=False):
    hi, mid, lo = _split3(x)
    return _bdot(mask, hi, ta=ta) + (_bdot(mask, mid, ta=ta) + _bdot(mask, lo, ta=ta))


def _transpose_by_identity(x):
    r = x.shape[0]
    eye = (lax.broadcasted_iota(jnp.int32, (r, r), 0) == lax.broadcasted_iota(jnp.int32, (r, r), 1)).astype(BF16)
    hi, mid, lo = _split3(x)
    return _bdot(hi, eye, ta=True) + (_bdot(mid, eye, ta=True) + _bdot(lo, eye, ta=True))


def _dot22(a, b, ta=False, tb=False):
    ah, al = _split2(a)
    bh, bl = _split2(b)
    return _bdot(ah, bh, ta, tb) + (_bdot(ah, bl, ta, tb) + _bdot(al, bh, ta, tb))


def _chunk_gates(bg, heads):
    n = CHUNK
    row = lax.broadcasted_iota(jnp.int32, (n, n), 0)
    col = lax.broadcasted_iota(jnp.int32, (n, n), 1)
    lane = lax.broadcasted_iota(jnp.int32, bg.shape, 1)
    graw = jnp.where(jnp.logical_and(lane >= heads, lane < 2 * heads), bg, 0.0)
    gc = _dot_mask((row >= col).astype(BF16), graw)
    return gc, _transpose_by_identity(gc)


def _in_lockstep(generators):
    results = [None] * len(generators)
    live = list(enumerate(generators))
    while live:
        still = []
        for i, gen in live:
            try:
                next(gen)
                still.append((i, gen))
            except StopIteration as stop:
                results[i] = stop.value
        live = still
    return results


def _chunk_local(q, k, v, beta, gc, grow):
    c = _Chunk()
    n = CHUNK
    row = lax.broadcasted_iota(jnp.int32, (n, n), 0)
    col = lax.broadcasted_iota(jnp.int32, (n, n), 1)
    c.tri = row >= col
    c.strict = row > col
    eye = row == col
    c.gcb = jnp.broadcast_to(gc, (n, HEAD_DIM))
    c.decay = jnp.where(c.tri, jnp.exp(jnp.where(c.tri, gc - grow, 0.0)), 0.0)
    c.eg = jnp.exp(c.gcb)
    glast = c.gcb[n - 1:n, :]
    c.egl = jnp.exp(glast)
    c.ekl = jnp.exp(glast - c.gcb)
    c.beta = beta
    c.q = q * (HEAD_DIM ** -0.5)
    c.k = k
    c.v = v
    c.kb = k * beta
    c.vb = v * beta
    c.kg = c.kb * c.eg
    both = _bdot(jnp.concatenate([c.kb, c.q], axis=0), k, tb=True)
    yield
    c.L = jnp.where(c.strict, both[:n] * c.decay, 0.0)
    c.A = jnp.where(c.tri, both[n:] * c.decay, 0.0)
    x = -c.L
    tinv = eye.astype(F32) + x
    p = _dot22(x, x)
    yield
    for _ in range(int(math.log2(n)) - 2):
        both = _dot22(jnp.concatenate([p, tinv], axis=0), p)
        yield
        p, tinv = both[:n], tinv + both[n:]
    c.T = tinv + _dot22(tinv, p)
    yield
    tinv = c.T
    uw = _dot22(tinv, jnp.concatenate([c.vb, c.kg], axis=1))
    yield
    c.u, c.w = uw[:, :HEAD_DIM], uw[:, HEAD_DIM:]
    c.qg = c.q * c.eg
    c.kdec = k * c.ekl
    return c


def _gdn_core(qkv, bg, heads):
    t = qkv.shape[0]
    nchunk = t // CHUNK

    gw = heads * HEAD_DIM

    def body(qkv_ref, bg_ref, o_ref, s_ref, state):
        @pl.when(pl.program_id(0) == 0)
        def _():
            state[...] = jnp.zeros_like(state)

        bg_v = bg_ref[...]
        gc_all, gc_rows = _chunk_gates(bg_v, heads)
        def one_head(h):
            col = lambda s: pl.ds(s * gw + h * HEAD_DIM, HEAD_DIM)
            c = yield from _chunk_local(qkv_ref[:, col(0)], qkv_ref[:, col(1)], qkv_ref[:, col(2)], bg_v[:, h:h + 1],
                                        gc_all[:, heads + h:heads + h + 1], gc_rows[heads + h:heads + h + 1, :])
            s0 = state[h]
            v_new = c.u - _bdot(c.w, s0)
            yield
            o = _bdot(c.qg, s0) + _bdot(c.A, v_new)
            return s0, o, s0 * c.egl + _bdot(c.kdec, v_new, ta=True)

        results = _in_lockstep([one_head(h) for h in range(heads)])
        for h, (s0, o, s1) in enumerate(results):
            s_ref[h, 0] = s0
            o_ref[:, pl.ds(h * HEAD_DIM, HEAD_DIM)] = o
            state[h] = s1

    return pl.pallas_call(
        body, name="gdn_core", grid=(nchunk,),
        in_specs=[pl.BlockSpec((CHUNK, 3 * gw), lambda n: (n, 0)), pl.BlockSpec((CHUNK, HEAD_DIM), lambda n: (n, 0))],
        out_specs=[pl.BlockSpec((CHUNK, gw), lambda n: (n, 0)),
                   pl.BlockSpec((heads, 1, HEAD_DIM, HEAD_DIM), lambda n: (0, n, 0, 0))],
        out_shape=[jax.ShapeDtypeStruct((t, gw), F32),
                   jax.ShapeDtypeStruct((heads, nchunk, HEAD_DIM, HEAD_DIM), F32)],
        scratch_shapes=[pltpu.VMEM((heads, HEAD_DIM, HEAD_DIM), F32)],
        compiler_params=_params("arbitrary"),
    )(qkv, bg)


def _gdn_core_backward(qkv, bg, states, do, heads):
    t = qkv.shape[0]
    nchunk = t // CHUNK
    n = CHUNK

    def one_head(chunk_local, s0, d_out, ds1):
        c = yield from chunk_local
        v_new = c.u - _bdot(c.w, s0)
        dqg = _bdot(d_out, s0, tb=True)
        ds0 = _bdot(c.qg, d_out, ta=True) + ds1 * c.egl
        dv_new = _bdot(c.A, d_out, ta=True) + _bdot(c.kdec, ds1)
        yield
        dA = jnp.where(c.tri, _bdot(d_out, v_new, tb=True), 0.0)
        dkdec = _bdot(v_new, ds1, tb=True)
        dgl = jnp.sum(jnp.sum(ds1 * s0, axis=1, keepdims=True), axis=0, keepdims=True) * c.egl
        dw = -_bdot(dv_new, s0, tb=True)
        ds0 = ds0 - _bdot(c.w, dv_new, ta=True)
        yield
        both = _dot22(c.T, jnp.concatenate([dv_new, dw], axis=1), ta=True)
        yield
        dvb, dkg = both[:, :HEAD_DIM], both[:, HEAD_DIM:]
        dL = jnp.where(c.strict, -(_bdot(dvb, c.u, tb=True) + _bdot(dkg, c.w, tb=True)), 0.0)
        yield
        dm1 = dL * c.decay
        dkb = _bdot(dm1, c.k) + dkg * c.eg
        dk = _bdot(dm1, c.kb, ta=True)
        dm2 = dA * c.decay
        dq = _bdot(dm2, c.k) + dqg * c.eg
        dk = dk + _bdot(dm2, c.q, ta=True) + dkdec * c.ekl + dkb * c.beta
        pm = dL * c.L + dA * c.A
        ones = jnp.ones((n, HEAD_DIM), BF16)
        pm_hi, pm_lo = _split2(pm)
        colsum = _bdot(pm_hi, ones, ta=True) + _bdot(pm_lo, ones, ta=True)
        tk_ = jnp.sum(dkdec * c.kdec, axis=1, keepdims=True)
        dgc = (jnp.sum(pm, axis=1, keepdims=True) - colsum
               + jnp.sum(dqg * c.qg, axis=1, keepdims=True)
               - tk_
               + jnp.sum(dkg * c.kg, axis=1, keepdims=True))
        dgl = dgl + jnp.sum(tk_, axis=0, keepdims=True)
        rowi = lax.broadcasted_iota(jnp.int32, (n, HEAD_DIM), 0)
        dgc = dgc + jnp.where(rowi == n - 1, dgl, 0.0)
        dbeta = jnp.sum(dkb * c.k, axis=1, keepdims=True) + jnp.sum(dvb * c.v, axis=1, keepdims=True)
        return dq * (HEAD_DIM ** -0.5), dk, dvb * c.beta, dbeta, dgc, ds0

    gw = heads * HEAD_DIM

    def body(qkv_ref, bg_ref, s_ref, do_ref, dqkv_ref, dbg_ref, dstate):
        @pl.when(pl.program_id(0) == 0)
        def _():
            dstate[...] = jnp.zeros_like(dstate)

        bg_v = bg_ref[...]
        gc_all, gc_rows = _chunk_gates(bg_v, heads)
        lane = lax.broadcasted_iota(jnp.int32, (n, HEAD_DIM), 1)
        dgates = jnp.zeros((n, HEAD_DIM), F32)
        chains = []
        for h in range(heads):
            col = lambda s: pl.ds(s * gw + h * HEAD_DIM, HEAD_DIM)
            c = _chunk_local(qkv_ref[:, col(0)], qkv_ref[:, col(1)], qkv_ref[:, col(2)], bg_v[:, h:h + 1],
                             gc_all[:, heads + h:heads + h + 1], gc_rows[heads + h:heads + h + 1, :])
            chains.append(one_head(c, s_ref[h, 0], do_ref[:, pl.ds(h * HEAD_DIM, HEAD_DIM)], dstate[h]))
        results = _in_lockstep(chains)
        for h, (dq, dk, dv, dbeta, dgc, ds0) in enumerate(results):
            dgates = jnp.where(lane == h, dbeta, jnp.where(lane == heads + h, dgc, dgates))
        for h, (dq, dk, dv, dbeta, dgc, ds0) in enumerate(results):
            dqkv_ref[:, pl.ds(h * HEAD_DIM, HEAD_DIM)] = dq
            dqkv_ref[:, pl.ds(gw + h * HEAD_DIM, HEAD_DIM)] = dk
            dqkv_ref[:, pl.ds(2 * gw + h * HEAD_DIM, HEAD_DIM)] = dv
            dstate[h] = ds0
        row = lax.broadcasted_iota(jnp.int32, (n, n), 0)
        colm = lax.broadcasted_iota(jnp.int32, (n, n), 1)
        draw = _dot_mask((row >= colm).astype(BF16), dgates, ta=True)
        dbg_ref[...] = jnp.where(lane < heads, dgates, draw)

    last = nchunk - 1
    return pl.pallas_call(
        body, name="gdn_core_bwd", grid=(nchunk,),
        in_specs=[pl.BlockSpec((CHUNK, 3 * gw), lambda i: (last - i, 0)),
                  pl.BlockSpec((CHUNK, HEAD_DIM), lambda i: (last - i, 0)),
                  pl.BlockSpec((heads, 1, HEAD_DIM, HEAD_DIM), lambda i: (0, last - i, 0, 0)),
                  pl.BlockSpec((CHUNK, gw), lambda i: (last - i, 0))],
        out_specs=[pl.BlockSpec((CHUNK, 3 * gw), lambda i: (last - i, 0)),
                   pl.BlockSpec((CHUNK, HEAD_DIM), lambda i: (last - i, 0))],
        out_shape=[jax.ShapeDtypeStruct((t, 3 * gw), F32), jax.ShapeDtypeStruct((t, HEAD_DIM), F32)],
        scratch_shapes=[pltpu.VMEM((heads, HEAD_DIM, HEAD_DIM), F32)],
        compiler_params=_params("arbitrary"),
    )(qkv, bg, states, do)


def _gdn_post(o, proj, z_col0, norm_w, heads, tt):
    t = o.shape[0]
    zb = z_col0 // HEAD_DIM

    def body(o_ref, z_ref, w_ref, out_ref):
        ov = o_ref[...]
        z = z_ref[...]
        rms = lax.rsqrt(jnp.mean(ov * ov, axis=-1, keepdims=True) + NORM_EPS)
        out_ref[...] = (ov * rms * w_ref[...] * (z * _sigmoid(z))).astype(BF16)

    return pl.pallas_call(
        body, name="gdn_post", grid=(t // tt, heads),
        in_specs=[pl.BlockSpec((tt, HEAD_DIM), lambda i, h: (i, h)),
                  pl.BlockSpec((tt, HEAD_DIM), lambda i, h: (i, zb + h)),
                  pl.BlockSpec((1, HEAD_DIM), lambda i, h: (0, 0))],
        out_specs=pl.BlockSpec((tt, HEAD_DIM), lambda i, h: (i, h)),
        out_shape=jax.ShapeDtypeStruct((t, heads * HEAD_DIM), BF16),
        compiler_params=_params("parallel", "parallel"),
    )(o, proj, norm_w)


def _gdn_post_backward(dcat, o, proj, z_col0, norm_w, heads, tt):
    t = o.shape[0]
    zb = z_col0 // HEAD_DIM

    def body(d_ref, o_ref, z_ref, w_ref, do_ref, dz_ref, dw_ref):
        d = d_ref[...]
        ov = o_ref[...]
        z = z_ref[...]
        w = w_ref[...]
        rms = lax.rsqrt(jnp.mean(ov * ov, axis=-1, keepdims=True) + NORM_EPS)
        ohat = ov * rms
        sg = _sigmoid(z)
        gate = z * sg
        dz_ref[...] = (d * ohat * w * (sg * (1.0 + z * (1.0 - sg)))).astype(BF16)
        don = d * gate
        dohat = don * w
        do_ref[...] = rms * (dohat - ohat * jnp.mean(dohat * ohat, axis=-1, keepdims=True))
        dw = jnp.sum(don * ohat, axis=0, keepdims=True)
        first = jnp.logical_and(pl.program_id(0) == 0, pl.program_id(1) == 0)

        @pl.when(first)
        def _():
            dw_ref[...] = dw

        @pl.when(jnp.logical_not(first))
        def _():
            dw_ref[...] += dw

    blk = pl.BlockSpec((tt, HEAD_DIM), lambda i, h: (i, h))
    return pl.pallas_call(
        body, name="gdn_post_bwd", grid=(t // tt, heads),
        in_specs=[blk, blk, pl.BlockSpec((tt, HEAD_DIM), lambda i, h: (i, zb + h)),
                  pl.BlockSpec((1, HEAD_DIM), lambda i, h: (0, 0))],
        out_specs=[blk, blk, pl.BlockSpec((1, HEAD_DIM), lambda i, h: (0, 0))],
        out_shape=[jax.ShapeDtypeStruct((t, heads * HEAD_DIM), F32),
                   jax.ShapeDtypeStruct((t, heads * HEAD_DIM), BF16),
                   jax.ShapeDtypeStruct((1, HEAD_DIM), F32)],
        compiler_params=_params("arbitrary", "arbitrary"),
    )(dcat, o, proj, norm_w)


def _pool_select(levels, group):
    out = levels[-1]
    for gi in range(len(levels) - 2, -1, -1):
        out = jnp.where(group == gi, levels[gi], out)
    return out


def _pool_counts(t, width, group):
    pos = lax.broadcasted_iota(jnp.int32, (t, width), 0)
    win = jnp.left_shift(2, group)
    return jnp.minimum(pos + 1, win).astype(F32)


def _pooled(p, group):
    levels, s, step = [], p, 1
    for _ in POOL_WINDOWS:
        s = s + _shift_down(s, step)
        levels.append(s)
        step *= 2
    cnt = _pool_counts(p.shape[0], p.shape[1], group)
    return _pool_select(levels, group) / cnt - p, cnt


def _pool_forward(proj, p_col0, pool_w, pool_scale):
    t = proj.shape[0]
    groups, cg, _ = pool_w.shape
    pb = p_col0 // cg

    def body(p_ref, w_ref, s_ref, o_ref):
        pooled, _ = _pooled(p_ref[...], pl.program_id(0))
        o_ref[...] = (_bdot(pooled, w_ref[0]) * s_ref[...]).astype(BF16)

    return pl.pallas_call(
        body, name="pool_fwd", grid=(groups,),
        in_specs=[pl.BlockSpec((t, cg), lambda g: (0, pb + g)), pl.BlockSpec((1, cg, cg), lambda g: (g, 0, 0)),
                  pl.BlockSpec((1, cg), lambda g: (0, g))],
        out_specs=pl.BlockSpec((t, cg), lambda g: (0, g)),
        out_shape=jax.ShapeDtypeStruct((t, groups * cg), BF16),
        compiler_params=_params("parallel"),
    )(proj, pool_w, pool_scale)


def _pool_backward(dcat, d_col0, proj, p_col0, pool_w, pool_scale):
    t = proj.shape[0]
    groups, cg, _ = pool_w.shape
    pb = p_col0 // cg
    db = d_col0 // cg

    def body(d_ref, p_ref, w_ref, s_ref, dp_ref, dw_ref, ds_ref):
        group = pl.program_id(0)
        pooled, cnt = _pooled(p_ref[...], group)
        w = w_ref[0]
        d = d_ref[...]
        mixed = _bdot(pooled, w)
        ds_ref[...] = jnp.sum(d * mixed, axis=0, keepdims=True)
        dmixed = d * s_ref[...]
        dw_ref[0] = _bdot(pooled, dmixed, ta=True)
        dpooled = _bdot(dmixed, w, tb=True)
        levels, s, step = [], dpooled / cnt, 1
        for _ in POOL_WINDOWS:
            s = s + _shift_up(s, step)
            levels.append(s)
            step *= 2
        dp_ref[...] = (_pool_select(levels, group) - dpooled).astype(BF16)

    return pl.pallas_call(
        body, name="pool_bwd", grid=(groups,),
        in_specs=[pl.BlockSpec((t, cg), lambda g: (0, db + g)), pl.BlockSpec((t, cg), lambda g: (0, pb + g)),
                  pl.BlockSpec((1, cg, cg), lambda g: (g, 0, 0)), pl.BlockSpec((1, cg), lambda g: (0, g))],
        out_specs=[pl.BlockSpec((t, cg), lambda g: (0, g)), pl.BlockSpec((1, cg, cg), lambda g: (g, 0, 0)),
                   pl.BlockSpec((1, cg), lambda g: (0, g))],
        out_shape=[jax.ShapeDtypeStruct((t, groups * cg), BF16), jax.ShapeDtypeStruct((groups, cg, cg), F32),
                   jax.ShapeDtypeStruct((1, groups * cg), F32)],
        compiler_params=_params("parallel"),
    )(dcat, proj, pool_w, pool_scale)


def _attention(q, k, v, tq):
    t, d = q.shape
    m = k.shape[0]
    dh = d // XATTN_HEADS
    scale = dh ** -0.5

    def body(q_ref, k_ref, v_ref, o_ref):
        s = _bdot(q_ref[...], k_ref[...], tb=True) * scale
        s = s - jnp.max(s, axis=-1, keepdims=True)
        e = jnp.exp(s)
        p = e / jnp.sum(e, axis=-1, keepdims=True)
        o_ref[...] = _bdot(p, v_ref[...]).astype(BF16)

    return pl.pallas_call(
        body, name="xattn_fwd", grid=(XATTN_HEADS, t // tq),
        in_specs=[pl.BlockSpec((tq, dh), lambda h, i: (i, h)), pl.BlockSpec((m, dh), lambda h, i: (0, h)),
                  pl.BlockSpec((m, dh), lambda h, i: (0, h))],
        out_specs=pl.BlockSpec((tq, dh), lambda h, i: (i, h)),
        out_shape=jax.ShapeDtypeStruct((t, d), BF16),
        compiler_params=_params("parallel", "parallel"),
    )(q, k, v)


def _attention_backward(q, k, v, do, tq):
    t, d = q.shape
    m = k.shape[0]
    dh = d // XATTN_HEADS
    scale = dh ** -0.5

    def body(q_ref, k_ref, v_ref, do_ref, dq_ref, dk_ref, dv_ref, dk_acc, dv_acc):
        i = pl.program_id(1)
        qv, kv, vv, dov = q_ref[...], k_ref[...], v_ref[...], do_ref[...]
        s = _bdot(qv, kv, tb=True) * scale
        s = s - jnp.max(s, axis=-1, keepdims=True)
        e = jnp.exp(s)
        p = e / jnp.sum(e, axis=-1, keepdims=True)
        dp = _bdot(dov, vv, tb=True)
        ds = p * (dp - jnp.sum(dp * p, axis=-1, keepdims=True)) * scale
        dq_ref[...] = _bdot(ds, kv).astype(BF16)
        dv_part = _bdot(p, dov, ta=True)
        dk_part = _bdot(ds, qv, ta=True)

        @pl.when(i == 0)
        def _():
            dk_acc[...] = dk_part
            dv_acc[...] = dv_part

        @pl.when(i > 0)
        def _():
            dk_acc[...] += dk_part
            dv_acc[...] += dv_part

        @pl.when(i == pl.num_programs(1) - 1)
        def _():
            dk_ref[...] = dk_acc[...].astype(BF16)
            dv_ref[...] = dv_acc[...].astype(BF16)

    qblk = pl.BlockSpec((tq, dh), lambda h, i: (i, h))
    kblk = pl.BlockSpec((m, dh), lambda h, i: (0, h))
    return pl.pallas_call(
        body, name="xattn_bwd", grid=(XATTN_HEADS, t // tq),
        in_specs=[qblk, kblk, kblk, qblk],
        out_specs=[qblk, kblk, kblk],
        out_shape=[jax.ShapeDtypeStruct((t, d), BF16), jax.ShapeDtypeStruct((m, d), BF16),
                   jax.ShapeDtypeStruct((m, d), BF16)],
        scratch_shapes=[pltpu.VMEM((m, dh), F32), pltpu.VMEM((m, dh), F32)],
        compiler_params=_params("parallel", "arbitrary"),
    )(q, k, v, do)


def _ln_backward_rows(name, dmain, dres, xhat, rstd, gamma, tm):
    t, d = xhat.shape

    def body(m_ref, r_ref, x_ref, s_ref, g_ref, du_ref, dub_ref, dg_ref, db_ref):
        du, dg, db = _ln_backward_math(m_ref[...] + ALPHA * r_ref[...], x_ref[...], s_ref[...], g_ref[...])
        du_ref[...] = du
        dub_ref[...] = du.astype(BF16)
        first = pl.program_id(0) == 0

        @pl.when(first)
        def _():
            dg_ref[...] = dg
            db_ref[...] = db

        @pl.when(jnp.logical_not(first))
        def _():
            dg_ref[...] += dg
            db_ref[...] += db

    row = pl.BlockSpec((tm, d), lambda i: (i, 0))
    vec = pl.BlockSpec((1, d), lambda i: (0, 0))
    return pl.pallas_call(
        body, name=name, grid=(t // tm,),
        in_specs=[row, row, row, pl.BlockSpec((tm, 1), lambda i: (i, 0)), vec],
        out_specs=[row, row, vec, vec],
        out_shape=[jax.ShapeDtypeStruct((t, d), F32), jax.ShapeDtypeStruct((t, d), BF16),
                   jax.ShapeDtypeStruct((1, d), F32), jax.ShapeDtypeStruct((1, d), F32)],
        compiler_params=_params("arbitrary"),
    )(dmain, dres, xhat, rstd, gamma)


def _loss_and_ln_backward(xhat, rstd, gamma, beta, target, tm):
    t, d = xhat.shape

    def body(x_ref, r_ref, g_ref, b_ref, t_ref, du_ref, dub_ref, dg_ref, db_ref, loss_ref):
        xh = x_ref[...]
        g = g_ref[...]
        diff = xh * g + b_ref[...] - t_ref[...]
        part = jnp.sum(jnp.sum(diff * diff, axis=1, keepdims=True), axis=0, keepdims=True) * (0.5 / d)
        dy = diff * (1.0 / d)
        du, dg, db = _ln_backward_math(dy, xh, r_ref[...], g)
        du_ref[...] = du
        dub_ref[...] = du.astype(BF16)
        lossrow = jnp.broadcast_to(part, (1, HEAD_DIM))
        first = pl.program_id(0) == 0

        @pl.when(first)
        def _():
            dg_ref[...] = dg
            db_ref[...] = db
            loss_ref[...] = lossrow

        @pl.when(jnp.logical_not(first))
        def _():
            dg_ref[...] += dg
            db_ref[...] += db
            loss_ref[...] += lossrow

    row = pl.BlockSpec((tm, d), lambda i: (i, 0))
    vec = pl.BlockSpec((1, d), lambda i: (0, 0))
    return pl.pallas_call(
        body, name="loss_ln3_bwd", grid=(t // tm,),
        in_specs=[row, pl.BlockSpec((tm, 1), lambda i: (i, 0)), vec, vec, row],
        out_specs=[row, row, vec, vec, pl.BlockSpec((1, HEAD_DIM), lambda i: (0, 0))],
        out_shape=[jax.ShapeDtypeStruct((t, d), F32), jax.ShapeDtypeStruct((t, d), BF16),
                   jax.ShapeDtypeStruct((1, d), F32), jax.ShapeDtypeStruct((1, d), F32),
                   jax.ShapeDtypeStruct((1, HEAD_DIM), F32)],
        compiler_params=_params("arbitrary"),
    )(xhat, rstd, gamma, beta, target)


def _after(token, a):
    return a if token is None else a + token[:1, :1].astype(a.dtype)


def _pick(n, prefs):
    for p in prefs:
        if n % p == 0:
            return p
    return n


def _local_step(x, mem, target, w):
    t, d = x.shape
    heads = w["a_log"].shape[1]
    gw = heads * HEAD_DIM
    groups, cg, _ = w["pool_w"].shape
    pw = groups * cg
    n_main = 4 * gw + pw
    in_cols = n_main + 2 * heads
    s_in = w["w_in_t"].shape[0]

    tm = _pick(t, (512, 256, 128))
    tm_ln = _pick(t, (256, 128))
    tm_big = _pick(t, (1024, 512, 256, 128))
    tk = _pick(d, K_STEPS)

    w_in_t = w["w_in_t"].reshape(in_cols, d)
    w_p_t = w_in_t[4 * gw + 2 * heads:]
    w_ba_t = jnp.pad(w_in_t[4 * gw:4 * gw + 2 * heads], ((0, HEAD_DIM - 2 * heads), (0, 0)))
    x_bf = x.astype(BF16)
    mem_bf = mem.astype(BF16)

    tn_d = _pick(d, (1024, 512, 256, 128))
    proj = _plain("proj_main", x_bf, w_in_t, tb=True, n_used=4 * gw, tm=tm_big, tn=_pick(4 * gw, (1024, 512, 256, 128)),
                  tk=tk, out_dtype=F32)
    pproj = _plain("proj_pool", x_bf, w_p_t, tb=True, tm=tm_big, tn=_pick(pw, (1024, 512, 256, 128)), tk=tk, out_dtype=F32)
    ea, dtb = _gate_vectors(w["a_log"], w["dt_bias"], heads)
    vec128 = lambda i, j: (0, 0)
    ba, bg = _matmul(
        "proj_gates", x_bf, w_ba_t, tb=True, tm=tm, tn=HEAD_DIM, tk=tk,
        extra=[(ea, (1, HEAD_DIM), vec128), (dtb, (1, HEAD_DIM), vec128)],
        outs=[(jax.ShapeDtypeStruct((t, HEAD_DIM), F32), (tm, HEAD_DIM), _tile)] * 2,
        epilogue=_gates_epilogue(heads))
    qkv = _gdn_pre(proj, w["conv_w"], heads)
    o_gdn, states = _gdn_core(qkv, bg, heads)
    cat_g = _gdn_post(o_gdn, proj, 3 * gw, w["gdn_norm_w"], heads, tm)
    cat_p = _pool_forward(pproj, 0, w["pool_w"], w["pool_scale"])
    cat = jnp.concatenate([cat_g, cat_p], axis=1)
    w = {**w, **(yield ("weights", 1, cat))}
    h1, h1_bf, xhat1, rstd1 = _ln_forward("mix_ln1", cat, w["w_out"], x, w["ln1_g"], w["ln1_b"], tm=tm_ln, tk=tk)

    q = _plain("xattn_q", h1_bf, w["xq_w"], tm=tm, tn=tn_d, tk=tk, out_dtype=BF16)
    mlen = mem.shape[0]
    tm_mem = _pick(mlen, (256, 128))
    k = _plain("xattn_k", mem_bf, w["xk_w"], tm=tm_mem, tn=tn_d, tk=tk, out_dtype=BF16)
    v = _plain("xattn_v", mem_bf, w["xv_w"], tm=tm_mem, tn=tn_d, tk=tk, out_dtype=BF16)
    att = _attention(q, k, v, tm)
    h2, h2_bf, xhat2, rstd2 = _ln_forward("xo_ln2", att, w["xo_w"], h1, w["ln2_g"], w["ln2_b"], tm=tm_ln, tk=tk)

    w = {**w, **(yield ("weights", 2, h2_bf))}
    s_up = w["w_up3"].shape[0]
    ff = s_up * w["w_up3"].shape[2]
    tn_f = _pick(ff // s_up, (1024, 512, 256, 128))

    def up_epi(acc, ex, out, i):
        r = jnp.maximum(acc, 0.0)
        out[0][...] = (r * r).astype(BF16)
        out[1][...] = (2.0 * r).astype(BF16)

    act, act_grad = _matmul(
        "mlp_up", h2_bf, w["w_up3"], b_blocks=s_up, tm=tm_big, tn=tn_f, tk=tk,
        outs=[(jax.ShapeDtypeStruct((t, ff), BF16), (tm_big, tn_f), _tile)] * 2, epilogue=up_epi)
    w = {**w, **(yield ("weights", 3, act))}
    tk_f = _pick(ff, K_STEPS)
    xhat3, rstd3 = _ln_forward("down_ln3", act, w["w_down"], h2, w["ln3_g"], w["ln3_b"], tm=tm, tk=tk_f, want_h=False)

    grads = {}
    du3, du3_bf, grads["ln3_g"], grads["ln3_b"], loss = _loss_and_ln_backward(
        xhat3, rstd3, w["ln3_g"], w["ln3_b"], target, tm_ln)

    def dup_epi(acc, ex, out, i):
        out[0][...] = (acc * ex[0][...].astype(F32)).astype(BF16)

    dup = _matmul(
        "mlp_down_dx", du3_bf, w["w_down"], tb=True, tm=tm_big, tn=tn_f, tk=tk,
        extra=[(act_grad, (tm_big, tn_f), _tile)],
        outs=[(jax.ShapeDtypeStruct((t, ff), BF16), (tm_big, tn_f), _tile)], epilogue=dup_epi)[0]
    tk_t = _pick(t, K_STEPS)
    tm_w = _pick(d, (512, 256, 128))
    grads["w_down"] = _plain("mlp_down_dw", act, du3_bf, ta=True, tm=_pick(ff, (512, 256, 128)), tn=d, tk=tk_t,
                             out_dtype=F32)
    grads["w_up3"] = _plain("mlp_up_dw", h2_bf, dup, ta=True, tm=tm_w, tn=ff // s_up, tk=tk_t, out_dtype=F32, out3=s_up,
                            n_outer=True)
    token = yield ("grads", 0, {n: grads.pop(n) for n in ("w_down", "w_up3")})
    dh2 = _plain("mlp_up_dx", dup, w["w_up3"], tb=True, b_blocks=s_up, tm=tm_big, tn=tn_d,
                 tk=_pick(ff // s_up, K_STEPS), out_dtype=F32)
    du2, du2_bf, grads["ln2_g"], grads["ln2_b"] = _ln_backward_rows(
        "ln2_bwd", dh2, du3, xhat2, rstd2, _after(token, w["ln2_g"]), tm_ln)
    token = yield ("poll", 0, du2_bf)

    grads["xo_w"] = _plain("xo_dw", att, du2_bf, ta=True, tm=tm_w, tn=d, tk=tk_t, out_dtype=F32)
    datt = _plain("xo_dx", du2_bf, w["xo_w"], tb=True, tm=tm, tn=tn_d, tk=tk, out_dtype=BF16)
    dq, dk, dv = _attention_backward(q, k, v, datt, tm)
    tk_m = _pick(mlen, (256, 128))
    grads["xq_w"] = _plain("xq_dw", h1_bf, dq, ta=True, tm=tm_w, tn=d, tk=tk_t, out_dtype=F32)
    grads["xk_w"] = _plain("xk_dw", mem_bf, dk, ta=True, tm=tm_w, tn=tn_d, tk=tk_m, out_dtype=F32)
    grads["xv_w"] = _plain("xv_dw", mem_bf, dv, ta=True, tm=tm_w, tn=tn_d, tk=tk_m, out_dtype=F32)
    du1, du1_bf, grads["ln1_g"], grads["ln1_b"] = _ln_backward(
        "xq_dx_ln1", dq, w["xq_w"], du2, xhat1, rstd1, _after(token, w["ln1_g"]), tm=tm_ln, tk=tk)

    grads["w_out"] = _plain("out_dw", cat, du1_bf, ta=True, tm=tm_w, tn=d, tk=tk_t, out_dtype=F32)
    token = yield ("grads", 1, {n: grads.pop(n) for n in ("xo_w", "xq_w", "xk_w", "xv_w", "w_out")})
    dcat = _plain("out_dx", du1_bf, w["w_out"], tb=True, tm=tm, tn=tn_d, tk=tk, out_dtype=F32)
    dp, grads["pool_w"], grads["pool_scale"] = _pool_backward(dcat, gw, pproj, 0, w["pool_w"],
                                                              _after(token, w["pool_scale"]))
    do_gdn, dz, grads["gdn_norm_w"] = _gdn_post_backward(dcat, o_gdn, proj, 3 * gw, _after(token, w["gdn_norm_w"]),
                                                         heads, tm)
    token = yield ("poll", 1, do_gdn)
    dqkv, dbg = _gdn_core_backward(qkv, _after(token, bg), states, do_gdn, heads)
    dqkv_pre, grads["conv_w"] = _gdn_pre_backward(proj, w["conv_w"], dqkv, heads)
    dba, dalog_row, ddt_row = _gates_backward(ba, bg, dbg, ea, dtb, heads)
    grads["a_log"] = dalog_row[:, heads:2 * heads]
    grads["dt_bias"] = ddt_row[:, heads:2 * heads]

    dproj = jnp.concatenate([dqkv_pre, dz, dp], axis=1)
    dw_main = _plain("proj_dw", dproj, x_bf, ta=True, tm=_pick(n_main, (512, 256, 128)), tn=d, tk=tk_t, out_dtype=F32)
    dw_ba = _plain("proj_gates_dw", dba, x_bf, ta=True, tm=HEAD_DIM, tn=tn_d, tk=tk_t, out_dtype=F32)
    dw_in_t = jnp.concatenate([dw_main[:4 * gw], dw_ba[:2 * heads], dw_main[4 * gw:]], axis=0)
    grads["w_in_t"] = dw_in_t.reshape(s_in, in_cols // s_in, d)

    def dx_epi(acc, ex, out, i):
        out[0][...] = acc + ex[1][...] + ALPHA * ex[0][...]

    def add_epi(acc, ex, out, i):
        out[0][...] = acc + ex[0][...]

    token = yield ("grads", 2, {n: grads.pop(n) for n in ("w_in_t", "pool_w")})
    dx_gates = _plain("proj_gates_dx", dba, _after(token, w_ba_t), tm=tm, tn=tn_d, tk=HEAD_DIM, out_dtype=F32)
    out_tile = [(jax.ShapeDtypeStruct((t, d), F32), (tm, tn_d), _tile)]
    dx_pool = _matmul("proj_pool_dx", dp, w_p_t, tm=tm, tn=tn_d, tk=_pick(pw, K_STEPS),
                      extra=[(dx_gates, (tm, tn_d), _tile)], outs=out_tile, epilogue=add_epi)[0]
    grad_x = _matmul(
        "proj_dx", dproj, w_in_t, k_used=4 * gw, tm=tm, tn=tn_d, tk=_pick(4 * gw, K_STEPS),
        extra=[(du1, (tm, tn_d), _tile), (dx_pool, (tm, tn_d), _tile)], outs=out_tile, epilogue=dx_epi)[0]
    yield ("poll", 2, grad_x)
    return loss, grad_x, grads


def _adamw(name, w, g, m, v):
    r, c = w.shape
    if r % 8 == 0:
        tr = _pick(r, (256, 128, 64, 32, 16, 8))
        blk, steps = pl.BlockSpec((tr, c), lambda i: (i, 0)), r // tr
    else:
        tc = _pick(c, (256, 128))
        blk, steps = pl.BlockSpec((r, tc), lambda i: (0, i)), c // tc
    c1 = 1.0 - ADAM_B1 ** ADAM_STEP
    c2 = 1.0 - ADAM_B2 ** ADAM_STEP

    def body(w_ref, g_ref, m_ref, v_ref, d_ref, mo_ref, vo_ref):
        gv = g_ref[...]
        mn = ADAM_B1 * m_ref[...] + (1.0 - ADAM_B1) * gv
        vn = ADAM_B2 * v_ref[...] + (1.0 - ADAM_B2) * (gv * gv)
        d_ref[...] = -ADAM_LR * ((mn / c1) / (jnp.sqrt(vn / c2) + ADAM_EPS) + ADAM_WD * w_ref[...])
        mo_ref[...] = mn
        vo_ref[...] = vn

    return pl.pallas_call(
        body, name=name, grid=(steps,), in_specs=[blk] * 4, out_specs=[blk] * 3,
        out_shape=[jax.ShapeDtypeStruct((r, c), F32)] * 3,
        compiler_params=_params("parallel"),
    )(w, g, m, v)


def _place():
    x, y, c = lax.axis_index("x"), lax.axis_index("y"), lax.axis_index("c")
    chips = [(1 - x, y), (x, 1 - y), (1 - x, 1 - y)]
    return x, y, c, chips


HBM = pl.BlockSpec(memory_space=pltpu.HBM)


SEM = pl.BlockSpec(memory_space=pltpu.SEMAPHORE)
ANY = pl.BlockSpec(memory_space=pl.ANY)
EFFECT = pltpu.SideEffectType.DATAFLOW_SIDE_EFFECTING


def _in_hbm(a):
    return pltpu.with_memory_space_constraint(a, pltpu.HBM)


def _remote(src, dst, send_sem, recv_sem, to):
    return pltpu.make_async_remote_copy(src_ref=src, dst_ref=dst, send_sem=send_sem, recv_sem=recv_sem,
                                        device_id=to, device_id_type=MESH)


def _by_rows(rows):
    return rows % 32 == 0


def _half_shape(rows, cols):
    return (rows // 2, cols) if _by_rows(rows) else (rows, cols // 2)


def _half(ref, which, *lead):
    rows, cols = ref.shape[-2:]
    if _by_rows(rows):
        return ref.at[(*lead, pl.ds(which * (rows // 2), rows // 2))]
    return ref.at[(*lead, slice(None), pl.ds(which * (cols // 2), cols // 2))]


def _landed(lands, i, shard_index, which):
    return _half(lands[i], which, shard_index)


def _gather_start(name, shards, after):
    n = len(shards)
    lands = [lax.empty((N_SHARD,) + s.shape, s.dtype) for s in shards]

    def body(*refs):
        ins, zones = refs[:n], refs[n:2 * n]
        ici_send, ici_recv, own_send, own_recv = refs[2 * n + 1:2 * n + 5]
        token = refs[-1]
        x, y, c, chips = _place()
        me = 2 * x + y
        for i in range(n):
            for j, chip in enumerate(chips):
                _remote(_half(ins[i], c), _landed(zones, i, me, c), ici_send.at[3 * i + j],
                        ici_recv.at[3 * i + j], (*chip, c)).start()
        for i in range(n):
            _remote(ins[i], zones[i].at[me], own_send.at[i], own_recv.at[i], (x, y, 1 - c)).start()
        token[...] = jnp.zeros_like(token)

    dma = pltpu.SemaphoreType.DMA
    outs = pl.pallas_call(
        body, name=name,
        in_specs=[HBM] * (2 * n) + [ANY],
        out_shape=(dma((3 * n,)), dma((3 * n,)), dma((n,)), dma((n,)),
                   *[pltpu.HBM(a.shape, a.dtype) for a in shards + lands], jax.ShapeDtypeStruct((8, LANES), F32)),
        out_specs=(SEM, SEM, SEM, SEM, *[HBM] * (2 * n), pl.BlockSpec(memory_space=pltpu.VMEM)),
        input_output_aliases={k: 4 + k for k in range(2 * n)},
        compiler_params=pltpu.CompilerParams(has_side_effects=EFFECT),
    )(*[_in_hbm(a) for a in shards + lands], after)
    sems = dict(zip(("ici_send", "ici_recv", "own_send", "own_recv"), outs[:4]))
    return sems, list(outs[4:4 + n]), list(outs[4 + n:4 + 2 * n]), outs[-1]


def _gather_forward(name, idx, lands, sems, after):
    n = len(idx)

    def body(*refs):
        zones = refs[:n]
        ici_recv = refs[n]
        fwd_send, fwd_recv = refs[n + 2], refs[n + 3]
        x, y, c, chips = _place()
        for k, i in enumerate(idx):
            for j, chip in enumerate(chips):
                half = _landed(zones, k, 2 * chip[0] + chip[1], c)
                _remote(half, half, fwd_send.at[3 * k + j], ici_recv.at[3 * i + j], (*chip, c)).wait_recv()
                _remote(half, half, fwd_send.at[3 * k + j], fwd_recv.at[3 * k + j], (x, y, 1 - c)).start()

    dma = pltpu.SemaphoreType.DMA
    outs = pl.pallas_call(
        body, name=name,
        in_specs=[HBM] * n + [SEM, ANY],
        out_shape=(dma((3 * n,)), dma((3 * n,)), *[pltpu.HBM(a.shape, a.dtype) for a in lands]),
        out_specs=(SEM, SEM, *[HBM] * n),
        input_output_aliases={k: 2 + k for k in range(n)},
        compiler_params=pltpu.CompilerParams(has_side_effects=EFFECT),
    )(*lands, sems["ici_recv"], after)
    return (outs[0], outs[1]), list(outs[2:])


def _gather_wait(name, idx, shards, lands, sems, fwd):
    n = len(idx)

    def body(*refs):
        ins, zones = refs[:n], refs[n:2 * n]
        ici_send, own_send, own_recv, fwd_send, fwd_recv = refs[2 * n:2 * n + 5]
        x, y, c, chips = _place()
        me = 2 * x + y
        for k, i in enumerate(idx):
            mine = _half(ins[k], c)
            for j, chip in enumerate(chips):
                theirs = 2 * chip[0] + chip[1]
                _remote(mine, _landed(zones, k, me, c), ici_send.at[3 * i + j], fwd_recv.at[3 * k + j],
                        (*chip, c)).wait_send()
                sent = _landed(zones, k, theirs, c)
                _remote(sent, sent, fwd_send.at[3 * k + j], fwd_recv.at[3 * k + j], (x, y, 1 - c)).wait_send()
                passed = _landed(zones, k, theirs, 1 - c)
                _remote(passed, passed, fwd_send.at[3 * k + j], fwd_recv.at[3 * k + j], (x, y, 1 - c)).wait_recv()
            own = _remote(ins[k], zones[k].at[me], own_send.at[i], own_recv.at[i], (x, y, 1 - c))
            own.wait_send()
            own.wait_recv()

    outs = pl.pallas_call(
        body, name=name,
        in_specs=[HBM] * (2 * n) + [SEM] * 5,
        out_shape=tuple(pltpu.HBM(a.shape, a.dtype) for a in lands),
        out_specs=tuple([HBM] * n),
        input_output_aliases={n + k: k for k in range(n)},
        compiler_params=pltpu.CompilerParams(has_side_effects=EFFECT),
    )(*shards, *lands, sems["ici_send"], sems["own_send"], sems["own_recv"], fwd[0], fwd[1])
    return list(outs)


def _all_reduce_small(name, slab, after=None):
    r, width = slab.shape
    ndev = 8

    def body(x_ref, after_ref, out_ref, buf, send_sems, recv_sems):
        x, y, c, _ = _place()
        me = 4 * x + 2 * y + c
        buf[me] = x_ref[...]
        copies = []
        for k in range(1, ndev):
            peer = jnp.bitwise_xor(me, k)
            to = (peer // 4, (peer // 2) % 2, peer % 2)
            cp = pltpu.make_async_remote_copy(src_ref=x_ref, dst_ref=buf.at[me], send_sem=send_sems.at[k - 1],
                                              recv_sem=recv_sems.at[k - 1], device_id=to, device_id_type=MESH)
            cp.start()
            copies.append(cp)
        for k in range(1, ndev):
            peer = jnp.bitwise_xor(me, k)
            pltpu.make_async_remote_copy(src_ref=x_ref, dst_ref=buf.at[peer], send_sem=send_sems.at[k - 1],
                                         recv_sem=recv_sems.at[k - 1], device_id=(x, y, c),
                                         device_id_type=MESH).wait_recv()
        for cp in copies:
            cp.wait_send()
        total = buf[0]
        for d in range(1, ndev):
            total = total + buf[d]
        out_ref[...] = total

    return pl.pallas_call(
        body, name=name,
        in_specs=[pl.BlockSpec(memory_space=pltpu.VMEM), ANY], out_specs=pl.BlockSpec(memory_space=pltpu.VMEM),
        out_shape=jax.ShapeDtypeStruct((r, width), F32),
        scratch_shapes=[pltpu.VMEM((ndev, r, width), F32), pltpu.SemaphoreType.DMA((ndev - 1,)),
                        pltpu.SemaphoreType.DMA((ndev - 1,))],
        compiler_params=pltpu.CompilerParams(vmem_limit_bytes=VMEM_LIMIT),
    )(slab, slab if after is None else after)


def _half_tiling(rows, cols):
    if _by_rows(rows):
        tr = _pick(rows // 2, (256, 128, 64, 32, 16))
        nb = (rows // 2) // tr
        return (tr, cols), nb, (lambda which, b: (which * nb + b, 0)), (lambda b: (b, 0))
    tc = _pick(cols // 2, (256, 128))
    nb = (cols // 2) // tc
    return (rows, tc), nb, (lambda which, b: (0, which * nb + b)), (lambda b: (0, b))


def _chip_partial(name, grad, other, core):
    s, r, cdim = grad.shape
    blk, nb, whole, within = _half_tiling(r, cdim)

    def body(core_ref, g_ref, o_ref, out_ref):
        out_ref[...] = (g_ref[...] + o_ref[...]).astype(BF16)

    return pl.pallas_call(
        body, name=name,
        grid_spec=pltpu.PrefetchScalarGridSpec(
            num_scalar_prefetch=1, grid=(s, nb),
            in_specs=[pl.BlockSpec((None,) + blk, lambda j, b, core_ref: (j,) + whole(core_ref[0], b)),
                      pl.BlockSpec((None,) + blk, lambda j, b, core_ref: (j,) + within(b))],
            out_specs=pl.BlockSpec((None,) + blk, lambda j, b, core_ref: (j,) + within(b))),
        out_shape=jax.ShapeDtypeStruct((s,) + _half_shape(r, cdim), BF16),
        compiler_params=_params("parallel", "parallel"),
    )(core, grad, other)


def _partial_copies(ins, zones, send_sems, recv_sems):
    x, y, c, chips = _place()
    return [_remote(ins[i].at[2 * chip[0] + chip[1]], zones[i].at[j], send_sems.at[3 * i + j],
                    recv_sems.at[3 * i + j], (*chip, c))
            for i in range(len(ins)) for j, chip in enumerate(chips)]


def _swap_copies(ins, zones, send_sems, recv_sems):
    x, y, c, _ = _place()
    copies = []
    for i in range(len(ins)):
        for s in range(N_SHARD):
            copies.append(_remote(_half(ins[i], 1 - c, s), zones[i].at[s],
                                  send_sems.at[N_SHARD * i + s], recv_sems.at[N_SHARD * i + s], (x, y, 1 - c)))
    return copies


def _exchange_start(name, plan, sources, lands, per_array):
    n = len(sources)
    lands = [lax.empty(shape, dtype) for shape, dtype in lands]

    def body(*refs):
        for cp in plan(refs[:n], refs[n:2 * n], refs[2 * n], refs[2 * n + 1]):
            cp.start()
        refs[-1][...] = jnp.zeros_like(refs[-1])

    dma = pltpu.SemaphoreType.DMA
    outs = pl.pallas_call(
        body, name=name,
        in_specs=[HBM] * (2 * n),
        out_shape=(dma((per_array * n,)), dma((per_array * n,)),
                   *[pltpu.HBM(a.shape, a.dtype) for a in list(sources) + lands], jax.ShapeDtypeStruct((8, LANES), F32)),
        out_specs=(SEM, SEM, *[HBM] * (2 * n), pl.BlockSpec(memory_space=pltpu.VMEM)),
        input_output_aliases={k: 2 + k for k in range(2 * n)},
        compiler_params=pltpu.CompilerParams(has_side_effects=EFFECT),
    )(*[_in_hbm(a) for a in list(sources) + lands])
    return (outs[0], outs[1]), list(outs[2:2 + n]), list(outs[2 + n:2 + 2 * n]), outs[-1]


def _exchange_wait(name, plan, started, after):
    sems, partials, lands, _ = started
    n = len(partials)

    def body(*refs):
        for cp in plan(refs[:n], refs[n:2 * n], refs[2 * n], refs[2 * n + 1]):
            cp.wait_send()
            cp.wait_recv()

    outs = pl.pallas_call(
        body, name=name,
        in_specs=[HBM] * (2 * n) + [SEM, SEM] + [ANY] * len(after),
        out_shape=tuple(pltpu.HBM(a.shape, a.dtype) for a in lands),
        out_specs=tuple([HBM] * n),
        input_output_aliases={n + k: k for k in range(n)},
        compiler_params=pltpu.CompilerParams(has_side_effects=EFFECT),
    )(*partials, *lands, sems[0], sems[1], *after)
    return list(outs)


def _reduce_own(name, grad, other, received, where):
    s, r, cdim = grad.shape
    blk, nb, whole, within = _half_tiling(r, cdim)

    def body(where_ref, g_ref, o_ref, r_ref, out_ref):
        total = g_ref[...] + o_ref[...]
        for j in range(3):
            total = total + r_ref[j].astype(F32)
        out_ref[...] = total

    return pl.pallas_call(
        body, name=name,
        grid_spec=pltpu.PrefetchScalarGridSpec(
            num_scalar_prefetch=1, grid=(nb,),
            in_specs=[pl.BlockSpec((None,) + blk, lambda b, w_ref: (w_ref[0],) + whole(w_ref[1], b)),
                      pl.BlockSpec((None,) + blk, lambda b, w_ref: (w_ref[0],) + within(b)),
                      pl.BlockSpec((3,) + blk, lambda b, w_ref: (0,) + within(b))],
            out_specs=pl.BlockSpec(blk, lambda b, w_ref: whole(w_ref[1], b))),
        out_shape=jax.ShapeDtypeStruct((r, cdim), F32),
        compiler_params=_params("parallel"),
    )(where, grad, other, received)


def _join_start(name, halves):
    n = len(halves)

    def body(*refs):
        bufs, send_sems, recv_sems = refs[:n], refs[n], refs[n + 1]
        x, y, c, _ = _place()
        for i in range(n):
            mine = _half(bufs[i], c)
            _remote(mine, mine, send_sems.at[i], recv_sems.at[i], (x, y, 1 - c)).start()
        refs[-1][...] = jnp.zeros_like(refs[-1])

    dma = pltpu.SemaphoreType.DMA
    outs = pl.pallas_call(
        body, name=name,
        in_specs=[HBM] * n,
        out_shape=(dma((n,)), dma((n,)), *[pltpu.HBM(h.shape, F32) for h in halves], jax.ShapeDtypeStruct((8, LANES), F32)),
        out_specs=(SEM, SEM, *[HBM] * n, pl.BlockSpec(memory_space=pltpu.VMEM)),
        input_output_aliases={k: 2 + k for k in range(n)},
        compiler_params=pltpu.CompilerParams(has_side_effects=EFFECT),
    )(*[_in_hbm(h) for h in halves])
    return (outs[0], outs[1]), list(outs[2:2 + n]), outs[-1]


def _join_wait(name, started, after):
    sems, bufs, _ = started
    n = len(bufs)

    def body(*refs):
        bufs, send_sems, recv_sems = refs[:n], refs[n], refs[n + 1]
        x, y, c, _ = _place()
        for i in range(n):
            mine, theirs = _half(bufs[i], c), _half(bufs[i], 1 - c)
            _remote(mine, mine, send_sems.at[i], recv_sems.at[i], (x, y, 1 - c)).wait_send()
            _remote(theirs, theirs, send_sems.at[i], recv_sems.at[i], (x, y, 1 - c)).wait_recv()

    outs = pl.pallas_call(
        body, name=name,
        in_specs=[HBM] * n + [SEM, SEM] + [ANY] * len(after),
        out_shape=tuple(pltpu.HBM(b.shape, F32) for b in bufs),
        out_specs=tuple([HBM] * n),
        input_output_aliases={k: k for k in range(n)},
        compiler_params=pltpu.CompilerParams(has_side_effects=EFFECT),
    )(*bufs, sems[0], sems[1], *after)
    return list(outs)


BIG = ("w_in", "pool_w", "w_out", "xq_w", "xk_w", "xv_w", "xo_w", "w_up", "w_down")
GATHER_GROUPS = ((0, 1), (2, 3, 4, 5, 6), (7,), (8,))
SMALL = ("conv_w", "a_log", "dt_bias", "gdn_norm_w", "pool_scale", "ln1_g", "ln1_b", "ln2_g", "ln2_b", "ln3_g", "ln3_b")
ORDER = ("w_in", "conv_w", "a_log", "dt_bias", "gdn_norm_w", "pool_w", "pool_scale", "w_out", "ln1_g", "ln1_b",
         "xq_w", "xk_w", "xv_w", "xo_w", "ln2_g", "ln2_b", "w_up", "w_down", "ln3_g", "ln3_b")
LANES = 128


def _rows(flat_len):
    return -(-flat_len // LANES)


def _pack(pieces):
    out = []
    for p in pieces:
        flat = p.reshape(-1).astype(F32)
        out.append(jnp.pad(flat, (0, _rows(flat.shape[0]) * LANES - flat.shape[0])).reshape(-1, LANES))
    slab = jnp.concatenate(out, axis=0)
    return jnp.pad(slab, ((0, -slab.shape[0] % 8), (0, 0)))


def _unpack(slab, shapes):
    out, row = [], 0
    for shp in shapes:
        size = math.prod(shp)
        out.append(slab[row:row + _rows(size)].reshape(-1)[:size].reshape(shp))
        row += _rows(size)
    return out


TRANSPOSED = ("w_in",)


def _as2d(name, a):
    a = a[0]
    if name in TRANSPOSED:
        return jnp.swapaxes(a, 0, 1)
    return a.reshape(-1, a.shape[-1]) if a.ndim == 3 else a


def _from2d(name, a, shape):
    return (jnp.swapaxes(a, 0, 1) if name in TRANSPOSED else a).reshape(shape)


def kernel(x, mem, w_in, conv_w, a_log, dt_bias, gdn_norm_w, pool_w, pool_scale, w_out, ln1_g, ln1_b, xq_w, xk_w, xv_w, xo_w, ln2_g, ln2_b, w_up, w_down, ln3_g, ln3_b, loss_target, m_w_in, m_conv_w, m_a_log, m_dt_bias, m_gdn_norm_w, m_pool_w, m_pool_scale, m_w_out, m_ln1_g, m_ln1_b, m_xq_w, m_xk_w, m_xv_w, m_xo_w, m_ln2_g, m_ln2_b, m_w_up, m_w_down, m_ln3_g, m_ln3_b, v_w_in, v_conv_w, v_a_log, v_dt_bias, v_gdn_norm_w, v_pool_w, v_pool_scale, v_w_out, v_ln1_g, v_ln1_b, v_xq_w, v_xk_w, v_xv_w, v_xo_w, v_ln2_g, v_ln2_b, v_w_up, v_w_down, v_ln3_g, v_ln3_b):
    given = dict(locals())
    cx, cy, cc = lax.axis_index("x"), lax.axis_index("y"), lax.axis_index("c")
    me = 2 * cx + cy
    groups = pool_w.shape[1]
    cs = pool_w.shape[2]
    kk, conv_cols = conv_w.shape[1], conv_w.shape[2]
    core = cc.astype(jnp.int32).reshape(1)
    where = jnp.stack([me, cc]).astype(jnp.int32)

    conv_slab = jnp.zeros((kk, N_SHARD * conv_cols), F32)
    conv_slab = lax.dynamic_update_slice(conv_slab, conv_w[0] * (cc == 0).astype(F32), (0, me * conv_cols))
    wts = {"conv_w": _unpack(_all_reduce_small("gather_conv_w", _pack([conv_slab])), [conv_slab.shape])[0]}

    started = {}

    def start(name, idx, after):
        sems, shards, lands, token = _gather_start(name, [_as2d(BIG[i], given[BIG[i]]).astype(BF16) for i in idx], after)
        for k, i in enumerate(idx):
            started[i] = (sems, k, shards[k], lands[k])
        return token

    token = start("gather_start_first", GATHER_GROUPS[0], wts["conv_w"])
    token = start("gather_start_rest", tuple(i for group in GATHER_GROUPS[1:] for i in group), token)

    def fetch(group, after):
        members = [started[i] for i in GATHER_GROUPS[group]]
        sems, idx = members[0][0], [m[1] for m in members]
        fwd, zones = _gather_forward(f"gather_forward_{group}", idx, [m[3] for m in members], sems, after)
        full = dict(zip([BIG[i] for i in GATHER_GROUPS[group]],
                        _gather_wait(f"gather_wait_{group}", idx, [m[2] for m in members], zones, sems, fwd)))
        out = {}
        for n, a in full.items():
            if n == "w_in":
                out["w_in_t"] = a
            elif n == "w_up":
                out["w_up3"] = a
            elif n == "pool_w":
                out[n] = a.reshape(N_SHARD, groups, cs, -1).transpose(1, 0, 2, 3).reshape(groups, N_SHARD * cs, -1)
            else:
                out[n] = a.reshape(-1, a.shape[-1])
        return out

    for n in ("a_log", "dt_bias", "gdn_norm_w", "pool_scale", "ln1_g", "ln1_b", "ln2_g", "ln2_b", "ln3_g", "ln3_b"):
        wts[n] = given[n]
    wts.update(fetch(0, token))

    def start_swap(group, grads):
        names, blocks = [], []
        for n, g in grads.items():
            if n == "pool_w":
                g = g.reshape(groups, N_SHARD, cs, -1).transpose(1, 0, 2, 3).reshape(N_SHARD, groups * cs, -1)
            elif g.ndim == 2:
                g = g.reshape(N_SHARD, -1, g.shape[-1])
            names.append({"w_in_t": "w_in", "w_up3": "w_up"}.get(n, n))
            blocks.append(g)
        zones = [((N_SHARD,) + _half_shape(b.shape[1], b.shape[2]), F32) for b in blocks]
        swap = _exchange_start(f"grad_swap_start_{group}", _swap_copies, blocks, zones, N_SHARD)
        return {"group": group, "names": names, "swap": swap, "token": swap[3]}

    def start_send(state, after):
        group, names = state["group"], state["names"]
        state["blocks"] = state["swap"][1]
        state["others"] = _exchange_wait(f"grad_swap_wait_{group}", _swap_copies, state["swap"], after)
        partials = [_chip_partial("chip_partial_" + n, gb, ob, core)
                    for n, gb, ob in zip(names, state["blocks"], state["others"])]
        zones = [((3,) + p.shape[1:], BF16) for p in partials]
        state["send"] = _exchange_start(f"grad_send_start_{group}", _partial_copies, partials, zones, 3)
        state["token"] = state["send"][3]

    grad, delta, new_m, new_v = {}, {}, {}, {}

    def start_join(state, after):
        group, names = state["group"], state["names"]
        received = _exchange_wait(f"grad_send_wait_{group}", _partial_copies, state["send"], after)
        halves = [_reduce_own("reduce_own_" + n, gb, ob, rb, where)
                  for n, gb, ob, rb in zip(names, state["blocks"], state["others"], received)]
        state["join"] = _join_start(f"grad_join_start_{group}", halves)
        return state["join"][2]

    def finish_reduce(state, after):
        group, names = state["group"], state["names"]
        for n, g in zip(names, _join_wait(f"grad_join_wait_{group}", state["join"], after)):
            shp = given[n].shape
            d2, m2, v2 = _adamw("adamw_" + n, _as2d(n, given[n]), g, _as2d(n, given["m_" + n]), _as2d(n, given["v_" + n]))
            grad[n], delta[n], new_m[n], new_v[n] = (_from2d(n, a, shp) for a in (g, d2, m2, v2))
        return d2

    step = _local_step(x[0], mem[0], loss_target[0], wts)
    pending = {}
    request = next(step)
    while True:
        try:
            kind, group, payload = request
            if kind == "weights":
                request = step.send(fetch(group, payload))
            elif kind == "grads":
                pending[group] = start_swap(group, payload)
                request = step.send(pending[group]["token"])
            else:
                start_send(pending[group], [payload])
                request = step.send(pending[group]["token"])
        except StopIteration as stop:
            loss_row, grad_x, g = stop.value
            break

    after = [pending[2]["token"], grad_x]
    for group in (0, 1):
        after = [start_join(pending[group], after)]
    for group in (0, 1):
        after = [finish_reduce(pending[group], after)]
    after = [finish_reduce(pending[2], [start_join(pending[2], after)])]

    small_names = ("a_log", "dt_bias", "gdn_norm_w", "pool_scale", "ln1_g", "ln1_b", "ln2_g", "ln2_b", "ln3_g", "ln3_b")
    pieces = [g["conv_w"]] + [g[n] for n in small_names] + [loss_row[:, :1]]
    shapes = [p.shape for p in pieces]
    summed = _unpack(_all_reduce_small("all_reduce_small", _pack(pieces), after[0]), shapes)
    gsmall = dict(zip(small_names, summed[1:-1]))
    gsmall["conv_w"] = lax.dynamic_slice(summed[0], (0, me * conv_cols), (kk, conv_cols))
    loss = summed[-1][0, 0]

    sshapes = [given[n].shape for n in SMALL]
    slabs = [_pack([given[p + n] for n in SMALL]) for p in ("", "m_", "v_")]
    gslab = _pack([gsmall[n] for n in SMALL])
    outs = _adamw("adamw_small", slabs[0], gslab, slabs[1], slabs[2])
    for dst, slab in zip((delta, new_m, new_v), outs):
        dst.update(zip(SMALL, _unpack(slab, sshapes)))
    for n in SMALL:
        grad[n] = gsmall[n].reshape(given[n].shape)

    return (loss, grad_x[None], *[grad[n] for n in ORDER], *[delta[n] for n in ORDER],
            *[new_m[n] for n in ORDER], *[new_v[n] for n in ORDER])
```

```python
import functools
import math

import jax
import jax.numpy as jnp
from jax import lax
from jax.experimental import pallas as pl
from jax.experimental.pallas import tpu as pltpu

F32 = jnp.float32
BF16 = jnp.bfloat16
MESH = pl.DeviceIdType.MESH

HEAD_DIM = 128
CHUNK = 64
POOL_WINDOWS = (2, 4, 8, 16)
XATTN_HEADS = 4
ALPHA = 2.0 ** 0.25
LN_EPS = 1e-5
NORM_EPS = 1e-6
ADAM_LR, ADAM_B1, ADAM_B2, ADAM_EPS, ADAM_WD, ADAM_STEP = 0.001, 0.9, 0.999, 1e-08, 0.01, 10
N_SHARD = 4
VMEM_LIMIT = 56 * 1024 * 1024
K_STEPS = (2048, 1024, 512, 256, 128)


def _params(*sem):
    return pltpu.CompilerParams(dimension_semantics=sem, vmem_limit_bytes=VMEM_LIMIT)


def _bdot(a, b, ta=False, tb=False):
    dims = (((0 if ta else 1,), (1 if tb else 0,)), ((), ()))
    return lax.dot_general(a.astype(BF16), b.astype(BF16), dims, preferred_element_type=F32)


def _sigmoid(x):
    return 1.0 / (1.0 + jnp.exp(-x))


def _matmul(name, a, b, *, ta=False, tb=False, tm, tn, tk, extra=(), outs, epilogue, b_blocks=None,
            sequential=False, n_used=None, k_used=None, n_outer=False):
    m, k_dim = (a.shape[1], a.shape[0]) if ta else a.shape
    if b_blocks and tb:
        n = b.shape[1]
        k_dim = b.shape[0] * b.shape[2]
        per = b.shape[2] // tk
        b_spec = pl.BlockSpec((None, tn, tk), lambda i, j, k: (k // per, j, k % per))
    elif b_blocks:
        n = b.shape[0] * b.shape[2]
        per = b.shape[2] // tn
        b_spec = pl.BlockSpec((None, tk, tn), lambda i, j, k: (j // per, k, j % per))
    elif tb:
        n = b.shape[0]
        b_spec = pl.BlockSpec((tn, tk), lambda i, j, k: (j, k))
    else:
        n = b.shape[1]
        b_spec = pl.BlockSpec((tk, tn), lambda i, j, k: (k, j))
    n, k_dim = n_used or n, k_used or k_dim
    assert m % tm == 0 and n % tn == 0 and k_dim % tk == 0, (name, m, n, k_dim, tm, tn, tk)
    nk = k_dim // tk
    a_spec = pl.BlockSpec((tk, tm), lambda i, j, k: (k, i)) if ta else pl.BlockSpec((tm, tk), lambda i, j, k: (i, k))
    n_extra, n_out = len(extra), len(outs)

    def wrap(index_map):
        return lambda i, j, k: index_map(i, j)

    def spec(block, index_map):
        if n_outer:
            return pl.BlockSpec(block, lambda j, i, k: index_map(i, j, k))
        return pl.BlockSpec(block, index_map)

    row_axis = 1 if n_outer else 0

    def body_one_step(*refs):
        ex = refs[2:2 + n_extra]
        out = refs[2 + n_extra:2 + n_extra + n_out]
        epilogue(_bdot(refs[0][...], refs[1][...], ta, tb), ex, out, pl.program_id(row_axis))

    def body(*refs):
        a_ref, b_ref = refs[0], refs[1]
        ex = refs[2:2 + n_extra]
        out = refs[2 + n_extra:2 + n_extra + n_out]
        acc = refs[-1]
        i, k = pl.program_id(row_axis), pl.program_id(2)
        part = _bdot(a_ref[...], b_ref[...], ta, tb)

        @pl.when(k == 0)
        def _():
            acc[...] = part

        @pl.when(jnp.logical_and(k > 0, k < nk - 1))
        def _():
            acc[...] += part

        @pl.when(k == nk - 1)
        def _():
            epilogue(acc[...] + part, ex, out, i)

    sem = ("arbitrary",) * 3 if sequential else ("parallel", "parallel", "arbitrary")
    res = pl.pallas_call(
        body_one_step if nk == 1 else body, name=name,
        grid=(n // tn, m // tm, nk) if n_outer else (m // tm, n // tn, nk),
        in_specs=[spec(a_spec.block_shape, a_spec.index_map), spec(b_spec.block_shape, b_spec.index_map)]
        + [spec(bs, wrap(im)) for _, bs, im in extra],
        out_specs=[spec(bs, wrap(im)) for _, bs, im in outs],
        out_shape=[s for s, _, _ in outs],
        scratch_shapes=[] if nk == 1 else [pltpu.VMEM((tm, tn), F32)],
        compiler_params=_params(*sem),
    )(a, b, *[x for x, _, _ in extra])
    return res


def _tile(i, j):
    return (i, j)


def _plain(name, a, b, *, ta=False, tb=False, tm, tn, tk, out_dtype, b_blocks=None, out3=None, n_used=None,
           n_outer=False):
    m = a.shape[1] if ta else a.shape[0]
    if b_blocks:
        n = b.shape[1] if tb else b.shape[0] * b.shape[2]
    else:
        n = n_used or (b.shape[0] if tb else b.shape[1])

    def epi(acc, ex, out, i):
        out[0][...] = acc.astype(out_dtype)

    if out3:
        per = (n // out3) // tn
        spec = (jax.ShapeDtypeStruct((out3, m, n // out3), out_dtype), (None, tm, tn),
                lambda i, j: (j // per, i, j % per))
    else:
        spec = (jax.ShapeDtypeStruct((m, n), out_dtype), (tm, tn), _tile)
    return _matmul(name, a, b, ta=ta, tb=tb, tm=tm, tn=tn, tk=tk, outs=[spec], epilogue=epi,
                   b_blocks=b_blocks, n_used=n_used, n_outer=n_outer)[0]


def _ln_forward(name, a, b, res, gamma, beta, *, tm, tk, want_h=True):
    m, n = res.shape

    def epi(acc, ex, out, i):
        u = ALPHA * ex[0][...] + acc
        mu = jnp.mean(u, axis=-1, keepdims=True)
        xc = u - mu
        var = jnp.mean(xc * xc, axis=-1, keepdims=True)
        rstd = lax.rsqrt(var + LN_EPS)
        xhat = xc * rstd
        out[-2][...] = xhat
        out[-1][...] = rstd
        if want_h:
            h = xhat * ex[1][...] + ex[2][...]
            out[0][...] = h
            out[1][...] = h.astype(BF16)

    row = lambda i, j: (i, 0)
    vec = lambda i, j: (0, 0)
    outs = [(jax.ShapeDtypeStruct((m, n), F32), (tm, n), row), (jax.ShapeDtypeStruct((m, n), BF16), (tm, n), row),
            (jax.ShapeDtypeStruct((m, n), F32), (tm, n), row), (jax.ShapeDtypeStruct((m, 1), F32), (tm, 1), row)]
    return _matmul(
        name, a, b, tm=tm, tn=n, tk=tk,
        extra=[(res, (tm, n), row), (gamma, (1, n), vec), (beta, (1, n), vec)],
        outs=outs if want_h else outs[2:], epilogue=epi)


def _ln_backward_math(dy, xhat, rstd, gamma):
    dxhat = dy * gamma
    m1 = jnp.mean(dxhat, axis=-1, keepdims=True)
    m2 = jnp.mean(dxhat * xhat, axis=-1, keepdims=True)
    du = rstd * (dxhat - m1 - xhat * m2)
    return du, jnp.sum(dy * xhat, axis=0, keepdims=True), jnp.sum(dy, axis=0, keepdims=True)


def _ln_backward(name, a, b, dres, xhat, rstd, gamma, *, tm, tk, b_blocks=None, tb=True):
    m, n = dres.shape

    def epi(acc, ex, out, i):
        dy = acc + ALPHA * ex[0][...]
        du, dg, db = _ln_backward_math(dy, ex[1][...], ex[2][...], ex[3][...])
        out[0][...] = du
        out[1][...] = du.astype(BF16)
        first = i == 0

        @pl.when(first)
        def _():
            out[2][...] = dg
            out[3][...] = db

        @pl.when(jnp.logical_not(first))
        def _():
            out[2][...] += dg
            out[3][...] += db

    row = lambda i, j: (i, 0)
    vec = lambda i, j: (0, 0)
    return _matmul(
        name, a, b, tb=tb, tm=tm, tn=n, tk=tk, b_blocks=b_blocks, sequential=True,
        extra=[(dres, (tm, n), row), (xhat, (tm, n), row), (rstd, (tm, 1), row), (gamma, (1, n), vec)],
        outs=[(jax.ShapeDtypeStruct((m, n), F32), (tm, n), row),
              (jax.ShapeDtypeStruct((m, n), BF16), (tm, n), row),
              (jax.ShapeDtypeStruct((1, n), F32), (1, n), vec),
              (jax.ShapeDtypeStruct((1, n), F32), (1, n), vec)],
        epilogue=epi)


def _shift_down(x, k):
    row = lax.broadcasted_iota(jnp.int32, x.shape, 0)
    return jnp.where(row >= k, pltpu.roll(x, k, axis=0), 0.0)


def _shift_up(x, k):
    t = x.shape[0]
    row = lax.broadcasted_iota(jnp.int32, x.shape, 0)
    return jnp.where(row < t - k, pltpu.roll(x, t - k, axis=0), 0.0)


def _conv_silu_norm(x, w, normalise):
    kk = w.shape[0]
    c = x * w[kk - 1:kk, :]
    for j in range(kk - 1):
        c = c + _shift_down(x, kk - 1 - j) * w[j:j + 1, :]
    sg = _sigmoid(c)
    s = c * sg
    r = lax.rsqrt(jnp.sum(s * s, axis=-1, keepdims=True) + NORM_EPS)
    y = jnp.where(normalise, s * r, s)
    return c, sg, s, r, y


def _gdn_pre(proj, conv_w, heads):
    t = proj.shape[0]
    kk = conv_w.shape[0]

    def body(x_ref, w_ref, o_ref):
        normalise = pl.program_id(0) < 2
        o_ref[...] = _conv_silu_norm(x_ref[...], w_ref[...], normalise)[4]

    col = lambda s, h: (0, s * heads + h)
    return pl.pallas_call(
        body, name="gdn_pre", grid=(3, heads),
        in_specs=[pl.BlockSpec((t, HEAD_DIM), col), pl.BlockSpec((kk, HEAD_DIM), col)],
        out_specs=pl.BlockSpec((t, HEAD_DIM), col),
        out_shape=jax.ShapeDtypeStruct((t, 3 * heads * HEAD_DIM), F32),
        compiler_params=_params("parallel", "parallel"),
    )(proj, conv_w)


def _gdn_pre_backward(proj, conv_w, dqkv, heads):
    t = proj.shape[0]
    kk = conv_w.shape[0]

    def body(x_ref, w_ref, dy_ref, dx_ref, dw_ref):
        normalise = pl.program_id(0) < 2
        x = x_ref[...]
        w = w_ref[...]
        dy = dy_ref[...]
        c, sg, s, r, y = _conv_silu_norm(x, w, normalise)
        ds_norm = r * (dy - y * jnp.sum(dy * y, axis=-1, keepdims=True))
        ds = jnp.where(normalise, ds_norm, dy)
        dc = ds * (sg * (1.0 + c * (1.0 - sg)))
        dx = dc * w[kk - 1:kk, :]
        rows = [None] * kk
        rows[kk - 1] = jnp.sum(dc * x, axis=0, keepdims=True)
        for j in range(kk - 1):
            lag = kk - 1 - j
            dx = dx + _shift_up(dc, lag) * w[j:j + 1, :]
            rows[j] = jnp.sum(dc * _shift_down(x, lag), axis=0, keepdims=True)
        dx_ref[...] = dx.astype(BF16)
        dw_ref[...] = jnp.concatenate(rows, axis=0)

    col = lambda s, h: (0, s * heads + h)
    return pl.pallas_call(
        body, name="gdn_pre_bwd", grid=(3, heads),
        in_specs=[pl.BlockSpec((t, HEAD_DIM), col), pl.BlockSpec((kk, HEAD_DIM), col),
                  pl.BlockSpec((t, HEAD_DIM), col)],
        out_specs=[pl.BlockSpec((t, HEAD_DIM), col), pl.BlockSpec((kk, HEAD_DIM), col)],
        out_shape=[jax.ShapeDtypeStruct((t, 3 * heads * HEAD_DIM), BF16),
                   jax.ShapeDtypeStruct((kk, 3 * heads * HEAD_DIM), F32)],
        compiler_params=_params("parallel", "parallel"),
    )(proj, conv_w, dqkv)


def _gate_vectors(a_log, dt_bias, heads):
    pad = lambda v: jnp.pad(v.astype(F32), ((0, 0), (heads, HEAD_DIM - 2 * heads)))
    return pad(jnp.exp(a_log.astype(F32))), pad(dt_bias)


def _softplus(x):
    return jnp.maximum(x, 0.0) + jnp.log(1.0 + jnp.exp(-jnp.abs(x)))


def _gates_epilogue(heads):
    def epi(acc, ex, out, i):
        lane = lax.broadcasted_iota(jnp.int32, acc.shape, 1)
        beta = _sigmoid(acc)
        g = -ex[0][...] * _softplus(acc + ex[1][...])
        out[0][...] = acc
        out[1][...] = jnp.where(lane < heads, beta, jnp.where(lane < 2 * heads, g, 0.0))
    return epi


def _gates_backward(ba, bg, dbg, ea, dtb, heads):
    t = ba.shape[0]

    def body(ba_ref, bg_ref, d_ref, ea_ref, dt_ref, dba_ref, dal_ref, ddt_ref):
        lane = lax.broadcasted_iota(jnp.int32, (t, HEAD_DIM), 1)
        bgv = bg_ref[...]
        d = d_ref[...]
        db = d * bgv * (1.0 - bgv)
        da = -d * ea_ref[...] * _sigmoid(ba_ref[...] + dt_ref[...])
        is_g = jnp.logical_and(lane >= heads, lane < 2 * heads)
        dba = jnp.where(lane < heads, db, jnp.where(is_g, da, 0.0))
        dba_ref[...] = dba.astype(BF16)
        dal_ref[...] = jnp.sum(jnp.where(is_g, d * bgv, 0.0), axis=0, keepdims=True)
        ddt_ref[...] = jnp.sum(jnp.where(is_g, da, 0.0), axis=0, keepdims=True)

    full = pl.BlockSpec((t, HEAD_DIM), lambda: (0, 0))
    vec = pl.BlockSpec((1, HEAD_DIM), lambda: (0, 0))
    return pl.pallas_call(
        body, name="gates_bwd", grid=(),
        in_specs=[full, full, full, vec, vec], out_specs=[full, vec, vec],
        out_shape=[jax.ShapeDtypeStruct((t, HEAD_DIM), BF16), jax.ShapeDtypeStruct((1, HEAD_DIM), F32),
                   jax.ShapeDtypeStruct((1, HEAD_DIM), F32)],
        compiler_params=pltpu.CompilerParams(vmem_limit_bytes=VMEM_LIMIT),
    )(ba, bg, dbg, ea, dtb)


class _Chunk:
    pass


def _split2(x):
    hi = x.astype(BF16)
    return hi, (x - hi.astype(F32)).astype(BF16)


def _split3(x):
    hi = x.astype(BF16)
    rest = x - hi.astype(F32)
    mid = rest.astype(BF16)
    return hi, mid, (rest - mid.astype(F32)).astype(BF16)


def _dot_mask(mask, x, ta=False):
    hi, mid, lo = _split3(x)
    return _bdot(mask, hi, ta=ta) + (_bdot(mask, mid, ta=ta) + _bdot(mask, lo, ta=ta))


def _transpose_by_identity(x):
    r = x.shape[0]
    eye = (lax.broadcasted_iota(jnp.int32, (r, r), 0) == lax.broadcasted_iota(jnp.int32, (r, r), 1)).astype(BF16)
    hi, mid, lo = _split3(x)
    return _bdot(hi, eye, ta=True) + (_bdot(mid, eye, ta=True) + _bdot(lo, eye, ta=True))


def _dot22(a, b, ta=False, tb=False):
    ah, al = _split2(a)
    bh, bl = _split2(b)
    return _bdot(ah, bh, ta, tb) + (_bdot(ah, bl, ta, tb) + _bdot(al, bh, ta, tb))


def _chunk_gates(bg, heads):
    n = CHUNK
    row = lax.broadcasted_iota(jnp.int32, (n, n), 0)
    col = lax.broadcasted_iota(jnp.int32, (n, n), 1)
    lane = lax.broadcasted_iota(jnp.int32, bg.shape, 1)
    graw = jnp.where(jnp.logical_and(lane >= heads, lane < 2 * heads), bg, 0.0)
    gc = _dot_mask((row >= col).astype(BF16), graw)
    return gc, _transpose_by_identity(gc)


def _in_lockstep(generators):
    results = [None] * len(generators)
    live = list(enumerate(generators))
    while live:
        still = []
        for i, gen in live:
            try:
                next(gen)
                still.append((i, gen))
            except StopIteration as stop:
                results[i] = stop.value
        live = still
    return results


def _chunk_local(q, k, v, beta, gc, grow):
    c = _Chunk()
    n = CHUNK
    row = lax.broadcasted_iota(jnp.int32, (n, n), 0)
    col = lax.broadcasted_iota(jnp.int32, (n, n), 1)
    c.tri = row >= col
    c.strict = row > col
    eye = row == col
    c.gcb = jnp.broadcast_to(gc, (n, HEAD_DIM))
    c.decay = jnp.where(c.tri, jnp.exp(jnp.where(c.tri, gc - grow, 0.0)), 0.0)
    c.eg = jnp.exp(c.gcb)
    glast = c.gcb[n - 1:n, :]
    c.egl = jnp.exp(glast)
    c.ekl = jnp.exp(glast - c.gcb)
    c.beta = beta
    c.q = q * (HEAD_DIM ** -0.5)
    c.k = k
    c.v = v
    c.kb = k * beta
    c.vb = v * beta
    c.kg = c.kb * c.eg
    both = _bdot(jnp.concatenate([c.kb, c.q], axis=0), k, tb=True)
    yield
    c.L = jnp.where(c.strict, both[:n] * c.decay, 0.0)
    c.A = jnp.where(c.tri, both[n:] * c.decay, 0.0)
    x = -c.L
    tinv = eye.astype(F32) + x
    p = _dot22(x, x)
    yield
    for _ in range(int(math.log2(n)) - 2):
        both = _dot22(jnp.concatenate([p, tinv], axis=0), p)
        yield
        p, tinv = both[:n], tinv + both[n:]
    c.T = tinv + _dot22(tinv, p)
    yield
    tinv = c.T
    uw = _dot22(tinv, jnp.concatenate([c.vb, c.kg], axis=1))
    yield
    c.u, c.w = uw[:, :HEAD_DIM], uw[:, HEAD_DIM:]
    c.qg = c.q * c.eg
    c.kdec = k * c.ekl
    return c


def _gdn_core(qkv, bg, heads):
    t = qkv.shape[0]
    nchunk = t // CHUNK

    gw = heads * HEAD_DIM

    def body(qkv_ref, bg_ref, o_ref, s_ref, state):
        @pl.when(pl.program_id(0) == 0)
        def _():
            state[...] = jnp.zeros_like(state)

        bg_v = bg_ref[...]
        gc_all, gc_rows = _chunk_gates(bg_v, heads)
        def one_head(h):
            col = lambda s: pl.ds(s * gw + h * HEAD_DIM, HEAD_DIM)
            c = yield from _chunk_local(qkv_ref[:, col(0)], qkv_ref[:, col(1)], qkv_ref[:, col(2)], bg_v[:, h:h + 1],
                                        gc_all[:, heads + h:heads + h + 1], gc_rows[heads + h:heads + h + 1, :])
            s0 = state[h]
            v_new = c.u - _bdot(c.w, s0)
            yield
            o = _bdot(c.qg, s0) + _bdot(c.A, v_new)
            return s0, o, s0 * c.egl + _bdot(c.kdec, v_new, ta=True)

        results = _in_lockstep([one_head(h) for h in range(heads)])
        for h, (s0, o, s1) in enumerate(results):
            s_ref[h, 0] = s0
            o_ref[:, pl.ds(h * HEAD_DIM, HEAD_DIM)] = o
            state[h] = s1

    return pl.pallas_call(
        body, name="gdn_core", grid=(nchunk,),
        in_specs=[pl.BlockSpec((CHUNK, 3 * gw), lambda n: (n, 0)), pl.BlockSpec((CHUNK, HEAD_DIM), lambda n: (n, 0))],
        out_specs=[pl.BlockSpec((CHUNK, gw), lambda n: (n, 0)),
                   pl.BlockSpec((heads, 1, HEAD_DIM, HEAD_DIM), lambda n: (0, n, 0, 0))],
        out_shape=[jax.ShapeDtypeStruct((t, gw), F32),
                   jax.ShapeDtypeStruct((heads, nchunk, HEAD_DIM, HEAD_DIM), F32)],
        scratch_shapes=[pltpu.VMEM((heads, HEAD_DIM, HEAD_DIM), F32)],
        compiler_params=_params("arbitrary"),
    )(qkv, bg)


def _gdn_core_backward(qkv, bg, states, do, heads):
    t = qkv.shape[0]
    nchunk = t // CHUNK
    n = CHUNK

    def one_head(chunk_local, s0, d_out, ds1):
        c = yield from chunk_local
        v_new = c.u - _bdot(c.w, s0)
        dqg = _bdot(d_out, s0, tb=True)
        ds0 = _bdot(c.qg, d_out, ta=True) + ds1 * c.egl
        dv_new = _bdot(c.A, d_out, ta=True) + _bdot(c.kdec, ds1)
        yield
        dA = jnp.where(c.tri, _bdot(d_out, v_new, tb=True), 0.0)
        dkdec = _bdot(v_new, ds1, tb=True)
        dgl = jnp.sum(jnp.sum(ds1 * s0, axis=1, keepdims=True), axis=0, keepdims=True) * c.egl
        dw = -_bdot(dv_new, s0, tb=True)
        ds0 = ds0 - _bdot(c.w, dv_new, ta=True)
        yield
        both = _dot22(c.T, jnp.concatenate([dv_new, dw], axis=1), ta=True)
        yield
        dvb, dkg = both[:, :HEAD_DIM], both[:, HEAD_DIM:]
        dL = jnp.where(c.strict, -(_bdot(dvb, c.u, tb=True) + _bdot(dkg, c.w, tb=True)), 0.0)
        yield
        dm1 = dL * c.decay
        dkb = _bdot(dm1, c.k) + dkg * c.eg
        dk = _bdot(dm1, c.kb, ta=True)
        dm2 = dA * c.decay
        dq = _bdot(dm2, c.k) + dqg * c.eg
        dk = dk + _bdot(dm2, c.q, ta=True) + dkdec * c.ekl + dkb * c.beta
        pm = dL * c.L + dA * c.A
        ones = jnp.ones((n, HEAD_DIM), BF16)
        pm_hi, pm_lo = _split2(pm)
        colsum = _bdot(pm_hi, ones, ta=True) + _bdot(pm_lo, ones, ta=True)
        tk_ = jnp.sum(dkdec * c.kdec, axis=1, keepdims=True)
        dgc = (jnp.sum(pm, axis=1, keepdims=True) - colsum
               + jnp.sum(dqg * c.qg, axis=1, keepdims=True)
               - tk_
               + jnp.sum(dkg * c.kg, axis=1, keepdims=True))
        dgl = dgl + jnp.sum(tk_, axis=0, keepdims=True)
        rowi = lax.broadcasted_iota(jnp.int32, (n, HEAD_DIM), 0)
        dgc = dgc + jnp.where(rowi == n - 1, dgl, 0.0)
        dbeta = jnp.sum(dkb * c.k, axis=1, keepdims=True) + jnp.sum(dvb * c.v, axis=1, keepdims=True)
        return dq * (HEAD_DIM ** -0.5), dk, dvb * c.beta, dbeta, dgc, ds0

    gw = heads * HEAD_DIM

    def body(qkv_ref, bg_ref, s_ref, do_ref, dqkv_ref, dbg_ref, dstate):
        @pl.when(pl.program_id(0) == 0)
        def _():
            dstate[...] = jnp.zeros_like(dstate)

        bg_v = bg_ref[...]
        gc_all, gc_rows = _chunk_gates(bg_v, heads)
        lane = lax.broadcasted_iota(jnp.int32, (n, HEAD_DIM), 1)
        dgates = jnp.zeros((n, HEAD_DIM), F32)
        chains = []
        for h in range(heads):
            col = lambda s: pl.ds(s * gw + h * HEAD_DIM, HEAD_DIM)
            c = _chunk_local(qkv_ref[:, col(0)], qkv_ref[:, col(1)], qkv_ref[:, col(2)], bg_v[:, h:h + 1],
                             gc_all[:, heads + h:heads + h + 1], gc_rows[heads + h:heads + h + 1, :])
            chains.append(one_head(c, s_ref[h, 0], do_ref[:, pl.ds(h * HEAD_DIM, HEAD_DIM)], dstate[h]))
        results = _in_lockstep(chains)
        for h, (dq, dk, dv, dbeta, dgc, ds0) in enumerate(results):
            dgates = jnp.where(lane == h, dbeta, jnp.where(lane == heads + h, dgc, dgates))
        for h, (dq, dk, dv, dbeta, dgc, ds0) in enumerate(results):
            dqkv_ref[:, pl.ds(h * HEAD_DIM, HEAD_DIM)] = dq
            dqkv_ref[:, pl.ds(gw + h * HEAD_DIM, HEAD_DIM)] = dk
            dqkv_ref[:, pl.ds(2 * gw + h * HEAD_DIM, HEAD_DIM)] = dv
            dstate[h] = ds0
        row = lax.broadcasted_iota(jnp.int32, (n, n), 0)
        colm = lax.broadcasted_iota(jnp.int32, (n, n), 1)
        draw = _dot_mask((row >= colm).astype(BF16), dgates, ta=True)
        dbg_ref[...] = jnp.where(lane < heads, dgates, draw)

    last = nchunk - 1
    return pl.pallas_call(
        body, name="gdn_core_bwd", grid=(nchunk,),
        in_specs=[pl.BlockSpec((CHUNK, 3 * gw), lambda i: (last - i, 0)),
                  pl.BlockSpec((CHUNK, HEAD_DIM), lambda i: (last - i, 0)),
                  pl.BlockSpec((heads, 1, HEAD_DIM, HEAD_DIM), lambda i: (0, last - i, 0, 0)),
                  pl.BlockSpec((CHUNK, gw), lambda i: (last - i, 0))],
        out_specs=[pl.BlockSpec((CHUNK, 3 * gw), lambda i: (last - i, 0)),
                   pl.BlockSpec((CHUNK, HEAD_DIM), lambda i: (last - i, 0))],
        out_shape=[jax.ShapeDtypeStruct((t, 3 * gw), F32), jax.ShapeDtypeStruct((t, HEAD_DIM), F32)],
        scratch_shapes=[pltpu.VMEM((heads, HEAD_DIM, HEAD_DIM), F32)],
        compiler_params=_params("arbitrary"),
    )(qkv, bg, states, do)


def _gdn_post(o, proj, z_col0, norm_w, heads, tt):
    t = o.shape[0]
    zb = z_col0 // HEAD_DIM

    def body(o_ref, z_ref, w_ref, out_ref):
        ov = o_ref[...]
        z = z_ref[...]
        rms = lax.rsqrt(jnp.mean(ov * ov, axis=-1, keepdims=True) + NORM_EPS)
        out_ref[...] = (ov * rms * w_ref[...] * (z * _sigmoid(z))).astype(BF16)

    return pl.pallas_call(
        body, name="gdn_post", grid=(t // tt, heads),
        in_specs=[pl.BlockSpec((tt, HEAD_DIM), lambda i, h: (i, h)),
                  pl.BlockSpec((tt, HEAD_DIM), lambda i, h: (i, zb + h)),
                  pl.BlockSpec((1, HEAD_DIM), lambda i, h: (0, 0))],
        out_specs=pl.BlockSpec((tt, HEAD_DIM), lambda i, h: (i, h)),
        out_shape=jax.ShapeDtypeStruct((t, heads * HEAD_DIM), BF16),
        compiler_params=_params("parallel", "parallel"),
    )(o, proj, norm_w)


def _gdn_post_backward(dcat, o, proj, z_col0, norm_w, heads, tt):
    t = o.shape[0]
    zb = z_col0 // HEAD_DIM

    def body(d_ref, o_ref, z_ref, w_ref, do_ref, dz_ref, dw_ref):
        d = d_ref[...]
        ov = o_ref[...]
        z = z_ref[...]
        w = w_ref[...]
        rms = lax.rsqrt(jnp.mean(ov * ov, axis=-1, keepdims=True) + NORM_EPS)
        ohat = ov * rms
        sg = _sigmoid(z)
        gate = z * sg
        dz_ref[...] = (d * ohat * w * (sg * (1.0 + z * (1.0 - sg)))).astype(BF16)
        don = d * gate
        dohat = don * w
        do_ref[...] = rms * (dohat - ohat * jnp.mean(dohat * ohat, axis=-1, keepdims=True))
        dw = jnp.sum(don * ohat, axis=0, keepdims=True)
        first = jnp.logical_and(pl.program_id(0) == 0, pl.program_id(1) == 0)

        @pl.when(first)
        def _():
            dw_ref[...] = dw

        @pl.when(jnp.logical_not(first))
        def _():
            dw_ref[...] += dw

    blk = pl.BlockSpec((tt, HEAD_DIM), lambda i, h: (i, h))
    return pl.pallas_call(
        body, name="gdn_post_bwd", grid=(t // tt, heads),
        in_specs=[blk, blk, pl.BlockSpec((tt, HEAD_DIM), lambda i, h: (i, zb + h)),
                  pl.BlockSpec((1, HEAD_DIM), lambda i, h: (0, 0))],
        out_specs=[blk, blk, pl.BlockSpec((1, HEAD_DIM), lambda i, h: (0, 0))],
        out_shape=[jax.ShapeDtypeStruct((t, heads * HEAD_DIM), F32),
                   jax.ShapeDtypeStruct((t, heads * HEAD_DIM), BF16),
                   jax.ShapeDtypeStruct((1, HEAD_DIM), F32)],
        compiler_params=_params("arbitrary", "arbitrary"),
    )(dcat, o, proj, norm_w)


def _pool_select(levels, group):
    out = levels[-1]
    for gi in range(len(levels) - 2, -1, -1):
        out = jnp.where(group == gi, levels[gi], out)
    return out


def _pool_counts(t, width, group):
    pos = lax.broadcasted_iota(jnp.int32, (t, width), 0)
    win = jnp.left_shift(2, group)
    return jnp.minimum(pos + 1, win).astype(F32)


def _pooled(p, group):
    levels, s, step = [], p, 1
    for _ in POOL_WINDOWS:
        s = s + _shift_down(s, step)
        levels.append(s)
        step *= 2
    cnt = _pool_counts(p.shape[0], p.shape[1], group)
    return _pool_select(levels, group) / cnt - p, cnt


def _pool_forward(proj, p_col0, pool_w, pool_scale):
    t = proj.shape[0]
    groups, cg, _ = pool_w.shape
    pb = p_col0 // cg

    def body(p_ref, w_ref, s_ref, o_ref):
        pooled, _ = _pooled(p_ref[...], pl.program_id(0))
        o_ref[...] = (_bdot(pooled, w_ref[0]) * s_ref[...]).astype(BF16)

    return pl.pallas_call(
        body, name="pool_fwd", grid=(groups,),
        in_specs=[pl.BlockSpec((t, cg), lambda g: (0, pb + g)), pl.BlockSpec((1, cg, cg), lambda g: (g, 0, 0)),
                  pl.BlockSpec((1, cg), lambda g: (0, g))],
        out_specs=pl.BlockSpec((t, cg), lambda g: (0, g)),
        out_shape=jax.ShapeDtypeStruct((t, groups * cg), BF16),
        compiler_params=_params("parallel"),
    )(proj, pool_w, pool_scale)


def _pool_backward(dcat, d_col0, proj, p_col0, pool_w, pool_scale):
    t = proj.shape[0]
    groups, cg, _ = pool_w.shape
    pb = p_col0 // cg
    db = d_col0 // cg

    def body(d_ref, p_ref, w_ref, s_ref, dp_ref, dw_ref, ds_ref):
        group = pl.program_id(0)
        pooled, cnt = _pooled(p_ref[...], group)
        w = w_ref[0]
        d = d_ref[...]
        mixed = _bdot(pooled, w)
        ds_ref[...] = jnp.sum(d * mixed, axis=0, keepdims=True)
        dmixed = d * s_ref[...]
        dw_ref[0] = _bdot(pooled, dmixed, ta=True)
        dpooled = _bdot(dmixed, w, tb=True)
        levels, s, step = [], dpooled / cnt, 1
        for _ in POOL_WINDOWS:
            s = s + _shift_up(s, step)
            levels.append(s)
            step *= 2
        dp_ref[...] = (_pool_select(levels, group) - dpooled).astype(BF16)

    return pl.pallas_call(
        body, name="pool_bwd", grid=(groups,),
        in_specs=[pl.BlockSpec((t, cg), lambda g: (0, db + g)), pl.BlockSpec((t, cg), lambda g: (0, pb + g)),
                  pl.BlockSpec((1, cg, cg), lambda g: (g, 0, 0)), pl.BlockSpec((1, cg), lambda g: (0, g))],
        out_specs=[pl.BlockSpec((t, cg), lambda g: (0, g)), pl.BlockSpec((1, cg, cg), lambda g: (g, 0, 0)),
                   pl.BlockSpec((1, cg), lambda g: (0, g))],
        out_shape=[jax.ShapeDtypeStruct((t, groups * cg), BF16), jax.ShapeDtypeStruct((groups, cg, cg), F32),
                   jax.ShapeDtypeStruct((1, groups * cg), F32)],
        compiler_params=_params("parallel"),
    )(dcat, proj, pool_w, pool_scale)


def _attention(q, k, v, tq):
    t, d = q.shape
    m = k.shape[0]
    dh = d // XATTN_HEADS
    scale = dh ** -0.5

    def body(q_ref, k_ref, v_ref, o_ref):
        s = _bdot(q_ref[...], k_ref[...], tb=True) * scale
        s = s - jnp.max(s, axis=-1, keepdims=True)
        e = jnp.exp(s)
        p = e / jnp.sum(e, axis=-1, keepdims=True)
        o_ref[...] = _bdot(p, v_ref[...]).astype(BF16)

    return pl.pallas_call(
        body, name="xattn_fwd", grid=(XATTN_HEADS, t // tq),
        in_specs=[pl.BlockSpec((tq, dh), lambda h, i: (i, h)), pl.BlockSpec((m, dh), lambda h, i: (0, h)),
                  pl.BlockSpec((m, dh), lambda h, i: (0, h))],
        out_specs=pl.BlockSpec((tq, dh), lambda h, i: (i, h)),
        out_shape=jax.ShapeDtypeStruct((t, d), BF16),
        compiler_params=_params("parallel", "parallel"),
    )(q, k, v)


def _attention_backward(q, k, v, do, tq):
    t, d = q.shape
    m = k.shape[0]
    dh = d // XATTN_HEADS
    scale = dh ** -0.5

    def body(q_ref, k_ref, v_ref, do_ref, dq_ref, dk_ref, dv_ref, dk_acc, dv_acc):
        i = pl.program_id(1)
        qv, kv, vv, dov = q_ref[...], k_ref[...], v_ref[...], do_ref[...]
        s = _bdot(qv, kv, tb=True) * scale
        s = s - jnp.max(s, axis=-1, keepdims=True)
        e = jnp.exp(s)
        p = e / jnp.sum(e, axis=-1, keepdims=True)
        dp = _bdot(dov, vv, tb=True)
        ds = p * (dp - jnp.sum(dp * p, axis=-1, keepdims=True)) * scale
        dq_ref[...] = _bdot(ds, kv).astype(BF16)
        dv_part = _bdot(p, dov, ta=True)
        dk_part = _bdot(ds, qv, ta=True)

        @pl.when(i == 0)
        def _():
            dk_acc[...] = dk_part
            dv_acc[...] = dv_part

        @pl.when(i > 0)
        def _():
            dk_acc[...] += dk_part
            dv_acc[...] += dv_part

        @pl.when(i == pl.num_programs(1) - 1)
        def _():
            dk_ref[...] = dk_acc[...].astype(BF16)
            dv_ref[...] = dv_acc[...].astype(BF16)

    qblk = pl.BlockSpec((tq, dh), lambda h, i: (i, h))
    kblk = pl.BlockSpec((m, dh), lambda h, i: (0, h))
    return pl.pallas_call(
        body, name="xattn_bwd", grid=(XATTN_HEADS, t // tq),
        in_specs=[qblk, kblk, kblk, qblk],
        out_specs=[qblk, kblk, kblk],
        out_shape=[jax.ShapeDtypeStruct((t, d), BF16), jax.ShapeDtypeStruct((m, d), BF16),
                   jax.ShapeDtypeStruct((m, d), BF16)],
        scratch_shapes=[pltpu.VMEM((m, dh), F32), pltpu.VMEM((m, dh), F32)],
        compiler_params=_params("parallel", "arbitrary"),
    )(q, k, v, do)


def _ln_backward_rows(name, dmain, dres, xhat, rstd, gamma, tm):
    t, d = xhat.shape

    def body(m_ref, r_ref, x_ref, s_ref, g_ref, du_ref, dub_ref, dg_ref, db_ref):
        du, dg, db = _ln_backward_math(m_ref[...] + ALPHA * r_ref[...], x_ref[...], s_ref[...], g_ref[...])
        du_ref[...] = du
        dub_ref[...] = du.astype(BF16)
        first = pl.program_id(0) == 0

        @pl.when(first)
        def _():
            dg_ref[...] = dg
            db_ref[...] = db

        @pl.when(jnp.logical_not(first))
        def _():
            dg_ref[...] += dg
            db_ref[...] += db

    row = pl.BlockSpec((tm, d), lambda i: (i, 0))
    vec = pl.BlockSpec((1, d), lambda i: (0, 0))
    return pl.pallas_call(
        body, name=name, grid=(t // tm,),
        in_specs=[row, row, row, pl.BlockSpec((tm, 1), lambda i: (i, 0)), vec],
        out_specs=[row, row, vec, vec],
        out_shape=[jax.ShapeDtypeStruct((t, d), F32), jax.ShapeDtypeStruct((t, d), BF16),
                   jax.ShapeDtypeStruct((1, d), F32), jax.ShapeDtypeStruct((1, d), F32)],
        compiler_params=_params("arbitrary"),
    )(dmain, dres, xhat, rstd, gamma)


def _loss_and_ln_backward(xhat, rstd, gamma, beta, target, tm):
    t, d = xhat.shape

    def body(x_ref, r_ref, g_ref, b_ref, t_ref, du_ref, dub_ref, dg_ref, db_ref, loss_ref):
        xh = x_ref[...]
        g = g_ref[...]
        diff = xh * g + b_ref[...] - t_ref[...]
        part = jnp.sum(jnp.sum(diff * diff, axis=1, keepdims=True), axis=0, keepdims=True) * (0.5 / d)
        dy = diff * (1.0 / d)
        du, dg, db = _ln_backward_math(dy, xh, r_ref[...], g)
        du_ref[...] = du
        dub_ref[...] = du.astype(BF16)
        lossrow = jnp.broadcast_to(part, (1, HEAD_DIM))
        first = pl.program_id(0) == 0

        @pl.when(first)
        def _():
            dg_ref[...] = dg
            db_ref[...] = db
            loss_ref[...] = lossrow

        @pl.when(jnp.logical_not(first))
        def _():
            dg_ref[...] += dg
            db_ref[...] += db
            loss_ref[...] += lossrow

    row = pl.BlockSpec((tm, d), lambda i: (i, 0))
    vec = pl.BlockSpec((1, d), lambda i: (0, 0))
    return pl.pallas_call(
        body, name="loss_ln3_bwd", grid=(t // tm,),
        in_specs=[row, pl.BlockSpec((tm, 1), lambda i: (i, 0)), vec, vec, row],
        out_specs=[row, row, vec, vec, pl.BlockSpec((1, HEAD_DIM), lambda i: (0, 0))],
        out_shape=[jax.ShapeDtypeStruct((t, d), F32), jax.ShapeDtypeStruct((t, d), BF16),
                   jax.ShapeDtypeStruct((1, d), F32), jax.ShapeDtypeStruct((1, d), F32),
                   jax.ShapeDtypeStruct((1, HEAD_DIM), F32)],
        compiler_params=_params("arbitrary"),
    )(xhat, rstd, gamma, beta, target)


def _after(token, a):
    return a if token is None else a + token[:1, :1].astype(a.dtype)


def _pick(n, prefs):
    for p in prefs:
        if n % p == 0:
            return p
    return n


def _local_step(x, mem, target, w, token=None):
    t, d = x.shape
    heads = w["a_log"].shape[1]
    gw = heads * HEAD_DIM
    groups, cg, _ = w["pool_w"].shape
    pw = groups * cg
    n_main = 4 * gw + pw
    in_cols = n_main + 2 * heads
    s_in = w["w_in_t"].shape[0]

    tm = _pick(t, (512, 256, 128))
    tm_ln = _pick(t, (256, 128))
    tm_big = _pick(t, (1024, 512, 256, 128))
    tk = _pick(d, K_STEPS)

    w_in_t = w["w_in_t"].reshape(in_cols, d)
    w_p_t = w_in_t[4 * gw + 2 * heads:]
    w_ba_t = jnp.pad(w_in_t[4 * gw:4 * gw + 2 * heads], ((0, HEAD_DIM - 2 * heads), (0, 0)))
    x_bf = _after(token, x).astype(BF16)
    mem_bf = _after(token, mem).astype(BF16)

    tn_d = _pick(d, (1024, 512, 256, 128))
    proj = _plain("proj_main", x_bf, w_in_t, tb=True, n_used=4 * gw, tm=tm_big, tn=_pick(4 * gw, (1024, 512, 256, 128)),
                  tk=tk, out_dtype=F32)
    pproj = _plain("proj_pool", x_bf, w_p_t, tb=True, tm=tm_big, tn=_pick(pw, (1024, 512, 256, 128)), tk=tk, out_dtype=F32)
    ea, dtb = _gate_vectors(w["a_log"], w["dt_bias"], heads)
    vec128 = lambda i, j: (0, 0)
    ba, bg = _matmul(
        "proj_gates", x_bf, w_ba_t, tb=True, tm=tm, tn=HEAD_DIM, tk=tk,
        extra=[(ea, (1, HEAD_DIM), vec128), (dtb, (1, HEAD_DIM), vec128)],
        outs=[(jax.ShapeDtypeStruct((t, HEAD_DIM), F32), (tm, HEAD_DIM), _tile)] * 2,
        epilogue=_gates_epilogue(heads))
    qkv = _gdn_pre(proj, w["conv_w"], heads)
    o_gdn, states = _gdn_core(qkv, bg, heads)
    cat_g = _gdn_post(o_gdn, proj, 3 * gw, w["gdn_norm_w"], heads, tm)
    cat_p = _pool_forward(pproj, 0, w["pool_w"], w["pool_scale"])
    cat = jnp.concatenate([cat_g, cat_p], axis=1)
    w = {**w, **(yield ("weights", 1, cat))}
    h1, h1_bf, xhat1, rstd1 = _ln_forward("mix_ln1", cat, w["w_out"], x, w["ln1_g"], w["ln1_b"], tm=tm_ln, tk=tk)

    q = _plain("xattn_q", h1_bf, w["xq_w"], tm=tm, tn=tn_d, tk=tk, out_dtype=BF16)
    mlen = mem.shape[0]
    tm_mem = _pick(mlen, (256, 128))
    k = _plain("xattn_k", mem_bf, w["xk_w"], tm=tm_mem, tn=tn_d, tk=tk, out_dtype=BF16)
    v = _plain("xattn_v", mem_bf, w["xv_w"], tm=tm_mem, tn=tn_d, tk=tk, out_dtype=BF16)
    att = _attention(q, k, v, tm)
    h2, h2_bf, xhat2, rstd2 = _ln_forward("xo_ln2", att, w["xo_w"], h1, w["ln2_g"], w["ln2_b"], tm=tm_ln, tk=tk)

    w = {**w, **(yield ("weights", 2, h2_bf))}
    s_up = w["w_up3"].shape[0]
    ff = s_up * w["w_up3"].shape[2]
    tn_f = _pick(ff // s_up, (1024, 512, 256, 128))

    def up_epi(acc, ex, out, i):
        r = jnp.maximum(acc, 0.0)
        out[0][...] = (r * r).astype(BF16)
        out[1][...] = (2.0 * r).astype(BF16)

    act, act_grad = _matmul(
        "mlp_up", h2_bf, w["w_up3"], b_blocks=s_up, tm=tm_big, tn=tn_f, tk=tk,
        outs=[(jax.ShapeDtypeStruct((t, ff), BF16), (tm_big, tn_f), _tile)] * 2, epilogue=up_epi)
    w = {**w, **(yield ("weights", 3, act))}
    tk_f = _pick(ff, K_STEPS)
    xhat3, rstd3 = _ln_forward("down_ln3", act, w["w_down"], h2, w["ln3_g"], w["ln3_b"], tm=tm, tk=tk_f, want_h=False)

    grads = {}
    du3, du3_bf, grads["ln3_g"], grads["ln3_b"], loss = _loss_and_ln_backward(
        xhat3, rstd3, w["ln3_g"], w["ln3_b"], target, tm_ln)

    def dup_epi(acc, ex, out, i):
        out[0][...] = (acc * ex[0][...].astype(F32)).astype(BF16)

    dup = _matmul(
        "mlp_down_dx", du3_bf, w["w_down"], tb=True, tm=tm_big, tn=tn_f, tk=tk,
        extra=[(act_grad, (tm_big, tn_f), _tile)],
        outs=[(jax.ShapeDtypeStruct((t, ff), BF16), (tm_big, tn_f), _tile)], epilogue=dup_epi)[0]
    tk_t = _pick(t, K_STEPS)
    tm_w = _pick(d, (512, 256, 128))
    grads["w_down"] = _plain("mlp_down_dw", act, du3_bf, ta=True, tm=_pick(ff, (512, 256, 128)), tn=d, tk=tk_t,
                             out_dtype=F32)
    grads["w_up3"] = _plain("mlp_up_dw", h2_bf, dup, ta=True, tm=tm_w, tn=ff // s_up, tk=tk_t, out_dtype=F32, out3=s_up,
                            n_outer=True)
    token = yield ("grads", 0, {n: grads.pop(n) for n in ("w_down", "w_up3")})
    dh2 = _plain("mlp_up_dx", dup, w["w_up3"], tb=True, b_blocks=s_up, tm=tm_big, tn=tn_d,
                 tk=_pick(ff // s_up, K_STEPS), out_dtype=F32)
    du2, du2_bf, grads["ln2_g"], grads["ln2_b"] = _ln_backward_rows(
        "ln2_bwd", dh2, du3, xhat2, rstd2, _after(token, w["ln2_g"]), tm_ln)
    token = yield ("poll", 0, du2_bf)

    grads["xo_w"] = _plain("xo_dw", att, du2_bf, ta=True, tm=tm_w, tn=d, tk=tk_t, out_dtype=F32)
    datt = _plain("xo_dx", du2_bf, w["xo_w"], tb=True, tm=tm, tn=tn_d, tk=tk, out_dtype=BF16)
    dq, dk, dv = _attention_backward(q, k, v, datt, tm)
    tk_m = _pick(mlen, (256, 128))
    grads["xq_w"] = _plain("xq_dw", h1_bf, dq, ta=True, tm=tm_w, tn=d, tk=tk_t, out_dtype=F32)
    grads["xk_w"] = _plain("xk_dw", mem_bf, dk, ta=True, tm=tm_w, tn=tn_d, tk=tk_m, out_dtype=F32)
    grads["xv_w"] = _plain("xv_dw", mem_bf, dv, ta=True, tm=tm_w, tn=tn_d, tk=tk_m, out_dtype=F32)
    du1, du1_bf, grads["ln1_g"], grads["ln1_b"] = _ln_backward(
        "xq_dx_ln1", dq, w["xq_w"], du2, xhat1, rstd1, _after(token, w["ln1_g"]), tm=tm_ln, tk=tk)

    grads["w_out"] = _plain("out_dw", cat, du1_bf, ta=True, tm=tm_w, tn=d, tk=tk_t, out_dtype=F32)
    token = yield ("grads", 1, {n: grads.pop(n) for n in ("xo_w", "xq_w", "xk_w", "xv_w", "w_out")})
    dcat = _plain("out_dx", du1_bf, w["w_out"], tb=True, tm=tm, tn=tn_d, tk=tk, out_dtype=F32)
    dp, grads["pool_w"], grads["pool_scale"] = _pool_backward(dcat, gw, pproj, 0, w["pool_w"],
                                                              _after(token, w["pool_scale"]))
    do_gdn, dz, grads["gdn_norm_w"] = _gdn_post_backward(dcat, o_gdn, proj, 3 * gw, _after(token, w["gdn_norm_w"]),
                                                         heads, tm)
    token = yield ("poll", 1, do_gdn)
    dqkv, dbg = _gdn_core_backward(qkv, _after(token, bg), states, do_gdn, heads)
    dqkv_pre, grads["conv_w"] = _gdn_pre_backward(proj, w["conv_w"], dqkv, heads)
    dba, dalog_row, ddt_row = _gates_backward(ba, bg, dbg, ea, dtb, heads)
    grads["a_log"] = dalog_row[:, heads:2 * heads]
    grads["dt_bias"] = ddt_row[:, heads:2 * heads]

    dproj = jnp.concatenate([dqkv_pre, dz, dp], axis=1)
    dw_main = _plain("proj_dw", dproj, x_bf, ta=True, tm=_pick(n_main, (512, 256, 128)), tn=d, tk=tk_t, out_dtype=F32)
    dw_ba = _plain("proj_gates_dw", dba, x_bf, ta=True, tm=HEAD_DIM, tn=tn_d, tk=tk_t, out_dtype=F32)
    dw_in_t = jnp.concatenate([dw_main[:4 * gw], dw_ba[:2 * heads], dw_main[4 * gw:]], axis=0)
    grads["w_in_t"] = dw_in_t.reshape(s_in, in_cols // s_in, d)

    def dx_epi(acc, ex, out, i):
        out[0][...] = acc + ex[1][...] + ALPHA * ex[0][...]

    def add_epi(acc, ex, out, i):
        out[0][...] = acc + ex[0][...]

    token = yield ("grads", 2, {n: grads.pop(n) for n in ("w_in_t", "pool_w")})
    dx_gates = _plain("proj_gates_dx", dba, _after(token, w_ba_t), tm=tm, tn=tn_d, tk=HEAD_DIM, out_dtype=F32)
    out_tile = [(jax.ShapeDtypeStruct((t, d), F32), (tm, tn_d), _tile)]
    dx_pool = _matmul("proj_pool_dx", dp, w_p_t, tm=tm, tn=tn_d, tk=_pick(pw, K_STEPS),
                      extra=[(dx_gates, (tm, tn_d), _tile)], outs=out_tile, epilogue=add_epi)[0]
    grad_x = _matmul(
        "proj_dx", dproj, w_in_t, k_used=4 * gw, tm=tm, tn=tn_d, tk=_pick(4 * gw, K_STEPS),
        extra=[(du1, (tm, tn_d), _tile), (dx_pool, (tm, tn_d), _tile)], outs=out_tile, epilogue=dx_epi)[0]
    yield ("poll", 2, grad_x)
    return loss, grad_x, grads


def _adamw(name, w, g, m, v):
    r, c = w.shape
    if r % 8 == 0:
        tr = _pick(r, (256, 128, 64, 32, 16, 8))
        blk, steps = pl.BlockSpec((tr, c), lambda i: (i, 0)), r // tr
    else:
        tc = _pick(c, (256, 128))
        blk, steps = pl.BlockSpec((r, tc), lambda i: (0, i)), c // tc
    c1 = 1.0 - ADAM_B1 ** ADAM_STEP
    c2 = 1.0 - ADAM_B2 ** ADAM_STEP

    def body(w_ref, g_ref, m_ref, v_ref, d_ref, mo_ref, vo_ref):
        gv = g_ref[...]
        mn = ADAM_B1 * m_ref[...] + (1.0 - ADAM_B1) * gv
        vn = ADAM_B2 * v_ref[...] + (1.0 - ADAM_B2) * (gv * gv)
        d_ref[...] = -ADAM_LR * ((mn / c1) / (jnp.sqrt(vn / c2) + ADAM_EPS) + ADAM_WD * w_ref[...])
        mo_ref[...] = mn
        vo_ref[...] = vn

    return pl.pallas_call(
        body, name=name, grid=(steps,), in_specs=[blk] * 4, out_specs=[blk] * 3,
        out_shape=[jax.ShapeDtypeStruct((r, c), F32)] * 3,
        compiler_params=_params("parallel"),
    )(w, g, m, v)


def _place():
    x, y, c = lax.axis_index("x"), lax.axis_index("y"), lax.axis_index("c")
    chips = [(1 - x, y), (x, 1 - y), (1 - x, 1 - y)]
    return x, y, c, chips


HBM = pl.BlockSpec(memory_space=pltpu.HBM)


SEM = pl.BlockSpec(memory_space=pltpu.SEMAPHORE)
ANY = pl.BlockSpec(memory_space=pl.ANY)
EFFECT = pltpu.SideEffectType.DATAFLOW_SIDE_EFFECTING


def _in_hbm(a):
    return pltpu.with_memory_space_constraint(a, pltpu.HBM)


def _remote(src, dst, send_sem, recv_sem, to):
    return pltpu.make_async_remote_copy(src_ref=src, dst_ref=dst, send_sem=send_sem, recv_sem=recv_sem,
                                        device_id=to, device_id_type=MESH)


def _by_rows(rows):
    return rows % 32 == 0


def _half_shape(rows, cols):
    return (rows // 2, cols) if _by_rows(rows) else (rows, cols // 2)


def _half(ref, which, *lead):
    rows, cols = ref.shape[-2:]
    if _by_rows(rows):
        return ref.at[(*lead, pl.ds(which * (rows // 2), rows // 2))]
    return ref.at[(*lead, slice(None), pl.ds(which * (cols // 2), cols // 2))]


def _landed(lands, i, shard_index, which):
    return _half(lands[i], which, shard_index)


def _gather_start(name, shards, after):
    n = len(shards)
    lands = [lax.empty((N_SHARD,) + s.shape, s.dtype) for s in shards]

    def body(*refs):
        ins, zones = refs[:n], refs[n:2 * n]
        ici_send, ici_recv, own_send, own_recv = refs[2 * n + 1:2 * n + 5]
        token = refs[-1]
        x, y, c, chips = _place()
        me = 2 * x + y
        for i in range(n):
            for j, chip in enumerate(chips):
                _remote(_half(ins[i], c), _landed(zones, i, me, c), ici_send.at[3 * i + j],
                        ici_recv.at[3 * i + j], (*chip, c)).start()
        for i in range(n):
            _remote(ins[i], zones[i].at[me], own_send.at[i], own_recv.at[i], (x, y, 1 - c)).start()
        token[...] = jnp.zeros_like(token)

    dma = pltpu.SemaphoreType.DMA
    outs = pl.pallas_call(
        body, name=name,
        in_specs=[HBM] * (2 * n) + [ANY],
        out_shape=(dma((3 * n,)), dma((3 * n,)), dma((n,)), dma((n,)),
                   *[pltpu.HBM(a.shape, a.dtype) for a in shards + lands], jax.ShapeDtypeStruct((8, LANES), F32)),
        out_specs=(SEM, SEM, SEM, SEM, *[HBM] * (2 * n), pl.BlockSpec(memory_space=pltpu.VMEM)),
        input_output_aliases={k: 4 + k for k in range(2 * n)},
        compiler_params=pltpu.CompilerParams(has_side_effects=EFFECT),
    )(*[_in_hbm(a) for a in shards + lands], after)
    sems = dict(zip(("ici_send", "ici_recv", "own_send", "own_recv"), outs[:4]))
    return sems, list(outs[4:4 + n]), list(outs[4 + n:4 + 2 * n]), outs[-1]


def _gather_forward(name, idx, lands, sems, after):
    n = len(idx)

    def body(*refs):
        zones = refs[:n]
        ici_recv = refs[n]
        fwd_send, fwd_recv = refs[n + 2], refs[n + 3]
        x, y, c, chips = _place()
        for k, i in enumerate(idx):
            for j, chip in enumerate(chips):
                half = _landed(zones, k, 2 * chip[0] + chip[1], c)
                _remote(half, half, fwd_send.at[3 * k + j], ici_recv.at[3 * i + j], (*chip, c)).wait_recv()
                _remote(half, half, fwd_send.at[3 * k + j], fwd_recv.at[3 * k + j], (x, y, 1 - c)).start()

    dma = pltpu.SemaphoreType.DMA
    outs = pl.pallas_call(
        body, name=name,
        in_specs=[HBM] * n + [SEM, ANY],
        out_shape=(dma((3 * n,)), dma((3 * n,)), *[pltpu.HBM(a.shape, a.dtype) for a in lands]),
        out_specs=(SEM, SEM, *[HBM] * n),
        input_output_aliases={k: 2 + k for k in range(n)},
        compiler_params=pltpu.CompilerParams(has_side_effects=EFFECT),
    )(*lands, sems["ici_recv"], after)
    return (outs[0], outs[1]), list(outs[2:])


def _gather_wait(name, idx, shards, lands, sems, fwd):
    n = len(idx)

    def body(*refs):
        ins, zones = refs[:n], refs[n:2 * n]
        ici_send, own_send, own_recv, fwd_send, fwd_recv = refs[2 * n:2 * n + 5]
        x, y, c, chips = _place()
        me = 2 * x + y
        for k, i in enumerate(idx):
            mine = _half(ins[k], c)
            for j, chip in enumerate(chips):
                theirs = 2 * chip[0] + chip[1]
                _remote(mine, _landed(zones, k, me, c), ici_send.at[3 * i + j], fwd_recv.at[3 * k + j],
                        (*chip, c)).wait_send()
                sent = _landed(zones, k, theirs, c)
                _remote(sent, sent, fwd_send.at[3 * k + j], fwd_recv.at[3 * k + j], (x, y, 1 - c)).wait_send()
                passed = _landed(zones, k, theirs, 1 - c)
                _remote(passed, passed, fwd_send.at[3 * k + j], fwd_recv.at[3 * k + j], (x, y, 1 - c)).wait_recv()
            own = _remote(ins[k], zones[k].at[me], own_send.at[i], own_recv.at[i], (x, y, 1 - c))
            own.wait_send()
            own.wait_recv()

    outs = pl.pallas_call(
        body, name=name,
        in_specs=[HBM] * (2 * n) + [SEM] * 5,
        out_shape=tuple(pltpu.HBM(a.shape, a.dtype) for a in lands),
        out_specs=tuple([HBM] * n),
        input_output_aliases={n + k: k for k in range(n)},
        compiler_params=pltpu.CompilerParams(has_side_effects=EFFECT),
    )(*shards, *lands, sems["ici_send"], sems["own_send"], sems["own_recv"], fwd[0], fwd[1])
    return list(outs)


def _all_reduce_small(name, slab, after=None):
    r, width = slab.shape
    ndev = 8

    def body(x_ref, after_ref, out_ref, buf, send_sems, recv_sems):
        x, y, c, _ = _place()
        me = 4 * x + 2 * y + c
        buf[me] = x_ref[...]
        copies = []
        for k in range(1, ndev):
            peer = jnp.bitwise_xor(me, k)
            to = (peer // 4, (peer // 2) % 2, peer % 2)
            cp = pltpu.make_async_remote_copy(src_ref=x_ref, dst_ref=buf.at[me], send_sem=send_sems.at[k - 1],
                                              recv_sem=recv_sems.at[k - 1], device_id=to, device_id_type=MESH)
            cp.start()
            copies.append(cp)
        for k in range(1, ndev):
            peer = jnp.bitwise_xor(me, k)
            pltpu.make_async_remote_copy(src_ref=x_ref, dst_ref=buf.at[peer], send_sem=send_sems.at[k - 1],
                                         recv_sem=recv_sems.at[k - 1], device_id=(x, y, c),
                                         device_id_type=MESH).wait_recv()
        for cp in copies:
            cp.wait_send()
        total = buf[0]
        for d in range(1, ndev):
            total = total + buf[d]
        out_ref[...] = total

    return pl.pallas_call(
        body, name=name,
        in_specs=[pl.BlockSpec(memory_space=pltpu.VMEM), ANY], out_specs=pl.BlockSpec(memory_space=pltpu.VMEM),
        out_shape=jax.ShapeDtypeStruct((r, width), F32),
        scratch_shapes=[pltpu.VMEM((ndev, r, width), F32), pltpu.SemaphoreType.DMA((ndev - 1,)),
                        pltpu.SemaphoreType.DMA((ndev - 1,))],
        compiler_params=pltpu.CompilerParams(vmem_limit_bytes=VMEM_LIMIT),
    )(slab, slab if after is None else after)


def _half_tiling(rows, cols):
    if _by_rows(rows):
        tr = _pick(rows // 2, (256, 128, 64, 32, 16))
        nb = (rows // 2) // tr
        return (tr, cols), nb, (lambda which, b: (which * nb + b, 0)), (lambda b: (b, 0))
    tc = _pick(cols // 2, (256, 128))
    nb = (cols // 2) // tc
    return (rows, tc), nb, (lambda which, b: (0, which * nb + b)), (lambda b: (0, b))


def _chip_partial(name, grad, other, core):
    s, r, cdim = grad.shape
    blk, nb, whole, within = _half_tiling(r, cdim)

    def body(core_ref, g_ref, o_ref, out_ref):
        out_ref[...] = (g_ref[...] + o_ref[...]).astype(BF16)

    return pl.pallas_call(
        body, name=name,
        grid_spec=pltpu.PrefetchScalarGridSpec(
            num_scalar_prefetch=1, grid=(s, nb),
            in_specs=[pl.BlockSpec((None,) + blk, lambda j, b, core_ref: (j,) + whole(core_ref[0], b)),
                      pl.BlockSpec((None,) + blk, lambda j, b, core_ref: (j,) + within(b))],
            out_specs=pl.BlockSpec((None,) + blk, lambda j, b, core_ref: (j,) + within(b))),
        out_shape=jax.ShapeDtypeStruct((s,) + _half_shape(r, cdim), BF16),
        compiler_params=_params("parallel", "parallel"),
    )(core, grad, other)


def _partial_copies(ins, zones, send_sems, recv_sems):
    x, y, c, chips = _place()
    return [_remote(ins[i].at[2 * chip[0] + chip[1]], zones[i].at[j], send_sems.at[3 * i + j],
                    recv_sems.at[3 * i + j], (*chip, c))
            for i in range(len(ins)) for j, chip in enumerate(chips)]


def _swap_copies(ins, zones, send_sems, recv_sems):
    x, y, c, _ = _place()
    copies = []
    for i in range(len(ins)):
        for s in range(N_SHARD):
            copies.append(_remote(_half(ins[i], 1 - c, s), zones[i].at[s],
                                  send_sems.at[N_SHARD * i + s], recv_sems.at[N_SHARD * i + s], (x, y, 1 - c)))
    return copies


def _exchange_start(name, plan, sources, lands, per_array):
    n = len(sources)
    lands = [lax.empty(shape, dtype) for shape, dtype in lands]

    def body(*refs):
        for cp in plan(refs[:n], refs[n:2 * n], refs[2 * n], refs[2 * n + 1]):
            cp.start()
        refs[-1][...] = jnp.zeros_like(refs[-1])

    dma = pltpu.SemaphoreType.DMA
    outs = pl.pallas_call(
        body, name=name,
        in_specs=[HBM] * (2 * n),
        out_shape=(dma((per_array * n,)), dma((per_array * n,)),
                   *[pltpu.HBM(a.shape, a.dtype) for a in list(sources) + lands], jax.ShapeDtypeStruct((8, LANES), F32)),
        out_specs=(SEM, SEM, *[HBM] * (2 * n), pl.BlockSpec(memory_space=pltpu.VMEM)),
        input_output_aliases={k: 2 + k for k in range(2 * n)},
        compiler_params=pltpu.CompilerParams(has_side_effects=EFFECT),
    )(*[_in_hbm(a) for a in list(sources) + lands])
    return (outs[0], outs[1]), list(outs[2:2 + n]), list(outs[2 + n:2 + 2 * n]), outs[-1]


def _exchange_wait(name, plan, started, after):
    sems, partials, lands, _ = started
    n = len(partials)

    def body(*refs):
        for cp in plan(refs[:n], refs[n:2 * n], refs[2 * n], refs[2 * n + 1]):
            cp.wait_send()
            cp.wait_recv()

    outs = pl.pallas_call(
        body, name=name,
        in_specs=[HBM] * (2 * n) + [SEM, SEM] + [ANY] * len(after),
        out_shape=tuple(pltpu.HBM(a.shape, a.dtype) for a in lands),
        out_specs=tuple([HBM] * n),
        input_output_aliases={n + k: k for k in range(n)},
        compiler_params=pltpu.CompilerParams(has_side_effects=EFFECT),
    )(*partials, *lands, sems[0], sems[1], *after)
    return list(outs)


def _reduce_own(name, grad, other, received, where):
    s, r, cdim = grad.shape
    blk, nb, whole, within = _half_tiling(r, cdim)

    def body(where_ref, g_ref, o_ref, r_ref, out_ref):
        total = g_ref[...] + o_ref[...]
        for j in range(3):
            total = total + r_ref[j].astype(F32)
        out_ref[...] = total

    return pl.pallas_call(
        body, name=name,
        grid_spec=pltpu.PrefetchScalarGridSpec(
            num_scalar_prefetch=1, grid=(nb,),
            in_specs=[pl.BlockSpec((None,) + blk, lambda b, w_ref: (w_ref[0],) + whole(w_ref[1], b)),
                      pl.BlockSpec((None,) + blk, lambda b, w_ref: (w_ref[0],) + within(b)),
                      pl.BlockSpec((3,) + blk, lambda b, w_ref: (0,) + within(b))],
            out_specs=pl.BlockSpec(blk, lambda b, w_ref: whole(w_ref[1], b))),
        out_shape=jax.ShapeDtypeStruct((r, cdim), F32),
        compiler_params=_params("parallel"),
    )(where, grad, other, received)


def _join_start(name, halves):
    n = len(halves)

    def body(*refs):
        bufs, send_sems, recv_sems = refs[:n], refs[n], refs[n + 1]
        x, y, c, _ = _place()
        for i in range(n):
            mine = _half(bufs[i], c)
            _remote(mine, mine, send_sems.at[i], recv_sems.at[i], (x, y, 1 - c)).start()
        refs[-1][...] = jnp.zeros_like(refs[-1])

    dma = pltpu.SemaphoreType.DMA
    outs = pl.pallas_call(
        body, name=name,
        in_specs=[HBM] * n,
        out_shape=(dma((n,)), dma((n,)), *[pltpu.HBM(h.shape, F32) for h in halves], jax.ShapeDtypeStruct((8, LANES), F32)),
        out_specs=(SEM, SEM, *[HBM] * n, pl.BlockSpec(memory_space=pltpu.VMEM)),
        input_output_aliases={k: 2 + k for k in range(n)},
        compiler_params=pltpu.CompilerParams(has_side_effects=EFFECT),
    )(*[_in_hbm(h) for h in halves])
    return (outs[0], outs[1]), list(outs[2:2 + n]), outs[-1]


def _join_wait(name, started, after):
    sems, bufs, _ = started
    n = len(bufs)

    def body(*refs):
        bufs, send_sems, recv_sems = refs[:n], refs[n], refs[n + 1]
        x, y, c, _ = _place()
        for i in range(n):
            mine, theirs = _half(bufs[i], c), _half(bufs[i], 1 - c)
            _remote(mine, mine, send_sems.at[i], recv_sems.at[i], (x, y, 1 - c)).wait_send()
            _remote(theirs, theirs, send_sems.at[i], recv_sems.at[i], (x, y, 1 - c)).wait_recv()

    outs = pl.pallas_call(
        body, name=name,
        in_specs=[HBM] * n + [SEM, SEM] + [ANY] * len(after),
        out_shape=tuple(pltpu.HBM(b.shape, F32) for b in bufs),
        out_specs=tuple([HBM] * n),
        input_output_aliases={k: k for k in range(n)},
        compiler_params=pltpu.CompilerParams(has_side_effects=EFFECT),
    )(*bufs, sems[0], sems[1], *after)
    return list(outs)


BIG = ("w_in", "pool_w", "w_out", "xq_w", "xk_w", "xv_w", "xo_w", "w_up", "w_down")
GATHER_GROUPS = ((0, 1), (2, 3, 4, 5, 6), (7,), (8,))
SMALL = ("conv_w", "a_log", "dt_bias", "gdn_norm_w", "pool_scale", "ln1_g", "ln1_b", "ln2_g", "ln2_b", "ln3_g", "ln3_b")
ORDER = ("w_in", "conv_w", "a_log", "dt_bias", "gdn_norm_w", "pool_w", "pool_scale", "w_out", "ln1_g", "ln1_b",
         "xq_w", "xk_w", "xv_w", "xo_w", "ln2_g", "ln2_b", "w_up", "w_down", "ln3_g", "ln3_b")
LANES = 128


def _rows(flat_len):
    return -(-flat_len // LANES)


def _pack(pieces):
    out = []
    for p in pieces:
        flat = p.reshape(-1).astype(F32)
        out.append(jnp.pad(flat, (0, _rows(flat.shape[0]) * LANES - flat.shape[0])).reshape(-1, LANES))
    slab = jnp.concatenate(out, axis=0)
    return jnp.pad(slab, ((0, -slab.shape[0] % 8), (0, 0)))


def _unpack(slab, shapes):
    out, row = [], 0
    for shp in shapes:
        size = math.prod(shp)
        out.append(slab[row:row + _rows(size)].reshape(-1)[:size].reshape(shp))
        row += _rows(size)
    return out


TRANSPOSED = ("w_in",)


def _as2d(name, a):
    a = a[0]
    if name in TRANSPOSED:
        return jnp.swapaxes(a, 0, 1)
    return a.reshape(-1, a.shape[-1]) if a.ndim == 3 else a


def _from2d(name, a, shape):
    return (jnp.swapaxes(a, 0, 1) if name in TRANSPOSED else a).reshape(shape)


def kernel(x, mem, w_in, conv_w, a_log, dt_bias, gdn_norm_w, pool_w, pool_scale, w_out, ln1_g, ln1_b, xq_w, xk_w, xv_w, xo_w, ln2_g, ln2_b, w_up, w_down, ln3_g, ln3_b, loss_target, m_w_in, m_conv_w, m_a_log, m_dt_bias, m_gdn_norm_w, m_pool_w, m_pool_scale, m_w_out, m_ln1_g, m_ln1_b, m_xq_w, m_xk_w, m_xv_w, m_xo_w, m_ln2_g, m_ln2_b, m_w_up, m_w_down, m_ln3_g, m_ln3_b, v_w_in, v_conv_w, v_a_log, v_dt_bias, v_gdn_norm_w, v_pool_w, v_pool_scale, v_w_out, v_ln1_g, v_ln1_b, v_xq_w, v_xk_w, v_xv_w, v_xo_w, v_ln2_g, v_ln2_b, v_w_up, v_w_down, v_ln3_g, v_ln3_b):
    given = dict(locals())
    cx, cy, cc = lax.axis_index("x"), lax.axis_index("y"), lax.axis_index("c")
    me = 2 * cx + cy
    groups = pool_w.shape[1]
    cs = pool_w.shape[2]
    kk, conv_cols = conv_w.shape[1], conv_w.shape[2]
    core = cc.astype(jnp.int32).reshape(1)
    where = jnp.stack([me, cc]).astype(jnp.int32)

    conv_slab = jnp.zeros((kk, N_SHARD * conv_cols), F32)
    conv_slab = lax.dynamic_update_slice(conv_slab, conv_w[0] * (cc == 0).astype(F32), (0, me * conv_cols))
    wts = {"conv_w": _unpack(_all_reduce_small("gather_conv_w", _pack([conv_slab])), [conv_slab.shape])[0]}

    started = {}

    def start(name, idx, after, token=None):
        casts = [_after(token, _as2d(BIG[i], given[BIG[i]])).astype(BF16) for i in idx]
        sems, shards, lands, token = _gather_start(name, casts, after)
        for k, i in enumerate(idx):
            started[i] = (sems, k, shards[k], lands[k])
        return token

    token = start("gather_start_first", GATHER_GROUPS[0], wts["conv_w"])
    token = start("gather_start_rest", tuple(i for group in GATHER_GROUPS[1:] for i in group), token, token)

    def fetch(group, after):
        members = [started[i] for i in GATHER_GROUPS[group]]
        sems, idx = members[0][0], [m[1] for m in members]
        fwd, zones = _gather_forward(f"gather_forward_{group}", idx, [m[3] for m in members], sems, after)
        full = dict(zip([BIG[i] for i in GATHER_GROUPS[group]],
                        _gather_wait(f"gather_wait_{group}", idx, [m[2] for m in members], zones, sems, fwd)))
        out = {}
        for n, a in full.items():
            if n == "w_in":
                out["w_in_t"] = a
            elif n == "w_up":
                out["w_up3"] = a
            elif n == "pool_w":
                out[n] = a.reshape(N_SHARD, groups, cs, -1).transpose(1, 0, 2, 3).reshape(groups, N_SHARD * cs, -1)
            else:
                out[n] = a.reshape(-1, a.shape[-1])
        return out

    for n in ("a_log", "dt_bias", "gdn_norm_w", "pool_scale", "ln1_g", "ln1_b", "ln2_g", "ln2_b", "ln3_g", "ln3_b"):
        wts[n] = given[n]
    wts.update(fetch(0, token))

    def start_swap(group, grads):
        names, blocks = [], []
        for n, g in grads.items():
            if n == "pool_w":
                g = g.reshape(groups, N_SHARD, cs, -1).transpose(1, 0, 2, 3).reshape(N_SHARD, groups * cs, -1)
            elif g.ndim == 2:
                g = g.reshape(N_SHARD, -1, g.shape[-1])
            names.append({"w_in_t": "w_in", "w_up3": "w_up"}.get(n, n))
            blocks.append(g)
        zones = [((N_SHARD,) + _half_shape(b.shape[1], b.shape[2]), F32) for b in blocks]
        swap = _exchange_start(f"grad_swap_start_{group}", _swap_copies, blocks, zones, N_SHARD)
        return {"group": group, "names": names, "swap": swap, "token": swap[3]}

    def start_send(state, after):
        group, names = state["group"], state["names"]
        state["blocks"] = state["swap"][1]
        state["others"] = _exchange_wait(f"grad_swap_wait_{group}", _swap_copies, state["swap"], after)
        partials = [_chip_partial("chip_partial_" + n, gb, ob, core)
                    for n, gb, ob in zip(names, state["blocks"], state["others"])]
        zones = [((3,) + p.shape[1:], BF16) for p in partials]
        state["send"] = _exchange_start(f"grad_send_start_{group}", _partial_copies, partials, zones, 3)
        state["token"] = state["send"][3]

    grad, delta, new_m, new_v = {}, {}, {}, {}

    def start_join(state, after):
        group, names = state["group"], state["names"]
        received = _exchange_wait(f"grad_send_wait_{group}", _partial_copies, state["send"], after)
        halves = [_reduce_own("reduce_own_" + n, gb, ob, rb, where)
                  for n, gb, ob, rb in zip(names, state["blocks"], state["others"], received)]
        state["join"] = _join_start(f"grad_join_start_{group}", halves)
        return state["join"][2]

    def finish_reduce(state, after):
        group, names = state["group"], state["names"]
        for n, g in zip(names, _join_wait(f"grad_join_wait_{group}", state["join"], after)):
            shp = given[n].shape
            d2, m2, v2 = _adamw("adamw_" + n, _as2d(n, given[n]), g, _as2d(n, given["m_" + n]), _as2d(n, given["v_" + n]))
            grad[n], delta[n], new_m[n], new_v[n] = (_from2d(n, a, shp) for a in (g, d2, m2, v2))
        return d2

    step = _local_step(x[0], mem[0], loss_target[0], wts, token)
    pending = {}
    request = next(step)
    while True:
        try:
            kind, group, payload = request
            if kind == "weights":
                request = step.send(fetch(group, payload))
            elif kind == "grads":
                pending[group] = start_swap(group, payload)
                request = step.send(pending[group]["token"])
            else:
                start_send(pending[group], [payload])
                request = step.send(pending[group]["token"])
        except StopIteration as stop:
            loss_row, grad_x, g = stop.value
            break

    after = [pending[2]["token"], grad_x]
    for group in (0, 1):
        after = [start_join(pending[group], after)]
    for group in (0, 1):
        after = [finish_reduce(pending[group], after)]
    after = [finish_reduce(pending[2], [start_join(pending[2], after)])]

    small_names = ("a_log", "dt_bias", "gdn_norm_w", "pool_scale", "ln1_g", "ln1_b", "ln2_g", "ln2_b", "ln3_g", "ln3_b")
    pieces = [g["conv_w"]] + [g[n] for n in small_names] + [loss_row[:, :1]]
    shapes = [p.shape for p in pieces]
    summed = _unpack(_all_reduce_small("all_reduce_small", _pack(pieces), after[0]), shapes)
    gsmall = dict(zip(small_names, summed[1:-1]))
    gsmall["conv_w"] = lax.dynamic_slice(summed[0], (0, me * conv_cols), (kk, conv_cols))
    loss = summed[-1][0, 0]

    sshapes = [given[n].shape for n in SMALL]
    slabs = [_pack([given[p + n] for n in SMALL]) for p in ("", "m_", "v_")]
    gslab = _pack([gsmall[n] for n in SMALL])
    outs = _adamw("adamw_small", slabs[0], gslab, slabs[1], slabs[2])
    for dst, slab in zip((delta, new_m, new_v), outs):
        dst.update(zip(SMALL, _unpack(slab, sshapes)))
    for n in SMALL:
        grad[n] = gsmall[n].reshape(given[n].shape)

    return (loss, grad_x[None], *[grad[n] for n in ORDER], *[delta[n] for n in ORDER],
            *[new_m[n] for n in ORDER], *[new_v[n] for n in ORDER])
```

```python
import functools
import math

import jax
import jax.numpy as jnp
from jax import lax
from jax.experimental import pallas as pl
from jax.experimental.pallas import tpu as pltpu

F32 = jnp.float32
BF16 = jnp.bfloat16
MESH = pl.DeviceIdType.MESH

HEAD_DIM = 128
CHUNK = 64
POOL_WINDOWS = (2, 4, 8, 16)
XATTN_HEADS = 4
ALPHA = 2.0 ** 0.25
LN_EPS = 1e-5
NORM_EPS = 1e-6
ADAM_LR, ADAM_B1, ADAM_B2, ADAM_EPS, ADAM_WD, ADAM_STEP = 0.001, 0.9, 0.999, 1e-08, 0.01, 10
N_SHARD = 4
VMEM_LIMIT = 56 * 1024 * 1024
K_STEPS = (2048, 1024, 512, 256, 128)


def _params(*sem):
    return pltpu.CompilerParams(dimension_semantics=sem, vmem_limit_bytes=VMEM_LIMIT)


def _bdot(a, b, ta=False, tb=False):
    dims = (((0 if ta else 1,), (1 if tb else 0,)), ((), ()))
    return lax.dot_general(a.astype(BF16), b.astype(BF16), dims, preferred_element_type=F32)


def _sigmoid(x):
    return 1.0 / (1.0 + jnp.exp(-x))


def _matmul(name, a, b, *, ta=False, tb=False, tm, tn, tk, extra=(), outs, epilogue, b_blocks=None,
            sequential=False, n_used=None, k_used=None, n_outer=False):
    m, k_dim = (a.shape[1], a.shape[0]) if ta else a.shape
    if b_blocks and tb:
        n = b.shape[1]
        k_dim = b.shape[0] * b.shape[2]
        per = b.shape[2] // tk
        b_spec = pl.BlockSpec((None, tn, tk), lambda i, j, k: (k // per, j, k % per))
    elif b_blocks:
        n = b.shape[0] * b.shape[2]
        per = b.shape[2] // tn
        b_spec = pl.BlockSpec((None, tk, tn), lambda i, j, k: (j // per, k, j % per))
    elif tb:
        n = b.shape[0]
        b_spec = pl.BlockSpec((tn, tk), lambda i, j, k: (j, k))
    else:
        n = b.shape[1]
        b_spec = pl.BlockSpec((tk, tn), lambda i, j, k: (k, j))
    n, k_dim = n_used or n, k_used or k_dim
    assert m % tm == 0 and n % tn == 0 and k_dim % tk == 0, (name, m, n, k_dim, tm, tn, tk)
    nk = k_dim // tk
    a_spec = pl.BlockSpec((tk, tm), lambda i, j, k: (k, i)) if ta else pl.BlockSpec((tm, tk), lambda i, j, k: (i, k))
    n_extra, n_out = len(extra), len(outs)

    def wrap(index_map):
        return lambda i, j, k: index_map(i, j)

    def spec(block, index_map):
        if n_outer:
            return pl.BlockSpec(block, lambda j, i, k: index_map(i, j, k))
        return pl.BlockSpec(block, index_map)

    row_axis = 1 if n_outer else 0

    def body_one_step(*refs):
        ex = refs[2:2 + n_extra]
        out = refs[2 + n_extra:2 + n_extra + n_out]
        epilogue(_bdot(refs[0][...], refs[1][...], ta, tb), ex, out, pl.program_id(row_axis))

    def body(*refs):
        a_ref, b_ref = refs[0], refs[1]
        ex = refs[2:2 + n_extra]
        out = refs[2 + n_extra:2 + n_extra + n_out]
        acc = refs[-1]
        i, k = pl.program_id(row_axis), pl.program_id(2)
        part = _bdot(a_ref[...], b_ref[...], ta, tb)

        @pl.when(k == 0)
        def _():
            acc[...] = part

        @pl.when(jnp.logical_and(k > 0, k < nk - 1))
        def _():
            acc[...] += part

        @pl.when(k == nk - 1)
        def _():
            epilogue(acc[...] + part, ex, out, i)

    sem = ("arbitrary",) * 3 if sequential else ("parallel", "parallel", "arbitrary")
    res = pl.pallas_call(
        body_one_step if nk == 1 else body, name=name,
        grid=(n // tn, m // tm, nk) if n_outer else (m // tm, n // tn, nk),
        in_specs=[spec(a_spec.block_shape, a_spec.index_map), spec(b_spec.block_shape, b_spec.index_map)]
        + [spec(bs, wrap(im)) for _, bs, im in extra],
        out_specs=[spec(bs, wrap(im)) for _, bs, im in outs],
        out_shape=[s for s, _, _ in outs],
        scratch_shapes=[] if nk == 1 else [pltpu.VMEM((tm, tn), F32)],
        compiler_params=_params(*sem),
    )(a, b, *[x for x, _, _ in extra])
    return res


def _tile(i, j):
    return (i, j)


def _plain(name, a, b, *, ta=False, tb=False, tm, tn, tk, out_dtype, b_blocks=None, out3=None, n_used=None,
           n_outer=False):
    m = a.shape[1] if ta else a.shape[0]
    if b_blocks:
        n = b.shape[1] if tb else b.shape[0] * b.shape[2]
    else:
        n = n_used or (b.shape[0] if tb else b.shape[1])

    def epi(acc, ex, out, i):
        out[0][...] = acc.astype(out_dtype)

    if out3:
        per = (n // out3) // tn
        spec = (jax.ShapeDtypeStruct((out3, m, n // out3), out_dtype), (None, tm, tn),
                lambda i, j: (j // per, i, j % per))
    else:
        spec = (jax.ShapeDtypeStruct((m, n), out_dtype), (tm, tn), _tile)
    return _matmul(name, a, b, ta=ta, tb=tb, tm=tm, tn=tn, tk=tk, outs=[spec], epilogue=epi,
                   b_blocks=b_blocks, n_used=n_used, n_outer=n_outer)[0]


def _ln_forward(name, a, b, res, gamma, beta, *, tm, tk, want_h=True):
    m, n = res.shape

    def epi(acc, ex, out, i):
        u = ALPHA * ex[0][...] + acc
        mu = jnp.mean(u, axis=-1, keepdims=True)
        xc = u - mu
        var = jnp.mean(xc * xc, axis=-1, keepdims=True)
        rstd = lax.rsqrt(var + LN_EPS)
        xhat = xc * rstd
        out[-2][...] = xhat
        out[-1][...] = rstd
        if want_h:
            h = xhat * ex[1][...] + ex[2][...]
            out[0][...] = h
            out[1][...] = h.astype(BF16)

    row = lambda i, j: (i, 0)
    vec = lambda i, j: (0, 0)
    outs = [(jax.ShapeDtypeStruct((m, n), F32), (tm, n), row), (jax.ShapeDtypeStruct((m, n), BF16), (tm, n), row),
            (jax.ShapeDtypeStruct((m, n), F32), (tm, n), row), (jax.ShapeDtypeStruct((m, 1), F32), (tm, 1), row)]
    return _matmul(
        name, a, b, tm=tm, tn=n, tk=tk,
        extra=[(res, (tm, n), row), (gamma, (1, n), vec), (beta, (1, n), vec)],
        outs=outs if want_h else outs[2:], epilogue=epi)


def _ln_backward_math(dy, xhat, rstd, gamma):
    dxhat = dy * gamma
    m1 = jnp.mean(dxhat, axis=-1, keepdims=True)
    m2 = jnp.mean(dxhat * xhat, axis=-1, keepdims=True)
    du = rstd * (dxhat - m1 - xhat * m2)
    return du, jnp.sum(dy * xhat, axis=0, keepdims=True), jnp.sum(dy, axis=0, keepdims=True)


def _ln_backward(name, a, b, dres, xhat, rstd, gamma, *, tm, tk, b_blocks=None, tb=True):
    m, n = dres.shape

    def epi(acc, ex, out, i):
        dy = acc + ALPHA * ex[0][...]
        du, dg, db = _ln_backward_math(dy, ex[1][...], ex[2][...], ex[3][...])
        out[0][...] = du
        out[1][...] = du.astype(BF16)
        first = i == 0

        @pl.when(first)
        def _():
            out[2][...] = dg
            out[3][...] = db

        @pl.when(jnp.logical_not(first))
        def _():
            out[2][...] += dg
            out[3][...] += db

    row = lambda i, j: (i, 0)
    vec = lambda i, j: (0, 0)
    return _matmul(
        name, a, b, tb=tb, tm=tm, tn=n, tk=tk, b_blocks=b_blocks, sequential=True,
        extra=[(dres, (tm, n), row), (xhat, (tm, n), row), (rstd, (tm, 1), row), (gamma, (1, n), vec)],
        outs=[(jax.ShapeDtypeStruct((m, n), F32), (tm, n), row),
              (jax.ShapeDtypeStruct((m, n), BF16), (tm, n), row),
              (jax.ShapeDtypeStruct((1, n), F32), (1, n), vec),
              (jax.ShapeDtypeStruct((1, n), F32), (1, n), vec)],
        epilogue=epi)


def _shift_down(x, k):
    row = lax.broadcasted_iota(jnp.int32, x.shape, 0)
    return jnp.where(row >= k, pltpu.roll(x, k, axis=0), 0.0)


def _shift_up(x, k):
    t = x.shape[0]
    row = lax.broadcasted_iota(jnp.int32, x.shape, 0)
    return jnp.where(row < t - k, pltpu.roll(x, t - k, axis=0), 0.0)


def _conv_silu_norm(x, w, normalise):
    kk = w.shape[0]
    c = x * w[kk - 1:kk, :]
    for j in range(kk - 1):
        c = c + _shift_down(x, kk - 1 - j) * w[j:j + 1, :]
    sg = _sigmoid(c)
    s = c * sg
    r = lax.rsqrt(jnp.sum(s * s, axis=-1, keepdims=True) + NORM_EPS)
    y = jnp.where(normalise, s * r, s)
    return c, sg, s, r, y


def _gdn_pre(proj, conv_w, heads):
    t = proj.shape[0]
    kk = conv_w.shape[0]

    def body(x_ref, w_ref, o_ref):
        normalise = pl.program_id(0) < 2
        o_ref[...] = _conv_silu_norm(x_ref[...], w_ref[...], normalise)[4]

    col = lambda s, h: (0, s * heads + h)
    return pl.pallas_call(
        body, name="gdn_pre", grid=(3, heads),
        in_specs=[pl.BlockSpec((t, HEAD_DIM), col), pl.BlockSpec((kk, HEAD_DIM), col)],
        out_specs=pl.BlockSpec((t, HEAD_DIM), col),
        out_shape=jax.ShapeDtypeStruct((t, 3 * heads * HEAD_DIM), F32),
        compiler_params=_params("parallel", "parallel"),
    )(proj, conv_w)


def _gdn_pre_backward(proj, conv_w, dqkv, heads):
    t = proj.shape[0]
    kk = conv_w.shape[0]

    def body(x_ref, w_ref, dy_ref, dx_ref, dw_ref):
        normalise = pl.program_id(0) < 2
        x = x_ref[...]
        w = w_ref[...]
        dy = dy_ref[...]
        c, sg, s, r, y = _conv_silu_norm(x, w, normalise)
        ds_norm = r * (dy - y * jnp.sum(dy * y, axis=-1, keepdims=True))
        ds = jnp.where(normalise, ds_norm, dy)
        dc = ds * (sg * (1.0 + c * (1.0 - sg)))
        dx = dc * w[kk - 1:kk, :]
        rows = [None] * kk
        rows[kk - 1] = jnp.sum(dc * x, axis=0, keepdims=True)
        for j in range(kk - 1):
            lag = kk - 1 - j
            dx = dx + _shift_up(dc, lag) * w[j:j + 1, :]
            rows[j] = jnp.sum(dc * _shift_down(x, lag), axis=0, keepdims=True)
        dx_ref[...] = dx.astype(BF16)
        dw_ref[...] = jnp.concatenate(rows, axis=0)

    col = lambda s, h: (0, s * heads + h)
    return pl.pallas_call(
        body, name="gdn_pre_bwd", grid=(3, heads),
        in_specs=[pl.BlockSpec((t, HEAD_DIM), col), pl.BlockSpec((kk, HEAD_DIM), col),
                  pl.BlockSpec((t, HEAD_DIM), col)],
        out_specs=[pl.BlockSpec((t, HEAD_DIM), col), pl.BlockSpec((kk, HEAD_DIM), col)],
        out_shape=[jax.ShapeDtypeStruct((t, 3 * heads * HEAD_DIM), BF16),
                   jax.ShapeDtypeStruct((kk, 3 * heads * HEAD_DIM), F32)],
        compiler_params=_params("parallel", "parallel"),
    )(proj, conv_w, dqkv)


def _gate_vectors(a_log, dt_bias, heads):
    pad = lambda v: jnp.pad(v.astype(F32), ((0, 0), (heads, HEAD_DIM - 2 * heads)))
    return pad(jnp.exp(a_log.astype(F32))), pad(dt_bias)


def _softplus(x):
    return jnp.maximum(x, 0.0) + jnp.log(1.0 + jnp.exp(-jnp.abs(x)))


def _gates_epilogue(heads):
    def epi(acc, ex, out, i):
        lane = lax.broadcasted_iota(jnp.int32, acc.shape, 1)
        beta = _sigmoid(acc)
        g = -ex[0][...] * _softplus(acc + ex[1][...])
        out[0][...] = acc
        out[1][...] = jnp.where(lane < heads, beta, jnp.where(lane < 2 * heads, g, 0.0))
    return epi


def _gates_backward(ba, bg, dbg, ea, dtb, heads):
    t = ba.shape[0]

    def body(ba_ref, bg_ref, d_ref, ea_ref, dt_ref, dba_ref, dal_ref, ddt_ref):
        lane = lax.broadcasted_iota(jnp.int32, (t, HEAD_DIM), 1)
        bgv = bg_ref[...]
        d = d_ref[...]
        db = d * bgv * (1.0 - bgv)
        da = -d * ea_ref[...] * _sigmoid(ba_ref[...] + dt_ref[...])
        is_g = jnp.logical_and(lane >= heads, lane < 2 * heads)
        dba = jnp.where(lane < heads, db, jnp.where(is_g, da, 0.0))
        dba_ref[...] = dba.astype(BF16)
        dal_ref[...] = jnp.sum(jnp.where(is_g, d * bgv, 0.0), axis=0, keepdims=True)
        ddt_ref[...] = jnp.sum(jnp.where(is_g, da, 0.0), axis=0, keepdims=True)

    full = pl.BlockSpec((t, HEAD_DIM), lambda: (0, 0))
    vec = pl.BlockSpec((1, HEAD_DIM), lambda: (0, 0))
    return pl.pallas_call(
        body, name="gates_bwd", grid=(),
        in_specs=[full, full, full, vec, vec], out_specs=[full, vec, vec],
        out_shape=[jax.ShapeDtypeStruct((t, HEAD_DIM), BF16), jax.ShapeDtypeStruct((1, HEAD_DIM), F32),
                   jax.ShapeDtypeStruct((1, HEAD_DIM), F32)],
        compiler_params=pltpu.CompilerParams(vmem_limit_bytes=VMEM_LIMIT),
    )(ba, bg, dbg, ea, dtb)


class _Chunk:
    pass


def _split2(x):
    hi = x.astype(BF16)
    return hi, (x - hi.astype(F32)).astype(BF16)


def _split3(x):
    hi = x.astype(BF16)
    rest = x - hi.astype(F32)
    mid = rest.astype(BF16)
    return hi, mid, (rest - mid.astype(F32)).astype(BF16)


def _dot_mask(mask, x, ta=False):
    hi, mid, lo = _split3(x)
    return _bdot(mask, hi, ta=ta) + (_bdot(mask, mid, ta=ta) + _bdot(mask, lo, ta=ta))


def _transpose_by_identity(x):
    r = x.shape[0]
    eye = (lax.broadcasted_iota(jnp.int32, (r, r), 0) == lax.broadcasted_iota(jnp.int32, (r, r), 1)).astype(BF16)
    hi, mid, lo = _split3(x)
    return _bdot(hi, eye, ta=True) + (_bdot(mid, eye, ta=True) + _bdot(lo, eye, ta=True))


def _dot22(a, b, ta=False, tb=False):
    ah, al = _split2(a)
    bh, bl = _split2(b)
    return _bdot(ah, bh, ta, tb) + (_bdot(ah, bl, ta, tb) + _bdot(al, bh, ta, tb))


def _chunk_gates(bg, heads):
    n = CHUNK
    row = lax.broadcasted_iota(jnp.int32, (n, n), 0)
    col = lax.broadcasted_iota(jnp.int32, (n, n), 1)
    lane = lax.broadcasted_iota(jnp.int32, bg.shape, 1)
    graw = jnp.where(jnp.logical_and(lane >= heads, lane < 2 * heads), bg, 0.0)
    gc = _dot_mask((row >= col).astype(BF16), graw)
    return gc, _transpose_by_identity(gc)


def _in_lockstep(generators):
    results = [None] * len(generators)
    live = list(enumerate(generators))
    while live:
        still = []
        for i, gen in live:
            try:
                next(gen)
                still.append((i, gen))
            except StopIteration as stop:
                results[i] = stop.value
        live = still
    return results


def _chunk_local(q, k, v, beta, gc, grow):
    c = _Chunk()
    n = CHUNK
    row = lax.broadcasted_iota(jnp.int32, (n, n), 0)
    col = lax.broadcasted_iota(jnp.int32, (n, n), 1)
    c.tri = row >= col
    c.strict = row > col
    eye = row == col
    c.gcb = jnp.broadcast_to(gc, (n, HEAD_DIM))
    c.decay = jnp.where(c.tri, jnp.exp(jnp.where(c.tri, gc - grow, 0.0)), 0.0)
    c.eg = jnp.exp(c.gcb)
    glast = c.gcb[n - 1:n, :]
    c.egl = jnp.exp(glast)
    c.ekl = jnp.exp(glast - c.gcb)
    c.beta = beta
    c.q = q * (HEAD_DIM ** -0.5)
    c.k = k
    c.v = v
    c.kb = k * beta
    c.vb = v * beta
    c.kg = c.kb * c.eg
    both = _bdot(jnp.concatenate([c.kb, c.q], axis=0), k, tb=True)
    yield
    c.L = jnp.where(c.strict, both[:n] * c.decay, 0.0)
    c.A = jnp.where(c.tri, both[n:] * c.decay, 0.0)
    x = -c.L
    tinv = eye.astype(F32) + x
    p = _dot22(x, x)
    yield
    for _ in range(int(math.log2(n)) - 2):
        both = _dot22(jnp.concatenate([p, tinv], axis=0), p)
        yield
        p, tinv = both[:n], tinv + both[n:]
    c.T = tinv + _dot22(tinv, p)
    yield
    tinv = c.T
    uw = _dot22(tinv, jnp.concatenate([c.vb, c.kg], axis=1))
    yield
    c.u, c.w = uw[:, :HEAD_DIM], uw[:, HEAD_DIM:]
    c.qg = c.q * c.eg
    c.kdec = k * c.ekl
    return c


def _gdn_core(qkv, bg, heads):
    t = qkv.shape[0]
    nchunk = t // CHUNK

    gw = heads * HEAD_DIM

    def body(qkv_ref, bg_ref, o_ref, s_ref, state):
        @pl.when(pl.program_id(0) == 0)
        def _():
            state[...] = jnp.zeros_like(state)

        bg_v = bg_ref[...]
        gc_all, gc_rows = _chunk_gates(bg_v, heads)
        def one_head(h):
            col = lambda s: pl.ds(s * gw + h * HEAD_DIM, HEAD_DIM)
            c = yield from _chunk_local(qkv_ref[:, col(0)], qkv_ref[:, col(1)], qkv_ref[:, col(2)], bg_v[:, h:h + 1],
                                        gc_all[:, heads + h:heads + h + 1], gc_rows[heads + h:heads + h + 1, :])
            s0 = state[h]
            v_new = c.u - _bdot(c.w, s0)
            yield
            o = _bdot(c.qg, s0) + _bdot(c.A, v_new)
            return s0, o, s0 * c.egl + _bdot(c.kdec, v_new, ta=True)

        results = _in_lockstep([one_head(h) for h in range(heads)])
        for h, (s0, o, s1) in enumerate(results):
            s_ref[h, 0] = s0
            o_ref[:, pl.ds(h * HEAD_DIM, HEAD_DIM)] = o
            state[h] = s1

    return pl.pallas_call(
        body, name="gdn_core", grid=(nchunk,),
        in_specs=[pl.BlockSpec((CHUNK, 3 * gw), lambda n: (n, 0)), pl.BlockSpec((CHUNK, HEAD_DIM), lambda n: (n, 0))],
        out_specs=[pl.BlockSpec((CHUNK, gw), lambda n: (n, 0)),
                   pl.BlockSpec((heads, 1, HEAD_DIM, HEAD_DIM), lambda n: (0, n, 0, 0))],
        out_shape=[jax.ShapeDtypeStruct((t, gw), F32),
                   jax.ShapeDtypeStruct((heads, nchunk, HEAD_DIM, HEAD_DIM), F32)],
        scratch_shapes=[pltpu.VMEM((heads, HEAD_DIM, HEAD_DIM), F32)],
        compiler_params=_params("arbitrary"),
    )(qkv, bg)


def _gdn_core_backward(qkv, bg, states, do, heads):
    t = qkv.shape[0]
    nchunk = t // CHUNK
    n = CHUNK

    def one_head(chunk_local, s0, d_out, ds1):
        c = yield from chunk_local
        v_new = c.u - _bdot(c.w, s0)
        dqg = _bdot(d_out, s0, tb=True)
        ds0 = _bdot(c.qg, d_out, ta=True) + ds1 * c.egl
        dv_new = _bdot(c.A, d_out, ta=True) + _bdot(c.kdec, ds1)
        yield
        dA = jnp.where(c.tri, _bdot(d_out, v_new, tb=True), 0.0)
        dkdec = _bdot(v_new, ds1, tb=True)
        dgl = jnp.sum(jnp.sum(ds1 * s0, axis=1, keepdims=True), axis=0, keepdims=True) * c.egl
        dw = -_bdot(dv_new, s0, tb=True)
        ds0 = ds0 - _bdot(c.w, dv_new, ta=True)
        yield
        both = _dot22(c.T, jnp.concatenate([dv_new, dw], axis=1), ta=True)
        yield
        dvb, dkg = both[:, :HEAD_DIM], both[:, HEAD_DIM:]
        dL = jnp.where(c.strict, -(_bdot(dvb, c.u, tb=True) + _bdot(dkg, c.w, tb=True)), 0.0)
        yield
        dm1 = dL * c.decay
        dkb = _bdot(dm1, c.k) + dkg * c.eg
        dk = _bdot(dm1, c.kb, ta=True)
        dm2 = dA * c.decay
        dq = _bdot(dm2, c.k) + dqg * c.eg
        dk = dk + _bdot(dm2, c.q, ta=True) + dkdec * c.ekl + dkb * c.beta
        pm = dL * c.L + dA * c.A
        ones = jnp.ones((n, HEAD_DIM), BF16)
        pm_hi, pm_lo = _split2(pm)
        colsum = _bdot(pm_hi, ones, ta=True) + _bdot(pm_lo, ones, ta=True)
        tk_ = jnp.sum(dkdec * c.kdec, axis=1, keepdims=True)
        dgc = (jnp.sum(pm, axis=1, keepdims=True) - colsum
               + jnp.sum(dqg * c.qg, axis=1, keepdims=True)
               - tk_
               + jnp.sum(dkg * c.kg, axis=1, keepdims=True))
        dgl = dgl + jnp.sum(tk_, axis=0, keepdims=True)
        rowi = lax.broadcasted_iota(jnp.int32, (n, HEAD_DIM), 0)
        dgc = dgc + jnp.where(rowi == n - 1, dgl, 0.0)
        dbeta = jnp.sum(dkb * c.k, axis=1, keepdims=True) + jnp.sum(dvb * c.v, axis=1, keepdims=True)
        return dq * (HEAD_DIM ** -0.5), dk, dvb * c.beta, dbeta, dgc, ds0

    gw = heads * HEAD_DIM

    def body(qkv_ref, bg_ref, s_ref, do_ref, dqkv_ref, dbg_ref, dstate):
        @pl.when(pl.program_id(0) == 0)
        def _():
            dstate[...] = jnp.zeros_like(dstate)

        bg_v = bg_ref[...]
        gc_all, gc_rows = _chunk_gates(bg_v, heads)
        lane = lax.broadcasted_iota(jnp.int32, (n, HEAD_DIM), 1)
        dgates = jnp.zeros((n, HEAD_DIM), F32)
        chains = []
        for h in range(heads):
            col = lambda s: pl.ds(s * gw + h * HEAD_DIM, HEAD_DIM)
            c = _chunk_local(qkv_ref[:, col(0)], qkv_ref[:, col(1)], qkv_ref[:, col(2)], bg_v[:, h:h + 1],
                             gc_all[:, heads + h:heads + h + 1], gc_rows[heads + h:heads + h + 1, :])
            chains.append(one_head(c, s_ref[h, 0], do_ref[:, pl.ds(h * HEAD_DIM, HEAD_DIM)], dstate[h]))
        results = _in_lockstep(chains)
        for h, (dq, dk, dv, dbeta, dgc, ds0) in enumerate(results):
            dgates = jnp.where(lane == h, dbeta, jnp.where(lane == heads + h, dgc, dgates))
        for h, (dq, dk, dv, dbeta, dgc, ds0) in enumerate(results):
            dqkv_ref[:, pl.ds(h * HEAD_DIM, HEAD_DIM)] = dq
            dqkv_ref[:, pl.ds(gw + h * HEAD_DIM, HEAD_DIM)] = dk
            dqkv_ref[:, pl.ds(2 * gw + h * HEAD_DIM, HEAD_DIM)] = dv
            dstate[h] = ds0
        row = lax.broadcasted_iota(jnp.int32, (n, n), 0)
        colm = lax.broadcasted_iota(jnp.int32, (n, n), 1)
        draw = _dot_mask((row >= colm).astype(BF16), dgates, ta=True)
        dbg_ref[...] = jnp.where(lane < heads, dgates, draw)

    last = nchunk - 1
    return pl.pallas_call(
        body, name="gdn_core_bwd", grid=(nchunk,),
        in_specs=[pl.BlockSpec((CHUNK, 3 * gw), lambda i: (last - i, 0)),
                  pl.BlockSpec((CHUNK, HEAD_DIM), lambda i: (last - i, 0)),
                  pl.BlockSpec((heads, 1, HEAD_DIM, HEAD_DIM), lambda i: (0, last - i, 0, 0)),
                  pl.BlockSpec((CHUNK, gw), lambda i: (last - i, 0))],
        out_specs=[pl.BlockSpec((CHUNK, 3 * gw), lambda i: (last - i, 0)),
                   pl.BlockSpec((CHUNK, HEAD_DIM), lambda i: (last - i, 0))],
        out_shape=[jax.ShapeDtypeStruct((t, 3 * gw), F32), jax.ShapeDtypeStruct((t, HEAD_DIM), F32)],
        scratch_shapes=[pltpu.VMEM((heads, HEAD_DIM, HEAD_DIM), F32)],
        compiler_params=_params("arbitrary"),
    )(qkv, bg, states, do)


def _gdn_post(o, proj, z_col0, norm_w, heads, tt):
    t = o.shape[0]
    zb = z_col0 // HEAD_DIM

    def body(o_ref, z_ref, w_ref, out_ref):
        ov = o_ref[...]
        z = z_ref[...]
        rms = lax.rsqrt(jnp.mean(ov * ov, axis=-1, keepdims=True) + NORM_EPS)
        out_ref[...] = (ov * rms * w_ref[...] * (z * _sigmoid(z))).astype(BF16)

    return pl.pallas_call(
        body, name="gdn_post", grid=(t // tt, heads),
        in_specs=[pl.BlockSpec((tt, HEAD_DIM), lambda i, h: (i, h)),
                  pl.BlockSpec((tt, HEAD_DIM), lambda i, h: (i, zb + h)),
                  pl.BlockSpec((1, HEAD_DIM), lambda i, h: (0, 0))],
        out_specs=pl.BlockSpec((tt, HEAD_DIM), lambda i, h: (i, h)),
        out_shape=jax.ShapeDtypeStruct((t, heads * HEAD_DIM), BF16),
        compiler_params=_params("parallel", "parallel"),
    )(o, proj, norm_w)


def _gdn_post_backward(dcat, o, proj, z_col0, norm_w, heads, tt):
    t = o.shape[0]
    zb = z_col0 // HEAD_DIM

    def body(d_ref, o_ref, z_ref, w_ref, do_ref, dz_ref, dw_ref):
        d = d_ref[...]
        ov = o_ref[...]
        z = z_ref[...]
        w = w_ref[...]
        rms = lax.rsqrt(jnp.mean(ov * ov, axis=-1, keepdims=True) + NORM_EPS)
        ohat = ov * rms
        sg = _sigmoid(z)
        gate = z * sg
        dz_ref[...] = (d * ohat * w * (sg * (1.0 + z * (1.0 - sg)))).astype(BF16)
        don = d * gate
        dohat = don * w
        do_ref[...] = rms * (dohat - ohat * jnp.mean(dohat * ohat, axis=-1, keepdims=True))
        dw = jnp.sum(don * ohat, axis=0, keepdims=True)
        first = jnp.logical_and(pl.program_id(0) == 0, pl.program_id(1) == 0)

        @pl.when(first)
        def _():
            dw_ref[...] = dw

        @pl.when(jnp.logical_not(first))
        def _():
            dw_ref[...] += dw

    blk = pl.BlockSpec((tt, HEAD_DIM), lambda i, h: (i, h))
    return pl.pallas_call(
        body, name="gdn_post_bwd", grid=(t // tt, heads),
        in_specs=[blk, blk, pl.BlockSpec((tt, HEAD_DIM), lambda i, h: (i, zb + h)),
                  pl.BlockSpec((1, HEAD_DIM), lambda i, h: (0, 0))],
        out_specs=[blk, blk, pl.BlockSpec((1, HEAD_DIM), lambda i, h: (0, 0))],
        out_shape=[jax.ShapeDtypeStruct((t, heads * HEAD_DIM), F32),
                   jax.ShapeDtypeStruct((t, heads * HEAD_DIM), BF16),
                   jax.ShapeDtypeStruct((1, HEAD_DIM), F32)],
        compiler_params=_params("arbitrary", "arbitrary"),
    )(dcat, o, proj, norm_w)


def _pool_select(levels, group):
    out = levels[-1]
    for gi in range(len(levels) - 2, -1, -1):
        out = jnp.where(group == gi, levels[gi], out)
    return out


def _pool_counts(t, width, group):
    pos = lax.broadcasted_iota(jnp.int32, (t, width), 0)
    win = jnp.left_shift(2, group)
    return jnp.minimum(pos + 1, win).astype(F32)


def _pooled(p, group):
    levels, s, step = [], p, 1
    for _ in POOL_WINDOWS:
        s = s + _shift_down(s, step)
        levels.append(s)
        step *= 2
    cnt = _pool_counts(p.shape[0], p.shape[1], group)
    return _pool_select(levels, group) / cnt - p, cnt


def _pool_forward(proj, p_col0, pool_w, pool_scale):
    t = proj.shape[0]
    groups, cg, _ = pool_w.shape
    pb = p_col0 // cg

    def body(p_ref, w_ref, s_ref, o_ref):
        pooled, _ = _pooled(p_ref[...], pl.program_id(0))
        o_ref[...] = (_bdot(pooled, w_ref[0]) * s_ref[...]).astype(BF16)

    return pl.pallas_call(
        body, name="pool_fwd", grid=(groups,),
        in_specs=[pl.BlockSpec((t, cg), lambda g: (0, pb + g)), pl.BlockSpec((1, cg, cg), lambda g: (g, 0, 0)),
                  pl.BlockSpec((1, cg), lambda g: (0, g))],
        out_specs=pl.BlockSpec((t, cg), lambda g: (0, g)),
        out_shape=jax.ShapeDtypeStruct((t, groups * cg), BF16),
        compiler_params=_params("parallel"),
    )(proj, pool_w, pool_scale)


def _pool_backward(dcat, d_col0, proj, p_col0, pool_w, pool_scale):
    t = proj.shape[0]
    groups, cg, _ = pool_w.shape
    pb = p_col0 // cg
    db = d_col0 // cg

    def body(d_ref, p_ref, w_ref, s_ref, dp_ref, dw_ref, ds_ref):
        group = pl.program_id(0)
        pooled, cnt = _pooled(p_ref[...], group)
        w = w_ref[0]
        d = d_ref[...]
        mixed = _bdot(pooled, w)
        ds_ref[...] = jnp.sum(d * mixed, axis=0, keepdims=True)
        dmixed = d * s_ref[...]
        dw_ref[0] = _bdot(pooled, dmixed, ta=True)
        dpooled = _bdot(dmixed, w, tb=True)
        levels, s, step = [], dpooled / cnt, 1
        for _ in POOL_WINDOWS:
            s = s + _shift_up(s, step)
            levels.append(s)
            step *= 2
        dp_ref[...] = (_pool_select(levels, group) - dpooled).astype(BF16)

    return pl.pallas_call(
        body, name="pool_bwd", grid=(groups,),
        in_specs=[pl.BlockSpec((t, cg), lambda g: (0, db + g)), pl.BlockSpec((t, cg), lambda g: (0, pb + g)),
                  pl.BlockSpec((1, cg, cg), lambda g: (g, 0, 0)), pl.BlockSpec((1, cg), lambda g: (0, g))],
        out_specs=[pl.BlockSpec((t, cg), lambda g: (0, g)), pl.BlockSpec((1, cg, cg), lambda g: (g, 0, 0)),
                   pl.BlockSpec((1, cg), lambda g: (0, g))],
        out_shape=[jax.ShapeDtypeStruct((t, groups * cg), BF16), jax.ShapeDtypeStruct((groups, cg, cg), F32),
                   jax.ShapeDtypeStruct((1, groups * cg), F32)],
        compiler_params=_params("parallel"),
    )(dcat, proj, pool_w, pool_scale)


def _attention(q, k, v, tq):
    t, d = q.shape
    m = k.shape[0]
    dh = d // XATTN_HEADS
    scale = dh ** -0.5

    def body(q_ref, k_ref, v_ref, o_ref):
        s = _bdot(q_ref[...], k_ref[...], tb=True) * scale
        s = s - jnp.max(s, axis=-1, keepdims=True)
        e = jnp.exp(s)
        p = e / jnp.sum(e, axis=-1, keepdims=True)
        o_ref[...] = _bdot(p, v_ref[...]).astype(BF16)

    return pl.pallas_call(
        body, name="xattn_fwd", grid=(XATTN_HEADS, t // tq),
        in_specs=[pl.BlockSpec((tq, dh), lambda h, i: (i, h)), pl.BlockSpec((m, dh), lambda h, i: (0, h)),
                  pl.BlockSpec((m, dh), lambda h, i: (0, h))],
        out_specs=pl.BlockSpec((tq, dh), lambda h, i: (i, h)),
        out_shape=jax.ShapeDtypeStruct((t, d), BF16),
        compiler_params=_params("parallel", "parallel"),
    )(q, k, v)


def _attention_backward(q, k, v, do, tq):
    t, d = q.shape
    m = k.shape[0]
    dh = d // XATTN_HEADS
    scale = dh ** -0.5

    def body(q_ref, k_ref, v_ref, do_ref, dq_ref, dk_ref, dv_ref, dk_acc, dv_acc):
        i = pl.program_id(1)
        qv, kv, vv, dov = q_ref[...], k_ref[...], v_ref[...], do_ref[...]
        s = _bdot(qv, kv, tb=True) * scale
        s = s - jnp.max(s, axis=-1, keepdims=True)
        e = jnp.exp(s)
        p = e / jnp.sum(e, axis=-1, keepdims=True)
        dp = _bdot(dov, vv, tb=True)
        ds = p * (dp - jnp.sum(dp * p, axis=-1, keepdims=True)) * scale
        dq_ref[...] = _bdot(ds, kv).astype(BF16)
        dv_part = _bdot(p, dov, ta=True)
        dk_part = _bdot(ds, qv, ta=True)

        @pl.when(i == 0)
        def _():
            dk_acc[...] = dk_part
            dv_acc[...] = dv_part

        @pl.when(i > 0)
        def _():
            dk_acc[...] += dk_part
            dv_acc[...] += dv_part

        @pl.when(i == pl.num_programs(1) - 1)
        def _():
            dk_ref[...] = dk_acc[...].astype(BF16)
            dv_ref[...] = dv_acc[...].astype(BF16)

    qblk = pl.BlockSpec((tq, dh), lambda h, i: (i, h))
    kblk = pl.BlockSpec((m, dh), lambda h, i: (0, h))
    return pl.pallas_call(
        body, name="xattn_bwd", grid=(XATTN_HEADS, t // tq),
        in_specs=[qblk, kblk, kblk, qblk],
        out_specs=[qblk, kblk, kblk],
        out_shape=[jax.ShapeDtypeStruct((t, d), BF16), jax.ShapeDtypeStruct((m, d), BF16),
                   jax.ShapeDtypeStruct((m, d), BF16)],
        scratch_shapes=[pltpu.VMEM((m, dh), F32), pltpu.VMEM((m, dh), F32)],
        compiler_params=_params("parallel", "arbitrary"),
    )(q, k, v, do)


def _ln_backward_rows(name, dmain, dres, xhat, rstd, gamma, tm):
    t, d = xhat.shape

    def body(m_ref, r_ref, x_ref, s_ref, g_ref, du_ref, dub_ref, dg_ref, db_ref):
        du, dg, db = _ln_backward_math(m_ref[...] + ALPHA * r_ref[...], x_ref[...], s_ref[...], g_ref[...])
        du_ref[...] = du
        dub_ref[...] = du.astype(BF16)
        first = pl.program_id(0) == 0

        @pl.when(first)
        def _():
            dg_ref[...] = dg
            db_ref[...] = db

        @pl.when(jnp.logical_not(first))
        def _():
            dg_ref[...] += dg
            db_ref[...] += db

    row = pl.BlockSpec((tm, d), lambda i: (i, 0))
    vec = pl.BlockSpec((1, d), lambda i: (0, 0))
    return pl.pallas_call(
        body, name=name, grid=(t // tm,),
        in_specs=[row, row, row, pl.BlockSpec((tm, 1), lambda i: (i, 0)), vec],
        out_specs=[row, row, vec, vec],
        out_shape=[jax.ShapeDtypeStruct((t, d), F32), jax.ShapeDtypeStruct((t, d), BF16),
                   jax.ShapeDtypeStruct((1, d), F32), jax.ShapeDtypeStruct((1, d), F32)],
        compiler_params=_params("arbitrary"),
    )(dmain, dres, xhat, rstd, gamma)


def _loss_and_ln_backward(xhat, rstd, gamma, beta, target, tm):
    t, d = xhat.shape

    def body(x_ref, r_ref, g_ref, b_ref, t_ref, du_ref, dub_ref, dg_ref, db_ref, loss_ref):
        xh = x_ref[...]
        g = g_ref[...]
        diff = xh * g + b_ref[...] - t_ref[...]
        part = jnp.sum(jnp.sum(diff * diff, axis=1, keepdims=True), axis=0, keepdims=True) * (0.5 / d)
        dy = diff * (1.0 / d)
        du, dg, db = _ln_backward_math(dy, xh, r_ref[...], g)
        du_ref[...] = du
        dub_ref[...] = du.astype(BF16)
        lossrow = jnp.broadcast_to(part, (1, HEAD_DIM))
        first = pl.program_id(0) == 0

        @pl.when(first)
        def _():
            dg_ref[...] = dg
            db_ref[...] = db
            loss_ref[...] = lossrow

        @pl.when(jnp.logical_not(first))
        def _():
            dg_ref[...] += dg
            db_ref[...] += db
            loss_ref[...] += lossrow

    row = pl.BlockSpec((tm, d), lambda i: (i, 0))
    vec = pl.BlockSpec((1, d), lambda i: (0, 0))
    return pl.pallas_call(
        body, name="loss_ln3_bwd", grid=(t // tm,),
        in_specs=[row, pl.BlockSpec((tm, 1), lambda i: (i, 0)), vec, vec, row],
        out_specs=[row, row, vec, vec, pl.BlockSpec((1, HEAD_DIM), lambda i: (0, 0))],
        out_shape=[jax.ShapeDtypeStruct((t, d), F32), jax.ShapeDtypeStruct((t, d), BF16),
                   jax.ShapeDtypeStruct((1, d), F32), jax.ShapeDtypeStruct((1, d), F32),
                   jax.ShapeDtypeStruct((1, HEAD_DIM), F32)],
        compiler_params=_params("arbitrary"),
    )(xhat, rstd, gamma, beta, target)


def _after(token, a):
    return a if token is None else a + token[:1, :1].astype(a.dtype)


def _pick(n, prefs):
    for p in prefs:
        if n % p == 0:
            return p
    return n


def _local_step(x, mem, target, w, token=None):
    t, d = x.shape
    heads = w["a_log"].shape[1]
    gw = heads * HEAD_DIM
    groups, cg, _ = w["pool_w"].shape
    pw = groups * cg
    n_main = 4 * gw + pw
    in_cols = n_main + 2 * heads
    s_in = w["w_in_t"].shape[0]

    tm = _pick(t, (512, 256, 128))
    tm_ln = _pick(t, (256, 128))
    tm_big = _pick(t, (1024, 512, 256, 128))
    tk = _pick(d, K_STEPS)

    w_in_t = w["w_in_t"].reshape(in_cols, d)
    w_p_t = w_in_t[4 * gw + 2 * heads:]
    w_ba_t = jnp.pad(w_in_t[4 * gw:4 * gw + 2 * heads], ((0, HEAD_DIM - 2 * heads), (0, 0)))
    x_bf = _after(token, x).astype(BF16)
    mem_bf = _after(token, mem).astype(BF16)

    tn_d = _pick(d, (1024, 512, 256, 128))
    proj = _plain("proj_main", x_bf, w_in_t, tb=True, n_used=4 * gw, tm=tm_big, tn=_pick(4 * gw, (1024, 512, 256, 128)),
                  tk=tk, out_dtype=F32)
    pproj = _plain("proj_pool", x_bf, w_p_t, tb=True, tm=tm_big, tn=_pick(pw, (1024, 512, 256, 128)), tk=tk, out_dtype=F32)
    ea, dtb = _gate_vectors(w["a_log"], w["dt_bias"], heads)
    vec128 = lambda i, j: (0, 0)
    ba, bg = _matmul(
        "proj_gates", x_bf, w_ba_t, tb=True, tm=tm, tn=HEAD_DIM, tk=tk,
        extra=[(ea, (1, HEAD_DIM), vec128), (dtb, (1, HEAD_DIM), vec128)],
        outs=[(jax.ShapeDtypeStruct((t, HEAD_DIM), F32), (tm, HEAD_DIM), _tile)] * 2,
        epilogue=_gates_epilogue(heads))
    qkv = _gdn_pre(proj, w["conv_w"], heads)
    token = yield ("forward", 1, qkv)
    o_gdn, states = _gdn_core(qkv, _after(token, bg), heads)
    cat_g = _gdn_post(o_gdn, proj, 3 * gw, w["gdn_norm_w"], heads, tm)
    cat_p = _pool_forward(pproj, 0, w["pool_w"], w["pool_scale"])
    cat = jnp.concatenate([cat_g, cat_p], axis=1)
    token = yield ("forward", 2, cat)
    w = {**w, **(yield ("weights", 1, cat))}
    h1, h1_bf, xhat1, rstd1 = _ln_forward("mix_ln1", cat, w["w_out"], x, _after(token, w["ln1_g"]), w["ln1_b"],
                                          tm=tm_ln, tk=tk)

    q = _plain("xattn_q", h1_bf, w["xq_w"], tm=tm, tn=tn_d, tk=tk, out_dtype=BF16)
    mlen = mem.shape[0]
    tm_mem = _pick(mlen, (256, 128))
    k = _plain("xattn_k", mem_bf, w["xk_w"], tm=tm_mem, tn=tn_d, tk=tk, out_dtype=BF16)
    v = _plain("xattn_v", mem_bf, w["xv_w"], tm=tm_mem, tn=tn_d, tk=tk, out_dtype=BF16)
    att = _attention(q, k, v, tm)
    h2, h2_bf, xhat2, rstd2 = _ln_forward("xo_ln2", att, w["xo_w"], h1, w["ln2_g"], w["ln2_b"], tm=tm_ln, tk=tk)

    w = {**w, **(yield ("weights", 2, h2_bf))}
    s_up = w["w_up3"].shape[0]
    ff = s_up * w["w_up3"].shape[2]
    tn_f = _pick(ff // s_up, (1024, 512, 256, 128))

    def up_epi(acc, ex, out, i):
        r = jnp.maximum(acc, 0.0)
        out[0][...] = (r * r).astype(BF16)
        out[1][...] = (2.0 * r).astype(BF16)

    act, act_grad = _matmul(
        "mlp_up", h2_bf, w["w_up3"], b_blocks=s_up, tm=tm_big, tn=tn_f, tk=tk,
        outs=[(jax.ShapeDtypeStruct((t, ff), BF16), (tm_big, tn_f), _tile)] * 2, epilogue=up_epi)
    w = {**w, **(yield ("weights", 3, act))}
    tk_f = _pick(ff, K_STEPS)
    xhat3, rstd3 = _ln_forward("down_ln3", act, w["w_down"], h2, w["ln3_g"], w["ln3_b"], tm=tm, tk=tk_f, want_h=False)

    grads = {}
    du3, du3_bf, grads["ln3_g"], grads["ln3_b"], loss = _loss_and_ln_backward(
        xhat3, rstd3, w["ln3_g"], w["ln3_b"], target, tm_ln)

    def dup_epi(acc, ex, out, i):
        out[0][...] = (acc * ex[0][...].astype(F32)).astype(BF16)

    dup = _matmul(
        "mlp_down_dx", du3_bf, w["w_down"], tb=True, tm=tm_big, tn=tn_f, tk=tk,
        extra=[(act_grad, (tm_big, tn_f), _tile)],
        outs=[(jax.ShapeDtypeStruct((t, ff), BF16), (tm_big, tn_f), _tile)], epilogue=dup_epi)[0]
    tk_t = _pick(t, K_STEPS)
    tm_w = _pick(d, (512, 256, 128))
    grads["w_down"] = _plain("mlp_down_dw", act, du3_bf, ta=True, tm=_pick(ff, (512, 256, 128)), tn=d, tk=tk_t,
                             out_dtype=F32)
    grads["w_up3"] = _plain("mlp_up_dw", h2_bf, dup, ta=True, tm=tm_w, tn=ff // s_up, tk=tk_t, out_dtype=F32, out3=s_up,
                            n_outer=True)
    token = yield ("grads", 0, {n: grads.pop(n) for n in ("w_down", "w_up3")})
    dh2 = _plain("mlp_up_dx", dup, w["w_up3"], tb=True, b_blocks=s_up, tm=tm_big, tn=tn_d,
                 tk=_pick(ff // s_up, K_STEPS), out_dtype=F32)
    du2, du2_bf, grads["ln2_g"], grads["ln2_b"] = _ln_backward_rows(
        "ln2_bwd", dh2, du3, xhat2, rstd2, _after(token, w["ln2_g"]), tm_ln)
    token = yield ("poll", 0, du2_bf)

    grads["xo_w"] = _plain("xo_dw", att, du2_bf, ta=True, tm=tm_w, tn=d, tk=tk_t, out_dtype=F32)
    datt = _plain("xo_dx", du2_bf, w["xo_w"], tb=True, tm=tm, tn=tn_d, tk=tk, out_dtype=BF16)
    dq, dk, dv = _attention_backward(q, k, v, datt, tm)
    tk_m = _pick(mlen, (256, 128))
    grads["xq_w"] = _plain("xq_dw", h1_bf, dq, ta=True, tm=tm_w, tn=d, tk=tk_t, out_dtype=F32)
    grads["xk_w"] = _plain("xk_dw", mem_bf, dk, ta=True, tm=tm_w, tn=tn_d, tk=tk_m, out_dtype=F32)
    grads["xv_w"] = _plain("xv_dw", mem_bf, dv, ta=True, tm=tm_w, tn=tn_d, tk=tk_m, out_dtype=F32)
    du1, du1_bf, grads["ln1_g"], grads["ln1_b"] = _ln_backward(
        "xq_dx_ln1", dq, w["xq_w"], du2, xhat1, rstd1, _after(token, w["ln1_g"]), tm=tm_ln, tk=tk)

    grads["w_out"] = _plain("out_dw", cat, du1_bf, ta=True, tm=tm_w, tn=d, tk=tk_t, out_dtype=F32)
    token = yield ("grads", 1, {n: grads.pop(n) for n in ("xo_w", "xq_w", "xk_w", "xv_w", "w_out")})
    dcat = _plain("out_dx", du1_bf, w["w_out"], tb=True, tm=tm, tn=tn_d, tk=tk, out_dtype=F32)
    dp, grads["pool_w"], grads["pool_scale"] = _pool_backward(dcat, gw, pproj, 0, w["pool_w"],
                                                              _after(token, w["pool_scale"]))
    do_gdn, dz, grads["gdn_norm_w"] = _gdn_post_backward(dcat, o_gdn, proj, 3 * gw, _after(token, w["gdn_norm_w"]),
                                                         heads, tm)
    dqkv, dbg = _gdn_core_backward(qkv, bg, states, do_gdn, heads)
    token = yield ("poll", 1, dqkv)
    dqkv_pre, grads["conv_w"] = _gdn_pre_backward(proj, _after(token, w["conv_w"]), dqkv, heads)
    dba, dalog_row, ddt_row = _gates_backward(ba, bg, dbg, ea, dtb, heads)
    grads["a_log"] = dalog_row[:, heads:2 * heads]
    grads["dt_bias"] = ddt_row[:, heads:2 * heads]

    dproj = jnp.concatenate([dqkv_pre, dz, dp], axis=1)
    dw_main = _plain("proj_dw", dproj, x_bf, ta=True, tm=_pick(n_main, (512, 256, 128)), tn=d, tk=tk_t, out_dtype=F32)
    dw_ba = _plain("proj_gates_dw", dba, x_bf, ta=True, tm=HEAD_DIM, tn=tn_d, tk=tk_t, out_dtype=F32)
    dw_in_t = jnp.concatenate([dw_main[:4 * gw], dw_ba[:2 * heads], dw_main[4 * gw:]], axis=0)
    grads["w_in_t"] = dw_in_t.reshape(s_in, in_cols // s_in, d)

    def dx_epi(acc, ex, out, i):
        out[0][...] = acc + ex[1][...] + ALPHA * ex[0][...]

    def add_epi(acc, ex, out, i):
        out[0][...] = acc + ex[0][...]

    token = yield ("grads", 2, {n: grads.pop(n) for n in ("w_in_t", "pool_w")})
    dx_gates = _plain("proj_gates_dx", dba, _after(token, w_ba_t), tm=tm, tn=tn_d, tk=HEAD_DIM, out_dtype=F32)
    out_tile = [(jax.ShapeDtypeStruct((t, d), F32), (tm, tn_d), _tile)]
    dx_pool = _matmul("proj_pool_dx", dp, w_p_t, tm=tm, tn=tn_d, tk=_pick(pw, K_STEPS),
                      extra=[(dx_gates, (tm, tn_d), _tile)], outs=out_tile, epilogue=add_epi)[0]
    grad_x = _matmul(
        "proj_dx", dproj, w_in_t, k_used=4 * gw, tm=tm, tn=tn_d, tk=_pick(4 * gw, K_STEPS),
        extra=[(du1, (tm, tn_d), _tile), (dx_pool, (tm, tn_d), _tile)], outs=out_tile, epilogue=dx_epi)[0]
    yield ("poll", 2, grad_x)
    return loss, grad_x, grads


def _adamw(name, w, g, m, v):
    r, c = w.shape
    if r % 8 == 0:
        tr = _pick(r, (256, 128, 64, 32, 16, 8))
        blk, steps = pl.BlockSpec((tr, c), lambda i: (i, 0)), r // tr
    else:
        tc = _pick(c, (256, 128))
        blk, steps = pl.BlockSpec((r, tc), lambda i: (0, i)), c // tc
    c1 = 1.0 - ADAM_B1 ** ADAM_STEP
    c2 = 1.0 - ADAM_B2 ** ADAM_STEP

    def body(w_ref, g_ref, m_ref, v_ref, d_ref, mo_ref, vo_ref):
        gv = g_ref[...]
        mn = ADAM_B1 * m_ref[...] + (1.0 - ADAM_B1) * gv
        vn = ADAM_B2 * v_ref[...] + (1.0 - ADAM_B2) * (gv * gv)
        d_ref[...] = -ADAM_LR * ((mn / c1) / (jnp.sqrt(vn / c2) + ADAM_EPS) + ADAM_WD * w_ref[...])
        mo_ref[...] = mn
        vo_ref[...] = vn

    return pl.pallas_call(
        body, name=name, grid=(steps,), in_specs=[blk] * 4, out_specs=[blk] * 3,
        out_shape=[jax.ShapeDtypeStruct((r, c), F32)] * 3,
        compiler_params=_params("parallel"),
    )(w, g, m, v)


def _place():
    x, y, c = lax.axis_index("x"), lax.axis_index("y"), lax.axis_index("c")
    chips = [(1 - x, y), (x, 1 - y), (1 - x, 1 - y)]
    return x, y, c, chips


HBM = pl.BlockSpec(memory_space=pltpu.HBM)


SEM = pl.BlockSpec(memory_space=pltpu.SEMAPHORE)
ANY = pl.BlockSpec(memory_space=pl.ANY)
EFFECT = pltpu.SideEffectType.DATAFLOW_SIDE_EFFECTING


def _in_hbm(a):
    return pltpu.with_memory_space_constraint(a, pltpu.HBM)


def _remote(src, dst, send_sem, recv_sem, to):
    return pltpu.make_async_remote_copy(src_ref=src, dst_ref=dst, send_sem=send_sem, recv_sem=recv_sem,
                                        device_id=to, device_id_type=MESH)


def _by_rows(rows):
    return rows % 32 == 0


def _half_shape(rows, cols):
    return (rows // 2, cols) if _by_rows(rows) else (rows, cols // 2)


def _half(ref, which, *lead):
    rows, cols = ref.shape[-2:]
    if _by_rows(rows):
        return ref.at[(*lead, pl.ds(which * (rows // 2), rows // 2))]
    return ref.at[(*lead, slice(None), pl.ds(which * (cols // 2), cols // 2))]


def _landed(lands, i, shard_index, which):
    return _half(lands[i], which, shard_index)


def _gather_start(name, shards, after):
    n = len(shards)
    lands = [lax.empty((N_SHARD,) + s.shape, s.dtype) for s in shards]

    def body(*refs):
        ins, zones = refs[:n], refs[n:2 * n]
        ici_send, ici_recv, own_send, own_recv = refs[2 * n + 1:2 * n + 5]
        token = refs[-1]
        x, y, c, chips = _place()
        me = 2 * x + y
        for i in range(n):
            for j, chip in enumerate(chips):
                _remote(_half(ins[i], c), _landed(zones, i, me, c), ici_send.at[3 * i + j],
                        ici_recv.at[3 * i + j], (*chip, c)).start()
        for i in range(n):
            _remote(ins[i], zones[i].at[me], own_send.at[i], own_recv.at[i], (x, y, 1 - c)).start()
        token[...] = jnp.zeros_like(token)

    dma = pltpu.SemaphoreType.DMA
    outs = pl.pallas_call(
        body, name=name,
        in_specs=[HBM] * (2 * n) + [ANY],
        out_shape=(dma((3 * n,)), dma((3 * n,)), dma((n,)), dma((n,)),
                   *[pltpu.HBM(a.shape, a.dtype) for a in shards + lands], jax.ShapeDtypeStruct((8, LANES), F32)),
        out_specs=(SEM, SEM, SEM, SEM, *[HBM] * (2 * n), pl.BlockSpec(memory_space=pltpu.VMEM)),
        input_output_aliases={k: 4 + k for k in range(2 * n)},
        compiler_params=pltpu.CompilerParams(has_side_effects=EFFECT),
    )(*[_in_hbm(a) for a in shards + lands], after)
    sems = dict(zip(("ici_send", "ici_recv", "own_send", "own_recv"), outs[:4]))
    return sems, list(outs[4:4 + n]), list(outs[4 + n:4 + 2 * n]), outs[-1]


def _gather_forward(name, idx, lands, sems, after):
    n = len(idx)

    def body(*refs):
        zones = refs[:n]
        ici_recv = refs[n]
        fwd_send, fwd_recv = refs[n + 2], refs[n + 3]
        x, y, c, chips = _place()
        for k, i in enumerate(idx):
            for j, chip in enumerate(chips):
                half = _landed(zones, k, 2 * chip[0] + chip[1], c)
                _remote(half, half, fwd_send.at[3 * k + j], ici_recv.at[3 * i + j], (*chip, c)).wait_recv()
                _remote(half, half, fwd_send.at[3 * k + j], fwd_recv.at[3 * k + j], (x, y, 1 - c)).start()
        refs[-1][...] = jnp.zeros_like(refs[-1])

    dma = pltpu.SemaphoreType.DMA
    outs = pl.pallas_call(
        body, name=name,
        in_specs=[HBM] * n + [SEM, ANY],
        out_shape=(dma((3 * n,)), dma((3 * n,)), *[pltpu.HBM(a.shape, a.dtype) for a in lands],
                   jax.ShapeDtypeStruct((8, LANES), F32)),
        out_specs=(SEM, SEM, *[HBM] * n, pl.BlockSpec(memory_space=pltpu.VMEM)),
        input_output_aliases={k: 2 + k for k in range(n)},
        compiler_params=pltpu.CompilerParams(has_side_effects=EFFECT),
    )(*lands, sems["ici_recv"], after)
    return (outs[0], outs[1]), list(outs[2:2 + n]), outs[-1]


def _gather_wait(name, idx, shards, lands, sems, fwd, after):
    n = len(idx)

    def body(*refs):
        ins, zones = refs[:n], refs[n:2 * n]
        ici_send, own_send, own_recv, fwd_send, fwd_recv = refs[2 * n:2 * n + 5]
        x, y, c, chips = _place()
        me = 2 * x + y
        for k, i in enumerate(idx):
            mine = _half(ins[k], c)
            for j, chip in enumerate(chips):
                theirs = 2 * chip[0] + chip[1]
                _remote(mine, _landed(zones, k, me, c), ici_send.at[3 * i + j], fwd_recv.at[3 * k + j],
                        (*chip, c)).wait_send()
                sent = _landed(zones, k, theirs, c)
                _remote(sent, sent, fwd_send.at[3 * k + j], fwd_recv.at[3 * k + j], (x, y, 1 - c)).wait_send()
                passed = _landed(zones, k, theirs, 1 - c)
                _remote(passed, passed, fwd_send.at[3 * k + j], fwd_recv.at[3 * k + j], (x, y, 1 - c)).wait_recv()
            own = _remote(ins[k], zones[k].at[me], own_send.at[i], own_recv.at[i], (x, y, 1 - c))
            own.wait_send()
            own.wait_recv()

    outs = pl.pallas_call(
        body, name=name,
        in_specs=[HBM] * (2 * n) + [SEM] * 5 + [ANY],
        out_shape=tuple(pltpu.HBM(a.shape, a.dtype) for a in lands),
        out_specs=tuple([HBM] * n),
        input_output_aliases={n + k: k for k in range(n)},
        compiler_params=pltpu.CompilerParams(has_side_effects=EFFECT),
    )(*shards, *lands, sems["ici_send"], sems["own_send"], sems["own_recv"], fwd[0], fwd[1], after)
    return list(outs)


def _all_reduce_small(name, slab, after=None):
    r, width = slab.shape
    ndev = 8

    def body(x_ref, after_ref, out_ref, buf, send_sems, recv_sems):
        x, y, c, _ = _place()
        me = 4 * x + 2 * y + c
        buf[me] = x_ref[...]
        copies = []
        for k in range(1, ndev):
            peer = jnp.bitwise_xor(me, k)
            to = (peer // 4, (peer // 2) % 2, peer % 2)
            cp = pltpu.make_async_remote_copy(src_ref=x_ref, dst_ref=buf.at[me], send_sem=send_sems.at[k - 1],
                                              recv_sem=recv_sems.at[k - 1], device_id=to, device_id_type=MESH)
            cp.start()
            copies.append(cp)
        for k in range(1, ndev):
            peer = jnp.bitwise_xor(me, k)
            pltpu.make_async_remote_copy(src_ref=x_ref, dst_ref=buf.at[peer], send_sem=send_sems.at[k - 1],
                                         recv_sem=recv_sems.at[k - 1], device_id=(x, y, c),
                                         device_id_type=MESH).wait_recv()
        for cp in copies:
            cp.wait_send()
        total = buf[0]
        for d in range(1, ndev):
            total = total + buf[d]
        out_ref[...] = total

    return pl.pallas_call(
        body, name=name,
        in_specs=[pl.BlockSpec(memory_space=pltpu.VMEM), ANY], out_specs=pl.BlockSpec(memory_space=pltpu.VMEM),
        out_shape=jax.ShapeDtypeStruct((r, width), F32),
        scratch_shapes=[pltpu.VMEM((ndev, r, width), F32), pltpu.SemaphoreType.DMA((ndev - 1,)),
                        pltpu.SemaphoreType.DMA((ndev - 1,))],
        compiler_params=pltpu.CompilerParams(vmem_limit_bytes=VMEM_LIMIT),
    )(slab, slab if after is None else after)


def _half_tiling(rows, cols):
    if _by_rows(rows):
        tr = _pick(rows // 2, (256, 128, 64, 32, 16))
        nb = (rows // 2) // tr
        return (tr, cols), nb, (lambda which, b: (which * nb + b, 0)), (lambda b: (b, 0))
    tc = _pick(cols // 2, (256, 128))
    nb = (cols // 2) // tc
    return (rows, tc), nb, (lambda which, b: (0, which * nb + b)), (lambda b: (0, b))


def _chip_partial(name, grad, other, core):
    s, r, cdim = grad.shape
    blk, nb, whole, within = _half_tiling(r, cdim)

    def body(core_ref, g_ref, o_ref, out_ref):
        out_ref[...] = (g_ref[...] + o_ref[...]).astype(BF16)

    return pl.pallas_call(
        body, name=name,
        grid_spec=pltpu.PrefetchScalarGridSpec(
            num_scalar_prefetch=1, grid=(s, nb),
            in_specs=[pl.BlockSpec((None,) + blk, lambda j, b, core_ref: (j,) + whole(core_ref[0], b)),
                      pl.BlockSpec((None,) + blk, lambda j, b, core_ref: (j,) + within(b))],
            out_specs=pl.BlockSpec((None,) + blk, lambda j, b, core_ref: (j,) + within(b))),
        out_shape=jax.ShapeDtypeStruct((s,) + _half_shape(r, cdim), BF16),
        compiler_params=_params("parallel", "parallel"),
    )(core, grad, other)


def _partial_copies(ins, zones, send_sems, recv_sems):
    x, y, c, chips = _place()
    return [_remote(ins[i].at[2 * chip[0] + chip[1]], zones[i].at[j], send_sems.at[3 * i + j],
                    recv_sems.at[3 * i + j], (*chip, c))
            for i in range(len(ins)) for j, chip in enumerate(chips)]


def _swap_copies(ins, zones, send_sems, recv_sems):
    x, y, c, _ = _place()
    copies = []
    for i in range(len(ins)):
        for s in range(N_SHARD):
            copies.append(_remote(_half(ins[i], 1 - c, s), zones[i].at[s],
                                  send_sems.at[N_SHARD * i + s], recv_sems.at[N_SHARD * i + s], (x, y, 1 - c)))
    return copies


def _exchange_start(name, plan, sources, lands, per_array):
    n = len(sources)
    lands = [lax.empty(shape, dtype) for shape, dtype in lands]

    def body(*refs):
        for cp in plan(refs[:n], refs[n:2 * n], refs[2 * n], refs[2 * n + 1]):
            cp.start()
        refs[-1][...] = jnp.zeros_like(refs[-1])

    dma = pltpu.SemaphoreType.DMA
    outs = pl.pallas_call(
        body, name=name,
        in_specs=[HBM] * (2 * n),
        out_shape=(dma((per_array * n,)), dma((per_array * n,)),
                   *[pltpu.HBM(a.shape, a.dtype) for a in list(sources) + lands], jax.ShapeDtypeStruct((8, LANES), F32)),
        out_specs=(SEM, SEM, *[HBM] * (2 * n), pl.BlockSpec(memory_space=pltpu.VMEM)),
        input_output_aliases={k: 2 + k for k in range(2 * n)},
        compiler_params=pltpu.CompilerParams(has_side_effects=EFFECT),
    )(*[_in_hbm(a) for a in list(sources) + lands])
    return (outs[0], outs[1]), list(outs[2:2 + n]), list(outs[2 + n:2 + 2 * n]), outs[-1]


def _exchange_wait(name, plan, started, after):
    sems, partials, lands, _ = started
    n = len(partials)

    def body(*refs):
        for cp in plan(refs[:n], refs[n:2 * n], refs[2 * n], refs[2 * n + 1]):
            cp.wait_send()
            cp.wait_recv()

    outs = pl.pallas_call(
        body, name=name,
        in_specs=[HBM] * (2 * n) + [SEM, SEM] + [ANY] * len(after),
        out_shape=tuple(pltpu.HBM(a.shape, a.dtype) for a in lands),
        out_specs=tuple([HBM] * n),
        input_output_aliases={n + k: k for k in range(n)},
        compiler_params=pltpu.CompilerParams(has_side_effects=EFFECT),
    )(*partials, *lands, sems[0], sems[1], *after)
    return list(outs)


def _reduce_own(name, grad, other, received, where):
    s, r, cdim = grad.shape
    blk, nb, whole, within = _half_tiling(r, cdim)

    def body(where_ref, g_ref, o_ref, r_ref, out_ref):
        total = g_ref[...] + o_ref[...]
        for j in range(3):
            total = total + r_ref[j].astype(F32)
        out_ref[...] = total

    return pl.pallas_call(
        body, name=name,
        grid_spec=pltpu.PrefetchScalarGridSpec(
            num_scalar_prefetch=1, grid=(nb,),
            in_specs=[pl.BlockSpec((None,) + blk, lambda b, w_ref: (w_ref[0],) + whole(w_ref[1], b)),
                      pl.BlockSpec((None,) + blk, lambda b, w_ref: (w_ref[0],) + within(b)),
                      pl.BlockSpec((3,) + blk, lambda b, w_ref: (0,) + within(b))],
            out_specs=pl.BlockSpec(blk, lambda b, w_ref: whole(w_ref[1], b))),
        out_shape=jax.ShapeDtypeStruct((r, cdim), F32),
        compiler_params=_params("parallel"),
    )(where, grad, other, received)


def _join_start(name, halves):
    n = len(halves)

    def body(*refs):
        bufs, send_sems, recv_sems = refs[:n], refs[n], refs[n + 1]
        x, y, c, _ = _place()
        for i in range(n):
            mine = _half(bufs[i], c)
            _remote(mine, mine, send_sems.at[i], recv_sems.at[i], (x, y, 1 - c)).start()
        refs[-1][...] = jnp.zeros_like(refs[-1])

    dma = pltpu.SemaphoreType.DMA
    outs = pl.pallas_call(
        body, name=name,
        in_specs=[HBM] * n,
        out_shape=(dma((n,)), dma((n,)), *[pltpu.HBM(h.shape, F32) for h in halves], jax.ShapeDtypeStruct((8, LANES), F32)),
        out_specs=(SEM, SEM, *[HBM] * n, pl.BlockSpec(memory_space=pltpu.VMEM)),
        input_output_aliases={k: 2 + k for k in range(n)},
        compiler_params=pltpu.CompilerParams(has_side_effects=EFFECT),
    )(*[_in_hbm(h) for h in halves])
    return (outs[0], outs[1]), list(outs[2:2 + n]), outs[-1]


def _join_wait(name, started, after):
    sems, bufs, _ = started
    n = len(bufs)

    def body(*refs):
        bufs, send_sems, recv_sems = refs[:n], refs[n], refs[n + 1]
        x, y, c, _ = _place()
        for i in range(n):
            mine, theirs = _half(bufs[i], c), _half(bufs[i], 1 - c)
            _remote(mine, mine, send_sems.at[i], recv_sems.at[i], (x, y, 1 - c)).wait_send()
            _remote(theirs, theirs, send_sems.at[i], recv_sems.at[i], (x, y, 1 - c)).wait_recv()

    outs = pl.pallas_call(
        body, name=name,
        in_specs=[HBM] * n + [SEM, SEM] + [ANY] * len(after),
        out_shape=tuple(pltpu.HBM(b.shape, F32) for b in bufs),
        out_specs=tuple([HBM] * n),
        input_output_aliases={k: k for k in range(n)},
        compiler_params=pltpu.CompilerParams(has_side_effects=EFFECT),
    )(*bufs, sems[0], sems[1], *after)
    return list(outs)


BIG = ("w_in", "pool_w", "w_out", "xq_w", "xk_w", "xv_w", "xo_w", "w_up", "w_down")
GATHER_GROUPS = ((0, 1), (2, 3, 4, 5, 6), (7,), (8,))
SMALL = ("conv_w", "a_log", "dt_bias", "gdn_norm_w", "pool_scale", "ln1_g", "ln1_b", "ln2_g", "ln2_b", "ln3_g", "ln3_b")
ORDER = ("w_in", "conv_w", "a_log", "dt_bias", "gdn_norm_w", "pool_w", "pool_scale", "w_out", "ln1_g", "ln1_b",
         "xq_w", "xk_w", "xv_w", "xo_w", "ln2_g", "ln2_b", "w_up", "w_down", "ln3_g", "ln3_b")
LANES = 128


def _rows(flat_len):
    return -(-flat_len // LANES)


def _pack(pieces):
    out = []
    for p in pieces:
        flat = p.reshape(-1).astype(F32)
        out.append(jnp.pad(flat, (0, _rows(flat.shape[0]) * LANES - flat.shape[0])).reshape(-1, LANES))
    slab = jnp.concatenate(out, axis=0)
    return jnp.pad(slab, ((0, -slab.shape[0] % 8), (0, 0)))


def _unpack(slab, shapes):
    out, row = [], 0
    for shp in shapes:
        size = math.prod(shp)
        out.append(slab[row:row + _rows(size)].reshape(-1)[:size].reshape(shp))
        row += _rows(size)
    return out


TRANSPOSED = ("w_in",)


def _as2d(name, a):
    a = a[0]
    if name in TRANSPOSED:
        return jnp.swapaxes(a, 0, 1)
    return a.reshape(-1, a.shape[-1]) if a.ndim == 3 else a


def _from2d(name, a, shape):
    return (jnp.swapaxes(a, 0, 1) if name in TRANSPOSED else a).reshape(shape)


def kernel(x, mem, w_in, conv_w, a_log, dt_bias, gdn_norm_w, pool_w, pool_scale, w_out, ln1_g, ln1_b, xq_w, xk_w, xv_w, xo_w, ln2_g, ln2_b, w_up, w_down, ln3_g, ln3_b, loss_target, m_w_in, m_conv_w, m_a_log, m_dt_bias, m_gdn_norm_w, m_pool_w, m_pool_scale, m_w_out, m_ln1_g, m_ln1_b, m_xq_w, m_xk_w, m_xv_w, m_xo_w, m_ln2_g, m_ln2_b, m_w_up, m_w_down, m_ln3_g, m_ln3_b, v_w_in, v_conv_w, v_a_log, v_dt_bias, v_gdn_norm_w, v_pool_w, v_pool_scale, v_w_out, v_ln1_g, v_ln1_b, v_xq_w, v_xk_w, v_xv_w, v_xo_w, v_ln2_g, v_ln2_b, v_w_up, v_w_down, v_ln3_g, v_ln3_b):
    given = dict(locals())
    cx, cy, cc = lax.axis_index("x"), lax.axis_index("y"), lax.axis_index("c")
    me = 2 * cx + cy
    groups = pool_w.shape[1]
    cs = pool_w.shape[2]
    kk, conv_cols = conv_w.shape[1], conv_w.shape[2]
    core = cc.astype(jnp.int32).reshape(1)
    where = jnp.stack([me, cc]).astype(jnp.int32)

    conv_slab = jnp.zeros((kk, N_SHARD * conv_cols), F32)
    conv_slab = lax.dynamic_update_slice(conv_slab, conv_w[0] * (cc == 0).astype(F32), (0, me * conv_cols))
    wts = {"conv_w": _unpack(_all_reduce_small("gather_conv_w", _pack([conv_slab])), [conv_slab.shape])[0]}

    started = {}

    def start(name, idx, after, token=None):
        casts = [_after(token, _as2d(BIG[i], given[BIG[i]])).astype(BF16) for i in idx]
        sems, shards, lands, token = _gather_start(name, casts, after)
        for k, i in enumerate(idx):
            started[i] = (sems, k, shards[k], lands[k])
        return token

    token = start("gather_start_first", GATHER_GROUPS[0], wts["conv_w"])
    token = start("gather_start_rest", tuple(i for group in GATHER_GROUPS[1:] for i in group), token, token)

    forwarded = {}

    def forward(group, after):
        members = [started[i] for i in GATHER_GROUPS[group]]
        forwarded[group] = _gather_forward(f"gather_forward_{group}", [m[1] for m in members], [m[3] for m in members],
                                           members[0][0], after)
        return forwarded[group][2]

    def fetch(group, after):
        if group not in forwarded:
            forward(group, after)
        members = [started[i] for i in GATHER_GROUPS[group]]
        sems, idx = members[0][0], [m[1] for m in members]
        fwd, zones, _ = forwarded[group]
        full = dict(zip([BIG[i] for i in GATHER_GROUPS[group]],
                        _gather_wait(f"gather_wait_{group}", idx, [m[2] for m in members], zones, sems, fwd, after)))
        out = {}
        for n, a in full.items():
            if n == "w_in":
                out["w_in_t"] = a
            elif n == "w_up":
                out["w_up3"] = a
            elif n == "pool_w":
                out[n] = a.reshape(N_SHARD, groups, cs, -1).transpose(1, 0, 2, 3).reshape(groups, N_SHARD * cs, -1)
            else:
                out[n] = a.reshape(-1, a.shape[-1])
        return out

    for n in ("a_log", "dt_bias", "gdn_norm_w", "pool_scale", "ln1_g", "ln1_b", "ln2_g", "ln2_b", "ln3_g", "ln3_b"):
        wts[n] = given[n]
    wts.update(fetch(0, token))

    def start_swap(group, grads):
        names, blocks = [], []
        for n, g in grads.items():
            if n == "pool_w":
                g = g.reshape(groups, N_SHARD, cs, -1).transpose(1, 0, 2, 3).reshape(N_SHARD, groups * cs, -1)
            elif g.ndim == 2:
                g = g.reshape(N_SHARD, -1, g.shape[-1])
            names.append({"w_in_t": "w_in", "w_up3": "w_up"}.get(n, n))
            blocks.append(g)
        zones = [((N_SHARD,) + _half_shape(b.shape[1], b.shape[2]), F32) for b in blocks]
        swap = _exchange_start(f"grad_swap_start_{group}", _swap_copies, blocks, zones, N_SHARD)
        return {"group": group, "names": names, "swap": swap, "token": swap[3]}

    def start_send(state, after):
        group, names = state["group"], state["names"]
        state["blocks"] = state["swap"][1]
        state["others"] = _exchange_wait(f"grad_swap_wait_{group}", _swap_copies, state["swap"], after)
        partials = [_chip_partial("chip_partial_" + n, gb, ob, core)
                    for n, gb, ob in zip(names, state["blocks"], state["others"])]
        zones = [((3,) + p.shape[1:], BF16) for p in partials]
        state["send"] = _exchange_start(f"grad_send_start_{group}", _partial_copies, partials, zones, 3)
        state["token"] = state["send"][3]

    grad, delta, new_m, new_v = {}, {}, {}, {}

    def start_join(state, after):
        group, names = state["group"], state["names"]
        received = _exchange_wait(f"grad_send_wait_{group}", _partial_copies, state["send"], after)
        halves = [_reduce_own("reduce_own_" + n, gb, ob, rb, where)
                  for n, gb, ob, rb in zip(names, state["blocks"], state["others"], received)]
        state["join"] = _join_start(f"grad_join_start_{group}", halves)
        return state["join"][2]

    def finish_reduce(state, after):
        group, names = state["group"], state["names"]
        for n, g in zip(names, _join_wait(f"grad_join_wait_{group}", state["join"], after)):
            shp = given[n].shape
            d2, m2, v2 = _adamw("adamw_" + n, _as2d(n, given[n]), g, _as2d(n, given["m_" + n]), _as2d(n, given["v_" + n]))
            grad[n], delta[n], new_m[n], new_v[n] = (_from2d(n, a, shp) for a in (g, d2, m2, v2))
        return d2

    step = _local_step(x[0], mem[0], loss_target[0], wts, token)
    pending = {}
    request = next(step)
    while True:
        try:
            kind, group, payload = request
            if kind == "weights":
                request = step.send(fetch(group, payload))
            elif kind == "forward":
                request = step.send(forward(group, payload))
            elif kind == "grads":
                pending[group] = start_swap(group, payload)
                request = step.send(pending[group]["token"])
            else:
                start_send(pending[group], [payload])
                request = step.send(pending[group]["token"])
        except StopIteration as stop:
            loss_row, grad_x, g = stop.value
            break

    after = [pending[2]["token"], grad_x]
    for group in (0, 1):
        after = [start_join(pending[group], after)]
    for group in (0, 1):
        after = [finish_reduce(pending[group], after)]
    after = [finish_reduce(pending[2], [start_join(pending[2], after)])]

    small_names = ("a_log", "dt_bias", "gdn_norm_w", "pool_scale", "ln1_g", "ln1_b", "ln2_g", "ln2_b", "ln3_g", "ln3_b")
    pieces = [g["conv_w"]] + [g[n] for n in small_names] + [loss_row[:, :1]]
    shapes = [p.shape for p in pieces]
    summed = _unpack(_all_reduce_small("all_reduce_small", _pack(pieces), after[0]), shapes)
    gsmall = dict(zip(small_names, summed[1:-1]))
    gsmall["conv_w"] = lax.dynamic_slice(summed[0], (0, me * conv_cols), (kk, conv_cols))
    loss = summed[-1][0, 0]

    sshapes = [given[n].shape for n in SMALL]
    slabs = [_pack([given[p + n] for n in SMALL]) for p in ("", "m_", "v_")]
    gslab = _pack([gsmall[n] for n in SMALL])
    outs = _adamw("adamw_small", slabs[0], gslab, slabs[1], slabs[2])
    for dst, slab in zip((delta, new_m, new_v), outs):
        dst.update(zip(SMALL, _unpack(slab, sshapes)))
    for n in SMALL:
        grad[n] = gsmall[n].reshape(given[n].shape)

    return (loss, grad_x[None], *[grad[n] for n in ORDER], *[delta[n] for n in ORDER],
            *[new_m[n] for n in ORDER], *[new_v[n] for n in ORDER])
```

```python
import functools
import math

import jax
import jax.numpy as jnp
from jax import lax
from jax.experimental import pallas as pl
from jax.experimental.pallas import tpu as pltpu

F32 = jnp.float32
BF16 = jnp.bfloat16
MESH = pl.DeviceIdType.MESH

HEAD_DIM = 128
CHUNK = 64
POOL_WINDOWS = (2, 4, 8, 16)
XATTN_HEADS = 4
ALPHA = 2.0 ** 0.25
LN_EPS = 1e-5
NORM_EPS = 1e-6
ADAM_LR, ADAM_B1, ADAM_B2, ADAM_EPS, ADAM_WD, ADAM_STEP = 0.001, 0.9, 0.999, 1e-08, 0.01, 10
N_SHARD = 4
VMEM_LIMIT = 56 * 1024 * 1024
K_STEPS = (2048, 1024, 512, 256, 128)


def _params(*sem):
    return pltpu.CompilerParams(dimension_semantics=sem, vmem_limit_bytes=VMEM_LIMIT)


def _bdot(a, b, ta=False, tb=False):
    dims = (((0 if ta else 1,), (1 if tb else 0,)), ((), ()))
    return lax.dot_general(a.astype(BF16), b.astype(BF16), dims, preferred_element_type=F32)


def _sigmoid(x):
    return 1.0 / (1.0 + jnp.exp(-x))


def _matmul(name, a, b, *, ta=False, tb=False, tm, tn, tk, extra=(), outs, epilogue, b_blocks=None,
            sequential=False, n_used=None, k_used=None, n_outer=False):
    m, k_dim = (a.shape[1], a.shape[0]) if ta else a.shape
    if b_blocks and tb:
        n = b.shape[1]
        k_dim = b.shape[0] * b.shape[2]
        per = b.shape[2] // tk
        b_spec = pl.BlockSpec((None, tn, tk), lambda i, j, k: (k // per, j, k % per))
    elif b_blocks:
        n = b.shape[0] * b.shape[2]
        per = b.shape[2] // tn
        b_spec = pl.BlockSpec((None, tk, tn), lambda i, j, k: (j // per, k, j % per))
    elif tb:
        n = b.shape[0]
        b_spec = pl.BlockSpec((tn, tk), lambda i, j, k: (j, k))
    else:
        n = b.shape[1]
        b_spec = pl.BlockSpec((tk, tn), lambda i, j, k: (k, j))
    n, k_dim = n_used or n, k_used or k_dim
    assert m % tm == 0 and n % tn == 0 and k_dim % tk == 0, (name, m, n, k_dim, tm, tn, tk)
    nk = k_dim // tk
    a_spec = pl.BlockSpec((tk, tm), lambda i, j, k: (k, i)) if ta else pl.BlockSpec((tm, tk), lambda i, j, k: (i, k))
    n_extra, n_out = len(extra), len(outs)

    def wrap(index_map):
        return lambda i, j, k: index_map(i, j)

    def spec(block, index_map):
        if n_outer:
            return pl.BlockSpec(block, lambda j, i, k: index_map(i, j, k))
        return pl.BlockSpec(block, index_map)

    row_axis = 1 if n_outer else 0

    def body_one_step(*refs):
        ex = refs[2:2 + n_extra]
        out = refs[2 + n_extra:2 + n_extra + n_out]
        epilogue(_bdot(refs[0][...], refs[1][...], ta, tb), ex, out, pl.program_id(row_axis))

    def body(*refs):
        a_ref, b_ref = refs[0], refs[1]
        ex = refs[2:2 + n_extra]
        out = refs[2 + n_extra:2 + n_extra + n_out]
        acc = refs[-1]
        i, k = pl.program_id(row_axis), pl.program_id(2)
        part = _bdot(a_ref[...], b_ref[...], ta, tb)

        @pl.when(k == 0)
        def _():
            acc[...] = part

        @pl.when(jnp.logical_and(k > 0, k < nk - 1))
        def _():
            acc[...] += part

        @pl.when(k == nk - 1)
        def _():
            epilogue(acc[...] + part, ex, out, i)

    sem = ("arbitrary",) * 3 if sequential else ("parallel", "parallel", "arbitrary")
    res = pl.pallas_call(
        body_one_step if nk == 1 else body, name=name,
        grid=(n // tn, m // tm, nk) if n_outer else (m // tm, n // tn, nk),
        in_specs=[spec(a_spec.block_shape, a_spec.index_map), spec(b_spec.block_shape, b_spec.index_map)]
        + [spec(bs, wrap(im)) for _, bs, im in extra],
        out_specs=[spec(bs, wrap(im)) for _, bs, im in outs],
        out_shape=[s for s, _, _ in outs],
        scratch_shapes=[] if nk == 1 else [pltpu.VMEM((tm, tn), F32)],
        compiler_params=_params(*sem),
    )(a, b, *[x for x, _, _ in extra])
    return res


def _tile(i, j):
    return (i, j)


def _plain(name, a, b, *, ta=False, tb=False, tm, tn, tk, out_dtype, b_blocks=None, out3=None, n_used=None,
           n_outer=False):
    m = a.shape[1] if ta else a.shape[0]
    if b_blocks:
        n = b.shape[1] if tb else b.shape[0] * b.shape[2]
    else:
        n = n_used or (b.shape[0] if tb else b.shape[1])

    def epi(acc, ex, out, i):
        out[0][...] = acc.astype(out_dtype)

    if out3:
        per = (n // out3) // tn
        spec = (jax.ShapeDtypeStruct((out3, m, n // out3), out_dtype), (None, tm, tn),
                lambda i, j: (j // per, i, j % per))
    else:
        spec = (jax.ShapeDtypeStruct((m, n), out_dtype), (tm, tn), _tile)
    return _matmul(name, a, b, ta=ta, tb=tb, tm=tm, tn=tn, tk=tk, outs=[spec], epilogue=epi,
                   b_blocks=b_blocks, n_used=n_used, n_outer=n_outer)[0]


def _ln_forward(name, a, b, res, gamma, beta, *, tm, tk, want_h=True):
    m, n = res.shape

    def epi(acc, ex, out, i):
        u = ALPHA * ex[0][...] + acc
        mu = jnp.mean(u, axis=-1, keepdims=True)
        xc = u - mu
        var = jnp.mean(xc * xc, axis=-1, keepdims=True)
        rstd = lax.rsqrt(var + LN_EPS)
        xhat = xc * rstd
        out[-2][...] = xhat
        out[-1][...] = rstd
        if want_h:
            h = xhat * ex[1][...] + ex[2][...]
            out[0][...] = h
            out[1][...] = h.astype(BF16)

    row = lambda i, j: (i, 0)
    vec = lambda i, j: (0, 0)
    outs = [(jax.ShapeDtypeStruct((m, n), F32), (tm, n), row), (jax.ShapeDtypeStruct((m, n), BF16), (tm, n), row),
            (jax.ShapeDtypeStruct((m, n), F32), (tm, n), row), (jax.ShapeDtypeStruct((m, 1), F32), (tm, 1), row)]
    return _matmul(
        name, a, b, tm=tm, tn=n, tk=tk,
        extra=[(res, (tm, n), row), (gamma, (1, n), vec), (beta, (1, n), vec)],
        outs=outs if want_h else outs[2:], epilogue=epi)


def _ln_backward_math(dy, xhat, rstd, gamma):
    dxhat = dy * gamma
    m1 = jnp.mean(dxhat, axis=-1, keepdims=True)
    m2 = jnp.mean(dxhat * xhat, axis=-1, keepdims=True)
    du = rstd * (dxhat - m1 - xhat * m2)
    return du, jnp.sum(dy * xhat, axis=0, keepdims=True), jnp.sum(dy, axis=0, keepdims=True)


def _ln_backward(name, a, b, dres, xhat, rstd, gamma, *, tm, tk, b_blocks=None, tb=True):
    m, n = dres.shape

    def epi(acc, ex, out, i):
        dy = acc + ALPHA * ex[0][...]
        du, dg, db = _ln_backward_math(dy, ex[1][...], ex[2][...], ex[3][...])
        out[0][...] = du
        out[1][...] = du.astype(BF16)
        first = i == 0

        @pl.when(first)
        def _():
            out[2][...] = dg
            out[3][...] = db

        @pl.when(jnp.logical_not(first))
        def _():
            out[2][...] += dg
            out[3][...] += db

    row = lambda i, j: (i, 0)
    vec = lambda i, j: (0, 0)
    return _matmul(
        name, a, b, tb=tb, tm=tm, tn=n, tk=tk, b_blocks=b_blocks, sequential=True,
        extra=[(dres, (tm, n), row), (xhat, (tm, n), row), (rstd, (tm, 1), row), (gamma, (1, n), vec)],
        outs=[(jax.ShapeDtypeStruct((m, n), F32), (tm, n), row),
              (jax.ShapeDtypeStruct((m, n), BF16), (tm, n), row),
              (jax.ShapeDtypeStruct((1, n), F32), (1, n), vec),
              (jax.ShapeDtypeStruct((1, n), F32), (1, n), vec)],
        epilogue=epi)


def _shift_down(x, k):
    row = lax.broadcasted_iota(jnp.int32, x.shape, 0)
    return jnp.where(row >= k, pltpu.roll(x, k, axis=0), 0.0)


def _shift_up(x, k):
    t = x.shape[0]
    row = lax.broadcasted_iota(jnp.int32, x.shape, 0)
    return jnp.where(row < t - k, pltpu.roll(x, t - k, axis=0), 0.0)


def _conv_silu_norm(x, w, normalise):
    kk = w.shape[0]
    c = x * w[kk - 1:kk, :]
    for j in range(kk - 1):
        c = c + _shift_down(x, kk - 1 - j) * w[j:j + 1, :]
    sg = _sigmoid(c)
    s = c * sg
    r = lax.rsqrt(jnp.sum(s * s, axis=-1, keepdims=True) + NORM_EPS)
    y = jnp.where(normalise, s * r, s)
    return c, sg, s, r, y


def _gdn_pre(proj, conv_w, heads):
    t = proj.shape[0]
    kk = conv_w.shape[0]

    def body(x_ref, w_ref, o_ref):
        normalise = pl.program_id(0) < 2
        o_ref[...] = _conv_silu_norm(x_ref[...], w_ref[...], normalise)[4]

    col = lambda s, h: (0, s * heads + h)
    return pl.pallas_call(
        body, name="gdn_pre", grid=(3, heads),
        in_specs=[pl.BlockSpec((t, HEAD_DIM), col), pl.BlockSpec((kk, HEAD_DIM), col)],
        out_specs=pl.BlockSpec((t, HEAD_DIM), col),
        out_shape=jax.ShapeDtypeStruct((t, 3 * heads * HEAD_DIM), F32),
        compiler_params=_params("parallel", "parallel"),
    )(proj, conv_w)


def _gdn_pre_backward(proj, conv_w, dqkv, heads):
    t = proj.shape[0]
    kk = conv_w.shape[0]

    def body(x_ref, w_ref, dy_ref, dx_ref, dw_ref):
        normalise = pl.program_id(0) < 2
        x = x_ref[...]
        w = w_ref[...]
        dy = dy_ref[...]
        c, sg, s, r, y = _conv_silu_norm(x, w, normalise)
        ds_norm = r * (dy - y * jnp.sum(dy * y, axis=-1, keepdims=True))
        ds = jnp.where(normalise, ds_norm, dy)
        dc = ds * (sg * (1.0 + c * (1.0 - sg)))
        dx = dc * w[kk - 1:kk, :]
        rows = [None] * kk
        rows[kk - 1] = jnp.sum(dc * x, axis=0, keepdims=True)
        for j in range(kk - 1):
            lag = kk - 1 - j
            dx = dx + _shift_up(dc, lag) * w[j:j + 1, :]
            rows[j] = jnp.sum(dc * _shift_down(x, lag), axis=0, keepdims=True)
        dx_ref[...] = dx.astype(BF16)
        dw_ref[...] = jnp.concatenate(rows, axis=0)

    col = lambda s, h: (0, s * heads + h)
    return pl.pallas_call(
        body, name="gdn_pre_bwd", grid=(3, heads),
        in_specs=[pl.BlockSpec((t, HEAD_DIM), col), pl.BlockSpec((kk, HEAD_DIM), col),
                  pl.BlockSpec((t, HEAD_DIM), col)],
        out_specs=[pl.BlockSpec((t, HEAD_DIM), col), pl.BlockSpec((kk, HEAD_DIM), col)],
        out_shape=[jax.ShapeDtypeStruct((t, 3 * heads * HEAD_DIM), BF16),
                   jax.ShapeDtypeStruct((kk, 3 * heads * HEAD_DIM), F32)],
        compiler_params=_params("parallel", "parallel"),
    )(proj, conv_w, dqkv)


def _gate_vectors(a_log, dt_bias, heads):
    pad = lambda v: jnp.pad(v.astype(F32), ((0, 0), (heads, HEAD_DIM - 2 * heads)))
    return pad(jnp.exp(a_log.astype(F32))), pad(dt_bias)


def _softplus(x):
    return jnp.maximum(x, 0.0) + jnp.log(1.0 + jnp.exp(-jnp.abs(x)))


def _gates_epilogue(heads):
    def epi(acc, ex, out, i):
        lane = lax.broadcasted_iota(jnp.int32, acc.shape, 1)
        beta = _sigmoid(acc)
        g = -ex[0][...] * _softplus(acc + ex[1][...])
        out[0][...] = acc
        out[1][...] = jnp.where(lane < heads, beta, jnp.where(lane < 2 * heads, g, 0.0))
    return epi


def _gates_backward(ba, bg, dbg, ea, dtb, heads):
    t = ba.shape[0]

    def body(ba_ref, bg_ref, d_ref, ea_ref, dt_ref, dba_ref, dal_ref, ddt_ref):
        lane = lax.broadcasted_iota(jnp.int32, (t, HEAD_DIM), 1)
        bgv = bg_ref[...]
        d = d_ref[...]
        db = d * bgv * (1.0 - bgv)
        da = -d * ea_ref[...] * _sigmoid(ba_ref[...] + dt_ref[...])
        is_g = jnp.logical_and(lane >= heads, lane < 2 * heads)
        dba = jnp.where(lane < heads, db, jnp.where(is_g, da, 0.0))
        dba_ref[...] = dba.astype(BF16)
        dal_ref[...] = jnp.sum(jnp.where(is_g, d * bgv, 0.0), axis=0, keepdims=True)
        ddt_ref[...] = jnp.sum(jnp.where(is_g, da, 0.0), axis=0, keepdims=True)

    full = pl.BlockSpec((t, HEAD_DIM), lambda: (0, 0))
    vec = pl.BlockSpec((1, HEAD_DIM), lambda: (0, 0))
    return pl.pallas_call(
        body, name="gates_bwd", grid=(),
        in_specs=[full, full, full, vec, vec], out_specs=[full, vec, vec],
        out_shape=[jax.ShapeDtypeStruct((t, HEAD_DIM), BF16), jax.ShapeDtypeStruct((1, HEAD_DIM), F32),
                   jax.ShapeDtypeStruct((1, HEAD_DIM), F32)],
        compiler_params=pltpu.CompilerParams(vmem_limit_bytes=VMEM_LIMIT),
    )(ba, bg, dbg, ea, dtb)


class _Chunk:
    pass


def _split2(x):
    hi = x.astype(BF16)
    return hi, (x - hi.astype(F32)).astype(BF16)


def _split3(x):
    hi = x.astype(BF16)
    rest = x - hi.astype(F32)
    mid = rest.astype(BF16)
    return hi, mid, (rest - mid.astype(F32)).astype(BF16)


def _dot_mask(mask, x, ta=False):
    hi, mid, lo = _split3(x)
    return _bdot(mask, hi, ta=ta) + (_bdot(mask, mid, ta=ta) + _bdot(mask, lo, ta=ta))


def _transpose_by_identity(x):
    r = x.shape[0]
    eye = (lax.broadcasted_iota(jnp.int32, (r, r), 0) == lax.broadcasted_iota(jnp.int32, (r, r), 1)).astype(BF16)
    hi, mid, lo = _split3(x)
    return _bdot(hi, eye, ta=True) + (_bdot(mid, eye, ta=True) + _bdot(lo, eye, ta=True))


def _dot22(a, b, ta=False, tb=False):
    ah, al = _split2(a)
    bh, bl = _split2(b)
    return _bdot(ah, bh, ta, tb) + (_bdot(ah, bl, ta, tb) + _bdot(al, bh, ta, tb))


def _chunk_gates(bg, heads):
    n = CHUNK
    row = lax.broadcasted_iota(jnp.int32, (n, n), 0)
    col = lax.broadcasted_iota(jnp.int32, (n, n), 1)
    lane = lax.broadcasted_iota(jnp.int32, bg.shape, 1)
    graw = jnp.where(jnp.logical_and(lane >= heads, lane < 2 * heads), bg, 0.0)
    gc = _dot_mask((row >= col).astype(BF16), graw)
    return gc, _transpose_by_identity(gc)


def _in_lockstep(generators):
    results = [None] * len(generators)
    live = list(enumerate(generators))
    while live:
        still = []
        for i, gen in live:
            try:
                next(gen)
                still.append((i, gen))
            except StopIteration as stop:
                results[i] = stop.value
        live = still
    return results


def _chunk_local(q, k, v, beta, gc, grow):
    c = _Chunk()
    n = CHUNK
    row = lax.broadcasted_iota(jnp.int32, (n, n), 0)
    col = lax.broadcasted_iota(jnp.int32, (n, n), 1)
    c.tri = row >= col
    c.strict = row > col
    eye = row == col
    c.gcb = jnp.broadcast_to(gc, (n, HEAD_DIM))
    c.decay = jnp.where(c.tri, jnp.exp(jnp.where(c.tri, gc - grow, 0.0)), 0.0)
    c.eg = jnp.exp(c.gcb)
    glast = c.gcb[n - 1:n, :]
    c.egl = jnp.exp(glast)
    c.ekl = jnp.exp(glast - c.gcb)
    c.beta = beta
    c.q = q * (HEAD_DIM ** -0.5)
    c.k = k
    c.v = v
    c.kb = k * beta
    c.vb = v * beta
    c.kg = c.kb * c.eg
    both = _bdot(jnp.concatenate([c.kb, c.q], axis=0), k, tb=True)
    yield
    c.L = jnp.where(c.strict, both[:n] * c.decay, 0.0)
    c.A = jnp.where(c.tri, both[n:] * c.decay, 0.0)
    x = -c.L
    tinv = eye.astype(F32) + x
    p = _dot22(x, x)
    yield
    for _ in range(int(math.log2(n)) - 2):
        both = _dot22(jnp.concatenate([p, tinv], axis=0), p)
        yield
        p, tinv = both[:n], tinv + both[n:]
    c.T = tinv + _dot22(tinv, p)
    yield
    tinv = c.T
    uw = _dot22(tinv, jnp.concatenate([c.vb, c.kg], axis=1))
    yield
    c.u, c.w = uw[:, :HEAD_DIM], uw[:, HEAD_DIM:]
    c.qg = c.q * c.eg
    c.kdec = k * c.ekl
    return c


def _gdn_core(qkv, bg, heads):
    t = qkv.shape[0]
    nchunk = t // CHUNK

    gw = heads * HEAD_DIM

    def body(qkv_ref, bg_ref, o_ref, s_ref, state):
        @pl.when(pl.program_id(0) == 0)
        def _():
            state[...] = jnp.zeros_like(state)

        bg_v = bg_ref[...]
        gc_all, gc_rows = _chunk_gates(bg_v, heads)
        def one_head(h):
            col = lambda s: pl.ds(s * gw + h * HEAD_DIM, HEAD_DIM)
            c = yield from _chunk_local(qkv_ref[:, col(0)], qkv_ref[:, col(1)], qkv_ref[:, col(2)], bg_v[:, h:h + 1],
                                        gc_all[:, heads + h:heads + h + 1], gc_rows[heads + h:heads + h + 1, :])
            s0 = state[h]
            v_new = c.u - _bdot(c.w, s0)
            yield
            o = _bdot(c.qg, s0) + _bdot(c.A, v_new)
            return s0, o, s0 * c.egl + _bdot(c.kdec, v_new, ta=True)

        results = _in_lockstep([one_head(h) for h in range(heads)])
        for h, (s0, o, s1) in enumerate(results):
            s_ref[h, 0] = s0
            o_ref[:, pl.ds(h * HEAD_DIM, HEAD_DIM)] = o
            state[h] = s1

    return pl.pallas_call(
        body, name="gdn_core", grid=(nchunk,),
        in_specs=[pl.BlockSpec((CHUNK, 3 * gw), lambda n: (n, 0)), pl.BlockSpec((CHUNK, HEAD_DIM), lambda n: (n, 0))],
        out_specs=[pl.BlockSpec((CHUNK, gw), lambda n: (n, 0)),
                   pl.BlockSpec((heads, 1, HEAD_DIM, HEAD_DIM), lambda n: (0, n, 0, 0))],
        out_shape=[jax.ShapeDtypeStruct((t, gw), F32),
                   jax.ShapeDtypeStruct((heads, nchunk, HEAD_DIM, HEAD_DIM), F32)],
        scratch_shapes=[pltpu.VMEM((heads, HEAD_DIM, HEAD_DIM), F32)],
        compiler_params=_params("arbitrary"),
    )(qkv, bg)


def _gdn_core_backward(qkv, bg, states, do, heads):
    t = qkv.shape[0]
    nchunk = t // CHUNK
    n = CHUNK

    def one_head(chunk_local, s0, d_out, ds1):
        c = yield from chunk_local
        v_new = c.u - _bdot(c.w, s0)
        dqg = _bdot(d_out, s0, tb=True)
        ds0 = _bdot(c.qg, d_out, ta=True) + ds1 * c.egl
        dv_new = _bdot(c.A, d_out, ta=True) + _bdot(c.kdec, ds1)
        yield
        dA = jnp.where(c.tri, _bdot(d_out, v_new, tb=True), 0.0)
        dkdec = _bdot(v_new, ds1, tb=True)
        dgl = jnp.sum(jnp.sum(ds1 * s0, axis=1, keepdims=True), axis=0, keepdims=True) * c.egl
        dw = -_bdot(dv_new, s0, tb=True)
        ds0 = ds0 - _bdot(c.w, dv_new, ta=True)
        yield
        both = _dot22(c.T, jnp.concatenate([dv_new, dw], axis=1), ta=True)
        yield
        dvb, dkg = both[:, :HEAD_DIM], both[:, HEAD_DIM:]
        dL = jnp.where(c.strict, -(_bdot(dvb, c.u, tb=True) + _bdot(dkg, c.w, tb=True)), 0.0)
        yield
        dm1 = dL * c.decay
        dkb = _bdot(dm1, c.k) + dkg * c.eg
        dk = _bdot(dm1, c.kb, ta=True)
        dm2 = dA * c.decay
        dq = _bdot(dm2, c.k) + dqg * c.eg
        dk = dk + _bdot(dm2, c.q, ta=True) + dkdec * c.ekl + dkb * c.beta
        pm = dL * c.L + dA * c.A
        ones = jnp.ones((n, HEAD_DIM), BF16)
        pm_hi, pm_lo = _split2(pm)
        colsum = _bdot(pm_hi, ones, ta=True) + _bdot(pm_lo, ones, ta=True)
        tk_ = jnp.sum(dkdec * c.kdec, axis=1, keepdims=True)
        dgc = (jnp.sum(pm, axis=1, keepdims=True) - colsum
               + jnp.sum(dqg * c.qg, axis=1, keepdims=True)
               - tk_
               + jnp.sum(dkg * c.kg, axis=1, keepdims=True))
        dgl = dgl + jnp.sum(tk_, axis=0, keepdims=True)
        rowi = lax.broadcasted_iota(jnp.int32, (n, HEAD_DIM), 0)
        dgc = dgc + jnp.where(rowi == n - 1, dgl, 0.0)
        dbeta = jnp.sum(dkb * c.k, axis=1, keepdims=True) + jnp.sum(dvb * c.v, axis=1, keepdims=True)
        return dq * (HEAD_DIM ** -0.5), dk, dvb * c.beta, dbeta, dgc, ds0

    gw = heads * HEAD_DIM

    def body(qkv_ref, bg_ref, s_ref, do_ref, dqkv_ref, dbg_ref, dstate):
        @pl.when(pl.program_id(0) == 0)
        def _():
            dstate[...] = jnp.zeros_like(dstate)

        bg_v = bg_ref[...]
        gc_all, gc_rows = _chunk_gates(bg_v, heads)
        lane = lax.broadcasted_iota(jnp.int32, (n, HEAD_DIM), 1)
        dgates = jnp.zeros((n, HEAD_DIM), F32)
        chains = []
        for h in range(heads):
            col = lambda s: pl.ds(s * gw + h * HEAD_DIM, HEAD_DIM)
            c = _chunk_local(qkv_ref[:, col(0)], qkv_ref[:, col(1)], qkv_ref[:, col(2)], bg_v[:, h:h + 1],
                             gc_all[:, heads + h:heads + h + 1], gc_rows[heads + h:heads + h + 1, :])
            chains.append(one_head(c, s_ref[h, 0], do_ref[:, pl.ds(h * HEAD_DIM, HEAD_DIM)], dstate[h]))
        results = _in_lockstep(chains)
        for h, (dq, dk, dv, dbeta, dgc, ds0) in enumerate(results):
            dgates = jnp.where(lane == h, dbeta, jnp.where(lane == heads + h, dgc, dgates))
        for h, (dq, dk, dv, dbeta, dgc, ds0) in enumerate(results):
            dqkv_ref[:, pl.ds(h * HEAD_DIM, HEAD_DIM)] = dq
            dqkv_ref[:, pl.ds(gw + h * HEAD_DIM, HEAD_DIM)] = dk
            dqkv_ref[:, pl.ds(2 * gw + h * HEAD_DIM, HEAD_DIM)] = dv
            dstate[h] = ds0
        row = lax.broadcasted_iota(jnp.int32, (n, n), 0)
        colm = lax.broadcasted_iota(jnp.int32, (n, n), 1)
        draw = _dot_mask((row >= colm).astype(BF16), dgates, ta=True)
        dbg_ref[...] = jnp.where(lane < heads, dgates, draw)

    last = nchunk - 1
    return pl.pallas_call(
        body, name="gdn_core_bwd", grid=(nchunk,),
        in_specs=[pl.BlockSpec((CHUNK, 3 * gw), lambda i: (last - i, 0)),
                  pl.BlockSpec((CHUNK, HEAD_DIM), lambda i: (last - i, 0)),
                  pl.BlockSpec((heads, 1, HEAD_DIM, HEAD_DIM), lambda i: (0, last - i, 0, 0)),
                  pl.BlockSpec((CHUNK, gw), lambda i: (last - i, 0))],
        out_specs=[pl.BlockSpec((CHUNK, 3 * gw), lambda i: (last - i, 0)),
                   pl.BlockSpec((CHUNK, HEAD_DIM), lambda i: (last - i, 0))],
        out_shape=[jax.ShapeDtypeStruct((t, 3 * gw), F32), jax.ShapeDtypeStruct((t, HEAD_DIM), F32)],
        scratch_shapes=[pltpu.VMEM((heads, HEAD_DIM, HEAD_DIM), F32)],
        compiler_params=_params("arbitrary"),
    )(qkv, bg, states, do)


def _gdn_post(o, proj, z_col0, norm_w, heads, tt):
    t = o.shape[0]
    zb = z_col0 // HEAD_DIM

    def body(o_ref, z_ref, w_ref, out_ref):
        ov = o_ref[...]
        z = z_ref[...]
        rms = lax.rsqrt(jnp.mean(ov * ov, axis=-1, keepdims=True) + NORM_EPS)
        out_ref[...] = (ov * rms * w_ref[...] * (z * _sigmoid(z))).astype(BF16)

    return pl.pallas_call(
        body, name="gdn_post", grid=(t // tt, heads),
        in_specs=[pl.BlockSpec((tt, HEAD_DIM), lambda i, h: (i, h)),
                  pl.BlockSpec((tt, HEAD_DIM), lambda i, h: (i, zb + h)),
                  pl.BlockSpec((1, HEAD_DIM), lambda i, h: (0, 0))],
        out_specs=pl.BlockSpec((tt, HEAD_DIM), lambda i, h: (i, h)),
        out_shape=jax.ShapeDtypeStruct((t, heads * HEAD_DIM), BF16),
        compiler_params=_params("parallel", "parallel"),
    )(o, proj, norm_w)


def _gdn_post_backward(dcat, o, proj, z_col0, norm_w, heads, tt):
    t = o.shape[0]
    zb = z_col0 // HEAD_DIM

    def body(d_ref, o_ref, z_ref, w_ref, do_ref, dz_ref, dw_ref):
        d = d_ref[...]
        ov = o_ref[...]
        z = z_ref[...]
        w = w_ref[...]
        rms = lax.rsqrt(jnp.mean(ov * ov, axis=-1, keepdims=True) + NORM_EPS)
        ohat = ov * rms
        sg = _sigmoid(z)
        gate = z * sg
        dz_ref[...] = (d * ohat * w * (sg * (1.0 + z * (1.0 - sg)))).astype(BF16)
        don = d * gate
        dohat = don * w
        do_ref[...] = rms * (dohat - ohat * jnp.mean(dohat * ohat, axis=-1, keepdims=True))
        dw = jnp.sum(don * ohat, axis=0, keepdims=True)
        first = jnp.logical_and(pl.program_id(0) == 0, pl.program_id(1) == 0)

        @pl.when(first)
        def _():
            dw_ref[...] = dw

        @pl.when(jnp.logical_not(first))
        def _():
            dw_ref[...] += dw

    blk = pl.BlockSpec((tt, HEAD_DIM), lambda i, h: (i, h))
    return pl.pallas_call(
        body, name="gdn_post_bwd", grid=(t // tt, heads),
        in_specs=[blk, blk, pl.BlockSpec((tt, HEAD_DIM), lambda i, h: (i, zb + h)),
                  pl.BlockSpec((1, HEAD_DIM), lambda i, h: (0, 0))],
        out_specs=[blk, blk, pl.BlockSpec((1, HEAD_DIM), lambda i, h: (0, 0))],
        out_shape=[jax.ShapeDtypeStruct((t, heads * HEAD_DIM), F32),
                   jax.ShapeDtypeStruct((t, heads * HEAD_DIM), BF16),
                   jax.ShapeDtypeStruct((1, HEAD_DIM), F32)],
        compiler_params=_params("arbitrary", "arbitrary"),
    )(dcat, o, proj, norm_w)


def _pool_select(levels, group):
    out = levels[-1]
    for gi in range(len(levels) - 2, -1, -1):
        out = jnp.where(group == gi, levels[gi], out)
    return out


def _pool_counts(t, width, group):
    pos = lax.broadcasted_iota(jnp.int32, (t, width), 0)
    win = jnp.left_shift(2, group)
    return jnp.minimum(pos + 1, win).astype(F32)


def _pooled(p, group):
    levels, s, step = [], p, 1
    for _ in POOL_WINDOWS:
        s = s + _shift_down(s, step)
        levels.append(s)
        step *= 2
    cnt = _pool_counts(p.shape[0], p.shape[1], group)
    return _pool_select(levels, group) / cnt - p, cnt


def _pool_forward(proj, p_col0, pool_w, pool_scale):
    t = proj.shape[0]
    groups, cg, _ = pool_w.shape
    pb = p_col0 // cg

    def body(p_ref, w_ref, s_ref, o_ref):
        pooled, _ = _pooled(p_ref[...], pl.program_id(0))
        o_ref[...] = (_bdot(pooled, w_ref[0]) * s_ref[...]).astype(BF16)

    return pl.pallas_call(
        body, name="pool_fwd", grid=(groups,),
        in_specs=[pl.BlockSpec((t, cg), lambda g: (0, pb + g)), pl.BlockSpec((1, cg, cg), lambda g: (g, 0, 0)),
                  pl.BlockSpec((1, cg), lambda g: (0, g))],
        out_specs=pl.BlockSpec((t, cg), lambda g: (0, g)),
        out_shape=jax.ShapeDtypeStruct((t, groups * cg), BF16),
        compiler_params=_params("parallel"),
    )(proj, pool_w, pool_scale)


def _pool_backward(dcat, d_col0, proj, p_col0, pool_w, pool_scale):
    t = proj.shape[0]
    groups, cg, _ = pool_w.shape
    pb = p_col0 // cg
    db = d_col0 // cg

    def body(d_ref, p_ref, w_ref, s_ref, dp_ref, dw_ref, ds_ref):
        group = pl.program_id(0)
        pooled, cnt = _pooled(p_ref[...], group)
        w = w_ref[0]
        d = d_ref[...]
        mixed = _bdot(pooled, w)
        ds_ref[...] = jnp.sum(d * mixed, axis=0, keepdims=True)
        dmixed = d * s_ref[...]
        dw_ref[0] = _bdot(pooled, dmixed, ta=True)
        dpooled = _bdot(dmixed, w, tb=True)
        levels, s, step = [], dpooled / cnt, 1
        for _ in POOL_WINDOWS:
            s = s + _shift_up(s, step)
            levels.append(s)
            step *= 2
        dp_ref[...] = (_pool_select(levels, group) - dpooled).astype(BF16)

    return pl.pallas_call(
        body, name="pool_bwd", grid=(groups,),
        in_specs=[pl.BlockSpec((t, cg), lambda g: (0, db + g)), pl.BlockSpec((t, cg), lambda g: (0, pb + g)),
                  pl.BlockSpec((1, cg, cg), lambda g: (g, 0, 0)), pl.BlockSpec((1, cg), lambda g: (0, g))],
        out_specs=[pl.BlockSpec((t, cg), lambda g: (0, g)), pl.BlockSpec((1, cg, cg), lambda g: (g, 0, 0)),
                   pl.BlockSpec((1, cg), lambda g: (0, g))],
        out_shape=[jax.ShapeDtypeStruct((t, groups * cg), BF16), jax.ShapeDtypeStruct((groups, cg, cg), F32),
                   jax.ShapeDtypeStruct((1, groups * cg), F32)],
        compiler_params=_params("parallel"),
    )(dcat, proj, pool_w, pool_scale)


def _attention(q, k, v, tq):
    t, d = q.shape
    m = k.shape[0]
    dh = d // XATTN_HEADS
    scale = dh ** -0.5

    def body(q_ref, k_ref, v_ref, o_ref):
        s = _bdot(q_ref[...], k_ref[...], tb=True) * scale
        s = s - jnp.max(s, axis=-1, keepdims=True)
        e = jnp.exp(s)
        p = e / jnp.sum(e, axis=-1, keepdims=True)
        o_ref[...] = _bdot(p, v_ref[...]).astype(BF16)

    return pl.pallas_call(
        body, name="xattn_fwd", grid=(XATTN_HEADS, t // tq),
        in_specs=[pl.BlockSpec((tq, dh), lambda h, i: (i, h)), pl.BlockSpec((m, dh), lambda h, i: (0, h)),
                  pl.BlockSpec((m, dh), lambda h, i: (0, h))],
        out_specs=pl.BlockSpec((tq, dh), lambda h, i: (i, h)),
        out_shape=jax.ShapeDtypeStruct((t, d), BF16),
        compiler_params=_params("parallel", "parallel"),
    )(q, k, v)


def _attention_backward(q, k, v, do, tq):
    t, d = q.shape
    m = k.shape[0]
    dh = d // XATTN_HEADS
    scale = dh ** -0.5

    def body(q_ref, k_ref, v_ref, do_ref, dq_ref, dk_ref, dv_ref, dk_acc, dv_acc):
        i = pl.program_id(1)
        qv, kv, vv, dov = q_ref[...], k_ref[...], v_ref[...], do_ref[...]
        s = _bdot(qv, kv, tb=True) * scale
        s = s - jnp.max(s, axis=-1, keepdims=True)
        e = jnp.exp(s)
        p = e / jnp.sum(e, axis=-1, keepdims=True)
        dp = _bdot(dov, vv, tb=True)
        ds = p * (dp - jnp.sum(dp * p, axis=-1, keepdims=True)) * scale
        dq_ref[...] = _bdot(ds, kv).astype(BF16)
        dv_part = _bdot(p, dov, ta=True)
        dk_part = _bdot(ds, qv, ta=True)

        @pl.when(i == 0)
        def _():
            dk_acc[...] = dk_part
            dv_acc[...] = dv_part

        @pl.when(i > 0)
        def _():
            dk_acc[...] += dk_part
            dv_acc[...] += dv_part

        @pl.when(i == pl.num_programs(1) - 1)
        def _():
            dk_ref[...] = dk_acc[...].astype(BF16)
            dv_ref[...] = dv_acc[...].astype(BF16)

    qblk = pl.BlockSpec((tq, dh), lambda h, i: (i, h))
    kblk = pl.BlockSpec((m, dh), lambda h, i: (0, h))
    return pl.pallas_call(
        body, name="xattn_bwd", grid=(XATTN_HEADS, t // tq),
        in_specs=[qblk, kblk, kblk, qblk],
        out_specs=[qblk, kblk, kblk],
        out_shape=[jax.ShapeDtypeStruct((t, d), BF16), jax.ShapeDtypeStruct((m, d), BF16),
                   jax.ShapeDtypeStruct((m, d), BF16)],
        scratch_shapes=[pltpu.VMEM((m, dh), F32), pltpu.VMEM((m, dh), F32)],
        compiler_params=_params("parallel", "arbitrary"),
    )(q, k, v, do)


def _ln_backward_rows(name, dmain, dres, xhat, rstd, gamma, tm):
    t, d = xhat.shape

    def body(m_ref, r_ref, x_ref, s_ref, g_ref, du_ref, dub_ref, dg_ref, db_ref):
        du, dg, db = _ln_backward_math(m_ref[...] + ALPHA * r_ref[...], x_ref[...], s_ref[...], g_ref[...])
        du_ref[...] = du
        dub_ref[...] = du.astype(BF16)
        first = pl.program_id(0) == 0

        @pl.when(first)
        def _():
            dg_ref[...] = dg
            db_ref[...] = db

        @pl.when(jnp.logical_not(first))
        def _():
            dg_ref[...] += dg
            db_ref[...] += db

    row = pl.BlockSpec((tm, d), lambda i: (i, 0))
    vec = pl.BlockSpec((1, d), lambda i: (0, 0))
    return pl.pallas_call(
        body, name=name, grid=(t // tm,),
        in_specs=[row, row, row, pl.BlockSpec((tm, 1), lambda i: (i, 0)), vec],
        out_specs=[row, row, vec, vec],
        out_shape=[jax.ShapeDtypeStruct((t, d), F32), jax.ShapeDtypeStruct((t, d), BF16),
                   jax.ShapeDtypeStruct((1, d), F32), jax.ShapeDtypeStruct((1, d), F32)],
        compiler_params=_params("arbitrary"),
    )(dmain, dres, xhat, rstd, gamma)


def _loss_and_ln_backward(xhat, rstd, gamma, beta, target, tm):
    t, d = xhat.shape

    def body(x_ref, r_ref, g_ref, b_ref, t_ref, du_ref, dub_ref, dg_ref, db_ref, loss_ref):
        xh = x_ref[...]
        g = g_ref[...]
        diff = xh * g + b_ref[...] - t_ref[...]
        part = jnp.sum(jnp.sum(diff * diff, axis=1, keepdims=True), axis=0, keepdims=True) * (0.5 / d)
        dy = diff * (1.0 / d)
        du, dg, db = _ln_backward_math(dy, xh, r_ref[...], g)
        du_ref[...] = du
        dub_ref[...] = du.astype(BF16)
        lossrow = jnp.broadcast_to(part, (1, HEAD_DIM))
        first = pl.program_id(0) == 0

        @pl.when(first)
        def _():
            dg_ref[...] = dg
            db_ref[...] = db
            loss_ref[...] = lossrow

        @pl.when(jnp.logical_not(first))
        def _():
            dg_ref[...] += dg
            db_ref[...] += db
            loss_ref[...] += lossrow

    row = pl.BlockSpec((tm, d), lambda i: (i, 0))
    vec = pl.BlockSpec((1, d), lambda i: (0, 0))
    return pl.pallas_call(
        body, name="loss_ln3_bwd", grid=(t // tm,),
        in_specs=[row, pl.BlockSpec((tm, 1), lambda i: (i, 0)), vec, vec, row],
        out_specs=[row, row, vec, vec, pl.BlockSpec((1, HEAD_DIM), lambda i: (0, 0))],
        out_shape=[jax.ShapeDtypeStruct((t, d), F32), jax.ShapeDtypeStruct((t, d), BF16),
                   jax.ShapeDtypeStruct((1, d), F32), jax.ShapeDtypeStruct((1, d), F32),
                   jax.ShapeDtypeStruct((1, HEAD_DIM), F32)],
        compiler_params=_params("arbitrary"),
    )(xhat, rstd, gamma, beta, target)


def _after(token, a):
    return a if token is None else a + token[:1, :1].astype(a.dtype)


def _pick(n, prefs):
    for p in prefs:
        if n % p == 0:
            return p
    return n


def _local_step(x, mem, target, w, token=None):
    t, d = x.shape
    heads = w["a_log"].shape[1]
    gw = heads * HEAD_DIM
    groups, cg, _ = w["pool_w"].shape
    pw = groups * cg
    n_main = 4 * gw + pw
    in_cols = n_main + 2 * heads
    s_in = w["w_in_t"].shape[0]

    tm = _pick(t, (512, 256, 128))
    tm_ln = _pick(t, (256, 128))
    tm_big = _pick(t, (1024, 512, 256, 128))
    tk = _pick(d, K_STEPS)

    w_in_t = w["w_in_t"].reshape(in_cols, d)
    w_p_t = w_in_t[4 * gw + 2 * heads:]
    w_ba_t = jnp.pad(w_in_t[4 * gw:4 * gw + 2 * heads], ((0, HEAD_DIM - 2 * heads), (0, 0)))
    x_bf = _after(token, x).astype(BF16)
    mem_bf = _after(token, mem).astype(BF16)

    tn_d = _pick(d, (1024, 512, 256, 128))
    proj = _plain("proj_main", x_bf, w_in_t, tb=True, n_used=4 * gw, tm=tm_big, tn=_pick(4 * gw, (1024, 512, 256, 128)),
                  tk=tk, out_dtype=F32)
    pproj = _plain("proj_pool", x_bf, w_p_t, tb=True, tm=tm_big, tn=_pick(pw, (1024, 512, 256, 128)), tk=tk, out_dtype=F32)
    ea, dtb = _gate_vectors(w["a_log"], w["dt_bias"], heads)
    vec128 = lambda i, j: (0, 0)
    ba, bg = _matmul(
        "proj_gates", x_bf, w_ba_t, tb=True, tm=tm, tn=HEAD_DIM, tk=tk,
        extra=[(ea, (1, HEAD_DIM), vec128), (dtb, (1, HEAD_DIM), vec128)],
        outs=[(jax.ShapeDtypeStruct((t, HEAD_DIM), F32), (tm, HEAD_DIM), _tile)] * 2,
        epilogue=_gates_epilogue(heads))
    qkv = _gdn_pre(proj, w["conv_w"], heads)
    o_gdn, states = _gdn_core(qkv, bg, heads)
    cat_g = _gdn_post(o_gdn, proj, 3 * gw, w["gdn_norm_w"], heads, tm)
    cat_p = _pool_forward(pproj, 0, w["pool_w"], w["pool_scale"])
    cat = jnp.concatenate([cat_g, cat_p], axis=1)
    w = {**w, **(yield ("weights", 1, cat))}
    h1, h1_bf, xhat1, rstd1 = _ln_forward("mix_ln1", cat, w["w_out"], x, w["ln1_g"], w["ln1_b"], tm=tm_ln, tk=tk)

    q = _plain("xattn_q", h1_bf, w["xq_w"], tm=tm, tn=tn_d, tk=tk, out_dtype=BF16)
    mlen = mem.shape[0]
    tm_mem = _pick(mlen, (256, 128))
    k = _plain("xattn_k", mem_bf, w["xk_w"], tm=tm_mem, tn=tn_d, tk=tk, out_dtype=BF16)
    v = _plain("xattn_v", mem_bf, w["xv_w"], tm=tm_mem, tn=tn_d, tk=tk, out_dtype=BF16)
    att = _attention(q, k, v, tm)
    h2, h2_bf, xhat2, rstd2 = _ln_forward("xo_ln2", att, w["xo_w"], h1, w["ln2_g"], w["ln2_b"], tm=tm_ln, tk=tk)

    w = {**w, **(yield ("weights", 2, h2_bf))}
    s_up = w["w_up3"].shape[0]
    ff = s_up * w["w_up3"].shape[2]
    tn_f = _pick(ff // s_up, (1024, 512, 256, 128))

    def up_epi(acc, ex, out, i):
        r = jnp.maximum(acc, 0.0)
        out[0][...] = (r * r).astype(BF16)
        out[1][...] = (2.0 * r).astype(BF16)

    act, act_grad = _matmul(
        "mlp_up", h2_bf, w["w_up3"], b_blocks=s_up, tm=tm_big, tn=tn_f, tk=tk,
        outs=[(jax.ShapeDtypeStruct((t, ff), BF16), (tm_big, tn_f), _tile)] * 2, epilogue=up_epi)
    w = {**w, **(yield ("weights", 3, act))}
    tk_f = _pick(ff, K_STEPS)
    xhat3, rstd3 = _ln_forward("down_ln3", act, w["w_down"], h2, w["ln3_g"], w["ln3_b"], tm=tm, tk=tk_f, want_h=False)

    grads = {}
    du3, du3_bf, grads["ln3_g"], grads["ln3_b"], loss = _loss_and_ln_backward(
        xhat3, rstd3, w["ln3_g"], w["ln3_b"], target, tm_ln)

    def dup_epi(acc, ex, out, i):
        out[0][...] = (acc * ex[0][...].astype(F32)).astype(BF16)

    dup = _matmul(
        "mlp_down_dx", du3_bf, w["w_down"], tb=True, tm=tm_big, tn=tn_f, tk=tk,
        extra=[(act_grad, (tm_big, tn_f), _tile)],
        outs=[(jax.ShapeDtypeStruct((t, ff), BF16), (tm_big, tn_f), _tile)], epilogue=dup_epi)[0]
    tk_t = _pick(t, K_STEPS)
    tm_w = _pick(d, (512, 256, 128))
    grads["w_down"] = _plain("mlp_down_dw", act, du3_bf, ta=True, tm=_pick(ff, (512, 256, 128)), tn=d, tk=tk_t,
                             out_dtype=F32)
    grads["w_up3"] = _plain("mlp_up_dw", h2_bf, dup, ta=True, tm=tm_w, tn=ff // s_up, tk=tk_t, out_dtype=F32, out3=s_up,
                            n_outer=True)
    token = yield ("grads", 0, {n: grads.pop(n) for n in ("w_down", "w_up3")})
    dh2 = _plain("mlp_up_dx", dup, w["w_up3"], tb=True, b_blocks=s_up, tm=tm_big, tn=tn_d,
                 tk=_pick(ff // s_up, K_STEPS), out_dtype=F32)
    du2, du2_bf, grads["ln2_g"], grads["ln2_b"] = _ln_backward_rows(
        "ln2_bwd", dh2, du3, xhat2, rstd2, _after(token, w["ln2_g"]), tm_ln)
    token = yield ("poll", 0, du2_bf)

    grads["xo_w"] = _plain("xo_dw", att, du2_bf, ta=True, tm=tm_w, tn=d, tk=tk_t, out_dtype=F32)
    datt = _plain("xo_dx", du2_bf, w["xo_w"], tb=True, tm=tm, tn=tn_d, tk=tk, out_dtype=BF16)
    dq, dk, dv = _attention_backward(q, k, v, datt, tm)
    tk_m = _pick(mlen, (256, 128))
    grads["xq_w"] = _plain("xq_dw", h1_bf, dq, ta=True, tm=tm_w, tn=d, tk=tk_t, out_dtype=F32)
    grads["xk_w"] = _plain("xk_dw", mem_bf, dk, ta=True, tm=tm_w, tn=tn_d, tk=tk_m, out_dtype=F32)
    grads["xv_w"] = _plain("xv_dw", mem_bf, dv, ta=True, tm=tm_w, tn=tn_d, tk=tk_m, out_dtype=F32)
    du1, du1_bf, grads["ln1_g"], grads["ln1_b"] = _ln_backward(
        "xq_dx_ln1", dq, w["xq_w"], du2, xhat1, rstd1, _after(token, w["ln1_g"]), tm=tm_ln, tk=tk)

    grads["w_out"] = _plain("out_dw", cat, du1_bf, ta=True, tm=tm_w, tn=d, tk=tk_t, out_dtype=F32)
    token = yield ("grads", 1, {n: grads.pop(n) for n in ("xo_w", "xq_w", "xk_w", "xv_w", "w_out")})
    dcat = _plain("out_dx", du1_bf, w["w_out"], tb=True, tm=tm, tn=tn_d, tk=tk, out_dtype=F32)
    dp, grads["pool_w"], grads["pool_scale"] = _pool_backward(dcat, gw, pproj, 0, w["pool_w"],
                                                              _after(token, w["pool_scale"]))
    do_gdn, dz, grads["gdn_norm_w"] = _gdn_post_backward(dcat, o_gdn, proj, 3 * gw, _after(token, w["gdn_norm_w"]),
                                                         heads, tm)
    dqkv, dbg = _gdn_core_backward(qkv, bg, states, do_gdn, heads)
    token = yield ("poll", 1, dqkv)
    dqkv_pre, grads["conv_w"] = _gdn_pre_backward(proj, _after(token, w["conv_w"]), dqkv, heads)
    dba, dalog_row, ddt_row = _gates_backward(ba, bg, dbg, ea, dtb, heads)
    grads["a_log"] = dalog_row[:, heads:2 * heads]
    grads["dt_bias"] = ddt_row[:, heads:2 * heads]

    dproj = jnp.concatenate([dqkv_pre, dz, dp], axis=1)
    dw_main = _plain("proj_dw", dproj, x_bf, ta=True, tm=_pick(n_main, (512, 256, 128)), tn=d, tk=tk_t, out_dtype=F32)
    dw_ba = _plain("proj_gates_dw", dba, x_bf, ta=True, tm=HEAD_DIM, tn=tn_d, tk=tk_t, out_dtype=F32)
    dw_in_t = jnp.concatenate([dw_main[:4 * gw], dw_ba[:2 * heads], dw_main[4 * gw:]], axis=0)
    grads["w_in_t"] = dw_in_t.reshape(s_in, in_cols // s_in, d)

    def dx_epi(acc, ex, out, i):
        out[0][...] = acc + ex[1][...] + ALPHA * ex[0][...]

    def add_epi(acc, ex, out, i):
        out[0][...] = acc + ex[0][...]

    token = yield ("grads", 2, {n: grads.pop(n) for n in ("w_in_t", "pool_w")})
    dx_gates = _plain("proj_gates_dx", dba, _after(token, w_ba_t), tm=tm, tn=tn_d, tk=HEAD_DIM, out_dtype=F32)
    out_tile = [(jax.ShapeDtypeStruct((t, d), F32), (tm, tn_d), _tile)]
    dx_pool = _matmul("proj_pool_dx", dp, w_p_t, tm=tm, tn=tn_d, tk=_pick(pw, K_STEPS),
                      extra=[(dx_gates, (tm, tn_d), _tile)], outs=out_tile, epilogue=add_epi)[0]
    grad_x = _matmul(
        "proj_dx", dproj, w_in_t, k_used=4 * gw, tm=tm, tn=tn_d, tk=_pick(4 * gw, K_STEPS),
        extra=[(du1, (tm, tn_d), _tile), (dx_pool, (tm, tn_d), _tile)], outs=out_tile, epilogue=dx_epi)[0]
    yield ("poll", 2, grad_x)
    return loss, grad_x, grads


def _adamw(name, w, g, m, v):
    r, c = w.shape
    if r % 8 == 0:
        tr = _pick(r, (256, 128, 64, 32, 16, 8))
        blk, steps = pl.BlockSpec((tr, c), lambda i: (i, 0)), r // tr
    else:
        tc = _pick(c, (256, 128))
        blk, steps = pl.BlockSpec((r, tc), lambda i: (0, i)), c // tc
    c1 = 1.0 - ADAM_B1 ** ADAM_STEP
    c2 = 1.0 - ADAM_B2 ** ADAM_STEP

    def body(w_ref, g_ref, m_ref, v_ref, d_ref, mo_ref, vo_ref):
        gv = g_ref[...]
        mn = ADAM_B1 * m_ref[...] + (1.0 - ADAM_B1) * gv
        vn = ADAM_B2 * v_ref[...] + (1.0 - ADAM_B2) * (gv * gv)
        d_ref[...] = -ADAM_LR * ((mn / c1) / (jnp.sqrt(vn / c2) + ADAM_EPS) + ADAM_WD * w_ref[...])
        mo_ref[...] = mn
        vo_ref[...] = vn

    return pl.pallas_call(
        body, name=name, grid=(steps,), in_specs=[blk] * 4, out_specs=[blk] * 3,
        out_shape=[jax.ShapeDtypeStruct((r, c), F32)] * 3,
        compiler_params=_params("parallel"),
    )(w, g, m, v)


def _place():
    x, y, c = lax.axis_index("x"), lax.axis_index("y"), lax.axis_index("c")
    chips = [(1 - x, y), (x, 1 - y), (1 - x, 1 - y)]
    return x, y, c, chips


HBM = pl.BlockSpec(memory_space=pltpu.HBM)


SEM = pl.BlockSpec(memory_space=pltpu.SEMAPHORE)
ANY = pl.BlockSpec(memory_space=pl.ANY)
EFFECT = pltpu.SideEffectType.DATAFLOW_SIDE_EFFECTING


def _in_hbm(a):
    return pltpu.with_memory_space_constraint(a, pltpu.HBM)


def _remote(src, dst, send_sem, recv_sem, to):
    return pltpu.make_async_remote_copy(src_ref=src, dst_ref=dst, send_sem=send_sem, recv_sem=recv_sem,
                                        device_id=to, device_id_type=MESH)


def _by_rows(rows):
    return rows % 32 == 0


def _half_shape(rows, cols):
    return (rows // 2, cols) if _by_rows(rows) else (rows, cols // 2)


def _half(ref, which, *lead):
    rows, cols = ref.shape[-2:]
    if _by_rows(rows):
        return ref.at[(*lead, pl.ds(which * (rows // 2), rows // 2))]
    return ref.at[(*lead, slice(None), pl.ds(which * (cols // 2), cols // 2))]


def _landed(lands, i, shard_index, which):
    return _half(lands[i], which, shard_index)


def _gather_start(name, shards, after):
    n = len(shards)
    lands = [lax.empty((N_SHARD,) + s.shape, s.dtype) for s in shards]

    def body(*refs):
        ins, zones = refs[:n], refs[n:2 * n]
        ici_send, ici_recv, own_send, own_recv = refs[2 * n + 1:2 * n + 5]
        token = refs[-1]
        x, y, c, chips = _place()
        me = 2 * x + y
        for i in range(n):
            for j, chip in enumerate(chips):
                _remote(_half(ins[i], c), _landed(zones, i, me, c), ici_send.at[3 * i + j],
                        ici_recv.at[3 * i + j], (*chip, c)).start()
        for i in range(n):
            _remote(ins[i], zones[i].at[me], own_send.at[i], own_recv.at[i], (x, y, 1 - c)).start()
        token[...] = jnp.zeros_like(token)

    dma = pltpu.SemaphoreType.DMA
    outs = pl.pallas_call(
        body, name=name,
        in_specs=[HBM] * (2 * n) + [ANY],
        out_shape=(dma((3 * n,)), dma((3 * n,)), dma((n,)), dma((n,)),
                   *[pltpu.HBM(a.shape, a.dtype) for a in shards + lands], jax.ShapeDtypeStruct((8, LANES), F32)),
        out_specs=(SEM, SEM, SEM, SEM, *[HBM] * (2 * n), pl.BlockSpec(memory_space=pltpu.VMEM)),
        input_output_aliases={k: 4 + k for k in range(2 * n)},
        compiler_params=pltpu.CompilerParams(has_side_effects=EFFECT),
    )(*[_in_hbm(a) for a in shards + lands], after)
    sems = dict(zip(("ici_send", "ici_recv", "own_send", "own_recv"), outs[:4]))
    return sems, list(outs[4:4 + n]), list(outs[4 + n:4 + 2 * n]), outs[-1]


def _gather_forward(name, idx, lands, sems, after):
    n = len(idx)

    def body(*refs):
        zones = refs[:n]
        ici_recv = refs[n]
        fwd_send, fwd_recv = refs[n + 2], refs[n + 3]
        x, y, c, chips = _place()
        for k, i in enumerate(idx):
            for j, chip in enumerate(chips):
                half = _landed(zones, k, 2 * chip[0] + chip[1], c)
                _remote(half, half, fwd_send.at[3 * k + j], ici_recv.at[3 * i + j], (*chip, c)).wait_recv()
                _remote(half, half, fwd_send.at[3 * k + j], fwd_recv.at[3 * k + j], (x, y, 1 - c)).start()

    dma = pltpu.SemaphoreType.DMA
    outs = pl.pallas_call(
        body, name=name,
        in_specs=[HBM] * n + [SEM, ANY],
        out_shape=(dma((3 * n,)), dma((3 * n,)), *[pltpu.HBM(a.shape, a.dtype) for a in lands]),
        out_specs=(SEM, SEM, *[HBM] * n),
        input_output_aliases={k: 2 + k for k in range(n)},
        compiler_params=pltpu.CompilerParams(has_side_effects=EFFECT),
    )(*lands, sems["ici_recv"], after)
    return (outs[0], outs[1]), list(outs[2:])


def _gather_wait(name, idx, shards, lands, sems, fwd, after):
    n = len(idx)

    def body(*refs):
        ins, zones = refs[:n], refs[n:2 * n]
        ici_send, own_send, own_recv, fwd_send, fwd_recv = refs[2 * n:2 * n + 5]
        x, y, c, chips = _place()
        me = 2 * x + y
        for k, i in enumerate(idx):
            mine = _half(ins[k], c)
            for j, chip in enumerate(chips):
                theirs = 2 * chip[0] + chip[1]
                _remote(mine, _landed(zones, k, me, c), ici_send.at[3 * i + j], fwd_recv.at[3 * k + j],
                        (*chip, c)).wait_send()
                sent = _landed(zones, k, theirs, c)
                _remote(sent, sent, fwd_send.at[3 * k + j], fwd_recv.at[3 * k + j], (x, y, 1 - c)).wait_send()
                passed = _landed(zones, k, theirs, 1 - c)
                _remote(passed, passed, fwd_send.at[3 * k + j], fwd_recv.at[3 * k + j], (x, y, 1 - c)).wait_recv()
            own = _remote(ins[k], zones[k].at[me], own_send.at[i], own_recv.at[i], (x, y, 1 - c))
            own.wait_send()
            own.wait_recv()

    outs = pl.pallas_call(
        body, name=name,
        in_specs=[HBM] * (2 * n) + [SEM] * 5 + [ANY],
        out_shape=tuple(pltpu.HBM(a.shape, a.dtype) for a in lands),
        out_specs=tuple([HBM] * n),
        input_output_aliases={n + k: k for k in range(n)},
        compiler_params=pltpu.CompilerParams(has_side_effects=EFFECT),
    )(*shards, *lands, sems["ici_send"], sems["own_send"], sems["own_recv"], fwd[0], fwd[1], after)
    return list(outs)


def _all_reduce_small(name, slab, after=None):
    r, width = slab.shape
    ndev = 8

    def body(x_ref, after_ref, out_ref, buf, send_sems, recv_sems):
        x, y, c, _ = _place()
        me = 4 * x + 2 * y + c
        buf[me] = x_ref[...]
        copies = []
        for k in range(1, ndev):
            peer = jnp.bitwise_xor(me, k)
            to = (peer // 4, (peer // 2) % 2, peer % 2)
            cp = pltpu.make_async_remote_copy(src_ref=x_ref, dst_ref=buf.at[me], send_sem=send_sems.at[k - 1],
                                              recv_sem=recv_sems.at[k - 1], device_id=to, device_id_type=MESH)
            cp.start()
            copies.append(cp)
        for k in range(1, ndev):
            peer = jnp.bitwise_xor(me, k)
            pltpu.make_async_remote_copy(src_ref=x_ref, dst_ref=buf.at[peer], send_sem=send_sems.at[k - 1],
                                         recv_sem=recv_sems.at[k - 1], device_id=(x, y, c),
                                         device_id_type=MESH).wait_recv()
        for cp in copies:
            cp.wait_send()
        total = buf[0]
        for d in range(1, ndev):
            total = total + buf[d]
        out_ref[...] = total

    return pl.pallas_call(
        body, name=name,
        in_specs=[pl.BlockSpec(memory_space=pltpu.VMEM), ANY], out_specs=pl.BlockSpec(memory_space=pltpu.VMEM),
        out_shape=jax.ShapeDtypeStruct((r, width), F32),
        scratch_shapes=[pltpu.VMEM((ndev, r, width), F32), pltpu.SemaphoreType.DMA((ndev - 1,)),
                        pltpu.SemaphoreType.DMA((ndev - 1,))],
        compiler_params=pltpu.CompilerParams(vmem_limit_bytes=VMEM_LIMIT),
    )(slab, slab if after is None else after)


def _half_tiling(rows, cols):
    if _by_rows(rows):
        tr = _pick(rows // 2, (256, 128, 64, 32, 16))
        nb = (rows // 2) // tr
        return (tr, cols), nb, (lambda which, b: (which * nb + b, 0)), (lambda b: (b, 0))
    tc = _pick(cols // 2, (256, 128))
    nb = (cols // 2) // tc
    return (rows, tc), nb, (lambda which, b: (0, which * nb + b)), (lambda b: (0, b))


def _chip_partial(name, grad, other, core):
    s, r, cdim = grad.shape
    blk, nb, whole, within = _half_tiling(r, cdim)

    def body(core_ref, g_ref, o_ref, out_ref):
        out_ref[...] = (g_ref[...] + o_ref[...]).astype(BF16)

    return pl.pallas_call(
        body, name=name,
        grid_spec=pltpu.PrefetchScalarGridSpec(
            num_scalar_prefetch=1, grid=(s, nb),
            in_specs=[pl.BlockSpec((None,) + blk, lambda j, b, core_ref: (j,) + whole(core_ref[0], b)),
                      pl.BlockSpec((None,) + blk, lambda j, b, core_ref: (j,) + within(b))],
            out_specs=pl.BlockSpec((None,) + blk, lambda j, b, core_ref: (j,) + within(b))),
        out_shape=jax.ShapeDtypeStruct((s,) + _half_shape(r, cdim), BF16),
        compiler_params=_params("parallel", "parallel"),
    )(core, grad, other)


def _partial_copies(ins, zones, send_sems, recv_sems):
    x, y, c, chips = _place()
    return [_remote(ins[i].at[2 * chip[0] + chip[1]], zones[i].at[j], send_sems.at[3 * i + j],
                    recv_sems.at[3 * i + j], (*chip, c))
            for i in range(len(ins)) for j, chip in enumerate(chips)]


def _swap_copies(ins, zones, send_sems, recv_sems):
    x, y, c, _ = _place()
    copies = []
    for i in range(len(ins)):
        for s in range(N_SHARD):
            copies.append(_remote(_half(ins[i], 1 - c, s), zones[i].at[s],
                                  send_sems.at[N_SHARD * i + s], recv_sems.at[N_SHARD * i + s], (x, y, 1 - c)))
    return copies


def _exchange_start(name, plan, sources, lands, per_array):
    n = len(sources)
    lands = [lax.empty(shape, dtype) for shape, dtype in lands]

    def body(*refs):
        for cp in plan(refs[:n], refs[n:2 * n], refs[2 * n], refs[2 * n + 1]):
            cp.start()
        refs[-1][...] = jnp.zeros_like(refs[-1])

    dma = pltpu.SemaphoreType.DMA
    outs = pl.pallas_call(
        body, name=name,
        in_specs=[HBM] * (2 * n),
        out_shape=(dma((per_array * n,)), dma((per_array * n,)),
                   *[pltpu.HBM(a.shape, a.dtype) for a in list(sources) + lands], jax.ShapeDtypeStruct((8, LANES), F32)),
        out_specs=(SEM, SEM, *[HBM] * (2 * n), pl.BlockSpec(memory_space=pltpu.VMEM)),
        input_output_aliases={k: 2 + k for k in range(2 * n)},
        compiler_params=pltpu.CompilerParams(has_side_effects=EFFECT),
    )(*[_in_hbm(a) for a in list(sources) + lands])
    return (outs[0], outs[1]), list(outs[2:2 + n]), list(outs[2 + n:2 + 2 * n]), outs[-1]


def _exchange_wait(name, plan, started, after):
    sems, partials, lands, _ = started
    n = len(partials)

    def body(*refs):
        for cp in plan(refs[:n], refs[n:2 * n], refs[2 * n], refs[2 * n + 1]):
            cp.wait_send()
            cp.wait_recv()

    outs = pl.pallas_call(
        body, name=name,
        in_specs=[HBM] * (2 * n) + [SEM, SEM] + [ANY] * len(after),
        out_shape=tuple(pltpu.HBM(a.shape, a.dtype) for a in lands),
        out_specs=tuple([HBM] * n),
        input_output_aliases={n + k: k for k in range(n)},
        compiler_params=pltpu.CompilerParams(has_side_effects=EFFECT),
    )(*partials, *lands, sems[0], sems[1], *after)
    return list(outs)


def _reduce_own(name, grad, other, received, where):
    s, r, cdim = grad.shape
    blk, nb, whole, within = _half_tiling(r, cdim)

    def body(where_ref, g_ref, o_ref, r_ref, out_ref):
        total = g_ref[...] + o_ref[...]
        for j in range(3):
            total = total + r_ref[j].astype(F32)
        out_ref[...] = total

    return pl.pallas_call(
        body, name=name,
        grid_spec=pltpu.PrefetchScalarGridSpec(
            num_scalar_prefetch=1, grid=(nb,),
            in_specs=[pl.BlockSpec((None,) + blk, lambda b, w_ref: (w_ref[0],) + whole(w_ref[1], b)),
                      pl.BlockSpec((None,) + blk, lambda b, w_ref: (w_ref[0],) + within(b)),
                      pl.BlockSpec((3,) + blk, lambda b, w_ref: (0,) + within(b))],
            out_specs=pl.BlockSpec(blk, lambda b, w_ref: whole(w_ref[1], b))),
        out_shape=jax.ShapeDtypeStruct((r, cdim), F32),
        compiler_params=_params("parallel"),
    )(where, grad, other, received)


def _join_start(name, halves):
    n = len(halves)

    def body(*refs):
        bufs, send_sems, recv_sems = refs[:n], refs[n], refs[n + 1]
        x, y, c, _ = _place()
        for i in range(n):
            mine = _half(bufs[i], c)
            _remote(mine, mine, send_sems.at[i], recv_sems.at[i], (x, y, 1 - c)).start()
        refs[-1][...] = jnp.zeros_like(refs[-1])

    dma = pltpu.SemaphoreType.DMA
    outs = pl.pallas_call(
        body, name=name,
        in_specs=[HBM] * n,
        out_shape=(dma((n,)), dma((n,)), *[pltpu.HBM(h.shape, F32) for h in halves], jax.ShapeDtypeStruct((8, LANES), F32)),
        out_specs=(SEM, SEM, *[HBM] * n, pl.BlockSpec(memory_space=pltpu.VMEM)),
        input_output_aliases={k: 2 + k for k in range(n)},
        compiler_params=pltpu.CompilerParams(has_side_effects=EFFECT),
    )(*[_in_hbm(h) for h in halves])
    return (outs[0], outs[1]), list(outs[2:2 + n]), outs[-1]


def _join_wait(name, started, after):
    sems, bufs, _ = started
    n = len(bufs)

    def body(*refs):
        bufs, send_sems, recv_sems = refs[:n], refs[n], refs[n + 1]
        x, y, c, _ = _place()
        for i in range(n):
            mine, theirs = _half(bufs[i], c), _half(bufs[i], 1 - c)
            _remote(mine, mine, send_sems.at[i], recv_sems.at[i], (x, y, 1 - c)).wait_send()
            _remote(theirs, theirs, send_sems.at[i], recv_sems.at[i], (x, y, 1 - c)).wait_recv()

    outs = pl.pallas_call(
        body, name=name,
        in_specs=[HBM] * n + [SEM, SEM] + [ANY] * len(after),
        out_shape=tuple(pltpu.HBM(b.shape, F32) for b in bufs),
        out_specs=tuple([HBM] * n),
        input_output_aliases={k: k for k in range(n)},
        compiler_params=pltpu.CompilerParams(has_side_effects=EFFECT),
    )(*bufs, sems[0], sems[1], *after)
    return list(outs)


BIG = ("w_in", "pool_w", "w_out", "xq_w", "xk_w", "xv_w", "xo_w", "w_up", "w_down")
GATHER_GROUPS = ((0, 1), (2, 3, 4, 5, 6), (7,), (8,))
SMALL = ("conv_w", "a_log", "dt_bias", "gdn_norm_w", "pool_scale", "ln1_g", "ln1_b", "ln2_g", "ln2_b", "ln3_g", "ln3_b")
ORDER = ("w_in", "conv_w", "a_log", "dt_bias", "gdn_norm_w", "pool_w", "pool_scale", "w_out", "ln1_g", "ln1_b",
         "xq_w", "xk_w", "xv_w", "xo_w", "ln2_g", "ln2_b", "w_up", "w_down", "ln3_g", "ln3_b")
LANES = 128


def _rows(flat_len):
    return -(-flat_len // LANES)


def _pack(pieces):
    out = []
    for p in pieces:
        flat = p.reshape(-1).astype(F32)
        out.append(jnp.pad(flat, (0, _rows(flat.shape[0]) * LANES - flat.shape[0])).reshape(-1, LANES))
    slab = jnp.concatenate(out, axis=0)
    return jnp.pad(slab, ((0, -slab.shape[0] % 8), (0, 0)))


def _unpack(slab, shapes):
    out, row = [], 0
    for shp in shapes:
        size = math.prod(shp)
        out.append(slab[row:row + _rows(size)].reshape(-1)[:size].reshape(shp))
        row += _rows(size)
    return out


TRANSPOSED = ("w_in",)


def _as2d(name, a):
    a = a[0]
    if name in TRANSPOSED:
        return jnp.swapaxes(a, 0, 1)
    return a.reshape(-1, a.shape[-1]) if a.ndim == 3 else a


def _from2d(name, a, shape):
    return (jnp.swapaxes(a, 0, 1) if name in TRANSPOSED else a).reshape(shape)


def kernel(x, mem, w_in, conv_w, a_log, dt_bias, gdn_norm_w, pool_w, pool_scale, w_out, ln1_g, ln1_b, xq_w, xk_w, xv_w, xo_w, ln2_g, ln2_b, w_up, w_down, ln3_g, ln3_b, loss_target, m_w_in, m_conv_w, m_a_log, m_dt_bias, m_gdn_norm_w, m_pool_w, m_pool_scale, m_w_out, m_ln1_g, m_ln1_b, m_xq_w, m_xk_w, m_xv_w, m_xo_w, m_ln2_g, m_ln2_b, m_w_up, m_w_down, m_ln3_g, m_ln3_b, v_w_in, v_conv_w, v_a_log, v_dt_bias, v_gdn_norm_w, v_pool_w, v_pool_scale, v_w_out, v_ln1_g, v_ln1_b, v_xq_w, v_xk_w, v_xv_w, v_xo_w, v_ln2_g, v_ln2_b, v_w_up, v_w_down, v_ln3_g, v_ln3_b):
    given = dict(locals())
    cx, cy, cc = lax.axis_index("x"), lax.axis_index("y"), lax.axis_index("c")
    me = 2 * cx + cy
    groups = pool_w.shape[1]
    cs = pool_w.shape[2]
    kk, conv_cols = conv_w.shape[1], conv_w.shape[2]
    core = cc.astype(jnp.int32).reshape(1)
    where = jnp.stack([me, cc]).astype(jnp.int32)

    conv_slab = jnp.zeros((kk, N_SHARD * conv_cols), F32)
    conv_slab = lax.dynamic_update_slice(conv_slab, conv_w[0] * (cc == 0).astype(F32), (0, me * conv_cols))
    wts = {"conv_w": _unpack(_all_reduce_small("gather_conv_w", _pack([conv_slab])), [conv_slab.shape])[0]}

    started = {}

    def start(name, idx, after, token=None):
        casts = [_after(token, _as2d(BIG[i], given[BIG[i]])).astype(BF16) for i in idx]
        sems, shards, lands, token = _gather_start(name, casts, after)
        for k, i in enumerate(idx):
            started[i] = (sems, k, shards[k], lands[k])
        return token

    token = start("gather_start_first", GATHER_GROUPS[0], wts["conv_w"])
    token = start("gather_start_rest", tuple(i for group in GATHER_GROUPS[1:] for i in group), token, token)

    def fetch(group, after):
        members = [started[i] for i in GATHER_GROUPS[group]]
        sems, idx = members[0][0], [m[1] for m in members]
        fwd, zones = _gather_forward(f"gather_forward_{group}", idx, [m[3] for m in members], sems, after)
        full = dict(zip([BIG[i] for i in GATHER_GROUPS[group]],
                        _gather_wait(f"gather_wait_{group}", idx, [m[2] for m in members], zones, sems, fwd, after)))
        out = {}
        for n, a in full.items():
            if n == "w_in":
                out["w_in_t"] = a
            elif n == "w_up":
                out["w_up3"] = a
            elif n == "pool_w":
                out[n] = a.reshape(N_SHARD, groups, cs, -1).transpose(1, 0, 2, 3).reshape(groups, N_SHARD * cs, -1)
            else:
                out[n] = a.reshape(-1, a.shape[-1])
        return out

    for n in ("a_log", "dt_bias", "gdn_norm_w", "pool_scale", "ln1_g", "ln1_b", "ln2_g", "ln2_b", "ln3_g", "ln3_b"):
        wts[n] = given[n]
    wts.update(fetch(0, token))

    def start_swap(group, grads):
        names, blocks = [], []
        for n, g in grads.items():
            if n == "pool_w":
                g = g.reshape(groups, N_SHARD, cs, -1).transpose(1, 0, 2, 3).reshape(N_SHARD, groups * cs, -1)
            elif g.ndim == 2:
                g = g.reshape(N_SHARD, -1, g.shape[-1])
            names.append({"w_in_t": "w_in", "w_up3": "w_up"}.get(n, n))
            blocks.append(g)
        zones = [((N_SHARD,) + _half_shape(b.shape[1], b.shape[2]), F32) for b in blocks]
        swap = _exchange_start(f"grad_swap_start_{group}", _swap_copies, blocks, zones, N_SHARD)
        return {"group": group, "names": names, "swap": swap, "token": swap[3]}

    def start_send(state, after):
        group, names = state["group"], state["names"]
        state["blocks"] = state["swap"][1]
        state["others"] = _exchange_wait(f"grad_swap_wait_{group}", _swap_copies, state["swap"], after)
        partials = [_chip_partial("chip_partial_" + n, gb, ob, core)
                    for n, gb, ob in zip(names, state["blocks"], state["others"])]
        zones = [((3,) + p.shape[1:], BF16) for p in partials]
        state["send"] = _exchange_start(f"grad_send_start_{group}", _partial_copies, partials, zones, 3)
        state["token"] = state["send"][3]

    grad, delta, new_m, new_v = {}, {}, {}, {}

    def start_join(state, after):
        group, names = state["group"], state["names"]
        received = _exchange_wait(f"grad_send_wait_{group}", _partial_copies, state["send"], after)
        halves = [_reduce_own("reduce_own_" + n, gb, ob, rb, where)
                  for n, gb, ob, rb in zip(names, state["blocks"], state["others"], received)]
        state["join"] = _join_start(f"grad_join_start_{group}", halves)
        return state["join"][2]

    def finish_reduce(state, after):
        group, names = state["group"], state["names"]
        for n, g in zip(names, _join_wait(f"grad_join_wait_{group}", state["join"], after)):
            shp = given[n].shape
            d2, m2, v2 = _adamw("adamw_" + n, _as2d(n, given[n]), g, _as2d(n, given["m_" + n]), _as2d(n, given["v_" + n]))
            grad[n], delta[n], new_m[n], new_v[n] = (_from2d(n, a, shp) for a in (g, d2, m2, v2))
        return d2

    step = _local_step(x[0], mem[0], loss_target[0], wts, token)
    pending = {}
    request = next(step)
    while True:
        try:
            kind, group, payload = request
            if kind == "weights":
                request = step.send(fetch(group, payload))
            elif kind == "grads":
                pending[group] = start_swap(group, payload)
                request = step.send(pending[group]["token"])
            else:
                start_send(pending[group], [payload])
                request = step.send(pending[group]["token"])
        except StopIteration as stop:
            loss_row, grad_x, g = stop.value
            break

    after = [pending[2]["token"], grad_x]
    for group in (0, 1):
        after = [start_join(pending[group], after)]
    for group in (0, 1):
        after = [finish_reduce(pending[group], after)]
    after = [finish_reduce(pending[2], [start_join(pending[2], after)])]

    small_names = ("a_log", "dt_bias", "gdn_norm_w", "pool_scale", "ln1_g", "ln1_b", "ln2_g", "ln2_b", "ln3_g", "ln3_b")
    pieces = [g["conv_w"]] + [g[n] for n in small_names] + [loss_row[:, :1]]
    shapes = [p.shape for p in pieces]
    summed = _unpack(_all_reduce_small("all_reduce_small", _pack(pieces), after[0]), shapes)
    gsmall = dict(zip(small_names, summed[1:-1]))
    gsmall["conv_w"] = lax.dynamic_slice(summed[0], (0, me * conv_cols), (kk, conv_cols))
    loss = summed[-1][0, 0]

    sshapes = [given[n].shape for n in SMALL]
    slabs = [_pack([given[p + n] for n in SMALL]) for p in ("", "m_", "v_")]
    gslab = _pack([gsmall[n] for n in SMALL])
    outs = _adamw("adamw_small", slabs[0], gslab, slabs[1], slabs[2])
    for dst, slab in zip((delta, new_m, new_v), outs):
        dst.update(zip(SMALL, _unpack(slab, sshapes)))
    for n in SMALL:
        grad[n] = gsmall[n].reshape(given[n].shape)

    return (loss, grad_x[None], *[grad[n] for n in ORDER], *[delta[n] for n in ORDER],
            *[new_m[n] for n in ORDER], *[new_v[n] for n in ORDER])
```

```python
import functools
import math

import jax
import jax.numpy as jnp
from jax import lax
from jax.experimental import pallas as pl
from jax.experimental.pallas import tpu as pltpu

F32 = jnp.float32
BF16 = jnp.bfloat16
MESH = pl.DeviceIdType.MESH

HEAD_DIM = 128
CHUNK = 64
POOL_WINDOWS = (2, 4, 8, 16)
XATTN_HEADS = 4
ALPHA = 2.0 ** 0.25
LN_EPS = 1e-5
NORM_EPS = 1e-6
ADAM_LR, ADAM_B1, ADAM_B2, ADAM_EPS, ADAM_WD, ADAM_STEP = 0.001, 0.9, 0.999, 1e-08, 0.01, 10
N_SHARD = 4
VMEM_LIMIT = 56 * 1024 * 1024
K_STEPS = (2048, 1024, 512, 256, 128)


def _params(*sem):
    return pltpu.CompilerParams(dimension_semantics=sem, vmem_limit_bytes=VMEM_LIMIT)


def _bdot(a, b, ta=False, tb=False):
    dims = (((0 if ta else 1,), (1 if tb else 0,)), ((), ()))
    return lax.dot_general(a.astype(BF16), b.astype(BF16), dims, preferred_element_type=F32)


def _sigmoid(x):
    return 1.0 / (1.0 + jnp.exp(-x))


def _matmul(name, a, b, *, ta=False, tb=False, tm, tn, tk, extra=(), outs, epilogue, b_blocks=None,
            sequential=False, n_used=None, k_used=None, n_outer=False):
    m, k_dim = (a.shape[1], a.shape[0]) if ta else a.shape
    if b_blocks and tb:
        n = b.shape[1]
        k_dim = b.shape[0] * b.shape[2]
        per = b.shape[2] // tk
        b_spec = pl.BlockSpec((None, tn, tk), lambda i, j, k: (k // per, j, k % per))
    elif b_blocks:
        n = b.shape[0] * b.shape[2]
        per = b.shape[2] // tn
        b_spec = pl.BlockSpec((None, tk, tn), lambda i, j, k: (j // per, k, j % per))
    elif tb:
        n = b.shape[0]
        b_spec = pl.BlockSpec((tn, tk), lambda i, j, k: (j, k))
    else:
        n = b.shape[1]
        b_spec = pl.BlockSpec((tk, tn), lambda i, j, k: (k, j))
    n, k_dim = n_used or n, k_used or k_dim
    assert m % tm == 0 and n % tn == 0 and k_dim % tk == 0, (name, m, n, k_dim, tm, tn, tk)
    nk = k_dim // tk
    a_spec = pl.BlockSpec((tk, tm), lambda i, j, k: (k, i)) if ta else pl.BlockSpec((tm, tk), lambda i, j, k: (i, k))
    n_extra, n_out = len(extra), len(outs)

    def wrap(index_map):
        return lambda i, j, k: index_map(i, j)

    def spec(block, index_map):
        if n_outer:
            return pl.BlockSpec(block, lambda j, i, k: index_map(i, j, k))
        return pl.BlockSpec(block, index_map)

    row_axis = 1 if n_outer else 0

    def body_one_step(*refs):
        ex = refs[2:2 + n_extra]
        out = refs[2 + n_extra:2 + n_extra + n_out]
        epilogue(_bdot(refs[0][...], refs[1][...], ta, tb), ex, out, pl.program_id(row_axis))

    def body(*refs):
        a_ref, b_ref = refs[0], refs[1]
        ex = refs[2:2 + n_extra]
        out = refs[2 + n_extra:2 + n_extra + n_out]
        acc = refs[-1]
        i, k = pl.program_id(row_axis), pl.program_id(2)
        part = _bdot(a_ref[...], b_ref[...], ta, tb)

        @pl.when(k == 0)
        def _():
            acc[...] = part

        @pl.when(jnp.logical_and(k > 0, k < nk - 1))
        def _():
            acc[...] += part

        @pl.when(k == nk - 1)
        def _():
            epilogue(acc[...] + part, ex, out, i)

    sem = ("arbitrary",) * 3 if sequential else ("parallel", "parallel", "arbitrary")
    res = pl.pallas_call(
        body_one_step if nk == 1 else body, name=name,
        grid=(n // tn, m // tm, nk) if n_outer else (m // tm, n // tn, nk),
        in_specs=[spec(a_spec.block_shape, a_spec.index_map), spec(b_spec.block_shape, b_spec.index_map)]
        + [spec(bs, wrap(im)) for _, bs, im in extra],
        out_specs=[spec(bs, wrap(im)) for _, bs, im in outs],
        out_shape=[s for s, _, _ in outs],
        scratch_shapes=[] if nk == 1 else [pltpu.VMEM((tm, tn), F32)],
        compiler_params=_params(*sem),
    )(a, b, *[x for x, _, _ in extra])
    return res


def _tile(i, j):
    return (i, j)


def _plain(name, a, b, *, ta=False, tb=False, tm, tn, tk, out_dtype, b_blocks=None, out3=None, n_used=None,
           n_outer=False):
    m = a.shape[1] if ta else a.shape[0]
    if b_blocks:
        n = b.shape[1] if tb else b.shape[0] * b.shape[2]
    else:
        n = n_used or (b.shape[0] if tb else b.shape[1])

    def epi(acc, ex, out, i):
        out[0][...] = acc.astype(out_dtype)

    if out3:
        per = (n // out3) // tn
        spec = (jax.ShapeDtypeStruct((out3, m, n // out3), out_dtype), (None, tm, tn),
                lambda i, j: (j // per, i, j % per))
    else:
        spec = (jax.ShapeDtypeStruct((m, n), out_dtype), (tm, tn), _tile)
    return _matmul(name, a, b, ta=ta, tb=tb, tm=tm, tn=tn, tk=tk, outs=[spec], epilogue=epi,
                   b_blocks=b_blocks, n_used=n_used, n_outer=n_outer)[0]


def _shard_matmul(name, a, w3, order, *, tm, extra=(), outs, epilogue):
    _, ks, n = w3.shape
    m = a.shape[0]
    nk = order.shape[0]
    n_extra, n_out = len(extra), len(outs)

    def body(order_ref, a_ref, w_ref, *rest):
        ex, out = rest[:n_extra], rest[n_extra:n_extra + n_out]
        part = _bdot(a_ref[...], w_ref[...])
        if nk == 1:
            epilogue(part, ex, out)
            return
        acc = rest[-1]
        k = pl.program_id(1)

        @pl.when(k == 0)
        def _():
            acc[...] = part

        @pl.when(jnp.logical_and(k > 0, k < nk - 1))
        def _():
            acc[...] += part

        @pl.when(k == nk - 1)
        def _():
            epilogue(acc[...] + part, ex, out)

    def rows(block, index_map):
        return pl.BlockSpec(block, lambda i, k, o: index_map(i))

    return pl.pallas_call(
        body, name=name,
        grid_spec=pltpu.PrefetchScalarGridSpec(
            num_scalar_prefetch=1, grid=(m // tm, nk),
            in_specs=[pl.BlockSpec((tm, ks), lambda i, k, o: (i, o[k])),
                      pl.BlockSpec((None, ks, n), lambda i, k, o: (o[k], 0, 0))]
            + [rows(bs, im) for _, bs, im in extra],
            out_specs=[rows(bs, im) for _, bs, im in outs],
            scratch_shapes=[] if nk == 1 else [pltpu.VMEM((tm, n), F32)]),
        out_shape=[s for s, _, _ in outs],
        compiler_params=_params("parallel", "arbitrary"),
    )(order, a, w3, *[x for x, _, _ in extra])


def _ln_forward(name, a, b, res, gamma, beta, *, tm, tk, want_h=True):
    m, n = res.shape

    def epi(acc, ex, out, i):
        u = ALPHA * ex[0][...] + acc
        mu = jnp.mean(u, axis=-1, keepdims=True)
        xc = u - mu
        var = jnp.mean(xc * xc, axis=-1, keepdims=True)
        rstd = lax.rsqrt(var + LN_EPS)
        xhat = xc * rstd
        out[-2][...] = xhat
        out[-1][...] = rstd
        if want_h:
            h = xhat * ex[1][...] + ex[2][...]
            out[0][...] = h
            out[1][...] = h.astype(BF16)

    row = lambda i, j: (i, 0)
    vec = lambda i, j: (0, 0)
    outs = [(jax.ShapeDtypeStruct((m, n), F32), (tm, n), row), (jax.ShapeDtypeStruct((m, n), BF16), (tm, n), row),
            (jax.ShapeDtypeStruct((m, n), F32), (tm, n), row), (jax.ShapeDtypeStruct((m, 1), F32), (tm, 1), row)]
    return _matmul(
        name, a, b, tm=tm, tn=n, tk=tk,
        extra=[(res, (tm, n), row), (gamma, (1, n), vec), (beta, (1, n), vec)],
        outs=outs if want_h else outs[2:], epilogue=epi)


def _ln_backward_math(dy, xhat, rstd, gamma):
    dxhat = dy * gamma
    m1 = jnp.mean(dxhat, axis=-1, keepdims=True)
    m2 = jnp.mean(dxhat * xhat, axis=-1, keepdims=True)
    du = rstd * (dxhat - m1 - xhat * m2)
    return du, jnp.sum(dy * xhat, axis=0, keepdims=True), jnp.sum(dy, axis=0, keepdims=True)


def _ln_backward(name, a, b, dres, xhat, rstd, gamma, *, tm, tk, b_blocks=None, tb=True):
    m, n = dres.shape

    def epi(acc, ex, out, i):
        dy = acc + ALPHA * ex[0][...]
        du, dg, db = _ln_backward_math(dy, ex[1][...], ex[2][...], ex[3][...])
        out[0][...] = du
        out[1][...] = du.astype(BF16)
        first = i == 0

        @pl.when(first)
        def _():
            out[2][...] = dg
            out[3][...] = db

        @pl.when(jnp.logical_not(first))
        def _():
            out[2][...] += dg
            out[3][...] += db

    row = lambda i, j: (i, 0)
    vec = lambda i, j: (0, 0)
    return _matmul(
        name, a, b, tb=tb, tm=tm, tn=n, tk=tk, b_blocks=b_blocks, sequential=True,
        extra=[(dres, (tm, n), row), (xhat, (tm, n), row), (rstd, (tm, 1), row), (gamma, (1, n), vec)],
        outs=[(jax.ShapeDtypeStruct((m, n), F32), (tm, n), row),
              (jax.ShapeDtypeStruct((m, n), BF16), (tm, n), row),
              (jax.ShapeDtypeStruct((1, n), F32), (1, n), vec),
              (jax.ShapeDtypeStruct((1, n), F32), (1, n), vec)],
        epilogue=epi)


def _shift_down(x, k):
    row = lax.broadcasted_iota(jnp.int32, x.shape, 0)
    return jnp.where(row >= k, pltpu.roll(x, k, axis=0), 0.0)


def _shift_up(x, k):
    t = x.shape[0]
    row = lax.broadcasted_iota(jnp.int32, x.shape, 0)
    return jnp.where(row < t - k, pltpu.roll(x, t - k, axis=0), 0.0)


def _conv_silu_norm(x, w, normalise):
    kk = w.shape[0]
    c = x * w[kk - 1:kk, :]
    for j in range(kk - 1):
        c = c + _shift_down(x, kk - 1 - j) * w[j:j + 1, :]
    sg = _sigmoid(c)
    s = c * sg
    r = lax.rsqrt(jnp.sum(s * s, axis=-1, keepdims=True) + NORM_EPS)
    y = jnp.where(normalise, s * r, s)
    return c, sg, s, r, y


def _gdn_pre(proj, conv_w, heads):
    t = proj.shape[0]
    kk = conv_w.shape[0]

    def body(x_ref, w_ref, o_ref):
        normalise = pl.program_id(0) < 2
        o_ref[...] = _conv_silu_norm(x_ref[...], w_ref[...], normalise)[4]

    col = lambda s, h: (0, s * heads + h)
    return pl.pallas_call(
        body, name="gdn_pre", grid=(3, heads),
        in_specs=[pl.BlockSpec((t, HEAD_DIM), col), pl.BlockSpec((kk, HEAD_DIM), col)],
        out_specs=pl.BlockSpec((t, HEAD_DIM), col),
        out_shape=jax.ShapeDtypeStruct((t, 3 * heads * HEAD_DIM), F32),
        compiler_params=_params("parallel", "parallel"),
    )(proj, conv_w)


def _gdn_pre_backward(proj, conv_w, dqkv, heads):
    t = proj.shape[0]
    kk = conv_w.shape[0]

    def body(x_ref, w_ref, dy_ref, dx_ref, dw_ref):
        normalise = pl.program_id(0) < 2
        x = x_ref[...]
        w = w_ref[...]
        dy = dy_ref[...]
        c, sg, s, r, y = _conv_silu_norm(x, w, normalise)
        ds_norm = r * (dy - y * jnp.sum(dy * y, axis=-1, keepdims=True))
        ds = jnp.where(normalise, ds_norm, dy)
        dc = ds * (sg * (1.0 + c * (1.0 - sg)))
        dx = dc * w[kk - 1:kk, :]
        rows = [None] * kk
        rows[kk - 1] = jnp.sum(dc * x, axis=0, keepdims=True)
        for j in range(kk - 1):
            lag = kk - 1 - j
            dx = dx + _shift_up(dc, lag) * w[j:j + 1, :]
            rows[j] = jnp.sum(dc * _shift_down(x, lag), axis=0, keepdims=True)
        dx_ref[...] = dx.astype(BF16)
        dw_ref[...] = jnp.concatenate(rows, axis=0)

    col = lambda s, h: (0, s * heads + h)
    return pl.pallas_call(
        body, name="gdn_pre_bwd", grid=(3, heads),
        in_specs=[pl.BlockSpec((t, HEAD_DIM), col), pl.BlockSpec((kk, HEAD_DIM), col),
                  pl.BlockSpec((t, HEAD_DIM), col)],
        out_specs=[pl.BlockSpec((t, HEAD_DIM), col), pl.BlockSpec((kk, HEAD_DIM), col)],
        out_shape=[jax.ShapeDtypeStruct((t, 3 * heads * HEAD_DIM), BF16),
                   jax.ShapeDtypeStruct((kk, 3 * heads * HEAD_DIM), F32)],
        compiler_params=_params("parallel", "parallel"),
    )(proj, conv_w, dqkv)


def _gate_vectors(a_log, dt_bias, heads):
    pad = lambda v: jnp.pad(v.astype(F32), ((0, 0), (heads, HEAD_DIM - 2 * heads)))
    return pad(jnp.exp(a_log.astype(F32))), pad(dt_bias)


def _softplus(x):
    return jnp.maximum(x, 0.0) + jnp.log(1.0 + jnp.exp(-jnp.abs(x)))


def _gates_epilogue(heads):
    def epi(acc, ex, out, i):
        lane = lax.broadcasted_iota(jnp.int32, acc.shape, 1)
        beta = _sigmoid(acc)
        g = -ex[0][...] * _softplus(acc + ex[1][...])
        out[0][...] = acc
        out[1][...] = jnp.where(lane < heads, beta, jnp.where(lane < 2 * heads, g, 0.0))
    return epi


def _gates_backward(ba, bg, dbg, ea, dtb, heads):
    t = ba.shape[0]

    def body(ba_ref, bg_ref, d_ref, ea_ref, dt_ref, dba_ref, dal_ref, ddt_ref):
        lane = lax.broadcasted_iota(jnp.int32, (t, HEAD_DIM), 1)
        bgv = bg_ref[...]
        d = d_ref[...]
        db = d * bgv * (1.0 - bgv)
        da = -d * ea_ref[...] * _sigmoid(ba_ref[...] + dt_ref[...])
        is_g = jnp.logical_and(lane >= heads, lane < 2 * heads)
        dba = jnp.where(lane < heads, db, jnp.where(is_g, da, 0.0))
        dba_ref[...] = dba.astype(BF16)
        dal_ref[...] = jnp.sum(jnp.where(is_g, d * bgv, 0.0), axis=0, keepdims=True)
        ddt_ref[...] = jnp.sum(jnp.where(is_g, da, 0.0), axis=0, keepdims=True)

    full = pl.BlockSpec((t, HEAD_DIM), lambda: (0, 0))
    vec = pl.BlockSpec((1, HEAD_DIM), lambda: (0, 0))
    return pl.pallas_call(
        body, name="gates_bwd", grid=(),
        in_specs=[full, full, full, vec, vec], out_specs=[full, vec, vec],
        out_shape=[jax.ShapeDtypeStruct((t, HEAD_DIM), BF16), jax.ShapeDtypeStruct((1, HEAD_DIM), F32),
                   jax.ShapeDtypeStruct((1, HEAD_DIM), F32)],
        compiler_params=pltpu.CompilerParams(vmem_limit_bytes=VMEM_LIMIT),
    )(ba, bg, dbg, ea, dtb)


class _Chunk:
    pass


def _split2(x):
    hi = x.astype(BF16)
    return hi, (x - hi.astype(F32)).astype(BF16)


def _split3(x):
    hi = x.astype(BF16)
    rest = x - hi.astype(F32)
    mid = rest.astype(BF16)
    return hi, mid, (rest - mid.astype(F32)).astype(BF16)


def _dot_mask(mask, x, ta=False):
    hi, mid, lo = _split3(x)
    return _bdot(mask, hi, ta=ta) + (_bdot(mask, mid, ta=ta) + _bdot(mask, lo, ta=ta))


def _transpose_by_identity(x):
    r = x.shape[0]
    eye = (lax.broadcasted_iota(jnp.int32, (r, r), 0) == lax.broadcasted_iota(jnp.int32, (r, r), 1)).astype(BF16)
    hi, mid, lo = _split3(x)
    return _bdot(hi, eye, ta=True) + (_bdot(mid, eye, ta=True) + _bdot(lo, eye, ta=True))


def _dot22(a, b, ta=False, tb=False):
    ah, al = _split2(a)
    bh, bl = _split2(b)
    return _bdot(ah, bh, ta, tb) + (_bdot(ah, bl, ta, tb) + _bdot(al, bh, ta, tb))


def _chunk_gates(bg, heads):
    n = CHUNK
    row = lax.broadcasted_iota(jnp.int32, (n, n), 0)
    col = lax.broadcasted_iota(jnp.int32, (n, n), 1)
    lane = lax.broadcasted_iota(jnp.int32, bg.shape, 1)
    graw = jnp.where(jnp.logical_and(lane >= heads, lane < 2 * heads), bg, 0.0)
    gc = _dot_mask((row >= col).astype(BF16), graw)
    return gc, _transpose_by_identity(gc)


def _in_lockstep(generators):
    results = [None] * len(generators)
    live = list(enumerate(generators))
    while live:
        still = []
        for i, gen in live:
            try:
                next(gen)
                still.append((i, gen))
            except StopIteration as stop:
                results[i] = stop.value
        live = still
    return results


def _chunk_local(q, k, v, beta, gc, grow):
    c = _Chunk()
    n = CHUNK
    row = lax.broadcasted_iota(jnp.int32, (n, n), 0)
    col = lax.broadcasted_iota(jnp.int32, (n, n), 1)
    c.tri = row >= col
    c.strict = row > col
    eye = row == col
    c.gcb = jnp.broadcast_to(gc, (n, HEAD_DIM))
    c.decay = jnp.where(c.tri, jnp.exp(jnp.where(c.tri, gc - grow, 0.0)), 0.0)
    c.eg = jnp.exp(c.gcb)
    glast = c.gcb[n - 1:n, :]
    c.egl = jnp.exp(glast)
    c.ekl = jnp.exp(glast - c.gcb)
    c.beta = beta
    c.q = q * (HEAD_DIM ** -0.5)
    c.k = k
    c.v = v
    c.kb = k * beta
    c.vb = v * beta
    c.kg = c.kb * c.eg
    both = _bdot(jnp.concatenate([c.kb, c.q], axis=0), k, tb=True)
    yield
    c.L = jnp.where(c.strict, both[:n] * c.decay, 0.0)
    c.A = jnp.where(c.tri, both[n:] * c.decay, 0.0)
    x = -c.L
    tinv = eye.astype(F32) + x
    p = _dot22(x, x)
    yield
    for _ in range(int(math.log2(n)) - 2):
        both = _dot22(jnp.concatenate([p, tinv], axis=0), p)
        yield
        p, tinv = both[:n], tinv + both[n:]
    c.T = tinv + _dot22(tinv, p)
    yield
    tinv = c.T
    uw = _dot22(tinv, jnp.concatenate([c.vb, c.kg], axis=1))
    yield
    c.u, c.w = uw[:, :HEAD_DIM], uw[:, HEAD_DIM:]
    c.qg = c.q * c.eg
    c.kdec = k * c.ekl
    return c


def _gdn_core(qkv, bg, heads):
    t = qkv.shape[0]
    nchunk = t // CHUNK

    gw = heads * HEAD_DIM

    def body(qkv_ref, bg_ref, o_ref, s_ref, state):
        @pl.when(pl.program_id(0) == 0)
        def _():
            state[...] = jnp.zeros_like(state)

        bg_v = bg_ref[...]
        gc_all, gc_rows = _chunk_gates(bg_v, heads)
        def one_head(h):
            col = lambda s: pl.ds(s * gw + h * HEAD_DIM, HEAD_DIM)
            c = yield from _chunk_local(qkv_ref[:, col(0)], qkv_ref[:, col(1)], qkv_ref[:, col(2)], bg_v[:, h:h + 1],
                                        gc_all[:, heads + h:heads + h + 1], gc_rows[heads + h:heads + h + 1, :])
            s0 = state[h]
            v_new = c.u - _bdot(c.w, s0)
            yield
            o = _bdot(c.qg, s0) + _bdot(c.A, v_new)
            return s0, o, s0 * c.egl + _bdot(c.kdec, v_new, ta=True)

        results = _in_lockstep([one_head(h) for h in range(heads)])
        for h, (s0, o, s1) in enumerate(results):
            s_ref[h, 0] = s0
            o_ref[:, pl.ds(h * HEAD_DIM, HEAD_DIM)] = o
            state[h] = s1

    return pl.pallas_call(
        body, name="gdn_core", grid=(nchunk,),
        in_specs=[pl.BlockSpec((CHUNK, 3 * gw), lambda n: (n, 0)), pl.BlockSpec((CHUNK, HEAD_DIM), lambda n: (n, 0))],
        out_specs=[pl.BlockSpec((CHUNK, gw), lambda n: (n, 0)),
                   pl.BlockSpec((heads, 1, HEAD_DIM, HEAD_DIM), lambda n: (0, n, 0, 0))],
        out_shape=[jax.ShapeDtypeStruct((t, gw), F32),
                   jax.ShapeDtypeStruct((heads, nchunk, HEAD_DIM, HEAD_DIM), F32)],
        scratch_shapes=[pltpu.VMEM((heads, HEAD_DIM, HEAD_DIM), F32)],
        compiler_params=_params("arbitrary"),
    )(qkv, bg)


def _gdn_core_backward(qkv, bg, states, do, heads):
    t = qkv.shape[0]
    nchunk = t // CHUNK
    n = CHUNK

    def one_head(chunk_local, s0, d_out, ds1):
        c = yield from chunk_local
        v_new = c.u - _bdot(c.w, s0)
        dqg = _bdot(d_out, s0, tb=True)
        ds0 = _bdot(c.qg, d_out, ta=True) + ds1 * c.egl
        dv_new = _bdot(c.A, d_out, ta=True) + _bdot(c.kdec, ds1)
        yield
        dA = jnp.where(c.tri, _bdot(d_out, v_new, tb=True), 0.0)
        dkdec = _bdot(v_new, ds1, tb=True)
        dgl = jnp.sum(jnp.sum(ds1 * s0, axis=1, keepdims=True), axis=0, keepdims=True) * c.egl
        dw = -_bdot(dv_new, s0, tb=True)
        ds0 = ds0 - _bdot(c.w, dv_new, ta=True)
        yield
        both = _dot22(c.T, jnp.concatenate([dv_new, dw], axis=1), ta=True)
        yield
        dvb, dkg = both[:, :HEAD_DIM], both[:, HEAD_DIM:]
        dL = jnp.where(c.strict, -(_bdot(dvb, c.u, tb=True) + _bdot(dkg, c.w, tb=True)), 0.0)
        yield
        dm1 = dL * c.decay
        dkb = _bdot(dm1, c.k) + dkg * c.eg
        dk = _bdot(dm1, c.kb, ta=True)
        dm2 = dA * c.decay
        dq = _bdot(dm2, c.k) + dqg * c.eg
        dk = dk + _bdot(dm2, c.q, ta=True) + dkdec * c.ekl + dkb * c.beta
        pm = dL * c.L + dA * c.A
        ones = jnp.ones((n, HEAD_DIM), BF16)
        pm_hi, pm_lo = _split2(pm)
        colsum = _bdot(pm_hi, ones, ta=True) + _bdot(pm_lo, ones, ta=True)
        tk_ = jnp.sum(dkdec * c.kdec, axis=1, keepdims=True)
        dgc = (jnp.sum(pm, axis=1, keepdims=True) - colsum
               + jnp.sum(dqg * c.qg, axis=1, keepdims=True)
               - tk_
               + jnp.sum(dkg * c.kg, axis=1, keepdims=True))
        dgl = dgl + jnp.sum(tk_, axis=0, keepdims=True)
        rowi = lax.broadcasted_iota(jnp.int32, (n, HEAD_DIM), 0)
        dgc = dgc + jnp.where(rowi == n - 1, dgl, 0.0)
        dbeta = jnp.sum(dkb * c.k, axis=1, keepdims=True) + jnp.sum(dvb * c.v, axis=1, keepdims=True)
        return dq * (HEAD_DIM ** -0.5), dk, dvb * c.beta, dbeta, dgc, ds0

    gw = heads * HEAD_DIM

    def body(qkv_ref, bg_ref, s_ref, do_ref, dqkv_ref, dbg_ref, dstate):
        @pl.when(pl.program_id(0) == 0)
        def _():
            dstate[...] = jnp.zeros_like(dstate)

        bg_v = bg_ref[...]
        gc_all, gc_rows = _chunk_gates(bg_v, heads)
        lane = lax.broadcasted_iota(jnp.int32, (n, HEAD_DIM), 1)
        dgates = jnp.zeros((n, HEAD_DIM), F32)
        chains = []
        for h in range(heads):
            col = lambda s: pl.ds(s * gw + h * HEAD_DIM, HEAD_DIM)
            c = _chunk_local(qkv_ref[:, col(0)], qkv_ref[:, col(1)], qkv_ref[:, col(2)], bg_v[:, h:h + 1],
                             gc_all[:, heads + h:heads + h + 1], gc_rows[heads + h:heads + h + 1, :])
            chains.append(one_head(c, s_ref[h, 0], do_ref[:, pl.ds(h * HEAD_DIM, HEAD_DIM)], dstate[h]))
        results = _in_lockstep(chains)
        for h, (dq, dk, dv, dbeta, dgc, ds0) in enumerate(results):
            dgates = jnp.where(lane == h, dbeta, jnp.where(lane == heads + h, dgc, dgates))
        for h, (dq, dk, dv, dbeta, dgc, ds0) in enumerate(results):
            dqkv_ref[:, pl.ds(h * HEAD_DIM, HEAD_DIM)] = dq
            dqkv_ref[:, pl.ds(gw + h * HEAD_DIM, HEAD_DIM)] = dk
            dqkv_ref[:, pl.ds(2 * gw + h * HEAD_DIM, HEAD_DIM)] = dv
            dstate[h] = ds0
        row = lax.broadcasted_iota(jnp.int32, (n, n), 0)
        colm = lax.broadcasted_iota(jnp.int32, (n, n), 1)
        draw = _dot_mask((row >= colm).astype(BF16), dgates, ta=True)
        dbg_ref[...] = jnp.where(lane < heads, dgates, draw)

    last = nchunk - 1
    return pl.pallas_call(
        body, name="gdn_core_bwd", grid=(nchunk,),
        in_specs=[pl.BlockSpec((CHUNK, 3 * gw), lambda i: (last - i, 0)),
                  pl.BlockSpec((CHUNK, HEAD_DIM), lambda i: (last - i, 0)),
                  pl.BlockSpec((heads, 1, HEAD_DIM, HEAD_DIM), lambda i: (0, last - i, 0, 0)),
                  pl.BlockSpec((CHUNK, gw), lambda i: (last - i, 0))],
        out_specs=[pl.BlockSpec((CHUNK, 3 * gw), lambda i: (last - i, 0)),
                   pl.BlockSpec((CHUNK, HEAD_DIM), lambda i: (last - i, 0))],
        out_shape=[jax.ShapeDtypeStruct((t, 3 * gw), F32), jax.ShapeDtypeStruct((t, HEAD_DIM), F32)],
        scratch_shapes=[pltpu.VMEM((heads, HEAD_DIM, HEAD_DIM), F32)],
        compiler_params=_params("arbitrary"),
    )(qkv, bg, states, do)


def _gdn_post(o, proj, z_col0, norm_w, heads, tt):
    t = o.shape[0]
    zb = z_col0 // HEAD_DIM

    def body(o_ref, z_ref, w_ref, out_ref):
        ov = o_ref[...]
        z = z_ref[...]
        rms = lax.rsqrt(jnp.mean(ov * ov, axis=-1, keepdims=True) + NORM_EPS)
        out_ref[...] = (ov * rms * w_ref[...] * (z * _sigmoid(z))).astype(BF16)

    return pl.pallas_call(
        body, name="gdn_post", grid=(t // tt, heads),
        in_specs=[pl.BlockSpec((tt, HEAD_DIM), lambda i, h: (i, h)),
                  pl.BlockSpec((tt, HEAD_DIM), lambda i, h: (i, zb + h)),
                  pl.BlockSpec((1, HEAD_DIM), lambda i, h: (0, 0))],
        out_specs=pl.BlockSpec((tt, HEAD_DIM), lambda i, h: (i, h)),
        out_shape=jax.ShapeDtypeStruct((t, heads * HEAD_DIM), BF16),
        compiler_params=_params("parallel", "parallel"),
    )(o, proj, norm_w)


def _gdn_post_backward(dcat, o, proj, z_col0, norm_w, heads, tt):
    t = o.shape[0]
    zb = z_col0 // HEAD_DIM

    def body(d_ref, o_ref, z_ref, w_ref, do_ref, dz_ref, dw_ref):
        d = d_ref[...]
        ov = o_ref[...]
        z = z_ref[...]
        w = w_ref[...]
        rms = lax.rsqrt(jnp.mean(ov * ov, axis=-1, keepdims=True) + NORM_EPS)
        ohat = ov * rms
        sg = _sigmoid(z)
        gate = z * sg
        dz_ref[...] = (d * ohat * w * (sg * (1.0 + z * (1.0 - sg)))).astype(BF16)
        don = d * gate
        dohat = don * w
        do_ref[...] = rms * (dohat - ohat * jnp.mean(dohat * ohat, axis=-1, keepdims=True))
        dw = jnp.sum(don * ohat, axis=0, keepdims=True)
        first = jnp.logical_and(pl.program_id(0) == 0, pl.program_id(1) == 0)

        @pl.when(first)
        def _():
            dw_ref[...] = dw

        @pl.when(jnp.logical_not(first))
        def _():
            dw_ref[...] += dw

    blk = pl.BlockSpec((tt, HEAD_DIM), lambda i, h: (i, h))
    return pl.pallas_call(
        body, name="gdn_post_bwd", grid=(t // tt, heads),
        in_specs=[blk, blk, pl.BlockSpec((tt, HEAD_DIM), lambda i, h: (i, zb + h)),
                  pl.BlockSpec((1, HEAD_DIM), lambda i, h: (0, 0))],
        out_specs=[blk, blk, pl.BlockSpec((1, HEAD_DIM), lambda i, h: (0, 0))],
        out_shape=[jax.ShapeDtypeStruct((t, heads * HEAD_DIM), F32),
                   jax.ShapeDtypeStruct((t, heads * HEAD_DIM), BF16),
                   jax.ShapeDtypeStruct((1, HEAD_DIM), F32)],
        compiler_params=_params("arbitrary", "arbitrary"),
    )(dcat, o, proj, norm_w)


def _pool_select(levels, group):
    out = levels[-1]
    for gi in range(len(levels) - 2, -1, -1):
        out = jnp.where(group == gi, levels[gi], out)
    return out


def _pool_counts(t, width, group):
    pos = lax.broadcasted_iota(jnp.int32, (t, width), 0)
    win = jnp.left_shift(2, group)
    return jnp.minimum(pos + 1, win).astype(F32)


def _pooled(p, group):
    levels, s, step = [], p, 1
    for _ in POOL_WINDOWS:
        s = s + _shift_down(s, step)
        levels.append(s)
        step *= 2
    cnt = _pool_counts(p.shape[0], p.shape[1], group)
    return _pool_select(levels, group) / cnt - p, cnt


def _pool_forward(proj, p_col0, pool_w, pool_scale):
    t = proj.shape[0]
    groups, cg, _ = pool_w.shape
    pb = p_col0 // cg

    def body(p_ref, w_ref, s_ref, o_ref):
        pooled, _ = _pooled(p_ref[...], pl.program_id(0))
        o_ref[...] = (_bdot(pooled, w_ref[0]) * s_ref[...]).astype(BF16)

    return pl.pallas_call(
        body, name="pool_fwd", grid=(groups,),
        in_specs=[pl.BlockSpec((t, cg), lambda g: (0, pb + g)), pl.BlockSpec((1, cg, cg), lambda g: (g, 0, 0)),
                  pl.BlockSpec((1, cg), lambda g: (0, g))],
        out_specs=pl.BlockSpec((t, cg), lambda g: (0, g)),
        out_shape=jax.ShapeDtypeStruct((t, groups * cg), BF16),
        compiler_params=_params("parallel"),
    )(proj, pool_w, pool_scale)


def _pool_backward(dcat, d_col0, proj, p_col0, pool_w, pool_scale):
    t = proj.shape[0]
    groups, cg, _ = pool_w.shape
    pb = p_col0 // cg
    db = d_col0 // cg

    def body(d_ref, p_ref, w_ref, s_ref, dp_ref, dw_ref, ds_ref):
        group = pl.program_id(0)
        pooled, cnt = _pooled(p_ref[...], group)
        w = w_ref[0]
        d = d_ref[...]
        mixed = _bdot(pooled, w)
        ds_ref[...] = jnp.sum(d * mixed, axis=0, keepdims=True)
        dmixed = d * s_ref[...]
        dw_ref[0] = _bdot(pooled, dmixed, ta=True)
        dpooled = _bdot(dmixed, w, tb=True)
        levels, s, step = [], dpooled / cnt, 1
        for _ in POOL_WINDOWS:
            s = s + _shift_up(s, step)
            levels.append(s)
            step *= 2
        dp_ref[...] = (_pool_select(levels, group) - dpooled).astype(BF16)

    return pl.pallas_call(
        body, name="pool_bwd", grid=(groups,),
        in_specs=[pl.BlockSpec((t, cg), lambda g: (0, db + g)), pl.BlockSpec((t, cg), lambda g: (0, pb + g)),
                  pl.BlockSpec((1, cg, cg), lambda g: (g, 0, 0)), pl.BlockSpec((1, cg), lambda g: (0, g))],
        out_specs=[pl.BlockSpec((t, cg), lambda g: (0, g)), pl.BlockSpec((1, cg, cg), lambda g: (g, 0, 0)),
                   pl.BlockSpec((1, cg), lambda g: (0, g))],
        out_shape=[jax.ShapeDtypeStruct((t, groups * cg), BF16), jax.ShapeDtypeStruct((groups, cg, cg), F32),
                   jax.ShapeDtypeStruct((1, groups * cg), F32)],
        compiler_params=_params("parallel"),
    )(dcat, proj, pool_w, pool_scale)


def _attention(q, k, v, tq):
    t, d = q.shape
    m = k.shape[0]
    dh = d // XATTN_HEADS
    scale = dh ** -0.5

    def body(q_ref, k_ref, v_ref, o_ref):
        s = _bdot(q_ref[...], k_ref[...], tb=True) * scale
        s = s - jnp.max(s, axis=-1, keepdims=True)
        e = jnp.exp(s)
        p = e / jnp.sum(e, axis=-1, keepdims=True)
        o_ref[...] = _bdot(p, v_ref[...]).astype(BF16)

    return pl.pallas_call(
        body, name="xattn_fwd", grid=(XATTN_HEADS, t // tq),
        in_specs=[pl.BlockSpec((tq, dh), lambda h, i: (i, h)), pl.BlockSpec((m, dh), lambda h, i: (0, h)),
                  pl.BlockSpec((m, dh), lambda h, i: (0, h))],
        out_specs=pl.BlockSpec((tq, dh), lambda h, i: (i, h)),
        out_shape=jax.ShapeDtypeStruct((t, d), BF16),
        compiler_params=_params("parallel", "parallel"),
    )(q, k, v)


def _attention_backward(q, k, v, do, tq):
    t, d = q.shape
    m = k.shape[0]
    dh = d // XATTN_HEADS
    scale = dh ** -0.5

    def body(q_ref, k_ref, v_ref, do_ref, dq_ref, dk_ref, dv_ref, dk_acc, dv_acc):
        i = pl.program_id(1)
        qv, kv, vv, dov = q_ref[...], k_ref[...], v_ref[...], do_ref[...]
        s = _bdot(qv, kv, tb=True) * scale
        s = s - jnp.max(s, axis=-1, keepdims=True)
        e = jnp.exp(s)
        p = e / jnp.sum(e, axis=-1, keepdims=True)
        dp = _bdot(dov, vv, tb=True)
        ds = p * (dp - jnp.sum(dp * p, axis=-1, keepdims=True)) * scale
        dq_ref[...] = _bdot(ds, kv).astype(BF16)
        dv_part = _bdot(p, dov, ta=True)
        dk_part = _bdot(ds, qv, ta=True)

        @pl.when(i == 0)
        def _():
            dk_acc[...] = dk_part
            dv_acc[...] = dv_part

        @pl.when(i > 0)
        def _():
            dk_acc[...] += dk_part
            dv_acc[...] += dv_part

        @pl.when(i == pl.num_programs(1) - 1)
        def _():
            dk_ref[...] = dk_acc[...].astype(BF16)
            dv_ref[...] = dv_acc[...].astype(BF16)

    qblk = pl.BlockSpec((tq, dh), lambda h, i: (i, h))
    kblk = pl.BlockSpec((m, dh), lambda h, i: (0, h))
    return pl.pallas_call(
        body, name="xattn_bwd", grid=(XATTN_HEADS, t // tq),
        in_specs=[qblk, kblk, kblk, qblk],
        out_specs=[qblk, kblk, kblk],
        out_shape=[jax.ShapeDtypeStruct((t, d), BF16), jax.ShapeDtypeStruct((m, d), BF16),
                   jax.ShapeDtypeStruct((m, d), BF16)],
        scratch_shapes=[pltpu.VMEM((m, dh), F32), pltpu.VMEM((m, dh), F32)],
        compiler_params=_params("parallel", "arbitrary"),
    )(q, k, v, do)


def _ln_backward_rows(name, dmain, dres, xhat, rstd, gamma, tm):
    t, d = xhat.shape

    def body(m_ref, r_ref, x_ref, s_ref, g_ref, du_ref, dub_ref, dg_ref, db_ref):
        du, dg, db = _ln_backward_math(m_ref[...] + ALPHA * r_ref[...], x_ref[...], s_ref[...], g_ref[...])
        du_ref[...] = du
        dub_ref[...] = du.astype(BF16)
        first = pl.program_id(0) == 0

        @pl.when(first)
        def _():
            dg_ref[...] = dg
            db_ref[...] = db

        @pl.when(jnp.logical_not(first))
        def _():
            dg_ref[...] += dg
            db_ref[...] += db

    row = pl.BlockSpec((tm, d), lambda i: (i, 0))
    vec = pl.BlockSpec((1, d), lambda i: (0, 0))
    return pl.pallas_call(
        body, name=name, grid=(t // tm,),
        in_specs=[row, row, row, pl.BlockSpec((tm, 1), lambda i: (i, 0)), vec],
        out_specs=[row, row, vec, vec],
        out_shape=[jax.ShapeDtypeStruct((t, d), F32), jax.ShapeDtypeStruct((t, d), BF16),
                   jax.ShapeDtypeStruct((1, d), F32), jax.ShapeDtypeStruct((1, d), F32)],
        compiler_params=_params("arbitrary"),
    )(dmain, dres, xhat, rstd, gamma)


def _loss_and_ln_backward(xhat, rstd, gamma, beta, target, tm):
    t, d = xhat.shape

    def body(x_ref, r_ref, g_ref, b_ref, t_ref, du_ref, dub_ref, dg_ref, db_ref, loss_ref):
        xh = x_ref[...]
        g = g_ref[...]
        diff = xh * g + b_ref[...] - t_ref[...]
        part = jnp.sum(jnp.sum(diff * diff, axis=1, keepdims=True), axis=0, keepdims=True) * (0.5 / d)
        dy = diff * (1.0 / d)
        du, dg, db = _ln_backward_math(dy, xh, r_ref[...], g)
        du_ref[...] = du
        dub_ref[...] = du.astype(BF16)
        lossrow = jnp.broadcast_to(part, (1, HEAD_DIM))
        first = pl.program_id(0) == 0

        @pl.when(first)
        def _():
            dg_ref[...] = dg
            db_ref[...] = db
            loss_ref[...] = lossrow

        @pl.when(jnp.logical_not(first))
        def _():
            dg_ref[...] += dg
            db_ref[...] += db
            loss_ref[...] += lossrow

    row = pl.BlockSpec((tm, d), lambda i: (i, 0))
    vec = pl.BlockSpec((1, d), lambda i: (0, 0))
    return pl.pallas_call(
        body, name="loss_ln3_bwd", grid=(t // tm,),
        in_specs=[row, pl.BlockSpec((tm, 1), lambda i: (i, 0)), vec, vec, row],
        out_specs=[row, row, vec, vec, pl.BlockSpec((1, HEAD_DIM), lambda i: (0, 0))],
        out_shape=[jax.ShapeDtypeStruct((t, d), F32), jax.ShapeDtypeStruct((t, d), BF16),
                   jax.ShapeDtypeStruct((1, d), F32), jax.ShapeDtypeStruct((1, d), F32),
                   jax.ShapeDtypeStruct((1, HEAD_DIM), F32)],
        compiler_params=_params("arbitrary"),
    )(xhat, rstd, gamma, beta, target)


def _after(token, a):
    return a if token is None else a + token[:1, :1].astype(a.dtype)


def _pick(n, prefs):
    for p in prefs:
        if n % p == 0:
            return p
    return n


def _local_step(x, mem, target, w, token=None):
    t, d = x.shape
    heads = w["a_log"].shape[1]
    gw = heads * HEAD_DIM
    groups, cg, _ = w["pool_w"].shape
    pw = groups * cg
    n_main = 4 * gw + pw
    in_cols = n_main + 2 * heads
    s_in = w["w_in_t"].shape[0]

    tm = _pick(t, (512, 256, 128))
    tm_ln = _pick(t, (256, 128))
    tm_big = _pick(t, (1024, 512, 256, 128))
    tk = _pick(d, K_STEPS)

    w_in_t = w["w_in_t"].reshape(in_cols, d)
    w_p_t = w_in_t[4 * gw + 2 * heads:]
    w_ba_t = jnp.pad(w_in_t[4 * gw:4 * gw + 2 * heads], ((0, HEAD_DIM - 2 * heads), (0, 0)))
    x_bf = _after(token, x).astype(BF16)
    mem_bf = _after(token, mem).astype(BF16)

    tn_d = _pick(d, (1024, 512, 256, 128))
    proj = _plain("proj_main", x_bf, w_in_t, tb=True, n_used=4 * gw, tm=tm_big, tn=_pick(4 * gw, (1024, 512, 256, 128)),
                  tk=tk, out_dtype=F32)
    pproj = _plain("proj_pool", x_bf, w_p_t, tb=True, tm=tm_big, tn=_pick(pw, (1024, 512, 256, 128)), tk=tk, out_dtype=F32)
    ea, dtb = _gate_vectors(w["a_log"], w["dt_bias"], heads)
    vec128 = lambda i, j: (0, 0)
    ba, bg = _matmul(
        "proj_gates", x_bf, w_ba_t, tb=True, tm=tm, tn=HEAD_DIM, tk=tk,
        extra=[(ea, (1, HEAD_DIM), vec128), (dtb, (1, HEAD_DIM), vec128)],
        outs=[(jax.ShapeDtypeStruct((t, HEAD_DIM), F32), (tm, HEAD_DIM), _tile)] * 2,
        epilogue=_gates_epilogue(heads))
    qkv = _gdn_pre(proj, w["conv_w"], heads)
    o_gdn, states = _gdn_core(qkv, bg, heads)
    cat_g = _gdn_post(o_gdn, proj, 3 * gw, w["gdn_norm_w"], heads, tm)
    cat_p = _pool_forward(pproj, 0, w["pool_w"], w["pool_scale"])
    cat = jnp.concatenate([cat_g, cat_p], axis=1)
    w = {**w, **(yield ("weights", 1, cat))}
    h1, h1_bf, xhat1, rstd1 = _ln_forward("mix_ln1", cat, w["w_out"], x, w["ln1_g"], w["ln1_b"], tm=tm_ln, tk=tk)

    q = _plain("xattn_q", h1_bf, w["xq_w"], tm=tm, tn=tn_d, tk=tk, out_dtype=BF16)
    mlen = mem.shape[0]
    tm_mem = _pick(mlen, (256, 128))
    k = _plain("xattn_k", mem_bf, w["xk_w"], tm=tm_mem, tn=tn_d, tk=tk, out_dtype=BF16)
    v = _plain("xattn_v", mem_bf, w["xv_w"], tm=tm_mem, tn=tn_d, tk=tk, out_dtype=BF16)
    att = _attention(q, k, v, tm)
    h2, h2_bf, xhat2, rstd2 = _ln_forward("xo_ln2", att, w["xo_w"], h1, w["ln2_g"], w["ln2_b"], tm=tm_ln, tk=tk)

    w = {**w, **(yield ("weights", 2, h2_bf))}
    s_up = w["w_up3"].shape[0]
    ff = s_up * w["w_up3"].shape[2]
    tn_f = _pick(ff // s_up, (1024, 512, 256, 128))

    def up_epi(acc, ex, out, i):
        r = jnp.maximum(acc, 0.0)
        out[0][...] = (r * r).astype(BF16)
        out[1][...] = (2.0 * r).astype(BF16)

    act, act_grad = _matmul(
        "mlp_up", h2_bf, w["w_up3"], b_blocks=s_up, tm=tm_big, tn=tn_f, tk=tk,
        outs=[(jax.ShapeDtypeStruct((t, ff), BF16), (tm_big, tn_f), _tile)] * 2, epilogue=up_epi)
    got = yield ("weights", 3, act)
    order = got["shard_order"]
    row = lambda i: (i, 0)
    full_rows = (jax.ShapeDtypeStruct((t, d), F32), (tm, d), row)

    def partial_epi(acc, ex, out):
        out[0][...] = acc

    def down_ln_epi(acc, ex, out):
        u = ALPHA * ex[1][...] + (ex[0][...] + acc)
        xc = u - jnp.mean(u, axis=-1, keepdims=True)
        rstd = lax.rsqrt(jnp.mean(xc * xc, axis=-1, keepdims=True) + LN_EPS)
        out[0][...] = xc * rstd
        out[1][...] = rstd

    early = order.shape[0] - 1
    ff_early = _shard_matmul("down_early", act, got["w_down3"], order[:early], tm=tm, outs=[full_rows],
                             epilogue=partial_epi)[0]
    w_down3 = (yield ("weights", 4, ff_early))["w_down3"]
    xhat3, rstd3 = _shard_matmul(
        "down_ln3", act, w_down3, order[early:], tm=tm, extra=[(ff_early, (tm, d), row), (h2, (tm, d), row)],
        outs=[full_rows, (jax.ShapeDtypeStruct((t, 1), F32), (tm, 1), row)], epilogue=down_ln_epi)
    w = {**w, "w_down": w_down3.reshape(ff, d)}

    grads = {}
    du3, du3_bf, grads["ln3_g"], grads["ln3_b"], loss = _loss_and_ln_backward(
        xhat3, rstd3, w["ln3_g"], w["ln3_b"], target, tm_ln)

    def dup_epi(acc, ex, out, i):
        out[0][...] = (acc * ex[0][...].astype(F32)).astype(BF16)

    dup = _matmul(
        "mlp_down_dx", du3_bf, w["w_down"], tb=True, tm=tm_big, tn=tn_f, tk=tk,
        extra=[(act_grad, (tm_big, tn_f), _tile)],
        outs=[(jax.ShapeDtypeStruct((t, ff), BF16), (tm_big, tn_f), _tile)], epilogue=dup_epi)[0]
    tk_t = _pick(t, K_STEPS)
    tm_w = _pick(d, (512, 256, 128))
    grads["w_down"] = _plain("mlp_down_dw", act, du3_bf, ta=True, tm=_pick(ff, (512, 256, 128)), tn=d, tk=tk_t,
                             out_dtype=F32)
    grads["w_up3"] = _plain("mlp_up_dw", h2_bf, dup, ta=True, tm=tm_w, tn=ff // s_up, tk=tk_t, out_dtype=F32, out3=s_up,
                            n_outer=True)
    token = yield ("grads", 0, {n: grads.pop(n) for n in ("w_down", "w_up3")})
    dh2 = _plain("mlp_up_dx", dup, w["w_up3"], tb=True, b_blocks=s_up, tm=tm_big, tn=tn_d,
                 tk=_pick(ff // s_up, K_STEPS), out_dtype=F32)
    du2, du2_bf, grads["ln2_g"], grads["ln2_b"] = _ln_backward_rows(
        "ln2_bwd", dh2, du3, xhat2, rstd2, _after(token, w["ln2_g"]), tm_ln)
    token = yield ("poll", 0, du2_bf)

    grads["xo_w"] = _plain("xo_dw", att, du2_bf, ta=True, tm=tm_w, tn=d, tk=tk_t, out_dtype=F32)
    datt = _plain("xo_dx", du2_bf, w["xo_w"], tb=True, tm=tm, tn=tn_d, tk=tk, out_dtype=BF16)
    dq, dk, dv = _attention_backward(q, k, v, datt, tm)
    tk_m = _pick(mlen, (256, 128))
    grads["xq_w"] = _plain("xq_dw", h1_bf, dq, ta=True, tm=tm_w, tn=d, tk=tk_t, out_dtype=F32)
    grads["xk_w"] = _plain("xk_dw", mem_bf, dk, ta=True, tm=tm_w, tn=tn_d, tk=tk_m, out_dtype=F32)
    grads["xv_w"] = _plain("xv_dw", mem_bf, dv, ta=True, tm=tm_w, tn=tn_d, tk=tk_m, out_dtype=F32)
    du1, du1_bf, grads["ln1_g"], grads["ln1_b"] = _ln_backward(
        "xq_dx_ln1", dq, w["xq_w"], du2, xhat1, rstd1, _after(token, w["ln1_g"]), tm=tm_ln, tk=tk)

    grads["w_out"] = _plain("out_dw", cat, du1_bf, ta=True, tm=tm_w, tn=d, tk=tk_t, out_dtype=F32)
    token = yield ("grads", 1, {n: grads.pop(n) for n in ("xo_w", "xq_w", "xk_w", "xv_w", "w_out")})
    dcat = _plain("out_dx", du1_bf, w["w_out"], tb=True, tm=tm, tn=tn_d, tk=tk, out_dtype=F32)
    dp, grads["pool_w"], grads["pool_scale"] = _pool_backward(dcat, gw, pproj, 0, w["pool_w"],
                                                              _after(token, w["pool_scale"]))
    do_gdn, dz, grads["gdn_norm_w"] = _gdn_post_backward(dcat, o_gdn, proj, 3 * gw, _after(token, w["gdn_norm_w"]),
                                                         heads, tm)
    dqkv, dbg = _gdn_core_backward(qkv, bg, states, do_gdn, heads)
    token = yield ("poll", 1, dqkv)
    dqkv_pre, grads["conv_w"] = _gdn_pre_backward(proj, _after(token, w["conv_w"]), dqkv, heads)
    dba, dalog_row, ddt_row = _gates_backward(ba, bg, dbg, ea, dtb, heads)
    grads["a_log"] = dalog_row[:, heads:2 * heads]
    grads["dt_bias"] = ddt_row[:, heads:2 * heads]

    dproj = jnp.concatenate([dqkv_pre, dz, dp], axis=1)
    dw_main = _plain("proj_dw", dproj, x_bf, ta=True, tm=_pick(n_main, (512, 256, 128)), tn=d, tk=tk_t, out_dtype=F32)
    dw_ba = _plain("proj_gates_dw", dba, x_bf, ta=True, tm=HEAD_DIM, tn=tn_d, tk=tk_t, out_dtype=F32)
    dw_in_t = jnp.concatenate([dw_main[:4 * gw], dw_ba[:2 * heads], dw_main[4 * gw:]], axis=0)
    grads["w_in_t"] = dw_in_t.reshape(s_in, in_cols // s_in, d)

    def dx_epi(acc, ex, out, i):
        out[0][...] = acc + ex[1][...] + ALPHA * ex[0][...]

    def add_epi(acc, ex, out, i):
        out[0][...] = acc + ex[0][...]

    token = yield ("grads", 2, {n: grads.pop(n) for n in ("w_in_t", "pool_w")})
    dx_gates = _plain("proj_gates_dx", dba, _after(token, w_ba_t), tm=tm, tn=tn_d, tk=HEAD_DIM, out_dtype=F32)
    out_tile = [(jax.ShapeDtypeStruct((t, d), F32), (tm, tn_d), _tile)]
    dx_pool = _matmul("proj_pool_dx", dp, w_p_t, tm=tm, tn=tn_d, tk=_pick(pw, K_STEPS),
                      extra=[(dx_gates, (tm, tn_d), _tile)], outs=out_tile, epilogue=add_epi)[0]
    grad_x = _matmul(
        "proj_dx", dproj, w_in_t, k_used=4 * gw, tm=tm, tn=tn_d, tk=_pick(4 * gw, K_STEPS),
        extra=[(du1, (tm, tn_d), _tile), (dx_pool, (tm, tn_d), _tile)], outs=out_tile, epilogue=dx_epi)[0]
    yield ("poll", 2, grad_x)
    return loss, grad_x, grads


def _adamw(name, w, g, m, v):
    r, c = w.shape
    if r % 8 == 0:
        tr = _pick(r, (256, 128, 64, 32, 16, 8))
        blk, steps = pl.BlockSpec((tr, c), lambda i: (i, 0)), r // tr
    else:
        tc = _pick(c, (256, 128))
        blk, steps = pl.BlockSpec((r, tc), lambda i: (0, i)), c // tc
    c1 = 1.0 - ADAM_B1 ** ADAM_STEP
    c2 = 1.0 - ADAM_B2 ** ADAM_STEP

    def body(w_ref, g_ref, m_ref, v_ref, d_ref, mo_ref, vo_ref):
        gv = g_ref[...]
        mn = ADAM_B1 * m_ref[...] + (1.0 - ADAM_B1) * gv
        vn = ADAM_B2 * v_ref[...] + (1.0 - ADAM_B2) * (gv * gv)
        d_ref[...] = -ADAM_LR * ((mn / c1) / (jnp.sqrt(vn / c2) + ADAM_EPS) + ADAM_WD * w_ref[...])
        mo_ref[...] = mn
        vo_ref[...] = vn

    return pl.pallas_call(
        body, name=name, grid=(steps,), in_specs=[blk] * 4, out_specs=[blk] * 3,
        out_shape=[jax.ShapeDtypeStruct((r, c), F32)] * 3,
        compiler_params=_params("parallel"),
    )(w, g, m, v)


def _place():
    x, y, c = lax.axis_index("x"), lax.axis_index("y"), lax.axis_index("c")
    chips = [(1 - x, y), (x, 1 - y), (1 - x, 1 - y)]
    return x, y, c, chips


HBM = pl.BlockSpec(memory_space=pltpu.HBM)


SEM = pl.BlockSpec(memory_space=pltpu.SEMAPHORE)
ANY = pl.BlockSpec(memory_space=pl.ANY)
EFFECT = pltpu.SideEffectType.DATAFLOW_SIDE_EFFECTING


def _in_hbm(a):
    return pltpu.with_memory_space_constraint(a, pltpu.HBM)


def _remote(src, dst, send_sem, recv_sem, to):
    return pltpu.make_async_remote_copy(src_ref=src, dst_ref=dst, send_sem=send_sem, recv_sem=recv_sem,
                                        device_id=to, device_id_type=MESH)


def _by_rows(rows):
    return rows % 32 == 0


def _half_shape(rows, cols):
    return (rows // 2, cols) if _by_rows(rows) else (rows, cols // 2)


def _half(ref, which, *lead):
    rows, cols = ref.shape[-2:]
    if _by_rows(rows):
        return ref.at[(*lead, pl.ds(which * (rows // 2), rows // 2))]
    return ref.at[(*lead, slice(None), pl.ds(which * (cols // 2), cols // 2))]


def _landed(lands, i, shard_index, which):
    return _half(lands[i], which, shard_index)


def _gather_start(name, shards, after):
    n = len(shards)
    lands = [lax.empty((N_SHARD,) + s.shape, s.dtype) for s in shards]

    def body(*refs):
        ins, zones = refs[:n], refs[n:2 * n]
        ici_send, ici_recv, own_send, own_recv = refs[2 * n + 1:2 * n + 5]
        token = refs[-1]
        x, y, c, chips = _place()
        me = 2 * x + y
        for i in range(n):
            for j, chip in enumerate(chips):
                _remote(_half(ins[i], c), _landed(zones, i, me, c), ici_send.at[3 * i + j],
                        ici_recv.at[3 * i + j], (*chip, c)).start()
        for i in range(n):
            _remote(ins[i], zones[i].at[me], own_send.at[i], own_recv.at[i], (x, y, 1 - c)).start()
        token[...] = jnp.zeros_like(token)

    dma = pltpu.SemaphoreType.DMA
    outs = pl.pallas_call(
        body, name=name,
        in_specs=[HBM] * (2 * n) + [ANY],
        out_shape=(dma((3 * n,)), dma((3 * n,)), dma((n,)), dma((n,)),
                   *[pltpu.HBM(a.shape, a.dtype) for a in shards + lands], jax.ShapeDtypeStruct((8, LANES), F32)),
        out_specs=(SEM, SEM, SEM, SEM, *[HBM] * (2 * n), pl.BlockSpec(memory_space=pltpu.VMEM)),
        input_output_aliases={k: 4 + k for k in range(2 * n)},
        compiler_params=pltpu.CompilerParams(has_side_effects=EFFECT),
    )(*[_in_hbm(a) for a in shards + lands], after)
    sems = dict(zip(("ici_send", "ici_recv", "own_send", "own_recv"), outs[:4]))
    return sems, list(outs[4:4 + n]), list(outs[4 + n:4 + 2 * n]), outs[-1]


def _gather_forward(name, idx, lands, sems, after, slots=(0, 1, 2)):
    n = len(idx)

    def body(*refs):
        zones = refs[:n]
        ici_recv = refs[n]
        fwd_send, fwd_recv = refs[n + 2], refs[n + 3]
        x, y, c, chips = _place()
        for k, i in enumerate(idx):
            for j, chip in enumerate(chips):
                if j not in slots:
                    continue
                half = _landed(zones, k, 2 * chip[0] + chip[1], c)
                _remote(half, half, fwd_send.at[3 * k + j], ici_recv.at[3 * i + j], (*chip, c)).wait_recv()
                _remote(half, half, fwd_send.at[3 * k + j], fwd_recv.at[3 * k + j], (x, y, 1 - c)).start()

    dma = pltpu.SemaphoreType.DMA
    outs = pl.pallas_call(
        body, name=name,
        in_specs=[HBM] * n + [SEM, ANY],
        out_shape=(dma((3 * n,)), dma((3 * n,)), *[pltpu.HBM(a.shape, a.dtype) for a in lands]),
        out_specs=(SEM, SEM, *[HBM] * n),
        input_output_aliases={k: 2 + k for k in range(n)},
        compiler_params=pltpu.CompilerParams(has_side_effects=EFFECT),
    )(*lands, sems["ici_recv"], after)
    return (outs[0], outs[1]), list(outs[2:])


def _gather_wait(name, idx, shards, lands, sems, fwd, after, slots=(0, 1, 2), own=True):
    n = len(idx)

    def body(*refs):
        ins, zones = refs[:n], refs[n:2 * n]
        ici_send, own_send, own_recv, fwd_send, fwd_recv = refs[2 * n:2 * n + 5]
        x, y, c, chips = _place()
        me = 2 * x + y
        for k, i in enumerate(idx):
            mine = _half(ins[k], c)
            for j, chip in enumerate(chips):
                if j not in slots:
                    continue
                theirs = 2 * chip[0] + chip[1]
                _remote(mine, _landed(zones, k, me, c), ici_send.at[3 * i + j], fwd_recv.at[3 * k + j],
                        (*chip, c)).wait_send()
                sent = _landed(zones, k, theirs, c)
                _remote(sent, sent, fwd_send.at[3 * k + j], fwd_recv.at[3 * k + j], (x, y, 1 - c)).wait_send()
                passed = _landed(zones, k, theirs, 1 - c)
                _remote(passed, passed, fwd_send.at[3 * k + j], fwd_recv.at[3 * k + j], (x, y, 1 - c)).wait_recv()
            if own:
                sibling = _remote(ins[k], zones[k].at[me], own_send.at[i], own_recv.at[i], (x, y, 1 - c))
                sibling.wait_send()
                sibling.wait_recv()

    outs = pl.pallas_call(
        body, name=name,
        in_specs=[HBM] * (2 * n) + [SEM] * 5 + [ANY],
        out_shape=tuple(pltpu.HBM(a.shape, a.dtype) for a in lands),
        out_specs=tuple([HBM] * n),
        input_output_aliases={n + k: k for k in range(n)},
        compiler_params=pltpu.CompilerParams(has_side_effects=EFFECT),
    )(*shards, *lands, sems["ici_send"], sems["own_send"], sems["own_recv"], fwd[0], fwd[1], after)
    return list(outs)


def _all_reduce_small(name, slab, after=None):
    r, width = slab.shape
    ndev = 8

    def body(x_ref, after_ref, out_ref, buf, send_sems, recv_sems):
        x, y, c, _ = _place()
        me = 4 * x + 2 * y + c
        buf[me] = x_ref[...]
        copies = []
        for k in range(1, ndev):
            peer = jnp.bitwise_xor(me, k)
            to = (peer // 4, (peer // 2) % 2, peer % 2)
            cp = pltpu.make_async_remote_copy(src_ref=x_ref, dst_ref=buf.at[me], send_sem=send_sems.at[k - 1],
                                              recv_sem=recv_sems.at[k - 1], device_id=to, device_id_type=MESH)
            cp.start()
            copies.append(cp)
        for k in range(1, ndev):
            peer = jnp.bitwise_xor(me, k)
            pltpu.make_async_remote_copy(src_ref=x_ref, dst_ref=buf.at[peer], send_sem=send_sems.at[k - 1],
                                         recv_sem=recv_sems.at[k - 1], device_id=(x, y, c),
                                         device_id_type=MESH).wait_recv()
        for cp in copies:
            cp.wait_send()
        total = buf[0]
        for d in range(1, ndev):
            total = total + buf[d]
        out_ref[...] = total

    return pl.pallas_call(
        body, name=name,
        in_specs=[pl.BlockSpec(memory_space=pltpu.VMEM), ANY], out_specs=pl.BlockSpec(memory_space=pltpu.VMEM),
        out_shape=jax.ShapeDtypeStruct((r, width), F32),
        scratch_shapes=[pltpu.VMEM((ndev, r, width), F32), pltpu.SemaphoreType.DMA((ndev - 1,)),
                        pltpu.SemaphoreType.DMA((ndev - 1,))],
        compiler_params=pltpu.CompilerParams(vmem_limit_bytes=VMEM_LIMIT),
    )(slab, slab if after is None else after)


def _half_tiling(rows, cols):
    if _by_rows(rows):
        tr = _pick(rows // 2, (256, 128, 64, 32, 16))
        nb = (rows // 2) // tr
        return (tr, cols), nb, (lambda which, b: (which * nb + b, 0)), (lambda b: (b, 0))
    tc = _pick(cols // 2, (256, 128))
    nb = (cols // 2) // tc
    return (rows, tc), nb, (lambda which, b: (0, which * nb + b)), (lambda b: (0, b))


def _chip_partial(name, grad, other, core):
    s, r, cdim = grad.shape
    blk, nb, whole, within = _half_tiling(r, cdim)

    def body(core_ref, g_ref, o_ref, out_ref):
        out_ref[...] = (g_ref[...] + o_ref[...]).astype(BF16)

    return pl.pallas_call(
        body, name=name,
        grid_spec=pltpu.PrefetchScalarGridSpec(
            num_scalar_prefetch=1, grid=(s, nb),
            in_specs=[pl.BlockSpec((None,) + blk, lambda j, b, core_ref: (j,) + whole(core_ref[0], b)),
                      pl.BlockSpec((None,) + blk, lambda j, b, core_ref: (j,) + within(b))],
            out_specs=pl.BlockSpec((None,) + blk, lambda j, b, core_ref: (j,) + within(b))),
        out_shape=jax.ShapeDtypeStruct((s,) + _half_shape(r, cdim), BF16),
        compiler_params=_params("parallel", "parallel"),
    )(core, grad, other)


def _partial_copies(ins, zones, send_sems, recv_sems):
    x, y, c, chips = _place()
    return [_remote(ins[i].at[2 * chip[0] + chip[1]], zones[i].at[j], send_sems.at[3 * i + j],
                    recv_sems.at[3 * i + j], (*chip, c))
            for i in range(len(ins)) for j, chip in enumerate(chips)]


def _swap_copies(ins, zones, send_sems, recv_sems):
    x, y, c, _ = _place()
    copies = []
    for i in range(len(ins)):
        for s in range(N_SHARD):
            copies.append(_remote(_half(ins[i], 1 - c, s), zones[i].at[s],
                                  send_sems.at[N_SHARD * i + s], recv_sems.at[N_SHARD * i + s], (x, y, 1 - c)))
    return copies


def _exchange_start(name, plan, sources, lands, per_array):
    n = len(sources)
    lands = [lax.empty(shape, dtype) for shape, dtype in lands]

    def body(*refs):
        for cp in plan(refs[:n], refs[n:2 * n], refs[2 * n], refs[2 * n + 1]):
            cp.start()
        refs[-1][...] = jnp.zeros_like(refs[-1])

    dma = pltpu.SemaphoreType.DMA
    outs = pl.pallas_call(
        body, name=name,
        in_specs=[HBM] * (2 * n),
        out_shape=(dma((per_array * n,)), dma((per_array * n,)),
                   *[pltpu.HBM(a.shape, a.dtype) for a in list(sources) + lands], jax.ShapeDtypeStruct((8, LANES), F32)),
        out_specs=(SEM, SEM, *[HBM] * (2 * n), pl.BlockSpec(memory_space=pltpu.VMEM)),
        input_output_aliases={k: 2 + k for k in range(2 * n)},
        compiler_params=pltpu.CompilerParams(has_side_effects=EFFECT),
    )(*[_in_hbm(a) for a in list(sources) + lands])
    return (outs[0], outs[1]), list(outs[2:2 + n]), list(outs[2 + n:2 + 2 * n]), outs[-1]


def _exchange_wait(name, plan, started, after):
    sems, partials, lands, _ = started
    n = len(partials)

    def body(*refs):
        for cp in plan(refs[:n], refs[n:2 * n], refs[2 * n], refs[2 * n + 1]):
            cp.wait_send()
            cp.wait_recv()

    outs = pl.pallas_call(
        body, name=name,
        in_specs=[HBM] * (2 * n) + [SEM, SEM] + [ANY] * len(after),
        out_shape=tuple(pltpu.HBM(a.shape, a.dtype) for a in lands),
        out_specs=tuple([HBM] * n),
        input_output_aliases={n + k: k for k in range(n)},
        compiler_params=pltpu.CompilerParams(has_side_effects=EFFECT),
    )(*partials, *lands, sems[0], sems[1], *after)
    return list(outs)


def _reduce_own(name, grad, other, received, where):
    s, r, cdim = grad.shape
    blk, nb, whole, within = _half_tiling(r, cdim)

    def body(where_ref, g_ref, o_ref, r_ref, out_ref):
        total = g_ref[...] + o_ref[...]
        for j in range(3):
            total = total + r_ref[j].astype(F32)
        out_ref[...] = total

    return pl.pallas_call(
        body, name=name,
        grid_spec=pltpu.PrefetchScalarGridSpec(
            num_scalar_prefetch=1, grid=(nb,),
            in_specs=[pl.BlockSpec((None,) + blk, lambda b, w_ref: (w_ref[0],) + whole(w_ref[1], b)),
                      pl.BlockSpec((None,) + blk, lambda b, w_ref: (w_ref[0],) + within(b)),
                      pl.BlockSpec((3,) + blk, lambda b, w_ref: (0,) + within(b))],
            out_specs=pl.BlockSpec(blk, lambda b, w_ref: whole(w_ref[1], b))),
        out_shape=jax.ShapeDtypeStruct((r, cdim), F32),
        compiler_params=_params("parallel"),
    )(where, grad, other, received)


def _join_start(name, halves):
    n = len(halves)

    def body(*refs):
        bufs, send_sems, recv_sems = refs[:n], refs[n], refs[n + 1]
        x, y, c, _ = _place()
        for i in range(n):
            mine = _half(bufs[i], c)
            _remote(mine, mine, send_sems.at[i], recv_sems.at[i], (x, y, 1 - c)).start()
        refs[-1][...] = jnp.zeros_like(refs[-1])

    dma = pltpu.SemaphoreType.DMA
    outs = pl.pallas_call(
        body, name=name,
        in_specs=[HBM] * n,
        out_shape=(dma((n,)), dma((n,)), *[pltpu.HBM(h.shape, F32) for h in halves], jax.ShapeDtypeStruct((8, LANES), F32)),
        out_specs=(SEM, SEM, *[HBM] * n, pl.BlockSpec(memory_space=pltpu.VMEM)),
        input_output_aliases={k: 2 + k for k in range(n)},
        compiler_params=pltpu.CompilerParams(has_side_effects=EFFECT),
    )(*[_in_hbm(h) for h in halves])
    return (outs[0], outs[1]), list(outs[2:2 + n]), outs[-1]


def _join_wait(name, started, after):
    sems, bufs, _ = started
    n = len(bufs)

    def body(*refs):
        bufs, send_sems, recv_sems = refs[:n], refs[n], refs[n + 1]
        x, y, c, _ = _place()
        for i in range(n):
            mine, theirs = _half(bufs[i], c), _half(bufs[i], 1 - c)
            _remote(mine, mine, send_sems.at[i], recv_sems.at[i], (x, y, 1 - c)).wait_send()
            _remote(theirs, theirs, send_sems.at[i], recv_sems.at[i], (x, y, 1 - c)).wait_recv()

    outs = pl.pallas_call(
        body, name=name,
        in_specs=[HBM] * n + [SEM, SEM] + [ANY] * len(after),
        out_shape=tuple(pltpu.HBM(b.shape, F32) for b in bufs),
        out_specs=tuple([HBM] * n),
        input_output_aliases={k: k for k in range(n)},
        compiler_params=pltpu.CompilerParams(has_side_effects=EFFECT),
    )(*bufs, sems[0], sems[1], *after)
    return list(outs)


BIG = ("w_in", "pool_w", "w_out", "xq_w", "xk_w", "xv_w", "xo_w", "w_up", "w_down")
GATHER_GROUPS = ((0, 1), (2, 3, 4, 5, 6), (7,), (8,))
SMALL = ("conv_w", "a_log", "dt_bias", "gdn_norm_w", "pool_scale", "ln1_g", "ln1_b", "ln2_g", "ln2_b", "ln3_g", "ln3_b")
ORDER = ("w_in", "conv_w", "a_log", "dt_bias", "gdn_norm_w", "pool_w", "pool_scale", "w_out", "ln1_g", "ln1_b",
         "xq_w", "xk_w", "xv_w", "xo_w", "ln2_g", "ln2_b", "w_up", "w_down", "ln3_g", "ln3_b")
LANES = 128


def _rows(flat_len):
    return -(-flat_len // LANES)


def _pack(pieces):
    out = []
    for p in pieces:
        flat = p.reshape(-1).astype(F32)
        out.append(jnp.pad(flat, (0, _rows(flat.shape[0]) * LANES - flat.shape[0])).reshape(-1, LANES))
    slab = jnp.concatenate(out, axis=0)
    return jnp.pad(slab, ((0, -slab.shape[0] % 8), (0, 0)))


def _unpack(slab, shapes):
    out, row = [], 0
    for shp in shapes:
        size = math.prod(shp)
        out.append(slab[row:row + _rows(size)].reshape(-1)[:size].reshape(shp))
        row += _rows(size)
    return out


TRANSPOSED = ("w_in",)


def _as2d(name, a):
    a = a[0]
    if name in TRANSPOSED:
        return jnp.swapaxes(a, 0, 1)
    return a.reshape(-1, a.shape[-1]) if a.ndim == 3 else a


def _from2d(name, a, shape):
    return (jnp.swapaxes(a, 0, 1) if name in TRANSPOSED else a).reshape(shape)


def kernel(x, mem, w_in, conv_w, a_log, dt_bias, gdn_norm_w, pool_w, pool_scale, w_out, ln1_g, ln1_b, xq_w, xk_w, xv_w, xo_w, ln2_g, ln2_b, w_up, w_down, ln3_g, ln3_b, loss_target, m_w_in, m_conv_w, m_a_log, m_dt_bias, m_gdn_norm_w, m_pool_w, m_pool_scale, m_w_out, m_ln1_g, m_ln1_b, m_xq_w, m_xk_w, m_xv_w, m_xo_w, m_ln2_g, m_ln2_b, m_w_up, m_w_down, m_ln3_g, m_ln3_b, v_w_in, v_conv_w, v_a_log, v_dt_bias, v_gdn_norm_w, v_pool_w, v_pool_scale, v_w_out, v_ln1_g, v_ln1_b, v_xq_w, v_xk_w, v_xv_w, v_xo_w, v_ln2_g, v_ln2_b, v_w_up, v_w_down, v_ln3_g, v_ln3_b):
    given = dict(locals())
    cx, cy, cc = lax.axis_index("x"), lax.axis_index("y"), lax.axis_index("c")
    me = 2 * cx + cy
    groups = pool_w.shape[1]
    cs = pool_w.shape[2]
    kk, conv_cols = conv_w.shape[1], conv_w.shape[2]
    core = cc.astype(jnp.int32).reshape(1)
    where = jnp.stack([me, cc]).astype(jnp.int32)

    conv_slab = jnp.zeros((kk, N_SHARD * conv_cols), F32)
    conv_slab = lax.dynamic_update_slice(conv_slab, conv_w[0] * (cc == 0).astype(F32), (0, me * conv_cols))
    wts = {"conv_w": _unpack(_all_reduce_small("gather_conv_w", _pack([conv_slab])), [conv_slab.shape])[0]}

    started = {}

    def start(name, idx, after, token=None):
        casts = [_after(token, _as2d(BIG[i], given[BIG[i]])).astype(BF16) for i in idx]
        sems, shards, lands, token = _gather_start(name, casts, after)
        for k, i in enumerate(idx):
            started[i] = (sems, k, shards[k], lands[k])
        return token

    token = start("gather_start_first", GATHER_GROUPS[0], wts["conv_w"])
    token = start("gather_start_rest", tuple(i for group in GATHER_GROUPS[1:] for i in group), token, token)

    w_down_group = len(GATHER_GROUPS) - 1
    w_down_zones = []

    def fetch_w_down(last, after):
        sems, k, shard, zone = started[GATHER_GROUPS[w_down_group][0]]
        slots = (2,) if last else (0, 1)
        name = "w_down_last" if last else "w_down_early"
        fwd, zones = _gather_forward("gather_forward_" + name, [k], w_down_zones if last else [zone], sems, after, slots)
        w_down_zones[:] = _gather_wait("gather_wait_" + name, [k], [shard], zones, sems, fwd, after, slots, own=not last)
        order = jnp.stack([me, 2 * (1 - cx) + cy, 2 * cx + 1 - cy, 2 * (1 - cx) + 1 - cy]).astype(jnp.int32)
        return {"w_down3": w_down_zones[0], "shard_order": order}

    def fetch(group, after):
        if group >= w_down_group:
            return fetch_w_down(group > w_down_group, after)
        members = [started[i] for i in GATHER_GROUPS[group]]
        sems, idx = members[0][0], [m[1] for m in members]
        fwd, zones = _gather_forward(f"gather_forward_{group}", idx, [m[3] for m in members], sems, after)
        full = dict(zip([BIG[i] for i in GATHER_GROUPS[group]],
                        _gather_wait(f"gather_wait_{group}", idx, [m[2] for m in members], zones, sems, fwd, after)))
        out = {}
        for n, a in full.items():
            if n == "w_in":
                out["w_in_t"] = a
            elif n == "w_up":
                out["w_up3"] = a
            elif n == "pool_w":
                out[n] = a.reshape(N_SHARD, groups, cs, -1).transpose(1, 0, 2, 3).reshape(groups, N_SHARD * cs, -1)
            else:
                out[n] = a.reshape(-1, a.shape[-1])
        return out

    for n in ("a_log", "dt_bias", "gdn_norm_w", "pool_scale", "ln1_g", "ln1_b", "ln2_g", "ln2_b", "ln3_g", "ln3_b"):
        wts[n] = given[n]
    wts.update(fetch(0, token))

    def start_swap(group, grads):
        names, blocks = [], []
        for n, g in grads.items():
            if n == "pool_w":
                g = g.reshape(groups, N_SHARD, cs, -1).transpose(1, 0, 2, 3).reshape(N_SHARD, groups * cs, -1)
            elif g.ndim == 2:
                g = g.reshape(N_SHARD, -1, g.shape[-1])
            names.append({"w_in_t": "w_in", "w_up3": "w_up"}.get(n, n))
            blocks.append(g)
        zones = [((N_SHARD,) + _half_shape(b.shape[1], b.shape[2]), F32) for b in blocks]
        swap = _exchange_start(f"grad_swap_start_{group}", _swap_copies, blocks, zones, N_SHARD)
        return {"group": group, "names": names, "swap": swap, "token": swap[3]}

    def start_send(state, after):
        group, names = state["group"], state["names"]
        state["blocks"] = state["swap"][1]
        state["others"] = _exchange_wait(f"grad_swap_wait_{group}", _swap_copies, state["swap"], after)
        partials = [_chip_partial("chip_partial_" + n, gb, ob, core)
                    for n, gb, ob in zip(names, state["blocks"], state["others"])]
        zones = [((3,) + p.shape[1:], BF16) for p in partials]
        state["send"] = _exchange_start(f"grad_send_start_{group}", _partial_copies, partials, zones, 3)
        state["token"] = state["send"][3]

    grad, delta, new_m, new_v = {}, {}, {}, {}

    def start_join(state, after):
        group, names = state["group"], state["names"]
        received = _exchange_wait(f"grad_send_wait_{group}", _partial_copies, state["send"], after)
        halves = [_reduce_own("reduce_own_" + n, gb, ob, rb, where)
                  for n, gb, ob, rb in zip(names, state["blocks"], state["others"], received)]
        state["join"] = _join_start(f"grad_join_start_{group}", halves)
        return state["join"][2]

    def finish_reduce(state, after):
        group, names = state["group"], state["names"]
        for n, g in zip(names, _join_wait(f"grad_join_wait_{group}", state["join"], after)):
            shp = given[n].shape
            d2, m2, v2 = _adamw("adamw_" + n, _as2d(n, given[n]), g, _as2d(n, given["m_" + n]), _as2d(n, given["v_" + n]))
            grad[n], delta[n], new_m[n], new_v[n] = (_from2d(n, a, shp) for a in (g, d2, m2, v2))
        return d2

    step = _local_step(x[0], mem[0], loss_target[0], wts, token)
    pending = {}
    request = next(step)
    while True:
        try:
            kind, group, payload = request
            if kind == "weights":
                request = step.send(fetch(group, payload))
            elif kind == "grads":
                pending[group] = start_swap(group, payload)
                request = step.send(pending[group]["token"])
            else:
                start_send(pending[group], [payload])
                request = step.send(pending[group]["token"])
        except StopIteration as stop:
            loss_row, grad_x, g = stop.value
            break

    after = [pending[2]["token"], grad_x]
    for group in (0, 1):
        after = [start_join(pending[group], after)]
    for group in (0, 1):
        after = [finish_reduce(pending[group], after)]
    after = [finish_reduce(pending[2], [start_join(pending[2], after)])]

    small_names = ("a_log", "dt_bias", "gdn_norm_w", "pool_scale", "ln1_g", "ln1_b", "ln2_g", "ln2_b", "ln3_g", "ln3_b")
    pieces = [g["conv_w"]] + [g[n] for n in small_names] + [loss_row[:, :1]]
    shapes = [p.shape for p in pieces]
    summed = _unpack(_all_reduce_small("all_reduce_small", _pack(pieces), after[0]), shapes)
    gsmall = dict(zip(small_names, summed[1:-1]))
    gsmall["conv_w"] = lax.dynamic_slice(summed[0], (0, me * conv_cols), (kk, conv_cols))
    loss = summed[-1][0, 0]

    sshapes = [given[n].shape for n in SMALL]
    slabs = [_pack([given[p + n] for n in SMALL]) for p in ("", "m_", "v_")]
    gslab = _pack([gsmall[n] for n in SMALL])
    outs = _adamw("adamw_small", slabs[0], gslab, slabs[1], slabs[2])
    for dst, slab in zip((delta, new_m, new_v), outs):
        dst.update(zip(SMALL, _unpack(slab, sshapes)))
    for n in SMALL:
        grad[n] = gsmall[n].reshape(given[n].shape)

    return (loss, grad_x[None], *[grad[n] for n in ORDER], *[delta[n] for n in ORDER],
            *[new_m[n] for n in ORDER], *[new_v[n] for n in ORDER])
```

```python
import functools
import math

import jax
import jax.numpy as jnp
from jax import lax
from jax.experimental import pallas as pl
from jax.experimental.pallas import tpu as pltpu

F32 = jnp.float32
BF16 = jnp.bfloat16
MESH = pl.DeviceIdType.MESH

HEAD_DIM = 128
CHUNK = 64
POOL_WINDOWS = (2, 4, 8, 16)
XATTN_HEADS = 4
ALPHA = 2.0 ** 0.25
LN_EPS = 1e-5
NORM_EPS = 1e-6
ADAM_LR, ADAM_B1, ADAM_B2, ADAM_EPS, ADAM_WD, ADAM_STEP = 0.001, 0.9, 0.999, 1e-08, 0.01, 10
N_SHARD = 4
VMEM_LIMIT = 56 * 1024 * 1024
K_STEPS = (2048, 1024, 512, 256, 128)


def _params(*sem):
    return pltpu.CompilerParams(dimension_semantics=sem, vmem_limit_bytes=VMEM_LIMIT)


def _bdot(a, b, ta=False, tb=False):
    dims = (((0 if ta else 1,), (1 if tb else 0,)), ((), ()))
    return lax.dot_general(a.astype(BF16), b.astype(BF16), dims, preferred_element_type=F32)


def _sigmoid(x):
    return 1.0 / (1.0 + jnp.exp(-x))


def _matmul(name, a, b, *, ta=False, tb=False, tm, tn, tk, extra=(), outs, epilogue, b_blocks=None,
            sequential=False, n_used=None, k_used=None, n_outer=False):
    m, k_dim = (a.shape[1], a.shape[0]) if ta else a.shape
    if b_blocks and tb:
        n = b.shape[1]
        k_dim = b.shape[0] * b.shape[2]
        per = b.shape[2] // tk
        b_spec = pl.BlockSpec((None, tn, tk), lambda i, j, k: (k // per, j, k % per))
    elif b_blocks:
        n = b.shape[0] * b.shape[2]
        per = b.shape[2] // tn
        b_spec = pl.BlockSpec((None, tk, tn), lambda i, j, k: (j // per, k, j % per))
    elif tb:
        n = b.shape[0]
        b_spec = pl.BlockSpec((tn, tk), lambda i, j, k: (j, k))
    else:
        n = b.shape[1]
        b_spec = pl.BlockSpec((tk, tn), lambda i, j, k: (k, j))
    n, k_dim = n_used or n, k_used or k_dim
    assert m % tm == 0 and n % tn == 0 and k_dim % tk == 0, (name, m, n, k_dim, tm, tn, tk)
    nk = k_dim // tk
    a_spec = pl.BlockSpec((tk, tm), lambda i, j, k: (k, i)) if ta else pl.BlockSpec((tm, tk), lambda i, j, k: (i, k))
    n_extra, n_out = len(extra), len(outs)

    def wrap(index_map):
        return lambda i, j, k: index_map(i, j)

    def spec(block, index_map):
        if n_outer:
            return pl.BlockSpec(block, lambda j, i, k: index_map(i, j, k))
        return pl.BlockSpec(block, index_map)

    row_axis = 1 if n_outer else 0

    def body_one_step(*refs):
        ex = refs[2:2 + n_extra]
        out = refs[2 + n_extra:2 + n_extra + n_out]
        epilogue(_bdot(refs[0][...], refs[1][...], ta, tb), ex, out, pl.program_id(row_axis))

    def body(*refs):
        a_ref, b_ref = refs[0], refs[1]
        ex = refs[2:2 + n_extra]
        out = refs[2 + n_extra:2 + n_extra + n_out]
        acc = refs[-1]
        i, k = pl.program_id(row_axis), pl.program_id(2)
        part = _bdot(a_ref[...], b_ref[...], ta, tb)

        @pl.when(k == 0)
        def _():
            acc[...] = part

        @pl.when(jnp.logical_and(k > 0, k < nk - 1))
        def _():
            acc[...] += part

        @pl.when(k == nk - 1)
        def _():
            epilogue(acc[...] + part, ex, out, i)

    sem = ("arbitrary",) * 3 if sequential else ("parallel", "parallel", "arbitrary")
    res = pl.pallas_call(
        body_one_step if nk == 1 else body, name=name,
        grid=(n // tn, m // tm, nk) if n_outer else (m // tm, n // tn, nk),
        in_specs=[spec(a_spec.block_shape, a_spec.index_map), spec(b_spec.block_shape, b_spec.index_map)]
        + [spec(bs, wrap(im)) for _, bs, im in extra],
        out_specs=[spec(bs, wrap(im)) for _, bs, im in outs],
        out_shape=[s for s, _, _ in outs],
        scratch_shapes=[] if nk == 1 else [pltpu.VMEM((tm, tn), F32)],
        compiler_params=_params(*sem),
    )(a, b, *[x for x, _, _ in extra])
    return res


def _tile(i, j):
    return (i, j)


def _plain(name, a, b, *, ta=False, tb=False, tm, tn, tk, out_dtype, b_blocks=None, out3=None, n_used=None,
           n_outer=False):
    m = a.shape[1] if ta else a.shape[0]
    if b_blocks:
        n = b.shape[1] if tb else b.shape[0] * b.shape[2]
    else:
        n = n_used or (b.shape[0] if tb else b.shape[1])

    def epi(acc, ex, out, i):
        out[0][...] = acc.astype(out_dtype)

    if out3:
        per = (n // out3) // tn
        spec = (jax.ShapeDtypeStruct((out3, m, n // out3), out_dtype), (None, tm, tn),
                lambda i, j: (j // per, i, j % per))
    else:
        spec = (jax.ShapeDtypeStruct((m, n), out_dtype), (tm, tn), _tile)
    return _matmul(name, a, b, ta=ta, tb=tb, tm=tm, tn=tn, tk=tk, outs=[spec], epilogue=epi,
                   b_blocks=b_blocks, n_used=n_used, n_outer=n_outer)[0]


def _ln_forward(name, a, b, res, gamma, beta, *, tm, tk, want_h=True):
    m, n = res.shape

    def epi(acc, ex, out, i):
        u = ALPHA * ex[0][...] + acc
        mu = jnp.mean(u, axis=-1, keepdims=True)
        xc = u - mu
        var = jnp.mean(xc * xc, axis=-1, keepdims=True)
        rstd = lax.rsqrt(var + LN_EPS)
        xhat = xc * rstd
        out[-2][...] = xhat
        out[-1][...] = rstd
        if want_h:
            h = xhat * ex[1][...] + ex[2][...]
            out[0][...] = h
            out[1][...] = h.astype(BF16)

    row = lambda i, j: (i, 0)
    vec = lambda i, j: (0, 0)
    outs = [(jax.ShapeDtypeStruct((m, n), F32), (tm, n), row), (jax.ShapeDtypeStruct((m, n), BF16), (tm, n), row),
            (jax.ShapeDtypeStruct((m, n), F32), (tm, n), row), (jax.ShapeDtypeStruct((m, 1), F32), (tm, 1), row)]
    return _matmul(
        name, a, b, tm=tm, tn=n, tk=tk,
        extra=[(res, (tm, n), row), (gamma, (1, n), vec), (beta, (1, n), vec)],
        outs=outs if want_h else outs[2:], epilogue=epi)


def _ln_backward_math(dy, xhat, rstd, gamma):
    dxhat = dy * gamma
    m1 = jnp.mean(dxhat, axis=-1, keepdims=True)
    m2 = jnp.mean(dxhat * xhat, axis=-1, keepdims=True)
    du = rstd * (dxhat - m1 - xhat * m2)
    return du, jnp.sum(dy * xhat, axis=0, keepdims=True), jnp.sum(dy, axis=0, keepdims=True)


def _ln_backward(name, a, b, dres, xhat, rstd, gamma, *, tm, tk, b_blocks=None, tb=True):
    m, n = dres.shape

    def epi(acc, ex, out, i):
        dy = acc + ALPHA * ex[0][...]
        du, dg, db = _ln_backward_math(dy, ex[1][...], ex[2][...], ex[3][...])
        out[0][...] = du
        out[1][...] = du.astype(BF16)
        first = i == 0

        @pl.when(first)
        def _():
            out[2][...] = dg
            out[3][...] = db

        @pl.when(jnp.logical_not(first))
        def _():
            out[2][...] += dg
            out[3][...] += db

    row = lambda i, j: (i, 0)
    vec = lambda i, j: (0, 0)
    return _matmul(
        name, a, b, tb=tb, tm=tm, tn=n, tk=tk, b_blocks=b_blocks, sequential=True,
        extra=[(dres, (tm, n), row), (xhat, (tm, n), row), (rstd, (tm, 1), row), (gamma, (1, n), vec)],
        outs=[(jax.ShapeDtypeStruct((m, n), F32), (tm, n), row),
              (jax.ShapeDtypeStruct((m, n), BF16), (tm, n), row),
              (jax.ShapeDtypeStruct((1, n), F32), (1, n), vec),
              (jax.ShapeDtypeStruct((1, n), F32), (1, n), vec)],
        epilogue=epi)


def _shift_down(x, k):
    row = lax.broadcasted_iota(jnp.int32, x.shape, 0)
    return jnp.where(row >= k, pltpu.roll(x, k, axis=0), 0.0)


def _shift_up(x, k):
    t = x.shape[0]
    row = lax.broadcasted_iota(jnp.int32, x.shape, 0)
    return jnp.where(row < t - k, pltpu.roll(x, t - k, axis=0), 0.0)


def _conv_silu_norm(x, w, normalise):
    kk = w.shape[0]
    c = x * w[kk - 1:kk, :]
    for j in range(kk - 1):
        c = c + _shift_down(x, kk - 1 - j) * w[j:j + 1, :]
    sg = _sigmoid(c)
    s = c * sg
    r = lax.rsqrt(jnp.sum(s * s, axis=-1, keepdims=True) + NORM_EPS)
    y = jnp.where(normalise, s * r, s)
    return c, sg, s, r, y


def _gdn_pre(proj, conv_w, heads):
    t = proj.shape[0]
    kk = conv_w.shape[0]

    def body(x_ref, w_ref, o_ref):
        normalise = pl.program_id(0) < 2
        o_ref[...] = _conv_silu_norm(x_ref[...], w_ref[...], normalise)[4]

    col = lambda s, h: (0, s * heads + h)
    return pl.pallas_call(
        body, name="gdn_pre", grid=(3, heads),
        in_specs=[pl.BlockSpec((t, HEAD_DIM), col), pl.BlockSpec((kk, HEAD_DIM), col)],
        out_specs=pl.BlockSpec((t, HEAD_DIM), col),
        out_shape=jax.ShapeDtypeStruct((t, 3 * heads * HEAD_DIM), F32),
        compiler_params=_params("parallel", "parallel"),
    )(proj, conv_w)


def _gdn_pre_backward(proj, conv_w, dqkv, heads):
    t = proj.shape[0]
    kk = conv_w.shape[0]

    def body(x_ref, w_ref, dy_ref, dx_ref, dw_ref):
        normalise = pl.program_id(0) < 2
        x = x_ref[...]
        w = w_ref[...]
        dy = dy_ref[...]
        c, sg, s, r, y = _conv_silu_norm(x, w, normalise)
        ds_norm = r * (dy - y * jnp.sum(dy * y, axis=-1, keepdims=True))
        ds = jnp.where(normalise, ds_norm, dy)
        dc = ds * (sg * (1.0 + c * (1.0 - sg)))
        dx = dc * w[kk - 1:kk, :]
        rows = [None] * kk
        rows[kk - 1] = jnp.sum(dc * x, axis=0, keepdims=True)
        for j in range(kk - 1):
            lag = kk - 1 - j
            dx = dx + _shift_up(dc, lag) * w[j:j + 1, :]
            rows[j] = jnp.sum(dc * _shift_down(x, lag), axis=0, keepdims=True)
        dx_ref[...] = dx.astype(BF16)
        dw_ref[...] = jnp.concatenate(rows, axis=0)

    col = lambda s, h: (0, s * heads + h)
    return pl.pallas_call(
        body, name="gdn_pre_bwd", grid=(3, heads),
        in_specs=[pl.BlockSpec((t, HEAD_DIM), col), pl.BlockSpec((kk, HEAD_DIM), col),
                  pl.BlockSpec((t, HEAD_DIM), col)],
        out_specs=[pl.BlockSpec((t, HEAD_DIM), col), pl.BlockSpec((kk, HEAD_DIM), col)],
        out_shape=[jax.ShapeDtypeStruct((t, 3 * heads * HEAD_DIM), BF16),
                   jax.ShapeDtypeStruct((kk, 3 * heads * HEAD_DIM), F32)],
        compiler_params=_params("parallel", "parallel"),
    )(proj, conv_w, dqkv)


def _gate_vectors(a_log, dt_bias, heads):
    pad = lambda v: jnp.pad(v.astype(F32), ((0, 0), (heads, HEAD_DIM - 2 * heads)))
    return pad(jnp.exp(a_log.astype(F32))), pad(dt_bias)


def _softplus(x):
    return jnp.maximum(x, 0.0) + jnp.log(1.0 + jnp.exp(-jnp.abs(x)))


def _gates_epilogue(heads):
    def epi(acc, ex, out, i):
        lane = lax.broadcasted_iota(jnp.int32, acc.shape, 1)
        beta = _sigmoid(acc)
        g = -ex[0][...] * _softplus(acc + ex[1][...])
        out[0][...] = acc
        out[1][...] = jnp.where(lane < heads, beta, jnp.where(lane < 2 * heads, g, 0.0))
    return epi


def _gates_backward(ba, bg, dbg, ea, dtb, heads):
    t = ba.shape[0]

    def body(ba_ref, bg_ref, d_ref, ea_ref, dt_ref, dba_ref, dal_ref, ddt_ref):
        lane = lax.broadcasted_iota(jnp.int32, (t, HEAD_DIM), 1)
        bgv = bg_ref[...]
        d = d_ref[...]
        db = d * bgv * (1.0 - bgv)
        da = -d * ea_ref[...] * _sigmoid(ba_ref[...] + dt_ref[...])
        is_g = jnp.logical_and(lane >= heads, lane < 2 * heads)
        dba = jnp.where(lane < heads, db, jnp.where(is_g, da, 0.0))
        dba_ref[...] = dba.astype(BF16)
        dal_ref[...] = jnp.sum(jnp.where(is_g, d * bgv, 0.0), axis=0, keepdims=True)
        ddt_ref[...] = jnp.sum(jnp.where(is_g, da, 0.0), axis=0, keepdims=True)

    full = pl.BlockSpec((t, HEAD_DIM), lambda: (0, 0))
    vec = pl.BlockSpec((1, HEAD_DIM), lambda: (0, 0))
    return pl.pallas_call(
        body, name="gates_bwd", grid=(),
        in_specs=[full, full, full, vec, vec], out_specs=[full, vec, vec],
        out_shape=[jax.ShapeDtypeStruct((t, HEAD_DIM), BF16), jax.ShapeDtypeStruct((1, HEAD_DIM), F32),
                   jax.ShapeDtypeStruct((1, HEAD_DIM), F32)],
        compiler_params=pltpu.CompilerParams(vmem_limit_bytes=VMEM_LIMIT),
    )(ba, bg, dbg, ea, dtb)


class _Chunk:
    pass


def _split2(x):
    hi = x.astype(BF16)
    return hi, (x - hi.astype(F32)).astype(BF16)


def _split3(x):
    hi = x.astype(BF16)
    rest = x - hi.astype(F32)
    mid = rest.astype(BF16)
    return hi, mid, (rest - mid.astype(F32)).astype(BF16)


def _dot_mask(mask, x, ta=False):
    hi, mid, lo = _split3(x)
    return _bdot(mask, hi, ta=ta) + (_bdot(mask, mid, ta=ta) + _bdot(mask, lo, ta=ta))


def _transpose_by_identity(x):
    r = x.shape[0]
    eye = (lax.broadcasted_iota(jnp.int32, (r, r), 0) == lax.broadcasted_iota(jnp.int32, (r, r), 1)).astype(BF16)
    hi, mid, lo = _split3(x)
    return _bdot(hi, eye, ta=True) + (_bdot(mid, eye, ta=True) + _bdot(lo, eye, ta=True))


def _dot22(a, b, ta=False, tb=False):
    ah, al = _split2(a)
    bh, bl = _split2(b)
    return _bdot(ah, bh, ta, tb) + (_bdot(ah, bl, ta, tb) + _bdot(al, bh, ta, tb))


def _chunk_gates(bg, heads):
    n = CHUNK
    row = lax.broadcasted_iota(jnp.int32, (n, n), 0)
    col = lax.broadcasted_iota(jnp.int32, (n, n), 1)
    lane = lax.broadcasted_iota(jnp.int32, bg.shape, 1)
    graw = jnp.where(jnp.logical_and(lane >= heads, lane < 2 * heads), bg, 0.0)
    gc = _dot_mask((row >= col).astype(BF16), graw)
    return gc, _transpose_by_identity(gc)


def _in_lockstep(generators):
    results = [None] * len(generators)
    live = list(enumerate(generators))
    while live:
        still = []
        for i, gen in live:
            try:
                next(gen)
                still.append((i, gen))
            except StopIteration as stop:
                results[i] = stop.value
        live = still
    return results


def _chunk_local(q, k, v, beta, gc, grow, solved=None):
    c = _Chunk()
    n = CHUNK
    row = lax.broadcasted_iota(jnp.int32, (n, n), 0)
    col = lax.broadcasted_iota(jnp.int32, (n, n), 1)
    c.tri = row >= col
    c.strict = row > col
    eye = row == col
    c.gcb = jnp.broadcast_to(gc, (n, HEAD_DIM))
    c.decay = jnp.where(c.tri, jnp.exp(jnp.where(c.tri, gc - grow, 0.0)), 0.0)
    c.eg = jnp.exp(c.gcb)
    glast = c.gcb[n - 1:n, :]
    c.egl = jnp.exp(glast)
    c.ekl = jnp.exp(glast - c.gcb)
    c.beta = beta
    c.q = q * (HEAD_DIM ** -0.5)
    c.k = k
    c.v = v
    c.kb = k * beta
    c.vb = v * beta
    c.kg = c.kb * c.eg
    both = _bdot(jnp.concatenate([c.kb, c.q], axis=0), k, tb=True)
    yield
    c.L = jnp.where(c.strict, both[:n] * c.decay, 0.0)
    c.A = jnp.where(c.tri, both[n:] * c.decay, 0.0)
    if solved is None:
        x = -c.L
        tinv = eye.astype(F32) + x
        p = _dot22(x, x)
        yield
        for _ in range(int(math.log2(n)) - 2):
            both = _dot22(jnp.concatenate([p, tinv], axis=0), p)
            yield
            p, tinv = both[:n], tinv + both[n:]
        c.T = tinv + _dot22(tinv, p)
        yield
        uw = _dot22(c.T, jnp.concatenate([c.vb, c.kg], axis=1))
        yield
        c.u, c.w = uw[:, :HEAD_DIM], uw[:, HEAD_DIM:]
    else:
        c.T, c.u, c.w = solved
    c.qg = c.q * c.eg
    c.kdec = k * c.ekl
    return c


def _gdn_core(qkv, bg, heads):
    t = qkv.shape[0]
    nchunk = t // CHUNK

    gw = heads * HEAD_DIM

    def body(qkv_ref, bg_ref, o_ref, s_ref, t_ref, u_ref, w_ref, state):
        @pl.when(pl.program_id(0) == 0)
        def _():
            state[...] = jnp.zeros_like(state)

        bg_v = bg_ref[...]
        gc_all, gc_rows = _chunk_gates(bg_v, heads)
        def one_head(h):
            col = lambda s: pl.ds(s * gw + h * HEAD_DIM, HEAD_DIM)
            c = yield from _chunk_local(qkv_ref[:, col(0)], qkv_ref[:, col(1)], qkv_ref[:, col(2)], bg_v[:, h:h + 1],
                                        gc_all[:, heads + h:heads + h + 1], gc_rows[heads + h:heads + h + 1, :])
            s0 = state[h]
            v_new = c.u - _bdot(c.w, s0)
            yield
            o = _bdot(c.qg, s0) + _bdot(c.A, v_new)
            return s0, o, s0 * c.egl + _bdot(c.kdec, v_new, ta=True), c

        results = _in_lockstep([one_head(h) for h in range(heads)])
        for h, (s0, o, s1, c) in enumerate(results):
            lanes = pl.ds(h * HEAD_DIM, HEAD_DIM)
            s_ref[h, 0] = s0
            o_ref[:, lanes] = o
            t_ref[:, lanes] = jnp.concatenate([c.T, jnp.zeros((CHUNK, HEAD_DIM - CHUNK), F32)], axis=1)
            u_ref[:, lanes] = c.u
            w_ref[:, lanes] = c.w
            state[h] = s1

    return pl.pallas_call(
        body, name="gdn_core", grid=(nchunk,),
        in_specs=[pl.BlockSpec((CHUNK, 3 * gw), lambda n: (n, 0)), pl.BlockSpec((CHUNK, HEAD_DIM), lambda n: (n, 0))],
        out_specs=[pl.BlockSpec((CHUNK, gw), lambda n: (n, 0)),
                   pl.BlockSpec((heads, 1, HEAD_DIM, HEAD_DIM), lambda n: (0, n, 0, 0))]
        + [pl.BlockSpec((CHUNK, gw), lambda n: (n, 0))] * 3,
        out_shape=[jax.ShapeDtypeStruct((t, gw), F32),
                   jax.ShapeDtypeStruct((heads, nchunk, HEAD_DIM, HEAD_DIM), F32)]
        + [jax.ShapeDtypeStruct((t, gw), F32)] * 3,
        scratch_shapes=[pltpu.VMEM((heads, HEAD_DIM, HEAD_DIM), F32)],
        compiler_params=_params("arbitrary"),
    )(qkv, bg)


def _gdn_core_backward(qkv, bg, states, solved, do, heads):
    t = qkv.shape[0]
    nchunk = t // CHUNK
    n = CHUNK

    def one_head(chunk_local, s0, d_out, ds1):
        c = yield from chunk_local
        v_new = c.u - _bdot(c.w, s0)
        dqg = _bdot(d_out, s0, tb=True)
        ds0 = _bdot(c.qg, d_out, ta=True) + ds1 * c.egl
        dv_new = _bdot(c.A, d_out, ta=True) + _bdot(c.kdec, ds1)
        yield
        dA = jnp.where(c.tri, _bdot(d_out, v_new, tb=True), 0.0)
        dkdec = _bdot(v_new, ds1, tb=True)
        dgl = jnp.sum(jnp.sum(ds1 * s0, axis=1, keepdims=True), axis=0, keepdims=True) * c.egl
        dw = -_bdot(dv_new, s0, tb=True)
        ds0 = ds0 - _bdot(c.w, dv_new, ta=True)
        yield
        both = _dot22(c.T, jnp.concatenate([dv_new, dw], axis=1), ta=True)
        yield
        dvb, dkg = both[:, :HEAD_DIM], both[:, HEAD_DIM:]
        dL = jnp.where(c.strict, -(_bdot(dvb, c.u, tb=True) + _bdot(dkg, c.w, tb=True)), 0.0)
        yield
        dm1 = dL * c.decay
        dkb = _bdot(dm1, c.k) + dkg * c.eg
        dk = _bdot(dm1, c.kb, ta=True)
        dm2 = dA * c.decay
        dq = _bdot(dm2, c.k) + dqg * c.eg
        dk = dk + _bdot(dm2, c.q, ta=True) + dkdec * c.ekl + dkb * c.beta
        pm = dL * c.L + dA * c.A
        ones = jnp.ones((n, HEAD_DIM), BF16)
        pm_hi, pm_lo = _split2(pm)
        colsum = _bdot(pm_hi, ones, ta=True) + _bdot(pm_lo, ones, ta=True)
        tk_ = jnp.sum(dkdec * c.kdec, axis=1, keepdims=True)
        dgc = (jnp.sum(pm, axis=1, keepdims=True) - colsum
               + jnp.sum(dqg * c.qg, axis=1, keepdims=True)
               - tk_
               + jnp.sum(dkg * c.kg, axis=1, keepdims=True))
        dgl = dgl + jnp.sum(tk_, axis=0, keepdims=True)
        rowi = lax.broadcasted_iota(jnp.int32, (n, HEAD_DIM), 0)
        dgc = dgc + jnp.where(rowi == n - 1, dgl, 0.0)
        dbeta = jnp.sum(dkb * c.k, axis=1, keepdims=True) + jnp.sum(dvb * c.v, axis=1, keepdims=True)
        return dq * (HEAD_DIM ** -0.5), dk, dvb * c.beta, dbeta, dgc, ds0

    gw = heads * HEAD_DIM

    def body(qkv_ref, bg_ref, s_ref, t_ref, u_ref, w_ref, do_ref, dqkv_ref, dbg_ref, dstate):
        @pl.when(pl.program_id(0) == 0)
        def _():
            dstate[...] = jnp.zeros_like(dstate)

        bg_v = bg_ref[...]
        gc_all, gc_rows = _chunk_gates(bg_v, heads)
        lane = lax.broadcasted_iota(jnp.int32, (n, HEAD_DIM), 1)
        dgates = jnp.zeros((n, HEAD_DIM), F32)
        chains = []
        for h in range(heads):
            col = lambda s: pl.ds(s * gw + h * HEAD_DIM, HEAD_DIM)
            lanes = pl.ds(h * HEAD_DIM, HEAD_DIM)
            c = _chunk_local(qkv_ref[:, col(0)], qkv_ref[:, col(1)], qkv_ref[:, col(2)], bg_v[:, h:h + 1],
                             gc_all[:, heads + h:heads + h + 1], gc_rows[heads + h:heads + h + 1, :],
                             (t_ref[:, pl.ds(h * HEAD_DIM, CHUNK)], u_ref[:, lanes], w_ref[:, lanes]))
            chains.append(one_head(c, s_ref[h, 0], do_ref[:, pl.ds(h * HEAD_DIM, HEAD_DIM)], dstate[h]))
        results = _in_lockstep(chains)
        for h, (dq, dk, dv, dbeta, dgc, ds0) in enumerate(results):
            dgates = jnp.where(lane == h, dbeta, jnp.where(lane == heads + h, dgc, dgates))
        for h, (dq, dk, dv, dbeta, dgc, ds0) in enumerate(results):
            dqkv_ref[:, pl.ds(h * HEAD_DIM, HEAD_DIM)] = dq
            dqkv_ref[:, pl.ds(gw + h * HEAD_DIM, HEAD_DIM)] = dk
            dqkv_ref[:, pl.ds(2 * gw + h * HEAD_DIM, HEAD_DIM)] = dv
            dstate[h] = ds0
        row = lax.broadcasted_iota(jnp.int32, (n, n), 0)
        colm = lax.broadcasted_iota(jnp.int32, (n, n), 1)
        draw = _dot_mask((row >= colm).astype(BF16), dgates, ta=True)
        dbg_ref[...] = jnp.where(lane < heads, dgates, draw)

    last = nchunk - 1
    return pl.pallas_call(
        body, name="gdn_core_bwd", grid=(nchunk,),
        in_specs=[pl.BlockSpec((CHUNK, 3 * gw), lambda i: (last - i, 0)),
                  pl.BlockSpec((CHUNK, HEAD_DIM), lambda i: (last - i, 0)),
                  pl.BlockSpec((heads, 1, HEAD_DIM, HEAD_DIM), lambda i: (0, last - i, 0, 0))]
        + [pl.BlockSpec((CHUNK, gw), lambda i: (last - i, 0))] * 4,
        out_specs=[pl.BlockSpec((CHUNK, 3 * gw), lambda i: (last - i, 0)),
                   pl.BlockSpec((CHUNK, HEAD_DIM), lambda i: (last - i, 0))],
        out_shape=[jax.ShapeDtypeStruct((t, 3 * gw), F32), jax.ShapeDtypeStruct((t, HEAD_DIM), F32)],
        scratch_shapes=[pltpu.VMEM((heads, HEAD_DIM, HEAD_DIM), F32)],
        compiler_params=_params("arbitrary"),
    )(qkv, bg, states, *solved, do)


def _gdn_post(o, proj, z_col0, norm_w, heads, tt):
    t = o.shape[0]
    zb = z_col0 // HEAD_DIM

    def body(o_ref, z_ref, w_ref, out_ref):
        ov = o_ref[...]
        z = z_ref[...]
        rms = lax.rsqrt(jnp.mean(ov * ov, axis=-1, keepdims=True) + NORM_EPS)
        out_ref[...] = (ov * rms * w_ref[...] * (z * _sigmoid(z))).astype(BF16)

    return pl.pallas_call(
        body, name="gdn_post", grid=(t // tt, heads),
        in_specs=[pl.BlockSpec((tt, HEAD_DIM), lambda i, h: (i, h)),
                  pl.BlockSpec((tt, HEAD_DIM), lambda i, h: (i, zb + h)),
                  pl.BlockSpec((1, HEAD_DIM), lambda i, h: (0, 0))],
        out_specs=pl.BlockSpec((tt, HEAD_DIM), lambda i, h: (i, h)),
        out_shape=jax.ShapeDtypeStruct((t, heads * HEAD_DIM), BF16),
        compiler_params=_params("parallel", "parallel"),
    )(o, proj, norm_w)


def _gdn_post_backward(dcat, o, proj, z_col0, norm_w, heads, tt):
    t = o.shape[0]
    zb = z_col0 // HEAD_DIM

    def body(d_ref, o_ref, z_ref, w_ref, do_ref, dz_ref, dw_ref):
        d = d_ref[...]
        ov = o_ref[...]
        z = z_ref[...]
        w = w_ref[...]
        rms = lax.rsqrt(jnp.mean(ov * ov, axis=-1, keepdims=True) + NORM_EPS)
        ohat = ov * rms
        sg = _sigmoid(z)
        gate = z * sg
        dz_ref[...] = (d * ohat * w * (sg * (1.0 + z * (1.0 - sg)))).astype(BF16)
        don = d * gate
        dohat = don * w
        do_ref[...] = rms * (dohat - ohat * jnp.mean(dohat * ohat, axis=-1, keepdims=True))
        dw = jnp.sum(don * ohat, axis=0, keepdims=True)
        first = jnp.logical_and(pl.program_id(0) == 0, pl.program_id(1) == 0)

        @pl.when(first)
        def _():
            dw_ref[...] = dw

        @pl.when(jnp.logical_not(first))
        def _():
            dw_ref[...] += dw

    blk = pl.BlockSpec((tt, HEAD_DIM), lambda i, h: (i, h))
    return pl.pallas_call(
        body, name="gdn_post_bwd", grid=(t // tt, heads),
        in_specs=[blk, blk, pl.BlockSpec((tt, HEAD_DIM), lambda i, h: (i, zb + h)),
                  pl.BlockSpec((1, HEAD_DIM), lambda i, h: (0, 0))],
        out_specs=[blk, blk, pl.BlockSpec((1, HEAD_DIM), lambda i, h: (0, 0))],
        out_shape=[jax.ShapeDtypeStruct((t, heads * HEAD_DIM), F32),
                   jax.ShapeDtypeStruct((t, heads * HEAD_DIM), BF16),
                   jax.ShapeDtypeStruct((1, HEAD_DIM), F32)],
        compiler_params=_params("arbitrary", "arbitrary"),
    )(dcat, o, proj, norm_w)


def _pool_select(levels, group):
    out = levels[-1]
    for gi in range(len(levels) - 2, -1, -1):
        out = jnp.where(group == gi, levels[gi], out)
    return out


def _pool_counts(t, width, group):
    pos = lax.broadcasted_iota(jnp.int32, (t, width), 0)
    win = jnp.left_shift(2, group)
    return jnp.minimum(pos + 1, win).astype(F32)


def _pooled(p, group):
    levels, s, step = [], p, 1
    for _ in POOL_WINDOWS:
        s = s + _shift_down(s, step)
        levels.append(s)
        step *= 2
    cnt = _pool_counts(p.shape[0], p.shape[1], group)
    return _pool_select(levels, group) / cnt - p, cnt


def _pool_forward(proj, p_col0, pool_w, pool_scale):
    t = proj.shape[0]
    groups, cg, _ = pool_w.shape
    pb = p_col0 // cg

    def body(p_ref, w_ref, s_ref, o_ref):
        pooled, _ = _pooled(p_ref[...], pl.program_id(0))
        o_ref[...] = (_bdot(pooled, w_ref[0]) * s_ref[...]).astype(BF16)

    return pl.pallas_call(
        body, name="pool_fwd", grid=(groups,),
        in_specs=[pl.BlockSpec((t, cg), lambda g: (0, pb + g)), pl.BlockSpec((1, cg, cg), lambda g: (g, 0, 0)),
                  pl.BlockSpec((1, cg), lambda g: (0, g))],
        out_specs=pl.BlockSpec((t, cg), lambda g: (0, g)),
        out_shape=jax.ShapeDtypeStruct((t, groups * cg), BF16),
        compiler_params=_params("parallel"),
    )(proj, pool_w, pool_scale)


def _pool_backward(dcat, d_col0, proj, p_col0, pool_w, pool_scale):
    t = proj.shape[0]
    groups, cg, _ = pool_w.shape
    pb = p_col0 // cg
    db = d_col0 // cg

    def body(d_ref, p_ref, w_ref, s_ref, dp_ref, dw_ref, ds_ref):
        group = pl.program_id(0)
        pooled, cnt = _pooled(p_ref[...], group)
        w = w_ref[0]
        d = d_ref[...]
        mixed = _bdot(pooled, w)
        ds_ref[...] = jnp.sum(d * mixed, axis=0, keepdims=True)
        dmixed = d * s_ref[...]
        dw_ref[0] = _bdot(pooled, dmixed, ta=True)
        dpooled = _bdot(dmixed, w, tb=True)
        levels, s, step = [], dpooled / cnt, 1
        for _ in POOL_WINDOWS:
            s = s + _shift_up(s, step)
            levels.append(s)
            step *= 2
        dp_ref[...] = (_pool_select(levels, group) - dpooled).astype(BF16)

    return pl.pallas_call(
        body, name="pool_bwd", grid=(groups,),
        in_specs=[pl.BlockSpec((t, cg), lambda g: (0, db + g)), pl.BlockSpec((t, cg), lambda g: (0, pb + g)),
                  pl.BlockSpec((1, cg, cg), lambda g: (g, 0, 0)), pl.BlockSpec((1, cg), lambda g: (0, g))],
        out_specs=[pl.BlockSpec((t, cg), lambda g: (0, g)), pl.BlockSpec((1, cg, cg), lambda g: (g, 0, 0)),
                   pl.BlockSpec((1, cg), lambda g: (0, g))],
        out_shape=[jax.ShapeDtypeStruct((t, groups * cg), BF16), jax.ShapeDtypeStruct((groups, cg, cg), F32),
                   jax.ShapeDtypeStruct((1, groups * cg), F32)],
        compiler_params=_params("parallel"),
    )(dcat, proj, pool_w, pool_scale)


def _attention(q, k, v, tq):
    t, d = q.shape
    m = k.shape[0]
    dh = d // XATTN_HEADS
    scale = dh ** -0.5

    def body(q_ref, k_ref, v_ref, o_ref):
        s = _bdot(q_ref[...], k_ref[...], tb=True) * scale
        s = s - jnp.max(s, axis=-1, keepdims=True)
        e = jnp.exp(s)
        p = e / jnp.sum(e, axis=-1, keepdims=True)
        o_ref[...] = _bdot(p, v_ref[...]).astype(BF16)

    return pl.pallas_call(
        body, name="xattn_fwd", grid=(XATTN_HEADS, t // tq),
        in_specs=[pl.BlockSpec((tq, dh), lambda h, i: (i, h)), pl.BlockSpec((m, dh), lambda h, i: (0, h)),
                  pl.BlockSpec((m, dh), lambda h, i: (0, h))],
        out_specs=pl.BlockSpec((tq, dh), lambda h, i: (i, h)),
        out_shape=jax.ShapeDtypeStruct((t, d), BF16),
        compiler_params=_params("parallel", "parallel"),
    )(q, k, v)


def _attention_backward(q, k, v, do, tq):
    t, d = q.shape
    m = k.shape[0]
    dh = d // XATTN_HEADS
    scale = dh ** -0.5

    def body(q_ref, k_ref, v_ref, do_ref, dq_ref, dk_ref, dv_ref, dk_acc, dv_acc):
        i = pl.program_id(1)
        qv, kv, vv, dov = q_ref[...], k_ref[...], v_ref[...], do_ref[...]
        s = _bdot(qv, kv, tb=True) * scale
        s = s - jnp.max(s, axis=-1, keepdims=True)
        e = jnp.exp(s)
        p = e / jnp.sum(e, axis=-1, keepdims=True)
        dp = _bdot(dov, vv, tb=True)
        ds = p * (dp - jnp.sum(dp * p, axis=-1, keepdims=True)) * scale
        dq_ref[...] = _bdot(ds, kv).astype(BF16)
        dv_part = _bdot(p, dov, ta=True)
        dk_part = _bdot(ds, qv, ta=True)

        @pl.when(i == 0)
        def _():
            dk_acc[...] = dk_part
            dv_acc[...] = dv_part

        @pl.when(i > 0)
        def _():
            dk_acc[...] += dk_part
            dv_acc[...] += dv_part

        @pl.when(i == pl.num_programs(1) - 1)
        def _():
            dk_ref[...] = dk_acc[...].astype(BF16)
            dv_ref[...] = dv_acc[...].astype(BF16)

    qblk = pl.BlockSpec((tq, dh), lambda h, i: (i, h))
    kblk = pl.BlockSpec((m, dh), lambda h, i: (0, h))
    return pl.pallas_call(
        body, name="xattn_bwd", grid=(XATTN_HEADS, t // tq),
        in_specs=[qblk, kblk, kblk, qblk],
        out_specs=[qblk, kblk, kblk],
        out_shape=[jax.ShapeDtypeStruct((t, d), BF16), jax.ShapeDtypeStruct((m, d), BF16),
                   jax.ShapeDtypeStruct((m, d), BF16)],
        scratch_shapes=[pltpu.VMEM((m, dh), F32), pltpu.VMEM((m, dh), F32)],
        compiler_params=_params("parallel", "arbitrary"),
    )(q, k, v, do)


def _ln_backward_rows(name, dmain, dres, xhat, rstd, gamma, tm):
    t, d = xhat.shape

    def body(m_ref, r_ref, x_ref, s_ref, g_ref, du_ref, dub_ref, dg_ref, db_ref):
        du, dg, db = _ln_backward_math(m_ref[...] + ALPHA * r_ref[...], x_ref[...], s_ref[...], g_ref[...])
        du_ref[...] = du
        dub_ref[...] = du.astype(BF16)
        first = pl.program_id(0) == 0

        @pl.when(first)
        def _():
            dg_ref[...] = dg
            db_ref[...] = db

        @pl.when(jnp.logical_not(first))
        def _():
            dg_ref[...] += dg
            db_ref[...] += db

    row = pl.BlockSpec((tm, d), lambda i: (i, 0))
    vec = pl.BlockSpec((1, d), lambda i: (0, 0))
    return pl.pallas_call(
        body, name=name, grid=(t // tm,),
        in_specs=[row, row, row, pl.BlockSpec((tm, 1), lambda i: (i, 0)), vec],
        out_specs=[row, row, vec, vec],
        out_shape=[jax.ShapeDtypeStruct((t, d), F32), jax.ShapeDtypeStruct((t, d), BF16),
                   jax.ShapeDtypeStruct((1, d), F32), jax.ShapeDtypeStruct((1, d), F32)],
        compiler_params=_params("arbitrary"),
    )(dmain, dres, xhat, rstd, gamma)


def _loss_and_ln_backward(xhat, rstd, gamma, beta, target, tm):
    t, d = xhat.shape

    def body(x_ref, r_ref, g_ref, b_ref, t_ref, du_ref, dub_ref, dg_ref, db_ref, loss_ref):
        xh = x_ref[...]
        g = g_ref[...]
        diff = xh * g + b_ref[...] - t_ref[...]
        part = jnp.sum(jnp.sum(diff * diff, axis=1, keepdims=True), axis=0, keepdims=True) * (0.5 / d)
        dy = diff * (1.0 / d)
        du, dg, db = _ln_backward_math(dy, xh, r_ref[...], g)
        du_ref[...] = du
        dub_ref[...] = du.astype(BF16)
        lossrow = jnp.broadcast_to(part, (1, HEAD_DIM))
        first = pl.program_id(0) == 0

        @pl.when(first)
        def _():
            dg_ref[...] = dg
            db_ref[...] = db
            loss_ref[...] = lossrow

        @pl.when(jnp.logical_not(first))
        def _():
            dg_ref[...] += dg
            db_ref[...] += db
            loss_ref[...] += lossrow

    row = pl.BlockSpec((tm, d), lambda i: (i, 0))
    vec = pl.BlockSpec((1, d), lambda i: (0, 0))
    return pl.pallas_call(
        body, name="loss_ln3_bwd", grid=(t // tm,),
        in_specs=[row, pl.BlockSpec((tm, 1), lambda i: (i, 0)), vec, vec, row],
        out_specs=[row, row, vec, vec, pl.BlockSpec((1, HEAD_DIM), lambda i: (0, 0))],
        out_shape=[jax.ShapeDtypeStruct((t, d), F32), jax.ShapeDtypeStruct((t, d), BF16),
                   jax.ShapeDtypeStruct((1, d), F32), jax.ShapeDtypeStruct((1, d), F32),
                   jax.ShapeDtypeStruct((1, HEAD_DIM), F32)],
        compiler_params=_params("arbitrary"),
    )(xhat, rstd, gamma, beta, target)


def _after(token, a):
    return a if token is None else a + token[:1, :1].astype(a.dtype)


def _pick(n, prefs):
    for p in prefs:
        if n % p == 0:
            return p
    return n


def _local_step(x, mem, target, w, token=None):
    t, d = x.shape
    heads = w["a_log"].shape[1]
    gw = heads * HEAD_DIM
    groups, cg, _ = w["pool_w"].shape
    pw = groups * cg
    n_main = 4 * gw + pw
    in_cols = n_main + 2 * heads
    s_in = w["w_in_t"].shape[0]

    tm = _pick(t, (512, 256, 128))
    tm_ln = _pick(t, (256, 128))
    tm_big = _pick(t, (1024, 512, 256, 128))
    tk = _pick(d, K_STEPS)

    w_in_t = w["w_in_t"].reshape(in_cols, d)
    w_p_t = w_in_t[4 * gw + 2 * heads:]
    w_ba_t = jnp.pad(w_in_t[4 * gw:4 * gw + 2 * heads], ((0, HEAD_DIM - 2 * heads), (0, 0)))
    x_bf = _after(token, x).astype(BF16)
    mem_bf = _after(token, mem).astype(BF16)

    tn_d = _pick(d, (1024, 512, 256, 128))
    proj = _plain("proj_main", x_bf, w_in_t, tb=True, n_used=4 * gw, tm=tm_big, tn=_pick(4 * gw, (1024, 512, 256, 128)),
                  tk=tk, out_dtype=F32)
    pproj = _plain("proj_pool", x_bf, w_p_t, tb=True, tm=tm_big, tn=_pick(pw, (1024, 512, 256, 128)), tk=tk, out_dtype=F32)
    ea, dtb = _gate_vectors(w["a_log"], w["dt_bias"], heads)
    vec128 = lambda i, j: (0, 0)
    ba, bg = _matmul(
        "proj_gates", x_bf, w_ba_t, tb=True, tm=tm, tn=HEAD_DIM, tk=tk,
        extra=[(ea, (1, HEAD_DIM), vec128), (dtb, (1, HEAD_DIM), vec128)],
        outs=[(jax.ShapeDtypeStruct((t, HEAD_DIM), F32), (tm, HEAD_DIM), _tile)] * 2,
        epilogue=_gates_epilogue(heads))
    qkv = _gdn_pre(proj, w["conv_w"], heads)
    o_gdn, states, *solved = _gdn_core(qkv, bg, heads)
    cat_g = _gdn_post(o_gdn, proj, 3 * gw, w["gdn_norm_w"], heads, tm)
    cat_p = _pool_forward(pproj, 0, w["pool_w"], w["pool_scale"])
    cat = jnp.concatenate([cat_g, cat_p], axis=1)
    w = {**w, **(yield ("weights", 1, cat))}
    h1, h1_bf, xhat1, rstd1 = _ln_forward("mix_ln1", cat, w["w_out"], x, w["ln1_g"], w["ln1_b"], tm=tm_ln, tk=tk)

    q = _plain("xattn_q", h1_bf, w["xq_w"], tm=tm, tn=tn_d, tk=tk, out_dtype=BF16)
    mlen = mem.shape[0]
    tm_mem = _pick(mlen, (256, 128))
    k = _plain("xattn_k", mem_bf, w["xk_w"], tm=tm_mem, tn=tn_d, tk=tk, out_dtype=BF16)
    v = _plain("xattn_v", mem_bf, w["xv_w"], tm=tm_mem, tn=tn_d, tk=tk, out_dtype=BF16)
    att = _attention(q, k, v, tm)
    h2, h2_bf, xhat2, rstd2 = _ln_forward("xo_ln2", att, w["xo_w"], h1, w["ln2_g"], w["ln2_b"], tm=tm_ln, tk=tk)

    w = {**w, **(yield ("weights", 2, h2_bf))}
    s_up = w["w_up3"].shape[0]
    ff = s_up * w["w_up3"].shape[2]
    tn_f = _pick(ff // s_up, (1024, 512, 256, 128))

    def up_epi(acc, ex, out, i):
        r = jnp.maximum(acc, 0.0)
        out[0][...] = (r * r).astype(BF16)
        out[1][...] = (2.0 * r).astype(BF16)

    act, act_grad = _matmul(
        "mlp_up", h2_bf, w["w_up3"], b_blocks=s_up, tm=tm_big, tn=tn_f, tk=tk,
        outs=[(jax.ShapeDtypeStruct((t, ff), BF16), (tm_big, tn_f), _tile)] * 2, epilogue=up_epi)
    w = {**w, **(yield ("weights", 3, act))}
    tk_f = _pick(ff, K_STEPS)
    xhat3, rstd3 = _ln_forward("down_ln3", act, w["w_down"], h2, w["ln3_g"], w["ln3_b"], tm=tm, tk=tk_f, want_h=False)

    grads = {}
    du3, du3_bf, grads["ln3_g"], grads["ln3_b"], loss = _loss_and_ln_backward(
        xhat3, rstd3, w["ln3_g"], w["ln3_b"], target, tm_ln)

    def dup_epi(acc, ex, out, i):
        out[0][...] = (acc * ex[0][...].astype(F32)).astype(BF16)

    dup = _matmul(
        "mlp_down_dx", du3_bf, w["w_down"], tb=True, tm=tm_big, tn=tn_f, tk=tk,
        extra=[(act_grad, (tm_big, tn_f), _tile)],
        outs=[(jax.ShapeDtypeStruct((t, ff), BF16), (tm_big, tn_f), _tile)], epilogue=dup_epi)[0]
    tk_t = _pick(t, K_STEPS)
    tm_w = _pick(d, (512, 256, 128))
    grads["w_down"] = _plain("mlp_down_dw", act, du3_bf, ta=True, tm=_pick(ff, (512, 256, 128)), tn=d, tk=tk_t,
                             out_dtype=F32)
    grads["w_up3"] = _plain("mlp_up_dw", h2_bf, dup, ta=True, tm=tm_w, tn=ff // s_up, tk=tk_t, out_dtype=F32, out3=s_up,
                            n_outer=True)
    token = yield ("grads", 0, {n: grads.pop(n) for n in ("w_down", "w_up3")})
    dh2 = _plain("mlp_up_dx", dup, w["w_up3"], tb=True, b_blocks=s_up, tm=tm_big, tn=tn_d,
                 tk=_pick(ff // s_up, K_STEPS), out_dtype=F32)
    du2, du2_bf, grads["ln2_g"], grads["ln2_b"] = _ln_backward_rows(
        "ln2_bwd", dh2, du3, xhat2, rstd2, _after(token, w["ln2_g"]), tm_ln)
    token = yield ("poll", 0, du2_bf)

    grads["xo_w"] = _plain("xo_dw", att, du2_bf, ta=True, tm=tm_w, tn=d, tk=tk_t, out_dtype=F32)
    datt = _plain("xo_dx", du2_bf, w["xo_w"], tb=True, tm=tm, tn=tn_d, tk=tk, out_dtype=BF16)
    dq, dk, dv = _attention_backward(q, k, v, datt, tm)
    tk_m = _pick(mlen, (256, 128))
    grads["xq_w"] = _plain("xq_dw", h1_bf, dq, ta=True, tm=tm_w, tn=d, tk=tk_t, out_dtype=F32)
    grads["xk_w"] = _plain("xk_dw", mem_bf, dk, ta=True, tm=tm_w, tn=tn_d, tk=tk_m, out_dtype=F32)
    grads["xv_w"] = _plain("xv_dw", mem_bf, dv, ta=True, tm=tm_w, tn=tn_d, tk=tk_m, out_dtype=F32)
    du1, du1_bf, grads["ln1_g"], grads["ln1_b"] = _ln_backward(
        "xq_dx_ln1", dq, w["xq_w"], du2, xhat1, rstd1, _after(token, w["ln1_g"]), tm=tm_ln, tk=tk)

    grads["w_out"] = _plain("out_dw", cat, du1_bf, ta=True, tm=tm_w, tn=d, tk=tk_t, out_dtype=F32)
    token = yield ("grads", 1, {n: grads.pop(n) for n in ("xo_w", "xq_w", "xk_w", "xv_w", "w_out")})
    dcat = _plain("out_dx", du1_bf, w["w_out"], tb=True, tm=tm, tn=tn_d, tk=tk, out_dtype=F32)
    dp, grads["pool_w"], grads["pool_scale"] = _pool_backward(dcat, gw, pproj, 0, w["pool_w"],
                                                              _after(token, w["pool_scale"]))
    do_gdn, dz, grads["gdn_norm_w"] = _gdn_post_backward(dcat, o_gdn, proj, 3 * gw, _after(token, w["gdn_norm_w"]),
                                                         heads, tm)
    dqkv, dbg = _gdn_core_backward(qkv, bg, states, solved, do_gdn, heads)
    token = yield ("poll", 1, dqkv)
    dqkv_pre, grads["conv_w"] = _gdn_pre_backward(proj, _after(token, w["conv_w"]), dqkv, heads)
    dba, dalog_row, ddt_row = _gates_backward(ba, bg, dbg, ea, dtb, heads)
    grads["a_log"] = dalog_row[:, heads:2 * heads]
    grads["dt_bias"] = ddt_row[:, heads:2 * heads]

    dproj = jnp.concatenate([dqkv_pre, dz, dp], axis=1)
    dw_main = _plain("proj_dw", dproj, x_bf, ta=True, tm=_pick(n_main, (512, 256, 128)), tn=d, tk=tk_t, out_dtype=F32)
    dw_ba = _plain("proj_gates_dw", dba, x_bf, ta=True, tm=HEAD_DIM, tn=tn_d, tk=tk_t, out_dtype=F32)
    dw_in_t = jnp.concatenate([dw_main[:4 * gw], dw_ba[:2 * heads], dw_main[4 * gw:]], axis=0)
    grads["w_in_t"] = dw_in_t.reshape(s_in, in_cols // s_in, d)

    def dx_epi(acc, ex, out, i):
        out[0][...] = acc + ex[1][...] + ALPHA * ex[0][...]

    def add_epi(acc, ex, out, i):
        out[0][...] = acc + ex[0][...]

    token = yield ("grads", 2, {n: grads.pop(n) for n in ("w_in_t", "pool_w")})
    dx_gates = _plain("proj_gates_dx", dba, _after(token, w_ba_t), tm=tm, tn=tn_d, tk=HEAD_DIM, out_dtype=F32)
    out_tile = [(jax.ShapeDtypeStruct((t, d), F32), (tm, tn_d), _tile)]
    dx_pool = _matmul("proj_pool_dx", dp, w_p_t, tm=tm, tn=tn_d, tk=_pick(pw, K_STEPS),
                      extra=[(dx_gates, (tm, tn_d), _tile)], outs=out_tile, epilogue=add_epi)[0]
    grad_x = _matmul(
        "proj_dx", dproj, w_in_t, k_used=4 * gw, tm=tm, tn=tn_d, tk=_pick(4 * gw, K_STEPS),
        extra=[(du1, (tm, tn_d), _tile), (dx_pool, (tm, tn_d), _tile)], outs=out_tile, epilogue=dx_epi)[0]
    yield ("poll", 2, grad_x)
    return loss, grad_x, grads


def _adamw(name, w, g, m, v):
    r, c = w.shape
    if r % 8 == 0:
        tr = _pick(r, (256, 128, 64, 32, 16, 8))
        blk, steps = pl.BlockSpec((tr, c), lambda i: (i, 0)), r // tr
    else:
        tc = _pick(c, (256, 128))
        blk, steps = pl.BlockSpec((r, tc), lambda i: (0, i)), c // tc
    c1 = 1.0 - ADAM_B1 ** ADAM_STEP
    c2 = 1.0 - ADAM_B2 ** ADAM_STEP

    def body(w_ref, g_ref, m_ref, v_ref, d_ref, mo_ref, vo_ref):
        gv = g_ref[...]
        mn = ADAM_B1 * m_ref[...] + (1.0 - ADAM_B1) * gv
        vn = ADAM_B2 * v_ref[...] + (1.0 - ADAM_B2) * (gv * gv)
        d_ref[...] = -ADAM_LR * ((mn / c1) / (jnp.sqrt(vn / c2) + ADAM_EPS) + ADAM_WD * w_ref[...])
        mo_ref[...] = mn
        vo_ref[...] = vn

    return pl.pallas_call(
        body, name=name, grid=(steps,), in_specs=[blk] * 4, out_specs=[blk] * 3,
        out_shape=[jax.ShapeDtypeStruct((r, c), F32)] * 3,
        compiler_params=_params("parallel"),
    )(w, g, m, v)


def _place():
    x, y, c = lax.axis_index("x"), lax.axis_index("y"), lax.axis_index("c")
    chips = [(1 - x, y), (x, 1 - y), (1 - x, 1 - y)]
    return x, y, c, chips


HBM = pl.BlockSpec(memory_space=pltpu.HBM)


SEM = pl.BlockSpec(memory_space=pltpu.SEMAPHORE)
ANY = pl.BlockSpec(memory_space=pl.ANY)
EFFECT = pltpu.SideEffectType.DATAFLOW_SIDE_EFFECTING


def _in_hbm(a):
    return pltpu.with_memory_space_constraint(a, pltpu.HBM)


def _remote(src, dst, send_sem, recv_sem, to):
    return pltpu.make_async_remote_copy(src_ref=src, dst_ref=dst, send_sem=send_sem, recv_sem=recv_sem,
                                        device_id=to, device_id_type=MESH)


def _by_rows(rows):
    return rows % 32 == 0


def _half_shape(rows, cols):
    return (rows // 2, cols) if _by_rows(rows) else (rows, cols // 2)


def _half(ref, which, *lead):
    rows, cols = ref.shape[-2:]
    if _by_rows(rows):
        return ref.at[(*lead, pl.ds(which * (rows // 2), rows // 2))]
    return ref.at[(*lead, slice(None), pl.ds(which * (cols // 2), cols // 2))]


def _landed(lands, i, shard_index, which):
    return _half(lands[i], which, shard_index)


def _gather_start(name, shards, after):
    n = len(shards)
    lands = [lax.empty((N_SHARD,) + s.shape, s.dtype) for s in shards]

    def body(*refs):
        ins, zones = refs[:n], refs[n:2 * n]
        ici_send, ici_recv, own_send, own_recv = refs[2 * n + 1:2 * n + 5]
        token = refs[-1]
        x, y, c, chips = _place()
        me = 2 * x + y
        for i in range(n):
            for j, chip in enumerate(chips):
                _remote(_half(ins[i], c), _landed(zones, i, me, c), ici_send.at[3 * i + j],
                        ici_recv.at[3 * i + j], (*chip, c)).start()
        for i in range(n):
            _remote(ins[i], zones[i].at[me], own_send.at[i], own_recv.at[i], (x, y, 1 - c)).start()
        token[...] = jnp.zeros_like(token)

    dma = pltpu.SemaphoreType.DMA
    outs = pl.pallas_call(
        body, name=name,
        in_specs=[HBM] * (2 * n) + [ANY],
        out_shape=(dma((3 * n,)), dma((3 * n,)), dma((n,)), dma((n,)),
                   *[pltpu.HBM(a.shape, a.dtype) for a in shards + lands], jax.ShapeDtypeStruct((8, LANES), F32)),
        out_specs=(SEM, SEM, SEM, SEM, *[HBM] * (2 * n), pl.BlockSpec(memory_space=pltpu.VMEM)),
        input_output_aliases={k: 4 + k for k in range(2 * n)},
        compiler_params=pltpu.CompilerParams(has_side_effects=EFFECT),
    )(*[_in_hbm(a) for a in shards + lands], after)
    sems = dict(zip(("ici_send", "ici_recv", "own_send", "own_recv"), outs[:4]))
    return sems, list(outs[4:4 + n]), list(outs[4 + n:4 + 2 * n]), outs[-1]


def _gather_forward(name, idx, lands, sems, after):
    n = len(idx)

    def body(*refs):
        zones = refs[:n]
        ici_recv = refs[n]
        fwd_send, fwd_recv = refs[n + 2], refs[n + 3]
        x, y, c, chips = _place()
        for k, i in enumerate(idx):
            for j, chip in enumerate(chips):
                half = _landed(zones, k, 2 * chip[0] + chip[1], c)
                _remote(half, half, fwd_send.at[3 * k + j], ici_recv.at[3 * i + j], (*chip, c)).wait_recv()
                _remote(half, half, fwd_send.at[3 * k + j], fwd_recv.at[3 * k + j], (x, y, 1 - c)).start()

    dma = pltpu.SemaphoreType.DMA
    outs = pl.pallas_call(
        body, name=name,
        in_specs=[HBM] * n + [SEM, ANY],
        out_shape=(dma((3 * n,)), dma((3 * n,)), *[pltpu.HBM(a.shape, a.dtype) for a in lands]),
        out_specs=(SEM, SEM, *[HBM] * n),
        input_output_aliases={k: 2 + k for k in range(n)},
        compiler_params=pltpu.CompilerParams(has_side_effects=EFFECT),
    )(*lands, sems["ici_recv"], after)
    return (outs[0], outs[1]), list(outs[2:])


def _gather_wait(name, idx, shards, lands, sems, fwd, after):
    n = len(idx)

    def body(*refs):
        ins, zones = refs[:n], refs[n:2 * n]
        ici_send, own_send, own_recv, fwd_send, fwd_recv = refs[2 * n:2 * n + 5]
        x, y, c, chips = _place()
        me = 2 * x + y
        for k, i in enumerate(idx):
            mine = _half(ins[k], c)
            for j, chip in enumerate(chips):
                theirs = 2 * chip[0] + chip[1]
                _remote(mine, _landed(zones, k, me, c), ici_send.at[3 * i + j], fwd_recv.at[3 * k + j],
                        (*chip, c)).wait_send()
                sent = _landed(zones, k, theirs, c)
                _remote(sent, sent, fwd_send.at[3 * k + j], fwd_recv.at[3 * k + j], (x, y, 1 - c)).wait_send()
                passed = _landed(zones, k, theirs, 1 - c)
                _remote(passed, passed, fwd_send.at[3 * k + j], fwd_recv.at[3 * k + j], (x, y, 1 - c)).wait_recv()
            own = _remote(ins[k], zones[k].at[me], own_send.at[i], own_recv.at[i], (x, y, 1 - c))
            own.wait_send()
            own.wait_recv()

    outs = pl.pallas_call(
        body, name=name,
        in_specs=[HBM] * (2 * n) + [SEM] * 5 + [ANY],
        out_shape=tuple(pltpu.HBM(a.shape, a.dtype) for a in lands),
        out_specs=tuple([HBM] * n),
        input_output_aliases={n + k: k for k in range(n)},
        compiler_params=pltpu.CompilerParams(has_side_effects=EFFECT),
    )(*shards, *lands, sems["ici_send"], sems["own_send"], sems["own_recv"], fwd[0], fwd[1], after)
    return list(outs)


def _all_reduce_small(name, slab, after=None):
    r, width = slab.shape
    ndev = 8

    def body(x_ref, after_ref, out_ref, buf, send_sems, recv_sems):
        x, y, c, _ = _place()
        me = 4 * x + 2 * y + c
        buf[me] = x_ref[...]
        copies = []
        for k in range(1, ndev):
            peer = jnp.bitwise_xor(me, k)
            to = (peer // 4, (peer // 2) % 2, peer % 2)
            cp = pltpu.make_async_remote_copy(src_ref=x_ref, dst_ref=buf.at[me], send_sem=send_sems.at[k - 1],
                                              recv_sem=recv_sems.at[k - 1], device_id=to, device_id_type=MESH)
            cp.start()
            copies.append(cp)
        for k in range(1, ndev):
            peer = jnp.bitwise_xor(me, k)
            pltpu.make_async_remote_copy(src_ref=x_ref, dst_ref=buf.at[peer], send_sem=send_sems.at[k - 1],
                                         recv_sem=recv_sems.at[k - 1], device_id=(x, y, c),
                                         device_id_type=MESH).wait_recv()
        for cp in copies:
            cp.wait_send()
        total = buf[0]
        for d in range(1, ndev):
            total = total + buf[d]
        out_ref[...] = total

    return pl.pallas_call(
        body, name=name,
        in_specs=[pl.BlockSpec(memory_space=pltpu.VMEM), ANY], out_specs=pl.BlockSpec(memory_space=pltpu.VMEM),
        out_shape=jax.ShapeDtypeStruct((r, width), F32),
        scratch_shapes=[pltpu.VMEM((ndev, r, width), F32), pltpu.SemaphoreType.DMA((ndev - 1,)),
                        pltpu.SemaphoreType.DMA((ndev - 1,))],
        compiler_params=pltpu.CompilerParams(vmem_limit_bytes=VMEM_LIMIT),
    )(slab, slab if after is None else after)


def _half_tiling(rows, cols):
    if _by_rows(rows):
        tr = _pick(rows // 2, (256, 128, 64, 32, 16))
        nb = (rows // 2) // tr
        return (tr, cols), nb, (lambda which, b: (which * nb + b, 0)), (lambda b: (b, 0))
    tc = _pick(cols // 2, (256, 128))
    nb = (cols // 2) // tc
    return (rows, tc), nb, (lambda which, b: (0, which * nb + b)), (lambda b: (0, b))


def _chip_partial(name, grad, other, core):
    s, r, cdim = grad.shape
    blk, nb, whole, within = _half_tiling(r, cdim)

    def body(core_ref, g_ref, o_ref, out_ref):
        out_ref[...] = (g_ref[...] + o_ref[...]).astype(BF16)

    return pl.pallas_call(
        body, name=name,
        grid_spec=pltpu.PrefetchScalarGridSpec(
            num_scalar_prefetch=1, grid=(s, nb),
            in_specs=[pl.BlockSpec((None,) + blk, lambda j, b, core_ref: (j,) + whole(core_ref[0], b)),
                      pl.BlockSpec((None,) + blk, lambda j, b, core_ref: (j,) + within(b))],
            out_specs=pl.BlockSpec((None,) + blk, lambda j, b, core_ref: (j,) + within(b))),
        out_shape=jax.ShapeDtypeStruct((s,) + _half_shape(r, cdim), BF16),
        compiler_params=_params("parallel", "parallel"),
    )(core, grad, other)


def _partial_copies(ins, zones, send_sems, recv_sems):
    x, y, c, chips = _place()
    return [_remote(ins[i].at[2 * chip[0] + chip[1]], zones[i].at[j], send_sems.at[3 * i + j],
                    recv_sems.at[3 * i + j], (*chip, c))
            for i in range(len(ins)) for j, chip in enumerate(chips)]


def _swap_copies(ins, zones, send_sems, recv_sems):
    x, y, c, _ = _place()
    copies = []
    for i in range(len(ins)):
        for s in range(N_SHARD):
            copies.append(_remote(_half(ins[i], 1 - c, s), zones[i].at[s],
                                  send_sems.at[N_SHARD * i + s], recv_sems.at[N_SHARD * i + s], (x, y, 1 - c)))
    return copies


def _exchange_start(name, plan, sources, lands, per_array):
    n = len(sources)
    lands = [lax.empty(shape, dtype) for shape, dtype in lands]

    def body(*refs):
        for cp in plan(refs[:n], refs[n:2 * n], refs[2 * n], refs[2 * n + 1]):
            cp.start()
        refs[-1][...] = jnp.zeros_like(refs[-1])

    dma = pltpu.SemaphoreType.DMA
    outs = pl.pallas_call(
        body, name=name,
        in_specs=[HBM] * (2 * n),
        out_shape=(dma((per_array * n,)), dma((per_array * n,)),
                   *[pltpu.HBM(a.shape, a.dtype) for a in list(sources) + lands], jax.ShapeDtypeStruct((8, LANES), F32)),
        out_specs=(SEM, SEM, *[HBM] * (2 * n), pl.BlockSpec(memory_space=pltpu.VMEM)),
        input_output_aliases={k: 2 + k for k in range(2 * n)},
        compiler_params=pltpu.CompilerParams(has_side_effects=EFFECT),
    )(*[_in_hbm(a) for a in list(sources) + lands])
    return (outs[0], outs[1]), list(outs[2:2 + n]), list(outs[2 + n:2 + 2 * n]), outs[-1]


def _exchange_wait(name, plan, started, after):
    sems, partials, lands, _ = started
    n = len(partials)

    def body(*refs):
        for cp in plan(refs[:n], refs[n:2 * n], refs[2 * n], refs[2 * n + 1]):
            cp.wait_send()
            cp.wait_recv()

    outs = pl.pallas_call(
        body, name=name,
        in_specs=[HBM] * (2 * n) + [SEM, SEM] + [ANY] * len(after),
        out_shape=tuple(pltpu.HBM(a.shape, a.dtype) for a in lands),
        out_specs=tuple([HBM] * n),
        input_output_aliases={n + k: k for k in range(n)},
        compiler_params=pltpu.CompilerParams(has_side_effects=EFFECT),
    )(*partials, *lands, sems[0], sems[1], *after)
    return list(outs)


def _reduce_own(name, grad, other, received, where):
    s, r, cdim = grad.shape
    blk, nb, whole, within = _half_tiling(r, cdim)

    def body(where_ref, g_ref, o_ref, r_ref, out_ref):
        total = g_ref[...] + o_ref[...]
        for j in range(3):
            total = total + r_ref[j].astype(F32)
        out_ref[...] = total

    return pl.pallas_call(
        body, name=name,
        grid_spec=pltpu.PrefetchScalarGridSpec(
            num_scalar_prefetch=1, grid=(nb,),
            in_specs=[pl.BlockSpec((None,) + blk, lambda b, w_ref: (w_ref[0],) + whole(w_ref[1], b)),
                      pl.BlockSpec((None,) + blk, lambda b, w_ref: (w_ref[0],) + within(b)),
                      pl.BlockSpec((3,) + blk, lambda b, w_ref: (0,) + within(b))],
            out_specs=pl.BlockSpec(blk, lambda b, w_ref: whole(w_ref[1], b))),
        out_shape=jax.ShapeDtypeStruct((r, cdim), F32),
        compiler_params=_params("parallel"),
    )(where, grad, other, received)


def _join_start(name, halves):
    n = len(halves)

    def body(*refs):
        bufs, send_sems, recv_sems = refs[:n], refs[n], refs[n + 1]
        x, y, c, _ = _place()
        for i in range(n):
            mine = _half(bufs[i], c)
            _remote(mine, mine, send_sems.at[i], recv_sems.at[i], (x, y, 1 - c)).start()
        refs[-1][...] = jnp.zeros_like(refs[-1])

    dma = pltpu.SemaphoreType.DMA
    outs = pl.pallas_call(
        body, name=name,
        in_specs=[HBM] * n,
        out_shape=(dma((n,)), dma((n,)), *[pltpu.HBM(h.shape, F32) for h in halves], jax.ShapeDtypeStruct((8, LANES), F32)),
        out_specs=(SEM, SEM, *[HBM] * n, pl.BlockSpec(memory_space=pltpu.VMEM)),
        input_output_aliases={k: 2 + k for k in range(n)},
        compiler_params=pltpu.CompilerParams(has_side_effects=EFFECT),
    )(*[_in_hbm(h) for h in halves])
    return (outs[0], outs[1]), list(outs[2:2 + n]), outs[-1]


def _join_wait(name, started, after):
    sems, bufs, _ = started
    n = len(bufs)

    def body(*refs):
        bufs, send_sems, recv_sems = refs[:n], refs[n], refs[n + 1]
        x, y, c, _ = _place()
        for i in range(n):
            mine, theirs = _half(bufs[i], c), _half(bufs[i], 1 - c)
            _remote(mine, mine, send_sems.at[i], recv_sems.at[i], (x, y, 1 - c)).wait_send()
            _remote(theirs, theirs, send_sems.at[i], recv_sems.at[i], (x, y, 1 - c)).wait_recv()

    outs = pl.pallas_call(
        body, name=name,
        in_specs=[HBM] * n + [SEM, SEM] + [ANY] * len(after),
        out_shape=tuple(pltpu.HBM(b.shape, F32) for b in bufs),
        out_specs=tuple([HBM] * n),
        input_output_aliases={k: k for k in range(n)},
        compiler_params=pltpu.CompilerParams(has_side_effects=EFFECT),
    )(*bufs, sems[0], sems[1], *after)
    return list(outs)


BIG = ("w_in", "pool_w", "w_out", "xq_w", "xk_w", "xv_w", "xo_w", "w_up", "w_down")
GATHER_GROUPS = ((0, 1), (2, 3, 4, 5, 6), (7,), (8,))
SMALL = ("conv_w", "a_log", "dt_bias", "gdn_norm_w", "pool_scale", "ln1_g", "ln1_b", "ln2_g", "ln2_b", "ln3_g", "ln3_b")
ORDER = ("w_in", "conv_w", "a_log", "dt_bias", "gdn_norm_w", "pool_w", "pool_scale", "w_out", "ln1_g", "ln1_b",
         "xq_w", "xk_w", "xv_w", "xo_w", "ln2_g", "ln2_b", "w_up", "w_down", "ln3_g", "ln3_b")
LANES = 128


def _rows(flat_len):
    return -(-flat_len // LANES)


def _pack(pieces):
    out = []
    for p in pieces:
        flat = p.reshape(-1).astype(F32)
        out.append(jnp.pad(flat, (0, _rows(flat.shape[0]) * LANES - flat.shape[0])).reshape(-1, LANES))
    slab = jnp.concatenate(out, axis=0)
    return jnp.pad(slab, ((0, -slab.shape[0] % 8), (0, 0)))


def _unpack(slab, shapes):
    out, row = [], 0
    for shp in shapes:
        size = math.prod(shp)
        out.append(slab[row:row + _rows(size)].reshape(-1)[:size].reshape(shp))
        row += _rows(size)
    return out


TRANSPOSED = ("w_in",)


def _as2d(name, a):
    a = a[0]
    if name in TRANSPOSED:
        return jnp.swapaxes(a, 0, 1)
    return a.reshape(-1, a.shape[-1]) if a.ndim == 3 else a


def _from2d(name, a, shape):
    return (jnp.swapaxes(a, 0, 1) if name in TRANSPOSED else a).reshape(shape)


def kernel(x, mem, w_in, conv_w, a_log, dt_bias, gdn_norm_w, pool_w, pool_scale, w_out, ln1_g, ln1_b, xq_w, xk_w, xv_w, xo_w, ln2_g, ln2_b, w_up, w_down, ln3_g, ln3_b, loss_target, m_w_in, m_conv_w, m_a_log, m_dt_bias, m_gdn_norm_w, m_pool_w, m_pool_scale, m_w_out, m_ln1_g, m_ln1_b, m_xq_w, m_xk_w, m_xv_w, m_xo_w, m_ln2_g, m_ln2_b, m_w_up, m_w_down, m_ln3_g, m_ln3_b, v_w_in, v_conv_w, v_a_log, v_dt_bias, v_gdn_norm_w, v_pool_w, v_pool_scale, v_w_out, v_ln1_g, v_ln1_b, v_xq_w, v_xk_w, v_xv_w, v_xo_w, v_ln2_g, v_ln2_b, v_w_up, v_w_down, v_ln3_g, v_ln3_b):
    given = dict(locals())
    cx, cy, cc = lax.axis_index("x"), lax.axis_index("y"), lax.axis_index("c")
    me = 2 * cx + cy
    groups = pool_w.shape[1]
    cs = pool_w.shape[2]
    kk, conv_cols = conv_w.shape[1], conv_w.shape[2]
    core = cc.astype(jnp.int32).reshape(1)
    where = jnp.stack([me, cc]).astype(jnp.int32)

    conv_slab = jnp.zeros((kk, N_SHARD * conv_cols), F32)
    conv_slab = lax.dynamic_update_slice(conv_slab, conv_w[0] * (cc == 0).astype(F32), (0, me * conv_cols))
    wts = {"conv_w": _unpack(_all_reduce_small("gather_conv_w", _pack([conv_slab])), [conv_slab.shape])[0]}

    started = {}

    def start(name, idx, after, token=None):
        casts = [_after(token, _as2d(BIG[i], given[BIG[i]])).astype(BF16) for i in idx]
        sems, shards, lands, token = _gather_start(name, casts, after)
        for k, i in enumerate(idx):
            started[i] = (sems, k, shards[k], lands[k])
        return token

    token = start("gather_start_first", GATHER_GROUPS[0], wts["conv_w"])
    token = start("gather_start_rest", tuple(i for group in GATHER_GROUPS[1:] for i in group), token, token)

    def fetch(group, after):
        members = [started[i] for i in GATHER_GROUPS[group]]
        sems, idx = members[0][0], [m[1] for m in members]
        fwd, zones = _gather_forward(f"gather_forward_{group}", idx, [m[3] for m in members], sems, after)
        full = dict(zip([BIG[i] for i in GATHER_GROUPS[group]],
                        _gather_wait(f"gather_wait_{group}", idx, [m[2] for m in members], zones, sems, fwd, after)))
        out = {}
        for n, a in full.items():
            if n == "w_in":
                out["w_in_t"] = a
            elif n == "w_up":
                out["w_up3"] = a
            elif n == "pool_w":
                out[n] = a.reshape(N_SHARD, groups, cs, -1).transpose(1, 0, 2, 3).reshape(groups, N_SHARD * cs, -1)
            else:
                out[n] = a.reshape(-1, a.shape[-1])
        return out

    for n in ("a_log", "dt_bias", "gdn_norm_w", "pool_scale", "ln1_g", "ln1_b", "ln2_g", "ln2_b", "ln3_g", "ln3_b"):
        wts[n] = given[n]
    wts.update(fetch(0, token))

    def start_swap(group, grads):
        names, blocks = [], []
        for n, g in grads.items():
            if n == "pool_w":
                g = g.reshape(groups, N_SHARD, cs, -1).transpose(1, 0, 2, 3).reshape(N_SHARD, groups * cs, -1)
            elif g.ndim == 2:
                g = g.reshape(N_SHARD, -1, g.shape[-1])
            names.append({"w_in_t": "w_in", "w_up3": "w_up"}.get(n, n))
            blocks.append(g)
        zones = [((N_SHARD,) + _half_shape(b.shape[1], b.shape[2]), F32) for b in blocks]
        swap = _exchange_start(f"grad_swap_start_{group}", _swap_copies, blocks, zones, N_SHARD)
        return {"group": group, "names": names, "swap": swap, "token": swap[3]}

    def start_send(state, after):
        group, names = state["group"], state["names"]
        state["blocks"] = state["swap"][1]
        state["others"] = _exchange_wait(f"grad_swap_wait_{group}", _swap_copies, state["swap"], after)
        partials = [_chip_partial("chip_partial_" + n, gb, ob, core)
                    for n, gb, ob in zip(names, state["blocks"], state["others"])]
        zones = [((3,) + p.shape[1:], BF16) for p in partials]
        state["send"] = _exchange_start(f"grad_send_start_{group}", _partial_copies, partials, zones, 3)
        state["token"] = state["send"][3]

    grad, delta, new_m, new_v = {}, {}, {}, {}

    def start_join(state, after):
        group, names = state["group"], state["names"]
        received = _exchange_wait(f"grad_send_wait_{group}", _partial_copies, state["send"], after)
        halves = [_reduce_own("reduce_own_" + n, gb, ob, rb, where)
                  for n, gb, ob, rb in zip(names, state["blocks"], state["others"], received)]
        state["join"] = _join_start(f"grad_join_start_{group}", halves)
        return state["join"][2]

    def finish_reduce(state, after):
        group, names = state["group"], state["names"]
        for n, g in zip(names, _join_wait(f"grad_join_wait_{group}", state["join"], after)):
            shp = given[n].shape
            d2, m2, v2 = _adamw("adamw_" + n, _as2d(n, given[n]), g, _as2d(n, given["m_" + n]), _as2d(n, given["v_" + n]))
            grad[n], delta[n], new_m[n], new_v[n] = (_from2d(n, a, shp) for a in (g, d2, m2, v2))
        return d2

    step = _local_step(x[0], mem[0], loss_target[0], wts, token)
    pending = {}
    request = next(step)
    while True:
        try:
            kind, group, payload = request
            if kind == "weights":
                request = step.send(fetch(group, payload))
            elif kind == "grads":
                pending[group] = start_swap(group, payload)
                request = step.send(pending[group]["token"])
            else:
                start_send(pending[group], [payload])
                request = step.send(pending[group]["token"])
        except StopIteration as stop:
            loss_row, grad_x, g = stop.value
            break

    after = [pending[2]["token"], grad_x]
    for group in (0, 1):
        after = [start_join(pending[group], after)]
    for group in (0, 1):
        after = [finish_reduce(pending[group], after)]
    after = [finish_reduce(pending[2], [start_join(pending[2], after)])]

    small_names = ("a_log", "dt_bias", "gdn_norm_w", "pool_scale", "ln1_g", "ln1_b", "ln2_g", "ln2_b", "ln3_g", "ln3_b")
    pieces = [g["conv_w"]] + [g[n] for n in small_names] + [loss_row[:, :1]]
    shapes = [p.shape for p in pieces]
    summed = _unpack(_all_reduce_small("all_reduce_small", _pack(pieces), after[0]), shapes)
    gsmall = dict(zip(small_names, summed[1:-1]))
    gsmall["conv_w"] = lax.dynamic_slice(summed[0], (0, me * conv_cols), (kk, conv_cols))
    loss = summed[-1][0, 0]

    sshapes = [given[n].shape for n in SMALL]
    slabs = [_pack([given[p + n] for n in SMALL]) for p in ("", "m_", "v_")]
    gslab = _pack([gsmall[n] for n in SMALL])
    outs = _adamw("adamw_small", slabs[0], gslab, slabs[1], slabs[2])
    for dst, slab in zip((delta, new_m, new_v), outs):
        dst.update(zip(SMALL, _unpack(slab, sshapes)))
    for n in SMALL:
        grad[n] = gsmall[n].reshape(given[n].shape)

    return (loss, grad_x[None], *[grad[n] for n in ORDER], *[delta[n] for n in ORDER],
            *[new_m[n] for n in ORDER], *[new_v[n] for n in ORDER])
```

```python
import functools
import math

import jax
import jax.numpy as jnp
from jax import lax
from jax.experimental import pallas as pl
from jax.experimental.pallas import tpu as pltpu

F32 = jnp.float32
BF16 = jnp.bfloat16
MESH = pl.DeviceIdType.MESH

HEAD_DIM = 128
CHUNK = 64
POOL_WINDOWS = (2, 4, 8, 16)
XATTN_HEADS = 4
ALPHA = 2.0 ** 0.25
LN_EPS = 1e-5
NORM_EPS = 1e-6
ADAM_LR, ADAM_B1, ADAM_B2, ADAM_EPS, ADAM_WD, ADAM_STEP = 0.001, 0.9, 0.999, 1e-08, 0.01, 10
N_SHARD = 4
VMEM_LIMIT = 56 * 1024 * 1024
K_STEPS = (2048, 1024, 512, 256, 128)


def _params(*sem):
    return pltpu.CompilerParams(dimension_semantics=sem, vmem_limit_bytes=VMEM_LIMIT)


def _bdot(a, b, ta=False, tb=False):
    dims = (((0 if ta else 1,), (1 if tb else 0,)), ((), ()))
    return lax.dot_general(a.astype(BF16), b.astype(BF16), dims, preferred_element_type=F32)


def _sigmoid(x):
    return 1.0 / (1.0 + jnp.exp(-x))


def _matmul(name, a, b, *, ta=False, tb=False, tm, tn, tk, extra=(), outs, epilogue, b_blocks=None,
            sequential=False, n_used=None, k_used=None, n_outer=False):
    m, k_dim = (a.shape[1], a.shape[0]) if ta else a.shape
    if b_blocks and tb:
        n = b.shape[1]
        k_dim = b.shape[0] * b.shape[2]
        per = b.shape[2] // tk
        b_spec = pl.BlockSpec((None, tn, tk), lambda i, j, k: (k // per, j, k % per))
    elif b_blocks:
        n = b.shape[0] * b.shape[2]
        per = b.shape[2] // tn
        b_spec = pl.BlockSpec((None, tk, tn), lambda i, j, k: (j // per, k, j % per))
    elif tb:
        n = b.shape[0]
        b_spec = pl.BlockSpec((tn, tk), lambda i, j, k: (j, k))
    else:
        n = b.shape[1]
        b_spec = pl.BlockSpec((tk, tn), lambda i, j, k: (k, j))
    n, k_dim = n_used or n, k_used or k_dim
    assert m % tm == 0 and n % tn == 0 and k_dim % tk == 0, (name, m, n, k_dim, tm, tn, tk)
    nk = k_dim // tk
    a_spec = pl.BlockSpec((tk, tm), lambda i, j, k: (k, i)) if ta else pl.BlockSpec((tm, tk), lambda i, j, k: (i, k))
    n_extra, n_out = len(extra), len(outs)

    def wrap(index_map):
        return lambda i, j, k: index_map(i, j)

    def spec(block, index_map):
        if n_outer:
            return pl.BlockSpec(block, lambda j, i, k: index_map(i, j, k))
        return pl.BlockSpec(block, index_map)

    row_axis = 1 if n_outer else 0

    def body_one_step(*refs):
        ex = refs[2:2 + n_extra]
        out = refs[2 + n_extra:2 + n_extra + n_out]
        epilogue(_bdot(refs[0][...], refs[1][...], ta, tb), ex, out, pl.program_id(row_axis))

    def body(*refs):
        a_ref, b_ref = refs[0], refs[1]
        ex = refs[2:2 + n_extra]
        out = refs[2 + n_extra:2 + n_extra + n_out]
        acc = refs[-1]
        i, k = pl.program_id(row_axis), pl.program_id(2)
        part = _bdot(a_ref[...], b_ref[...], ta, tb)

        @pl.when(k == 0)
        def _():
            acc[...] = part

        @pl.when(jnp.logical_and(k > 0, k < nk - 1))
        def _():
            acc[...] += part

        @pl.when(k == nk - 1)
        def _():
            epilogue(acc[...] + part, ex, out, i)

    sem = ("arbitrary",) * 3 if sequential else ("parallel", "parallel", "arbitrary")
    res = pl.pallas_call(
        body_one_step if nk == 1 else body, name=name,
        grid=(n // tn, m // tm, nk) if n_outer else (m // tm, n // tn, nk),
        in_specs=[spec(a_spec.block_shape, a_spec.index_map), spec(b_spec.block_shape, b_spec.index_map)]
        + [spec(bs, wrap(im)) for _, bs, im in extra],
        out_specs=[spec(bs, wrap(im)) for _, bs, im in outs],
        out_shape=[s for s, _, _ in outs],
        scratch_shapes=[] if nk == 1 else [pltpu.VMEM((tm, tn), F32)],
        compiler_params=_params(*sem),
    )(a, b, *[x for x, _, _ in extra])
    return res


def _tile(i, j):
    return (i, j)


def _plain(name, a, b, *, ta=False, tb=False, tm, tn, tk, out_dtype, b_blocks=None, out3=None, n_used=None,
           n_outer=False):
    m = a.shape[1] if ta else a.shape[0]
    if b_blocks:
        n = b.shape[1] if tb else b.shape[0] * b.shape[2]
    else:
        n = n_used or (b.shape[0] if tb else b.shape[1])

    def epi(acc, ex, out, i):
        out[0][...] = acc.astype(out_dtype)

    if out3:
        per = (n // out3) // tn
        spec = (jax.ShapeDtypeStruct((out3, m, n // out3), out_dtype), (None, tm, tn),
                lambda i, j: (j // per, i, j % per))
    else:
        spec = (jax.ShapeDtypeStruct((m, n), out_dtype), (tm, tn), _tile)
    return _matmul(name, a, b, ta=ta, tb=tb, tm=tm, tn=tn, tk=tk, outs=[spec], epilogue=epi,
                   b_blocks=b_blocks, n_used=n_used, n_outer=n_outer)[0]


def _ln_forward(name, a, b, res, gamma, beta, *, tm, tk, want_h=True):
    m, n = res.shape

    def epi(acc, ex, out, i):
        u = ALPHA * ex[0][...] + acc
        mu = jnp.mean(u, axis=-1, keepdims=True)
        xc = u - mu
        var = jnp.mean(xc * xc, axis=-1, keepdims=True)
        rstd = lax.rsqrt(var + LN_EPS)
        xhat = xc * rstd
        out[-2][...] = xhat
        out[-1][...] = rstd
        if want_h:
            h = xhat * ex[1][...] + ex[2][...]
            out[0][...] = h
            out[1][...] = h.astype(BF16)

    row = lambda i, j: (i, 0)
    vec = lambda i, j: (0, 0)
    outs = [(jax.ShapeDtypeStruct((m, n), F32), (tm, n), row), (jax.ShapeDtypeStruct((m, n), BF16), (tm, n), row),
            (jax.ShapeDtypeStruct((m, n), F32), (tm, n), row), (jax.ShapeDtypeStruct((m, 1), F32), (tm, 1), row)]
    return _matmul(
        name, a, b, tm=tm, tn=n, tk=tk,
        extra=[(res, (tm, n), row), (gamma, (1, n), vec), (beta, (1, n), vec)],
        outs=outs if want_h else outs[2:], epilogue=epi)


def _ln_backward_math(dy, xhat, rstd, gamma):
    dxhat = dy * gamma
    m1 = jnp.mean(dxhat, axis=-1, keepdims=True)
    m2 = jnp.mean(dxhat * xhat, axis=-1, keepdims=True)
    du = rstd * (dxhat - m1 - xhat * m2)
    return du, jnp.sum(dy * xhat, axis=0, keepdims=True), jnp.sum(dy, axis=0, keepdims=True)


def _ln_backward(name, a, b, dres, xhat, rstd, gamma, *, tm, tk, b_blocks=None, tb=True):
    m, n = dres.shape

    def epi(acc, ex, out, i):
        dy = acc + ALPHA * ex[0][...]
        du, dg, db = _ln_backward_math(dy, ex[1][...], ex[2][...], ex[3][...])
        out[0][...] = du
        out[1][...] = du.astype(BF16)
        first = i == 0

        @pl.when(first)
        def _():
            out[2][...] = dg
            out[3][...] = db

        @pl.when(jnp.logical_not(first))
        def _():
            out[2][...] += dg
            out[3][...] += db

    row = lambda i, j: (i, 0)
    vec = lambda i, j: (0, 0)
    return _matmul(
        name, a, b, tb=tb, tm=tm, tn=n, tk=tk, b_blocks=b_blocks, sequential=True,
        extra=[(dres, (tm, n), row), (xhat, (tm, n), row), (rstd, (tm, 1), row), (gamma, (1, n), vec)],
        outs=[(jax.ShapeDtypeStruct((m, n), F32), (tm, n), row),
              (jax.ShapeDtypeStruct((m, n), BF16), (tm, n), row),
              (jax.ShapeDtypeStruct((1, n), F32), (1, n), vec),
              (jax.ShapeDtypeStruct((1, n), F32), (1, n), vec)],
        epilogue=epi)


def _shift_down(x, k):
    row = lax.broadcasted_iota(jnp.int32, x.shape, 0)
    return jnp.where(row >= k, pltpu.roll(x, k, axis=0), 0.0)


def _shift_up(x, k):
    t = x.shape[0]
    row = lax.broadcasted_iota(jnp.int32, x.shape, 0)
    return jnp.where(row < t - k, pltpu.roll(x, t - k, axis=0), 0.0)


def _conv_silu_norm(x, w, normalise):
    kk = w.shape[0]
    c = x * w[kk - 1:kk, :]
    for j in range(kk - 1):
        c = c + _shift_down(x, kk - 1 - j) * w[j:j + 1, :]
    sg = _sigmoid(c)
    s = c * sg
    r = lax.rsqrt(jnp.sum(s * s, axis=-1, keepdims=True) + NORM_EPS)
    y = jnp.where(normalise, s * r, s)
    return c, sg, s, r, y


def _gdn_pre(proj, conv_w, heads):
    t = proj.shape[0]
    kk = conv_w.shape[0]

    def body(x_ref, w_ref, o_ref):
        normalise = pl.program_id(0) < 2
        o_ref[...] = _conv_silu_norm(x_ref[...], w_ref[...], normalise)[4]

    col = lambda s, h: (0, s * heads + h)
    return pl.pallas_call(
        body, name="gdn_pre", grid=(3, heads),
        in_specs=[pl.BlockSpec((t, HEAD_DIM), col), pl.BlockSpec((kk, HEAD_DIM), col)],
        out_specs=pl.BlockSpec((t, HEAD_DIM), col),
        out_shape=jax.ShapeDtypeStruct((t, 3 * heads * HEAD_DIM), F32),
        compiler_params=_params("parallel", "parallel"),
    )(proj, conv_w)


def _gdn_pre_backward(proj, conv_w, dqkv, heads):
    t = proj.shape[0]
    kk = conv_w.shape[0]

    def body(x_ref, w_ref, dy_ref, dx_ref, dw_ref):
        normalise = pl.program_id(0) < 2
        x = x_ref[...]
        w = w_ref[...]
        dy = dy_ref[...]
        c, sg, s, r, y = _conv_silu_norm(x, w, normalise)
        ds_norm = r * (dy - y * jnp.sum(dy * y, axis=-1, keepdims=True))
        ds = jnp.where(normalise, ds_norm, dy)
        dc = ds * (sg * (1.0 + c * (1.0 - sg)))
        dx = dc * w[kk - 1:kk, :]
        rows = [None] * kk
        rows[kk - 1] = jnp.sum(dc * x, axis=0, keepdims=True)
        for j in range(kk - 1):
            lag = kk - 1 - j
            dx = dx + _shift_up(dc, lag) * w[j:j + 1, :]
            rows[j] = jnp.sum(dc * _shift_down(x, lag), axis=0, keepdims=True)
        dx_ref[...] = dx.astype(BF16)
        dw_ref[...] = jnp.concatenate(rows, axis=0)

    col = lambda s, h: (0, s * heads + h)
    return pl.pallas_call(
        body, name="gdn_pre_bwd", grid=(3, heads),
        in_specs=[pl.BlockSpec((t, HEAD_DIM), col), pl.BlockSpec((kk, HEAD_DIM), col),
                  pl.BlockSpec((t, HEAD_DIM), col)],
        out_specs=[pl.BlockSpec((t, HEAD_DIM), col), pl.BlockSpec((kk, HEAD_DIM), col)],
        out_shape=[jax.ShapeDtypeStruct((t, 3 * heads * HEAD_DIM), BF16),
                   jax.ShapeDtypeStruct((kk, 3 * heads * HEAD_DIM), F32)],
        compiler_params=_params("parallel", "parallel"),
    )(proj, conv_w, dqkv)


def _gate_vectors(a_log, dt_bias, heads):
    pad = lambda v: jnp.pad(v.astype(F32), ((0, 0), (heads, HEAD_DIM - 2 * heads)))
    return pad(jnp.exp(a_log.astype(F32))), pad(dt_bias)


def _softplus(x):
    return jnp.maximum(x, 0.0) + jnp.log(1.0 + jnp.exp(-jnp.abs(x)))


def _gates_epilogue(heads):
    def epi(acc, ex, out, i):
        lane = lax.broadcasted_iota(jnp.int32, acc.shape, 1)
        beta = _sigmoid(acc)
        g = -ex[0][...] * _softplus(acc + ex[1][...])
        out[0][...] = acc
        out[1][...] = jnp.where(lane < heads, beta, jnp.where(lane < 2 * heads, g, 0.0))
    return epi


def _gates_backward(ba, bg, dbg, ea, dtb, heads):
    t = ba.shape[0]

    def body(ba_ref, bg_ref, d_ref, ea_ref, dt_ref, dba_ref, dal_ref, ddt_ref):
        lane = lax.broadcasted_iota(jnp.int32, (t, HEAD_DIM), 1)
        bgv = bg_ref[...]
        d = d_ref[...]
        db = d * bgv * (1.0 - bgv)
        da = -d * ea_ref[...] * _sigmoid(ba_ref[...] + dt_ref[...])
        is_g = jnp.logical_and(lane >= heads, lane < 2 * heads)
        dba = jnp.where(lane < heads, db, jnp.where(is_g, da, 0.0))
        dba_ref[...] = dba.astype(BF16)
        dal_ref[...] = jnp.sum(jnp.where(is_g, d * bgv, 0.0), axis=0, keepdims=True)
        ddt_ref[...] = jnp.sum(jnp.where(is_g, da, 0.0), axis=0, keepdims=True)

    full = pl.BlockSpec((t, HEAD_DIM), lambda: (0, 0))
    vec = pl.BlockSpec((1, HEAD_DIM), lambda: (0, 0))
    return pl.pallas_call(
        body, name="gates_bwd", grid=(),
        in_specs=[full, full, full, vec, vec], out_specs=[full, vec, vec],
        out_shape=[jax.ShapeDtypeStruct((t, HEAD_DIM), BF16), jax.ShapeDtypeStruct((1, HEAD_DIM), F32),
                   jax.ShapeDtypeStruct((1, HEAD_DIM), F32)],
        compiler_params=pltpu.CompilerParams(vmem_limit_bytes=VMEM_LIMIT),
    )(ba, bg, dbg, ea, dtb)


class _Chunk:
    pass


def _split2(x):
    hi = x.astype(BF16)
    return hi, (x - hi.astype(F32)).astype(BF16)


def _split3(x):
    hi = x.astype(BF16)
    rest = x - hi.astype(F32)
    mid = rest.astype(BF16)
    return hi, mid, (rest - mid.astype(F32)).astype(BF16)


def _dot_mask(mask, x, ta=False):
    hi, mid, lo = _split3(x)
    return _bdot(mask, hi, ta=ta) + (_bdot(mask, mid, ta=ta) + _bdot(mask, lo, ta=ta))


def _transpose_by_identity(x):
    r = x.shape[0]
    eye = (lax.broadcasted_iota(jnp.int32, (r, r), 0) == lax.broadcasted_iota(jnp.int32, (r, r), 1)).astype(BF16)
    hi, mid, lo = _split3(x)
    return _bdot(hi, eye, ta=True) + (_bdot(mid, eye, ta=True) + _bdot(lo, eye, ta=True))


def _dot22(a, b, ta=False, tb=False):
    ah, al = _split2(a)
    bh, bl = _split2(b)
    return _bdot(ah, bh, ta, tb) + (_bdot(ah, bl, ta, tb) + _bdot(al, bh, ta, tb))


def _chunk_gates(bg, heads):
    n = CHUNK
    row = lax.broadcasted_iota(jnp.int32, (n, n), 0)
    col = lax.broadcasted_iota(jnp.int32, (n, n), 1)
    lane = lax.broadcasted_iota(jnp.int32, bg.shape, 1)
    graw = jnp.where(jnp.logical_and(lane >= heads, lane < 2 * heads), bg, 0.0)
    gc = _dot_mask((row >= col).astype(BF16), graw)
    return gc, _transpose_by_identity(gc)


def _in_lockstep(generators):
    results = [None] * len(generators)
    live = list(enumerate(generators))
    while live:
        still = []
        for i, gen in live:
            try:
                next(gen)
                still.append((i, gen))
            except StopIteration as stop:
                results[i] = stop.value
        live = still
    return results


def _chunk_local(q, k, v, beta, gc, grow, solved=None):
    c = _Chunk()
    n = CHUNK
    row = lax.broadcasted_iota(jnp.int32, (n, n), 0)
    col = lax.broadcasted_iota(jnp.int32, (n, n), 1)
    c.tri = row >= col
    c.strict = row > col
    eye = row == col
    c.gcb = jnp.broadcast_to(gc, (n, HEAD_DIM))
    c.decay = jnp.where(c.tri, jnp.exp(jnp.where(c.tri, gc - grow, 0.0)), 0.0)
    c.eg = jnp.exp(c.gcb)
    glast = c.gcb[n - 1:n, :]
    c.egl = jnp.exp(glast)
    c.ekl = jnp.exp(glast - c.gcb)
    c.beta = beta
    c.q = q * (HEAD_DIM ** -0.5)
    c.k = k
    c.v = v
    c.kb = k * beta
    c.vb = v * beta
    c.kg = c.kb * c.eg
    both = _bdot(jnp.concatenate([c.kb, c.q], axis=0), k, tb=True)
    yield
    c.L = jnp.where(c.strict, both[:n] * c.decay, 0.0)
    c.A = jnp.where(c.tri, both[n:] * c.decay, 0.0)
    if solved is None:
        x = -c.L
        tinv = eye.astype(F32) + x
        p = _dot22(x, x)
        yield
        for _ in range(int(math.log2(n)) - 2):
            both = _dot22(jnp.concatenate([p, tinv], axis=0), p)
            yield
            p, tinv = both[:n], tinv + both[n:]
        c.T = tinv + _dot22(tinv, p)
        yield
        uw = _dot22(c.T, jnp.concatenate([c.vb, c.kg], axis=1))
        yield
        c.u, c.w = uw[:, :HEAD_DIM], uw[:, HEAD_DIM:]
    else:
        c.T, c.u, c.w = solved
    c.qg = c.q * c.eg
    c.kdec = k * c.ekl
    return c


def _gdn_core(qkv, bg, heads):
    t = qkv.shape[0]
    nchunk = t // CHUNK

    gw = heads * HEAD_DIM

    def body(qkv_ref, bg_ref, o_ref, s_ref, t_ref, u_ref, w_ref, state):
        @pl.when(pl.program_id(0) == 0)
        def _():
            state[...] = jnp.zeros_like(state)

        bg_v = bg_ref[...]
        gc_all, gc_rows = _chunk_gates(bg_v, heads)
        def one_head(h):
            col = lambda s: pl.ds(s * gw + h * HEAD_DIM, HEAD_DIM)
            c = yield from _chunk_local(qkv_ref[:, col(0)], qkv_ref[:, col(1)], qkv_ref[:, col(2)], bg_v[:, h:h + 1],
                                        gc_all[:, heads + h:heads + h + 1], gc_rows[heads + h:heads + h + 1, :])
            s0 = state[h]
            v_new = c.u - _bdot(c.w, s0)
            yield
            o = _bdot(c.qg, s0) + _bdot(c.A, v_new)
            return s0, o, s0 * c.egl + _bdot(c.kdec, v_new, ta=True), c

        results = _in_lockstep([one_head(h) for h in range(heads)])
        for h, (s0, o, s1, c) in enumerate(results):
            lanes = pl.ds(h * HEAD_DIM, HEAD_DIM)
            s_ref[h, 0] = s0
            o_ref[:, lanes] = o
            t_ref[:, lanes] = jnp.concatenate([c.T, jnp.zeros((CHUNK, HEAD_DIM - CHUNK), F32)], axis=1)
            u_ref[:, lanes] = c.u
            w_ref[:, lanes] = c.w
            state[h] = s1

    return pl.pallas_call(
        body, name="gdn_core", grid=(nchunk,),
        in_specs=[pl.BlockSpec((CHUNK, 3 * gw), lambda n: (n, 0)), pl.BlockSpec((CHUNK, HEAD_DIM), lambda n: (n, 0))],
        out_specs=[pl.BlockSpec((CHUNK, gw), lambda n: (n, 0)),
                   pl.BlockSpec((heads, 1, HEAD_DIM, HEAD_DIM), lambda n: (0, n, 0, 0))]
        + [pl.BlockSpec((CHUNK, gw), lambda n: (n, 0))] * 3,
        out_shape=[jax.ShapeDtypeStruct((t, gw), F32),
                   jax.ShapeDtypeStruct((heads, nchunk, HEAD_DIM, HEAD_DIM), F32)]
        + [jax.ShapeDtypeStruct((t, gw), F32)] * 3,
        scratch_shapes=[pltpu.VMEM((heads, HEAD_DIM, HEAD_DIM), F32)],
        compiler_params=_params("arbitrary"),
    )(qkv, bg)


def _gdn_core_backward(qkv, bg, states, solved, do, heads):
    t = qkv.shape[0]
    nchunk = t // CHUNK
    n = CHUNK

    def one_head(chunk_local, s0, d_out, ds1):
        c = yield from chunk_local
        v_new = c.u - _bdot(c.w, s0)
        dqg = _bdot(d_out, s0, tb=True)
        ds0 = _bdot(c.qg, d_out, ta=True) + ds1 * c.egl
        dv_new = _bdot(c.A, d_out, ta=True) + _bdot(c.kdec, ds1)
        yield
        dA = jnp.where(c.tri, _bdot(d_out, v_new, tb=True), 0.0)
        dkdec = _bdot(v_new, ds1, tb=True)
        dgl = jnp.sum(jnp.sum(ds1 * s0, axis=1, keepdims=True), axis=0, keepdims=True) * c.egl
        dw = -_bdot(dv_new, s0, tb=True)
        ds0 = ds0 - _bdot(c.w, dv_new, ta=True)
        yield
        both = _dot22(c.T, jnp.concatenate([dv_new, dw], axis=1), ta=True)
        yield
        dvb, dkg = both[:, :HEAD_DIM], both[:, HEAD_DIM:]
        dL = jnp.where(c.strict, -(_bdot(dvb, c.u, tb=True) + _bdot(dkg, c.w, tb=True)), 0.0)
        yield
        dm1 = dL * c.decay
        dkb = _bdot(dm1, c.k) + dkg * c.eg
        dk = _bdot(dm1, c.kb, ta=True)
        dm2 = dA * c.decay
        dq = _bdot(dm2, c.k) + dqg * c.eg
        dk = dk + _bdot(dm2, c.q, ta=True) + dkdec * c.ekl + dkb * c.beta
        pm = dL * c.L + dA * c.A
        ones = jnp.ones((n, HEAD_DIM), BF16)
        pm_hi, pm_lo = _split2(pm)
        colsum = _bdot(pm_hi, ones, ta=True) + _bdot(pm_lo, ones, ta=True)
        tk_ = jnp.sum(dkdec * c.kdec, axis=1, keepdims=True)
        dgc = (jnp.sum(pm, axis=1, keepdims=True) - colsum
               + jnp.sum(dqg * c.qg, axis=1, keepdims=True)
               - tk_
               + jnp.sum(dkg * c.kg, axis=1, keepdims=True))
        dgl = dgl + jnp.sum(tk_, axis=0, keepdims=True)
        rowi = lax.broadcasted_iota(jnp.int32, (n, HEAD_DIM), 0)
        dgc = dgc + jnp.where(rowi == n - 1, dgl, 0.0)
        dbeta = jnp.sum(dkb * c.k, axis=1, keepdims=True) + jnp.sum(dvb * c.v, axis=1, keepdims=True)
        return dq * (HEAD_DIM ** -0.5), dk, dvb * c.beta, dbeta, dgc, ds0

    gw = heads * HEAD_DIM

    def body(qkv_ref, bg_ref, s_ref, t_ref, u_ref, w_ref, do_ref, dqkv_ref, dbg_ref, dstate):
        @pl.when(pl.program_id(0) == 0)
        def _():
            dstate[...] = jnp.zeros_like(dstate)

        bg_v = bg_ref[...]
        gc_all, gc_rows = _chunk_gates(bg_v, heads)
        lane = lax.broadcasted_iota(jnp.int32, (n, HEAD_DIM), 1)
        dgates = jnp.zeros((n, HEAD_DIM), F32)
        chains = []
        for h in range(heads):
            col = lambda s: pl.ds(s * gw + h * HEAD_DIM, HEAD_DIM)
            lanes = pl.ds(h * HEAD_DIM, HEAD_DIM)
            c = _chunk_local(qkv_ref[:, col(0)], qkv_ref[:, col(1)], qkv_ref[:, col(2)], bg_v[:, h:h + 1],
                             gc_all[:, heads + h:heads + h + 1], gc_rows[heads + h:heads + h + 1, :],
                             (t_ref[:, pl.ds(h * HEAD_DIM, CHUNK)], u_ref[:, lanes], w_ref[:, lanes]))
            chains.append(one_head(c, s_ref[h, 0], do_ref[:, pl.ds(h * HEAD_DIM, HEAD_DIM)], dstate[h]))
        results = _in_lockstep(chains)
        for h, (dq, dk, dv, dbeta, dgc, ds0) in enumerate(results):
            dgates = jnp.where(lane == h, dbeta, jnp.where(lane == heads + h, dgc, dgates))
        for h, (dq, dk, dv, dbeta, dgc, ds0) in enumerate(results):
            dqkv_ref[:, pl.ds(h * HEAD_DIM, HEAD_DIM)] = dq
            dqkv_ref[:, pl.ds(gw + h * HEAD_DIM, HEAD_DIM)] = dk
            dqkv_ref[:, pl.ds(2 * gw + h * HEAD_DIM, HEAD_DIM)] = dv
            dstate[h] = ds0
        row = lax.broadcasted_iota(jnp.int32, (n, n), 0)
        colm = lax.broadcasted_iota(jnp.int32, (n, n), 1)
        draw = _dot_mask((row >= colm).astype(BF16), dgates, ta=True)
        dbg_ref[...] = jnp.where(lane < heads, dgates, draw)

    last = nchunk - 1
    return pl.pallas_call(
        body, name="gdn_core_bwd", grid=(nchunk,),
        in_specs=[pl.BlockSpec((CHUNK, 3 * gw), lambda i: (last - i, 0)),
                  pl.BlockSpec((CHUNK, HEAD_DIM), lambda i: (last - i, 0)),
                  pl.BlockSpec((heads, 1, HEAD_DIM, HEAD_DIM), lambda i: (0, last - i, 0, 0))]
        + [pl.BlockSpec((CHUNK, gw), lambda i: (last - i, 0))] * 4,
        out_specs=[pl.BlockSpec((CHUNK, 3 * gw), lambda i: (last - i, 0)),
                   pl.BlockSpec((CHUNK, HEAD_DIM), lambda i: (last - i, 0))],
        out_shape=[jax.ShapeDtypeStruct((t, 3 * gw), F32), jax.ShapeDtypeStruct((t, HEAD_DIM), F32)],
        scratch_shapes=[pltpu.VMEM((heads, HEAD_DIM, HEAD_DIM), F32)],
        compiler_params=_params("arbitrary"),
    )(qkv, bg, states, *solved, do)


def _gdn_post(o, proj, z_col0, norm_w, heads, tt):
    t = o.shape[0]
    zb = z_col0 // HEAD_DIM

    def body(o_ref, z_ref, w_ref, out_ref):
        ov = o_ref[...]
        z = z_ref[...]
        rms = lax.rsqrt(jnp.mean(ov * ov, axis=-1, keepdims=True) + NORM_EPS)
        out_ref[...] = (ov * rms * w_ref[...] * (z * _sigmoid(z))).astype(BF16)

    return pl.pallas_call(
        body, name="gdn_post", grid=(t // tt, heads),
        in_specs=[pl.BlockSpec((tt, HEAD_DIM), lambda i, h: (i, h)),
                  pl.BlockSpec((tt, HEAD_DIM), lambda i, h: (i, zb + h)),
                  pl.BlockSpec((1, HEAD_DIM), lambda i, h: (0, 0))],
        out_specs=pl.BlockSpec((tt, HEAD_DIM), lambda i, h: (i, h)),
        out_shape=jax.ShapeDtypeStruct((t, heads * HEAD_DIM), BF16),
        compiler_params=_params("parallel", "parallel"),
    )(o, proj, norm_w)


def _gdn_post_backward(dcat, o, proj, z_col0, norm_w, heads, tt):
    t = o.shape[0]
    zb = z_col0 // HEAD_DIM

    def body(d_ref, o_ref, z_ref, w_ref, do_ref, dz_ref, dw_ref):
        d = d_ref[...]
        ov = o_ref[...]
        z = z_ref[...]
        w = w_ref[...]
        rms = lax.rsqrt(jnp.mean(ov * ov, axis=-1, keepdims=True) + NORM_EPS)
        ohat = ov * rms
        sg = _sigmoid(z)
        gate = z * sg
        dz_ref[...] = (d * ohat * w * (sg * (1.0 + z * (1.0 - sg)))).astype(BF16)
        don = d * gate
        dohat = don * w
        do_ref[...] = rms * (dohat - ohat * jnp.mean(dohat * ohat, axis=-1, keepdims=True))
        dw = jnp.sum(don * ohat, axis=0, keepdims=True)
        first = jnp.logical_and(pl.program_id(0) == 0, pl.program_id(1) == 0)

        @pl.when(first)
        def _():
            dw_ref[...] = dw

        @pl.when(jnp.logical_not(first))
        def _():
            dw_ref[...] += dw

    blk = pl.BlockSpec((tt, HEAD_DIM), lambda i, h: (i, h))
    return pl.pallas_call(
        body, name="gdn_post_bwd", grid=(t // tt, heads),
        in_specs=[blk, blk, pl.BlockSpec((tt, HEAD_DIM), lambda i, h: (i, zb + h)),
                  pl.BlockSpec((1, HEAD_DIM), lambda i, h: (0, 0))],
        out_specs=[blk, blk, pl.BlockSpec((1, HEAD_DIM), lambda i, h: (0, 0))],
        out_shape=[jax.ShapeDtypeStruct((t, heads * HEAD_DIM), F32),
                   jax.ShapeDtypeStruct((t, heads * HEAD_DIM), BF16),
                   jax.ShapeDtypeStruct((1, HEAD_DIM), F32)],
        compiler_params=_params("arbitrary", "arbitrary"),
    )(dcat, o, proj, norm_w)


def _pool_select(levels, group):
    out = levels[-1]
    for gi in range(len(levels) - 2, -1, -1):
        out = jnp.where(group == gi, levels[gi], out)
    return out


def _pool_counts(t, width, group):
    pos = lax.broadcasted_iota(jnp.int32, (t, width), 0)
    win = jnp.left_shift(2, group)
    return jnp.minimum(pos + 1, win).astype(F32)


def _pooled(p, group):
    levels, s, step = [], p, 1
    for _ in POOL_WINDOWS:
        s = s + _shift_down(s, step)
        levels.append(s)
        step *= 2
    cnt = _pool_counts(p.shape[0], p.shape[1], group)
    return _pool_select(levels, group) / cnt - p, cnt


def _pool_forward(proj, p_col0, pool_w, pool_scale):
    t = proj.shape[0]
    groups, cg, _ = pool_w.shape
    pb = p_col0 // cg

    def body(p_ref, w_ref, s_ref, o_ref):
        pooled, _ = _pooled(p_ref[...], pl.program_id(0))
        o_ref[...] = (_bdot(pooled, w_ref[0]) * s_ref[...]).astype(BF16)

    return pl.pallas_call(
        body, name="pool_fwd", grid=(groups,),
        in_specs=[pl.BlockSpec((t, cg), lambda g: (0, pb + g)), pl.BlockSpec((1, cg, cg), lambda g: (g, 0, 0)),
                  pl.BlockSpec((1, cg), lambda g: (0, g))],
        out_specs=pl.BlockSpec((t, cg), lambda g: (0, g)),
        out_shape=jax.ShapeDtypeStruct((t, groups * cg), BF16),
        compiler_params=_params("parallel"),
    )(proj, pool_w, pool_scale)


def _pool_backward(dcat, d_col0, proj, p_col0, pool_w, pool_scale):
    t = proj.shape[0]
    groups, cg, _ = pool_w.shape
    pb = p_col0 // cg
    db = d_col0 // cg

    def body(d_ref, p_ref, w_ref, s_ref, dp_ref, dw_ref, ds_ref):
        group = pl.program_id(0)
        pooled, cnt = _pooled(p_ref[...], group)
        w = w_ref[0]
        d = d_ref[...]
        mixed = _bdot(pooled, w)
        ds_ref[...] = jnp.sum(d * mixed, axis=0, keepdims=True)
        dmixed = d * s_ref[...]
        dw_ref[0] = _bdot(pooled, dmixed, ta=True)
        dpooled = _bdot(dmixed, w, tb=True)
        levels, s, step = [], dpooled / cnt, 1
        for _ in POOL_WINDOWS:
            s = s + _shift_up(s, step)
            levels.append(s)
            step *= 2
        dp_ref[...] = (_pool_select(levels, group) - dpooled).astype(BF16)

    return pl.pallas_call(
        body, name="pool_bwd", grid=(groups,),
        in_specs=[pl.BlockSpec((t, cg), lambda g: (0, db + g)), pl.BlockSpec((t, cg), lambda g: (0, pb + g)),
                  pl.BlockSpec((1, cg, cg), lambda g: (g, 0, 0)), pl.BlockSpec((1, cg), lambda g: (0, g))],
        out_specs=[pl.BlockSpec((t, cg), lambda g: (0, g)), pl.BlockSpec((1, cg, cg), lambda g: (g, 0, 0)),
                   pl.BlockSpec((1, cg), lambda g: (0, g))],
        out_shape=[jax.ShapeDtypeStruct((t, groups * cg), BF16), jax.ShapeDtypeStruct((groups, cg, cg), F32),
                   jax.ShapeDtypeStruct((1, groups * cg), F32)],
        compiler_params=_params("parallel"),
    )(dcat, proj, pool_w, pool_scale)


def _attention(q, k, v, tq):
    t, d = q.shape
    m = k.shape[0]
    dh = d // XATTN_HEADS
    scale = dh ** -0.5

    def body(q_ref, k_ref, v_ref, o_ref):
        s = _bdot(q_ref[...], k_ref[...], tb=True) * scale
        s = s - jnp.max(s, axis=-1, keepdims=True)
        e = jnp.exp(s)
        p = e / jnp.sum(e, axis=-1, keepdims=True)
        o_ref[...] = _bdot(p, v_ref[...]).astype(BF16)

    return pl.pallas_call(
        body, name="xattn_fwd", grid=(XATTN_HEADS, t // tq),
        in_specs=[pl.BlockSpec((tq, dh), lambda h, i: (i, h)), pl.BlockSpec((m, dh), lambda h, i: (0, h)),
                  pl.BlockSpec((m, dh), lambda h, i: (0, h))],
        out_specs=pl.BlockSpec((tq, dh), lambda h, i: (i, h)),
        out_shape=jax.ShapeDtypeStruct((t, d), BF16),
        compiler_params=_params("parallel", "parallel"),
    )(q, k, v)


def _attention_backward(q, k, v, do, tq):
    t, d = q.shape
    m = k.shape[0]
    dh = d // XATTN_HEADS
    scale = dh ** -0.5

    def body(q_ref, k_ref, v_ref, do_ref, dq_ref, dk_ref, dv_ref, dk_acc, dv_acc):
        i = pl.program_id(1)
        qv, kv, vv, dov = q_ref[...], k_ref[...], v_ref[...], do_ref[...]
        s = _bdot(qv, kv, tb=True) * scale
        s = s - jnp.max(s, axis=-1, keepdims=True)
        e = jnp.exp(s)
        p = e / jnp.sum(e, axis=-1, keepdims=True)
        dp = _bdot(dov, vv, tb=True)
        ds = p * (dp - jnp.sum(dp * p, axis=-1, keepdims=True)) * scale
        dq_ref[...] = _bdot(ds, kv).astype(BF16)
        dv_part = _bdot(p, dov, ta=True)
        dk_part = _bdot(ds, qv, ta=True)

        @pl.when(i == 0)
        def _():
            dk_acc[...] = dk_part
            dv_acc[...] = dv_part

        @pl.when(i > 0)
        def _():
            dk_acc[...] += dk_part
            dv_acc[...] += dv_part

        @pl.when(i == pl.num_programs(1) - 1)
        def _():
            dk_ref[...] = dk_acc[...].astype(BF16)
            dv_ref[...] = dv_acc[...].astype(BF16)

    qblk = pl.BlockSpec((tq, dh), lambda h, i: (i, h))
    kblk = pl.BlockSpec((m, dh), lambda h, i: (0, h))
    return pl.pallas_call(
        body, name="xattn_bwd", grid=(XATTN_HEADS, t // tq),
        in_specs=[qblk, kblk, kblk, qblk],
        out_specs=[qblk, kblk, kblk],
        out_shape=[jax.ShapeDtypeStruct((t, d), BF16), jax.ShapeDtypeStruct((m, d), BF16),
                   jax.ShapeDtypeStruct((m, d), BF16)],
        scratch_shapes=[pltpu.VMEM((m, dh), F32), pltpu.VMEM((m, dh), F32)],
        compiler_params=_params("parallel", "arbitrary"),
    )(q, k, v, do)


def _ln_backward_rows(name, dmain, dres, xhat, rstd, gamma, tm):
    t, d = xhat.shape

    def body(m_ref, r_ref, x_ref, s_ref, g_ref, du_ref, dub_ref, dg_ref, db_ref):
        du, dg, db = _ln_backward_math(m_ref[...] + ALPHA * r_ref[...], x_ref[...], s_ref[...], g_ref[...])
        du_ref[...] = du
        dub_ref[...] = du.astype(BF16)
        first = pl.program_id(0) == 0

        @pl.when(first)
        def _():
            dg_ref[...] = dg
            db_ref[...] = db

        @pl.when(jnp.logical_not(first))
        def _():
            dg_ref[...] += dg
            db_ref[...] += db

    row = pl.BlockSpec((tm, d), lambda i: (i, 0))
    vec = pl.BlockSpec((1, d), lambda i: (0, 0))
    return pl.pallas_call(
        body, name=name, grid=(t // tm,),
        in_specs=[row, row, row, pl.BlockSpec((tm, 1), lambda i: (i, 0)), vec],
        out_specs=[row, row, vec, vec],
        out_shape=[jax.ShapeDtypeStruct((t, d), F32), jax.ShapeDtypeStruct((t, d), BF16),
                   jax.ShapeDtypeStruct((1, d), F32), jax.ShapeDtypeStruct((1, d), F32)],
        compiler_params=_params("arbitrary"),
    )(dmain, dres, xhat, rstd, gamma)


def _loss_and_ln_backward(xhat, rstd, gamma, beta, target, tm):
    t, d = xhat.shape

    def body(x_ref, r_ref, g_ref, b_ref, t_ref, du_ref, dub_ref, dg_ref, db_ref, loss_ref):
        xh = x_ref[...]
        g = g_ref[...]
        diff = xh * g + b_ref[...] - t_ref[...]
        part = jnp.sum(jnp.sum(diff * diff, axis=1, keepdims=True), axis=0, keepdims=True) * (0.5 / d)
        dy = diff * (1.0 / d)
        du, dg, db = _ln_backward_math(dy, xh, r_ref[...], g)
        du_ref[...] = du
        dub_ref[...] = du.astype(BF16)
        lossrow = jnp.broadcast_to(part, (1, HEAD_DIM))
        first = pl.program_id(0) == 0

        @pl.when(first)
        def _():
            dg_ref[...] = dg
            db_ref[...] = db
            loss_ref[...] = lossrow

        @pl.when(jnp.logical_not(first))
        def _():
            dg_ref[...] += dg
            db_ref[...] += db
            loss_ref[...] += lossrow

    row = pl.BlockSpec((tm, d), lambda i: (i, 0))
    vec = pl.BlockSpec((1, d), lambda i: (0, 0))
    return pl.pallas_call(
        body, name="loss_ln3_bwd", grid=(t // tm,),
        in_specs=[row, pl.BlockSpec((tm, 1), lambda i: (i, 0)), vec, vec, row],
        out_specs=[row, row, vec, vec, pl.BlockSpec((1, HEAD_DIM), lambda i: (0, 0))],
        out_shape=[jax.ShapeDtypeStruct((t, d), F32), jax.ShapeDtypeStruct((t, d), BF16),
                   jax.ShapeDtypeStruct((1, d), F32), jax.ShapeDtypeStruct((1, d), F32),
                   jax.ShapeDtypeStruct((1, HEAD_DIM), F32)],
        compiler_params=_params("arbitrary"),
    )(xhat, rstd, gamma, beta, target)


def _after(token, a):
    return a if token is None else a + token[:1, :1].astype(a.dtype)


def _pick(n, prefs):
    for p in prefs:
        if n % p == 0:
            return p
    return n


def _local_step(x, mem, target, w, token=None):
    t, d = x.shape
    heads = w["a_log"].shape[1]
    gw = heads * HEAD_DIM
    groups, cg, _ = w["pool_w"].shape
    pw = groups * cg
    n_main = 4 * gw + pw
    in_cols = n_main + 2 * heads
    s_in = w["w_in_t"].shape[0]

    tm = _pick(t, (512, 256, 128))
    tm_ln = _pick(t, (256, 128))
    tm_big = _pick(t, (1024, 512, 256, 128))
    tk = _pick(d, K_STEPS)

    w_in_t = w["w_in_t"].reshape(in_cols, d)
    w_p_t = w_in_t[4 * gw + 2 * heads:]
    w_ba_t = jnp.pad(w_in_t[4 * gw:4 * gw + 2 * heads], ((0, HEAD_DIM - 2 * heads), (0, 0)))
    x_bf = _after(token, x).astype(BF16)
    mem_bf = _after(token, mem).astype(BF16)

    tn_d = _pick(d, (1024, 512, 256, 128))
    proj = _plain("proj_main", x_bf, w_in_t, tb=True, n_used=4 * gw, tm=tm_big, tn=_pick(4 * gw, (1024, 512, 256, 128)),
                  tk=tk, out_dtype=F32)
    pproj = _plain("proj_pool", x_bf, w_p_t, tb=True, tm=tm_big, tn=_pick(pw, (1024, 512, 256, 128)), tk=tk, out_dtype=F32)
    ea, dtb = _gate_vectors(w["a_log"], w["dt_bias"], heads)
    vec128 = lambda i, j: (0, 0)
    ba, bg = _matmul(
        "proj_gates", x_bf, w_ba_t, tb=True, tm=tm, tn=HEAD_DIM, tk=tk,
        extra=[(ea, (1, HEAD_DIM), vec128), (dtb, (1, HEAD_DIM), vec128)],
        outs=[(jax.ShapeDtypeStruct((t, HEAD_DIM), F32), (tm, HEAD_DIM), _tile)] * 2,
        epilogue=_gates_epilogue(heads))
    qkv = _gdn_pre(proj, w["conv_w"], heads)
    o_gdn, states, *solved = _gdn_core(qkv, bg, heads)
    cat_g = _gdn_post(o_gdn, proj, 3 * gw, w["gdn_norm_w"], heads, tm)
    cat_p = _pool_forward(pproj, 0, w["pool_w"], w["pool_scale"])
    cat = jnp.concatenate([cat_g, cat_p], axis=1)
    w = {**w, **(yield ("weights", 1, cat))}
    h1, h1_bf, xhat1, rstd1 = _ln_forward("mix_ln1", cat, w["w_out"], x, w["ln1_g"], w["ln1_b"], tm=tm_ln, tk=tk)

    q = _plain("xattn_q", h1_bf, w["xq_w"], tm=tm, tn=tn_d, tk=tk, out_dtype=BF16)
    mlen = mem.shape[0]
    tm_mem = _pick(mlen, (256, 128))
    k = _plain("xattn_k", mem_bf, w["xk_w"], tm=tm_mem, tn=tn_d, tk=tk, out_dtype=BF16)
    v = _plain("xattn_v", mem_bf, w["xv_w"], tm=tm_mem, tn=tn_d, tk=tk, out_dtype=BF16)
    att = _attention(q, k, v, tm)
    token = yield ("relay", None, att)
    h2, h2_bf, xhat2, rstd2 = _ln_forward("xo_ln2", att, w["xo_w"], h1, _after(token, w["ln2_g"]), w["ln2_b"],
                                          tm=tm_ln, tk=tk)

    w = {**w, **(yield ("weights", 2, h2_bf))}
    s_up = w["w_up3"].shape[0]
    ff = s_up * w["w_up3"].shape[2]
    tn_f = _pick(ff // s_up, (1024, 512, 256, 128))

    def up_epi(acc, ex, out, i):
        r = jnp.maximum(acc, 0.0)
        out[0][...] = (r * r).astype(BF16)
        out[1][...] = (2.0 * r).astype(BF16)

    act, act_grad = _matmul(
        "mlp_up", h2_bf, w["w_up3"], b_blocks=s_up, tm=tm_big, tn=tn_f, tk=tk,
        outs=[(jax.ShapeDtypeStruct((t, ff), BF16), (tm_big, tn_f), _tile)] * 2, epilogue=up_epi)
    w = {**w, **(yield ("weights", 3, act))}
    tk_f = _pick(ff, K_STEPS)
    xhat3, rstd3 = _ln_forward("down_ln3", act, w["w_down"], h2, w["ln3_g"], w["ln3_b"], tm=tm, tk=tk_f, want_h=False)

    grads = {}
    du3, du3_bf, grads["ln3_g"], grads["ln3_b"], loss = _loss_and_ln_backward(
        xhat3, rstd3, w["ln3_g"], w["ln3_b"], target, tm_ln)

    def dup_epi(acc, ex, out, i):
        out[0][...] = (acc * ex[0][...].astype(F32)).astype(BF16)

    dup = _matmul(
        "mlp_down_dx", du3_bf, w["w_down"], tb=True, tm=tm_big, tn=tn_f, tk=tk,
        extra=[(act_grad, (tm_big, tn_f), _tile)],
        outs=[(jax.ShapeDtypeStruct((t, ff), BF16), (tm_big, tn_f), _tile)], epilogue=dup_epi)[0]
    tk_t = _pick(t, K_STEPS)
    tm_w = _pick(d, (512, 256, 128))
    grads["w_down"] = _plain("mlp_down_dw", act, du3_bf, ta=True, tm=_pick(ff, (512, 256, 128)), tn=d, tk=tk_t,
                             out_dtype=F32)
    grads["w_up3"] = _plain("mlp_up_dw", h2_bf, dup, ta=True, tm=tm_w, tn=ff // s_up, tk=tk_t, out_dtype=F32, out3=s_up,
                            n_outer=True)
    token = yield ("grads", 0, {n: grads.pop(n) for n in ("w_down", "w_up3")})
    dh2 = _plain("mlp_up_dx", dup, w["w_up3"], tb=True, b_blocks=s_up, tm=tm_big, tn=tn_d,
                 tk=_pick(ff // s_up, K_STEPS), out_dtype=F32)
    du2, du2_bf, grads["ln2_g"], grads["ln2_b"] = _ln_backward_rows(
        "ln2_bwd", dh2, du3, xhat2, rstd2, _after(token, w["ln2_g"]), tm_ln)
    token = yield ("poll", 0, du2_bf)

    grads["xo_w"] = _plain("xo_dw", att, du2_bf, ta=True, tm=tm_w, tn=d, tk=tk_t, out_dtype=F32)
    datt = _plain("xo_dx", du2_bf, w["xo_w"], tb=True, tm=tm, tn=tn_d, tk=tk, out_dtype=BF16)
    dq, dk, dv = _attention_backward(q, k, v, datt, tm)
    tk_m = _pick(mlen, (256, 128))
    grads["xq_w"] = _plain("xq_dw", h1_bf, dq, ta=True, tm=tm_w, tn=d, tk=tk_t, out_dtype=F32)
    grads["xk_w"] = _plain("xk_dw", mem_bf, dk, ta=True, tm=tm_w, tn=tn_d, tk=tk_m, out_dtype=F32)
    grads["xv_w"] = _plain("xv_dw", mem_bf, dv, ta=True, tm=tm_w, tn=tn_d, tk=tk_m, out_dtype=F32)
    du1, du1_bf, grads["ln1_g"], grads["ln1_b"] = _ln_backward(
        "xq_dx_ln1", dq, w["xq_w"], du2, xhat1, rstd1, _after(token, w["ln1_g"]), tm=tm_ln, tk=tk)

    grads["w_out"] = _plain("out_dw", cat, du1_bf, ta=True, tm=tm_w, tn=d, tk=tk_t, out_dtype=F32)
    token = yield ("grads", 1, {n: grads.pop(n) for n in ("xo_w", "xq_w", "xk_w", "xv_w", "w_out")})
    dcat = _plain("out_dx", du1_bf, w["w_out"], tb=True, tm=tm, tn=tn_d, tk=tk, out_dtype=F32)
    dp, grads["pool_w"], grads["pool_scale"] = _pool_backward(dcat, gw, pproj, 0, w["pool_w"],
                                                              _after(token, w["pool_scale"]))
    do_gdn, dz, grads["gdn_norm_w"] = _gdn_post_backward(dcat, o_gdn, proj, 3 * gw, _after(token, w["gdn_norm_w"]),
                                                         heads, tm)
    dqkv, dbg = _gdn_core_backward(qkv, bg, states, solved, do_gdn, heads)
    token = yield ("poll", 1, dqkv)
    dqkv_pre, grads["conv_w"] = _gdn_pre_backward(proj, _after(token, w["conv_w"]), dqkv, heads)
    dba, dalog_row, ddt_row = _gates_backward(ba, bg, dbg, ea, dtb, heads)
    grads["a_log"] = dalog_row[:, heads:2 * heads]
    grads["dt_bias"] = ddt_row[:, heads:2 * heads]

    dproj = jnp.concatenate([dqkv_pre, dz, dp], axis=1)
    dw_main = _plain("proj_dw", dproj, x_bf, ta=True, tm=_pick(n_main, (512, 256, 128)), tn=d, tk=tk_t, out_dtype=F32)
    dw_ba = _plain("proj_gates_dw", dba, x_bf, ta=True, tm=HEAD_DIM, tn=tn_d, tk=tk_t, out_dtype=F32)
    dw_in_t = jnp.concatenate([dw_main[:4 * gw], dw_ba[:2 * heads], dw_main[4 * gw:]], axis=0)
    grads["w_in_t"] = dw_in_t.reshape(s_in, in_cols // s_in, d)

    def dx_epi(acc, ex, out, i):
        out[0][...] = acc + ex[1][...] + ALPHA * ex[0][...]

    def add_epi(acc, ex, out, i):
        out[0][...] = acc + ex[0][...]

    token = yield ("grads", 2, {n: grads.pop(n) for n in ("w_in_t", "pool_w")})
    dx_gates = _plain("proj_gates_dx", dba, _after(token, w_ba_t), tm=tm, tn=tn_d, tk=HEAD_DIM, out_dtype=F32)
    out_tile = [(jax.ShapeDtypeStruct((t, d), F32), (tm, tn_d), _tile)]
    dx_pool = _matmul("proj_pool_dx", dp, w_p_t, tm=tm, tn=tn_d, tk=_pick(pw, K_STEPS),
                      extra=[(dx_gates, (tm, tn_d), _tile)], outs=out_tile, epilogue=add_epi)[0]
    grad_x = _matmul(
        "proj_dx", dproj, w_in_t, k_used=4 * gw, tm=tm, tn=tn_d, tk=_pick(4 * gw, K_STEPS),
        extra=[(du1, (tm, tn_d), _tile), (dx_pool, (tm, tn_d), _tile)], outs=out_tile, epilogue=dx_epi)[0]
    yield ("poll", 2, grad_x)
    return loss, grad_x, grads


def _adamw(name, w, g, m, v):
    r, c = w.shape
    if r % 8 == 0:
        tr = _pick(r, (256, 128, 64, 32, 16, 8))
        blk, steps = pl.BlockSpec((tr, c), lambda i: (i, 0)), r // tr
    else:
        tc = _pick(c, (256, 128))
        blk, steps = pl.BlockSpec((r, tc), lambda i: (0, i)), c // tc
    c1 = 1.0 - ADAM_B1 ** ADAM_STEP
    c2 = 1.0 - ADAM_B2 ** ADAM_STEP

    def body(w_ref, g_ref, m_ref, v_ref, d_ref, mo_ref, vo_ref):
        gv = g_ref[...]
        mn = ADAM_B1 * m_ref[...] + (1.0 - ADAM_B1) * gv
        vn = ADAM_B2 * v_ref[...] + (1.0 - ADAM_B2) * (gv * gv)
        d_ref[...] = -ADAM_LR * ((mn / c1) / (jnp.sqrt(vn / c2) + ADAM_EPS) + ADAM_WD * w_ref[...])
        mo_ref[...] = mn
        vo_ref[...] = vn

    return pl.pallas_call(
        body, name=name, grid=(steps,), in_specs=[blk] * 4, out_specs=[blk] * 3,
        out_shape=[jax.ShapeDtypeStruct((r, c), F32)] * 3,
        compiler_params=_params("parallel"),
    )(w, g, m, v)


def _place():
    x, y, c = lax.axis_index("x"), lax.axis_index("y"), lax.axis_index("c")
    chips = [(1 - x, y), (x, 1 - y), (1 - x, 1 - y)]
    return x, y, c, chips


HBM = pl.BlockSpec(memory_space=pltpu.HBM)


SEM = pl.BlockSpec(memory_space=pltpu.SEMAPHORE)
ANY = pl.BlockSpec(memory_space=pl.ANY)
EFFECT = pltpu.SideEffectType.DATAFLOW_SIDE_EFFECTING


def _in_hbm(a):
    return pltpu.with_memory_space_constraint(a, pltpu.HBM)


def _remote(src, dst, send_sem, recv_sem, to):
    return pltpu.make_async_remote_copy(src_ref=src, dst_ref=dst, send_sem=send_sem, recv_sem=recv_sem,
                                        device_id=to, device_id_type=MESH)


def _by_rows(rows):
    return rows % 32 == 0


def _half_shape(rows, cols):
    return (rows // 2, cols) if _by_rows(rows) else (rows, cols // 2)


def _half(ref, which, *lead):
    rows, cols = ref.shape[-2:]
    if _by_rows(rows):
        return ref.at[(*lead, pl.ds(which * (rows // 2), rows // 2))]
    return ref.at[(*lead, slice(None), pl.ds(which * (cols // 2), cols // 2))]


def _landed(lands, i, shard_index, which):
    return _half(lands[i], which, shard_index)


def _routes():
    x, y, c, _ = _place()
    first = (jnp.where(c == 0, 1 - x, x), jnp.where(c == 0, y, 1 - y))
    second = (jnp.where(c == 0, x, 1 - x), jnp.where(c == 0, 1 - y, y))
    return first, second, (1 - x, 1 - y)


def _shard_of(chip):
    return 2 * chip[0] + chip[1]


def _gather_start(name, shards, after, relayed=()):
    n = len(shards)
    lands = [lax.empty((N_SHARD,) + s.shape, s.dtype) for s in shards]

    def body(*refs):
        ins, zones = refs[:n], refs[n:2 * n]
        ici_send, ici_recv, own_send, own_recv = refs[2 * n + 1:2 * n + 5]
        token = refs[-1]
        x, y, c, chips = _place()
        me = 2 * x + y
        first, _, _ = _routes()
        for i in range(n):
            if i in relayed:
                _remote(_half(ins[i], c), _landed(zones, i, me, c), ici_send.at[3 * i], ici_recv.at[3 * i],
                        (*first, c)).start()
                continue
            for j, chip in enumerate(chips):
                _remote(_half(ins[i], c), _landed(zones, i, me, c), ici_send.at[3 * i + j],
                        ici_recv.at[3 * i + j], (*chip, c)).start()
        for i in range(n):
            _remote(ins[i], zones[i].at[me], own_send.at[i], own_recv.at[i], (x, y, 1 - c)).start()
        token[...] = jnp.zeros_like(token)

    dma = pltpu.SemaphoreType.DMA
    outs = pl.pallas_call(
        body, name=name,
        in_specs=[HBM] * (2 * n) + [ANY],
        out_shape=(dma((3 * n,)), dma((3 * n,)), dma((n,)), dma((n,)),
                   *[pltpu.HBM(a.shape, a.dtype) for a in shards + lands], jax.ShapeDtypeStruct((8, LANES), F32)),
        out_specs=(SEM, SEM, SEM, SEM, *[HBM] * (2 * n), pl.BlockSpec(memory_space=pltpu.VMEM)),
        input_output_aliases={k: 4 + k for k in range(2 * n)},
        compiler_params=pltpu.CompilerParams(has_side_effects=EFFECT),
    )(*[_in_hbm(a) for a in shards + lands], after)
    sems = dict(zip(("ici_send", "ici_recv", "own_send", "own_recv"), outs[:4]))
    return sems, list(outs[4:4 + n]), list(outs[4 + n:4 + 2 * n]), outs[-1]


def _gather_forward(name, idx, lands, sems, after):
    n = len(idx)

    def body(*refs):
        zones = refs[:n]
        ici_recv = refs[n]
        fwd_send, fwd_recv = refs[n + 2], refs[n + 3]
        x, y, c, chips = _place()
        for k, i in enumerate(idx):
            for j, chip in enumerate(chips):
                half = _landed(zones, k, 2 * chip[0] + chip[1], c)
                _remote(half, half, fwd_send.at[3 * k + j], ici_recv.at[3 * i + j], (*chip, c)).wait_recv()
                _remote(half, half, fwd_send.at[3 * k + j], fwd_recv.at[3 * k + j], (x, y, 1 - c)).start()

    dma = pltpu.SemaphoreType.DMA
    outs = pl.pallas_call(
        body, name=name,
        in_specs=[HBM] * n + [SEM, ANY],
        out_shape=(dma((3 * n,)), dma((3 * n,)), *[pltpu.HBM(a.shape, a.dtype) for a in lands]),
        out_specs=(SEM, SEM, *[HBM] * n),
        input_output_aliases={k: 2 + k for k in range(n)},
        compiler_params=pltpu.CompilerParams(has_side_effects=EFFECT),
    )(*lands, sems["ici_recv"], after)
    return (outs[0], outs[1]), list(outs[2:])


def _gather_wait(name, idx, shards, lands, sems, fwd, after):
    n = len(idx)

    def body(*refs):
        ins, zones = refs[:n], refs[n:2 * n]
        ici_send, own_send, own_recv, fwd_send, fwd_recv = refs[2 * n:2 * n + 5]
        x, y, c, chips = _place()
        me = 2 * x + y
        for k, i in enumerate(idx):
            mine = _half(ins[k], c)
            for j, chip in enumerate(chips):
                theirs = 2 * chip[0] + chip[1]
                _remote(mine, _landed(zones, k, me, c), ici_send.at[3 * i + j], fwd_recv.at[3 * k + j],
                        (*chip, c)).wait_send()
                sent = _landed(zones, k, theirs, c)
                _remote(sent, sent, fwd_send.at[3 * k + j], fwd_recv.at[3 * k + j], (x, y, 1 - c)).wait_send()
                passed = _landed(zones, k, theirs, 1 - c)
                _remote(passed, passed, fwd_send.at[3 * k + j], fwd_recv.at[3 * k + j], (x, y, 1 - c)).wait_recv()
            own = _remote(ins[k], zones[k].at[me], own_send.at[i], own_recv.at[i], (x, y, 1 - c))
            own.wait_send()
            own.wait_recv()

    outs = pl.pallas_call(
        body, name=name,
        in_specs=[HBM] * (2 * n) + [SEM] * 5 + [ANY],
        out_shape=tuple(pltpu.HBM(a.shape, a.dtype) for a in lands),
        out_specs=tuple([HBM] * n),
        input_output_aliases={n + k: k for k in range(n)},
        compiler_params=pltpu.CompilerParams(has_side_effects=EFFECT),
    )(*shards, *lands, sems["ici_send"], sems["own_send"], sems["own_recv"], fwd[0], fwd[1], after)
    return list(outs)


def _gather_relay(name, idx, shards, lands, sems, after):
    n = len(idx)

    def body(*refs):
        ins, zones, ici_recv = refs[:n], refs[n:2 * n], refs[2 * n]
        relay_send, relay_recv, pass_send, pass_recv = refs[2 * n + 2:2 * n + 6]
        x, y, c, _ = _place()
        first, second, _ = _routes()
        for k, i in enumerate(idx):
            landed = _landed(zones, k, _shard_of(first), c)
            _remote(landed, landed, pass_send.at[k], ici_recv.at[3 * i], (*first, c)).wait_recv()
            _remote(_half(ins[k], c), _landed(zones, k, 2 * x + y, c), relay_send.at[2 * k], relay_recv.at[2 * k],
                    (*second, c)).start()
            _remote(landed, landed, relay_send.at[2 * k + 1], relay_recv.at[2 * k + 1], (*second, c)).start()
            _remote(landed, landed, pass_send.at[k], pass_recv.at[k], (x, y, 1 - c)).start()
        refs[-1][...] = jnp.zeros_like(refs[-1])

    dma = pltpu.SemaphoreType.DMA
    outs = pl.pallas_call(
        body, name=name,
        in_specs=[HBM] * (2 * n) + [SEM, ANY],
        out_shape=(dma((2 * n,)), dma((2 * n,)), dma((n,)), dma((n,)), *[pltpu.HBM(a.shape, a.dtype) for a in lands],
                   jax.ShapeDtypeStruct((8, LANES), F32)),
        out_specs=(SEM, SEM, SEM, SEM, *[HBM] * n, pl.BlockSpec(memory_space=pltpu.VMEM)),
        input_output_aliases={n + k: 4 + k for k in range(n)},
        compiler_params=pltpu.CompilerParams(has_side_effects=EFFECT),
    )(*shards, *lands, sems["ici_recv"], after)
    return outs[:4], list(outs[4:4 + n]), outs[-1]


def _gather_forward_relayed(name, ks, lands, relay, after):
    n = len(ks)

    def body(*refs):
        zones, relay_recv = refs[:n], refs[n]
        fwd_send, fwd_recv = refs[n + 2], refs[n + 3]
        x, y, c, _ = _place()
        _, second, diagonal = _routes()
        for p, k in enumerate(ks):
            for j, chip in enumerate((second, diagonal)):
                landed = _landed(zones, p, _shard_of(chip), c)
                _remote(landed, landed, fwd_send.at[2 * p + j], relay_recv.at[2 * k + j], (*second, c)).wait_recv()
                _remote(landed, landed, fwd_send.at[2 * p + j], fwd_recv.at[2 * p + j], (x, y, 1 - c)).start()

    dma = pltpu.SemaphoreType.DMA
    outs = pl.pallas_call(
        body, name=name,
        in_specs=[HBM] * n + [SEM, ANY],
        out_shape=(dma((2 * n,)), dma((2 * n,)), *[pltpu.HBM(a.shape, a.dtype) for a in lands]),
        out_specs=(SEM, SEM, *[HBM] * n),
        input_output_aliases={k: 2 + k for k in range(n)},
        compiler_params=pltpu.CompilerParams(has_side_effects=EFFECT),
    )(*lands, relay[1], after)
    return (outs[0], outs[1]), list(outs[2:])


def _gather_wait_relayed(name, idx, ks, shards, lands, sems, relay, fwd, after):
    n = len(idx)

    def body(*refs):
        ins, zones = refs[:n], refs[n:2 * n]
        ici_send, own_send, own_recv, relay_send, pass_send, pass_recv, fwd_send, fwd_recv = refs[2 * n:2 * n + 8]
        x, y, c, _ = _place()
        me = 2 * x + y
        sibling = (x, y, 1 - c)
        first, second, diagonal = _routes()
        for p, (i, k) in enumerate(zip(idx, ks)):
            mine, at_peer = _half(ins[p], c), _landed(zones, p, me, c)
            from_first = _landed(zones, p, _shard_of(first), c)
            _remote(mine, at_peer, ici_send.at[3 * i], pass_recv.at[k], (*first, c)).wait_send()
            _remote(mine, at_peer, relay_send.at[2 * k], pass_recv.at[k], (*second, c)).wait_send()
            _remote(from_first, from_first, relay_send.at[2 * k + 1], pass_recv.at[k], (*second, c)).wait_send()
            _remote(from_first, from_first, pass_send.at[k], pass_recv.at[k], sibling).wait_send()
            theirs = _landed(zones, p, _shard_of(second), 1 - c)
            _remote(theirs, theirs, pass_send.at[k], pass_recv.at[k], sibling).wait_recv()
            for j, (sent, got) in enumerate(((second, first), (diagonal, diagonal))):
                out_half = _landed(zones, p, _shard_of(sent), c)
                _remote(out_half, out_half, fwd_send.at[2 * p + j], fwd_recv.at[2 * p + j], sibling).wait_send()
                in_half = _landed(zones, p, _shard_of(got), 1 - c)
                _remote(in_half, in_half, fwd_send.at[2 * p + j], fwd_recv.at[2 * p + j], sibling).wait_recv()
            own = _remote(ins[p], zones[p].at[me], own_send.at[i], own_recv.at[i], sibling)
            own.wait_send()
            own.wait_recv()

    outs = pl.pallas_call(
        body, name=name,
        in_specs=[HBM] * (2 * n) + [SEM] * 8 + [ANY],
        out_shape=tuple(pltpu.HBM(a.shape, a.dtype) for a in lands),
        out_specs=tuple([HBM] * n),
        input_output_aliases={n + k: k for k in range(n)},
        compiler_params=pltpu.CompilerParams(has_side_effects=EFFECT),
    )(*shards, *lands, sems["ici_send"], sems["own_send"], sems["own_recv"], relay[0], relay[2], relay[3],
      fwd[0], fwd[1], after)
    return list(outs)


def _all_reduce_small(name, slab, after=None):
    r, width = slab.shape
    ndev = 8

    def body(x_ref, after_ref, out_ref, buf, send_sems, recv_sems):
        x, y, c, _ = _place()
        me = 4 * x + 2 * y + c
        buf[me] = x_ref[...]
        copies = []
        for k in range(1, ndev):
            peer = jnp.bitwise_xor(me, k)
            to = (peer // 4, (peer // 2) % 2, peer % 2)
            cp = pltpu.make_async_remote_copy(src_ref=x_ref, dst_ref=buf.at[me], send_sem=send_sems.at[k - 1],
                                              recv_sem=recv_sems.at[k - 1], device_id=to, device_id_type=MESH)
            cp.start()
            copies.append(cp)
        for k in range(1, ndev):
            peer = jnp.bitwise_xor(me, k)
            pltpu.make_async_remote_copy(src_ref=x_ref, dst_ref=buf.at[peer], send_sem=send_sems.at[k - 1],
                                         recv_sem=recv_sems.at[k - 1], device_id=(x, y, c),
                                         device_id_type=MESH).wait_recv()
        for cp in copies:
            cp.wait_send()
        total = buf[0]
        for d in range(1, ndev):
            total = total + buf[d]
        out_ref[...] = total

    return pl.pallas_call(
        body, name=name,
        in_specs=[pl.BlockSpec(memory_space=pltpu.VMEM), ANY], out_specs=pl.BlockSpec(memory_space=pltpu.VMEM),
        out_shape=jax.ShapeDtypeStruct((r, width), F32),
        scratch_shapes=[pltpu.VMEM((ndev, r, width), F32), pltpu.SemaphoreType.DMA((ndev - 1,)),
                        pltpu.SemaphoreType.DMA((ndev - 1,))],
        compiler_params=pltpu.CompilerParams(vmem_limit_bytes=VMEM_LIMIT),
    )(slab, slab if after is None else after)


def _half_tiling(rows, cols):
    if _by_rows(rows):
        tr = _pick(rows // 2, (256, 128, 64, 32, 16))
        nb = (rows // 2) // tr
        return (tr, cols), nb, (lambda which, b: (which * nb + b, 0)), (lambda b: (b, 0))
    tc = _pick(cols // 2, (256, 128))
    nb = (cols // 2) // tc
    return (rows, tc), nb, (lambda which, b: (0, which * nb + b)), (lambda b: (0, b))


def _chip_partial(name, grad, other, core):
    s, r, cdim = grad.shape
    blk, nb, whole, within = _half_tiling(r, cdim)

    def body(core_ref, g_ref, o_ref, out_ref):
        out_ref[...] = (g_ref[...] + o_ref[...]).astype(BF16)

    return pl.pallas_call(
        body, name=name,
        grid_spec=pltpu.PrefetchScalarGridSpec(
            num_scalar_prefetch=1, grid=(s, nb),
            in_specs=[pl.BlockSpec((None,) + blk, lambda j, b, core_ref: (j,) + whole(core_ref[0], b)),
                      pl.BlockSpec((None,) + blk, lambda j, b, core_ref: (j,) + within(b))],
            out_specs=pl.BlockSpec((None,) + blk, lambda j, b, core_ref: (j,) + within(b))),
        out_shape=jax.ShapeDtypeStruct((s,) + _half_shape(r, cdim), BF16),
        compiler_params=_params("parallel", "parallel"),
    )(core, grad, other)


def _partial_copies(ins, zones, send_sems, recv_sems):
    x, y, c, chips = _place()
    return [_remote(ins[i].at[2 * chip[0] + chip[1]], zones[i].at[j], send_sems.at[3 * i + j],
                    recv_sems.at[3 * i + j], (*chip, c))
            for i in range(len(ins)) for j, chip in enumerate(chips)]


def _swap_copies(ins, zones, send_sems, recv_sems):
    x, y, c, _ = _place()
    copies = []
    for i in range(len(ins)):
        for s in range(N_SHARD):
            copies.append(_remote(_half(ins[i], 1 - c, s), zones[i].at[s],
                                  send_sems.at[N_SHARD * i + s], recv_sems.at[N_SHARD * i + s], (x, y, 1 - c)))
    return copies


def _exchange_start(name, plan, sources, lands, per_array):
    n = len(sources)
    lands = [lax.empty(shape, dtype) for shape, dtype in lands]

    def body(*refs):
        for cp in plan(refs[:n], refs[n:2 * n], refs[2 * n], refs[2 * n + 1]):
            cp.start()
        refs[-1][...] = jnp.zeros_like(refs[-1])

    dma = pltpu.SemaphoreType.DMA
    outs = pl.pallas_call(
        body, name=name,
        in_specs=[HBM] * (2 * n),
        out_shape=(dma((per_array * n,)), dma((per_array * n,)),
                   *[pltpu.HBM(a.shape, a.dtype) for a in list(sources) + lands], jax.ShapeDtypeStruct((8, LANES), F32)),
        out_specs=(SEM, SEM, *[HBM] * (2 * n), pl.BlockSpec(memory_space=pltpu.VMEM)),
        input_output_aliases={k: 2 + k for k in range(2 * n)},
        compiler_params=pltpu.CompilerParams(has_side_effects=EFFECT),
    )(*[_in_hbm(a) for a in list(sources) + lands])
    return (outs[0], outs[1]), list(outs[2:2 + n]), list(outs[2 + n:2 + 2 * n]), outs[-1]


def _exchange_wait(name, plan, started, after):
    sems, partials, lands, _ = started
    n = len(partials)

    def body(*refs):
        for cp in plan(refs[:n], refs[n:2 * n], refs[2 * n], refs[2 * n + 1]):
            cp.wait_send()
            cp.wait_recv()

    outs = pl.pallas_call(
        body, name=name,
        in_specs=[HBM] * (2 * n) + [SEM, SEM] + [ANY] * len(after),
        out_shape=tuple(pltpu.HBM(a.shape, a.dtype) for a in lands),
        out_specs=tuple([HBM] * n),
        input_output_aliases={n + k: k for k in range(n)},
        compiler_params=pltpu.CompilerParams(has_side_effects=EFFECT),
    )(*partials, *lands, sems[0], sems[1], *after)
    return list(outs)


def _reduce_own(name, grad, other, received, where):
    s, r, cdim = grad.shape
    blk, nb, whole, within = _half_tiling(r, cdim)

    def body(where_ref, g_ref, o_ref, r_ref, out_ref):
        total = g_ref[...] + o_ref[...]
        for j in range(3):
            total = total + r_ref[j].astype(F32)
        out_ref[...] = total

    return pl.pallas_call(
        body, name=name,
        grid_spec=pltpu.PrefetchScalarGridSpec(
            num_scalar_prefetch=1, grid=(nb,),
            in_specs=[pl.BlockSpec((None,) + blk, lambda b, w_ref: (w_ref[0],) + whole(w_ref[1], b)),
                      pl.BlockSpec((None,) + blk, lambda b, w_ref: (w_ref[0],) + within(b)),
                      pl.BlockSpec((3,) + blk, lambda b, w_ref: (0,) + within(b))],
            out_specs=pl.BlockSpec(blk, lambda b, w_ref: whole(w_ref[1], b))),
        out_shape=jax.ShapeDtypeStruct((r, cdim), F32),
        compiler_params=_params("parallel"),
    )(where, grad, other, received)


def _join_start(name, halves):
    n = len(halves)

    def body(*refs):
        bufs, send_sems, recv_sems = refs[:n], refs[n], refs[n + 1]
        x, y, c, _ = _place()
        for i in range(n):
            mine = _half(bufs[i], c)
            _remote(mine, mine, send_sems.at[i], recv_sems.at[i], (x, y, 1 - c)).start()
        refs[-1][...] = jnp.zeros_like(refs[-1])

    dma = pltpu.SemaphoreType.DMA
    outs = pl.pallas_call(
        body, name=name,
        in_specs=[HBM] * n,
        out_shape=(dma((n,)), dma((n,)), *[pltpu.HBM(h.shape, F32) for h in halves], jax.ShapeDtypeStruct((8, LANES), F32)),
        out_specs=(SEM, SEM, *[HBM] * n, pl.BlockSpec(memory_space=pltpu.VMEM)),
        input_output_aliases={k: 2 + k for k in range(n)},
        compiler_params=pltpu.CompilerParams(has_side_effects=EFFECT),
    )(*[_in_hbm(h) for h in halves])
    return (outs[0], outs[1]), list(outs[2:2 + n]), outs[-1]


def _join_wait(name, started, after):
    sems, bufs, _ = started
    n = len(bufs)

    def body(*refs):
        bufs, send_sems, recv_sems = refs[:n], refs[n], refs[n + 1]
        x, y, c, _ = _place()
        for i in range(n):
            mine, theirs = _half(bufs[i], c), _half(bufs[i], 1 - c)
            _remote(mine, mine, send_sems.at[i], recv_sems.at[i], (x, y, 1 - c)).wait_send()
            _remote(theirs, theirs, send_sems.at[i], recv_sems.at[i], (x, y, 1 - c)).wait_recv()

    outs = pl.pallas_call(
        body, name=name,
        in_specs=[HBM] * n + [SEM, SEM] + [ANY] * len(after),
        out_shape=tuple(pltpu.HBM(b.shape, F32) for b in bufs),
        out_specs=tuple([HBM] * n),
        input_output_aliases={k: k for k in range(n)},
        compiler_params=pltpu.CompilerParams(has_side_effects=EFFECT),
    )(*bufs, sems[0], sems[1], *after)
    return list(outs)


BIG = ("w_in", "pool_w", "w_out", "xq_w", "xk_w", "xv_w", "xo_w", "w_up", "w_down")
GATHER_GROUPS = ((0, 1), (2, 3, 4, 5, 6), (7,), (8,))
RELAYED = (7, 8)
SMALL = ("conv_w", "a_log", "dt_bias", "gdn_norm_w", "pool_scale", "ln1_g", "ln1_b", "ln2_g", "ln2_b", "ln3_g", "ln3_b")
ORDER = ("w_in", "conv_w", "a_log", "dt_bias", "gdn_norm_w", "pool_w", "pool_scale", "w_out", "ln1_g", "ln1_b",
         "xq_w", "xk_w", "xv_w", "xo_w", "ln2_g", "ln2_b", "w_up", "w_down", "ln3_g", "ln3_b")
LANES = 128


def _rows(flat_len):
    return -(-flat_len // LANES)


def _pack(pieces):
    out = []
    for p in pieces:
        flat = p.reshape(-1).astype(F32)
        out.append(jnp.pad(flat, (0, _rows(flat.shape[0]) * LANES - flat.shape[0])).reshape(-1, LANES))
    slab = jnp.concatenate(out, axis=0)
    return jnp.pad(slab, ((0, -slab.shape[0] % 8), (0, 0)))


def _unpack(slab, shapes):
    out, row = [], 0
    for shp in shapes:
        size = math.prod(shp)
        out.append(slab[row:row + _rows(size)].reshape(-1)[:size].reshape(shp))
        row += _rows(size)
    return out


TRANSPOSED = ("w_in",)


def _as2d(name, a):
    a = a[0]
    if name in TRANSPOSED:
        return jnp.swapaxes(a, 0, 1)
    return a.reshape(-1, a.shape[-1]) if a.ndim == 3 else a


def _from2d(name, a, shape):
    return (jnp.swapaxes(a, 0, 1) if name in TRANSPOSED else a).reshape(shape)


def kernel(x, mem, w_in, conv_w, a_log, dt_bias, gdn_norm_w, pool_w, pool_scale, w_out, ln1_g, ln1_b, xq_w, xk_w, xv_w, xo_w, ln2_g, ln2_b, w_up, w_down, ln3_g, ln3_b, loss_target, m_w_in, m_conv_w, m_a_log, m_dt_bias, m_gdn_norm_w, m_pool_w, m_pool_scale, m_w_out, m_ln1_g, m_ln1_b, m_xq_w, m_xk_w, m_xv_w, m_xo_w, m_ln2_g, m_ln2_b, m_w_up, m_w_down, m_ln3_g, m_ln3_b, v_w_in, v_conv_w, v_a_log, v_dt_bias, v_gdn_norm_w, v_pool_w, v_pool_scale, v_w_out, v_ln1_g, v_ln1_b, v_xq_w, v_xk_w, v_xv_w, v_xo_w, v_ln2_g, v_ln2_b, v_w_up, v_w_down, v_ln3_g, v_ln3_b):
    given = dict(locals())
    cx, cy, cc = lax.axis_index("x"), lax.axis_index("y"), lax.axis_index("c")
    me = 2 * cx + cy
    groups = pool_w.shape[1]
    cs = pool_w.shape[2]
    kk, conv_cols = conv_w.shape[1], conv_w.shape[2]
    core = cc.astype(jnp.int32).reshape(1)
    where = jnp.stack([me, cc]).astype(jnp.int32)

    conv_slab = jnp.zeros((kk, N_SHARD * conv_cols), F32)
    conv_slab = lax.dynamic_update_slice(conv_slab, conv_w[0] * (cc == 0).astype(F32), (0, me * conv_cols))
    wts = {"conv_w": _unpack(_all_reduce_small("gather_conv_w", _pack([conv_slab])), [conv_slab.shape])[0]}

    started = {}

    def start(name, idx, after, token=None):
        casts = [_after(token, _as2d(BIG[i], given[BIG[i]])).astype(BF16) for i in idx]
        relayed = tuple(k for k, i in enumerate(idx) if i in RELAYED)
        sems, shards, lands, token = _gather_start(name, casts, after, relayed)
        for k, i in enumerate(idx):
            started[i] = (sems, k, shards[k], lands[k])
        return token

    token = start("gather_start_first", GATHER_GROUPS[0], wts["conv_w"])
    token = start("gather_start_rest", tuple(i for group in GATHER_GROUPS[1:] for i in group), token, token)

    relay = {}

    def send_on(after):
        members = [started[i] for i in RELAYED]
        relay["sems"], zones, token = _gather_relay("gather_relay", [m[1] for m in members], [m[2] for m in members],
                                                    [m[3] for m in members], members[0][0], after)
        relay["zones"] = dict(zip(RELAYED, zones))
        return token

    def fetch(group, after):
        members = [started[i] for i in GATHER_GROUPS[group]]
        sems, idx = members[0][0], [m[1] for m in members]
        shards = [m[2] for m in members]
        if GATHER_GROUPS[group][0] in RELAYED:
            ks = [RELAYED.index(i) for i in GATHER_GROUPS[group]]
            zones = [relay["zones"][i] for i in GATHER_GROUPS[group]]
            fwd, zones = _gather_forward_relayed(f"gather_forward_{group}", ks, zones, relay["sems"], after)
            got = _gather_wait_relayed(f"gather_wait_{group}", idx, ks, shards, zones, sems, relay["sems"], fwd, after)
        else:
            fwd, zones = _gather_forward(f"gather_forward_{group}", idx, [m[3] for m in members], sems, after)
            got = _gather_wait(f"gather_wait_{group}", idx, shards, zones, sems, fwd, after)
        full = dict(zip([BIG[i] for i in GATHER_GROUPS[group]], got))
        out = {}
        for n, a in full.items():
            if n == "w_in":
                out["w_in_t"] = a
            elif n == "w_up":
                out["w_up3"] = a
            elif n == "pool_w":
                out[n] = a.reshape(N_SHARD, groups, cs, -1).transpose(1, 0, 2, 3).reshape(groups, N_SHARD * cs, -1)
            else:
                out[n] = a.reshape(-1, a.shape[-1])
        return out

    for n in ("a_log", "dt_bias", "gdn_norm_w", "pool_scale", "ln1_g", "ln1_b", "ln2_g", "ln2_b", "ln3_g", "ln3_b"):
        wts[n] = given[n]
    wts.update(fetch(0, token))

    def start_swap(group, grads):
        names, blocks = [], []
        for n, g in grads.items():
            if n == "pool_w":
                g = g.reshape(groups, N_SHARD, cs, -1).transpose(1, 0, 2, 3).reshape(N_SHARD, groups * cs, -1)
            elif g.ndim == 2:
                g = g.reshape(N_SHARD, -1, g.shape[-1])
            names.append({"w_in_t": "w_in", "w_up3": "w_up"}.get(n, n))
            blocks.append(g)
        zones = [((N_SHARD,) + _half_shape(b.shape[1], b.shape[2]), F32) for b in blocks]
        swap = _exchange_start(f"grad_swap_start_{group}", _swap_copies, blocks, zones, N_SHARD)
        return {"group": group, "names": names, "swap": swap, "token": swap[3]}

    def start_send(state, after):
        group, names = state["group"], state["names"]
        state["blocks"] = state["swap"][1]
        state["others"] = _exchange_wait(f"grad_swap_wait_{group}", _swap_copies, state["swap"], after)
        partials = [_chip_partial("chip_partial_" + n, gb, ob, core)
                    for n, gb, ob in zip(names, state["blocks"], state["others"])]
        zones = [((3,) + p.shape[1:], BF16) for p in partials]
        state["send"] = _exchange_start(f"grad_send_start_{group}", _partial_copies, partials, zones, 3)
        state["token"] = state["send"][3]

    grad, delta, new_m, new_v = {}, {}, {}, {}

    def start_join(state, after):
        group, names = state["group"], state["names"]
        received = _exchange_wait(f"grad_send_wait_{group}", _partial_copies, state["send"], after)
        halves = [_reduce_own("reduce_own_" + n, gb, ob, rb, where)
                  for n, gb, ob, rb in zip(names, state["blocks"], state["others"], received)]
        state["join"] = _join_start(f"grad_join_start_{group}", halves)
        return state["join"][2]

    def finish_reduce(state, after):
        group, names = state["group"], state["names"]
        for n, g in zip(names, _join_wait(f"grad_join_wait_{group}", state["join"], after)):
            shp = given[n].shape
            d2, m2, v2 = _adamw("adamw_" + n, _as2d(n, given[n]), g, _as2d(n, given["m_" + n]), _as2d(n, given["v_" + n]))
            grad[n], delta[n], new_m[n], new_v[n] = (_from2d(n, a, shp) for a in (g, d2, m2, v2))
        return d2

    step = _local_step(x[0], mem[0], loss_target[0], wts, token)
    pending = {}
    request = next(step)
    while True:
        try:
            kind, group, payload = request
            if kind == "weights":
                request = step.send(fetch(group, payload))
            elif kind == "relay":
                request = step.send(send_on(payload))
            elif kind == "grads":
                pending[group] = start_swap(group, payload)
                request = step.send(pending[group]["token"])
            else:
                start_send(pending[group], [payload])
                request = step.send(pending[group]["token"])
        except StopIteration as stop:
            loss_row, grad_x, g = stop.value
            break

    after = [pending[2]["token"], grad_x]
    for group in (0, 1):
        after = [start_join(pending[group], after)]
    for group in (0, 1):
        after = [finish_reduce(pending[group], after)]
    after = [finish_reduce(pending[2], [start_join(pending[2], after)])]

    small_names = ("a_log", "dt_bias", "gdn_norm_w", "pool_scale", "ln1_g", "ln1_b", "ln2_g", "ln2_b", "ln3_g", "ln3_b")
    pieces = [g["conv_w"]] + [g[n] for n in small_names] + [loss_row[:, :1]]
    shapes = [p.shape for p in pieces]
    summed = _unpack(_all_reduce_small("all_reduce_small", _pack(pieces), after[0]), shapes)
    gsmall = dict(zip(small_names, summed[1:-1]))
    gsmall["conv_w"] = lax.dynamic_slice(summed[0], (0, me * conv_cols), (kk, conv_cols))
    loss = summed[-1][0, 0]

    sshapes = [given[n].shape for n in SMALL]
    slabs = [_pack([given[p + n] for n in SMALL]) for p in ("", "m_", "v_")]
    gslab = _pack([gsmall[n] for n in SMALL])
    outs = _adamw("adamw_small", slabs[0], gslab, slabs[1], slabs[2])
    for dst, slab in zip((delta, new_m, new_v), outs):
        dst.update(zip(SMALL, _unpack(slab, sshapes)))
    for n in SMALL:
        grad[n] = gsmall[n].reshape(given[n].shape)

    return (loss, grad_x[None], *[grad[n] for n in ORDER], *[delta[n] for n in ORDER],
            *[new_m[n] for n in ORDER], *[new_v[n] for n in ORDER])
```

```python
import functools
import math

import jax
import jax.numpy as jnp
from jax import lax
from jax.experimental import pallas as pl
from jax.experimental.pallas import tpu as pltpu

F32 = jnp.float32
BF16 = jnp.bfloat16
MESH = pl.DeviceIdType.MESH

HEAD_DIM = 128
CHUNK = 64
POOL_WINDOWS = (2, 4, 8, 16)
XATTN_HEADS = 4
ALPHA = 2.0 ** 0.25
LN_EPS = 1e-5
NORM_EPS = 1e-6
ADAM_LR, ADAM_B1, ADAM_B2, ADAM_EPS, ADAM_WD, ADAM_STEP = 0.001, 0.9, 0.999, 1e-08, 0.01, 10
N_SHARD = 4
VMEM_LIMIT = 56 * 1024 * 1024
K_STEPS = (2048, 1024, 512, 256, 128)


def _params(*sem):
    return pltpu.CompilerParams(dimension_semantics=sem, vmem_limit_bytes=VMEM_LIMIT)


def _bdot(a, b, ta=False, tb=False):
    dims = (((0 if ta else 1,), (1 if tb else 0,)), ((), ()))
    return lax.dot_general(a.astype(BF16), b.astype(BF16), dims, preferred_element_type=F32)


def _sigmoid(x):
    return 1.0 / (1.0 + jnp.exp(-x))


def _matmul(name, a, b, *, ta=False, tb=False, tm, tn, tk, extra=(), outs, epilogue, b_blocks=None,
            sequential=False, n_used=None, k_used=None, n_outer=False):
    m, k_dim = (a.shape[1], a.shape[0]) if ta else a.shape
    if b_blocks and tb:
        n = b.shape[1]
        k_dim = b.shape[0] * b.shape[2]
        per = b.shape[2] // tk
        b_spec = pl.BlockSpec((None, tn, tk), lambda i, j, k: (k // per, j, k % per))
    elif b_blocks:
        n = b.shape[0] * b.shape[2]
        per = b.shape[2] // tn
        b_spec = pl.BlockSpec((None, tk, tn), lambda i, j, k: (j // per, k, j % per))
    elif tb:
        n = b.shape[0]
        b_spec = pl.BlockSpec((tn, tk), lambda i, j, k: (j, k))
    else:
        n = b.shape[1]
        b_spec = pl.BlockSpec((tk, tn), lambda i, j, k: (k, j))
    n, k_dim = n_used or n, k_used or k_dim
    assert m % tm == 0 and n % tn == 0 and k_dim % tk == 0, (name, m, n, k_dim, tm, tn, tk)
    nk = k_dim // tk
    a_spec = pl.BlockSpec((tk, tm), lambda i, j, k: (k, i)) if ta else pl.BlockSpec((tm, tk), lambda i, j, k: (i, k))
    n_extra, n_out = len(extra), len(outs)

    def wrap(index_map):
        return lambda i, j, k: index_map(i, j)

    def spec(block, index_map):
        if n_outer:
            return pl.BlockSpec(block, lambda j, i, k: index_map(i, j, k))
        return pl.BlockSpec(block, index_map)

    row_axis = 1 if n_outer else 0

    def body_one_step(*refs):
        ex = refs[2:2 + n_extra]
        out = refs[2 + n_extra:2 + n_extra + n_out]
        epilogue(_bdot(refs[0][...], refs[1][...], ta, tb), ex, out, pl.program_id(row_axis))

    def body(*refs):
        a_ref, b_ref = refs[0], refs[1]
        ex = refs[2:2 + n_extra]
        out = refs[2 + n_extra:2 + n_extra + n_out]
        acc = refs[-1]
        i, k = pl.program_id(row_axis), pl.program_id(2)
        part = _bdot(a_ref[...], b_ref[...], ta, tb)

        @pl.when(k == 0)
        def _():
            acc[...] = part

        @pl.when(jnp.logical_and(k > 0, k < nk - 1))
        def _():
            acc[...] += part

        @pl.when(k == nk - 1)
        def _():
            epilogue(acc[...] + part, ex, out, i)

    sem = ("arbitrary",) * 3 if sequential else ("parallel", "parallel", "arbitrary")
    res = pl.pallas_call(
        body_one_step if nk == 1 else body, name=name,
        grid=(n // tn, m // tm, nk) if n_outer else (m // tm, n // tn, nk),
        in_specs=[spec(a_spec.block_shape, a_spec.index_map), spec(b_spec.block_shape, b_spec.index_map)]
        + [spec(bs, wrap(im)) for _, bs, im in extra],
        out_specs=[spec(bs, wrap(im)) for _, bs, im in outs],
        out_shape=[s for s, _, _ in outs],
        scratch_shapes=[] if nk == 1 else [pltpu.VMEM((tm, tn), F32)],
        compiler_params=_params(*sem),
    )(a, b, *[x for x, _, _ in extra])
    return res


def _tile(i, j):
    return (i, j)


def _plain(name, a, b, *, ta=False, tb=False, tm, tn, tk, out_dtype, b_blocks=None, out3=None, n_used=None,
           n_outer=False):
    m = a.shape[1] if ta else a.shape[0]
    if b_blocks:
        n = b.shape[1] if tb else b.shape[0] * b.shape[2]
    else:
        n = n_used or (b.shape[0] if tb else b.shape[1])

    def epi(acc, ex, out, i):
        out[0][...] = acc.astype(out_dtype)

    if out3:
        per = (n // out3) // tn
        spec = (jax.ShapeDtypeStruct((out3, m, n // out3), out_dtype), (None, tm, tn),
                lambda i, j: (j // per, i, j % per))
    else:
        spec = (jax.ShapeDtypeStruct((m, n), out_dtype), (tm, tn), _tile)
    return _matmul(name, a, b, ta=ta, tb=tb, tm=tm, tn=tn, tk=tk, outs=[spec], epilogue=epi,
                   b_blocks=b_blocks, n_used=n_used, n_outer=n_outer)[0]


def _ln_forward(name, a, b, res, gamma, beta, *, tm, tk, want_h=True):
    m, n = res.shape

    def epi(acc, ex, out, i):
        u = ALPHA * ex[0][...] + acc
        mu = jnp.mean(u, axis=-1, keepdims=True)
        xc = u - mu
        var = jnp.mean(xc * xc, axis=-1, keepdims=True)
        rstd = lax.rsqrt(var + LN_EPS)
        xhat = xc * rstd
        out[-2][...] = xhat
        out[-1][...] = rstd
        if want_h:
            h = xhat * ex[1][...] + ex[2][...]
            out[0][...] = h
            out[1][...] = h.astype(BF16)

    row = lambda i, j: (i, 0)
    vec = lambda i, j: (0, 0)
    outs = [(jax.ShapeDtypeStruct((m, n), F32), (tm, n), row), (jax.ShapeDtypeStruct((m, n), BF16), (tm, n), row),
            (jax.ShapeDtypeStruct((m, n), F32), (tm, n), row), (jax.ShapeDtypeStruct((m, 1), F32), (tm, 1), row)]
    return _matmul(
        name, a, b, tm=tm, tn=n, tk=tk,
        extra=[(res, (tm, n), row), (gamma, (1, n), vec), (beta, (1, n), vec)],
        outs=outs if want_h else outs[2:], epilogue=epi)


def _ln_backward_math(dy, xhat, rstd, gamma):
    dxhat = dy * gamma
    m1 = jnp.mean(dxhat, axis=-1, keepdims=True)
    m2 = jnp.mean(dxhat * xhat, axis=-1, keepdims=True)
    du = rstd * (dxhat - m1 - xhat * m2)
    return du, jnp.sum(dy * xhat, axis=0, keepdims=True), jnp.sum(dy, axis=0, keepdims=True)


def _ln_backward(name, a, b, dres, xhat, rstd, gamma, *, tm, tk, b_blocks=None, tb=True):
    m, n = dres.shape

    def epi(acc, ex, out, i):
        dy = acc + ALPHA * ex[0][...]
        du, dg, db = _ln_backward_math(dy, ex[1][...], ex[2][...], ex[3][...])
        out[0][...] = du
        out[1][...] = du.astype(BF16)
        first = i == 0

        @pl.when(first)
        def _():
            out[2][...] = dg
            out[3][...] = db

        @pl.when(jnp.logical_not(first))
        def _():
            out[2][...] += dg
            out[3][...] += db

    row = lambda i, j: (i, 0)
    vec = lambda i, j: (0, 0)
    return _matmul(
        name, a, b, tb=tb, tm=tm, tn=n, tk=tk, b_blocks=b_blocks, sequential=True,
        extra=[(dres, (tm, n), row), (xhat, (tm, n), row), (rstd, (tm, 1), row), (gamma, (1, n), vec)],
        outs=[(jax.ShapeDtypeStruct((m, n), F32), (tm, n), row),
              (jax.ShapeDtypeStruct((m, n), BF16), (tm, n), row),
              (jax.ShapeDtypeStruct((1, n), F32), (1, n), vec),
              (jax.ShapeDtypeStruct((1, n), F32), (1, n), vec)],
        epilogue=epi)


def _shift_down(x, k):
    row = lax.broadcasted_iota(jnp.int32, x.shape, 0)
    return jnp.where(row >= k, pltpu.roll(x, k, axis=0), 0.0)


def _shift_up(x, k):
    t = x.shape[0]
    row = lax.broadcasted_iota(jnp.int32, x.shape, 0)
    return jnp.where(row < t - k, pltpu.roll(x, t - k, axis=0), 0.0)


def _conv_silu_norm(x, w, normalise):
    kk = w.shape[0]
    c = x * w[kk - 1:kk, :]
    for j in range(kk - 1):
        c = c + _shift_down(x, kk - 1 - j) * w[j:j + 1, :]
    sg = _sigmoid(c)
    s = c * sg
    r = lax.rsqrt(jnp.sum(s * s, axis=-1, keepdims=True) + NORM_EPS)
    y = jnp.where(normalise, s * r, s)
    return c, sg, s, r, y


def _gdn_pre(proj, conv_w, heads):
    t = proj.shape[0]
    kk = conv_w.shape[0]

    def body(x_ref, w_ref, o_ref):
        normalise = pl.program_id(0) < 2
        o_ref[...] = _conv_silu_norm(x_ref[...], w_ref[...], normalise)[4]

    col = lambda s, h: (0, s * heads + h)
    return pl.pallas_call(
        body, name="gdn_pre", grid=(3, heads),
        in_specs=[pl.BlockSpec((t, HEAD_DIM), col), pl.BlockSpec((kk, HEAD_DIM), col)],
        out_specs=pl.BlockSpec((t, HEAD_DIM), col),
        out_shape=jax.ShapeDtypeStruct((t, 3 * heads * HEAD_DIM), F32),
        compiler_params=_params("parallel", "parallel"),
    )(proj, conv_w)


def _gdn_pre_backward(proj, conv_w, dqkv, heads):
    t = proj.shape[0]
    kk = conv_w.shape[0]

    def body(x_ref, w_ref, dy_ref, dx_ref, dw_ref):
        normalise = pl.program_id(0) < 2
        x = x_ref[...]
        w = w_ref[...]
        dy = dy_ref[...]
        c, sg, s, r, y = _conv_silu_norm(x, w, normalise)
        ds_norm = r * (dy - y * jnp.sum(dy * y, axis=-1, keepdims=True))
        ds = jnp.where(normalise, ds_norm, dy)
        dc = ds * (sg * (1.0 + c * (1.0 - sg)))
        dx = dc * w[kk - 1:kk, :]
        rows = [None] * kk
        rows[kk - 1] = jnp.sum(dc * x, axis=0, keepdims=True)
        for j in range(kk - 1):
            lag = kk - 1 - j
            dx = dx + _shift_up(dc, lag) * w[j:j + 1, :]
            rows[j] = jnp.sum(dc * _shift_down(x, lag), axis=0, keepdims=True)
        dx_ref[...] = dx.astype(BF16)
        dw_ref[...] = jnp.concatenate(rows, axis=0)

    col = lambda s, h: (0, s * heads + h)
    return pl.pallas_call(
        body, name="gdn_pre_bwd", grid=(3, heads),
        in_specs=[pl.BlockSpec((t, HEAD_DIM), col), pl.BlockSpec((kk, HEAD_DIM), col),
                  pl.BlockSpec((t, HEAD_DIM), col)],
        out_specs=[pl.BlockSpec((t, HEAD_DIM), col), pl.BlockSpec((kk, HEAD_DIM), col)],
        out_shape=[jax.ShapeDtypeStruct((t, 3 * heads * HEAD_DIM), BF16),
                   jax.ShapeDtypeStruct((kk, 3 * heads * HEAD_DIM), F32)],
        compiler_params=_params("parallel", "parallel"),
    )(proj, conv_w, dqkv)


def _gate_vectors(a_log, dt_bias, heads):
    pad = lambda v: jnp.pad(v.astype(F32), ((0, 0), (heads, HEAD_DIM - 2 * heads)))
    return pad(jnp.exp(a_log.astype(F32))), pad(dt_bias)


def _softplus(x):
    return jnp.maximum(x, 0.0) + jnp.log(1.0 + jnp.exp(-jnp.abs(x)))


def _gates_epilogue(heads):
    def epi(acc, ex, out, i):
        lane = lax.broadcasted_iota(jnp.int32, acc.shape, 1)
        beta = _sigmoid(acc)
        g = -ex[0][...] * _softplus(acc + ex[1][...])
        out[0][...] = acc
        out[1][...] = jnp.where(lane < heads, beta, jnp.where(lane < 2 * heads, g, 0.0))
    return epi


def _gates_backward(ba, bg, dbg, ea, dtb, heads):
    t = ba.shape[0]

    def body(ba_ref, bg_ref, d_ref, ea_ref, dt_ref, dba_ref, dal_ref, ddt_ref):
        lane = lax.broadcasted_iota(jnp.int32, (t, HEAD_DIM), 1)
        bgv = bg_ref[...]
        d = d_ref[...]
        db = d * bgv * (1.0 - bgv)
        da = -d * ea_ref[...] * _sigmoid(ba_ref[...] + dt_ref[...])
        is_g = jnp.logical_and(lane >= heads, lane < 2 * heads)
        dba = jnp.where(lane < heads, db, jnp.where(is_g, da, 0.0))
        dba_ref[...] = dba.astype(BF16)
        dal_ref[...] = jnp.sum(jnp.where(is_g, d * bgv, 0.0), axis=0, keepdims=True)
        ddt_ref[...] = jnp.sum(jnp.where(is_g, da, 0.0), axis=0, keepdims=True)

    full = pl.BlockSpec((t, HEAD_DIM), lambda: (0, 0))
    vec = pl.BlockSpec((1, HEAD_DIM), lambda: (0, 0))
    return pl.pallas_call(
        body, name="gates_bwd", grid=(),
        in_specs=[full, full, full, vec, vec], out_specs=[full, vec, vec],
        out_shape=[jax.ShapeDtypeStruct((t, HEAD_DIM), BF16), jax.ShapeDtypeStruct((1, HEAD_DIM), F32),
                   jax.ShapeDtypeStruct((1, HEAD_DIM), F32)],
        compiler_params=pltpu.CompilerParams(vmem_limit_bytes=VMEM_LIMIT),
    )(ba, bg, dbg, ea, dtb)


class _Chunk:
    pass


def _split2(x):
    hi = x.astype(BF16)
    return hi, (x - hi.astype(F32)).astype(BF16)


def _split3(x):
    hi = x.astype(BF16)
    rest = x - hi.astype(F32)
    mid = rest.astype(BF16)
    return hi, mid, (rest - mid.astype(F32)).astype(BF16)


def _dot_mask(mask, x, ta=False):
    hi, mid, lo = _split3(x)
    return _bdot(mask, hi, ta=ta) + (_bdot(mask, mid, ta=ta) + _bdot(mask, lo, ta=ta))


def _transpose_by_identity(x):
    r = x.shape[0]
    eye = (lax.broadcasted_iota(jnp.int32, (r, r), 0) == lax.broadcasted_iota(jnp.int32, (r, r), 1)).astype(BF16)
    hi, mid, lo = _split3(x)
    return _bdot(hi, eye, ta=True) + (_bdot(mid, eye, ta=True) + _bdot(lo, eye, ta=True))


def _dot22(a, b, ta=False, tb=False):
    ah, al = _split2(a)
    bh, bl = _split2(b)
    return _bdot(ah, bh, ta, tb) + (_bdot(ah, bl, ta, tb) + _bdot(al, bh, ta, tb))


def _chunk_gates(bg, heads):
    n = CHUNK
    row = lax.broadcasted_iota(jnp.int32, (n, n), 0)
    col = lax.broadcasted_iota(jnp.int32, (n, n), 1)
    lane = lax.broadcasted_iota(jnp.int32, bg.shape, 1)
    graw = jnp.where(jnp.logical_and(lane >= heads, lane < 2 * heads), bg, 0.0)
    gc = _dot_mask((row >= col).astype(BF16), graw)
    return gc, _transpose_by_identity(gc)


def _in_lockstep(generators):
    results = [None] * len(generators)
    live = list(enumerate(generators))
    while live:
        still = []
        for i, gen in live:
            try:
                next(gen)
                still.append((i, gen))
            except StopIteration as stop:
                results[i] = stop.value
        live = still
    return results


def _chunk_local(q, k, v, beta, gc, grow, solved=None):
    c = _Chunk()
    n = CHUNK
    row = lax.broadcasted_iota(jnp.int32, (n, n), 0)
    col = lax.broadcasted_iota(jnp.int32, (n, n), 1)
    c.tri = row >= col
    c.strict = row > col
    eye = row == col
    c.gcb = jnp.broadcast_to(gc, (n, HEAD_DIM))
    c.decay = jnp.where(c.tri, jnp.exp(jnp.where(c.tri, gc - grow, 0.0)), 0.0)
    c.eg = jnp.exp(c.gcb)
    glast = c.gcb[n - 1:n, :]
    c.egl = jnp.exp(glast)
    c.ekl = jnp.exp(glast - c.gcb)
    c.beta = beta
    c.q = q * (HEAD_DIM ** -0.5)
    c.k = k
    c.v = v
    c.kb = k * beta
    c.vb = v * beta
    c.kg = c.kb * c.eg
    both = _bdot(jnp.concatenate([c.kb, c.q], axis=0), k, tb=True)
    yield
    c.L = jnp.where(c.strict, both[:n] * c.decay, 0.0)
    c.A = jnp.where(c.tri, both[n:] * c.decay, 0.0)
    if solved is None:
        x = -c.L
        tinv = eye.astype(F32) + x
        p = _dot22(x, x)
        yield
        for _ in range(int(math.log2(n)) - 2):
            both = _dot22(jnp.concatenate([p, tinv], axis=0), p)
            yield
            p, tinv = both[:n], tinv + both[n:]
        c.T = tinv + _dot22(tinv, p)
        yield
        uw = _dot22(c.T, jnp.concatenate([c.vb, c.kg], axis=1))
        yield
        c.u, c.w = uw[:, :HEAD_DIM], uw[:, HEAD_DIM:]
    else:
        c.T, c.u, c.w = solved
    c.qg = c.q * c.eg
    c.kdec = k * c.ekl
    return c


def _gdn_core(qkv, bg, heads):
    t = qkv.shape[0]
    nchunk = t // CHUNK

    gw = heads * HEAD_DIM

    def body(qkv_ref, bg_ref, o_ref, s_ref, t_ref, u_ref, w_ref, state):
        @pl.when(pl.program_id(0) == 0)
        def _():
            state[...] = jnp.zeros_like(state)

        bg_v = bg_ref[...]
        gc_all, gc_rows = _chunk_gates(bg_v, heads)
        def one_head(h):
            col = lambda s: pl.ds(s * gw + h * HEAD_DIM, HEAD_DIM)
            c = yield from _chunk_local(qkv_ref[:, col(0)], qkv_ref[:, col(1)], qkv_ref[:, col(2)], bg_v[:, h:h + 1],
                                        gc_all[:, heads + h:heads + h + 1], gc_rows[heads + h:heads + h + 1, :])
            s0 = state[h]
            v_new = c.u - _bdot(c.w, s0)
            yield
            o = _bdot(c.qg, s0) + _bdot(c.A, v_new)
            return s0, o, s0 * c.egl + _bdot(c.kdec, v_new, ta=True), c

        results = _in_lockstep([one_head(h) for h in range(heads)])
        for h, (s0, o, s1, c) in enumerate(results):
            lanes = pl.ds(h * HEAD_DIM, HEAD_DIM)
            s_ref[h, 0] = s0
            o_ref[:, lanes] = o
            t_ref[:, lanes] = jnp.concatenate([c.T, jnp.zeros((CHUNK, HEAD_DIM - CHUNK), F32)], axis=1)
            u_ref[:, lanes] = c.u
            w_ref[:, lanes] = c.w
            state[h] = s1

    return pl.pallas_call(
        body, name="gdn_core", grid=(nchunk,),
        in_specs=[pl.BlockSpec((CHUNK, 3 * gw), lambda n: (n, 0)), pl.BlockSpec((CHUNK, HEAD_DIM), lambda n: (n, 0))],
        out_specs=[pl.BlockSpec((CHUNK, gw), lambda n: (n, 0)),
                   pl.BlockSpec((heads, 1, HEAD_DIM, HEAD_DIM), lambda n: (0, n, 0, 0))]
        + [pl.BlockSpec((CHUNK, gw), lambda n: (n, 0))] * 3,
        out_shape=[jax.ShapeDtypeStruct((t, gw), F32),
                   jax.ShapeDtypeStruct((heads, nchunk, HEAD_DIM, HEAD_DIM), F32)]
        + [jax.ShapeDtypeStruct((t, gw), F32)] * 3,
        scratch_shapes=[pltpu.VMEM((heads, HEAD_DIM, HEAD_DIM), F32)],
        compiler_params=_params("arbitrary"),
    )(qkv, bg)


def _gdn_core_backward(qkv, bg, states, solved, do, heads):
    t = qkv.shape[0]
    nchunk = t // CHUNK
    n = CHUNK

    def one_head(chunk_local, s0, d_out, ds1):
        c = yield from chunk_local
        v_new = c.u - _bdot(c.w, s0)
        dqg = _bdot(d_out, s0, tb=True)
        ds0 = _bdot(c.qg, d_out, ta=True) + ds1 * c.egl
        dv_new = _bdot(c.A, d_out, ta=True) + _bdot(c.kdec, ds1)
        yield
        dA = jnp.where(c.tri, _bdot(d_out, v_new, tb=True), 0.0)
        dkdec = _bdot(v_new, ds1, tb=True)
        dgl = jnp.sum(jnp.sum(ds1 * s0, axis=1, keepdims=True), axis=0, keepdims=True) * c.egl
        dw = -_bdot(dv_new, s0, tb=True)
        ds0 = ds0 - _bdot(c.w, dv_new, ta=True)
        yield
        both = _dot22(c.T, jnp.concatenate([dv_new, dw], axis=1), ta=True)
        yield
        dvb, dkg = both[:, :HEAD_DIM], both[:, HEAD_DIM:]
        dL = jnp.where(c.strict, -(_bdot(dvb, c.u, tb=True) + _bdot(dkg, c.w, tb=True)), 0.0)
        yield
        dm1 = dL * c.decay
        dkb = _bdot(dm1, c.k) + dkg * c.eg
        dk = _bdot(dm1, c.kb, ta=True)
        dm2 = dA * c.decay
        dq = _bdot(dm2, c.k) + dqg * c.eg
        dk = dk + _bdot(dm2, c.q, ta=True) + dkdec * c.ekl + dkb * c.beta
        pm = dL * c.L + dA * c.A
        ones = jnp.ones((n, HEAD_DIM), BF16)
        pm_hi, pm_lo = _split2(pm)
        colsum = _bdot(pm_hi, ones, ta=True) + _bdot(pm_lo, ones, ta=True)
        tk_ = jnp.sum(dkdec * c.kdec, axis=1, keepdims=True)
        dgc = (jnp.sum(pm, axis=1, keepdims=True) - colsum
               + jnp.sum(dqg * c.qg, axis=1, keepdims=True)
               - tk_
               + jnp.sum(dkg * c.kg, axis=1, keepdims=True))
        dgl = dgl + jnp.sum(tk_, axis=0, keepdims=True)
        rowi = lax.broadcasted_iota(jnp.int32, (n, HEAD_DIM), 0)
        dgc = dgc + jnp.where(rowi == n - 1, dgl, 0.0)
        dbeta = jnp.sum(dkb * c.k, axis=1, keepdims=True) + jnp.sum(dvb * c.v, axis=1, keepdims=True)
        return dq * (HEAD_DIM ** -0.5), dk, dvb * c.beta, dbeta, dgc, ds0

    gw = heads * HEAD_DIM

    def body(qkv_ref, bg_ref, s_ref, t_ref, u_ref, w_ref, do_ref, dqkv_ref, dbg_ref, dstate):
        @pl.when(pl.program_id(0) == 0)
        def _():
            dstate[...] = jnp.zeros_like(dstate)

        bg_v = bg_ref[...]
        gc_all, gc_rows = _chunk_gates(bg_v, heads)
        lane = lax.broadcasted_iota(jnp.int32, (n, HEAD_DIM), 1)
        dgates = jnp.zeros((n, HEAD_DIM), F32)
        chains = []
        for h in range(heads):
            col = lambda s: pl.ds(s * gw + h * HEAD_DIM, HEAD_DIM)
            lanes = pl.ds(h * HEAD_DIM, HEAD_DIM)
            c = _chunk_local(qkv_ref[:, col(0)], qkv_ref[:, col(1)], qkv_ref[:, col(2)], bg_v[:, h:h + 1],
                             gc_all[:, heads + h:heads + h + 1], gc_rows[heads + h:heads + h + 1, :],
                             (t_ref[:, pl.ds(h * HEAD_DIM, CHUNK)], u_ref[:, lanes], w_ref[:, lanes]))
            chains.append(one_head(c, s_ref[h, 0], do_ref[:, pl.ds(h * HEAD_DIM, HEAD_DIM)], dstate[h]))
        results = _in_lockstep(chains)
        for h, (dq, dk, dv, dbeta, dgc, ds0) in enumerate(results):
            dgates = jnp.where(lane == h, dbeta, jnp.where(lane == heads + h, dgc, dgates))
        for h, (dq, dk, dv, dbeta, dgc, ds0) in enumerate(results):
            dqkv_ref[:, pl.ds(h * HEAD_DIM, HEAD_DIM)] = dq
            dqkv_ref[:, pl.ds(gw + h * HEAD_DIM, HEAD_DIM)] = dk
            dqkv_ref[:, pl.ds(2 * gw + h * HEAD_DIM, HEAD_DIM)] = dv
            dstate[h] = ds0
        row = lax.broadcasted_iota(jnp.int32, (n, n), 0)
        colm = lax.broadcasted_iota(jnp.int32, (n, n), 1)
        draw = _dot_mask((row >= colm).astype(BF16), dgates, ta=True)
        dbg_ref[...] = jnp.where(lane < heads, dgates, draw)

    last = nchunk - 1
    return pl.pallas_call(
        body, name="gdn_core_bwd", grid=(nchunk,),
        in_specs=[pl.BlockSpec((CHUNK, 3 * gw), lambda i: (last - i, 0)),
                  pl.BlockSpec((CHUNK, HEAD_DIM), lambda i: (last - i, 0)),
                  pl.BlockSpec((heads, 1, HEAD_DIM, HEAD_DIM), lambda i: (0, last - i, 0, 0))]
        + [pl.BlockSpec((CHUNK, gw), lambda i: (last - i, 0))] * 4,
        out_specs=[pl.BlockSpec((CHUNK, 3 * gw), lambda i: (last - i, 0)),
                   pl.BlockSpec((CHUNK, HEAD_DIM), lambda i: (last - i, 0))],
        out_shape=[jax.ShapeDtypeStruct((t, 3 * gw), F32), jax.ShapeDtypeStruct((t, HEAD_DIM), F32)],
        scratch_shapes=[pltpu.VMEM((heads, HEAD_DIM, HEAD_DIM), F32)],
        compiler_params=_params("arbitrary"),
    )(qkv, bg, states, *solved, do)


def _gdn_post(o, proj, z_col0, norm_w, heads, tt):
    t = o.shape[0]
    zb = z_col0 // HEAD_DIM

    def body(o_ref, z_ref, w_ref, out_ref):
        ov = o_ref[...]
        z = z_ref[...]
        rms = lax.rsqrt(jnp.mean(ov * ov, axis=-1, keepdims=True) + NORM_EPS)
        out_ref[...] = (ov * rms * w_ref[...] * (z * _sigmoid(z))).astype(BF16)

    return pl.pallas_call(
        body, name="gdn_post", grid=(t // tt, heads),
        in_specs=[pl.BlockSpec((tt, HEAD_DIM), lambda i, h: (i, h)),
                  pl.BlockSpec((tt, HEAD_DIM), lambda i, h: (i, zb + h)),
                  pl.BlockSpec((1, HEAD_DIM), lambda i, h: (0, 0))],
        out_specs=pl.BlockSpec((tt, HEAD_DIM), lambda i, h: (i, h)),
        out_shape=jax.ShapeDtypeStruct((t, heads * HEAD_DIM), BF16),
        compiler_params=_params("parallel", "parallel"),
    )(o, proj, norm_w)


def _gdn_post_backward(dcat, o, proj, z_col0, norm_w, heads, tt):
    t = o.shape[0]
    zb = z_col0 // HEAD_DIM

    def body(d_ref, o_ref, z_ref, w_ref, do_ref, dz_ref, dw_ref):
        d = d_ref[...]
        ov = o_ref[...]
        z = z_ref[...]
        w = w_ref[...]
        rms = lax.rsqrt(jnp.mean(ov * ov, axis=-1, keepdims=True) + NORM_EPS)
        ohat = ov * rms
        sg = _sigmoid(z)
        gate = z * sg
        dz_ref[...] = (d * ohat * w * (sg * (1.0 + z * (1.0 - sg)))).astype(BF16)
        don = d * gate
        dohat = don * w
        do_ref[...] = rms * (dohat - ohat * jnp.mean(dohat * ohat, axis=-1, keepdims=True))
        dw = jnp.sum(don * ohat, axis=0, keepdims=True)
        first = jnp.logical_and(pl.program_id(0) == 0, pl.program_id(1) == 0)

        @pl.when(first)
        def _():
            dw_ref[...] = dw

        @pl.when(jnp.logical_not(first))
        def _():
            dw_ref[...] += dw

    blk = pl.BlockSpec((tt, HEAD_DIM), lambda i, h: (i, h))
    return pl.pallas_call(
        body, name="gdn_post_bwd", grid=(t // tt, heads),
        in_specs=[blk, blk, pl.BlockSpec((tt, HEAD_DIM), lambda i, h: (i, zb + h)),
                  pl.BlockSpec((1, HEAD_DIM), lambda i, h: (0, 0))],
        out_specs=[blk, blk, pl.BlockSpec((1, HEAD_DIM), lambda i, h: (0, 0))],
        out_shape=[jax.ShapeDtypeStruct((t, heads * HEAD_DIM), F32),
                   jax.ShapeDtypeStruct((t, heads * HEAD_DIM), BF16),
                   jax.ShapeDtypeStruct((1, HEAD_DIM), F32)],
        compiler_params=_params("arbitrary", "arbitrary"),
    )(dcat, o, proj, norm_w)


def _pool_select(levels, group):
    out = levels[-1]
    for gi in range(len(levels) - 2, -1, -1):
        out = jnp.where(group == gi, levels[gi], out)
    return out


def _pool_counts(t, width, group):
    pos = lax.broadcasted_iota(jnp.int32, (t, width), 0)
    win = jnp.left_shift(2, group)
    return jnp.minimum(pos + 1, win).astype(F32)


def _pooled(p, group):
    levels, s, step = [], p, 1
    for _ in POOL_WINDOWS:
        s = s + _shift_down(s, step)
        levels.append(s)
        step *= 2
    cnt = _pool_counts(p.shape[0], p.shape[1], group)
    return _pool_select(levels, group) / cnt - p, cnt


def _pool_forward(proj, p_col0, pool_w, pool_scale):
    t = proj.shape[0]
    groups, cg, _ = pool_w.shape
    pb = p_col0 // cg

    def body(p_ref, w_ref, s_ref, o_ref):
        pooled, _ = _pooled(p_ref[...], pl.program_id(0))
        o_ref[...] = (_bdot(pooled, w_ref[0]) * s_ref[...]).astype(BF16)

    return pl.pallas_call(
        body, name="pool_fwd", grid=(groups,),
        in_specs=[pl.BlockSpec((t, cg), lambda g: (0, pb + g)), pl.BlockSpec((1, cg, cg), lambda g: (g, 0, 0)),
                  pl.BlockSpec((1, cg), lambda g: (0, g))],
        out_specs=pl.BlockSpec((t, cg), lambda g: (0, g)),
        out_shape=jax.ShapeDtypeStruct((t, groups * cg), BF16),
        compiler_params=_params("parallel"),
    )(proj, pool_w, pool_scale)


def _pool_backward(dcat, d_col0, proj, p_col0, pool_w, pool_scale):
    t = proj.shape[0]
    groups, cg, _ = pool_w.shape
    pb = p_col0 // cg
    db = d_col0 // cg

    def body(d_ref, p_ref, w_ref, s_ref, dp_ref, dw_ref, ds_ref):
        group = pl.program_id(0)
        pooled, cnt = _pooled(p_ref[...], group)
        w = w_ref[0]
        d = d_ref[...]
        mixed = _bdot(pooled, w)
        ds_ref[...] = jnp.sum(d * mixed, axis=0, keepdims=True)
        dmixed = d * s_ref[...]
        dw_ref[0] = _bdot(pooled, dmixed, ta=True)
        dpooled = _bdot(dmixed, w, tb=True)
        levels, s, step = [], dpooled / cnt, 1
        for _ in POOL_WINDOWS:
            s = s + _shift_up(s, step)
            levels.append(s)
            step *= 2
        dp_ref[...] = (_pool_select(levels, group) - dpooled).astype(BF16)

    return pl.pallas_call(
        body, name="pool_bwd", grid=(groups,),
        in_specs=[pl.BlockSpec((t, cg), lambda g: (0, db + g)), pl.BlockSpec((t, cg), lambda g: (0, pb + g)),
                  pl.BlockSpec((1, cg, cg), lambda g: (g, 0, 0)), pl.BlockSpec((1, cg), lambda g: (0, g))],
        out_specs=[pl.BlockSpec((t, cg), lambda g: (0, g)), pl.BlockSpec((1, cg, cg), lambda g: (g, 0, 0)),
                   pl.BlockSpec((1, cg), lambda g: (0, g))],
        out_shape=[jax.ShapeDtypeStruct((t, groups * cg), BF16), jax.ShapeDtypeStruct((groups, cg, cg), F32),
                   jax.ShapeDtypeStruct((1, groups * cg), F32)],
        compiler_params=_params("parallel"),
    )(dcat, proj, pool_w, pool_scale)


def _attention(q, k, v, tq):
    t, d = q.shape
    m = k.shape[0]
    dh = d // XATTN_HEADS
    scale = dh ** -0.5

    def body(q_ref, k_ref, v_ref, o_ref):
        s = _bdot(q_ref[...], k_ref[...], tb=True) * scale
        s = s - jnp.max(s, axis=-1, keepdims=True)
        e = jnp.exp(s)
        p = e / jnp.sum(e, axis=-1, keepdims=True)
        o_ref[...] = _bdot(p, v_ref[...]).astype(BF16)

    return pl.pallas_call(
        body, name="xattn_fwd", grid=(XATTN_HEADS, t // tq),
        in_specs=[pl.BlockSpec((tq, dh), lambda h, i: (i, h)), pl.BlockSpec((m, dh), lambda h, i: (0, h)),
                  pl.BlockSpec((m, dh), lambda h, i: (0, h))],
        out_specs=pl.BlockSpec((tq, dh), lambda h, i: (i, h)),
        out_shape=jax.ShapeDtypeStruct((t, d), BF16),
        compiler_params=_params("parallel", "parallel"),
    )(q, k, v)


def _attention_backward(q, k, v, do, tq):
    t, d = q.shape
    m = k.shape[0]
    dh = d // XATTN_HEADS
    scale = dh ** -0.5

    def body(q_ref, k_ref, v_ref, do_ref, dq_ref, dk_ref, dv_ref, dk_acc, dv_acc):
        i = pl.program_id(1)
        qv, kv, vv, dov = q_ref[...], k_ref[...], v_ref[...], do_ref[...]
        s = _bdot(qv, kv, tb=True) * scale
        s = s - jnp.max(s, axis=-1, keepdims=True)
        e = jnp.exp(s)
        p = e / jnp.sum(e, axis=-1, keepdims=True)
        dp = _bdot(dov, vv, tb=True)
        ds = p * (dp - jnp.sum(dp * p, axis=-1, keepdims=True)) * scale
        dq_ref[...] = _bdot(ds, kv).astype(BF16)
        dv_part = _bdot(p, dov, ta=True)
        dk_part = _bdot(ds, qv, ta=True)

        @pl.when(i == 0)
        def _():
            dk_acc[...] = dk_part
            dv_acc[...] = dv_part

        @pl.when(i > 0)
        def _():
            dk_acc[...] += dk_part
            dv_acc[...] += dv_part

        @pl.when(i == pl.num_programs(1) - 1)
        def _():
            dk_ref[...] = dk_acc[...].astype(BF16)
            dv_ref[...] = dv_acc[...].astype(BF16)

    qblk = pl.BlockSpec((tq, dh), lambda h, i: (i, h))
    kblk = pl.BlockSpec((m, dh), lambda h, i: (0, h))
    return pl.pallas_call(
        body, name="xattn_bwd", grid=(XATTN_HEADS, t // tq),
        in_specs=[qblk, kblk, kblk, qblk],
        out_specs=[qblk, kblk, kblk],
        out_shape=[jax.ShapeDtypeStruct((t, d), BF16), jax.ShapeDtypeStruct((m, d), BF16),
                   jax.ShapeDtypeStruct((m, d), BF16)],
        scratch_shapes=[pltpu.VMEM((m, dh), F32), pltpu.VMEM((m, dh), F32)],
        compiler_params=_params("parallel", "arbitrary"),
    )(q, k, v, do)


def _ln_backward_rows(name, dmain, dres, xhat, rstd, gamma, tm):
    t, d = xhat.shape

    def body(m_ref, r_ref, x_ref, s_ref, g_ref, du_ref, dub_ref, dg_ref, db_ref):
        du, dg, db = _ln_backward_math(m_ref[...] + ALPHA * r_ref[...], x_ref[...], s_ref[...], g_ref[...])
        du_ref[...] = du
        dub_ref[...] = du.astype(BF16)
        first = pl.program_id(0) == 0

        @pl.when(first)
        def _():
            dg_ref[...] = dg
            db_ref[...] = db

        @pl.when(jnp.logical_not(first))
        def _():
            dg_ref[...] += dg
            db_ref[...] += db

    row = pl.BlockSpec((tm, d), lambda i: (i, 0))
    vec = pl.BlockSpec((1, d), lambda i: (0, 0))
    return pl.pallas_call(
        body, name=name, grid=(t // tm,),
        in_specs=[row, row, row, pl.BlockSpec((tm, 1), lambda i: (i, 0)), vec],
        out_specs=[row, row, vec, vec],
        out_shape=[jax.ShapeDtypeStruct((t, d), F32), jax.ShapeDtypeStruct((t, d), BF16),
                   jax.ShapeDtypeStruct((1, d), F32), jax.ShapeDtypeStruct((1, d), F32)],
        compiler_params=_params("arbitrary"),
    )(dmain, dres, xhat, rstd, gamma)


def _loss_and_ln_backward(xhat, rstd, gamma, beta, target, tm):
    t, d = xhat.shape

    def body(x_ref, r_ref, g_ref, b_ref, t_ref, du_ref, dub_ref, dg_ref, db_ref, loss_ref):
        xh = x_ref[...]
        g = g_ref[...]
        diff = xh * g + b_ref[...] - t_ref[...]
        part = jnp.sum(jnp.sum(diff * diff, axis=1, keepdims=True), axis=0, keepdims=True) * (0.5 / d)
        dy = diff * (1.0 / d)
        du, dg, db = _ln_backward_math(dy, xh, r_ref[...], g)
        du_ref[...] = du
        dub_ref[...] = du.astype(BF16)
        lossrow = jnp.broadcast_to(part, (1, HEAD_DIM))
        first = pl.program_id(0) == 0

        @pl.when(first)
        def _():
            dg_ref[...] = dg
            db_ref[...] = db
            loss_ref[...] = lossrow

        @pl.when(jnp.logical_not(first))
        def _():
            dg_ref[...] += dg
            db_ref[...] += db
            loss_ref[...] += lossrow

    row = pl.BlockSpec((tm, d), lambda i: (i, 0))
    vec = pl.BlockSpec((1, d), lambda i: (0, 0))
    return pl.pallas_call(
        body, name="loss_ln3_bwd", grid=(t // tm,),
        in_specs=[row, pl.BlockSpec((tm, 1), lambda i: (i, 0)), vec, vec, row],
        out_specs=[row, row, vec, vec, pl.BlockSpec((1, HEAD_DIM), lambda i: (0, 0))],
        out_shape=[jax.ShapeDtypeStruct((t, d), F32), jax.ShapeDtypeStruct((t, d), BF16),
                   jax.ShapeDtypeStruct((1, d), F32), jax.ShapeDtypeStruct((1, d), F32),
                   jax.ShapeDtypeStruct((1, HEAD_DIM), F32)],
        compiler_params=_params("arbitrary"),
    )(xhat, rstd, gamma, beta, target)


def _after(token, a):
    return a if token is None else a + token[:1, :1].astype(a.dtype)


def _pick(n, prefs):
    for p in prefs:
        if n % p == 0:
            return p
    return n


def _local_step(x, mem, target, w, token=None):
    t, d = x.shape
    heads = w["a_log"].shape[1]
    gw = heads * HEAD_DIM
    groups, cg, _ = w["pool_w"].shape
    pw = groups * cg
    n_main = 4 * gw + pw
    in_cols = n_main + 2 * heads
    s_in = w["w_in_t"].shape[0]

    tm = _pick(t, (512, 256, 128))
    tm_ln = _pick(t, (256, 128))
    tm_big = _pick(t, (1024, 512, 256, 128))
    tk = _pick(d, K_STEPS)

    w_in_t = w["w_in_t"].reshape(in_cols, d)
    w_p_t = w_in_t[4 * gw + 2 * heads:]
    w_ba_t = jnp.pad(w_in_t[4 * gw:4 * gw + 2 * heads], ((0, HEAD_DIM - 2 * heads), (0, 0)))
    x_bf = _after(token, x).astype(BF16)
    mem_bf = _after(token, mem).astype(BF16)

    tn_d = _pick(d, (1024, 512, 256, 128))
    proj = _plain("proj_main", x_bf, w_in_t, tb=True, n_used=4 * gw, tm=tm_big, tn=_pick(4 * gw, (1024, 512, 256, 128)),
                  tk=tk, out_dtype=F32)
    pproj = _plain("proj_pool", x_bf, w_p_t, tb=True, tm=tm_big, tn=_pick(pw, (1024, 512, 256, 128)), tk=tk, out_dtype=F32)
    ea, dtb = _gate_vectors(w["a_log"], w["dt_bias"], heads)
    vec128 = lambda i, j: (0, 0)
    ba, bg = _matmul(
        "proj_gates", x_bf, w_ba_t, tb=True, tm=tm, tn=HEAD_DIM, tk=tk,
        extra=[(ea, (1, HEAD_DIM), vec128), (dtb, (1, HEAD_DIM), vec128)],
        outs=[(jax.ShapeDtypeStruct((t, HEAD_DIM), F32), (tm, HEAD_DIM), _tile)] * 2,
        epilogue=_gates_epilogue(heads))
    qkv = _gdn_pre(proj, w["conv_w"], heads)
    o_gdn, states, *solved = _gdn_core(qkv, bg, heads)
    cat_g = _gdn_post(o_gdn, proj, 3 * gw, w["gdn_norm_w"], heads, tm)
    cat_p = _pool_forward(pproj, 0, w["pool_w"], w["pool_scale"])
    cat = jnp.concatenate([cat_g, cat_p], axis=1)
    w = {**w, **(yield ("weights", 1, cat))}
    h1, h1_bf, xhat1, rstd1 = _ln_forward("mix_ln1", cat, w["w_out"], x, w["ln1_g"], w["ln1_b"], tm=tm_ln, tk=tk)

    h1_bf = _after((yield ("relay", None, h1_bf)), h1_bf)
    q = _plain("xattn_q", h1_bf, w["xq_w"], tm=tm, tn=tn_d, tk=tk, out_dtype=BF16)
    mlen = mem.shape[0]
    tm_mem = _pick(mlen, (256, 128))
    k = _plain("xattn_k", mem_bf, w["xk_w"], tm=tm_mem, tn=tn_d, tk=tk, out_dtype=BF16)
    v = _plain("xattn_v", mem_bf, w["xv_w"], tm=tm_mem, tn=tn_d, tk=tk, out_dtype=BF16)
    att = _attention(q, k, v, tm)
    h2, h2_bf, xhat2, rstd2 = _ln_forward("xo_ln2", att, w["xo_w"], h1, w["ln2_g"], w["ln2_b"], tm=tm_ln, tk=tk)

    w = {**w, **(yield ("weights", 2, h2_bf))}
    s_up = w["w_up3"].shape[0]
    ff = s_up * w["w_up3"].shape[2]
    tn_f = _pick(ff // s_up, (1024, 512, 256, 128))

    def up_epi(acc, ex, out, i):
        r = jnp.maximum(acc, 0.0)
        out[0][...] = (r * r).astype(BF16)
        out[1][...] = (2.0 * r).astype(BF16)

    act, act_grad = _matmul(
        "mlp_up", h2_bf, w["w_up3"], b_blocks=s_up, tm=tm_big, tn=tn_f, tk=tk,
        outs=[(jax.ShapeDtypeStruct((t, ff), BF16), (tm_big, tn_f), _tile)] * 2, epilogue=up_epi)
    w = {**w, **(yield ("weights", 3, act))}
    tk_f = _pick(ff, K_STEPS)
    xhat3, rstd3 = _ln_forward("down_ln3", act, w["w_down"], h2, w["ln3_g"], w["ln3_b"], tm=tm, tk=tk_f, want_h=False)

    grads = {}
    du3, du3_bf, grads["ln3_g"], grads["ln3_b"], loss = _loss_and_ln_backward(
        xhat3, rstd3, w["ln3_g"], w["ln3_b"], target, tm_ln)

    def dup_epi(acc, ex, out, i):
        out[0][...] = (acc * ex[0][...].astype(F32)).astype(BF16)

    dup = _matmul(
        "mlp_down_dx", du3_bf, w["w_down"], tb=True, tm=tm_big, tn=tn_f, tk=tk,
        extra=[(act_grad, (tm_big, tn_f), _tile)],
        outs=[(jax.ShapeDtypeStruct((t, ff), BF16), (tm_big, tn_f), _tile)], epilogue=dup_epi)[0]
    tk_t = _pick(t, K_STEPS)
    tm_w = _pick(d, (512, 256, 128))
    grads["w_down"] = _plain("mlp_down_dw", act, du3_bf, ta=True, tm=_pick(ff, (512, 256, 128)), tn=d, tk=tk_t,
                             out_dtype=F32)
    grads["w_up3"] = _plain("mlp_up_dw", h2_bf, dup, ta=True, tm=tm_w, tn=ff // s_up, tk=tk_t, out_dtype=F32, out3=s_up,
                            n_outer=True)
    token = yield ("grads", 0, {n: grads.pop(n) for n in ("w_down", "w_up3")})
    dh2 = _plain("mlp_up_dx", dup, w["w_up3"], tb=True, b_blocks=s_up, tm=tm_big, tn=tn_d,
                 tk=_pick(ff // s_up, K_STEPS), out_dtype=F32)
    du2, du2_bf, grads["ln2_g"], grads["ln2_b"] = _ln_backward_rows(
        "ln2_bwd", dh2, du3, xhat2, rstd2, _after(token, w["ln2_g"]), tm_ln)
    token = yield ("poll", 0, du2_bf)

    grads["xo_w"] = _plain("xo_dw", att, du2_bf, ta=True, tm=tm_w, tn=d, tk=tk_t, out_dtype=F32)
    datt = _plain("xo_dx", du2_bf, w["xo_w"], tb=True, tm=tm, tn=tn_d, tk=tk, out_dtype=BF16)
    dq, dk, dv = _attention_backward(q, k, v, datt, tm)
    tk_m = _pick(mlen, (256, 128))
    grads["xq_w"] = _plain("xq_dw", h1_bf, dq, ta=True, tm=tm_w, tn=d, tk=tk_t, out_dtype=F32)
    grads["xk_w"] = _plain("xk_dw", mem_bf, dk, ta=True, tm=tm_w, tn=tn_d, tk=tk_m, out_dtype=F32)
    grads["xv_w"] = _plain("xv_dw", mem_bf, dv, ta=True, tm=tm_w, tn=tn_d, tk=tk_m, out_dtype=F32)
    du1, du1_bf, grads["ln1_g"], grads["ln1_b"] = _ln_backward(
        "xq_dx_ln1", dq, w["xq_w"], du2, xhat1, rstd1, _after(token, w["ln1_g"]), tm=tm_ln, tk=tk)

    grads["w_out"] = _plain("out_dw", cat, du1_bf, ta=True, tm=tm_w, tn=d, tk=tk_t, out_dtype=F32)
    token = yield ("grads", 1, {n: grads.pop(n) for n in ("xo_w", "xq_w", "xk_w", "xv_w", "w_out")})
    dcat = _plain("out_dx", du1_bf, w["w_out"], tb=True, tm=tm, tn=tn_d, tk=tk, out_dtype=F32)
    dp, grads["pool_w"], grads["pool_scale"] = _pool_backward(dcat, gw, pproj, 0, w["pool_w"],
                                                              _after(token, w["pool_scale"]))
    do_gdn, dz, grads["gdn_norm_w"] = _gdn_post_backward(dcat, o_gdn, proj, 3 * gw, _after(token, w["gdn_norm_w"]),
                                                         heads, tm)
    dqkv, dbg = _gdn_core_backward(qkv, bg, states, solved, do_gdn, heads)
    token = yield ("poll", 1, dqkv)
    dqkv_pre, grads["conv_w"] = _gdn_pre_backward(proj, _after(token, w["conv_w"]), dqkv, heads)
    dba, dalog_row, ddt_row = _gates_backward(ba, bg, dbg, ea, dtb, heads)
    grads["a_log"] = dalog_row[:, heads:2 * heads]
    grads["dt_bias"] = ddt_row[:, heads:2 * heads]

    dproj = jnp.concatenate([dqkv_pre, dz, dp], axis=1)
    dw_main = _plain("proj_dw", dproj, x_bf, ta=True, tm=_pick(n_main, (512, 256, 128)), tn=d, tk=tk_t, out_dtype=F32)
    dw_ba = _plain("proj_gates_dw", dba, x_bf, ta=True, tm=HEAD_DIM, tn=tn_d, tk=tk_t, out_dtype=F32)
    dw_in_t = jnp.concatenate([dw_main[:4 * gw], dw_ba[:2 * heads], dw_main[4 * gw:]], axis=0)
    grads["w_in_t"] = dw_in_t.reshape(s_in, in_cols // s_in, d)

    def dx_epi(acc, ex, out, i):
        out[0][...] = acc + ex[1][...] + ALPHA * ex[0][...]

    def add_epi(acc, ex, out, i):
        out[0][...] = acc + ex[0][...]

    token = yield ("grads", 2, {n: grads.pop(n) for n in ("w_in_t", "pool_w")})
    dx_gates = _plain("proj_gates_dx", dba, _after(token, w_ba_t), tm=tm, tn=tn_d, tk=HEAD_DIM, out_dtype=F32)
    out_tile = [(jax.ShapeDtypeStruct((t, d), F32), (tm, tn_d), _tile)]
    dx_pool = _matmul("proj_pool_dx", dp, w_p_t, tm=tm, tn=tn_d, tk=_pick(pw, K_STEPS),
                      extra=[(dx_gates, (tm, tn_d), _tile)], outs=out_tile, epilogue=add_epi)[0]
    grad_x = _matmul(
        "proj_dx", dproj, w_in_t, k_used=4 * gw, tm=tm, tn=tn_d, tk=_pick(4 * gw, K_STEPS),
        extra=[(du1, (tm, tn_d), _tile), (dx_pool, (tm, tn_d), _tile)], outs=out_tile, epilogue=dx_epi)[0]
    yield ("poll", 2, grad_x)
    return loss, grad_x, grads


def _adamw(name, w, g, m, v):
    r, c = w.shape
    if r % 8 == 0:
        tr = _pick(r, (256, 128, 64, 32, 16, 8))
        blk, steps = pl.BlockSpec((tr, c), lambda i: (i, 0)), r // tr
    else:
        tc = _pick(c, (256, 128))
        blk, steps = pl.BlockSpec((r, tc), lambda i: (0, i)), c // tc
    c1 = 1.0 - ADAM_B1 ** ADAM_STEP
    c2 = 1.0 - ADAM_B2 ** ADAM_STEP

    def body(w_ref, g_ref, m_ref, v_ref, d_ref, mo_ref, vo_ref):
        gv = g_ref[...]
        mn = ADAM_B1 * m_ref[...] + (1.0 - ADAM_B1) * gv
        vn = ADAM_B2 * v_ref[...] + (1.0 - ADAM_B2) * (gv * gv)
        d_ref[...] = -ADAM_LR * ((mn / c1) / (jnp.sqrt(vn / c2) + ADAM_EPS) + ADAM_WD * w_ref[...])
        mo_ref[...] = mn
        vo_ref[...] = vn

    return pl.pallas_call(
        body, name=name, grid=(steps,), in_specs=[blk] * 4, out_specs=[blk] * 3,
        out_shape=[jax.ShapeDtypeStruct((r, c), F32)] * 3,
        compiler_params=_params("parallel"),
    )(w, g, m, v)


def _place():
    x, y, c = lax.axis_index("x"), lax.axis_index("y"), lax.axis_index("c")
    chips = [(1 - x, y), (x, 1 - y), (1 - x, 1 - y)]
    return x, y, c, chips


HBM = pl.BlockSpec(memory_space=pltpu.HBM)


SEM = pl.BlockSpec(memory_space=pltpu.SEMAPHORE)
ANY = pl.BlockSpec(memory_space=pl.ANY)
EFFECT = pltpu.SideEffectType.DATAFLOW_SIDE_EFFECTING


def _in_hbm(a):
    return pltpu.with_memory_space_constraint(a, pltpu.HBM)


def _remote(src, dst, send_sem, recv_sem, to):
    return pltpu.make_async_remote_copy(src_ref=src, dst_ref=dst, send_sem=send_sem, recv_sem=recv_sem,
                                        device_id=to, device_id_type=MESH)


def _by_rows(rows):
    return rows % 32 == 0


def _half_shape(rows, cols):
    return (rows // 2, cols) if _by_rows(rows) else (rows, cols // 2)


def _half(ref, which, *lead):
    rows, cols = ref.shape[-2:]
    if _by_rows(rows):
        return ref.at[(*lead, pl.ds(which * (rows // 2), rows // 2))]
    return ref.at[(*lead, slice(None), pl.ds(which * (cols // 2), cols // 2))]


def _landed(lands, i, shard_index, which):
    return _half(lands[i], which, shard_index)


def _routes():
    x, y, c, _ = _place()
    first = (jnp.where(c == 0, 1 - x, x), jnp.where(c == 0, y, 1 - y))
    second = (jnp.where(c == 0, x, 1 - x), jnp.where(c == 0, 1 - y, y))
    return first, second, (1 - x, 1 - y)


def _shard_of(chip):
    return 2 * chip[0] + chip[1]


def _gather_start(name, shards, after, relayed=()):
    n = len(shards)
    lands = [lax.empty((N_SHARD,) + s.shape, s.dtype) for s in shards]

    def body(*refs):
        ins, zones = refs[:n], refs[n:2 * n]
        ici_send, ici_recv, own_send, own_recv = refs[2 * n + 1:2 * n + 5]
        token = refs[-1]
        x, y, c, chips = _place()
        me = 2 * x + y
        first, _, _ = _routes()
        for i in range(n):
            if i in relayed:
                _remote(_half(ins[i], c), _landed(zones, i, me, c), ici_send.at[3 * i], ici_recv.at[3 * i],
                        (*first, c)).start()
                continue
            for j, chip in enumerate(chips):
                _remote(_half(ins[i], c), _landed(zones, i, me, c), ici_send.at[3 * i + j],
                        ici_recv.at[3 * i + j], (*chip, c)).start()
        for i in range(n):
            _remote(ins[i], zones[i].at[me], own_send.at[i], own_recv.at[i], (x, y, 1 - c)).start()
        token[...] = jnp.zeros_like(token)

    dma = pltpu.SemaphoreType.DMA
    outs = pl.pallas_call(
        body, name=name,
        in_specs=[HBM] * (2 * n) + [ANY],
        out_shape=(dma((3 * n,)), dma((3 * n,)), dma((n,)), dma((n,)),
                   *[pltpu.HBM(a.shape, a.dtype) for a in shards + lands], jax.ShapeDtypeStruct((8, LANES), F32)),
        out_specs=(SEM, SEM, SEM, SEM, *[HBM] * (2 * n), pl.BlockSpec(memory_space=pltpu.VMEM)),
        input_output_aliases={k: 4 + k for k in range(2 * n)},
        compiler_params=pltpu.CompilerParams(has_side_effects=EFFECT),
    )(*[_in_hbm(a) for a in shards + lands], after)
    sems = dict(zip(("ici_send", "ici_recv", "own_send", "own_recv"), outs[:4]))
    return sems, list(outs[4:4 + n]), list(outs[4 + n:4 + 2 * n]), outs[-1]


def _gather_forward(name, idx, lands, sems, after):
    n = len(idx)

    def body(*refs):
        zones = refs[:n]
        ici_recv = refs[n]
        fwd_send, fwd_recv = refs[n + 2], refs[n + 3]
        x, y, c, chips = _place()
        for k, i in enumerate(idx):
            for j, chip in enumerate(chips):
                half = _landed(zones, k, 2 * chip[0] + chip[1], c)
                _remote(half, half, fwd_send.at[3 * k + j], ici_recv.at[3 * i + j], (*chip, c)).wait_recv()
                _remote(half, half, fwd_send.at[3 * k + j], fwd_recv.at[3 * k + j], (x, y, 1 - c)).start()

    dma = pltpu.SemaphoreType.DMA
    outs = pl.pallas_call(
        body, name=name,
        in_specs=[HBM] * n + [SEM, ANY],
        out_shape=(dma((3 * n,)), dma((3 * n,)), *[pltpu.HBM(a.shape, a.dtype) for a in lands]),
        out_specs=(SEM, SEM, *[HBM] * n),
        input_output_aliases={k: 2 + k for k in range(n)},
        compiler_params=pltpu.CompilerParams(has_side_effects=EFFECT),
    )(*lands, sems["ici_recv"], after)
    return (outs[0], outs[1]), list(outs[2:])


def _gather_wait(name, idx, shards, lands, sems, fwd, after):
    n = len(idx)

    def body(*refs):
        ins, zones = refs[:n], refs[n:2 * n]
        ici_send, own_send, own_recv, fwd_send, fwd_recv = refs[2 * n:2 * n + 5]
        x, y, c, chips = _place()
        me = 2 * x + y
        for k, i in enumerate(idx):
            mine = _half(ins[k], c)
            for j, chip in enumerate(chips):
                theirs = 2 * chip[0] + chip[1]
                _remote(mine, _landed(zones, k, me, c), ici_send.at[3 * i + j], fwd_recv.at[3 * k + j],
                        (*chip, c)).wait_send()
                sent = _landed(zones, k, theirs, c)
                _remote(sent, sent, fwd_send.at[3 * k + j], fwd_recv.at[3 * k + j], (x, y, 1 - c)).wait_send()
                passed = _landed(zones, k, theirs, 1 - c)
                _remote(passed, passed, fwd_send.at[3 * k + j], fwd_recv.at[3 * k + j], (x, y, 1 - c)).wait_recv()
            own = _remote(ins[k], zones[k].at[me], own_send.at[i], own_recv.at[i], (x, y, 1 - c))
            own.wait_send()
            own.wait_recv()

    outs = pl.pallas_call(
        body, name=name,
        in_specs=[HBM] * (2 * n) + [SEM] * 5 + [ANY],
        out_shape=tuple(pltpu.HBM(a.shape, a.dtype) for a in lands),
        out_specs=tuple([HBM] * n),
        input_output_aliases={n + k: k for k in range(n)},
        compiler_params=pltpu.CompilerParams(has_side_effects=EFFECT),
    )(*shards, *lands, sems["ici_send"], sems["own_send"], sems["own_recv"], fwd[0], fwd[1], after)
    return list(outs)


def _gather_relay(name, idx, shards, lands, sems, after):
    n = len(idx)

    def body(*refs):
        ins, zones, ici_recv = refs[:n], refs[n:2 * n], refs[2 * n]
        relay_send, relay_recv, pass_send, pass_recv = refs[2 * n + 2:2 * n + 6]
        x, y, c, _ = _place()
        first, second, _ = _routes()
        for k, i in enumerate(idx):
            landed = _landed(zones, k, _shard_of(first), c)
            _remote(landed, landed, pass_send.at[k], ici_recv.at[3 * i], (*first, c)).wait_recv()
            _remote(_half(ins[k], c), _landed(zones, k, 2 * x + y, c), relay_send.at[2 * k], relay_recv.at[2 * k],
                    (*second, c)).start()
            _remote(landed, landed, relay_send.at[2 * k + 1], relay_recv.at[2 * k + 1], (*second, c)).start()
            _remote(landed, landed, pass_send.at[k], pass_recv.at[k], (x, y, 1 - c)).start()
        refs[-1][...] = jnp.zeros_like(refs[-1])

    dma = pltpu.SemaphoreType.DMA
    outs = pl.pallas_call(
        body, name=name,
        in_specs=[HBM] * (2 * n) + [SEM, ANY],
        out_shape=(dma((2 * n,)), dma((2 * n,)), dma((n,)), dma((n,)), *[pltpu.HBM(a.shape, a.dtype) for a in lands],
                   jax.ShapeDtypeStruct((8, LANES), F32)),
        out_specs=(SEM, SEM, SEM, SEM, *[HBM] * n, pl.BlockSpec(memory_space=pltpu.VMEM)),
        input_output_aliases={n + k: 4 + k for k in range(n)},
        compiler_params=pltpu.CompilerParams(has_side_effects=EFFECT),
    )(*shards, *lands, sems["ici_recv"], after)
    return outs[:4], list(outs[4:4 + n]), outs[-1]


def _gather_forward_relayed(name, ks, lands, relay, after):
    n = len(ks)

    def body(*refs):
        zones, relay_recv = refs[:n], refs[n]
        fwd_send, fwd_recv = refs[n + 2], refs[n + 3]
        x, y, c, _ = _place()
        _, second, diagonal = _routes()
        for p, k in enumerate(ks):
            for j, chip in enumerate((second, diagonal)):
                landed = _landed(zones, p, _shard_of(chip), c)
                _remote(landed, landed, fwd_send.at[2 * p + j], relay_recv.at[2 * k + j], (*second, c)).wait_recv()
                _remote(landed, landed, fwd_send.at[2 * p + j], fwd_recv.at[2 * p + j], (x, y, 1 - c)).start()

    dma = pltpu.SemaphoreType.DMA
    outs = pl.pallas_call(
        body, name=name,
        in_specs=[HBM] * n + [SEM, ANY],
        out_shape=(dma((2 * n,)), dma((2 * n,)), *[pltpu.HBM(a.shape, a.dtype) for a in lands]),
        out_specs=(SEM, SEM, *[HBM] * n),
        input_output_aliases={k: 2 + k for k in range(n)},
        compiler_params=pltpu.CompilerParams(has_side_effects=EFFECT),
    )(*lands, relay[1], after)
    return (outs[0], outs[1]), list(outs[2:])


def _gather_wait_relayed(name, idx, ks, shards, lands, sems, relay, fwd, after):
    n = len(idx)

    def body(*refs):
        ins, zones = refs[:n], refs[n:2 * n]
        ici_send, own_send, own_recv, relay_send, pass_send, pass_recv, fwd_send, fwd_recv = refs[2 * n:2 * n + 8]
        x, y, c, _ = _place()
        me = 2 * x + y
        sibling = (x, y, 1 - c)
        first, second, diagonal = _routes()
        for p, (i, k) in enumerate(zip(idx, ks)):
            mine, at_peer = _half(ins[p], c), _landed(zones, p, me, c)
            from_first = _landed(zones, p, _shard_of(first), c)
            _remote(mine, at_peer, ici_send.at[3 * i], pass_recv.at[k], (*first, c)).wait_send()
            _remote(mine, at_peer, relay_send.at[2 * k], pass_recv.at[k], (*second, c)).wait_send()
            _remote(from_first, from_first, relay_send.at[2 * k + 1], pass_recv.at[k], (*second, c)).wait_send()
            _remote(from_first, from_first, pass_send.at[k], pass_recv.at[k], sibling).wait_send()
            theirs = _landed(zones, p, _shard_of(second), 1 - c)
            _remote(theirs, theirs, pass_send.at[k], pass_recv.at[k], sibling).wait_recv()
            for j, (sent, got) in enumerate(((second, first), (diagonal, diagonal))):
                out_half = _landed(zones, p, _shard_of(sent), c)
                _remote(out_half, out_half, fwd_send.at[2 * p + j], fwd_recv.at[2 * p + j], sibling).wait_send()
                in_half = _landed(zones, p, _shard_of(got), 1 - c)
                _remote(in_half, in_half, fwd_send.at[2 * p + j], fwd_recv.at[2 * p + j], sibling).wait_recv()
            own = _remote(ins[p], zones[p].at[me], own_send.at[i], own_recv.at[i], sibling)
            own.wait_send()
            own.wait_recv()

    outs = pl.pallas_call(
        body, name=name,
        in_specs=[HBM] * (2 * n) + [SEM] * 8 + [ANY],
        out_shape=tuple(pltpu.HBM(a.shape, a.dtype) for a in lands),
        out_specs=tuple([HBM] * n),
        input_output_aliases={n + k: k for k in range(n)},
        compiler_params=pltpu.CompilerParams(has_side_effects=EFFECT),
    )(*shards, *lands, sems["ici_send"], sems["own_send"], sems["own_recv"], relay[0], relay[2], relay[3],
      fwd[0], fwd[1], after)
    return list(outs)


def _all_reduce_small(name, slab, after=None):
    r, width = slab.shape
    ndev = 8

    def body(x_ref, after_ref, out_ref, buf, send_sems, recv_sems):
        x, y, c, _ = _place()
        me = 4 * x + 2 * y + c
        buf[me] = x_ref[...]
        copies = []
        for k in range(1, ndev):
            peer = jnp.bitwise_xor(me, k)
            to = (peer // 4, (peer // 2) % 2, peer % 2)
            cp = pltpu.make_async_remote_copy(src_ref=x_ref, dst_ref=buf.at[me], send_sem=send_sems.at[k - 1],
                                              recv_sem=recv_sems.at[k - 1], device_id=to, device_id_type=MESH)
            cp.start()
            copies.append(cp)
        for k in range(1, ndev):
            peer = jnp.bitwise_xor(me, k)
            pltpu.make_async_remote_copy(src_ref=x_ref, dst_ref=buf.at[peer], send_sem=send_sems.at[k - 1],
                                         recv_sem=recv_sems.at[k - 1], device_id=(x, y, c),
                                         device_id_type=MESH).wait_recv()
        for cp in copies:
            cp.wait_send()
        total = buf[0]
        for d in range(1, ndev):
            total = total + buf[d]
        out_ref[...] = total

    return pl.pallas_call(
        body, name=name,
        in_specs=[pl.BlockSpec(memory_space=pltpu.VMEM), ANY], out_specs=pl.BlockSpec(memory_space=pltpu.VMEM),
        out_shape=jax.ShapeDtypeStruct((r, width), F32),
        scratch_shapes=[pltpu.VMEM((ndev, r, width), F32), pltpu.SemaphoreType.DMA((ndev - 1,)),
                        pltpu.SemaphoreType.DMA((ndev - 1,))],
        compiler_params=pltpu.CompilerParams(vmem_limit_bytes=VMEM_LIMIT),
    )(slab, slab if after is None else after)


def _half_tiling(rows, cols):
    if _by_rows(rows):
        tr = _pick(rows // 2, (256, 128, 64, 32, 16))
        nb = (rows // 2) // tr
        return (tr, cols), nb, (lambda which, b: (which * nb + b, 0)), (lambda b: (b, 0))
    tc = _pick(cols // 2, (256, 128))
    nb = (cols // 2) // tc
    return (rows, tc), nb, (lambda which, b: (0, which * nb + b)), (lambda b: (0, b))


def _chip_partial(name, grad, other, core):
    s, r, cdim = grad.shape
    blk, nb, whole, within = _half_tiling(r, cdim)

    def body(core_ref, g_ref, o_ref, out_ref):
        out_ref[...] = (g_ref[...] + o_ref[...]).astype(BF16)

    return pl.pallas_call(
        body, name=name,
        grid_spec=pltpu.PrefetchScalarGridSpec(
            num_scalar_prefetch=1, grid=(s, nb),
            in_specs=[pl.BlockSpec((None,) + blk, lambda j, b, core_ref: (j,) + whole(core_ref[0], b)),
                      pl.BlockSpec((None,) + blk, lambda j, b, core_ref: (j,) + within(b))],
            out_specs=pl.BlockSpec((None,) + blk, lambda j, b, core_ref: (j,) + within(b))),
        out_shape=jax.ShapeDtypeStruct((s,) + _half_shape(r, cdim), BF16),
        compiler_params=_params("parallel", "parallel"),
    )(core, grad, other)


def _partial_copies(ins, zones, send_sems, recv_sems):
    x, y, c, chips = _place()
    return [_remote(ins[i].at[2 * chip[0] + chip[1]], zones[i].at[j], send_sems.at[3 * i + j],
                    recv_sems.at[3 * i + j], (*chip, c))
            for i in range(len(ins)) for j, chip in enumerate(chips)]


def _swap_copies(ins, zones, send_sems, recv_sems):
    x, y, c, _ = _place()
    copies = []
    for i in range(len(ins)):
        for s in range(N_SHARD):
            copies.append(_remote(_half(ins[i], 1 - c, s), zones[i].at[s],
                                  send_sems.at[N_SHARD * i + s], recv_sems.at[N_SHARD * i + s], (x, y, 1 - c)))
    return copies


def _exchange_start(name, plan, sources, lands, per_array):
    n = len(sources)
    lands = [lax.empty(shape, dtype) for shape, dtype in lands]

    def body(*refs):
        for cp in plan(refs[:n], refs[n:2 * n], refs[2 * n], refs[2 * n + 1]):
            cp.start()
        refs[-1][...] = jnp.zeros_like(refs[-1])

    dma = pltpu.SemaphoreType.DMA
    outs = pl.pallas_call(
        body, name=name,
        in_specs=[HBM] * (2 * n),
        out_shape=(dma((per_array * n,)), dma((per_array * n,)),
                   *[pltpu.HBM(a.shape, a.dtype) for a in list(sources) + lands], jax.ShapeDtypeStruct((8, LANES), F32)),
        out_specs=(SEM, SEM, *[HBM] * (2 * n), pl.BlockSpec(memory_space=pltpu.VMEM)),
        input_output_aliases={k: 2 + k for k in range(2 * n)},
        compiler_params=pltpu.CompilerParams(has_side_effects=EFFECT),
    )(*[_in_hbm(a) for a in list(sources) + lands])
    return (outs[0], outs[1]), list(outs[2:2 + n]), list(outs[2 + n:2 + 2 * n]), outs[-1]


def _exchange_wait(name, plan, started, after):
    sems, partials, lands, _ = started
    n = len(partials)

    def body(*refs):
        for cp in plan(refs[:n], refs[n:2 * n], refs[2 * n], refs[2 * n + 1]):
            cp.wait_send()
            cp.wait_recv()

    outs = pl.pallas_call(
        body, name=name,
        in_specs=[HBM] * (2 * n) + [SEM, SEM] + [ANY] * len(after),
        out_shape=tuple(pltpu.HBM(a.shape, a.dtype) for a in lands),
        out_specs=tuple([HBM] * n),
        input_output_aliases={n + k: k for k in range(n)},
        compiler_params=pltpu.CompilerParams(has_side_effects=EFFECT),
    )(*partials, *lands, sems[0], sems[1], *after)
    return list(outs)


def _reduce_own(name, grad, other, received, where):
    s, r, cdim = grad.shape
    blk, nb, whole, within = _half_tiling(r, cdim)

    def body(where_ref, g_ref, o_ref, r_ref, out_ref):
        total = g_ref[...] + o_ref[...]
        for j in range(3):
            total = total + r_ref[j].astype(F32)
        out_ref[...] = total

    return pl.pallas_call(
        body, name=name,
        grid_spec=pltpu.PrefetchScalarGridSpec(
            num_scalar_prefetch=1, grid=(nb,),
            in_specs=[pl.BlockSpec((None,) + blk, lambda b, w_ref: (w_ref[0],) + whole(w_ref[1], b)),
                      pl.BlockSpec((None,) + blk, lambda b, w_ref: (w_ref[0],) + within(b)),
                      pl.BlockSpec((3,) + blk, lambda b, w_ref: (0,) + within(b))],
            out_specs=pl.BlockSpec(blk, lambda b, w_ref: whole(w_ref[1], b))),
        out_shape=jax.ShapeDtypeStruct((r, cdim), F32),
        compiler_params=_params("parallel"),
    )(where, grad, other, received)


def _join_start(name, halves):
    n = len(halves)

    def body(*refs):
        bufs, send_sems, recv_sems = refs[:n], refs[n], refs[n + 1]
        x, y, c, _ = _place()
        for i in range(n):
            mine = _half(bufs[i], c)
            _remote(mine, mine, send_sems.at[i], recv_sems.at[i], (x, y, 1 - c)).start()
        refs[-1][...] = jnp.zeros_like(refs[-1])

    dma = pltpu.SemaphoreType.DMA
    outs = pl.pallas_call(
        body, name=name,
        in_specs=[HBM] * n,
        out_shape=(dma((n,)), dma((n,)), *[pltpu.HBM(h.shape, F32) for h in halves], jax.ShapeDtypeStruct((8, LANES), F32)),
        out_specs=(SEM, SEM, *[HBM] * n, pl.BlockSpec(memory_space=pltpu.VMEM)),
        input_output_aliases={k: 2 + k for k in range(n)},
        compiler_params=pltpu.CompilerParams(has_side_effects=EFFECT),
    )(*[_in_hbm(h) for h in halves])
    return (outs[0], outs[1]), list(outs[2:2 + n]), outs[-1]


def _join_wait(name, started, after):
    sems, bufs, _ = started
    n = len(bufs)

    def body(*refs):
        bufs, send_sems, recv_sems = refs[:n], refs[n], refs[n + 1]
        x, y, c, _ = _place()
        for i in range(n):
            mine, theirs = _half(bufs[i], c), _half(bufs[i], 1 - c)
            _remote(mine, mine, send_sems.at[i], recv_sems.at[i], (x, y, 1 - c)).wait_send()
            _remote(theirs, theirs, send_sems.at[i], recv_sems.at[i], (x, y, 1 - c)).wait_recv()

    outs = pl.pallas_call(
        body, name=name,
        in_specs=[HBM] * n + [SEM, SEM] + [ANY] * len(after),
        out_shape=tuple(pltpu.HBM(b.shape, F32) for b in bufs),
        out_specs=tuple([HBM] * n),
        input_output_aliases={k: k for k in range(n)},
        compiler_params=pltpu.CompilerParams(has_side_effects=EFFECT),
    )(*bufs, sems[0], sems[1], *after)
    return list(outs)


BIG = ("w_in", "pool_w", "w_out", "xq_w", "xk_w", "xv_w", "xo_w", "w_up", "w_down")
GATHER_GROUPS = ((0, 1), (2, 3, 4, 5, 6), (7,), (8,))
RELAYED = (7, 8)
SMALL = ("conv_w", "a_log", "dt_bias", "gdn_norm_w", "pool_scale", "ln1_g", "ln1_b", "ln2_g", "ln2_b", "ln3_g", "ln3_b")
ORDER = ("w_in", "conv_w", "a_log", "dt_bias", "gdn_norm_w", "pool_w", "pool_scale", "w_out", "ln1_g", "ln1_b",
         "xq_w", "xk_w", "xv_w", "xo_w", "ln2_g", "ln2_b", "w_up", "w_down", "ln3_g", "ln3_b")
LANES = 128


def _rows(flat_len):
    return -(-flat_len // LANES)


def _pack(pieces):
    out = []
    for p in pieces:
        flat = p.reshape(-1).astype(F32)
        out.append(jnp.pad(flat, (0, _rows(flat.shape[0]) * LANES - flat.shape[0])).reshape(-1, LANES))
    slab = jnp.concatenate(out, axis=0)
    return jnp.pad(slab, ((0, -slab.shape[0] % 8), (0, 0)))


def _unpack(slab, shapes):
    out, row = [], 0
    for shp in shapes:
        size = math.prod(shp)
        out.append(slab[row:row + _rows(size)].reshape(-1)[:size].reshape(shp))
        row += _rows(size)
    return out


TRANSPOSED = ("w_in",)


def _as2d(name, a):
    a = a[0]
    if name in TRANSPOSED:
        return jnp.swapaxes(a, 0, 1)
    return a.reshape(-1, a.shape[-1]) if a.ndim == 3 else a


def _from2d(name, a, shape):
    return (jnp.swapaxes(a, 0, 1) if name in TRANSPOSED else a).reshape(shape)


def kernel(x, mem, w_in, conv_w, a_log, dt_bias, gdn_norm_w, pool_w, pool_scale, w_out, ln1_g, ln1_b, xq_w, xk_w, xv_w, xo_w, ln2_g, ln2_b, w_up, w_down, ln3_g, ln3_b, loss_target, m_w_in, m_conv_w, m_a_log, m_dt_bias, m_gdn_norm_w, m_pool_w, m_pool_scale, m_w_out, m_ln1_g, m_ln1_b, m_xq_w, m_xk_w, m_xv_w, m_xo_w, m_ln2_g, m_ln2_b, m_w_up, m_w_down, m_ln3_g, m_ln3_b, v_w_in, v_conv_w, v_a_log, v_dt_bias, v_gdn_norm_w, v_pool_w, v_pool_scale, v_w_out, v_ln1_g, v_ln1_b, v_xq_w, v_xk_w, v_xv_w, v_xo_w, v_ln2_g, v_ln2_b, v_w_up, v_w_down, v_ln3_g, v_ln3_b):
    given = dict(locals())
    cx, cy, cc = lax.axis_index("x"), lax.axis_index("y"), lax.axis_index("c")
    me = 2 * cx + cy
    groups = pool_w.shape[1]
    cs = pool_w.shape[2]
    kk, conv_cols = conv_w.shape[1], conv_w.shape[2]
    core = cc.astype(jnp.int32).reshape(1)
    where = jnp.stack([me, cc]).astype(jnp.int32)

    conv_slab = jnp.zeros((kk, N_SHARD * conv_cols), F32)
    conv_slab = lax.dynamic_update_slice(conv_slab, conv_w[0] * (cc == 0).astype(F32), (0, me * conv_cols))
    wts = {"conv_w": _unpack(_all_reduce_small("gather_conv_w", _pack([conv_slab])), [conv_slab.shape])[0]}

    started = {}

    def start(name, idx, after, token=None):
        casts = [_after(token, _as2d(BIG[i], given[BIG[i]])).astype(BF16) for i in idx]
        relayed = tuple(k for k, i in enumerate(idx) if i in RELAYED)
        sems, shards, lands, token = _gather_start(name, casts, after, relayed)
        for k, i in enumerate(idx):
            started[i] = (sems, k, shards[k], lands[k])
        return token

    token = start("gather_start_first", GATHER_GROUPS[0], wts["conv_w"])
    token = start("gather_start_rest", tuple(i for group in GATHER_GROUPS[1:] for i in group), token, token)

    relay = {}

    def send_on(after):
        members = [started[i] for i in RELAYED]
        relay["sems"], zones, token = _gather_relay("gather_relay", [m[1] for m in members], [m[2] for m in members],
                                                    [m[3] for m in members], members[0][0], after)
        relay["zones"] = dict(zip(RELAYED, zones))
        return token

    def fetch(group, after):
        members = [started[i] for i in GATHER_GROUPS[group]]
        sems, idx = members[0][0], [m[1] for m in members]
        shards = [m[2] for m in members]
        if GATHER_GROUPS[group][0] in RELAYED:
            ks = [RELAYED.index(i) for i in GATHER_GROUPS[group]]
            zones = [relay["zones"][i] for i in GATHER_GROUPS[group]]
            fwd, zones = _gather_forward_relayed(f"gather_forward_{group}", ks, zones, relay["sems"], after)
            got = _gather_wait_relayed(f"gather_wait_{group}", idx, ks, shards, zones, sems, relay["sems"], fwd, after)
        else:
            fwd, zones = _gather_forward(f"gather_forward_{group}", idx, [m[3] for m in members], sems, after)
            got = _gather_wait(f"gather_wait_{group}", idx, shards, zones, sems, fwd, after)
        full = dict(zip([BIG[i] for i in GATHER_GROUPS[group]], got))
        out = {}
        for n, a in full.items():
            if n == "w_in":
                out["w_in_t"] = a
            elif n == "w_up":
                out["w_up3"] = a
            elif n == "pool_w":
                out[n] = a.reshape(N_SHARD, groups, cs, -1).transpose(1, 0, 2, 3).reshape(groups, N_SHARD * cs, -1)
            else:
                out[n] = a.reshape(-1, a.shape[-1])
        return out

    for n in ("a_log", "dt_bias", "gdn_norm_w", "pool_scale", "ln1_g", "ln1_b", "ln2_g", "ln2_b", "ln3_g", "ln3_b"):
        wts[n] = given[n]
    wts.update(fetch(0, token))

    def start_swap(group, grads):
        names, blocks = [], []
        for n, g in grads.items():
            if n == "pool_w":
                g = g.reshape(groups, N_SHARD, cs, -1).transpose(1, 0, 2, 3).reshape(N_SHARD, groups * cs, -1)
            elif g.ndim == 2:
                g = g.reshape(N_SHARD, -1, g.shape[-1])
            names.append({"w_in_t": "w_in", "w_up3": "w_up"}.get(n, n))
            blocks.append(g)
        zones = [((N_SHARD,) + _half_shape(b.shape[1], b.shape[2]), F32) for b in blocks]
        swap = _exchange_start(f"grad_swap_start_{group}", _swap_copies, blocks, zones, N_SHARD)
        return {"group": group, "names": names, "swap": swap, "token": swap[3]}

    def start_send(state, after):
        group, names = state["group"], state["names"]
        state["blocks"] = state["swap"][1]
        state["others"] = _exchange_wait(f"grad_swap_wait_{group}", _swap_copies, state["swap"], after)
        partials = [_chip_partial("chip_partial_" + n, gb, ob, core)
                    for n, gb, ob in zip(names, state["blocks"], state["others"])]
        zones = [((3,) + p.shape[1:], BF16) for p in partials]
        state["send"] = _exchange_start(f"grad_send_start_{group}", _partial_copies, partials, zones, 3)
        state["token"] = state["send"][3]

    grad, delta, new_m, new_v = {}, {}, {}, {}

    def start_join(state, after):
        group, names = state["group"], state["names"]
        received = _exchange_wait(f"grad_send_wait_{group}", _partial_copies, state["send"], after)
        halves = [_reduce_own("reduce_own_" + n, gb, ob, rb, where)
                  for n, gb, ob, rb in zip(names, state["blocks"], state["others"], received)]
        state["join"] = _join_start(f"grad_join_start_{group}", halves)
        return state["join"][2]

    def finish_reduce(state, after):
        group, names = state["group"], state["names"]
        for n, g in zip(names, _join_wait(f"grad_join_wait_{group}", state["join"], after)):
            shp = given[n].shape
            d2, m2, v2 = _adamw("adamw_" + n, _as2d(n, given[n]), g, _as2d(n, given["m_" + n]), _as2d(n, given["v_" + n]))
            grad[n], delta[n], new_m[n], new_v[n] = (_from2d(n, a, shp) for a in (g, d2, m2, v2))
        return d2

    step = _local_step(x[0], mem[0], loss_target[0], wts, token)
    pending = {}
    request = next(step)
    while True:
        try:
            kind, group, payload = request
            if kind == "weights":
                request = step.send(fetch(group, payload))
            elif kind == "relay":
                request = step.send(send_on(payload))
            elif kind == "grads":
                pending[group] = start_swap(group, payload)
                request = step.send(pending[group]["token"])
            else:
                start_send(pending[group], [payload])
                request = step.send(pending[group]["token"])
        except StopIteration as stop:
            loss_row, grad_x, g = stop.value
            break

    after = [pending[2]["token"], grad_x]
    for group in (0, 1):
        after = [start_join(pending[group], after)]
    for group in (0, 1):
        after = [finish_reduce(pending[group], after)]
    after = [finish_reduce(pending[2], [start_join(pending[2], after)])]

    small_names = ("a_log", "dt_bias", "gdn_norm_w", "pool_scale", "ln1_g", "ln1_b", "ln2_g", "ln2_b", "ln3_g", "ln3_b")
    pieces = [g["conv_w"]] + [g[n] for n in small_names] + [loss_row[:, :1]]
    shapes = [p.shape for p in pieces]
    summed = _unpack(_all_reduce_small("all_reduce_small", _pack(pieces), after[0]), shapes)
    gsmall = dict(zip(small_names, summed[1:-1]))
    gsmall["conv_w"] = lax.dynamic_slice(summed[0], (0, me * conv_cols), (kk, conv_cols))
    loss = summed[-1][0, 0]

    sshapes = [given[n].shape for n in SMALL]
    slabs = [_pack([given[p + n] for n in SMALL]) for p in ("", "m_", "v_")]
    gslab = _pack([gsmall[n] for n in SMALL])
    outs = _adamw("adamw_small", slabs[0], gslab, slabs[1], slabs[2])
    for dst, slab in zip((delta, new_m, new_v), outs):
        dst.update(zip(SMALL, _unpack(slab, sshapes)))
    for n in SMALL:
        grad[n] = gsmall[n].reshape(given[n].shape)

    return (loss, grad_x[None], *[grad[n] for n in ORDER], *[delta[n] for n in ORDER],
            *[new_m[n] for n in ORDER], *[new_v[n] for n in ORDER])
```

```python
import math

import jax
import jax.numpy as jnp
from jax import lax
from jax.experimental import pallas as pl
from jax.experimental.pallas import tpu as pltpu

F32 = jnp.float32
BF16 = jnp.bfloat16
MESH = pl.DeviceIdType.MESH

HEAD_DIM = 128
CHUNK = 64
POOL_WINDOWS = (2, 4, 8, 16)
XATTN_HEADS = 4
ALPHA = 2.0 ** 0.25
LN_EPS = 1e-5
NORM_EPS = 1e-6
ADAM_LR, ADAM_B1, ADAM_B2, ADAM_EPS, ADAM_WD, ADAM_STEP = 0.001, 0.9, 0.999, 1e-08, 0.01, 10
N_SHARD = 4
VMEM_LIMIT = 56 * 1024 * 1024
K_STEPS = (2048, 1024, 512, 256, 128)


def _params(*sem):
    return pltpu.CompilerParams(dimension_semantics=sem, vmem_limit_bytes=VMEM_LIMIT)


def _bdot(a, b, ta=False, tb=False):
    dims = (((0 if ta else 1,), (1 if tb else 0,)), ((), ()))
    return lax.dot_general(a.astype(BF16), b.astype(BF16), dims, preferred_element_type=F32)


def _sigmoid(x):
    return 1.0 / (1.0 + jnp.exp(-x))


def _matmul(name, a, b, *, ta=False, tb=False, tm, tn, tk, extra=(), outs, epilogue, b_blocks=None,
            sequential=False, n_used=None, k_used=None, n_outer=False):
    m, k_dim = (a.shape[1], a.shape[0]) if ta else a.shape
    if b_blocks and tb:
        n = b.shape[1]
        k_dim = b.shape[0] * b.shape[2]
        per = b.shape[2] // tk
        b_spec = pl.BlockSpec((None, tn, tk), lambda i, j, k: (k // per, j, k % per))
    elif b_blocks:
        n = b.shape[0] * b.shape[2]
        per = b.shape[2] // tn
        b_spec = pl.BlockSpec((None, tk, tn), lambda i, j, k: (j // per, k, j % per))
    elif tb:
        n = b.shape[0]
        b_spec = pl.BlockSpec((tn, tk), lambda i, j, k: (j, k))
    else:
        n = b.shape[1]
        b_spec = pl.BlockSpec((tk, tn), lambda i, j, k: (k, j))
    n, k_dim = n_used or n, k_used or k_dim
    assert m % tm == 0 and n % tn == 0 and k_dim % tk == 0, (name, m, n, k_dim, tm, tn, tk)
    nk = k_dim // tk
    a_spec = pl.BlockSpec((tk, tm), lambda i, j, k: (k, i)) if ta else pl.BlockSpec((tm, tk), lambda i, j, k: (i, k))
    n_extra, n_out = len(extra), len(outs)

    def wrap(index_map):
        return lambda i, j, k: index_map(i, j)

    def spec(block, index_map):
        if n_outer:
            return pl.BlockSpec(block, lambda j, i, k: index_map(i, j, k))
        return pl.BlockSpec(block, index_map)

    row_axis = 1 if n_outer else 0

    def body_one_step(*refs):
        ex = refs[2:2 + n_extra]
        out = refs[2 + n_extra:2 + n_extra + n_out]
        epilogue(_bdot(refs[0][...], refs[1][...], ta, tb), ex, out, pl.program_id(row_axis))

    def body(*refs):
        a_ref, b_ref = refs[0], refs[1]
        ex = refs[2:2 + n_extra]
        out = refs[2 + n_extra:2 + n_extra + n_out]
        acc = refs[-1]
        i, k = pl.program_id(row_axis), pl.program_id(2)
        part = _bdot(a_ref[...], b_ref[...], ta, tb)

        @pl.when(k == 0)
        def _():
            acc[...] = part

        @pl.when(jnp.logical_and(k > 0, k < nk - 1))
        def _():
            acc[...] += part

        @pl.when(k == nk - 1)
        def _():
            epilogue(acc[...] + part, ex, out, i)

    sem = ("arbitrary",) * 3 if sequential else ("parallel", "parallel", "arbitrary")
    res = pl.pallas_call(
        body_one_step if nk == 1 else body, name=name,
        grid=(n // tn, m // tm, nk) if n_outer else (m // tm, n // tn, nk),
        in_specs=[spec(a_spec.block_shape, a_spec.index_map), spec(b_spec.block_shape, b_spec.index_map)]
        + [spec(bs, wrap(im)) for _, bs, im in extra],
        out_specs=[spec(bs, wrap(im)) for _, bs, im in outs],
        out_shape=[s for s, _, _ in outs],
        scratch_shapes=[] if nk == 1 else [pltpu.VMEM((tm, tn), F32)],
        compiler_params=_params(*sem),
    )(a, b, *[x for x, _, _ in extra])
    return res


def _tile(i, j):
    return (i, j)


def _plain(name, a, b, *, ta=False, tb=False, tm, tn, tk, out_dtype, b_blocks=None, out3=None, n_used=None,
           n_outer=False):
    m = a.shape[1] if ta else a.shape[0]
    if b_blocks:
        n = b.shape[1] if tb else b.shape[0] * b.shape[2]
    else:
        n = n_used or (b.shape[0] if tb else b.shape[1])

    def epi(acc, ex, out, i):
        out[0][...] = acc.astype(out_dtype)

    if out3:
        per = (n // out3) // tn
        spec = (jax.ShapeDtypeStruct((out3, m, n // out3), out_dtype), (None, tm, tn),
                lambda i, j: (j // per, i, j % per))
    else:
        spec = (jax.ShapeDtypeStruct((m, n), out_dtype), (tm, tn), _tile)
    return _matmul(name, a, b, ta=ta, tb=tb, tm=tm, tn=tn, tk=tk, outs=[spec], epilogue=epi,
                   b_blocks=b_blocks, n_used=n_used, n_outer=n_outer)[0]


def _ln_forward(name, a, b, res, gamma, beta, *, tm, tk, want_h=True):
    m, n = res.shape

    def epi(acc, ex, out, i):
        u = ALPHA * ex[0][...] + acc
        mu = jnp.mean(u, axis=-1, keepdims=True)
        xc = u - mu
        var = jnp.mean(xc * xc, axis=-1, keepdims=True)
        rstd = lax.rsqrt(var + LN_EPS)
        xhat = xc * rstd
        out[-2][...] = xhat
        out[-1][...] = rstd
        if want_h:
            h = xhat * ex[1][...] + ex[2][...]
            out[0][...] = h
            out[1][...] = h.astype(BF16)

    row = lambda i, j: (i, 0)
    vec = lambda i, j: (0, 0)
    outs = [(jax.ShapeDtypeStruct((m, n), F32), (tm, n), row), (jax.ShapeDtypeStruct((m, n), BF16), (tm, n), row),
            (jax.ShapeDtypeStruct((m, n), F32), (tm, n), row), (jax.ShapeDtypeStruct((m, 1), F32), (tm, 1), row)]
    return _matmul(
        name, a, b, tm=tm, tn=n, tk=tk,
        extra=[(res, (tm, n), row), (gamma, (1, n), vec), (beta, (1, n), vec)],
        outs=outs if want_h else outs[2:], epilogue=epi)


def _ln_backward_math(dy, xhat, rstd, gamma):
    dxhat = dy * gamma
    m1 = jnp.mean(dxhat, axis=-1, keepdims=True)
    m2 = jnp.mean(dxhat * xhat, axis=-1, keepdims=True)
    du = rstd * (dxhat - m1 - xhat * m2)
    return du, jnp.sum(dy * xhat, axis=0, keepdims=True), jnp.sum(dy, axis=0, keepdims=True)


def _ln_backward(name, a, b, dres, xhat, rstd, gamma, *, tm, tk, b_blocks=None, tb=True):
    m, n = dres.shape

    def epi(acc, ex, out, i):
        dy = acc + ALPHA * ex[0][...]
        du, dg, db = _ln_backward_math(dy, ex[1][...], ex[2][...], ex[3][...])
        out[0][...] = du
        out[1][...] = du.astype(BF16)
        first = i == 0

        @pl.when(first)
        def _():
            out[2][...] = dg
            out[3][...] = db

        @pl.when(jnp.logical_not(first))
        def _():
            out[2][...] += dg
            out[3][...] += db

    row = lambda i, j: (i, 0)
    vec = lambda i, j: (0, 0)
    return _matmul(
        name, a, b, tb=tb, tm=tm, tn=n, tk=tk, b_blocks=b_blocks, sequential=True,
        extra=[(dres, (tm, n), row), (xhat, (tm, n), row), (rstd, (tm, 1), row), (gamma, (1, n), vec)],
        outs=[(jax.ShapeDtypeStruct((m, n), F32), (tm, n), row),
              (jax.ShapeDtypeStruct((m, n), BF16), (tm, n), row),
              (jax.ShapeDtypeStruct((1, n), F32), (1, n), vec),
              (jax.ShapeDtypeStruct((1, n), F32), (1, n), vec)],
        epilogue=epi)


def _shift_down(x, k):
    row = lax.broadcasted_iota(jnp.int32, x.shape, 0)
    return jnp.where(row >= k, pltpu.roll(x, k, axis=0), 0.0)


def _shift_up(x, k):
    t = x.shape[0]
    row = lax.broadcasted_iota(jnp.int32, x.shape, 0)
    return jnp.where(row < t - k, pltpu.roll(x, t - k, axis=0), 0.0)


def _conv_silu_norm(x, w, normalise):
    kk = w.shape[0]
    c = x * w[kk - 1:kk, :]
    for j in range(kk - 1):
        c = c + _shift_down(x, kk - 1 - j) * w[j:j + 1, :]
    sg = _sigmoid(c)
    s = c * sg
    r = lax.rsqrt(jnp.sum(s * s, axis=-1, keepdims=True) + NORM_EPS)
    y = jnp.where(normalise, s * r, s)
    return c, sg, s, r, y


def _gdn_pre(proj, conv_w, heads):
    t = proj.shape[0]
    kk = conv_w.shape[0]

    def body(x_ref, w_ref, o_ref):
        normalise = pl.program_id(0) < 2
        o_ref[...] = _conv_silu_norm(x_ref[...], w_ref[...], normalise)[4]

    col = lambda s, h: (0, s * heads + h)
    return pl.pallas_call(
        body, name="gdn_pre", grid=(3, heads),
        in_specs=[pl.BlockSpec((t, HEAD_DIM), col), pl.BlockSpec((kk, HEAD_DIM), col)],
        out_specs=pl.BlockSpec((t, HEAD_DIM), col),
        out_shape=jax.ShapeDtypeStruct((t, 3 * heads * HEAD_DIM), F32),
        compiler_params=_params("parallel", "parallel"),
    )(proj, conv_w)


def _gdn_pre_backward(proj, conv_w, dqkv, heads):
    t = proj.shape[0]
    kk = conv_w.shape[0]

    def body(x_ref, w_ref, dy_ref, dx_ref, dw_ref):
        normalise = pl.program_id(0) < 2
        x = x_ref[...]
        w = w_ref[...]
        dy = dy_ref[...]
        c, sg, s, r, y = _conv_silu_norm(x, w, normalise)
        ds_norm = r * (dy - y * jnp.sum(dy * y, axis=-1, keepdims=True))
        ds = jnp.where(normalise, ds_norm, dy)
        dc = ds * (sg * (1.0 + c * (1.0 - sg)))
        dx = dc * w[kk - 1:kk, :]
        rows = [None] * kk
        rows[kk - 1] = jnp.sum(dc * x, axis=0, keepdims=True)
        for j in range(kk - 1):
            lag = kk - 1 - j
            dx = dx + _shift_up(dc, lag) * w[j:j + 1, :]
            rows[j] = jnp.sum(dc * _shift_down(x, lag), axis=0, keepdims=True)
        dx_ref[...] = dx.astype(BF16)
        dw_ref[...] = jnp.concatenate(rows, axis=0)

    col = lambda s, h: (0, s * heads + h)
    return pl.pallas_call(
        body, name="gdn_pre_bwd", grid=(3, heads),
        in_specs=[pl.BlockSpec((t, HEAD_DIM), col), pl.BlockSpec((kk, HEAD_DIM), col),
                  pl.BlockSpec((t, HEAD_DIM), col)],
        out_specs=[pl.BlockSpec((t, HEAD_DIM), col), pl.BlockSpec((kk, HEAD_DIM), col)],
        out_shape=[jax.ShapeDtypeStruct((t, 3 * heads * HEAD_DIM), BF16),
                   jax.ShapeDtypeStruct((kk, 3 * heads * HEAD_DIM), F32)],
        compiler_params=_params("parallel", "parallel"),
    )(proj, conv_w, dqkv)


def _gate_vectors(a_log, dt_bias, heads):
    pad = lambda v: jnp.pad(v.astype(F32), ((0, 0), (heads, HEAD_DIM - 2 * heads)))
    return pad(jnp.exp(a_log.astype(F32))), pad(dt_bias)


def _softplus(x):
    return jnp.maximum(x, 0.0) + jnp.log(1.0 + jnp.exp(-jnp.abs(x)))


def _gates_epilogue(heads):
    def epi(acc, ex, out, i):
        lane = lax.broadcasted_iota(jnp.int32, acc.shape, 1)
        beta = _sigmoid(acc)
        g = -ex[0][...] * _softplus(acc + ex[1][...])
        out[0][...] = acc
        out[1][...] = jnp.where(lane < heads, beta, jnp.where(lane < 2 * heads, g, 0.0))
    return epi


def _gates_backward(ba, bg, dbg, ea, dtb, heads):
    t = ba.shape[0]

    def body(ba_ref, bg_ref, d_ref, ea_ref, dt_ref, dba_ref, dal_ref, ddt_ref):
        lane = lax.broadcasted_iota(jnp.int32, (t, HEAD_DIM), 1)
        bgv = bg_ref[...]
        d = d_ref[...]
        db = d * bgv * (1.0 - bgv)
        da = -d * ea_ref[...] * _sigmoid(ba_ref[...] + dt_ref[...])
        is_g = jnp.logical_and(lane >= heads, lane < 2 * heads)
        dba = jnp.where(lane < heads, db, jnp.where(is_g, da, 0.0))
        dba_ref[...] = dba.astype(BF16)
        dal_ref[...] = jnp.sum(jnp.where(is_g, d * bgv, 0.0), axis=0, keepdims=True)
        ddt_ref[...] = jnp.sum(jnp.where(is_g, da, 0.0), axis=0, keepdims=True)

    full = pl.BlockSpec((t, HEAD_DIM), lambda: (0, 0))
    vec = pl.BlockSpec((1, HEAD_DIM), lambda: (0, 0))
    return pl.pallas_call(
        body, name="gates_bwd", grid=(),
        in_specs=[full, full, full, vec, vec], out_specs=[full, vec, vec],
        out_shape=[jax.ShapeDtypeStruct((t, HEAD_DIM), BF16), jax.ShapeDtypeStruct((1, HEAD_DIM), F32),
                   jax.ShapeDtypeStruct((1, HEAD_DIM), F32)],
        compiler_params=pltpu.CompilerParams(vmem_limit_bytes=VMEM_LIMIT),
    )(ba, bg, dbg, ea, dtb)


class _Chunk:
    pass


def _split2(x):
    hi = x.astype(BF16)
    return hi, (x - hi.astype(F32)).astype(BF16)


def _split3(x):
    hi = x.astype(BF16)
    rest = x - hi.astype(F32)
    mid = rest.astype(BF16)
    return hi, mid, (rest - mid.astype(F32)).astype(BF16)


def _dot_mask(mask, x, ta=False):
    hi, mid, lo = _split3(x)
    return _bdot(mask, hi, ta=ta) + (_bdot(mask, mid, ta=ta) + _bdot(mask, lo, ta=ta))


def _transpose_by_identity(x):
    r = x.shape[0]
    eye = (lax.broadcasted_iota(jnp.int32, (r, r), 0) == lax.broadcasted_iota(jnp.int32, (r, r), 1)).astype(BF16)
    hi, mid, lo = _split3(x)
    return _bdot(hi, eye, ta=True) + (_bdot(mid, eye, ta=True) + _bdot(lo, eye, ta=True))


def _dot22(a, b, ta=False, tb=False):
    ah, al = _split2(a)
    bh, bl = _split2(b)
    return _bdot(ah, bh, ta, tb) + (_bdot(ah, bl, ta, tb) + _bdot(al, bh, ta, tb))


def _chunk_gates(bg, heads):
    n = CHUNK
    row = lax.broadcasted_iota(jnp.int32, (n, n), 0)
    col = lax.broadcasted_iota(jnp.int32, (n, n), 1)
    lane = lax.broadcasted_iota(jnp.int32, bg.shape, 1)
    graw = jnp.where(jnp.logical_and(lane >= heads, lane < 2 * heads), bg, 0.0)
    gc = _dot_mask((row >= col).astype(BF16), graw)
    return gc, _transpose_by_identity(gc)


def _in_lockstep(generators):
    results = [None] * len(generators)
    live = list(enumerate(generators))
    while live:
        still = []
        for i, gen in live:
            try:
                next(gen)
                still.append((i, gen))
            except StopIteration as stop:
                results[i] = stop.value
        live = still
    return results


def _chunk_local(q, k, v, beta, gc, grow, solved=None):
    c = _Chunk()
    n = CHUNK
    row = lax.broadcasted_iota(jnp.int32, (n, n), 0)
    col = lax.broadcasted_iota(jnp.int32, (n, n), 1)
    c.tri = row >= col
    c.strict = row > col
    eye = row == col
    c.gcb = jnp.broadcast_to(gc, (n, HEAD_DIM))
    c.decay = jnp.where(c.tri, jnp.exp(jnp.where(c.tri, gc - grow, 0.0)), 0.0)
    c.eg = jnp.exp(c.gcb)
    glast = c.gcb[n - 1:n, :]
    c.egl = jnp.exp(glast)
    c.ekl = jnp.exp(glast - c.gcb)
    c.beta = beta
    c.q = q * (HEAD_DIM ** -0.5)
    c.k = k
    c.v = v
    c.kb = k * beta
    c.vb = v * beta
    c.kg = c.kb * c.eg
    both = _bdot(jnp.concatenate([c.kb, c.q], axis=0), k, tb=True)
    yield
    c.L = jnp.where(c.strict, both[:n] * c.decay, 0.0)
    c.A = jnp.where(c.tri, both[n:] * c.decay, 0.0)
    if solved is None:
        x = -c.L
        tinv = eye.astype(F32) + x
        p = _dot22(x, x)
        yield
        for _ in range(int(math.log2(n)) - 2):
            both = _dot22(jnp.concatenate([p, tinv], axis=0), p)
            yield
            p, tinv = both[:n], tinv + both[n:]
        c.T = tinv + _dot22(tinv, p)
        yield
        uw = _dot22(c.T, jnp.concatenate([c.vb, c.kg], axis=1))
        yield
        c.u, c.w = uw[:, :HEAD_DIM], uw[:, HEAD_DIM:]
    else:
        c.T, c.u, c.w = solved
    c.qg = c.q * c.eg
    c.kdec = k * c.ekl
    return c


def _gdn_core(qkv, bg, heads):
    t = qkv.shape[0]
    nchunk = t // CHUNK

    gw = heads * HEAD_DIM

    def body(qkv_ref, bg_ref, o_ref, s_ref, t_ref, u_ref, w_ref, state):
        @pl.when(pl.program_id(0) == 0)
        def _():
            state[...] = jnp.zeros_like(state)

        bg_v = bg_ref[...]
        gc_all, gc_rows = _chunk_gates(bg_v, heads)
        def one_head(h):
            col = lambda s: pl.ds(s * gw + h * HEAD_DIM, HEAD_DIM)
            c = yield from _chunk_local(qkv_ref[:, col(0)], qkv_ref[:, col(1)], qkv_ref[:, col(2)], bg_v[:, h:h + 1],
                                        gc_all[:, heads + h:heads + h + 1], gc_rows[heads + h:heads + h + 1, :])
            s0 = state[h]
            v_new = c.u - _bdot(c.w, s0)
            yield
            o = _bdot(c.qg, s0) + _bdot(c.A, v_new)
            return s0, o, s0 * c.egl + _bdot(c.kdec, v_new, ta=True), c

        results = _in_lockstep([one_head(h) for h in range(heads)])
        for h, (s0, o, s1, c) in enumerate(results):
            lanes = pl.ds(h * HEAD_DIM, HEAD_DIM)
            s_ref[h, 0] = s0
            o_ref[:, lanes] = o
            t_ref[:, lanes] = jnp.concatenate([c.T, jnp.zeros((CHUNK, HEAD_DIM - CHUNK), F32)], axis=1)
            u_ref[:, lanes] = c.u
            w_ref[:, lanes] = c.w
            state[h] = s1

    return pl.pallas_call(
        body, name="gdn_core", grid=(nchunk,),
        in_specs=[pl.BlockSpec((CHUNK, 3 * gw), lambda n: (n, 0)), pl.BlockSpec((CHUNK, HEAD_DIM), lambda n: (n, 0))],
        out_specs=[pl.BlockSpec((CHUNK, gw), lambda n: (n, 0)),
                   pl.BlockSpec((heads, 1, HEAD_DIM, HEAD_DIM), lambda n: (0, n, 0, 0))]
        + [pl.BlockSpec((CHUNK, gw), lambda n: (n, 0))] * 3,
        out_shape=[jax.ShapeDtypeStruct((t, gw), F32),
                   jax.ShapeDtypeStruct((heads, nchunk, HEAD_DIM, HEAD_DIM), F32)]
        + [jax.ShapeDtypeStruct((t, gw), F32)] * 3,
        scratch_shapes=[pltpu.VMEM((heads, HEAD_DIM, HEAD_DIM), F32)],
        compiler_params=_params("arbitrary"),
    )(qkv, bg)


def _gdn_core_backward(qkv, bg, states, solved, do, heads):
    t = qkv.shape[0]
    nchunk = t // CHUNK
    n = CHUNK

    def one_head(chunk_local, s0, d_out, ds1):
        c = yield from chunk_local
        v_new = c.u - _bdot(c.w, s0)
        dqg = _bdot(d_out, s0, tb=True)
        ds0 = _bdot(c.qg, d_out, ta=True) + ds1 * c.egl
        dv_new = _bdot(c.A, d_out, ta=True) + _bdot(c.kdec, ds1)
        yield
        dA = jnp.where(c.tri, _bdot(d_out, v_new, tb=True), 0.0)
        dkdec = _bdot(v_new, ds1, tb=True)
        dgl = jnp.sum(jnp.sum(ds1 * s0, axis=1, keepdims=True), axis=0, keepdims=True) * c.egl
        dw = -_bdot(dv_new, s0, tb=True)
        ds0 = ds0 - _bdot(c.w, dv_new, ta=True)
        yield
        both = _dot22(c.T, jnp.concatenate([dv_new, dw], axis=1), ta=True)
        yield
        dvb, dkg = both[:, :HEAD_DIM], both[:, HEAD_DIM:]
        dL = jnp.where(c.strict, -(_bdot(dvb, c.u, tb=True) + _bdot(dkg, c.w, tb=True)), 0.0)
        yield
        dm1 = dL * c.decay
        dkb = _bdot(dm1, c.k) + dkg * c.eg
        dk = _bdot(dm1, c.kb, ta=True)
        dm2 = dA * c.decay
        dq = _bdot(dm2, c.k) + dqg * c.eg
        dk = dk + _bdot(dm2, c.q, ta=True) + dkdec * c.ekl + dkb * c.beta
        pm = dL * c.L + dA * c.A
        ones = jnp.ones((n, HEAD_DIM), BF16)
        pm_hi, pm_lo = _split2(pm)
        colsum = _bdot(pm_hi, ones, ta=True) + _bdot(pm_lo, ones, ta=True)
        tk_ = jnp.sum(dkdec * c.kdec, axis=1, keepdims=True)
        dgc = (jnp.sum(pm, axis=1, keepdims=True) - colsum
               + jnp.sum(dqg * c.qg, axis=1, keepdims=True)
               - tk_
               + jnp.sum(dkg * c.kg, axis=1, keepdims=True))
        dgl = dgl + jnp.sum(tk_, axis=0, keepdims=True)
        rowi = lax.broadcasted_iota(jnp.int32, (n, HEAD_DIM), 0)
        dgc = dgc + jnp.where(rowi == n - 1, dgl, 0.0)
        dbeta = jnp.sum(dkb * c.k, axis=1, keepdims=True) + jnp.sum(dvb * c.v, axis=1, keepdims=True)
        return dq * (HEAD_DIM ** -0.5), dk, dvb * c.beta, dbeta, dgc, ds0

    gw = heads * HEAD_DIM

    def body(qkv_ref, bg_ref, s_ref, t_ref, u_ref, w_ref, do_ref, dqkv_ref, dbg_ref, dstate):
        @pl.when(pl.program_id(0) == 0)
        def _():
            dstate[...] = jnp.zeros_like(dstate)

        bg_v = bg_ref[...]
        gc_all, gc_rows = _chunk_gates(bg_v, heads)
        lane = lax.broadcasted_iota(jnp.int32, (n, HEAD_DIM), 1)
        dgates = jnp.zeros((n, HEAD_DIM), F32)
        chains = []
        for h in range(heads):
            col = lambda s: pl.ds(s * gw + h * HEAD_DIM, HEAD_DIM)
            lanes = pl.ds(h * HEAD_DIM, HEAD_DIM)
            c = _chunk_local(qkv_ref[:, col(0)], qkv_ref[:, col(1)], qkv_ref[:, col(2)], bg_v[:, h:h + 1],
                             gc_all[:, heads + h:heads + h + 1], gc_rows[heads + h:heads + h + 1, :],
                             (t_ref[:, pl.ds(h * HEAD_DIM, CHUNK)], u_ref[:, lanes], w_ref[:, lanes]))
            chains.append(one_head(c, s_ref[h, 0], do_ref[:, pl.ds(h * HEAD_DIM, HEAD_DIM)], dstate[h]))
        results = _in_lockstep(chains)
        for h, (dq, dk, dv, dbeta, dgc, ds0) in enumerate(results):
            dgates = jnp.where(lane == h, dbeta, jnp.where(lane == heads + h, dgc, dgates))
        for h, (dq, dk, dv, dbeta, dgc, ds0) in enumerate(results):
            dqkv_ref[:, pl.ds(h * HEAD_DIM, HEAD_DIM)] = dq
            dqkv_ref[:, pl.ds(gw + h * HEAD_DIM, HEAD_DIM)] = dk
            dqkv_ref[:, pl.ds(2 * gw + h * HEAD_DIM, HEAD_DIM)] = dv
            dstate[h] = ds0
        row = lax.broadcasted_iota(jnp.int32, (n, n), 0)
        colm = lax.broadcasted_iota(jnp.int32, (n, n), 1)
        draw = _dot_mask((row >= colm).astype(BF16), dgates, ta=True)
        dbg_ref[...] = jnp.where(lane < heads, dgates, draw)

    last = nchunk - 1
    return pl.pallas_call(
        body, name="gdn_core_bwd", grid=(nchunk,),
        in_specs=[pl.BlockSpec((CHUNK, 3 * gw), lambda i: (last - i, 0)),
                  pl.BlockSpec((CHUNK, HEAD_DIM), lambda i: (last - i, 0)),
                  pl.BlockSpec((heads, 1, HEAD_DIM, HEAD_DIM), lambda i: (0, last - i, 0, 0))]
        + [pl.BlockSpec((CHUNK, gw), lambda i: (last - i, 0))] * 4,
        out_specs=[pl.BlockSpec((CHUNK, 3 * gw), lambda i: (last - i, 0)),
                   pl.BlockSpec((CHUNK, HEAD_DIM), lambda i: (last - i, 0))],
        out_shape=[jax.ShapeDtypeStruct((t, 3 * gw), F32), jax.ShapeDtypeStruct((t, HEAD_DIM), F32)],
        scratch_shapes=[pltpu.VMEM((heads, HEAD_DIM, HEAD_DIM), F32)],
        compiler_params=_params("arbitrary"),
    )(qkv, bg, states, *solved, do)


def _gdn_post(o, proj, z_col0, norm_w, heads, tt):
    t = o.shape[0]
    zb = z_col0 // HEAD_DIM

    def body(o_ref, z_ref, w_ref, out_ref):
        ov = o_ref[...]
        z = z_ref[...]
        rms = lax.rsqrt(jnp.mean(ov * ov, axis=-1, keepdims=True) + NORM_EPS)
        out_ref[...] = (ov * rms * w_ref[...] * (z * _sigmoid(z))).astype(BF16)

    return pl.pallas_call(
        body, name="gdn_post", grid=(t // tt, heads),
        in_specs=[pl.BlockSpec((tt, HEAD_DIM), lambda i, h: (i, h)),
                  pl.BlockSpec((tt, HEAD_DIM), lambda i, h: (i, zb + h)),
                  pl.BlockSpec((1, HEAD_DIM), lambda i, h: (0, 0))],
        out_specs=pl.BlockSpec((tt, HEAD_DIM), lambda i, h: (i, h)),
        out_shape=jax.ShapeDtypeStruct((t, heads * HEAD_DIM), BF16),
        compiler_params=_params("parallel", "parallel"),
    )(o, proj, norm_w)


def _gdn_post_backward(dcat, o, proj, z_col0, norm_w, heads, tt):
    t = o.shape[0]
    zb = z_col0 // HEAD_DIM

    def body(d_ref, o_ref, z_ref, w_ref, do_ref, dz_ref, dw_ref):
        d = d_ref[...]
        ov = o_ref[...]
        z = z_ref[...]
        w = w_ref[...]
        rms = lax.rsqrt(jnp.mean(ov * ov, axis=-1, keepdims=True) + NORM_EPS)
        ohat = ov * rms
        sg = _sigmoid(z)
        gate = z * sg
        dz_ref[...] = (d * ohat * w * (sg * (1.0 + z * (1.0 - sg)))).astype(BF16)
        don = d * gate
        dohat = don * w
        do_ref[...] = rms * (dohat - ohat * jnp.mean(dohat * ohat, axis=-1, keepdims=True))
        dw = jnp.sum(don * ohat, axis=0, keepdims=True)
        first = jnp.logical_and(pl.program_id(0) == 0, pl.program_id(1) == 0)

        @pl.when(first)
        def _():
            dw_ref[...] = dw

        @pl.when(jnp.logical_not(first))
        def _():
            dw_ref[...] += dw

    blk = pl.BlockSpec((tt, HEAD_DIM), lambda i, h: (i, h))
    return pl.pallas_call(
        body, name="gdn_post_bwd", grid=(t // tt, heads),
        in_specs=[blk, blk, pl.BlockSpec((tt, HEAD_DIM), lambda i, h: (i, zb + h)),
                  pl.BlockSpec((1, HEAD_DIM), lambda i, h: (0, 0))],
        out_specs=[blk, blk, pl.BlockSpec((1, HEAD_DIM), lambda i, h: (0, 0))],
        out_shape=[jax.ShapeDtypeStruct((t, heads * HEAD_DIM), F32),
                   jax.ShapeDtypeStruct((t, heads * HEAD_DIM), BF16),
                   jax.ShapeDtypeStruct((1, HEAD_DIM), F32)],
        compiler_params=_params("arbitrary", "arbitrary"),
    )(dcat, o, proj, norm_w)


def _pool_select(levels, group):
    out = levels[-1]
    for gi in range(len(levels) - 2, -1, -1):
        out = jnp.where(group == gi, levels[gi], out)
    return out


def _pool_counts(t, width, group):
    pos = lax.broadcasted_iota(jnp.int32, (t, width), 0)
    win = jnp.left_shift(2, group)
    return jnp.minimum(pos + 1, win).astype(F32)


def _pooled(p, group):
    levels, s, step = [], p, 1
    for _ in POOL_WINDOWS:
        s = s + _shift_down(s, step)
        levels.append(s)
        step *= 2
    cnt = _pool_counts(p.shape[0], p.shape[1], group)
    return _pool_select(levels, group) / cnt - p, cnt


def _pool_forward(proj, p_col0, pool_w, pool_scale):
    t = proj.shape[0]
    groups, cg, _ = pool_w.shape
    pb = p_col0 // cg

    def body(p_ref, w_ref, s_ref, o_ref):
        pooled, _ = _pooled(p_ref[...], pl.program_id(0))
        o_ref[...] = (_bdot(pooled, w_ref[0]) * s_ref[...]).astype(BF16)

    return pl.pallas_call(
        body, name="pool_fwd", grid=(groups,),
        in_specs=[pl.BlockSpec((t, cg), lambda g: (0, pb + g)), pl.BlockSpec((1, cg, cg), lambda g: (g, 0, 0)),
                  pl.BlockSpec((1, cg), lambda g: (0, g))],
        out_specs=pl.BlockSpec((t, cg), lambda g: (0, g)),
        out_shape=jax.ShapeDtypeStruct((t, groups * cg), BF16),
        compiler_params=_params("parallel"),
    )(proj, pool_w, pool_scale)


def _pool_backward(dcat, d_col0, proj, p_col0, pool_w, pool_scale):
    t = proj.shape[0]
    groups, cg, _ = pool_w.shape
    pb = p_col0 // cg
    db = d_col0 // cg

    def body(d_ref, p_ref, w_ref, s_ref, dp_ref, dw_ref, ds_ref):
        group = pl.program_id(0)
        pooled, cnt = _pooled(p_ref[...], group)
        w = w_ref[0]
        d = d_ref[...]
        mixed = _bdot(pooled, w)
        ds_ref[...] = jnp.sum(d * mixed, axis=0, keepdims=True)
        dmixed = d * s_ref[...]
        dw_ref[0] = _bdot(pooled, dmixed, ta=True)
        dpooled = _bdot(dmixed, w, tb=True)
        levels, s, step = [], dpooled / cnt, 1
        for _ in POOL_WINDOWS:
            s = s + _shift_up(s, step)
            levels.append(s)
            step *= 2
        dp_ref[...] = (_pool_select(levels, group) - dpooled).astype(BF16)

    return pl.pallas_call(
        body, name="pool_bwd", grid=(groups,),
        in_specs=[pl.BlockSpec((t, cg), lambda g: (0, db + g)), pl.BlockSpec((t, cg), lambda g: (0, pb + g)),
                  pl.BlockSpec((1, cg, cg), lambda g: (g, 0, 0)), pl.BlockSpec((1, cg), lambda g: (0, g))],
        out_specs=[pl.BlockSpec((t, cg), lambda g: (0, g)), pl.BlockSpec((1, cg, cg), lambda g: (g, 0, 0)),
                   pl.BlockSpec((1, cg), lambda g: (0, g))],
        out_shape=[jax.ShapeDtypeStruct((t, groups * cg), BF16), jax.ShapeDtypeStruct((groups, cg, cg), F32),
                   jax.ShapeDtypeStruct((1, groups * cg), F32)],
        compiler_params=_params("parallel"),
    )(dcat, proj, pool_w, pool_scale)


def _attention(q, k, v, tq):
    t, d = q.shape
    m = k.shape[0]
    dh = d // XATTN_HEADS
    scale = dh ** -0.5

    def body(q_ref, k_ref, v_ref, o_ref):
        s = _bdot(q_ref[...], k_ref[...], tb=True) * scale
        s = s - jnp.max(s, axis=-1, keepdims=True)
        e = jnp.exp(s)
        p = e / jnp.sum(e, axis=-1, keepdims=True)
        o_ref[...] = _bdot(p, v_ref[...]).astype(BF16)

    return pl.pallas_call(
        body, name="xattn_fwd", grid=(XATTN_HEADS, t // tq),
        in_specs=[pl.BlockSpec((tq, dh), lambda h, i: (i, h)), pl.BlockSpec((m, dh), lambda h, i: (0, h)),
                  pl.BlockSpec((m, dh), lambda h, i: (0, h))],
        out_specs=pl.BlockSpec((tq, dh), lambda h, i: (i, h)),
        out_shape=jax.ShapeDtypeStruct((t, d), BF16),
        compiler_params=_params("parallel", "parallel"),
    )(q, k, v)


def _attention_backward(q, k, v, do, tq):
    t, d = q.shape
    m = k.shape[0]
    dh = d // XATTN_HEADS
    scale = dh ** -0.5

    def body(q_ref, k_ref, v_ref, do_ref, dq_ref, dk_ref, dv_ref, dk_acc, dv_acc):
        i = pl.program_id(1)
        qv, kv, vv, dov = q_ref[...], k_ref[...], v_ref[...], do_ref[...]
        s = _bdot(qv, kv, tb=True) * scale
        s = s - jnp.max(s, axis=-1, keepdims=True)
        e = jnp.exp(s)
        p = e / jnp.sum(e, axis=-1, keepdims=True)
        dp = _bdot(dov, vv, tb=True)
        ds = p * (dp - jnp.sum(dp * p, axis=-1, keepdims=True)) * scale
        dq_ref[...] = _bdot(ds, kv).astype(BF16)
        dv_part = _bdot(p, dov, ta=True)
        dk_part = _bdot(ds, qv, ta=True)

        @pl.when(i == 0)
        def _():
            dk_acc[...] = dk_part
            dv_acc[...] = dv_part

        @pl.when(i > 0)
        def _():
            dk_acc[...] += dk_part
            dv_acc[...] += dv_part

        @pl.when(i == pl.num_programs(1) - 1)
        def _():
            dk_ref[...] = dk_acc[...].astype(BF16)
            dv_ref[...] = dv_acc[...].astype(BF16)

    qblk = pl.BlockSpec((tq, dh), lambda h, i: (i, h))
    kblk = pl.BlockSpec((m, dh), lambda h, i: (0, h))
    return pl.pallas_call(
        body, name="xattn_bwd", grid=(XATTN_HEADS, t // tq),
        in_specs=[qblk, kblk, kblk, qblk],
        out_specs=[qblk, kblk, kblk],
        out_shape=[jax.ShapeDtypeStruct((t, d), BF16), jax.ShapeDtypeStruct((m, d), BF16),
                   jax.ShapeDtypeStruct((m, d), BF16)],
        scratch_shapes=[pltpu.VMEM((m, dh), F32), pltpu.VMEM((m, dh), F32)],
        compiler_params=_params("parallel", "arbitrary"),
    )(q, k, v, do)


def _ln_backward_rows(name, dmain, dres, xhat, rstd, gamma, tm):
    t, d = xhat.shape

    def body(m_ref, r_ref, x_ref, s_ref, g_ref, du_ref, dub_ref, dg_ref, db_ref):
        du, dg, db = _ln_backward_math(m_ref[...] + ALPHA * r_ref[...], x_ref[...], s_ref[...], g_ref[...])
        du_ref[...] = du
        dub_ref[...] = du.astype(BF16)
        first = pl.program_id(0) == 0

        @pl.when(first)
        def _():
            dg_ref[...] = dg
            db_ref[...] = db

        @pl.when(jnp.logical_not(first))
        def _():
            dg_ref[...] += dg
            db_ref[...] += db

    row = pl.BlockSpec((tm, d), lambda i: (i, 0))
    vec = pl.BlockSpec((1, d), lambda i: (0, 0))
    return pl.pallas_call(
        body, name=name, grid=(t // tm,),
        in_specs=[row, row, row, pl.BlockSpec((tm, 1), lambda i: (i, 0)), vec],
        out_specs=[row, row, vec, vec],
        out_shape=[jax.ShapeDtypeStruct((t, d), F32), jax.ShapeDtypeStruct((t, d), BF16),
                   jax.ShapeDtypeStruct((1, d), F32), jax.ShapeDtypeStruct((1, d), F32)],
        compiler_params=_params("arbitrary"),
    )(dmain, dres, xhat, rstd, gamma)


def _loss_and_ln_backward(xhat, rstd, gamma, beta, target, tm):
    t, d = xhat.shape

    def body(x_ref, r_ref, g_ref, b_ref, t_ref, du_ref, dub_ref, dg_ref, db_ref, loss_ref):
        xh = x_ref[...]
        g = g_ref[...]
        diff = xh * g + b_ref[...] - t_ref[...]
        part = jnp.sum(jnp.sum(diff * diff, axis=1, keepdims=True), axis=0, keepdims=True) * (0.5 / d)
        dy = diff * (1.0 / d)
        du, dg, db = _ln_backward_math(dy, xh, r_ref[...], g)
        du_ref[...] = du
        dub_ref[...] = du.astype(BF16)
        lossrow = jnp.broadcast_to(part, (1, HEAD_DIM))
        first = pl.program_id(0) == 0

        @pl.when(first)
        def _():
            dg_ref[...] = dg
            db_ref[...] = db
            loss_ref[...] = lossrow

        @pl.when(jnp.logical_not(first))
        def _():
            dg_ref[...] += dg
            db_ref[...] += db
            loss_ref[...] += lossrow

    row = pl.BlockSpec((tm, d), lambda i: (i, 0))
    vec = pl.BlockSpec((1, d), lambda i: (0, 0))
    return pl.pallas_call(
        body, name="loss_ln3_bwd", grid=(t // tm,),
        in_specs=[row, pl.BlockSpec((tm, 1), lambda i: (i, 0)), vec, vec, row],
        out_specs=[row, row, vec, vec, pl.BlockSpec((1, HEAD_DIM), lambda i: (0, 0))],
        out_shape=[jax.ShapeDtypeStruct((t, d), F32), jax.ShapeDtypeStruct((t, d), BF16),
                   jax.ShapeDtypeStruct((1, d), F32), jax.ShapeDtypeStruct((1, d), F32),
                   jax.ShapeDtypeStruct((1, HEAD_DIM), F32)],
        compiler_params=_params("arbitrary"),
    )(xhat, rstd, gamma, beta, target)


def _after(token, a):
    return a if token is None else a + token[:1, :1].astype(a.dtype)


def _pick(n, prefs):
    for p in prefs:
        if n % p == 0:
            return p
    return n


def _local_step(x, mem, target, w, token=None):
    t, d = x.shape
    heads = w["a_log"].shape[1]
    gw = heads * HEAD_DIM
    groups, cg, _ = w["pool_w"].shape
    pw = groups * cg
    n_main = 4 * gw + pw
    in_cols = n_main + 2 * heads
    s_in = w["w_in_t"].shape[0]

    tm = _pick(t, (512, 256, 128))
    tm_ln = _pick(t, (256, 128))
    tm_big = _pick(t, (1024, 512, 256, 128))
    tk = _pick(d, K_STEPS)

    w_in_t = w["w_in_t"].reshape(in_cols, d)
    w_p_t = w_in_t[4 * gw + 2 * heads:]
    w_ba_t = jnp.pad(w_in_t[4 * gw:4 * gw + 2 * heads], ((0, HEAD_DIM - 2 * heads), (0, 0)))
    x_bf = _after(token, x).astype(BF16)
    mem_bf = _after(token, mem).astype(BF16)

    tn_d = _pick(d, (1024, 512, 256, 128))
    proj = _plain("proj_main", x_bf, w_in_t, tb=True, n_used=4 * gw, tm=tm_big, tn=_pick(4 * gw, (1024, 512, 256, 128)),
                  tk=tk, out_dtype=F32)
    pproj = _plain("proj_pool", x_bf, w_p_t, tb=True, tm=tm_big, tn=_pick(pw, (1024, 512, 256, 128)), tk=tk, out_dtype=F32)
    ea, dtb = _gate_vectors(w["a_log"], w["dt_bias"], heads)
    vec128 = lambda i, j: (0, 0)
    ba, bg = _matmul(
        "proj_gates", x_bf, w_ba_t, tb=True, tm=tm, tn=HEAD_DIM, tk=tk,
        extra=[(ea, (1, HEAD_DIM), vec128), (dtb, (1, HEAD_DIM), vec128)],
        outs=[(jax.ShapeDtypeStruct((t, HEAD_DIM), F32), (tm, HEAD_DIM), _tile)] * 2,
        epilogue=_gates_epilogue(heads))
    qkv = _gdn_pre(proj, w["conv_w"], heads)
    o_gdn, states, *solved = _gdn_core(qkv, bg, heads)
    cat_g = _gdn_post(o_gdn, proj, 3 * gw, w["gdn_norm_w"], heads, tm)
    cat_p = _pool_forward(pproj, 0, w["pool_w"], w["pool_scale"])
    cat = jnp.concatenate([cat_g, cat_p], axis=1)
    w = {**w, **(yield ("weights", 1, cat))}
    h1, h1_bf, xhat1, rstd1 = _ln_forward("mix_ln1", cat, w["w_out"], x, w["ln1_g"], w["ln1_b"], tm=tm_ln, tk=tk)

    h1_bf = _after((yield ("relay", None, h1_bf)), h1_bf)
    q = _plain("xattn_q", h1_bf, w["xq_w"], tm=tm, tn=tn_d, tk=tk, out_dtype=BF16)
    mlen = mem.shape[0]
    tm_mem = _pick(mlen, (256, 128))
    k = _plain("xattn_k", mem_bf, w["xk_w"], tm=tm_mem, tn=tn_d, tk=tk, out_dtype=BF16)
    v = _plain("xattn_v", mem_bf, w["xv_w"], tm=tm_mem, tn=tn_d, tk=tk, out_dtype=BF16)
    att = _attention(q, k, v, tm)
    h2, h2_bf, xhat2, rstd2 = _ln_forward("xo_ln2", att, w["xo_w"], h1, w["ln2_g"], w["ln2_b"], tm=tm_ln, tk=tk)

    w = {**w, **(yield ("weights", 2, h2_bf))}
    s_up = w["w_up3"].shape[0]
    ff = s_up * w["w_up3"].shape[2]
    tn_f = _pick(ff // s_up, (1024, 512, 256, 128))

    def up_epi(acc, ex, out, i):
        r = jnp.maximum(acc, 0.0)
        out[0][...] = (r * r).astype(BF16)
        out[1][...] = (2.0 * r).astype(BF16)

    act, act_grad = _matmul(
        "mlp_up", h2_bf, w["w_up3"], b_blocks=s_up, tm=tm_big, tn=tn_f, tk=tk,
        outs=[(jax.ShapeDtypeStruct((t, ff), BF16), (tm_big, tn_f), _tile)] * 2, epilogue=up_epi)
    w = {**w, **(yield ("weights", 3, act))}
    tk_f = _pick(ff, K_STEPS)
    xhat3, rstd3 = _ln_forward("down_ln3", act, w["w_down"], h2, w["ln3_g"], w["ln3_b"], tm=tm, tk=tk_f, want_h=False)

    grads = {}
    du3, du3_bf, grads["ln3_g"], grads["ln3_b"], loss = _loss_and_ln_backward(
        xhat3, rstd3, w["ln3_g"], w["ln3_b"], target, tm_ln)

    def dup_epi(acc, ex, out, i):
        out[0][...] = (acc * ex[0][...].astype(F32)).astype(BF16)

    dup = _matmul(
        "mlp_down_dx", du3_bf, w["w_down"], tb=True, tm=tm_big, tn=tn_f, tk=tk,
        extra=[(act_grad, (tm_big, tn_f), _tile)],
        outs=[(jax.ShapeDtypeStruct((t, ff), BF16), (tm_big, tn_f), _tile)], epilogue=dup_epi)[0]
    tk_t = _pick(t, K_STEPS)
    tm_w = _pick(d, (512, 256, 128))
    grads["w_down"] = _plain("mlp_down_dw", act, du3_bf, ta=True, tm=_pick(ff, (512, 256, 128)), tn=d, tk=tk_t,
                             out_dtype=F32)
    grads["w_up3"] = _plain("mlp_up_dw", h2_bf, dup, ta=True, tm=tm_w, tn=ff // s_up, tk=tk_t, out_dtype=F32, out3=s_up,
                            n_outer=True)
    token = yield ("grads", 0, {n: grads.pop(n) for n in ("w_down", "w_up3")})
    dh2 = _plain("mlp_up_dx", dup, w["w_up3"], tb=True, b_blocks=s_up, tm=tm_big, tn=tn_d,
                 tk=_pick(ff // s_up, K_STEPS), out_dtype=F32)
    du2, du2_bf, grads["ln2_g"], grads["ln2_b"] = _ln_backward_rows(
        "ln2_bwd", dh2, du3, xhat2, rstd2, _after(token, w["ln2_g"]), tm_ln)
    token = yield ("poll", 0, du2_bf)

    grads["xo_w"] = _plain("xo_dw", att, du2_bf, ta=True, tm=tm_w, tn=d, tk=tk_t, out_dtype=F32)
    datt = _plain("xo_dx", du2_bf, w["xo_w"], tb=True, tm=tm, tn=tn_d, tk=tk, out_dtype=BF16)
    dq, dk, dv = _attention_backward(q, k, v, datt, tm)
    tk_m = _pick(mlen, (256, 128))
    grads["xq_w"] = _plain("xq_dw", h1_bf, dq, ta=True, tm=tm_w, tn=d, tk=tk_t, out_dtype=F32)
    grads["xk_w"] = _plain("xk_dw", mem_bf, dk, ta=True, tm=tm_w, tn=tn_d, tk=tk_m, out_dtype=F32)
    grads["xv_w"] = _plain("xv_dw", mem_bf, dv, ta=True, tm=tm_w, tn=tn_d, tk=tk_m, out_dtype=F32)
    du1, du1_bf, grads["ln1_g"], grads["ln1_b"] = _ln_backward(
        "xq_dx_ln1", dq, w["xq_w"], du2, xhat1, rstd1, _after(token, w["ln1_g"]), tm=tm_ln, tk=tk)

    grads["w_out"] = _plain("out_dw", cat, du1_bf, ta=True, tm=tm_w, tn=d, tk=tk_t, out_dtype=F32)
    token = yield ("grads", 1, {n: grads.pop(n) for n in ("xo_w", "xq_w", "xk_w", "xv_w", "w_out")})
    dcat = _plain("out_dx", du1_bf, w["w_out"], tb=True, tm=tm, tn=tn_d, tk=tk, out_dtype=F32)
    dp, grads["pool_w"], grads["pool_scale"] = _pool_backward(dcat, gw, pproj, 0, w["pool_w"],
                                                              _after(token, w["pool_scale"]))
    do_gdn, dz, grads["gdn_norm_w"] = _gdn_post_backward(dcat, o_gdn, proj, 3 * gw, _after(token, w["gdn_norm_w"]),
                                                         heads, tm)
    dqkv, dbg = _gdn_core_backward(qkv, bg, states, solved, do_gdn, heads)
    token = yield ("poll", 1, dqkv)
    dqkv_pre, grads["conv_w"] = _gdn_pre_backward(proj, _after(token, w["conv_w"]), dqkv, heads)
    dba, dalog_row, ddt_row = _gates_backward(ba, bg, dbg, ea, dtb, heads)
    grads["a_log"] = dalog_row[:, heads:2 * heads]
    grads["dt_bias"] = ddt_row[:, heads:2 * heads]

    dproj = jnp.concatenate([dqkv_pre, dz, dp], axis=1)
    dw_main = _plain("proj_dw", dproj, x_bf, ta=True, tm=_pick(n_main, (512, 256, 128)), tn=d, tk=tk_t, out_dtype=F32)
    dw_ba = _plain("proj_gates_dw", dba, x_bf, ta=True, tm=HEAD_DIM, tn=tn_d, tk=tk_t, out_dtype=F32)
    dw_in_t = jnp.concatenate([dw_main[:4 * gw], dw_ba[:2 * heads], dw_main[4 * gw:]], axis=0)
    grads["w_in_t"] = dw_in_t.reshape(s_in, in_cols // s_in, d)

    def dx_epi(acc, ex, out, i):
        out[0][...] = acc + ex[1][...] + ALPHA * ex[0][...]

    def add_epi(acc, ex, out, i):
        out[0][...] = acc + ex[0][...]

    token = yield ("grads", 2, {n: grads.pop(n) for n in ("w_in_t", "pool_w")})
    dx_gates = _plain("proj_gates_dx", dba, _after(token, w_ba_t), tm=tm, tn=tn_d, tk=HEAD_DIM, out_dtype=F32)
    out_tile = [(jax.ShapeDtypeStruct((t, d), F32), (tm, tn_d), _tile)]
    dx_pool = _matmul("proj_pool_dx", dp, w_p_t, tm=tm, tn=tn_d, tk=_pick(pw, K_STEPS),
                      extra=[(dx_gates, (tm, tn_d), _tile)], outs=out_tile, epilogue=add_epi)[0]
    grad_x = _matmul(
        "proj_dx", dproj, w_in_t, k_used=4 * gw, tm=tm, tn=tn_d, tk=_pick(4 * gw, K_STEPS),
        extra=[(du1, (tm, tn_d), _tile), (dx_pool, (tm, tn_d), _tile)], outs=out_tile, epilogue=dx_epi)[0]
    yield ("poll", 2, grad_x)
    return loss, grad_x, grads


def _adamw(name, w, g, m, v):
    r, c = w.shape
    if r % 8 == 0:
        tr = _pick(r, (256, 128, 64, 32, 16, 8))
        blk, steps = pl.BlockSpec((tr, c), lambda i: (i, 0)), r // tr
    else:
        tc = _pick(c, (256, 128))
        blk, steps = pl.BlockSpec((r, tc), lambda i: (0, i)), c // tc
    c1 = 1.0 - ADAM_B1 ** ADAM_STEP
    c2 = 1.0 - ADAM_B2 ** ADAM_STEP

    def body(w_ref, g_ref, m_ref, v_ref, d_ref, mo_ref, vo_ref, go_ref):
        gv = g_ref[...]
        mn = ADAM_B1 * m_ref[...] + (1.0 - ADAM_B1) * gv
        vn = ADAM_B2 * v_ref[...] + (1.0 - ADAM_B2) * (gv * gv)
        d_ref[...] = -ADAM_LR * ((mn / c1) / (jnp.sqrt(vn / c2) + ADAM_EPS) + ADAM_WD * w_ref[...])
        mo_ref[...] = mn
        vo_ref[...] = vn
        go_ref[...] = gv

    return pl.pallas_call(
        body, name=name, grid=(steps,), in_specs=[blk] * 4, out_specs=[blk] * 4,
        out_shape=[jax.ShapeDtypeStruct((r, c), F32)] * 4,
        compiler_params=_params("parallel"),
    )(w, g, m, v)


def _place():
    x, y, c = lax.axis_index("x"), lax.axis_index("y"), lax.axis_index("c")
    chips = [(1 - x, y), (x, 1 - y), (1 - x, 1 - y)]
    return x, y, c, chips


HBM = pl.BlockSpec(memory_space=pltpu.HBM)


SEM = pl.BlockSpec(memory_space=pltpu.SEMAPHORE)
ANY = pl.BlockSpec(memory_space=pl.ANY)
EFFECT = pltpu.SideEffectType.DATAFLOW_SIDE_EFFECTING


def _in_hbm(a):
    return pltpu.with_memory_space_constraint(a, pltpu.HBM)


def _remote(src, dst, send_sem, recv_sem, to):
    return pltpu.make_async_remote_copy(src_ref=src, dst_ref=dst, send_sem=send_sem, recv_sem=recv_sem,
                                        device_id=to, device_id_type=MESH)


def _by_rows(rows):
    return rows % 32 == 0


def _half_shape(rows, cols):
    return (rows // 2, cols) if _by_rows(rows) else (rows, cols // 2)


def _half(ref, which, *lead):
    rows, cols = ref.shape[-2:]
    if _by_rows(rows):
        return ref.at[(*lead, pl.ds(which * (rows // 2), rows // 2))]
    return ref.at[(*lead, slice(None), pl.ds(which * (cols // 2), cols // 2))]


def _landed(lands, i, shard_index, which):
    return _half(lands[i], which, shard_index)


def _routes():
    x, y, c, _ = _place()
    first = (jnp.where(c == 0, 1 - x, x), jnp.where(c == 0, y, 1 - y))
    second = (jnp.where(c == 0, x, 1 - x), jnp.where(c == 0, 1 - y, y))
    return first, second, (1 - x, 1 - y)


def _shard_of(chip):
    return 2 * chip[0] + chip[1]


def _gather_start(name, shards, after, relayed=()):
    n = len(shards)
    lands = [lax.empty((N_SHARD,) + s.shape, s.dtype) for s in shards]

    def body(*refs):
        ins, zones = refs[:n], refs[n:2 * n]
        ici_send, ici_recv, own_send, own_recv = refs[2 * n + 1:2 * n + 5]
        token = refs[-1]
        x, y, c, chips = _place()
        me = 2 * x + y
        first, _, _ = _routes()
        for i in range(n):
            if i in relayed:
                _remote(_half(ins[i], c), _landed(zones, i, me, c), ici_send.at[3 * i], ici_recv.at[3 * i],
                        (*first, c)).start()
                continue
            for j, chip in enumerate(chips):
                _remote(_half(ins[i], c), _landed(zones, i, me, c), ici_send.at[3 * i + j],
                        ici_recv.at[3 * i + j], (*chip, c)).start()
        for i in range(n):
            _remote(ins[i], zones[i].at[me], own_send.at[i], own_recv.at[i], (x, y, 1 - c)).start()
        token[...] = jnp.zeros_like(token)

    dma = pltpu.SemaphoreType.DMA
    outs = pl.pallas_call(
        body, name=name,
        in_specs=[HBM] * (2 * n) + [ANY],
        out_shape=(dma((3 * n,)), dma((3 * n,)), dma((n,)), dma((n,)),
                   *[pltpu.HBM(a.shape, a.dtype) for a in shards + lands], jax.ShapeDtypeStruct((8, LANES), F32)),
        out_specs=(SEM, SEM, SEM, SEM, *[HBM] * (2 * n), pl.BlockSpec(memory_space=pltpu.VMEM)),
        input_output_aliases={k: 4 + k for k in range(2 * n)},
        compiler_params=pltpu.CompilerParams(has_side_effects=EFFECT),
    )(*[_in_hbm(a) for a in shards + lands], after)
    sems = dict(zip(("ici_send", "ici_recv", "own_send", "own_recv"), outs[:4]))
    return sems, list(outs[4:4 + n]), list(outs[4 + n:4 + 2 * n]), outs[-1]


def _gather_forward(name, idx, lands, sems, after):
    n = len(idx)

    def body(*refs):
        zones = refs[:n]
        ici_recv = refs[n]
        fwd_send, fwd_recv = refs[n + 2], refs[n + 3]
        x, y, c, chips = _place()
        for k, i in enumerate(idx):
            for j, chip in enumerate(chips):
                half = _landed(zones, k, 2 * chip[0] + chip[1], c)
                _remote(half, half, fwd_send.at[3 * k + j], ici_recv.at[3 * i + j], (*chip, c)).wait_recv()
                _remote(half, half, fwd_send.at[3 * k + j], fwd_recv.at[3 * k + j], (x, y, 1 - c)).start()

    dma = pltpu.SemaphoreType.DMA
    outs = pl.pallas_call(
        body, name=name,
        in_specs=[HBM] * n + [SEM, ANY],
        out_shape=(dma((3 * n,)), dma((3 * n,)), *[pltpu.HBM(a.shape, a.dtype) for a in lands]),
        out_specs=(SEM, SEM, *[HBM] * n),
        input_output_aliases={k: 2 + k for k in range(n)},
        compiler_params=pltpu.CompilerParams(has_side_effects=EFFECT),
    )(*lands, sems["ici_recv"], after)
    return (outs[0], outs[1]), list(outs[2:])


def _gather_wait(name, idx, shards, lands, sems, fwd, after):
    n = len(idx)

    def body(*refs):
        ins, zones = refs[:n], refs[n:2 * n]
        ici_send, own_send, own_recv, fwd_send, fwd_recv = refs[2 * n:2 * n + 5]
        x, y, c, chips = _place()
        me = 2 * x + y
        for k, i in enumerate(idx):
            mine = _half(ins[k], c)
            for j, chip in enumerate(chips):
                theirs = 2 * chip[0] + chip[1]
                _remote(mine, _landed(zones, k, me, c), ici_send.at[3 * i + j], fwd_recv.at[3 * k + j],
                        (*chip, c)).wait_send()
                sent = _landed(zones, k, theirs, c)
                _remote(sent, sent, fwd_send.at[3 * k + j], fwd_recv.at[3 * k + j], (x, y, 1 - c)).wait_send()
                passed = _landed(zones, k, theirs, 1 - c)
                _remote(passed, passed, fwd_send.at[3 * k + j], fwd_recv.at[3 * k + j], (x, y, 1 - c)).wait_recv()
            own = _remote(ins[k], zones[k].at[me], own_send.at[i], own_recv.at[i], (x, y, 1 - c))
            own.wait_send()
            own.wait_recv()

    outs = pl.pallas_call(
        body, name=name,
        in_specs=[HBM] * (2 * n) + [SEM] * 5 + [ANY],
        out_shape=tuple(pltpu.HBM(a.shape, a.dtype) for a in lands),
        out_specs=tuple([HBM] * n),
        input_output_aliases={n + k: k for k in range(n)},
        compiler_params=pltpu.CompilerParams(has_side_effects=EFFECT),
    )(*shards, *lands, sems["ici_send"], sems["own_send"], sems["own_recv"], fwd[0], fwd[1], after)
    return list(outs)


def _gather_relay(name, idx, shards, lands, sems, after):
    n = len(idx)

    def body(*refs):
        ins, zones, ici_recv = refs[:n], refs[n:2 * n], refs[2 * n]
        relay_send, relay_recv, pass_send, pass_recv = refs[2 * n + 2:2 * n + 6]
        x, y, c, _ = _place()
        first, second, _ = _routes()
        for k, i in enumerate(idx):
            landed = _landed(zones, k, _shard_of(first), c)
            _remote(landed, landed, pass_send.at[k], ici_recv.at[3 * i], (*first, c)).wait_recv()
            _remote(_half(ins[k], c), _landed(zones, k, 2 * x + y, c), relay_send.at[2 * k], relay_recv.at[2 * k],
                    (*second, c)).start()
            _remote(landed, landed, relay_send.at[2 * k + 1], relay_recv.at[2 * k + 1], (*second, c)).start()
            _remote(landed, landed, pass_send.at[k], pass_recv.at[k], (x, y, 1 - c)).start()
        refs[-1][...] = jnp.zeros_like(refs[-1])

    dma = pltpu.SemaphoreType.DMA
    outs = pl.pallas_call(
        body, name=name,
        in_specs=[HBM] * (2 * n) + [SEM, ANY],
        out_shape=(dma((2 * n,)), dma((2 * n,)), dma((n,)), dma((n,)), *[pltpu.HBM(a.shape, a.dtype) for a in lands],
                   jax.ShapeDtypeStruct((8, LANES), F32)),
        out_specs=(SEM, SEM, SEM, SEM, *[HBM] * n, pl.BlockSpec(memory_space=pltpu.VMEM)),
        input_output_aliases={n + k: 4 + k for k in range(n)},
        compiler_params=pltpu.CompilerParams(has_side_effects=EFFECT),
    )(*shards, *lands, sems["ici_recv"], after)
    return outs[:4], list(outs[4:4 + n]), outs[-1]


def _gather_forward_relayed(name, ks, lands, relay, after):
    n = len(ks)

    def body(*refs):
        zones, relay_recv = refs[:n], refs[n]
        fwd_send, fwd_recv = refs[n + 2], refs[n + 3]
        x, y, c, _ = _place()
        _, second, diagonal = _routes()
        for p, k in enumerate(ks):
            for j, chip in enumerate((second, diagonal)):
                landed = _landed(zones, p, _shard_of(chip), c)
                _remote(landed, landed, fwd_send.at[2 * p + j], relay_recv.at[2 * k + j], (*second, c)).wait_recv()
                _remote(landed, landed, fwd_send.at[2 * p + j], fwd_recv.at[2 * p + j], (x, y, 1 - c)).start()

    dma = pltpu.SemaphoreType.DMA
    outs = pl.pallas_call(
        body, name=name,
        in_specs=[HBM] * n + [SEM, ANY],
        out_shape=(dma((2 * n,)), dma((2 * n,)), *[pltpu.HBM(a.shape, a.dtype) for a in lands]),
        out_specs=(SEM, SEM, *[HBM] * n),
        input_output_aliases={k: 2 + k for k in range(n)},
        compiler_params=pltpu.CompilerParams(has_side_effects=EFFECT),
    )(*lands, relay[1], after)
    return (outs[0], outs[1]), list(outs[2:])


def _gather_wait_relayed(name, idx, ks, shards, lands, sems, relay, fwd, after):
    n = len(idx)

    def body(*refs):
        ins, zones = refs[:n], refs[n:2 * n]
        ici_send, own_send, own_recv, relay_send, pass_send, pass_recv, fwd_send, fwd_recv = refs[2 * n:2 * n + 8]
        x, y, c, _ = _place()
        me = 2 * x + y
        sibling = (x, y, 1 - c)
        first, second, diagonal = _routes()
        for p, (i, k) in enumerate(zip(idx, ks)):
            mine, at_peer = _half(ins[p], c), _landed(zones, p, me, c)
            from_first = _landed(zones, p, _shard_of(first), c)
            _remote(mine, at_peer, ici_send.at[3 * i], pass_recv.at[k], (*first, c)).wait_send()
            _remote(mine, at_peer, relay_send.at[2 * k], pass_recv.at[k], (*second, c)).wait_send()
            _remote(from_first, from_first, relay_send.at[2 * k + 1], pass_recv.at[k], (*second, c)).wait_send()
            _remote(from_first, from_first, pass_send.at[k], pass_recv.at[k], sibling).wait_send()
            theirs = _landed(zones, p, _shard_of(second), 1 - c)
            _remote(theirs, theirs, pass_send.at[k], pass_recv.at[k], sibling).wait_recv()
            for j, (sent, got) in enumerate(((second, first), (diagonal, diagonal))):
                out_half = _landed(zones, p, _shard_of(sent), c)
                _remote(out_half, out_half, fwd_send.at[2 * p + j], fwd_recv.at[2 * p + j], sibling).wait_send()
                in_half = _landed(zones, p, _shard_of(got), 1 - c)
                _remote(in_half, in_half, fwd_send.at[2 * p + j], fwd_recv.at[2 * p + j], sibling).wait_recv()
            own = _remote(ins[p], zones[p].at[me], own_send.at[i], own_recv.at[i], sibling)
            own.wait_send()
            own.wait_recv()

    outs = pl.pallas_call(
        body, name=name,
        in_specs=[HBM] * (2 * n) + [SEM] * 8 + [ANY],
        out_shape=tuple(pltpu.HBM(a.shape, a.dtype) for a in lands),
        out_specs=tuple([HBM] * n),
        input_output_aliases={n + k: k for k in range(n)},
        compiler_params=pltpu.CompilerParams(has_side_effects=EFFECT),
    )(*shards, *lands, sems["ici_send"], sems["own_send"], sems["own_recv"], relay[0], relay[2], relay[3],
      fwd[0], fwd[1], after)
    return list(outs)


def _all_reduce_small(name, slab, after=None):
    r, width = slab.shape
    ndev = 8

    def body(x_ref, after_ref, out_ref, buf, send_sems, recv_sems):
        x, y, c, _ = _place()
        me = 4 * x + 2 * y + c
        buf[me] = x_ref[...]
        copies = []
        for k in range(1, ndev):
            peer = jnp.bitwise_xor(me, k)
            to = (peer // 4, (peer // 2) % 2, peer % 2)
            cp = pltpu.make_async_remote_copy(src_ref=x_ref, dst_ref=buf.at[me], send_sem=send_sems.at[k - 1],
                                              recv_sem=recv_sems.at[k - 1], device_id=to, device_id_type=MESH)
            cp.start()
            copies.append(cp)
        for k in range(1, ndev):
            peer = jnp.bitwise_xor(me, k)
            pltpu.make_async_remote_copy(src_ref=x_ref, dst_ref=buf.at[peer], send_sem=send_sems.at[k - 1],
                                         recv_sem=recv_sems.at[k - 1], device_id=(x, y, c),
                                         device_id_type=MESH).wait_recv()
        for cp in copies:
            cp.wait_send()
        total = buf[0]
        for d in range(1, ndev):
            total = total + buf[d]
        out_ref[...] = total

    return pl.pallas_call(
        body, name=name,
        in_specs=[pl.BlockSpec(memory_space=pltpu.VMEM), ANY], out_specs=pl.BlockSpec(memory_space=pltpu.VMEM),
        out_shape=jax.ShapeDtypeStruct((r, width), F32),
        scratch_shapes=[pltpu.VMEM((ndev, r, width), F32), pltpu.SemaphoreType.DMA((ndev - 1,)),
                        pltpu.SemaphoreType.DMA((ndev - 1,))],
        compiler_params=pltpu.CompilerParams(vmem_limit_bytes=VMEM_LIMIT),
    )(slab, slab if after is None else after)


def _half_tiling(rows, cols):
    if _by_rows(rows):
        tr = _pick(rows // 2, (256, 128, 64, 32, 16))
        nb = (rows // 2) // tr
        return (tr, cols), nb, (lambda which, b: (which * nb + b, 0)), (lambda b: (b, 0))
    tc = _pick(cols // 2, (256, 128))
    nb = (cols // 2) // tc
    return (rows, tc), nb, (lambda which, b: (0, which * nb + b)), (lambda b: (0, b))


def _chip_partial(name, grad, other, core):
    s, r, cdim = grad.shape
    blk, nb, whole, within = _half_tiling(r, cdim)

    def body(core_ref, g_ref, o_ref, out_ref):
        out_ref[...] = (g_ref[...] + o_ref[...]).astype(BF16)

    return pl.pallas_call(
        body, name=name,
        grid_spec=pltpu.PrefetchScalarGridSpec(
            num_scalar_prefetch=1, grid=(s, nb),
            in_specs=[pl.BlockSpec((None,) + blk, lambda j, b, core_ref: (j,) + whole(core_ref[0], b)),
                      pl.BlockSpec((None,) + blk, lambda j, b, core_ref: (j,) + within(b))],
            out_specs=pl.BlockSpec((None,) + blk, lambda j, b, core_ref: (j,) + within(b))),
        out_shape=jax.ShapeDtypeStruct((s,) + _half_shape(r, cdim), BF16),
        compiler_params=_params("parallel", "parallel"),
    )(core, grad, other)


def _partial_copies(ins, zones, send_sems, recv_sems):
    x, y, c, chips = _place()
    return [_remote(ins[i].at[2 * chip[0] + chip[1]], zones[i].at[j], send_sems.at[3 * i + j],
                    recv_sems.at[3 * i + j], (*chip, c))
            for i in range(len(ins)) for j, chip in enumerate(chips)]


def _swap_copies(ins, zones, send_sems, recv_sems):
    x, y, c, _ = _place()
    copies = []
    for i in range(len(ins)):
        for s in range(N_SHARD):
            copies.append(_remote(_half(ins[i], 1 - c, s), zones[i].at[s],
                                  send_sems.at[N_SHARD * i + s], recv_sems.at[N_SHARD * i + s], (x, y, 1 - c)))
    return copies


def _exchange_start(name, plan, sources, lands, per_array):
    n = len(sources)
    lands = [lax.empty(shape, dtype) for shape, dtype in lands]

    def body(*refs):
        for cp in plan(refs[:n], refs[n:2 * n], refs[2 * n], refs[2 * n + 1]):
            cp.start()
        refs[-1][...] = jnp.zeros_like(refs[-1])

    dma = pltpu.SemaphoreType.DMA
    outs = pl.pallas_call(
        body, name=name,
        in_specs=[HBM] * (2 * n),
        out_shape=(dma((per_array * n,)), dma((per_array * n,)),
                   *[pltpu.HBM(a.shape, a.dtype) for a in list(sources) + lands], jax.ShapeDtypeStruct((8, LANES), F32)),
        out_specs=(SEM, SEM, *[HBM] * (2 * n), pl.BlockSpec(memory_space=pltpu.VMEM)),
        input_output_aliases={k: 2 + k for k in range(2 * n)},
        compiler_params=pltpu.CompilerParams(has_side_effects=EFFECT),
    )(*[_in_hbm(a) for a in list(sources) + lands])
    return (outs[0], outs[1]), list(outs[2:2 + n]), list(outs[2 + n:2 + 2 * n]), outs[-1]


def _exchange_wait(name, plan, started, after):
    sems, partials, lands, _ = started
    n = len(partials)

    def body(*refs):
        for cp in plan(refs[:n], refs[n:2 * n], refs[2 * n], refs[2 * n + 1]):
            cp.wait_send()
            cp.wait_recv()

    outs = pl.pallas_call(
        body, name=name,
        in_specs=[HBM] * (2 * n) + [SEM, SEM] + [ANY] * len(after),
        out_shape=tuple(pltpu.HBM(a.shape, a.dtype) for a in lands),
        out_specs=tuple([HBM] * n),
        input_output_aliases={n + k: k for k in range(n)},
        compiler_params=pltpu.CompilerParams(has_side_effects=EFFECT),
    )(*partials, *lands, sems[0], sems[1], *after)
    return list(outs)


def _reduce_own(name, grad, other, received, where):
    s, r, cdim = grad.shape
    blk, nb, whole, within = _half_tiling(r, cdim)

    def body(where_ref, g_ref, o_ref, r_ref, out_ref):
        total = g_ref[...] + o_ref[...]
        for j in range(3):
            total = total + r_ref[j].astype(F32)
        out_ref[...] = total

    return pl.pallas_call(
        body, name=name,
        grid_spec=pltpu.PrefetchScalarGridSpec(
            num_scalar_prefetch=1, grid=(nb,),
            in_specs=[pl.BlockSpec((None,) + blk, lambda b, w_ref: (w_ref[0],) + whole(w_ref[1], b)),
                      pl.BlockSpec((None,) + blk, lambda b, w_ref: (w_ref[0],) + within(b)),
                      pl.BlockSpec((3,) + blk, lambda b, w_ref: (0,) + within(b))],
            out_specs=pl.BlockSpec(blk, lambda b, w_ref: whole(w_ref[1], b))),
        out_shape=jax.ShapeDtypeStruct((r, cdim), F32),
        compiler_params=_params("parallel"),
    )(where, grad, other, received)


def _join_start(name, halves):
    n = len(halves)

    def body(*refs):
        bufs, send_sems, recv_sems = refs[:n], refs[n], refs[n + 1]
        x, y, c, _ = _place()
        for i in range(n):
            mine = _half(bufs[i], c)
            _remote(mine, mine, send_sems.at[i], recv_sems.at[i], (x, y, 1 - c)).start()
        refs[-1][...] = jnp.zeros_like(refs[-1])

    dma = pltpu.SemaphoreType.DMA
    outs = pl.pallas_call(
        body, name=name,
        in_specs=[HBM] * n,
        out_shape=(dma((n,)), dma((n,)), *[pltpu.HBM(h.shape, F32) for h in halves], jax.ShapeDtypeStruct((8, LANES), F32)),
        out_specs=(SEM, SEM, *[HBM] * n, pl.BlockSpec(memory_space=pltpu.VMEM)),
        input_output_aliases={k: 2 + k for k in range(n)},
        compiler_params=pltpu.CompilerParams(has_side_effects=EFFECT),
    )(*[_in_hbm(h) for h in halves])
    return (outs[0], outs[1]), list(outs[2:2 + n]), outs[-1]


def _join_wait(name, started, after):
    sems, bufs, _ = started
    n = len(bufs)

    def body(*refs):
        bufs, send_sems, recv_sems = refs[:n], refs[n], refs[n + 1]
        x, y, c, _ = _place()
        for i in range(n):
            mine, theirs = _half(bufs[i], c), _half(bufs[i], 1 - c)
            _remote(mine, mine, send_sems.at[i], recv_sems.at[i], (x, y, 1 - c)).wait_send()
            _remote(theirs, theirs, send_sems.at[i], recv_sems.at[i], (x, y, 1 - c)).wait_recv()

    outs = pl.pallas_call(
        body, name=name,
        in_specs=[HBM] * n + [SEM, SEM] + [ANY] * len(after),
        out_shape=tuple(pltpu.HBM(b.shape, F32) for b in bufs),
        out_specs=tuple([HBM] * n),
        input_output_aliases={k: k for k in range(n)},
        compiler_params=pltpu.CompilerParams(has_side_effects=EFFECT),
    )(*bufs, sems[0], sems[1], *after)
    return list(outs)


BIG = ("w_in", "pool_w", "w_out", "xq_w", "xk_w", "xv_w", "xo_w", "w_up", "w_down")
GATHER_GROUPS = ((0, 1), (2, 3, 4, 5, 6), (7,), (8,))
RELAYED = (7, 8)
SMALL = ("conv_w", "a_log", "dt_bias", "gdn_norm_w", "pool_scale", "ln1_g", "ln1_b", "ln2_g", "ln2_b", "ln3_g", "ln3_b")
ORDER = ("w_in", "conv_w", "a_log", "dt_bias", "gdn_norm_w", "pool_w", "pool_scale", "w_out", "ln1_g", "ln1_b",
         "xq_w", "xk_w", "xv_w", "xo_w", "ln2_g", "ln2_b", "w_up", "w_down", "ln3_g", "ln3_b")
LANES = 128


def _rows(flat_len):
    return -(-flat_len // LANES)


def _pack(pieces):
    out = []
    for p in pieces:
        flat = p.reshape(-1).astype(F32)
        out.append(jnp.pad(flat, (0, _rows(flat.shape[0]) * LANES - flat.shape[0])).reshape(-1, LANES))
    slab = jnp.concatenate(out, axis=0)
    return jnp.pad(slab, ((0, -slab.shape[0] % 8), (0, 0)))


def _unpack(slab, shapes):
    out, row = [], 0
    for shp in shapes:
        size = math.prod(shp)
        out.append(slab[row:row + _rows(size)].reshape(-1)[:size].reshape(shp))
        row += _rows(size)
    return out


TRANSPOSED = ("w_in",)


def _as2d(name, a):
    a = a[0]
    if name in TRANSPOSED:
        return jnp.swapaxes(a, 0, 1)
    return a.reshape(-1, a.shape[-1]) if a.ndim == 3 else a


def _from2d(name, a, shape):
    return (jnp.swapaxes(a, 0, 1) if name in TRANSPOSED else a).reshape(shape)


def kernel(x, mem, w_in, conv_w, a_log, dt_bias, gdn_norm_w, pool_w, pool_scale, w_out, ln1_g, ln1_b, xq_w, xk_w, xv_w, xo_w, ln2_g, ln2_b, w_up, w_down, ln3_g, ln3_b, loss_target, m_w_in, m_conv_w, m_a_log, m_dt_bias, m_gdn_norm_w, m_pool_w, m_pool_scale, m_w_out, m_ln1_g, m_ln1_b, m_xq_w, m_xk_w, m_xv_w, m_xo_w, m_ln2_g, m_ln2_b, m_w_up, m_w_down, m_ln3_g, m_ln3_b, v_w_in, v_conv_w, v_a_log, v_dt_bias, v_gdn_norm_w, v_pool_w, v_pool_scale, v_w_out, v_ln1_g, v_ln1_b, v_xq_w, v_xk_w, v_xv_w, v_xo_w, v_ln2_g, v_ln2_b, v_w_up, v_w_down, v_ln3_g, v_ln3_b):
    given = dict(locals())
    cx, cy, cc = lax.axis_index("x"), lax.axis_index("y"), lax.axis_index("c")
    me = 2 * cx + cy
    groups = pool_w.shape[1]
    cs = pool_w.shape[2]
    kk, conv_cols = conv_w.shape[1], conv_w.shape[2]
    core = cc.astype(jnp.int32).reshape(1)
    where = jnp.stack([me, cc]).astype(jnp.int32)

    conv_slab = jnp.zeros((kk, N_SHARD * conv_cols), F32)
    conv_slab = lax.dynamic_update_slice(conv_slab, conv_w[0] * (cc == 0).astype(F32), (0, me * conv_cols))
    wts = {"conv_w": _unpack(_all_reduce_small("gather_conv_w", _pack([conv_slab])), [conv_slab.shape])[0]}

    started = {}

    def start(name, idx, after, token=None):
        casts = [_after(token, _as2d(BIG[i], given[BIG[i]])).astype(BF16) for i in idx]
        relayed = tuple(k for k, i in enumerate(idx) if i in RELAYED)
        sems, shards, lands, token = _gather_start(name, casts, after, relayed)
        for k, i in enumerate(idx):
            started[i] = (sems, k, shards[k], lands[k])
        return token

    token = start("gather_start_first", GATHER_GROUPS[0], wts["conv_w"])
    token = start("gather_start_rest", tuple(i for group in GATHER_GROUPS[1:] for i in group), token, token)

    relay = {}

    def send_on(after):
        members = [started[i] for i in RELAYED]
        relay["sems"], zones, token = _gather_relay("gather_relay", [m[1] for m in members], [m[2] for m in members],
                                                    [m[3] for m in members], members[0][0], after)
        relay["zones"] = dict(zip(RELAYED, zones))
        return token

    def fetch(group, after):
        members = [started[i] for i in GATHER_GROUPS[group]]
        sems, idx = members[0][0], [m[1] for m in members]
        shards = [m[2] for m in members]
        if GATHER_GROUPS[group][0] in RELAYED:
            ks = [RELAYED.index(i) for i in GATHER_GROUPS[group]]
            zones = [relay["zones"][i] for i in GATHER_GROUPS[group]]
            fwd, zones = _gather_forward_relayed(f"gather_forward_{group}", ks, zones, relay["sems"], after)
            got = _gather_wait_relayed(f"gather_wait_{group}", idx, ks, shards, zones, sems, relay["sems"], fwd, after)
        else:
            fwd, zones = _gather_forward(f"gather_forward_{group}", idx, [m[3] for m in members], sems, after)
            got = _gather_wait(f"gather_wait_{group}", idx, shards, zones, sems, fwd, after)
        full = dict(zip([BIG[i] for i in GATHER_GROUPS[group]], got))
        out = {}
        for n, a in full.items():
            if n == "w_in":
                out["w_in_t"] = a
            elif n == "w_up":
                out["w_up3"] = a
            elif n == "pool_w":
                out[n] = a.reshape(N_SHARD, groups, cs, -1).transpose(1, 0, 2, 3).reshape(groups, N_SHARD * cs, -1)
            else:
                out[n] = a.reshape(-1, a.shape[-1])
        return out

    for n in ("a_log", "dt_bias", "gdn_norm_w", "pool_scale", "ln1_g", "ln1_b", "ln2_g", "ln2_b", "ln3_g", "ln3_b"):
        wts[n] = given[n]
    wts.update(fetch(0, token))

    def start_swap(group, grads):
        names, blocks = [], []
        for n, g in grads.items():
            if n == "pool_w":
                g = g.reshape(groups, N_SHARD, cs, -1).transpose(1, 0, 2, 3).reshape(N_SHARD, groups * cs, -1)
            elif g.ndim == 2:
                g = g.reshape(N_SHARD, -1, g.shape[-1])
            names.append({"w_in_t": "w_in", "w_up3": "w_up"}.get(n, n))
            blocks.append(g)
        zones = [((N_SHARD,) + _half_shape(b.shape[1], b.shape[2]), F32) for b in blocks]
        swap = _exchange_start(f"grad_swap_start_{group}", _swap_copies, blocks, zones, N_SHARD)
        return {"group": group, "names": names, "swap": swap, "token": swap[3]}

    def start_send(state, after):
        group, names = state["group"], state["names"]
        state["blocks"] = state["swap"][1]
        state["others"] = _exchange_wait(f"grad_swap_wait_{group}", _swap_copies, state["swap"], after)
        partials = [_chip_partial("chip_partial_" + n, gb, ob, core)
                    for n, gb, ob in zip(names, state["blocks"], state["others"])]
        zones = [((3,) + p.shape[1:], BF16) for p in partials]
        state["send"] = _exchange_start(f"grad_send_start_{group}", _partial_copies, partials, zones, 3)
        state["token"] = state["send"][3]

    grad, delta, new_m, new_v = {}, {}, {}, {}

    def start_join(state, after):
        group, names = state["group"], state["names"]
        received = _exchange_wait(f"grad_send_wait_{group}", _partial_copies, state["send"], after)
        halves = [_reduce_own("reduce_own_" + n, gb, ob, rb, where)
                  for n, gb, ob, rb in zip(names, state["blocks"], state["others"], received)]
        state["join"] = _join_start(f"grad_join_start_{group}", halves)
        return state["join"][2]

    def finish_reduce(state, after):
        group, names = state["group"], state["names"]
        for n, g in zip(names, _join_wait(f"grad_join_wait_{group}", state["join"], after)):
            shp = given[n].shape
            d2, m2, v2, g2 = _adamw("adamw_" + n, _as2d(n, given[n]), g, _as2d(n, given["m_" + n]),
                                    _as2d(n, given["v_" + n]))
            grad[n], delta[n], new_m[n], new_v[n] = (_from2d(n, a, shp) for a in (g2, d2, m2, v2))
        return d2

    step = _local_step(x[0], mem[0], loss_target[0], wts, token)
    pending = {}
    request = next(step)
    while True:
        try:
            kind, group, payload = request
            if kind == "weights":
                request = step.send(fetch(group, payload))
            elif kind == "relay":
                request = step.send(send_on(payload))
            elif kind == "grads":
                pending[group] = start_swap(group, payload)
                request = step.send(pending[group]["token"])
            else:
                start_send(pending[group], [payload])
                request = step.send(pending[group]["token"])
        except StopIteration as stop:
            loss_row, grad_x, g = stop.value
            break

    after = [pending[2]["token"], grad_x]
    for group in (0, 1):
        after = [start_join(pending[group], after)]
    for group in (0, 1):
        after = [finish_reduce(pending[group], after)]
    after = [finish_reduce(pending[2], [start_join(pending[2], after)])]

    small_names = ("a_log", "dt_bias", "gdn_norm_w", "pool_scale", "ln1_g", "ln1_b", "ln2_g", "ln2_b", "ln3_g", "ln3_b")
    pieces = [g["conv_w"]] + [g[n] for n in small_names] + [loss_row[:, :1]]
    shapes = [p.shape for p in pieces]
    summed = _unpack(_all_reduce_small("all_reduce_small", _pack(pieces), after[0]), shapes)
    gsmall = dict(zip(small_names, summed[1:-1]))
    gsmall["conv_w"] = lax.dynamic_slice(summed[0], (0, me * conv_cols), (kk, conv_cols))
    loss = summed[-1][0, 0]

    sshapes = [given[n].shape for n in SMALL]
    slabs = [_pack([given[p + n] for n in SMALL]) for p in ("", "m_", "v_")]
    gslab = _pack([gsmall[n] for n in SMALL])
    outs = _adamw("adamw_small", slabs[0], gslab, slabs[1], slabs[2])[:3]
    for dst, slab in zip((delta, new_m, new_v), outs):
        dst.update(zip(SMALL, _unpack(slab, sshapes)))
    for n in SMALL:
        grad[n] = gsmall[n].reshape(given[n].shape)

    return (loss, grad_x[None], *[grad[n] for n in ORDER], *[delta[n] for n in ORDER],
            *[new_m[n] for n in ORDER], *[new_v[n] for n in ORDER])
```

```python
import math

import jax
import jax.numpy as jnp
from jax import lax
from jax.experimental import pallas as pl
from jax.experimental.pallas import tpu as pltpu

F32 = jnp.float32
BF16 = jnp.bfloat16
MESH = pl.DeviceIdType.MESH

HEAD_DIM = 128
CHUNK = 64
POOL_WINDOWS = (2, 4, 8, 16)
XATTN_HEADS = 4
ALPHA = 2.0 ** 0.25
LN_EPS = 1e-5
NORM_EPS = 1e-6
ADAM_LR, ADAM_B1, ADAM_B2, ADAM_EPS, ADAM_WD, ADAM_STEP = 0.001, 0.9, 0.999, 1e-08, 0.01, 10
N_SHARD = 4
VMEM_LIMIT = 56 * 1024 * 1024
K_STEPS = (2048, 1024, 512, 256, 128)


def _params(*sem):
    return pltpu.CompilerParams(dimension_semantics=sem, vmem_limit_bytes=VMEM_LIMIT)


def _bdot(a, b, ta=False, tb=False):
    dims = (((0 if ta else 1,), (1 if tb else 0,)), ((), ()))
    return lax.dot_general(a.astype(BF16), b.astype(BF16), dims, preferred_element_type=F32)


def _sigmoid(x):
    return 1.0 / (1.0 + jnp.exp(-x))


def _matmul(name, a, b, *, ta=False, tb=False, tm, tn, tk, extra=(), outs, epilogue, b_blocks=None,
            sequential=False, n_used=None, k_used=None, n_outer=False):
    m, k_dim = (a.shape[1], a.shape[0]) if ta else a.shape
    if b_blocks and tb:
        n = b.shape[1]
        k_dim = b.shape[0] * b.shape[2]
        per = b.shape[2] // tk
        b_spec = pl.BlockSpec((None, tn, tk), lambda i, j, k: (k // per, j, k % per))
    elif b_blocks:
        n = b.shape[0] * b.shape[2]
        per = b.shape[2] // tn
        b_spec = pl.BlockSpec((None, tk, tn), lambda i, j, k: (j // per, k, j % per))
    elif tb:
        n = b.shape[0]
        b_spec = pl.BlockSpec((tn, tk), lambda i, j, k: (j, k))
    else:
        n = b.shape[1]
        b_spec = pl.BlockSpec((tk, tn), lambda i, j, k: (k, j))
    n, k_dim = n_used or n, k_used or k_dim
    assert m % tm == 0 and n % tn == 0 and k_dim % tk == 0, (name, m, n, k_dim, tm, tn, tk)
    nk = k_dim // tk
    a_spec = pl.BlockSpec((tk, tm), lambda i, j, k: (k, i)) if ta else pl.BlockSpec((tm, tk), lambda i, j, k: (i, k))
    n_extra, n_out = len(extra), len(outs)

    def wrap(index_map):
        return lambda i, j, k: index_map(i, j)

    def spec(block, index_map):
        if n_outer:
            return pl.BlockSpec(block, lambda j, i, k: index_map(i, j, k))
        return pl.BlockSpec(block, index_map)

    row_axis = 1 if n_outer else 0

    def body_one_step(*refs):
        ex = refs[2:2 + n_extra]
        out = refs[2 + n_extra:2 + n_extra + n_out]
        epilogue(_bdot(refs[0][...], refs[1][...], ta, tb), ex, out, pl.program_id(row_axis))

    def body(*refs):
        a_ref, b_ref = refs[0], refs[1]
        ex = refs[2:2 + n_extra]
        out = refs[2 + n_extra:2 + n_extra + n_out]
        acc = refs[-1]
        i, k = pl.program_id(row_axis), pl.program_id(2)
        part = _bdot(a_ref[...], b_ref[...], ta, tb)

        @pl.when(k == 0)
        def _():
            acc[...] = part

        @pl.when(jnp.logical_and(k > 0, k < nk - 1))
        def _():
            acc[...] += part

        @pl.when(k == nk - 1)
        def _():
            epilogue(acc[...] + part, ex, out, i)

    sem = ("arbitrary",) * 3 if sequential else ("parallel", "parallel", "arbitrary")
    res = pl.pallas_call(
        body_one_step if nk == 1 else body, name=name,
        grid=(n // tn, m // tm, nk) if n_outer else (m // tm, n // tn, nk),
        in_specs=[spec(a_spec.block_shape, a_spec.index_map), spec(b_spec.block_shape, b_spec.index_map)]
        + [spec(bs, wrap(im)) for _, bs, im in extra],
        out_specs=[spec(bs, wrap(im)) for _, bs, im in outs],
        out_shape=[s for s, _, _ in outs],
        scratch_shapes=[] if nk == 1 else [pltpu.VMEM((tm, tn), F32)],
        compiler_params=_params(*sem),
    )(a, b, *[x for x, _, _ in extra])
    return res


def _tile(i, j):
    return (i, j)


def _plain(name, a, b, *, ta=False, tb=False, tm, tn, tk, out_dtype, b_blocks=None, out3=None, n_used=None,
           n_outer=False):
    m = a.shape[1] if ta else a.shape[0]
    if b_blocks:
        n = b.shape[1] if tb else b.shape[0] * b.shape[2]
    else:
        n = n_used or (b.shape[0] if tb else b.shape[1])

    def epi(acc, ex, out, i):
        out[0][...] = acc.astype(out_dtype)

    if out3:
        per = (n // out3) // tn
        spec = (jax.ShapeDtypeStruct((out3, m, n // out3), out_dtype), (None, tm, tn),
                lambda i, j: (j // per, i, j % per))
    else:
        spec = (jax.ShapeDtypeStruct((m, n), out_dtype), (tm, tn), _tile)
    return _matmul(name, a, b, ta=ta, tb=tb, tm=tm, tn=tn, tk=tk, outs=[spec], epilogue=epi,
                   b_blocks=b_blocks, n_used=n_used, n_outer=n_outer)[0]


def _ln_forward(name, a, b, res, gamma, beta, *, tm, tk, want_h=True):
    m, n = res.shape

    def epi(acc, ex, out, i):
        u = ALPHA * ex[0][...] + acc
        mu = jnp.mean(u, axis=-1, keepdims=True)
        xc = u - mu
        var = jnp.mean(xc * xc, axis=-1, keepdims=True)
        rstd = lax.rsqrt(var + LN_EPS)
        xhat = xc * rstd
        out[-2][...] = xhat
        out[-1][...] = rstd
        if want_h:
            h = xhat * ex[1][...] + ex[2][...]
            out[0][...] = h
            out[1][...] = h.astype(BF16)

    row = lambda i, j: (i, 0)
    vec = lambda i, j: (0, 0)
    outs = [(jax.ShapeDtypeStruct((m, n), F32), (tm, n), row), (jax.ShapeDtypeStruct((m, n), BF16), (tm, n), row),
            (jax.ShapeDtypeStruct((m, n), F32), (tm, n), row), (jax.ShapeDtypeStruct((m, 1), F32), (tm, 1), row)]
    return _matmul(
        name, a, b, tm=tm, tn=n, tk=tk,
        extra=[(res, (tm, n), row), (gamma, (1, n), vec), (beta, (1, n), vec)],
        outs=outs if want_h else outs[2:], epilogue=epi)


def _ln_backward_math(dy, xhat, rstd, gamma):
    dxhat = dy * gamma
    m1 = jnp.mean(dxhat, axis=-1, keepdims=True)
    m2 = jnp.mean(dxhat * xhat, axis=-1, keepdims=True)
    du = rstd * (dxhat - m1 - xhat * m2)
    return du, jnp.sum(dy * xhat, axis=0, keepdims=True), jnp.sum(dy, axis=0, keepdims=True)


def _ln_backward(name, a, b, dres, xhat, rstd, gamma, *, tm, tk, b_blocks=None, tb=True):
    m, n = dres.shape

    def epi(acc, ex, out, i):
        dy = acc + ALPHA * ex[0][...]
        du, dg, db = _ln_backward_math(dy, ex[1][...], ex[2][...], ex[3][...])
        out[0][...] = du
        out[1][...] = du.astype(BF16)
        first = i == 0

        @pl.when(first)
        def _():
            out[2][...] = dg
            out[3][...] = db

        @pl.when(jnp.logical_not(first))
        def _():
            out[2][...] += dg
            out[3][...] += db

    row = lambda i, j: (i, 0)
    vec = lambda i, j: (0, 0)
    return _matmul(
        name, a, b, tb=tb, tm=tm, tn=n, tk=tk, b_blocks=b_blocks, sequential=True,
        extra=[(dres, (tm, n), row), (xhat, (tm, n), row), (rstd, (tm, 1), row), (gamma, (1, n), vec)],
        outs=[(jax.ShapeDtypeStruct((m, n), F32), (tm, n), row),
              (jax.ShapeDtypeStruct((m, n), BF16), (tm, n), row),
              (jax.ShapeDtypeStruct((1, n), F32), (1, n), vec),
              (jax.ShapeDtypeStruct((1, n), F32), (1, n), vec)],
        epilogue=epi)


def _shift_down(x, k):
    row = lax.broadcasted_iota(jnp.int32, x.shape, 0)
    return jnp.where(row >= k, pltpu.roll(x, k, axis=0), 0.0)


def _shift_up(x, k):
    t = x.shape[0]
    row = lax.broadcasted_iota(jnp.int32, x.shape, 0)
    return jnp.where(row < t - k, pltpu.roll(x, t - k, axis=0), 0.0)


def _conv_silu_norm(x, w, normalise):
    kk = w.shape[0]
    c = x * w[kk - 1:kk, :]
    for j in range(kk - 1):
        c = c + _shift_down(x, kk - 1 - j) * w[j:j + 1, :]
    sg = _sigmoid(c)
    s = c * sg
    r = lax.rsqrt(jnp.sum(s * s, axis=-1, keepdims=True) + NORM_EPS)
    y = jnp.where(normalise, s * r, s)
    return c, sg, s, r, y


def _gdn_pre(proj, conv_w, heads):
    t = proj.shape[0]
    kk = conv_w.shape[0]

    def body(x_ref, w_ref, o_ref):
        normalise = pl.program_id(0) < 2
        o_ref[...] = _conv_silu_norm(x_ref[...], w_ref[...], normalise)[4]

    col = lambda s, h: (0, s * heads + h)
    return pl.pallas_call(
        body, name="gdn_pre", grid=(3, heads),
        in_specs=[pl.BlockSpec((t, HEAD_DIM), col), pl.BlockSpec((kk, HEAD_DIM), col)],
        out_specs=pl.BlockSpec((t, HEAD_DIM), col),
        out_shape=jax.ShapeDtypeStruct((t, 3 * heads * HEAD_DIM), F32),
        compiler_params=_params("parallel", "parallel"),
    )(proj, conv_w)


def _gdn_pre_backward(proj, conv_w, dqkv, heads):
    t = proj.shape[0]
    kk = conv_w.shape[0]

    def body(x_ref, w_ref, dy_ref, dx_ref, dw_ref):
        normalise = pl.program_id(0) < 2
        x = x_ref[...]
        w = w_ref[...]
        dy = dy_ref[...]
        c, sg, s, r, y = _conv_silu_norm(x, w, normalise)
        ds_norm = r * (dy - y * jnp.sum(dy * y, axis=-1, keepdims=True))
        ds = jnp.where(normalise, ds_norm, dy)
        dc = ds * (sg * (1.0 + c * (1.0 - sg)))
        dx = dc * w[kk - 1:kk, :]
        rows = [None] * kk
        rows[kk - 1] = jnp.sum(dc * x, axis=0, keepdims=True)
        for j in range(kk - 1):
            lag = kk - 1 - j
            dx = dx + _shift_up(dc, lag) * w[j:j + 1, :]
            rows[j] = jnp.sum(dc * _shift_down(x, lag), axis=0, keepdims=True)
        dx_ref[...] = dx.astype(BF16)
        dw_ref[...] = jnp.concatenate(rows, axis=0)

    col = lambda s, h: (0, s * heads + h)
    return pl.pallas_call(
        body, name="gdn_pre_bwd", grid=(3, heads),
        in_specs=[pl.BlockSpec((t, HEAD_DIM), col), pl.BlockSpec((kk, HEAD_DIM), col),
                  pl.BlockSpec((t, HEAD_DIM), col)],
        out_specs=[pl.BlockSpec((t, HEAD_DIM), col), pl.BlockSpec((kk, HEAD_DIM), col)],
        out_shape=[jax.ShapeDtypeStruct((t, 3 * heads * HEAD_DIM), BF16),
                   jax.ShapeDtypeStruct((kk, 3 * heads * HEAD_DIM), F32)],
        compiler_params=_params("parallel", "parallel"),
    )(proj, conv_w, dqkv)


def _gate_vectors(a_log, dt_bias, heads):
    pad = lambda v: jnp.pad(v.astype(F32), ((0, 0), (heads, HEAD_DIM - 2 * heads)))
    return pad(jnp.exp(a_log.astype(F32))), pad(dt_bias)


def _softplus(x):
    return jnp.maximum(x, 0.0) + jnp.log(1.0 + jnp.exp(-jnp.abs(x)))


def _gates_epilogue(heads):
    def epi(acc, ex, out, i):
        lane = lax.broadcasted_iota(jnp.int32, acc.shape, 1)
        beta = _sigmoid(acc)
        g = -ex[0][...] * _softplus(acc + ex[1][...])
        out[0][...] = acc
        out[1][...] = jnp.where(lane < heads, beta, jnp.where(lane < 2 * heads, g, 0.0))
    return epi


def _gates_backward(ba, bg, dbg, ea, dtb, heads):
    t = ba.shape[0]

    def body(ba_ref, bg_ref, d_ref, ea_ref, dt_ref, dba_ref, dal_ref, ddt_ref):
        lane = lax.broadcasted_iota(jnp.int32, (t, HEAD_DIM), 1)
        bgv = bg_ref[...]
        d = d_ref[...]
        db = d * bgv * (1.0 - bgv)
        da = -d * ea_ref[...] * _sigmoid(ba_ref[...] + dt_ref[...])
        is_g = jnp.logical_and(lane >= heads, lane < 2 * heads)
        dba = jnp.where(lane < heads, db, jnp.where(is_g, da, 0.0))
        dba_ref[...] = dba.astype(BF16)
        dal_ref[...] = jnp.sum(jnp.where(is_g, d * bgv, 0.0), axis=0, keepdims=True)
        ddt_ref[...] = jnp.sum(jnp.where(is_g, da, 0.0), axis=0, keepdims=True)

    full = pl.BlockSpec((t, HEAD_DIM), lambda: (0, 0))
    vec = pl.BlockSpec((1, HEAD_DIM), lambda: (0, 0))
    return pl.pallas_call(
        body, name="gates_bwd", grid=(),
        in_specs=[full, full, full, vec, vec], out_specs=[full, vec, vec],
        out_shape=[jax.ShapeDtypeStruct((t, HEAD_DIM), BF16), jax.ShapeDtypeStruct((1, HEAD_DIM), F32),
                   jax.ShapeDtypeStruct((1, HEAD_DIM), F32)],
        compiler_params=pltpu.CompilerParams(vmem_limit_bytes=VMEM_LIMIT),
    )(ba, bg, dbg, ea, dtb)


class _Chunk:
    pass


def _split2(x):
    hi = x.astype(BF16)
    return hi, (x - hi.astype(F32)).astype(BF16)


def _split3(x):
    hi = x.astype(BF16)
    rest = x - hi.astype(F32)
    mid = rest.astype(BF16)
    return hi, mid, (rest - mid.astype(F32)).astype(BF16)


def _dot_mask(mask, x, ta=False):
    hi, mid, lo = _split3(x)
    return _bdot(mask, hi, ta=ta) + (_bdot(mask, mid, ta=ta) + _bdot(mask, lo, ta=ta))


def _transpose_by_identity(x):
    r = x.shape[0]
    eye = (lax.broadcasted_iota(jnp.int32, (r, r), 0) == lax.broadcasted_iota(jnp.int32, (r, r), 1)).astype(BF16)
    hi, mid, lo = _split3(x)
    return _bdot(hi, eye, ta=True) + (_bdot(mid, eye, ta=True) + _bdot(lo, eye, ta=True))


def _dot22(a, b, ta=False, tb=False):
    ah, al = _split2(a)
    bh, bl = _split2(b)
    return _bdot(ah, bh, ta, tb) + (_bdot(ah, bl, ta, tb) + _bdot(al, bh, ta, tb))


def _chunk_gates(bg, heads):
    n = CHUNK
    row = lax.broadcasted_iota(jnp.int32, (n, n), 0)
    col = lax.broadcasted_iota(jnp.int32, (n, n), 1)
    lane = lax.broadcasted_iota(jnp.int32, bg.shape, 1)
    graw = jnp.where(jnp.logical_and(lane >= heads, lane < 2 * heads), bg, 0.0)
    gc = _dot_mask((row >= col).astype(BF16), graw)
    return gc, _transpose_by_identity(gc)


def _in_lockstep(generators):
    results = [None] * len(generators)
    live = list(enumerate(generators))
    while live:
        still = []
        for i, gen in live:
            try:
                next(gen)
                still.append((i, gen))
            except StopIteration as stop:
                results[i] = stop.value
        live = still
    return results


def _chunk_local(q, k, v, beta, gc, grow, solved=None):
    c = _Chunk()
    n = CHUNK
    row = lax.broadcasted_iota(jnp.int32, (n, n), 0)
    col = lax.broadcasted_iota(jnp.int32, (n, n), 1)
    c.tri = row >= col
    c.strict = row > col
    eye = row == col
    c.gcb = jnp.broadcast_to(gc, (n, HEAD_DIM))
    c.decay = jnp.where(c.tri, jnp.exp(jnp.where(c.tri, gc - grow, 0.0)), 0.0)
    c.eg = jnp.exp(c.gcb)
    glast = c.gcb[n - 1:n, :]
    c.egl = jnp.exp(glast)
    c.ekl = jnp.exp(glast - c.gcb)
    c.beta = beta
    c.q = q * (HEAD_DIM ** -0.5)
    c.k = k
    c.v = v
    c.kb = k * beta
    c.vb = v * beta
    c.kg = c.kb * c.eg
    both = _bdot(jnp.concatenate([c.kb, c.q], axis=0), k, tb=True)
    yield
    c.L = jnp.where(c.strict, both[:n] * c.decay, 0.0)
    c.A = jnp.where(c.tri, both[n:] * c.decay, 0.0)
    if solved is None:
        x = -c.L
        tinv = eye.astype(F32) + x
        p = _dot22(x, x)
        yield
        for _ in range(int(math.log2(n)) - 2):
            both = _dot22(jnp.concatenate([p, tinv], axis=0), p)
            yield
            p, tinv = both[:n], tinv + both[n:]
        c.T = tinv + _dot22(tinv, p)
        yield
        uw = _dot22(c.T, jnp.concatenate([c.vb, c.kg], axis=1))
        yield
        c.u, c.w = uw[:, :HEAD_DIM], uw[:, HEAD_DIM:]
    else:
        c.T, c.u, c.w = solved
    c.qg = c.q * c.eg
    c.kdec = k * c.ekl
    return c


def _gdn_core(qkv, bg, heads):
    t = qkv.shape[0]
    nchunk = t // CHUNK

    gw = heads * HEAD_DIM

    def body(qkv_ref, bg_ref, o_ref, s_ref, t_ref, u_ref, w_ref, state):
        @pl.when(pl.program_id(0) == 0)
        def _():
            state[...] = jnp.zeros_like(state)

        bg_v = bg_ref[...]
        gc_all, gc_rows = _chunk_gates(bg_v, heads)
        def one_head(h):
            col = lambda s: pl.ds(s * gw + h * HEAD_DIM, HEAD_DIM)
            c = yield from _chunk_local(qkv_ref[:, col(0)], qkv_ref[:, col(1)], qkv_ref[:, col(2)], bg_v[:, h:h + 1],
                                        gc_all[:, heads + h:heads + h + 1], gc_rows[heads + h:heads + h + 1, :])
            s0 = state[h]
            v_new = c.u - _bdot(c.w, s0)
            yield
            o = _bdot(c.qg, s0) + _bdot(c.A, v_new)
            return s0, o, s0 * c.egl + _bdot(c.kdec, v_new, ta=True), c

        results = _in_lockstep([one_head(h) for h in range(heads)])
        for h, (s0, o, s1, c) in enumerate(results):
            lanes = pl.ds(h * HEAD_DIM, HEAD_DIM)
            s_ref[h, 0] = s0
            o_ref[:, lanes] = o
            t_ref[:, lanes] = jnp.concatenate([c.T, jnp.zeros((CHUNK, HEAD_DIM - CHUNK), F32)], axis=1)
            u_ref[:, lanes] = c.u
            w_ref[:, lanes] = c.w
            state[h] = s1

    return pl.pallas_call(
        body, name="gdn_core", grid=(nchunk,),
        in_specs=[pl.BlockSpec((CHUNK, 3 * gw), lambda n: (n, 0)), pl.BlockSpec((CHUNK, HEAD_DIM), lambda n: (n, 0))],
        out_specs=[pl.BlockSpec((CHUNK, gw), lambda n: (n, 0)),
                   pl.BlockSpec((heads, 1, HEAD_DIM, HEAD_DIM), lambda n: (0, n, 0, 0))]
        + [pl.BlockSpec((CHUNK, gw), lambda n: (n, 0))] * 3,
        out_shape=[jax.ShapeDtypeStruct((t, gw), F32),
                   jax.ShapeDtypeStruct((heads, nchunk, HEAD_DIM, HEAD_DIM), F32)]
        + [jax.ShapeDtypeStruct((t, gw), F32)] * 3,
        scratch_shapes=[pltpu.VMEM((heads, HEAD_DIM, HEAD_DIM), F32)],
        compiler_params=_params("arbitrary"),
    )(qkv, bg)


def _gdn_core_backward(qkv, bg, states, solved, do, heads):
    t = qkv.shape[0]
    nchunk = t // CHUNK
    n = CHUNK

    def one_head(chunk_local, s0, d_out, ds1):
        c = yield from chunk_local
        v_new = c.u - _bdot(c.w, s0)
        dqg = _bdot(d_out, s0, tb=True)
        ds0 = _bdot(c.qg, d_out, ta=True) + ds1 * c.egl
        dv_new = _bdot(c.A, d_out, ta=True) + _bdot(c.kdec, ds1)
        yield
        dA = jnp.where(c.tri, _bdot(d_out, v_new, tb=True), 0.0)
        dkdec = _bdot(v_new, ds1, tb=True)
        dgl = jnp.sum(jnp.sum(ds1 * s0, axis=1, keepdims=True), axis=0, keepdims=True) * c.egl
        dw = -_bdot(dv_new, s0, tb=True)
        ds0 = ds0 - _bdot(c.w, dv_new, ta=True)
        yield
        both = _dot22(c.T, jnp.concatenate([dv_new, dw], axis=1), ta=True)
        yield
        dvb, dkg = both[:, :HEAD_DIM], both[:, HEAD_DIM:]
        dL = jnp.where(c.strict, -(_bdot(dvb, c.u, tb=True) + _bdot(dkg, c.w, tb=True)), 0.0)
        yield
        dm1 = dL * c.decay
        dkb = _bdot(dm1, c.k) + dkg * c.eg
        dk = _bdot(dm1, c.kb, ta=True)
        dm2 = dA * c.decay
        dq = _bdot(dm2, c.k) + dqg * c.eg
        dk = dk + _bdot(dm2, c.q, ta=True) + dkdec * c.ekl + dkb * c.beta
        pm = dL * c.L + dA * c.A
        ones = jnp.ones((n, HEAD_DIM), BF16)
        pm_hi, pm_lo = _split2(pm)
        colsum = _bdot(pm_hi, ones, ta=True) + _bdot(pm_lo, ones, ta=True)
        tk_ = jnp.sum(dkdec * c.kdec, axis=1, keepdims=True)
        dgc = (jnp.sum(pm, axis=1, keepdims=True) - colsum
               + jnp.sum(dqg * c.qg, axis=1, keepdims=True)
               - tk_
               + jnp.sum(dkg * c.kg, axis=1, keepdims=True))
        dgl = dgl + jnp.sum(tk_, axis=0, keepdims=True)
        rowi = lax.broadcasted_iota(jnp.int32, (n, HEAD_DIM), 0)
        dgc = dgc + jnp.where(rowi == n - 1, dgl, 0.0)
        dbeta = jnp.sum(dkb * c.k, axis=1, keepdims=True) + jnp.sum(dvb * c.v, axis=1, keepdims=True)
        return dq * (HEAD_DIM ** -0.5), dk, dvb * c.beta, dbeta, dgc, ds0

    gw = heads * HEAD_DIM

    def body(qkv_ref, bg_ref, s_ref, t_ref, u_ref, w_ref, do_ref, dqkv_ref, dbg_ref, dstate):
        @pl.when(pl.program_id(0) == 0)
        def _():
            dstate[...] = jnp.zeros_like(dstate)

        bg_v = bg_ref[...]
        gc_all, gc_rows = _chunk_gates(bg_v, heads)
        lane = lax.broadcasted_iota(jnp.int32, (n, HEAD_DIM), 1)
        dgates = jnp.zeros((n, HEAD_DIM), F32)
        chains = []
        for h in range(heads):
            col = lambda s: pl.ds(s * gw + h * HEAD_DIM, HEAD_DIM)
            lanes = pl.ds(h * HEAD_DIM, HEAD_DIM)
            c = _chunk_local(qkv_ref[:, col(0)], qkv_ref[:, col(1)], qkv_ref[:, col(2)], bg_v[:, h:h + 1],
                             gc_all[:, heads + h:heads + h + 1], gc_rows[heads + h:heads + h + 1, :],
                             (t_ref[:, pl.ds(h * HEAD_DIM, CHUNK)], u_ref[:, lanes], w_ref[:, lanes]))
            chains.append(one_head(c, s_ref[h, 0], do_ref[:, pl.ds(h * HEAD_DIM, HEAD_DIM)], dstate[h]))
        results = _in_lockstep(chains)
        for h, (dq, dk, dv, dbeta, dgc, ds0) in enumerate(results):
            dgates = jnp.where(lane == h, dbeta, jnp.where(lane == heads + h, dgc, dgates))
        for h, (dq, dk, dv, dbeta, dgc, ds0) in enumerate(results):
            dqkv_ref[:, pl.ds(h * HEAD_DIM, HEAD_DIM)] = dq
            dqkv_ref[:, pl.ds(gw + h * HEAD_DIM, HEAD_DIM)] = dk
            dqkv_ref[:, pl.ds(2 * gw + h * HEAD_DIM, HEAD_DIM)] = dv
            dstate[h] = ds0
        row = lax.broadcasted_iota(jnp.int32, (n, n), 0)
        colm = lax.broadcasted_iota(jnp.int32, (n, n), 1)
        draw = _dot_mask((row >= colm).astype(BF16), dgates, ta=True)
        dbg_ref[...] = jnp.where(lane < heads, dgates, draw)

    last = nchunk - 1
    return pl.pallas_call(
        body, name="gdn_core_bwd", grid=(nchunk,),
        in_specs=[pl.BlockSpec((CHUNK, 3 * gw), lambda i: (last - i, 0)),
                  pl.BlockSpec((CHUNK, HEAD_DIM), lambda i: (last - i, 0)),
                  pl.BlockSpec((heads, 1, HEAD_DIM, HEAD_DIM), lambda i: (0, last - i, 0, 0))]
        + [pl.BlockSpec((CHUNK, gw), lambda i: (last - i, 0))] * 4,
        out_specs=[pl.BlockSpec((CHUNK, 3 * gw), lambda i: (last - i, 0)),
                   pl.BlockSpec((CHUNK, HEAD_DIM), lambda i: (last - i, 0))],
        out_shape=[jax.ShapeDtypeStruct((t, 3 * gw), F32), jax.ShapeDtypeStruct((t, HEAD_DIM), F32)],
        scratch_shapes=[pltpu.VMEM((heads, HEAD_DIM, HEAD_DIM), F32)],
        compiler_params=_params("arbitrary"),
    )(qkv, bg, states, *solved, do)


def _gdn_post(o, proj, z_col0, norm_w, heads, tt):
    t = o.shape[0]
    zb = z_col0 // HEAD_DIM

    def body(o_ref, z_ref, w_ref, out_ref):
        ov = o_ref[...]
        z = z_ref[...]
        rms = lax.rsqrt(jnp.mean(ov * ov, axis=-1, keepdims=True) + NORM_EPS)
        out_ref[...] = (ov * rms * w_ref[...] * (z * _sigmoid(z))).astype(BF16)

    return pl.pallas_call(
        body, name="gdn_post", grid=(t // tt, heads),
        in_specs=[pl.BlockSpec((tt, HEAD_DIM), lambda i, h: (i, h)),
                  pl.BlockSpec((tt, HEAD_DIM), lambda i, h: (i, zb + h)),
                  pl.BlockSpec((1, HEAD_DIM), lambda i, h: (0, 0))],
        out_specs=pl.BlockSpec((tt, HEAD_DIM), lambda i, h: (i, h)),
        out_shape=jax.ShapeDtypeStruct((t, heads * HEAD_DIM), BF16),
        compiler_params=_params("parallel", "parallel"),
    )(o, proj, norm_w)


def _gdn_post_backward(dcat, o, proj, z_col0, norm_w, heads, tt):
    t = o.shape[0]
    zb = z_col0 // HEAD_DIM

    def body(d_ref, o_ref, z_ref, w_ref, do_ref, dz_ref, dw_ref):
        d = d_ref[...]
        ov = o_ref[...]
        z = z_ref[...]
        w = w_ref[...]
        rms = lax.rsqrt(jnp.mean(ov * ov, axis=-1, keepdims=True) + NORM_EPS)
        ohat = ov * rms
        sg = _sigmoid(z)
        gate = z * sg
        dz_ref[...] = (d * ohat * w * (sg * (1.0 + z * (1.0 - sg)))).astype(BF16)
        don = d * gate
        dohat = don * w
        do_ref[...] = rms * (dohat - ohat * jnp.mean(dohat * ohat, axis=-1, keepdims=True))
        dw = jnp.sum(don * ohat, axis=0, keepdims=True)
        first = jnp.logical_and(pl.program_id(0) == 0, pl.program_id(1) == 0)

        @pl.when(first)
        def _():
            dw_ref[...] = dw

        @pl.when(jnp.logical_not(first))
        def _():
            dw_ref[...] += dw

    blk = pl.BlockSpec((tt, HEAD_DIM), lambda i, h: (i, h))
    return pl.pallas_call(
        body, name="gdn_post_bwd", grid=(t // tt, heads),
        in_specs=[blk, blk, pl.BlockSpec((tt, HEAD_DIM), lambda i, h: (i, zb + h)),
                  pl.BlockSpec((1, HEAD_DIM), lambda i, h: (0, 0))],
        out_specs=[blk, blk, pl.BlockSpec((1, HEAD_DIM), lambda i, h: (0, 0))],
        out_shape=[jax.ShapeDtypeStruct((t, heads * HEAD_DIM), F32),
                   jax.ShapeDtypeStruct((t, heads * HEAD_DIM), BF16),
                   jax.ShapeDtypeStruct((1, HEAD_DIM), F32)],
        compiler_params=_params("arbitrary", "arbitrary"),
    )(dcat, o, proj, norm_w)


def _pool_select(levels, group):
    out = levels[-1]
    for gi in range(len(levels) - 2, -1, -1):
        out = jnp.where(group == gi, levels[gi], out)
    return out


def _pool_counts(t, width, group):
    pos = lax.broadcasted_iota(jnp.int32, (t, width), 0)
    win = jnp.left_shift(2, group)
    return jnp.minimum(pos + 1, win).astype(F32)


def _pooled(p, group):
    levels, s, step = [], p, 1
    for _ in POOL_WINDOWS:
        s = s + _shift_down(s, step)
        levels.append(s)
        step *= 2
    cnt = _pool_counts(p.shape[0], p.shape[1], group)
    return _pool_select(levels, group) / cnt - p, cnt


def _pool_forward(proj, p_col0, pool_w, pool_scale):
    t = proj.shape[0]
    groups, cg, _ = pool_w.shape
    pb = p_col0 // cg

    def body(p_ref, w_ref, s_ref, o_ref):
        pooled, _ = _pooled(p_ref[...], pl.program_id(0))
        o_ref[...] = (_bdot(pooled, w_ref[0]) * s_ref[...]).astype(BF16)

    return pl.pallas_call(
        body, name="pool_fwd", grid=(groups,),
        in_specs=[pl.BlockSpec((t, cg), lambda g: (0, pb + g)), pl.BlockSpec((1, cg, cg), lambda g: (g, 0, 0)),
                  pl.BlockSpec((1, cg), lambda g: (0, g))],
        out_specs=pl.BlockSpec((t, cg), lambda g: (0, g)),
        out_shape=jax.ShapeDtypeStruct((t, groups * cg), BF16),
        compiler_params=_params("parallel"),
    )(proj, pool_w, pool_scale)


def _pool_backward(dcat, d_col0, proj, p_col0, pool_w, pool_scale):
    t = proj.shape[0]
    groups, cg, _ = pool_w.shape
    pb = p_col0 // cg
    db = d_col0 // cg

    def body(d_ref, p_ref, w_ref, s_ref, dp_ref, dw_ref, ds_ref):
        group = pl.program_id(0)
        pooled, cnt = _pooled(p_ref[...], group)
        w = w_ref[0]
        d = d_ref[...]
        mixed = _bdot(pooled, w)
        ds_ref[...] = jnp.sum(d * mixed, axis=0, keepdims=True)
        dmixed = d * s_ref[...]
        dw_ref[0] = _bdot(pooled, dmixed, ta=True)
        dpooled = _bdot(dmixed, w, tb=True)
        levels, s, step = [], dpooled / cnt, 1
        for _ in POOL_WINDOWS:
            s = s + _shift_up(s, step)
            levels.append(s)
            step *= 2
        dp_ref[...] = (_pool_select(levels, group) - dpooled).astype(BF16)

    return pl.pallas_call(
        body, name="pool_bwd", grid=(groups,),
        in_specs=[pl.BlockSpec((t, cg), lambda g: (0, db + g)), pl.BlockSpec((t, cg), lambda g: (0, pb + g)),
                  pl.BlockSpec((1, cg, cg), lambda g: (g, 0, 0)), pl.BlockSpec((1, cg), lambda g: (0, g))],
        out_specs=[pl.BlockSpec((t, cg), lambda g: (0, g)), pl.BlockSpec((1, cg, cg), lambda g: (g, 0, 0)),
                   pl.BlockSpec((1, cg), lambda g: (0, g))],
        out_shape=[jax.ShapeDtypeStruct((t, groups * cg), BF16), jax.ShapeDtypeStruct((groups, cg, cg), F32),
                   jax.ShapeDtypeStruct((1, groups * cg), F32)],
        compiler_params=_params("parallel"),
    )(dcat, proj, pool_w, pool_scale)


def _attention(q, k, v, tq):
    t, d = q.shape
    m = k.shape[0]
    dh = d // XATTN_HEADS
    scale = dh ** -0.5

    def body(q_ref, k_ref, v_ref, o_ref):
        s = _bdot(q_ref[...], k_ref[...], tb=True) * scale
        s = s - jnp.max(s, axis=-1, keepdims=True)
        e = jnp.exp(s)
        p = e / jnp.sum(e, axis=-1, keepdims=True)
        o_ref[...] = _bdot(p, v_ref[...]).astype(BF16)

    return pl.pallas_call(
        body, name="xattn_fwd", grid=(XATTN_HEADS, t // tq),
        in_specs=[pl.BlockSpec((tq, dh), lambda h, i: (i, h)), pl.BlockSpec((m, dh), lambda h, i: (0, h)),
                  pl.BlockSpec((m, dh), lambda h, i: (0, h))],
        out_specs=pl.BlockSpec((tq, dh), lambda h, i: (i, h)),
        out_shape=jax.ShapeDtypeStruct((t, d), BF16),
        compiler_params=_params("parallel", "parallel"),
    )(q, k, v)


def _attention_backward(q, k, v, do, tq):
    t, d = q.shape
    m = k.shape[0]
    dh = d // XATTN_HEADS
    scale = dh ** -0.5

    def body(q_ref, k_ref, v_ref, do_ref, dq_ref, dk_ref, dv_ref, dk_acc, dv_acc):
        i = pl.program_id(1)
        qv, kv, vv, dov = q_ref[...], k_ref[...], v_ref[...], do_ref[...]
        s = _bdot(qv, kv, tb=True) * scale
        s = s - jnp.max(s, axis=-1, keepdims=True)
        e = jnp.exp(s)
        p = e / jnp.sum(e, axis=-1, keepdims=True)
        dp = _bdot(dov, vv, tb=True)
        ds = p * (dp - jnp.sum(dp * p, axis=-1, keepdims=True)) * scale
        dq_ref[...] = _bdot(ds, kv).astype(BF16)
        dv_part = _bdot(p, dov, ta=True)
        dk_part = _bdot(ds, qv, ta=True)

        @pl.when(i == 0)
        def _():
            dk_acc[...] = dk_part
            dv_acc[...] = dv_part

        @pl.when(i > 0)
        def _():
            dk_acc[...] += dk_part
            dv_acc[...] += dv_part

        @pl.when(i == pl.num_programs(1) - 1)
        def _():
            dk_ref[...] = dk_acc[...].astype(BF16)
            dv_ref[...] = dv_acc[...].astype(BF16)

    qblk = pl.BlockSpec((tq, dh), lambda h, i: (i, h))
    kblk = pl.BlockSpec((m, dh), lambda h, i: (0, h))
    return pl.pallas_call(
        body, name="xattn_bwd", grid=(XATTN_HEADS, t // tq),
        in_specs=[qblk, kblk, kblk, qblk],
        out_specs=[qblk, kblk, kblk],
        out_shape=[jax.ShapeDtypeStruct((t, d), BF16), jax.ShapeDtypeStruct((m, d), BF16),
                   jax.ShapeDtypeStruct((m, d), BF16)],
        scratch_shapes=[pltpu.VMEM((m, dh), F32), pltpu.VMEM((m, dh), F32)],
        compiler_params=_params("parallel", "arbitrary"),
    )(q, k, v, do)


def _ln_backward_rows(name, dmain, dres, xhat, rstd, gamma, tm):
    t, d = xhat.shape

    def body(m_ref, r_ref, x_ref, s_ref, g_ref, du_ref, dub_ref, dg_ref, db_ref):
        du, dg, db = _ln_backward_math(m_ref[...] + ALPHA * r_ref[...], x_ref[...], s_ref[...], g_ref[...])
        du_ref[...] = du
        dub_ref[...] = du.astype(BF16)
        first = pl.program_id(0) == 0

        @pl.when(first)
        def _():
            dg_ref[...] = dg
            db_ref[...] = db

        @pl.when(jnp.logical_not(first))
        def _():
            dg_ref[...] += dg
            db_ref[...] += db

    row = pl.BlockSpec((tm, d), lambda i: (i, 0))
    vec = pl.BlockSpec((1, d), lambda i: (0, 0))
    return pl.pallas_call(
        body, name=name, grid=(t // tm,),
        in_specs=[row, row, row, pl.BlockSpec((tm, 1), lambda i: (i, 0)), vec],
        out_specs=[row, row, vec, vec],
        out_shape=[jax.ShapeDtypeStruct((t, d), F32), jax.ShapeDtypeStruct((t, d), BF16),
                   jax.ShapeDtypeStruct((1, d), F32), jax.ShapeDtypeStruct((1, d), F32)],
        compiler_params=_params("arbitrary"),
    )(dmain, dres, xhat, rstd, gamma)


def _loss_and_ln_backward(xhat, rstd, gamma, beta, target, tm):
    t, d = xhat.shape

    def body(x_ref, r_ref, g_ref, b_ref, t_ref, du_ref, dub_ref, dg_ref, db_ref, loss_ref):
        xh = x_ref[...]
        g = g_ref[...]
        diff = xh * g + b_ref[...] - t_ref[...]
        part = jnp.sum(jnp.sum(diff * diff, axis=1, keepdims=True), axis=0, keepdims=True) * (0.5 / d)
        dy = diff * (1.0 / d)
        du, dg, db = _ln_backward_math(dy, xh, r_ref[...], g)
        du_ref[...] = du
        dub_ref[...] = du.astype(BF16)
        lossrow = jnp.broadcast_to(part, (1, HEAD_DIM))
        first = pl.program_id(0) == 0

        @pl.when(first)
        def _():
            dg_ref[...] = dg
            db_ref[...] = db
            loss_ref[...] = lossrow

        @pl.when(jnp.logical_not(first))
        def _():
            dg_ref[...] += dg
            db_ref[...] += db
            loss_ref[...] += lossrow

    row = pl.BlockSpec((tm, d), lambda i: (i, 0))
    vec = pl.BlockSpec((1, d), lambda i: (0, 0))
    return pl.pallas_call(
        body, name="loss_ln3_bwd", grid=(t // tm,),
        in_specs=[row, pl.BlockSpec((tm, 1), lambda i: (i, 0)), vec, vec, row],
        out_specs=[row, row, vec, vec, pl.BlockSpec((1, HEAD_DIM), lambda i: (0, 0))],
        out_shape=[jax.ShapeDtypeStruct((t, d), F32), jax.ShapeDtypeStruct((t, d), BF16),
                   jax.ShapeDtypeStruct((1, d), F32), jax.ShapeDtypeStruct((1, d), F32),
                   jax.ShapeDtypeStruct((1, HEAD_DIM), F32)],
        compiler_params=_params("arbitrary"),
    )(xhat, rstd, gamma, beta, target)


def _after(token, a):
    return a if token is None else a + token[:1, :1].astype(a.dtype)


def _pick(n, prefs):
    for p in prefs:
        if n % p == 0:
            return p
    return n


def _local_step(x, mem, target, w, token=None):
    t, d = x.shape
    heads = w["a_log"].shape[1]
    gw = heads * HEAD_DIM
    groups, cg, _ = w["pool_w"].shape
    pw = groups * cg
    n_main = 4 * gw + pw
    in_cols = n_main + 2 * heads
    s_in = w["w_in_t"].shape[0]

    tm = _pick(t, (512, 256, 128))
    tm_ln = _pick(t, (256, 128))
    tm_big = _pick(t, (1024, 512, 256, 128))
    tk = _pick(d, K_STEPS)

    w_in_t = w["w_in_t"].reshape(in_cols, d)
    w_p_t = w_in_t[4 * gw + 2 * heads:]
    w_ba_t = jnp.pad(w_in_t[4 * gw:4 * gw + 2 * heads], ((0, HEAD_DIM - 2 * heads), (0, 0)))
    x_bf = _after(token, x).astype(BF16)
    mem_bf = _after(token, mem).astype(BF16)

    tn_d = _pick(d, (1024, 512, 256, 128))
    proj = _plain("proj_main", x_bf, w_in_t, tb=True, n_used=4 * gw, tm=tm_big, tn=_pick(4 * gw, (1024, 512, 256, 128)),
                  tk=tk, out_dtype=F32)
    pproj = _plain("proj_pool", x_bf, w_p_t, tb=True, tm=tm_big, tn=_pick(pw, (1024, 512, 256, 128)), tk=tk, out_dtype=F32)
    ea, dtb = _gate_vectors(w["a_log"], w["dt_bias"], heads)
    vec128 = lambda i, j: (0, 0)
    ba, bg = _matmul(
        "proj_gates", x_bf, w_ba_t, tb=True, tm=tm, tn=HEAD_DIM, tk=tk,
        extra=[(ea, (1, HEAD_DIM), vec128), (dtb, (1, HEAD_DIM), vec128)],
        outs=[(jax.ShapeDtypeStruct((t, HEAD_DIM), F32), (tm, HEAD_DIM), _tile)] * 2,
        epilogue=_gates_epilogue(heads))
    qkv = _gdn_pre(proj, w["conv_w"], heads)
    o_gdn, states, *solved = _gdn_core(qkv, bg, heads)
    cat_g = _gdn_post(o_gdn, proj, 3 * gw, w["gdn_norm_w"], heads, tm)
    cat_p = _pool_forward(pproj, 0, w["pool_w"], w["pool_scale"])
    cat = jnp.concatenate([cat_g, cat_p], axis=1)
    w = {**w, **(yield ("weights", 1, cat))}
    h1, h1_bf, xhat1, rstd1 = _ln_forward("mix_ln1", cat, w["w_out"], x, w["ln1_g"], w["ln1_b"], tm=tm_ln, tk=tk)

    h1_bf = _after((yield ("relay", None, h1_bf)), h1_bf)
    q = _plain("xattn_q", h1_bf, w["xq_w"], tm=tm, tn=tn_d, tk=tk, out_dtype=BF16)
    mlen = mem.shape[0]
    tm_mem = _pick(mlen, (256, 128))
    k = _plain("xattn_k", mem_bf, w["xk_w"], tm=tm_mem, tn=tn_d, tk=tk, out_dtype=BF16)
    v = _plain("xattn_v", mem_bf, w["xv_w"], tm=tm_mem, tn=tn_d, tk=tk, out_dtype=BF16)
    att = _attention(q, k, v, tm)
    h2, h2_bf, xhat2, rstd2 = _ln_forward("xo_ln2", att, w["xo_w"], h1, w["ln2_g"], w["ln2_b"], tm=tm_ln, tk=tk)

    w = {**w, **(yield ("weights", 2, h2_bf))}
    s_up = w["w_up3"].shape[0]
    ff = s_up * w["w_up3"].shape[2]
    tn_f = _pick(ff // s_up, (1024, 512, 256, 128))

    def up_epi(acc, ex, out, i):
        r = jnp.maximum(acc, 0.0)
        out[0][...] = (r * r).astype(BF16)
        out[1][...] = (2.0 * r).astype(BF16)

    act, act_grad = _matmul(
        "mlp_up", h2_bf, w["w_up3"], b_blocks=s_up, tm=tm_big, tn=tn_f, tk=tk,
        outs=[(jax.ShapeDtypeStruct((t, ff), BF16), (tm_big, tn_f), _tile)] * 2, epilogue=up_epi)
    w = {**w, **(yield ("weights", 3, act))}
    tk_f = _pick(ff, K_STEPS)
    xhat3, rstd3 = _ln_forward("down_ln3", act, w["w_down"], h2, w["ln3_g"], w["ln3_b"], tm=tm, tk=tk_f, want_h=False)

    grads = {}
    du3, du3_bf, grads["ln3_g"], grads["ln3_b"], loss = _loss_and_ln_backward(
        xhat3, rstd3, w["ln3_g"], w["ln3_b"], target, tm_ln)

    def dup_epi(acc, ex, out, i):
        out[0][...] = (acc * ex[0][...].astype(F32)).astype(BF16)

    dup = _matmul(
        "mlp_down_dx", du3_bf, w["w_down"], tb=True, tm=tm_big, tn=tn_f, tk=tk,
        extra=[(act_grad, (tm_big, tn_f), _tile)],
        outs=[(jax.ShapeDtypeStruct((t, ff), BF16), (tm_big, tn_f), _tile)], epilogue=dup_epi)[0]
    tk_t = _pick(t, K_STEPS)
    tm_w = _pick(d, (512, 256, 128))
    grads["w_down"] = _plain("mlp_down_dw", act, du3_bf, ta=True, tm=_pick(ff, (512, 256, 128)), tn=d, tk=tk_t,
                             out_dtype=F32)
    grads["w_up3"] = _plain("mlp_up_dw", h2_bf, dup, ta=True, tm=tm_w, tn=ff // s_up, tk=tk_t, out_dtype=F32, out3=s_up,
                            n_outer=True)
    token = yield ("grads", 0, {n: grads.pop(n) for n in ("w_down", "w_up3")})
    dh2 = _plain("mlp_up_dx", dup, w["w_up3"], tb=True, b_blocks=s_up, tm=tm_big, tn=tn_d,
                 tk=_pick(ff // s_up, K_STEPS), out_dtype=F32)
    du2, du2_bf, grads["ln2_g"], grads["ln2_b"] = _ln_backward_rows(
        "ln2_bwd", dh2, du3, xhat2, rstd2, _after(token, w["ln2_g"]), tm_ln)
    token = yield ("poll", 0, du2_bf)

    grads["xo_w"] = _plain("xo_dw", att, du2_bf, ta=True, tm=tm_w, tn=d, tk=tk_t, out_dtype=F32)
    datt = _plain("xo_dx", du2_bf, w["xo_w"], tb=True, tm=tm, tn=tn_d, tk=tk, out_dtype=BF16)
    dq, dk, dv = _attention_backward(q, k, v, datt, tm)
    tk_m = _pick(mlen, (256, 128))
    grads["xq_w"] = _plain("xq_dw", h1_bf, dq, ta=True, tm=tm_w, tn=d, tk=tk_t, out_dtype=F32)
    grads["xk_w"] = _plain("xk_dw", mem_bf, dk, ta=True, tm=tm_w, tn=tn_d, tk=tk_m, out_dtype=F32)
    grads["xv_w"] = _plain("xv_dw", mem_bf, dv, ta=True, tm=tm_w, tn=tn_d, tk=tk_m, out_dtype=F32)
    du1, du1_bf, grads["ln1_g"], grads["ln1_b"] = _ln_backward(
        "xq_dx_ln1", dq, w["xq_w"], du2, xhat1, rstd1, _after(token, w["ln1_g"]), tm=tm_ln, tk=tk)

    grads["w_out"] = _plain("out_dw", cat, du1_bf, ta=True, tm=tm_w, tn=d, tk=tk_t, out_dtype=F32)
    token = yield ("grads", 1, {n: grads.pop(n) for n in ("xo_w", "xq_w", "xk_w", "xv_w", "w_out")})
    dcat = _plain("out_dx", du1_bf, w["w_out"], tb=True, tm=tm, tn=tn_d, tk=tk, out_dtype=F32)
    dp, grads["pool_w"], grads["pool_scale"] = _pool_backward(dcat, gw, pproj, 0, w["pool_w"],
                                                              _after(token, w["pool_scale"]))
    do_gdn, dz, grads["gdn_norm_w"] = _gdn_post_backward(dcat, o_gdn, proj, 3 * gw, _after(token, w["gdn_norm_w"]),
                                                         heads, tm)
    dqkv, dbg = _gdn_core_backward(qkv, bg, states, solved, do_gdn, heads)
    token = yield ("poll", 1, dqkv)
    dqkv_pre, grads["conv_w"] = _gdn_pre_backward(proj, _after(token, w["conv_w"]), dqkv, heads)
    dba, dalog_row, ddt_row = _gates_backward(ba, bg, dbg, ea, dtb, heads)
    grads["a_log"] = dalog_row[:, heads:2 * heads]
    grads["dt_bias"] = ddt_row[:, heads:2 * heads]

    dproj = jnp.concatenate([dqkv_pre, dz, dp], axis=1)
    dw_main = _plain("proj_dw", dproj, x_bf, ta=True, tm=_pick(n_main, (512, 256, 128)), tn=d, tk=tk_t, out_dtype=F32)
    dw_ba = _plain("proj_gates_dw", dba, x_bf, ta=True, tm=HEAD_DIM, tn=tn_d, tk=tk_t, out_dtype=F32)
    dw_in_t = jnp.concatenate([dw_main[:4 * gw], dw_ba[:2 * heads], dw_main[4 * gw:]], axis=0)
    grads["w_in_t"] = dw_in_t.reshape(s_in, in_cols // s_in, d)

    def dx_epi(acc, ex, out, i):
        out[0][...] = acc + ex[1][...] + ALPHA * ex[0][...]

    def add_epi(acc, ex, out, i):
        out[0][...] = acc + ex[0][...]

    token = yield ("grads", 2, {n: grads.pop(n) for n in ("w_in_t", "pool_w")})
    dx_gates = _plain("proj_gates_dx", dba, _after(token, w_ba_t), tm=tm, tn=tn_d, tk=HEAD_DIM, out_dtype=F32)
    out_tile = [(jax.ShapeDtypeStruct((t, d), F32), (tm, tn_d), _tile)]
    dx_pool = _matmul("proj_pool_dx", dp, w_p_t, tm=tm, tn=tn_d, tk=_pick(pw, K_STEPS),
                      extra=[(dx_gates, (tm, tn_d), _tile)], outs=out_tile, epilogue=add_epi)[0]
    grad_x = _matmul(
        "proj_dx", dproj, w_in_t, k_used=4 * gw, tm=tm, tn=tn_d, tk=_pick(4 * gw, K_STEPS),
        extra=[(du1, (tm, tn_d), _tile), (dx_pool, (tm, tn_d), _tile)], outs=out_tile, epilogue=dx_epi)[0]
    yield ("poll", 2, grad_x)
    return loss, grad_x, grads


def _adamw(name, w, g, m, v):
    r, c = w.shape
    if r % 8 == 0:
        tr = _pick(r, (256, 128, 64, 32, 16, 8))
        blk, steps = pl.BlockSpec((tr, c), lambda i: (i, 0)), r // tr
    else:
        tc = _pick(c, (256, 128))
        blk, steps = pl.BlockSpec((r, tc), lambda i: (0, i)), c // tc
    c1 = 1.0 - ADAM_B1 ** ADAM_STEP
    c2 = 1.0 - ADAM_B2 ** ADAM_STEP

    def body(w_ref, g_ref, m_ref, v_ref, d_ref, mo_ref, vo_ref, go_ref):
        gv = g_ref[...]
        mn = ADAM_B1 * m_ref[...] + (1.0 - ADAM_B1) * gv
        vn = ADAM_B2 * v_ref[...] + (1.0 - ADAM_B2) * (gv * gv)
        d_ref[...] = -ADAM_LR * ((mn / c1) / (jnp.sqrt(vn / c2) + ADAM_EPS) + ADAM_WD * w_ref[...])
        mo_ref[...] = mn
        vo_ref[...] = vn
        go_ref[...] = gv

    return pl.pallas_call(
        body, name=name, grid=(steps,), in_specs=[blk] * 4, out_specs=[blk] * 4,
        out_shape=[jax.ShapeDtypeStruct((r, c), F32)] * 4,
        compiler_params=_params("parallel"),
    )(w, g, m, v)


def _place():
    x, y, c = lax.axis_index("x"), lax.axis_index("y"), lax.axis_index("c")
    chips = [(1 - x, y), (x, 1 - y), (1 - x, 1 - y)]
    return x, y, c, chips


HBM = pl.BlockSpec(memory_space=pltpu.HBM)


SEM = pl.BlockSpec(memory_space=pltpu.SEMAPHORE)
ANY = pl.BlockSpec(memory_space=pl.ANY)
EFFECT = pltpu.SideEffectType.DATAFLOW_SIDE_EFFECTING


def _in_hbm(a):
    return pltpu.with_memory_space_constraint(a, pltpu.HBM)


def _remote(src, dst, send_sem, recv_sem, to):
    return pltpu.make_async_remote_copy(src_ref=src, dst_ref=dst, send_sem=send_sem, recv_sem=recv_sem,
                                        device_id=to, device_id_type=MESH)


def _by_rows(rows):
    return rows % 32 == 0


def _half_shape(rows, cols):
    return (rows // 2, cols) if _by_rows(rows) else (rows, cols // 2)


def _half(ref, which, *lead):
    rows, cols = ref.shape[-2:]
    if _by_rows(rows):
        return ref.at[(*lead, pl.ds(which * (rows // 2), rows // 2))]
    return ref.at[(*lead, slice(None), pl.ds(which * (cols // 2), cols // 2))]


def _landed(lands, i, shard_index, which):
    return _half(lands[i], which, shard_index)


def _routes():
    x, y, c, _ = _place()
    first = (jnp.where(c == 0, 1 - x, x), jnp.where(c == 0, y, 1 - y))
    second = (jnp.where(c == 0, x, 1 - x), jnp.where(c == 0, 1 - y, y))
    return first, second, (1 - x, 1 - y)


def _shard_of(chip):
    return 2 * chip[0] + chip[1]


def _gather_start(name, shards, after, relayed=()):
    n = len(shards)
    lands = [lax.empty((N_SHARD,) + s.shape, s.dtype) for s in shards]

    def body(*refs):
        ins, zones = refs[:n], refs[n:2 * n]
        ici_send, ici_recv, own_send, own_recv = refs[2 * n + 1:2 * n + 5]
        token = refs[-1]
        x, y, c, chips = _place()
        me = 2 * x + y
        first, _, _ = _routes()
        for i in range(n):
            if i in relayed:
                _remote(_half(ins[i], c), _landed(zones, i, me, c), ici_send.at[3 * i], ici_recv.at[3 * i],
                        (*first, c)).start()
                continue
            for j, chip in enumerate(chips):
                _remote(_half(ins[i], c), _landed(zones, i, me, c), ici_send.at[3 * i + j],
                        ici_recv.at[3 * i + j], (*chip, c)).start()
        for i in range(n):
            _remote(ins[i], zones[i].at[me], own_send.at[i], own_recv.at[i], (x, y, 1 - c)).start()
        token[...] = jnp.zeros_like(token)

    dma = pltpu.SemaphoreType.DMA
    outs = pl.pallas_call(
        body, name=name,
        in_specs=[HBM] * (2 * n) + [ANY],
        out_shape=(dma((3 * n,)), dma((3 * n,)), dma((n,)), dma((n,)),
                   *[pltpu.HBM(a.shape, a.dtype) for a in shards + lands], jax.ShapeDtypeStruct((8, LANES), F32)),
        out_specs=(SEM, SEM, SEM, SEM, *[HBM] * (2 * n), pl.BlockSpec(memory_space=pltpu.VMEM)),
        input_output_aliases={k: 4 + k for k in range(2 * n)},
        compiler_params=pltpu.CompilerParams(has_side_effects=EFFECT),
    )(*[_in_hbm(a) for a in shards + lands], after)
    sems = dict(zip(("ici_send", "ici_recv", "own_send", "own_recv"), outs[:4]))
    return sems, list(outs[4:4 + n]), list(outs[4 + n:4 + 2 * n]), outs[-1]


def _gather_forward(name, idx, lands, sems, after):
    n = len(idx)

    def body(*refs):
        zones = refs[:n]
        ici_recv = refs[n]
        fwd_send, fwd_recv = refs[n + 2], refs[n + 3]
        x, y, c, chips = _place()
        for k, i in enumerate(idx):
            for j, chip in enumerate(chips):
                half = _landed(zones, k, 2 * chip[0] + chip[1], c)
                _remote(half, half, fwd_send.at[3 * k + j], ici_recv.at[3 * i + j], (*chip, c)).wait_recv()
                _remote(half, half, fwd_send.at[3 * k + j], fwd_recv.at[3 * k + j], (x, y, 1 - c)).start()

    dma = pltpu.SemaphoreType.DMA
    outs = pl.pallas_call(
        body, name=name,
        in_specs=[HBM] * n + [SEM, ANY],
        out_shape=(dma((3 * n,)), dma((3 * n,)), *[pltpu.HBM(a.shape, a.dtype) for a in lands]),
        out_specs=(SEM, SEM, *[HBM] * n),
        input_output_aliases={k: 2 + k for k in range(n)},
        compiler_params=pltpu.CompilerParams(has_side_effects=EFFECT),
    )(*lands, sems["ici_recv"], after)
    return (outs[0], outs[1]), list(outs[2:])


def _gather_wait(name, idx, shards, lands, sems, fwd, after):
    n = len(idx)

    def body(*refs):
        ins, zones = refs[:n], refs[n:2 * n]
        ici_send, own_send, own_recv, fwd_send, fwd_recv = refs[2 * n:2 * n + 5]
        x, y, c, chips = _place()
        me = 2 * x + y
        for k, i in enumerate(idx):
            mine = _half(ins[k], c)
            for j, chip in enumerate(chips):
                theirs = 2 * chip[0] + chip[1]
                _remote(mine, _landed(zones, k, me, c), ici_send.at[3 * i + j], fwd_recv.at[3 * k + j],
                        (*chip, c)).wait_send()
                sent = _landed(zones, k, theirs, c)
                _remote(sent, sent, fwd_send.at[3 * k + j], fwd_recv.at[3 * k + j], (x, y, 1 - c)).wait_send()
                passed = _landed(zones, k, theirs, 1 - c)
                _remote(passed, passed, fwd_send.at[3 * k + j], fwd_recv.at[3 * k + j], (x, y, 1 - c)).wait_recv()
            own = _remote(ins[k], zones[k].at[me], own_send.at[i], own_recv.at[i], (x, y, 1 - c))
            own.wait_send()
            own.wait_recv()

    outs = pl.pallas_call(
        body, name=name,
        in_specs=[HBM] * (2 * n) + [SEM] * 5 + [ANY],
        out_shape=tuple(pltpu.HBM(a.shape, a.dtype) for a in lands),
        out_specs=tuple([HBM] * n),
        input_output_aliases={n + k: k for k in range(n)},
        compiler_params=pltpu.CompilerParams(has_side_effects=EFFECT),
    )(*shards, *lands, sems["ici_send"], sems["own_send"], sems["own_recv"], fwd[0], fwd[1], after)
    return list(outs)


def _gather_relay(name, idx, shards, lands, sems, after):
    n = len(idx)

    def body(*refs):
        ins, zones, ici_recv = refs[:n], refs[n:2 * n], refs[2 * n]
        relay_send, relay_recv, pass_send, pass_recv = refs[2 * n + 2:2 * n + 6]
        x, y, c, _ = _place()
        first, second, _ = _routes()
        for k, i in enumerate(idx):
            landed = _landed(zones, k, _shard_of(first), c)
            _remote(landed, landed, pass_send.at[k], ici_recv.at[3 * i], (*first, c)).wait_recv()
            _remote(_half(ins[k], c), _landed(zones, k, 2 * x + y, c), relay_send.at[2 * k], relay_recv.at[2 * k],
                    (*second, c)).start()
            _remote(landed, landed, relay_send.at[2 * k + 1], relay_recv.at[2 * k + 1], (*second, c)).start()
            _remote(landed, landed, pass_send.at[k], pass_recv.at[k], (x, y, 1 - c)).start()
        refs[-1][...] = jnp.zeros_like(refs[-1])

    dma = pltpu.SemaphoreType.DMA
    outs = pl.pallas_call(
        body, name=name,
        in_specs=[HBM] * (2 * n) + [SEM, ANY],
        out_shape=(dma((2 * n,)), dma((2 * n,)), dma((n,)), dma((n,)), *[pltpu.HBM(a.shape, a.dtype) for a in lands],
                   jax.ShapeDtypeStruct((8, LANES), F32)),
        out_specs=(SEM, SEM, SEM, SEM, *[HBM] * n, pl.BlockSpec(memory_space=pltpu.VMEM)),
        input_output_aliases={n + k: 4 + k for k in range(n)},
        compiler_params=pltpu.CompilerParams(has_side_effects=EFFECT),
    )(*shards, *lands, sems["ici_recv"], after)
    return outs[:4], list(outs[4:4 + n]), outs[-1]


def _gather_forward_relayed(name, ks, lands, relay, after):
    n = len(ks)

    def body(*refs):
        zones, relay_recv = refs[:n], refs[n]
        fwd_send, fwd_recv = refs[n + 2], refs[n + 3]
        x, y, c, _ = _place()
        _, second, diagonal = _routes()
        for p, k in enumerate(ks):
            for j, chip in enumerate((second, diagonal)):
                landed = _landed(zones, p, _shard_of(chip), c)
                _remote(landed, landed, fwd_send.at[2 * p + j], relay_recv.at[2 * k + j], (*second, c)).wait_recv()
                _remote(landed, landed, fwd_send.at[2 * p + j], fwd_recv.at[2 * p + j], (x, y, 1 - c)).start()

    dma = pltpu.SemaphoreType.DMA
    outs = pl.pallas_call(
        body, name=name,
        in_specs=[HBM] * n + [SEM, ANY],
        out_shape=(dma((2 * n,)), dma((2 * n,)), *[pltpu.HBM(a.shape, a.dtype) for a in lands]),
        out_specs=(SEM, SEM, *[HBM] * n),
        input_output_aliases={k: 2 + k for k in range(n)},
        compiler_params=pltpu.CompilerParams(has_side_effects=EFFECT),
    )(*lands, relay[1], after)
    return (outs[0], outs[1]), list(outs[2:])


def _gather_wait_relayed(name, idx, ks, shards, lands, sems, relay, fwd, after):
    n = len(idx)

    def body(*refs):
        ins, zones = refs[:n], refs[n:2 * n]
        ici_send, own_send, own_recv, relay_send, pass_send, pass_recv, fwd_send, fwd_recv = refs[2 * n:2 * n + 8]
        x, y, c, _ = _place()
        me = 2 * x + y
        sibling = (x, y, 1 - c)
        first, second, diagonal = _routes()
        for p, (i, k) in enumerate(zip(idx, ks)):
            mine, at_peer = _half(ins[p], c), _landed(zones, p, me, c)
            from_first = _landed(zones, p, _shard_of(first), c)
            _remote(mine, at_peer, ici_send.at[3 * i], pass_recv.at[k], (*first, c)).wait_send()
            _remote(mine, at_peer, relay_send.at[2 * k], pass_recv.at[k], (*second, c)).wait_send()
            _remote(from_first, from_first, relay_send.at[2 * k + 1], pass_recv.at[k], (*second, c)).wait_send()
            _remote(from_first, from_first, pass_send.at[k], pass_recv.at[k], sibling).wait_send()
            theirs = _landed(zones, p, _shard_of(second), 1 - c)
            _remote(theirs, theirs, pass_send.at[k], pass_recv.at[k], sibling).wait_recv()
            for j, (sent, got) in enumerate(((second, first), (diagonal, diagonal))):
                out_half = _landed(zones, p, _shard_of(sent), c)
                _remote(out_half, out_half, fwd_send.at[2 * p + j], fwd_recv.at[2 * p + j], sibling).wait_send()
                in_half = _landed(zones, p, _shard_of(got), 1 - c)
                _remote(in_half, in_half, fwd_send.at[2 * p + j], fwd_recv.at[2 * p + j], sibling).wait_recv()
            own = _remote(ins[p], zones[p].at[me], own_send.at[i], own_recv.at[i], sibling)
            own.wait_send()
            own.wait_recv()

    outs = pl.pallas_call(
        body, name=name,
        in_specs=[HBM] * (2 * n) + [SEM] * 8 + [ANY],
        out_shape=tuple(pltpu.HBM(a.shape, a.dtype) for a in lands),
        out_specs=tuple([HBM] * n),
        input_output_aliases={n + k: k for k in range(n)},
        compiler_params=pltpu.CompilerParams(has_side_effects=EFFECT),
    )(*shards, *lands, sems["ici_send"], sems["own_send"], sems["own_recv"], relay[0], relay[2], relay[3],
      fwd[0], fwd[1], after)
    return list(outs)


def _all_reduce_small(name, slab, after=None):
    r, width = slab.shape
    ndev = 8

    def body(x_ref, after_ref, out_ref, buf, send_sems, recv_sems):
        x, y, c, _ = _place()
        me = 4 * x + 2 * y + c
        buf[me] = x_ref[...]
        copies = []
        for k in range(1, ndev):
            peer = jnp.bitwise_xor(me, k)
            to = (peer // 4, (peer // 2) % 2, peer % 2)
            cp = pltpu.make_async_remote_copy(src_ref=x_ref, dst_ref=buf.at[me], send_sem=send_sems.at[k - 1],
                                              recv_sem=recv_sems.at[k - 1], device_id=to, device_id_type=MESH)
            cp.start()
            copies.append(cp)
        for k in range(1, ndev):
            peer = jnp.bitwise_xor(me, k)
            pltpu.make_async_remote_copy(src_ref=x_ref, dst_ref=buf.at[peer], send_sem=send_sems.at[k - 1],
                                         recv_sem=recv_sems.at[k - 1], device_id=(x, y, c),
                                         device_id_type=MESH).wait_recv()
        for cp in copies:
            cp.wait_send()
        total = buf[0]
        for d in range(1, ndev):
            total = total + buf[d]
        out_ref[...] = total

    return pl.pallas_call(
        body, name=name,
        in_specs=[pl.BlockSpec(memory_space=pltpu.VMEM), ANY], out_specs=pl.BlockSpec(memory_space=pltpu.VMEM),
        out_shape=jax.ShapeDtypeStruct((r, width), F32),
        scratch_shapes=[pltpu.VMEM((ndev, r, width), F32), pltpu.SemaphoreType.DMA((ndev - 1,)),
                        pltpu.SemaphoreType.DMA((ndev - 1,))],
        compiler_params=pltpu.CompilerParams(vmem_limit_bytes=VMEM_LIMIT),
    )(slab, slab if after is None else after)


def _half_tiling(rows, cols):
    if _by_rows(rows):
        tr = _pick(rows // 2, (256, 128, 64, 32, 16))
        nb = (rows // 2) // tr
        return (tr, cols), nb, (lambda which, b: (which * nb + b, 0)), (lambda b: (b, 0))
    tc = _pick(cols // 2, (256, 128))
    nb = (cols // 2) // tc
    return (rows, tc), nb, (lambda which, b: (0, which * nb + b)), (lambda b: (0, b))


def _chip_partial(name, grad, other, core):
    s, r, cdim = grad.shape
    blk, nb, whole, within = _half_tiling(r, cdim)

    def body(core_ref, g_ref, o_ref, out_ref):
        out_ref[...] = (g_ref[...] + o_ref[...]).astype(BF16)

    return pl.pallas_call(
        body, name=name,
        grid_spec=pltpu.PrefetchScalarGridSpec(
            num_scalar_prefetch=1, grid=(s, nb),
            in_specs=[pl.BlockSpec((None,) + blk, lambda j, b, core_ref: (j,) + whole(core_ref[0], b)),
                      pl.BlockSpec((None,) + blk, lambda j, b, core_ref: (j,) + within(b))],
            out_specs=pl.BlockSpec((None,) + blk, lambda j, b, core_ref: (j,) + within(b))),
        out_shape=jax.ShapeDtypeStruct((s,) + _half_shape(r, cdim), BF16),
        compiler_params=_params("parallel", "parallel"),
    )(core, grad, other)


def _partial_copies(ins, zones, send_sems, recv_sems):
    x, y, c, chips = _place()
    return [_remote(ins[i].at[2 * chip[0] + chip[1]], zones[i].at[j], send_sems.at[3 * i + j],
                    recv_sems.at[3 * i + j], (*chip, c))
            for i in range(len(ins)) for j, chip in enumerate(chips)]


def _swap_copies(ins, zones, send_sems, recv_sems):
    x, y, c, _ = _place()
    copies = []
    for i in range(len(ins)):
        for s in range(N_SHARD):
            copies.append(_remote(_half(ins[i], 1 - c, s), zones[i].at[s],
                                  send_sems.at[N_SHARD * i + s], recv_sems.at[N_SHARD * i + s], (x, y, 1 - c)))
    return copies


def _exchange_start(name, plan, sources, lands, per_array):
    n = len(sources)
    lands = [lax.empty(shape, dtype) for shape, dtype in lands]

    def body(*refs):
        for cp in plan(refs[:n], refs[n:2 * n], refs[2 * n], refs[2 * n + 1]):
            cp.start()
        refs[-1][...] = jnp.zeros_like(refs[-1])

    dma = pltpu.SemaphoreType.DMA
    outs = pl.pallas_call(
        body, name=name,
        in_specs=[HBM] * (2 * n),
        out_shape=(dma((per_array * n,)), dma((per_array * n,)),
                   *[pltpu.HBM(a.shape, a.dtype) for a in list(sources) + lands], jax.ShapeDtypeStruct((8, LANES), F32)),
        out_specs=(SEM, SEM, *[HBM] * (2 * n), pl.BlockSpec(memory_space=pltpu.VMEM)),
        input_output_aliases={k: 2 + k for k in range(2 * n)},
        compiler_params=pltpu.CompilerParams(has_side_effects=EFFECT),
    )(*[_in_hbm(a) for a in list(sources) + lands])
    return (outs[0], outs[1]), list(outs[2:2 + n]), list(outs[2 + n:2 + 2 * n]), outs[-1]


def _exchange_wait(name, plan, started, after):
    sems, partials, lands, _ = started
    n = len(partials)

    def body(*refs):
        for cp in plan(refs[:n], refs[n:2 * n], refs[2 * n], refs[2 * n + 1]):
            cp.wait_send()
            cp.wait_recv()

    outs = pl.pallas_call(
        body, name=name,
        in_specs=[HBM] * (2 * n) + [SEM, SEM] + [ANY] * len(after),
        out_shape=tuple(pltpu.HBM(a.shape, a.dtype) for a in lands),
        out_specs=tuple([HBM] * n),
        input_output_aliases={n + k: k for k in range(n)},
        compiler_params=pltpu.CompilerParams(has_side_effects=EFFECT),
    )(*partials, *lands, sems[0], sems[1], *after)
    return list(outs)


def _reduce_own(name, grad, other, received, where):
    s, r, cdim = grad.shape
    blk, nb, whole, within = _half_tiling(r, cdim)

    def body(where_ref, g_ref, o_ref, r_ref, out_ref):
        total = g_ref[...] + o_ref[...]
        for j in range(3):
            total = total + r_ref[j].astype(F32)
        out_ref[...] = total

    return pl.pallas_call(
        body, name=name,
        grid_spec=pltpu.PrefetchScalarGridSpec(
            num_scalar_prefetch=1, grid=(nb,),
            in_specs=[pl.BlockSpec((None,) + blk, lambda b, w_ref: (w_ref[0],) + whole(w_ref[1], b)),
                      pl.BlockSpec((None,) + blk, lambda b, w_ref: (w_ref[0],) + within(b)),
                      pl.BlockSpec((3,) + blk, lambda b, w_ref: (0,) + within(b))],
            out_specs=pl.BlockSpec(blk, lambda b, w_ref: whole(w_ref[1], b))),
        out_shape=jax.ShapeDtypeStruct((r, cdim), F32),
        compiler_params=_params("parallel"),
    )(where, grad, other, received)


def _join_start(name, halves):
    n = len(halves)

    def body(*refs):
        bufs, send_sems, recv_sems = refs[:n], refs[n], refs[n + 1]
        x, y, c, _ = _place()
        for i in range(n):
            mine = _half(bufs[i], c)
            _remote(mine, mine, send_sems.at[i], recv_sems.at[i], (x, y, 1 - c)).start()
        refs[-1][...] = jnp.zeros_like(refs[-1])

    dma = pltpu.SemaphoreType.DMA
    outs = pl.pallas_call(
        body, name=name,
        in_specs=[HBM] * n,
        out_shape=(dma((n,)), dma((n,)), *[pltpu.HBM(h.shape, F32) for h in halves], jax.ShapeDtypeStruct((8, LANES), F32)),
        out_specs=(SEM, SEM, *[HBM] * n, pl.BlockSpec(memory_space=pltpu.VMEM)),
        input_output_aliases={k: 2 + k for k in range(n)},
        compiler_params=pltpu.CompilerParams(has_side_effects=EFFECT),
    )(*[_in_hbm(h) for h in halves])
    return (outs[0], outs[1]), list(outs[2:2 + n]), outs[-1]


def _join_wait(name, started, after):
    sems, bufs, _ = started
    n = len(bufs)

    def body(*refs):
        bufs, send_sems, recv_sems = refs[:n], refs[n], refs[n + 1]
        x, y, c, _ = _place()
        for i in range(n):
            mine, theirs = _half(bufs[i], c), _half(bufs[i], 1 - c)
            _remote(mine, mine, send_sems.at[i], recv_sems.at[i], (x, y, 1 - c)).wait_send()
            _remote(theirs, theirs, send_sems.at[i], recv_sems.at[i], (x, y, 1 - c)).wait_recv()

    outs = pl.pallas_call(
        body, name=name,
        in_specs=[HBM] * n + [SEM, SEM] + [ANY] * len(after),
        out_shape=tuple(pltpu.HBM(b.shape, F32) for b in bufs),
        out_specs=tuple([HBM] * n),
        input_output_aliases={k: k for k in range(n)},
        compiler_params=pltpu.CompilerParams(has_side_effects=EFFECT),
    )(*bufs, sems[0], sems[1], *after)
    return list(outs)


BIG = ("w_in", "pool_w", "w_out", "xq_w", "xk_w", "xv_w", "xo_w", "w_up", "w_down", "conv_w")
KEPT_F32 = ("conv_w",)
GATHER_GROUPS = ((0, 1, 9), (2, 3, 4, 5, 6), (7,), (8,))
RELAYED = (7, 8)
SMALL = ("conv_w", "a_log", "dt_bias", "gdn_norm_w", "pool_scale", "ln1_g", "ln1_b", "ln2_g", "ln2_b", "ln3_g", "ln3_b")
ORDER = ("w_in", "conv_w", "a_log", "dt_bias", "gdn_norm_w", "pool_w", "pool_scale", "w_out", "ln1_g", "ln1_b",
         "xq_w", "xk_w", "xv_w", "xo_w", "ln2_g", "ln2_b", "w_up", "w_down", "ln3_g", "ln3_b")
LANES = 128


def _rows(flat_len):
    return -(-flat_len // LANES)


def _pack(pieces):
    out = []
    for p in pieces:
        flat = p.reshape(-1).astype(F32)
        out.append(jnp.pad(flat, (0, _rows(flat.shape[0]) * LANES - flat.shape[0])).reshape(-1, LANES))
    slab = jnp.concatenate(out, axis=0)
    return jnp.pad(slab, ((0, -slab.shape[0] % 8), (0, 0)))


def _unpack(slab, shapes):
    out, row = [], 0
    for shp in shapes:
        size = math.prod(shp)
        out.append(slab[row:row + _rows(size)].reshape(-1)[:size].reshape(shp))
        row += _rows(size)
    return out


TRANSPOSED = ("w_in",)


def _as2d(name, a):
    a = a[0]
    if name in TRANSPOSED:
        return jnp.swapaxes(a, 0, 1)
    return a.reshape(-1, a.shape[-1]) if a.ndim == 3 else a


def _from2d(name, a, shape):
    return (jnp.swapaxes(a, 0, 1) if name in TRANSPOSED else a).reshape(shape)


def kernel(x, mem, w_in, conv_w, a_log, dt_bias, gdn_norm_w, pool_w, pool_scale, w_out, ln1_g, ln1_b, xq_w, xk_w, xv_w, xo_w, ln2_g, ln2_b, w_up, w_down, ln3_g, ln3_b, loss_target, m_w_in, m_conv_w, m_a_log, m_dt_bias, m_gdn_norm_w, m_pool_w, m_pool_scale, m_w_out, m_ln1_g, m_ln1_b, m_xq_w, m_xk_w, m_xv_w, m_xo_w, m_ln2_g, m_ln2_b, m_w_up, m_w_down, m_ln3_g, m_ln3_b, v_w_in, v_conv_w, v_a_log, v_dt_bias, v_gdn_norm_w, v_pool_w, v_pool_scale, v_w_out, v_ln1_g, v_ln1_b, v_xq_w, v_xk_w, v_xv_w, v_xo_w, v_ln2_g, v_ln2_b, v_w_up, v_w_down, v_ln3_g, v_ln3_b):
    given = dict(locals())
    cx, cy, cc = lax.axis_index("x"), lax.axis_index("y"), lax.axis_index("c")
    me = 2 * cx + cy
    groups = pool_w.shape[1]
    cs = pool_w.shape[2]
    kk, conv_cols = conv_w.shape[1], conv_w.shape[2]
    core = cc.astype(jnp.int32).reshape(1)
    where = jnp.stack([me, cc]).astype(jnp.int32)

    started = {}
    wts = {}

    def start(name, idx, after, token=None):
        casts = [_after(token, _as2d(BIG[i], given[BIG[i]])).astype(F32 if BIG[i] in KEPT_F32 else BF16) for i in idx]
        relayed = tuple(k for k, i in enumerate(idx) if i in RELAYED)
        sems, shards, lands, token = _gather_start(name, casts, after, relayed)
        for k, i in enumerate(idx):
            started[i] = (sems, k, shards[k], lands[k])
        return token

    token = start("gather_start_first", GATHER_GROUPS[0], x)
    token = start("gather_start_rest", tuple(i for group in GATHER_GROUPS[1:] for i in group), token, token)

    relay = {}

    def send_on(after):
        members = [started[i] for i in RELAYED]
        relay["sems"], zones, token = _gather_relay("gather_relay", [m[1] for m in members], [m[2] for m in members],
                                                    [m[3] for m in members], members[0][0], after)
        relay["zones"] = dict(zip(RELAYED, zones))
        return token

    def fetch(group, after):
        members = [started[i] for i in GATHER_GROUPS[group]]
        sems, idx = members[0][0], [m[1] for m in members]
        shards = [m[2] for m in members]
        if GATHER_GROUPS[group][0] in RELAYED:
            ks = [RELAYED.index(i) for i in GATHER_GROUPS[group]]
            zones = [relay["zones"][i] for i in GATHER_GROUPS[group]]
            fwd, zones = _gather_forward_relayed(f"gather_forward_{group}", ks, zones, relay["sems"], after)
            got = _gather_wait_relayed(f"gather_wait_{group}", idx, ks, shards, zones, sems, relay["sems"], fwd, after)
        else:
            fwd, zones = _gather_forward(f"gather_forward_{group}", idx, [m[3] for m in members], sems, after)
            got = _gather_wait(f"gather_wait_{group}", idx, shards, zones, sems, fwd, after)
        full = dict(zip([BIG[i] for i in GATHER_GROUPS[group]], got))
        out = {}
        for n, a in full.items():
            if n == "w_in":
                out["w_in_t"] = a
            elif n == "w_up":
                out["w_up3"] = a
            elif n == "pool_w":
                out[n] = a.reshape(N_SHARD, groups, cs, -1).transpose(1, 0, 2, 3).reshape(groups, N_SHARD * cs, -1)
            elif n == "conv_w":
                out[n] = a.transpose(1, 0, 2).reshape(kk, N_SHARD * conv_cols)
            else:
                out[n] = a.reshape(-1, a.shape[-1])
        return out

    for n in ("a_log", "dt_bias", "gdn_norm_w", "pool_scale", "ln1_g", "ln1_b", "ln2_g", "ln2_b", "ln3_g", "ln3_b"):
        wts[n] = given[n]
    wts.update(fetch(0, token))

    def start_swap(group, grads):
        names, blocks = [], []
        for n, g in grads.items():
            if n == "pool_w":
                g = g.reshape(groups, N_SHARD, cs, -1).transpose(1, 0, 2, 3).reshape(N_SHARD, groups * cs, -1)
            elif g.ndim == 2:
                g = g.reshape(N_SHARD, -1, g.shape[-1])
            names.append({"w_in_t": "w_in", "w_up3": "w_up"}.get(n, n))
            blocks.append(g)
        zones = [((N_SHARD,) + _half_shape(b.shape[1], b.shape[2]), F32) for b in blocks]
        swap = _exchange_start(f"grad_swap_start_{group}", _swap_copies, blocks, zones, N_SHARD)
        return {"group": group, "names": names, "swap": swap, "token": swap[3]}

    def start_send(state, after):
        group, names = state["group"], state["names"]
        state["blocks"] = state["swap"][1]
        state["others"] = _exchange_wait(f"grad_swap_wait_{group}", _swap_copies, state["swap"], after)
        partials = [_chip_partial("chip_partial_" + n, gb, ob, core)
                    for n, gb, ob in zip(names, state["blocks"], state["others"])]
        zones = [((3,) + p.shape[1:], BF16) for p in partials]
        state["send"] = _exchange_start(f"grad_send_start_{group}", _partial_copies, partials, zones, 3)
        state["token"] = state["send"][3]

    grad, delta, new_m, new_v = {}, {}, {}, {}

    def start_join(state, after):
        group, names = state["group"], state["names"]
        received = _exchange_wait(f"grad_send_wait_{group}", _partial_copies, state["send"], after)
        halves = [_reduce_own("reduce_own_" + n, gb, ob, rb, where)
                  for n, gb, ob, rb in zip(names, state["blocks"], state["others"], received)]
        state["join"] = _join_start(f"grad_join_start_{group}", halves)
        return state["join"][2]

    def finish_reduce(state, after):
        group, names = state["group"], state["names"]
        for n, g in zip(names, _join_wait(f"grad_join_wait_{group}", state["join"], after)):
            shp = given[n].shape
            d2, m2, v2, g2 = _adamw("adamw_" + n, _as2d(n, given[n]), g, _as2d(n, given["m_" + n]),
                                    _as2d(n, given["v_" + n]))
            grad[n], delta[n], new_m[n], new_v[n] = (_from2d(n, a, shp) for a in (g2, d2, m2, v2))
        return d2

    step = _local_step(x[0], mem[0], loss_target[0], wts, token)
    pending = {}
    request = next(step)
    while True:
        try:
            kind, group, payload = request
            if kind == "weights":
                request = step.send(fetch(group, payload))
            elif kind == "relay":
                request = step.send(send_on(payload))
            elif kind == "grads":
                pending[group] = start_swap(group, payload)
                request = step.send(pending[group]["token"])
            else:
                start_send(pending[group], [payload])
                request = step.send(pending[group]["token"])
        except StopIteration as stop:
            loss_row, grad_x, g = stop.value
            break

    after = [pending[2]["token"], grad_x]
    for group in (0, 1):
        after = [start_join(pending[group], after)]
    for group in (0, 1):
        after = [finish_reduce(pending[group], after)]
    after = [finish_reduce(pending[2], [start_join(pending[2], after)])]

    small_names = ("a_log", "dt_bias", "gdn_norm_w", "pool_scale", "ln1_g", "ln1_b", "ln2_g", "ln2_b", "ln3_g", "ln3_b")
    pieces = [g["conv_w"]] + [g[n] for n in small_names] + [loss_row[:, :1]]
    shapes = [p.shape for p in pieces]
    summed = _unpack(_all_reduce_small("all_reduce_small", _pack(pieces), after[0]), shapes)
    gsmall = dict(zip(small_names, summed[1:-1]))
    gsmall["conv_w"] = lax.dynamic_slice(summed[0], (0, me * conv_cols), (kk, conv_cols))
    loss = summed[-1][0, 0]

    sshapes = [given[n].shape for n in SMALL]
    slabs = [_pack([given[p + n] for n in SMALL]) for p in ("", "m_", "v_")]
    gslab = _pack([gsmall[n] for n in SMALL])
    outs = _adamw("adamw_small", slabs[0], gslab, slabs[1], slabs[2])[:3]
    for dst, slab in zip((delta, new_m, new_v), outs):
        dst.update(zip(SMALL, _unpack(slab, sshapes)))
    for n in SMALL:
        grad[n] = gsmall[n].reshape(given[n].shape)

    return (loss, grad_x[None], *[grad[n] for n in ORDER], *[delta[n] for n in ORDER],
            *[new_m[n] for n in ORDER], *[new_v[n] for n in ORDER])
```

```python
import math

import jax
import jax.numpy as jnp
from jax import lax
from jax.experimental import pallas as pl
from jax.experimental.pallas import tpu as pltpu

F32 = jnp.float32
BF16 = jnp.bfloat16
MESH = pl.DeviceIdType.MESH

HEAD_DIM = 128
CHUNK = 64
POOL_WINDOWS = (2, 4, 8, 16)
XATTN_HEADS = 4
ALPHA = 2.0 ** 0.25
LN_EPS = 1e-5
NORM_EPS = 1e-6
ADAM_LR, ADAM_B1, ADAM_B2, ADAM_EPS, ADAM_WD, ADAM_STEP = 0.001, 0.9, 0.999, 1e-08, 0.01, 10
N_SHARD = 4
VMEM_LIMIT = 56 * 1024 * 1024
K_STEPS = (2048, 1024, 512, 256, 128)


def _params(*sem):
    return pltpu.CompilerParams(dimension_semantics=sem, vmem_limit_bytes=VMEM_LIMIT)


def _bdot(a, b, ta=False, tb=False):
    dims = (((0 if ta else 1,), (1 if tb else 0,)), ((), ()))
    return lax.dot_general(a.astype(BF16), b.astype(BF16), dims, preferred_element_type=F32)


def _sigmoid(x):
    return 1.0 / (1.0 + jnp.exp(-x))


def _matmul(name, a, b, *, ta=False, tb=False, tm, tn, tk, extra=(), outs, epilogue, b_blocks=None,
            sequential=False, n_used=None, k_used=None, n_outer=False):
    m, k_dim = (a.shape[1], a.shape[0]) if ta else a.shape
    if b_blocks and tb:
        n = b.shape[1]
        k_dim = b.shape[0] * b.shape[2]
        per = b.shape[2] // tk
        b_spec = pl.BlockSpec((None, tn, tk), lambda i, j, k: (k // per, j, k % per))
    elif b_blocks:
        n = b.shape[0] * b.shape[2]
        per = b.shape[2] // tn
        b_spec = pl.BlockSpec((None, tk, tn), lambda i, j, k: (j // per, k, j % per))
    elif tb:
        n = b.shape[0]
        b_spec = pl.BlockSpec((tn, tk), lambda i, j, k: (j, k))
    else:
        n = b.shape[1]
        b_spec = pl.BlockSpec((tk, tn), lambda i, j, k: (k, j))
    n, k_dim = n_used or n, k_used or k_dim
    assert m % tm == 0 and n % tn == 0 and k_dim % tk == 0, (name, m, n, k_dim, tm, tn, tk)
    nk = k_dim // tk
    a_spec = pl.BlockSpec((tk, tm), lambda i, j, k: (k, i)) if ta else pl.BlockSpec((tm, tk), lambda i, j, k: (i, k))
    n_extra, n_out = len(extra), len(outs)

    def wrap(index_map):
        return lambda i, j, k: index_map(i, j)

    def spec(block, index_map):
        if n_outer:
            return pl.BlockSpec(block, lambda j, i, k: index_map(i, j, k))
        return pl.BlockSpec(block, index_map)

    row_axis = 1 if n_outer else 0

    def body_one_step(*refs):
        ex = refs[2:2 + n_extra]
        out = refs[2 + n_extra:2 + n_extra + n_out]
        epilogue(_bdot(refs[0][...], refs[1][...], ta, tb), ex, out, pl.program_id(row_axis))

    def body(*refs):
        a_ref, b_ref = refs[0], refs[1]
        ex = refs[2:2 + n_extra]
        out = refs[2 + n_extra:2 + n_extra + n_out]
        acc = refs[-1]
        i, k = pl.program_id(row_axis), pl.program_id(2)
        part = _bdot(a_ref[...], b_ref[...], ta, tb)

        @pl.when(k == 0)
        def _():
            acc[...] = part

        @pl.when(jnp.logical_and(k > 0, k < nk - 1))
        def _():
            acc[...] += part

        @pl.when(k == nk - 1)
        def _():
            epilogue(acc[...] + part, ex, out, i)

    sem = ("arbitrary",) * 3 if sequential else ("parallel", "parallel", "arbitrary")
    res = pl.pallas_call(
        body_one_step if nk == 1 else body, name=name,
        grid=(n // tn, m // tm, nk) if n_outer else (m // tm, n // tn, nk),
        in_specs=[spec(a_spec.block_shape, a_spec.index_map), spec(b_spec.block_shape, b_spec.index_map)]
        + [spec(bs, wrap(im)) for _, bs, im in extra],
        out_specs=[spec(bs, wrap(im)) for _, bs, im in outs],
        out_shape=[s for s, _, _ in outs],
        scratch_shapes=[] if nk == 1 else [pltpu.VMEM((tm, tn), F32)],
        compiler_params=_params(*sem),
    )(a, b, *[x for x, _, _ in extra])
    return res


def _tile(i, j):
    return (i, j)


def _plain(name, a, b, *, ta=False, tb=False, tm, tn, tk, out_dtype, b_blocks=None, out3=None, n_used=None,
           n_outer=False):
    m = a.shape[1] if ta else a.shape[0]
    if b_blocks:
        n = b.shape[1] if tb else b.shape[0] * b.shape[2]
    else:
        n = n_used or (b.shape[0] if tb else b.shape[1])

    def epi(acc, ex, out, i):
        out[0][...] = acc.astype(out_dtype)

    if out3:
        per = (n // out3) // tn
        spec = (jax.ShapeDtypeStruct((out3, m, n // out3), out_dtype), (None, tm, tn),
                lambda i, j: (j // per, i, j % per))
    else:
        spec = (jax.ShapeDtypeStruct((m, n), out_dtype), (tm, tn), _tile)
    return _matmul(name, a, b, ta=ta, tb=tb, tm=tm, tn=tn, tk=tk, outs=[spec], epilogue=epi,
                   b_blocks=b_blocks, n_used=n_used, n_outer=n_outer)[0]


def _ln_forward(name, a, b, res, gamma, beta, *, tm, tk, want_h=True):
    m, n = res.shape

    def epi(acc, ex, out, i):
        u = ALPHA * ex[0][...] + acc
        mu = jnp.mean(u, axis=-1, keepdims=True)
        xc = u - mu
        var = jnp.mean(xc * xc, axis=-1, keepdims=True)
        rstd = lax.rsqrt(var + LN_EPS)
        xhat = xc * rstd
        out[-2][...] = xhat
        out[-1][...] = rstd
        if want_h:
            h = xhat * ex[1][...] + ex[2][...]
            out[0][...] = h
            out[1][...] = h.astype(BF16)

    row = lambda i, j: (i, 0)
    vec = lambda i, j: (0, 0)
    outs = [(jax.ShapeDtypeStruct((m, n), F32), (tm, n), row), (jax.ShapeDtypeStruct((m, n), BF16), (tm, n), row),
            (jax.ShapeDtypeStruct((m, n), F32), (tm, n), row), (jax.ShapeDtypeStruct((m, 1), F32), (tm, 1), row)]
    return _matmul(
        name, a, b, tm=tm, tn=n, tk=tk,
        extra=[(res, (tm, n), row), (gamma, (1, n), vec), (beta, (1, n), vec)],
        outs=outs if want_h else outs[2:], epilogue=epi)


def _ln_backward_math(dy, xhat, rstd, gamma):
    dxhat = dy * gamma
    m1 = jnp.mean(dxhat, axis=-1, keepdims=True)
    m2 = jnp.mean(dxhat * xhat, axis=-1, keepdims=True)
    du = rstd * (dxhat - m1 - xhat * m2)
    return du, jnp.sum(dy * xhat, axis=0, keepdims=True), jnp.sum(dy, axis=0, keepdims=True)


def _ln_backward(name, a, b, dres, xhat, rstd, gamma, *, tm, tk, b_blocks=None, tb=True):
    m, n = dres.shape

    def epi(acc, ex, out, i):
        dy = acc + ALPHA * ex[0][...]
        du, dg, db = _ln_backward_math(dy, ex[1][...], ex[2][...], ex[3][...])
        out[0][...] = du
        out[1][...] = du.astype(BF16)
        first = i == 0

        @pl.when(first)
        def _():
            out[2][...] = dg
            out[3][...] = db

        @pl.when(jnp.logical_not(first))
        def _():
            out[2][...] += dg
            out[3][...] += db

    row = lambda i, j: (i, 0)
    vec = lambda i, j: (0, 0)
    return _matmul(
        name, a, b, tb=tb, tm=tm, tn=n, tk=tk, b_blocks=b_blocks, sequential=True,
        extra=[(dres, (tm, n), row), (xhat, (tm, n), row), (rstd, (tm, 1), row), (gamma, (1, n), vec)],
        outs=[(jax.ShapeDtypeStruct((m, n), F32), (tm, n), row),
              (jax.ShapeDtypeStruct((m, n), BF16), (tm, n), row),
              (jax.ShapeDtypeStruct((1, n), F32), (1, n), vec),
              (jax.ShapeDtypeStruct((1, n), F32), (1, n), vec)],
        epilogue=epi)


def _shift_down(x, k):
    row = lax.broadcasted_iota(jnp.int32, x.shape, 0)
    return jnp.where(row >= k, pltpu.roll(x, k, axis=0), 0.0)


def _shift_up(x, k):
    t = x.shape[0]
    row = lax.broadcasted_iota(jnp.int32, x.shape, 0)
    return jnp.where(row < t - k, pltpu.roll(x, t - k, axis=0), 0.0)


def _conv_silu_norm(x, w, normalise):
    kk = w.shape[0]
    c = x * w[kk - 1:kk, :]
    for j in range(kk - 1):
        c = c + _shift_down(x, kk - 1 - j) * w[j:j + 1, :]
    sg = _sigmoid(c)
    s = c * sg
    r = lax.rsqrt(jnp.sum(s * s, axis=-1, keepdims=True) + NORM_EPS)
    y = jnp.where(normalise, s * r, s)
    return c, sg, s, r, y


def _gdn_pre(proj, conv_w, heads):
    t = proj.shape[0]
    kk = conv_w.shape[0]

    def body(x_ref, w_ref, o_ref):
        normalise = pl.program_id(0) < 2
        o_ref[...] = _conv_silu_norm(x_ref[...], w_ref[...], normalise)[4]

    col = lambda s, h: (0, s * heads + h)
    return pl.pallas_call(
        body, name="gdn_pre", grid=(3, heads),
        in_specs=[pl.BlockSpec((t, HEAD_DIM), col), pl.BlockSpec((kk, HEAD_DIM), col)],
        out_specs=pl.BlockSpec((t, HEAD_DIM), col),
        out_shape=jax.ShapeDtypeStruct((t, 3 * heads * HEAD_DIM), F32),
        compiler_params=_params("parallel", "parallel"),
    )(proj, conv_w)


def _gdn_pre_backward(proj, conv_w, dqkv, heads):
    t = proj.shape[0]
    kk = conv_w.shape[0]

    def body(x_ref, w_ref, dy_ref, dx_ref, dw_ref):
        normalise = pl.program_id(0) < 2
        x = x_ref[...]
        w = w_ref[...]
        dy = dy_ref[...]
        c, sg, s, r, y = _conv_silu_norm(x, w, normalise)
        ds_norm = r * (dy - y * jnp.sum(dy * y, axis=-1, keepdims=True))
        ds = jnp.where(normalise, ds_norm, dy)
        dc = ds * (sg * (1.0 + c * (1.0 - sg)))
        dx = dc * w[kk - 1:kk, :]
        rows = [None] * kk
        rows[kk - 1] = jnp.sum(dc * x, axis=0, keepdims=True)
        for j in range(kk - 1):
            lag = kk - 1 - j
            dx = dx + _shift_up(dc, lag) * w[j:j + 1, :]
            rows[j] = jnp.sum(dc * _shift_down(x, lag), axis=0, keepdims=True)
        dx_ref[...] = dx.astype(BF16)
        dw_ref[...] = jnp.concatenate(rows, axis=0)

    col = lambda s, h: (0, s * heads + h)
    return pl.pallas_call(
        body, name="gdn_pre_bwd", grid=(3, heads),
        in_specs=[pl.BlockSpec((t, HEAD_DIM), col), pl.BlockSpec((kk, HEAD_DIM), col),
                  pl.BlockSpec((t, HEAD_DIM), col)],
        out_specs=[pl.BlockSpec((t, HEAD_DIM), col), pl.BlockSpec((kk, HEAD_DIM), col)],
        out_shape=[jax.ShapeDtypeStruct((t, 3 * heads * HEAD_DIM), BF16),
                   jax.ShapeDtypeStruct((kk, 3 * heads * HEAD_DIM), F32)],
        compiler_params=_params("parallel", "parallel"),
    )(proj, conv_w, dqkv)


def _gate_vectors(a_log, dt_bias, heads):
    pad = lambda v: jnp.pad(v.astype(F32), ((0, 0), (heads, HEAD_DIM - 2 * heads)))
    return pad(jnp.exp(a_log.astype(F32))), pad(dt_bias)


def _softplus(x):
    return jnp.maximum(x, 0.0) + jnp.log(1.0 + jnp.exp(-jnp.abs(x)))


def _gates_epilogue(heads):
    def epi(acc, ex, out, i):
        lane = lax.broadcasted_iota(jnp.int32, acc.shape, 1)
        beta = _sigmoid(acc)
        g = -ex[0][...] * _softplus(acc + ex[1][...])
        out[0][...] = acc
        out[1][...] = jnp.where(lane < heads, beta, jnp.where(lane < 2 * heads, g, 0.0))
    return epi


def _gates_backward(ba, bg, dbg, ea, dtb, heads):
    t = ba.shape[0]

    def body(ba_ref, bg_ref, d_ref, ea_ref, dt_ref, dba_ref, dal_ref, ddt_ref):
        lane = lax.broadcasted_iota(jnp.int32, (t, HEAD_DIM), 1)
        bgv = bg_ref[...]
        d = d_ref[...]
        db = d * bgv * (1.0 - bgv)
        da = -d * ea_ref[...] * _sigmoid(ba_ref[...] + dt_ref[...])
        is_g = jnp.logical_and(lane >= heads, lane < 2 * heads)
        dba = jnp.where(lane < heads, db, jnp.where(is_g, da, 0.0))
        dba_ref[...] = dba.astype(BF16)
        dal_ref[...] = jnp.sum(jnp.where(is_g, d * bgv, 0.0), axis=0, keepdims=True)
        ddt_ref[...] = jnp.sum(jnp.where(is_g, da, 0.0), axis=0, keepdims=True)

    full = pl.BlockSpec((t, HEAD_DIM), lambda: (0, 0))
    vec = pl.BlockSpec((1, HEAD_DIM), lambda: (0, 0))
    return pl.pallas_call(
        body, name="gates_bwd", grid=(),
        in_specs=[full, full, full, vec, vec], out_specs=[full, vec, vec],
        out_shape=[jax.ShapeDtypeStruct((t, HEAD_DIM), BF16), jax.ShapeDtypeStruct((1, HEAD_DIM), F32),
                   jax.ShapeDtypeStruct((1, HEAD_DIM), F32)],
        compiler_params=pltpu.CompilerParams(vmem_limit_bytes=VMEM_LIMIT),
    )(ba, bg, dbg, ea, dtb)


class _Chunk:
    pass


def _split2(x):
    hi = x.astype(BF16)
    return hi, (x - hi.astype(F32)).astype(BF16)


def _split3(x):
    hi = x.astype(BF16)
    rest = x - hi.astype(F32)
    mid = rest.astype(BF16)
    return hi, mid, (rest - mid.astype(F32)).astype(BF16)


def _dot_mask(mask, x, ta=False):
    hi, mid, lo = _split3(x)
    return _bdot(mask, hi, ta=ta) + (_bdot(mask, mid, ta=ta) + _bdot(mask, lo, ta=ta))


def _transpose_by_identity(x):
    r = x.shape[0]
    eye = (lax.broadcasted_iota(jnp.int32, (r, r), 0) == lax.broadcasted_iota(jnp.int32, (r, r), 1)).astype(BF16)
    hi, mid, lo = _split3(x)
    return _bdot(hi, eye, ta=True) + (_bdot(mid, eye, ta=True) + _bdot(lo, eye, ta=True))


def _dot22(a, b, ta=False, tb=False):
    ah, al = _split2(a)
    bh, bl = _split2(b)
    return _bdot(ah, bh, ta, tb) + (_bdot(ah, bl, ta, tb) + _bdot(al, bh, ta, tb))


def _chunk_gates(bg, heads):
    n = CHUNK
    row = lax.broadcasted_iota(jnp.int32, (n, n), 0)
    col = lax.broadcasted_iota(jnp.int32, (n, n), 1)
    lane = lax.broadcasted_iota(jnp.int32, bg.shape, 1)
    graw = jnp.where(jnp.logical_and(lane >= heads, lane < 2 * heads), bg, 0.0)
    gc = _dot_mask((row >= col).astype(BF16), graw)
    return gc, _transpose_by_identity(gc)


def _in_lockstep(generators):
    results = [None] * len(generators)
    live = list(enumerate(generators))
    while live:
        still = []
        for i, gen in live:
            try:
                next(gen)
                still.append((i, gen))
            except StopIteration as stop:
                results[i] = stop.value
        live = still
    return results


def _chunk_local(q, k, v, beta, gc, grow, solved=None):
    c = _Chunk()
    n = CHUNK
    row = lax.broadcasted_iota(jnp.int32, (n, n), 0)
    col = lax.broadcasted_iota(jnp.int32, (n, n), 1)
    c.tri = row >= col
    c.strict = row > col
    eye = row == col
    c.gcb = jnp.broadcast_to(gc, (n, HEAD_DIM))
    c.decay = jnp.where(c.tri, jnp.exp(jnp.where(c.tri, gc - grow, 0.0)), 0.0)
    c.eg = jnp.exp(c.gcb)
    glast = c.gcb[n - 1:n, :]
    c.egl = jnp.exp(glast)
    c.ekl = jnp.exp(glast - c.gcb)
    c.beta = beta
    c.q = q * (HEAD_DIM ** -0.5)
    c.k = k
    c.v = v
    c.kb = k * beta
    c.vb = v * beta
    c.kg = c.kb * c.eg
    both = _bdot(jnp.concatenate([c.kb, c.q], axis=0), k, tb=True)
    yield
    c.L = jnp.where(c.strict, both[:n] * c.decay, 0.0)
    c.A = jnp.where(c.tri, both[n:] * c.decay, 0.0)
    if solved is None:
        x = -c.L
        tinv = eye.astype(F32) + x
        p = _dot22(x, x)
        yield
        for _ in range(int(math.log2(n)) - 2):
            both = _dot22(jnp.concatenate([p, tinv], axis=0), p)
            yield
            p, tinv = both[:n], tinv + both[n:]
        c.T = tinv + _dot22(tinv, p)
        yield
        uw = _dot22(c.T, jnp.concatenate([c.vb, c.kg], axis=1))
        yield
        c.u, c.w = uw[:, :HEAD_DIM], uw[:, HEAD_DIM:]
    else:
        c.T, c.u, c.w = solved
    c.qg = c.q * c.eg
    c.kdec = k * c.ekl
    return c


def _gdn_core(qkv, bg, heads):
    t = qkv.shape[0]
    nchunk = t // CHUNK

    gw = heads * HEAD_DIM

    def body(qkv_ref, bg_ref, o_ref, s_ref, t_ref, u_ref, w_ref, state):
        @pl.when(pl.program_id(0) == 0)
        def _():
            state[...] = jnp.zeros_like(state)

        bg_v = bg_ref[...]
        gc_all, gc_rows = _chunk_gates(bg_v, heads)
        def one_head(h):
            col = lambda s: pl.ds(s * gw + h * HEAD_DIM, HEAD_DIM)
            c = yield from _chunk_local(qkv_ref[:, col(0)], qkv_ref[:, col(1)], qkv_ref[:, col(2)], bg_v[:, h:h + 1],
                                        gc_all[:, heads + h:heads + h + 1], gc_rows[heads + h:heads + h + 1, :])
            s0 = state[h]
            v_new = c.u - _bdot(c.w, s0)
            yield
            o = _bdot(c.qg, s0) + _bdot(c.A, v_new)
            return s0, o, s0 * c.egl + _bdot(c.kdec, v_new, ta=True), c

        results = _in_lockstep([one_head(h) for h in range(heads)])
        for h, (s0, o, s1, c) in enumerate(results):
            lanes = pl.ds(h * HEAD_DIM, HEAD_DIM)
            s_ref[h, 0] = s0
            o_ref[:, lanes] = o
            t_ref[:, lanes] = jnp.concatenate([c.T, jnp.zeros((CHUNK, HEAD_DIM - CHUNK), F32)], axis=1)
            u_ref[:, lanes] = c.u
            w_ref[:, lanes] = c.w
            state[h] = s1

    return pl.pallas_call(
        body, name="gdn_core", grid=(nchunk,),
        in_specs=[pl.BlockSpec((CHUNK, 3 * gw), lambda n: (n, 0)), pl.BlockSpec((CHUNK, HEAD_DIM), lambda n: (n, 0))],
        out_specs=[pl.BlockSpec((CHUNK, gw), lambda n: (n, 0)),
                   pl.BlockSpec((heads, 1, HEAD_DIM, HEAD_DIM), lambda n: (0, n, 0, 0))]
        + [pl.BlockSpec((CHUNK, gw), lambda n: (n, 0))] * 3,
        out_shape=[jax.ShapeDtypeStruct((t, gw), F32),
                   jax.ShapeDtypeStruct((heads, nchunk, HEAD_DIM, HEAD_DIM), F32)]
        + [jax.ShapeDtypeStruct((t, gw), F32)] * 3,
        scratch_shapes=[pltpu.VMEM((heads, HEAD_DIM, HEAD_DIM), F32)],
        compiler_params=_params("arbitrary"),
    )(qkv, bg)


def _gdn_core_backward(qkv, bg, states, solved, do, heads):
    t = qkv.shape[0]
    nchunk = t // CHUNK
    n = CHUNK

    def one_head(chunk_local, s0, d_out, ds1):
        c = yield from chunk_local
        v_new = c.u - _bdot(c.w, s0)
        dqg = _bdot(d_out, s0, tb=True)
        ds0 = _bdot(c.qg, d_out, ta=True) + ds1 * c.egl
        dv_new = _bdot(c.A, d_out, ta=True) + _bdot(c.kdec, ds1)
        yield
        dA = jnp.where(c.tri, _bdot(d_out, v_new, tb=True), 0.0)
        dkdec = _bdot(v_new, ds1, tb=True)
        dgl = jnp.sum(jnp.sum(ds1 * s0, axis=1, keepdims=True), axis=0, keepdims=True) * c.egl
        dw = -_bdot(dv_new, s0, tb=True)
        ds0 = ds0 - _bdot(c.w, dv_new, ta=True)
        yield
        both = _dot22(c.T, jnp.concatenate([dv_new, dw], axis=1), ta=True)
        yield
        dvb, dkg = both[:, :HEAD_DIM], both[:, HEAD_DIM:]
        dL = jnp.where(c.strict, -(_bdot(dvb, c.u, tb=True) + _bdot(dkg, c.w, tb=True)), 0.0)
        yield
        dm1 = dL * c.decay
        dkb = _bdot(dm1, c.k) + dkg * c.eg
        dk = _bdot(dm1, c.kb, ta=True)
        dm2 = dA * c.decay
        dq = _bdot(dm2, c.k) + dqg * c.eg
        dk = dk + _bdot(dm2, c.q, ta=True) + dkdec * c.ekl + dkb * c.beta
        pm = dL * c.L + dA * c.A
        ones = jnp.ones((n, HEAD_DIM), BF16)
        pm_hi, pm_lo = _split2(pm)
        colsum = _bdot(pm_hi, ones, ta=True) + _bdot(pm_lo, ones, ta=True)
        tk_ = jnp.sum(dkdec * c.kdec, axis=1, keepdims=True)
        dgc = (jnp.sum(pm, axis=1, keepdims=True) - colsum
               + jnp.sum(dqg * c.qg, axis=1, keepdims=True)
               - tk_
               + jnp.sum(dkg * c.kg, axis=1, keepdims=True))
        dgl = dgl + jnp.sum(tk_, axis=0, keepdims=True)
        rowi = lax.broadcasted_iota(jnp.int32, (n, HEAD_DIM), 0)
        dgc = dgc + jnp.where(rowi == n - 1, dgl, 0.0)
        dbeta = jnp.sum(dkb * c.k, axis=1, keepdims=True) + jnp.sum(dvb * c.v, axis=1, keepdims=True)
        return dq * (HEAD_DIM ** -0.5), dk, dvb * c.beta, dbeta, dgc, ds0

    gw = heads * HEAD_DIM

    def body(qkv_ref, bg_ref, s_ref, t_ref, u_ref, w_ref, do_ref, dqkv_ref, dbg_ref, dstate):
        @pl.when(pl.program_id(0) == 0)
        def _():
            dstate[...] = jnp.zeros_like(dstate)

        bg_v = bg_ref[...]
        gc_all, gc_rows = _chunk_gates(bg_v, heads)
        lane = lax.broadcasted_iota(jnp.int32, (n, HEAD_DIM), 1)
        dgates = jnp.zeros((n, HEAD_DIM), F32)
        chains = []
        for h in range(heads):
            col = lambda s: pl.ds(s * gw + h * HEAD_DIM, HEAD_DIM)
            lanes = pl.ds(h * HEAD_DIM, HEAD_DIM)
            c = _chunk_local(qkv_ref[:, col(0)], qkv_ref[:, col(1)], qkv_ref[:, col(2)], bg_v[:, h:h + 1],
                             gc_all[:, heads + h:heads + h + 1], gc_rows[heads + h:heads + h + 1, :],
                             (t_ref[:, pl.ds(h * HEAD_DIM, CHUNK)], u_ref[:, lanes], w_ref[:, lanes]))
            chains.append(one_head(c, s_ref[h, 0], do_ref[:, pl.ds(h * HEAD_DIM, HEAD_DIM)], dstate[h]))
        results = _in_lockstep(chains)
        for h, (dq, dk, dv, dbeta, dgc, ds0) in enumerate(results):
            dgates = jnp.where(lane == h, dbeta, jnp.where(lane == heads + h, dgc, dgates))
        for h, (dq, dk, dv, dbeta, dgc, ds0) in enumerate(results):
            dqkv_ref[:, pl.ds(h * HEAD_DIM, HEAD_DIM)] = dq
            dqkv_ref[:, pl.ds(gw + h * HEAD_DIM, HEAD_DIM)] = dk
            dqkv_ref[:, pl.ds(2 * gw + h * HEAD_DIM, HEAD_DIM)] = dv
            dstate[h] = ds0
        row = lax.broadcasted_iota(jnp.int32, (n, n), 0)
        colm = lax.broadcasted_iota(jnp.int32, (n, n), 1)
        draw = _dot_mask((row >= colm).astype(BF16), dgates, ta=True)
        dbg_ref[...] = jnp.where(lane < heads, dgates, draw)

    last = nchunk - 1
    return pl.pallas_call(
        body, name="gdn_core_bwd", grid=(nchunk,),
        in_specs=[pl.BlockSpec((CHUNK, 3 * gw), lambda i: (last - i, 0)),
                  pl.BlockSpec((CHUNK, HEAD_DIM), lambda i: (last - i, 0)),
                  pl.BlockSpec((heads, 1, HEAD_DIM, HEAD_DIM), lambda i: (0, last - i, 0, 0))]
        + [pl.BlockSpec((CHUNK, gw), lambda i: (last - i, 0))] * 4,
        out_specs=[pl.BlockSpec((CHUNK, 3 * gw), lambda i: (last - i, 0)),
                   pl.BlockSpec((CHUNK, HEAD_DIM), lambda i: (last - i, 0))],
        out_shape=[jax.ShapeDtypeStruct((t, 3 * gw), F32), jax.ShapeDtypeStruct((t, HEAD_DIM), F32)],
        scratch_shapes=[pltpu.VMEM((heads, HEAD_DIM, HEAD_DIM), F32)],
        compiler_params=_params("arbitrary"),
    )(qkv, bg, states, *solved, do)


def _gdn_post(o, proj, z_col0, norm_w, heads, tt):
    t = o.shape[0]
    zb = z_col0 // HEAD_DIM

    def body(o_ref, z_ref, w_ref, out_ref):
        ov = o_ref[...]
        z = z_ref[...]
        rms = lax.rsqrt(jnp.mean(ov * ov, axis=-1, keepdims=True) + NORM_EPS)
        out_ref[...] = (ov * rms * w_ref[...] * (z * _sigmoid(z))).astype(BF16)

    return pl.pallas_call(
        body, name="gdn_post", grid=(t // tt, heads),
        in_specs=[pl.BlockSpec((tt, HEAD_DIM), lambda i, h: (i, h)),
                  pl.BlockSpec((tt, HEAD_DIM), lambda i, h: (i, zb + h)),
                  pl.BlockSpec((1, HEAD_DIM), lambda i, h: (0, 0))],
        out_specs=pl.BlockSpec((tt, HEAD_DIM), lambda i, h: (i, h)),
        out_shape=jax.ShapeDtypeStruct((t, heads * HEAD_DIM), BF16),
        compiler_params=_params("parallel", "parallel"),
    )(o, proj, norm_w)


def _gdn_post_backward(dcat, o, proj, z_col0, norm_w, heads, tt):
    t = o.shape[0]
    zb = z_col0 // HEAD_DIM

    def body(d_ref, o_ref, z_ref, w_ref, do_ref, dz_ref, dw_ref):
        d = d_ref[...]
        ov = o_ref[...]
        z = z_ref[...]
        w = w_ref[...]
        rms = lax.rsqrt(jnp.mean(ov * ov, axis=-1, keepdims=True) + NORM_EPS)
        ohat = ov * rms
        sg = _sigmoid(z)
        gate = z * sg
        dz_ref[...] = (d * ohat * w * (sg * (1.0 + z * (1.0 - sg)))).astype(BF16)
        don = d * gate
        dohat = don * w
        do_ref[...] = rms * (dohat - ohat * jnp.mean(dohat * ohat, axis=-1, keepdims=True))
        dw = jnp.sum(don * ohat, axis=0, keepdims=True)
        first = jnp.logical_and(pl.program_id(0) == 0, pl.program_id(1) == 0)

        @pl.when(first)
        def _():
            dw_ref[...] = dw

        @pl.when(jnp.logical_not(first))
        def _():
            dw_ref[...] += dw

    blk = pl.BlockSpec((tt, HEAD_DIM), lambda i, h: (i, h))
    return pl.pallas_call(
        body, name="gdn_post_bwd", grid=(t // tt, heads),
        in_specs=[blk, blk, pl.BlockSpec((tt, HEAD_DIM), lambda i, h: (i, zb + h)),
                  pl.BlockSpec((1, HEAD_DIM), lambda i, h: (0, 0))],
        out_specs=[blk, blk, pl.BlockSpec((1, HEAD_DIM), lambda i, h: (0, 0))],
        out_shape=[jax.ShapeDtypeStruct((t, heads * HEAD_DIM), F32),
                   jax.ShapeDtypeStruct((t, heads * HEAD_DIM), BF16),
                   jax.ShapeDtypeStruct((1, HEAD_DIM), F32)],
        compiler_params=_params("arbitrary", "arbitrary"),
    )(dcat, o, proj, norm_w)


def _pool_select(levels, group):
    out = levels[-1]
    for gi in range(len(levels) - 2, -1, -1):
        out = jnp.where(group == gi, levels[gi], out)
    return out


def _pool_counts(t, width, group):
    pos = lax.broadcasted_iota(jnp.int32, (t, width), 0)
    win = jnp.left_shift(2, group)
    return jnp.minimum(pos + 1, win).astype(F32)


def _pooled(p, group):
    levels, s, step = [], p, 1
    for _ in POOL_WINDOWS:
        s = s + _shift_down(s, step)
        levels.append(s)
        step *= 2
    cnt = _pool_counts(p.shape[0], p.shape[1], group)
    return _pool_select(levels, group) / cnt - p, cnt


def _pool_forward(proj, p_col0, pool_w, pool_scale):
    t = proj.shape[0]
    groups, cg, _ = pool_w.shape
    pb = p_col0 // cg

    def body(p_ref, w_ref, s_ref, o_ref):
        pooled, _ = _pooled(p_ref[...], pl.program_id(0))
        o_ref[...] = (_bdot(pooled, w_ref[0]) * s_ref[...]).astype(BF16)

    return pl.pallas_call(
        body, name="pool_fwd", grid=(groups,),
        in_specs=[pl.BlockSpec((t, cg), lambda g: (0, pb + g)), pl.BlockSpec((1, cg, cg), lambda g: (g, 0, 0)),
                  pl.BlockSpec((1, cg), lambda g: (0, g))],
        out_specs=pl.BlockSpec((t, cg), lambda g: (0, g)),
        out_shape=jax.ShapeDtypeStruct((t, groups * cg), BF16),
        compiler_params=_params("parallel"),
    )(proj, pool_w, pool_scale)


def _pool_backward(dcat, d_col0, proj, p_col0, pool_w, pool_scale):
    t = proj.shape[0]
    groups, cg, _ = pool_w.shape
    pb = p_col0 // cg
    db = d_col0 // cg

    def body(d_ref, p_ref, w_ref, s_ref, dp_ref, dw_ref, ds_ref):
        group = pl.program_id(0)
        pooled, cnt = _pooled(p_ref[...], group)
        w = w_ref[0]
        d = d_ref[...]
        mixed = _bdot(pooled, w)
        ds_ref[...] = jnp.sum(d * mixed, axis=0, keepdims=True)
        dmixed = d * s_ref[...]
        dw_ref[0] = _bdot(pooled, dmixed, ta=True)
        dpooled = _bdot(dmixed, w, tb=True)
        levels, s, step = [], dpooled / cnt, 1
        for _ in POOL_WINDOWS:
            s = s + _shift_up(s, step)
            levels.append(s)
            step *= 2
        dp_ref[...] = (_pool_select(levels, group) - dpooled).astype(BF16)

    return pl.pallas_call(
        body, name="pool_bwd", grid=(groups,),
        in_specs=[pl.BlockSpec((t, cg), lambda g: (0, db + g)), pl.BlockSpec((t, cg), lambda g: (0, pb + g)),
                  pl.BlockSpec((1, cg, cg), lambda g: (g, 0, 0)), pl.BlockSpec((1, cg), lambda g: (0, g))],
        out_specs=[pl.BlockSpec((t, cg), lambda g: (0, g)), pl.BlockSpec((1, cg, cg), lambda g: (g, 0, 0)),
                   pl.BlockSpec((1, cg), lambda g: (0, g))],
        out_shape=[jax.ShapeDtypeStruct((t, groups * cg), BF16), jax.ShapeDtypeStruct((groups, cg, cg), F32),
                   jax.ShapeDtypeStruct((1, groups * cg), F32)],
        compiler_params=_params("parallel"),
    )(dcat, proj, pool_w, pool_scale)


def _attention(q, k, v, tq):
    t, d = q.shape
    m = k.shape[0]
    dh = d // XATTN_HEADS
    scale = dh ** -0.5

    def body(q_ref, k_ref, v_ref, o_ref):
        s = _bdot(q_ref[...], k_ref[...], tb=True) * scale
        s = s - jnp.max(s, axis=-1, keepdims=True)
        e = jnp.exp(s)
        p = e / jnp.sum(e, axis=-1, keepdims=True)
        o_ref[...] = _bdot(p, v_ref[...]).astype(BF16)

    return pl.pallas_call(
        body, name="xattn_fwd", grid=(XATTN_HEADS, t // tq),
        in_specs=[pl.BlockSpec((tq, dh), lambda h, i: (i, h)), pl.BlockSpec((m, dh), lambda h, i: (0, h)),
                  pl.BlockSpec((m, dh), lambda h, i: (0, h))],
        out_specs=pl.BlockSpec((tq, dh), lambda h, i: (i, h)),
        out_shape=jax.ShapeDtypeStruct((t, d), BF16),
        compiler_params=_params("parallel", "parallel"),
    )(q, k, v)


def _attention_backward(q, k, v, do, tq):
    t, d = q.shape
    m = k.shape[0]
    dh = d // XATTN_HEADS
    scale = dh ** -0.5

    def body(q_ref, k_ref, v_ref, do_ref, dq_ref, dk_ref, dv_ref, dk_acc, dv_acc):
        i = pl.program_id(1)
        qv, kv, vv, dov = q_ref[...], k_ref[...], v_ref[...], do_ref[...]
        s = _bdot(qv, kv, tb=True) * scale
        s = s - jnp.max(s, axis=-1, keepdims=True)
        e = jnp.exp(s)
        p = e / jnp.sum(e, axis=-1, keepdims=True)
        dp = _bdot(dov, vv, tb=True)
        ds = p * (dp - jnp.sum(dp * p, axis=-1, keepdims=True)) * scale
        dq_ref[...] = _bdot(ds, kv).astype(BF16)
        dv_part = _bdot(p, dov, ta=True)
        dk_part = _bdot(ds, qv, ta=True)

        @pl.when(i == 0)
        def _():
            dk_acc[...] = dk_part
            dv_acc[...] = dv_part

        @pl.when(i > 0)
        def _():
            dk_acc[...] += dk_part
            dv_acc[...] += dv_part

        @pl.when(i == pl.num_programs(1) - 1)
        def _():
            dk_ref[...] = dk_acc[...].astype(BF16)
            dv_ref[...] = dv_acc[...].astype(BF16)

    qblk = pl.BlockSpec((tq, dh), lambda h, i: (i, h))
    kblk = pl.BlockSpec((m, dh), lambda h, i: (0, h))
    return pl.pallas_call(
        body, name="xattn_bwd", grid=(XATTN_HEADS, t // tq),
        in_specs=[qblk, kblk, kblk, qblk],
        out_specs=[qblk, kblk, kblk],
        out_shape=[jax.ShapeDtypeStruct((t, d), BF16), jax.ShapeDtypeStruct((m, d), BF16),
                   jax.ShapeDtypeStruct((m, d), BF16)],
        scratch_shapes=[pltpu.VMEM((m, dh), F32), pltpu.VMEM((m, dh), F32)],
        compiler_params=_params("parallel", "arbitrary"),
    )(q, k, v, do)


def _ln_backward_rows(name, dmain, dres, xhat, rstd, gamma, tm):
    t, d = xhat.shape

    def body(m_ref, r_ref, x_ref, s_ref, g_ref, du_ref, dub_ref, dg_ref, db_ref):
        du, dg, db = _ln_backward_math(m_ref[...] + ALPHA * r_ref[...], x_ref[...], s_ref[...], g_ref[...])
        du_ref[...] = du
        dub_ref[...] = du.astype(BF16)
        first = pl.program_id(0) == 0

        @pl.when(first)
        def _():
            dg_ref[...] = dg
            db_ref[...] = db

        @pl.when(jnp.logical_not(first))
        def _():
            dg_ref[...] += dg
            db_ref[...] += db

    row = pl.BlockSpec((tm, d), lambda i: (i, 0))
    vec = pl.BlockSpec((1, d), lambda i: (0, 0))
    return pl.pallas_call(
        body, name=name, grid=(t // tm,),
        in_specs=[row, row, row, pl.BlockSpec((tm, 1), lambda i: (i, 0)), vec],
        out_specs=[row, row, vec, vec],
        out_shape=[jax.ShapeDtypeStruct((t, d), F32), jax.ShapeDtypeStruct((t, d), BF16),
                   jax.ShapeDtypeStruct((1, d), F32), jax.ShapeDtypeStruct((1, d), F32)],
        compiler_params=_params("arbitrary"),
    )(dmain, dres, xhat, rstd, gamma)


def _loss_and_ln_backward(xhat, rstd, gamma, beta, target, tm):
    t, d = xhat.shape

    def body(x_ref, r_ref, g_ref, b_ref, t_ref, du_ref, dub_ref, dg_ref, db_ref, loss_ref):
        xh = x_ref[...]
        g = g_ref[...]
        diff = xh * g + b_ref[...] - t_ref[...]
        part = jnp.sum(jnp.sum(diff * diff, axis=1, keepdims=True), axis=0, keepdims=True) * (0.5 / d)
        dy = diff * (1.0 / d)
        du, dg, db = _ln_backward_math(dy, xh, r_ref[...], g)
        du_ref[...] = du
        dub_ref[...] = du.astype(BF16)
        lossrow = jnp.broadcast_to(part, (1, HEAD_DIM))
        first = pl.program_id(0) == 0

        @pl.when(first)
        def _():
            dg_ref[...] = dg
            db_ref[...] = db
            loss_ref[...] = lossrow

        @pl.when(jnp.logical_not(first))
        def _():
            dg_ref[...] += dg
            db_ref[...] += db
            loss_ref[...] += lossrow

    row = pl.BlockSpec((tm, d), lambda i: (i, 0))
    vec = pl.BlockSpec((1, d), lambda i: (0, 0))
    return pl.pallas_call(
        body, name="loss_ln3_bwd", grid=(t // tm,),
        in_specs=[row, pl.BlockSpec((tm, 1), lambda i: (i, 0)), vec, vec, row],
        out_specs=[row, row, vec, vec, pl.BlockSpec((1, HEAD_DIM), lambda i: (0, 0))],
        out_shape=[jax.ShapeDtypeStruct((t, d), F32), jax.ShapeDtypeStruct((t, d), BF16),
                   jax.ShapeDtypeStruct((1, d), F32), jax.ShapeDtypeStruct((1, d), F32),
                   jax.ShapeDtypeStruct((1, HEAD_DIM), F32)],
        compiler_params=_params("arbitrary"),
    )(xhat, rstd, gamma, beta, target)


def _after(token, a):
    return a if token is None else a + token[:1, :1].astype(a.dtype)


def _pick(n, prefs):
    for p in prefs:
        if n % p == 0:
            return p
    return n


def _local_step(x, mem, target, w, x_bf=None):
    t, d = x.shape
    heads = w["a_log"].shape[1]
    gw = heads * HEAD_DIM
    groups, cg, _ = w["pool_w"].shape
    pw = groups * cg
    n_main = 4 * gw + pw
    in_cols = n_main + 2 * heads
    s_in = w["w_in_t"].shape[0]

    tm = _pick(t, (512, 256, 128))
    tm_ln = _pick(t, (256, 128))
    tm_big = _pick(t, (1024, 512, 256, 128))
    tk = _pick(d, K_STEPS)

    w_in_t = w["w_in_t"].reshape(in_cols, d)
    w_p_t = w_in_t[4 * gw + 2 * heads:]
    w_ba_t = jnp.pad(w_in_t[4 * gw:4 * gw + 2 * heads], ((0, HEAD_DIM - 2 * heads), (0, 0)))
    x_bf = x.astype(BF16) if x_bf is None else x_bf
    mem_bf = mem.astype(BF16)

    tn_d = _pick(d, (1024, 512, 256, 128))
    proj = _plain("proj_main", x_bf, w_in_t, tb=True, n_used=4 * gw, tm=tm_big, tn=_pick(4 * gw, (1024, 512, 256, 128)),
                  tk=tk, out_dtype=F32)
    pproj = _plain("proj_pool", x_bf, w_p_t, tb=True, tm=tm_big, tn=_pick(pw, (1024, 512, 256, 128)), tk=tk, out_dtype=F32)
    ea, dtb = _gate_vectors(w["a_log"], w["dt_bias"], heads)
    vec128 = lambda i, j: (0, 0)
    ba, bg = _matmul(
        "proj_gates", x_bf, w_ba_t, tb=True, tm=tm, tn=HEAD_DIM, tk=tk,
        extra=[(ea, (1, HEAD_DIM), vec128), (dtb, (1, HEAD_DIM), vec128)],
        outs=[(jax.ShapeDtypeStruct((t, HEAD_DIM), F32), (tm, HEAD_DIM), _tile)] * 2,
        epilogue=_gates_epilogue(heads))
    qkv = _gdn_pre(proj, w["conv_w"], heads)
    o_gdn, states, *solved = _gdn_core(qkv, bg, heads)
    cat_g = _gdn_post(o_gdn, proj, 3 * gw, w["gdn_norm_w"], heads, tm)
    cat_p = _pool_forward(pproj, 0, w["pool_w"], w["pool_scale"])
    cat = jnp.concatenate([cat_g, cat_p], axis=1)
    w = {**w, **(yield ("weights", 1, cat))}
    h1, h1_bf, xhat1, rstd1 = _ln_forward("mix_ln1", cat, w["w_out"], x, w["ln1_g"], w["ln1_b"], tm=tm_ln, tk=tk)

    h1_bf = _after((yield ("relay", None, h1_bf)), h1_bf)
    q = _plain("xattn_q", h1_bf, w["xq_w"], tm=tm, tn=tn_d, tk=tk, out_dtype=BF16)
    mlen = mem.shape[0]
    tm_mem = _pick(mlen, (256, 128))
    k = _plain("xattn_k", mem_bf, w["xk_w"], tm=tm_mem, tn=tn_d, tk=tk, out_dtype=BF16)
    v = _plain("xattn_v", mem_bf, w["xv_w"], tm=tm_mem, tn=tn_d, tk=tk, out_dtype=BF16)
    att = _attention(q, k, v, tm)
    h2, h2_bf, xhat2, rstd2 = _ln_forward("xo_ln2", att, w["xo_w"], h1, w["ln2_g"], w["ln2_b"], tm=tm_ln, tk=tk)

    w = {**w, **(yield ("weights", 2, h2_bf))}
    s_up = w["w_up3"].shape[0]
    ff = s_up * w["w_up3"].shape[2]
    tn_f = _pick(ff // s_up, (1024, 512, 256, 128))

    def up_epi(acc, ex, out, i):
        r = jnp.maximum(acc, 0.0)
        out[0][...] = (r * r).astype(BF16)
        out[1][...] = (2.0 * r).astype(BF16)

    act, act_grad = _matmul(
        "mlp_up", h2_bf, w["w_up3"], b_blocks=s_up, tm=tm_big, tn=tn_f, tk=tk,
        outs=[(jax.ShapeDtypeStruct((t, ff), BF16), (tm_big, tn_f), _tile)] * 2, epilogue=up_epi)
    w = {**w, **(yield ("weights", 3, act))}
    tk_f = _pick(ff, K_STEPS)
    xhat3, rstd3 = _ln_forward("down_ln3", act, w["w_down"], h2, w["ln3_g"], w["ln3_b"], tm=tm, tk=tk_f, want_h=False)

    grads = {}
    du3, du3_bf, grads["ln3_g"], grads["ln3_b"], loss = _loss_and_ln_backward(
        xhat3, rstd3, w["ln3_g"], w["ln3_b"], target, tm_ln)

    def dup_epi(acc, ex, out, i):
        out[0][...] = (acc * ex[0][...].astype(F32)).astype(BF16)

    dup = _matmul(
        "mlp_down_dx", du3_bf, w["w_down"], tb=True, tm=tm_big, tn=tn_f, tk=tk,
        extra=[(act_grad, (tm_big, tn_f), _tile)],
        outs=[(jax.ShapeDtypeStruct((t, ff), BF16), (tm_big, tn_f), _tile)], epilogue=dup_epi)[0]
    tk_t = _pick(t, K_STEPS)
    tm_w = _pick(d, (512, 256, 128))
    grads["w_down"] = _plain("mlp_down_dw", act, du3_bf, ta=True, tm=_pick(ff, (512, 256, 128)), tn=d, tk=tk_t,
                             out_dtype=F32)
    grads["w_up3"] = _plain("mlp_up_dw", h2_bf, dup, ta=True, tm=tm_w, tn=ff // s_up, tk=tk_t, out_dtype=F32, out3=s_up,
                            n_outer=True)
    token = yield ("grads", 0, {n: grads.pop(n) for n in ("w_down", "w_up3")})
    dh2 = _plain("mlp_up_dx", dup, w["w_up3"], tb=True, b_blocks=s_up, tm=tm_big, tn=tn_d,
                 tk=_pick(ff // s_up, K_STEPS), out_dtype=F32)
    du2, du2_bf, grads["ln2_g"], grads["ln2_b"] = _ln_backward_rows(
        "ln2_bwd", dh2, du3, xhat2, rstd2, _after(token, w["ln2_g"]), tm_ln)
    token = yield ("poll", 0, du2_bf)

    grads["xo_w"] = _plain("xo_dw", att, du2_bf, ta=True, tm=tm_w, tn=d, tk=tk_t, out_dtype=F32)
    datt = _plain("xo_dx", du2_bf, w["xo_w"], tb=True, tm=tm, tn=tn_d, tk=tk, out_dtype=BF16)
    dq, dk, dv = _attention_backward(q, k, v, datt, tm)
    tk_m = _pick(mlen, (256, 128))
    grads["xq_w"] = _plain("xq_dw", h1_bf, dq, ta=True, tm=tm_w, tn=d, tk=tk_t, out_dtype=F32)
    grads["xk_w"] = _plain("xk_dw", mem_bf, dk, ta=True, tm=tm_w, tn=tn_d, tk=tk_m, out_dtype=F32)
    grads["xv_w"] = _plain("xv_dw", mem_bf, dv, ta=True, tm=tm_w, tn=tn_d, tk=tk_m, out_dtype=F32)
    du1, du1_bf, grads["ln1_g"], grads["ln1_b"] = _ln_backward(
        "xq_dx_ln1", dq, w["xq_w"], du2, xhat1, rstd1, _after(token, w["ln1_g"]), tm=tm_ln, tk=tk)

    grads["w_out"] = _plain("out_dw", cat, du1_bf, ta=True, tm=tm_w, tn=d, tk=tk_t, out_dtype=F32)
    token = yield ("grads", 1, {n: grads.pop(n) for n in ("xo_w", "xq_w", "xk_w", "xv_w", "w_out")})
    dcat = _plain("out_dx", du1_bf, w["w_out"], tb=True, tm=tm, tn=tn_d, tk=tk, out_dtype=F32)
    dp, grads["pool_w"], grads["pool_scale"] = _pool_backward(dcat, gw, pproj, 0, w["pool_w"],
                                                              _after(token, w["pool_scale"]))
    do_gdn, dz, grads["gdn_norm_w"] = _gdn_post_backward(dcat, o_gdn, proj, 3 * gw, _after(token, w["gdn_norm_w"]),
                                                         heads, tm)
    dqkv, dbg = _gdn_core_backward(qkv, bg, states, solved, do_gdn, heads)
    token = yield ("poll", 1, dqkv)
    dqkv_pre, grads["conv_w"] = _gdn_pre_backward(proj, _after(token, w["conv_w"]), dqkv, heads)
    dba, dalog_row, ddt_row = _gates_backward(ba, bg, dbg, ea, dtb, heads)
    grads["a_log"] = dalog_row[:, heads:2 * heads]
    grads["dt_bias"] = ddt_row[:, heads:2 * heads]

    dproj = jnp.concatenate([dqkv_pre, dz, dp], axis=1)
    dw_main = _plain("proj_dw", dproj, x_bf, ta=True, tm=_pick(n_main, (512, 256, 128)), tn=d, tk=tk_t, out_dtype=F32)
    dw_ba = _plain("proj_gates_dw", dba, x_bf, ta=True, tm=HEAD_DIM, tn=tn_d, tk=tk_t, out_dtype=F32)
    dw_in_t = jnp.concatenate([dw_main[:4 * gw], dw_ba[:2 * heads], dw_main[4 * gw:]], axis=0)
    grads["w_in_t"] = dw_in_t.reshape(s_in, in_cols // s_in, d)

    def dx_epi(acc, ex, out, i):
        out[0][...] = acc + ex[1][...] + ALPHA * ex[0][...]

    def add_epi(acc, ex, out, i):
        out[0][...] = acc + ex[0][...]

    token = yield ("grads", 2, {n: grads.pop(n) for n in ("w_in_t", "pool_w")})
    dx_gates = _plain("proj_gates_dx", dba, _after(token, w_ba_t), tm=tm, tn=tn_d, tk=HEAD_DIM, out_dtype=F32)
    out_tile = [(jax.ShapeDtypeStruct((t, d), F32), (tm, tn_d), _tile)]
    dx_pool = _matmul("proj_pool_dx", dp, w_p_t, tm=tm, tn=tn_d, tk=_pick(pw, K_STEPS),
                      extra=[(dx_gates, (tm, tn_d), _tile)], outs=out_tile, epilogue=add_epi)[0]
    grad_x = _matmul(
        "proj_dx", dproj, w_in_t, k_used=4 * gw, tm=tm, tn=tn_d, tk=_pick(4 * gw, K_STEPS),
        extra=[(du1, (tm, tn_d), _tile), (dx_pool, (tm, tn_d), _tile)], outs=out_tile, epilogue=dx_epi)[0]
    yield ("poll", 2, grad_x)
    return loss, grad_x, grads


def _adamw(name, w, g, m, v):
    r, c = w.shape
    if r % 8 == 0:
        tr = _pick(r, (256, 128, 64, 32, 16, 8))
        blk, steps = pl.BlockSpec((tr, c), lambda i: (i, 0)), r // tr
    else:
        tc = _pick(c, (256, 128))
        blk, steps = pl.BlockSpec((r, tc), lambda i: (0, i)), c // tc
    c1 = 1.0 - ADAM_B1 ** ADAM_STEP
    c2 = 1.0 - ADAM_B2 ** ADAM_STEP

    def body(w_ref, g_ref, m_ref, v_ref, d_ref, mo_ref, vo_ref, go_ref):
        gv = g_ref[...]
        mn = ADAM_B1 * m_ref[...] + (1.0 - ADAM_B1) * gv
        vn = ADAM_B2 * v_ref[...] + (1.0 - ADAM_B2) * (gv * gv)
        d_ref[...] = -ADAM_LR * ((mn / c1) / (jnp.sqrt(vn / c2) + ADAM_EPS) + ADAM_WD * w_ref[...])
        mo_ref[...] = mn
        vo_ref[...] = vn
        go_ref[...] = gv

    return pl.pallas_call(
        body, name=name, grid=(steps,), in_specs=[blk] * 4, out_specs=[blk] * 4,
        out_shape=[jax.ShapeDtypeStruct((r, c), F32)] * 4,
        compiler_params=_params("parallel"),
    )(w, g, m, v)


def _place():
    x, y, c = lax.axis_index("x"), lax.axis_index("y"), lax.axis_index("c")
    chips = [(1 - x, y), (x, 1 - y), (1 - x, 1 - y)]
    return x, y, c, chips


HBM = pl.BlockSpec(memory_space=pltpu.HBM)


SEM = pl.BlockSpec(memory_space=pltpu.SEMAPHORE)
ANY = pl.BlockSpec(memory_space=pl.ANY)
EFFECT = pltpu.SideEffectType.DATAFLOW_SIDE_EFFECTING


def _in_hbm(a):
    return pltpu.with_memory_space_constraint(a, pltpu.HBM)


def _remote(src, dst, send_sem, recv_sem, to):
    return pltpu.make_async_remote_copy(src_ref=src, dst_ref=dst, send_sem=send_sem, recv_sem=recv_sem,
                                        device_id=to, device_id_type=MESH)


def _by_rows(rows):
    return rows % 32 == 0


def _half_shape(rows, cols):
    return (rows // 2, cols) if _by_rows(rows) else (rows, cols // 2)


def _half(ref, which, *lead):
    rows, cols = ref.shape[-2:]
    if _by_rows(rows):
        return ref.at[(*lead, pl.ds(which * (rows // 2), rows // 2))]
    return ref.at[(*lead, slice(None), pl.ds(which * (cols // 2), cols // 2))]


def _landed(lands, i, shard_index, which):
    return _half(lands[i], which, shard_index)


def _routes():
    x, y, c, _ = _place()
    first = (jnp.where(c == 0, 1 - x, x), jnp.where(c == 0, y, 1 - y))
    second = (jnp.where(c == 0, x, 1 - x), jnp.where(c == 0, 1 - y, y))
    return first, second, (1 - x, 1 - y)


def _shard_of(chip):
    return 2 * chip[0] + chip[1]


def _gather_start(name, shards, after, relayed=()):
    n = len(shards)
    lands = [lax.empty((N_SHARD,) + s.shape, s.dtype) for s in shards]

    def body(*refs):
        ins, zones = refs[:n], refs[n:2 * n]
        ici_send, ici_recv, own_send, own_recv = refs[2 * n + 1:2 * n + 5]
        token = refs[-1]
        x, y, c, chips = _place()
        me = 2 * x + y
        first, _, _ = _routes()
        for i in range(n):
            if i in relayed:
                _remote(_half(ins[i], c), _landed(zones, i, me, c), ici_send.at[3 * i], ici_recv.at[3 * i],
                        (*first, c)).start()
                continue
            for j, chip in enumerate(chips):
                _remote(_half(ins[i], c), _landed(zones, i, me, c), ici_send.at[3 * i + j],
                        ici_recv.at[3 * i + j], (*chip, c)).start()
        for i in range(n):
            _remote(ins[i], zones[i].at[me], own_send.at[i], own_recv.at[i], (x, y, 1 - c)).start()
        token[...] = jnp.zeros_like(token)

    dma = pltpu.SemaphoreType.DMA
    outs = pl.pallas_call(
        body, name=name,
        in_specs=[HBM] * (2 * n) + [ANY],
        out_shape=(dma((3 * n,)), dma((3 * n,)), dma((n,)), dma((n,)),
                   *[pltpu.HBM(a.shape, a.dtype) for a in shards + lands], jax.ShapeDtypeStruct((8, LANES), F32)),
        out_specs=(SEM, SEM, SEM, SEM, *[HBM] * (2 * n), pl.BlockSpec(memory_space=pltpu.VMEM)),
        input_output_aliases={k: 4 + k for k in range(2 * n)},
        compiler_params=pltpu.CompilerParams(has_side_effects=EFFECT),
    )(*[_in_hbm(a) for a in shards + lands], after)
    sems = dict(zip(("ici_send", "ici_recv", "own_send", "own_recv"), outs[:4]))
    return sems, list(outs[4:4 + n]), list(outs[4 + n:4 + 2 * n]), outs[-1]


def _gather_forward(name, idx, lands, sems, after):
    n = len(idx)

    def body(*refs):
        zones = refs[:n]
        ici_recv = refs[n]
        fwd_send, fwd_recv = refs[n + 2], refs[n + 3]
        x, y, c, chips = _place()
        for k, i in enumerate(idx):
            for j, chip in enumerate(chips):
                half = _landed(zones, k, 2 * chip[0] + chip[1], c)
                _remote(half, half, fwd_send.at[3 * k + j], ici_recv.at[3 * i + j], (*chip, c)).wait_recv()
                _remote(half, half, fwd_send.at[3 * k + j], fwd_recv.at[3 * k + j], (x, y, 1 - c)).start()

    dma = pltpu.SemaphoreType.DMA
    outs = pl.pallas_call(
        body, name=name,
        in_specs=[HBM] * n + [SEM, ANY],
        out_shape=(dma((3 * n,)), dma((3 * n,)), *[pltpu.HBM(a.shape, a.dtype) for a in lands]),
        out_specs=(SEM, SEM, *[HBM] * n),
        input_output_aliases={k: 2 + k for k in range(n)},
        compiler_params=pltpu.CompilerParams(has_side_effects=EFFECT),
    )(*lands, sems["ici_recv"], after)
    return (outs[0], outs[1]), list(outs[2:])


def _gather_wait(name, idx, shards, lands, sems, fwd, after):
    n = len(idx)

    def body(*refs):
        ins, zones = refs[:n], refs[n:2 * n]
        ici_send, own_send, own_recv, fwd_send, fwd_recv = refs[2 * n:2 * n + 5]
        x, y, c, chips = _place()
        me = 2 * x + y
        for k, i in enumerate(idx):
            mine = _half(ins[k], c)
            for j, chip in enumerate(chips):
                theirs = 2 * chip[0] + chip[1]
                _remote(mine, _landed(zones, k, me, c), ici_send.at[3 * i + j], fwd_recv.at[3 * k + j],
                        (*chip, c)).wait_send()
                sent = _landed(zones, k, theirs, c)
                _remote(sent, sent, fwd_send.at[3 * k + j], fwd_recv.at[3 * k + j], (x, y, 1 - c)).wait_send()
                passed = _landed(zones, k, theirs, 1 - c)
                _remote(passed, passed, fwd_send.at[3 * k + j], fwd_recv.at[3 * k + j], (x, y, 1 - c)).wait_recv()
            own = _remote(ins[k], zones[k].at[me], own_send.at[i], own_recv.at[i], (x, y, 1 - c))
            own.wait_send()
            own.wait_recv()

    outs = pl.pallas_call(
        body, name=name,
        in_specs=[HBM] * (2 * n) + [SEM] * 5 + [ANY],
        out_shape=tuple(pltpu.HBM(a.shape, a.dtype) for a in lands),
        out_specs=tuple([HBM] * n),
        input_output_aliases={n + k: k for k in range(n)},
        compiler_params=pltpu.CompilerParams(has_side_effects=EFFECT),
    )(*shards, *lands, sems["ici_send"], sems["own_send"], sems["own_recv"], fwd[0], fwd[1], after)
    return list(outs)


def _gather_relay(name, idx, shards, lands, sems, after):
    n = len(idx)

    def body(*refs):
        ins, zones, ici_recv = refs[:n], refs[n:2 * n], refs[2 * n]
        relay_send, relay_recv, pass_send, pass_recv = refs[2 * n + 2:2 * n + 6]
        x, y, c, _ = _place()
        first, second, _ = _routes()
        for k, i in enumerate(idx):
            landed = _landed(zones, k, _shard_of(first), c)
            _remote(landed, landed, pass_send.at[k], ici_recv.at[3 * i], (*first, c)).wait_recv()
            _remote(_half(ins[k], c), _landed(zones, k, 2 * x + y, c), relay_send.at[2 * k], relay_recv.at[2 * k],
                    (*second, c)).start()
            _remote(landed, landed, relay_send.at[2 * k + 1], relay_recv.at[2 * k + 1], (*second, c)).start()
            _remote(landed, landed, pass_send.at[k], pass_recv.at[k], (x, y, 1 - c)).start()
        refs[-1][...] = jnp.zeros_like(refs[-1])

    dma = pltpu.SemaphoreType.DMA
    outs = pl.pallas_call(
        body, name=name,
        in_specs=[HBM] * (2 * n) + [SEM, ANY],
        out_shape=(dma((2 * n,)), dma((2 * n,)), dma((n,)), dma((n,)), *[pltpu.HBM(a.shape, a.dtype) for a in lands],
                   jax.ShapeDtypeStruct((8, LANES), F32)),
        out_specs=(SEM, SEM, SEM, SEM, *[HBM] * n, pl.BlockSpec(memory_space=pltpu.VMEM)),
        input_output_aliases={n + k: 4 + k for k in range(n)},
        compiler_params=pltpu.CompilerParams(has_side_effects=EFFECT),
    )(*shards, *lands, sems["ici_recv"], after)
    return outs[:4], list(outs[4:4 + n]), outs[-1]


def _gather_forward_relayed(name, ks, lands, relay, after):
    n = len(ks)

    def body(*refs):
        zones, relay_recv = refs[:n], refs[n]
        fwd_send, fwd_recv = refs[n + 2], refs[n + 3]
        x, y, c, _ = _place()
        _, second, diagonal = _routes()
        for p, k in enumerate(ks):
            for j, chip in enumerate((second, diagonal)):
                landed = _landed(zones, p, _shard_of(chip), c)
                _remote(landed, landed, fwd_send.at[2 * p + j], relay_recv.at[2 * k + j], (*second, c)).wait_recv()
                _remote(landed, landed, fwd_send.at[2 * p + j], fwd_recv.at[2 * p + j], (x, y, 1 - c)).start()

    dma = pltpu.SemaphoreType.DMA
    outs = pl.pallas_call(
        body, name=name,
        in_specs=[HBM] * n + [SEM, ANY],
        out_shape=(dma((2 * n,)), dma((2 * n,)), *[pltpu.HBM(a.shape, a.dtype) for a in lands]),
        out_specs=(SEM, SEM, *[HBM] * n),
        input_output_aliases={k: 2 + k for k in range(n)},
        compiler_params=pltpu.CompilerParams(has_side_effects=EFFECT),
    )(*lands, relay[1], after)
    return (outs[0], outs[1]), list(outs[2:])


def _gather_wait_relayed(name, idx, ks, shards, lands, sems, relay, fwd, after):
    n = len(idx)

    def body(*refs):
        ins, zones = refs[:n], refs[n:2 * n]
        ici_send, own_send, own_recv, relay_send, pass_send, pass_recv, fwd_send, fwd_recv = refs[2 * n:2 * n + 8]
        x, y, c, _ = _place()
        me = 2 * x + y
        sibling = (x, y, 1 - c)
        first, second, diagonal = _routes()
        for p, (i, k) in enumerate(zip(idx, ks)):
            mine, at_peer = _half(ins[p], c), _landed(zones, p, me, c)
            from_first = _landed(zones, p, _shard_of(first), c)
            _remote(mine, at_peer, ici_send.at[3 * i], pass_recv.at[k], (*first, c)).wait_send()
            _remote(mine, at_peer, relay_send.at[2 * k], pass_recv.at[k], (*second, c)).wait_send()
            _remote(from_first, from_first, relay_send.at[2 * k + 1], pass_recv.at[k], (*second, c)).wait_send()
            _remote(from_first, from_first, pass_send.at[k], pass_recv.at[k], sibling).wait_send()
            theirs = _landed(zones, p, _shard_of(second), 1 - c)
            _remote(theirs, theirs, pass_send.at[k], pass_recv.at[k], sibling).wait_recv()
            for j, (sent, got) in enumerate(((second, first), (diagonal, diagonal))):
                out_half = _landed(zones, p, _shard_of(sent), c)
                _remote(out_half, out_half, fwd_send.at[2 * p + j], fwd_recv.at[2 * p + j], sibling).wait_send()
                in_half = _landed(zones, p, _shard_of(got), 1 - c)
                _remote(in_half, in_half, fwd_send.at[2 * p + j], fwd_recv.at[2 * p + j], sibling).wait_recv()
            own = _remote(ins[p], zones[p].at[me], own_send.at[i], own_recv.at[i], sibling)
            own.wait_send()
            own.wait_recv()

    outs = pl.pallas_call(
        body, name=name,
        in_specs=[HBM] * (2 * n) + [SEM] * 8 + [ANY],
        out_shape=tuple(pltpu.HBM(a.shape, a.dtype) for a in lands),
        out_specs=tuple([HBM] * n),
        input_output_aliases={n + k: k for k in range(n)},
        compiler_params=pltpu.CompilerParams(has_side_effects=EFFECT),
    )(*shards, *lands, sems["ici_send"], sems["own_send"], sems["own_recv"], relay[0], relay[2], relay[3],
      fwd[0], fwd[1], after)
    return list(outs)


def _all_reduce_small(name, slab, after=None):
    r, width = slab.shape
    ndev = 8

    def body(x_ref, after_ref, out_ref, buf, send_sems, recv_sems):
        x, y, c, _ = _place()
        me = 4 * x + 2 * y + c
        buf[me] = x_ref[...]
        copies = []
        for k in range(1, ndev):
            peer = jnp.bitwise_xor(me, k)
            to = (peer // 4, (peer // 2) % 2, peer % 2)
            cp = pltpu.make_async_remote_copy(src_ref=x_ref, dst_ref=buf.at[me], send_sem=send_sems.at[k - 1],
                                              recv_sem=recv_sems.at[k - 1], device_id=to, device_id_type=MESH)
            cp.start()
            copies.append(cp)
        for k in range(1, ndev):
            peer = jnp.bitwise_xor(me, k)
            pltpu.make_async_remote_copy(src_ref=x_ref, dst_ref=buf.at[peer], send_sem=send_sems.at[k - 1],
                                         recv_sem=recv_sems.at[k - 1], device_id=(x, y, c),
                                         device_id_type=MESH).wait_recv()
        for cp in copies:
            cp.wait_send()
        total = buf[0]
        for d in range(1, ndev):
            total = total + buf[d]
        out_ref[...] = total

    return pl.pallas_call(
        body, name=name,
        in_specs=[pl.BlockSpec(memory_space=pltpu.VMEM), ANY], out_specs=pl.BlockSpec(memory_space=pltpu.VMEM),
        out_shape=jax.ShapeDtypeStruct((r, width), F32),
        scratch_shapes=[pltpu.VMEM((ndev, r, width), F32), pltpu.SemaphoreType.DMA((ndev - 1,)),
                        pltpu.SemaphoreType.DMA((ndev - 1,))],
        compiler_params=pltpu.CompilerParams(vmem_limit_bytes=VMEM_LIMIT),
    )(slab, slab if after is None else after)


def _half_tiling(rows, cols):
    if _by_rows(rows):
        tr = _pick(rows // 2, (256, 128, 64, 32, 16))
        nb = (rows // 2) // tr
        return (tr, cols), nb, (lambda which, b: (which * nb + b, 0)), (lambda b: (b, 0))
    tc = _pick(cols // 2, (256, 128))
    nb = (cols // 2) // tc
    return (rows, tc), nb, (lambda which, b: (0, which * nb + b)), (lambda b: (0, b))


def _chip_partial(name, grad, other, core):
    s, r, cdim = grad.shape
    blk, nb, whole, within = _half_tiling(r, cdim)

    def body(core_ref, g_ref, o_ref, out_ref):
        out_ref[...] = (g_ref[...] + o_ref[...]).astype(BF16)

    return pl.pallas_call(
        body, name=name,
        grid_spec=pltpu.PrefetchScalarGridSpec(
            num_scalar_prefetch=1, grid=(s, nb),
            in_specs=[pl.BlockSpec((None,) + blk, lambda j, b, core_ref: (j,) + whole(core_ref[0], b)),
                      pl.BlockSpec((None,) + blk, lambda j, b, core_ref: (j,) + within(b))],
            out_specs=pl.BlockSpec((None,) + blk, lambda j, b, core_ref: (j,) + within(b))),
        out_shape=jax.ShapeDtypeStruct((s,) + _half_shape(r, cdim), BF16),
        compiler_params=_params("parallel", "parallel"),
    )(core, grad, other)


def _partial_copies(ins, zones, send_sems, recv_sems):
    x, y, c, chips = _place()
    return [_remote(ins[i].at[2 * chip[0] + chip[1]], zones[i].at[j], send_sems.at[3 * i + j],
                    recv_sems.at[3 * i + j], (*chip, c))
            for i in range(len(ins)) for j, chip in enumerate(chips)]


def _swap_copies(ins, zones, send_sems, recv_sems):
    x, y, c, _ = _place()
    copies = []
    for i in range(len(ins)):
        for s in range(N_SHARD):
            copies.append(_remote(_half(ins[i], 1 - c, s), zones[i].at[s],
                                  send_sems.at[N_SHARD * i + s], recv_sems.at[N_SHARD * i + s], (x, y, 1 - c)))
    return copies


def _exchange_start(name, plan, sources, lands, per_array):
    n = len(sources)
    lands = [lax.empty(shape, dtype) for shape, dtype in lands]

    def body(*refs):
        for cp in plan(refs[:n], refs[n:2 * n], refs[2 * n], refs[2 * n + 1]):
            cp.start()
        refs[-1][...] = jnp.zeros_like(refs[-1])

    dma = pltpu.SemaphoreType.DMA
    outs = pl.pallas_call(
        body, name=name,
        in_specs=[HBM] * (2 * n),
        out_shape=(dma((per_array * n,)), dma((per_array * n,)),
                   *[pltpu.HBM(a.shape, a.dtype) for a in list(sources) + lands], jax.ShapeDtypeStruct((8, LANES), F32)),
        out_specs=(SEM, SEM, *[HBM] * (2 * n), pl.BlockSpec(memory_space=pltpu.VMEM)),
        input_output_aliases={k: 2 + k for k in range(2 * n)},
        compiler_params=pltpu.CompilerParams(has_side_effects=EFFECT),
    )(*[_in_hbm(a) for a in list(sources) + lands])
    return (outs[0], outs[1]), list(outs[2:2 + n]), list(outs[2 + n:2 + 2 * n]), outs[-1]


def _exchange_wait(name, plan, started, after):
    sems, partials, lands, _ = started
    n = len(partials)

    def body(*refs):
        for cp in plan(refs[:n], refs[n:2 * n], refs[2 * n], refs[2 * n + 1]):
            cp.wait_send()
            cp.wait_recv()

    outs = pl.pallas_call(
        body, name=name,
        in_specs=[HBM] * (2 * n) + [SEM, SEM] + [ANY] * len(after),
        out_shape=tuple(pltpu.HBM(a.shape, a.dtype) for a in lands),
        out_specs=tuple([HBM] * n),
        input_output_aliases={n + k: k for k in range(n)},
        compiler_params=pltpu.CompilerParams(has_side_effects=EFFECT),
    )(*partials, *lands, sems[0], sems[1], *after)
    return list(outs)


def _reduce_own(name, grad, other, received, where):
    s, r, cdim = grad.shape
    blk, nb, whole, within = _half_tiling(r, cdim)

    def body(where_ref, g_ref, o_ref, r_ref, out_ref):
        total = g_ref[...] + o_ref[...]
        for j in range(3):
            total = total + r_ref[j].astype(F32)
        out_ref[...] = total

    return pl.pallas_call(
        body, name=name,
        grid_spec=pltpu.PrefetchScalarGridSpec(
            num_scalar_prefetch=1, grid=(nb,),
            in_specs=[pl.BlockSpec((None,) + blk, lambda b, w_ref: (w_ref[0],) + whole(w_ref[1], b)),
                      pl.BlockSpec((None,) + blk, lambda b, w_ref: (w_ref[0],) + within(b)),
                      pl.BlockSpec((3,) + blk, lambda b, w_ref: (0,) + within(b))],
            out_specs=pl.BlockSpec(blk, lambda b, w_ref: whole(w_ref[1], b))),
        out_shape=jax.ShapeDtypeStruct((r, cdim), F32),
        compiler_params=_params("parallel"),
    )(where, grad, other, received)


def _join_start(name, halves):
    n = len(halves)

    def body(*refs):
        bufs, send_sems, recv_sems = refs[:n], refs[n], refs[n + 1]
        x, y, c, _ = _place()
        for i in range(n):
            mine = _half(bufs[i], c)
            _remote(mine, mine, send_sems.at[i], recv_sems.at[i], (x, y, 1 - c)).start()
        refs[-1][...] = jnp.zeros_like(refs[-1])

    dma = pltpu.SemaphoreType.DMA
    outs = pl.pallas_call(
        body, name=name,
        in_specs=[HBM] * n,
        out_shape=(dma((n,)), dma((n,)), *[pltpu.HBM(h.shape, F32) for h in halves], jax.ShapeDtypeStruct((8, LANES), F32)),
        out_specs=(SEM, SEM, *[HBM] * n, pl.BlockSpec(memory_space=pltpu.VMEM)),
        input_output_aliases={k: 2 + k for k in range(n)},
        compiler_params=pltpu.CompilerParams(has_side_effects=EFFECT),
    )(*[_in_hbm(h) for h in halves])
    return (outs[0], outs[1]), list(outs[2:2 + n]), outs[-1]


def _join_wait(name, started, after):
    sems, bufs, _ = started
    n = len(bufs)

    def body(*refs):
        bufs, send_sems, recv_sems = refs[:n], refs[n], refs[n + 1]
        x, y, c, _ = _place()
        for i in range(n):
            mine, theirs = _half(bufs[i], c), _half(bufs[i], 1 - c)
            _remote(mine, mine, send_sems.at[i], recv_sems.at[i], (x, y, 1 - c)).wait_send()
            _remote(theirs, theirs, send_sems.at[i], recv_sems.at[i], (x, y, 1 - c)).wait_recv()

    outs = pl.pallas_call(
        body, name=name,
        in_specs=[HBM] * n + [SEM, SEM] + [ANY] * len(after),
        out_shape=tuple(pltpu.HBM(b.shape, F32) for b in bufs),
        out_specs=tuple([HBM] * n),
        input_output_aliases={k: k for k in range(n)},
        compiler_params=pltpu.CompilerParams(has_side_effects=EFFECT),
    )(*bufs, sems[0], sems[1], *after)
    return list(outs)


BIG = ("w_in", "pool_w", "w_out", "xq_w", "xk_w", "xv_w", "xo_w", "w_up", "w_down", "conv_w")
KEPT_F32 = ("conv_w",)
GATHER_GROUPS = ((0, 1, 9), (2, 3, 4, 5, 6), (7,), (8,))
RELAYED = (7, 8)
SMALL = ("conv_w", "a_log", "dt_bias", "gdn_norm_w", "pool_scale", "ln1_g", "ln1_b", "ln2_g", "ln2_b", "ln3_g", "ln3_b")
ORDER = ("w_in", "conv_w", "a_log", "dt_bias", "gdn_norm_w", "pool_w", "pool_scale", "w_out", "ln1_g", "ln1_b",
         "xq_w", "xk_w", "xv_w", "xo_w", "ln2_g", "ln2_b", "w_up", "w_down", "ln3_g", "ln3_b")
LANES = 128


def _rows(flat_len):
    return -(-flat_len // LANES)


def _pack(pieces):
    out = []
    for p in pieces:
        flat = p.reshape(-1).astype(F32)
        out.append(jnp.pad(flat, (0, _rows(flat.shape[0]) * LANES - flat.shape[0])).reshape(-1, LANES))
    slab = jnp.concatenate(out, axis=0)
    return jnp.pad(slab, ((0, -slab.shape[0] % 8), (0, 0)))


def _unpack(slab, shapes):
    out, row = [], 0
    for shp in shapes:
        size = math.prod(shp)
        out.append(slab[row:row + _rows(size)].reshape(-1)[:size].reshape(shp))
        row += _rows(size)
    return out


TRANSPOSED = ("w_in",)


def _as2d(name, a):
    a = a[0]
    if name in TRANSPOSED:
        return jnp.swapaxes(a, 0, 1)
    return a.reshape(-1, a.shape[-1]) if a.ndim == 3 else a


def _from2d(name, a, shape):
    return (jnp.swapaxes(a, 0, 1) if name in TRANSPOSED else a).reshape(shape)


def kernel(x, mem, w_in, conv_w, a_log, dt_bias, gdn_norm_w, pool_w, pool_scale, w_out, ln1_g, ln1_b, xq_w, xk_w, xv_w, xo_w, ln2_g, ln2_b, w_up, w_down, ln3_g, ln3_b, loss_target, m_w_in, m_conv_w, m_a_log, m_dt_bias, m_gdn_norm_w, m_pool_w, m_pool_scale, m_w_out, m_ln1_g, m_ln1_b, m_xq_w, m_xk_w, m_xv_w, m_xo_w, m_ln2_g, m_ln2_b, m_w_up, m_w_down, m_ln3_g, m_ln3_b, v_w_in, v_conv_w, v_a_log, v_dt_bias, v_gdn_norm_w, v_pool_w, v_pool_scale, v_w_out, v_ln1_g, v_ln1_b, v_xq_w, v_xk_w, v_xv_w, v_xo_w, v_ln2_g, v_ln2_b, v_w_up, v_w_down, v_ln3_g, v_ln3_b):
    given = dict(locals())
    cx, cy, cc = lax.axis_index("x"), lax.axis_index("y"), lax.axis_index("c")
    me = 2 * cx + cy
    groups = pool_w.shape[1]
    cs = pool_w.shape[2]
    kk, conv_cols = conv_w.shape[1], conv_w.shape[2]
    core = cc.astype(jnp.int32).reshape(1)
    where = jnp.stack([me, cc]).astype(jnp.int32)

    started = {}
    wts = {}

    def start(name, idx, after, token=None):
        casts = [_after(token, _as2d(BIG[i], given[BIG[i]])).astype(F32 if BIG[i] in KEPT_F32 else BF16) for i in idx]
        relayed = tuple(k for k, i in enumerate(idx) if i in RELAYED)
        sems, shards, lands, token = _gather_start(name, casts, after, relayed)
        for k, i in enumerate(idx):
            started[i] = (sems, k, shards[k], lands[k])
        return token

    token = start("gather_start_first", GATHER_GROUPS[0], x)
    token = start("gather_start_rest", tuple(i for group in GATHER_GROUPS[1:] for i in group), token, token)

    relay = {}

    def send_on(after):
        members = [started[i] for i in RELAYED]
        relay["sems"], zones, token = _gather_relay("gather_relay", [m[1] for m in members], [m[2] for m in members],
                                                    [m[3] for m in members], members[0][0], after)
        relay["zones"] = dict(zip(RELAYED, zones))
        return token

    def fetch(group, after):
        members = [started[i] for i in GATHER_GROUPS[group]]
        sems, idx = members[0][0], [m[1] for m in members]
        shards = [m[2] for m in members]
        if GATHER_GROUPS[group][0] in RELAYED:
            ks = [RELAYED.index(i) for i in GATHER_GROUPS[group]]
            zones = [relay["zones"][i] for i in GATHER_GROUPS[group]]
            fwd, zones = _gather_forward_relayed(f"gather_forward_{group}", ks, zones, relay["sems"], after)
            got = _gather_wait_relayed(f"gather_wait_{group}", idx, ks, shards, zones, sems, relay["sems"], fwd, after)
        else:
            fwd, zones = _gather_forward(f"gather_forward_{group}", idx, [m[3] for m in members], sems, after)
            got = _gather_wait(f"gather_wait_{group}", idx, shards, zones, sems, fwd, after)
        full = dict(zip([BIG[i] for i in GATHER_GROUPS[group]], got))
        out = {}
        for n, a in full.items():
            if n == "w_in":
                out["w_in_t"] = a
            elif n == "w_up":
                out["w_up3"] = a
            elif n == "pool_w":
                out[n] = a.reshape(N_SHARD, groups, cs, -1).transpose(1, 0, 2, 3).reshape(groups, N_SHARD * cs, -1)
            elif n == "conv_w":
                out[n] = a.transpose(1, 0, 2).reshape(kk, N_SHARD * conv_cols)
            else:
                out[n] = a.reshape(-1, a.shape[-1])
        return out

    for n in ("a_log", "dt_bias", "gdn_norm_w", "pool_scale", "ln1_g", "ln1_b", "ln2_g", "ln2_b", "ln3_g", "ln3_b"):
        wts[n] = given[n]
    x_bf = _after(token, x[0]).astype(BF16)
    wts.update(fetch(0, x_bf))

    def start_swap(group, grads):
        names, blocks = [], []
        for n, g in grads.items():
            if n == "pool_w":
                g = g.reshape(groups, N_SHARD, cs, -1).transpose(1, 0, 2, 3).reshape(N_SHARD, groups * cs, -1)
            elif g.ndim == 2:
                g = g.reshape(N_SHARD, -1, g.shape[-1])
            names.append({"w_in_t": "w_in", "w_up3": "w_up"}.get(n, n))
            blocks.append(g)
        zones = [((N_SHARD,) + _half_shape(b.shape[1], b.shape[2]), F32) for b in blocks]
        swap = _exchange_start(f"grad_swap_start_{group}", _swap_copies, blocks, zones, N_SHARD)
        return {"group": group, "names": names, "swap": swap, "token": swap[3]}

    def start_send(state, after):
        group, names = state["group"], state["names"]
        state["blocks"] = state["swap"][1]
        state["others"] = _exchange_wait(f"grad_swap_wait_{group}", _swap_copies, state["swap"], after)
        partials = [_chip_partial("chip_partial_" + n, gb, ob, core)
                    for n, gb, ob in zip(names, state["blocks"], state["others"])]
        zones = [((3,) + p.shape[1:], BF16) for p in partials]
        state["send"] = _exchange_start(f"grad_send_start_{group}", _partial_copies, partials, zones, 3)
        state["token"] = state["send"][3]

    grad, delta, new_m, new_v = {}, {}, {}, {}

    def start_join(state, after):
        group, names = state["group"], state["names"]
        received = _exchange_wait(f"grad_send_wait_{group}", _partial_copies, state["send"], after)
        halves = [_reduce_own("reduce_own_" + n, gb, ob, rb, where)
                  for n, gb, ob, rb in zip(names, state["blocks"], state["others"], received)]
        state["join"] = _join_start(f"grad_join_start_{group}", halves)
        return state["join"][2]

    def finish_reduce(state, after):
        group, names = state["group"], state["names"]
        for n, g in zip(names, _join_wait(f"grad_join_wait_{group}", state["join"], after)):
            shp = given[n].shape
            d2, m2, v2, g2 = _adamw("adamw_" + n, _as2d(n, given[n]), g, _as2d(n, given["m_" + n]),
                                    _as2d(n, given["v_" + n]))
            grad[n], delta[n], new_m[n], new_v[n] = (_from2d(n, a, shp) for a in (g2, d2, m2, v2))
        return d2

    step = _local_step(x[0], mem[0], loss_target[0], wts, x_bf)
    pending = {}
    request = next(step)
    while True:
        try:
            kind, group, payload = request
            if kind == "weights":
                request = step.send(fetch(group, payload))
            elif kind == "relay":
                request = step.send(send_on(payload))
            elif kind == "grads":
                pending[group] = start_swap(group, payload)
                request = step.send(pending[group]["token"])
            else:
                start_send(pending[group], [payload])
                request = step.send(pending[group]["token"])
        except StopIteration as stop:
            loss_row, grad_x, g = stop.value
            break

    after = [pending[2]["token"], grad_x]
    for group in (0, 1):
        after = [start_join(pending[group], after)]
    for group in (0, 1):
        after = [finish_reduce(pending[group], after)]
    after = [finish_reduce(pending[2], [start_join(pending[2], after)])]

    small_names = ("a_log", "dt_bias", "gdn_norm_w", "pool_scale", "ln1_g", "ln1_b", "ln2_g", "ln2_b", "ln3_g", "ln3_b")
    pieces = [g["conv_w"]] + [g[n] for n in small_names] + [loss_row[:, :1]]
    shapes = [p.shape for p in pieces]
    summed = _unpack(_all_reduce_small("all_reduce_small", _pack(pieces), after[0]), shapes)
    gsmall = dict(zip(small_names, summed[1:-1]))
    gsmall["conv_w"] = lax.dynamic_slice(summed[0], (0, me * conv_cols), (kk, conv_cols))
    loss = summed[-1][0, 0]

    sshapes = [given[n].shape for n in SMALL]
    slabs = [_pack([given[p + n] for n in SMALL]) for p in ("", "m_", "v_")]
    gslab = _pack([gsmall[n] for n in SMALL])
    outs = _adamw("adamw_small", slabs[0], gslab, slabs[1], slabs[2])[:3]
    for dst, slab in zip((delta, new_m, new_v), outs):
        dst.update(zip(SMALL, _unpack(slab, sshapes)))
    for n in SMALL:
        grad[n] = gsmall[n].reshape(given[n].shape)

    return (loss, grad_x[None], *[grad[n] for n in ORDER], *[delta[n] for n in ORDER],
            *[new_m[n] for n in ORDER], *[new_v[n] for n in ORDER])
```

```python
import math

import jax
import jax.numpy as jnp
from jax import lax
from jax.experimental import pallas as pl
from jax.experimental.pallas import tpu as pltpu

F32 = jnp.float32
BF16 = jnp.bfloat16
MESH = pl.DeviceIdType.MESH

HEAD_DIM = 128
CHUNK = 64
POOL_WINDOWS = (2, 4, 8, 16)
XATTN_HEADS = 4
ALPHA = 2.0 ** 0.25
LN_EPS = 1e-5
NORM_EPS = 1e-6
ADAM_LR, ADAM_B1, ADAM_B2, ADAM_EPS, ADAM_WD, ADAM_STEP = 0.001, 0.9, 0.999, 1e-08, 0.01, 10
N_SHARD = 4
VMEM_LIMIT = 56 * 1024 * 1024
K_STEPS = (2048, 1024, 512, 256, 128)


def _params(*sem):
    return pltpu.CompilerParams(dimension_semantics=sem, vmem_limit_bytes=VMEM_LIMIT)


def _bdot(a, b, ta=False, tb=False):
    dims = (((0 if ta else 1,), (1 if tb else 0,)), ((), ()))
    return lax.dot_general(a.astype(BF16), b.astype(BF16), dims, preferred_element_type=F32)


def _sigmoid(x):
    return 1.0 / (1.0 + jnp.exp(-x))


def _matmul(name, a, b, *, ta=False, tb=False, tm, tn, tk, extra=(), outs, epilogue, b_blocks=None,
            sequential=False, n_used=None, k_used=None, n_outer=False):
    m, k_dim = (a.shape[1], a.shape[0]) if ta else a.shape
    if b_blocks and tb:
        n = b.shape[1]
        k_dim = b.shape[0] * b.shape[2]
        per = b.shape[2] // tk
        b_spec = pl.BlockSpec((None, tn, tk), lambda i, j, k: (k // per, j, k % per))
    elif b_blocks:
        n = b.shape[0] * b.shape[2]
        per = b.shape[2] // tn
        b_spec = pl.BlockSpec((None, tk, tn), lambda i, j, k: (j // per, k, j % per))
    elif tb:
        n = b.shape[0]
        b_spec = pl.BlockSpec((tn, tk), lambda i, j, k: (j, k))
    else:
        n = b.shape[1]
        b_spec = pl.BlockSpec((tk, tn), lambda i, j, k: (k, j))
    n, k_dim = n_used or n, k_used or k_dim
    assert m % tm == 0 and n % tn == 0 and k_dim % tk == 0, (name, m, n, k_dim, tm, tn, tk)
    nk = k_dim // tk
    a_spec = pl.BlockSpec((tk, tm), lambda i, j, k: (k, i)) if ta else pl.BlockSpec((tm, tk), lambda i, j, k: (i, k))
    n_extra, n_out = len(extra), len(outs)

    def wrap(index_map):
        return lambda i, j, k: index_map(i, j)

    def spec(block, index_map):
        if n_outer:
            return pl.BlockSpec(block, lambda j, i, k: index_map(i, j, k))
        return pl.BlockSpec(block, index_map)

    row_axis = 1 if n_outer else 0

    def body_one_step(*refs):
        ex = refs[2:2 + n_extra]
        out = refs[2 + n_extra:2 + n_extra + n_out]
        epilogue(_bdot(refs[0][...], refs[1][...], ta, tb), ex, out, pl.program_id(row_axis))

    def body(*refs):
        a_ref, b_ref = refs[0], refs[1]
        ex = refs[2:2 + n_extra]
        out = refs[2 + n_extra:2 + n_extra + n_out]
        acc = refs[-1]
        i, k = pl.program_id(row_axis), pl.program_id(2)
        part = _bdot(a_ref[...], b_ref[...], ta, tb)

        @pl.when(k == 0)
        def _():
            acc[...] = part

        @pl.when(jnp.logical_and(k > 0, k < nk - 1))
        def _():
            acc[...] += part

        @pl.when(k == nk - 1)
        def _():
            epilogue(acc[...] + part, ex, out, i)

    sem = ("arbitrary",) * 3 if sequential else ("parallel", "parallel", "arbitrary")
    res = pl.pallas_call(
        body_one_step if nk == 1 else body, name=name,
        grid=(n // tn, m // tm, nk) if n_outer else (m // tm, n // tn, nk),
        in_specs=[spec(a_spec.block_shape, a_spec.index_map), spec(b_spec.block_shape, b_spec.index_map)]
        + [spec(bs, wrap(im)) for _, bs, im in extra],
        out_specs=[spec(bs, wrap(im)) for _, bs, im in outs],
        out_shape=[s for s, _, _ in outs],
        scratch_shapes=[] if nk == 1 else [pltpu.VMEM((tm, tn), F32)],
        compiler_params=_params(*sem),
    )(a, b, *[x for x, _, _ in extra])
    return res


def _tile(i, j):
    return (i, j)


def _plain(name, a, b, *, ta=False, tb=False, tm, tn, tk, out_dtype, b_blocks=None, out3=None, n_used=None,
           n_outer=False):
    m = a.shape[1] if ta else a.shape[0]
    if b_blocks:
        n = b.shape[1] if tb else b.shape[0] * b.shape[2]
    else:
        n = n_used or (b.shape[0] if tb else b.shape[1])

    def epi(acc, ex, out, i):
        out[0][...] = acc.astype(out_dtype)

    if out3:
        per = (n // out3) // tn
        spec = (jax.ShapeDtypeStruct((out3, m, n // out3), out_dtype), (None, tm, tn),
                lambda i, j: (j // per, i, j % per))
    else:
        spec = (jax.ShapeDtypeStruct((m, n), out_dtype), (tm, tn), _tile)
    return _matmul(name, a, b, ta=ta, tb=tb, tm=tm, tn=tn, tk=tk, outs=[spec], epilogue=epi,
                   b_blocks=b_blocks, n_used=n_used, n_outer=n_outer)[0]


def _ln_forward(name, a, b, res, gamma, beta, *, tm, tk, want_h=True):
    m, n = res.shape

    def epi(acc, ex, out, i):
        u = ALPHA * ex[0][...] + acc
        mu = jnp.mean(u, axis=-1, keepdims=True)
        xc = u - mu
        var = jnp.mean(xc * xc, axis=-1, keepdims=True)
        rstd = lax.rsqrt(var + LN_EPS)
        xhat = xc * rstd
        out[-2][...] = xhat
        out[-1][...] = rstd
        if want_h:
            h = xhat * ex[1][...] + ex[2][...]
            out[0][...] = h
            out[1][...] = h.astype(BF16)

    row = lambda i, j: (i, 0)
    vec = lambda i, j: (0, 0)
    outs = [(jax.ShapeDtypeStruct((m, n), F32), (tm, n), row), (jax.ShapeDtypeStruct((m, n), BF16), (tm, n), row),
            (jax.ShapeDtypeStruct((m, n), F32), (tm, n), row), (jax.ShapeDtypeStruct((m, 1), F32), (tm, 1), row)]
    return _matmul(
        name, a, b, tm=tm, tn=n, tk=tk,
        extra=[(res, (tm, n), row), (gamma, (1, n), vec), (beta, (1, n), vec)],
        outs=outs if want_h else outs[2:], epilogue=epi)


def _ln_backward_math(dy, xhat, rstd, gamma):
    dxhat = dy * gamma
    m1 = jnp.mean(dxhat, axis=-1, keepdims=True)
    m2 = jnp.mean(dxhat * xhat, axis=-1, keepdims=True)
    du = rstd * (dxhat - m1 - xhat * m2)
    return du, jnp.sum(dy * xhat, axis=0, keepdims=True), jnp.sum(dy, axis=0, keepdims=True)


def _ln_backward(name, a, b, dres, xhat, rstd, gamma, *, tm, tk, b_blocks=None, tb=True):
    m, n = dres.shape

    def epi(acc, ex, out, i):
        dy = acc + ALPHA * ex[0][...]
        du, dg, db = _ln_backward_math(dy, ex[1][...], ex[2][...], ex[3][...])
        out[0][...] = du
        out[1][...] = du.astype(BF16)
        first = i == 0

        @pl.when(first)
        def _():
            out[2][...] = dg
            out[3][...] = db

        @pl.when(jnp.logical_not(first))
        def _():
            out[2][...] += dg
            out[3][...] += db

    row = lambda i, j: (i, 0)
    vec = lambda i, j: (0, 0)
    return _matmul(
        name, a, b, tb=tb, tm=tm, tn=n, tk=tk, b_blocks=b_blocks, sequential=True,
        extra=[(dres, (tm, n), row), (xhat, (tm, n), row), (rstd, (tm, 1), row), (gamma, (1, n), vec)],
        outs=[(jax.ShapeDtypeStruct((m, n), F32), (tm, n), row),
              (jax.ShapeDtypeStruct((m, n), BF16), (tm, n), row),
              (jax.ShapeDtypeStruct((1, n), F32), (1, n), vec),
              (jax.ShapeDtypeStruct((1, n), F32), (1, n), vec)],
        epilogue=epi)


def _shift_down(x, k):
    row = lax.broadcasted_iota(jnp.int32, x.shape, 0)
    return jnp.where(row >= k, pltpu.roll(x, k, axis=0), 0.0)


def _shift_up(x, k):
    t = x.shape[0]
    row = lax.broadcasted_iota(jnp.int32, x.shape, 0)
    return jnp.where(row < t - k, pltpu.roll(x, t - k, axis=0), 0.0)


def _conv_silu_norm(x, w, normalise):
    kk = w.shape[0]
    c = x * w[kk - 1:kk, :]
    for j in range(kk - 1):
        c = c + _shift_down(x, kk - 1 - j) * w[j:j + 1, :]
    sg = _sigmoid(c)
    s = c * sg
    r = lax.rsqrt(jnp.sum(s * s, axis=-1, keepdims=True) + NORM_EPS)
    y = jnp.where(normalise, s * r, s)
    return c, sg, s, r, y


def _gdn_pre(proj, conv_w, heads):
    t = proj.shape[0]
    kk = conv_w.shape[0]

    def body(x_ref, w_ref, o_ref):
        normalise = pl.program_id(0) < 2
        o_ref[...] = _conv_silu_norm(x_ref[...], w_ref[...], normalise)[4]

    col = lambda s, h: (0, s * heads + h)
    return pl.pallas_call(
        body, name="gdn_pre", grid=(3, heads),
        in_specs=[pl.BlockSpec((t, HEAD_DIM), col), pl.BlockSpec((kk, HEAD_DIM), col)],
        out_specs=pl.BlockSpec((t, HEAD_DIM), col),
        out_shape=jax.ShapeDtypeStruct((t, 3 * heads * HEAD_DIM), F32),
        compiler_params=_params("parallel", "parallel"),
    )(proj, conv_w)


def _gdn_pre_backward(proj, conv_w, dqkv, heads):
    t = proj.shape[0]
    kk = conv_w.shape[0]

    def body(x_ref, w_ref, dy_ref, dx_ref, dw_ref):
        normalise = pl.program_id(0) < 2
        x = x_ref[...]
        w = w_ref[...]
        dy = dy_ref[...]
        c, sg, s, r, y = _conv_silu_norm(x, w, normalise)
        ds_norm = r * (dy - y * jnp.sum(dy * y, axis=-1, keepdims=True))
        ds = jnp.where(normalise, ds_norm, dy)
        dc = ds * (sg * (1.0 + c * (1.0 - sg)))
        dx = dc * w[kk - 1:kk, :]
        rows = [None] * kk
        rows[kk - 1] = jnp.sum(dc * x, axis=0, keepdims=True)
        for j in range(kk - 1):
            lag = kk - 1 - j
            dx = dx + _shift_up(dc, lag) * w[j:j + 1, :]
            rows[j] = jnp.sum(dc * _shift_down(x, lag), axis=0, keepdims=True)
        dx_ref[...] = dx.astype(BF16)
        dw_ref[...] = jnp.concatenate(rows, axis=0)

    col = lambda s, h: (0, s * heads + h)
    return pl.pallas_call(
        body, name="gdn_pre_bwd", grid=(3, heads),
        in_specs=[pl.BlockSpec((t, HEAD_DIM), col), pl.BlockSpec((kk, HEAD_DIM), col),
                  pl.BlockSpec((t, HEAD_DIM), col)],
        out_specs=[pl.BlockSpec((t, HEAD_DIM), col), pl.BlockSpec((kk, HEAD_DIM), col)],
        out_shape=[jax.ShapeDtypeStruct((t, 3 * heads * HEAD_DIM), BF16),
                   jax.ShapeDtypeStruct((kk, 3 * heads * HEAD_DIM), F32)],
        compiler_params=_params("parallel", "parallel"),
    )(proj, conv_w, dqkv)


def _gate_vectors(a_log, dt_bias, heads):
    pad = lambda v: jnp.pad(v.astype(F32), ((0, 0), (heads, HEAD_DIM - 2 * heads)))
    return pad(jnp.exp(a_log.astype(F32))), pad(dt_bias)


def _softplus(x):
    return jnp.maximum(x, 0.0) + jnp.log(1.0 + jnp.exp(-jnp.abs(x)))


def _gates_epilogue(heads):
    def epi(acc, ex, out, i):
        lane = lax.broadcasted_iota(jnp.int32, acc.shape, 1)
        beta = _sigmoid(acc)
        g = -ex[0][...] * _softplus(acc + ex[1][...])
        out[0][...] = acc
        out[1][...] = jnp.where(lane < heads, beta, jnp.where(lane < 2 * heads, g, 0.0))
    return epi


def _gates_backward(ba, bg, dbg, ea, dtb, heads):
    t = ba.shape[0]

    def body(ba_ref, bg_ref, d_ref, ea_ref, dt_ref, dba_ref, dal_ref, ddt_ref):
        lane = lax.broadcasted_iota(jnp.int32, (t, HEAD_DIM), 1)
        bgv = bg_ref[...]
        d = d_ref[...]
        db = d * bgv * (1.0 - bgv)
        da = -d * ea_ref[...] * _sigmoid(ba_ref[...] + dt_ref[...])
        is_g = jnp.logical_and(lane >= heads, lane < 2 * heads)
        dba = jnp.where(lane < heads, db, jnp.where(is_g, da, 0.0))
        dba_ref[...] = dba.astype(BF16)
        dal_ref[...] = jnp.sum(jnp.where(is_g, d * bgv, 0.0), axis=0, keepdims=True)
        ddt_ref[...] = jnp.sum(jnp.where(is_g, da, 0.0), axis=0, keepdims=True)

    full = pl.BlockSpec((t, HEAD_DIM), lambda: (0, 0))
    vec = pl.BlockSpec((1, HEAD_DIM), lambda: (0, 0))
    return pl.pallas_call(
        body, name="gates_bwd", grid=(),
        in_specs=[full, full, full, vec, vec], out_specs=[full, vec, vec],
        out_shape=[jax.ShapeDtypeStruct((t, HEAD_DIM), BF16), jax.ShapeDtypeStruct((1, HEAD_DIM), F32),
                   jax.ShapeDtypeStruct((1, HEAD_DIM), F32)],
        compiler_params=pltpu.CompilerParams(vmem_limit_bytes=VMEM_LIMIT),
    )(ba, bg, dbg, ea, dtb)


class _Chunk:
    pass


def _split2(x):
    hi = x.astype(BF16)
    return hi, (x - hi.astype(F32)).astype(BF16)


def _split3(x):
    hi = x.astype(BF16)
    rest = x - hi.astype(F32)
    mid = rest.astype(BF16)
    return hi, mid, (rest - mid.astype(F32)).astype(BF16)


def _dot_mask(mask, x, ta=False):
    hi, mid, lo = _split3(x)
    return _bdot(mask, hi, ta=ta) + (_bdot(mask, mid, ta=ta) + _bdot(mask, lo, ta=ta))


def _transpose_by_identity(x):
    r = x.shape[0]
    eye = (lax.broadcasted_iota(jnp.int32, (r, r), 0) == lax.broadcasted_iota(jnp.int32, (r, r), 1)).astype(BF16)
    hi, mid, lo = _split3(x)
    return _bdot(hi, eye, ta=True) + (_bdot(mid, eye, ta=True) + _bdot(lo, eye, ta=True))


def _dot22(a, b, ta=False, tb=False):
    ah, al = _split2(a)
    bh, bl = _split2(b)
    return _bdot(ah, bh, ta, tb) + (_bdot(ah, bl, ta, tb) + _bdot(al, bh, ta, tb))


def _chunk_gates(bg, heads):
    n = CHUNK
    row = lax.broadcasted_iota(jnp.int32, (n, n), 0)
    col = lax.broadcasted_iota(jnp.int32, (n, n), 1)
    lane = lax.broadcasted_iota(jnp.int32, bg.shape, 1)
    graw = jnp.where(jnp.logical_and(lane >= heads, lane < 2 * heads), bg, 0.0)
    gc = _dot_mask((row >= col).astype(BF16), graw)
    return gc, _transpose_by_identity(gc)


def _in_lockstep(generators):
    results = [None] * len(generators)
    live = list(enumerate(generators))
    while live:
        still = []
        for i, gen in live:
            try:
                next(gen)
                still.append((i, gen))
            except StopIteration as stop:
                results[i] = stop.value
        live = still
    return results


def _chunk_local(q, k, v, beta, gc, grow, solved=None):
    c = _Chunk()
    n = CHUNK
    row = lax.broadcasted_iota(jnp.int32, (n, n), 0)
    col = lax.broadcasted_iota(jnp.int32, (n, n), 1)
    c.tri = row >= col
    c.strict = row > col
    eye = row == col
    c.gcb = jnp.broadcast_to(gc, (n, HEAD_DIM))
    c.decay = jnp.where(c.tri, jnp.exp(jnp.where(c.tri, gc - grow, 0.0)), 0.0)
    c.eg = jnp.exp(c.gcb)
    glast = c.gcb[n - 1:n, :]
    c.egl = jnp.exp(glast)
    c.ekl = jnp.exp(glast - c.gcb)
    c.beta = beta
    c.q = q * (HEAD_DIM ** -0.5)
    c.k = k
    c.v = v
    c.kb = k * beta
    c.vb = v * beta
    c.kg = c.kb * c.eg
    both = _bdot(jnp.concatenate([c.kb, c.q], axis=0), k, tb=True)
    yield
    c.L = jnp.where(c.strict, both[:n] * c.decay, 0.0)
    c.A = jnp.where(c.tri, both[n:] * c.decay, 0.0)
    if solved is None:
        x = -c.L
        tinv = eye.astype(F32) + x
        p = _dot22(x, x)
        yield
        for _ in range(int(math.log2(n)) - 2):
            both = _dot22(jnp.concatenate([p, tinv], axis=0), p)
            yield
            p, tinv = both[:n], tinv + both[n:]
        c.T = tinv + _dot22(tinv, p)
        yield
        uw = _dot22(c.T, jnp.concatenate([c.vb, c.kg], axis=1))
        yield
        c.u, c.w = uw[:, :HEAD_DIM], uw[:, HEAD_DIM:]
    else:
        c.T, c.u, c.w = solved
    c.qg = c.q * c.eg
    c.kdec = k * c.ekl
    return c


def _gdn_core(qkv, bg, heads):
    t = qkv.shape[0]
    nchunk = t // CHUNK

    gw = heads * HEAD_DIM

    def body(qkv_ref, bg_ref, o_ref, s_ref, t_ref, u_ref, w_ref, state):
        @pl.when(pl.program_id(0) == 0)
        def _():
            state[...] = jnp.zeros_like(state)

        bg_v = bg_ref[...]
        gc_all, gc_rows = _chunk_gates(bg_v, heads)
        def one_head(h):
            col = lambda s: pl.ds(s * gw + h * HEAD_DIM, HEAD_DIM)
            c = yield from _chunk_local(qkv_ref[:, col(0)], qkv_ref[:, col(1)], qkv_ref[:, col(2)], bg_v[:, h:h + 1],
                                        gc_all[:, heads + h:heads + h + 1], gc_rows[heads + h:heads + h + 1, :])
            s0 = state[h]
            v_new = c.u - _bdot(c.w, s0)
            yield
            o = _bdot(c.qg, s0) + _bdot(c.A, v_new)
            return s0, o, s0 * c.egl + _bdot(c.kdec, v_new, ta=True), c

        results = _in_lockstep([one_head(h) for h in range(heads)])
        for h, (s0, o, s1, c) in enumerate(results):
            lanes = pl.ds(h * HEAD_DIM, HEAD_DIM)
            s_ref[h, 0] = s0
            o_ref[:, lanes] = o
            t_ref[:, lanes] = jnp.concatenate([c.T, jnp.zeros((CHUNK, HEAD_DIM - CHUNK), F32)], axis=1)
            u_ref[:, lanes] = c.u
            w_ref[:, lanes] = c.w
            state[h] = s1

    return pl.pallas_call(
        body, name="gdn_core", grid=(nchunk,),
        in_specs=[pl.BlockSpec((CHUNK, 3 * gw), lambda n: (n, 0)), pl.BlockSpec((CHUNK, HEAD_DIM), lambda n: (n, 0))],
        out_specs=[pl.BlockSpec((CHUNK, gw), lambda n: (n, 0)),
                   pl.BlockSpec((heads, 1, HEAD_DIM, HEAD_DIM), lambda n: (0, n, 0, 0))]
        + [pl.BlockSpec((CHUNK, gw), lambda n: (n, 0))] * 3,
        out_shape=[jax.ShapeDtypeStruct((t, gw), F32),
                   jax.ShapeDtypeStruct((heads, nchunk, HEAD_DIM, HEAD_DIM), F32)]
        + [jax.ShapeDtypeStruct((t, gw), F32)] * 3,
        scratch_shapes=[pltpu.VMEM((heads, HEAD_DIM, HEAD_DIM), F32)],
        compiler_params=_params("arbitrary"),
    )(qkv, bg)


def _gdn_core_backward(qkv, bg, states, solved, do, heads):
    t = qkv.shape[0]
    nchunk = t // CHUNK
    n = CHUNK

    def one_head(chunk_local, s0, d_out, ds1):
        c = yield from chunk_local
        v_new = c.u - _bdot(c.w, s0)
        dqg = _bdot(d_out, s0, tb=True)
        ds0 = _bdot(c.qg, d_out, ta=True) + ds1 * c.egl
        dv_new = _bdot(c.A, d_out, ta=True) + _bdot(c.kdec, ds1)
        yield
        dA = jnp.where(c.tri, _bdot(d_out, v_new, tb=True), 0.0)
        dkdec = _bdot(v_new, ds1, tb=True)
        dgl = jnp.sum(jnp.sum(ds1 * s0, axis=1, keepdims=True), axis=0, keepdims=True) * c.egl
        dw = -_bdot(dv_new, s0, tb=True)
        ds0 = ds0 - _bdot(c.w, dv_new, ta=True)
        yield
        both = _dot22(c.T, jnp.concatenate([dv_new, dw], axis=1), ta=True)
        yield
        dvb, dkg = both[:, :HEAD_DIM], both[:, HEAD_DIM:]
        dL = jnp.where(c.strict, -(_bdot(dvb, c.u, tb=True) + _bdot(dkg, c.w, tb=True)), 0.0)
        yield
        dm1 = dL * c.decay
        dkb = _bdot(dm1, c.k) + dkg * c.eg
        dk = _bdot(dm1, c.kb, ta=True)
        dm2 = dA * c.decay
        dq = _bdot(dm2, c.k) + dqg * c.eg
        dk = dk + _bdot(dm2, c.q, ta=True) + dkdec * c.ekl + dkb * c.beta
        pm = dL * c.L + dA * c.A
        ones = jnp.ones((n, HEAD_DIM), BF16)
        pm_hi, pm_lo = _split2(pm)
        colsum = _bdot(pm_hi, ones, ta=True) + _bdot(pm_lo, ones, ta=True)
        tk_ = jnp.sum(dkdec * c.kdec, axis=1, keepdims=True)
        dgc = (jnp.sum(pm, axis=1, keepdims=True) - colsum
               + jnp.sum(dqg * c.qg, axis=1, keepdims=True)
               - tk_
               + jnp.sum(dkg * c.kg, axis=1, keepdims=True))
        dgl = dgl + jnp.sum(tk_, axis=0, keepdims=True)
        rowi = lax.broadcasted_iota(jnp.int32, (n, HEAD_DIM), 0)
        dgc = dgc + jnp.where(rowi == n - 1, dgl, 0.0)
        dbeta = jnp.sum(dkb * c.k, axis=1, keepdims=True) + jnp.sum(dvb * c.v, axis=1, keepdims=True)
        return dq * (HEAD_DIM ** -0.5), dk, dvb * c.beta, dbeta, dgc, ds0

    gw = heads * HEAD_DIM

    def body(qkv_ref, bg_ref, s_ref, t_ref, u_ref, w_ref, do_ref, dqkv_ref, dbg_ref, dstate):
        @pl.when(pl.program_id(0) == 0)
        def _():
            dstate[...] = jnp.zeros_like(dstate)

        bg_v = bg_ref[...]
        gc_all, gc_rows = _chunk_gates(bg_v, heads)
        lane = lax.broadcasted_iota(jnp.int32, (n, HEAD_DIM), 1)
        dgates = jnp.zeros((n, HEAD_DIM), F32)
        chains = []
        for h in range(heads):
            col = lambda s: pl.ds(s * gw + h * HEAD_DIM, HEAD_DIM)
            lanes = pl.ds(h * HEAD_DIM, HEAD_DIM)
            c = _chunk_local(qkv_ref[:, col(0)], qkv_ref[:, col(1)], qkv_ref[:, col(2)], bg_v[:, h:h + 1],
                             gc_all[:, heads + h:heads + h + 1], gc_rows[heads + h:heads + h + 1, :],
                             (t_ref[:, pl.ds(h * HEAD_DIM, CHUNK)], u_ref[:, lanes], w_ref[:, lanes]))
            chains.append(one_head(c, s_ref[h, 0], do_ref[:, pl.ds(h * HEAD_DIM, HEAD_DIM)], dstate[h]))
        results = _in_lockstep(chains)
        for h, (dq, dk, dv, dbeta, dgc, ds0) in enumerate(results):
            dgates = jnp.where(lane == h, dbeta, jnp.where(lane == heads + h, dgc, dgates))
        for h, (dq, dk, dv, dbeta, dgc, ds0) in enumerate(results):
            dqkv_ref[:, pl.ds(h * HEAD_DIM, HEAD_DIM)] = dq
            dqkv_ref[:, pl.ds(gw + h * HEAD_DIM, HEAD_DIM)] = dk
            dqkv_ref[:, pl.ds(2 * gw + h * HEAD_DIM, HEAD_DIM)] = dv
            dstate[h] = ds0
        row = lax.broadcasted_iota(jnp.int32, (n, n), 0)
        colm = lax.broadcasted_iota(jnp.int32, (n, n), 1)
        draw = _dot_mask((row >= colm).astype(BF16), dgates, ta=True)
        dbg_ref[...] = jnp.where(lane < heads, dgates, draw)

    last = nchunk - 1
    return pl.pallas_call(
        body, name="gdn_core_bwd", grid=(nchunk,),
        in_specs=[pl.BlockSpec((CHUNK, 3 * gw), lambda i: (last - i, 0)),
                  pl.BlockSpec((CHUNK, HEAD_DIM), lambda i: (last - i, 0)),
                  pl.BlockSpec((heads, 1, HEAD_DIM, HEAD_DIM), lambda i: (0, last - i, 0, 0))]
        + [pl.BlockSpec((CHUNK, gw), lambda i: (last - i, 0))] * 4,
        out_specs=[pl.BlockSpec((CHUNK, 3 * gw), lambda i: (last - i, 0)),
                   pl.BlockSpec((CHUNK, HEAD_DIM), lambda i: (last - i, 0))],
        out_shape=[jax.ShapeDtypeStruct((t, 3 * gw), F32), jax.ShapeDtypeStruct((t, HEAD_DIM), F32)],
        scratch_shapes=[pltpu.VMEM((heads, HEAD_DIM, HEAD_DIM), F32)],
        compiler_params=_params("arbitrary"),
    )(qkv, bg, states, *solved, do)


def _gdn_post(o, proj, z_col0, norm_w, heads, tt):
    t = o.shape[0]
    zb = z_col0 // HEAD_DIM

    def body(o_ref, z_ref, w_ref, out_ref):
        ov = o_ref[...]
        z = z_ref[...]
        rms = lax.rsqrt(jnp.mean(ov * ov, axis=-1, keepdims=True) + NORM_EPS)
        out_ref[...] = (ov * rms * w_ref[...] * (z * _sigmoid(z))).astype(BF16)

    return pl.pallas_call(
        body, name="gdn_post", grid=(t // tt, heads),
        in_specs=[pl.BlockSpec((tt, HEAD_DIM), lambda i, h: (i, h)),
                  pl.BlockSpec((tt, HEAD_DIM), lambda i, h: (i, zb + h)),
                  pl.BlockSpec((1, HEAD_DIM), lambda i, h: (0, 0))],
        out_specs=pl.BlockSpec((tt, HEAD_DIM), lambda i, h: (i, h)),
        out_shape=jax.ShapeDtypeStruct((t, heads * HEAD_DIM), BF16),
        compiler_params=_params("parallel", "parallel"),
    )(o, proj, norm_w)


def _gdn_post_backward(dcat, o, proj, z_col0, norm_w, heads, tt):
    t = o.shape[0]
    zb = z_col0 // HEAD_DIM

    def body(d_ref, o_ref, z_ref, w_ref, do_ref, dz_ref, dw_ref):
        d = d_ref[...]
        ov = o_ref[...]
        z = z_ref[...]
        w = w_ref[...]
        rms = lax.rsqrt(jnp.mean(ov * ov, axis=-1, keepdims=True) + NORM_EPS)
        ohat = ov * rms
        sg = _sigmoid(z)
        gate = z * sg
        dz_ref[...] = (d * ohat * w * (sg * (1.0 + z * (1.0 - sg)))).astype(BF16)
        don = d * gate
        dohat = don * w
        do_ref[...] = rms * (dohat - ohat * jnp.mean(dohat * ohat, axis=-1, keepdims=True))
        dw = jnp.sum(don * ohat, axis=0, keepdims=True)
        first = jnp.logical_and(pl.program_id(0) == 0, pl.program_id(1) == 0)

        @pl.when(first)
        def _():
            dw_ref[...] = dw

        @pl.when(jnp.logical_not(first))
        def _():
            dw_ref[...] += dw

    blk = pl.BlockSpec((tt, HEAD_DIM), lambda i, h: (i, h))
    return pl.pallas_call(
        body, name="gdn_post_bwd", grid=(t // tt, heads),
        in_specs=[blk, blk, pl.BlockSpec((tt, HEAD_DIM), lambda i, h: (i, zb + h)),
                  pl.BlockSpec((1, HEAD_DIM), lambda i, h: (0, 0))],
        out_specs=[blk, blk, pl.BlockSpec((1, HEAD_DIM), lambda i, h: (0, 0))],
        out_shape=[jax.ShapeDtypeStruct((t, heads * HEAD_DIM), F32),
                   jax.ShapeDtypeStruct((t, heads * HEAD_DIM), BF16),
                   jax.ShapeDtypeStruct((1, HEAD_DIM), F32)],
        compiler_params=_params("arbitrary", "arbitrary"),
    )(dcat, o, proj, norm_w)


def _pool_select(levels, group):
    out = levels[-1]
    for gi in range(len(levels) - 2, -1, -1):
        out = jnp.where(group == gi, levels[gi], out)
    return out


def _pool_counts(t, width, group):
    pos = lax.broadcasted_iota(jnp.int32, (t, width), 0)
    win = jnp.left_shift(2, group)
    return jnp.minimum(pos + 1, win).astype(F32)


def _pooled(p, group):
    levels, s, step = [], p, 1
    for _ in POOL_WINDOWS:
        s = s + _shift_down(s, step)
        levels.append(s)
        step *= 2
    cnt = _pool_counts(p.shape[0], p.shape[1], group)
    return _pool_select(levels, group) / cnt - p, cnt


def _pool_forward(proj, p_col0, pool_w, pool_scale):
    t = proj.shape[0]
    groups, cg, _ = pool_w.shape
    pb = p_col0 // cg

    def body(p_ref, w_ref, s_ref, o_ref):
        pooled, _ = _pooled(p_ref[...], pl.program_id(0))
        o_ref[...] = (_bdot(pooled, w_ref[0]) * s_ref[...]).astype(BF16)

    return pl.pallas_call(
        body, name="pool_fwd", grid=(groups,),
        in_specs=[pl.BlockSpec((t, cg), lambda g: (0, pb + g)), pl.BlockSpec((1, cg, cg), lambda g: (g, 0, 0)),
                  pl.BlockSpec((1, cg), lambda g: (0, g))],
        out_specs=pl.BlockSpec((t, cg), lambda g: (0, g)),
        out_shape=jax.ShapeDtypeStruct((t, groups * cg), BF16),
        compiler_params=_params("parallel"),
    )(proj, pool_w, pool_scale)


def _pool_backward(dcat, d_col0, proj, p_col0, pool_w, pool_scale):
    t = proj.shape[0]
    groups, cg, _ = pool_w.shape
    pb = p_col0 // cg
    db = d_col0 // cg

    def body(d_ref, p_ref, w_ref, s_ref, dp_ref, dw_ref, ds_ref):
        group = pl.program_id(0)
        pooled, cnt = _pooled(p_ref[...], group)
        w = w_ref[0]
        d = d_ref[...]
        mixed = _bdot(pooled, w)
        ds_ref[...] = jnp.sum(d * mixed, axis=0, keepdims=True)
        dmixed = d * s_ref[...]
        dw_ref[0] = _bdot(pooled, dmixed, ta=True)
        dpooled = _bdot(dmixed, w, tb=True)
        levels, s, step = [], dpooled / cnt, 1
        for _ in POOL_WINDOWS:
            s = s + _shift_up(s, step)
            levels.append(s)
            step *= 2
        dp_ref[...] = (_pool_select(levels, group) - dpooled).astype(BF16)

    return pl.pallas_call(
        body, name="pool_bwd", grid=(groups,),
        in_specs=[pl.BlockSpec((t, cg), lambda g: (0, db + g)), pl.BlockSpec((t, cg), lambda g: (0, pb + g)),
                  pl.BlockSpec((1, cg, cg), lambda g: (g, 0, 0)), pl.BlockSpec((1, cg), lambda g: (0, g))],
        out_specs=[pl.BlockSpec((t, cg), lambda g: (0, g)), pl.BlockSpec((1, cg, cg), lambda g: (g, 0, 0)),
                   pl.BlockSpec((1, cg), lambda g: (0, g))],
        out_shape=[jax.ShapeDtypeStruct((t, groups * cg), BF16), jax.ShapeDtypeStruct((groups, cg, cg), F32),
                   jax.ShapeDtypeStruct((1, groups * cg), F32)],
        compiler_params=_params("parallel"),
    )(dcat, proj, pool_w, pool_scale)


def _attention(q, k, v, tq):
    t, d = q.shape
    m = k.shape[0]
    dh = d // XATTN_HEADS
    scale = dh ** -0.5

    def body(q_ref, k_ref, v_ref, o_ref):
        s = _bdot(q_ref[...], k_ref[...], tb=True) * scale
        s = s - jnp.max(s, axis=-1, keepdims=True)
        e = jnp.exp(s)
        p = e / jnp.sum(e, axis=-1, keepdims=True)
        o_ref[...] = _bdot(p, v_ref[...]).astype(BF16)

    return pl.pallas_call(
        body, name="xattn_fwd", grid=(XATTN_HEADS, t // tq),
        in_specs=[pl.BlockSpec((tq, dh), lambda h, i: (i, h)), pl.BlockSpec((m, dh), lambda h, i: (0, h)),
                  pl.BlockSpec((m, dh), lambda h, i: (0, h))],
        out_specs=pl.BlockSpec((tq, dh), lambda h, i: (i, h)),
        out_shape=jax.ShapeDtypeStruct((t, d), BF16),
        compiler_params=_params("parallel", "parallel"),
    )(q, k, v)


def _attention_backward(q, k, v, do, tq):
    t, d = q.shape
    m = k.shape[0]
    dh = d // XATTN_HEADS
    scale = dh ** -0.5

    def body(q_ref, k_ref, v_ref, do_ref, dq_ref, dk_ref, dv_ref, dk_acc, dv_acc):
        i = pl.program_id(1)
        qv, kv, vv, dov = q_ref[...], k_ref[...], v_ref[...], do_ref[...]
        s = _bdot(qv, kv, tb=True) * scale
        s = s - jnp.max(s, axis=-1, keepdims=True)
        e = jnp.exp(s)
        p = e / jnp.sum(e, axis=-1, keepdims=True)
        dp = _bdot(dov, vv, tb=True)
        ds = p * (dp - jnp.sum(dp * p, axis=-1, keepdims=True)) * scale
        dq_ref[...] = _bdot(ds, kv).astype(BF16)
        dv_part = _bdot(p, dov, ta=True)
        dk_part = _bdot(ds, qv, ta=True)

        @pl.when(i == 0)
        def _():
            dk_acc[...] = dk_part
            dv_acc[...] = dv_part

        @pl.when(i > 0)
        def _():
            dk_acc[...] += dk_part
            dv_acc[...] += dv_part

        @pl.when(i == pl.num_programs(1) - 1)
        def _():
            dk_ref[...] = dk_acc[...].astype(BF16)
            dv_ref[...] = dv_acc[...].astype(BF16)

    qblk = pl.BlockSpec((tq, dh), lambda h, i: (i, h))
    kblk = pl.BlockSpec((m, dh), lambda h, i: (0, h))
    return pl.pallas_call(
        body, name="xattn_bwd", grid=(XATTN_HEADS, t // tq),
        in_specs=[qblk, kblk, kblk, qblk],
        out_specs=[qblk, kblk, kblk],
        out_shape=[jax.ShapeDtypeStruct((t, d), BF16), jax.ShapeDtypeStruct((m, d), BF16),
                   jax.ShapeDtypeStruct((m, d), BF16)],
        scratch_shapes=[pltpu.VMEM((m, dh), F32), pltpu.VMEM((m, dh), F32)],
        compiler_params=_params("parallel", "arbitrary"),
    )(q, k, v, do)


def _ln_backward_rows(name, dmain, dres, xhat, rstd, gamma, tm):
    t, d = xhat.shape

    def body(m_ref, r_ref, x_ref, s_ref, g_ref, du_ref, dub_ref, dg_ref, db_ref):
        du, dg, db = _ln_backward_math(m_ref[...] + ALPHA * r_ref[...], x_ref[...], s_ref[...], g_ref[...])
        du_ref[...] = du
        dub_ref[...] = du.astype(BF16)
        first = pl.program_id(0) == 0

        @pl.when(first)
        def _():
            dg_ref[...] = dg
            db_ref[...] = db

        @pl.when(jnp.logical_not(first))
        def _():
            dg_ref[...] += dg
            db_ref[...] += db

    row = pl.BlockSpec((tm, d), lambda i: (i, 0))
    vec = pl.BlockSpec((1, d), lambda i: (0, 0))
    return pl.pallas_call(
        body, name=name, grid=(t // tm,),
        in_specs=[row, row, row, pl.BlockSpec((tm, 1), lambda i: (i, 0)), vec],
        out_specs=[row, row, vec, vec],
        out_shape=[jax.ShapeDtypeStruct((t, d), F32), jax.ShapeDtypeStruct((t, d), BF16),
                   jax.ShapeDtypeStruct((1, d), F32), jax.ShapeDtypeStruct((1, d), F32)],
        compiler_params=_params("arbitrary"),
    )(dmain, dres, xhat, rstd, gamma)


def _loss_and_ln_backward(xhat, rstd, gamma, beta, target, tm):
    t, d = xhat.shape

    def body(x_ref, r_ref, g_ref, b_ref, t_ref, du_ref, dub_ref, dg_ref, db_ref, loss_ref):
        xh = x_ref[...]
        g = g_ref[...]
        diff = xh * g + b_ref[...] - t_ref[...]
        part = jnp.sum(jnp.sum(diff * diff, axis=1, keepdims=True), axis=0, keepdims=True) * (0.5 / d)
        dy = diff * (1.0 / d)
        du, dg, db = _ln_backward_math(dy, xh, r_ref[...], g)
        du_ref[...] = du
        dub_ref[...] = du.astype(BF16)
        lossrow = jnp.broadcast_to(part, (1, HEAD_DIM))
        first = pl.program_id(0) == 0

        @pl.when(first)
        def _():
            dg_ref[...] = dg
            db_ref[...] = db
            loss_ref[...] = lossrow

        @pl.when(jnp.logical_not(first))
        def _():
            dg_ref[...] += dg
            db_ref[...] += db
            loss_ref[...] += lossrow

    row = pl.BlockSpec((tm, d), lambda i: (i, 0))
    vec = pl.BlockSpec((1, d), lambda i: (0, 0))
    return pl.pallas_call(
        body, name="loss_ln3_bwd", grid=(t // tm,),
        in_specs=[row, pl.BlockSpec((tm, 1), lambda i: (i, 0)), vec, vec, row],
        out_specs=[row, row, vec, vec, pl.BlockSpec((1, HEAD_DIM), lambda i: (0, 0))],
        out_shape=[jax.ShapeDtypeStruct((t, d), F32), jax.ShapeDtypeStruct((t, d), BF16),
                   jax.ShapeDtypeStruct((1, d), F32), jax.ShapeDtypeStruct((1, d), F32),
                   jax.ShapeDtypeStruct((1, HEAD_DIM), F32)],
        compiler_params=_params("arbitrary"),
    )(xhat, rstd, gamma, beta, target)


def _after(token, a):
    return a if token is None else a + token[:1, :1].astype(a.dtype)


def _pick(n, prefs):
    for p in prefs:
        if n % p == 0:
            return p
    return n


def _local_step(x, mem, target, w, x_bf=None):
    t, d = x.shape
    heads = w["a_log"].shape[1]
    gw = heads * HEAD_DIM
    groups, cg, _ = w["pool_w"].shape
    pw = groups * cg
    n_main = 4 * gw + pw
    in_cols = n_main + 2 * heads
    s_in = w["w_in_t"].shape[0]

    tm = _pick(t, (512, 256, 128))
    tm_ln = _pick(t, (256, 128))
    tm_big = _pick(t, (1024, 512, 256, 128))
    tk = _pick(d, K_STEPS)

    w_in_t = w["w_in_t"].reshape(in_cols, d)
    w_p_t = w_in_t[4 * gw + 2 * heads:]
    w_ba_t = jnp.pad(w_in_t[4 * gw:4 * gw + 2 * heads], ((0, HEAD_DIM - 2 * heads), (0, 0)))
    x_bf = x.astype(BF16) if x_bf is None else x_bf
    mem_bf = mem.astype(BF16)

    tn_d = _pick(d, (1024, 512, 256, 128))
    proj = _plain("proj_main", x_bf, w_in_t, tb=True, n_used=4 * gw, tm=tm_big, tn=_pick(4 * gw, (1024, 512, 256, 128)),
                  tk=tk, out_dtype=F32)
    pproj = _plain("proj_pool", x_bf, w_p_t, tb=True, tm=tm_big, tn=_pick(pw, (1024, 512, 256, 128)), tk=tk, out_dtype=F32)
    ea, dtb = _gate_vectors(w["a_log"], w["dt_bias"], heads)
    vec128 = lambda i, j: (0, 0)
    ba, bg = _matmul(
        "proj_gates", x_bf, w_ba_t, tb=True, tm=tm, tn=HEAD_DIM, tk=tk,
        extra=[(ea, (1, HEAD_DIM), vec128), (dtb, (1, HEAD_DIM), vec128)],
        outs=[(jax.ShapeDtypeStruct((t, HEAD_DIM), F32), (tm, HEAD_DIM), _tile)] * 2,
        epilogue=_gates_epilogue(heads))
    qkv = _gdn_pre(proj, w["conv_w"], heads)
    o_gdn, states, *solved = _gdn_core(qkv, bg, heads)
    cat_g = _gdn_post(o_gdn, proj, 3 * gw, w["gdn_norm_w"], heads, tm)
    token = yield ("pass", 1, cat_g)
    cat_p = _pool_forward(pproj, 0, w["pool_w"], _after(token, w["pool_scale"]))
    cat = jnp.concatenate([cat_g, cat_p], axis=1)
    w = {**w, **(yield ("weights", 1, cat))}
    h1, h1_bf, xhat1, rstd1 = _ln_forward("mix_ln1", cat, w["w_out"], x, w["ln1_g"], w["ln1_b"], tm=tm_ln, tk=tk)

    h1_bf = _after((yield ("relay", None, h1_bf)), h1_bf)
    q = _plain("xattn_q", h1_bf, w["xq_w"], tm=tm, tn=tn_d, tk=tk, out_dtype=BF16)
    mlen = mem.shape[0]
    tm_mem = _pick(mlen, (256, 128))
    k = _plain("xattn_k", mem_bf, w["xk_w"], tm=tm_mem, tn=tn_d, tk=tk, out_dtype=BF16)
    v = _plain("xattn_v", mem_bf, w["xv_w"], tm=tm_mem, tn=tn_d, tk=tk, out_dtype=BF16)
    att = _attention(q, k, v, tm)
    h2, h2_bf, xhat2, rstd2 = _ln_forward("xo_ln2", att, w["xo_w"], h1, w["ln2_g"], w["ln2_b"], tm=tm_ln, tk=tk)

    w = {**w, **(yield ("weights", 2, h2_bf))}
    s_up = w["w_up3"].shape[0]
    ff = s_up * w["w_up3"].shape[2]
    tn_f = _pick(ff // s_up, (1024, 512, 256, 128))

    def up_epi(acc, ex, out, i):
        r = jnp.maximum(acc, 0.0)
        out[0][...] = (r * r).astype(BF16)
        out[1][...] = (2.0 * r).astype(BF16)

    act, act_grad = _matmul(
        "mlp_up", h2_bf, w["w_up3"], b_blocks=s_up, tm=tm_big, tn=tn_f, tk=tk,
        outs=[(jax.ShapeDtypeStruct((t, ff), BF16), (tm_big, tn_f), _tile)] * 2, epilogue=up_epi)
    w = {**w, **(yield ("weights", 3, act))}
    tk_f = _pick(ff, K_STEPS)
    xhat3, rstd3 = _ln_forward("down_ln3", act, w["w_down"], h2, w["ln3_g"], w["ln3_b"], tm=tm, tk=tk_f, want_h=False)

    grads = {}
    du3, du3_bf, grads["ln3_g"], grads["ln3_b"], loss = _loss_and_ln_backward(
        xhat3, rstd3, w["ln3_g"], w["ln3_b"], target, tm_ln)

    def dup_epi(acc, ex, out, i):
        out[0][...] = (acc * ex[0][...].astype(F32)).astype(BF16)

    dup = _matmul(
        "mlp_down_dx", du3_bf, w["w_down"], tb=True, tm=tm_big, tn=tn_f, tk=tk,
        extra=[(act_grad, (tm_big, tn_f), _tile)],
        outs=[(jax.ShapeDtypeStruct((t, ff), BF16), (tm_big, tn_f), _tile)], epilogue=dup_epi)[0]
    tk_t = _pick(t, K_STEPS)
    tm_w = _pick(d, (512, 256, 128))
    grads["w_down"] = _plain("mlp_down_dw", act, du3_bf, ta=True, tm=_pick(ff, (512, 256, 128)), tn=d, tk=tk_t,
                             out_dtype=F32)
    grads["w_up3"] = _plain("mlp_up_dw", h2_bf, dup, ta=True, tm=tm_w, tn=ff // s_up, tk=tk_t, out_dtype=F32, out3=s_up,
                            n_outer=True)
    token = yield ("grads", 0, {n: grads.pop(n) for n in ("w_down", "w_up3")})
    dh2 = _plain("mlp_up_dx", dup, w["w_up3"], tb=True, b_blocks=s_up, tm=tm_big, tn=tn_d,
                 tk=_pick(ff // s_up, K_STEPS), out_dtype=F32)
    du2, du2_bf, grads["ln2_g"], grads["ln2_b"] = _ln_backward_rows(
        "ln2_bwd", dh2, du3, xhat2, rstd2, _after(token, w["ln2_g"]), tm_ln)
    token = yield ("poll", 0, du2_bf)

    grads["xo_w"] = _plain("xo_dw", att, du2_bf, ta=True, tm=tm_w, tn=d, tk=tk_t, out_dtype=F32)
    datt = _plain("xo_dx", du2_bf, w["xo_w"], tb=True, tm=tm, tn=tn_d, tk=tk, out_dtype=BF16)
    dq, dk, dv = _attention_backward(q, k, v, datt, tm)
    tk_m = _pick(mlen, (256, 128))
    grads["xq_w"] = _plain("xq_dw", h1_bf, dq, ta=True, tm=tm_w, tn=d, tk=tk_t, out_dtype=F32)
    grads["xk_w"] = _plain("xk_dw", mem_bf, dk, ta=True, tm=tm_w, tn=tn_d, tk=tk_m, out_dtype=F32)
    grads["xv_w"] = _plain("xv_dw", mem_bf, dv, ta=True, tm=tm_w, tn=tn_d, tk=tk_m, out_dtype=F32)
    du1, du1_bf, grads["ln1_g"], grads["ln1_b"] = _ln_backward(
        "xq_dx_ln1", dq, w["xq_w"], du2, xhat1, rstd1, _after(token, w["ln1_g"]), tm=tm_ln, tk=tk)

    grads["w_out"] = _plain("out_dw", cat, du1_bf, ta=True, tm=tm_w, tn=d, tk=tk_t, out_dtype=F32)
    token = yield ("grads", 1, {n: grads.pop(n) for n in ("xo_w", "xq_w", "xk_w", "xv_w", "w_out")})
    dcat = _plain("out_dx", du1_bf, w["w_out"], tb=True, tm=tm, tn=tn_d, tk=tk, out_dtype=F32)
    dp, grads["pool_w"], grads["pool_scale"] = _pool_backward(dcat, gw, pproj, 0, w["pool_w"],
                                                              _after(token, w["pool_scale"]))
    do_gdn, dz, grads["gdn_norm_w"] = _gdn_post_backward(dcat, o_gdn, proj, 3 * gw, _after(token, w["gdn_norm_w"]),
                                                         heads, tm)
    dqkv, dbg = _gdn_core_backward(qkv, bg, states, solved, do_gdn, heads)
    token = yield ("poll", 1, dqkv)
    dqkv_pre, grads["conv_w"] = _gdn_pre_backward(proj, _after(token, w["conv_w"]), dqkv, heads)
    dba, dalog_row, ddt_row = _gates_backward(ba, bg, dbg, ea, dtb, heads)
    grads["a_log"] = dalog_row[:, heads:2 * heads]
    grads["dt_bias"] = ddt_row[:, heads:2 * heads]

    dproj = jnp.concatenate([dqkv_pre, dz, dp], axis=1)
    dw_main = _plain("proj_dw", dproj, x_bf, ta=True, tm=_pick(n_main, (512, 256, 128)), tn=d, tk=tk_t, out_dtype=F32)
    dw_ba = _plain("proj_gates_dw", dba, x_bf, ta=True, tm=HEAD_DIM, tn=tn_d, tk=tk_t, out_dtype=F32)
    dw_in_t = jnp.concatenate([dw_main[:4 * gw], dw_ba[:2 * heads], dw_main[4 * gw:]], axis=0)
    grads["w_in_t"] = dw_in_t.reshape(s_in, in_cols // s_in, d)

    def dx_epi(acc, ex, out, i):
        out[0][...] = acc + ex[1][...] + ALPHA * ex[0][...]

    def add_epi(acc, ex, out, i):
        out[0][...] = acc + ex[0][...]

    token = yield ("grads", 2, {n: grads.pop(n) for n in ("w_in_t", "pool_w")})
    dx_gates = _plain("proj_gates_dx", dba, _after(token, w_ba_t), tm=tm, tn=tn_d, tk=HEAD_DIM, out_dtype=F32)
    out_tile = [(jax.ShapeDtypeStruct((t, d), F32), (tm, tn_d), _tile)]
    dx_pool = _matmul("proj_pool_dx", dp, w_p_t, tm=tm, tn=tn_d, tk=_pick(pw, K_STEPS),
                      extra=[(dx_gates, (tm, tn_d), _tile)], outs=out_tile, epilogue=add_epi)[0]
    grad_x = _matmul(
        "proj_dx", dproj, w_in_t, k_used=4 * gw, tm=tm, tn=tn_d, tk=_pick(4 * gw, K_STEPS),
        extra=[(du1, (tm, tn_d), _tile), (dx_pool, (tm, tn_d), _tile)], outs=out_tile, epilogue=dx_epi)[0]
    yield ("poll", 2, grad_x)
    return loss, grad_x, grads


def _adamw(name, w, g, m, v):
    r, c = w.shape
    if r % 8 == 0:
        tr = _pick(r, (256, 128, 64, 32, 16, 8))
        blk, steps = pl.BlockSpec((tr, c), lambda i: (i, 0)), r // tr
    else:
        tc = _pick(c, (256, 128))
        blk, steps = pl.BlockSpec((r, tc), lambda i: (0, i)), c // tc
    c1 = 1.0 - ADAM_B1 ** ADAM_STEP
    c2 = 1.0 - ADAM_B2 ** ADAM_STEP

    def body(w_ref, g_ref, m_ref, v_ref, d_ref, mo_ref, vo_ref, go_ref):
        gv = g_ref[...]
        mn = ADAM_B1 * m_ref[...] + (1.0 - ADAM_B1) * gv
        vn = ADAM_B2 * v_ref[...] + (1.0 - ADAM_B2) * (gv * gv)
        d_ref[...] = -ADAM_LR * ((mn / c1) / (jnp.sqrt(vn / c2) + ADAM_EPS) + ADAM_WD * w_ref[...])
        mo_ref[...] = mn
        vo_ref[...] = vn
        go_ref[...] = gv

    return pl.pallas_call(
        body, name=name, grid=(steps,), in_specs=[blk] * 4, out_specs=[blk] * 4,
        out_shape=[jax.ShapeDtypeStruct((r, c), F32)] * 4,
        compiler_params=_params("parallel"),
    )(w, g, m, v)


def _place():
    x, y, c = lax.axis_index("x"), lax.axis_index("y"), lax.axis_index("c")
    chips = [(1 - x, y), (x, 1 - y), (1 - x, 1 - y)]
    return x, y, c, chips


HBM = pl.BlockSpec(memory_space=pltpu.HBM)


SEM = pl.BlockSpec(memory_space=pltpu.SEMAPHORE)
ANY = pl.BlockSpec(memory_space=pl.ANY)
EFFECT = pltpu.SideEffectType.DATAFLOW_SIDE_EFFECTING


def _in_hbm(a):
    return pltpu.with_memory_space_constraint(a, pltpu.HBM)


def _remote(src, dst, send_sem, recv_sem, to):
    return pltpu.make_async_remote_copy(src_ref=src, dst_ref=dst, send_sem=send_sem, recv_sem=recv_sem,
                                        device_id=to, device_id_type=MESH)


def _by_rows(rows):
    return rows % 32 == 0


def _half_shape(rows, cols):
    return (rows // 2, cols) if _by_rows(rows) else (rows, cols // 2)


def _half(ref, which, *lead):
    rows, cols = ref.shape[-2:]
    if _by_rows(rows):
        return ref.at[(*lead, pl.ds(which * (rows // 2), rows // 2))]
    return ref.at[(*lead, slice(None), pl.ds(which * (cols // 2), cols // 2))]


def _landed(lands, i, shard_index, which):
    return _half(lands[i], which, shard_index)


def _routes():
    x, y, c, _ = _place()
    first = (jnp.where(c == 0, 1 - x, x), jnp.where(c == 0, y, 1 - y))
    second = (jnp.where(c == 0, x, 1 - x), jnp.where(c == 0, 1 - y, y))
    return first, second, (1 - x, 1 - y)


def _shard_of(chip):
    return 2 * chip[0] + chip[1]


def _gather_start(name, shards, after, relayed=()):
    n = len(shards)
    lands = [lax.empty((N_SHARD,) + s.shape, s.dtype) for s in shards]

    def body(*refs):
        ins, zones = refs[:n], refs[n:2 * n]
        ici_send, ici_recv, own_send, own_recv = refs[2 * n + 1:2 * n + 5]
        token = refs[-1]
        x, y, c, chips = _place()
        me = 2 * x + y
        first, _, _ = _routes()
        for i in range(n):
            if i in relayed:
                _remote(_half(ins[i], c), _landed(zones, i, me, c), ici_send.at[3 * i], ici_recv.at[3 * i],
                        (*first, c)).start()
                continue
            for j, chip in enumerate(chips):
                _remote(_half(ins[i], c), _landed(zones, i, me, c), ici_send.at[3 * i + j],
                        ici_recv.at[3 * i + j], (*chip, c)).start()
        for i in range(n):
            _remote(ins[i], zones[i].at[me], own_send.at[i], own_recv.at[i], (x, y, 1 - c)).start()
        token[...] = jnp.zeros_like(token)

    dma = pltpu.SemaphoreType.DMA
    outs = pl.pallas_call(
        body, name=name,
        in_specs=[HBM] * (2 * n) + [ANY],
        out_shape=(dma((3 * n,)), dma((3 * n,)), dma((n,)), dma((n,)),
                   *[pltpu.HBM(a.shape, a.dtype) for a in shards + lands], jax.ShapeDtypeStruct((8, LANES), F32)),
        out_specs=(SEM, SEM, SEM, SEM, *[HBM] * (2 * n), pl.BlockSpec(memory_space=pltpu.VMEM)),
        input_output_aliases={k: 4 + k for k in range(2 * n)},
        compiler_params=pltpu.CompilerParams(has_side_effects=EFFECT),
    )(*[_in_hbm(a) for a in shards + lands], after)
    sems = dict(zip(("ici_send", "ici_recv", "own_send", "own_recv"), outs[:4]))
    return sems, list(outs[4:4 + n]), list(outs[4 + n:4 + 2 * n]), outs[-1]


def _gather_forward(name, idx, lands, sems, after):
    n = len(idx)

    def body(*refs):
        zones = refs[:n]
        ici_recv = refs[n]
        fwd_send, fwd_recv = refs[n + 2], refs[n + 3]
        x, y, c, chips = _place()
        for k, i in enumerate(idx):
            for j, chip in enumerate(chips):
                half = _landed(zones, k, 2 * chip[0] + chip[1], c)
                _remote(half, half, fwd_send.at[3 * k + j], ici_recv.at[3 * i + j], (*chip, c)).wait_recv()
                _remote(half, half, fwd_send.at[3 * k + j], fwd_recv.at[3 * k + j], (x, y, 1 - c)).start()
        refs[-1][...] = jnp.zeros_like(refs[-1])

    dma = pltpu.SemaphoreType.DMA
    outs = pl.pallas_call(
        body, name=name,
        in_specs=[HBM] * n + [SEM, ANY],
        out_shape=(dma((3 * n,)), dma((3 * n,)), *[pltpu.HBM(a.shape, a.dtype) for a in lands],
                   jax.ShapeDtypeStruct((8, LANES), F32)),
        out_specs=(SEM, SEM, *[HBM] * n, pl.BlockSpec(memory_space=pltpu.VMEM)),
        input_output_aliases={k: 2 + k for k in range(n)},
        compiler_params=pltpu.CompilerParams(has_side_effects=EFFECT),
    )(*lands, sems["ici_recv"], after)
    return (outs[0], outs[1]), list(outs[2:2 + n]), outs[-1]


def _gather_wait(name, idx, shards, lands, sems, fwd, after):
    n = len(idx)

    def body(*refs):
        ins, zones = refs[:n], refs[n:2 * n]
        ici_send, own_send, own_recv, fwd_send, fwd_recv = refs[2 * n:2 * n + 5]
        x, y, c, chips = _place()
        me = 2 * x + y
        for k, i in enumerate(idx):
            mine = _half(ins[k], c)
            for j, chip in enumerate(chips):
                theirs = 2 * chip[0] + chip[1]
                _remote(mine, _landed(zones, k, me, c), ici_send.at[3 * i + j], fwd_recv.at[3 * k + j],
                        (*chip, c)).wait_send()
                sent = _landed(zones, k, theirs, c)
                _remote(sent, sent, fwd_send.at[3 * k + j], fwd_recv.at[3 * k + j], (x, y, 1 - c)).wait_send()
                passed = _landed(zones, k, theirs, 1 - c)
                _remote(passed, passed, fwd_send.at[3 * k + j], fwd_recv.at[3 * k + j], (x, y, 1 - c)).wait_recv()
            own = _remote(ins[k], zones[k].at[me], own_send.at[i], own_recv.at[i], (x, y, 1 - c))
            own.wait_send()
            own.wait_recv()

    outs = pl.pallas_call(
        body, name=name,
        in_specs=[HBM] * (2 * n) + [SEM] * 5 + [ANY],
        out_shape=tuple(pltpu.HBM(a.shape, a.dtype) for a in lands),
        out_specs=tuple([HBM] * n),
        input_output_aliases={n + k: k for k in range(n)},
        compiler_params=pltpu.CompilerParams(has_side_effects=EFFECT),
    )(*shards, *lands, sems["ici_send"], sems["own_send"], sems["own_recv"], fwd[0], fwd[1], after)
    return list(outs)


def _gather_relay(name, idx, shards, lands, sems, after):
    n = len(idx)

    def body(*refs):
        ins, zones, ici_recv = refs[:n], refs[n:2 * n], refs[2 * n]
        relay_send, relay_recv, pass_send, pass_recv = refs[2 * n + 2:2 * n + 6]
        x, y, c, _ = _place()
        first, second, _ = _routes()
        for k, i in enumerate(idx):
            landed = _landed(zones, k, _shard_of(first), c)
            _remote(landed, landed, pass_send.at[k], ici_recv.at[3 * i], (*first, c)).wait_recv()
            _remote(_half(ins[k], c), _landed(zones, k, 2 * x + y, c), relay_send.at[2 * k], relay_recv.at[2 * k],
                    (*second, c)).start()
            _remote(landed, landed, relay_send.at[2 * k + 1], relay_recv.at[2 * k + 1], (*second, c)).start()
            _remote(landed, landed, pass_send.at[k], pass_recv.at[k], (x, y, 1 - c)).start()
        refs[-1][...] = jnp.zeros_like(refs[-1])

    dma = pltpu.SemaphoreType.DMA
    outs = pl.pallas_call(
        body, name=name,
        in_specs=[HBM] * (2 * n) + [SEM, ANY],
        out_shape=(dma((2 * n,)), dma((2 * n,)), dma((n,)), dma((n,)), *[pltpu.HBM(a.shape, a.dtype) for a in lands],
                   jax.ShapeDtypeStruct((8, LANES), F32)),
        out_specs=(SEM, SEM, SEM, SEM, *[HBM] * n, pl.BlockSpec(memory_space=pltpu.VMEM)),
        input_output_aliases={n + k: 4 + k for k in range(n)},
        compiler_params=pltpu.CompilerParams(has_side_effects=EFFECT),
    )(*shards, *lands, sems["ici_recv"], after)
    return outs[:4], list(outs[4:4 + n]), outs[-1]


def _gather_forward_relayed(name, ks, lands, relay, after):
    n = len(ks)

    def body(*refs):
        zones, relay_recv = refs[:n], refs[n]
        fwd_send, fwd_recv = refs[n + 2], refs[n + 3]
        x, y, c, _ = _place()
        _, second, diagonal = _routes()
        for p, k in enumerate(ks):
            for j, chip in enumerate((second, diagonal)):
                landed = _landed(zones, p, _shard_of(chip), c)
                _remote(landed, landed, fwd_send.at[2 * p + j], relay_recv.at[2 * k + j], (*second, c)).wait_recv()
                _remote(landed, landed, fwd_send.at[2 * p + j], fwd_recv.at[2 * p + j], (x, y, 1 - c)).start()

    dma = pltpu.SemaphoreType.DMA
    outs = pl.pallas_call(
        body, name=name,
        in_specs=[HBM] * n + [SEM, ANY],
        out_shape=(dma((2 * n,)), dma((2 * n,)), *[pltpu.HBM(a.shape, a.dtype) for a in lands]),
        out_specs=(SEM, SEM, *[HBM] * n),
        input_output_aliases={k: 2 + k for k in range(n)},
        compiler_params=pltpu.CompilerParams(has_side_effects=EFFECT),
    )(*lands, relay[1], after)
    return (outs[0], outs[1]), list(outs[2:])


def _gather_wait_relayed(name, idx, ks, shards, lands, sems, relay, fwd, after):
    n = len(idx)

    def body(*refs):
        ins, zones = refs[:n], refs[n:2 * n]
        ici_send, own_send, own_recv, relay_send, pass_send, pass_recv, fwd_send, fwd_recv = refs[2 * n:2 * n + 8]
        x, y, c, _ = _place()
        me = 2 * x + y
        sibling = (x, y, 1 - c)
        first, second, diagonal = _routes()
        for p, (i, k) in enumerate(zip(idx, ks)):
            mine, at_peer = _half(ins[p], c), _landed(zones, p, me, c)
            from_first = _landed(zones, p, _shard_of(first), c)
            _remote(mine, at_peer, ici_send.at[3 * i], pass_recv.at[k], (*first, c)).wait_send()
            _remote(mine, at_peer, relay_send.at[2 * k], pass_recv.at[k], (*second, c)).wait_send()
            _remote(from_first, from_first, relay_send.at[2 * k + 1], pass_recv.at[k], (*second, c)).wait_send()
            _remote(from_first, from_first, pass_send.at[k], pass_recv.at[k], sibling).wait_send()
            theirs = _landed(zones, p, _shard_of(second), 1 - c)
            _remote(theirs, theirs, pass_send.at[k], pass_recv.at[k], sibling).wait_recv()
            for j, (sent, got) in enumerate(((second, first), (diagonal, diagonal))):
                out_half = _landed(zones, p, _shard_of(sent), c)
                _remote(out_half, out_half, fwd_send.at[2 * p + j], fwd_recv.at[2 * p + j], sibling).wait_send()
                in_half = _landed(zones, p, _shard_of(got), 1 - c)
                _remote(in_half, in_half, fwd_send.at[2 * p + j], fwd_recv.at[2 * p + j], sibling).wait_recv()
            own = _remote(ins[p], zones[p].at[me], own_send.at[i], own_recv.at[i], sibling)
            own.wait_send()
            own.wait_recv()

    outs = pl.pallas_call(
        body, name=name,
        in_specs=[HBM] * (2 * n) + [SEM] * 8 + [ANY],
        out_shape=tuple(pltpu.HBM(a.shape, a.dtype) for a in lands),
        out_specs=tuple([HBM] * n),
        input_output_aliases={n + k: k for k in range(n)},
        compiler_params=pltpu.CompilerParams(has_side_effects=EFFECT),
    )(*shards, *lands, sems["ici_send"], sems["own_send"], sems["own_recv"], relay[0], relay[2], relay[3],
      fwd[0], fwd[1], after)
    return list(outs)


def _all_reduce_small(name, slab, after=None):
    r, width = slab.shape
    ndev = 8

    def body(x_ref, after_ref, out_ref, buf, send_sems, recv_sems):
        x, y, c, _ = _place()
        me = 4 * x + 2 * y + c
        buf[me] = x_ref[...]
        copies = []
        for k in range(1, ndev):
            peer = jnp.bitwise_xor(me, k)
            to = (peer // 4, (peer // 2) % 2, peer % 2)
            cp = pltpu.make_async_remote_copy(src_ref=x_ref, dst_ref=buf.at[me], send_sem=send_sems.at[k - 1],
                                              recv_sem=recv_sems.at[k - 1], device_id=to, device_id_type=MESH)
            cp.start()
            copies.append(cp)
        for k in range(1, ndev):
            peer = jnp.bitwise_xor(me, k)
            pltpu.make_async_remote_copy(src_ref=x_ref, dst_ref=buf.at[peer], send_sem=send_sems.at[k - 1],
                                         recv_sem=recv_sems.at[k - 1], device_id=(x, y, c),
                                         device_id_type=MESH).wait_recv()
        for cp in copies:
            cp.wait_send()
        total = buf[0]
        for d in range(1, ndev):
            total = total + buf[d]
        out_ref[...] = total

    return pl.pallas_call(
        body, name=name,
        in_specs=[pl.BlockSpec(memory_space=pltpu.VMEM), ANY], out_specs=pl.BlockSpec(memory_space=pltpu.VMEM),
        out_shape=jax.ShapeDtypeStruct((r, width), F32),
        scratch_shapes=[pltpu.VMEM((ndev, r, width), F32), pltpu.SemaphoreType.DMA((ndev - 1,)),
                        pltpu.SemaphoreType.DMA((ndev - 1,))],
        compiler_params=pltpu.CompilerParams(vmem_limit_bytes=VMEM_LIMIT),
    )(slab, slab if after is None else after)


def _half_tiling(rows, cols):
    if _by_rows(rows):
        tr = _pick(rows // 2, (256, 128, 64, 32, 16))
        nb = (rows // 2) // tr
        return (tr, cols), nb, (lambda which, b: (which * nb + b, 0)), (lambda b: (b, 0))
    tc = _pick(cols // 2, (256, 128))
    nb = (cols // 2) // tc
    return (rows, tc), nb, (lambda which, b: (0, which * nb + b)), (lambda b: (0, b))


def _chip_partial(name, grad, other, core):
    s, r, cdim = grad.shape
    blk, nb, whole, within = _half_tiling(r, cdim)

    def body(core_ref, g_ref, o_ref, out_ref):
        out_ref[...] = (g_ref[...] + o_ref[...]).astype(BF16)

    return pl.pallas_call(
        body, name=name,
        grid_spec=pltpu.PrefetchScalarGridSpec(
            num_scalar_prefetch=1, grid=(s, nb),
            in_specs=[pl.BlockSpec((None,) + blk, lambda j, b, core_ref: (j,) + whole(core_ref[0], b)),
                      pl.BlockSpec((None,) + blk, lambda j, b, core_ref: (j,) + within(b))],
            out_specs=pl.BlockSpec((None,) + blk, lambda j, b, core_ref: (j,) + within(b))),
        out_shape=jax.ShapeDtypeStruct((s,) + _half_shape(r, cdim), BF16),
        compiler_params=_params("parallel", "parallel"),
    )(core, grad, other)


def _partial_copies(ins, zones, send_sems, recv_sems):
    x, y, c, chips = _place()
    return [_remote(ins[i].at[2 * chip[0] + chip[1]], zones[i].at[j], send_sems.at[3 * i + j],
                    recv_sems.at[3 * i + j], (*chip, c))
            for i in range(len(ins)) for j, chip in enumerate(chips)]


def _swap_copies(ins, zones, send_sems, recv_sems):
    x, y, c, _ = _place()
    copies = []
    for i in range(len(ins)):
        for s in range(N_SHARD):
            copies.append(_remote(_half(ins[i], 1 - c, s), zones[i].at[s],
                                  send_sems.at[N_SHARD * i + s], recv_sems.at[N_SHARD * i + s], (x, y, 1 - c)))
    return copies


def _exchange_start(name, plan, sources, lands, per_array):
    n = len(sources)
    lands = [lax.empty(shape, dtype) for shape, dtype in lands]

    def body(*refs):
        for cp in plan(refs[:n], refs[n:2 * n], refs[2 * n], refs[2 * n + 1]):
            cp.start()
        refs[-1][...] = jnp.zeros_like(refs[-1])

    dma = pltpu.SemaphoreType.DMA
    outs = pl.pallas_call(
        body, name=name,
        in_specs=[HBM] * (2 * n),
        out_shape=(dma((per_array * n,)), dma((per_array * n,)),
                   *[pltpu.HBM(a.shape, a.dtype) for a in list(sources) + lands], jax.ShapeDtypeStruct((8, LANES), F32)),
        out_specs=(SEM, SEM, *[HBM] * (2 * n), pl.BlockSpec(memory_space=pltpu.VMEM)),
        input_output_aliases={k: 2 + k for k in range(2 * n)},
        compiler_params=pltpu.CompilerParams(has_side_effects=EFFECT),
    )(*[_in_hbm(a) for a in list(sources) + lands])
    return (outs[0], outs[1]), list(outs[2:2 + n]), list(outs[2 + n:2 + 2 * n]), outs[-1]


def _exchange_wait(name, plan, started, after):
    sems, partials, lands, _ = started
    n = len(partials)

    def body(*refs):
        for cp in plan(refs[:n], refs[n:2 * n], refs[2 * n], refs[2 * n + 1]):
            cp.wait_send()
            cp.wait_recv()

    outs = pl.pallas_call(
        body, name=name,
        in_specs=[HBM] * (2 * n) + [SEM, SEM] + [ANY] * len(after),
        out_shape=tuple(pltpu.HBM(a.shape, a.dtype) for a in lands),
        out_specs=tuple([HBM] * n),
        input_output_aliases={n + k: k for k in range(n)},
        compiler_params=pltpu.CompilerParams(has_side_effects=EFFECT),
    )(*partials, *lands, sems[0], sems[1], *after)
    return list(outs)


def _reduce_own(name, grad, other, received, where):
    s, r, cdim = grad.shape
    blk, nb, whole, within = _half_tiling(r, cdim)

    def body(where_ref, g_ref, o_ref, r_ref, out_ref):
        total = g_ref[...] + o_ref[...]
        for j in range(3):
            total = total + r_ref[j].astype(F32)
        out_ref[...] = total

    return pl.pallas_call(
        body, name=name,
        grid_spec=pltpu.PrefetchScalarGridSpec(
            num_scalar_prefetch=1, grid=(nb,),
            in_specs=[pl.BlockSpec((None,) + blk, lambda b, w_ref: (w_ref[0],) + whole(w_ref[1], b)),
                      pl.BlockSpec((None,) + blk, lambda b, w_ref: (w_ref[0],) + within(b)),
                      pl.BlockSpec((3,) + blk, lambda b, w_ref: (0,) + within(b))],
            out_specs=pl.BlockSpec(blk, lambda b, w_ref: whole(w_ref[1], b))),
        out_shape=jax.ShapeDtypeStruct((r, cdim), F32),
        compiler_params=_params("parallel"),
    )(where, grad, other, received)


def _join_start(name, halves):
    n = len(halves)

    def body(*refs):
        bufs, send_sems, recv_sems = refs[:n], refs[n], refs[n + 1]
        x, y, c, _ = _place()
        for i in range(n):
            mine = _half(bufs[i], c)
            _remote(mine, mine, send_sems.at[i], recv_sems.at[i], (x, y, 1 - c)).start()
        refs[-1][...] = jnp.zeros_like(refs[-1])

    dma = pltpu.SemaphoreType.DMA
    outs = pl.pallas_call(
        body, name=name,
        in_specs=[HBM] * n,
        out_shape=(dma((n,)), dma((n,)), *[pltpu.HBM(h.shape, F32) for h in halves], jax.ShapeDtypeStruct((8, LANES), F32)),
        out_specs=(SEM, SEM, *[HBM] * n, pl.BlockSpec(memory_space=pltpu.VMEM)),
        input_output_aliases={k: 2 + k for k in range(n)},
        compiler_params=pltpu.CompilerParams(has_side_effects=EFFECT),
    )(*[_in_hbm(h) for h in halves])
    return (outs[0], outs[1]), list(outs[2:2 + n]), outs[-1]


def _join_wait(name, started, after):
    sems, bufs, _ = started
    n = len(bufs)

    def body(*refs):
        bufs, send_sems, recv_sems = refs[:n], refs[n], refs[n + 1]
        x, y, c, _ = _place()
        for i in range(n):
            mine, theirs = _half(bufs[i], c), _half(bufs[i], 1 - c)
            _remote(mine, mine, send_sems.at[i], recv_sems.at[i], (x, y, 1 - c)).wait_send()
            _remote(theirs, theirs, send_sems.at[i], recv_sems.at[i], (x, y, 1 - c)).wait_recv()

    outs = pl.pallas_call(
        body, name=name,
        in_specs=[HBM] * n + [SEM, SEM] + [ANY] * len(after),
        out_shape=tuple(pltpu.HBM(b.shape, F32) for b in bufs),
        out_specs=tuple([HBM] * n),
        input_output_aliases={k: k for k in range(n)},
        compiler_params=pltpu.CompilerParams(has_side_effects=EFFECT),
    )(*bufs, sems[0], sems[1], *after)
    return list(outs)


BIG = ("w_in", "pool_w", "w_out", "xq_w", "xk_w", "xv_w", "xo_w", "w_up", "w_down", "conv_w")
KEPT_F32 = ("conv_w",)
GATHER_GROUPS = ((0, 1, 9), (2, 3, 4, 5, 6), (7,), (8,))
RELAYED = (7, 8)
SMALL = ("conv_w", "a_log", "dt_bias", "gdn_norm_w", "pool_scale", "ln1_g", "ln1_b", "ln2_g", "ln2_b", "ln3_g", "ln3_b")
ORDER = ("w_in", "conv_w", "a_log", "dt_bias", "gdn_norm_w", "pool_w", "pool_scale", "w_out", "ln1_g", "ln1_b",
         "xq_w", "xk_w", "xv_w", "xo_w", "ln2_g", "ln2_b", "w_up", "w_down", "ln3_g", "ln3_b")
LANES = 128


def _rows(flat_len):
    return -(-flat_len // LANES)


def _pack(pieces):
    out = []
    for p in pieces:
        flat = p.reshape(-1).astype(F32)
        out.append(jnp.pad(flat, (0, _rows(flat.shape[0]) * LANES - flat.shape[0])).reshape(-1, LANES))
    slab = jnp.concatenate(out, axis=0)
    return jnp.pad(slab, ((0, -slab.shape[0] % 8), (0, 0)))


def _unpack(slab, shapes):
    out, row = [], 0
    for shp in shapes:
        size = math.prod(shp)
        out.append(slab[row:row + _rows(size)].reshape(-1)[:size].reshape(shp))
        row += _rows(size)
    return out


TRANSPOSED = ("w_in",)


def _as2d(name, a):
    a = a[0]
    if name in TRANSPOSED:
        return jnp.swapaxes(a, 0, 1)
    return a.reshape(-1, a.shape[-1]) if a.ndim == 3 else a


def _from2d(name, a, shape):
    return (jnp.swapaxes(a, 0, 1) if name in TRANSPOSED else a).reshape(shape)


def kernel(x, mem, w_in, conv_w, a_log, dt_bias, gdn_norm_w, pool_w, pool_scale, w_out, ln1_g, ln1_b, xq_w, xk_w, xv_w, xo_w, ln2_g, ln2_b, w_up, w_down, ln3_g, ln3_b, loss_target, m_w_in, m_conv_w, m_a_log, m_dt_bias, m_gdn_norm_w, m_pool_w, m_pool_scale, m_w_out, m_ln1_g, m_ln1_b, m_xq_w, m_xk_w, m_xv_w, m_xo_w, m_ln2_g, m_ln2_b, m_w_up, m_w_down, m_ln3_g, m_ln3_b, v_w_in, v_conv_w, v_a_log, v_dt_bias, v_gdn_norm_w, v_pool_w, v_pool_scale, v_w_out, v_ln1_g, v_ln1_b, v_xq_w, v_xk_w, v_xv_w, v_xo_w, v_ln2_g, v_ln2_b, v_w_up, v_w_down, v_ln3_g, v_ln3_b):
    given = dict(locals())
    cx, cy, cc = lax.axis_index("x"), lax.axis_index("y"), lax.axis_index("c")
    me = 2 * cx + cy
    groups = pool_w.shape[1]
    cs = pool_w.shape[2]
    kk, conv_cols = conv_w.shape[1], conv_w.shape[2]
    core = cc.astype(jnp.int32).reshape(1)
    where = jnp.stack([me, cc]).astype(jnp.int32)

    started = {}
    wts = {}

    def start(name, idx, after, token=None):
        casts = [_after(token, _as2d(BIG[i], given[BIG[i]])).astype(F32 if BIG[i] in KEPT_F32 else BF16) for i in idx]
        relayed = tuple(k for k, i in enumerate(idx) if i in RELAYED)
        sems, shards, lands, token = _gather_start(name, casts, after, relayed)
        for k, i in enumerate(idx):
            started[i] = (sems, k, shards[k], lands[k])
        return token

    token = start("gather_start_first", GATHER_GROUPS[0], x)
    token = start("gather_start_rest", tuple(i for group in GATHER_GROUPS[1:] for i in group), token, token)

    relay = {}

    def send_on(after):
        members = [started[i] for i in RELAYED]
        relay["sems"], zones, token = _gather_relay("gather_relay", [m[1] for m in members], [m[2] for m in members],
                                                    [m[3] for m in members], members[0][0], after)
        relay["zones"] = dict(zip(RELAYED, zones))
        return token

    passed = {}

    def pass_on(group, after):
        members = [started[i] for i in GATHER_GROUPS[group]]
        fwd, zones, token = _gather_forward(f"gather_forward_{group}", [m[1] for m in members], [m[3] for m in members],
                                            members[0][0], after)
        passed[group] = (fwd, zones)
        return token

    def fetch(group, after):
        members = [started[i] for i in GATHER_GROUPS[group]]
        sems, idx = members[0][0], [m[1] for m in members]
        shards = [m[2] for m in members]
        if GATHER_GROUPS[group][0] in RELAYED:
            ks = [RELAYED.index(i) for i in GATHER_GROUPS[group]]
            zones = [relay["zones"][i] for i in GATHER_GROUPS[group]]
            fwd, zones = _gather_forward_relayed(f"gather_forward_{group}", ks, zones, relay["sems"], after)
            got = _gather_wait_relayed(f"gather_wait_{group}", idx, ks, shards, zones, sems, relay["sems"], fwd, after)
        else:
            if group not in passed:
                pass_on(group, after)
            fwd, zones = passed[group]
            got = _gather_wait(f"gather_wait_{group}", idx, shards, zones, sems, fwd, after)
        full = dict(zip([BIG[i] for i in GATHER_GROUPS[group]], got))
        out = {}
        for n, a in full.items():
            if n == "w_in":
                out["w_in_t"] = a
            elif n == "w_up":
                out["w_up3"] = a
            elif n == "pool_w":
                out[n] = a.reshape(N_SHARD, groups, cs, -1).transpose(1, 0, 2, 3).reshape(groups, N_SHARD * cs, -1)
            elif n == "conv_w":
                out[n] = a.transpose(1, 0, 2).reshape(kk, N_SHARD * conv_cols)
            else:
                out[n] = a.reshape(-1, a.shape[-1])
        return out

    for n in ("a_log", "dt_bias", "gdn_norm_w", "pool_scale", "ln1_g", "ln1_b", "ln2_g", "ln2_b", "ln3_g", "ln3_b"):
        wts[n] = given[n]
    x_bf = _after(token, x[0]).astype(BF16)
    wts.update(fetch(0, x_bf))

    def start_swap(group, grads):
        names, blocks = [], []
        for n, g in grads.items():
            if n == "pool_w":
                g = g.reshape(groups, N_SHARD, cs, -1).transpose(1, 0, 2, 3).reshape(N_SHARD, groups * cs, -1)
            elif g.ndim == 2:
                g = g.reshape(N_SHARD, -1, g.shape[-1])
            names.append({"w_in_t": "w_in", "w_up3": "w_up"}.get(n, n))
            blocks.append(g)
        zones = [((N_SHARD,) + _half_shape(b.shape[1], b.shape[2]), F32) for b in blocks]
        swap = _exchange_start(f"grad_swap_start_{group}", _swap_copies, blocks, zones, N_SHARD)
        return {"group": group, "names": names, "swap": swap, "token": swap[3]}

    def start_send(state, after):
        group, names = state["group"], state["names"]
        state["blocks"] = state["swap"][1]
        state["others"] = _exchange_wait(f"grad_swap_wait_{group}", _swap_copies, state["swap"], after)
        partials = [_chip_partial("chip_partial_" + n, gb, ob, core)
                    for n, gb, ob in zip(names, state["blocks"], state["others"])]
        zones = [((3,) + p.shape[1:], BF16) for p in partials]
        state["send"] = _exchange_start(f"grad_send_start_{group}", _partial_copies, partials, zones, 3)
        state["token"] = state["send"][3]

    grad, delta, new_m, new_v = {}, {}, {}, {}

    def start_join(state, after):
        group, names = state["group"], state["names"]
        received = _exchange_wait(f"grad_send_wait_{group}", _partial_copies, state["send"], after)
        halves = [_reduce_own("reduce_own_" + n, gb, ob, rb, where)
                  for n, gb, ob, rb in zip(names, state["blocks"], state["others"], received)]
        state["join"] = _join_start(f"grad_join_start_{group}", halves)
        return state["join"][2]

    def finish_reduce(state, after):
        group, names = state["group"], state["names"]
        for n, g in zip(names, _join_wait(f"grad_join_wait_{group}", state["join"], after)):
            shp = given[n].shape
            d2, m2, v2, g2 = _adamw("adamw_" + n, _as2d(n, given[n]), g, _as2d(n, given["m_" + n]),
                                    _as2d(n, given["v_" + n]))
            grad[n], delta[n], new_m[n], new_v[n] = (_from2d(n, a, shp) for a in (g2, d2, m2, v2))
        return d2

    step = _local_step(x[0], mem[0], loss_target[0], wts, x_bf)
    pending = {}
    request = next(step)
    while True:
        try:
            kind, group, payload = request
            if kind == "weights":
                request = step.send(fetch(group, payload))
            elif kind == "relay":
                request = step.send(send_on(payload))
            elif kind == "pass":
                request = step.send(pass_on(group, payload))
            elif kind == "grads":
                pending[group] = start_swap(group, payload)
                request = step.send(pending[group]["token"])
            else:
                start_send(pending[group], [payload])
                request = step.send(pending[group]["token"])
        except StopIteration as stop:
            loss_row, grad_x, g = stop.value
            break

    after = [pending[2]["token"], grad_x]
    for group in (0, 1):
        after = [start_join(pending[group], after)]
    for group in (0, 1):
        after = [finish_reduce(pending[group], after)]
    after = [finish_reduce(pending[2], [start_join(pending[2], after)])]

    small_names = ("a_log", "dt_bias", "gdn_norm_w", "pool_scale", "ln1_g", "ln1_b", "ln2_g", "ln2_b", "ln3_g", "ln3_b")
    pieces = [g["conv_w"]] + [g[n] for n in small_names] + [loss_row[:, :1]]
    shapes = [p.shape for p in pieces]
    summed = _unpack(_all_reduce_small("all_reduce_small", _pack(pieces), after[0]), shapes)
    gsmall = dict(zip(small_names, summed[1:-1]))
    gsmall["conv_w"] = lax.dynamic_slice(summed[0], (0, me * conv_cols), (kk, conv_cols))
    loss = summed[-1][0, 0]

    sshapes = [given[n].shape for n in SMALL]
    slabs = [_pack([given[p + n] for n in SMALL]) for p in ("", "m_", "v_")]
    gslab = _pack([gsmall[n] for n in SMALL])
    outs = _adamw("adamw_small", slabs[0], gslab, slabs[1], slabs[2])[:3]
    for dst, slab in zip((delta, new_m, new_v), outs):
        dst.update(zip(SMALL, _unpack(slab, sshapes)))
    for n in SMALL:
        grad[n] = gsmall[n].reshape(given[n].shape)

    return (loss, grad_x[None], *[grad[n] for n in ORDER], *[delta[n] for n in ORDER],
            *[new_m[n] for n in ORDER], *[new_v[n] for n in ORDER])
```

```python
import math

import jax
import jax.numpy as jnp
from jax import lax
from jax.experimental import pallas as pl
from jax.experimental.pallas import tpu as pltpu

F32 = jnp.float32
BF16 = jnp.bfloat16
MESH = pl.DeviceIdType.MESH

HEAD_DIM = 128
CHUNK = 64
POOL_WINDOWS = (2, 4, 8, 16)
XATTN_HEADS = 4
ALPHA = 2.0 ** 0.25
LN_EPS = 1e-5
NORM_EPS = 1e-6
ADAM_LR, ADAM_B1, ADAM_B2, ADAM_EPS, ADAM_WD, ADAM_STEP = 0.001, 0.9, 0.999, 1e-08, 0.01, 10
N_SHARD = 4
VMEM_LIMIT = 56 * 1024 * 1024
K_STEPS = (2048, 1024, 512, 256, 128)


def _params(*sem):
    return pltpu.CompilerParams(dimension_semantics=sem, vmem_limit_bytes=VMEM_LIMIT)


def _bdot(a, b, ta=False, tb=False):
    dims = (((0 if ta else 1,), (1 if tb else 0,)), ((), ()))
    return lax.dot_general(a.astype(BF16), b.astype(BF16), dims, preferred_element_type=F32)


def _sigmoid(x):
    return 1.0 / (1.0 + jnp.exp(-x))


def _matmul(name, a, b, *, ta=False, tb=False, tm, tn, tk, extra=(), outs, epilogue, b_blocks=None,
            sequential=False, n_used=None, n_outer=False):
    m, k_dim = (a.shape[1], a.shape[0]) if ta else a.shape
    if b_blocks and tb:
        n = b.shape[1]
        k_dim = b.shape[0] * b.shape[2]
        per = b.shape[2] // tk
        b_spec = pl.BlockSpec((None, tn, tk), lambda i, j, k: (k // per, j, k % per))
    elif b_blocks:
        n = b.shape[0] * b.shape[2]
        per = b.shape[2] // tn
        b_spec = pl.BlockSpec((None, tk, tn), lambda i, j, k: (j // per, k, j % per))
    elif tb:
        n = b.shape[0]
        b_spec = pl.BlockSpec((tn, tk), lambda i, j, k: (j, k))
    else:
        n = b.shape[1]
        b_spec = pl.BlockSpec((tk, tn), lambda i, j, k: (k, j))
    n = n_used or n
    assert m % tm == 0 and n % tn == 0 and k_dim % tk == 0, (name, m, n, k_dim, tm, tn, tk)
    nk = k_dim // tk
    a_spec = pl.BlockSpec((tk, tm), lambda i, j, k: (k, i)) if ta else pl.BlockSpec((tm, tk), lambda i, j, k: (i, k))
    n_extra, n_out = len(extra), len(outs)

    def wrap(index_map):
        return lambda i, j, k: index_map(i, j)

    def spec(block, index_map):
        if n_outer:
            return pl.BlockSpec(block, lambda j, i, k: index_map(i, j, k))
        return pl.BlockSpec(block, index_map)

    row_axis = 1 if n_outer else 0

    def body_one_step(*refs):
        ex = refs[2:2 + n_extra]
        out = refs[2 + n_extra:2 + n_extra + n_out]
        epilogue(_bdot(refs[0][...], refs[1][...], ta, tb), ex, out, pl.program_id(row_axis))

    def body(*refs):
        a_ref, b_ref = refs[0], refs[1]
        ex = refs[2:2 + n_extra]
        out = refs[2 + n_extra:2 + n_extra + n_out]
        acc = refs[-1]
        i, k = pl.program_id(row_axis), pl.program_id(2)
        part = _bdot(a_ref[...], b_ref[...], ta, tb)

        @pl.when(k == 0)
        def _():
            acc[...] = part

        @pl.when(jnp.logical_and(k > 0, k < nk - 1))
        def _():
            acc[...] += part

        @pl.when(k == nk - 1)
        def _():
            epilogue(acc[...] + part, ex, out, i)

    sem = ("arbitrary",) * 3 if sequential else ("parallel", "parallel", "arbitrary")
    res = pl.pallas_call(
        body_one_step if nk == 1 else body, name=name,
        grid=(n // tn, m // tm, nk) if n_outer else (m // tm, n // tn, nk),
        in_specs=[spec(a_spec.block_shape, a_spec.index_map), spec(b_spec.block_shape, b_spec.index_map)]
        + [spec(bs, wrap(im)) for _, bs, im in extra],
        out_specs=[spec(bs, wrap(im)) for _, bs, im in outs],
        out_shape=[s for s, _, _ in outs],
        scratch_shapes=[] if nk == 1 else [pltpu.VMEM((tm, tn), F32)],
        compiler_params=_params(*sem),
    )(a, b, *[x for x, _, _ in extra])
    return res


def _tile(i, j):
    return (i, j)


def _plain(name, a, b, *, ta=False, tb=False, tm, tn, tk, out_dtype, b_blocks=None, out3=None, n_used=None,
           n_outer=False):
    m = a.shape[1] if ta else a.shape[0]
    if b_blocks:
        n = b.shape[1] if tb else b.shape[0] * b.shape[2]
    else:
        n = n_used or (b.shape[0] if tb else b.shape[1])

    def epi(acc, ex, out, i):
        out[0][...] = acc.astype(out_dtype)

    if out3:
        per = (n // out3) // tn
        spec = (jax.ShapeDtypeStruct((out3, m, n // out3), out_dtype), (None, tm, tn),
                lambda i, j: (j // per, i, j % per))
    else:
        spec = (jax.ShapeDtypeStruct((m, n), out_dtype), (tm, tn), _tile)
    return _matmul(name, a, b, ta=ta, tb=tb, tm=tm, tn=tn, tk=tk, outs=[spec], epilogue=epi,
                   b_blocks=b_blocks, n_used=n_used, n_outer=n_outer)[0]


def _ln_forward(name, a, b, res, gamma, beta, *, tm, tk, want_h=True):
    m, n = res.shape

    def epi(acc, ex, out, i):
        u = ALPHA * ex[0][...] + acc
        mu = jnp.mean(u, axis=-1, keepdims=True)
        xc = u - mu
        var = jnp.mean(xc * xc, axis=-1, keepdims=True)
        rstd = lax.rsqrt(var + LN_EPS)
        xhat = xc * rstd
        out[-2][...] = xhat
        out[-1][...] = rstd
        if want_h:
            h = xhat * ex[1][...] + ex[2][...]
            out[0][...] = h
            out[1][...] = h.astype(BF16)

    row = lambda i, j: (i, 0)
    vec = lambda i, j: (0, 0)
    outs = [(jax.ShapeDtypeStruct((m, n), F32), (tm, n), row), (jax.ShapeDtypeStruct((m, n), BF16), (tm, n), row),
            (jax.ShapeDtypeStruct((m, n), F32), (tm, n), row), (jax.ShapeDtypeStruct((m, 1), F32), (tm, 1), row)]
    return _matmul(
        name, a, b, tm=tm, tn=n, tk=tk,
        extra=[(res, (tm, n), row), (gamma, (1, n), vec), (beta, (1, n), vec)],
        outs=outs if want_h else outs[2:], epilogue=epi)


def _ln_backward_math(dy, xhat, rstd, gamma):
    dxhat = dy * gamma
    m1 = jnp.mean(dxhat, axis=-1, keepdims=True)
    m2 = jnp.mean(dxhat * xhat, axis=-1, keepdims=True)
    du = rstd * (dxhat - m1 - xhat * m2)
    return du, jnp.sum(dy * xhat, axis=0, keepdims=True), jnp.sum(dy, axis=0, keepdims=True)


def _ln_backward(name, a, b, dres, xhat, rstd, gamma, *, tm, tk, b_blocks=None, tb=True):
    m, n = dres.shape

    def epi(acc, ex, out, i):
        dy = acc + ALPHA * ex[0][...]
        du, dg, db = _ln_backward_math(dy, ex[1][...], ex[2][...], ex[3][...])
        out[0][...] = du
        out[1][...] = du.astype(BF16)
        first = i == 0

        @pl.when(first)
        def _():
            out[2][...] = dg
            out[3][...] = db

        @pl.when(jnp.logical_not(first))
        def _():
            out[2][...] += dg
            out[3][...] += db

    row = lambda i, j: (i, 0)
    vec = lambda i, j: (0, 0)
    return _matmul(
        name, a, b, tb=tb, tm=tm, tn=n, tk=tk, b_blocks=b_blocks, sequential=True,
        extra=[(dres, (tm, n), row), (xhat, (tm, n), row), (rstd, (tm, 1), row), (gamma, (1, n), vec)],
        outs=[(jax.ShapeDtypeStruct((m, n), F32), (tm, n), row),
              (jax.ShapeDtypeStruct((m, n), BF16), (tm, n), row),
              (jax.ShapeDtypeStruct((1, n), F32), (1, n), vec),
              (jax.ShapeDtypeStruct((1, n), F32), (1, n), vec)],
        epilogue=epi)


def _shift_down(x, k):
    row = lax.broadcasted_iota(jnp.int32, x.shape, 0)
    return jnp.where(row >= k, pltpu.roll(x, k, axis=0), 0.0)


def _shift_up(x, k):
    t = x.shape[0]
    row = lax.broadcasted_iota(jnp.int32, x.shape, 0)
    return jnp.where(row < t - k, pltpu.roll(x, t - k, axis=0), 0.0)


def _conv_silu_norm(x, w, normalise):
    kk = w.shape[0]
    c = x * w[kk - 1:kk, :]
    for j in range(kk - 1):
        c = c + _shift_down(x, kk - 1 - j) * w[j:j + 1, :]
    sg = _sigmoid(c)
    s = c * sg
    r = lax.rsqrt(jnp.sum(s * s, axis=-1, keepdims=True) + NORM_EPS)
    y = jnp.where(normalise, s * r, s)
    return c, sg, s, r, y


def _gdn_pre(proj, conv_w, heads):
    t = proj.shape[0]
    kk = conv_w.shape[0]

    def body(x_ref, w_ref, o_ref):
        normalise = pl.program_id(0) < 2
        o_ref[...] = _conv_silu_norm(x_ref[...], w_ref[...], normalise)[4]

    col = lambda s, h: (0, s * heads + h)
    return pl.pallas_call(
        body, name="gdn_pre", grid=(3, heads),
        in_specs=[pl.BlockSpec((t, HEAD_DIM), col), pl.BlockSpec((kk, HEAD_DIM), col)],
        out_specs=pl.BlockSpec((t, HEAD_DIM), col),
        out_shape=jax.ShapeDtypeStruct((t, 3 * heads * HEAD_DIM), F32),
        compiler_params=_params("parallel", "parallel"),
    )(proj, conv_w)


def _gdn_pre_backward(proj, conv_w, dqkv, heads):
    t = proj.shape[0]
    kk = conv_w.shape[0]

    def body(x_ref, w_ref, dy_ref, dx_ref, dw_ref):
        normalise = pl.program_id(0) < 2
        x = x_ref[...]
        w = w_ref[...]
        dy = dy_ref[...]
        c, sg, s, r, y = _conv_silu_norm(x, w, normalise)
        ds_norm = r * (dy - y * jnp.sum(dy * y, axis=-1, keepdims=True))
        ds = jnp.where(normalise, ds_norm, dy)
        dc = ds * (sg * (1.0 + c * (1.0 - sg)))
        dx = dc * w[kk - 1:kk, :]
        rows = [None] * kk
        rows[kk - 1] = jnp.sum(dc * x, axis=0, keepdims=True)
        for j in range(kk - 1):
            lag = kk - 1 - j
            dx = dx + _shift_up(dc, lag) * w[j:j + 1, :]
            rows[j] = jnp.sum(dc * _shift_down(x, lag), axis=0, keepdims=True)
        dx_ref[...] = dx.astype(BF16)
        dw_ref[...] = jnp.concatenate(rows, axis=0)

    col = lambda s, h: (0, s * heads + h)
    return pl.pallas_call(
        body, name="gdn_pre_bwd", grid=(3, heads),
        in_specs=[pl.BlockSpec((t, HEAD_DIM), col), pl.BlockSpec((kk, HEAD_DIM), col),
                  pl.BlockSpec((t, HEAD_DIM), col)],
        out_specs=[pl.BlockSpec((t, HEAD_DIM), col), pl.BlockSpec((kk, HEAD_DIM), col)],
        out_shape=[jax.ShapeDtypeStruct((t, 3 * heads * HEAD_DIM), BF16),
                   jax.ShapeDtypeStruct((kk, 3 * heads * HEAD_DIM), F32)],
        compiler_params=_params("parallel", "parallel"),
    )(proj, conv_w, dqkv)


def _gate_vectors(a_log, dt_bias, heads):
    pad = lambda v: jnp.pad(v.astype(F32), ((0, 0), (heads, HEAD_DIM - 2 * heads)))
    return pad(jnp.exp(a_log.astype(F32))), pad(dt_bias)


def _softplus(x):
    return jnp.maximum(x, 0.0) + jnp.log(1.0 + jnp.exp(-jnp.abs(x)))


def _gates_epilogue(heads):
    def epi(acc, ex, out, i):
        lane = lax.broadcasted_iota(jnp.int32, acc.shape, 1)
        beta = _sigmoid(acc)
        g = -ex[0][...] * _softplus(acc + ex[1][...])
        out[0][...] = acc
        out[1][...] = jnp.where(lane < heads, beta, jnp.where(lane < 2 * heads, g, 0.0))
    return epi


def _gates_backward(ba, bg, dbg, ea, dtb, heads):
    t = ba.shape[0]

    def body(ba_ref, bg_ref, d_ref, ea_ref, dt_ref, dba_ref, dal_ref, ddt_ref):
        lane = lax.broadcasted_iota(jnp.int32, (t, HEAD_DIM), 1)
        bgv = bg_ref[...]
        d = d_ref[...]
        db = d * bgv * (1.0 - bgv)
        da = -d * ea_ref[...] * _sigmoid(ba_ref[...] + dt_ref[...])
        is_g = jnp.logical_and(lane >= heads, lane < 2 * heads)
        dba = jnp.where(lane < heads, db, jnp.where(is_g, da, 0.0))
        dba_ref[...] = dba.astype(BF16)
        dal_ref[...] = jnp.sum(jnp.where(is_g, d * bgv, 0.0), axis=0, keepdims=True)
        ddt_ref[...] = jnp.sum(jnp.where(is_g, da, 0.0), axis=0, keepdims=True)

    full = pl.BlockSpec((t, HEAD_DIM), lambda: (0, 0))
    vec = pl.BlockSpec((1, HEAD_DIM), lambda: (0, 0))
    return pl.pallas_call(
        body, name="gates_bwd", grid=(),
        in_specs=[full, full, full, vec, vec], out_specs=[full, vec, vec],
        out_shape=[jax.ShapeDtypeStruct((t, HEAD_DIM), BF16), jax.ShapeDtypeStruct((1, HEAD_DIM), F32),
                   jax.ShapeDtypeStruct((1, HEAD_DIM), F32)],
        compiler_params=pltpu.CompilerParams(vmem_limit_bytes=VMEM_LIMIT),
    )(ba, bg, dbg, ea, dtb)


class _Chunk:
    pass


def _split2(x):
    hi = x.astype(BF16)
    return hi, (x - hi.astype(F32)).astype(BF16)


def _split3(x):
    hi = x.astype(BF16)
    rest = x - hi.astype(F32)
    mid = rest.astype(BF16)
    return hi, mid, (rest - mid.astype(F32)).astype(BF16)


def _dot_mask(mask, x, ta=False):
    hi, mid, lo = _split3(x)
    return _bdot(mask, hi, ta=ta) + (_bdot(mask, mid, ta=ta) + _bdot(mask, lo, ta=ta))


def _transpose_by_identity(x):
    r = x.shape[0]
    eye = (lax.broadcasted_iota(jnp.int32, (r, r), 0) == lax.broadcasted_iota(jnp.int32, (r, r), 1)).astype(BF16)
    hi, mid, lo = _split3(x)
    return _bdot(hi, eye, ta=True) + (_bdot(mid, eye, ta=True) + _bdot(lo, eye, ta=True))


def _dot22(a, b, ta=False, tb=False):
    ah, al = _split2(a)
    bh, bl = _split2(b)
    return _bdot(ah, bh, ta, tb) + (_bdot(ah, bl, ta, tb) + _bdot(al, bh, ta, tb))


def _chunk_gates(bg, heads):
    n = CHUNK
    row = lax.broadcasted_iota(jnp.int32, (n, n), 0)
    col = lax.broadcasted_iota(jnp.int32, (n, n), 1)
    lane = lax.broadcasted_iota(jnp.int32, bg.shape, 1)
    graw = jnp.where(jnp.logical_and(lane >= heads, lane < 2 * heads), bg, 0.0)
    gc = _dot_mask((row >= col).astype(BF16), graw)
    return gc, _transpose_by_identity(gc)


def _in_lockstep(generators):
    results = [None] * len(generators)
    live = list(enumerate(generators))
    while live:
        still = []
        for i, gen in live:
            try:
                next(gen)
                still.append((i, gen))
            except StopIteration as stop:
                results[i] = stop.value
        live = still
    return results


def _chunk_local(q, k, v, beta, gc, grow, solved=None):
    c = _Chunk()
    n = CHUNK
    row = lax.broadcasted_iota(jnp.int32, (n, n), 0)
    col = lax.broadcasted_iota(jnp.int32, (n, n), 1)
    c.tri = row >= col
    c.strict = row > col
    eye = row == col
    c.gcb = jnp.broadcast_to(gc, (n, HEAD_DIM))
    c.decay = jnp.where(c.tri, jnp.exp(jnp.where(c.tri, gc - grow, 0.0)), 0.0)
    c.eg = jnp.exp(c.gcb)
    glast = c.gcb[n - 1:n, :]
    c.egl = jnp.exp(glast)
    c.ekl = jnp.exp(glast - c.gcb)
    c.beta = beta
    c.q = q * (HEAD_DIM ** -0.5)
    c.k = k
    c.v = v
    c.kb = k * beta
    c.vb = v * beta
    c.kg = c.kb * c.eg
    both = _bdot(jnp.concatenate([c.kb, c.q], axis=0), k, tb=True)
    yield
    c.L = jnp.where(c.strict, both[:n] * c.decay, 0.0)
    c.A = jnp.where(c.tri, both[n:] * c.decay, 0.0)
    if solved is None:
        x = -c.L
        tinv = eye.astype(F32) + x
        p = _dot22(x, x)
        yield
        for _ in range(int(math.log2(n)) - 2):
            both = _dot22(jnp.concatenate([p, tinv], axis=0), p)
            yield
            p, tinv = both[:n], tinv + both[n:]
        c.T = tinv + _dot22(tinv, p)
        yield
        uw = _dot22(c.T, jnp.concatenate([c.vb, c.kg], axis=1))
        yield
        c.u, c.w = uw[:, :HEAD_DIM], uw[:, HEAD_DIM:]
    else:
        c.T, c.u, c.w = solved
    c.qg = c.q * c.eg
    c.kdec = k * c.ekl
    return c


def _gdn_core(qkv, bg, heads):
    t = qkv.shape[0]
    nchunk = t // CHUNK

    gw = heads * HEAD_DIM

    def body(qkv_ref, bg_ref, o_ref, s_ref, t_ref, u_ref, w_ref, state):
        @pl.when(pl.program_id(0) == 0)
        def _():
            state[...] = jnp.zeros_like(state)

        bg_v = bg_ref[...]
        gc_all, gc_rows = _chunk_gates(bg_v, heads)
        def one_head(h):
            col = lambda s: pl.ds(s * gw + h * HEAD_DIM, HEAD_DIM)
            c = yield from _chunk_local(qkv_ref[:, col(0)], qkv_ref[:, col(1)], qkv_ref[:, col(2)], bg_v[:, h:h + 1],
                                        gc_all[:, heads + h:heads + h + 1], gc_rows[heads + h:heads + h + 1, :])
            s0 = state[h]
            v_new = c.u - _bdot(c.w, s0)
            yield
            o = _bdot(c.qg, s0) + _bdot(c.A, v_new)
            return s0, o, s0 * c.egl + _bdot(c.kdec, v_new, ta=True), c

        results = _in_lockstep([one_head(h) for h in range(heads)])
        for h, (s0, o, s1, c) in enumerate(results):
            lanes = pl.ds(h * HEAD_DIM, HEAD_DIM)
            s_ref[h, 0] = s0
            o_ref[:, lanes] = o
            t_ref[:, lanes] = jnp.concatenate([c.T, jnp.zeros((CHUNK, HEAD_DIM - CHUNK), F32)], axis=1)
            u_ref[:, lanes] = c.u
            w_ref[:, lanes] = c.w
            state[h] = s1

    return pl.pallas_call(
        body, name="gdn_core", grid=(nchunk,),
        in_specs=[pl.BlockSpec((CHUNK, 3 * gw), lambda n: (n, 0)), pl.BlockSpec((CHUNK, HEAD_DIM), lambda n: (n, 0))],
        out_specs=[pl.BlockSpec((CHUNK, gw), lambda n: (n, 0)),
                   pl.BlockSpec((heads, 1, HEAD_DIM, HEAD_DIM), lambda n: (0, n, 0, 0))]
        + [pl.BlockSpec((CHUNK, gw), lambda n: (n, 0))] * 3,
        out_shape=[jax.ShapeDtypeStruct((t, gw), F32),
                   jax.ShapeDtypeStruct((heads, nchunk, HEAD_DIM, HEAD_DIM), F32)]
        + [jax.ShapeDtypeStruct((t, gw), F32)] * 3,
        scratch_shapes=[pltpu.VMEM((heads, HEAD_DIM, HEAD_DIM), F32)],
        compiler_params=_params("arbitrary"),
    )(qkv, bg)


def _gdn_core_backward(qkv, bg, states, solved, do, heads):
    t = qkv.shape[0]
    nchunk = t // CHUNK
    n = CHUNK

    def one_head(chunk_local, s0, d_out, ds1):
        c = yield from chunk_local
        v_new = c.u - _bdot(c.w, s0)
        dqg = _bdot(d_out, s0, tb=True)
        ds0 = _bdot(c.qg, d_out, ta=True) + ds1 * c.egl
        dv_new = _bdot(c.A, d_out, ta=True) + _bdot(c.kdec, ds1)
        yield
        dA = jnp.where(c.tri, _bdot(d_out, v_new, tb=True), 0.0)
        dkdec = _bdot(v_new, ds1, tb=True)
        dgl = jnp.sum(jnp.sum(ds1 * s0, axis=1, keepdims=True), axis=0, keepdims=True) * c.egl
        dw = -_bdot(dv_new, s0, tb=True)
        ds0 = ds0 - _bdot(c.w, dv_new, ta=True)
        yield
        both = _dot22(c.T, jnp.concatenate([dv_new, dw], axis=1), ta=True)
        yield
        dvb, dkg = both[:, :HEAD_DIM], both[:, HEAD_DIM:]
        dL = jnp.where(c.strict, -(_bdot(dvb, c.u, tb=True) + _bdot(dkg, c.w, tb=True)), 0.0)
        yield
        dm1 = dL * c.decay
        dkb = _bdot(dm1, c.k) + dkg * c.eg
        dk = _bdot(dm1, c.kb, ta=True)
        dm2 = dA * c.decay
        dq = _bdot(dm2, c.k) + dqg * c.eg
        dk = dk + _bdot(dm2, c.q, ta=True) + dkdec * c.ekl + dkb * c.beta
        pm = dL * c.L + dA * c.A
        ones = jnp.ones((n, HEAD_DIM), BF16)
        pm_hi, pm_lo = _split2(pm)
        colsum = _bdot(pm_hi, ones, ta=True) + _bdot(pm_lo, ones, ta=True)
        tk_ = jnp.sum(dkdec * c.kdec, axis=1, keepdims=True)
        dgc = (jnp.sum(pm, axis=1, keepdims=True) - colsum
               + jnp.sum(dqg * c.qg, axis=1, keepdims=True)
               - tk_
               + jnp.sum(dkg * c.kg, axis=1, keepdims=True))
        dgl = dgl + jnp.sum(tk_, axis=0, keepdims=True)
        rowi = lax.broadcasted_iota(jnp.int32, (n, HEAD_DIM), 0)
        dgc = dgc + jnp.where(rowi == n - 1, dgl, 0.0)
        dbeta = jnp.sum(dkb * c.k, axis=1, keepdims=True) + jnp.sum(dvb * c.v, axis=1, keepdims=True)
        return dq * (HEAD_DIM ** -0.5), dk, dvb * c.beta, dbeta, dgc, ds0

    gw = heads * HEAD_DIM

    def body(qkv_ref, bg_ref, s_ref, t_ref, u_ref, w_ref, do_ref, dqkv_ref, dbg_ref, dstate):
        @pl.when(pl.program_id(0) == 0)
        def _():
            dstate[...] = jnp.zeros_like(dstate)

        bg_v = bg_ref[...]
        gc_all, gc_rows = _chunk_gates(bg_v, heads)
        lane = lax.broadcasted_iota(jnp.int32, (n, HEAD_DIM), 1)
        dgates = jnp.zeros((n, HEAD_DIM), F32)
        chains = []
        for h in range(heads):
            col = lambda s: pl.ds(s * gw + h * HEAD_DIM, HEAD_DIM)
            lanes = pl.ds(h * HEAD_DIM, HEAD_DIM)
            c = _chunk_local(qkv_ref[:, col(0)], qkv_ref[:, col(1)], qkv_ref[:, col(2)], bg_v[:, h:h + 1],
                             gc_all[:, heads + h:heads + h + 1], gc_rows[heads + h:heads + h + 1, :],
                             (t_ref[:, pl.ds(h * HEAD_DIM, CHUNK)], u_ref[:, lanes], w_ref[:, lanes]))
            chains.append(one_head(c, s_ref[h, 0], do_ref[:, pl.ds(h * HEAD_DIM, HEAD_DIM)], dstate[h]))
        results = _in_lockstep(chains)
        for h, (dq, dk, dv, dbeta, dgc, ds0) in enumerate(results):
            dgates = jnp.where(lane == h, dbeta, jnp.where(lane == heads + h, dgc, dgates))
        for h, (dq, dk, dv, dbeta, dgc, ds0) in enumerate(results):
            dqkv_ref[:, pl.ds(h * HEAD_DIM, HEAD_DIM)] = dq
            dqkv_ref[:, pl.ds(gw + h * HEAD_DIM, HEAD_DIM)] = dk
            dqkv_ref[:, pl.ds(2 * gw + h * HEAD_DIM, HEAD_DIM)] = dv
            dstate[h] = ds0
        row = lax.broadcasted_iota(jnp.int32, (n, n), 0)
        colm = lax.broadcasted_iota(jnp.int32, (n, n), 1)
        draw = _dot_mask((row >= colm).astype(BF16), dgates, ta=True)
        dbg_ref[...] = jnp.where(lane < heads, dgates, draw)

    last = nchunk - 1
    return pl.pallas_call(
        body, name="gdn_core_bwd", grid=(nchunk,),
        in_specs=[pl.BlockSpec((CHUNK, 3 * gw), lambda i: (last - i, 0)),
                  pl.BlockSpec((CHUNK, HEAD_DIM), lambda i: (last - i, 0)),
                  pl.BlockSpec((heads, 1, HEAD_DIM, HEAD_DIM), lambda i: (0, last - i, 0, 0))]
        + [pl.BlockSpec((CHUNK, gw), lambda i: (last - i, 0))] * 4,
        out_specs=[pl.BlockSpec((CHUNK, 3 * gw), lambda i: (last - i, 0)),
                   pl.BlockSpec((CHUNK, HEAD_DIM), lambda i: (last - i, 0))],
        out_shape=[jax.ShapeDtypeStruct((t, 3 * gw), F32), jax.ShapeDtypeStruct((t, HEAD_DIM), F32)],
        scratch_shapes=[pltpu.VMEM((heads, HEAD_DIM, HEAD_DIM), F32)],
        compiler_params=_params("arbitrary"),
    )(qkv, bg, states, *solved, do)


def _gdn_post(o, proj, z_col0, norm_w, heads, tt):
    t = o.shape[0]
    zb = z_col0 // HEAD_DIM

    def body(o_ref, z_ref, w_ref, out_ref):
        ov = o_ref[...]
        z = z_ref[...]
        rms = lax.rsqrt(jnp.mean(ov * ov, axis=-1, keepdims=True) + NORM_EPS)
        out_ref[...] = (ov * rms * w_ref[...] * (z * _sigmoid(z))).astype(BF16)

    return pl.pallas_call(
        body, name="gdn_post", grid=(t // tt, heads),
        in_specs=[pl.BlockSpec((tt, HEAD_DIM), lambda i, h: (i, h)),
                  pl.BlockSpec((tt, HEAD_DIM), lambda i, h: (i, zb + h)),
                  pl.BlockSpec((1, HEAD_DIM), lambda i, h: (0, 0))],
        out_specs=pl.BlockSpec((tt, HEAD_DIM), lambda i, h: (i, h)),
        out_shape=jax.ShapeDtypeStruct((t, heads * HEAD_DIM), BF16),
        compiler_params=_params("parallel", "parallel"),
    )(o, proj, norm_w)


def _gdn_post_backward(dcat, o, proj, z_col0, norm_w, heads, tt):
    t = o.shape[0]
    zb = z_col0 // HEAD_DIM

    def body(d_ref, o_ref, z_ref, w_ref, do_ref, dz_ref, dw_ref):
        d = d_ref[...]
        ov = o_ref[...]
        z = z_ref[...]
        w = w_ref[...]
        rms = lax.rsqrt(jnp.mean(ov * ov, axis=-1, keepdims=True) + NORM_EPS)
        ohat = ov * rms
        sg = _sigmoid(z)
        gate = z * sg
        dz_ref[...] = (d * ohat * w * (sg * (1.0 + z * (1.0 - sg)))).astype(BF16)
        don = d * gate
        dohat = don * w
        do_ref[...] = rms * (dohat - ohat * jnp.mean(dohat * ohat, axis=-1, keepdims=True))
        dw = jnp.sum(don * ohat, axis=0, keepdims=True)
        first = jnp.logical_and(pl.program_id(0) == 0, pl.program_id(1) == 0)

        @pl.when(first)
        def _():
            dw_ref[...] = dw

        @pl.when(jnp.logical_not(first))
        def _():
            dw_ref[...] += dw

    blk = pl.BlockSpec((tt, HEAD_DIM), lambda i, h: (i, h))
    return pl.pallas_call(
        body, name="gdn_post_bwd", grid=(t // tt, heads),
        in_specs=[blk, blk, pl.BlockSpec((tt, HEAD_DIM), lambda i, h: (i, zb + h)),
                  pl.BlockSpec((1, HEAD_DIM), lambda i, h: (0, 0))],
        out_specs=[blk, blk, pl.BlockSpec((1, HEAD_DIM), lambda i, h: (0, 0))],
        out_shape=[jax.ShapeDtypeStruct((t, heads * HEAD_DIM), F32),
                   jax.ShapeDtypeStruct((t, heads * HEAD_DIM), BF16),
                   jax.ShapeDtypeStruct((1, HEAD_DIM), F32)],
        compiler_params=_params("arbitrary", "arbitrary"),
    )(dcat, o, proj, norm_w)


def _pool_select(levels, group):
    out = levels[-1]
    for gi in range(len(levels) - 2, -1, -1):
        out = jnp.where(group == gi, levels[gi], out)
    return out


def _pool_counts(t, width, group):
    pos = lax.broadcasted_iota(jnp.int32, (t, width), 0)
    win = jnp.left_shift(2, group)
    return jnp.minimum(pos + 1, win).astype(F32)


def _pooled(p, group):
    levels, s, step = [], p, 1
    for _ in POOL_WINDOWS:
        s = s + _shift_down(s, step)
        levels.append(s)
        step *= 2
    cnt = _pool_counts(p.shape[0], p.shape[1], group)
    return _pool_select(levels, group) / cnt - p, cnt


def _pool_forward(proj, p_col0, pool_w, pool_scale):
    t = proj.shape[0]
    groups, cg, _ = pool_w.shape
    pb = p_col0 // cg

    def body(p_ref, w_ref, s_ref, o_ref):
        pooled, _ = _pooled(p_ref[...], pl.program_id(0))
        o_ref[...] = (_bdot(pooled, w_ref[0]) * s_ref[...]).astype(BF16)

    return pl.pallas_call(
        body, name="pool_fwd", grid=(groups,),
        in_specs=[pl.BlockSpec((t, cg), lambda g: (0, pb + g)), pl.BlockSpec((1, cg, cg), lambda g: (g, 0, 0)),
                  pl.BlockSpec((1, cg), lambda g: (0, g))],
        out_specs=pl.BlockSpec((t, cg), lambda g: (0, g)),
        out_shape=jax.ShapeDtypeStruct((t, groups * cg), BF16),
        compiler_params=_params("parallel"),
    )(proj, pool_w, pool_scale)


def _pool_backward(dcat, d_col0, proj, p_col0, pool_w, pool_scale):
    t = proj.shape[0]
    groups, cg, _ = pool_w.shape
    pb = p_col0 // cg
    db = d_col0 // cg

    def body(d_ref, p_ref, w_ref, s_ref, dp_ref, dw_ref, ds_ref):
        group = pl.program_id(0)
        pooled, cnt = _pooled(p_ref[...], group)
        w = w_ref[0]
        d = d_ref[...]
        mixed = _bdot(pooled, w)
        ds_ref[...] = jnp.sum(d * mixed, axis=0, keepdims=True)
        dmixed = d * s_ref[...]
        dw_ref[0] = _bdot(pooled, dmixed, ta=True)
        dpooled = _bdot(dmixed, w, tb=True)
        levels, s, step = [], dpooled / cnt, 1
        for _ in POOL_WINDOWS:
            s = s + _shift_up(s, step)
            levels.append(s)
            step *= 2
        dp_ref[...] = (_pool_select(levels, group) - dpooled).astype(BF16)

    return pl.pallas_call(
        body, name="pool_bwd", grid=(groups,),
        in_specs=[pl.BlockSpec((t, cg), lambda g: (0, db + g)), pl.BlockSpec((t, cg), lambda g: (0, pb + g)),
                  pl.BlockSpec((1, cg, cg), lambda g: (g, 0, 0)), pl.BlockSpec((1, cg), lambda g: (0, g))],
        out_specs=[pl.BlockSpec((t, cg), lambda g: (0, g)), pl.BlockSpec((1, cg, cg), lambda g: (g, 0, 0)),
                   pl.BlockSpec((1, cg), lambda g: (0, g))],
        out_shape=[jax.ShapeDtypeStruct((t, groups * cg), BF16), jax.ShapeDtypeStruct((groups, cg, cg), F32),
                   jax.ShapeDtypeStruct((1, groups * cg), F32)],
        compiler_params=_params("parallel"),
    )(dcat, proj, pool_w, pool_scale)


def _attention(q, k, v, tq):
    t, d = q.shape
    m = k.shape[0]
    dh = d // XATTN_HEADS
    scale = dh ** -0.5

    def body(q_ref, k_ref, v_ref, o_ref):
        s = _bdot(q_ref[...], k_ref[...], tb=True) * scale
        s = s - jnp.max(s, axis=-1, keepdims=True)
        e = jnp.exp(s)
        p = e / jnp.sum(e, axis=-1, keepdims=True)
        o_ref[...] = _bdot(p, v_ref[...]).astype(BF16)

    return pl.pallas_call(
        body, name="xattn_fwd", grid=(XATTN_HEADS, t // tq),
        in_specs=[pl.BlockSpec((tq, dh), lambda h, i: (i, h)), pl.BlockSpec((m, dh), lambda h, i: (0, h)),
                  pl.BlockSpec((m, dh), lambda h, i: (0, h))],
        out_specs=pl.BlockSpec((tq, dh), lambda h, i: (i, h)),
        out_shape=jax.ShapeDtypeStruct((t, d), BF16),
        compiler_params=_params("parallel", "parallel"),
    )(q, k, v)


def _attention_backward(q, k, v, do, tq):
    t, d = q.shape
    m = k.shape[0]
    dh = d // XATTN_HEADS
    scale = dh ** -0.5

    def body(q_ref, k_ref, v_ref, do_ref, dq_ref, dk_ref, dv_ref, dk_acc, dv_acc):
        i = pl.program_id(1)
        qv, kv, vv, dov = q_ref[...], k_ref[...], v_ref[...], do_ref[...]
        s = _bdot(qv, kv, tb=True) * scale
        s = s - jnp.max(s, axis=-1, keepdims=True)
        e = jnp.exp(s)
        p = e / jnp.sum(e, axis=-1, keepdims=True)
        dp = _bdot(dov, vv, tb=True)
        ds = p * (dp - jnp.sum(dp * p, axis=-1, keepdims=True)) * scale
        dq_ref[...] = _bdot(ds, kv).astype(BF16)
        dv_part = _bdot(p, dov, ta=True)
        dk_part = _bdot(ds, qv, ta=True)

        @pl.when(i == 0)
        def _():
            dk_acc[...] = dk_part
            dv_acc[...] = dv_part

        @pl.when(i > 0)
        def _():
            dk_acc[...] += dk_part
            dv_acc[...] += dv_part

        @pl.when(i == pl.num_programs(1) - 1)
        def _():
            dk_ref[...] = dk_acc[...].astype(BF16)
            dv_ref[...] = dv_acc[...].astype(BF16)

    qblk = pl.BlockSpec((tq, dh), lambda h, i: (i, h))
    kblk = pl.BlockSpec((m, dh), lambda h, i: (0, h))
    return pl.pallas_call(
        body, name="xattn_bwd", grid=(XATTN_HEADS, t // tq),
        in_specs=[qblk, kblk, kblk, qblk],
        out_specs=[qblk, kblk, kblk],
        out_shape=[jax.ShapeDtypeStruct((t, d), BF16), jax.ShapeDtypeStruct((m, d), BF16),
                   jax.ShapeDtypeStruct((m, d), BF16)],
        scratch_shapes=[pltpu.VMEM((m, dh), F32), pltpu.VMEM((m, dh), F32)],
        compiler_params=_params("parallel", "arbitrary"),
    )(q, k, v, do)


def _ln_backward_rows(name, dmain, dres, xhat, rstd, gamma, tm):
    t, d = xhat.shape

    def body(m_ref, r_ref, x_ref, s_ref, g_ref, du_ref, dub_ref, dg_ref, db_ref):
        du, dg, db = _ln_backward_math(m_ref[...] + ALPHA * r_ref[...], x_ref[...], s_ref[...], g_ref[...])
        du_ref[...] = du
        dub_ref[...] = du.astype(BF16)
        first = pl.program_id(0) == 0

        @pl.when(first)
        def _():
            dg_ref[...] = dg
            db_ref[...] = db

        @pl.when(jnp.logical_not(first))
        def _():
            dg_ref[...] += dg
            db_ref[...] += db

    row = pl.BlockSpec((tm, d), lambda i: (i, 0))
    vec = pl.BlockSpec((1, d), lambda i: (0, 0))
    return pl.pallas_call(
        body, name=name, grid=(t // tm,),
        in_specs=[row, row, row, pl.BlockSpec((tm, 1), lambda i: (i, 0)), vec],
        out_specs=[row, row, vec, vec],
        out_shape=[jax.ShapeDtypeStruct((t, d), F32), jax.ShapeDtypeStruct((t, d), BF16),
                   jax.ShapeDtypeStruct((1, d), F32), jax.ShapeDtypeStruct((1, d), F32)],
        compiler_params=_params("arbitrary"),
    )(dmain, dres, xhat, rstd, gamma)


def _loss_and_ln_backward(xhat, rstd, gamma, beta, target, tm):
    t, d = xhat.shape

    def body(x_ref, r_ref, g_ref, b_ref, t_ref, du_ref, dub_ref, dg_ref, db_ref, loss_ref):
        xh = x_ref[...]
        g = g_ref[...]
        diff = xh * g + b_ref[...] - t_ref[...]
        part = jnp.sum(jnp.sum(diff * diff, axis=1, keepdims=True), axis=0, keepdims=True) * (0.5 / d)
        dy = diff * (1.0 / d)
        du, dg, db = _ln_backward_math(dy, xh, r_ref[...], g)
        du_ref[...] = du
        dub_ref[...] = du.astype(BF16)
        lossrow = jnp.broadcast_to(part, (1, HEAD_DIM))
        first = pl.program_id(0) == 0

        @pl.when(first)
        def _():
            dg_ref[...] = dg
            db_ref[...] = db
            loss_ref[...] = lossrow

        @pl.when(jnp.logical_not(first))
        def _():
            dg_ref[...] += dg
            db_ref[...] += db
            loss_ref[...] += lossrow

    row = pl.BlockSpec((tm, d), lambda i: (i, 0))
    vec = pl.BlockSpec((1, d), lambda i: (0, 0))
    return pl.pallas_call(
        body, name="loss_ln3_bwd", grid=(t // tm,),
        in_specs=[row, pl.BlockSpec((tm, 1), lambda i: (i, 0)), vec, vec, row],
        out_specs=[row, row, vec, vec, pl.BlockSpec((1, HEAD_DIM), lambda i: (0, 0))],
        out_shape=[jax.ShapeDtypeStruct((t, d), F32), jax.ShapeDtypeStruct((t, d), BF16),
                   jax.ShapeDtypeStruct((1, d), F32), jax.ShapeDtypeStruct((1, d), F32),
                   jax.ShapeDtypeStruct((1, HEAD_DIM), F32)],
        compiler_params=_params("arbitrary"),
    )(xhat, rstd, gamma, beta, target)


def _after(token, a):
    return a if token is None else a + token[:1, :1].astype(a.dtype)


def _pick(n, prefs):
    for p in prefs:
        if n % p == 0:
            return p
    return n


def _local_step(x, mem, target, w, x_bf=None):
    t, d = x.shape
    heads = w["a_log"].shape[1]
    gw = heads * HEAD_DIM
    groups, cg, _ = w["pool_w"].shape
    pw = groups * cg
    n_main = 4 * gw + pw
    in_cols = n_main + 2 * heads
    s_in = w["w_in_t"].shape[0]

    tm = _pick(t, (512, 256, 128))
    tm_ln = _pick(t, (256, 128))
    tm_big = _pick(t, (1024, 512, 256, 128))
    tk = _pick(d, K_STEPS)

    w_in_t = w["w_in_t"].reshape(in_cols, d)
    w_p_t = w_in_t[4 * gw + 2 * heads:]
    w_ba_t = jnp.pad(w_in_t[4 * gw:4 * gw + 2 * heads], ((0, HEAD_DIM - 2 * heads), (0, 0)))
    x_bf = x.astype(BF16) if x_bf is None else x_bf
    mem_bf = mem.astype(BF16)

    tn_d = _pick(d, (1024, 512, 256, 128))
    proj = _plain("proj_main", x_bf, w_in_t, tb=True, n_used=4 * gw, tm=tm_big, tn=_pick(4 * gw, (1024, 512, 256, 128)),
                  tk=tk, out_dtype=F32)
    pproj = _plain("proj_pool", x_bf, w_p_t, tb=True, tm=tm_big, tn=_pick(pw, (1024, 512, 256, 128)), tk=tk, out_dtype=F32)
    ea, dtb = _gate_vectors(w["a_log"], w["dt_bias"], heads)
    vec128 = lambda i, j: (0, 0)
    ba, bg = _matmul(
        "proj_gates", x_bf, w_ba_t, tb=True, tm=tm, tn=HEAD_DIM, tk=tk,
        extra=[(ea, (1, HEAD_DIM), vec128), (dtb, (1, HEAD_DIM), vec128)],
        outs=[(jax.ShapeDtypeStruct((t, HEAD_DIM), F32), (tm, HEAD_DIM), _tile)] * 2,
        epilogue=_gates_epilogue(heads))
    qkv = _gdn_pre(proj, w["conv_w"], heads)
    o_gdn, states, *solved = _gdn_core(qkv, bg, heads)
    cat_g = _gdn_post(o_gdn, proj, 3 * gw, w["gdn_norm_w"], heads, tm)
    token = yield ("pass", 1, cat_g)
    cat_p = _pool_forward(pproj, 0, w["pool_w"], _after(token, w["pool_scale"]))
    cat = jnp.concatenate([cat_g, cat_p], axis=1)
    w = {**w, **(yield ("weights", 1, cat))}
    h1, h1_bf, xhat1, rstd1 = _ln_forward("mix_ln1", cat, w["w_out"], x, w["ln1_g"], w["ln1_b"], tm=tm_ln, tk=tk)

    h1_bf = _after((yield ("relay", None, h1_bf)), h1_bf)
    q = _plain("xattn_q", h1_bf, w["xq_w"], tm=tm, tn=tn_d, tk=tk, out_dtype=BF16)
    mlen = mem.shape[0]
    tm_mem = _pick(mlen, (256, 128))
    k = _plain("xattn_k", mem_bf, w["xk_w"], tm=tm_mem, tn=tn_d, tk=tk, out_dtype=BF16)
    v = _plain("xattn_v", mem_bf, w["xv_w"], tm=tm_mem, tn=tn_d, tk=tk, out_dtype=BF16)
    att = _attention(q, k, v, tm)
    h2, h2_bf, xhat2, rstd2 = _ln_forward("xo_ln2", att, w["xo_w"], h1, w["ln2_g"], w["ln2_b"], tm=tm_ln, tk=tk)

    w = {**w, **(yield ("weights", 2, h2_bf))}
    s_up = w["w_up3"].shape[0]
    ff = s_up * w["w_up3"].shape[2]
    tn_f = _pick(ff // s_up, (1024, 512, 256, 128))

    def up_epi(acc, ex, out, i):
        r = jnp.maximum(acc, 0.0)
        out[0][...] = (r * r).astype(BF16)
        out[1][...] = (2.0 * r).astype(BF16)

    act, act_grad = _matmul(
        "mlp_up", h2_bf, w["w_up3"], b_blocks=s_up, tm=tm_big, tn=tn_f, tk=tk,
        outs=[(jax.ShapeDtypeStruct((t, ff), BF16), (tm_big, tn_f), _tile)] * 2, epilogue=up_epi)
    w = {**w, **(yield ("weights", 3, act))}
    tk_f = _pick(ff, K_STEPS)
    xhat3, rstd3 = _ln_forward("down_ln3", act, w["w_down"], h2, w["ln3_g"], w["ln3_b"], tm=tm, tk=tk_f, want_h=False)

    grads = {}
    du3, du3_bf, grads["ln3_g"], grads["ln3_b"], loss = _loss_and_ln_backward(
        xhat3, rstd3, w["ln3_g"], w["ln3_b"], target, tm_ln)

    def dup_epi(acc, ex, out, i):
        out[0][...] = (acc * ex[0][...].astype(F32)).astype(BF16)

    dup = _matmul(
        "mlp_down_dx", du3_bf, w["w_down"], tb=True, tm=tm_big, tn=tn_f, tk=tk,
        extra=[(act_grad, (tm_big, tn_f), _tile)],
        outs=[(jax.ShapeDtypeStruct((t, ff), BF16), (tm_big, tn_f), _tile)], epilogue=dup_epi)[0]
    tk_t = _pick(t, K_STEPS)
    tm_w = _pick(d, (512, 256, 128))
    grads["w_down"] = _plain("mlp_down_dw", act, du3_bf, ta=True, tm=_pick(ff, (512, 256, 128)), tn=d, tk=tk_t,
                             out_dtype=F32)
    grads["w_up3"] = _plain("mlp_up_dw", h2_bf, dup, ta=True, tm=tm_w, tn=ff // s_up, tk=tk_t, out_dtype=F32, out3=s_up,
                            n_outer=True)
    token = yield ("grads", 0, {n: grads.pop(n) for n in ("w_down", "w_up3")})
    dh2 = _plain("mlp_up_dx", dup, w["w_up3"], tb=True, b_blocks=s_up, tm=tm_big, tn=tn_d,
                 tk=_pick(ff // s_up, K_STEPS), out_dtype=F32)
    du2, du2_bf, grads["ln2_g"], grads["ln2_b"] = _ln_backward_rows(
        "ln2_bwd", dh2, du3, xhat2, rstd2, _after(token, w["ln2_g"]), tm_ln)
    token = yield ("poll", 0, du2_bf)

    grads["xo_w"] = _plain("xo_dw", att, du2_bf, ta=True, tm=tm_w, tn=d, tk=tk_t, out_dtype=F32)
    datt = _plain("xo_dx", du2_bf, w["xo_w"], tb=True, tm=tm, tn=tn_d, tk=tk, out_dtype=BF16)
    dq, dk, dv = _attention_backward(q, k, v, datt, tm)
    tk_m = _pick(mlen, (256, 128))
    grads["xq_w"] = _plain("xq_dw", h1_bf, dq, ta=True, tm=tm_w, tn=d, tk=tk_t, out_dtype=F32)
    grads["xk_w"] = _plain("xk_dw", mem_bf, dk, ta=True, tm=tm_w, tn=tn_d, tk=tk_m, out_dtype=F32)
    grads["xv_w"] = _plain("xv_dw", mem_bf, dv, ta=True, tm=tm_w, tn=tn_d, tk=tk_m, out_dtype=F32)
    du1, du1_bf, grads["ln1_g"], grads["ln1_b"] = _ln_backward(
        "xq_dx_ln1", dq, w["xq_w"], du2, xhat1, rstd1, _after(token, w["ln1_g"]), tm=tm_ln, tk=tk)

    grads["w_out"] = _plain("out_dw", cat, du1_bf, ta=True, tm=tm_w, tn=d, tk=tk_t, out_dtype=F32)
    token = yield ("grads", 1, {n: grads.pop(n) for n in ("xo_w", "xq_w", "xk_w", "xv_w", "w_out")})
    dcat = _plain("out_dx", du1_bf, w["w_out"], tb=True, tm=tm, tn=tn_d, tk=tk, out_dtype=F32)
    dp, grads["pool_w"], grads["pool_scale"] = _pool_backward(dcat, gw, pproj, 0, w["pool_w"],
                                                              _after(token, w["pool_scale"]))
    do_gdn, dz, grads["gdn_norm_w"] = _gdn_post_backward(dcat, o_gdn, proj, 3 * gw, _after(token, w["gdn_norm_w"]),
                                                         heads, tm)
    dqkv, dbg = _gdn_core_backward(qkv, bg, states, solved, do_gdn, heads)
    token = yield ("poll", 1, dqkv)
    dqkv_pre, grads["conv_w"] = _gdn_pre_backward(proj, _after(token, w["conv_w"]), dqkv, heads)
    dba, dalog_row, ddt_row = _gates_backward(ba, bg, dbg, ea, dtb, heads)
    grads["a_log"] = dalog_row[:, heads:2 * heads]
    grads["dt_bias"] = ddt_row[:, heads:2 * heads]

    k_pad = -(-in_cols // HEAD_DIM) * HEAD_DIM
    dproj = jnp.concatenate([dqkv_pre, dz, dba[:, :2 * heads], dp, jnp.zeros((t, k_pad - in_cols), BF16)], axis=1)
    dw_in_t = _plain("proj_dw", dproj, x_bf, ta=True, tm=_pick(k_pad, (512, 256, 128)), tn=d, tk=tk_t, out_dtype=F32)
    grads["w_in_t"] = dw_in_t[:in_cols].reshape(s_in, in_cols // s_in, d)

    def dx_epi(acc, ex, out, i):
        out[0][...] = acc + ALPHA * ex[0][...]

    token = yield ("grads", 2, {n: grads.pop(n) for n in ("w_in_t", "pool_w")})
    w_in_t_pad = jnp.pad(w_in_t, ((0, k_pad - in_cols), (0, 0)))
    grad_x = _matmul(
        "proj_dx", dproj, w_in_t_pad, tm=tm, tn=tn_d, tk=k_pad, extra=[(_after(token, du1), (tm, tn_d), _tile)],
        outs=[(jax.ShapeDtypeStruct((t, d), F32), (tm, tn_d), _tile)], epilogue=dx_epi)[0]
    yield ("poll", 2, grad_x)
    return loss, grad_x, grads


def _adamw(name, w, g, m, v):
    r, c = w.shape
    if r % 8 == 0:
        tr = _pick(r, (256, 128, 64, 32, 16, 8))
        blk, steps = pl.BlockSpec((tr, c), lambda i: (i, 0)), r // tr
    else:
        tc = _pick(c, (256, 128))
        blk, steps = pl.BlockSpec((r, tc), lambda i: (0, i)), c // tc
    c1 = 1.0 - ADAM_B1 ** ADAM_STEP
    c2 = 1.0 - ADAM_B2 ** ADAM_STEP

    def body(w_ref, g_ref, m_ref, v_ref, d_ref, mo_ref, vo_ref, go_ref):
        gv = g_ref[...]
        mn = ADAM_B1 * m_ref[...] + (1.0 - ADAM_B1) * gv
        vn = ADAM_B2 * v_ref[...] + (1.0 - ADAM_B2) * (gv * gv)
        d_ref[...] = -ADAM_LR * ((mn / c1) / (jnp.sqrt(vn / c2) + ADAM_EPS) + ADAM_WD * w_ref[...])
        mo_ref[...] = mn
        vo_ref[...] = vn
        go_ref[...] = gv

    return pl.pallas_call(
        body, name=name, grid=(steps,), in_specs=[blk] * 4, out_specs=[blk] * 4,
        out_shape=[jax.ShapeDtypeStruct((r, c), F32)] * 4,
        compiler_params=_params("parallel"),
    )(w, g, m, v)


def _place():
    x, y, c = lax.axis_index("x"), lax.axis_index("y"), lax.axis_index("c")
    chips = [(1 - x, y), (x, 1 - y), (1 - x, 1 - y)]
    return x, y, c, chips


HBM = pl.BlockSpec(memory_space=pltpu.HBM)


SEM = pl.BlockSpec(memory_space=pltpu.SEMAPHORE)
ANY = pl.BlockSpec(memory_space=pl.ANY)
EFFECT = pltpu.SideEffectType.DATAFLOW_SIDE_EFFECTING


def _in_hbm(a):
    return pltpu.with_memory_space_constraint(a, pltpu.HBM)


def _remote(src, dst, send_sem, recv_sem, to):
    return pltpu.make_async_remote_copy(src_ref=src, dst_ref=dst, send_sem=send_sem, recv_sem=recv_sem,
                                        device_id=to, device_id_type=MESH)


def _by_rows(rows):
    return rows % 32 == 0


def _half_shape(rows, cols):
    return (rows // 2, cols) if _by_rows(rows) else (rows, cols // 2)


def _half(ref, which, *lead):
    rows, cols = ref.shape[-2:]
    if _by_rows(rows):
        return ref.at[(*lead, pl.ds(which * (rows // 2), rows // 2))]
    return ref.at[(*lead, slice(None), pl.ds(which * (cols // 2), cols // 2))]


def _landed(lands, i, shard_index, which):
    return _half(lands[i], which, shard_index)


def _routes():
    x, y, c, _ = _place()
    first = (jnp.where(c == 0, 1 - x, x), jnp.where(c == 0, y, 1 - y))
    second = (jnp.where(c == 0, x, 1 - x), jnp.where(c == 0, 1 - y, y))
    return first, second, (1 - x, 1 - y)


def _shard_of(chip):
    return 2 * chip[0] + chip[1]


def _gather_start(name, shards, after, relayed=()):
    n = len(shards)
    lands = [lax.empty((N_SHARD,) + s.shape, s.dtype) for s in shards]

    def body(*refs):
        ins, zones = refs[:n], refs[n:2 * n]
        ici_send, ici_recv, own_send, own_recv = refs[2 * n + 1:2 * n + 5]
        token = refs[-1]
        x, y, c, chips = _place()
        me = 2 * x + y
        first, _, _ = _routes()
        for i in range(n):
            if i in relayed:
                _remote(_half(ins[i], c), _landed(zones, i, me, c), ici_send.at[3 * i], ici_recv.at[3 * i],
                        (*first, c)).start()
                continue
            for j, chip in enumerate(chips):
                _remote(_half(ins[i], c), _landed(zones, i, me, c), ici_send.at[3 * i + j],
                        ici_recv.at[3 * i + j], (*chip, c)).start()
        for i in range(n):
            _remote(ins[i], zones[i].at[me], own_send.at[i], own_recv.at[i], (x, y, 1 - c)).start()
        token[...] = jnp.zeros_like(token)

    dma = pltpu.SemaphoreType.DMA
    outs = pl.pallas_call(
        body, name=name,
        in_specs=[HBM] * (2 * n) + [ANY],
        out_shape=(dma((3 * n,)), dma((3 * n,)), dma((n,)), dma((n,)),
                   *[pltpu.HBM(a.shape, a.dtype) for a in shards + lands], jax.ShapeDtypeStruct((8, LANES), F32)),
        out_specs=(SEM, SEM, SEM, SEM, *[HBM] * (2 * n), pl.BlockSpec(memory_space=pltpu.VMEM)),
        input_output_aliases={k: 4 + k for k in range(2 * n)},
        compiler_params=pltpu.CompilerParams(has_side_effects=EFFECT),
    )(*[_in_hbm(a) for a in shards + lands], after)
    sems = dict(zip(("ici_send", "ici_recv", "own_send", "own_recv"), outs[:4]))
    return sems, list(outs[4:4 + n]), list(outs[4 + n:4 + 2 * n]), outs[-1]


def _gather_forward(name, idx, lands, sems, after):
    n = len(idx)

    def body(*refs):
        zones = refs[:n]
        ici_recv = refs[n]
        fwd_send, fwd_recv = refs[n + 2], refs[n + 3]
        x, y, c, chips = _place()
        for k, i in enumerate(idx):
            for j, chip in enumerate(chips):
                half = _landed(zones, k, 2 * chip[0] + chip[1], c)
                _remote(half, half, fwd_send.at[3 * k + j], ici_recv.at[3 * i + j], (*chip, c)).wait_recv()
                _remote(half, half, fwd_send.at[3 * k + j], fwd_recv.at[3 * k + j], (x, y, 1 - c)).start()
        refs[-1][...] = jnp.zeros_like(refs[-1])

    dma = pltpu.SemaphoreType.DMA
    outs = pl.pallas_call(
        body, name=name,
        in_specs=[HBM] * n + [SEM, ANY],
        out_shape=(dma((3 * n,)), dma((3 * n,)), *[pltpu.HBM(a.shape, a.dtype) for a in lands],
                   jax.ShapeDtypeStruct((8, LANES), F32)),
        out_specs=(SEM, SEM, *[HBM] * n, pl.BlockSpec(memory_space=pltpu.VMEM)),
        input_output_aliases={k: 2 + k for k in range(n)},
        compiler_params=pltpu.CompilerParams(has_side_effects=EFFECT),
    )(*lands, sems["ici_recv"], after)
    return (outs[0], outs[1]), list(outs[2:2 + n]), outs[-1]


def _gather_wait(name, idx, shards, lands, sems, fwd, after):
    n = len(idx)

    def body(*refs):
        ins, zones = refs[:n], refs[n:2 * n]
        ici_send, own_send, own_recv, fwd_send, fwd_recv = refs[2 * n:2 * n + 5]
        x, y, c, chips = _place()
        me = 2 * x + y
        for k, i in enumerate(idx):
            mine = _half(ins[k], c)
            for j, chip in enumerate(chips):
                theirs = 2 * chip[0] + chip[1]
                _remote(mine, _landed(zones, k, me, c), ici_send.at[3 * i + j], fwd_recv.at[3 * k + j],
                        (*chip, c)).wait_send()
                sent = _landed(zones, k, theirs, c)
                _remote(sent, sent, fwd_send.at[3 * k + j], fwd_recv.at[3 * k + j], (x, y, 1 - c)).wait_send()
                passed = _landed(zones, k, theirs, 1 - c)
                _remote(passed, passed, fwd_send.at[3 * k + j], fwd_recv.at[3 * k + j], (x, y, 1 - c)).wait_recv()
            own = _remote(ins[k], zones[k].at[me], own_send.at[i], own_recv.at[i], (x, y, 1 - c))
            own.wait_send()
            own.wait_recv()

    outs = pl.pallas_call(
        body, name=name,
        in_specs=[HBM] * (2 * n) + [SEM] * 5 + [ANY],
        out_shape=tuple(pltpu.HBM(a.shape, a.dtype) for a in lands),
        out_specs=tuple([HBM] * n),
        input_output_aliases={n + k: k for k in range(n)},
        compiler_params=pltpu.CompilerParams(has_side_effects=EFFECT),
    )(*shards, *lands, sems["ici_send"], sems["own_send"], sems["own_recv"], fwd[0], fwd[1], after)
    return list(outs)


def _gather_relay(name, idx, shards, lands, sems, after):
    n = len(idx)

    def body(*refs):
        ins, zones, ici_recv = refs[:n], refs[n:2 * n], refs[2 * n]
        relay_send, relay_recv, pass_send, pass_recv = refs[2 * n + 2:2 * n + 6]
        x, y, c, _ = _place()
        first, second, _ = _routes()
        for k, i in enumerate(idx):
            landed = _landed(zones, k, _shard_of(first), c)
            _remote(landed, landed, pass_send.at[k], ici_recv.at[3 * i], (*first, c)).wait_recv()
            _remote(_half(ins[k], c), _landed(zones, k, 2 * x + y, c), relay_send.at[2 * k], relay_recv.at[2 * k],
                    (*second, c)).start()
            _remote(landed, landed, relay_send.at[2 * k + 1], relay_recv.at[2 * k + 1], (*second, c)).start()
            _remote(landed, landed, pass_send.at[k], pass_recv.at[k], (x, y, 1 - c)).start()
        refs[-1][...] = jnp.zeros_like(refs[-1])

    dma = pltpu.SemaphoreType.DMA
    outs = pl.pallas_call(
        body, name=name,
        in_specs=[HBM] * (2 * n) + [SEM, ANY],
        out_shape=(dma((2 * n,)), dma((2 * n,)), dma((n,)), dma((n,)), *[pltpu.HBM(a.shape, a.dtype) for a in lands],
                   jax.ShapeDtypeStruct((8, LANES), F32)),
        out_specs=(SEM, SEM, SEM, SEM, *[HBM] * n, pl.BlockSpec(memory_space=pltpu.VMEM)),
        input_output_aliases={n + k: 4 + k for k in range(n)},
        compiler_params=pltpu.CompilerParams(has_side_effects=EFFECT),
    )(*shards, *lands, sems["ici_recv"], after)
    return outs[:4], list(outs[4:4 + n]), outs[-1]


def _gather_forward_relayed(name, ks, lands, relay, after):
    n = len(ks)

    def body(*refs):
        zones, relay_recv = refs[:n], refs[n]
        fwd_send, fwd_recv = refs[n + 2], refs[n + 3]
        x, y, c, _ = _place()
        _, second, diagonal = _routes()
        for p, k in enumerate(ks):
            for j, chip in enumerate((second, diagonal)):
                landed = _landed(zones, p, _shard_of(chip), c)
                _remote(landed, landed, fwd_send.at[2 * p + j], relay_recv.at[2 * k + j], (*second, c)).wait_recv()
                _remote(landed, landed, fwd_send.at[2 * p + j], fwd_recv.at[2 * p + j], (x, y, 1 - c)).start()

    dma = pltpu.SemaphoreType.DMA
    outs = pl.pallas_call(
        body, name=name,
        in_specs=[HBM] * n + [SEM, ANY],
        out_shape=(dma((2 * n,)), dma((2 * n,)), *[pltpu.HBM(a.shape, a.dtype) for a in lands]),
        out_specs=(SEM, SEM, *[HBM] * n),
        input_output_aliases={k: 2 + k for k in range(n)},
        compiler_params=pltpu.CompilerParams(has_side_effects=EFFECT),
    )(*lands, relay[1], after)
    return (outs[0], outs[1]), list(outs[2:])


def _gather_wait_relayed(name, idx, ks, shards, lands, sems, relay, fwd, after):
    n = len(idx)

    def body(*refs):
        ins, zones = refs[:n], refs[n:2 * n]
        ici_send, own_send, own_recv, relay_send, pass_send, pass_recv, fwd_send, fwd_recv = refs[2 * n:2 * n + 8]
        x, y, c, _ = _place()
        me = 2 * x + y
        sibling = (x, y, 1 - c)
        first, second, diagonal = _routes()
        for p, (i, k) in enumerate(zip(idx, ks)):
            mine, at_peer = _half(ins[p], c), _landed(zones, p, me, c)
            from_first = _landed(zones, p, _shard_of(first), c)
            _remote(mine, at_peer, ici_send.at[3 * i], pass_recv.at[k], (*first, c)).wait_send()
            _remote(mine, at_peer, relay_send.at[2 * k], pass_recv.at[k], (*second, c)).wait_send()
            _remote(from_first, from_first, relay_send.at[2 * k + 1], pass_recv.at[k], (*second, c)).wait_send()
            _remote(from_first, from_first, pass_send.at[k], pass_recv.at[k], sibling).wait_send()
            theirs = _landed(zones, p, _shard_of(second), 1 - c)
            _remote(theirs, theirs, pass_send.at[k], pass_recv.at[k], sibling).wait_recv()
            for j, (sent, got) in enumerate(((second, first), (diagonal, diagonal))):
                out_half = _landed(zones, p, _shard_of(sent), c)
                _remote(out_half, out_half, fwd_send.at[2 * p + j], fwd_recv.at[2 * p + j], sibling).wait_send()
                in_half = _landed(zones, p, _shard_of(got), 1 - c)
                _remote(in_half, in_half, fwd_send.at[2 * p + j], fwd_recv.at[2 * p + j], sibling).wait_recv()
            own = _remote(ins[p], zones[p].at[me], own_send.at[i], own_recv.at[i], sibling)
            own.wait_send()
            own.wait_recv()

    outs = pl.pallas_call(
        body, name=name,
        in_specs=[HBM] * (2 * n) + [SEM] * 8 + [ANY],
        out_shape=tuple(pltpu.HBM(a.shape, a.dtype) for a in lands),
        out_specs=tuple([HBM] * n),
        input_output_aliases={n + k: k for k in range(n)},
        compiler_params=pltpu.CompilerParams(has_side_effects=EFFECT),
    )(*shards, *lands, sems["ici_send"], sems["own_send"], sems["own_recv"], relay[0], relay[2], relay[3],
      fwd[0], fwd[1], after)
    return list(outs)


def _all_reduce_small(name, slab, after=None):
    r, width = slab.shape
    ndev = 8

    def body(x_ref, after_ref, out_ref, buf, send_sems, recv_sems):
        x, y, c, _ = _place()
        me = 4 * x + 2 * y + c
        buf[me] = x_ref[...]
        copies = []
        for k in range(1, ndev):
            peer = jnp.bitwise_xor(me, k)
            to = (peer // 4, (peer // 2) % 2, peer % 2)
            cp = pltpu.make_async_remote_copy(src_ref=x_ref, dst_ref=buf.at[me], send_sem=send_sems.at[k - 1],
                                              recv_sem=recv_sems.at[k - 1], device_id=to, device_id_type=MESH)
            cp.start()
            copies.append(cp)
        for k in range(1, ndev):
            peer = jnp.bitwise_xor(me, k)
            pltpu.make_async_remote_copy(src_ref=x_ref, dst_ref=buf.at[peer], send_sem=send_sems.at[k - 1],
                                         recv_sem=recv_sems.at[k - 1], device_id=(x, y, c),
                                         device_id_type=MESH).wait_recv()
        for cp in copies:
            cp.wait_send()
        total = buf[0]
        for d in range(1, ndev):
            total = total + buf[d]
        out_ref[...] = total

    return pl.pallas_call(
        body, name=name,
        in_specs=[pl.BlockSpec(memory_space=pltpu.VMEM), ANY], out_specs=pl.BlockSpec(memory_space=pltpu.VMEM),
        out_shape=jax.ShapeDtypeStruct((r, width), F32),
        scratch_shapes=[pltpu.VMEM((ndev, r, width), F32), pltpu.SemaphoreType.DMA((ndev - 1,)),
                        pltpu.SemaphoreType.DMA((ndev - 1,))],
        compiler_params=pltpu.CompilerParams(vmem_limit_bytes=VMEM_LIMIT),
    )(slab, slab if after is None else after)


def _half_tiling(rows, cols):
    if _by_rows(rows):
        tr = _pick(rows // 2, (256, 128, 64, 32, 16))
        nb = (rows // 2) // tr
        return (tr, cols), nb, (lambda which, b: (which * nb + b, 0)), (lambda b: (b, 0))
    tc = _pick(cols // 2, (256, 128))
    nb = (cols // 2) // tc
    return (rows, tc), nb, (lambda which, b: (0, which * nb + b)), (lambda b: (0, b))


def _chip_partial(name, grad, other, core):
    s, r, cdim = grad.shape
    blk, nb, whole, within = _half_tiling(r, cdim)

    def body(core_ref, g_ref, o_ref, out_ref):
        out_ref[...] = (g_ref[...] + o_ref[...]).astype(BF16)

    return pl.pallas_call(
        body, name=name,
        grid_spec=pltpu.PrefetchScalarGridSpec(
            num_scalar_prefetch=1, grid=(s, nb),
            in_specs=[pl.BlockSpec((None,) + blk, lambda j, b, core_ref: (j,) + whole(core_ref[0], b)),
                      pl.BlockSpec((None,) + blk, lambda j, b, core_ref: (j,) + within(b))],
            out_specs=pl.BlockSpec((None,) + blk, lambda j, b, core_ref: (j,) + within(b))),
        out_shape=jax.ShapeDtypeStruct((s,) + _half_shape(r, cdim), BF16),
        compiler_params=_params("parallel", "parallel"),
    )(core, grad, other)


def _partial_copies(ins, zones, send_sems, recv_sems):
    x, y, c, chips = _place()
    return [_remote(ins[i].at[2 * chip[0] + chip[1]], zones[i].at[j], send_sems.at[3 * i + j],
                    recv_sems.at[3 * i + j], (*chip, c))
            for i in range(len(ins)) for j, chip in enumerate(chips)]


def _swap_copies(ins, zones, send_sems, recv_sems):
    x, y, c, _ = _place()
    copies = []
    for i in range(len(ins)):
        for s in range(N_SHARD):
            copies.append(_remote(_half(ins[i], 1 - c, s), zones[i].at[s],
                                  send_sems.at[N_SHARD * i + s], recv_sems.at[N_SHARD * i + s], (x, y, 1 - c)))
    return copies


def _exchange_start(name, plan, sources, lands, per_array):
    n = len(sources)
    lands = [lax.empty(shape, dtype) for shape, dtype in lands]

    def body(*refs):
        for cp in plan(refs[:n], refs[n:2 * n], refs[2 * n], refs[2 * n + 1]):
            cp.start()
        refs[-1][...] = jnp.zeros_like(refs[-1])

    dma = pltpu.SemaphoreType.DMA
    outs = pl.pallas_call(
        body, name=name,
        in_specs=[HBM] * (2 * n),
        out_shape=(dma((per_array * n,)), dma((per_array * n,)),
                   *[pltpu.HBM(a.shape, a.dtype) for a in list(sources) + lands], jax.ShapeDtypeStruct((8, LANES), F32)),
        out_specs=(SEM, SEM, *[HBM] * (2 * n), pl.BlockSpec(memory_space=pltpu.VMEM)),
        input_output_aliases={k: 2 + k for k in range(2 * n)},
        compiler_params=pltpu.CompilerParams(has_side_effects=EFFECT),
    )(*[_in_hbm(a) for a in list(sources) + lands])
    return (outs[0], outs[1]), list(outs[2:2 + n]), list(outs[2 + n:2 + 2 * n]), outs[-1]


def _exchange_wait(name, plan, started, after):
    sems, partials, lands, _ = started
    n = len(partials)

    def body(*refs):
        for cp in plan(refs[:n], refs[n:2 * n], refs[2 * n], refs[2 * n + 1]):
            cp.wait_send()
            cp.wait_recv()

    outs = pl.pallas_call(
        body, name=name,
        in_specs=[HBM] * (2 * n) + [SEM, SEM] + [ANY] * len(after),
        out_shape=tuple(pltpu.HBM(a.shape, a.dtype) for a in lands),
        out_specs=tuple([HBM] * n),
        input_output_aliases={n + k: k for k in range(n)},
        compiler_params=pltpu.CompilerParams(has_side_effects=EFFECT),
    )(*partials, *lands, sems[0], sems[1], *after)
    return list(outs)


def _reduce_own(name, grad, other, received, where):
    s, r, cdim = grad.shape
    blk, nb, whole, within = _half_tiling(r, cdim)

    def body(where_ref, g_ref, o_ref, r_ref, out_ref):
        total = g_ref[...] + o_ref[...]
        for j in range(3):
            total = total + r_ref[j].astype(F32)
        out_ref[...] = total

    return pl.pallas_call(
        body, name=name,
        grid_spec=pltpu.PrefetchScalarGridSpec(
            num_scalar_prefetch=1, grid=(nb,),
            in_specs=[pl.BlockSpec((None,) + blk, lambda b, w_ref: (w_ref[0],) + whole(w_ref[1], b)),
                      pl.BlockSpec((None,) + blk, lambda b, w_ref: (w_ref[0],) + within(b)),
                      pl.BlockSpec((3,) + blk, lambda b, w_ref: (0,) + within(b))],
            out_specs=pl.BlockSpec(blk, lambda b, w_ref: whole(w_ref[1], b))),
        out_shape=jax.ShapeDtypeStruct((r, cdim), F32),
        compiler_params=_params("parallel"),
    )(where, grad, other, received)


def _join_start(name, halves):
    n = len(halves)

    def body(*refs):
        bufs, send_sems, recv_sems = refs[:n], refs[n], refs[n + 1]
        x, y, c, _ = _place()
        for i in range(n):
            mine = _half(bufs[i], c)
            _remote(mine, mine, send_sems.at[i], recv_sems.at[i], (x, y, 1 - c)).start()
        refs[-1][...] = jnp.zeros_like(refs[-1])

    dma = pltpu.SemaphoreType.DMA
    outs = pl.pallas_call(
        body, name=name,
        in_specs=[HBM] * n,
        out_shape=(dma((n,)), dma((n,)), *[pltpu.HBM(h.shape, F32) for h in halves], jax.ShapeDtypeStruct((8, LANES), F32)),
        out_specs=(SEM, SEM, *[HBM] * n, pl.BlockSpec(memory_space=pltpu.VMEM)),
        input_output_aliases={k: 2 + k for k in range(n)},
        compiler_params=pltpu.CompilerParams(has_side_effects=EFFECT),
    )(*[_in_hbm(h) for h in halves])
    return (outs[0], outs[1]), list(outs[2:2 + n]), outs[-1]


def _join_wait(name, started, after):
    sems, bufs, _ = started
    n = len(bufs)

    def body(*refs):
        bufs, send_sems, recv_sems = refs[:n], refs[n], refs[n + 1]
        x, y, c, _ = _place()
        for i in range(n):
            mine, theirs = _half(bufs[i], c), _half(bufs[i], 1 - c)
            _remote(mine, mine, send_sems.at[i], recv_sems.at[i], (x, y, 1 - c)).wait_send()
            _remote(theirs, theirs, send_sems.at[i], recv_sems.at[i], (x, y, 1 - c)).wait_recv()

    outs = pl.pallas_call(
        body, name=name,
        in_specs=[HBM] * n + [SEM, SEM] + [ANY] * len(after),
        out_shape=tuple(pltpu.HBM(b.shape, F32) for b in bufs),
        out_specs=tuple([HBM] * n),
        input_output_aliases={k: k for k in range(n)},
        compiler_params=pltpu.CompilerParams(has_side_effects=EFFECT),
    )(*bufs, sems[0], sems[1], *after)
    return list(outs)


BIG = ("w_in", "pool_w", "w_out", "xq_w", "xk_w", "xv_w", "xo_w", "w_up", "w_down", "conv_w")
KEPT_F32 = ("conv_w",)
GATHER_GROUPS = ((0, 1, 9), (2, 3, 4, 5, 6), (7,), (8,))
RELAYED = (7, 8)
SMALL = ("conv_w", "a_log", "dt_bias", "gdn_norm_w", "pool_scale", "ln1_g", "ln1_b", "ln2_g", "ln2_b", "ln3_g", "ln3_b")
ORDER = ("w_in", "conv_w", "a_log", "dt_bias", "gdn_norm_w", "pool_w", "pool_scale", "w_out", "ln1_g", "ln1_b",
         "xq_w", "xk_w", "xv_w", "xo_w", "ln2_g", "ln2_b", "w_up", "w_down", "ln3_g", "ln3_b")
LANES = 128


def _rows(flat_len):
    return -(-flat_len // LANES)


def _pack(pieces):
    out = []
    for p in pieces:
        flat = p.reshape(-1).astype(F32)
        out.append(jnp.pad(flat, (0, _rows(flat.shape[0]) * LANES - flat.shape[0])).reshape(-1, LANES))
    slab = jnp.concatenate(out, axis=0)
    return jnp.pad(slab, ((0, -slab.shape[0] % 8), (0, 0)))


def _unpack(slab, shapes):
    out, row = [], 0
    for shp in shapes:
        size = math.prod(shp)
        out.append(slab[row:row + _rows(size)].reshape(-1)[:size].reshape(shp))
        row += _rows(size)
    return out


TRANSPOSED = ("w_in",)


def _as2d(name, a):
    a = a[0]
    if name in TRANSPOSED:
        return jnp.swapaxes(a, 0, 1)
    return a.reshape(-1, a.shape[-1]) if a.ndim == 3 else a


def _from2d(name, a, shape):
    return (jnp.swapaxes(a, 0, 1) if name in TRANSPOSED else a).reshape(shape)


def kernel(x, mem, w_in, conv_w, a_log, dt_bias, gdn_norm_w, pool_w, pool_scale, w_out, ln1_g, ln1_b, xq_w, xk_w, xv_w, xo_w, ln2_g, ln2_b, w_up, w_down, ln3_g, ln3_b, loss_target, m_w_in, m_conv_w, m_a_log, m_dt_bias, m_gdn_norm_w, m_pool_w, m_pool_scale, m_w_out, m_ln1_g, m_ln1_b, m_xq_w, m_xk_w, m_xv_w, m_xo_w, m_ln2_g, m_ln2_b, m_w_up, m_w_down, m_ln3_g, m_ln3_b, v_w_in, v_conv_w, v_a_log, v_dt_bias, v_gdn_norm_w, v_pool_w, v_pool_scale, v_w_out, v_ln1_g, v_ln1_b, v_xq_w, v_xk_w, v_xv_w, v_xo_w, v_ln2_g, v_ln2_b, v_w_up, v_w_down, v_ln3_g, v_ln3_b):
    given = dict(locals())
    cx, cy, cc = lax.axis_index("x"), lax.axis_index("y"), lax.axis_index("c")
    me = 2 * cx + cy
    groups = pool_w.shape[1]
    cs = pool_w.shape[2]
    kk, conv_cols = conv_w.shape[1], conv_w.shape[2]
    core = cc.astype(jnp.int32).reshape(1)
    where = jnp.stack([me, cc]).astype(jnp.int32)

    started = {}
    wts = {}

    def start(name, idx, after, token=None):
        casts = [_after(token, _as2d(BIG[i], given[BIG[i]])).astype(F32 if BIG[i] in KEPT_F32 else BF16) for i in idx]
        relayed = tuple(k for k, i in enumerate(idx) if i in RELAYED)
        sems, shards, lands, token = _gather_start(name, casts, after, relayed)
        for k, i in enumerate(idx):
            started[i] = (sems, k, shards[k], lands[k])
        return token

    token = start("gather_start_first", GATHER_GROUPS[0], x)
    token = start("gather_start_rest", tuple(i for group in GATHER_GROUPS[1:] for i in group), token, token)

    relay = {}

    def send_on(after):
        members = [started[i] for i in RELAYED]
        relay["sems"], zones, token = _gather_relay("gather_relay", [m[1] for m in members], [m[2] for m in members],
                                                    [m[3] for m in members], members[0][0], after)
        relay["zones"] = dict(zip(RELAYED, zones))
        return token

    passed = {}

    def pass_on(group, after):
        members = [started[i] for i in GATHER_GROUPS[group]]
        fwd, zones, token = _gather_forward(f"gather_forward_{group}", [m[1] for m in members], [m[3] for m in members],
                                            members[0][0], after)
        passed[group] = (fwd, zones)
        return token

    def fetch(group, after):
        members = [started[i] for i in GATHER_GROUPS[group]]
        sems, idx = members[0][0], [m[1] for m in members]
        shards = [m[2] for m in members]
        if GATHER_GROUPS[group][0] in RELAYED:
            ks = [RELAYED.index(i) for i in GATHER_GROUPS[group]]
            zones = [relay["zones"][i] for i in GATHER_GROUPS[group]]
            fwd, zones = _gather_forward_relayed(f"gather_forward_{group}", ks, zones, relay["sems"], after)
            got = _gather_wait_relayed(f"gather_wait_{group}", idx, ks, shards, zones, sems, relay["sems"], fwd, after)
        else:
            if group not in passed:
                pass_on(group, after)
            fwd, zones = passed[group]
            got = _gather_wait(f"gather_wait_{group}", idx, shards, zones, sems, fwd, after)
        full = dict(zip([BIG[i] for i in GATHER_GROUPS[group]], got))
        out = {}
        for n, a in full.items():
            if n == "w_in":
                out["w_in_t"] = a
            elif n == "w_up":
                out["w_up3"] = a
            elif n == "pool_w":
                out[n] = a.reshape(N_SHARD, groups, cs, -1).transpose(1, 0, 2, 3).reshape(groups, N_SHARD * cs, -1)
            elif n == "conv_w":
                out[n] = a.transpose(1, 0, 2).reshape(kk, N_SHARD * conv_cols)
            else:
                out[n] = a.reshape(-1, a.shape[-1])
        return out

    for n in ("a_log", "dt_bias", "gdn_norm_w", "pool_scale", "ln1_g", "ln1_b", "ln2_g", "ln2_b", "ln3_g", "ln3_b"):
        wts[n] = given[n]
    x_bf = _after(token, x[0]).astype(BF16)
    wts.update(fetch(0, x_bf))

    def start_swap(group, grads):
        names, blocks = [], []
        for n, g in grads.items():
            if n == "pool_w":
                g = g.reshape(groups, N_SHARD, cs, -1).transpose(1, 0, 2, 3).reshape(N_SHARD, groups * cs, -1)
            elif g.ndim == 2:
                g = g.reshape(N_SHARD, -1, g.shape[-1])
            names.append({"w_in_t": "w_in", "w_up3": "w_up"}.get(n, n))
            blocks.append(g)
        zones = [((N_SHARD,) + _half_shape(b.shape[1], b.shape[2]), F32) for b in blocks]
        swap = _exchange_start(f"grad_swap_start_{group}", _swap_copies, blocks, zones, N_SHARD)
        return {"group": group, "names": names, "swap": swap, "token": swap[3]}

    def start_send(state, after):
        group, names = state["group"], state["names"]
        state["blocks"] = state["swap"][1]
        state["others"] = _exchange_wait(f"grad_swap_wait_{group}", _swap_copies, state["swap"], after)
        partials = [_chip_partial("chip_partial_" + n, gb, ob, core)
                    for n, gb, ob in zip(names, state["blocks"], state["others"])]
        zones = [((3,) + p.shape[1:], BF16) for p in partials]
        state["send"] = _exchange_start(f"grad_send_start_{group}", _partial_copies, partials, zones, 3)
        state["token"] = state["send"][3]

    grad, delta, new_m, new_v = {}, {}, {}, {}

    def start_join(state, after):
        group, names = state["group"], state["names"]
        received = _exchange_wait(f"grad_send_wait_{group}", _partial_copies, state["send"], after)
        halves = [_reduce_own("reduce_own_" + n, gb, ob, rb, where)
                  for n, gb, ob, rb in zip(names, state["blocks"], state["others"], received)]
        state["join"] = _join_start(f"grad_join_start_{group}", halves)
        return state["join"][2]

    def finish_reduce(state, after):
        group, names = state["group"], state["names"]
        for n, g in zip(names, _join_wait(f"grad_join_wait_{group}", state["join"], after)):
            shp = given[n].shape
            d2, m2, v2, g2 = _adamw("adamw_" + n, _as2d(n, given[n]), g, _as2d(n, given["m_" + n]),
                                    _as2d(n, given["v_" + n]))
            grad[n], delta[n], new_m[n], new_v[n] = (_from2d(n, a, shp) for a in (g2, d2, m2, v2))
        return d2

    step = _local_step(x[0], mem[0], loss_target[0], wts, x_bf)
    pending = {}
    request = next(step)
    while True:
        try:
            kind, group, payload = request
            if kind == "weights":
                request = step.send(fetch(group, payload))
            elif kind == "relay":
                request = step.send(send_on(payload))
            elif kind == "pass":
                request = step.send(pass_on(group, payload))
            elif kind == "grads":
                pending[group] = start_swap(group, payload)
                request = step.send(pending[group]["token"])
            else:
                start_send(pending[group], [payload])
                request = step.send(pending[group]["token"])
        except StopIteration as stop:
            loss_row, grad_x, g = stop.value
            break

    after = [pending[2]["token"], grad_x]
    for group in (0, 1):
        after = [start_join(pending[group], after)]
    for group in (0, 1):
        after = [finish_reduce(pending[group], after)]
    after = [finish_reduce(pending[2], [start_join(pending[2], after)])]

    small_names = ("a_log", "dt_bias", "gdn_norm_w", "pool_scale", "ln1_g", "ln1_b", "ln2_g", "ln2_b", "ln3_g", "ln3_b")
    pieces = [g["conv_w"]] + [g[n] for n in small_names] + [loss_row[:, :1]]
    shapes = [p.shape for p in pieces]
    summed = _unpack(_all_reduce_small("all_reduce_small", _pack(pieces), after[0]), shapes)
    gsmall = dict(zip(small_names, summed[1:-1]))
    gsmall["conv_w"] = lax.dynamic_slice(summed[0], (0, me * conv_cols), (kk, conv_cols))
    loss = summed[-1][0, 0]

    sshapes = [given[n].shape for n in SMALL]
    slabs = [_pack([given[p + n] for n in SMALL]) for p in ("", "m_", "v_")]
    gslab = _pack([gsmall[n] for n in SMALL])
    outs = _adamw("adamw_small", slabs[0], gslab, slabs[1], slabs[2])[:3]
    for dst, slab in zip((delta, new_m, new_v), outs):
        dst.update(zip(SMALL, _unpack(slab, sshapes)))
    for n in SMALL:
        grad[n] = gsmall[n].reshape(given[n].shape)

    return (loss, grad_x[None], *[grad[n] for n in ORDER], *[delta[n] for n in ORDER],
            *[new_m[n] for n in ORDER], *[new_v[n] for n in ORDER])
```

```python
import math

import jax
import jax.numpy as jnp
from jax import lax
from jax.experimental import pallas as pl
from jax.experimental.pallas import tpu as pltpu

F32 = jnp.float32
BF16 = jnp.bfloat16
MESH = pl.DeviceIdType.MESH

HEAD_DIM = 128
CHUNK = 64
POOL_WINDOWS = (2, 4, 8, 16)
XATTN_HEADS = 4
ALPHA = 2.0 ** 0.25
LN_EPS = 1e-5
NORM_EPS = 1e-6
ADAM_LR, ADAM_B1, ADAM_B2, ADAM_EPS, ADAM_WD, ADAM_STEP = 0.001, 0.9, 0.999, 1e-08, 0.01, 10
N_SHARD = 4
VMEM_LIMIT = 56 * 1024 * 1024
K_STEPS = (2048, 1024, 512, 256, 128)


def _params(*sem):
    return pltpu.CompilerParams(dimension_semantics=sem, vmem_limit_bytes=VMEM_LIMIT)


def _bdot(a, b, ta=False, tb=False):
    dims = (((0 if ta else 1,), (1 if tb else 0,)), ((), ()))
    return lax.dot_general(a.astype(BF16), b.astype(BF16), dims, preferred_element_type=F32)


def _sigmoid(x):
    return 1.0 / (1.0 + jnp.exp(-x))


def _matmul(name, a, b, *, ta=False, tb=False, tm, tn, tk, extra=(), outs, epilogue, b_blocks=None,
            sequential=False, n_used=None, n_outer=False):
    m, k_dim = (a.shape[1], a.shape[0]) if ta else a.shape
    if b_blocks and tb:
        n = b.shape[1]
        k_dim = b.shape[0] * b.shape[2]
        per = b.shape[2] // tk
        b_spec = pl.BlockSpec((None, tn, tk), lambda i, j, k: (k // per, j, k % per))
    elif b_blocks:
        n = b.shape[0] * b.shape[2]
        per = b.shape[2] // tn
        b_spec = pl.BlockSpec((None, tk, tn), lambda i, j, k: (j // per, k, j % per))
    elif tb:
        n = b.shape[0]
        b_spec = pl.BlockSpec((tn, tk), lambda i, j, k: (j, k))
    else:
        n = b.shape[1]
        b_spec = pl.BlockSpec((tk, tn), lambda i, j, k: (k, j))
    n = n_used or n
    assert m % tm == 0 and n % tn == 0 and k_dim % tk == 0, (name, m, n, k_dim, tm, tn, tk)
    nk = k_dim // tk
    a_spec = pl.BlockSpec((tk, tm), lambda i, j, k: (k, i)) if ta else pl.BlockSpec((tm, tk), lambda i, j, k: (i, k))
    n_extra, n_out = len(extra), len(outs)

    def wrap(index_map):
        return lambda i, j, k: index_map(i, j)

    def spec(block, index_map):
        if n_outer:
            return pl.BlockSpec(block, lambda j, i, k: index_map(i, j, k))
        return pl.BlockSpec(block, index_map)

    row_axis = 1 if n_outer else 0

    def body_one_step(*refs):
        ex = refs[2:2 + n_extra]
        out = refs[2 + n_extra:2 + n_extra + n_out]
        epilogue(_bdot(refs[0][...], refs[1][...], ta, tb), ex, out, pl.program_id(row_axis))

    def body(*refs):
        a_ref, b_ref = refs[0], refs[1]
        ex = refs[2:2 + n_extra]
        out = refs[2 + n_extra:2 + n_extra + n_out]
        acc = refs[-1]
        i, k = pl.program_id(row_axis), pl.program_id(2)
        part = _bdot(a_ref[...], b_ref[...], ta, tb)

        @pl.when(k == 0)
        def _():
            acc[...] = part

        @pl.when(jnp.logical_and(k > 0, k < nk - 1))
        def _():
            acc[...] += part

        @pl.when(k == nk - 1)
        def _():
            epilogue(acc[...] + part, ex, out, i)

    sem = ("arbitrary",) * 3 if sequential else ("parallel", "parallel", "arbitrary")
    res = pl.pallas_call(
        body_one_step if nk == 1 else body, name=name,
        grid=(n // tn, m // tm, nk) if n_outer else (m // tm, n // tn, nk),
        in_specs=[spec(a_spec.block_shape, a_spec.index_map), spec(b_spec.block_shape, b_spec.index_map)]
        + [spec(bs, wrap(im)) for _, bs, im in extra],
        out_specs=[spec(bs, wrap(im)) for _, bs, im in outs],
        out_shape=[s for s, _, _ in outs],
        scratch_shapes=[] if nk == 1 else [pltpu.VMEM((tm, tn), F32)],
        compiler_params=_params(*sem),
    )(a, b, *[x for x, _, _ in extra])
    return res


def _tile(i, j):
    return (i, j)


def _plain(name, a, b, *, ta=False, tb=False, tm, tn, tk, out_dtype, b_blocks=None, out3=None, n_used=None,
           n_outer=False, m_kept=None):
    m = a.shape[1] if ta else a.shape[0]
    if b_blocks:
        n = b.shape[1] if tb else b.shape[0] * b.shape[2]
    else:
        n = n_used or (b.shape[0] if tb else b.shape[1])

    def epi(acc, ex, out, i):
        out[0][...] = acc.astype(out_dtype)

    if out3:
        per = (n // out3) // tn
        spec = (jax.ShapeDtypeStruct((out3, m, n // out3), out_dtype), (None, tm, tn),
                lambda i, j: (j // per, i, j % per))
    else:
        spec = (jax.ShapeDtypeStruct((m_kept or m, n), out_dtype), (tm, tn), _tile)
    return _matmul(name, a, b, ta=ta, tb=tb, tm=tm, tn=tn, tk=tk, outs=[spec], epilogue=epi,
                   b_blocks=b_blocks, n_used=n_used, n_outer=n_outer)[0]


def _ln_forward(name, a, b, res, gamma, beta, *, tm, tk, want_h=True):
    m, n = res.shape

    def epi(acc, ex, out, i):
        u = ALPHA * ex[0][...] + acc
        mu = jnp.mean(u, axis=-1, keepdims=True)
        xc = u - mu
        var = jnp.mean(xc * xc, axis=-1, keepdims=True)
        rstd = lax.rsqrt(var + LN_EPS)
        xhat = xc * rstd
        out[-2][...] = xhat
        out[-1][...] = rstd
        if want_h:
            h = xhat * ex[1][...] + ex[2][...]
            out[0][...] = h
            out[1][...] = h.astype(BF16)

    row = lambda i, j: (i, 0)
    vec = lambda i, j: (0, 0)
    outs = [(jax.ShapeDtypeStruct((m, n), F32), (tm, n), row), (jax.ShapeDtypeStruct((m, n), BF16), (tm, n), row),
            (jax.ShapeDtypeStruct((m, n), F32), (tm, n), row), (jax.ShapeDtypeStruct((m, 1), F32), (tm, 1), row)]
    return _matmul(
        name, a, b, tm=tm, tn=n, tk=tk,
        extra=[(res, (tm, n), row), (gamma, (1, n), vec), (beta, (1, n), vec)],
        outs=outs if want_h else outs[2:], epilogue=epi)


def _ln_backward_math(dy, xhat, rstd, gamma):
    dxhat = dy * gamma
    m1 = jnp.mean(dxhat, axis=-1, keepdims=True)
    m2 = jnp.mean(dxhat * xhat, axis=-1, keepdims=True)
    du = rstd * (dxhat - m1 - xhat * m2)
    return du, jnp.sum(dy * xhat, axis=0, keepdims=True), jnp.sum(dy, axis=0, keepdims=True)


def _ln_backward(name, a, b, dres, xhat, rstd, gamma, *, tm, tk, b_blocks=None, tb=True):
    m, n = dres.shape

    def epi(acc, ex, out, i):
        dy = acc + ALPHA * ex[0][...]
        du, dg, db = _ln_backward_math(dy, ex[1][...], ex[2][...], ex[3][...])
        out[0][...] = du
        out[1][...] = du.astype(BF16)
        first = i == 0

        @pl.when(first)
        def _():
            out[2][...] = dg
            out[3][...] = db

        @pl.when(jnp.logical_not(first))
        def _():
            out[2][...] += dg
            out[3][...] += db

    row = lambda i, j: (i, 0)
    vec = lambda i, j: (0, 0)
    return _matmul(
        name, a, b, tb=tb, tm=tm, tn=n, tk=tk, b_blocks=b_blocks, sequential=True,
        extra=[(dres, (tm, n), row), (xhat, (tm, n), row), (rstd, (tm, 1), row), (gamma, (1, n), vec)],
        outs=[(jax.ShapeDtypeStruct((m, n), F32), (tm, n), row),
              (jax.ShapeDtypeStruct((m, n), BF16), (tm, n), row),
              (jax.ShapeDtypeStruct((1, n), F32), (1, n), vec),
              (jax.ShapeDtypeStruct((1, n), F32), (1, n), vec)],
        epilogue=epi)


def _shift_down(x, k):
    row = lax.broadcasted_iota(jnp.int32, x.shape, 0)
    return jnp.where(row >= k, pltpu.roll(x, k, axis=0), 0.0)


def _shift_up(x, k):
    t = x.shape[0]
    row = lax.broadcasted_iota(jnp.int32, x.shape, 0)
    return jnp.where(row < t - k, pltpu.roll(x, t - k, axis=0), 0.0)


def _conv_silu_norm(x, w, normalise):
    kk = w.shape[0]
    c = x * w[kk - 1:kk, :]
    for j in range(kk - 1):
        c = c + _shift_down(x, kk - 1 - j) * w[j:j + 1, :]
    sg = _sigmoid(c)
    s = c * sg
    r = lax.rsqrt(jnp.sum(s * s, axis=-1, keepdims=True) + NORM_EPS)
    y = jnp.where(normalise, s * r, s)
    return c, sg, s, r, y


def _gdn_pre(proj, conv_w, heads):
    t = proj.shape[0]
    kk = conv_w.shape[0]

    def body(x_ref, w_ref, o_ref):
        normalise = pl.program_id(0) < 2
        o_ref[...] = _conv_silu_norm(x_ref[...], w_ref[...], normalise)[4]

    col = lambda s, h: (0, s * heads + h)
    return pl.pallas_call(
        body, name="gdn_pre", grid=(3, heads),
        in_specs=[pl.BlockSpec((t, HEAD_DIM), col), pl.BlockSpec((kk, HEAD_DIM), col)],
        out_specs=pl.BlockSpec((t, HEAD_DIM), col),
        out_shape=jax.ShapeDtypeStruct((t, 3 * heads * HEAD_DIM), F32),
        compiler_params=_params("parallel", "parallel"),
    )(proj, conv_w)


def _gdn_pre_backward(proj, conv_w, dqkv, heads):
    t = proj.shape[0]
    kk = conv_w.shape[0]

    def body(x_ref, w_ref, dy_ref, dx_ref, dw_ref):
        normalise = pl.program_id(0) < 2
        x = x_ref[...]
        w = w_ref[...]
        dy = dy_ref[...]
        c, sg, s, r, y = _conv_silu_norm(x, w, normalise)
        ds_norm = r * (dy - y * jnp.sum(dy * y, axis=-1, keepdims=True))
        ds = jnp.where(normalise, ds_norm, dy)
        dc = ds * (sg * (1.0 + c * (1.0 - sg)))
        dx = dc * w[kk - 1:kk, :]
        rows = [None] * kk
        rows[kk - 1] = jnp.sum(dc * x, axis=0, keepdims=True)
        for j in range(kk - 1):
            lag = kk - 1 - j
            dx = dx + _shift_up(dc, lag) * w[j:j + 1, :]
            rows[j] = jnp.sum(dc * _shift_down(x, lag), axis=0, keepdims=True)
        dx_ref[...] = dx.astype(BF16)
        dw_ref[...] = jnp.concatenate(rows, axis=0)

    col = lambda s, h: (0, s * heads + h)
    return pl.pallas_call(
        body, name="gdn_pre_bwd", grid=(3, heads),
        in_specs=[pl.BlockSpec((t, HEAD_DIM), col), pl.BlockSpec((kk, HEAD_DIM), col),
                  pl.BlockSpec((t, HEAD_DIM), col)],
        out_specs=[pl.BlockSpec((t, HEAD_DIM), col), pl.BlockSpec((kk, HEAD_DIM), col)],
        out_shape=[jax.ShapeDtypeStruct((t, 3 * heads * HEAD_DIM), BF16),
                   jax.ShapeDtypeStruct((kk, 3 * heads * HEAD_DIM), F32)],
        compiler_params=_params("parallel", "parallel"),
    )(proj, conv_w, dqkv)


def _gate_vectors(a_log, dt_bias, heads):
    pad = lambda v: jnp.pad(v.astype(F32), ((0, 0), (heads, HEAD_DIM - 2 * heads)))
    return pad(jnp.exp(a_log.astype(F32))), pad(dt_bias)


def _softplus(x):
    return jnp.maximum(x, 0.0) + jnp.log(1.0 + jnp.exp(-jnp.abs(x)))


def _gates_epilogue(heads):
    def epi(acc, ex, out, i):
        lane = lax.broadcasted_iota(jnp.int32, acc.shape, 1)
        beta = _sigmoid(acc)
        g = -ex[0][...] * _softplus(acc + ex[1][...])
        out[0][...] = acc
        out[1][...] = jnp.where(lane < heads, beta, jnp.where(lane < 2 * heads, g, 0.0))
    return epi


def _gates_backward(ba, bg, dbg, ea, dtb, heads):
    t = ba.shape[0]

    def body(ba_ref, bg_ref, d_ref, ea_ref, dt_ref, dba_ref, dal_ref, ddt_ref):
        lane = lax.broadcasted_iota(jnp.int32, (t, HEAD_DIM), 1)
        bgv = bg_ref[...]
        d = d_ref[...]
        db = d * bgv * (1.0 - bgv)
        da = -d * ea_ref[...] * _sigmoid(ba_ref[...] + dt_ref[...])
        is_g = jnp.logical_and(lane >= heads, lane < 2 * heads)
        dba = jnp.where(lane < heads, db, jnp.where(is_g, da, 0.0))
        dba_ref[...] = dba.astype(BF16)
        dal_ref[...] = jnp.sum(jnp.where(is_g, d * bgv, 0.0), axis=0, keepdims=True)
        ddt_ref[...] = jnp.sum(jnp.where(is_g, da, 0.0), axis=0, keepdims=True)

    full = pl.BlockSpec((t, HEAD_DIM), lambda: (0, 0))
    vec = pl.BlockSpec((1, HEAD_DIM), lambda: (0, 0))
    return pl.pallas_call(
        body, name="gates_bwd", grid=(),
        in_specs=[full, full, full, vec, vec], out_specs=[full, vec, vec],
        out_shape=[jax.ShapeDtypeStruct((t, HEAD_DIM), BF16), jax.ShapeDtypeStruct((1, HEAD_DIM), F32),
                   jax.ShapeDtypeStruct((1, HEAD_DIM), F32)],
        compiler_params=pltpu.CompilerParams(vmem_limit_bytes=VMEM_LIMIT),
    )(ba, bg, dbg, ea, dtb)


class _Chunk:
    pass


def _split2(x):
    hi = x.astype(BF16)
    return hi, (x - hi.astype(F32)).astype(BF16)


def _split3(x):
    hi = x.astype(BF16)
    rest = x - hi.astype(F32)
    mid = rest.astype(BF16)
    return hi, mid, (rest - mid.astype(F32)).astype(BF16)


def _dot_mask(mask, x, ta=False):
    hi, mid, lo = _split3(x)
    return _bdot(mask, hi, ta=ta) + (_bdot(mask, mid, ta=ta) + _bdot(mask, lo, ta=ta))


def _transpose_by_identity(x):
    r = x.shape[0]
    eye = (lax.broadcasted_iota(jnp.int32, (r, r), 0) == lax.broadcasted_iota(jnp.int32, (r, r), 1)).astype(BF16)
    hi, mid, lo = _split3(x)
    return _bdot(hi, eye, ta=True) + (_bdot(mid, eye, ta=True) + _bdot(lo, eye, ta=True))


def _dot22(a, b, ta=False, tb=False):
    ah, al = _split2(a)
    bh, bl = _split2(b)
    return _bdot(ah, bh, ta, tb) + (_bdot(ah, bl, ta, tb) + _bdot(al, bh, ta, tb))


def _chunk_gates(bg, heads):
    n = CHUNK
    row = lax.broadcasted_iota(jnp.int32, (n, n), 0)
    col = lax.broadcasted_iota(jnp.int32, (n, n), 1)
    lane = lax.broadcasted_iota(jnp.int32, bg.shape, 1)
    graw = jnp.where(jnp.logical_and(lane >= heads, lane < 2 * heads), bg, 0.0)
    gc = _dot_mask((row >= col).astype(BF16), graw)
    return gc, _transpose_by_identity(gc)


def _in_lockstep(generators):
    results = [None] * len(generators)
    live = list(enumerate(generators))
    while live:
        still = []
        for i, gen in live:
            try:
                next(gen)
                still.append((i, gen))
            except StopIteration as stop:
                results[i] = stop.value
        live = still
    return results


def _chunk_local(q, k, v, beta, gc, grow, solved=None):
    c = _Chunk()
    n = CHUNK
    row = lax.broadcasted_iota(jnp.int32, (n, n), 0)
    col = lax.broadcasted_iota(jnp.int32, (n, n), 1)
    c.tri = row >= col
    c.strict = row > col
    eye = row == col
    c.gcb = jnp.broadcast_to(gc, (n, HEAD_DIM))
    c.decay = jnp.where(c.tri, jnp.exp(jnp.where(c.tri, gc - grow, 0.0)), 0.0)
    c.eg = jnp.exp(c.gcb)
    glast = c.gcb[n - 1:n, :]
    c.egl = jnp.exp(glast)
    c.ekl = jnp.exp(glast - c.gcb)
    c.beta = beta
    c.q = q * (HEAD_DIM ** -0.5)
    c.k = k
    c.v = v
    c.kb = k * beta
    c.vb = v * beta
    c.kg = c.kb * c.eg
    both = _bdot(jnp.concatenate([c.kb, c.q], axis=0), k, tb=True)
    yield
    c.L = jnp.where(c.strict, both[:n] * c.decay, 0.0)
    c.A = jnp.where(c.tri, both[n:] * c.decay, 0.0)
    if solved is None:
        x = -c.L
        tinv = eye.astype(F32) + x
        p = _dot22(x, x)
        yield
        for _ in range(int(math.log2(n)) - 2):
            both = _dot22(jnp.concatenate([p, tinv], axis=0), p)
            yield
            p, tinv = both[:n], tinv + both[n:]
        c.T = tinv + _dot22(tinv, p)
        yield
        uw = _dot22(c.T, jnp.concatenate([c.vb, c.kg], axis=1))
        yield
        c.u, c.w = uw[:, :HEAD_DIM], uw[:, HEAD_DIM:]
    else:
        c.T, c.u, c.w = solved
    c.qg = c.q * c.eg
    c.kdec = k * c.ekl
    return c


def _gdn_core(qkv, bg, heads):
    t = qkv.shape[0]
    nchunk = t // CHUNK

    gw = heads * HEAD_DIM

    def body(qkv_ref, bg_ref, o_ref, s_ref, t_ref, u_ref, w_ref, state):
        @pl.when(pl.program_id(0) == 0)
        def _():
            state[...] = jnp.zeros_like(state)

        bg_v = bg_ref[...]
        gc_all, gc_rows = _chunk_gates(bg_v, heads)
        def one_head(h):
            col = lambda s: pl.ds(s * gw + h * HEAD_DIM, HEAD_DIM)
            c = yield from _chunk_local(qkv_ref[:, col(0)], qkv_ref[:, col(1)], qkv_ref[:, col(2)], bg_v[:, h:h + 1],
                                        gc_all[:, heads + h:heads + h + 1], gc_rows[heads + h:heads + h + 1, :])
            s0 = state[h]
            v_new = c.u - _bdot(c.w, s0)
            yield
            o = _bdot(c.qg, s0) + _bdot(c.A, v_new)
            return s0, o, s0 * c.egl + _bdot(c.kdec, v_new, ta=True), c

        results = _in_lockstep([one_head(h) for h in range(heads)])
        for h, (s0, o, s1, c) in enumerate(results):
            lanes = pl.ds(h * HEAD_DIM, HEAD_DIM)
            s_ref[h, 0] = s0
            o_ref[:, lanes] = o
            t_ref[:, lanes] = jnp.concatenate([c.T, jnp.zeros((CHUNK, HEAD_DIM - CHUNK), F32)], axis=1)
            u_ref[:, lanes] = c.u
            w_ref[:, lanes] = c.w
            state[h] = s1

    return pl.pallas_call(
        body, name="gdn_core", grid=(nchunk,),
        in_specs=[pl.BlockSpec((CHUNK, 3 * gw), lambda n: (n, 0)), pl.BlockSpec((CHUNK, HEAD_DIM), lambda n: (n, 0))],
        out_specs=[pl.BlockSpec((CHUNK, gw), lambda n: (n, 0)),
                   pl.BlockSpec((heads, 1, HEAD_DIM, HEAD_DIM), lambda n: (0, n, 0, 0))]
        + [pl.BlockSpec((CHUNK, gw), lambda n: (n, 0))] * 3,
        out_shape=[jax.ShapeDtypeStruct((t, gw), F32),
                   jax.ShapeDtypeStruct((heads, nchunk, HEAD_DIM, HEAD_DIM), F32)]
        + [jax.ShapeDtypeStruct((t, gw), F32)] * 3,
        scratch_shapes=[pltpu.VMEM((heads, HEAD_DIM, HEAD_DIM), F32)],
        compiler_params=_params("arbitrary"),
    )(qkv, bg)


def _gdn_core_backward(qkv, bg, states, solved, do, heads):
    t = qkv.shape[0]
    nchunk = t // CHUNK
    n = CHUNK

    def one_head(chunk_local, s0, d_out, ds1):
        c = yield from chunk_local
        v_new = c.u - _bdot(c.w, s0)
        dqg = _bdot(d_out, s0, tb=True)
        ds0 = _bdot(c.qg, d_out, ta=True) + ds1 * c.egl
        dv_new = _bdot(c.A, d_out, ta=True) + _bdot(c.kdec, ds1)
        yield
        dA = jnp.where(c.tri, _bdot(d_out, v_new, tb=True), 0.0)
        dkdec = _bdot(v_new, ds1, tb=True)
        dgl = jnp.sum(jnp.sum(ds1 * s0, axis=1, keepdims=True), axis=0, keepdims=True) * c.egl
        dw = -_bdot(dv_new, s0, tb=True)
        ds0 = ds0 - _bdot(c.w, dv_new, ta=True)
        yield
        both = _dot22(c.T, jnp.concatenate([dv_new, dw], axis=1), ta=True)
        yield
        dvb, dkg = both[:, :HEAD_DIM], both[:, HEAD_DIM:]
        dL = jnp.where(c.strict, -(_bdot(dvb, c.u, tb=True) + _bdot(dkg, c.w, tb=True)), 0.0)
        yield
        dm1 = dL * c.decay
        dkb = _bdot(dm1, c.k) + dkg * c.eg
        dk = _bdot(dm1, c.kb, ta=True)
        dm2 = dA * c.decay
        dq = _bdot(dm2, c.k) + dqg * c.eg
        dk = dk + _bdot(dm2, c.q, ta=True) + dkdec * c.ekl + dkb * c.beta
        pm = dL * c.L + dA * c.A
        ones = jnp.ones((n, HEAD_DIM), BF16)
        pm_hi, pm_lo = _split2(pm)
        colsum = _bdot(pm_hi, ones, ta=True) + _bdot(pm_lo, ones, ta=True)
        tk_ = jnp.sum(dkdec * c.kdec, axis=1, keepdims=True)
        dgc = (jnp.sum(pm, axis=1, keepdims=True) - colsum
               + jnp.sum(dqg * c.qg, axis=1, keepdims=True)
               - tk_
               + jnp.sum(dkg * c.kg, axis=1, keepdims=True))
        dgl = dgl + jnp.sum(tk_, axis=0, keepdims=True)
        rowi = lax.broadcasted_iota(jnp.int32, (n, HEAD_DIM), 0)
        dgc = dgc + jnp.where(rowi == n - 1, dgl, 0.0)
        dbeta = jnp.sum(dkb * c.k, axis=1, keepdims=True) + jnp.sum(dvb * c.v, axis=1, keepdims=True)
        return dq * (HEAD_DIM ** -0.5), dk, dvb * c.beta, dbeta, dgc, ds0

    gw = heads * HEAD_DIM

    def body(qkv_ref, bg_ref, s_ref, t_ref, u_ref, w_ref, do_ref, dqkv_ref, dbg_ref, dstate):
        @pl.when(pl.program_id(0) == 0)
        def _():
            dstate[...] = jnp.zeros_like(dstate)

        bg_v = bg_ref[...]
        gc_all, gc_rows = _chunk_gates(bg_v, heads)
        lane = lax.broadcasted_iota(jnp.int32, (n, HEAD_DIM), 1)
        dgates = jnp.zeros((n, HEAD_DIM), F32)
        chains = []
        for h in range(heads):
            col = lambda s: pl.ds(s * gw + h * HEAD_DIM, HEAD_DIM)
            lanes = pl.ds(h * HEAD_DIM, HEAD_DIM)
            c = _chunk_local(qkv_ref[:, col(0)], qkv_ref[:, col(1)], qkv_ref[:, col(2)], bg_v[:, h:h + 1],
                             gc_all[:, heads + h:heads + h + 1], gc_rows[heads + h:heads + h + 1, :],
                             (t_ref[:, pl.ds(h * HEAD_DIM, CHUNK)], u_ref[:, lanes], w_ref[:, lanes]))
            chains.append(one_head(c, s_ref[h, 0], do_ref[:, pl.ds(h * HEAD_DIM, HEAD_DIM)], dstate[h]))
        results = _in_lockstep(chains)
        for h, (dq, dk, dv, dbeta, dgc, ds0) in enumerate(results):
            dgates = jnp.where(lane == h, dbeta, jnp.where(lane == heads + h, dgc, dgates))
        for h, (dq, dk, dv, dbeta, dgc, ds0) in enumerate(results):
            dqkv_ref[:, pl.ds(h * HEAD_DIM, HEAD_DIM)] = dq
            dqkv_ref[:, pl.ds(gw + h * HEAD_DIM, HEAD_DIM)] = dk
            dqkv_ref[:, pl.ds(2 * gw + h * HEAD_DIM, HEAD_DIM)] = dv
            dstate[h] = ds0
        row = lax.broadcasted_iota(jnp.int32, (n, n), 0)
        colm = lax.broadcasted_iota(jnp.int32, (n, n), 1)
        draw = _dot_mask((row >= colm).astype(BF16), dgates, ta=True)
        dbg_ref[...] = jnp.where(lane < heads, dgates, draw)

    last = nchunk - 1
    return pl.pallas_call(
        body, name="gdn_core_bwd", grid=(nchunk,),
        in_specs=[pl.BlockSpec((CHUNK, 3 * gw), lambda i: (last - i, 0)),
                  pl.BlockSpec((CHUNK, HEAD_DIM), lambda i: (last - i, 0)),
                  pl.BlockSpec((heads, 1, HEAD_DIM, HEAD_DIM), lambda i: (0, last - i, 0, 0))]
        + [pl.BlockSpec((CHUNK, gw), lambda i: (last - i, 0))] * 4,
        out_specs=[pl.BlockSpec((CHUNK, 3 * gw), lambda i: (last - i, 0)),
                   pl.BlockSpec((CHUNK, HEAD_DIM), lambda i: (last - i, 0))],
        out_shape=[jax.ShapeDtypeStruct((t, 3 * gw), F32), jax.ShapeDtypeStruct((t, HEAD_DIM), F32)],
        scratch_shapes=[pltpu.VMEM((heads, HEAD_DIM, HEAD_DIM), F32)],
        compiler_params=_params("arbitrary"),
    )(qkv, bg, states, *solved, do)


def _gdn_post(o, proj, z_col0, norm_w, heads, tt):
    t = o.shape[0]
    zb = z_col0 // HEAD_DIM

    def body(o_ref, z_ref, w_ref, out_ref):
        ov = o_ref[...]
        z = z_ref[...]
        rms = lax.rsqrt(jnp.mean(ov * ov, axis=-1, keepdims=True) + NORM_EPS)
        out_ref[...] = (ov * rms * w_ref[...] * (z * _sigmoid(z))).astype(BF16)

    return pl.pallas_call(
        body, name="gdn_post", grid=(t // tt, heads),
        in_specs=[pl.BlockSpec((tt, HEAD_DIM), lambda i, h: (i, h)),
                  pl.BlockSpec((tt, HEAD_DIM), lambda i, h: (i, zb + h)),
                  pl.BlockSpec((1, HEAD_DIM), lambda i, h: (0, 0))],
        out_specs=pl.BlockSpec((tt, HEAD_DIM), lambda i, h: (i, h)),
        out_shape=jax.ShapeDtypeStruct((t, heads * HEAD_DIM), BF16),
        compiler_params=_params("parallel", "parallel"),
    )(o, proj, norm_w)


def _gdn_post_backward(dcat, o, proj, z_col0, norm_w, heads, tt):
    t = o.shape[0]
    zb = z_col0 // HEAD_DIM

    def body(d_ref, o_ref, z_ref, w_ref, do_ref, dz_ref, dw_ref):
        d = d_ref[...]
        ov = o_ref[...]
        z = z_ref[...]
        w = w_ref[...]
        rms = lax.rsqrt(jnp.mean(ov * ov, axis=-1, keepdims=True) + NORM_EPS)
        ohat = ov * rms
        sg = _sigmoid(z)
        gate = z * sg
        dz_ref[...] = (d * ohat * w * (sg * (1.0 + z * (1.0 - sg)))).astype(BF16)
        don = d * gate
        dohat = don * w
        do_ref[...] = rms * (dohat - ohat * jnp.mean(dohat * ohat, axis=-1, keepdims=True))
        dw = jnp.sum(don * ohat, axis=0, keepdims=True)
        first = jnp.logical_and(pl.program_id(0) == 0, pl.program_id(1) == 0)

        @pl.when(first)
        def _():
            dw_ref[...] = dw

        @pl.when(jnp.logical_not(first))
        def _():
            dw_ref[...] += dw

    blk = pl.BlockSpec((tt, HEAD_DIM), lambda i, h: (i, h))
    return pl.pallas_call(
        body, name="gdn_post_bwd", grid=(t // tt, heads),
        in_specs=[blk, blk, pl.BlockSpec((tt, HEAD_DIM), lambda i, h: (i, zb + h)),
                  pl.BlockSpec((1, HEAD_DIM), lambda i, h: (0, 0))],
        out_specs=[blk, blk, pl.BlockSpec((1, HEAD_DIM), lambda i, h: (0, 0))],
        out_shape=[jax.ShapeDtypeStruct((t, heads * HEAD_DIM), F32),
                   jax.ShapeDtypeStruct((t, heads * HEAD_DIM), BF16),
                   jax.ShapeDtypeStruct((1, HEAD_DIM), F32)],
        compiler_params=_params("arbitrary", "arbitrary"),
    )(dcat, o, proj, norm_w)


def _pool_select(levels, group):
    out = levels[-1]
    for gi in range(len(levels) - 2, -1, -1):
        out = jnp.where(group == gi, levels[gi], out)
    return out


def _pool_counts(t, width, group):
    pos = lax.broadcasted_iota(jnp.int32, (t, width), 0)
    win = jnp.left_shift(2, group)
    return jnp.minimum(pos + 1, win).astype(F32)


def _pooled(p, group):
    levels, s, step = [], p, 1
    for _ in POOL_WINDOWS:
        s = s + _shift_down(s, step)
        levels.append(s)
        step *= 2
    cnt = _pool_counts(p.shape[0], p.shape[1], group)
    return _pool_select(levels, group) / cnt - p, cnt


def _pool_forward(proj, p_col0, pool_w, pool_scale):
    t = proj.shape[0]
    groups, cg, _ = pool_w.shape
    pb = p_col0 // cg

    def body(p_ref, w_ref, s_ref, o_ref):
        pooled, _ = _pooled(p_ref[...], pl.program_id(0))
        o_ref[...] = (_bdot(pooled, w_ref[0]) * s_ref[...]).astype(BF16)

    return pl.pallas_call(
        body, name="pool_fwd", grid=(groups,),
        in_specs=[pl.BlockSpec((t, cg), lambda g: (0, pb + g)), pl.BlockSpec((1, cg, cg), lambda g: (g, 0, 0)),
                  pl.BlockSpec((1, cg), lambda g: (0, g))],
        out_specs=pl.BlockSpec((t, cg), lambda g: (0, g)),
        out_shape=jax.ShapeDtypeStruct((t, groups * cg), BF16),
        compiler_params=_params("parallel"),
    )(proj, pool_w, pool_scale)


def _pool_backward(dcat, d_col0, proj, p_col0, pool_w, pool_scale):
    t = proj.shape[0]
    groups, cg, _ = pool_w.shape
    pb = p_col0 // cg
    db = d_col0 // cg

    def body(d_ref, p_ref, w_ref, s_ref, dp_ref, dw_ref, ds_ref):
        group = pl.program_id(0)
        pooled, cnt = _pooled(p_ref[...], group)
        w = w_ref[0]
        d = d_ref[...]
        mixed = _bdot(pooled, w)
        ds_ref[...] = jnp.sum(d * mixed, axis=0, keepdims=True)
        dmixed = d * s_ref[...]
        dw_ref[0] = _bdot(pooled, dmixed, ta=True)
        dpooled = _bdot(dmixed, w, tb=True)
        levels, s, step = [], dpooled / cnt, 1
        for _ in POOL_WINDOWS:
            s = s + _shift_up(s, step)
            levels.append(s)
            step *= 2
        dp_ref[...] = (_pool_select(levels, group) - dpooled).astype(BF16)

    return pl.pallas_call(
        body, name="pool_bwd", grid=(groups,),
        in_specs=[pl.BlockSpec((t, cg), lambda g: (0, db + g)), pl.BlockSpec((t, cg), lambda g: (0, pb + g)),
                  pl.BlockSpec((1, cg, cg), lambda g: (g, 0, 0)), pl.BlockSpec((1, cg), lambda g: (0, g))],
        out_specs=[pl.BlockSpec((t, cg), lambda g: (0, g)), pl.BlockSpec((1, cg, cg), lambda g: (g, 0, 0)),
                   pl.BlockSpec((1, cg), lambda g: (0, g))],
        out_shape=[jax.ShapeDtypeStruct((t, groups * cg), BF16), jax.ShapeDtypeStruct((groups, cg, cg), F32),
                   jax.ShapeDtypeStruct((1, groups * cg), F32)],
        compiler_params=_params("parallel"),
    )(dcat, proj, pool_w, pool_scale)


def _attention(q, k, v, tq):
    t, d = q.shape
    m = k.shape[0]
    dh = d // XATTN_HEADS
    scale = dh ** -0.5

    def body(q_ref, k_ref, v_ref, o_ref):
        s = _bdot(q_ref[...], k_ref[...], tb=True) * scale
        s = s - jnp.max(s, axis=-1, keepdims=True)
        e = jnp.exp(s)
        p = e / jnp.sum(e, axis=-1, keepdims=True)
        o_ref[...] = _bdot(p, v_ref[...]).astype(BF16)

    return pl.pallas_call(
        body, name="xattn_fwd", grid=(XATTN_HEADS, t // tq),
        in_specs=[pl.BlockSpec((tq, dh), lambda h, i: (i, h)), pl.BlockSpec((m, dh), lambda h, i: (0, h)),
                  pl.BlockSpec((m, dh), lambda h, i: (0, h))],
        out_specs=pl.BlockSpec((tq, dh), lambda h, i: (i, h)),
        out_shape=jax.ShapeDtypeStruct((t, d), BF16),
        compiler_params=_params("parallel", "parallel"),
    )(q, k, v)


def _attention_backward(q, k, v, do, tq):
    t, d = q.shape
    m = k.shape[0]
    dh = d // XATTN_HEADS
    scale = dh ** -0.5

    def body(q_ref, k_ref, v_ref, do_ref, dq_ref, dk_ref, dv_ref, dk_acc, dv_acc):
        i = pl.program_id(1)
        qv, kv, vv, dov = q_ref[...], k_ref[...], v_ref[...], do_ref[...]
        s = _bdot(qv, kv, tb=True) * scale
        s = s - jnp.max(s, axis=-1, keepdims=True)
        e = jnp.exp(s)
        p = e / jnp.sum(e, axis=-1, keepdims=True)
        dp = _bdot(dov, vv, tb=True)
        ds = p * (dp - jnp.sum(dp * p, axis=-1, keepdims=True)) * scale
        dq_ref[...] = _bdot(ds, kv).astype(BF16)
        dv_part = _bdot(p, dov, ta=True)
        dk_part = _bdot(ds, qv, ta=True)

        @pl.when(i == 0)
        def _():
            dk_acc[...] = dk_part
            dv_acc[...] = dv_part

        @pl.when(i > 0)
        def _():
            dk_acc[...] += dk_part
            dv_acc[...] += dv_part

        @pl.when(i == pl.num_programs(1) - 1)
        def _():
            dk_ref[...] = dk_acc[...].astype(BF16)
            dv_ref[...] = dv_acc[...].astype(BF16)

    qblk = pl.BlockSpec((tq, dh), lambda h, i: (i, h))
    kblk = pl.BlockSpec((m, dh), lambda h, i: (0, h))
    return pl.pallas_call(
        body, name="xattn_bwd", grid=(XATTN_HEADS, t // tq),
        in_specs=[qblk, kblk, kblk, qblk],
        out_specs=[qblk, kblk, kblk],
        out_shape=[jax.ShapeDtypeStruct((t, d), BF16), jax.ShapeDtypeStruct((m, d), BF16),
                   jax.ShapeDtypeStruct((m, d), BF16)],
        scratch_shapes=[pltpu.VMEM((m, dh), F32), pltpu.VMEM((m, dh), F32)],
        compiler_params=_params("parallel", "arbitrary"),
    )(q, k, v, do)


def _ln_backward_rows(name, dmain, dres, xhat, rstd, gamma, tm):
    t, d = xhat.shape

    def body(m_ref, r_ref, x_ref, s_ref, g_ref, du_ref, dub_ref, dg_ref, db_ref):
        du, dg, db = _ln_backward_math(m_ref[...] + ALPHA * r_ref[...], x_ref[...], s_ref[...], g_ref[...])
        du_ref[...] = du
        dub_ref[...] = du.astype(BF16)
        first = pl.program_id(0) == 0

        @pl.when(first)
        def _():
            dg_ref[...] = dg
            db_ref[...] = db

        @pl.when(jnp.logical_not(first))
        def _():
            dg_ref[...] += dg
            db_ref[...] += db

    row = pl.BlockSpec((tm, d), lambda i: (i, 0))
    vec = pl.BlockSpec((1, d), lambda i: (0, 0))
    return pl.pallas_call(
        body, name=name, grid=(t // tm,),
        in_specs=[row, row, row, pl.BlockSpec((tm, 1), lambda i: (i, 0)), vec],
        out_specs=[row, row, vec, vec],
        out_shape=[jax.ShapeDtypeStruct((t, d), F32), jax.ShapeDtypeStruct((t, d), BF16),
                   jax.ShapeDtypeStruct((1, d), F32), jax.ShapeDtypeStruct((1, d), F32)],
        compiler_params=_params("arbitrary"),
    )(dmain, dres, xhat, rstd, gamma)


def _loss_and_ln_backward(xhat, rstd, gamma, beta, target, tm):
    t, d = xhat.shape

    def body(x_ref, r_ref, g_ref, b_ref, t_ref, du_ref, dub_ref, dg_ref, db_ref, loss_ref):
        xh = x_ref[...]
        g = g_ref[...]
        diff = xh * g + b_ref[...] - t_ref[...]
        part = jnp.sum(jnp.sum(diff * diff, axis=1, keepdims=True), axis=0, keepdims=True) * (0.5 / d)
        dy = diff * (1.0 / d)
        du, dg, db = _ln_backward_math(dy, xh, r_ref[...], g)
        du_ref[...] = du
        dub_ref[...] = du.astype(BF16)
        lossrow = jnp.broadcast_to(part, (1, HEAD_DIM))
        first = pl.program_id(0) == 0

        @pl.when(first)
        def _():
            dg_ref[...] = dg
            db_ref[...] = db
            loss_ref[...] = lossrow

        @pl.when(jnp.logical_not(first))
        def _():
            dg_ref[...] += dg
            db_ref[...] += db
            loss_ref[...] += lossrow

    row = pl.BlockSpec((tm, d), lambda i: (i, 0))
    vec = pl.BlockSpec((1, d), lambda i: (0, 0))
    return pl.pallas_call(
        body, name="loss_ln3_bwd", grid=(t // tm,),
        in_specs=[row, pl.BlockSpec((tm, 1), lambda i: (i, 0)), vec, vec, row],
        out_specs=[row, row, vec, vec, pl.BlockSpec((1, HEAD_DIM), lambda i: (0, 0))],
        out_shape=[jax.ShapeDtypeStruct((t, d), F32), jax.ShapeDtypeStruct((t, d), BF16),
                   jax.ShapeDtypeStruct((1, d), F32), jax.ShapeDtypeStruct((1, d), F32),
                   jax.ShapeDtypeStruct((1, HEAD_DIM), F32)],
        compiler_params=_params("arbitrary"),
    )(xhat, rstd, gamma, beta, target)


def _after(token, a):
    return a if token is None else a + token[:1, :1].astype(a.dtype)


def _pick(n, prefs):
    for p in prefs:
        if n % p == 0:
            return p
    return n


def _local_step(x, mem, target, w, x_bf=None):
    t, d = x.shape
    heads = w["a_log"].shape[1]
    gw = heads * HEAD_DIM
    groups, cg, _ = w["pool_w"].shape
    pw = groups * cg
    n_main = 4 * gw + pw
    in_cols = n_main + 2 * heads
    s_in = w["w_in_t"].shape[0]

    tm = _pick(t, (512, 256, 128))
    tm_ln = _pick(t, (256, 128))
    tm_big = _pick(t, (1024, 512, 256, 128))
    tk = _pick(d, K_STEPS)

    w_in_t = w["w_in_t"].reshape(in_cols, d)
    w_p_t = w_in_t[4 * gw + 2 * heads:]
    w_ba_t = jnp.pad(w_in_t[4 * gw:4 * gw + 2 * heads], ((0, HEAD_DIM - 2 * heads), (0, 0)))
    x_bf = x.astype(BF16) if x_bf is None else x_bf
    mem_bf = mem.astype(BF16)

    tn_d = _pick(d, (1024, 512, 256, 128))
    proj = _plain("proj_main", x_bf, w_in_t, tb=True, n_used=4 * gw, tm=tm_big, tn=_pick(4 * gw, (1024, 512, 256, 128)),
                  tk=tk, out_dtype=F32)
    pproj = _plain("proj_pool", x_bf, w_p_t, tb=True, tm=tm_big, tn=_pick(pw, (1024, 512, 256, 128)), tk=tk, out_dtype=F32)
    ea, dtb = _gate_vectors(w["a_log"], w["dt_bias"], heads)
    vec128 = lambda i, j: (0, 0)
    ba, bg = _matmul(
        "proj_gates", x_bf, w_ba_t, tb=True, tm=tm, tn=HEAD_DIM, tk=tk,
        extra=[(ea, (1, HEAD_DIM), vec128), (dtb, (1, HEAD_DIM), vec128)],
        outs=[(jax.ShapeDtypeStruct((t, HEAD_DIM), F32), (tm, HEAD_DIM), _tile)] * 2,
        epilogue=_gates_epilogue(heads))
    qkv = _gdn_pre(proj, w["conv_w"], heads)
    o_gdn, states, *solved = _gdn_core(qkv, bg, heads)
    cat_g = _gdn_post(o_gdn, proj, 3 * gw, w["gdn_norm_w"], heads, tm)
    token = yield ("pass", 1, cat_g)
    cat_p = _pool_forward(pproj, 0, w["pool_w"], _after(token, w["pool_scale"]))
    cat = jnp.concatenate([cat_g, cat_p], axis=1)
    w = {**w, **(yield ("weights", 1, cat))}
    h1, h1_bf, xhat1, rstd1 = _ln_forward("mix_ln1", cat, w["w_out"], x, w["ln1_g"], w["ln1_b"], tm=tm_ln, tk=tk)

    h1_bf = _after((yield ("relay", None, h1_bf)), h1_bf)
    q = _plain("xattn_q", h1_bf, w["xq_w"], tm=tm, tn=tn_d, tk=tk, out_dtype=BF16)
    mlen = mem.shape[0]
    tm_mem = _pick(mlen, (256, 128))
    k = _plain("xattn_k", mem_bf, w["xk_w"], tm=tm_mem, tn=tn_d, tk=tk, out_dtype=BF16)
    v = _plain("xattn_v", mem_bf, w["xv_w"], tm=tm_mem, tn=tn_d, tk=tk, out_dtype=BF16)
    att = _attention(q, k, v, tm)
    h2, h2_bf, xhat2, rstd2 = _ln_forward("xo_ln2", att, w["xo_w"], h1, w["ln2_g"], w["ln2_b"], tm=tm_ln, tk=tk)

    w = {**w, **(yield ("weights", 2, h2_bf))}
    s_up = w["w_up3"].shape[0]
    ff = s_up * w["w_up3"].shape[2]
    tn_f = _pick(ff // s_up, (1024, 512, 256, 128))

    def up_epi(acc, ex, out, i):
        r = jnp.maximum(acc, 0.0)
        out[0][...] = (r * r).astype(BF16)
        out[1][...] = (2.0 * r).astype(BF16)

    act, act_grad = _matmul(
        "mlp_up", h2_bf, w["w_up3"], b_blocks=s_up, tm=tm_big, tn=tn_f, tk=tk,
        outs=[(jax.ShapeDtypeStruct((t, ff), BF16), (tm_big, tn_f), _tile)] * 2, epilogue=up_epi)
    w = {**w, **(yield ("weights", 3, act))}
    tk_f = _pick(ff, K_STEPS)
    xhat3, rstd3 = _ln_forward("down_ln3", act, w["w_down"], h2, w["ln3_g"], w["ln3_b"], tm=tm, tk=tk_f, want_h=False)

    grads = {}
    du3, du3_bf, grads["ln3_g"], grads["ln3_b"], loss = _loss_and_ln_backward(
        xhat3, rstd3, w["ln3_g"], w["ln3_b"], target, tm_ln)

    def dup_epi(acc, ex, out, i):
        out[0][...] = (acc * ex[0][...].astype(F32)).astype(BF16)

    dup = _matmul(
        "mlp_down_dx", du3_bf, w["w_down"], tb=True, tm=tm_big, tn=tn_f, tk=tk,
        extra=[(act_grad, (tm_big, tn_f), _tile)],
        outs=[(jax.ShapeDtypeStruct((t, ff), BF16), (tm_big, tn_f), _tile)], epilogue=dup_epi)[0]
    tk_t = _pick(t, K_STEPS)
    tm_w = _pick(d, (512, 256, 128))
    grads["w_down"] = _plain("mlp_down_dw", act, du3_bf, ta=True, tm=_pick(ff, (512, 256, 128)), tn=d, tk=tk_t,
                             out_dtype=F32)
    grads["w_up3"] = _plain("mlp_up_dw", h2_bf, dup, ta=True, tm=tm_w, tn=ff // s_up, tk=tk_t, out_dtype=F32, out3=s_up,
                            n_outer=True)
    token = yield ("grads", 0, {n: grads.pop(n) for n in ("w_down", "w_up3")})
    dh2 = _plain("mlp_up_dx", dup, w["w_up3"], tb=True, b_blocks=s_up, tm=tm_big, tn=tn_d,
                 tk=_pick(ff // s_up, K_STEPS), out_dtype=F32)
    du2, du2_bf, grads["ln2_g"], grads["ln2_b"] = _ln_backward_rows(
        "ln2_bwd", dh2, du3, xhat2, rstd2, _after(token, w["ln2_g"]), tm_ln)
    token = yield ("poll", 0, du2_bf)

    grads["xo_w"] = _plain("xo_dw", att, du2_bf, ta=True, tm=tm_w, tn=d, tk=tk_t, out_dtype=F32)
    datt = _plain("xo_dx", du2_bf, w["xo_w"], tb=True, tm=tm, tn=tn_d, tk=tk, out_dtype=BF16)
    dq, dk, dv = _attention_backward(q, k, v, datt, tm)
    tk_m = _pick(mlen, (256, 128))
    grads["xq_w"] = _plain("xq_dw", h1_bf, dq, ta=True, tm=tm_w, tn=d, tk=tk_t, out_dtype=F32)
    grads["xk_w"] = _plain("xk_dw", mem_bf, dk, ta=True, tm=tm_w, tn=tn_d, tk=tk_m, out_dtype=F32)
    grads["xv_w"] = _plain("xv_dw", mem_bf, dv, ta=True, tm=tm_w, tn=tn_d, tk=tk_m, out_dtype=F32)
    du1, du1_bf, grads["ln1_g"], grads["ln1_b"] = _ln_backward(
        "xq_dx_ln1", dq, w["xq_w"], du2, xhat1, rstd1, _after(token, w["ln1_g"]), tm=tm_ln, tk=tk)

    grads["w_out"] = _plain("out_dw", cat, du1_bf, ta=True, tm=tm_w, tn=d, tk=tk_t, out_dtype=F32)
    token = yield ("grads", 1, {n: grads.pop(n) for n in ("xo_w", "xq_w", "xk_w", "xv_w", "w_out")})
    dcat = _plain("out_dx", du1_bf, w["w_out"], tb=True, tm=tm, tn=tn_d, tk=tk, out_dtype=F32)
    dp, grads["pool_w"], grads["pool_scale"] = _pool_backward(dcat, gw, pproj, 0, w["pool_w"],
                                                              _after(token, w["pool_scale"]))
    do_gdn, dz, grads["gdn_norm_w"] = _gdn_post_backward(dcat, o_gdn, proj, 3 * gw, _after(token, w["gdn_norm_w"]),
                                                         heads, tm)
    dqkv, dbg = _gdn_core_backward(qkv, bg, states, solved, do_gdn, heads)
    token = yield ("poll", 1, dqkv)
    dqkv_pre, grads["conv_w"] = _gdn_pre_backward(proj, _after(token, w["conv_w"]), dqkv, heads)
    dba, dalog_row, ddt_row = _gates_backward(ba, bg, dbg, ea, dtb, heads)
    grads["a_log"] = dalog_row[:, heads:2 * heads]
    grads["dt_bias"] = ddt_row[:, heads:2 * heads]

    k_pad = -(-in_cols // HEAD_DIM) * HEAD_DIM
    dproj = jnp.concatenate([dqkv_pre, dz, dba[:, :2 * heads], dp, jnp.zeros((t, k_pad - in_cols), BF16)], axis=1)
    dw_in_t = _plain("proj_dw", dproj, x_bf, ta=True, tm=_pick(k_pad, (512, 256, 128)), tn=d, tk=tk_t, out_dtype=F32,
                     m_kept=in_cols)
    grads["w_in_t"] = dw_in_t.reshape(s_in, in_cols // s_in, d)

    def dx_epi(acc, ex, out, i):
        out[0][...] = acc + ALPHA * ex[0][...]

    token = yield ("grads", 2, {n: grads.pop(n) for n in ("w_in_t", "pool_w")})
    w_in_t_pad = jnp.concatenate([w_in_t, _after(token, jnp.zeros((k_pad - in_cols, d), BF16))], axis=0)
    grad_x = _matmul(
        "proj_dx", dproj, w_in_t_pad, tm=tm, tn=tn_d, tk=k_pad, extra=[(du1, (tm, tn_d), _tile)],
        outs=[(jax.ShapeDtypeStruct((t, d), F32), (tm, tn_d), _tile)], epilogue=dx_epi)[0]
    yield ("poll", 2, grad_x)
    return loss, grad_x, grads


def _adamw(name, w, g, m, v):
    r, c = w.shape
    if r % 8 == 0:
        tr = _pick(r, (256, 128, 64, 32, 16, 8))
        blk, steps = pl.BlockSpec((tr, c), lambda i: (i, 0)), r // tr
    else:
        tc = _pick(c, (256, 128))
        blk, steps = pl.BlockSpec((r, tc), lambda i: (0, i)), c // tc
    c1 = 1.0 - ADAM_B1 ** ADAM_STEP
    c2 = 1.0 - ADAM_B2 ** ADAM_STEP

    def body(w_ref, g_ref, m_ref, v_ref, d_ref, mo_ref, vo_ref, go_ref):
        gv = g_ref[...]
        mn = ADAM_B1 * m_ref[...] + (1.0 - ADAM_B1) * gv
        vn = ADAM_B2 * v_ref[...] + (1.0 - ADAM_B2) * (gv * gv)
        d_ref[...] = -ADAM_LR * ((mn / c1) / (jnp.sqrt(vn / c2) + ADAM_EPS) + ADAM_WD * w_ref[...])
        mo_ref[...] = mn
        vo_ref[...] = vn
        go_ref[...] = gv

    return pl.pallas_call(
        body, name=name, grid=(steps,), in_specs=[blk] * 4, out_specs=[blk] * 4,
        out_shape=[jax.ShapeDtypeStruct((r, c), F32)] * 4,
        compiler_params=_params("parallel"),
    )(w, g, m, v)


def _place():
    x, y, c = lax.axis_index("x"), lax.axis_index("y"), lax.axis_index("c")
    chips = [(1 - x, y), (x, 1 - y), (1 - x, 1 - y)]
    return x, y, c, chips


HBM = pl.BlockSpec(memory_space=pltpu.HBM)


SEM = pl.BlockSpec(memory_space=pltpu.SEMAPHORE)
ANY = pl.BlockSpec(memory_space=pl.ANY)
EFFECT = pltpu.SideEffectType.DATAFLOW_SIDE_EFFECTING


def _in_hbm(a):
    return pltpu.with_memory_space_constraint(a, pltpu.HBM)


def _remote(src, dst, send_sem, recv_sem, to):
    return pltpu.make_async_remote_copy(src_ref=src, dst_ref=dst, send_sem=send_sem, recv_sem=recv_sem,
                                        device_id=to, device_id_type=MESH)


def _by_rows(rows):
    return rows % 32 == 0


def _half_shape(rows, cols):
    return (rows // 2, cols) if _by_rows(rows) else (rows, cols // 2)


def _half(ref, which, *lead):
    rows, cols = ref.shape[-2:]
    if _by_rows(rows):
        return ref.at[(*lead, pl.ds(which * (rows // 2), rows // 2))]
    return ref.at[(*lead, slice(None), pl.ds(which * (cols // 2), cols // 2))]


def _landed(lands, i, shard_index, which):
    return _half(lands[i], which, shard_index)


def _routes():
    x, y, c, _ = _place()
    first = (jnp.where(c == 0, 1 - x, x), jnp.where(c == 0, y, 1 - y))
    second = (jnp.where(c == 0, x, 1 - x), jnp.where(c == 0, 1 - y, y))
    return first, second, (1 - x, 1 - y)


def _shard_of(chip):
    return 2 * chip[0] + chip[1]


def _gather_start(name, shards, after, relayed=()):
    n = len(shards)
    lands = [lax.empty((N_SHARD,) + s.shape, s.dtype) for s in shards]

    def body(*refs):
        ins, zones = refs[:n], refs[n:2 * n]
        ici_send, ici_recv, own_send, own_recv = refs[2 * n + 1:2 * n + 5]
        token = refs[-1]
        x, y, c, chips = _place()
        me = 2 * x + y
        first, _, _ = _routes()
        for i in range(n):
            if i in relayed:
                _remote(_half(ins[i], c), _landed(zones, i, me, c), ici_send.at[3 * i], ici_recv.at[3 * i],
                        (*first, c)).start()
                continue
            for j, chip in enumerate(chips):
                _remote(_half(ins[i], c), _landed(zones, i, me, c), ici_send.at[3 * i + j],
                        ici_recv.at[3 * i + j], (*chip, c)).start()
        for i in range(n):
            _remote(ins[i], zones[i].at[me], own_send.at[i], own_recv.at[i], (x, y, 1 - c)).start()
        token[...] = jnp.zeros_like(token)

    dma = pltpu.SemaphoreType.DMA
    outs = pl.pallas_call(
        body, name=name,
        in_specs=[HBM] * (2 * n) + [ANY],
        out_shape=(dma((3 * n,)), dma((3 * n,)), dma((n,)), dma((n,)),
                   *[pltpu.HBM(a.shape, a.dtype) for a in shards + lands], jax.ShapeDtypeStruct((8, LANES), F32)),
        out_specs=(SEM, SEM, SEM, SEM, *[HBM] * (2 * n), pl.BlockSpec(memory_space=pltpu.VMEM)),
        input_output_aliases={k: 4 + k for k in range(2 * n)},
        compiler_params=pltpu.CompilerParams(has_side_effects=EFFECT),
    )(*[_in_hbm(a) for a in shards + lands], after)
    sems = dict(zip(("ici_send", "ici_recv", "own_send", "own_recv"), outs[:4]))
    return sems, list(outs[4:4 + n]), list(outs[4 + n:4 + 2 * n]), outs[-1]


def _gather_forward(name, idx, lands, sems, after):
    n = len(idx)

    def body(*refs):
        zones = refs[:n]
        ici_recv = refs[n]
        fwd_send, fwd_recv = refs[n + 2], refs[n + 3]
        x, y, c, chips = _place()
        for k, i in enumerate(idx):
            for j, chip in enumerate(chips):
                half = _landed(zones, k, 2 * chip[0] + chip[1], c)
                _remote(half, half, fwd_send.at[3 * k + j], ici_recv.at[3 * i + j], (*chip, c)).wait_recv()
                _remote(half, half, fwd_send.at[3 * k + j], fwd_recv.at[3 * k + j], (x, y, 1 - c)).start()
        refs[-1][...] = jnp.zeros_like(refs[-1])

    dma = pltpu.SemaphoreType.DMA
    outs = pl.pallas_call(
        body, name=name,
        in_specs=[HBM] * n + [SEM, ANY],
        out_shape=(dma((3 * n,)), dma((3 * n,)), *[pltpu.HBM(a.shape, a.dtype) for a in lands],
                   jax.ShapeDtypeStruct((8, LANES), F32)),
        out_specs=(SEM, SEM, *[HBM] * n, pl.BlockSpec(memory_space=pltpu.VMEM)),
        input_output_aliases={k: 2 + k for k in range(n)},
        compiler_params=pltpu.CompilerParams(has_side_effects=EFFECT),
    )(*lands, sems["ici_recv"], after)
    return (outs[0], outs[1]), list(outs[2:2 + n]), outs[-1]


def _gather_wait(name, idx, shards, lands, sems, fwd, after):
    n = len(idx)

    def body(*refs):
        ins, zones = refs[:n], refs[n:2 * n]
        ici_send, own_send, own_recv, fwd_send, fwd_recv = refs[2 * n:2 * n + 5]
        x, y, c, chips = _place()
        me = 2 * x + y
        for k, i in enumerate(idx):
            mine = _half(ins[k], c)
            for j, chip in enumerate(chips):
                theirs = 2 * chip[0] + chip[1]
                _remote(mine, _landed(zones, k, me, c), ici_send.at[3 * i + j], fwd_recv.at[3 * k + j],
                        (*chip, c)).wait_send()
                sent = _landed(zones, k, theirs, c)
                _remote(sent, sent, fwd_send.at[3 * k + j], fwd_recv.at[3 * k + j], (x, y, 1 - c)).wait_send()
                passed = _landed(zones, k, theirs, 1 - c)
                _remote(passed, passed, fwd_send.at[3 * k + j], fwd_recv.at[3 * k + j], (x, y, 1 - c)).wait_recv()
            own = _remote(ins[k], zones[k].at[me], own_send.at[i], own_recv.at[i], (x, y, 1 - c))
            own.wait_send()
            own.wait_recv()

    outs = pl.pallas_call(
        body, name=name,
        in_specs=[HBM] * (2 * n) + [SEM] * 5 + [ANY],
        out_shape=tuple(pltpu.HBM(a.shape, a.dtype) for a in lands),
        out_specs=tuple([HBM] * n),
        input_output_aliases={n + k: k for k in range(n)},
        compiler_params=pltpu.CompilerParams(has_side_effects=EFFECT),
    )(*shards, *lands, sems["ici_send"], sems["own_send"], sems["own_recv"], fwd[0], fwd[1], after)
    return list(outs)


def _gather_relay(name, idx, shards, lands, sems, after):
    n = len(idx)

    def body(*refs):
        ins, zones, ici_recv = refs[:n], refs[n:2 * n], refs[2 * n]
        relay_send, relay_recv, pass_send, pass_recv = refs[2 * n + 2:2 * n + 6]
        x, y, c, _ = _place()
        first, second, _ = _routes()
        for k, i in enumerate(idx):
            landed = _landed(zones, k, _shard_of(first), c)
            _remote(landed, landed, pass_send.at[k], ici_recv.at[3 * i], (*first, c)).wait_recv()
            _remote(_half(ins[k], c), _landed(zones, k, 2 * x + y, c), relay_send.at[2 * k], relay_recv.at[2 * k],
                    (*second, c)).start()
            _remote(landed, landed, relay_send.at[2 * k + 1], relay_recv.at[2 * k + 1], (*second, c)).start()
            _remote(landed, landed, pass_send.at[k], pass_recv.at[k], (x, y, 1 - c)).start()
        refs[-1][...] = jnp.zeros_like(refs[-1])

    dma = pltpu.SemaphoreType.DMA
    outs = pl.pallas_call(
        body, name=name,
        in_specs=[HBM] * (2 * n) + [SEM, ANY],
        out_shape=(dma((2 * n,)), dma((2 * n,)), dma((n,)), dma((n,)), *[pltpu.HBM(a.shape, a.dtype) for a in lands],
                   jax.ShapeDtypeStruct((8, LANES), F32)),
        out_specs=(SEM, SEM, SEM, SEM, *[HBM] * n, pl.BlockSpec(memory_space=pltpu.VMEM)),
        input_output_aliases={n + k: 4 + k for k in range(n)},
        compiler_params=pltpu.CompilerParams(has_side_effects=EFFECT),
    )(*shards, *lands, sems["ici_recv"], after)
    return outs[:4], list(outs[4:4 + n]), outs[-1]


def _gather_forward_relayed(name, ks, lands, relay, after):
    n = len(ks)

    def body(*refs):
        zones, relay_recv = refs[:n], refs[n]
        fwd_send, fwd_recv = refs[n + 2], refs[n + 3]
        x, y, c, _ = _place()
        _, second, diagonal = _routes()
        for p, k in enumerate(ks):
            for j, chip in enumerate((second, diagonal)):
                landed = _landed(zones, p, _shard_of(chip), c)
                _remote(landed, landed, fwd_send.at[2 * p + j], relay_recv.at[2 * k + j], (*second, c)).wait_recv()
                _remote(landed, landed, fwd_send.at[2 * p + j], fwd_recv.at[2 * p + j], (x, y, 1 - c)).start()

    dma = pltpu.SemaphoreType.DMA
    outs = pl.pallas_call(
        body, name=name,
        in_specs=[HBM] * n + [SEM, ANY],
        out_shape=(dma((2 * n,)), dma((2 * n,)), *[pltpu.HBM(a.shape, a.dtype) for a in lands]),
        out_specs=(SEM, SEM, *[HBM] * n),
        input_output_aliases={k: 2 + k for k in range(n)},
        compiler_params=pltpu.CompilerParams(has_side_effects=EFFECT),
    )(*lands, relay[1], after)
    return (outs[0], outs[1]), list(outs[2:])


def _gather_wait_relayed(name, idx, ks, shards, lands, sems, relay, fwd, after):
    n = len(idx)

    def body(*refs):
        ins, zones = refs[:n], refs[n:2 * n]
        ici_send, own_send, own_recv, relay_send, pass_send, pass_recv, fwd_send, fwd_recv = refs[2 * n:2 * n + 8]
        x, y, c, _ = _place()
        me = 2 * x + y
        sibling = (x, y, 1 - c)
        first, second, diagonal = _routes()
        for p, (i, k) in enumerate(zip(idx, ks)):
            mine, at_peer = _half(ins[p], c), _landed(zones, p, me, c)
            from_first = _landed(zones, p, _shard_of(first), c)
            _remote(mine, at_peer, ici_send.at[3 * i], pass_recv.at[k], (*first, c)).wait_send()
            _remote(mine, at_peer, relay_send.at[2 * k], pass_recv.at[k], (*second, c)).wait_send()
            _remote(from_first, from_first, relay_send.at[2 * k + 1], pass_recv.at[k], (*second, c)).wait_send()
            _remote(from_first, from_first, pass_send.at[k], pass_recv.at[k], sibling).wait_send()
            theirs = _landed(zones, p, _shard_of(second), 1 - c)
            _remote(theirs, theirs, pass_send.at[k], pass_recv.at[k], sibling).wait_recv()
            for j, (sent, got) in enumerate(((second, first), (diagonal, diagonal))):
                out_half = _landed(zones, p, _shard_of(sent), c)
                _remote(out_half, out_half, fwd_send.at[2 * p + j], fwd_recv.at[2 * p + j], sibling).wait_send()
                in_half = _landed(zones, p, _shard_of(got), 1 - c)
                _remote(in_half, in_half, fwd_send.at[2 * p + j], fwd_recv.at[2 * p + j], sibling).wait_recv()
            own = _remote(ins[p], zones[p].at[me], own_send.at[i], own_recv.at[i], sibling)
            own.wait_send()
            own.wait_recv()

    outs = pl.pallas_call(
        body, name=name,
        in_specs=[HBM] * (2 * n) + [SEM] * 8 + [ANY],
        out_shape=tuple(pltpu.HBM(a.shape, a.dtype) for a in lands),
        out_specs=tuple([HBM] * n),
        input_output_aliases={n + k: k for k in range(n)},
        compiler_params=pltpu.CompilerParams(has_side_effects=EFFECT),
    )(*shards, *lands, sems["ici_send"], sems["own_send"], sems["own_recv"], relay[0], relay[2], relay[3],
      fwd[0], fwd[1], after)
    return list(outs)


def _all_reduce_small(name, slab, after=None):
    r, width = slab.shape
    ndev = 8

    def body(x_ref, after_ref, out_ref, buf, send_sems, recv_sems):
        x, y, c, _ = _place()
        me = 4 * x + 2 * y + c
        buf[me] = x_ref[...]
        copies = []
        for k in range(1, ndev):
            peer = jnp.bitwise_xor(me, k)
            to = (peer // 4, (peer // 2) % 2, peer % 2)
            cp = pltpu.make_async_remote_copy(src_ref=x_ref, dst_ref=buf.at[me], send_sem=send_sems.at[k - 1],
                                              recv_sem=recv_sems.at[k - 1], device_id=to, device_id_type=MESH)
            cp.start()
            copies.append(cp)
        for k in range(1, ndev):
            peer = jnp.bitwise_xor(me, k)
            pltpu.make_async_remote_copy(src_ref=x_ref, dst_ref=buf.at[peer], send_sem=send_sems.at[k - 1],
                                         recv_sem=recv_sems.at[k - 1], device_id=(x, y, c),
                                         device_id_type=MESH).wait_recv()
        for cp in copies:
            cp.wait_send()
        total = buf[0]
        for d in range(1, ndev):
            total = total + buf[d]
        out_ref[...] = total

    return pl.pallas_call(
        body, name=name,
        in_specs=[pl.BlockSpec(memory_space=pltpu.VMEM), ANY], out_specs=pl.BlockSpec(memory_space=pltpu.VMEM),
        out_shape=jax.ShapeDtypeStruct((r, width), F32),
        scratch_shapes=[pltpu.VMEM((ndev, r, width), F32), pltpu.SemaphoreType.DMA((ndev - 1,)),
                        pltpu.SemaphoreType.DMA((ndev - 1,))],
        compiler_params=pltpu.CompilerParams(vmem_limit_bytes=VMEM_LIMIT),
    )(slab, slab if after is None else after)


def _half_tiling(rows, cols):
    if _by_rows(rows):
        tr = _pick(rows // 2, (256, 128, 64, 32, 16))
        nb = (rows // 2) // tr
        return (tr, cols), nb, (lambda which, b: (which * nb + b, 0)), (lambda b: (b, 0))
    tc = _pick(cols // 2, (256, 128))
    nb = (cols // 2) // tc
    return (rows, tc), nb, (lambda which, b: (0, which * nb + b)), (lambda b: (0, b))


def _chip_partial(name, grad, other, core):
    s, r, cdim = grad.shape
    blk, nb, whole, within = _half_tiling(r, cdim)

    def body(core_ref, g_ref, o_ref, out_ref):
        out_ref[...] = (g_ref[...] + o_ref[...]).astype(BF16)

    return pl.pallas_call(
        body, name=name,
        grid_spec=pltpu.PrefetchScalarGridSpec(
            num_scalar_prefetch=1, grid=(s, nb),
            in_specs=[pl.BlockSpec((None,) + blk, lambda j, b, core_ref: (j,) + whole(core_ref[0], b)),
                      pl.BlockSpec((None,) + blk, lambda j, b, core_ref: (j,) + within(b))],
            out_specs=pl.BlockSpec((None,) + blk, lambda j, b, core_ref: (j,) + within(b))),
        out_shape=jax.ShapeDtypeStruct((s,) + _half_shape(r, cdim), BF16),
        compiler_params=_params("parallel", "parallel"),
    )(core, grad, other)


def _partial_copies(ins, zones, send_sems, recv_sems):
    x, y, c, chips = _place()
    return [_remote(ins[i].at[2 * chip[0] + chip[1]], zones[i].at[j], send_sems.at[3 * i + j],
                    recv_sems.at[3 * i + j], (*chip, c))
            for i in range(len(ins)) for j, chip in enumerate(chips)]


def _swap_copies(ins, zones, send_sems, recv_sems):
    x, y, c, _ = _place()
    copies = []
    for i in range(len(ins)):
        for s in range(N_SHARD):
            copies.append(_remote(_half(ins[i], 1 - c, s), zones[i].at[s],
                                  send_sems.at[N_SHARD * i + s], recv_sems.at[N_SHARD * i + s], (x, y, 1 - c)))
    return copies


def _exchange_start(name, plan, sources, lands, per_array):
    n = len(sources)
    lands = [lax.empty(shape, dtype) for shape, dtype in lands]

    def body(*refs):
        for cp in plan(refs[:n], refs[n:2 * n], refs[2 * n], refs[2 * n + 1]):
            cp.start()
        refs[-1][...] = jnp.zeros_like(refs[-1])

    dma = pltpu.SemaphoreType.DMA
    outs = pl.pallas_call(
        body, name=name,
        in_specs=[HBM] * (2 * n),
        out_shape=(dma((per_array * n,)), dma((per_array * n,)),
                   *[pltpu.HBM(a.shape, a.dtype) for a in list(sources) + lands], jax.ShapeDtypeStruct((8, LANES), F32)),
        out_specs=(SEM, SEM, *[HBM] * (2 * n), pl.BlockSpec(memory_space=pltpu.VMEM)),
        input_output_aliases={k: 2 + k for k in range(2 * n)},
        compiler_params=pltpu.CompilerParams(has_side_effects=EFFECT),
    )(*[_in_hbm(a) for a in list(sources) + lands])
    return (outs[0], outs[1]), list(outs[2:2 + n]), list(outs[2 + n:2 + 2 * n]), outs[-1]


def _exchange_wait(name, plan, started, after):
    sems, partials, lands, _ = started
    n = len(partials)

    def body(*refs):
        for cp in plan(refs[:n], refs[n:2 * n], refs[2 * n], refs[2 * n + 1]):
            cp.wait_send()
            cp.wait_recv()

    outs = pl.pallas_call(
        body, name=name,
        in_specs=[HBM] * (2 * n) + [SEM, SEM] + [ANY] * len(after),
        out_shape=tuple(pltpu.HBM(a.shape, a.dtype) for a in lands),
        out_specs=tuple([HBM] * n),
        input_output_aliases={n + k: k for k in range(n)},
        compiler_params=pltpu.CompilerParams(has_side_effects=EFFECT),
    )(*partials, *lands, sems[0], sems[1], *after)
    return list(outs)


def _reduce_own(name, grad, other, received, where):
    s, r, cdim = grad.shape
    blk, nb, whole, within = _half_tiling(r, cdim)

    def body(where_ref, g_ref, o_ref, r_ref, out_ref):
        total = g_ref[...] + o_ref[...]
        for j in range(3):
            total = total + r_ref[j].astype(F32)
        out_ref[...] = total

    return pl.pallas_call(
        body, name=name,
        grid_spec=pltpu.PrefetchScalarGridSpec(
            num_scalar_prefetch=1, grid=(nb,),
            in_specs=[pl.BlockSpec((None,) + blk, lambda b, w_ref: (w_ref[0],) + whole(w_ref[1], b)),
                      pl.BlockSpec((None,) + blk, lambda b, w_ref: (w_ref[0],) + within(b)),
                      pl.BlockSpec((3,) + blk, lambda b, w_ref: (0,) + within(b))],
            out_specs=pl.BlockSpec(blk, lambda b, w_ref: whole(w_ref[1], b))),
        out_shape=jax.ShapeDtypeStruct((r, cdim), F32),
        compiler_params=_params("parallel"),
    )(where, grad, other, received)


def _join_start(name, halves):
    n = len(halves)

    def body(*refs):
        bufs, send_sems, recv_sems = refs[:n], refs[n], refs[n + 1]
        x, y, c, _ = _place()
        for i in range(n):
            mine = _half(bufs[i], c)
            _remote(mine, mine, send_sems.at[i], recv_sems.at[i], (x, y, 1 - c)).start()
        refs[-1][...] = jnp.zeros_like(refs[-1])

    dma = pltpu.SemaphoreType.DMA
    outs = pl.pallas_call(
        body, name=name,
        in_specs=[HBM] * n,
        out_shape=(dma((n,)), dma((n,)), *[pltpu.HBM(h.shape, F32) for h in halves], jax.ShapeDtypeStruct((8, LANES), F32)),
        out_specs=(SEM, SEM, *[HBM] * n, pl.BlockSpec(memory_space=pltpu.VMEM)),
        input_output_aliases={k: 2 + k for k in range(n)},
        compiler_params=pltpu.CompilerParams(has_side_effects=EFFECT),
    )(*[_in_hbm(h) for h in halves])
    return (outs[0], outs[1]), list(outs[2:2 + n]), outs[-1]


def _join_wait(name, started, after):
    sems, bufs, _ = started
    n = len(bufs)

    def body(*refs):
        bufs, send_sems, recv_sems = refs[:n], refs[n], refs[n + 1]
        x, y, c, _ = _place()
        for i in range(n):
            mine, theirs = _half(bufs[i], c), _half(bufs[i], 1 - c)
            _remote(mine, mine, send_sems.at[i], recv_sems.at[i], (x, y, 1 - c)).wait_send()
            _remote(theirs, theirs, send_sems.at[i], recv_sems.at[i], (x, y, 1 - c)).wait_recv()

    outs = pl.pallas_call(
        body, name=name,
        in_specs=[HBM] * n + [SEM, SEM] + [ANY] * len(after),
        out_shape=tuple(pltpu.HBM(b.shape, F32) for b in bufs),
        out_specs=tuple([HBM] * n),
        input_output_aliases={k: k for k in range(n)},
        compiler_params=pltpu.CompilerParams(has_side_effects=EFFECT),
    )(*bufs, sems[0], sems[1], *after)
    return list(outs)


BIG = ("w_in", "pool_w", "w_out", "xq_w", "xk_w", "xv_w", "xo_w", "w_up", "w_down", "conv_w")
KEPT_F32 = ("conv_w",)
GATHER_GROUPS = ((0, 1, 9), (2, 3, 4, 5, 6), (7,), (8,))
RELAYED = (7, 8)
SMALL = ("conv_w", "a_log", "dt_bias", "gdn_norm_w", "pool_scale", "ln1_g", "ln1_b", "ln2_g", "ln2_b", "ln3_g", "ln3_b")
ORDER = ("w_in", "conv_w", "a_log", "dt_bias", "gdn_norm_w", "pool_w", "pool_scale", "w_out", "ln1_g", "ln1_b",
         "xq_w", "xk_w", "xv_w", "xo_w", "ln2_g", "ln2_b", "w_up", "w_down", "ln3_g", "ln3_b")
LANES = 128


def _rows(flat_len):
    return -(-flat_len // LANES)


def _pack(pieces):
    out = []
    for p in pieces:
        flat = p.reshape(-1).astype(F32)
        out.append(jnp.pad(flat, (0, _rows(flat.shape[0]) * LANES - flat.shape[0])).reshape(-1, LANES))
    slab = jnp.concatenate(out, axis=0)
    return jnp.pad(slab, ((0, -slab.shape[0] % 8), (0, 0)))


def _unpack(slab, shapes):
    out, row = [], 0
    for shp in shapes:
        size = math.prod(shp)
        out.append(slab[row:row + _rows(size)].reshape(-1)[:size].reshape(shp))
        row += _rows(size)
    return out


TRANSPOSED = ("w_in",)


def _as2d(name, a):
    a = a[0]
    if name in TRANSPOSED:
        return jnp.swapaxes(a, 0, 1)
    return a.reshape(-1, a.shape[-1]) if a.ndim == 3 else a


def _from2d(name, a, shape):
    return (jnp.swapaxes(a, 0, 1) if name in TRANSPOSED else a).reshape(shape)


def kernel(x, mem, w_in, conv_w, a_log, dt_bias, gdn_norm_w, pool_w, pool_scale, w_out, ln1_g, ln1_b, xq_w, xk_w, xv_w, xo_w, ln2_g, ln2_b, w_up, w_down, ln3_g, ln3_b, loss_target, m_w_in, m_conv_w, m_a_log, m_dt_bias, m_gdn_norm_w, m_pool_w, m_pool_scale, m_w_out, m_ln1_g, m_ln1_b, m_xq_w, m_xk_w, m_xv_w, m_xo_w, m_ln2_g, m_ln2_b, m_w_up, m_w_down, m_ln3_g, m_ln3_b, v_w_in, v_conv_w, v_a_log, v_dt_bias, v_gdn_norm_w, v_pool_w, v_pool_scale, v_w_out, v_ln1_g, v_ln1_b, v_xq_w, v_xk_w, v_xv_w, v_xo_w, v_ln2_g, v_ln2_b, v_w_up, v_w_down, v_ln3_g, v_ln3_b):
    given = dict(locals())
    cx, cy, cc = lax.axis_index("x"), lax.axis_index("y"), lax.axis_index("c")
    me = 2 * cx + cy
    groups = pool_w.shape[1]
    cs = pool_w.shape[2]
    kk, conv_cols = conv_w.shape[1], conv_w.shape[2]
    core = cc.astype(jnp.int32).reshape(1)
    where = jnp.stack([me, cc]).astype(jnp.int32)

    started = {}
    wts = {}

    def start(name, idx, after, token=None):
        casts = [_after(token, _as2d(BIG[i], given[BIG[i]])).astype(F32 if BIG[i] in KEPT_F32 else BF16) for i in idx]
        relayed = tuple(k for k, i in enumerate(idx) if i in RELAYED)
        sems, shards, lands, token = _gather_start(name, casts, after, relayed)
        for k, i in enumerate(idx):
            started[i] = (sems, k, shards[k], lands[k])
        return token

    token = start("gather_start_first", GATHER_GROUPS[0], x)
    token = start("gather_start_rest", tuple(i for group in GATHER_GROUPS[1:] for i in group), token, token)

    relay = {}

    def send_on(after):
        members = [started[i] for i in RELAYED]
        relay["sems"], zones, token = _gather_relay("gather_relay", [m[1] for m in members], [m[2] for m in members],
                                                    [m[3] for m in members], members[0][0], after)
        relay["zones"] = dict(zip(RELAYED, zones))
        return token

    passed = {}

    def pass_on(group, after):
        members = [started[i] for i in GATHER_GROUPS[group]]
        fwd, zones, token = _gather_forward(f"gather_forward_{group}", [m[1] for m in members], [m[3] for m in members],
                                            members[0][0], after)
        passed[group] = (fwd, zones)
        return token

    def fetch(group, after):
        members = [started[i] for i in GATHER_GROUPS[group]]
        sems, idx = members[0][0], [m[1] for m in members]
        shards = [m[2] for m in members]
        if GATHER_GROUPS[group][0] in RELAYED:
            ks = [RELAYED.index(i) for i in GATHER_GROUPS[group]]
            zones = [relay["zones"][i] for i in GATHER_GROUPS[group]]
            fwd, zones = _gather_forward_relayed(f"gather_forward_{group}", ks, zones, relay["sems"], after)
            got = _gather_wait_relayed(f"gather_wait_{group}", idx, ks, shards, zones, sems, relay["sems"], fwd, after)
        else:
            if group not in passed:
                pass_on(group, after)
            fwd, zones = passed[group]
            got = _gather_wait(f"gather_wait_{group}", idx, shards, zones, sems, fwd, after)
        full = dict(zip([BIG[i] for i in GATHER_GROUPS[group]], got))
        out = {}
        for n, a in full.items():
            if n == "w_in":
                out["w_in_t"] = a
            elif n == "w_up":
                out["w_up3"] = a
            elif n == "pool_w":
                out[n] = a.reshape(N_SHARD, groups, cs, -1).transpose(1, 0, 2, 3).reshape(groups, N_SHARD * cs, -1)
            elif n == "conv_w":
                out[n] = a.transpose(1, 0, 2).reshape(kk, N_SHARD * conv_cols)
            else:
                out[n] = a.reshape(-1, a.shape[-1])
        return out

    for n in ("a_log", "dt_bias", "gdn_norm_w", "pool_scale", "ln1_g", "ln1_b", "ln2_g", "ln2_b", "ln3_g", "ln3_b"):
        wts[n] = given[n]
    x_bf = _after(token, x[0]).astype(BF16)
    wts.update(fetch(0, x_bf))

    def start_swap(group, grads):
        names, blocks = [], []
        for n, g in grads.items():
            if n == "pool_w":
                g = g.reshape(groups, N_SHARD, cs, -1).transpose(1, 0, 2, 3).reshape(N_SHARD, groups * cs, -1)
            elif g.ndim == 2:
                g = g.reshape(N_SHARD, -1, g.shape[-1])
            names.append({"w_in_t": "w_in", "w_up3": "w_up"}.get(n, n))
            blocks.append(g)
        zones = [((N_SHARD,) + _half_shape(b.shape[1], b.shape[2]), F32) for b in blocks]
        swap = _exchange_start(f"grad_swap_start_{group}", _swap_copies, blocks, zones, N_SHARD)
        return {"group": group, "names": names, "swap": swap, "token": swap[3]}

    def start_send(state, after):
        group, names = state["group"], state["names"]
        state["blocks"] = state["swap"][1]
        state["others"] = _exchange_wait(f"grad_swap_wait_{group}", _swap_copies, state["swap"], after)
        partials = [_chip_partial("chip_partial_" + n, gb, ob, core)
                    for n, gb, ob in zip(names, state["blocks"], state["others"])]
        zones = [((3,) + p.shape[1:], BF16) for p in partials]
        state["send"] = _exchange_start(f"grad_send_start_{group}", _partial_copies, partials, zones, 3)
        state["token"] = state["send"][3]

    grad, delta, new_m, new_v = {}, {}, {}, {}

    def start_join(state, after):
        group, names = state["group"], state["names"]
        received = _exchange_wait(f"grad_send_wait_{group}", _partial_copies, state["send"], after)
        halves = [_reduce_own("reduce_own_" + n, gb, ob, rb, where)
                  for n, gb, ob, rb in zip(names, state["blocks"], state["others"], received)]
        state["join"] = _join_start(f"grad_join_start_{group}", halves)
        return state["join"][2]

    def finish_reduce(state, after):
        group, names = state["group"], state["names"]
        for n, g in zip(names, _join_wait(f"grad_join_wait_{group}", state["join"], after)):
            shp = given[n].shape
            d2, m2, v2, g2 = _adamw("adamw_" + n, _as2d(n, given[n]), g, _as2d(n, given["m_" + n]),
                                    _as2d(n, given["v_" + n]))
            grad[n], delta[n], new_m[n], new_v[n] = (_from2d(n, a, shp) for a in (g2, d2, m2, v2))
        return d2

    step = _local_step(x[0], mem[0], loss_target[0], wts, x_bf)
    pending = {}
    request = next(step)
    while True:
        try:
            kind, group, payload = request
            if kind == "weights":
                request = step.send(fetch(group, payload))
            elif kind == "relay":
                request = step.send(send_on(payload))
            elif kind == "pass":
                request = step.send(pass_on(group, payload))
            elif kind == "grads":
                pending[group] = start_swap(group, payload)
                request = step.send(pending[group]["token"])
            else:
                start_send(pending[group], [payload])
                request = step.send(pending[group]["token"])
        except StopIteration as stop:
            loss_row, grad_x, g = stop.value
            break

    after = [pending[2]["token"], grad_x]
    for group in (0, 1):
        after = [start_join(pending[group], after)]
    for group in (0, 1):
        after = [finish_reduce(pending[group], after)]
    after = [finish_reduce(pending[2], [start_join(pending[2], after)])]

    small_names = ("a_log", "dt_bias", "gdn_norm_w", "pool_scale", "ln1_g", "ln1_b", "ln2_g", "ln2_b", "ln3_g", "ln3_b")
    pieces = [g["conv_w"]] + [g[n] for n in small_names] + [loss_row[:, :1]]
    shapes = [p.shape for p in pieces]
    summed = _unpack(_all_reduce_small("all_reduce_small", _pack(pieces), after[0]), shapes)
    gsmall = dict(zip(small_names, summed[1:-1]))
    gsmall["conv_w"] = lax.dynamic_slice(summed[0], (0, me * conv_cols), (kk, conv_cols))
    loss = summed[-1][0, 0]

    sshapes = [given[n].shape for n in SMALL]
    slabs = [_pack([given[p + n] for n in SMALL]) for p in ("", "m_", "v_")]
    gslab = _pack([gsmall[n] for n in SMALL])
    outs = _adamw("adamw_small", slabs[0], gslab, slabs[1], slabs[2])[:3]
    for dst, slab in zip((delta, new_m, new_v), outs):
        dst.update(zip(SMALL, _unpack(slab, sshapes)))
    for n in SMALL:
        grad[n] = gsmall[n].reshape(given[n].shape)

    return (loss, grad_x[None], *[grad[n] for n in ORDER], *[delta[n] for n in ORDER],
            *[new_m[n] for n in ORDER], *[new_v[n] for n in ORDER])
```

```python
import math

import jax
import jax.numpy as jnp
from jax import lax
from jax.experimental import pallas as pl
from jax.experimental.pallas import tpu as pltpu

F32 = jnp.float32
BF16 = jnp.bfloat16
MESH = pl.DeviceIdType.MESH

HEAD_DIM = 128
CHUNK = 64
POOL_WINDOWS = (2, 4, 8, 16)
XATTN_HEADS = 4
ALPHA = 2.0 ** 0.25
LN_EPS = 1e-5
NORM_EPS = 1e-6
ADAM_LR, ADAM_B1, ADAM_B2, ADAM_EPS, ADAM_WD, ADAM_STEP = 0.001, 0.9, 0.999, 1e-08, 0.01, 10
N_SHARD = 4
VMEM_LIMIT = 56 * 1024 * 1024
K_STEPS = (2048, 1024, 512, 256, 128)


def _params(*sem):
    return pltpu.CompilerParams(dimension_semantics=sem, vmem_limit_bytes=VMEM_LIMIT)


def _bdot(a, b, ta=False, tb=False):
    dims = (((0 if ta else 1,), (1 if tb else 0,)), ((), ()))
    return lax.dot_general(a.astype(BF16), b.astype(BF16), dims, preferred_element_type=F32)


def _sigmoid(x):
    return 1.0 / (1.0 + jnp.exp(-x))


def _matmul(name, a, b, *, ta=False, tb=False, tm, tn, tk, extra=(), outs, epilogue, b_blocks=None,
            sequential=False, n_used=None, n_outer=False):
    m, k_dim = (a.shape[1], a.shape[0]) if ta else a.shape
    if b_blocks and tb:
        n = b.shape[1]
        k_dim = b.shape[0] * b.shape[2]
        per = b.shape[2] // tk
        b_spec = pl.BlockSpec((None, tn, tk), lambda i, j, k: (k // per, j, k % per))
    elif b_blocks:
        n = b.shape[0] * b.shape[2]
        per = b.shape[2] // tn
        b_spec = pl.BlockSpec((None, tk, tn), lambda i, j, k: (j // per, k, j % per))
    elif tb:
        n = b.shape[0]
        b_spec = pl.BlockSpec((tn, tk), lambda i, j, k: (j, k))
    else:
        n = b.shape[1]
        b_spec = pl.BlockSpec((tk, tn), lambda i, j, k: (k, j))
    n = n_used or n
    assert m % tm == 0 and n % tn == 0 and k_dim % tk == 0, (name, m, n, k_dim, tm, tn, tk)
    nk = k_dim // tk
    a_spec = pl.BlockSpec((tk, tm), lambda i, j, k: (k, i)) if ta else pl.BlockSpec((tm, tk), lambda i, j, k: (i, k))
    n_extra, n_out = len(extra), len(outs)

    def wrap(index_map):
        return lambda i, j, k: index_map(i, j)

    def spec(block, index_map):
        if n_outer:
            return pl.BlockSpec(block, lambda j, i, k: index_map(i, j, k))
        return pl.BlockSpec(block, index_map)

    row_axis = 1 if n_outer else 0

    def body_one_step(*refs):
        ex = refs[2:2 + n_extra]
        out = refs[2 + n_extra:2 + n_extra + n_out]
        epilogue(_bdot(refs[0][...], refs[1][...], ta, tb), ex, out, pl.program_id(row_axis))

    def body(*refs):
        a_ref, b_ref = refs[0], refs[1]
        ex = refs[2:2 + n_extra]
        out = refs[2 + n_extra:2 + n_extra + n_out]
        acc = refs[-1]
        i, k = pl.program_id(row_axis), pl.program_id(2)
        part = _bdot(a_ref[...], b_ref[...], ta, tb)

        @pl.when(k == 0)
        def _():
            acc[...] = part

        @pl.when(jnp.logical_and(k > 0, k < nk - 1))
        def _():
            acc[...] += part

        @pl.when(k == nk - 1)
        def _():
            epilogue(acc[...] + part, ex, out, i)

    sem = ("arbitrary",) * 3 if sequential else ("parallel", "parallel", "arbitrary")
    res = pl.pallas_call(
        body_one_step if nk == 1 else body, name=name,
        grid=(n // tn, m // tm, nk) if n_outer else (m // tm, n // tn, nk),
        in_specs=[spec(a_spec.block_shape, a_spec.index_map), spec(b_spec.block_shape, b_spec.index_map)]
        + [spec(bs, wrap(im)) for _, bs, im in extra],
        out_specs=[spec(bs, wrap(im)) for _, bs, im in outs],
        out_shape=[s for s, _, _ in outs],
        scratch_shapes=[] if nk == 1 else [pltpu.VMEM((tm, tn), F32)],
        compiler_params=_params(*sem),
    )(a, b, *[x for x, _, _ in extra])
    return res


def _tile(i, j):
    return (i, j)


def _plain(name, a, b, *, ta=False, tb=False, tm, tn, tk, out_dtype, b_blocks=None, out3=None, n_used=None,
           n_outer=False, m_kept=None):
    m = a.shape[1] if ta else a.shape[0]
    if b_blocks:
        n = b.shape[1] if tb else b.shape[0] * b.shape[2]
    else:
        n = n_used or (b.shape[0] if tb else b.shape[1])

    def epi(acc, ex, out, i):
        out[0][...] = acc.astype(out_dtype)

    if out3:
        per = (n // out3) // tn
        spec = (jax.ShapeDtypeStruct((out3, m, n // out3), out_dtype), (None, tm, tn),
                lambda i, j: (j // per, i, j % per))
    else:
        spec = (jax.ShapeDtypeStruct((m_kept or m, n), out_dtype), (tm, tn), _tile)
    return _matmul(name, a, b, ta=ta, tb=tb, tm=tm, tn=tn, tk=tk, outs=[spec], epilogue=epi,
                   b_blocks=b_blocks, n_used=n_used, n_outer=n_outer)[0]


def _ln_forward(name, a, b, res, gamma, beta, *, tm, tk, want_h=True):
    m, n = res.shape

    def epi(acc, ex, out, i):
        u = ALPHA * ex[0][...] + acc
        mu = jnp.mean(u, axis=-1, keepdims=True)
        xc = u - mu
        var = jnp.mean(xc * xc, axis=-1, keepdims=True)
        rstd = lax.rsqrt(var + LN_EPS)
        xhat = xc * rstd
        out[-2][...] = xhat
        out[-1][...] = rstd
        if want_h:
            h = xhat * ex[1][...] + ex[2][...]
            out[0][...] = h
            out[1][...] = h.astype(BF16)

    row = lambda i, j: (i, 0)
    vec = lambda i, j: (0, 0)
    outs = [(jax.ShapeDtypeStruct((m, n), F32), (tm, n), row), (jax.ShapeDtypeStruct((m, n), BF16), (tm, n), row),
            (jax.ShapeDtypeStruct((m, n), F32), (tm, n), row), (jax.ShapeDtypeStruct((m, 1), F32), (tm, 1), row)]
    return _matmul(
        name, a, b, tm=tm, tn=n, tk=tk,
        extra=[(res, (tm, n), row), (gamma, (1, n), vec), (beta, (1, n), vec)],
        outs=outs if want_h else outs[2:], epilogue=epi)


def _ln_backward_math(dy, xhat, rstd, gamma):
    dxhat = dy * gamma
    m1 = jnp.mean(dxhat, axis=-1, keepdims=True)
    m2 = jnp.mean(dxhat * xhat, axis=-1, keepdims=True)
    du = rstd * (dxhat - m1 - xhat * m2)
    return du, jnp.sum(dy * xhat, axis=0, keepdims=True), jnp.sum(dy, axis=0, keepdims=True)


def _ln_backward(name, a, b, dres, xhat, rstd, gamma, *, tm, tk, b_blocks=None, tb=True):
    m, n = dres.shape

    def epi(acc, ex, out, i):
        dy = acc + ALPHA * ex[0][...]
        du, dg, db = _ln_backward_math(dy, ex[1][...], ex[2][...], ex[3][...])
        out[0][...] = du
        out[1][...] = du.astype(BF16)
        first = i == 0

        @pl.when(first)
        def _():
            out[2][...] = dg
            out[3][...] = db

        @pl.when(jnp.logical_not(first))
        def _():
            out[2][...] += dg
            out[3][...] += db

    row = lambda i, j: (i, 0)
    vec = lambda i, j: (0, 0)
    return _matmul(
        name, a, b, tb=tb, tm=tm, tn=n, tk=tk, b_blocks=b_blocks, sequential=True,
        extra=[(dres, (tm, n), row), (xhat, (tm, n), row), (rstd, (tm, 1), row), (gamma, (1, n), vec)],
        outs=[(jax.ShapeDtypeStruct((m, n), F32), (tm, n), row),
              (jax.ShapeDtypeStruct((m, n), BF16), (tm, n), row),
              (jax.ShapeDtypeStruct((1, n), F32), (1, n), vec),
              (jax.ShapeDtypeStruct((1, n), F32), (1, n), vec)],
        epilogue=epi)


def _shift_down(x, k):
    row = lax.broadcasted_iota(jnp.int32, x.shape, 0)
    return jnp.where(row >= k, pltpu.roll(x, k, axis=0), 0.0)


def _shift_up(x, k):
    t = x.shape[0]
    row = lax.broadcasted_iota(jnp.int32, x.shape, 0)
    return jnp.where(row < t - k, pltpu.roll(x, t - k, axis=0), 0.0)


def _conv_silu_norm(x, w, normalise):
    kk = w.shape[0]
    c = x * w[kk - 1:kk, :]
    for j in range(kk - 1):
        c = c + _shift_down(x, kk - 1 - j) * w[j:j + 1, :]
    sg = _sigmoid(c)
    s = c * sg
    r = lax.rsqrt(jnp.sum(s * s, axis=-1, keepdims=True) + NORM_EPS)
    y = jnp.where(normalise, s * r, s)
    return c, sg, s, r, y


def _gdn_pre(proj, conv_w, heads):
    t = proj.shape[0]
    kk = conv_w.shape[0]

    def body(x_ref, w_ref, o_ref):
        normalise = pl.program_id(0) < 2
        o_ref[...] = _conv_silu_norm(x_ref[...], w_ref[...], normalise)[4]

    col = lambda s, h: (0, s * heads + h)
    return pl.pallas_call(
        body, name="gdn_pre", grid=(3, heads),
        in_specs=[pl.BlockSpec((t, HEAD_DIM), col), pl.BlockSpec((kk, HEAD_DIM), col)],
        out_specs=pl.BlockSpec((t, HEAD_DIM), col),
        out_shape=jax.ShapeDtypeStruct((t, 3 * heads * HEAD_DIM), F32),
        compiler_params=_params("parallel", "parallel"),
    )(proj, conv_w)


def _gdn_pre_backward(proj, conv_w, dqkv, heads):
    t = proj.shape[0]
    kk = conv_w.shape[0]

    def body(x_ref, w_ref, dy_ref, dx_ref, dw_ref):
        normalise = pl.program_id(0) < 2
        x = x_ref[...]
        w = w_ref[...]
        dy = dy_ref[...]
        c, sg, s, r, y = _conv_silu_norm(x, w, normalise)
        ds_norm = r * (dy - y * jnp.sum(dy * y, axis=-1, keepdims=True))
        ds = jnp.where(normalise, ds_norm, dy)
        dc = ds * (sg * (1.0 + c * (1.0 - sg)))
        dx = dc * w[kk - 1:kk, :]
        rows = [None] * kk
        rows[kk - 1] = jnp.sum(dc * x, axis=0, keepdims=True)
        for j in range(kk - 1):
            lag = kk - 1 - j
            dx = dx + _shift_up(dc, lag) * w[j:j + 1, :]
            rows[j] = jnp.sum(dc * _shift_down(x, lag), axis=0, keepdims=True)
        dx_ref[...] = dx.astype(BF16)
        dw_ref[...] = jnp.concatenate(rows, axis=0)

    col = lambda s, h: (0, s * heads + h)
    return pl.pallas_call(
        body, name="gdn_pre_bwd", grid=(3, heads),
        in_specs=[pl.BlockSpec((t, HEAD_DIM), col), pl.BlockSpec((kk, HEAD_DIM), col),
                  pl.BlockSpec((t, HEAD_DIM), col)],
        out_specs=[pl.BlockSpec((t, HEAD_DIM), col), pl.BlockSpec((kk, HEAD_DIM), col)],
        out_shape=[jax.ShapeDtypeStruct((t, 3 * heads * HEAD_DIM), BF16),
                   jax.ShapeDtypeStruct((kk, 3 * heads * HEAD_DIM), F32)],
        compiler_params=_params("parallel", "parallel"),
    )(proj, conv_w, dqkv)


def _gate_vectors(a_log, dt_bias, heads):
    pad = lambda v: jnp.pad(v.astype(F32), ((0, 0), (heads, HEAD_DIM - 2 * heads)))
    return pad(jnp.exp(a_log.astype(F32))), pad(dt_bias)


def _softplus(x):
    return jnp.maximum(x, 0.0) + jnp.log(1.0 + jnp.exp(-jnp.abs(x)))


def _gates_epilogue(heads):
    def epi(acc, ex, out, i):
        lane = lax.broadcasted_iota(jnp.int32, acc.shape, 1)
        beta = _sigmoid(acc)
        g = -ex[0][...] * _softplus(acc + ex[1][...])
        out[0][...] = acc
        out[1][...] = jnp.where(lane < heads, beta, jnp.where(lane < 2 * heads, g, 0.0))
    return epi


def _gates_backward(ba, bg, dbg, ea, dtb, heads):
    t = ba.shape[0]

    def body(ba_ref, bg_ref, d_ref, ea_ref, dt_ref, dba_ref, dal_ref, ddt_ref):
        lane = lax.broadcasted_iota(jnp.int32, (t, HEAD_DIM), 1)
        bgv = bg_ref[...]
        d = d_ref[...]
        db = d * bgv * (1.0 - bgv)
        da = -d * ea_ref[...] * _sigmoid(ba_ref[...] + dt_ref[...])
        is_g = jnp.logical_and(lane >= heads, lane < 2 * heads)
        dba = jnp.where(lane < heads, db, jnp.where(is_g, da, 0.0))
        dba_ref[...] = dba.astype(BF16)
        dal_ref[...] = jnp.sum(jnp.where(is_g, d * bgv, 0.0), axis=0, keepdims=True)
        ddt_ref[...] = jnp.sum(jnp.where(is_g, da, 0.0), axis=0, keepdims=True)

    full = pl.BlockSpec((t, HEAD_DIM), lambda: (0, 0))
    vec = pl.BlockSpec((1, HEAD_DIM), lambda: (0, 0))
    return pl.pallas_call(
        body, name="gates_bwd", grid=(),
        in_specs=[full, full, full, vec, vec], out_specs=[full, vec, vec],
        out_shape=[jax.ShapeDtypeStruct((t, HEAD_DIM), BF16), jax.ShapeDtypeStruct((1, HEAD_DIM), F32),
                   jax.ShapeDtypeStruct((1, HEAD_DIM), F32)],
        compiler_params=pltpu.CompilerParams(vmem_limit_bytes=VMEM_LIMIT),
    )(ba, bg, dbg, ea, dtb)


class _Chunk:
    pass


def _split2(x):
    hi = x.astype(BF16)
    return hi, (x - hi.astype(F32)).astype(BF16)


def _split3(x):
    hi = x.astype(BF16)
    rest = x - hi.astype(F32)
    mid = rest.astype(BF16)
    return hi, mid, (rest - mid.astype(F32)).astype(BF16)


def _dot_mask(mask, x, ta=False):
    hi, mid, lo = _split3(x)
    return _bdot(mask, hi, ta=ta) + (_bdot(mask, mid, ta=ta) + _bdot(mask, lo, ta=ta))


def _transpose_by_identity(x):
    r = x.shape[0]
    eye = (lax.broadcasted_iota(jnp.int32, (r, r), 0) == lax.broadcasted_iota(jnp.int32, (r, r), 1)).astype(BF16)
    hi, mid, lo = _split3(x)
    return _bdot(hi, eye, ta=True) + (_bdot(mid, eye, ta=True) + _bdot(lo, eye, ta=True))


def _dot22(a, b, ta=False, tb=False):
    ah, al = _split2(a)
    bh, bl = _split2(b)
    return _bdot(ah, bh, ta, tb) + (_bdot(ah, bl, ta, tb) + _bdot(al, bh, ta, tb))


def _chunk_gates(bg, heads):
    n = CHUNK
    row = lax.broadcasted_iota(jnp.int32, (n, n), 0)
    col = lax.broadcasted_iota(jnp.int32, (n, n), 1)
    lane = lax.broadcasted_iota(jnp.int32, bg.shape, 1)
    graw = jnp.where(jnp.logical_and(lane >= heads, lane < 2 * heads), bg, 0.0)
    gc = _dot_mask((row >= col).astype(BF16), graw)
    return gc, _transpose_by_identity(gc)


def _in_lockstep(generators):
    results = [None] * len(generators)
    live = list(enumerate(generators))
    while live:
        still = []
        for i, gen in live:
            try:
                next(gen)
                still.append((i, gen))
            except StopIteration as stop:
                results[i] = stop.value
        live = still
    return results


def _chunk_local(q, k, v, beta, gc, grow, solved=None):
    c = _Chunk()
    n = CHUNK
    row = lax.broadcasted_iota(jnp.int32, (n, n), 0)
    col = lax.broadcasted_iota(jnp.int32, (n, n), 1)
    c.tri = row >= col
    c.strict = row > col
    eye = row == col
    c.gcb = jnp.broadcast_to(gc, (n, HEAD_DIM))
    c.decay = jnp.where(c.tri, jnp.exp(jnp.where(c.tri, gc - grow, 0.0)), 0.0)
    c.eg = jnp.exp(c.gcb)
    glast = c.gcb[n - 1:n, :]
    c.egl = jnp.exp(glast)
    c.ekl = jnp.exp(glast - c.gcb)
    c.beta = beta
    c.q = q * (HEAD_DIM ** -0.5)
    c.k = k
    c.v = v
    c.kb = k * beta
    c.vb = v * beta
    c.kg = c.kb * c.eg
    both = _bdot(jnp.concatenate([c.kb, c.q], axis=0), k, tb=True)
    yield
    c.L = jnp.where(c.strict, both[:n] * c.decay, 0.0)
    c.A = jnp.where(c.tri, both[n:] * c.decay, 0.0)
    if solved is None:
        x = -c.L
        tinv = eye.astype(F32) + x
        p = _dot22(x, x)
        yield
        for _ in range(int(math.log2(n)) - 2):
            both = _dot22(jnp.concatenate([p, tinv], axis=0), p)
            yield
            p, tinv = both[:n], tinv + both[n:]
        c.T = tinv + _dot22(tinv, p)
        yield
        uw = _dot22(c.T, jnp.concatenate([c.vb, c.kg], axis=1))
        yield
        c.u, c.w = uw[:, :HEAD_DIM], uw[:, HEAD_DIM:]
    else:
        c.T, c.u, c.w = solved
    c.qg = c.q * c.eg
    c.kdec = k * c.ekl
    return c


def _gdn_core(qkv, bg, heads):
    t = qkv.shape[0]
    nchunk = t // CHUNK

    gw = heads * HEAD_DIM

    def body(qkv_ref, bg_ref, o_ref, s_ref, t_ref, u_ref, w_ref, state):
        @pl.when(pl.program_id(0) == 0)
        def _():
            state[...] = jnp.zeros_like(state)

        bg_v = bg_ref[...]
        gc_all, gc_rows = _chunk_gates(bg_v, heads)
        def one_head(h):
            col = lambda s: pl.ds(s * gw + h * HEAD_DIM, HEAD_DIM)
            c = yield from _chunk_local(qkv_ref[:, col(0)], qkv_ref[:, col(1)], qkv_ref[:, col(2)], bg_v[:, h:h + 1],
                                        gc_all[:, heads + h:heads + h + 1], gc_rows[heads + h:heads + h + 1, :])
            s0 = state[h]
            v_new = c.u - _bdot(c.w, s0)
            yield
            o = _bdot(c.qg, s0) + _bdot(c.A, v_new)
            return s0, o, s0 * c.egl + _bdot(c.kdec, v_new, ta=True), c

        results = _in_lockstep([one_head(h) for h in range(heads)])
        for h, (s0, o, s1, c) in enumerate(results):
            lanes = pl.ds(h * HEAD_DIM, HEAD_DIM)
            s_ref[h, 0] = s0
            o_ref[:, lanes] = o
            t_ref[:, lanes] = jnp.concatenate([c.T, jnp.zeros((CHUNK, HEAD_DIM - CHUNK), F32)], axis=1)
            u_ref[:, lanes] = c.u
            w_ref[:, lanes] = c.w
            state[h] = s1

    return pl.pallas_call(
        body, name="gdn_core", grid=(nchunk,),
        in_specs=[pl.BlockSpec((CHUNK, 3 * gw), lambda n: (n, 0)), pl.BlockSpec((CHUNK, HEAD_DIM), lambda n: (n, 0))],
        out_specs=[pl.BlockSpec((CHUNK, gw), lambda n: (n, 0)),
                   pl.BlockSpec((heads, 1, HEAD_DIM, HEAD_DIM), lambda n: (0, n, 0, 0))]
        + [pl.BlockSpec((CHUNK, gw), lambda n: (n, 0))] * 3,
        out_shape=[jax.ShapeDtypeStruct((t, gw), F32),
                   jax.ShapeDtypeStruct((heads, nchunk, HEAD_DIM, HEAD_DIM), F32)]
        + [jax.ShapeDtypeStruct((t, gw), F32)] * 3,
        scratch_shapes=[pltpu.VMEM((heads, HEAD_DIM, HEAD_DIM), F32)],
        compiler_params=_params("arbitrary"),
    )(qkv, bg)


def _gdn_core_backward(qkv, bg, states, solved, do, heads):
    t = qkv.shape[0]
    nchunk = t // CHUNK
    n = CHUNK

    def one_head(chunk_local, s0, d_out, ds1):
        c = yield from chunk_local
        v_new = c.u - _bdot(c.w, s0)
        dqg = _bdot(d_out, s0, tb=True)
        ds0 = _bdot(c.qg, d_out, ta=True) + ds1 * c.egl
        dv_new = _bdot(c.A, d_out, ta=True) + _bdot(c.kdec, ds1)
        yield
        dA = jnp.where(c.tri, _bdot(d_out, v_new, tb=True), 0.0)
        dkdec = _bdot(v_new, ds1, tb=True)
        dgl = jnp.sum(jnp.sum(ds1 * s0, axis=1, keepdims=True), axis=0, keepdims=True) * c.egl
        dw = -_bdot(dv_new, s0, tb=True)
        ds0 = ds0 - _bdot(c.w, dv_new, ta=True)
        yield
        both = _dot22(c.T, jnp.concatenate([dv_new, dw], axis=1), ta=True)
        yield
        dvb, dkg = both[:, :HEAD_DIM], both[:, HEAD_DIM:]
        dL = jnp.where(c.strict, -(_bdot(dvb, c.u, tb=True) + _bdot(dkg, c.w, tb=True)), 0.0)
        yield
        dm1 = dL * c.decay
        dkb = _bdot(dm1, c.k) + dkg * c.eg
        dk = _bdot(dm1, c.kb, ta=True)
        dm2 = dA * c.decay
        dq = _bdot(dm2, c.k) + dqg * c.eg
        dk = dk + _bdot(dm2, c.q, ta=True) + dkdec * c.ekl + dkb * c.beta
        pm = dL * c.L + dA * c.A
        ones = jnp.ones((n, HEAD_DIM), BF16)
        pm_hi, pm_lo = _split2(pm)
        colsum = _bdot(pm_hi, ones, ta=True) + _bdot(pm_lo, ones, ta=True)
        tk_ = jnp.sum(dkdec * c.kdec, axis=1, keepdims=True)
        dgc = (jnp.sum(pm, axis=1, keepdims=True) - colsum
               + jnp.sum(dqg * c.qg, axis=1, keepdims=True)
               - tk_
               + jnp.sum(dkg * c.kg, axis=1, keepdims=True))
        dgl = dgl + jnp.sum(tk_, axis=0, keepdims=True)
        rowi = lax.broadcasted_iota(jnp.int32, (n, HEAD_DIM), 0)
        dgc = dgc + jnp.where(rowi == n - 1, dgl, 0.0)
        dbeta = jnp.sum(dkb * c.k, axis=1, keepdims=True) + jnp.sum(dvb * c.v, axis=1, keepdims=True)
        return dq * (HEAD_DIM ** -0.5), dk, dvb * c.beta, dbeta, dgc, ds0

    gw = heads * HEAD_DIM

    def body(qkv_ref, bg_ref, s_ref, t_ref, u_ref, w_ref, do_ref, dqkv_ref, dbg_ref, dstate):
        @pl.when(pl.program_id(0) == 0)
        def _():
            dstate[...] = jnp.zeros_like(dstate)

        bg_v = bg_ref[...]
        gc_all, gc_rows = _chunk_gates(bg_v, heads)
        lane = lax.broadcasted_iota(jnp.int32, (n, HEAD_DIM), 1)
        dgates = jnp.zeros((n, HEAD_DIM), F32)
        chains = []
        for h in range(heads):
            col = lambda s: pl.ds(s * gw + h * HEAD_DIM, HEAD_DIM)
            lanes = pl.ds(h * HEAD_DIM, HEAD_DIM)
            c = _chunk_local(qkv_ref[:, col(0)], qkv_ref[:, col(1)], qkv_ref[:, col(2)], bg_v[:, h:h + 1],
                             gc_all[:, heads + h:heads + h + 1], gc_rows[heads + h:heads + h + 1, :],
                             (t_ref[:, pl.ds(h * HEAD_DIM, CHUNK)], u_ref[:, lanes], w_ref[:, lanes]))
            chains.append(one_head(c, s_ref[h, 0], do_ref[:, pl.ds(h * HEAD_DIM, HEAD_DIM)], dstate[h]))
        results = _in_lockstep(chains)
        for h, (dq, dk, dv, dbeta, dgc, ds0) in enumerate(results):
            dgates = jnp.where(lane == h, dbeta, jnp.where(lane == heads + h, dgc, dgates))
        for h, (dq, dk, dv, dbeta, dgc, ds0) in enumerate(results):
            dqkv_ref[:, pl.ds(h * HEAD_DIM, HEAD_DIM)] = dq
            dqkv_ref[:, pl.ds(gw + h * HEAD_DIM, HEAD_DIM)] = dk
            dqkv_ref[:, pl.ds(2 * gw + h * HEAD_DIM, HEAD_DIM)] = dv
            dstate[h] = ds0
        row = lax.broadcasted_iota(jnp.int32, (n, n), 0)
        colm = lax.broadcasted_iota(jnp.int32, (n, n), 1)
        draw = _dot_mask((row >= colm).astype(BF16), dgates, ta=True)
        dbg_ref[...] = jnp.where(lane < heads, dgates, draw)

    last = nchunk - 1
    return pl.pallas_call(
        body, name="gdn_core_bwd", grid=(nchunk,),
        in_specs=[pl.BlockSpec((CHUNK, 3 * gw), lambda i: (last - i, 0)),
                  pl.BlockSpec((CHUNK, HEAD_DIM), lambda i: (last - i, 0)),
                  pl.BlockSpec((heads, 1, HEAD_DIM, HEAD_DIM), lambda i: (0, last - i, 0, 0))]
        + [pl.BlockSpec((CHUNK, gw), lambda i: (last - i, 0))] * 4,
        out_specs=[pl.BlockSpec((CHUNK, 3 * gw), lambda i: (last - i, 0)),
                   pl.BlockSpec((CHUNK, HEAD_DIM), lambda i: (last - i, 0))],
        out_shape=[jax.ShapeDtypeStruct((t, 3 * gw), F32), jax.ShapeDtypeStruct((t, HEAD_DIM), F32)],
        scratch_shapes=[pltpu.VMEM((heads, HEAD_DIM, HEAD_DIM), F32)],
        compiler_params=_params("arbitrary"),
    )(qkv, bg, states, *solved, do)


def _gdn_post(o, proj, z_col0, norm_w, heads, tt):
    t = o.shape[0]
    zb = z_col0 // HEAD_DIM

    def body(o_ref, z_ref, w_ref, out_ref):
        ov = o_ref[...]
        z = z_ref[...]
        rms = lax.rsqrt(jnp.mean(ov * ov, axis=-1, keepdims=True) + NORM_EPS)
        out_ref[...] = (ov * rms * w_ref[...] * (z * _sigmoid(z))).astype(BF16)

    return pl.pallas_call(
        body, name="gdn_post", grid=(t // tt, heads),
        in_specs=[pl.BlockSpec((tt, HEAD_DIM), lambda i, h: (i, h)),
                  pl.BlockSpec((tt, HEAD_DIM), lambda i, h: (i, zb + h)),
                  pl.BlockSpec((1, HEAD_DIM), lambda i, h: (0, 0))],
        out_specs=pl.BlockSpec((tt, HEAD_DIM), lambda i, h: (i, h)),
        out_shape=jax.ShapeDtypeStruct((t, heads * HEAD_DIM), BF16),
        compiler_params=_params("parallel", "parallel"),
    )(o, proj, norm_w)


def _gdn_post_backward(dcat, o, proj, z_col0, norm_w, heads, tt):
    t = o.shape[0]
    zb = z_col0 // HEAD_DIM

    def body(d_ref, o_ref, z_ref, w_ref, do_ref, dz_ref, dw_ref):
        d = d_ref[...]
        ov = o_ref[...]
        z = z_ref[...]
        w = w_ref[...]
        rms = lax.rsqrt(jnp.mean(ov * ov, axis=-1, keepdims=True) + NORM_EPS)
        ohat = ov * rms
        sg = _sigmoid(z)
        gate = z * sg
        dz_ref[...] = (d * ohat * w * (sg * (1.0 + z * (1.0 - sg)))).astype(BF16)
        don = d * gate
        dohat = don * w
        do_ref[...] = rms * (dohat - ohat * jnp.mean(dohat * ohat, axis=-1, keepdims=True))
        dw = jnp.sum(don * ohat, axis=0, keepdims=True)
        first = jnp.logical_and(pl.program_id(0) == 0, pl.program_id(1) == 0)

        @pl.when(first)
        def _():
            dw_ref[...] = dw

        @pl.when(jnp.logical_not(first))
        def _():
            dw_ref[...] += dw

    blk = pl.BlockSpec((tt, HEAD_DIM), lambda i, h: (i, h))
    return pl.pallas_call(
        body, name="gdn_post_bwd", grid=(t // tt, heads),
        in_specs=[blk, blk, pl.BlockSpec((tt, HEAD_DIM), lambda i, h: (i, zb + h)),
                  pl.BlockSpec((1, HEAD_DIM), lambda i, h: (0, 0))],
        out_specs=[blk, blk, pl.BlockSpec((1, HEAD_DIM), lambda i, h: (0, 0))],
        out_shape=[jax.ShapeDtypeStruct((t, heads * HEAD_DIM), F32),
                   jax.ShapeDtypeStruct((t, heads * HEAD_DIM), BF16),
                   jax.ShapeDtypeStruct((1, HEAD_DIM), F32)],
        compiler_params=_params("arbitrary", "arbitrary"),
    )(dcat, o, proj, norm_w)


def _pool_select(levels, group):
    out = levels[-1]
    for gi in range(len(levels) - 2, -1, -1):
        out = jnp.where(group == gi, levels[gi], out)
    return out


def _pool_counts(t, width, group):
    pos = lax.broadcasted_iota(jnp.int32, (t, width), 0)
    win = jnp.left_shift(2, group)
    return jnp.minimum(pos + 1, win).astype(F32)


def _pooled(p, group):
    levels, s, step = [], p, 1
    for _ in POOL_WINDOWS:
        s = s + _shift_down(s, step)
        levels.append(s)
        step *= 2
    cnt = _pool_counts(p.shape[0], p.shape[1], group)
    return _pool_select(levels, group) / cnt - p, cnt


def _pool_forward(proj, p_col0, pool_w, pool_scale):
    t = proj.shape[0]
    groups, cg, _ = pool_w.shape
    pb = p_col0 // cg

    def body(p_ref, w_ref, s_ref, o_ref):
        pooled, _ = _pooled(p_ref[...], pl.program_id(0))
        o_ref[...] = (_bdot(pooled, w_ref[0]) * s_ref[...]).astype(BF16)

    return pl.pallas_call(
        body, name="pool_fwd", grid=(groups,),
        in_specs=[pl.BlockSpec((t, cg), lambda g: (0, pb + g)), pl.BlockSpec((1, cg, cg), lambda g: (g, 0, 0)),
                  pl.BlockSpec((1, cg), lambda g: (0, g))],
        out_specs=pl.BlockSpec((t, cg), lambda g: (0, g)),
        out_shape=jax.ShapeDtypeStruct((t, groups * cg), BF16),
        compiler_params=_params("parallel"),
    )(proj, pool_w, pool_scale)


def _pool_backward(dcat, d_col0, proj, p_col0, pool_w, pool_scale):
    t = proj.shape[0]
    groups, cg, _ = pool_w.shape
    pb = p_col0 // cg
    db = d_col0 // cg

    def body(d_ref, p_ref, w_ref, s_ref, dp_ref, dw_ref, ds_ref):
        group = pl.program_id(0)
        pooled, cnt = _pooled(p_ref[...], group)
        w = w_ref[0]
        d = d_ref[...]
        mixed = _bdot(pooled, w)
        ds_ref[...] = jnp.sum(d * mixed, axis=0, keepdims=True)
        dmixed = d * s_ref[...]
        dw_ref[0] = _bdot(pooled, dmixed, ta=True)
        dpooled = _bdot(dmixed, w, tb=True)
        levels, s, step = [], dpooled / cnt, 1
        for _ in POOL_WINDOWS:
            s = s + _shift_up(s, step)
            levels.append(s)
            step *= 2
        dp_ref[...] = (_pool_select(levels, group) - dpooled).astype(BF16)

    return pl.pallas_call(
        body, name="pool_bwd", grid=(groups,),
        in_specs=[pl.BlockSpec((t, cg), lambda g: (0, db + g)), pl.BlockSpec((t, cg), lambda g: (0, pb + g)),
                  pl.BlockSpec((1, cg, cg), lambda g: (g, 0, 0)), pl.BlockSpec((1, cg), lambda g: (0, g))],
        out_specs=[pl.BlockSpec((t, cg), lambda g: (0, g)), pl.BlockSpec((1, cg, cg), lambda g: (g, 0, 0)),
                   pl.BlockSpec((1, cg), lambda g: (0, g))],
        out_shape=[jax.ShapeDtypeStruct((t, groups * cg), BF16), jax.ShapeDtypeStruct((groups, cg, cg), F32),
                   jax.ShapeDtypeStruct((1, groups * cg), F32)],
        compiler_params=_params("parallel"),
    )(dcat, proj, pool_w, pool_scale)


def _attention(q, k, v, tq):
    t, d = q.shape
    m = k.shape[0]
    dh = d // XATTN_HEADS
    scale = dh ** -0.5

    def body(q_ref, k_ref, v_ref, o_ref):
        s = _bdot(q_ref[...], k_ref[...], tb=True) * scale
        s = s - jnp.max(s, axis=-1, keepdims=True)
        e = jnp.exp(s)
        p = e / jnp.sum(e, axis=-1, keepdims=True)
        o_ref[...] = _bdot(p, v_ref[...]).astype(BF16)

    return pl.pallas_call(
        body, name="xattn_fwd", grid=(XATTN_HEADS, t // tq),
        in_specs=[pl.BlockSpec((tq, dh), lambda h, i: (i, h)), pl.BlockSpec((m, dh), lambda h, i: (0, h)),
                  pl.BlockSpec((m, dh), lambda h, i: (0, h))],
        out_specs=pl.BlockSpec((tq, dh), lambda h, i: (i, h)),
        out_shape=jax.ShapeDtypeStruct((t, d), BF16),
        compiler_params=_params("parallel", "parallel"),
    )(q, k, v)


def _attention_backward(q, k, v, do, tq):
    t, d = q.shape
    m = k.shape[0]
    dh = d // XATTN_HEADS
    scale = dh ** -0.5

    def body(q_ref, k_ref, v_ref, do_ref, dq_ref, dk_ref, dv_ref, dk_acc, dv_acc):
        i = pl.program_id(1)
        qv, kv, vv, dov = q_ref[...], k_ref[...], v_ref[...], do_ref[...]
        s = _bdot(qv, kv, tb=True) * scale
        s = s - jnp.max(s, axis=-1, keepdims=True)
        e = jnp.exp(s)
        p = e / jnp.sum(e, axis=-1, keepdims=True)
        dp = _bdot(dov, vv, tb=True)
        ds = p * (dp - jnp.sum(dp * p, axis=-1, keepdims=True)) * scale
        dq_ref[...] = _bdot(ds, kv).astype(BF16)
        dv_part = _bdot(p, dov, ta=True)
        dk_part = _bdot(ds, qv, ta=True)

        @pl.when(i == 0)
        def _():
            dk_acc[...] = dk_part
            dv_acc[...] = dv_part

        @pl.when(i > 0)
        def _():
            dk_acc[...] += dk_part
            dv_acc[...] += dv_part

        @pl.when(i == pl.num_programs(1) - 1)
        def _():
            dk_ref[...] = dk_acc[...].astype(BF16)
            dv_ref[...] = dv_acc[...].astype(BF16)

    qblk = pl.BlockSpec((tq, dh), lambda h, i: (i, h))
    kblk = pl.BlockSpec((m, dh), lambda h, i: (0, h))
    return pl.pallas_call(
        body, name="xattn_bwd", grid=(XATTN_HEADS, t // tq),
        in_specs=[qblk, kblk, kblk, qblk],
        out_specs=[qblk, kblk, kblk],
        out_shape=[jax.ShapeDtypeStruct((t, d), BF16), jax.ShapeDtypeStruct((m, d), BF16),
                   jax.ShapeDtypeStruct((m, d), BF16)],
        scratch_shapes=[pltpu.VMEM((m, dh), F32), pltpu.VMEM((m, dh), F32)],
        compiler_params=_params("parallel", "arbitrary"),
    )(q, k, v, do)


def _ln_backward_rows(name, dmain, dres, xhat, rstd, gamma, tm):
    t, d = xhat.shape

    def body(m_ref, r_ref, x_ref, s_ref, g_ref, du_ref, dub_ref, dg_ref, db_ref):
        du, dg, db = _ln_backward_math(m_ref[...] + ALPHA * r_ref[...], x_ref[...], s_ref[...], g_ref[...])
        du_ref[...] = du
        dub_ref[...] = du.astype(BF16)
        first = pl.program_id(0) == 0

        @pl.when(first)
        def _():
            dg_ref[...] = dg
            db_ref[...] = db

        @pl.when(jnp.logical_not(first))
        def _():
            dg_ref[...] += dg
            db_ref[...] += db

    row = pl.BlockSpec((tm, d), lambda i: (i, 0))
    vec = pl.BlockSpec((1, d), lambda i: (0, 0))
    return pl.pallas_call(
        body, name=name, grid=(t // tm,),
        in_specs=[row, row, row, pl.BlockSpec((tm, 1), lambda i: (i, 0)), vec],
        out_specs=[row, row, vec, vec],
        out_shape=[jax.ShapeDtypeStruct((t, d), F32), jax.ShapeDtypeStruct((t, d), BF16),
                   jax.ShapeDtypeStruct((1, d), F32), jax.ShapeDtypeStruct((1, d), F32)],
        compiler_params=_params("arbitrary"),
    )(dmain, dres, xhat, rstd, gamma)


def _loss_and_ln_backward(xhat, rstd, gamma, beta, target, tm):
    t, d = xhat.shape

    def body(x_ref, r_ref, g_ref, b_ref, t_ref, du_ref, dub_ref, dg_ref, db_ref, loss_ref):
        xh = x_ref[...]
        g = g_ref[...]
        diff = xh * g + b_ref[...] - t_ref[...]
        part = jnp.sum(jnp.sum(diff * diff, axis=1, keepdims=True), axis=0, keepdims=True) * (0.5 / d)
        dy = diff * (1.0 / d)
        du, dg, db = _ln_backward_math(dy, xh, r_ref[...], g)
        du_ref[...] = du
        dub_ref[...] = du.astype(BF16)
        lossrow = jnp.broadcast_to(part, (1, HEAD_DIM))
        first = pl.program_id(0) == 0

        @pl.when(first)
        def _():
            dg_ref[...] = dg
            db_ref[...] = db
            loss_ref[...] = lossrow

        @pl.when(jnp.logical_not(first))
        def _():
            dg_ref[...] += dg
            db_ref[...] += db
            loss_ref[...] += lossrow

    row = pl.BlockSpec((tm, d), lambda i: (i, 0))
    vec = pl.BlockSpec((1, d), lambda i: (0, 0))
    return pl.pallas_call(
        body, name="loss_ln3_bwd", grid=(t // tm,),
        in_specs=[row, pl.BlockSpec((tm, 1), lambda i: (i, 0)), vec, vec, row],
        out_specs=[row, row, vec, vec, pl.BlockSpec((1, HEAD_DIM), lambda i: (0, 0))],
        out_shape=[jax.ShapeDtypeStruct((t, d), F32), jax.ShapeDtypeStruct((t, d), BF16),
                   jax.ShapeDtypeStruct((1, d), F32), jax.ShapeDtypeStruct((1, d), F32),
                   jax.ShapeDtypeStruct((1, HEAD_DIM), F32)],
        compiler_params=_params("arbitrary"),
    )(xhat, rstd, gamma, beta, target)


def _after(token, a):
    return a if token is None else a + token[:1, :1].astype(a.dtype)


def _pick(n, prefs):
    for p in prefs:
        if n % p == 0:
            return p
    return n


def _local_step(x, mem, target, w, x_bf=None):
    t, d = x.shape
    heads = w["a_log"].shape[1]
    gw = heads * HEAD_DIM
    groups, cg, _ = w["pool_w"].shape
    pw = groups * cg
    n_main = 4 * gw + pw
    in_cols = n_main + 2 * heads
    s_in = w["w_in_t"].shape[0]

    tm = _pick(t, (512, 256, 128))
    tm_ln = _pick(t, (256, 128))
    tm_big = _pick(t, (1024, 512, 256, 128))
    tk = _pick(d, K_STEPS)

    k_pad = -(-in_cols // HEAD_DIM) * HEAD_DIM
    w_in_t = jnp.pad(w["w_in_t"].reshape(in_cols, d), ((0, k_pad - in_cols), (0, 0)))
    w_p_t = w_in_t[4 * gw + 2 * heads:in_cols]
    w_ba_t = jnp.pad(w_in_t[4 * gw:4 * gw + 2 * heads], ((0, HEAD_DIM - 2 * heads), (0, 0)))
    x_bf = x.astype(BF16) if x_bf is None else x_bf
    mem_bf = mem.astype(BF16)

    tn_d = _pick(d, (1024, 512, 256, 128))
    proj = _plain("proj_main", x_bf, w_in_t, tb=True, n_used=4 * gw, tm=tm_big, tn=_pick(4 * gw, (1024, 512, 256, 128)),
                  tk=tk, out_dtype=F32)
    pproj = _plain("proj_pool", x_bf, w_p_t, tb=True, tm=tm_big, tn=_pick(pw, (1024, 512, 256, 128)), tk=tk, out_dtype=F32)
    ea, dtb = _gate_vectors(w["a_log"], w["dt_bias"], heads)
    vec128 = lambda i, j: (0, 0)
    ba, bg = _matmul(
        "proj_gates", x_bf, w_ba_t, tb=True, tm=tm, tn=HEAD_DIM, tk=tk,
        extra=[(ea, (1, HEAD_DIM), vec128), (dtb, (1, HEAD_DIM), vec128)],
        outs=[(jax.ShapeDtypeStruct((t, HEAD_DIM), F32), (tm, HEAD_DIM), _tile)] * 2,
        epilogue=_gates_epilogue(heads))
    qkv = _gdn_pre(proj, w["conv_w"], heads)
    o_gdn, states, *solved = _gdn_core(qkv, bg, heads)
    cat_g = _gdn_post(o_gdn, proj, 3 * gw, w["gdn_norm_w"], heads, tm)
    token = yield ("pass", 1, cat_g)
    cat_p = _pool_forward(pproj, 0, w["pool_w"], _after(token, w["pool_scale"]))
    cat = jnp.concatenate([cat_g, cat_p], axis=1)
    w = {**w, **(yield ("weights", 1, cat))}
    h1, h1_bf, xhat1, rstd1 = _ln_forward("mix_ln1", cat, w["w_out"], x, w["ln1_g"], w["ln1_b"], tm=tm_ln, tk=tk)

    h1_bf = _after((yield ("relay", None, h1_bf)), h1_bf)
    q = _plain("xattn_q", h1_bf, w["xq_w"], tm=tm, tn=tn_d, tk=tk, out_dtype=BF16)
    mlen = mem.shape[0]
    tm_mem = _pick(mlen, (256, 128))
    k = _plain("xattn_k", mem_bf, w["xk_w"], tm=tm_mem, tn=tn_d, tk=tk, out_dtype=BF16)
    v = _plain("xattn_v", mem_bf, w["xv_w"], tm=tm_mem, tn=tn_d, tk=tk, out_dtype=BF16)
    att = _attention(q, k, v, tm)
    h2, h2_bf, xhat2, rstd2 = _ln_forward("xo_ln2", att, w["xo_w"], h1, w["ln2_g"], w["ln2_b"], tm=tm_ln, tk=tk)

    w = {**w, **(yield ("weights", 2, h2_bf))}
    s_up = w["w_up3"].shape[0]
    ff = s_up * w["w_up3"].shape[2]
    tn_f = _pick(ff // s_up, (1024, 512, 256, 128))

    def up_epi(acc, ex, out, i):
        r = jnp.maximum(acc, 0.0)
        out[0][...] = (r * r).astype(BF16)
        out[1][...] = (2.0 * r).astype(BF16)

    act, act_grad = _matmul(
        "mlp_up", h2_bf, w["w_up3"], b_blocks=s_up, tm=tm_big, tn=tn_f, tk=tk,
        outs=[(jax.ShapeDtypeStruct((t, ff), BF16), (tm_big, tn_f), _tile)] * 2, epilogue=up_epi)
    w = {**w, **(yield ("weights", 3, act))}
    tk_f = _pick(ff, K_STEPS)
    xhat3, rstd3 = _ln_forward("down_ln3", act, w["w_down"], h2, w["ln3_g"], w["ln3_b"], tm=tm, tk=tk_f, want_h=False)

    grads = {}
    du3, du3_bf, grads["ln3_g"], grads["ln3_b"], loss = _loss_and_ln_backward(
        xhat3, rstd3, w["ln3_g"], w["ln3_b"], target, tm_ln)

    def dup_epi(acc, ex, out, i):
        out[0][...] = (acc * ex[0][...].astype(F32)).astype(BF16)

    dup = _matmul(
        "mlp_down_dx", du3_bf, w["w_down"], tb=True, tm=tm_big, tn=tn_f, tk=tk,
        extra=[(act_grad, (tm_big, tn_f), _tile)],
        outs=[(jax.ShapeDtypeStruct((t, ff), BF16), (tm_big, tn_f), _tile)], epilogue=dup_epi)[0]
    tk_t = _pick(t, K_STEPS)
    tm_w = _pick(d, (512, 256, 128))
    grads["w_down"] = _plain("mlp_down_dw", act, du3_bf, ta=True, tm=_pick(ff, (512, 256, 128)), tn=d, tk=tk_t,
                             out_dtype=F32)
    grads["w_up3"] = _plain("mlp_up_dw", h2_bf, dup, ta=True, tm=tm_w, tn=ff // s_up, tk=tk_t, out_dtype=F32, out3=s_up,
                            n_outer=True)
    token = yield ("grads", 0, {n: grads.pop(n) for n in ("w_down", "w_up3")})
    dh2 = _plain("mlp_up_dx", dup, w["w_up3"], tb=True, b_blocks=s_up, tm=tm_big, tn=tn_d,
                 tk=_pick(ff // s_up, K_STEPS), out_dtype=F32)
    du2, du2_bf, grads["ln2_g"], grads["ln2_b"] = _ln_backward_rows(
        "ln2_bwd", dh2, du3, xhat2, rstd2, _after(token, w["ln2_g"]), tm_ln)
    token = yield ("poll", 0, du2_bf)

    grads["xo_w"] = _plain("xo_dw", att, du2_bf, ta=True, tm=tm_w, tn=d, tk=tk_t, out_dtype=F32)
    datt = _plain("xo_dx", du2_bf, w["xo_w"], tb=True, tm=tm, tn=tn_d, tk=tk, out_dtype=BF16)
    dq, dk, dv = _attention_backward(q, k, v, datt, tm)
    tk_m = _pick(mlen, (256, 128))
    grads["xq_w"] = _plain("xq_dw", h1_bf, dq, ta=True, tm=tm_w, tn=d, tk=tk_t, out_dtype=F32)
    grads["xk_w"] = _plain("xk_dw", mem_bf, dk, ta=True, tm=tm_w, tn=tn_d, tk=tk_m, out_dtype=F32)
    grads["xv_w"] = _plain("xv_dw", mem_bf, dv, ta=True, tm=tm_w, tn=tn_d, tk=tk_m, out_dtype=F32)
    du1, du1_bf, grads["ln1_g"], grads["ln1_b"] = _ln_backward(
        "xq_dx_ln1", dq, w["xq_w"], du2, xhat1, rstd1, _after(token, w["ln1_g"]), tm=tm_ln, tk=tk)

    grads["w_out"] = _plain("out_dw", cat, du1_bf, ta=True, tm=tm_w, tn=d, tk=tk_t, out_dtype=F32)
    token = yield ("grads", 1, {n: grads.pop(n) for n in ("xo_w", "xq_w", "xk_w", "xv_w", "w_out")})
    dcat = _plain("out_dx", du1_bf, w["w_out"], tb=True, tm=tm, tn=tn_d, tk=tk, out_dtype=F32)
    dp, grads["pool_w"], grads["pool_scale"] = _pool_backward(dcat, gw, pproj, 0, w["pool_w"],
                                                              _after(token, w["pool_scale"]))
    do_gdn, dz, grads["gdn_norm_w"] = _gdn_post_backward(dcat, o_gdn, proj, 3 * gw, _after(token, w["gdn_norm_w"]),
                                                         heads, tm)
    dqkv, dbg = _gdn_core_backward(qkv, bg, states, solved, do_gdn, heads)
    token = yield ("poll", 1, dqkv)
    dqkv_pre, grads["conv_w"] = _gdn_pre_backward(proj, _after(token, w["conv_w"]), dqkv, heads)
    dba, dalog_row, ddt_row = _gates_backward(ba, bg, dbg, ea, dtb, heads)
    grads["a_log"] = dalog_row[:, heads:2 * heads]
    grads["dt_bias"] = ddt_row[:, heads:2 * heads]

    dproj = jnp.concatenate([dqkv_pre, dz, dba[:, :2 * heads], dp, jnp.zeros((t, k_pad - in_cols), BF16)], axis=1)
    dw_in_t = _plain("proj_dw", dproj, x_bf, ta=True, tm=_pick(k_pad, (512, 256, 128)), tn=d, tk=tk_t, out_dtype=F32,
                     m_kept=in_cols)
    grads["w_in_t"] = dw_in_t.reshape(s_in, in_cols // s_in, d)

    def dx_epi(acc, ex, out, i):
        out[0][...] = acc + ALPHA * ex[0][...] + ex[1][:1, :1]

    token = yield ("grads", 2, {n: grads.pop(n) for n in ("w_in_t", "pool_w")})
    token = jnp.zeros((8, HEAD_DIM), F32) if token is None else token
    grad_x = _matmul(
        "proj_dx", dproj, w_in_t, tm=tm, tn=tn_d, tk=k_pad,
        extra=[(du1, (tm, tn_d), _tile), (token, (8, HEAD_DIM), lambda i, j: (0, 0))],
        outs=[(jax.ShapeDtypeStruct((t, d), F32), (tm, tn_d), _tile)], epilogue=dx_epi)[0]
    yield ("poll", 2, grad_x)
    return loss, grad_x, grads


def _adamw(name, w, g, m, v):
    r, c = w.shape
    if r % 8 == 0:
        tr = _pick(r, (256, 128, 64, 32, 16, 8))
        blk, steps = pl.BlockSpec((tr, c), lambda i: (i, 0)), r // tr
    else:
        tc = _pick(c, (256, 128))
        blk, steps = pl.BlockSpec((r, tc), lambda i: (0, i)), c // tc
    c1 = 1.0 - ADAM_B1 ** ADAM_STEP
    c2 = 1.0 - ADAM_B2 ** ADAM_STEP

    def body(w_ref, g_ref, m_ref, v_ref, d_ref, mo_ref, vo_ref, go_ref):
        gv = g_ref[...]
        mn = ADAM_B1 * m_ref[...] + (1.0 - ADAM_B1) * gv
        vn = ADAM_B2 * v_ref[...] + (1.0 - ADAM_B2) * (gv * gv)
        d_ref[...] = -ADAM_LR * ((mn / c1) / (jnp.sqrt(vn / c2) + ADAM_EPS) + ADAM_WD * w_ref[...])
        mo_ref[...] = mn
        vo_ref[...] = vn
        go_ref[...] = gv

    return pl.pallas_call(
        body, name=name, grid=(steps,), in_specs=[blk] * 4, out_specs=[blk] * 4,
        out_shape=[jax.ShapeDtypeStruct((r, c), F32)] * 4,
        compiler_params=_params("parallel"),
    )(w, g, m, v)


def _place():
    x, y, c = lax.axis_index("x"), lax.axis_index("y"), lax.axis_index("c")
    chips = [(1 - x, y), (x, 1 - y), (1 - x, 1 - y)]
    return x, y, c, chips


HBM = pl.BlockSpec(memory_space=pltpu.HBM)


SEM = pl.BlockSpec(memory_space=pltpu.SEMAPHORE)
ANY = pl.BlockSpec(memory_space=pl.ANY)
EFFECT = pltpu.SideEffectType.DATAFLOW_SIDE_EFFECTING


def _in_hbm(a):
    return pltpu.with_memory_space_constraint(a, pltpu.HBM)


def _remote(src, dst, send_sem, recv_sem, to):
    return pltpu.make_async_remote_copy(src_ref=src, dst_ref=dst, send_sem=send_sem, recv_sem=recv_sem,
                                        device_id=to, device_id_type=MESH)


def _by_rows(rows):
    return rows % 32 == 0


def _half_shape(rows, cols):
    return (rows // 2, cols) if _by_rows(rows) else (rows, cols // 2)


def _half(ref, which, *lead):
    rows, cols = ref.shape[-2:]
    if _by_rows(rows):
        return ref.at[(*lead, pl.ds(which * (rows // 2), rows // 2))]
    return ref.at[(*lead, slice(None), pl.ds(which * (cols // 2), cols // 2))]


def _landed(lands, i, shard_index, which):
    return _half(lands[i], which, shard_index)


def _routes():
    x, y, c, _ = _place()
    first = (jnp.where(c == 0, 1 - x, x), jnp.where(c == 0, y, 1 - y))
    second = (jnp.where(c == 0, x, 1 - x), jnp.where(c == 0, 1 - y, y))
    return first, second, (1 - x, 1 - y)


def _shard_of(chip):
    return 2 * chip[0] + chip[1]


def _gather_start(name, shards, after, relayed=()):
    n = len(shards)
    lands = [lax.empty((N_SHARD,) + s.shape, s.dtype) for s in shards]

    def body(*refs):
        ins, zones = refs[:n], refs[n:2 * n]
        ici_send, ici_recv, own_send, own_recv = refs[2 * n + 1:2 * n + 5]
        token = refs[-1]
        x, y, c, chips = _place()
        me = 2 * x + y
        first, _, _ = _routes()
        for i in range(n):
            if i in relayed:
                _remote(_half(ins[i], c), _landed(zones, i, me, c), ici_send.at[3 * i], ici_recv.at[3 * i],
                        (*first, c)).start()
                continue
            for j, chip in enumerate(chips):
                _remote(_half(ins[i], c), _landed(zones, i, me, c), ici_send.at[3 * i + j],
                        ici_recv.at[3 * i + j], (*chip, c)).start()
        for i in range(n):
            _remote(ins[i], zones[i].at[me], own_send.at[i], own_recv.at[i], (x, y, 1 - c)).start()
        token[...] = jnp.zeros_like(token)

    dma = pltpu.SemaphoreType.DMA
    outs = pl.pallas_call(
        body, name=name,
        in_specs=[HBM] * (2 * n) + [ANY],
        out_shape=(dma((3 * n,)), dma((3 * n,)), dma((n,)), dma((n,)),
                   *[pltpu.HBM(a.shape, a.dtype) for a in shards + lands], jax.ShapeDtypeStruct((8, LANES), F32)),
        out_specs=(SEM, SEM, SEM, SEM, *[HBM] * (2 * n), pl.BlockSpec(memory_space=pltpu.VMEM)),
        input_output_aliases={k: 4 + k for k in range(2 * n)},
        compiler_params=pltpu.CompilerParams(has_side_effects=EFFECT),
    )(*[_in_hbm(a) for a in shards + lands], after)
    sems = dict(zip(("ici_send", "ici_recv", "own_send", "own_recv"), outs[:4]))
    return sems, list(outs[4:4 + n]), list(outs[4 + n:4 + 2 * n]), outs[-1]


def _gather_forward(name, idx, lands, sems, after):
    n = len(idx)

    def body(*refs):
        zones = refs[:n]
        ici_recv = refs[n]
        fwd_send, fwd_recv = refs[n + 2], refs[n + 3]
        x, y, c, chips = _place()
        for k, i in enumerate(idx):
            for j, chip in enumerate(chips):
                half = _landed(zones, k, 2 * chip[0] + chip[1], c)
                _remote(half, half, fwd_send.at[3 * k + j], ici_recv.at[3 * i + j], (*chip, c)).wait_recv()
                _remote(half, half, fwd_send.at[3 * k + j], fwd_recv.at[3 * k + j], (x, y, 1 - c)).start()
        refs[-1][...] = jnp.zeros_like(refs[-1])

    dma = pltpu.SemaphoreType.DMA
    outs = pl.pallas_call(
        body, name=name,
        in_specs=[HBM] * n + [SEM, ANY],
        out_shape=(dma((3 * n,)), dma((3 * n,)), *[pltpu.HBM(a.shape, a.dtype) for a in lands],
                   jax.ShapeDtypeStruct((8, LANES), F32)),
        out_specs=(SEM, SEM, *[HBM] * n, pl.BlockSpec(memory_space=pltpu.VMEM)),
        input_output_aliases={k: 2 + k for k in range(n)},
        compiler_params=pltpu.CompilerParams(has_side_effects=EFFECT),
    )(*lands, sems["ici_recv"], after)
    return (outs[0], outs[1]), list(outs[2:2 + n]), outs[-1]


def _gather_wait(name, idx, shards, lands, sems, fwd, after):
    n = len(idx)

    def body(*refs):
        ins, zones = refs[:n], refs[n:2 * n]
        ici_send, own_send, own_recv, fwd_send, fwd_recv = refs[2 * n:2 * n + 5]
        x, y, c, chips = _place()
        me = 2 * x + y
        for k, i in enumerate(idx):
            mine = _half(ins[k], c)
            for j, chip in enumerate(chips):
                theirs = 2 * chip[0] + chip[1]
                _remote(mine, _landed(zones, k, me, c), ici_send.at[3 * i + j], fwd_recv.at[3 * k + j],
                        (*chip, c)).wait_send()
                sent = _landed(zones, k, theirs, c)
                _remote(sent, sent, fwd_send.at[3 * k + j], fwd_recv.at[3 * k + j], (x, y, 1 - c)).wait_send()
                passed = _landed(zones, k, theirs, 1 - c)
                _remote(passed, passed, fwd_send.at[3 * k + j], fwd_recv.at[3 * k + j], (x, y, 1 - c)).wait_recv()
            own = _remote(ins[k], zones[k].at[me], own_send.at[i], own_recv.at[i], (x, y, 1 - c))
            own.wait_send()
            own.wait_recv()

    outs = pl.pallas_call(
        body, name=name,
        in_specs=[HBM] * (2 * n) + [SEM] * 5 + [ANY],
        out_shape=tuple(pltpu.HBM(a.shape, a.dtype) for a in lands),
        out_specs=tuple([HBM] * n),
        input_output_aliases={n + k: k for k in range(n)},
        compiler_params=pltpu.CompilerParams(has_side_effects=EFFECT),
    )(*shards, *lands, sems["ici_send"], sems["own_send"], sems["own_recv"], fwd[0], fwd[1], after)
    return list(outs)


def _gather_relay(name, idx, shards, lands, sems, after):
    n = len(idx)

    def body(*refs):
        ins, zones, ici_recv = refs[:n], refs[n:2 * n], refs[2 * n]
        relay_send, relay_recv, pass_send, pass_recv = refs[2 * n + 2:2 * n + 6]
        x, y, c, _ = _place()
        first, second, _ = _routes()
        for k, i in enumerate(idx):
            landed = _landed(zones, k, _shard_of(first), c)
            _remote(landed, landed, pass_send.at[k], ici_recv.at[3 * i], (*first, c)).wait_recv()
            _remote(_half(ins[k], c), _landed(zones, k, 2 * x + y, c), relay_send.at[2 * k], relay_recv.at[2 * k],
                    (*second, c)).start()
            _remote(landed, landed, relay_send.at[2 * k + 1], relay_recv.at[2 * k + 1], (*second, c)).start()
            _remote(landed, landed, pass_send.at[k], pass_recv.at[k], (x, y, 1 - c)).start()
        refs[-1][...] = jnp.zeros_like(refs[-1])

    dma = pltpu.SemaphoreType.DMA
    outs = pl.pallas_call(
        body, name=name,
        in_specs=[HBM] * (2 * n) + [SEM, ANY],
        out_shape=(dma((2 * n,)), dma((2 * n,)), dma((n,)), dma((n,)), *[pltpu.HBM(a.shape, a.dtype) for a in lands],
                   jax.ShapeDtypeStruct((8, LANES), F32)),
        out_specs=(SEM, SEM, SEM, SEM, *[HBM] * n, pl.BlockSpec(memory_space=pltpu.VMEM)),
        input_output_aliases={n + k: 4 + k for k in range(n)},
        compiler_params=pltpu.CompilerParams(has_side_effects=EFFECT),
    )(*shards, *lands, sems["ici_recv"], after)
    return outs[:4], list(outs[4:4 + n]), outs[-1]


def _gather_forward_relayed(name, ks, lands, relay, after):
    n = len(ks)

    def body(*refs):
        zones, relay_recv = refs[:n], refs[n]
        fwd_send, fwd_recv = refs[n + 2], refs[n + 3]
        x, y, c, _ = _place()
        _, second, diagonal = _routes()
        for p, k in enumerate(ks):
            for j, chip in enumerate((second, diagonal)):
                landed = _landed(zones, p, _shard_of(chip), c)
                _remote(landed, landed, fwd_send.at[2 * p + j], relay_recv.at[2 * k + j], (*second, c)).wait_recv()
                _remote(landed, landed, fwd_send.at[2 * p + j], fwd_recv.at[2 * p + j], (x, y, 1 - c)).start()

    dma = pltpu.SemaphoreType.DMA
    outs = pl.pallas_call(
        body, name=name,
        in_specs=[HBM] * n + [SEM, ANY],
        out_shape=(dma((2 * n,)), dma((2 * n,)), *[pltpu.HBM(a.shape, a.dtype) for a in lands]),
        out_specs=(SEM, SEM, *[HBM] * n),
        input_output_aliases={k: 2 + k for k in range(n)},
        compiler_params=pltpu.CompilerParams(has_side_effects=EFFECT),
    )(*lands, relay[1], after)
    return (outs[0], outs[1]), list(outs[2:])


def _gather_wait_relayed(name, idx, ks, shards, lands, sems, relay, fwd, after):
    n = len(idx)

    def body(*refs):
        ins, zones = refs[:n], refs[n:2 * n]
        ici_send, own_send, own_recv, relay_send, pass_send, pass_recv, fwd_send, fwd_recv = refs[2 * n:2 * n + 8]
        x, y, c, _ = _place()
        me = 2 * x + y
        sibling = (x, y, 1 - c)
        first, second, diagonal = _routes()
        for p, (i, k) in enumerate(zip(idx, ks)):
            mine, at_peer = _half(ins[p], c), _landed(zones, p, me, c)
            from_first = _landed(zones, p, _shard_of(first), c)
            _remote(mine, at_peer, ici_send.at[3 * i], pass_recv.at[k], (*first, c)).wait_send()
            _remote(mine, at_peer, relay_send.at[2 * k], pass_recv.at[k], (*second, c)).wait_send()
            _remote(from_first, from_first, relay_send.at[2 * k + 1], pass_recv.at[k], (*second, c)).wait_send()
            _remote(from_first, from_first, pass_send.at[k], pass_recv.at[k], sibling).wait_send()
            theirs = _landed(zones, p, _shard_of(second), 1 - c)
            _remote(theirs, theirs, pass_send.at[k], pass_recv.at[k], sibling).wait_recv()
            for j, (sent, got) in enumerate(((second, first), (diagonal, diagonal))):
                out_half = _landed(zones, p, _shard_of(sent), c)
                _remote(out_half, out_half, fwd_send.at[2 * p + j], fwd_recv.at[2 * p + j], sibling).wait_send()
                in_half = _landed(zones, p, _shard_of(got), 1 - c)
                _remote(in_half, in_half, fwd_send.at[2 * p + j], fwd_recv.at[2 * p + j], sibling).wait_recv()
            own = _remote(ins[p], zones[p].at[me], own_send.at[i], own_recv.at[i], sibling)
            own.wait_send()
            own.wait_recv()

    outs = pl.pallas_call(
        body, name=name,
        in_specs=[HBM] * (2 * n) + [SEM] * 8 + [ANY],
        out_shape=tuple(pltpu.HBM(a.shape, a.dtype) for a in lands),
        out_specs=tuple([HBM] * n),
        input_output_aliases={n + k: k for k in range(n)},
        compiler_params=pltpu.CompilerParams(has_side_effects=EFFECT),
    )(*shards, *lands, sems["ici_send"], sems["own_send"], sems["own_recv"], relay[0], relay[2], relay[3],
      fwd[0], fwd[1], after)
    return list(outs)


def _all_reduce_small(name, slab, after=None):
    r, width = slab.shape
    ndev = 8

    def body(x_ref, after_ref, out_ref, buf, send_sems, recv_sems):
        x, y, c, _ = _place()
        me = 4 * x + 2 * y + c
        buf[me] = x_ref[...]
        copies = []
        for k in range(1, ndev):
            peer = jnp.bitwise_xor(me, k)
            to = (peer // 4, (peer // 2) % 2, peer % 2)
            cp = pltpu.make_async_remote_copy(src_ref=x_ref, dst_ref=buf.at[me], send_sem=send_sems.at[k - 1],
                                              recv_sem=recv_sems.at[k - 1], device_id=to, device_id_type=MESH)
            cp.start()
            copies.append(cp)
        for k in range(1, ndev):
            peer = jnp.bitwise_xor(me, k)
            pltpu.make_async_remote_copy(src_ref=x_ref, dst_ref=buf.at[peer], send_sem=send_sems.at[k - 1],
                                         recv_sem=recv_sems.at[k - 1], device_id=(x, y, c),
                                         device_id_type=MESH).wait_recv()
        for cp in copies:
            cp.wait_send()
        total = buf[0]
        for d in range(1, ndev):
            total = total + buf[d]
        out_ref[...] = total

    return pl.pallas_call(
        body, name=name,
        in_specs=[pl.BlockSpec(memory_space=pltpu.VMEM), ANY], out_specs=pl.BlockSpec(memory_space=pltpu.VMEM),
        out_shape=jax.ShapeDtypeStruct((r, width), F32),
        scratch_shapes=[pltpu.VMEM((ndev, r, width), F32), pltpu.SemaphoreType.DMA((ndev - 1,)),
                        pltpu.SemaphoreType.DMA((ndev - 1,))],
        compiler_params=pltpu.CompilerParams(vmem_limit_bytes=VMEM_LIMIT),
    )(slab, slab if after is None else after)


def _half_tiling(rows, cols):
    if _by_rows(rows):
        tr = _pick(rows // 2, (256, 128, 64, 32, 16))
        nb = (rows // 2) // tr
        return (tr, cols), nb, (lambda which, b: (which * nb + b, 0)), (lambda b: (b, 0))
    tc = _pick(cols // 2, (256, 128))
    nb = (cols // 2) // tc
    return (rows, tc), nb, (lambda which, b: (0, which * nb + b)), (lambda b: (0, b))


def _chip_partial(name, grad, other, core):
    s, r, cdim = grad.shape
    blk, nb, whole, within = _half_tiling(r, cdim)

    def body(core_ref, g_ref, o_ref, out_ref):
        out_ref[...] = (g_ref[...] + o_ref[...]).astype(BF16)

    return pl.pallas_call(
        body, name=name,
        grid_spec=pltpu.PrefetchScalarGridSpec(
            num_scalar_prefetch=1, grid=(s, nb),
            in_specs=[pl.BlockSpec((None,) + blk, lambda j, b, core_ref: (j,) + whole(core_ref[0], b)),
                      pl.BlockSpec((None,) + blk, lambda j, b, core_ref: (j,) + within(b))],
            out_specs=pl.BlockSpec((None,) + blk, lambda j, b, core_ref: (j,) + within(b))),
        out_shape=jax.ShapeDtypeStruct((s,) + _half_shape(r, cdim), BF16),
        compiler_params=_params("parallel", "parallel"),
    )(core, grad, other)


def _partial_copies(ins, zones, send_sems, recv_sems):
    x, y, c, chips = _place()
    return [_remote(ins[i].at[2 * chip[0] + chip[1]], zones[i].at[j], send_sems.at[3 * i + j],
                    recv_sems.at[3 * i + j], (*chip, c))
            for i in range(len(ins)) for j, chip in enumerate(chips)]


def _swap_copies(ins, zones, send_sems, recv_sems):
    x, y, c, _ = _place()
    copies = []
    for i in range(len(ins)):
        for s in range(N_SHARD):
            copies.append(_remote(_half(ins[i], 1 - c, s), zones[i].at[s],
                                  send_sems.at[N_SHARD * i + s], recv_sems.at[N_SHARD * i + s], (x, y, 1 - c)))
    return copies


def _exchange_start(name, plan, sources, lands, per_array):
    n = len(sources)
    lands = [lax.empty(shape, dtype) for shape, dtype in lands]

    def body(*refs):
        for cp in plan(refs[:n], refs[n:2 * n], refs[2 * n], refs[2 * n + 1]):
            cp.start()
        refs[-1][...] = jnp.zeros_like(refs[-1])

    dma = pltpu.SemaphoreType.DMA
    outs = pl.pallas_call(
        body, name=name,
        in_specs=[HBM] * (2 * n),
        out_shape=(dma((per_array * n,)), dma((per_array * n,)),
                   *[pltpu.HBM(a.shape, a.dtype) for a in list(sources) + lands], jax.ShapeDtypeStruct((8, LANES), F32)),
        out_specs=(SEM, SEM, *[HBM] * (2 * n), pl.BlockSpec(memory_space=pltpu.VMEM)),
        input_output_aliases={k: 2 + k for k in range(2 * n)},
        compiler_params=pltpu.CompilerParams(has_side_effects=EFFECT),
    )(*[_in_hbm(a) for a in list(sources) + lands])
    return (outs[0], outs[1]), list(outs[2:2 + n]), list(outs[2 + n:2 + 2 * n]), outs[-1]


def _exchange_wait(name, plan, started, after):
    sems, partials, lands, _ = started
    n = len(partials)

    def body(*refs):
        for cp in plan(refs[:n], refs[n:2 * n], refs[2 * n], refs[2 * n + 1]):
            cp.wait_send()
            cp.wait_recv()

    outs = pl.pallas_call(
        body, name=name,
        in_specs=[HBM] * (2 * n) + [SEM, SEM] + [ANY] * len(after),
        out_shape=tuple(pltpu.HBM(a.shape, a.dtype) for a in lands),
        out_specs=tuple([HBM] * n),
        input_output_aliases={n + k: k for k in range(n)},
        compiler_params=pltpu.CompilerParams(has_side_effects=EFFECT),
    )(*partials, *lands, sems[0], sems[1], *after)
    return list(outs)


def _reduce_own(name, grad, other, received, where):
    s, r, cdim = grad.shape
    blk, nb, whole, within = _half_tiling(r, cdim)

    def body(where_ref, g_ref, o_ref, r_ref, out_ref):
        total = g_ref[...] + o_ref[...]
        for j in range(3):
            total = total + r_ref[j].astype(F32)
        out_ref[...] = total

    return pl.pallas_call(
        body, name=name,
        grid_spec=pltpu.PrefetchScalarGridSpec(
            num_scalar_prefetch=1, grid=(nb,),
            in_specs=[pl.BlockSpec((None,) + blk, lambda b, w_ref: (w_ref[0],) + whole(w_ref[1], b)),
                      pl.BlockSpec((None,) + blk, lambda b, w_ref: (w_ref[0],) + within(b)),
                      pl.BlockSpec((3,) + blk, lambda b, w_ref: (0,) + within(b))],
            out_specs=pl.BlockSpec(blk, lambda b, w_ref: whole(w_ref[1], b))),
        out_shape=jax.ShapeDtypeStruct((r, cdim), F32),
        compiler_params=_params("parallel"),
    )(where, grad, other, received)


def _join_start(name, halves):
    n = len(halves)

    def body(*refs):
        bufs, send_sems, recv_sems = refs[:n], refs[n], refs[n + 1]
        x, y, c, _ = _place()
        for i in range(n):
            mine = _half(bufs[i], c)
            _remote(mine, mine, send_sems.at[i], recv_sems.at[i], (x, y, 1 - c)).start()
        refs[-1][...] = jnp.zeros_like(refs[-1])

    dma = pltpu.SemaphoreType.DMA
    outs = pl.pallas_call(
        body, name=name,
        in_specs=[HBM] * n,
        out_shape=(dma((n,)), dma((n,)), *[pltpu.HBM(h.shape, F32) for h in halves], jax.ShapeDtypeStruct((8, LANES), F32)),
        out_specs=(SEM, SEM, *[HBM] * n, pl.BlockSpec(memory_space=pltpu.VMEM)),
        input_output_aliases={k: 2 + k for k in range(n)},
        compiler_params=pltpu.CompilerParams(has_side_effects=EFFECT),
    )(*[_in_hbm(h) for h in halves])
    return (outs[0], outs[1]), list(outs[2:2 + n]), outs[-1]


def _join_wait(name, started, after):
    sems, bufs, _ = started
    n = len(bufs)

    def body(*refs):
        bufs, send_sems, recv_sems = refs[:n], refs[n], refs[n + 1]
        x, y, c, _ = _place()
        for i in range(n):
            mine, theirs = _half(bufs[i], c), _half(bufs[i], 1 - c)
            _remote(mine, mine, send_sems.at[i], recv_sems.at[i], (x, y, 1 - c)).wait_send()
            _remote(theirs, theirs, send_sems.at[i], recv_sems.at[i], (x, y, 1 - c)).wait_recv()

    outs = pl.pallas_call(
        body, name=name,
        in_specs=[HBM] * n + [SEM, SEM] + [ANY] * len(after),
        out_shape=tuple(pltpu.HBM(b.shape, F32) for b in bufs),
        out_specs=tuple([HBM] * n),
        input_output_aliases={k: k for k in range(n)},
        compiler_params=pltpu.CompilerParams(has_side_effects=EFFECT),
    )(*bufs, sems[0], sems[1], *after)
    return list(outs)


BIG = ("w_in", "pool_w", "w_out", "xq_w", "xk_w", "xv_w", "xo_w", "w_up", "w_down", "conv_w")
KEPT_F32 = ("conv_w",)
GATHER_GROUPS = ((0, 1, 9), (2, 3, 4, 5, 6), (7,), (8,))
RELAYED = (7, 8)
SMALL = ("conv_w", "a_log", "dt_bias", "gdn_norm_w", "pool_scale", "ln1_g", "ln1_b", "ln2_g", "ln2_b", "ln3_g", "ln3_b")
ORDER = ("w_in", "conv_w", "a_log", "dt_bias", "gdn_norm_w", "pool_w", "pool_scale", "w_out", "ln1_g", "ln1_b",
         "xq_w", "xk_w", "xv_w", "xo_w", "ln2_g", "ln2_b", "w_up", "w_down", "ln3_g", "ln3_b")
LANES = 128


def _rows(flat_len):
    return -(-flat_len // LANES)


def _pack(pieces):
    out = []
    for p in pieces:
        flat = p.reshape(-1).astype(F32)
        out.append(jnp.pad(flat, (0, _rows(flat.shape[0]) * LANES - flat.shape[0])).reshape(-1, LANES))
    slab = jnp.concatenate(out, axis=0)
    return jnp.pad(slab, ((0, -slab.shape[0] % 8), (0, 0)))


def _unpack(slab, shapes):
    out, row = [], 0
    for shp in shapes:
        size = math.prod(shp)
        out.append(slab[row:row + _rows(size)].reshape(-1)[:size].reshape(shp))
        row += _rows(size)
    return out


TRANSPOSED = ("w_in",)


def _as2d(name, a):
    a = a[0]
    if name in TRANSPOSED:
        return jnp.swapaxes(a, 0, 1)
    return a.reshape(-1, a.shape[-1]) if a.ndim == 3 else a


def _from2d(name, a, shape):
    return (jnp.swapaxes(a, 0, 1) if name in TRANSPOSED else a).reshape(shape)


def kernel(x, mem, w_in, conv_w, a_log, dt_bias, gdn_norm_w, pool_w, pool_scale, w_out, ln1_g, ln1_b, xq_w, xk_w, xv_w, xo_w, ln2_g, ln2_b, w_up, w_down, ln3_g, ln3_b, loss_target, m_w_in, m_conv_w, m_a_log, m_dt_bias, m_gdn_norm_w, m_pool_w, m_pool_scale, m_w_out, m_ln1_g, m_ln1_b, m_xq_w, m_xk_w, m_xv_w, m_xo_w, m_ln2_g, m_ln2_b, m_w_up, m_w_down, m_ln3_g, m_ln3_b, v_w_in, v_conv_w, v_a_log, v_dt_bias, v_gdn_norm_w, v_pool_w, v_pool_scale, v_w_out, v_ln1_g, v_ln1_b, v_xq_w, v_xk_w, v_xv_w, v_xo_w, v_ln2_g, v_ln2_b, v_w_up, v_w_down, v_ln3_g, v_ln3_b):
    given = dict(locals())
    cx, cy, cc = lax.axis_index("x"), lax.axis_index("y"), lax.axis_index("c")
    me = 2 * cx + cy
    groups = pool_w.shape[1]
    cs = pool_w.shape[2]
    kk, conv_cols = conv_w.shape[1], conv_w.shape[2]
    core = cc.astype(jnp.int32).reshape(1)
    where = jnp.stack([me, cc]).astype(jnp.int32)

    started = {}
    wts = {}

    def start(name, idx, after, token=None):
        casts = [_after(token, _as2d(BIG[i], given[BIG[i]])).astype(F32 if BIG[i] in KEPT_F32 else BF16) for i in idx]
        relayed = tuple(k for k, i in enumerate(idx) if i in RELAYED)
        sems, shards, lands, token = _gather_start(name, casts, after, relayed)
        for k, i in enumerate(idx):
            started[i] = (sems, k, shards[k], lands[k])
        return token

    token = start("gather_start_first", GATHER_GROUPS[0], x)
    token = start("gather_start_rest", tuple(i for group in GATHER_GROUPS[1:] for i in group), token, token)

    relay = {}

    def send_on(after):
        members = [started[i] for i in RELAYED]
        relay["sems"], zones, token = _gather_relay("gather_relay", [m[1] for m in members], [m[2] for m in members],
                                                    [m[3] for m in members], members[0][0], after)
        relay["zones"] = dict(zip(RELAYED, zones))
        return token

    passed = {}

    def pass_on(group, after):
        members = [started[i] for i in GATHER_GROUPS[group]]
        fwd, zones, token = _gather_forward(f"gather_forward_{group}", [m[1] for m in members], [m[3] for m in members],
                                            members[0][0], after)
        passed[group] = (fwd, zones)
        return token

    def fetch(group, after):
        members = [started[i] for i in GATHER_GROUPS[group]]
        sems, idx = members[0][0], [m[1] for m in members]
        shards = [m[2] for m in members]
        if GATHER_GROUPS[group][0] in RELAYED:
            ks = [RELAYED.index(i) for i in GATHER_GROUPS[group]]
            zones = [relay["zones"][i] for i in GATHER_GROUPS[group]]
            fwd, zones = _gather_forward_relayed(f"gather_forward_{group}", ks, zones, relay["sems"], after)
            got = _gather_wait_relayed(f"gather_wait_{group}", idx, ks, shards, zones, sems, relay["sems"], fwd, after)
        else:
            if group not in passed:
                pass_on(group, after)
            fwd, zones = passed[group]
            got = _gather_wait(f"gather_wait_{group}", idx, shards, zones, sems, fwd, after)
        full = dict(zip([BIG[i] for i in GATHER_GROUPS[group]], got))
        out = {}
        for n, a in full.items():
            if n == "w_in":
                out["w_in_t"] = a
            elif n == "w_up":
                out["w_up3"] = a
            elif n == "pool_w":
                out[n] = a.reshape(N_SHARD, groups, cs, -1).transpose(1, 0, 2, 3).reshape(groups, N_SHARD * cs, -1)
            elif n == "conv_w":
                out[n] = a.transpose(1, 0, 2).reshape(kk, N_SHARD * conv_cols)
            else:
                out[n] = a.reshape(-1, a.shape[-1])
        return out

    for n in ("a_log", "dt_bias", "gdn_norm_w", "pool_scale", "ln1_g", "ln1_b", "ln2_g", "ln2_b", "ln3_g", "ln3_b"):
        wts[n] = given[n]
    x_bf = _after(token, x[0]).astype(BF16)
    wts.update(fetch(0, x_bf))

    def start_swap(group, grads):
        names, blocks = [], []
        for n, g in grads.items():
            if n == "pool_w":
                g = g.reshape(groups, N_SHARD, cs, -1).transpose(1, 0, 2, 3).reshape(N_SHARD, groups * cs, -1)
            elif g.ndim == 2:
                g = g.reshape(N_SHARD, -1, g.shape[-1])
            names.append({"w_in_t": "w_in", "w_up3": "w_up"}.get(n, n))
            blocks.append(g)
        zones = [((N_SHARD,) + _half_shape(b.shape[1], b.shape[2]), F32) for b in blocks]
        swap = _exchange_start(f"grad_swap_start_{group}", _swap_copies, blocks, zones, N_SHARD)
        return {"group": group, "names": names, "swap": swap, "token": swap[3]}

    def start_send(state, after):
        group, names = state["group"], state["names"]
        state["blocks"] = state["swap"][1]
        state["others"] = _exchange_wait(f"grad_swap_wait_{group}", _swap_copies, state["swap"], after)
        partials = [_chip_partial("chip_partial_" + n, gb, ob, core)
                    for n, gb, ob in zip(names, state["blocks"], state["others"])]
        zones = [((3,) + p.shape[1:], BF16) for p in partials]
        state["send"] = _exchange_start(f"grad_send_start_{group}", _partial_copies, partials, zones, 3)
        state["token"] = state["send"][3]

    grad, delta, new_m, new_v = {}, {}, {}, {}

    def start_join(state, after):
        group, names = state["group"], state["names"]
        received = _exchange_wait(f"grad_send_wait_{group}", _partial_copies, state["send"], after)
        halves = [_reduce_own("reduce_own_" + n, gb, ob, rb, where)
                  for n, gb, ob, rb in zip(names, state["blocks"], state["others"], received)]
        state["join"] = _join_start(f"grad_join_start_{group}", halves)
        return state["join"][2]

    def finish_reduce(state, after):
        group, names = state["group"], state["names"]
        for n, g in zip(names, _join_wait(f"grad_join_wait_{group}", state["join"], after)):
            shp = given[n].shape
            d2, m2, v2, g2 = _adamw("adamw_" + n, _as2d(n, given[n]), g, _as2d(n, given["m_" + n]),
                                    _as2d(n, given["v_" + n]))
            grad[n], delta[n], new_m[n], new_v[n] = (_from2d(n, a, shp) for a in (g2, d2, m2, v2))
        return d2

    step = _local_step(x[0], mem[0], loss_target[0], wts, x_bf)
    pending = {}
    request = next(step)
    while True:
        try:
            kind, group, payload = request
            if kind == "weights":
                request = step.send(fetch(group, payload))
            elif kind == "relay":
                request = step.send(send_on(payload))
            elif kind == "pass":
                request = step.send(pass_on(group, payload))
            elif kind == "grads":
                pending[group] = start_swap(group, payload)
                request = step.send(pending[group]["token"])
            else:
                start_send(pending[group], [payload])
                request = step.send(pending[group]["token"])
        except StopIteration as stop:
            loss_row, grad_x, g = stop.value
            break

    after = [pending[2]["token"], grad_x]
    for group in (0, 1):
        after = [start_join(pending[group], after)]
    for group in (0, 1):
        after = [finish_reduce(pending[group], after)]
    after = [finish_reduce(pending[2], [start_join(pending[2], after)])]

    small_names = ("a_log", "dt_bias", "gdn_norm_w", "pool_scale", "ln1_g", "ln1_b", "ln2_g", "ln2_b", "ln3_g", "ln3_b")
    pieces = [g["conv_w"]] + [g[n] for n in small_names] + [loss_row[:, :1]]
    shapes = [p.shape for p in pieces]
    summed = _unpack(_all_reduce_small("all_reduce_small", _pack(pieces), after[0]), shapes)
    gsmall = dict(zip(small_names, summed[1:-1]))
    gsmall["conv_w"] = lax.dynamic_slice(summed[0], (0, me * conv_cols), (kk, conv_cols))
    loss = summed[-1][0, 0]

    sshapes = [given[n].shape for n in SMALL]
    slabs = [_pack([given[p + n] for n in SMALL]) for p in ("", "m_", "v_")]
    gslab = _pack([gsmall[n] for n in SMALL])
    outs = _adamw("adamw_small", slabs[0], gslab, slabs[1], slabs[2])[:3]
    for dst, slab in zip((delta, new_m, new_v), outs):
        dst.update(zip(SMALL, _unpack(slab, sshapes)))
    for n in SMALL:
        grad[n] = gsmall[n].reshape(given[n].shape)

    return (loss, grad_x[None], *[grad[n] for n in ORDER], *[delta[n] for n in ORDER],
            *[new_m[n] for n in ORDER], *[new_v[n] for n in ORDER])
```

```python
import math

import jax
import jax.numpy as jnp
from jax import lax
from jax.experimental import pallas as pl
from jax.experimental.pallas import tpu as pltpu

F32 = jnp.float32
BF16 = jnp.bfloat16
MESH = pl.DeviceIdType.MESH

HEAD_DIM = 128
CHUNK = 64
POOL_WINDOWS = (2, 4, 8, 16)
XATTN_HEADS = 4
ALPHA = 2.0 ** 0.25
LN_EPS = 1e-5
NORM_EPS = 1e-6
ADAM_LR, ADAM_B1, ADAM_B2, ADAM_EPS, ADAM_WD, ADAM_STEP = 0.001, 0.9, 0.999, 1e-08, 0.01, 10
N_SHARD = 4
VMEM_LIMIT = 56 * 1024 * 1024
K_STEPS = (2048, 1024, 512, 256, 128)


def _params(*sem):
    return pltpu.CompilerParams(dimension_semantics=sem, vmem_limit_bytes=VMEM_LIMIT)


def _bdot(a, b, ta=False, tb=False):
    dims = (((0 if ta else 1,), (1 if tb else 0,)), ((), ()))
    return lax.dot_general(a.astype(BF16), b.astype(BF16), dims, preferred_element_type=F32)


def _sigmoid(x):
    return 1.0 / (1.0 + jnp.exp(-x))


def _matmul(name, a, b, *, ta=False, tb=False, tm, tn, tk, extra=(), outs, epilogue, b_blocks=None,
            sequential=False, n_used=None, n_outer=False):
    m, k_dim = (a.shape[1], a.shape[0]) if ta else a.shape
    if b_blocks and tb:
        n = b.shape[1]
        k_dim = b.shape[0] * b.shape[2]
        per = b.shape[2] // tk
        b_spec = pl.BlockSpec((None, tn, tk), lambda i, j, k: (k // per, j, k % per))
    elif b_blocks:
        n = b.shape[0] * b.shape[2]
        per = b.shape[2] // tn
        b_spec = pl.BlockSpec((None, tk, tn), lambda i, j, k: (j // per, k, j % per))
    elif tb:
        n = b.shape[0]
        b_spec = pl.BlockSpec((tn, tk), lambda i, j, k: (j, k))
    else:
        n = b.shape[1]
        b_spec = pl.BlockSpec((tk, tn), lambda i, j, k: (k, j))
    n = n_used or n
    assert m % tm == 0 and n % tn == 0 and k_dim % tk == 0, (name, m, n, k_dim, tm, tn, tk)
    nk = k_dim // tk
    a_spec = pl.BlockSpec((tk, tm), lambda i, j, k: (k, i)) if ta else pl.BlockSpec((tm, tk), lambda i, j, k: (i, k))
    n_extra, n_out = len(extra), len(outs)

    def wrap(index_map):
        return lambda i, j, k: index_map(i, j)

    def spec(block, index_map):
        if n_outer:
            return pl.BlockSpec(block, lambda j, i, k: index_map(i, j, k))
        return pl.BlockSpec(block, index_map)

    row_axis = 1 if n_outer else 0

    def body_one_step(*refs):
        ex = refs[2:2 + n_extra]
        out = refs[2 + n_extra:2 + n_extra + n_out]
        epilogue(_bdot(refs[0][...], refs[1][...], ta, tb), ex, out, pl.program_id(row_axis))

    def body(*refs):
        a_ref, b_ref = refs[0], refs[1]
        ex = refs[2:2 + n_extra]
        out = refs[2 + n_extra:2 + n_extra + n_out]
        acc = refs[-1]
        i, k = pl.program_id(row_axis), pl.program_id(2)
        part = _bdot(a_ref[...], b_ref[...], ta, tb)

        @pl.when(k == 0)
        def _():
            acc[...] = part

        @pl.when(jnp.logical_and(k > 0, k < nk - 1))
        def _():
            acc[...] += part

        @pl.when(k == nk - 1)
        def _():
            epilogue(acc[...] + part, ex, out, i)

    sem = ("arbitrary",) * 3 if sequential else ("parallel", "parallel", "arbitrary")
    res = pl.pallas_call(
        body_one_step if nk == 1 else body, name=name,
        grid=(n // tn, m // tm, nk) if n_outer else (m // tm, n // tn, nk),
        in_specs=[spec(a_spec.block_shape, a_spec.index_map), spec(b_spec.block_shape, b_spec.index_map)]
        + [spec(bs, wrap(im)) for _, bs, im in extra],
        out_specs=[spec(bs, wrap(im)) for _, bs, im in outs],
        out_shape=[s for s, _, _ in outs],
        scratch_shapes=[] if nk == 1 else [pltpu.VMEM((tm, tn), F32)],
        compiler_params=_params(*sem),
    )(a, b, *[x for x, _, _ in extra])
    return res


def _tile(i, j):
    return (i, j)


def _plain(name, a, b, *, ta=False, tb=False, tm, tn, tk, out_dtype, b_blocks=None, out3=None, n_used=None,
           n_outer=False, m_kept=None):
    m = a.shape[1] if ta else a.shape[0]
    if b_blocks:
        n = b.shape[1] if tb else b.shape[0] * b.shape[2]
    else:
        n = n_used or (b.shape[0] if tb else b.shape[1])

    def epi(acc, ex, out, i):
        out[0][...] = acc.astype(out_dtype)

    if out3:
        per = (n // out3) // tn
        spec = (jax.ShapeDtypeStruct((out3, m, n // out3), out_dtype), (None, tm, tn),
                lambda i, j: (j // per, i, j % per))
    else:
        spec = (jax.ShapeDtypeStruct((m_kept or m, n), out_dtype), (tm, tn), _tile)
    return _matmul(name, a, b, ta=ta, tb=tb, tm=tm, tn=tn, tk=tk, outs=[spec], epilogue=epi,
                   b_blocks=b_blocks, n_used=n_used, n_outer=n_outer)[0]


def _ln_forward(name, a, b, res, gamma, beta, *, tm, tk, want_h=True):
    m, n = res.shape

    def epi(acc, ex, out, i):
        u = ALPHA * ex[0][...] + acc
        mu = jnp.mean(u, axis=-1, keepdims=True)
        xc = u - mu
        var = jnp.mean(xc * xc, axis=-1, keepdims=True)
        rstd = lax.rsqrt(var + LN_EPS)
        xhat = xc * rstd
        out[-2][...] = xhat
        out[-1][...] = rstd
        if want_h:
            h = xhat * ex[1][...] + ex[2][...]
            out[0][...] = h
            out[1][...] = h.astype(BF16)

    row = lambda i, j: (i, 0)
    vec = lambda i, j: (0, 0)
    outs = [(jax.ShapeDtypeStruct((m, n), F32), (tm, n), row), (jax.ShapeDtypeStruct((m, n), BF16), (tm, n), row),
            (jax.ShapeDtypeStruct((m, n), F32), (tm, n), row), (jax.ShapeDtypeStruct((m, 1), F32), (tm, 1), row)]
    return _matmul(
        name, a, b, tm=tm, tn=n, tk=tk,
        extra=[(res, (tm, n), row), (gamma, (1, n), vec), (beta, (1, n), vec)],
        outs=outs if want_h else outs[2:], epilogue=epi)


def _ln_backward_math(dy, xhat, rstd, gamma):
    dxhat = dy * gamma
    m1 = jnp.mean(dxhat, axis=-1, keepdims=True)
    m2 = jnp.mean(dxhat * xhat, axis=-1, keepdims=True)
    du = rstd * (dxhat - m1 - xhat * m2)
    return du, jnp.sum(dy * xhat, axis=0, keepdims=True), jnp.sum(dy, axis=0, keepdims=True)


def _ln_backward(name, a, b, dres, xhat, rstd, gamma, *, tm, tk, b_blocks=None, tb=True):
    m, n = dres.shape

    def epi(acc, ex, out, i):
        dy = acc + ALPHA * ex[0][...]
        du, dg, db = _ln_backward_math(dy, ex[1][...], ex[2][...], ex[3][...])
        out[0][...] = du
        out[1][...] = du.astype(BF16)
        first = i == 0

        @pl.when(first)
        def _():
            out[2][...] = dg
            out[3][...] = db

        @pl.when(jnp.logical_not(first))
        def _():
            out[2][...] += dg
            out[3][...] += db

    row = lambda i, j: (i, 0)
    vec = lambda i, j: (0, 0)
    return _matmul(
        name, a, b, tb=tb, tm=tm, tn=n, tk=tk, b_blocks=b_blocks, sequential=True,
        extra=[(dres, (tm, n), row), (xhat, (tm, n), row), (rstd, (tm, 1), row), (gamma, (1, n), vec)],
        outs=[(jax.ShapeDtypeStruct((m, n), F32), (tm, n), row),
              (jax.ShapeDtypeStruct((m, n), BF16), (tm, n), row),
              (jax.ShapeDtypeStruct((1, n), F32), (1, n), vec),
              (jax.ShapeDtypeStruct((1, n), F32), (1, n), vec)],
        epilogue=epi)


def _shift_down(x, k):
    row = lax.broadcasted_iota(jnp.int32, x.shape, 0)
    return jnp.where(row >= k, pltpu.roll(x, k, axis=0), 0.0)


def _shift_up(x, k):
    t = x.shape[0]
    row = lax.broadcasted_iota(jnp.int32, x.shape, 0)
    return jnp.where(row < t - k, pltpu.roll(x, t - k, axis=0), 0.0)


def _conv_silu_norm(x, w, normalise):
    kk = w.shape[0]
    c = x * w[kk - 1:kk, :]
    for j in range(kk - 1):
        c = c + _shift_down(x, kk - 1 - j) * w[j:j + 1, :]
    sg = _sigmoid(c)
    s = c * sg
    r = lax.rsqrt(jnp.sum(s * s, axis=-1, keepdims=True) + NORM_EPS)
    y = jnp.where(normalise, s * r, s)
    return c, sg, s, r, y


def _gdn_pre(proj, conv_w, heads):
    t = proj.shape[0]
    kk = conv_w.shape[0]

    def body(x_ref, w_ref, o_ref):
        normalise = pl.program_id(0) < 2
        o_ref[...] = _conv_silu_norm(x_ref[...], w_ref[...], normalise)[4]

    col = lambda s, h: (0, s * heads + h)
    return pl.pallas_call(
        body, name="gdn_pre", grid=(3, heads),
        in_specs=[pl.BlockSpec((t, HEAD_DIM), col), pl.BlockSpec((kk, HEAD_DIM), col)],
        out_specs=pl.BlockSpec((t, HEAD_DIM), col),
        out_shape=jax.ShapeDtypeStruct((t, 3 * heads * HEAD_DIM), F32),
        compiler_params=_params("parallel", "parallel"),
    )(proj, conv_w)


def _gdn_pre_backward(proj, conv_w, dqkv, heads):
    t = proj.shape[0]
    kk = conv_w.shape[0]

    def body(x_ref, w_ref, dy_ref, dx_ref, dw_ref):
        normalise = pl.program_id(0) < 2
        x = x_ref[...]
        w = w_ref[...]
        dy = dy_ref[...]
        c, sg, s, r, y = _conv_silu_norm(x, w, normalise)
        ds_norm = r * (dy - y * jnp.sum(dy * y, axis=-1, keepdims=True))
        ds = jnp.where(normalise, ds_norm, dy)
        dc = ds * (sg * (1.0 + c * (1.0 - sg)))
        dx = dc * w[kk - 1:kk, :]
        rows = [None] * kk
        rows[kk - 1] = jnp.sum(dc * x, axis=0, keepdims=True)
        for j in range(kk - 1):
            lag = kk - 1 - j
            dx = dx + _shift_up(dc, lag) * w[j:j + 1, :]
            rows[j] = jnp.sum(dc * _shift_down(x, lag), axis=0, keepdims=True)
        dx_ref[...] = dx.astype(BF16)
        dw_ref[...] = jnp.concatenate(rows, axis=0)

    col = lambda s, h: (0, s * heads + h)
    return pl.pallas_call(
        body, name="gdn_pre_bwd", grid=(3, heads),
        in_specs=[pl.BlockSpec((t, HEAD_DIM), col), pl.BlockSpec((kk, HEAD_DIM), col),
                  pl.BlockSpec((t, HEAD_DIM), col)],
        out_specs=[pl.BlockSpec((t, HEAD_DIM), col), pl.BlockSpec((kk, HEAD_DIM), col)],
        out_shape=[jax.ShapeDtypeStruct((t, 3 * heads * HEAD_DIM), BF16),
                   jax.ShapeDtypeStruct((kk, 3 * heads * HEAD_DIM), F32)],
        compiler_params=_params("parallel", "parallel"),
    )(proj, conv_w, dqkv)


def _gate_vectors(a_log, dt_bias, heads):
    pad = lambda v: jnp.pad(v.astype(F32), ((0, 0), (heads, HEAD_DIM - 2 * heads)))
    return pad(jnp.exp(a_log.astype(F32))), pad(dt_bias)


def _softplus(x):
    return jnp.maximum(x, 0.0) + jnp.log(1.0 + jnp.exp(-jnp.abs(x)))


def _gates_epilogue(heads):
    def epi(acc, ex, out, i):
        lane = lax.broadcasted_iota(jnp.int32, acc.shape, 1)
        beta = _sigmoid(acc)
        g = -ex[0][...] * _softplus(acc + ex[1][...])
        out[0][...] = acc
        out[1][...] = jnp.where(lane < heads, beta, jnp.where(lane < 2 * heads, g, 0.0))
    return epi


def _gates_backward(ba, bg, dbg, ea, dtb, heads):
    t = ba.shape[0]

    def body(ba_ref, bg_ref, d_ref, ea_ref, dt_ref, dba_ref, dal_ref, ddt_ref):
        lane = lax.broadcasted_iota(jnp.int32, (t, HEAD_DIM), 1)
        bgv = bg_ref[...]
        d = d_ref[...]
        db = d * bgv * (1.0 - bgv)
        da = -d * ea_ref[...] * _sigmoid(ba_ref[...] + dt_ref[...])
        is_g = jnp.logical_and(lane >= heads, lane < 2 * heads)
        dba = jnp.where(lane < heads, db, jnp.where(is_g, da, 0.0))
        dba_ref[...] = dba.astype(BF16)
        dal_ref[...] = jnp.sum(jnp.where(is_g, d * bgv, 0.0), axis=0, keepdims=True)
        ddt_ref[...] = jnp.sum(jnp.where(is_g, da, 0.0), axis=0, keepdims=True)

    full = pl.BlockSpec((t, HEAD_DIM), lambda: (0, 0))
    vec = pl.BlockSpec((1, HEAD_DIM), lambda: (0, 0))
    return pl.pallas_call(
        body, name="gates_bwd", grid=(),
        in_specs=[full, full, full, vec, vec], out_specs=[full, vec, vec],
        out_shape=[jax.ShapeDtypeStruct((t, HEAD_DIM), BF16), jax.ShapeDtypeStruct((1, HEAD_DIM), F32),
                   jax.ShapeDtypeStruct((1, HEAD_DIM), F32)],
        compiler_params=pltpu.CompilerParams(vmem_limit_bytes=VMEM_LIMIT),
    )(ba, bg, dbg, ea, dtb)


class _Chunk:
    pass


def _split2(x):
    hi = x.astype(BF16)
    return hi, (x - hi.astype(F32)).astype(BF16)


def _split3(x):
    hi = x.astype(BF16)
    rest = x - hi.astype(F32)
    mid = rest.astype(BF16)
    return hi, mid, (rest - mid.astype(F32)).astype(BF16)


def _dot_mask(mask, x, ta=False):
    hi, mid, lo = _split3(x)
    return _bdot(mask, hi, ta=ta) + (_bdot(mask, mid, ta=ta) + _bdot(mask, lo, ta=ta))


def _transpose_by_identity(x):
    r = x.shape[0]
    eye = (lax.broadcasted_iota(jnp.int32, (r, r), 0) == lax.broadcasted_iota(jnp.int32, (r, r), 1)).astype(BF16)
    hi, mid, lo = _split3(x)
    return _bdot(hi, eye, ta=True) + (_bdot(mid, eye, ta=True) + _bdot(lo, eye, ta=True))


def _dot22(a, b, ta=False, tb=False):
    ah, al = _split2(a)
    bh, bl = _split2(b)
    return _bdot(ah, bh, ta, tb) + (_bdot(ah, bl, ta, tb) + _bdot(al, bh, ta, tb))


def _chunk_gates(bg, heads):
    n = CHUNK
    row = lax.broadcasted_iota(jnp.int32, (n, n), 0)
    col = lax.broadcasted_iota(jnp.int32, (n, n), 1)
    lane = lax.broadcasted_iota(jnp.int32, bg.shape, 1)
    graw = jnp.where(jnp.logical_and(lane >= heads, lane < 2 * heads), bg, 0.0)
    gc = _dot_mask((row >= col).astype(BF16), graw)
    return gc, _transpose_by_identity(gc)


def _in_lockstep(generators):
    results = [None] * len(generators)
    live = list(enumerate(generators))
    while live:
        still = []
        for i, gen in live:
            try:
                next(gen)
                still.append((i, gen))
            except StopIteration as stop:
                results[i] = stop.value
        live = still
    return results


def _chunk_local(q, k, v, beta, gc, grow, solved=None):
    c = _Chunk()
    n = CHUNK
    row = lax.broadcasted_iota(jnp.int32, (n, n), 0)
    col = lax.broadcasted_iota(jnp.int32, (n, n), 1)
    c.tri = row >= col
    c.strict = row > col
    eye = row == col
    c.gcb = jnp.broadcast_to(gc, (n, HEAD_DIM))
    c.decay = jnp.where(c.tri, jnp.exp(jnp.where(c.tri, gc - grow, 0.0)), 0.0)
    c.eg = jnp.exp(c.gcb)
    glast = c.gcb[n - 1:n, :]
    c.egl = jnp.exp(glast)
    c.ekl = jnp.exp(glast - c.gcb)
    c.beta = beta
    c.q = q * (HEAD_DIM ** -0.5)
    c.k = k
    c.v = v
    c.kb = k * beta
    c.vb = v * beta
    c.kg = c.kb * c.eg
    both = _bdot(jnp.concatenate([c.kb, c.q], axis=0), k, tb=True)
    yield
    c.L = jnp.where(c.strict, both[:n] * c.decay, 0.0)
    c.A = jnp.where(c.tri, both[n:] * c.decay, 0.0)
    if solved is None:
        x = -c.L
        tinv = eye.astype(F32) + x
        p = _dot22(x, x)
        yield
        for _ in range(int(math.log2(n)) - 2):
            both = _dot22(jnp.concatenate([p, tinv], axis=0), p)
            yield
            p, tinv = both[:n], tinv + both[n:]
        c.T = tinv + _dot22(tinv, p)
        yield
        uw = _dot22(c.T, jnp.concatenate([c.vb, c.kg], axis=1))
        yield
        c.u, c.w = uw[:, :HEAD_DIM], uw[:, HEAD_DIM:]
    else:
        c.T, c.u, c.w = solved
    c.qg = c.q * c.eg
    c.kdec = k * c.ekl
    return c


def _gdn_core(qkv, bg, heads):
    t = qkv.shape[0]
    nchunk = t // CHUNK

    gw = heads * HEAD_DIM

    def body(qkv_ref, bg_ref, o_ref, s_ref, t_ref, u_ref, w_ref, state):
        @pl.when(pl.program_id(0) == 0)
        def _():
            state[...] = jnp.zeros_like(state)

        bg_v = bg_ref[...]
        gc_all, gc_rows = _chunk_gates(bg_v, heads)
        def one_head(h):
            col = lambda s: pl.ds(s * gw + h * HEAD_DIM, HEAD_DIM)
            c = yield from _chunk_local(qkv_ref[:, col(0)], qkv_ref[:, col(1)], qkv_ref[:, col(2)], bg_v[:, h:h + 1],
                                        gc_all[:, heads + h:heads + h + 1], gc_rows[heads + h:heads + h + 1, :])
            s0 = state[h]
            v_new = c.u - _bdot(c.w, s0)
            yield
            o = _bdot(c.qg, s0) + _bdot(c.A, v_new)
            return s0, o, s0 * c.egl + _bdot(c.kdec, v_new, ta=True), c

        results = _in_lockstep([one_head(h) for h in range(heads)])
        for h, (s0, o, s1, c) in enumerate(results):
            lanes = pl.ds(h * HEAD_DIM, HEAD_DIM)
            s_ref[h, 0] = s0
            o_ref[:, lanes] = o
            t_ref[:, lanes] = jnp.concatenate([c.T, jnp.zeros((CHUNK, HEAD_DIM - CHUNK), F32)], axis=1)
            u_ref[:, lanes] = c.u
            w_ref[:, lanes] = c.w
            state[h] = s1

    return pl.pallas_call(
        body, name="gdn_core", grid=(nchunk,),
        in_specs=[pl.BlockSpec((CHUNK, 3 * gw), lambda n: (n, 0)), pl.BlockSpec((CHUNK, HEAD_DIM), lambda n: (n, 0))],
        out_specs=[pl.BlockSpec((CHUNK, gw), lambda n: (n, 0)),
                   pl.BlockSpec((heads, 1, HEAD_DIM, HEAD_DIM), lambda n: (0, n, 0, 0))]
        + [pl.BlockSpec((CHUNK, gw), lambda n: (n, 0))] * 3,
        out_shape=[jax.ShapeDtypeStruct((t, gw), F32),
                   jax.ShapeDtypeStruct((heads, nchunk, HEAD_DIM, HEAD_DIM), F32)]
        + [jax.ShapeDtypeStruct((t, gw), F32)] * 3,
        scratch_shapes=[pltpu.VMEM((heads, HEAD_DIM, HEAD_DIM), F32)],
        compiler_params=_params("arbitrary"),
    )(qkv, bg)


def _gdn_core_backward(qkv, bg, states, solved, do, heads):
    t = qkv.shape[0]
    nchunk = t // CHUNK
    n = CHUNK

    def one_head(chunk_local, s0, d_out, ds1):
        c = yield from chunk_local
        v_new = c.u - _bdot(c.w, s0)
        dqg = _bdot(d_out, s0, tb=True)
        ds0 = _bdot(c.qg, d_out, ta=True) + ds1 * c.egl
        dv_new = _bdot(c.A, d_out, ta=True) + _bdot(c.kdec, ds1)
        yield
        dA = jnp.where(c.tri, _bdot(d_out, v_new, tb=True), 0.0)
        dkdec = _bdot(v_new, ds1, tb=True)
        dgl = jnp.sum(jnp.sum(ds1 * s0, axis=1, keepdims=True), axis=0, keepdims=True) * c.egl
        dw = -_bdot(dv_new, s0, tb=True)
        ds0 = ds0 - _bdot(c.w, dv_new, ta=True)
        yield
        both = _dot22(c.T, jnp.concatenate([dv_new, dw], axis=1), ta=True)
        yield
        dvb, dkg = both[:, :HEAD_DIM], both[:, HEAD_DIM:]
        dL = jnp.where(c.strict, -(_bdot(dvb, c.u, tb=True) + _bdot(dkg, c.w, tb=True)), 0.0)
        yield
        dm1 = dL * c.decay
        dkb = _bdot(dm1, c.k) + dkg * c.eg
        dk = _bdot(dm1, c.kb, ta=True)
        dm2 = dA * c.decay
        dq = _bdot(dm2, c.k) + dqg * c.eg
        dk = dk + _bdot(dm2, c.q, ta=True) + dkdec * c.ekl + dkb * c.beta
        pm = dL * c.L + dA * c.A
        ones = jnp.ones((n, HEAD_DIM), BF16)
        pm_hi, pm_lo = _split2(pm)
        colsum = _bdot(pm_hi, ones, ta=True) + _bdot(pm_lo, ones, ta=True)
        tk_ = jnp.sum(dkdec * c.kdec, axis=1, keepdims=True)
        dgc = (jnp.sum(pm, axis=1, keepdims=True) - colsum
               + jnp.sum(dqg * c.qg, axis=1, keepdims=True)
               - tk_
               + jnp.sum(dkg * c.kg, axis=1, keepdims=True))
        dgl = dgl + jnp.sum(tk_, axis=0, keepdims=True)
        rowi = lax.broadcasted_iota(jnp.int32, (n, HEAD_DIM), 0)
        dgc = dgc + jnp.where(rowi == n - 1, dgl, 0.0)
        dbeta = jnp.sum(dkb * c.k, axis=1, keepdims=True) + jnp.sum(dvb * c.v, axis=1, keepdims=True)
        return dq * (HEAD_DIM ** -0.5), dk, dvb * c.beta, dbeta, dgc, ds0

    gw = heads * HEAD_DIM

    def body(qkv_ref, bg_ref, s_ref, t_ref, u_ref, w_ref, do_ref, dqkv_ref, dbg_ref, dstate):
        @pl.when(pl.program_id(0) == 0)
        def _():
            dstate[...] = jnp.zeros_like(dstate)

        bg_v = bg_ref[...]
        gc_all, gc_rows = _chunk_gates(bg_v, heads)
        lane = lax.broadcasted_iota(jnp.int32, (n, HEAD_DIM), 1)
        dgates = jnp.zeros((n, HEAD_DIM), F32)
        chains = []
        for h in range(heads):
            col = lambda s: pl.ds(s * gw + h * HEAD_DIM, HEAD_DIM)
            lanes = pl.ds(h * HEAD_DIM, HEAD_DIM)
            c = _chunk_local(qkv_ref[:, col(0)], qkv_ref[:, col(1)], qkv_ref[:, col(2)], bg_v[:, h:h + 1],
                             gc_all[:, heads + h:heads + h + 1], gc_rows[heads + h:heads + h + 1, :],
                             (t_ref[:, pl.ds(h * HEAD_DIM, CHUNK)], u_ref[:, lanes], w_ref[:, lanes]))
            chains.append(one_head(c, s_ref[h, 0], do_ref[:, pl.ds(h * HEAD_DIM, HEAD_DIM)], dstate[h]))
        results = _in_lockstep(chains)
        for h, (dq, dk, dv, dbeta, dgc, ds0) in enumerate(results):
            dgates = jnp.where(lane == h, dbeta, jnp.where(lane == heads + h, dgc, dgates))
        for h, (dq, dk, dv, dbeta, dgc, ds0) in enumerate(results):
            dqkv_ref[:, pl.ds(h * HEAD_DIM, HEAD_DIM)] = dq
            dqkv_ref[:, pl.ds(gw + h * HEAD_DIM, HEAD_DIM)] = dk
            dqkv_ref[:, pl.ds(2 * gw + h * HEAD_DIM, HEAD_DIM)] = dv
            dstate[h] = ds0
        row = lax.broadcasted_iota(jnp.int32, (n, n), 0)
        colm = lax.broadcasted_iota(jnp.int32, (n, n), 1)
        draw = _dot_mask((row >= colm).astype(BF16), dgates, ta=True)
        dbg_ref[...] = jnp.where(lane < heads, dgates, draw)

    last = nchunk - 1
    return pl.pallas_call(
        body, name="gdn_core_bwd", grid=(nchunk,),
        in_specs=[pl.BlockSpec((CHUNK, 3 * gw), lambda i: (last - i, 0)),
                  pl.BlockSpec((CHUNK, HEAD_DIM), lambda i: (last - i, 0)),
                  pl.BlockSpec((heads, 1, HEAD_DIM, HEAD_DIM), lambda i: (0, last - i, 0, 0))]
        + [pl.BlockSpec((CHUNK, gw), lambda i: (last - i, 0))] * 4,
        out_specs=[pl.BlockSpec((CHUNK, 3 * gw), lambda i: (last - i, 0)),
                   pl.BlockSpec((CHUNK, HEAD_DIM), lambda i: (last - i, 0))],
        out_shape=[jax.ShapeDtypeStruct((t, 3 * gw), F32), jax.ShapeDtypeStruct((t, HEAD_DIM), F32)],
        scratch_shapes=[pltpu.VMEM((heads, HEAD_DIM, HEAD_DIM), F32)],
        compiler_params=_params("arbitrary"),
    )(qkv, bg, states, *solved, do)


def _gdn_post(o, proj, z_col0, norm_w, heads, tt):
    t = o.shape[0]
    zb = z_col0 // HEAD_DIM

    def body(o_ref, z_ref, w_ref, out_ref):
        ov = o_ref[...]
        z = z_ref[...]
        rms = lax.rsqrt(jnp.mean(ov * ov, axis=-1, keepdims=True) + NORM_EPS)
        out_ref[...] = (ov * rms * w_ref[...] * (z * _sigmoid(z))).astype(BF16)

    return pl.pallas_call(
        body, name="gdn_post", grid=(t // tt, heads),
        in_specs=[pl.BlockSpec((tt, HEAD_DIM), lambda i, h: (i, h)),
                  pl.BlockSpec((tt, HEAD_DIM), lambda i, h: (i, zb + h)),
                  pl.BlockSpec((1, HEAD_DIM), lambda i, h: (0, 0))],
        out_specs=pl.BlockSpec((tt, HEAD_DIM), lambda i, h: (i, h)),
        out_shape=jax.ShapeDtypeStruct((t, heads * HEAD_DIM), BF16),
        compiler_params=_params("parallel", "parallel"),
    )(o, proj, norm_w)


def _gdn_post_backward(dcat, o, proj, z_col0, norm_w, heads, tt):
    t = o.shape[0]
    zb = z_col0 // HEAD_DIM

    def body(d_ref, o_ref, z_ref, w_ref, do_ref, dz_ref, dw_ref):
        d = d_ref[...]
        ov = o_ref[...]
        z = z_ref[...]
        w = w_ref[...]
        rms = lax.rsqrt(jnp.mean(ov * ov, axis=-1, keepdims=True) + NORM_EPS)
        ohat = ov * rms
        sg = _sigmoid(z)
        gate = z * sg
        dz_ref[...] = (d * ohat * w * (sg * (1.0 + z * (1.0 - sg)))).astype(BF16)
        don = d * gate
        dohat = don * w
        do_ref[...] = rms * (dohat - ohat * jnp.mean(dohat * ohat, axis=-1, keepdims=True))
        dw = jnp.sum(don * ohat, axis=0, keepdims=True)
        first = jnp.logical_and(pl.program_id(0) == 0, pl.program_id(1) == 0)

        @pl.when(first)
        def _():
            dw_ref[...] = dw

        @pl.when(jnp.logical_not(first))
        def _():
            dw_ref[...] += dw

    blk = pl.BlockSpec((tt, HEAD_DIM), lambda i, h: (i, h))
    return pl.pallas_call(
        body, name="gdn_post_bwd", grid=(t // tt, heads),
        in_specs=[blk, blk, pl.BlockSpec((tt, HEAD_DIM), lambda i, h: (i, zb + h)),
                  pl.BlockSpec((1, HEAD_DIM), lambda i, h: (0, 0))],
        out_specs=[blk, blk, pl.BlockSpec((1, HEAD_DIM), lambda i, h: (0, 0))],
        out_shape=[jax.ShapeDtypeStruct((t, heads * HEAD_DIM), F32),
                   jax.ShapeDtypeStruct((t, heads * HEAD_DIM), BF16),
                   jax.ShapeDtypeStruct((1, HEAD_DIM), F32)],
        compiler_params=_params("arbitrary", "arbitrary"),
    )(dcat, o, proj, norm_w)


def _pool_select(levels, group):
    out = levels[-1]
    for gi in range(len(levels) - 2, -1, -1):
        out = jnp.where(group == gi, levels[gi], out)
    return out


def _pool_counts(t, width, group):
    pos = lax.broadcasted_iota(jnp.int32, (t, width), 0)
    win = jnp.left_shift(2, group)
    return jnp.minimum(pos + 1, win).astype(F32)


def _pooled(p, group):
    levels, s, step = [], p, 1
    for _ in POOL_WINDOWS:
        s = s + _shift_down(s, step)
        levels.append(s)
        step *= 2
    cnt = _pool_counts(p.shape[0], p.shape[1], group)
    return _pool_select(levels, group) / cnt - p, cnt


def _pool_forward(proj, p_col0, pool_w, pool_scale):
    t = proj.shape[0]
    groups, cg, _ = pool_w.shape
    pb = p_col0 // cg

    def body(p_ref, w_ref, s_ref, o_ref):
        pooled, _ = _pooled(p_ref[...], pl.program_id(0))
        o_ref[...] = (_bdot(pooled, w_ref[0]) * s_ref[...]).astype(BF16)

    return pl.pallas_call(
        body, name="pool_fwd", grid=(groups,),
        in_specs=[pl.BlockSpec((t, cg), lambda g: (0, pb + g)), pl.BlockSpec((1, cg, cg), lambda g: (g, 0, 0)),
                  pl.BlockSpec((1, cg), lambda g: (0, g))],
        out_specs=pl.BlockSpec((t, cg), lambda g: (0, g)),
        out_shape=jax.ShapeDtypeStruct((t, groups * cg), BF16),
        compiler_params=_params("parallel"),
    )(proj, pool_w, pool_scale)


def _pool_backward(dcat, d_col0, proj, p_col0, pool_w, pool_scale):
    t = proj.shape[0]
    groups, cg, _ = pool_w.shape
    pb = p_col0 // cg
    db = d_col0 // cg

    def body(d_ref, p_ref, w_ref, s_ref, dp_ref, dw_ref, ds_ref):
        group = pl.program_id(0)
        pooled, cnt = _pooled(p_ref[...], group)
        w = w_ref[0]
        d = d_ref[...]
        mixed = _bdot(pooled, w)
        ds_ref[...] = jnp.sum(d * mixed, axis=0, keepdims=True)
        dmixed = d * s_ref[...]
        dw_ref[0] = _bdot(pooled, dmixed, ta=True)
        dpooled = _bdot(dmixed, w, tb=True)
        levels, s, step = [], dpooled / cnt, 1
        for _ in POOL_WINDOWS:
            s = s + _shift_up(s, step)
            levels.append(s)
            step *= 2
        dp_ref[...] = (_pool_select(levels, group) - dpooled).astype(BF16)

    return pl.pallas_call(
        body, name="pool_bwd", grid=(groups,),
        in_specs=[pl.BlockSpec((t, cg), lambda g: (0, db + g)), pl.BlockSpec((t, cg), lambda g: (0, pb + g)),
                  pl.BlockSpec((1, cg, cg), lambda g: (g, 0, 0)), pl.BlockSpec((1, cg), lambda g: (0, g))],
        out_specs=[pl.BlockSpec((t, cg), lambda g: (0, g)), pl.BlockSpec((1, cg, cg), lambda g: (g, 0, 0)),
                   pl.BlockSpec((1, cg), lambda g: (0, g))],
        out_shape=[jax.ShapeDtypeStruct((t, groups * cg), BF16), jax.ShapeDtypeStruct((groups, cg, cg), F32),
                   jax.ShapeDtypeStruct((1, groups * cg), F32)],
        compiler_params=_params("parallel"),
    )(dcat, proj, pool_w, pool_scale)


def _attention(q, k, v, tq):
    t, d = q.shape
    m = k.shape[0]
    dh = d // XATTN_HEADS
    scale = dh ** -0.5

    def body(q_ref, k_ref, v_ref, o_ref):
        s = _bdot(q_ref[...], k_ref[...], tb=True) * scale
        s = s - jnp.max(s, axis=-1, keepdims=True)
        e = jnp.exp(s)
        p = e / jnp.sum(e, axis=-1, keepdims=True)
        o_ref[...] = _bdot(p, v_ref[...]).astype(BF16)

    return pl.pallas_call(
        body, name="xattn_fwd", grid=(XATTN_HEADS, t // tq),
        in_specs=[pl.BlockSpec((tq, dh), lambda h, i: (i, h)), pl.BlockSpec((m, dh), lambda h, i: (0, h)),
                  pl.BlockSpec((m, dh), lambda h, i: (0, h))],
        out_specs=pl.BlockSpec((tq, dh), lambda h, i: (i, h)),
        out_shape=jax.ShapeDtypeStruct((t, d), BF16),
        compiler_params=_params("parallel", "parallel"),
    )(q, k, v)


def _attention_backward(q, k, v, do, tq):
    t, d = q.shape
    m = k.shape[0]
    dh = d // XATTN_HEADS
    scale = dh ** -0.5

    def body(q_ref, k_ref, v_ref, do_ref, dq_ref, dk_ref, dv_ref, dk_acc, dv_acc):
        i = pl.program_id(1)
        qv, kv, vv, dov = q_ref[...], k_ref[...], v_ref[...], do_ref[...]
        s = _bdot(qv, kv, tb=True) * scale
        s = s - jnp.max(s, axis=-1, keepdims=True)
        e = jnp.exp(s)
        p = e / jnp.sum(e, axis=-1, keepdims=True)
        dp = _bdot(dov, vv, tb=True)
        ds = p * (dp - jnp.sum(dp * p, axis=-1, keepdims=True)) * scale
        dq_ref[...] = _bdot(ds, kv).astype(BF16)
        dv_part = _bdot(p, dov, ta=True)
        dk_part = _bdot(ds, qv, ta=True)

        @pl.when(i == 0)
        def _():
            dk_acc[...] = dk_part
            dv_acc[...] = dv_part

        @pl.when(i > 0)
        def _():
            dk_acc[...] += dk_part
            dv_acc[...] += dv_part

        @pl.when(i == pl.num_programs(1) - 1)
        def _():
            dk_ref[...] = dk_acc[...].astype(BF16)
            dv_ref[...] = dv_acc[...].astype(BF16)

    qblk = pl.BlockSpec((tq, dh), lambda h, i: (i, h))
    kblk = pl.BlockSpec((m, dh), lambda h, i: (0, h))
    return pl.pallas_call(
        body, name="xattn_bwd", grid=(XATTN_HEADS, t // tq),
        in_specs=[qblk, kblk, kblk, qblk],
        out_specs=[qblk, kblk, kblk],
        out_shape=[jax.ShapeDtypeStruct((t, d), BF16), jax.ShapeDtypeStruct((m, d), BF16),
                   jax.ShapeDtypeStruct((m, d), BF16)],
        scratch_shapes=[pltpu.VMEM((m, dh), F32), pltpu.VMEM((m, dh), F32)],
        compiler_params=_params("parallel", "arbitrary"),
    )(q, k, v, do)


def _ln_backward_rows(name, dmain, dres, xhat, rstd, gamma, tm):
    t, d = xhat.shape

    def body(m_ref, r_ref, x_ref, s_ref, g_ref, du_ref, dub_ref, dg_ref, db_ref):
        du, dg, db = _ln_backward_math(m_ref[...] + ALPHA * r_ref[...], x_ref[...], s_ref[...], g_ref[...])
        du_ref[...] = du
        dub_ref[...] = du.astype(BF16)
        first = pl.program_id(0) == 0

        @pl.when(first)
        def _():
            dg_ref[...] = dg
            db_ref[...] = db

        @pl.when(jnp.logical_not(first))
        def _():
            dg_ref[...] += dg
            db_ref[...] += db

    row = pl.BlockSpec((tm, d), lambda i: (i, 0))
    vec = pl.BlockSpec((1, d), lambda i: (0, 0))
    return pl.pallas_call(
        body, name=name, grid=(t // tm,),
        in_specs=[row, row, row, pl.BlockSpec((tm, 1), lambda i: (i, 0)), vec],
        out_specs=[row, row, vec, vec],
        out_shape=[jax.ShapeDtypeStruct((t, d), F32), jax.ShapeDtypeStruct((t, d), BF16),
                   jax.ShapeDtypeStruct((1, d), F32), jax.ShapeDtypeStruct((1, d), F32)],
        compiler_params=_params("arbitrary"),
    )(dmain, dres, xhat, rstd, gamma)


def _loss_and_ln_backward(xhat, rstd, gamma, beta, target, tm):
    t, d = xhat.shape

    def body(x_ref, r_ref, g_ref, b_ref, t_ref, du_ref, dub_ref, dg_ref, db_ref, loss_ref):
        xh = x_ref[...]
        g = g_ref[...]
        diff = xh * g + b_ref[...] - t_ref[...]
        part = jnp.sum(jnp.sum(diff * diff, axis=1, keepdims=True), axis=0, keepdims=True) * (0.5 / d)
        dy = diff * (1.0 / d)
        du, dg, db = _ln_backward_math(dy, xh, r_ref[...], g)
        du_ref[...] = du
        dub_ref[...] = du.astype(BF16)
        lossrow = jnp.broadcast_to(part, (1, HEAD_DIM))
        first = pl.program_id(0) == 0

        @pl.when(first)
        def _():
            dg_ref[...] = dg
            db_ref[...] = db
            loss_ref[...] = lossrow

        @pl.when(jnp.logical_not(first))
        def _():
            dg_ref[...] += dg
            db_ref[...] += db
            loss_ref[...] += lossrow

    row = pl.BlockSpec((tm, d), lambda i: (i, 0))
    vec = pl.BlockSpec((1, d), lambda i: (0, 0))
    return pl.pallas_call(
        body, name="loss_ln3_bwd", grid=(t // tm,),
        in_specs=[row, pl.BlockSpec((tm, 1), lambda i: (i, 0)), vec, vec, row],
        out_specs=[row, row, vec, vec, pl.BlockSpec((1, HEAD_DIM), lambda i: (0, 0))],
        out_shape=[jax.ShapeDtypeStruct((t, d), F32), jax.ShapeDtypeStruct((t, d), BF16),
                   jax.ShapeDtypeStruct((1, d), F32), jax.ShapeDtypeStruct((1, d), F32),
                   jax.ShapeDtypeStruct((1, HEAD_DIM), F32)],
        compiler_params=_params("arbitrary"),
    )(xhat, rstd, gamma, beta, target)


def _after(token, a):
    return a if token is None else a + token[:1, :1].astype(a.dtype)


def _pick(n, prefs):
    for p in prefs:
        if n % p == 0:
            return p
    return n


def _local_step(x, mem, target, w, x_bf=None):
    t, d = x.shape
    heads = w["a_log"].shape[1]
    gw = heads * HEAD_DIM
    groups, cg, _ = w["pool_w"].shape
    pw = groups * cg
    n_main = 4 * gw + pw
    in_cols = n_main + 2 * heads
    s_in = w["w_in_t"].shape[0]

    tm = _pick(t, (512, 256, 128))
    tm_ln = _pick(t, (256, 128))
    tm_big = _pick(t, (1024, 512, 256, 128))
    tk = _pick(d, K_STEPS)

    w_in_t = w["w_in_t"].reshape(in_cols, d)
    w_p_t = w_in_t[4 * gw + 2 * heads:]
    w_ba_t = jnp.pad(w_in_t[4 * gw:4 * gw + 2 * heads], ((0, HEAD_DIM - 2 * heads), (0, 0)))
    x_bf = x.astype(BF16) if x_bf is None else x_bf
    mem_bf = mem.astype(BF16)

    tn_d = _pick(d, (1024, 512, 256, 128))
    proj = _plain("proj_main", x_bf, w_in_t, tb=True, n_used=4 * gw, tm=tm_big, tn=_pick(4 * gw, (1024, 512, 256, 128)),
                  tk=tk, out_dtype=F32)
    pproj = _plain("proj_pool", x_bf, w_p_t, tb=True, tm=tm_big, tn=_pick(pw, (1024, 512, 256, 128)), tk=tk, out_dtype=F32)
    ea, dtb = _gate_vectors(w["a_log"], w["dt_bias"], heads)
    vec128 = lambda i, j: (0, 0)
    ba, bg = _matmul(
        "proj_gates", x_bf, w_ba_t, tb=True, tm=tm, tn=HEAD_DIM, tk=tk,
        extra=[(ea, (1, HEAD_DIM), vec128), (dtb, (1, HEAD_DIM), vec128)],
        outs=[(jax.ShapeDtypeStruct((t, HEAD_DIM), F32), (tm, HEAD_DIM), _tile)] * 2,
        epilogue=_gates_epilogue(heads))
    qkv = _gdn_pre(proj, w["conv_w"], heads)
    o_gdn, states, *solved = _gdn_core(qkv, bg, heads)
    cat_g = _gdn_post(o_gdn, proj, 3 * gw, w["gdn_norm_w"], heads, tm_big)
    token = yield ("pass", 1, cat_g)
    cat_p = _pool_forward(pproj, 0, w["pool_w"], _after(token, w["pool_scale"]))
    cat = jnp.concatenate([cat_g, cat_p], axis=1)
    w = {**w, **(yield ("weights", 1, cat))}
    h1, h1_bf, xhat1, rstd1 = _ln_forward("mix_ln1", cat, w["w_out"], x, w["ln1_g"], w["ln1_b"], tm=tm_ln, tk=tk)

    h1_bf = _after((yield ("relay", None, h1_bf)), h1_bf)
    q = _plain("xattn_q", h1_bf, w["xq_w"], tm=tm, tn=tn_d, tk=tk, out_dtype=BF16)
    mlen = mem.shape[0]
    tm_mem = _pick(mlen, (256, 128))
    k = _plain("xattn_k", mem_bf, w["xk_w"], tm=tm_mem, tn=tn_d, tk=tk, out_dtype=BF16)
    v = _plain("xattn_v", mem_bf, w["xv_w"], tm=tm_mem, tn=tn_d, tk=tk, out_dtype=BF16)
    att = _attention(q, k, v, tm)
    h2, h2_bf, xhat2, rstd2 = _ln_forward("xo_ln2", att, w["xo_w"], h1, w["ln2_g"], w["ln2_b"], tm=tm_ln, tk=tk)

    w = {**w, **(yield ("weights", 2, h2_bf))}
    s_up = w["w_up3"].shape[0]
    ff = s_up * w["w_up3"].shape[2]
    tn_f = _pick(ff // s_up, (1024, 512, 256, 128))

    def up_epi(acc, ex, out, i):
        r = jnp.maximum(acc, 0.0)
        out[0][...] = (r * r).astype(BF16)
        out[1][...] = (2.0 * r).astype(BF16)

    act, act_grad = _matmul(
        "mlp_up", h2_bf, w["w_up3"], b_blocks=s_up, tm=tm_big, tn=tn_f, tk=tk,
        outs=[(jax.ShapeDtypeStruct((t, ff), BF16), (tm_big, tn_f), _tile)] * 2, epilogue=up_epi)
    w = {**w, **(yield ("weights", 3, act))}
    tk_f = _pick(ff, K_STEPS)
    xhat3, rstd3 = _ln_forward("down_ln3", act, w["w_down"], h2, w["ln3_g"], w["ln3_b"], tm=tm, tk=tk_f, want_h=False)

    grads = {}
    du3, du3_bf, grads["ln3_g"], grads["ln3_b"], loss = _loss_and_ln_backward(
        xhat3, rstd3, w["ln3_g"], w["ln3_b"], target, tm)

    def dup_epi(acc, ex, out, i):
        out[0][...] = (acc * ex[0][...].astype(F32)).astype(BF16)

    dup = _matmul(
        "mlp_down_dx", du3_bf, w["w_down"], tb=True, tm=tm_big, tn=tn_f, tk=tk,
        extra=[(act_grad, (tm_big, tn_f), _tile)],
        outs=[(jax.ShapeDtypeStruct((t, ff), BF16), (tm_big, tn_f), _tile)], epilogue=dup_epi)[0]
    tk_t = _pick(t, K_STEPS)
    tm_w = _pick(d, (512, 256, 128))
    grads["w_down"] = _plain("mlp_down_dw", act, du3_bf, ta=True, tm=_pick(ff, (512, 256, 128)), tn=d, tk=tk_t,
                             out_dtype=F32)
    grads["w_up3"] = _plain("mlp_up_dw", h2_bf, dup, ta=True, tm=tm_w, tn=ff // s_up, tk=tk_t, out_dtype=F32, out3=s_up,
                            n_outer=True)
    token = yield ("grads", 0, {n: grads.pop(n) for n in ("w_down", "w_up3")})
    dh2 = _plain("mlp_up_dx", dup, w["w_up3"], tb=True, b_blocks=s_up, tm=tm_big, tn=tn_d,
                 tk=_pick(ff // s_up, K_STEPS), out_dtype=F32)
    du2, du2_bf, grads["ln2_g"], grads["ln2_b"] = _ln_backward_rows(
        "ln2_bwd", dh2, du3, xhat2, rstd2, _after(token, w["ln2_g"]), tm)
    token = yield ("poll", 0, du2_bf)

    grads["xo_w"] = _plain("xo_dw", att, du2_bf, ta=True, tm=tm_w, tn=d, tk=tk_t, out_dtype=F32)
    datt = _plain("xo_dx", du2_bf, w["xo_w"], tb=True, tm=tm, tn=tn_d, tk=tk, out_dtype=BF16)
    dq, dk, dv = _attention_backward(q, k, v, datt, tm)
    tk_m = _pick(mlen, (256, 128))
    grads["xq_w"] = _plain("xq_dw", h1_bf, dq, ta=True, tm=tm_w, tn=d, tk=tk_t, out_dtype=F32)
    grads["xk_w"] = _plain("xk_dw", mem_bf, dk, ta=True, tm=tm_w, tn=tn_d, tk=tk_m, out_dtype=F32)
    grads["xv_w"] = _plain("xv_dw", mem_bf, dv, ta=True, tm=tm_w, tn=tn_d, tk=tk_m, out_dtype=F32)
    du1, du1_bf, grads["ln1_g"], grads["ln1_b"] = _ln_backward(
        "xq_dx_ln1", dq, w["xq_w"], du2, xhat1, rstd1, _after(token, w["ln1_g"]), tm=tm_ln, tk=tk)

    grads["w_out"] = _plain("out_dw", cat, du1_bf, ta=True, tm=tm_w, tn=d, tk=tk_t, out_dtype=F32)
    token = yield ("grads", 1, {n: grads.pop(n) for n in ("xo_w", "xq_w", "xk_w", "xv_w", "w_out")})
    dcat = _plain("out_dx", du1_bf, w["w_out"], tb=True, tm=tm, tn=tn_d, tk=tk, out_dtype=F32)
    dp, grads["pool_w"], grads["pool_scale"] = _pool_backward(dcat, gw, pproj, 0, w["pool_w"],
                                                              _after(token, w["pool_scale"]))
    do_gdn, dz, grads["gdn_norm_w"] = _gdn_post_backward(dcat, o_gdn, proj, 3 * gw, _after(token, w["gdn_norm_w"]),
                                                         heads, tm_big)
    dqkv, dbg = _gdn_core_backward(qkv, bg, states, solved, do_gdn, heads)
    token = yield ("poll", 1, dqkv)
    dqkv_pre, grads["conv_w"] = _gdn_pre_backward(proj, _after(token, w["conv_w"]), dqkv, heads)
    dba, dalog_row, ddt_row = _gates_backward(ba, bg, dbg, ea, dtb, heads)
    grads["a_log"] = dalog_row[:, heads:2 * heads]
    grads["dt_bias"] = ddt_row[:, heads:2 * heads]

    k_pad = -(-in_cols // HEAD_DIM) * HEAD_DIM
    dproj = jnp.concatenate([dqkv_pre, dz, dba[:, :2 * heads], dp, jnp.zeros((t, k_pad - in_cols), BF16)], axis=1)
    dw_in_t = _plain("proj_dw", dproj, x_bf, ta=True, tm=_pick(k_pad, (512, 256, 128)), tn=d, tk=tk_t, out_dtype=F32,
                     m_kept=in_cols)
    grads["w_in_t"] = dw_in_t.reshape(s_in, in_cols // s_in, d)

    def dx_epi(acc, ex, out, i):
        out[0][...] = acc + ALPHA * ex[0][...]

    token = yield ("grads", 2, {n: grads.pop(n) for n in ("w_in_t", "pool_w")})
    w_in_t_pad = jnp.concatenate([w_in_t, _after(token, jnp.zeros((k_pad - in_cols, d), BF16))], axis=0)
    grad_x = _matmul(
        "proj_dx", dproj, w_in_t_pad, tm=tm, tn=tn_d, tk=k_pad, extra=[(du1, (tm, tn_d), _tile)],
        outs=[(jax.ShapeDtypeStruct((t, d), F32), (tm, tn_d), _tile)], epilogue=dx_epi)[0]
    yield ("poll", 2, grad_x)
    return loss, grad_x, grads


def _adamw(name, w, g, m, v):
    r, c = w.shape
    if r % 8 == 0:
        tr = _pick(r, (256, 128, 64, 32, 16, 8))
        blk, steps = pl.BlockSpec((tr, c), lambda i: (i, 0)), r // tr
    else:
        tc = _pick(c, (256, 128))
        blk, steps = pl.BlockSpec((r, tc), lambda i: (0, i)), c // tc
    c1 = 1.0 - ADAM_B1 ** ADAM_STEP
    c2 = 1.0 - ADAM_B2 ** ADAM_STEP

    def body(w_ref, g_ref, m_ref, v_ref, d_ref, mo_ref, vo_ref, go_ref):
        gv = g_ref[...]
        mn = ADAM_B1 * m_ref[...] + (1.0 - ADAM_B1) * gv
        vn = ADAM_B2 * v_ref[...] + (1.0 - ADAM_B2) * (gv * gv)
        d_ref[...] = -ADAM_LR * ((mn / c1) / (jnp.sqrt(vn / c2) + ADAM_EPS) + ADAM_WD * w_ref[...])
        mo_ref[...] = mn
        vo_ref[...] = vn
        go_ref[...] = gv

    return pl.pallas_call(
        body, name=name, grid=(steps,), in_specs=[blk] * 4, out_specs=[blk] * 4,
        out_shape=[jax.ShapeDtypeStruct((r, c), F32)] * 4,
        compiler_params=_params("parallel"),
    )(w, g, m, v)


def _place():
    x, y, c = lax.axis_index("x"), lax.axis_index("y"), lax.axis_index("c")
    chips = [(1 - x, y), (x, 1 - y), (1 - x, 1 - y)]
    return x, y, c, chips


HBM = pl.BlockSpec(memory_space=pltpu.HBM)


SEM = pl.BlockSpec(memory_space=pltpu.SEMAPHORE)
ANY = pl.BlockSpec(memory_space=pl.ANY)
EFFECT = pltpu.SideEffectType.DATAFLOW_SIDE_EFFECTING


def _in_hbm(a):
    return pltpu.with_memory_space_constraint(a, pltpu.HBM)


def _remote(src, dst, send_sem, recv_sem, to):
    return pltpu.make_async_remote_copy(src_ref=src, dst_ref=dst, send_sem=send_sem, recv_sem=recv_sem,
                                        device_id=to, device_id_type=MESH)


def _by_rows(rows):
    return rows % 32 == 0


def _half_shape(rows, cols):
    return (rows // 2, cols) if _by_rows(rows) else (rows, cols // 2)


def _half(ref, which, *lead):
    rows, cols = ref.shape[-2:]
    if _by_rows(rows):
        return ref.at[(*lead, pl.ds(which * (rows // 2), rows // 2))]
    return ref.at[(*lead, slice(None), pl.ds(which * (cols // 2), cols // 2))]


def _landed(lands, i, shard_index, which):
    return _half(lands[i], which, shard_index)


def _routes():
    x, y, c, _ = _place()
    first = (jnp.where(c == 0, 1 - x, x), jnp.where(c == 0, y, 1 - y))
    second = (jnp.where(c == 0, x, 1 - x), jnp.where(c == 0, 1 - y, y))
    return first, second, (1 - x, 1 - y)


def _shard_of(chip):
    return 2 * chip[0] + chip[1]


def _gather_start(name, shards, after, relayed=()):
    n = len(shards)
    lands = [lax.empty((N_SHARD,) + s.shape, s.dtype) for s in shards]

    def body(*refs):
        ins, zones = refs[:n], refs[n:2 * n]
        ici_send, ici_recv, own_send, own_recv = refs[2 * n + 1:2 * n + 5]
        token = refs[-1]
        x, y, c, chips = _place()
        me = 2 * x + y
        first, _, _ = _routes()
        for i in range(n):
            if i in relayed:
                _remote(_half(ins[i], c), _landed(zones, i, me, c), ici_send.at[3 * i], ici_recv.at[3 * i],
                        (*first, c)).start()
                continue
            for j, chip in enumerate(chips):
                _remote(_half(ins[i], c), _landed(zones, i, me, c), ici_send.at[3 * i + j],
                        ici_recv.at[3 * i + j], (*chip, c)).start()
        for i in range(n):
            _remote(ins[i], zones[i].at[me], own_send.at[i], own_recv.at[i], (x, y, 1 - c)).start()
        token[...] = jnp.zeros_like(token)

    dma = pltpu.SemaphoreType.DMA
    outs = pl.pallas_call(
        body, name=name,
        in_specs=[HBM] * (2 * n) + [ANY],
        out_shape=(dma((3 * n,)), dma((3 * n,)), dma((n,)), dma((n,)),
                   *[pltpu.HBM(a.shape, a.dtype) for a in shards + lands], jax.ShapeDtypeStruct((8, LANES), F32)),
        out_specs=(SEM, SEM, SEM, SEM, *[HBM] * (2 * n), pl.BlockSpec(memory_space=pltpu.VMEM)),
        input_output_aliases={k: 4 + k for k in range(2 * n)},
        compiler_params=pltpu.CompilerParams(has_side_effects=EFFECT),
    )(*[_in_hbm(a) for a in shards + lands], after)
    sems = dict(zip(("ici_send", "ici_recv", "own_send", "own_recv"), outs[:4]))
    return sems, list(outs[4:4 + n]), list(outs[4 + n:4 + 2 * n]), outs[-1]


def _gather_forward(name, idx, lands, sems, after):
    n = len(idx)

    def body(*refs):
        zones = refs[:n]
        ici_recv = refs[n]
        fwd_send, fwd_recv = refs[n + 2], refs[n + 3]
        x, y, c, chips = _place()
        for k, i in enumerate(idx):
            for j, chip in enumerate(chips):
                half = _landed(zones, k, 2 * chip[0] + chip[1], c)
                _remote(half, half, fwd_send.at[3 * k + j], ici_recv.at[3 * i + j], (*chip, c)).wait_recv()
                _remote(half, half, fwd_send.at[3 * k + j], fwd_recv.at[3 * k + j], (x, y, 1 - c)).start()
        refs[-1][...] = jnp.zeros_like(refs[-1])

    dma = pltpu.SemaphoreType.DMA
    outs = pl.pallas_call(
        body, name=name,
        in_specs=[HBM] * n + [SEM, ANY],
        out_shape=(dma((3 * n,)), dma((3 * n,)), *[pltpu.HBM(a.shape, a.dtype) for a in lands],
                   jax.ShapeDtypeStruct((8, LANES), F32)),
        out_specs=(SEM, SEM, *[HBM] * n, pl.BlockSpec(memory_space=pltpu.VMEM)),
        input_output_aliases={k: 2 + k for k in range(n)},
        compiler_params=pltpu.CompilerParams(has_side_effects=EFFECT),
    )(*lands, sems["ici_recv"], after)
    return (outs[0], outs[1]), list(outs[2:2 + n]), outs[-1]


def _gather_wait(name, idx, shards, lands, sems, fwd, after):
    n = len(idx)

    def body(*refs):
        ins, zones = refs[:n], refs[n:2 * n]
        ici_send, own_send, own_recv, fwd_send, fwd_recv = refs[2 * n:2 * n + 5]
        x, y, c, chips = _place()
        me = 2 * x + y
        for k, i in enumerate(idx):
            mine = _half(ins[k], c)
            for j, chip in enumerate(chips):
                theirs = 2 * chip[0] + chip[1]
                _remote(mine, _landed(zones, k, me, c), ici_send.at[3 * i + j], fwd_recv.at[3 * k + j],
                        (*chip, c)).wait_send()
                sent = _landed(zones, k, theirs, c)
                _remote(sent, sent, fwd_send.at[3 * k + j], fwd_recv.at[3 * k + j], (x, y, 1 - c)).wait_send()
                passed = _landed(zones, k, theirs, 1 - c)
                _remote(passed, passed, fwd_send.at[3 * k + j], fwd_recv.at[3 * k + j], (x, y, 1 - c)).wait_recv()
            own = _remote(ins[k], zones[k].at[me], own_send.at[i], own_recv.at[i], (x, y, 1 - c))
            own.wait_send()
            own.wait_recv()

    outs = pl.pallas_call(
        body, name=name,
        in_specs=[HBM] * (2 * n) + [SEM] * 5 + [ANY],
        out_shape=tuple(pltpu.HBM(a.shape, a.dtype) for a in lands),
        out_specs=tuple([HBM] * n),
        input_output_aliases={n + k: k for k in range(n)},
        compiler_params=pltpu.CompilerParams(has_side_effects=EFFECT),
    )(*shards, *lands, sems["ici_send"], sems["own_send"], sems["own_recv"], fwd[0], fwd[1], after)
    return list(outs)


def _gather_relay(name, idx, shards, lands, sems, after):
    n = len(idx)

    def body(*refs):
        ins, zones, ici_recv = refs[:n], refs[n:2 * n], refs[2 * n]
        relay_send, relay_recv, pass_send, pass_recv = refs[2 * n + 2:2 * n + 6]
        x, y, c, _ = _place()
        first, second, _ = _routes()
        for k, i in enumerate(idx):
            landed = _landed(zones, k, _shard_of(first), c)
            _remote(landed, landed, pass_send.at[k], ici_recv.at[3 * i], (*first, c)).wait_recv()
            _remote(_half(ins[k], c), _landed(zones, k, 2 * x + y, c), relay_send.at[2 * k], relay_recv.at[2 * k],
                    (*second, c)).start()
            _remote(landed, landed, relay_send.at[2 * k + 1], relay_recv.at[2 * k + 1], (*second, c)).start()
            _remote(landed, landed, pass_send.at[k], pass_recv.at[k], (x, y, 1 - c)).start()
        refs[-1][...] = jnp.zeros_like(refs[-1])

    dma = pltpu.SemaphoreType.DMA
    outs = pl.pallas_call(
        body, name=name,
        in_specs=[HBM] * (2 * n) + [SEM, ANY],
        out_shape=(dma((2 * n,)), dma((2 * n,)), dma((n,)), dma((n,)), *[pltpu.HBM(a.shape, a.dtype) for a in lands],
                   jax.ShapeDtypeStruct((8, LANES), F32)),
        out_specs=(SEM, SEM, SEM, SEM, *[HBM] * n, pl.BlockSpec(memory_space=pltpu.VMEM)),
        input_output_aliases={n + k: 4 + k for k in range(n)},
        compiler_params=pltpu.CompilerParams(has_side_effects=EFFECT),
    )(*shards, *lands, sems["ici_recv"], after)
    return outs[:4], list(outs[4:4 + n]), outs[-1]


def _gather_forward_relayed(name, ks, lands, relay, after):
    n = len(ks)

    def body(*refs):
        zones, relay_recv = refs[:n], refs[n]
        fwd_send, fwd_recv = refs[n + 2], refs[n + 3]
        x, y, c, _ = _place()
        _, second, diagonal = _routes()
        for p, k in enumerate(ks):
            for j, chip in enumerate((second, diagonal)):
                landed = _landed(zones, p, _shard_of(chip), c)
                _remote(landed, landed, fwd_send.at[2 * p + j], relay_recv.at[2 * k + j], (*second, c)).wait_recv()
                _remote(landed, landed, fwd_send.at[2 * p + j], fwd_recv.at[2 * p + j], (x, y, 1 - c)).start()

    dma = pltpu.SemaphoreType.DMA
    outs = pl.pallas_call(
        body, name=name,
        in_specs=[HBM] * n + [SEM, ANY],
        out_shape=(dma((2 * n,)), dma((2 * n,)), *[pltpu.HBM(a.shape, a.dtype) for a in lands]),
        out_specs=(SEM, SEM, *[HBM] * n),
        input_output_aliases={k: 2 + k for k in range(n)},
        compiler_params=pltpu.CompilerParams(has_side_effects=EFFECT),
    )(*lands, relay[1], after)
    return (outs[0], outs[1]), list(outs[2:])


def _gather_wait_relayed(name, idx, ks, shards, lands, sems, relay, fwd, after):
    n = len(idx)

    def body(*refs):
        ins, zones = refs[:n], refs[n:2 * n]
        ici_send, own_send, own_recv, relay_send, pass_send, pass_recv, fwd_send, fwd_recv = refs[2 * n:2 * n + 8]
        x, y, c, _ = _place()
        me = 2 * x + y
        sibling = (x, y, 1 - c)
        first, second, diagonal = _routes()
        for p, (i, k) in enumerate(zip(idx, ks)):
            mine, at_peer = _half(ins[p], c), _landed(zones, p, me, c)
            from_first = _landed(zones, p, _shard_of(first), c)
            _remote(mine, at_peer, ici_send.at[3 * i], pass_recv.at[k], (*first, c)).wait_send()
            _remote(mine, at_peer, relay_send.at[2 * k], pass_recv.at[k], (*second, c)).wait_send()
            _remote(from_first, from_first, relay_send.at[2 * k + 1], pass_recv.at[k], (*second, c)).wait_send()
            _remote(from_first, from_first, pass_send.at[k], pass_recv.at[k], sibling).wait_send()
            theirs = _landed(zones, p, _shard_of(second), 1 - c)
            _remote(theirs, theirs, pass_send.at[k], pass_recv.at[k], sibling).wait_recv()
            for j, (sent, got) in enumerate(((second, first), (diagonal, diagonal))):
                out_half = _landed(zones, p, _shard_of(sent), c)
                _remote(out_half, out_half, fwd_send.at[2 * p + j], fwd_recv.at[2 * p + j], sibling).wait_send()
                in_half = _landed(zones, p, _shard_of(got), 1 - c)
                _remote(in_half, in_half, fwd_send.at[2 * p + j], fwd_recv.at[2 * p + j], sibling).wait_recv()
            own = _remote(ins[p], zones[p].at[me], own_send.at[i], own_recv.at[i], sibling)
            own.wait_send()
            own.wait_recv()

    outs = pl.pallas_call(
        body, name=name,
        in_specs=[HBM] * (2 * n) + [SEM] * 8 + [ANY],
        out_shape=tuple(pltpu.HBM(a.shape, a.dtype) for a in lands),
        out_specs=tuple([HBM] * n),
        input_output_aliases={n + k: k for k in range(n)},
        compiler_params=pltpu.CompilerParams(has_side_effects=EFFECT),
    )(*shards, *lands, sems["ici_send"], sems["own_send"], sems["own_recv"], relay[0], relay[2], relay[3],
      fwd[0], fwd[1], after)
    return list(outs)


def _all_reduce_small(name, slab, after=None):
    r, width = slab.shape
    ndev = 8

    def body(x_ref, after_ref, out_ref, buf, send_sems, recv_sems):
        x, y, c, _ = _place()
        me = 4 * x + 2 * y + c
        buf[me] = x_ref[...]
        copies = []
        for k in range(1, ndev):
            peer = jnp.bitwise_xor(me, k)
            to = (peer // 4, (peer // 2) % 2, peer % 2)
            cp = pltpu.make_async_remote_copy(src_ref=x_ref, dst_ref=buf.at[me], send_sem=send_sems.at[k - 1],
                                              recv_sem=recv_sems.at[k - 1], device_id=to, device_id_type=MESH)
            cp.start()
            copies.append(cp)
        for k in range(1, ndev):
            peer = jnp.bitwise_xor(me, k)
            pltpu.make_async_remote_copy(src_ref=x_ref, dst_ref=buf.at[peer], send_sem=send_sems.at[k - 1],
                                         recv_sem=recv_sems.at[k - 1], device_id=(x, y, c),
                                         device_id_type=MESH).wait_recv()
        for cp in copies:
            cp.wait_send()
        total = buf[0]
        for d in range(1, ndev):
            total = total + buf[d]
        out_ref[...] = total

    return pl.pallas_call(
        body, name=name,
        in_specs=[pl.BlockSpec(memory_space=pltpu.VMEM), ANY], out_specs=pl.BlockSpec(memory_space=pltpu.VMEM),
        out_shape=jax.ShapeDtypeStruct((r, width), F32),
        scratch_shapes=[pltpu.VMEM((ndev, r, width), F32), pltpu.SemaphoreType.DMA((ndev - 1,)),
                        pltpu.SemaphoreType.DMA((ndev - 1,))],
        compiler_params=pltpu.CompilerParams(vmem_limit_bytes=VMEM_LIMIT),
    )(slab, slab if after is None else after)


def _half_tiling(rows, cols):
    if _by_rows(rows):
        tr = _pick(rows // 2, (256, 128, 64, 32, 16))
        nb = (rows // 2) // tr
        return (tr, cols), nb, (lambda which, b: (which * nb + b, 0)), (lambda b: (b, 0))
    tc = _pick(cols // 2, (256, 128))
    nb = (cols // 2) // tc
    return (rows, tc), nb, (lambda which, b: (0, which * nb + b)), (lambda b: (0, b))


def _chip_partial(name, grad, other, core):
    s, r, cdim = grad.shape
    blk, nb, whole, within = _half_tiling(r, cdim)

    def body(core_ref, g_ref, o_ref, out_ref):
        out_ref[...] = (g_ref[...] + o_ref[...]).astype(BF16)

    return pl.pallas_call(
        body, name=name,
        grid_spec=pltpu.PrefetchScalarGridSpec(
            num_scalar_prefetch=1, grid=(s, nb),
            in_specs=[pl.BlockSpec((None,) + blk, lambda j, b, core_ref: (j,) + whole(core_ref[0], b)),
                      pl.BlockSpec((None,) + blk, lambda j, b, core_ref: (j,) + within(b))],
            out_specs=pl.BlockSpec((None,) + blk, lambda j, b, core_ref: (j,) + within(b))),
        out_shape=jax.ShapeDtypeStruct((s,) + _half_shape(r, cdim), BF16),
        compiler_params=_params("parallel", "parallel"),
    )(core, grad, other)


def _partial_copies(ins, zones, send_sems, recv_sems):
    x, y, c, chips = _place()
    return [_remote(ins[i].at[2 * chip[0] + chip[1]], zones[i].at[j], send_sems.at[3 * i + j],
                    recv_sems.at[3 * i + j], (*chip, c))
            for i in range(len(ins)) for j, chip in enumerate(chips)]


def _swap_copies(ins, zones, send_sems, recv_sems):
    x, y, c, _ = _place()
    copies = []
    for i in range(len(ins)):
        for s in range(N_SHARD):
            copies.append(_remote(_half(ins[i], 1 - c, s), zones[i].at[s],
                                  send_sems.at[N_SHARD * i + s], recv_sems.at[N_SHARD * i + s], (x, y, 1 - c)))
    return copies


def _exchange_start(name, plan, sources, lands, per_array):
    n = len(sources)
    lands = [lax.empty(shape, dtype) for shape, dtype in lands]

    def body(*refs):
        for cp in plan(refs[:n], refs[n:2 * n], refs[2 * n], refs[2 * n + 1]):
            cp.start()
        refs[-1][...] = jnp.zeros_like(refs[-1])

    dma = pltpu.SemaphoreType.DMA
    outs = pl.pallas_call(
        body, name=name,
        in_specs=[HBM] * (2 * n),
        out_shape=(dma((per_array * n,)), dma((per_array * n,)),
                   *[pltpu.HBM(a.shape, a.dtype) for a in list(sources) + lands], jax.ShapeDtypeStruct((8, LANES), F32)),
        out_specs=(SEM, SEM, *[HBM] * (2 * n), pl.BlockSpec(memory_space=pltpu.VMEM)),
        input_output_aliases={k: 2 + k for k in range(2 * n)},
        compiler_params=pltpu.CompilerParams(has_side_effects=EFFECT),
    )(*[_in_hbm(a) for a in list(sources) + lands])
    return (outs[0], outs[1]), list(outs[2:2 + n]), list(outs[2 + n:2 + 2 * n]), outs[-1]


def _exchange_wait(name, plan, started, after):
    sems, partials, lands, _ = started
    n = len(partials)

    def body(*refs):
        for cp in plan(refs[:n], refs[n:2 * n], refs[2 * n], refs[2 * n + 1]):
            cp.wait_send()
            cp.wait_recv()

    outs = pl.pallas_call(
        body, name=name,
        in_specs=[HBM] * (2 * n) + [SEM, SEM] + [ANY] * len(after),
        out_shape=tuple(pltpu.HBM(a.shape, a.dtype) for a in lands),
        out_specs=tuple([HBM] * n),
        input_output_aliases={n + k: k for k in range(n)},
        compiler_params=pltpu.CompilerParams(has_side_effects=EFFECT),
    )(*partials, *lands, sems[0], sems[1], *after)
    return list(outs)


def _reduce_own(name, grad, other, received, where):
    s, r, cdim = grad.shape
    blk, nb, whole, within = _half_tiling(r, cdim)

    def body(where_ref, g_ref, o_ref, r_ref, out_ref):
        total = g_ref[...] + o_ref[...]
        for j in range(3):
            total = total + r_ref[j].astype(F32)
        out_ref[...] = total

    return pl.pallas_call(
        body, name=name,
        grid_spec=pltpu.PrefetchScalarGridSpec(
            num_scalar_prefetch=1, grid=(nb,),
            in_specs=[pl.BlockSpec((None,) + blk, lambda b, w_ref: (w_ref[0],) + whole(w_ref[1], b)),
                      pl.BlockSpec((None,) + blk, lambda b, w_ref: (w_ref[0],) + within(b)),
                      pl.BlockSpec((3,) + blk, lambda b, w_ref: (0,) + within(b))],
            out_specs=pl.BlockSpec(blk, lambda b, w_ref: whole(w_ref[1], b))),
        out_shape=jax.ShapeDtypeStruct((r, cdim), F32),
        compiler_params=_params("parallel"),
    )(where, grad, other, received)


def _join_start(name, halves):
    n = len(halves)

    def body(*refs):
        bufs, send_sems, recv_sems = refs[:n], refs[n], refs[n + 1]
        x, y, c, _ = _place()
        for i in range(n):
            mine = _half(bufs[i], c)
            _remote(mine, mine, send_sems.at[i], recv_sems.at[i], (x, y, 1 - c)).start()
        refs[-1][...] = jnp.zeros_like(refs[-1])

    dma = pltpu.SemaphoreType.DMA
    outs = pl.pallas_call(
        body, name=name,
        in_specs=[HBM] * n,
        out_shape=(dma((n,)), dma((n,)), *[pltpu.HBM(h.shape, F32) for h in halves], jax.ShapeDtypeStruct((8, LANES), F32)),
        out_specs=(SEM, SEM, *[HBM] * n, pl.BlockSpec(memory_space=pltpu.VMEM)),
        input_output_aliases={k: 2 + k for k in range(n)},
        compiler_params=pltpu.CompilerParams(has_side_effects=EFFECT),
    )(*[_in_hbm(h) for h in halves])
    return (outs[0], outs[1]), list(outs[2:2 + n]), outs[-1]


def _join_wait(name, started, after):
    sems, bufs, _ = started
    n = len(bufs)

    def body(*refs):
        bufs, send_sems, recv_sems = refs[:n], refs[n], refs[n + 1]
        x, y, c, _ = _place()
        for i in range(n):
            mine, theirs = _half(bufs[i], c), _half(bufs[i], 1 - c)
            _remote(mine, mine, send_sems.at[i], recv_sems.at[i], (x, y, 1 - c)).wait_send()
            _remote(theirs, theirs, send_sems.at[i], recv_sems.at[i], (x, y, 1 - c)).wait_recv()

    outs = pl.pallas_call(
        body, name=name,
        in_specs=[HBM] * n + [SEM, SEM] + [ANY] * len(after),
        out_shape=tuple(pltpu.HBM(b.shape, F32) for b in bufs),
        out_specs=tuple([HBM] * n),
        input_output_aliases={k: k for k in range(n)},
        compiler_params=pltpu.CompilerParams(has_side_effects=EFFECT),
    )(*bufs, sems[0], sems[1], *after)
    return list(outs)


BIG = ("w_in", "pool_w", "w_out", "xq_w", "xk_w", "xv_w", "xo_w", "w_up", "w_down", "conv_w")
KEPT_F32 = ("conv_w",)
GATHER_GROUPS = ((0, 1, 9), (2, 3, 4, 5, 6), (7,), (8,))
RELAYED = (7, 8)
SMALL = ("conv_w", "a_log", "dt_bias", "gdn_norm_w", "pool_scale", "ln1_g", "ln1_b", "ln2_g", "ln2_b", "ln3_g", "ln3_b")
ORDER = ("w_in", "conv_w", "a_log", "dt_bias", "gdn_norm_w", "pool_w", "pool_scale", "w_out", "ln1_g", "ln1_b",
         "xq_w", "xk_w", "xv_w", "xo_w", "ln2_g", "ln2_b", "w_up", "w_down", "ln3_g", "ln3_b")
LANES = 128


def _rows(flat_len):
    return -(-flat_len // LANES)


def _pack(pieces):
    out = []
    for p in pieces:
        flat = p.reshape(-1).astype(F32)
        out.append(jnp.pad(flat, (0, _rows(flat.shape[0]) * LANES - flat.shape[0])).reshape(-1, LANES))
    slab = jnp.concatenate(out, axis=0)
    return jnp.pad(slab, ((0, -slab.shape[0] % 8), (0, 0)))


def _unpack(slab, shapes):
    out, row = [], 0
    for shp in shapes:
        size = math.prod(shp)
        out.append(slab[row:row + _rows(size)].reshape(-1)[:size].reshape(shp))
        row += _rows(size)
    return out


TRANSPOSED = ("w_in",)


def _as2d(name, a):
    a = a[0]
    if name in TRANSPOSED:
        return jnp.swapaxes(a, 0, 1)
    return a.reshape(-1, a.shape[-1]) if a.ndim == 3 else a


def _from2d(name, a, shape):
    return (jnp.swapaxes(a, 0, 1) if name in TRANSPOSED else a).reshape(shape)


def kernel(x, mem, w_in, conv_w, a_log, dt_bias, gdn_norm_w, pool_w, pool_scale, w_out, ln1_g, ln1_b, xq_w, xk_w, xv_w, xo_w, ln2_g, ln2_b, w_up, w_down, ln3_g, ln3_b, loss_target, m_w_in, m_conv_w, m_a_log, m_dt_bias, m_gdn_norm_w, m_pool_w, m_pool_scale, m_w_out, m_ln1_g, m_ln1_b, m_xq_w, m_xk_w, m_xv_w, m_xo_w, m_ln2_g, m_ln2_b, m_w_up, m_w_down, m_ln3_g, m_ln3_b, v_w_in, v_conv_w, v_a_log, v_dt_bias, v_gdn_norm_w, v_pool_w, v_pool_scale, v_w_out, v_ln1_g, v_ln1_b, v_xq_w, v_xk_w, v_xv_w, v_xo_w, v_ln2_g, v_ln2_b, v_w_up, v_w_down, v_ln3_g, v_ln3_b):
    given = dict(locals())
    cx, cy, cc = lax.axis_index("x"), lax.axis_index("y"), lax.axis_index("c")
    me = 2 * cx + cy
    groups = pool_w.shape[1]
    cs = pool_w.shape[2]
    kk, conv_cols = conv_w.shape[1], conv_w.shape[2]
    core = cc.astype(jnp.int32).reshape(1)
    where = jnp.stack([me, cc]).astype(jnp.int32)

    started = {}
    wts = {}

    def start(name, idx, after, token=None):
        casts = [_after(token, _as2d(BIG[i], given[BIG[i]])).astype(F32 if BIG[i] in KEPT_F32 else BF16) for i in idx]
        relayed = tuple(k for k, i in enumerate(idx) if i in RELAYED)
        sems, shards, lands, token = _gather_start(name, casts, after, relayed)
        for k, i in enumerate(idx):
            started[i] = (sems, k, shards[k], lands[k])
        return token

    token = start("gather_start_first", GATHER_GROUPS[0], x)
    token = start("gather_start_rest", tuple(i for group in GATHER_GROUPS[1:] for i in group), token, token)

    relay = {}

    def send_on(after):
        members = [started[i] for i in RELAYED]
        relay["sems"], zones, token = _gather_relay("gather_relay", [m[1] for m in members], [m[2] for m in members],
                                                    [m[3] for m in members], members[0][0], after)
        relay["zones"] = dict(zip(RELAYED, zones))
        return token

    passed = {}

    def pass_on(group, after):
        members = [started[i] for i in GATHER_GROUPS[group]]
        fwd, zones, token = _gather_forward(f"gather_forward_{group}", [m[1] for m in members], [m[3] for m in members],
                                            members[0][0], after)
        passed[group] = (fwd, zones)
        return token

    def fetch(group, after):
        members = [started[i] for i in GATHER_GROUPS[group]]
        sems, idx = members[0][0], [m[1] for m in members]
        shards = [m[2] for m in members]
        if GATHER_GROUPS[group][0] in RELAYED:
            ks = [RELAYED.index(i) for i in GATHER_GROUPS[group]]
            zones = [relay["zones"][i] for i in GATHER_GROUPS[group]]
            fwd, zones = _gather_forward_relayed(f"gather_forward_{group}", ks, zones, relay["sems"], after)
            got = _gather_wait_relayed(f"gather_wait_{group}", idx, ks, shards, zones, sems, relay["sems"], fwd, after)
        else:
            if group not in passed:
                pass_on(group, after)
            fwd, zones = passed[group]
            got = _gather_wait(f"gather_wait_{group}", idx, shards, zones, sems, fwd, after)
        full = dict(zip([BIG[i] for i in GATHER_GROUPS[group]], got))
        out = {}
        for n, a in full.items():
            if n == "w_in":
                out["w_in_t"] = a
            elif n == "w_up":
                out["w_up3"] = a
            elif n == "pool_w":
                out[n] = a.reshape(N_SHARD, groups, cs, -1).transpose(1, 0, 2, 3).reshape(groups, N_SHARD * cs, -1)
            elif n == "conv_w":
                out[n] = a.transpose(1, 0, 2).reshape(kk, N_SHARD * conv_cols)
            else:
                out[n] = a.reshape(-1, a.shape[-1])
        return out

    for n in ("a_log", "dt_bias", "gdn_norm_w", "pool_scale", "ln1_g", "ln1_b", "ln2_g", "ln2_b", "ln3_g", "ln3_b"):
        wts[n] = given[n]
    x_bf = _after(token, x[0]).astype(BF16)
    wts.update(fetch(0, x_bf))

    def start_swap(group, grads):
        names, blocks = [], []
        for n, g in grads.items():
            if n == "pool_w":
                g = g.reshape(groups, N_SHARD, cs, -1).transpose(1, 0, 2, 3).reshape(N_SHARD, groups * cs, -1)
            elif g.ndim == 2:
                g = g.reshape(N_SHARD, -1, g.shape[-1])
            names.append({"w_in_t": "w_in", "w_up3": "w_up"}.get(n, n))
            blocks.append(g)
        zones = [((N_SHARD,) + _half_shape(b.shape[1], b.shape[2]), F32) for b in blocks]
        swap = _exchange_start(f"grad_swap_start_{group}", _swap_copies, blocks, zones, N_SHARD)
        return {"group": group, "names": names, "swap": swap, "token": swap[3]}

    def start_send(state, after):
        group, names = state["group"], state["names"]
        state["blocks"] = state["swap"][1]
        state["others"] = _exchange_wait(f"grad_swap_wait_{group}", _swap_copies, state["swap"], after)
        partials = [_chip_partial("chip_partial_" + n, gb, ob, core)
                    for n, gb, ob in zip(names, state["blocks"], state["others"])]
        zones = [((3,) + p.shape[1:], BF16) for p in partials]
        state["send"] = _exchange_start(f"grad_send_start_{group}", _partial_copies, partials, zones, 3)
        state["token"] = state["send"][3]

    grad, delta, new_m, new_v = {}, {}, {}, {}

    def start_join(state, after):
        group, names = state["group"], state["names"]
        received = _exchange_wait(f"grad_send_wait_{group}", _partial_copies, state["send"], after)
        halves = [_reduce_own("reduce_own_" + n, gb, ob, rb, where)
                  for n, gb, ob, rb in zip(names, state["blocks"], state["others"], received)]
        state["join"] = _join_start(f"grad_join_start_{group}", halves)
        return state["join"][2]

    def finish_reduce(state, after):
        group, names = state["group"], state["names"]
        for n, g in zip(names, _join_wait(f"grad_join_wait_{group}", state["join"], after)):
            shp = given[n].shape
            d2, m2, v2, g2 = _adamw("adamw_" + n, _as2d(n, given[n]), g, _as2d(n, given["m_" + n]),
                                    _as2d(n, given["v_" + n]))
            grad[n], delta[n], new_m[n], new_v[n] = (_from2d(n, a, shp) for a in (g2, d2, m2, v2))
        return d2

    step = _local_step(x[0], mem[0], loss_target[0], wts, x_bf)
    pending = {}
    request = next(step)
    while True:
        try:
            kind, group, payload = request
            if kind == "weights":
                request = step.send(fetch(group, payload))
            elif kind == "relay":
                request = step.send(send_on(payload))
            elif kind == "pass":
                request = step.send(pass_on(group, payload))
            elif kind == "grads":
                pending[group] = start_swap(group, payload)
                request = step.send(pending[group]["token"])
            else:
                start_send(pending[group], [payload])
                request = step.send(pending[group]["token"])
        except StopIteration as stop:
            loss_row, grad_x, g = stop.value
            break

    after = [pending[2]["token"], grad_x]
    for group in (0, 1):
        after = [start_join(pending[group], after)]
    for group in (0, 1):
        after = [finish_reduce(pending[group], after)]
    after = [finish_reduce(pending[2], [start_join(pending[2], after)])]

    small_names = ("a_log", "dt_bias", "gdn_norm_w", "pool_scale", "ln1_g", "ln1_b", "ln2_g", "ln2_b", "ln3_g", "ln3_b")
    pieces = [g["conv_w"]] + [g[n] for n in small_names] + [loss_row[:, :1]]
    shapes = [p.shape for p in pieces]
    summed = _unpack(_all_reduce_small("all_reduce_small", _pack(pieces), after[0]), shapes)
    gsmall = dict(zip(small_names, summed[1:-1]))
    gsmall["conv_w"] = lax.dynamic_slice(summed[0], (0, me * conv_cols), (kk, conv_cols))
    loss = summed[-1][0, 0]

    sshapes = [given[n].shape for n in SMALL]
    slabs = [_pack([given[p + n] for n in SMALL]) for p in ("", "m_", "v_")]
    gslab = _pack([gsmall[n] for n in SMALL])
    outs = _adamw("adamw_small", slabs[0], gslab, slabs[1], slabs[2])[:3]
    for dst, slab in zip((delta, new_m, new_v), outs):
        dst.update(zip(SMALL, _unpack(slab, sshapes)))
    for n in SMALL:
        grad[n] = gsmall[n].reshape(given[n].shape)

    return (loss, grad_x[None], *[grad[n] for n in ORDER], *[delta[n] for n in ORDER],
            *[new_m[n] for n in ORDER], *[new_v[n] for n in ORDER])
```

```python
import math

import jax
import jax.numpy as jnp
from jax import lax
from jax.experimental import pallas as pl
from jax.experimental.pallas import tpu as pltpu

F32 = jnp.float32
BF16 = jnp.bfloat16
MESH = pl.DeviceIdType.MESH

HEAD_DIM = 128
CHUNK = 64
POOL_WINDOWS = (2, 4, 8, 16)
XATTN_HEADS = 4
ALPHA = 2.0 ** 0.25
LN_EPS = 1e-5
NORM_EPS = 1e-6
ADAM_LR, ADAM_B1, ADAM_B2, ADAM_EPS, ADAM_WD, ADAM_STEP = 0.001, 0.9, 0.999, 1e-08, 0.01, 10
N_SHARD = 4
VMEM_LIMIT = 56 * 1024 * 1024
K_STEPS = (2048, 1024, 512, 256, 128)


def _params(*sem):
    return pltpu.CompilerParams(dimension_semantics=sem, vmem_limit_bytes=VMEM_LIMIT)


def _bdot(a, b, ta=False, tb=False):
    dims = (((0 if ta else 1,), (1 if tb else 0,)), ((), ()))
    return lax.dot_general(a.astype(BF16), b.astype(BF16), dims, preferred_element_type=F32)


def _sigmoid(x):
    return 1.0 / (1.0 + jnp.exp(-x))


def _matmul(name, a, b, *, ta=False, tb=False, tm, tn, tk, extra=(), outs, epilogue, b_blocks=None,
            sequential=False, n_used=None, n_outer=False):
    m, k_dim = (a.shape[1], a.shape[0]) if ta else a.shape
    if b_blocks and tb:
        n = b.shape[1]
        k_dim = b.shape[0] * b.shape[2]
        per = b.shape[2] // tk
        b_spec = pl.BlockSpec((None, tn, tk), lambda i, j, k: (k // per, j, k % per))
    elif b_blocks:
        n = b.shape[0] * b.shape[2]
        per = b.shape[2] // tn
        b_spec = pl.BlockSpec((None, tk, tn), lambda i, j, k: (j // per, k, j % per))
    elif tb:
        n = b.shape[0]
        b_spec = pl.BlockSpec((tn, tk), lambda i, j, k: (j, k))
    else:
        n = b.shape[1]
        b_spec = pl.BlockSpec((tk, tn), lambda i, j, k: (k, j))
    n = n_used or n
    assert m % tm == 0 and n % tn == 0 and k_dim % tk == 0, (name, m, n, k_dim, tm, tn, tk)
    nk = k_dim // tk
    a_spec = pl.BlockSpec((tk, tm), lambda i, j, k: (k, i)) if ta else pl.BlockSpec((tm, tk), lambda i, j, k: (i, k))
    n_extra, n_out = len(extra), len(outs)

    def wrap(index_map):
        return lambda i, j, k: index_map(i, j)

    def spec(block, index_map):
        if n_outer:
            return pl.BlockSpec(block, lambda j, i, k: index_map(i, j, k))
        return pl.BlockSpec(block, index_map)

    row_axis = 1 if n_outer else 0

    def body_one_step(*refs):
        ex = refs[2:2 + n_extra]
        out = refs[2 + n_extra:2 + n_extra + n_out]
        epilogue(_bdot(refs[0][...], refs[1][...], ta, tb), ex, out, pl.program_id(row_axis))

    def body(*refs):
        a_ref, b_ref = refs[0], refs[1]
        ex = refs[2:2 + n_extra]
        out = refs[2 + n_extra:2 + n_extra + n_out]
        acc = refs[-1]
        i, k = pl.program_id(row_axis), pl.program_id(2)
        part = _bdot(a_ref[...], b_ref[...], ta, tb)

        @pl.when(k == 0)
        def _():
            acc[...] = part

        @pl.when(jnp.logical_and(k > 0, k < nk - 1))
        def _():
            acc[...] += part

        @pl.when(k == nk - 1)
        def _():
            epilogue(acc[...] + part, ex, out, i)

    sem = ("arbitrary",) * 3 if sequential else ("parallel", "parallel", "arbitrary")
    res = pl.pallas_call(
        body_one_step if nk == 1 else body, name=name,
        grid=(n // tn, m // tm, nk) if n_outer else (m // tm, n // tn, nk),
        in_specs=[spec(a_spec.block_shape, a_spec.index_map), spec(b_spec.block_shape, b_spec.index_map)]
        + [spec(bs, wrap(im)) for _, bs, im in extra],
        out_specs=[spec(bs, wrap(im)) for _, bs, im in outs],
        out_shape=[s for s, _, _ in outs],
        scratch_shapes=[] if nk == 1 else [pltpu.VMEM((tm, tn), F32)],
        compiler_params=_params(*sem),
    )(a, b, *[x for x, _, _ in extra])
    return res


def _tile(i, j):
    return (i, j)


def _plain(name, a, b, *, ta=False, tb=False, tm, tn, tk, out_dtype, b_blocks=None, out3=None, n_used=None,
           n_outer=False, m_kept=None):
    m = a.shape[1] if ta else a.shape[0]
    if b_blocks:
        n = b.shape[1] if tb else b.shape[0] * b.shape[2]
    else:
        n = n_used or (b.shape[0] if tb else b.shape[1])

    def epi(acc, ex, out, i):
        out[0][...] = acc.astype(out_dtype)

    if out3:
        per = (n // out3) // tn
        spec = (jax.ShapeDtypeStruct((out3, m, n // out3), out_dtype), (None, tm, tn),
                lambda i, j: (j // per, i, j % per))
    else:
        spec = (jax.ShapeDtypeStruct((m_kept or m, n), out_dtype), (tm, tn), _tile)
    return _matmul(name, a, b, ta=ta, tb=tb, tm=tm, tn=tn, tk=tk, outs=[spec], epilogue=epi,
                   b_blocks=b_blocks, n_used=n_used, n_outer=n_outer)[0]


def _ln_forward(name, a, b, res, gamma, beta, *, tm, tk, want_h=True):
    m, n = res.shape

    def epi(acc, ex, out, i):
        u = ALPHA * ex[0][...] + acc
        mu = jnp.mean(u, axis=-1, keepdims=True)
        xc = u - mu
        var = jnp.mean(xc * xc, axis=-1, keepdims=True)
        rstd = lax.rsqrt(var + LN_EPS)
        xhat = xc * rstd
        out[-2][...] = xhat
        out[-1][...] = rstd
        if want_h:
            h = xhat * ex[1][...] + ex[2][...]
            out[0][...] = h
            out[1][...] = h.astype(BF16)

    row = lambda i, j: (i, 0)
    vec = lambda i, j: (0, 0)
    outs = [(jax.ShapeDtypeStruct((m, n), F32), (tm, n), row), (jax.ShapeDtypeStruct((m, n), BF16), (tm, n), row),
            (jax.ShapeDtypeStruct((m, n), F32), (tm, n), row), (jax.ShapeDtypeStruct((m, 1), F32), (tm, 1), row)]
    return _matmul(
        name, a, b, tm=tm, tn=n, tk=tk,
        extra=[(res, (tm, n), row), (gamma, (1, n), vec), (beta, (1, n), vec)],
        outs=outs if want_h else outs[2:], epilogue=epi)


def _ln_backward_math(dy, xhat, rstd, gamma):
    dxhat = dy * gamma
    m1 = jnp.mean(dxhat, axis=-1, keepdims=True)
    m2 = jnp.mean(dxhat * xhat, axis=-1, keepdims=True)
    du = rstd * (dxhat - m1 - xhat * m2)
    return du, jnp.sum(dy * xhat, axis=0, keepdims=True), jnp.sum(dy, axis=0, keepdims=True)


def _ln_backward(name, a, b, dres, xhat, rstd, gamma, *, tm, tk, b_blocks=None, tb=True):
    m, n = dres.shape

    def epi(acc, ex, out, i):
        dy = acc + ALPHA * ex[0][...]
        du, dg, db = _ln_backward_math(dy, ex[1][...], ex[2][...], ex[3][...])
        out[0][...] = du
        out[1][...] = du.astype(BF16)
        first = i == 0

        @pl.when(first)
        def _():
            out[2][...] = dg
            out[3][...] = db

        @pl.when(jnp.logical_not(first))
        def _():
            out[2][...] += dg
            out[3][...] += db

    row = lambda i, j: (i, 0)
    vec = lambda i, j: (0, 0)
    return _matmul(
        name, a, b, tb=tb, tm=tm, tn=n, tk=tk, b_blocks=b_blocks, sequential=True,
        extra=[(dres, (tm, n), row), (xhat, (tm, n), row), (rstd, (tm, 1), row), (gamma, (1, n), vec)],
        outs=[(jax.ShapeDtypeStruct((m, n), F32), (tm, n), row),
              (jax.ShapeDtypeStruct((m, n), BF16), (tm, n), row),
              (jax.ShapeDtypeStruct((1, n), F32), (1, n), vec),
              (jax.ShapeDtypeStruct((1, n), F32), (1, n), vec)],
        epilogue=epi)


def _shift_down(x, k):
    row = lax.broadcasted_iota(jnp.int32, x.shape, 0)
    return jnp.where(row >= k, pltpu.roll(x, k, axis=0), 0.0)


def _shift_up(x, k):
    t = x.shape[0]
    row = lax.broadcasted_iota(jnp.int32, x.shape, 0)
    return jnp.where(row < t - k, pltpu.roll(x, t - k, axis=0), 0.0)


def _conv_silu_norm(x, w, normalise):
    kk = w.shape[0]
    c = x * w[kk - 1:kk, :]
    for j in range(kk - 1):
        c = c + _shift_down(x, kk - 1 - j) * w[j:j + 1, :]
    sg = _sigmoid(c)
    s = c * sg
    r = lax.rsqrt(jnp.sum(s * s, axis=-1, keepdims=True) + NORM_EPS)
    y = jnp.where(normalise, s * r, s)
    return c, sg, s, r, y


def _gdn_pre(proj, conv_w, heads):
    t = proj.shape[0]
    kk = conv_w.shape[0]

    def body(x_ref, w_ref, o_ref):
        normalise = pl.program_id(0) < 2
        o_ref[...] = _conv_silu_norm(x_ref[...], w_ref[...], normalise)[4]

    col = lambda s, h: (0, s * heads + h)
    return pl.pallas_call(
        body, name="gdn_pre", grid=(3, heads),
        in_specs=[pl.BlockSpec((t, HEAD_DIM), col), pl.BlockSpec((kk, HEAD_DIM), col)],
        out_specs=pl.BlockSpec((t, HEAD_DIM), col),
        out_shape=jax.ShapeDtypeStruct((t, 3 * heads * HEAD_DIM), F32),
        compiler_params=_params("parallel", "parallel"),
    )(proj, conv_w)


def _gdn_pre_backward(proj, conv_w, dqkv, heads):
    t = proj.shape[0]
    kk = conv_w.shape[0]

    def body(x_ref, w_ref, dy_ref, dx_ref, dw_ref):
        normalise = pl.program_id(0) < 2
        x = x_ref[...]
        w = w_ref[...]
        dy = dy_ref[...]
        c, sg, s, r, y = _conv_silu_norm(x, w, normalise)
        ds_norm = r * (dy - y * jnp.sum(dy * y, axis=-1, keepdims=True))
        ds = jnp.where(normalise, ds_norm, dy)
        dc = ds * (sg * (1.0 + c * (1.0 - sg)))
        dx = dc * w[kk - 1:kk, :]
        rows = [None] * kk
        rows[kk - 1] = jnp.sum(dc * x, axis=0, keepdims=True)
        for j in range(kk - 1):
            lag = kk - 1 - j
            dx = dx + _shift_up(dc, lag) * w[j:j + 1, :]
            rows[j] = jnp.sum(dc * _shift_down(x, lag), axis=0, keepdims=True)
        dx_ref[...] = dx.astype(BF16)
        dw_ref[...] = jnp.concatenate(rows, axis=0)

    col = lambda s, h: (0, s * heads + h)
    return pl.pallas_call(
        body, name="gdn_pre_bwd", grid=(3, heads),
        in_specs=[pl.BlockSpec((t, HEAD_DIM), col), pl.BlockSpec((kk, HEAD_DIM), col),
                  pl.BlockSpec((t, HEAD_DIM), col)],
        out_specs=[pl.BlockSpec((t, HEAD_DIM), col), pl.BlockSpec((kk, HEAD_DIM), col)],
        out_shape=[jax.ShapeDtypeStruct((t, 3 * heads * HEAD_DIM), BF16),
                   jax.ShapeDtypeStruct((kk, 3 * heads * HEAD_DIM), F32)],
        compiler_params=_params("parallel", "parallel"),
    )(proj, conv_w, dqkv)


def _gate_vectors(a_log, dt_bias, heads):
    pad = lambda v: jnp.pad(v.astype(F32), ((0, 0), (heads, HEAD_DIM - 2 * heads)))
    return pad(jnp.exp(a_log.astype(F32))), pad(dt_bias)


def _softplus(x):
    return jnp.maximum(x, 0.0) + jnp.log(1.0 + jnp.exp(-jnp.abs(x)))


def _gates_epilogue(heads):
    def epi(acc, ex, out, i):
        lane = lax.broadcasted_iota(jnp.int32, acc.shape, 1)
        beta = _sigmoid(acc)
        g = -ex[0][...] * _softplus(acc + ex[1][...])
        out[0][...] = acc
        out[1][...] = jnp.where(lane < heads, beta, jnp.where(lane < 2 * heads, g, 0.0))
    return epi


def _gates_backward(ba, bg, dbg, ea, dtb, heads):
    t = ba.shape[0]

    def body(ba_ref, bg_ref, d_ref, ea_ref, dt_ref, dba_ref, dal_ref, ddt_ref):
        lane = lax.broadcasted_iota(jnp.int32, (t, HEAD_DIM), 1)
        bgv = bg_ref[...]
        d = d_ref[...]
        db = d * bgv * (1.0 - bgv)
        da = -d * ea_ref[...] * _sigmoid(ba_ref[...] + dt_ref[...])
        is_g = jnp.logical_and(lane >= heads, lane < 2 * heads)
        dba = jnp.where(lane < heads, db, jnp.where(is_g, da, 0.0))
        dba_ref[...] = dba.astype(BF16)
        dal_ref[...] = jnp.sum(jnp.where(is_g, d * bgv, 0.0), axis=0, keepdims=True)
        ddt_ref[...] = jnp.sum(jnp.where(is_g, da, 0.0), axis=0, keepdims=True)

    full = pl.BlockSpec((t, HEAD_DIM), lambda: (0, 0))
    vec = pl.BlockSpec((1, HEAD_DIM), lambda: (0, 0))
    return pl.pallas_call(
        body, name="gates_bwd", grid=(),
        in_specs=[full, full, full, vec, vec], out_specs=[full, vec, vec],
        out_shape=[jax.ShapeDtypeStruct((t, HEAD_DIM), BF16), jax.ShapeDtypeStruct((1, HEAD_DIM), F32),
                   jax.ShapeDtypeStruct((1, HEAD_DIM), F32)],
        compiler_params=pltpu.CompilerParams(vmem_limit_bytes=VMEM_LIMIT),
    )(ba, bg, dbg, ea, dtb)


class _Chunk:
    pass


def _split2(x):
    hi = x.astype(BF16)
    return hi, (x - hi.astype(F32)).astype(BF16)


def _split3(x):
    hi = x.astype(BF16)
    rest = x - hi.astype(F32)
    mid = rest.astype(BF16)
    return hi, mid, (rest - mid.astype(F32)).astype(BF16)


def _dot_mask(mask, x, ta=False):
    hi, mid, lo = _split3(x)
    return _bdot(mask, hi, ta=ta) + (_bdot(mask, mid, ta=ta) + _bdot(mask, lo, ta=ta))


def _transpose_by_identity(x):
    r = x.shape[0]
    eye = (lax.broadcasted_iota(jnp.int32, (r, r), 0) == lax.broadcasted_iota(jnp.int32, (r, r), 1)).astype(BF16)
    hi, mid, lo = _split3(x)
    return _bdot(hi, eye, ta=True) + (_bdot(mid, eye, ta=True) + _bdot(lo, eye, ta=True))


def _dot22(a, b, ta=False, tb=False):
    ah, al = _split2(a)
    bh, bl = _split2(b)
    return _bdot(ah, bh, ta, tb) + (_bdot(ah, bl, ta, tb) + _bdot(al, bh, ta, tb))


def _chunk_gates(bg, heads):
    n = CHUNK
    row = lax.broadcasted_iota(jnp.int32, (n, n), 0)
    col = lax.broadcasted_iota(jnp.int32, (n, n), 1)
    lane = lax.broadcasted_iota(jnp.int32, bg.shape, 1)
    graw = jnp.where(jnp.logical_and(lane >= heads, lane < 2 * heads), bg, 0.0)
    gc = _dot_mask((row >= col).astype(BF16), graw)
    return gc, _transpose_by_identity(gc)


def _in_lockstep(generators):
    results = [None] * len(generators)
    live = list(enumerate(generators))
    while live:
        still = []
        for i, gen in live:
            try:
                next(gen)
                still.append((i, gen))
            except StopIteration as stop:
                results[i] = stop.value
        live = still
    return results


def _chunk_local(q, k, v, beta, gc, grow, solved=None):
    c = _Chunk()
    n = CHUNK
    row = lax.broadcasted_iota(jnp.int32, (n, n), 0)
    col = lax.broadcasted_iota(jnp.int32, (n, n), 1)
    c.tri = row >= col
    c.strict = row > col
    eye = row == col
    c.gcb = jnp.broadcast_to(gc, (n, HEAD_DIM))
    c.decay = jnp.where(c.tri, jnp.exp(jnp.where(c.tri, gc - grow, 0.0)), 0.0)
    c.eg = jnp.exp(c.gcb)
    glast = c.gcb[n - 1:n, :]
    c.egl = jnp.exp(glast)
    c.ekl = jnp.exp(glast - c.gcb)
    c.beta = beta
    c.q = q * (HEAD_DIM ** -0.5)
    c.k = k
    c.v = v
    c.kb = k * beta
    c.vb = v * beta
    c.kg = c.kb * c.eg
    both = _bdot(jnp.concatenate([c.kb, c.q], axis=0), k, tb=True)
    yield
    c.L = jnp.where(c.strict, both[:n] * c.decay, 0.0)
    c.A = jnp.where(c.tri, both[n:] * c.decay, 0.0)
    if solved is None:
        x = -c.L
        tinv = eye.astype(F32) + x
        p = _dot22(x, x)
        yield
        for _ in range(int(math.log2(n)) - 2):
            both = _dot22(jnp.concatenate([p, tinv], axis=0), p)
            yield
            p, tinv = both[:n], tinv + both[n:]
        c.T = tinv + _dot22(tinv, p)
        yield
        uw = _dot22(c.T, jnp.concatenate([c.vb, c.kg], axis=1))
        yield
        c.u, c.w = uw[:, :HEAD_DIM], uw[:, HEAD_DIM:]
    else:
        c.T, c.u, c.w = solved
    c.qg = c.q * c.eg
    c.kdec = k * c.ekl
    return c


def _gdn_core(qkv, bg, heads):
    t = qkv.shape[0]
    nchunk = t // CHUNK

    gw = heads * HEAD_DIM

    def body(qkv_ref, bg_ref, o_ref, s_ref, t_ref, u_ref, w_ref, state):
        @pl.when(pl.program_id(0) == 0)
        def _():
            state[...] = jnp.zeros_like(state)

        bg_v = bg_ref[...]
        gc_all, gc_rows = _chunk_gates(bg_v, heads)
        def one_head(h):
            col = lambda s: pl.ds(s * gw + h * HEAD_DIM, HEAD_DIM)
            c = yield from _chunk_local(qkv_ref[:, col(0)], qkv_ref[:, col(1)], qkv_ref[:, col(2)], bg_v[:, h:h + 1],
                                        gc_all[:, heads + h:heads + h + 1], gc_rows[heads + h:heads + h + 1, :])
            s0 = state[h]
            v_new = c.u - _bdot(c.w, s0)
            yield
            o = _bdot(c.qg, s0) + _bdot(c.A, v_new)
            return s0, o, s0 * c.egl + _bdot(c.kdec, v_new, ta=True), c

        results = _in_lockstep([one_head(h) for h in range(heads)])
        for h, (s0, o, s1, c) in enumerate(results):
            lanes = pl.ds(h * HEAD_DIM, HEAD_DIM)
            s_ref[h, 0] = s0
            o_ref[:, lanes] = o
            t_ref[:, lanes] = jnp.concatenate([c.T, jnp.zeros((CHUNK, HEAD_DIM - CHUNK), F32)], axis=1)
            u_ref[:, lanes] = c.u
            w_ref[:, lanes] = c.w
            state[h] = s1

    return pl.pallas_call(
        body, name="gdn_core", grid=(nchunk,),
        in_specs=[pl.BlockSpec((CHUNK, 3 * gw), lambda n: (n, 0)), pl.BlockSpec((CHUNK, HEAD_DIM), lambda n: (n, 0))],
        out_specs=[pl.BlockSpec((CHUNK, gw), lambda n: (n, 0)),
                   pl.BlockSpec((heads, 1, HEAD_DIM, HEAD_DIM), lambda n: (0, n, 0, 0))]
        + [pl.BlockSpec((CHUNK, gw), lambda n: (n, 0))] * 3,
        out_shape=[jax.ShapeDtypeStruct((t, gw), F32),
                   jax.ShapeDtypeStruct((heads, nchunk, HEAD_DIM, HEAD_DIM), F32)]
        + [jax.ShapeDtypeStruct((t, gw), F32)] * 3,
        scratch_shapes=[pltpu.VMEM((heads, HEAD_DIM, HEAD_DIM), F32)],
        compiler_params=_params("arbitrary"),
    )(qkv, bg)


def _gdn_core_backward(qkv, bg, states, solved, do, heads):
    t = qkv.shape[0]
    nchunk = t // CHUNK
    n = CHUNK

    def one_head(chunk_local, s0, d_out, ds1):
        c = yield from chunk_local
        v_new = c.u - _bdot(c.w, s0)
        dqg = _bdot(d_out, s0, tb=True)
        ds0 = _bdot(c.qg, d_out, ta=True) + ds1 * c.egl
        dv_new = _bdot(c.A, d_out, ta=True) + _bdot(c.kdec, ds1)
        yield
        dA = jnp.where(c.tri, _bdot(d_out, v_new, tb=True), 0.0)
        dkdec = _bdot(v_new, ds1, tb=True)
        dgl = jnp.sum(jnp.sum(ds1 * s0, axis=1, keepdims=True), axis=0, keepdims=True) * c.egl
        dw = -_bdot(dv_new, s0, tb=True)
        ds0 = ds0 - _bdot(c.w, dv_new, ta=True)
        yield
        both = _dot22(c.T, jnp.concatenate([dv_new, dw], axis=1), ta=True)
        yield
        dvb, dkg = both[:, :HEAD_DIM], both[:, HEAD_DIM:]
        dL = jnp.where(c.strict, -(_bdot(dvb, c.u, tb=True) + _bdot(dkg, c.w, tb=True)), 0.0)
        yield
        dm1 = dL * c.decay
        dkb = _bdot(dm1, c.k) + dkg * c.eg
        dk = _bdot(dm1, c.kb, ta=True)
        dm2 = dA * c.decay
        dq = _bdot(dm2, c.k) + dqg * c.eg
        dk = dk + _bdot(dm2, c.q, ta=True) + dkdec * c.ekl + dkb * c.beta
        pm = dL * c.L + dA * c.A
        ones = jnp.ones((n, HEAD_DIM), BF16)
        pm_hi, pm_lo = _split2(pm)
        colsum = _bdot(pm_hi, ones, ta=True) + _bdot(pm_lo, ones, ta=True)
        tk_ = jnp.sum(dkdec * c.kdec, axis=1, keepdims=True)
        dgc = (jnp.sum(pm, axis=1, keepdims=True) - colsum
               + jnp.sum(dqg * c.qg, axis=1, keepdims=True)
               - tk_
               + jnp.sum(dkg * c.kg, axis=1, keepdims=True))
        dgl = dgl + jnp.sum(tk_, axis=0, keepdims=True)
        rowi = lax.broadcasted_iota(jnp.int32, (n, HEAD_DIM), 0)
        dgc = dgc + jnp.where(rowi == n - 1, dgl, 0.0)
        dbeta = jnp.sum(dkb * c.k, axis=1, keepdims=True) + jnp.sum(dvb * c.v, axis=1, keepdims=True)
        return dq * (HEAD_DIM ** -0.5), dk, dvb * c.beta, dbeta, dgc, ds0

    gw = heads * HEAD_DIM

    def body(qkv_ref, bg_ref, s_ref, t_ref, u_ref, w_ref, do_ref, dqkv_ref, dbg_ref, dstate):
        @pl.when(pl.program_id(0) == 0)
        def _():
            dstate[...] = jnp.zeros_like(dstate)

        bg_v = bg_ref[...]
        gc_all, gc_rows = _chunk_gates(bg_v, heads)
        lane = lax.broadcasted_iota(jnp.int32, (n, HEAD_DIM), 1)
        dgates = jnp.zeros((n, HEAD_DIM), F32)
        chains = []
        for h in range(heads):
            col = lambda s: pl.ds(s * gw + h * HEAD_DIM, HEAD_DIM)
            lanes = pl.ds(h * HEAD_DIM, HEAD_DIM)
            c = _chunk_local(qkv_ref[:, col(0)], qkv_ref[:, col(1)], qkv_ref[:, col(2)], bg_v[:, h:h + 1],
                             gc_all[:, heads + h:heads + h + 1], gc_rows[heads + h:heads + h + 1, :],
                             (t_ref[:, pl.ds(h * HEAD_DIM, CHUNK)], u_ref[:, lanes], w_ref[:, lanes]))
            chains.append(one_head(c, s_ref[h, 0], do_ref[:, pl.ds(h * HEAD_DIM, HEAD_DIM)], dstate[h]))
        results = _in_lockstep(chains)
        for h, (dq, dk, dv, dbeta, dgc, ds0) in enumerate(results):
            dgates = jnp.where(lane == h, dbeta, jnp.where(lane == heads + h, dgc, dgates))
        for h, (dq, dk, dv, dbeta, dgc, ds0) in enumerate(results):
            dqkv_ref[:, pl.ds(h * HEAD_DIM, HEAD_DIM)] = dq
            dqkv_ref[:, pl.ds(gw + h * HEAD_DIM, HEAD_DIM)] = dk
            dqkv_ref[:, pl.ds(2 * gw + h * HEAD_DIM, HEAD_DIM)] = dv
            dstate[h] = ds0
        row = lax.broadcasted_iota(jnp.int32, (n, n), 0)
        colm = lax.broadcasted_iota(jnp.int32, (n, n), 1)
        draw = _dot_mask((row >= colm).astype(BF16), dgates, ta=True)
        dbg_ref[...] = jnp.where(lane < heads, dgates, draw)

    last = nchunk - 1
    return pl.pallas_call(
        body, name="gdn_core_bwd", grid=(nchunk,),
        in_specs=[pl.BlockSpec((CHUNK, 3 * gw), lambda i: (last - i, 0)),
                  pl.BlockSpec((CHUNK, HEAD_DIM), lambda i: (last - i, 0)),
                  pl.BlockSpec((heads, 1, HEAD_DIM, HEAD_DIM), lambda i: (0, last - i, 0, 0))]
        + [pl.BlockSpec((CHUNK, gw), lambda i: (last - i, 0))] * 4,
        out_specs=[pl.BlockSpec((CHUNK, 3 * gw), lambda i: (last - i, 0)),
                   pl.BlockSpec((CHUNK, HEAD_DIM), lambda i: (last - i, 0))],
        out_shape=[jax.ShapeDtypeStruct((t, 3 * gw), F32), jax.ShapeDtypeStruct((t, HEAD_DIM), F32)],
        scratch_shapes=[pltpu.VMEM((heads, HEAD_DIM, HEAD_DIM), F32)],
        compiler_params=_params("arbitrary"),
    )(qkv, bg, states, *solved, do)


def _gdn_post(o, proj, z_col0, norm_w, heads, tt):
    t = o.shape[0]
    zb = z_col0 // HEAD_DIM

    def body(o_ref, z_ref, w_ref, out_ref):
        ov = o_ref[...]
        z = z_ref[...]
        rms = lax.rsqrt(jnp.mean(ov * ov, axis=-1, keepdims=True) + NORM_EPS)
        out_ref[...] = (ov * rms * w_ref[...] * (z * _sigmoid(z))).astype(BF16)

    return pl.pallas_call(
        body, name="gdn_post", grid=(t // tt, heads),
        in_specs=[pl.BlockSpec((tt, HEAD_DIM), lambda i, h: (i, h)),
                  pl.BlockSpec((tt, HEAD_DIM), lambda i, h: (i, zb + h)),
                  pl.BlockSpec((1, HEAD_DIM), lambda i, h: (0, 0))],
        out_specs=pl.BlockSpec((tt, HEAD_DIM), lambda i, h: (i, h)),
        out_shape=jax.ShapeDtypeStruct((t, heads * HEAD_DIM), BF16),
        compiler_params=_params("parallel", "parallel"),
    )(o, proj, norm_w)


def _gdn_post_backward(dcat, o, proj, z_col0, norm_w, heads, tt):
    t = o.shape[0]
    zb = z_col0 // HEAD_DIM

    def body(d_ref, o_ref, z_ref, w_ref, do_ref, dz_ref, dw_ref):
        d = d_ref[...]
        ov = o_ref[...]
        z = z_ref[...]
        w = w_ref[...]
        rms = lax.rsqrt(jnp.mean(ov * ov, axis=-1, keepdims=True) + NORM_EPS)
        ohat = ov * rms
        sg = _sigmoid(z)
        gate = z * sg
        dz_ref[...] = (d * ohat * w * (sg * (1.0 + z * (1.0 - sg)))).astype(BF16)
        don = d * gate
        dohat = don * w
        do_ref[...] = rms * (dohat - ohat * jnp.mean(dohat * ohat, axis=-1, keepdims=True))
        dw = jnp.sum(don * ohat, axis=0, keepdims=True)
        first = jnp.logical_and(pl.program_id(0) == 0, pl.program_id(1) == 0)

        @pl.when(first)
        def _():
            dw_ref[...] = dw

        @pl.when(jnp.logical_not(first))
        def _():
            dw_ref[...] += dw

    blk = pl.BlockSpec((tt, HEAD_DIM), lambda i, h: (i, h))
    return pl.pallas_call(
        body, name="gdn_post_bwd", grid=(t // tt, heads),
        in_specs=[blk, blk, pl.BlockSpec((tt, HEAD_DIM), lambda i, h: (i, zb + h)),
                  pl.BlockSpec((1, HEAD_DIM), lambda i, h: (0, 0))],
        out_specs=[blk, blk, pl.BlockSpec((1, HEAD_DIM), lambda i, h: (0, 0))],
        out_shape=[jax.ShapeDtypeStruct((t, heads * HEAD_DIM), F32),
                   jax.ShapeDtypeStruct((t, heads * HEAD_DIM), BF16),
                   jax.ShapeDtypeStruct((1, HEAD_DIM), F32)],
        compiler_params=_params("arbitrary", "arbitrary"),
    )(dcat, o, proj, norm_w)


def _pool_select(levels, group):
    out = levels[-1]
    for gi in range(len(levels) - 2, -1, -1):
        out = jnp.where(group == gi, levels[gi], out)
    return out


def _pool_counts(t, width, group):
    pos = lax.broadcasted_iota(jnp.int32, (t, width), 0)
    win = jnp.left_shift(2, group)
    return jnp.minimum(pos + 1, win).astype(F32)


def _pooled(p, group):
    levels, s, step = [], p, 1
    for _ in POOL_WINDOWS:
        s = s + _shift_down(s, step)
        levels.append(s)
        step *= 2
    cnt = _pool_counts(p.shape[0], p.shape[1], group)
    return _pool_select(levels, group) / cnt - p, cnt


def _pool_forward(proj, p_col0, pool_w, pool_scale):
    t = proj.shape[0]
    groups, cg, _ = pool_w.shape
    pb = p_col0 // cg

    def body(p_ref, w_ref, s_ref, o_ref):
        pooled, _ = _pooled(p_ref[...], pl.program_id(0))
        o_ref[...] = (_bdot(pooled, w_ref[0]) * s_ref[...]).astype(BF16)

    return pl.pallas_call(
        body, name="pool_fwd", grid=(groups,),
        in_specs=[pl.BlockSpec((t, cg), lambda g: (0, pb + g)), pl.BlockSpec((1, cg, cg), lambda g: (g, 0, 0)),
                  pl.BlockSpec((1, cg), lambda g: (0, g))],
        out_specs=pl.BlockSpec((t, cg), lambda g: (0, g)),
        out_shape=jax.ShapeDtypeStruct((t, groups * cg), BF16),
        compiler_params=_params("parallel"),
    )(proj, pool_w, pool_scale)


def _pool_backward(dcat, d_col0, proj, p_col0, pool_w, pool_scale):
    t = proj.shape[0]
    groups, cg, _ = pool_w.shape
    pb = p_col0 // cg
    db = d_col0 // cg

    def body(d_ref, p_ref, w_ref, s_ref, dp_ref, dw_ref, ds_ref):
        group = pl.program_id(0)
        pooled, cnt = _pooled(p_ref[...], group)
        w = w_ref[0]
        d = d_ref[...]
        mixed = _bdot(pooled, w)
        ds_ref[...] = jnp.sum(d * mixed, axis=0, keepdims=True)
        dmixed = d * s_ref[...]
        dw_ref[0] = _bdot(pooled, dmixed, ta=True)
        dpooled = _bdot(dmixed, w, tb=True)
        levels, s, step = [], dpooled / cnt, 1
        for _ in POOL_WINDOWS:
            s = s + _shift_up(s, step)
            levels.append(s)
            step *= 2
        dp_ref[...] = (_pool_select(levels, group) - dpooled).astype(BF16)

    return pl.pallas_call(
        body, name="pool_bwd", grid=(groups,),
        in_specs=[pl.BlockSpec((t, cg), lambda g: (0, db + g)), pl.BlockSpec((t, cg), lambda g: (0, pb + g)),
                  pl.BlockSpec((1, cg, cg), lambda g: (g, 0, 0)), pl.BlockSpec((1, cg), lambda g: (0, g))],
        out_specs=[pl.BlockSpec((t, cg), lambda g: (0, g)), pl.BlockSpec((1, cg, cg), lambda g: (g, 0, 0)),
                   pl.BlockSpec((1, cg), lambda g: (0, g))],
        out_shape=[jax.ShapeDtypeStruct((t, groups * cg), BF16), jax.ShapeDtypeStruct((groups, cg, cg), F32),
                   jax.ShapeDtypeStruct((1, groups * cg), F32)],
        compiler_params=_params("parallel"),
    )(dcat, proj, pool_w, pool_scale)


def _attention(q, k, v, tq):
    t, d = q.shape
    m = k.shape[0]
    dh = d // XATTN_HEADS
    scale = dh ** -0.5

    def body(q_ref, k_ref, v_ref, o_ref):
        s = _bdot(q_ref[...], k_ref[...], tb=True) * scale
        s = s - jnp.max(s, axis=-1, keepdims=True)
        e = jnp.exp(s)
        p = e / jnp.sum(e, axis=-1, keepdims=True)
        o_ref[...] = _bdot(p, v_ref[...]).astype(BF16)

    return pl.pallas_call(
        body, name="xattn_fwd", grid=(XATTN_HEADS, t // tq),
        in_specs=[pl.BlockSpec((tq, dh), lambda h, i: (i, h)), pl.BlockSpec((m, dh), lambda h, i: (0, h)),
                  pl.BlockSpec((m, dh), lambda h, i: (0, h))],
        out_specs=pl.BlockSpec((tq, dh), lambda h, i: (i, h)),
        out_shape=jax.ShapeDtypeStruct((t, d), BF16),
        compiler_params=_params("parallel", "parallel"),
    )(q, k, v)


def _attention_backward(q, k, v, do, tq):
    t, d = q.shape
    m = k.shape[0]
    dh = d // XATTN_HEADS
    scale = dh ** -0.5

    def body(q_ref, k_ref, v_ref, do_ref, dq_ref, dk_ref, dv_ref, dk_acc, dv_acc):
        i = pl.program_id(1)
        qv, kv, vv, dov = q_ref[...], k_ref[...], v_ref[...], do_ref[...]
        s = _bdot(qv, kv, tb=True) * scale
        s = s - jnp.max(s, axis=-1, keepdims=True)
        e = jnp.exp(s)
        p = e / jnp.sum(e, axis=-1, keepdims=True)
        dp = _bdot(dov, vv, tb=True)
        ds = p * (dp - jnp.sum(dp * p, axis=-1, keepdims=True)) * scale
        dq_ref[...] = _bdot(ds, kv).astype(BF16)
        dv_part = _bdot(p, dov, ta=True)
        dk_part = _bdot(ds, qv, ta=True)

        @pl.when(i == 0)
        def _():
            dk_acc[...] = dk_part
            dv_acc[...] = dv_part

        @pl.when(i > 0)
        def _():
            dk_acc[...] += dk_part
            dv_acc[...] += dv_part

        @pl.when(i == pl.num_programs(1) - 1)
        def _():
            dk_ref[...] = dk_acc[...].astype(BF16)
            dv_ref[...] = dv_acc[...].astype(BF16)

    qblk = pl.BlockSpec((tq, dh), lambda h, i: (i, h))
    kblk = pl.BlockSpec((m, dh), lambda h, i: (0, h))
    return pl.pallas_call(
        body, name="xattn_bwd", grid=(XATTN_HEADS, t // tq),
        in_specs=[qblk, kblk, kblk, qblk],
        out_specs=[qblk, kblk, kblk],
        out_shape=[jax.ShapeDtypeStruct((t, d), BF16), jax.ShapeDtypeStruct((m, d), BF16),
                   jax.ShapeDtypeStruct((m, d), BF16)],
        scratch_shapes=[pltpu.VMEM((m, dh), F32), pltpu.VMEM((m, dh), F32)],
        compiler_params=_params("parallel", "arbitrary"),
    )(q, k, v, do)


def _ln_backward_rows(name, dmain, dres, xhat, rstd, gamma, tm):
    t, d = xhat.shape

    def body(m_ref, r_ref, x_ref, s_ref, g_ref, du_ref, dub_ref, dg_ref, db_ref):
        du, dg, db = _ln_backward_math(m_ref[...] + ALPHA * r_ref[...], x_ref[...], s_ref[...], g_ref[...])
        du_ref[...] = du
        dub_ref[...] = du.astype(BF16)
        first = pl.program_id(0) == 0

        @pl.when(first)
        def _():
            dg_ref[...] = dg
            db_ref[...] = db

        @pl.when(jnp.logical_not(first))
        def _():
            dg_ref[...] += dg
            db_ref[...] += db

    row = pl.BlockSpec((tm, d), lambda i: (i, 0))
    vec = pl.BlockSpec((1, d), lambda i: (0, 0))
    return pl.pallas_call(
        body, name=name, grid=(t // tm,),
        in_specs=[row, row, row, pl.BlockSpec((tm, 1), lambda i: (i, 0)), vec],
        out_specs=[row, row, vec, vec],
        out_shape=[jax.ShapeDtypeStruct((t, d), F32), jax.ShapeDtypeStruct((t, d), BF16),
                   jax.ShapeDtypeStruct((1, d), F32), jax.ShapeDtypeStruct((1, d), F32)],
        compiler_params=_params("arbitrary"),
    )(dmain, dres, xhat, rstd, gamma)


def _loss_and_ln_backward(xhat, rstd, gamma, beta, target, tm):
    t, d = xhat.shape

    def body(x_ref, r_ref, g_ref, b_ref, t_ref, du_ref, dub_ref, dg_ref, db_ref, loss_ref):
        xh = x_ref[...]
        g = g_ref[...]
        diff = xh * g + b_ref[...] - t_ref[...]
        part = jnp.sum(jnp.sum(diff * diff, axis=1, keepdims=True), axis=0, keepdims=True) * (0.5 / d)
        dy = diff * (1.0 / d)
        du, dg, db = _ln_backward_math(dy, xh, r_ref[...], g)
        du_ref[...] = du
        dub_ref[...] = du.astype(BF16)
        lossrow = jnp.broadcast_to(part, (1, HEAD_DIM))
        first = pl.program_id(0) == 0

        @pl.when(first)
        def _():
            dg_ref[...] = dg
            db_ref[...] = db
            loss_ref[...] = lossrow

        @pl.when(jnp.logical_not(first))
        def _():
            dg_ref[...] += dg
            db_ref[...] += db
            loss_ref[...] += lossrow

    row = pl.BlockSpec((tm, d), lambda i: (i, 0))
    vec = pl.BlockSpec((1, d), lambda i: (0, 0))
    return pl.pallas_call(
        body, name="loss_ln3_bwd", grid=(t // tm,),
        in_specs=[row, pl.BlockSpec((tm, 1), lambda i: (i, 0)), vec, vec, row],
        out_specs=[row, row, vec, vec, pl.BlockSpec((1, HEAD_DIM), lambda i: (0, 0))],
        out_shape=[jax.ShapeDtypeStruct((t, d), F32), jax.ShapeDtypeStruct((t, d), BF16),
                   jax.ShapeDtypeStruct((1, d), F32), jax.ShapeDtypeStruct((1, d), F32),
                   jax.ShapeDtypeStruct((1, HEAD_DIM), F32)],
        compiler_params=_params("arbitrary"),
    )(xhat, rstd, gamma, beta, target)


def _after(token, a):
    return a if token is None else a + token[:1, :1].astype(a.dtype)


def _pick(n, prefs):
    for p in prefs:
        if n % p == 0:
            return p
    return n


def _local_step(x, mem, target, w, x_bf=None):
    t, d = x.shape
    heads = w["a_log"].shape[1]
    gw = heads * HEAD_DIM
    groups, cg, _ = w["pool_w"].shape
    pw = groups * cg
    n_main = 4 * gw + pw
    in_cols = n_main + 2 * heads
    s_in = w["w_in_t"].shape[0]

    tm = _pick(t, (512, 256, 128))
    tm_ln = _pick(t, (256, 128))
    tm_big = _pick(t, (1024, 512, 256, 128))
    tk = _pick(d, K_STEPS)

    w_in_t = w["w_in_t"].reshape(in_cols, d)
    w_p_t = w_in_t[4 * gw + 2 * heads:]
    w_ba_t = jnp.pad(w_in_t[4 * gw:4 * gw + 2 * heads], ((0, HEAD_DIM - 2 * heads), (0, 0)))
    x_bf = x.astype(BF16) if x_bf is None else x_bf
    mem_bf = mem.astype(BF16)

    tn_d = _pick(d, (1024, 512, 256, 128))
    proj = _plain("proj_main", x_bf, w_in_t, tb=True, n_used=4 * gw, tm=tm_big, tn=_pick(4 * gw, (1024, 512, 256, 128)),
                  tk=tk, out_dtype=F32)
    pproj = _plain("proj_pool", x_bf, w_p_t, tb=True, tm=tm_big, tn=_pick(pw, (1024, 512, 256, 128)), tk=tk, out_dtype=F32)
    ea, dtb = _gate_vectors(w["a_log"], w["dt_bias"], heads)
    vec128 = lambda i, j: (0, 0)
    ba, bg = _matmul(
        "proj_gates", x_bf, w_ba_t, tb=True, tm=tm, tn=HEAD_DIM, tk=tk,
        extra=[(ea, (1, HEAD_DIM), vec128), (dtb, (1, HEAD_DIM), vec128)],
        outs=[(jax.ShapeDtypeStruct((t, HEAD_DIM), F32), (tm, HEAD_DIM), _tile)] * 2,
        epilogue=_gates_epilogue(heads))
    qkv = _gdn_pre(proj, w["conv_w"], heads)
    o_gdn, states, *solved = _gdn_core(qkv, bg, heads)
    cat_g = _gdn_post(o_gdn, proj, 3 * gw, w["gdn_norm_w"], heads, tm_big)
    token = yield ("pass", 1, cat_g)
    cat_p = _pool_forward(pproj, 0, w["pool_w"], _after(token, w["pool_scale"]))
    cat = jnp.concatenate([cat_g, cat_p], axis=1)
    w = {**w, **(yield ("weights", 1, cat))}
    h1, h1_bf, xhat1, rstd1 = _ln_forward("mix_ln1", cat, w["w_out"], x, w["ln1_g"], w["ln1_b"], tm=tm_ln, tk=tk)

    h1_bf = _after((yield ("relay", None, h1_bf)), h1_bf)
    q = _plain("xattn_q", h1_bf, w["xq_w"], tm=tm, tn=tn_d, tk=tk, out_dtype=BF16)
    mlen = mem.shape[0]
    tm_mem = _pick(mlen, (256, 128))
    k = _plain("xattn_k", mem_bf, w["xk_w"], tm=tm_mem, tn=tn_d, tk=tk, out_dtype=BF16)
    v = _plain("xattn_v", mem_bf, w["xv_w"], tm=tm_mem, tn=tn_d, tk=tk, out_dtype=BF16)
    att = _attention(q, k, v, tm)
    h2, h2_bf, xhat2, rstd2 = _ln_forward("xo_ln2", att, w["xo_w"], h1, w["ln2_g"], w["ln2_b"], tm=tm_ln, tk=tk)

    w = {**w, **(yield ("weights", 2, h2_bf))}
    s_up = w["w_up3"].shape[0]
    ff = s_up * w["w_up3"].shape[2]
    tn_f = _pick(ff // s_up, (1024, 512, 256, 128))

    def up_epi(acc, ex, out, i):
        r = jnp.maximum(acc, 0.0)
        out[0][...] = (r * r).astype(BF16)
        out[1][...] = (2.0 * r).astype(BF16)

    act, act_grad = _matmul(
        "mlp_up", h2_bf, w["w_up3"], b_blocks=s_up, tm=tm_big, tn=tn_f, tk=tk,
        outs=[(jax.ShapeDtypeStruct((t, ff), BF16), (tm_big, tn_f), _tile)] * 2, epilogue=up_epi)
    w = {**w, **(yield ("weights", 3, act))}
    tk_f = _pick(ff, K_STEPS)
    xhat3, rstd3 = _ln_forward("down_ln3", act, w["w_down"], h2, w["ln3_g"], w["ln3_b"], tm=tm, tk=tk_f, want_h=False)

    grads = {}
    du3, du3_bf, grads["ln3_g"], grads["ln3_b"], loss = _loss_and_ln_backward(
        xhat3, rstd3, w["ln3_g"], w["ln3_b"], target, tm)

    def dup_epi(acc, ex, out, i):
        out[0][...] = (acc * ex[0][...].astype(F32)).astype(BF16)

    dup = _matmul(
        "mlp_down_dx", du3_bf, w["w_down"], tb=True, tm=tm_big, tn=tn_f, tk=tk,
        extra=[(act_grad, (tm_big, tn_f), _tile)],
        outs=[(jax.ShapeDtypeStruct((t, ff), BF16), (tm_big, tn_f), _tile)], epilogue=dup_epi)[0]
    tk_t = _pick(t, K_STEPS)
    tm_w = _pick(d, (512, 256, 128))
    grads["w_down"] = _plain("mlp_down_dw", act, du3_bf, ta=True, tm=_pick(ff, (512, 256, 128)), tn=d, tk=tk_t,
                             out_dtype=F32)
    grads["w_up3"] = _plain("mlp_up_dw", h2_bf, dup, ta=True, tm=tm_w, tn=ff // s_up, tk=tk_t, out_dtype=F32, out3=s_up,
                            n_outer=True)
    token = yield ("grads", 0, {n: grads.pop(n) for n in ("w_down", "w_up3")})
    dh2 = _plain("mlp_up_dx", dup, w["w_up3"], tb=True, b_blocks=s_up, tm=tm_big, tn=tn_d,
                 tk=_pick(ff // s_up, K_STEPS), out_dtype=F32)
    du2, du2_bf, grads["ln2_g"], grads["ln2_b"] = _ln_backward_rows(
        "ln2_bwd", dh2, du3, xhat2, rstd2, _after(token, w["ln2_g"]), tm)
    token = yield ("poll", 0, du2_bf)

    grads["xo_w"] = _plain("xo_dw", att, du2_bf, ta=True, tm=tm_w, tn=d, tk=tk_t, out_dtype=F32)
    datt = _plain("xo_dx", du2_bf, w["xo_w"], tb=True, tm=tm, tn=tn_d, tk=tk, out_dtype=BF16)
    dq, dk, dv = _attention_backward(q, k, v, datt, tm)
    tk_m = _pick(mlen, (256, 128))
    grads["xq_w"] = _plain("xq_dw", h1_bf, dq, ta=True, tm=tm_w, tn=d, tk=tk_t, out_dtype=F32)
    grads["xk_w"] = _plain("xk_dw", mem_bf, dk, ta=True, tm=tm_w, tn=tn_d, tk=tk_m, out_dtype=F32)
    grads["xv_w"] = _plain("xv_dw", mem_bf, dv, ta=True, tm=tm_w, tn=tn_d, tk=tk_m, out_dtype=F32)
    du1, du1_bf, grads["ln1_g"], grads["ln1_b"] = _ln_backward(
        "xq_dx_ln1", dq, w["xq_w"], du2, xhat1, rstd1, _after(token, w["ln1_g"]), tm=tm_ln, tk=tk)

    grads["w_out"] = _plain("out_dw", cat, du1_bf, ta=True, tm=tm_w, tn=d, tk=tk_t, out_dtype=F32)
    token = yield ("grads", 1, {n: grads.pop(n) for n in ("xo_w", "xq_w", "xk_w", "xv_w", "w_out")})
    dcat = _plain("out_dx", du1_bf, w["w_out"], tb=True, tm=tm, tn=tn_d, tk=tk, out_dtype=F32)
    dp, grads["pool_w"], grads["pool_scale"] = _pool_backward(dcat, gw, pproj, 0, w["pool_w"],
                                                              _after(token, w["pool_scale"]))
    do_gdn, dz, grads["gdn_norm_w"] = _gdn_post_backward(dcat, o_gdn, proj, 3 * gw, _after(token, w["gdn_norm_w"]),
                                                         heads, tm_big)
    dqkv, dbg = _gdn_core_backward(qkv, bg, states, solved, do_gdn, heads)
    token = yield ("poll", 1, dqkv)
    dqkv_pre, grads["conv_w"] = _gdn_pre_backward(proj, _after(token, w["conv_w"]), dqkv, heads)
    dba, dalog_row, ddt_row = _gates_backward(ba, bg, dbg, ea, dtb, heads)
    grads["a_log"] = dalog_row[:, heads:2 * heads]
    grads["dt_bias"] = ddt_row[:, heads:2 * heads]

    k_pad = -(-in_cols // (2 * HEAD_DIM)) * (2 * HEAD_DIM)
    dproj = jnp.concatenate([dqkv_pre, dz, dba[:, :2 * heads], dp, jnp.zeros((t, k_pad - in_cols), BF16)], axis=1)
    dw_in_t = _plain("proj_dw", dproj, x_bf, ta=True, tm=_pick(k_pad, (512, 256, 128)), tn=d, tk=tk_t, out_dtype=F32,
                     m_kept=in_cols)
    grads["w_in_t"] = dw_in_t.reshape(s_in, in_cols // s_in, d)

    def dx_epi(acc, ex, out, i):
        out[0][...] = acc + ALPHA * ex[0][...]

    token = yield ("grads", 2, {n: grads.pop(n) for n in ("w_in_t", "pool_w")})
    w_in_t_pad = jnp.concatenate([w_in_t, _after(token, jnp.zeros((k_pad - in_cols, d), BF16))], axis=0)
    grad_x = _matmul(
        "proj_dx", dproj, w_in_t_pad, tm=tm, tn=tn_d, tk=k_pad, extra=[(du1, (tm, tn_d), _tile)],
        outs=[(jax.ShapeDtypeStruct((t, d), F32), (tm, tn_d), _tile)], epilogue=dx_epi)[0]
    yield ("poll", 2, grad_x)
    return loss, grad_x, grads


def _adamw(name, w, g, m, v):
    r, c = w.shape
    if r % 8 == 0:
        tr = _pick(r, (256, 128, 64, 32, 16, 8))
        blk, steps = pl.BlockSpec((tr, c), lambda i: (i, 0)), r // tr
    else:
        tc = _pick(c, (256, 128))
        blk, steps = pl.BlockSpec((r, tc), lambda i: (0, i)), c // tc
    c1 = 1.0 - ADAM_B1 ** ADAM_STEP
    c2 = 1.0 - ADAM_B2 ** ADAM_STEP

    def body(w_ref, g_ref, m_ref, v_ref, d_ref, mo_ref, vo_ref, go_ref):
        gv = g_ref[...]
        mn = ADAM_B1 * m_ref[...] + (1.0 - ADAM_B1) * gv
        vn = ADAM_B2 * v_ref[...] + (1.0 - ADAM_B2) * (gv * gv)
        d_ref[...] = -ADAM_LR * ((mn / c1) / (jnp.sqrt(vn / c2) + ADAM_EPS) + ADAM_WD * w_ref[...])
        mo_ref[...] = mn
        vo_ref[...] = vn
        go_ref[...] = gv

    return pl.pallas_call(
        body, name=name, grid=(steps,), in_specs=[blk] * 4, out_specs=[blk] * 4,
        out_shape=[jax.ShapeDtypeStruct((r, c), F32)] * 4,
        compiler_params=_params("parallel"),
    )(w, g, m, v)


def _place():
    x, y, c = lax.axis_index("x"), lax.axis_index("y"), lax.axis_index("c")
    chips = [(1 - x, y), (x, 1 - y), (1 - x, 1 - y)]
    return x, y, c, chips


HBM = pl.BlockSpec(memory_space=pltpu.HBM)


SEM = pl.BlockSpec(memory_space=pltpu.SEMAPHORE)
ANY = pl.BlockSpec(memory_space=pl.ANY)
EFFECT = pltpu.SideEffectType.DATAFLOW_SIDE_EFFECTING


def _in_hbm(a):
    return pltpu.with_memory_space_constraint(a, pltpu.HBM)


def _remote(src, dst, send_sem, recv_sem, to):
    return pltpu.make_async_remote_copy(src_ref=src, dst_ref=dst, send_sem=send_sem, recv_sem=recv_sem,
                                        device_id=to, device_id_type=MESH)


def _by_rows(rows):
    return rows % 32 == 0


def _half_shape(rows, cols):
    return (rows // 2, cols) if _by_rows(rows) else (rows, cols // 2)


def _half(ref, which, *lead):
    rows, cols = ref.shape[-2:]
    if _by_rows(rows):
        return ref.at[(*lead, pl.ds(which * (rows // 2), rows // 2))]
    return ref.at[(*lead, slice(None), pl.ds(which * (cols // 2), cols // 2))]


def _landed(lands, i, shard_index, which):
    return _half(lands[i], which, shard_index)


def _routes():
    x, y, c, _ = _place()
    first = (jnp.where(c == 0, 1 - x, x), jnp.where(c == 0, y, 1 - y))
    second = (jnp.where(c == 0, x, 1 - x), jnp.where(c == 0, 1 - y, y))
    return first, second, (1 - x, 1 - y)


def _shard_of(chip):
    return 2 * chip[0] + chip[1]


def _gather_start(name, shards, after, relayed=()):
    n = len(shards)
    lands = [lax.empty((N_SHARD,) + s.shape, s.dtype) for s in shards]

    def body(*refs):
        ins, zones = refs[:n], refs[n:2 * n]
        ici_send, ici_recv, own_send, own_recv = refs[2 * n + 1:2 * n + 5]
        token = refs[-1]
        x, y, c, chips = _place()
        me = 2 * x + y
        first, _, _ = _routes()
        for i in range(n):
            if i in relayed:
                _remote(_half(ins[i], c), _landed(zones, i, me, c), ici_send.at[3 * i], ici_recv.at[3 * i],
                        (*first, c)).start()
                continue
            for j, chip in enumerate(chips):
                _remote(_half(ins[i], c), _landed(zones, i, me, c), ici_send.at[3 * i + j],
                        ici_recv.at[3 * i + j], (*chip, c)).start()
        for i in range(n):
            _remote(ins[i], zones[i].at[me], own_send.at[i], own_recv.at[i], (x, y, 1 - c)).start()
        token[...] = jnp.zeros_like(token)

    dma = pltpu.SemaphoreType.DMA
    outs = pl.pallas_call(
        body, name=name,
        in_specs=[HBM] * (2 * n) + [ANY],
        out_shape=(dma((3 * n,)), dma((3 * n,)), dma((n,)), dma((n,)),
                   *[pltpu.HBM(a.shape, a.dtype) for a in shards + lands], jax.ShapeDtypeStruct((8, LANES), F32)),
        out_specs=(SEM, SEM, SEM, SEM, *[HBM] * (2 * n), pl.BlockSpec(memory_space=pltpu.VMEM)),
        input_output_aliases={k: 4 + k for k in range(2 * n)},
        compiler_params=pltpu.CompilerParams(has_side_effects=EFFECT),
    )(*[_in_hbm(a) for a in shards + lands], after)
    sems = dict(zip(("ici_send", "ici_recv", "own_send", "own_recv"), outs[:4]))
    return sems, list(outs[4:4 + n]), list(outs[4 + n:4 + 2 * n]), outs[-1]


def _gather_forward(name, idx, lands, sems, after):
    n = len(idx)

    def body(*refs):
        zones = refs[:n]
        ici_recv = refs[n]
        fwd_send, fwd_recv = refs[n + 2], refs[n + 3]
        x, y, c, chips = _place()
        for k, i in enumerate(idx):
            for j, chip in enumerate(chips):
                half = _landed(zones, k, 2 * chip[0] + chip[1], c)
                _remote(half, half, fwd_send.at[3 * k + j], ici_recv.at[3 * i + j], (*chip, c)).wait_recv()
                _remote(half, half, fwd_send.at[3 * k + j], fwd_recv.at[3 * k + j], (x, y, 1 - c)).start()
        refs[-1][...] = jnp.zeros_like(refs[-1])

    dma = pltpu.SemaphoreType.DMA
    outs = pl.pallas_call(
        body, name=name,
        in_specs=[HBM] * n + [SEM, ANY],
        out_shape=(dma((3 * n,)), dma((3 * n,)), *[pltpu.HBM(a.shape, a.dtype) for a in lands],
                   jax.ShapeDtypeStruct((8, LANES), F32)),
        out_specs=(SEM, SEM, *[HBM] * n, pl.BlockSpec(memory_space=pltpu.VMEM)),
        input_output_aliases={k: 2 + k for k in range(n)},
        compiler_params=pltpu.CompilerParams(has_side_effects=EFFECT),
    )(*lands, sems["ici_recv"], after)
    return (outs[0], outs[1]), list(outs[2:2 + n]), outs[-1]


def _gather_wait(name, idx, shards, lands, sems, fwd, after):
    n = len(idx)

    def body(*refs):
        ins, zones = refs[:n], refs[n:2 * n]
        ici_send, own_send, own_recv, fwd_send, fwd_recv = refs[2 * n:2 * n + 5]
        x, y, c, chips = _place()
        me = 2 * x + y
        for k, i in enumerate(idx):
            mine = _half(ins[k], c)
            for j, chip in enumerate(chips):
                theirs = 2 * chip[0] + chip[1]
                _remote(mine, _landed(zones, k, me, c), ici_send.at[3 * i + j], fwd_recv.at[3 * k + j],
                        (*chip, c)).wait_send()
                sent = _landed(zones, k, theirs, c)
                _remote(sent, sent, fwd_send.at[3 * k + j], fwd_recv.at[3 * k + j], (x, y, 1 - c)).wait_send()
                passed = _landed(zones, k, theirs, 1 - c)
                _remote(passed, passed, fwd_send.at[3 * k + j], fwd_recv.at[3 * k + j], (x, y, 1 - c)).wait_recv()
            own = _remote(ins[k], zones[k].at[me], own_send.at[i], own_recv.at[i], (x, y, 1 - c))
            own.wait_send()
            own.wait_recv()

    outs = pl.pallas_call(
        body, name=name,
        in_specs=[HBM] * (2 * n) + [SEM] * 5 + [ANY],
        out_shape=tuple(pltpu.HBM(a.shape, a.dtype) for a in lands),
        out_specs=tuple([HBM] * n),
        input_output_aliases={n + k: k for k in range(n)},
        compiler_params=pltpu.CompilerParams(has_side_effects=EFFECT),
    )(*shards, *lands, sems["ici_send"], sems["own_send"], sems["own_recv"], fwd[0], fwd[1], after)
    return list(outs)


def _gather_relay(name, idx, shards, lands, sems, after):
    n = len(idx)

    def body(*refs):
        ins, zones, ici_recv = refs[:n], refs[n:2 * n], refs[2 * n]
        relay_send, relay_recv, pass_send, pass_recv = refs[2 * n + 2:2 * n + 6]
        x, y, c, _ = _place()
        first, second, _ = _routes()
        for k, i in enumerate(idx):
            landed = _landed(zones, k, _shard_of(first), c)
            _remote(landed, landed, pass_send.at[k], ici_recv.at[3 * i], (*first, c)).wait_recv()
            _remote(_half(ins[k], c), _landed(zones, k, 2 * x + y, c), relay_send.at[2 * k], relay_recv.at[2 * k],
                    (*second, c)).start()
            _remote(landed, landed, relay_send.at[2 * k + 1], relay_recv.at[2 * k + 1], (*second, c)).start()
            _remote(landed, landed, pass_send.at[k], pass_recv.at[k], (x, y, 1 - c)).start()
        refs[-1][...] = jnp.zeros_like(refs[-1])

    dma = pltpu.SemaphoreType.DMA
    outs = pl.pallas_call(
        body, name=name,
        in_specs=[HBM] * (2 * n) + [SEM, ANY],
        out_shape=(dma((2 * n,)), dma((2 * n,)), dma((n,)), dma((n,)), *[pltpu.HBM(a.shape, a.dtype) for a in lands],
                   jax.ShapeDtypeStruct((8, LANES), F32)),
        out_specs=(SEM, SEM, SEM, SEM, *[HBM] * n, pl.BlockSpec(memory_space=pltpu.VMEM)),
        input_output_aliases={n + k: 4 + k for k in range(n)},
        compiler_params=pltpu.CompilerParams(has_side_effects=EFFECT),
    )(*shards, *lands, sems["ici_recv"], after)
    return outs[:4], list(outs[4:4 + n]), outs[-1]


def _gather_forward_relayed(name, ks, lands, relay, after):
    n = len(ks)

    def body(*refs):
        zones, relay_recv = refs[:n], refs[n]
        fwd_send, fwd_recv = refs[n + 2], refs[n + 3]
        x, y, c, _ = _place()
        _, second, diagonal = _routes()
        for p, k in enumerate(ks):
            for j, chip in enumerate((second, diagonal)):
                landed = _landed(zones, p, _shard_of(chip), c)
                _remote(landed, landed, fwd_send.at[2 * p + j], relay_recv.at[2 * k + j], (*second, c)).wait_recv()
                _remote(landed, landed, fwd_send.at[2 * p + j], fwd_recv.at[2 * p + j], (x, y, 1 - c)).start()

    dma = pltpu.SemaphoreType.DMA
    outs = pl.pallas_call(
        body, name=name,
        in_specs=[HBM] * n + [SEM, ANY],
        out_shape=(dma((2 * n,)), dma((2 * n,)), *[pltpu.HBM(a.shape, a.dtype) for a in lands]),
        out_specs=(SEM, SEM, *[HBM] * n),
        input_output_aliases={k: 2 + k for k in range(n)},
        compiler_params=pltpu.CompilerParams(has_side_effects=EFFECT),
    )(*lands, relay[1], after)
    return (outs[0], outs[1]), list(outs[2:])


def _gather_wait_relayed(name, idx, ks, shards, lands, sems, relay, fwd, after):
    n = len(idx)

    def body(*refs):
        ins, zones = refs[:n], refs[n:2 * n]
        ici_send, own_send, own_recv, relay_send, pass_send, pass_recv, fwd_send, fwd_recv = refs[2 * n:2 * n + 8]
        x, y, c, _ = _place()
        me = 2 * x + y
        sibling = (x, y, 1 - c)
        first, second, diagonal = _routes()
        for p, (i, k) in enumerate(zip(idx, ks)):
            mine, at_peer = _half(ins[p], c), _landed(zones, p, me, c)
            from_first = _landed(zones, p, _shard_of(first), c)
            _remote(mine, at_peer, ici_send.at[3 * i], pass_recv.at[k], (*first, c)).wait_send()
            _remote(mine, at_peer, relay_send.at[2 * k], pass_recv.at[k], (*second, c)).wait_send()
            _remote(from_first, from_first, relay_send.at[2 * k + 1], pass_recv.at[k], (*second, c)).wait_send()
            _remote(from_first, from_first, pass_send.at[k], pass_recv.at[k], sibling).wait_send()
            theirs = _landed(zones, p, _shard_of(second), 1 - c)
            _remote(theirs, theirs, pass_send.at[k], pass_recv.at[k], sibling).wait_recv()
            for j, (sent, got) in enumerate(((second, first), (diagonal, diagonal))):
                out_half = _landed(zones, p, _shard_of(sent), c)
                _remote(out_half, out_half, fwd_send.at[2 * p + j], fwd_recv.at[2 * p + j], sibling).wait_send()
                in_half = _landed(zones, p, _shard_of(got), 1 - c)
                _remote(in_half, in_half, fwd_send.at[2 * p + j], fwd_recv.at[2 * p + j], sibling).wait_recv()
            own = _remote(ins[p], zones[p].at[me], own_send.at[i], own_recv.at[i], sibling)
            own.wait_send()
            own.wait_recv()

    outs = pl.pallas_call(
        body, name=name,
        in_specs=[HBM] * (2 * n) + [SEM] * 8 + [ANY],
        out_shape=tuple(pltpu.HBM(a.shape, a.dtype) for a in lands),
        out_specs=tuple([HBM] * n),
        input_output_aliases={n + k: k for k in range(n)},
        compiler_params=pltpu.CompilerParams(has_side_effects=EFFECT),
    )(*shards, *lands, sems["ici_send"], sems["own_send"], sems["own_recv"], relay[0], relay[2], relay[3],
      fwd[0], fwd[1], after)
    return list(outs)


def _all_reduce_small(name, slab, after=None):
    r, width = slab.shape
    ndev = 8

    def body(x_ref, after_ref, out_ref, buf, send_sems, recv_sems):
        x, y, c, _ = _place()
        me = 4 * x + 2 * y + c
        buf[me] = x_ref[...]
        copies = []
        for k in range(1, ndev):
            peer = jnp.bitwise_xor(me, k)
            to = (peer // 4, (peer // 2) % 2, peer % 2)
            cp = pltpu.make_async_remote_copy(src_ref=x_ref, dst_ref=buf.at[me], send_sem=send_sems.at[k - 1],
                                              recv_sem=recv_sems.at[k - 1], device_id=to, device_id_type=MESH)
            cp.start()
            copies.append(cp)
        for k in range(1, ndev):
            peer = jnp.bitwise_xor(me, k)
            pltpu.make_async_remote_copy(src_ref=x_ref, dst_ref=buf.at[peer], send_sem=send_sems.at[k - 1],
                                         recv_sem=recv_sems.at[k - 1], device_id=(x, y, c),
                                         device_id_type=MESH).wait_recv()
        for cp in copies:
            cp.wait_send()
        total = buf[0]
        for d in range(1, ndev):
            total = total + buf[d]
        out_ref[...] = total

    return pl.pallas_call(
        body, name=name,
        in_specs=[pl.BlockSpec(memory_space=pltpu.VMEM), ANY], out_specs=pl.BlockSpec(memory_space=pltpu.VMEM),
        out_shape=jax.ShapeDtypeStruct((r, width), F32),
        scratch_shapes=[pltpu.VMEM((ndev, r, width), F32), pltpu.SemaphoreType.DMA((ndev - 1,)),
                        pltpu.SemaphoreType.DMA((ndev - 1,))],
        compiler_params=pltpu.CompilerParams(vmem_limit_bytes=VMEM_LIMIT),
    )(slab, slab if after is None else after)


def _half_tiling(rows, cols):
    if _by_rows(rows):
        tr = _pick(rows // 2, (256, 128, 64, 32, 16))
        nb = (rows // 2) // tr
        return (tr, cols), nb, (lambda which, b: (which * nb + b, 0)), (lambda b: (b, 0))
    tc = _pick(cols // 2, (256, 128))
    nb = (cols // 2) // tc
    return (rows, tc), nb, (lambda which, b: (0, which * nb + b)), (lambda b: (0, b))


def _chip_partial(name, grad, other, core):
    s, r, cdim = grad.shape
    blk, nb, whole, within = _half_tiling(r, cdim)

    def body(core_ref, g_ref, o_ref, out_ref):
        out_ref[...] = (g_ref[...] + o_ref[...]).astype(BF16)

    return pl.pallas_call(
        body, name=name,
        grid_spec=pltpu.PrefetchScalarGridSpec(
            num_scalar_prefetch=1, grid=(s, nb),
            in_specs=[pl.BlockSpec((None,) + blk, lambda j, b, core_ref: (j,) + whole(core_ref[0], b)),
                      pl.BlockSpec((None,) + blk, lambda j, b, core_ref: (j,) + within(b))],
            out_specs=pl.BlockSpec((None,) + blk, lambda j, b, core_ref: (j,) + within(b))),
        out_shape=jax.ShapeDtypeStruct((s,) + _half_shape(r, cdim), BF16),
        compiler_params=_params("parallel", "parallel"),
    )(core, grad, other)


def _partial_copies(ins, zones, send_sems, recv_sems):
    x, y, c, chips = _place()
    return [_remote(ins[i].at[2 * chip[0] + chip[1]], zones[i].at[j], send_sems.at[3 * i + j],
                    recv_sems.at[3 * i + j], (*chip, c))
            for i in range(len(ins)) for j, chip in enumerate(chips)]


def _swap_copies(ins, zones, send_sems, recv_sems):
    x, y, c, _ = _place()
    copies = []
    for i in range(len(ins)):
        for s in range(N_SHARD):
            copies.append(_remote(_half(ins[i], 1 - c, s), zones[i].at[s],
                                  send_sems.at[N_SHARD * i + s], recv_sems.at[N_SHARD * i + s], (x, y, 1 - c)))
    return copies


def _exchange_start(name, plan, sources, lands, per_array):
    n = len(sources)
    lands = [lax.empty(shape, dtype) for shape, dtype in lands]

    def body(*refs):
        for cp in plan(refs[:n], refs[n:2 * n], refs[2 * n], refs[2 * n + 1]):
            cp.start()
        refs[-1][...] = jnp.zeros_like(refs[-1])

    dma = pltpu.SemaphoreType.DMA
    outs = pl.pallas_call(
        body, name=name,
        in_specs=[HBM] * (2 * n),
        out_shape=(dma((per_array * n,)), dma((per_array * n,)),
                   *[pltpu.HBM(a.shape, a.dtype) for a in list(sources) + lands], jax.ShapeDtypeStruct((8, LANES), F32)),
        out_specs=(SEM, SEM, *[HBM] * (2 * n), pl.BlockSpec(memory_space=pltpu.VMEM)),
        input_output_aliases={k: 2 + k for k in range(2 * n)},
        compiler_params=pltpu.CompilerParams(has_side_effects=EFFECT),
    )(*[_in_hbm(a) for a in list(sources) + lands])
    return (outs[0], outs[1]), list(outs[2:2 + n]), list(outs[2 + n:2 + 2 * n]), outs[-1]


def _exchange_wait(name, plan, started, after):
    sems, partials, lands, _ = started
    n = len(partials)

    def body(*refs):
        for cp in plan(refs[:n], refs[n:2 * n], refs[2 * n], refs[2 * n + 1]):
            cp.wait_send()
            cp.wait_recv()

    outs = pl.pallas_call(
        body, name=name,
        in_specs=[HBM] * (2 * n) + [SEM, SEM] + [ANY] * len(after),
        out_shape=tuple(pltpu.HBM(a.shape, a.dtype) for a in lands),
        out_specs=tuple([HBM] * n),
        input_output_aliases={n + k: k for k in range(n)},
        compiler_params=pltpu.CompilerParams(has_side_effects=EFFECT),
    )(*partials, *lands, sems[0], sems[1], *after)
    return list(outs)


def _reduce_own(name, grad, other, received, where):
    s, r, cdim = grad.shape
    blk, nb, whole, within = _half_tiling(r, cdim)

    def body(where_ref, g_ref, o_ref, r_ref, out_ref):
        total = g_ref[...] + o_ref[...]
        for j in range(3):
            total = total + r_ref[j].astype(F32)
        out_ref[...] = total

    return pl.pallas_call(
        body, name=name,
        grid_spec=pltpu.PrefetchScalarGridSpec(
            num_scalar_prefetch=1, grid=(nb,),
            in_specs=[pl.BlockSpec((None,) + blk, lambda b, w_ref: (w_ref[0],) + whole(w_ref[1], b)),
                      pl.BlockSpec((None,) + blk, lambda b, w_ref: (w_ref[0],) + within(b)),
                      pl.BlockSpec((3,) + blk, lambda b, w_ref: (0,) + within(b))],
            out_specs=pl.BlockSpec(blk, lambda b, w_ref: whole(w_ref[1], b))),
        out_shape=jax.ShapeDtypeStruct((r, cdim), F32),
        compiler_params=_params("parallel"),
    )(where, grad, other, received)


def _join_start(name, halves):
    n = len(halves)

    def body(*refs):
        bufs, send_sems, recv_sems = refs[:n], refs[n], refs[n + 1]
        x, y, c, _ = _place()
        for i in range(n):
            mine = _half(bufs[i], c)
            _remote(mine, mine, send_sems.at[i], recv_sems.at[i], (x, y, 1 - c)).start()
        refs[-1][...] = jnp.zeros_like(refs[-1])

    dma = pltpu.SemaphoreType.DMA
    outs = pl.pallas_call(
        body, name=name,
        in_specs=[HBM] * n,
        out_shape=(dma((n,)), dma((n,)), *[pltpu.HBM(h.shape, F32) for h in halves], jax.ShapeDtypeStruct((8, LANES), F32)),
        out_specs=(SEM, SEM, *[HBM] * n, pl.BlockSpec(memory_space=pltpu.VMEM)),
        input_output_aliases={k: 2 + k for k in range(n)},
        compiler_params=pltpu.CompilerParams(has_side_effects=EFFECT),
    )(*[_in_hbm(h) for h in halves])
    return (outs[0], outs[1]), list(outs[2:2 + n]), outs[-1]


def _join_wait(name, started, after):
    sems, bufs, _ = started
    n = len(bufs)

    def body(*refs):
        bufs, send_sems, recv_sems = refs[:n], refs[n], refs[n + 1]
        x, y, c, _ = _place()
        for i in range(n):
            mine, theirs = _half(bufs[i], c), _half(bufs[i], 1 - c)
            _remote(mine, mine, send_sems.at[i], recv_sems.at[i], (x, y, 1 - c)).wait_send()
            _remote(theirs, theirs, send_sems.at[i], recv_sems.at[i], (x, y, 1 - c)).wait_recv()

    outs = pl.pallas_call(
        body, name=name,
        in_specs=[HBM] * n + [SEM, SEM] + [ANY] * len(after),
        out_shape=tuple(pltpu.HBM(b.shape, F32) for b in bufs),
        out_specs=tuple([HBM] * n),
        input_output_aliases={k: k for k in range(n)},
        compiler_params=pltpu.CompilerParams(has_side_effects=EFFECT),
    )(*bufs, sems[0], sems[1], *after)
    return list(outs)


BIG = ("w_in", "pool_w", "w_out", "xq_w", "xk_w", "xv_w", "xo_w", "w_up", "w_down", "conv_w")
KEPT_F32 = ("conv_w",)
GATHER_GROUPS = ((0, 1, 9), (2, 3, 4, 5, 6), (7,), (8,))
RELAYED = (7, 8)
SMALL = ("conv_w", "a_log", "dt_bias", "gdn_norm_w", "pool_scale", "ln1_g", "ln1_b", "ln2_g", "ln2_b", "ln3_g", "ln3_b")
ORDER = ("w_in", "conv_w", "a_log", "dt_bias", "gdn_norm_w", "pool_w", "pool_scale", "w_out", "ln1_g", "ln1_b",
         "xq_w", "xk_w", "xv_w", "xo_w", "ln2_g", "ln2_b", "w_up", "w_down", "ln3_g", "ln3_b")
LANES = 128


def _rows(flat_len):
    return -(-flat_len // LANES)


def _pack(pieces):
    out = []
    for p in pieces:
        flat = p.reshape(-1).astype(F32)
        out.append(jnp.pad(flat, (0, _rows(flat.shape[0]) * LANES - flat.shape[0])).reshape(-1, LANES))
    slab = jnp.concatenate(out, axis=0)
    return jnp.pad(slab, ((0, -slab.shape[0] % 8), (0, 0)))


def _unpack(slab, shapes):
    out, row = [], 0
    for shp in shapes:
        size = math.prod(shp)
        out.append(slab[row:row + _rows(size)].reshape(-1)[:size].reshape(shp))
        row += _rows(size)
    return out


TRANSPOSED = ("w_in",)


def _as2d(name, a):
    a = a[0]
    if name in TRANSPOSED:
        return jnp.swapaxes(a, 0, 1)
    return a.reshape(-1, a.shape[-1]) if a.ndim == 3 else a


def _from2d(name, a, shape):
    return (jnp.swapaxes(a, 0, 1) if name in TRANSPOSED else a).reshape(shape)


def kernel(x, mem, w_in, conv_w, a_log, dt_bias, gdn_norm_w, pool_w, pool_scale, w_out, ln1_g, ln1_b, xq_w, xk_w, xv_w, xo_w, ln2_g, ln2_b, w_up, w_down, ln3_g, ln3_b, loss_target, m_w_in, m_conv_w, m_a_log, m_dt_bias, m_gdn_norm_w, m_pool_w, m_pool_scale, m_w_out, m_ln1_g, m_ln1_b, m_xq_w, m_xk_w, m_xv_w, m_xo_w, m_ln2_g, m_ln2_b, m_w_up, m_w_down, m_ln3_g, m_ln3_b, v_w_in, v_conv_w, v_a_log, v_dt_bias, v_gdn_norm_w, v_pool_w, v_pool_scale, v_w_out, v_ln1_g, v_ln1_b, v_xq_w, v_xk_w, v_xv_w, v_xo_w, v_ln2_g, v_ln2_b, v_w_up, v_w_down, v_ln3_g, v_ln3_b):
    given = dict(locals())
    cx, cy, cc = lax.axis_index("x"), lax.axis_index("y"), lax.axis_index("c")
    me = 2 * cx + cy
    groups = pool_w.shape[1]
    cs = pool_w.shape[2]
    kk, conv_cols = conv_w.shape[1], conv_w.shape[2]
    core = cc.astype(jnp.int32).reshape(1)
    where = jnp.stack([me, cc]).astype(jnp.int32)

    started = {}
    wts = {}

    def start(name, idx, after, token=None):
        casts = [_after(token, _as2d(BIG[i], given[BIG[i]])).astype(F32 if BIG[i] in KEPT_F32 else BF16) for i in idx]
        relayed = tuple(k for k, i in enumerate(idx) if i in RELAYED)
        sems, shards, lands, token = _gather_start(name, casts, after, relayed)
        for k, i in enumerate(idx):
            started[i] = (sems, k, shards[k], lands[k])
        return token

    token = start("gather_start_first", GATHER_GROUPS[0], x)
    token = start("gather_start_rest", tuple(i for group in GATHER_GROUPS[1:] for i in group), token, token)

    relay = {}

    def send_on(after):
        members = [started[i] for i in RELAYED]
        relay["sems"], zones, token = _gather_relay("gather_relay", [m[1] for m in members], [m[2] for m in members],
                                                    [m[3] for m in members], members[0][0], after)
        relay["zones"] = dict(zip(RELAYED, zones))
        return token

    passed = {}

    def pass_on(group, after):
        members = [started[i] for i in GATHER_GROUPS[group]]
        fwd, zones, token = _gather_forward(f"gather_forward_{group}", [m[1] for m in members], [m[3] for m in members],
                                            members[0][0], after)
        passed[group] = (fwd, zones)
        return token

    def fetch(group, after):
        members = [started[i] for i in GATHER_GROUPS[group]]
        sems, idx = members[0][0], [m[1] for m in members]
        shards = [m[2] for m in members]
        if GATHER_GROUPS[group][0] in RELAYED:
            ks = [RELAYED.index(i) for i in GATHER_GROUPS[group]]
            zones = [relay["zones"][i] for i in GATHER_GROUPS[group]]
            fwd, zones = _gather_forward_relayed(f"gather_forward_{group}", ks, zones, relay["sems"], after)
            got = _gather_wait_relayed(f"gather_wait_{group}", idx, ks, shards, zones, sems, relay["sems"], fwd, after)
        else:
            if group not in passed:
                pass_on(group, after)
            fwd, zones = passed[group]
            got = _gather_wait(f"gather_wait_{group}", idx, shards, zones, sems, fwd, after)
        full = dict(zip([BIG[i] for i in GATHER_GROUPS[group]], got))
        out = {}
        for n, a in full.items():
            if n == "w_in":
                out["w_in_t"] = a
            elif n == "w_up":
                out["w_up3"] = a
            elif n == "pool_w":
                out[n] = a.reshape(N_SHARD, groups, cs, -1).transpose(1, 0, 2, 3).reshape(groups, N_SHARD * cs, -1)
            elif n == "conv_w":
                out[n] = a.transpose(1, 0, 2).reshape(kk, N_SHARD * conv_cols)
            else:
                out[n] = a.reshape(-1, a.shape[-1])
        return out

    for n in ("a_log", "dt_bias", "gdn_norm_w", "pool_scale", "ln1_g", "ln1_b", "ln2_g", "ln2_b", "ln3_g", "ln3_b"):
        wts[n] = given[n]
    x_bf = _after(token, x[0]).astype(BF16)
    wts.update(fetch(0, x_bf))

    def start_swap(group, grads):
        names, blocks = [], []
        for n, g in grads.items():
            if n == "pool_w":
                g = g.reshape(groups, N_SHARD, cs, -1).transpose(1, 0, 2, 3).reshape(N_SHARD, groups * cs, -1)
            elif g.ndim == 2:
                g = g.reshape(N_SHARD, -1, g.shape[-1])
            names.append({"w_in_t": "w_in", "w_up3": "w_up"}.get(n, n))
            blocks.append(g)
        zones = [((N_SHARD,) + _half_shape(b.shape[1], b.shape[2]), F32) for b in blocks]
        swap = _exchange_start(f"grad_swap_start_{group}", _swap_copies, blocks, zones, N_SHARD)
        return {"group": group, "names": names, "swap": swap, "token": swap[3]}

    def start_send(state, after):
        group, names = state["group"], state["names"]
        state["blocks"] = state["swap"][1]
        state["others"] = _exchange_wait(f"grad_swap_wait_{group}", _swap_copies, state["swap"], after)
        partials = [_chip_partial("chip_partial_" + n, gb, ob, core)
                    for n, gb, ob in zip(names, state["blocks"], state["others"])]
        zones = [((3,) + p.shape[1:], BF16) for p in partials]
        state["send"] = _exchange_start(f"grad_send_start_{group}", _partial_copies, partials, zones, 3)
        state["token"] = state["send"][3]

    grad, delta, new_m, new_v = {}, {}, {}, {}

    def start_join(state, after):
        group, names = state["group"], state["names"]
        received = _exchange_wait(f"grad_send_wait_{group}", _partial_copies, state["send"], after)
        halves = [_reduce_own("reduce_own_" + n, gb, ob, rb, where)
                  for n, gb, ob, rb in zip(names, state["blocks"], state["others"], received)]
        state["join"] = _join_start(f"grad_join_start_{group}", halves)
        return state["join"][2]

    def finish_reduce(state, after):
        group, names = state["group"], state["names"]
        for n, g in zip(names, _join_wait(f"grad_join_wait_{group}", state["join"], after)):
            shp = given[n].shape
            d2, m2, v2, g2 = _adamw("adamw_" + n, _as2d(n, given[n]), g, _as2d(n, given["m_" + n]),
                                    _as2d(n, given["v_" + n]))
            grad[n], delta[n], new_m[n], new_v[n] = (_from2d(n, a, shp) for a in (g2, d2, m2, v2))
        return d2

    step = _local_step(x[0], mem[0], loss_target[0], wts, x_bf)
    pending = {}
    request = next(step)
    while True:
        try:
            kind, group, payload = request
            if kind == "weights":
                request = step.send(fetch(group, payload))
            elif kind == "relay":
                request = step.send(send_on(payload))
            elif kind == "pass":
                request = step.send(pass_on(group, payload))
            elif kind == "grads":
                pending[group] = start_swap(group, payload)
                request = step.send(pending[group]["token"])
            else:
                start_send(pending[group], [payload])
                request = step.send(pending[group]["token"])
        except StopIteration as stop:
            loss_row, grad_x, g = stop.value
            break

    after = [pending[2]["token"], grad_x]
    for group in (0, 1):
        after = [start_join(pending[group], after)]
    for group in (0, 1):
        after = [finish_reduce(pending[group], after)]
    after = [finish_reduce(pending[2], [start_join(pending[2], after)])]

    small_names = ("a_log", "dt_bias", "gdn_norm_w", "pool_scale", "ln1_g", "ln1_b", "ln2_g", "ln2_b", "ln3_g", "ln3_b")
    pieces = [g["conv_w"]] + [g[n] for n in small_names] + [loss_row[:, :1]]
    shapes = [p.shape for p in pieces]
    summed = _unpack(_all_reduce_small("all_reduce_small", _pack(pieces), after[0]), shapes)
    gsmall = dict(zip(small_names, summed[1:-1]))
    gsmall["conv_w"] = lax.dynamic_slice(summed[0], (0, me * conv_cols), (kk, conv_cols))
    loss = summed[-1][0, 0]

    sshapes = [given[n].shape for n in SMALL]
    slabs = [_pack([given[p + n] for n in SMALL]) for p in ("", "m_", "v_")]
    gslab = _pack([gsmall[n] for n in SMALL])
    outs = _adamw("adamw_small", slabs[0], gslab, slabs[1], slabs[2])[:3]
    for dst, slab in zip((delta, new_m, new_v), outs):
        dst.update(zip(SMALL, _unpack(slab, sshapes)))
    for n in SMALL:
        grad[n] = gsmall[n].reshape(given[n].shape)

    return (loss, grad_x[None], *[grad[n] for n in ORDER], *[delta[n] for n in ORDER],
            *[new_m[n] for n in ORDER], *[new_v[n] for n in ORDER])
```

```python
import math

import jax
import jax.numpy as jnp
from jax import lax
from jax.experimental import pallas as pl
from jax.experimental.pallas import tpu as pltpu

F32 = jnp.float32
BF16 = jnp.bfloat16
MESH = pl.DeviceIdType.MESH

HEAD_DIM = 128
CHUNK = 64
POOL_WINDOWS = (2, 4, 8, 16)
XATTN_HEADS = 4
ALPHA = 2.0 ** 0.25
LN_EPS = 1e-5
NORM_EPS = 1e-6
ADAM_LR, ADAM_B1, ADAM_B2, ADAM_EPS, ADAM_WD, ADAM_STEP = 0.001, 0.9, 0.999, 1e-08, 0.01, 10
N_SHARD = 4
VMEM_LIMIT = 56 * 1024 * 1024
K_STEPS = (2048, 1024, 512, 256, 128)


def _params(*sem):
    return pltpu.CompilerParams(dimension_semantics=sem, vmem_limit_bytes=VMEM_LIMIT)


def _bdot(a, b, ta=False, tb=False):
    dims = (((0 if ta else 1,), (1 if tb else 0,)), ((), ()))
    return lax.dot_general(a.astype(BF16), b.astype(BF16), dims, preferred_element_type=F32)


def _sigmoid(x):
    return 1.0 / (1.0 + jnp.exp(-x))


def _matmul(name, a, b, *, ta=False, tb=False, tm, tn, tk, extra=(), outs, epilogue, b_blocks=None,
            sequential=False, n_used=None, n_outer=False):
    m, k_dim = (a.shape[1], a.shape[0]) if ta else a.shape
    if b_blocks and tb:
        n = b.shape[1]
        k_dim = b.shape[0] * b.shape[2]
        per = b.shape[2] // tk
        b_spec = pl.BlockSpec((None, tn, tk), lambda i, j, k: (k // per, j, k % per))
    elif b_blocks:
        n = b.shape[0] * b.shape[2]
        per = b.shape[2] // tn
        b_spec = pl.BlockSpec((None, tk, tn), lambda i, j, k: (j // per, k, j % per))
    elif tb:
        n = b.shape[0]
        b_spec = pl.BlockSpec((tn, tk), lambda i, j, k: (j, k))
    else:
        n = b.shape[1]
        b_spec = pl.BlockSpec((tk, tn), lambda i, j, k: (k, j))
    n = n_used or n
    assert m % tm == 0 and n % tn == 0 and k_dim % tk == 0, (name, m, n, k_dim, tm, tn, tk)
    nk = k_dim // tk
    a_spec = pl.BlockSpec((tk, tm), lambda i, j, k: (k, i)) if ta else pl.BlockSpec((tm, tk), lambda i, j, k: (i, k))
    n_extra, n_out = len(extra), len(outs)

    def wrap(index_map):
        return lambda i, j, k: index_map(i, j)

    def spec(block, index_map):
        if n_outer:
            return pl.BlockSpec(block, lambda j, i, k: index_map(i, j, k))
        return pl.BlockSpec(block, index_map)

    row_axis = 1 if n_outer else 0

    def body_one_step(*refs):
        ex = refs[2:2 + n_extra]
        out = refs[2 + n_extra:2 + n_extra + n_out]
        epilogue(_bdot(refs[0][...], refs[1][...], ta, tb), ex, out, pl.program_id(row_axis))

    def body(*refs):
        a_ref, b_ref = refs[0], refs[1]
        ex = refs[2:2 + n_extra]
        out = refs[2 + n_extra:2 + n_extra + n_out]
        acc = refs[-1]
        i, k = pl.program_id(row_axis), pl.program_id(2)
        part = _bdot(a_ref[...], b_ref[...], ta, tb)

        @pl.when(k == 0)
        def _():
            acc[...] = part

        @pl.when(jnp.logical_and(k > 0, k < nk - 1))
        def _():
            acc[...] += part

        @pl.when(k == nk - 1)
        def _():
            epilogue(acc[...] + part, ex, out, i)

    sem = ("arbitrary",) * 3 if sequential else ("parallel", "parallel", "arbitrary")
    res = pl.pallas_call(
        body_one_step if nk == 1 else body, name=name,
        grid=(n // tn, m // tm, nk) if n_outer else (m // tm, n // tn, nk),
        in_specs=[spec(a_spec.block_shape, a_spec.index_map), spec(b_spec.block_shape, b_spec.index_map)]
        + [spec(bs, wrap(im)) for _, bs, im in extra],
        out_specs=[spec(bs, wrap(im)) for _, bs, im in outs],
        out_shape=[s for s, _, _ in outs],
        scratch_shapes=[] if nk == 1 else [pltpu.VMEM((tm, tn), F32)],
        compiler_params=_params(*sem),
    )(a, b, *[x for x, _, _ in extra])
    return res


def _tile(i, j):
    return (i, j)


def _plain(name, a, b, *, ta=False, tb=False, tm, tn, tk, out_dtype, b_blocks=None, out3=None, n_used=None,
           n_outer=False, m_kept=None):
    m = a.shape[1] if ta else a.shape[0]
    if b_blocks:
        n = b.shape[1] if tb else b.shape[0] * b.shape[2]
    else:
        n = n_used or (b.shape[0] if tb else b.shape[1])

    def epi(acc, ex, out, i):
        out[0][...] = acc.astype(out_dtype)

    if out3:
        per = (n // out3) // tn
        spec = (jax.ShapeDtypeStruct((out3, m, n // out3), out_dtype), (None, tm, tn),
                lambda i, j: (j // per, i, j % per))
    else:
        spec = (jax.ShapeDtypeStruct((m_kept or m, n), out_dtype), (tm, tn), _tile)
    return _matmul(name, a, b, ta=ta, tb=tb, tm=tm, tn=tn, tk=tk, outs=[spec], epilogue=epi,
                   b_blocks=b_blocks, n_used=n_used, n_outer=n_outer)[0]


def _ln_forward(name, a, b, res, gamma, beta, *, tm, tk, want_h=True):
    m, n = res.shape

    def epi(acc, ex, out, i):
        u = ALPHA * ex[0][...] + acc
        mu = jnp.mean(u, axis=-1, keepdims=True)
        xc = u - mu
        var = jnp.mean(xc * xc, axis=-1, keepdims=True)
        rstd = lax.rsqrt(var + LN_EPS)
        xhat = xc * rstd
        out[-2][...] = xhat
        out[-1][...] = rstd
        if want_h:
            h = xhat * ex[1][...] + ex[2][...]
            out[0][...] = h
            out[1][...] = h.astype(BF16)

    row = lambda i, j: (i, 0)
    vec = lambda i, j: (0, 0)
    outs = [(jax.ShapeDtypeStruct((m, n), F32), (tm, n), row), (jax.ShapeDtypeStruct((m, n), BF16), (tm, n), row),
            (jax.ShapeDtypeStruct((m, n), F32), (tm, n), row), (jax.ShapeDtypeStruct((m, 1), F32), (tm, 1), row)]
    return _matmul(
        name, a, b, tm=tm, tn=n, tk=tk,
        extra=[(res, (tm, n), row), (gamma, (1, n), vec), (beta, (1, n), vec)],
        outs=outs if want_h else outs[2:], epilogue=epi)


def _ln_backward_math(dy, xhat, rstd, gamma):
    dxhat = dy * gamma
    m1 = jnp.mean(dxhat, axis=-1, keepdims=True)
    m2 = jnp.mean(dxhat * xhat, axis=-1, keepdims=True)
    du = rstd * (dxhat - m1 - xhat * m2)
    return du, jnp.sum(dy * xhat, axis=0, keepdims=True), jnp.sum(dy, axis=0, keepdims=True)


def _ln_backward(name, a, b, dres, xhat, rstd, gamma, *, tm, tk, b_blocks=None, tb=True):
    m, n = dres.shape

    def epi(acc, ex, out, i):
        dy = acc + ALPHA * ex[0][...]
        du, dg, db = _ln_backward_math(dy, ex[1][...], ex[2][...], ex[3][...])
        out[0][...] = du
        out[1][...] = du.astype(BF16)
        first = i == 0

        @pl.when(first)
        def _():
            out[2][...] = dg
            out[3][...] = db

        @pl.when(jnp.logical_not(first))
        def _():
            out[2][...] += dg
            out[3][...] += db

    row = lambda i, j: (i, 0)
    vec = lambda i, j: (0, 0)
    return _matmul(
        name, a, b, tb=tb, tm=tm, tn=n, tk=tk, b_blocks=b_blocks, sequential=True,
        extra=[(dres, (tm, n), row), (xhat, (tm, n), row), (rstd, (tm, 1), row), (gamma, (1, n), vec)],
        outs=[(jax.ShapeDtypeStruct((m, n), F32), (tm, n), row),
              (jax.ShapeDtypeStruct((m, n), BF16), (tm, n), row),
              (jax.ShapeDtypeStruct((1, n), F32), (1, n), vec),
              (jax.ShapeDtypeStruct((1, n), F32), (1, n), vec)],
        epilogue=epi)


def _shift_down(x, k):
    row = lax.broadcasted_iota(jnp.int32, x.shape, 0)
    return jnp.where(row >= k, pltpu.roll(x, k, axis=0), 0.0)


def _shift_up(x, k):
    t = x.shape[0]
    row = lax.broadcasted_iota(jnp.int32, x.shape, 0)
    return jnp.where(row < t - k, pltpu.roll(x, t - k, axis=0), 0.0)


def _conv_silu_norm(x, w, normalise):
    kk = w.shape[0]
    c = x * w[kk - 1:kk, :]
    for j in range(kk - 1):
        c = c + _shift_down(x, kk - 1 - j) * w[j:j + 1, :]
    sg = _sigmoid(c)
    s = c * sg
    r = lax.rsqrt(jnp.sum(s * s, axis=-1, keepdims=True) + NORM_EPS)
    y = jnp.where(normalise, s * r, s)
    return c, sg, s, r, y


def _gdn_pre(proj, conv_w, heads):
    t = proj.shape[0]
    kk = conv_w.shape[0]

    def body(x_ref, w_ref, o_ref):
        normalise = pl.program_id(0) < 2
        o_ref[...] = _conv_silu_norm(x_ref[...], w_ref[...], normalise)[4]

    col = lambda s, h: (0, s * heads + h)
    return pl.pallas_call(
        body, name="gdn_pre", grid=(3, heads),
        in_specs=[pl.BlockSpec((t, HEAD_DIM), col), pl.BlockSpec((kk, HEAD_DIM), col)],
        out_specs=pl.BlockSpec((t, HEAD_DIM), col),
        out_shape=jax.ShapeDtypeStruct((t, 3 * heads * HEAD_DIM), F32),
        compiler_params=_params("parallel", "parallel"),
    )(proj, conv_w)


def _gdn_pre_backward(proj, conv_w, dqkv, heads):
    t = proj.shape[0]
    kk = conv_w.shape[0]

    def body(x_ref, w_ref, dy_ref, dx_ref, dw_ref):
        normalise = pl.program_id(0) < 2
        x = x_ref[...]
        w = w_ref[...]
        dy = dy_ref[...]
        c, sg, s, r, y = _conv_silu_norm(x, w, normalise)
        ds_norm = r * (dy - y * jnp.sum(dy * y, axis=-1, keepdims=True))
        ds = jnp.where(normalise, ds_norm, dy)
        dc = ds * (sg * (1.0 + c * (1.0 - sg)))
        dx = dc * w[kk - 1:kk, :]
        rows = [None] * kk
        rows[kk - 1] = jnp.sum(dc * x, axis=0, keepdims=True)
        for j in range(kk - 1):
            lag = kk - 1 - j
            dx = dx + _shift_up(dc, lag) * w[j:j + 1, :]
            rows[j] = jnp.sum(dc * _shift_down(x, lag), axis=0, keepdims=True)
        dx_ref[...] = dx.astype(BF16)
        dw_ref[...] = jnp.concatenate(rows, axis=0)

    col = lambda s, h: (0, s * heads + h)
    return pl.pallas_call(
        body, name="gdn_pre_bwd", grid=(3, heads),
        in_specs=[pl.BlockSpec((t, HEAD_DIM), col), pl.BlockSpec((kk, HEAD_DIM), col),
                  pl.BlockSpec((t, HEAD_DIM), col)],
        out_specs=[pl.BlockSpec((t, HEAD_DIM), col), pl.BlockSpec((kk, HEAD_DIM), col)],
        out_shape=[jax.ShapeDtypeStruct((t, 3 * heads * HEAD_DIM), BF16),
                   jax.ShapeDtypeStruct((kk, 3 * heads * HEAD_DIM), F32)],
        compiler_params=_params("parallel", "parallel"),
    )(proj, conv_w, dqkv)


def _gate_vectors(a_log, dt_bias, heads):
    pad = lambda v: jnp.pad(v.astype(F32), ((0, 0), (heads, HEAD_DIM - 2 * heads)))
    return pad(jnp.exp(a_log.astype(F32))), pad(dt_bias)


def _softplus(x):
    return jnp.maximum(x, 0.0) + jnp.log(1.0 + jnp.exp(-jnp.abs(x)))


def _gates_epilogue(heads):
    def epi(acc, ex, out, i):
        lane = lax.broadcasted_iota(jnp.int32, acc.shape, 1)
        beta = _sigmoid(acc)
        g = -ex[0][...] * _softplus(acc + ex[1][...])
        out[0][...] = acc
        out[1][...] = jnp.where(lane < heads, beta, jnp.where(lane < 2 * heads, g, 0.0))
    return epi


def _gates_backward(ba, bg, dbg, ea, dtb, heads):
    t = ba.shape[0]

    def body(ba_ref, bg_ref, d_ref, ea_ref, dt_ref, dba_ref, dal_ref, ddt_ref):
        lane = lax.broadcasted_iota(jnp.int32, (t, HEAD_DIM), 1)
        bgv = bg_ref[...]
        d = d_ref[...]
        db = d * bgv * (1.0 - bgv)
        da = -d * ea_ref[...] * _sigmoid(ba_ref[...] + dt_ref[...])
        is_g = jnp.logical_and(lane >= heads, lane < 2 * heads)
        dba = jnp.where(lane < heads, db, jnp.where(is_g, da, 0.0))
        dba_ref[...] = dba.astype(BF16)
        dal_ref[...] = jnp.sum(jnp.where(is_g, d * bgv, 0.0), axis=0, keepdims=True)
        ddt_ref[...] = jnp.sum(jnp.where(is_g, da, 0.0), axis=0, keepdims=True)

    full = pl.BlockSpec((t, HEAD_DIM), lambda: (0, 0))
    vec = pl.BlockSpec((1, HEAD_DIM), lambda: (0, 0))
    return pl.pallas_call(
        body, name="gates_bwd", grid=(),
        in_specs=[full, full, full, vec, vec], out_specs=[full, vec, vec],
        out_shape=[jax.ShapeDtypeStruct((t, HEAD_DIM), BF16), jax.ShapeDtypeStruct((1, HEAD_DIM), F32),
                   jax.ShapeDtypeStruct((1, HEAD_DIM), F32)],
        compiler_params=pltpu.CompilerParams(vmem_limit_bytes=VMEM_LIMIT),
    )(ba, bg, dbg, ea, dtb)


class _Chunk:
    pass


def _split2(x):
    hi = x.astype(BF16)
    return hi, (x - hi.astype(F32)).astype(BF16)


def _split3(x):
    hi = x.astype(BF16)
    rest = x - hi.astype(F32)
    mid = rest.astype(BF16)
    return hi, mid, (rest - mid.astype(F32)).astype(BF16)


def _dot_mask(mask, x, ta=False):
    hi, mid, lo = _split3(x)
    return _bdot(mask, hi, ta=ta) + (_bdot(mask, mid, ta=ta) + _bdot(mask, lo, ta=ta))


def _transpose_by_identity(x):
    r = x.shape[0]
    eye = (lax.broadcasted_iota(jnp.int32, (r, r), 0) == lax.broadcasted_iota(jnp.int32, (r, r), 1)).astype(BF16)
    hi, mid, lo = _split3(x)
    return _bdot(hi, eye, ta=True) + (_bdot(mid, eye, ta=True) + _bdot(lo, eye, ta=True))


def _dot22(a, b, ta=False, tb=False):
    ah, al = _split2(a)
    bh, bl = _split2(b)
    return _bdot(ah, bh, ta, tb) + (_bdot(ah, bl, ta, tb) + _bdot(al, bh, ta, tb))


def _chunk_gates(bg, heads):
    n = CHUNK
    row = lax.broadcasted_iota(jnp.int32, (n, n), 0)
    col = lax.broadcasted_iota(jnp.int32, (n, n), 1)
    lane = lax.broadcasted_iota(jnp.int32, bg.shape, 1)
    graw = jnp.where(jnp.logical_and(lane >= heads, lane < 2 * heads), bg, 0.0)
    gc = _dot_mask((row >= col).astype(BF16), graw)
    return gc, _transpose_by_identity(gc)


def _in_lockstep(generators):
    results = [None] * len(generators)
    live = list(enumerate(generators))
    while live:
        still = []
        for i, gen in live:
            try:
                next(gen)
                still.append((i, gen))
            except StopIteration as stop:
                results[i] = stop.value
        live = still
    return results


def _chunk_local(q, k, v, beta, gc, grow, solved=None):
    c = _Chunk()
    n = CHUNK
    row = lax.broadcasted_iota(jnp.int32, (n, n), 0)
    col = lax.broadcasted_iota(jnp.int32, (n, n), 1)
    c.tri = row >= col
    c.strict = row > col
    eye = row == col
    c.gcb = jnp.broadcast_to(gc, (n, HEAD_DIM))
    c.decay = jnp.where(c.tri, jnp.exp(jnp.where(c.tri, gc - grow, 0.0)), 0.0)
    c.eg = jnp.exp(c.gcb)
    glast = c.gcb[n - 1:n, :]
    c.egl = jnp.exp(glast)
    c.ekl = jnp.exp(glast - c.gcb)
    c.beta = beta
    c.q = q * (HEAD_DIM ** -0.5)
    c.k = k
    c.v = v
    c.kb = k * beta
    c.vb = v * beta
    c.kg = c.kb * c.eg
    both = _bdot(jnp.concatenate([c.kb, c.q], axis=0), k, tb=True)
    yield
    c.L = jnp.where(c.strict, both[:n] * c.decay, 0.0)
    c.A = jnp.where(c.tri, both[n:] * c.decay, 0.0)
    if solved is None:
        x = -c.L
        tinv = eye.astype(F32) + x
        p = _dot22(x, x)
        yield
        for _ in range(int(math.log2(n)) - 2):
            both = _dot22(jnp.concatenate([p, tinv], axis=0), p)
            yield
            p, tinv = both[:n], tinv + both[n:]
        c.T = tinv + _dot22(tinv, p)
        yield
        uw = _dot22(c.T, jnp.concatenate([c.vb, c.kg], axis=1))
        yield
        c.u, c.w = uw[:, :HEAD_DIM], uw[:, HEAD_DIM:]
    else:
        c.T, c.u, c.w = solved
    c.qg = c.q * c.eg
    c.kdec = k * c.ekl
    return c


def _gdn_core(qkv, bg, heads):
    t = qkv.shape[0]
    nchunk = t // CHUNK

    gw = heads * HEAD_DIM

    def body(qkv_ref, bg_ref, o_ref, s_ref, t_ref, u_ref, w_ref, state):
        @pl.when(pl.program_id(0) == 0)
        def _():
            state[...] = jnp.zeros_like(state)

        bg_v = bg_ref[...]
        gc_all, gc_rows = _chunk_gates(bg_v, heads)
        def one_head(h):
            col = lambda s: pl.ds(s * gw + h * HEAD_DIM, HEAD_DIM)
            c = yield from _chunk_local(qkv_ref[:, col(0)], qkv_ref[:, col(1)], qkv_ref[:, col(2)], bg_v[:, h:h + 1],
                                        gc_all[:, heads + h:heads + h + 1], gc_rows[heads + h:heads + h + 1, :])
            s0 = state[h]
            v_new = c.u - _bdot(c.w, s0)
            yield
            o = _bdot(c.qg, s0) + _bdot(c.A, v_new)
            return s0, o, s0 * c.egl + _bdot(c.kdec, v_new, ta=True), c

        results = _in_lockstep([one_head(h) for h in range(heads)])
        for h, (s0, o, s1, c) in enumerate(results):
            lanes = pl.ds(h * HEAD_DIM, HEAD_DIM)
            s_ref[h, 0] = s0
            o_ref[:, lanes] = o
            t_ref[:, lanes] = jnp.concatenate([c.T, jnp.zeros((CHUNK, HEAD_DIM - CHUNK), F32)], axis=1)
            u_ref[:, lanes] = c.u
            w_ref[:, lanes] = c.w
            state[h] = s1

    return pl.pallas_call(
        body, name="gdn_core", grid=(nchunk,),
        in_specs=[pl.BlockSpec((CHUNK, 3 * gw), lambda n: (n, 0)), pl.BlockSpec((CHUNK, HEAD_DIM), lambda n: (n, 0))],
        out_specs=[pl.BlockSpec((CHUNK, gw), lambda n: (n, 0)),
                   pl.BlockSpec((heads, 1, HEAD_DIM, HEAD_DIM), lambda n: (0, n, 0, 0))]
        + [pl.BlockSpec((CHUNK, gw), lambda n: (n, 0))] * 3,
        out_shape=[jax.ShapeDtypeStruct((t, gw), F32),
                   jax.ShapeDtypeStruct((heads, nchunk, HEAD_DIM, HEAD_DIM), F32)]
        + [jax.ShapeDtypeStruct((t, gw), F32)] * 3,
        scratch_shapes=[pltpu.VMEM((heads, HEAD_DIM, HEAD_DIM), F32)],
        compiler_params=_params("arbitrary"),
    )(qkv, bg)


def _gdn_core_backward(qkv, bg, states, solved, do, heads):
    t = qkv.shape[0]
    nchunk = t // CHUNK
    n = CHUNK

    def one_head(chunk_local, s0, d_out, ds1):
        c = yield from chunk_local
        v_new = c.u - _bdot(c.w, s0)
        dqg = _bdot(d_out, s0, tb=True)
        ds0 = _bdot(c.qg, d_out, ta=True) + ds1 * c.egl
        dv_new = _bdot(c.A, d_out, ta=True) + _bdot(c.kdec, ds1)
        yield
        dA = jnp.where(c.tri, _bdot(d_out, v_new, tb=True), 0.0)
        dkdec = _bdot(v_new, ds1, tb=True)
        dgl = jnp.sum(jnp.sum(ds1 * s0, axis=1, keepdims=True), axis=0, keepdims=True) * c.egl
        dw = -_bdot(dv_new, s0, tb=True)
        ds0 = ds0 - _bdot(c.w, dv_new, ta=True)
        yield
        both = _dot22(c.T, jnp.concatenate([dv_new, dw], axis=1), ta=True)
        yield
        dvb, dkg = both[:, :HEAD_DIM], both[:, HEAD_DIM:]
        dL = jnp.where(c.strict, -(_bdot(dvb, c.u, tb=True) + _bdot(dkg, c.w, tb=True)), 0.0)
        yield
        dm1 = dL * c.decay
        dkb = _bdot(dm1, c.k) + dkg * c.eg
        dk = _bdot(dm1, c.kb, ta=True)
        dm2 = dA * c.decay
        dq = _bdot(dm2, c.k) + dqg * c.eg
        dk = dk + _bdot(dm2, c.q, ta=True) + dkdec * c.ekl + dkb * c.beta
        pm = dL * c.L + dA * c.A
        ones = jnp.ones((n, HEAD_DIM), BF16)
        pm_hi, pm_lo = _split2(pm)
        colsum = _bdot(pm_hi, ones, ta=True) + _bdot(pm_lo, ones, ta=True)
        tk_ = jnp.sum(dkdec * c.kdec, axis=1, keepdims=True)
        dgc = (jnp.sum(pm, axis=1, keepdims=True) - colsum
               + jnp.sum(dqg * c.qg, axis=1, keepdims=True)
               - tk_
               + jnp.sum(dkg * c.kg, axis=1, keepdims=True))
        dgl = dgl + jnp.sum(tk_, axis=0, keepdims=True)
        rowi = lax.broadcasted_iota(jnp.int32, (n, HEAD_DIM), 0)
        dgc = dgc + jnp.where(rowi == n - 1, dgl, 0.0)
        dbeta = jnp.sum(dkb * c.k, axis=1, keepdims=True) + jnp.sum(dvb * c.v, axis=1, keepdims=True)
        return dq * (HEAD_DIM ** -0.5), dk, dvb * c.beta, dbeta, dgc, ds0

    gw = heads * HEAD_DIM

    def body(qkv_ref, bg_ref, s_ref, t_ref, u_ref, w_ref, do_ref, dqkv_ref, dbg_ref, dstate):
        @pl.when(pl.program_id(0) == 0)
        def _():
            dstate[...] = jnp.zeros_like(dstate)

        bg_v = bg_ref[...]
        gc_all, gc_rows = _chunk_gates(bg_v, heads)
        lane = lax.broadcasted_iota(jnp.int32, (n, HEAD_DIM), 1)
        dgates = jnp.zeros((n, HEAD_DIM), F32)
        chains = []
        for h in range(heads):
            col = lambda s: pl.ds(s * gw + h * HEAD_DIM, HEAD_DIM)
            lanes = pl.ds(h * HEAD_DIM, HEAD_DIM)
            c = _chunk_local(qkv_ref[:, col(0)], qkv_ref[:, col(1)], qkv_ref[:, col(2)], bg_v[:, h:h + 1],
                             gc_all[:, heads + h:heads + h + 1], gc_rows[heads + h:heads + h + 1, :],
                             (t_ref[:, pl.ds(h * HEAD_DIM, CHUNK)], u_ref[:, lanes], w_ref[:, lanes]))
            chains.append(one_head(c, s_ref[h, 0], do_ref[:, pl.ds(h * HEAD_DIM, HEAD_DIM)], dstate[h]))
        results = _in_lockstep(chains)
        for h, (dq, dk, dv, dbeta, dgc, ds0) in enumerate(results):
            dgates = jnp.where(lane == h, dbeta, jnp.where(lane == heads + h, dgc, dgates))
        for h, (dq, dk, dv, dbeta, dgc, ds0) in enumerate(results):
            dqkv_ref[:, pl.ds(h * HEAD_DIM, HEAD_DIM)] = dq
            dqkv_ref[:, pl.ds(gw + h * HEAD_DIM, HEAD_DIM)] = dk
            dqkv_ref[:, pl.ds(2 * gw + h * HEAD_DIM, HEAD_DIM)] = dv
            dstate[h] = ds0
        row = lax.broadcasted_iota(jnp.int32, (n, n), 0)
        colm = lax.broadcasted_iota(jnp.int32, (n, n), 1)
        draw = _dot_mask((row >= colm).astype(BF16), dgates, ta=True)
        dbg_ref[...] = jnp.where(lane < heads, dgates, draw)

    last = nchunk - 1
    return pl.pallas_call(
        body, name="gdn_core_bwd", grid=(nchunk,),
        in_specs=[pl.BlockSpec((CHUNK, 3 * gw), lambda i: (last - i, 0)),
                  pl.BlockSpec((CHUNK, HEAD_DIM), lambda i: (last - i, 0)),
                  pl.BlockSpec((heads, 1, HEAD_DIM, HEAD_DIM), lambda i: (0, last - i, 0, 0))]
        + [pl.BlockSpec((CHUNK, gw), lambda i: (last - i, 0))] * 4,
        out_specs=[pl.BlockSpec((CHUNK, 3 * gw), lambda i: (last - i, 0)),
                   pl.BlockSpec((CHUNK, HEAD_DIM), lambda i: (last - i, 0))],
        out_shape=[jax.ShapeDtypeStruct((t, 3 * gw), F32), jax.ShapeDtypeStruct((t, HEAD_DIM), F32)],
        scratch_shapes=[pltpu.VMEM((heads, HEAD_DIM, HEAD_DIM), F32)],
        compiler_params=_params("arbitrary"),
    )(qkv, bg, states, *solved, do)


def _gdn_post(o, proj, z_col0, norm_w, heads, tt):
    t = o.shape[0]
    zb = z_col0 // HEAD_DIM

    def body(o_ref, z_ref, w_ref, out_ref):
        ov = o_ref[...]
        z = z_ref[...]
        rms = lax.rsqrt(jnp.mean(ov * ov, axis=-1, keepdims=True) + NORM_EPS)
        out_ref[...] = (ov * rms * w_ref[...] * (z * _sigmoid(z))).astype(BF16)

    return pl.pallas_call(
        body, name="gdn_post", grid=(t // tt, heads),
        in_specs=[pl.BlockSpec((tt, HEAD_DIM), lambda i, h: (i, h)),
                  pl.BlockSpec((tt, HEAD_DIM), lambda i, h: (i, zb + h)),
                  pl.BlockSpec((1, HEAD_DIM), lambda i, h: (0, 0))],
        out_specs=pl.BlockSpec((tt, HEAD_DIM), lambda i, h: (i, h)),
        out_shape=jax.ShapeDtypeStruct((t, heads * HEAD_DIM), BF16),
        compiler_params=_params("parallel", "parallel"),
    )(o, proj, norm_w)


def _gdn_post_backward(dcat, o, proj, z_col0, norm_w, heads, tt):
    t = o.shape[0]
    zb = z_col0 // HEAD_DIM

    def body(d_ref, o_ref, z_ref, w_ref, do_ref, dz_ref, dw_ref):
        d = d_ref[...]
        ov = o_ref[...]
        z = z_ref[...]
        w = w_ref[...]
        rms = lax.rsqrt(jnp.mean(ov * ov, axis=-1, keepdims=True) + NORM_EPS)
        ohat = ov * rms
        sg = _sigmoid(z)
        gate = z * sg
        dz_ref[...] = (d * ohat * w * (sg * (1.0 + z * (1.0 - sg)))).astype(BF16)
        don = d * gate
        dohat = don * w
        do_ref[...] = rms * (dohat - ohat * jnp.mean(dohat * ohat, axis=-1, keepdims=True))
        dw = jnp.sum(don * ohat, axis=0, keepdims=True)
        first = jnp.logical_and(pl.program_id(0) == 0, pl.program_id(1) == 0)

        @pl.when(first)
        def _():
            dw_ref[...] = dw

        @pl.when(jnp.logical_not(first))
        def _():
            dw_ref[...] += dw

    blk = pl.BlockSpec((tt, HEAD_DIM), lambda i, h: (i, h))
    return pl.pallas_call(
        body, name="gdn_post_bwd", grid=(t // tt, heads),
        in_specs=[blk, blk, pl.BlockSpec((tt, HEAD_DIM), lambda i, h: (i, zb + h)),
                  pl.BlockSpec((1, HEAD_DIM), lambda i, h: (0, 0))],
        out_specs=[blk, blk, pl.BlockSpec((1, HEAD_DIM), lambda i, h: (0, 0))],
        out_shape=[jax.ShapeDtypeStruct((t, heads * HEAD_DIM), F32),
                   jax.ShapeDtypeStruct((t, heads * HEAD_DIM), BF16),
                   jax.ShapeDtypeStruct((1, HEAD_DIM), F32)],
        compiler_params=_params("arbitrary", "arbitrary"),
    )(dcat, o, proj, norm_w)


def _pool_select(levels, group):
    out = levels[-1]
    for gi in range(len(levels) - 2, -1, -1):
        out = jnp.where(group == gi, levels[gi], out)
    return out


def _pool_counts(t, width, group):
    pos = lax.broadcasted_iota(jnp.int32, (t, width), 0)
    win = jnp.left_shift(2, group)
    return jnp.minimum(pos + 1, win).astype(F32)


def _pooled(p, group):
    levels, s, step = [], p, 1
    for _ in POOL_WINDOWS:
        s = s + _shift_down(s, step)
        levels.append(s)
        step *= 2
    cnt = _pool_counts(p.shape[0], p.shape[1], group)
    return _pool_select(levels, group) / cnt - p, cnt


def _pool_forward(proj, p_col0, pool_w, pool_scale):
    t = proj.shape[0]
    groups, cg, _ = pool_w.shape
    pb = p_col0 // cg

    def body(p_ref, w_ref, s_ref, o_ref):
        pooled, _ = _pooled(p_ref[...], pl.program_id(0))
        o_ref[...] = (_bdot(pooled, w_ref[0]) * s_ref[...]).astype(BF16)

    return pl.pallas_call(
        body, name="pool_fwd", grid=(groups,),
        in_specs=[pl.BlockSpec((t, cg), lambda g: (0, pb + g)), pl.BlockSpec((1, cg, cg), lambda g: (g, 0, 0)),
                  pl.BlockSpec((1, cg), lambda g: (0, g))],
        out_specs=pl.BlockSpec((t, cg), lambda g: (0, g)),
        out_shape=jax.ShapeDtypeStruct((t, groups * cg), BF16),
        compiler_params=_params("parallel"),
    )(proj, pool_w, pool_scale)


def _pool_backward(dcat, d_col0, proj, p_col0, pool_w, pool_scale):
    t = proj.shape[0]
    groups, cg, _ = pool_w.shape
    pb = p_col0 // cg
    db = d_col0 // cg

    def body(d_ref, p_ref, w_ref, s_ref, dp_ref, dw_ref, ds_ref):
        group = pl.program_id(0)
        pooled, cnt = _pooled(p_ref[...], group)
        w = w_ref[0]
        d = d_ref[...]
        mixed = _bdot(pooled, w)
        ds_ref[...] = jnp.sum(d * mixed, axis=0, keepdims=True)
        dmixed = d * s_ref[...]
        dw_ref[0] = _bdot(pooled, dmixed, ta=True)
        dpooled = _bdot(dmixed, w, tb=True)
        levels, s, step = [], dpooled / cnt, 1
        for _ in POOL_WINDOWS:
            s = s + _shift_up(s, step)
            levels.append(s)
            step *= 2
        dp_ref[...] = (_pool_select(levels, group) - dpooled).astype(BF16)

    return pl.pallas_call(
        body, name="pool_bwd", grid=(groups,),
        in_specs=[pl.BlockSpec((t, cg), lambda g: (0, db + g)), pl.BlockSpec((t, cg), lambda g: (0, pb + g)),
                  pl.BlockSpec((1, cg, cg), lambda g: (g, 0, 0)), pl.BlockSpec((1, cg), lambda g: (0, g))],
        out_specs=[pl.BlockSpec((t, cg), lambda g: (0, g)), pl.BlockSpec((1, cg, cg), lambda g: (g, 0, 0)),
                   pl.BlockSpec((1, cg), lambda g: (0, g))],
        out_shape=[jax.ShapeDtypeStruct((t, groups * cg), BF16), jax.ShapeDtypeStruct((groups, cg, cg), F32),
                   jax.ShapeDtypeStruct((1, groups * cg), F32)],
        compiler_params=_params("parallel"),
    )(dcat, proj, pool_w, pool_scale)


def _attention(q, k, v, tq):
    t, d = q.shape
    m = k.shape[0]
    dh = d // XATTN_HEADS
    scale = dh ** -0.5

    def body(q_ref, k_ref, v_ref, o_ref):
        s = _bdot(q_ref[...], k_ref[...], tb=True) * scale
        s = s - jnp.max(s, axis=-1, keepdims=True)
        e = jnp.exp(s)
        p = e / jnp.sum(e, axis=-1, keepdims=True)
        o_ref[...] = _bdot(p, v_ref[...]).astype(BF16)

    return pl.pallas_call(
        body, name="xattn_fwd", grid=(XATTN_HEADS, t // tq),
        in_specs=[pl.BlockSpec((tq, dh), lambda h, i: (i, h)), pl.BlockSpec((m, dh), lambda h, i: (0, h)),
                  pl.BlockSpec((m, dh), lambda h, i: (0, h))],
        out_specs=pl.BlockSpec((tq, dh), lambda h, i: (i, h)),
        out_shape=jax.ShapeDtypeStruct((t, d), BF16),
        compiler_params=_params("parallel", "parallel"),
    )(q, k, v)


def _attention_backward(q, k, v, do, tq):
    t, d = q.shape
    m = k.shape[0]
    dh = d // XATTN_HEADS
    scale = dh ** -0.5

    def body(q_ref, k_ref, v_ref, do_ref, dq_ref, dk_ref, dv_ref, dk_acc, dv_acc):
        i = pl.program_id(1)
        qv, kv, vv, dov = q_ref[...], k_ref[...], v_ref[...], do_ref[...]
        s = _bdot(qv, kv, tb=True) * scale
        s = s - jnp.max(s, axis=-1, keepdims=True)
        e = jnp.exp(s)
        p = e / jnp.sum(e, axis=-1, keepdims=True)
        dp = _bdot(dov, vv, tb=True)
        ds = p * (dp - jnp.sum(dp * p, axis=-1, keepdims=True)) * scale
        dq_ref[...] = _bdot(ds, kv).astype(BF16)
        dv_part = _bdot(p, dov, ta=True)
        dk_part = _bdot(ds, qv, ta=True)

        @pl.when(i == 0)
        def _():
            dk_acc[...] = dk_part
            dv_acc[...] = dv_part

        @pl.when(i > 0)
        def _():
            dk_acc[...] += dk_part
            dv_acc[...] += dv_part

        @pl.when(i == pl.num_programs(1) - 1)
        def _():
            dk_ref[...] = dk_acc[...].astype(BF16)
            dv_ref[...] = dv_acc[...].astype(BF16)

    qblk = pl.BlockSpec((tq, dh), lambda h, i: (i, h))
    kblk = pl.BlockSpec((m, dh), lambda h, i: (0, h))
    return pl.pallas_call(
        body, name="xattn_bwd", grid=(XATTN_HEADS, t // tq),
        in_specs=[qblk, kblk, kblk, qblk],
        out_specs=[qblk, kblk, kblk],
        out_shape=[jax.ShapeDtypeStruct((t, d), BF16), jax.ShapeDtypeStruct((m, d), BF16),
                   jax.ShapeDtypeStruct((m, d), BF16)],
        scratch_shapes=[pltpu.VMEM((m, dh), F32), pltpu.VMEM((m, dh), F32)],
        compiler_params=_params("parallel", "arbitrary"),
    )(q, k, v, do)


def _ln_backward_rows(name, dmain, dres, xhat, rstd, gamma, tm):
    t, d = xhat.shape

    def body(m_ref, r_ref, x_ref, s_ref, g_ref, du_ref, dub_ref, dg_ref, db_ref):
        du, dg, db = _ln_backward_math(m_ref[...] + ALPHA * r_ref[...], x_ref[...], s_ref[...], g_ref[...])
        du_ref[...] = du
        dub_ref[...] = du.astype(BF16)
        first = pl.program_id(0) == 0

        @pl.when(first)
        def _():
            dg_ref[...] = dg
            db_ref[...] = db

        @pl.when(jnp.logical_not(first))
        def _():
            dg_ref[...] += dg
            db_ref[...] += db

    row = pl.BlockSpec((tm, d), lambda i: (i, 0))
    vec = pl.BlockSpec((1, d), lambda i: (0, 0))
    return pl.pallas_call(
        body, name=name, grid=(t // tm,),
        in_specs=[row, row, row, pl.BlockSpec((tm, 1), lambda i: (i, 0)), vec],
        out_specs=[row, row, vec, vec],
        out_shape=[jax.ShapeDtypeStruct((t, d), F32), jax.ShapeDtypeStruct((t, d), BF16),
                   jax.ShapeDtypeStruct((1, d), F32), jax.ShapeDtypeStruct((1, d), F32)],
        compiler_params=_params("arbitrary"),
    )(dmain, dres, xhat, rstd, gamma)


def _loss_and_ln_backward(xhat, rstd, gamma, beta, target, tm):
    t, d = xhat.shape

    def body(x_ref, r_ref, g_ref, b_ref, t_ref, du_ref, dub_ref, dg_ref, db_ref, loss_ref):
        xh = x_ref[...]
        g = g_ref[...]
        diff = xh * g + b_ref[...] - t_ref[...]
        part = jnp.sum(jnp.sum(diff * diff, axis=1, keepdims=True), axis=0, keepdims=True) * (0.5 / d)
        dy = diff * (1.0 / d)
        du, dg, db = _ln_backward_math(dy, xh, r_ref[...], g)
        du_ref[...] = du
        dub_ref[...] = du.astype(BF16)
        lossrow = jnp.broadcast_to(part, (1, HEAD_DIM))
        first = pl.program_id(0) == 0

        @pl.when(first)
        def _():
            dg_ref[...] = dg
            db_ref[...] = db
            loss_ref[...] = lossrow

        @pl.when(jnp.logical_not(first))
        def _():
            dg_ref[...] += dg
            db_ref[...] += db
            loss_ref[...] += lossrow

    row = pl.BlockSpec((tm, d), lambda i: (i, 0))
    vec = pl.BlockSpec((1, d), lambda i: (0, 0))
    return pl.pallas_call(
        body, name="loss_ln3_bwd", grid=(t // tm,),
        in_specs=[row, pl.BlockSpec((tm, 1), lambda i: (i, 0)), vec, vec, row],
        out_specs=[row, row, vec, vec, pl.BlockSpec((1, HEAD_DIM), lambda i: (0, 0))],
        out_shape=[jax.ShapeDtypeStruct((t, d), F32), jax.ShapeDtypeStruct((t, d), BF16),
                   jax.ShapeDtypeStruct((1, d), F32), jax.ShapeDtypeStruct((1, d), F32),
                   jax.ShapeDtypeStruct((1, HEAD_DIM), F32)],
        compiler_params=_params("arbitrary"),
    )(xhat, rstd, gamma, beta, target)


def _after(token, a):
    return a if token is None else a + token[:1, :1].astype(a.dtype)


def _pick(n, prefs):
    for p in prefs:
        if n % p == 0:
            return p
    return n


def _local_step(x, mem, target, w, x_bf=None):
    t, d = x.shape
    heads = w["a_log"].shape[1]
    gw = heads * HEAD_DIM
    groups, cg, _ = w["pool_w"].shape
    pw = groups * cg
    n_main = 4 * gw + pw
    in_cols = n_main + 2 * heads
    s_in = w["w_in_t"].shape[0]

    tm = _pick(t, (512, 256, 128))
    tm_ln = _pick(t, (256, 128))
    tm_big = _pick(t, (1024, 512, 256, 128))
    tk = _pick(d, K_STEPS)

    w_in_t = w["w_in_t"].reshape(in_cols, d)
    w_p_t = w_in_t[4 * gw + 2 * heads:]
    w_ba_t = jnp.pad(w_in_t[4 * gw:4 * gw + 2 * heads], ((0, HEAD_DIM - 2 * heads), (0, 0)))
    x_bf = x.astype(BF16) if x_bf is None else x_bf
    mem_bf = mem.astype(BF16)

    tn_d = _pick(d, (1024, 512, 256, 128))
    proj = _plain("proj_main", x_bf, w_in_t, tb=True, n_used=4 * gw, tm=tm_big, tn=_pick(4 * gw, (1024, 512, 256, 128)),
                  tk=tk, out_dtype=F32)
    pproj = _plain("proj_pool", x_bf, w_p_t, tb=True, tm=tm_big, tn=_pick(pw, (1024, 512, 256, 128)), tk=tk, out_dtype=F32)
    ea, dtb = _gate_vectors(w["a_log"], w["dt_bias"], heads)
    vec128 = lambda i, j: (0, 0)
    ba, bg = _matmul(
        "proj_gates", x_bf, w_ba_t, tb=True, tm=tm, tn=HEAD_DIM, tk=tk,
        extra=[(ea, (1, HEAD_DIM), vec128), (dtb, (1, HEAD_DIM), vec128)],
        outs=[(jax.ShapeDtypeStruct((t, HEAD_DIM), F32), (tm, HEAD_DIM), _tile)] * 2,
        epilogue=_gates_epilogue(heads))
    qkv = _gdn_pre(proj, w["conv_w"], heads)
    o_gdn, states, *solved = _gdn_core(qkv, bg, heads)
    cat_g = _gdn_post(o_gdn, proj, 3 * gw, w["gdn_norm_w"], heads, tm_big)
    token = yield ("pass", 1, cat_g)
    cat_p = _pool_forward(pproj, 0, w["pool_w"], _after(token, w["pool_scale"]))
    cat = jnp.concatenate([cat_g, cat_p], axis=1)
    w = {**w, **(yield ("weights", 1, cat))}
    h1, h1_bf, xhat1, rstd1 = _ln_forward("mix_ln1", cat, w["w_out"], x, w["ln1_g"], w["ln1_b"], tm=tm_ln, tk=tk)

    h1_bf = _after((yield ("relay", None, h1_bf)), h1_bf)
    q = _plain("xattn_q", h1_bf, w["xq_w"], tm=tm, tn=tn_d, tk=tk, out_dtype=BF16)
    mlen = mem.shape[0]
    tm_mem = _pick(mlen, (256, 128))
    k = _plain("xattn_k", mem_bf, w["xk_w"], tm=tm_mem, tn=tn_d, tk=tk, out_dtype=BF16)
    v = _plain("xattn_v", mem_bf, w["xv_w"], tm=tm_mem, tn=tn_d, tk=tk, out_dtype=BF16)
    att = _attention(q, k, v, tm)
    h2, h2_bf, xhat2, rstd2 = _ln_forward("xo_ln2", att, w["xo_w"], h1, w["ln2_g"], w["ln2_b"], tm=tm_ln, tk=tk)

    w = {**w, **(yield ("weights", 2, h2_bf))}
    s_up = w["w_up3"].shape[0]
    ff = s_up * w["w_up3"].shape[2]
    tn_f = _pick(ff // s_up, (1024, 512, 256, 128))

    def up_epi(acc, ex, out, i):
        r = jnp.maximum(acc, 0.0)
        out[0][...] = (r * r).astype(BF16)
        out[1][...] = (2.0 * r).astype(BF16)

    act, act_grad = _matmul(
        "mlp_up", h2_bf, w["w_up3"], b_blocks=s_up, tm=tm_big, tn=tn_f, tk=tk,
        outs=[(jax.ShapeDtypeStruct((t, ff), BF16), (tm_big, tn_f), _tile)] * 2, epilogue=up_epi)
    w = {**w, **(yield ("weights", 3, act))}
    tk_f = _pick(ff, K_STEPS)
    xhat3, rstd3 = _ln_forward("down_ln3", act, w["w_down"], h2, w["ln3_g"], w["ln3_b"], tm=tm, tk=tk_f, want_h=False)

    grads = {}
    du3, du3_bf, grads["ln3_g"], grads["ln3_b"], loss = _loss_and_ln_backward(
        xhat3, rstd3, w["ln3_g"], w["ln3_b"], target, tm)

    def dup_epi(acc, ex, out, i):
        out[0][...] = (acc * ex[0][...].astype(F32)).astype(BF16)

    dup = _matmul(
        "mlp_down_dx", du3_bf, w["w_down"], tb=True, tm=tm_big, tn=tn_f, tk=tk,
        extra=[(act_grad, (tm_big, tn_f), _tile)],
        outs=[(jax.ShapeDtypeStruct((t, ff), BF16), (tm_big, tn_f), _tile)], epilogue=dup_epi)[0]
    tk_t = _pick(t, K_STEPS)
    tm_w = _pick(d, (512, 256, 128))
    grads["w_down"] = _plain("mlp_down_dw", act, du3_bf, ta=True, tm=_pick(ff, (512, 256, 128)), tn=d, tk=tk_t,
                             out_dtype=F32)
    grads["w_up3"] = _plain("mlp_up_dw", h2_bf, dup, ta=True, tm=tm_w, tn=ff // s_up, tk=tk_t, out_dtype=F32, out3=s_up,
                            n_outer=True)
    token = yield ("grads", 0, {n: grads.pop(n) for n in ("w_down", "w_up3")})
    dh2 = _plain("mlp_up_dx", dup, w["w_up3"], tb=True, b_blocks=s_up, tm=tm_big, tn=tn_d,
                 tk=_pick(ff // s_up, K_STEPS), out_dtype=F32)
    du2, du2_bf, grads["ln2_g"], grads["ln2_b"] = _ln_backward_rows(
        "ln2_bwd", dh2, du3, xhat2, rstd2, _after(token, w["ln2_g"]), tm)
    token = yield ("poll", 0, du2_bf)

    grads["xo_w"] = _plain("xo_dw", att, du2_bf, ta=True, tm=tm_w, tn=d, tk=tk_t, out_dtype=F32)
    datt = _plain("xo_dx", du2_bf, w["xo_w"], tb=True, tm=tm, tn=tn_d, tk=tk, out_dtype=BF16)
    dq, dk, dv = _attention_backward(q, k, v, datt, tm)
    tk_m = _pick(mlen, (256, 128))
    grads["xq_w"] = _plain("xq_dw", h1_bf, dq, ta=True, tm=tm_w, tn=d, tk=tk_t, out_dtype=F32)
    grads["xk_w"] = _plain("xk_dw", mem_bf, dk, ta=True, tm=tm_w, tn=tn_d, tk=tk_m, out_dtype=F32)
    grads["xv_w"] = _plain("xv_dw", mem_bf, dv, ta=True, tm=tm_w, tn=tn_d, tk=tk_m, out_dtype=F32)
    du1, du1_bf, grads["ln1_g"], grads["ln1_b"] = _ln_backward(
        "xq_dx_ln1", dq, w["xq_w"], du2, xhat1, rstd1, _after(token, w["ln1_g"]), tm=tm_ln, tk=tk)

    grads["w_out"] = _plain("out_dw", cat, du1_bf, ta=True, tm=tm_w, tn=d, tk=tk_t, out_dtype=F32)
    token = yield ("grads", 1, {n: grads.pop(n) for n in ("xo_w", "xq_w", "xk_w", "xv_w", "w_out")})
    dcat = _plain("out_dx", du1_bf, w["w_out"], tb=True, tm=tm, tn=tn_d, tk=tk, out_dtype=F32)
    dp, grads["pool_w"], grads["pool_scale"] = _pool_backward(dcat, gw, pproj, 0, w["pool_w"],
                                                              _after(token, w["pool_scale"]))
    do_gdn, dz, grads["gdn_norm_w"] = _gdn_post_backward(dcat, o_gdn, proj, 3 * gw, _after(token, w["gdn_norm_w"]),
                                                         heads, tm_big)
    dqkv, dbg = _gdn_core_backward(qkv, bg, states, solved, do_gdn, heads)
    token = yield ("poll", 1, dqkv)
    dqkv_pre, grads["conv_w"] = _gdn_pre_backward(proj, _after(token, w["conv_w"]), dqkv, heads)
    dba, dalog_row, ddt_row = _gates_backward(ba, bg, dbg, ea, dtb, heads)
    grads["a_log"] = dalog_row[:, heads:2 * heads]
    grads["dt_bias"] = ddt_row[:, heads:2 * heads]

    k_pad = -(-in_cols // (4 * HEAD_DIM)) * (4 * HEAD_DIM)
    dproj = jnp.concatenate([dqkv_pre, dz, dba[:, :2 * heads], dp, jnp.zeros((t, k_pad - in_cols), BF16)], axis=1)
    dw_in_t = _plain("proj_dw", dproj, x_bf, ta=True, tm=_pick(k_pad, (512, 256, 128)), tn=d, tk=tk_t, out_dtype=F32,
                     m_kept=in_cols)
    grads["w_in_t"] = dw_in_t.reshape(s_in, in_cols // s_in, d)

    def dx_epi(acc, ex, out, i):
        out[0][...] = acc + ALPHA * ex[0][...]

    token = yield ("grads", 2, {n: grads.pop(n) for n in ("w_in_t", "pool_w")})
    w_in_t_pad = jnp.concatenate([w_in_t, _after(token, jnp.zeros((k_pad - in_cols, d), BF16))], axis=0)
    grad_x = _matmul(
        "proj_dx", dproj, w_in_t_pad, tm=tm, tn=tn_d, tk=k_pad, extra=[(du1, (tm, tn_d), _tile)],
        outs=[(jax.ShapeDtypeStruct((t, d), F32), (tm, tn_d), _tile)], epilogue=dx_epi)[0]
    yield ("poll", 2, grad_x)
    return loss, grad_x, grads


def _adamw(name, w, g, m, v):
    r, c = w.shape
    if r % 8 == 0:
        tr = _pick(r, (256, 128, 64, 32, 16, 8))
        blk, steps = pl.BlockSpec((tr, c), lambda i: (i, 0)), r // tr
    else:
        tc = _pick(c, (256, 128))
        blk, steps = pl.BlockSpec((r, tc), lambda i: (0, i)), c // tc
    c1 = 1.0 - ADAM_B1 ** ADAM_STEP
    c2 = 1.0 - ADAM_B2 ** ADAM_STEP

    def body(w_ref, g_ref, m_ref, v_ref, d_ref, mo_ref, vo_ref, go_ref):
        gv = g_ref[...]
        mn = ADAM_B1 * m_ref[...] + (1.0 - ADAM_B1) * gv
        vn = ADAM_B2 * v_ref[...] + (1.0 - ADAM_B2) * (gv * gv)
        d_ref[...] = -ADAM_LR * ((mn / c1) / (jnp.sqrt(vn / c2) + ADAM_EPS) + ADAM_WD * w_ref[...])
        mo_ref[...] = mn
        vo_ref[...] = vn
        go_ref[...] = gv

    return pl.pallas_call(
        body, name=name, grid=(steps,), in_specs=[blk] * 4, out_specs=[blk] * 4,
        out_shape=[jax.ShapeDtypeStruct((r, c), F32)] * 4,
        compiler_params=_params("parallel"),
    )(w, g, m, v)


def _place():
    x, y, c = lax.axis_index("x"), lax.axis_index("y"), lax.axis_index("c")
    chips = [(1 - x, y), (x, 1 - y), (1 - x, 1 - y)]
    return x, y, c, chips


HBM = pl.BlockSpec(memory_space=pltpu.HBM)


SEM = pl.BlockSpec(memory_space=pltpu.SEMAPHORE)
ANY = pl.BlockSpec(memory_space=pl.ANY)
EFFECT = pltpu.SideEffectType.DATAFLOW_SIDE_EFFECTING


def _in_hbm(a):
    return pltpu.with_memory_space_constraint(a, pltpu.HBM)


def _remote(src, dst, send_sem, recv_sem, to):
    return pltpu.make_async_remote_copy(src_ref=src, dst_ref=dst, send_sem=send_sem, recv_sem=recv_sem,
                                        device_id=to, device_id_type=MESH)


def _by_rows(rows):
    return rows % 32 == 0


def _half_shape(rows, cols):
    return (rows // 2, cols) if _by_rows(rows) else (rows, cols // 2)


def _half(ref, which, *lead):
    rows, cols = ref.shape[-2:]
    if _by_rows(rows):
        return ref.at[(*lead, pl.ds(which * (rows // 2), rows // 2))]
    return ref.at[(*lead, slice(None), pl.ds(which * (cols // 2), cols // 2))]


def _landed(lands, i, shard_index, which):
    return _half(lands[i], which, shard_index)


def _routes():
    x, y, c, _ = _place()
    first = (jnp.where(c == 0, 1 - x, x), jnp.where(c == 0, y, 1 - y))
    second = (jnp.where(c == 0, x, 1 - x), jnp.where(c == 0, 1 - y, y))
    return first, second, (1 - x, 1 - y)


def _shard_of(chip):
    return 2 * chip[0] + chip[1]


def _gather_start(name, shards, after, relayed=()):
    n = len(shards)
    lands = [lax.empty((N_SHARD,) + s.shape, s.dtype) for s in shards]

    def body(*refs):
        ins, zones = refs[:n], refs[n:2 * n]
        ici_send, ici_recv, own_send, own_recv = refs[2 * n + 1:2 * n + 5]
        token = refs[-1]
        x, y, c, chips = _place()
        me = 2 * x + y
        first, _, _ = _routes()
        for i in range(n):
            if i in relayed:
                _remote(_half(ins[i], c), _landed(zones, i, me, c), ici_send.at[3 * i], ici_recv.at[3 * i],
                        (*first, c)).start()
                continue
            for j, chip in enumerate(chips):
                _remote(_half(ins[i], c), _landed(zones, i, me, c), ici_send.at[3 * i + j],
                        ici_recv.at[3 * i + j], (*chip, c)).start()
        for i in range(n):
            _remote(ins[i], zones[i].at[me], own_send.at[i], own_recv.at[i], (x, y, 1 - c)).start()
        token[...] = jnp.zeros_like(token)

    dma = pltpu.SemaphoreType.DMA
    outs = pl.pallas_call(
        body, name=name,
        in_specs=[HBM] * (2 * n) + [ANY],
        out_shape=(dma((3 * n,)), dma((3 * n,)), dma((n,)), dma((n,)),
                   *[pltpu.HBM(a.shape, a.dtype) for a in shards + lands], jax.ShapeDtypeStruct((8, LANES), F32)),
        out_specs=(SEM, SEM, SEM, SEM, *[HBM] * (2 * n), pl.BlockSpec(memory_space=pltpu.VMEM)),
        input_output_aliases={k: 4 + k for k in range(2 * n)},
        compiler_params=pltpu.CompilerParams(has_side_effects=EFFECT),
    )(*[_in_hbm(a) for a in shards + lands], after)
    sems = dict(zip(("ici_send", "ici_recv", "own_send", "own_recv"), outs[:4]))
    return sems, list(outs[4:4 + n]), list(outs[4 + n:4 + 2 * n]), outs[-1]


def _gather_forward(name, idx, lands, sems, after):
    n = len(idx)

    def body(*refs):
        zones = refs[:n]
        ici_recv = refs[n]
        fwd_send, fwd_recv = refs[n + 2], refs[n + 3]
        x, y, c, chips = _place()
        for k, i in enumerate(idx):
            for j, chip in enumerate(chips):
                half = _landed(zones, k, 2 * chip[0] + chip[1], c)
                _remote(half, half, fwd_send.at[3 * k + j], ici_recv.at[3 * i + j], (*chip, c)).wait_recv()
                _remote(half, half, fwd_send.at[3 * k + j], fwd_recv.at[3 * k + j], (x, y, 1 - c)).start()
        refs[-1][...] = jnp.zeros_like(refs[-1])

    dma = pltpu.SemaphoreType.DMA
    outs = pl.pallas_call(
        body, name=name,
        in_specs=[HBM] * n + [SEM, ANY],
        out_shape=(dma((3 * n,)), dma((3 * n,)), *[pltpu.HBM(a.shape, a.dtype) for a in lands],
                   jax.ShapeDtypeStruct((8, LANES), F32)),
        out_specs=(SEM, SEM, *[HBM] * n, pl.BlockSpec(memory_space=pltpu.VMEM)),
        input_output_aliases={k: 2 + k for k in range(n)},
        compiler_params=pltpu.CompilerParams(has_side_effects=EFFECT),
    )(*lands, sems["ici_recv"], after)
    return (outs[0], outs[1]), list(outs[2:2 + n]), outs[-1]


def _gather_wait(name, idx, shards, lands, sems, fwd, after):
    n = len(idx)

    def body(*refs):
        ins, zones = refs[:n], refs[n:2 * n]
        ici_send, own_send, own_recv, fwd_send, fwd_recv = refs[2 * n:2 * n + 5]
        x, y, c, chips = _place()
        me = 2 * x + y
        for k, i in enumerate(idx):
            mine = _half(ins[k], c)
            for j, chip in enumerate(chips):
                theirs = 2 * chip[0] + chip[1]
                _remote(mine, _landed(zones, k, me, c), ici_send.at[3 * i + j], fwd_recv.at[3 * k + j],
                        (*chip, c)).wait_send()
                sent = _landed(zones, k, theirs, c)
                _remote(sent, sent, fwd_send.at[3 * k + j], fwd_recv.at[3 * k + j], (x, y, 1 - c)).wait_send()
                passed = _landed(zones, k, theirs, 1 - c)
                _remote(passed, passed, fwd_send.at[3 * k + j], fwd_recv.at[3 * k + j], (x, y, 1 - c)).wait_recv()
            own = _remote(ins[k], zones[k].at[me], own_send.at[i], own_recv.at[i], (x, y, 1 - c))
            own.wait_send()
            own.wait_recv()

    outs = pl.pallas_call(
        body, name=name,
        in_specs=[HBM] * (2 * n) + [SEM] * 5 + [ANY],
        out_shape=tuple(pltpu.HBM(a.shape, a.dtype) for a in lands),
        out_specs=tuple([HBM] * n),
        input_output_aliases={n + k: k for k in range(n)},
        compiler_params=pltpu.CompilerParams(has_side_effects=EFFECT),
    )(*shards, *lands, sems["ici_send"], sems["own_send"], sems["own_recv"], fwd[0], fwd[1], after)
    return list(outs)


def _gather_relay(name, idx, shards, lands, sems, after):
    n = len(idx)

    def body(*refs):
        ins, zones, ici_recv = refs[:n], refs[n:2 * n], refs[2 * n]
        relay_send, relay_recv, pass_send, pass_recv = refs[2 * n + 2:2 * n + 6]
        x, y, c, _ = _place()
        first, second, _ = _routes()
        for k, i in enumerate(idx):
            landed = _landed(zones, k, _shard_of(first), c)
            _remote(landed, landed, pass_send.at[k], ici_recv.at[3 * i], (*first, c)).wait_recv()
            _remote(_half(ins[k], c), _landed(zones, k, 2 * x + y, c), relay_send.at[2 * k], relay_recv.at[2 * k],
                    (*second, c)).start()
            _remote(landed, landed, relay_send.at[2 * k + 1], relay_recv.at[2 * k + 1], (*second, c)).start()
            _remote(landed, landed, pass_send.at[k], pass_recv.at[k], (x, y, 1 - c)).start()
        refs[-1][...] = jnp.zeros_like(refs[-1])

    dma = pltpu.SemaphoreType.DMA
    outs = pl.pallas_call(
        body, name=name,
        in_specs=[HBM] * (2 * n) + [SEM, ANY],
        out_shape=(dma((2 * n,)), dma((2 * n,)), dma((n,)), dma((n,)), *[pltpu.HBM(a.shape, a.dtype) for a in lands],
                   jax.ShapeDtypeStruct((8, LANES), F32)),
        out_specs=(SEM, SEM, SEM, SEM, *[HBM] * n, pl.BlockSpec(memory_space=pltpu.VMEM)),
        input_output_aliases={n + k: 4 + k for k in range(n)},
        compiler_params=pltpu.CompilerParams(has_side_effects=EFFECT),
    )(*shards, *lands, sems["ici_recv"], after)
    return outs[:4], list(outs[4:4 + n]), outs[-1]


def _gather_forward_relayed(name, ks, lands, relay, after):
    n = len(ks)

    def body(*refs):
        zones, relay_recv = refs[:n], refs[n]
        fwd_send, fwd_recv = refs[n + 2], refs[n + 3]
        x, y, c, _ = _place()
        _, second, diagonal = _routes()
        for p, k in enumerate(ks):
            for j, chip in enumerate((second, diagonal)):
                landed = _landed(zones, p, _shard_of(chip), c)
                _remote(landed, landed, fwd_send.at[2 * p + j], relay_recv.at[2 * k + j], (*second, c)).wait_recv()
                _remote(landed, landed, fwd_send.at[2 * p + j], fwd_recv.at[2 * p + j], (x, y, 1 - c)).start()

    dma = pltpu.SemaphoreType.DMA
    outs = pl.pallas_call(
        body, name=name,
        in_specs=[HBM] * n + [SEM, ANY],
        out_shape=(dma((2 * n,)), dma((2 * n,)), *[pltpu.HBM(a.shape, a.dtype) for a in lands]),
        out_specs=(SEM, SEM, *[HBM] * n),
        input_output_aliases={k: 2 + k for k in range(n)},
        compiler_params=pltpu.CompilerParams(has_side_effects=EFFECT),
    )(*lands, relay[1], after)
    return (outs[0], outs[1]), list(outs[2:])


def _gather_wait_relayed(name, idx, ks, shards, lands, sems, relay, fwd, after):
    n = len(idx)

    def body(*refs):
        ins, zones = refs[:n], refs[n:2 * n]
        ici_send, own_send, own_recv, relay_send, pass_send, pass_recv, fwd_send, fwd_recv = refs[2 * n:2 * n + 8]
        x, y, c, _ = _place()
        me = 2 * x + y
        sibling = (x, y, 1 - c)
        first, second, diagonal = _routes()
        for p, (i, k) in enumerate(zip(idx, ks)):
            mine, at_peer = _half(ins[p], c), _landed(zones, p, me, c)
            from_first = _landed(zones, p, _shard_of(first), c)
            _remote(mine, at_peer, ici_send.at[3 * i], pass_recv.at[k], (*first, c)).wait_send()
            _remote(mine, at_peer, relay_send.at[2 * k], pass_recv.at[k], (*second, c)).wait_send()
            _remote(from_first, from_first, relay_send.at[2 * k + 1], pass_recv.at[k], (*second, c)).wait_send()
            _remote(from_first, from_first, pass_send.at[k], pass_recv.at[k], sibling).wait_send()
            theirs = _landed(zones, p, _shard_of(second), 1 - c)
            _remote(theirs, theirs, pass_send.at[k], pass_recv.at[k], sibling).wait_recv()
            for j, (sent, got) in enumerate(((second, first), (diagonal, diagonal))):
                out_half = _landed(zones, p, _shard_of(sent), c)
                _remote(out_half, out_half, fwd_send.at[2 * p + j], fwd_recv.at[2 * p + j], sibling).wait_send()
                in_half = _landed(zones, p, _shard_of(got), 1 - c)
                _remote(in_half, in_half, fwd_send.at[2 * p + j], fwd_recv.at[2 * p + j], sibling).wait_recv()
            own = _remote(ins[p], zones[p].at[me], own_send.at[i], own_recv.at[i], sibling)
            own.wait_send()
            own.wait_recv()

    outs = pl.pallas_call(
        body, name=name,
        in_specs=[HBM] * (2 * n) + [SEM] * 8 + [ANY],
        out_shape=tuple(pltpu.HBM(a.shape, a.dtype) for a in lands),
        out_specs=tuple([HBM] * n),
        input_output_aliases={n + k: k for k in range(n)},
        compiler_params=pltpu.CompilerParams(has_side_effects=EFFECT),
    )(*shards, *lands, sems["ici_send"], sems["own_send"], sems["own_recv"], relay[0], relay[2], relay[3],
      fwd[0], fwd[1], after)
    return list(outs)


def _all_reduce_small(name, slab, after=None):
    r, width = slab.shape
    ndev = 8

    def body(x_ref, after_ref, out_ref, buf, send_sems, recv_sems):
        x, y, c, _ = _place()
        me = 4 * x + 2 * y + c
        buf[me] = x_ref[...]
        copies = []
        for k in range(1, ndev):
            peer = jnp.bitwise_xor(me, k)
            to = (peer // 4, (peer // 2) % 2, peer % 2)
            cp = pltpu.make_async_remote_copy(src_ref=x_ref, dst_ref=buf.at[me], send_sem=send_sems.at[k - 1],
                                              recv_sem=recv_sems.at[k - 1], device_id=to, device_id_type=MESH)
            cp.start()
            copies.append(cp)
        for k in range(1, ndev):
            peer = jnp.bitwise_xor(me, k)
            pltpu.make_async_remote_copy(src_ref=x_ref, dst_ref=buf.at[peer], send_sem=send_sems.at[k - 1],
                                         recv_sem=recv_sems.at[k - 1], device_id=(x, y, c),
                                         device_id_type=MESH).wait_recv()
        for cp in copies:
            cp.wait_send()
        total = buf[0]
        for d in range(1, ndev):
            total = total + buf[d]
        out_ref[...] = total

    return pl.pallas_call(
        body, name=name,
        in_specs=[pl.BlockSpec(memory_space=pltpu.VMEM), ANY], out_specs=pl.BlockSpec(memory_space=pltpu.VMEM),
        out_shape=jax.ShapeDtypeStruct((r, width), F32),
        scratch_shapes=[pltpu.VMEM((ndev, r, width), F32), pltpu.SemaphoreType.DMA((ndev - 1,)),
                        pltpu.SemaphoreType.DMA((ndev - 1,))],
        compiler_params=pltpu.CompilerParams(vmem_limit_bytes=VMEM_LIMIT),
    )(slab, slab if after is None else after)


def _half_tiling(rows, cols):
    if _by_rows(rows):
        tr = _pick(rows // 2, (256, 128, 64, 32, 16))
        nb = (rows // 2) // tr
        return (tr, cols), nb, (lambda which, b: (which * nb + b, 0)), (lambda b: (b, 0))
    tc = _pick(cols // 2, (256, 128))
    nb = (cols // 2) // tc
    return (rows, tc), nb, (lambda which, b: (0, which * nb + b)), (lambda b: (0, b))


def _chip_partial(name, grad, other, core):
    s, r, cdim = grad.shape
    blk, nb, whole, within = _half_tiling(r, cdim)

    def body(core_ref, g_ref, o_ref, out_ref):
        out_ref[...] = (g_ref[...] + o_ref[...]).astype(BF16)

    return pl.pallas_call(
        body, name=name,
        grid_spec=pltpu.PrefetchScalarGridSpec(
            num_scalar_prefetch=1, grid=(s, nb),
            in_specs=[pl.BlockSpec((None,) + blk, lambda j, b, core_ref: (j,) + whole(core_ref[0], b)),
                      pl.BlockSpec((None,) + blk, lambda j, b, core_ref: (j,) + within(b))],
            out_specs=pl.BlockSpec((None,) + blk, lambda j, b, core_ref: (j,) + within(b))),
        out_shape=jax.ShapeDtypeStruct((s,) + _half_shape(r, cdim), BF16),
        compiler_params=_params("parallel", "parallel"),
    )(core, grad, other)


def _partial_copies(ins, zones, send_sems, recv_sems):
    x, y, c, chips = _place()
    return [_remote(ins[i].at[2 * chip[0] + chip[1]], zones[i].at[j], send_sems.at[3 * i + j],
                    recv_sems.at[3 * i + j], (*chip, c))
            for i in range(len(ins)) for j, chip in enumerate(chips)]


def _swap_copies(ins, zones, send_sems, recv_sems):
    x, y, c, _ = _place()
    copies = []
    for i in range(len(ins)):
        for s in range(N_SHARD):
            copies.append(_remote(_half(ins[i], 1 - c, s), zones[i].at[s],
                                  send_sems.at[N_SHARD * i + s], recv_sems.at[N_SHARD * i + s], (x, y, 1 - c)))
    return copies


def _exchange_start(name, plan, sources, lands, per_array):
    n = len(sources)
    lands = [lax.empty(shape, dtype) for shape, dtype in lands]

    def body(*refs):
        for cp in plan(refs[:n], refs[n:2 * n], refs[2 * n], refs[2 * n + 1]):
            cp.start()
        refs[-1][...] = jnp.zeros_like(refs[-1])

    dma = pltpu.SemaphoreType.DMA
    outs = pl.pallas_call(
        body, name=name,
        in_specs=[HBM] * (2 * n),
        out_shape=(dma((per_array * n,)), dma((per_array * n,)),
                   *[pltpu.HBM(a.shape, a.dtype) for a in list(sources) + lands], jax.ShapeDtypeStruct((8, LANES), F32)),
        out_specs=(SEM, SEM, *[HBM] * (2 * n), pl.BlockSpec(memory_space=pltpu.VMEM)),
        input_output_aliases={k: 2 + k for k in range(2 * n)},
        compiler_params=pltpu.CompilerParams(has_side_effects=EFFECT),
    )(*[_in_hbm(a) for a in list(sources) + lands])
    return (outs[0], outs[1]), list(outs[2:2 + n]), list(outs[2 + n:2 + 2 * n]), outs[-1]


def _exchange_wait(name, plan, started, after):
    sems, partials, lands, _ = started
    n = len(partials)

    def body(*refs):
        for cp in plan(refs[:n], refs[n:2 * n], refs[2 * n], refs[2 * n + 1]):
            cp.wait_send()
            cp.wait_recv()

    outs = pl.pallas_call(
        body, name=name,
        in_specs=[HBM] * (2 * n) + [SEM, SEM] + [ANY] * len(after),
        out_shape=tuple(pltpu.HBM(a.shape, a.dtype) for a in lands),
        out_specs=tuple([HBM] * n),
        input_output_aliases={n + k: k for k in range(n)},
        compiler_params=pltpu.CompilerParams(has_side_effects=EFFECT),
    )(*partials, *lands, sems[0], sems[1], *after)
    return list(outs)


def _reduce_own(name, grad, other, received, where):
    s, r, cdim = grad.shape
    blk, nb, whole, within = _half_tiling(r, cdim)

    def body(where_ref, g_ref, o_ref, r_ref, out_ref):
        total = g_ref[...] + o_ref[...]
        for j in range(3):
            total = total + r_ref[j].astype(F32)
        out_ref[...] = total

    return pl.pallas_call(
        body, name=name,
        grid_spec=pltpu.PrefetchScalarGridSpec(
            num_scalar_prefetch=1, grid=(nb,),
            in_specs=[pl.BlockSpec((None,) + blk, lambda b, w_ref: (w_ref[0],) + whole(w_ref[1], b)),
                      pl.BlockSpec((None,) + blk, lambda b, w_ref: (w_ref[0],) + within(b)),
                      pl.BlockSpec((3,) + blk, lambda b, w_ref: (0,) + within(b))],
            out_specs=pl.BlockSpec(blk, lambda b, w_ref: whole(w_ref[1], b))),
        out_shape=jax.ShapeDtypeStruct((r, cdim), F32),
        compiler_params=_params("parallel"),
    )(where, grad, other, received)


def _join_start(name, halves):
    n = len(halves)

    def body(*refs):
        bufs, send_sems, recv_sems = refs[:n], refs[n], refs[n + 1]
        x, y, c, _ = _place()
        for i in range(n):
            mine = _half(bufs[i], c)
            _remote(mine, mine, send_sems.at[i], recv_sems.at[i], (x, y, 1 - c)).start()
        refs[-1][...] = jnp.zeros_like(refs[-1])

    dma = pltpu.SemaphoreType.DMA
    outs = pl.pallas_call(
        body, name=name,
        in_specs=[HBM] * n,
        out_shape=(dma((n,)), dma((n,)), *[pltpu.HBM(h.shape, F32) for h in halves], jax.ShapeDtypeStruct((8, LANES), F32)),
        out_specs=(SEM, SEM, *[HBM] * n, pl.BlockSpec(memory_space=pltpu.VMEM)),
        input_output_aliases={k: 2 + k for k in range(n)},
        compiler_params=pltpu.CompilerParams(has_side_effects=EFFECT),
    )(*[_in_hbm(h) for h in halves])
    return (outs[0], outs[1]), list(outs[2:2 + n]), outs[-1]


def _join_wait(name, started, after):
    sems, bufs, _ = started
    n = len(bufs)

    def body(*refs):
        bufs, send_sems, recv_sems = refs[:n], refs[n], refs[n + 1]
        x, y, c, _ = _place()
        for i in range(n):
            mine, theirs = _half(bufs[i], c), _half(bufs[i], 1 - c)
            _remote(mine, mine, send_sems.at[i], recv_sems.at[i], (x, y, 1 - c)).wait_send()
            _remote(theirs, theirs, send_sems.at[i], recv_sems.at[i], (x, y, 1 - c)).wait_recv()

    outs = pl.pallas_call(
        body, name=name,
        in_specs=[HBM] * n + [SEM, SEM] + [ANY] * len(after),
        out_shape=tuple(pltpu.HBM(b.shape, F32) for b in bufs),
        out_specs=tuple([HBM] * n),
        input_output_aliases={k: k for k in range(n)},
        compiler_params=pltpu.CompilerParams(has_side_effects=EFFECT),
    )(*bufs, sems[0], sems[1], *after)
    return list(outs)


BIG = ("w_in", "pool_w", "w_out", "xq_w", "xk_w", "xv_w", "xo_w", "w_up", "w_down", "conv_w")
KEPT_F32 = ("conv_w",)
GATHER_GROUPS = ((0, 1, 9), (2, 3, 4, 5, 6), (7,), (8,))
RELAYED = (7, 8)
SMALL = ("conv_w", "a_log", "dt_bias", "gdn_norm_w", "pool_scale", "ln1_g", "ln1_b", "ln2_g", "ln2_b", "ln3_g", "ln3_b")
ORDER = ("w_in", "conv_w", "a_log", "dt_bias", "gdn_norm_w", "pool_w", "pool_scale", "w_out", "ln1_g", "ln1_b",
         "xq_w", "xk_w", "xv_w", "xo_w", "ln2_g", "ln2_b", "w_up", "w_down", "ln3_g", "ln3_b")
LANES = 128


def _rows(flat_len):
    return -(-flat_len // LANES)


def _pack(pieces):
    out = []
    for p in pieces:
        flat = p.reshape(-1).astype(F32)
        out.append(jnp.pad(flat, (0, _rows(flat.shape[0]) * LANES - flat.shape[0])).reshape(-1, LANES))
    slab = jnp.concatenate(out, axis=0)
    return jnp.pad(slab, ((0, -slab.shape[0] % 8), (0, 0)))


def _unpack(slab, shapes):
    out, row = [], 0
    for shp in shapes:
        size = math.prod(shp)
        out.append(slab[row:row + _rows(size)].reshape(-1)[:size].reshape(shp))
        row += _rows(size)
    return out


TRANSPOSED = ("w_in",)


def _as2d(name, a):
    a = a[0]
    if name in TRANSPOSED:
        return jnp.swapaxes(a, 0, 1)
    return a.reshape(-1, a.shape[-1]) if a.ndim == 3 else a


def _from2d(name, a, shape):
    return (jnp.swapaxes(a, 0, 1) if name in TRANSPOSED else a).reshape(shape)


def kernel(x, mem, w_in, conv_w, a_log, dt_bias, gdn_norm_w, pool_w, pool_scale, w_out, ln1_g, ln1_b, xq_w, xk_w, xv_w, xo_w, ln2_g, ln2_b, w_up, w_down, ln3_g, ln3_b, loss_target, m_w_in, m_conv_w, m_a_log, m_dt_bias, m_gdn_norm_w, m_pool_w, m_pool_scale, m_w_out, m_ln1_g, m_ln1_b, m_xq_w, m_xk_w, m_xv_w, m_xo_w, m_ln2_g, m_ln2_b, m_w_up, m_w_down, m_ln3_g, m_ln3_b, v_w_in, v_conv_w, v_a_log, v_dt_bias, v_gdn_norm_w, v_pool_w, v_pool_scale, v_w_out, v_ln1_g, v_ln1_b, v_xq_w, v_xk_w, v_xv_w, v_xo_w, v_ln2_g, v_ln2_b, v_w_up, v_w_down, v_ln3_g, v_ln3_b):
    given = dict(locals())
    cx, cy, cc = lax.axis_index("x"), lax.axis_index("y"), lax.axis_index("c")
    me = 2 * cx + cy
    groups = pool_w.shape[1]
    cs = pool_w.shape[2]
    kk, conv_cols = conv_w.shape[1], conv_w.shape[2]
    core = cc.astype(jnp.int32).reshape(1)
    where = jnp.stack([me, cc]).astype(jnp.int32)

    started = {}
    wts = {}

    def start(name, idx, after, token=None):
        casts = [_after(token, _as2d(BIG[i], given[BIG[i]])).astype(F32 if BIG[i] in KEPT_F32 else BF16) for i in idx]
        relayed = tuple(k for k, i in enumerate(idx) if i in RELAYED)
        sems, shards, lands, token = _gather_start(name, casts, after, relayed)
        for k, i in enumerate(idx):
            started[i] = (sems, k, shards[k], lands[k])
        return token

    token = start("gather_start_first", GATHER_GROUPS[0], x)
    token = start("gather_start_rest", tuple(i for group in GATHER_GROUPS[1:] for i in group), token, token)

    relay = {}

    def send_on(after):
        members = [started[i] for i in RELAYED]
        relay["sems"], zones, token = _gather_relay("gather_relay", [m[1] for m in members], [m[2] for m in members],
                                                    [m[3] for m in members], members[0][0], after)
        relay["zones"] = dict(zip(RELAYED, zones))
        return token

    passed = {}

    def pass_on(group, after):
        members = [started[i] for i in GATHER_GROUPS[group]]
        fwd, zones, token = _gather_forward(f"gather_forward_{group}", [m[1] for m in members], [m[3] for m in members],
                                            members[0][0], after)
        passed[group] = (fwd, zones)
        return token

    def fetch(group, after):
        members = [started[i] for i in GATHER_GROUPS[group]]
        sems, idx = members[0][0], [m[1] for m in members]
        shards = [m[2] for m in members]
        if GATHER_GROUPS[group][0] in RELAYED:
            ks = [RELAYED.index(i) for i in GATHER_GROUPS[group]]
            zones = [relay["zones"][i] for i in GATHER_GROUPS[group]]
            fwd, zones = _gather_forward_relayed(f"gather_forward_{group}", ks, zones, relay["sems"], after)
            got = _gather_wait_relayed(f"gather_wait_{group}", idx, ks, shards, zones, sems, relay["sems"], fwd, after)
        else:
            if group not in passed:
                pass_on(group, after)
            fwd, zones = passed[group]
            got = _gather_wait(f"gather_wait_{group}", idx, shards, zones, sems, fwd, after)
        full = dict(zip([BIG[i] for i in GATHER_GROUPS[group]], got))
        out = {}
        for n, a in full.items():
            if n == "w_in":
                out["w_in_t"] = a
            elif n == "w_up":
                out["w_up3"] = a
            elif n == "pool_w":
                out[n] = a.reshape(N_SHARD, groups, cs, -1).transpose(1, 0, 2, 3).reshape(groups, N_SHARD * cs, -1)
            elif n == "conv_w":
                out[n] = a.transpose(1, 0, 2).reshape(kk, N_SHARD * conv_cols)
            else:
                out[n] = a.reshape(-1, a.shape[-1])
        return out

    for n in ("a_log", "dt_bias", "gdn_norm_w", "pool_scale", "ln1_g", "ln1_b", "ln2_g", "ln2_b", "ln3_g", "ln3_b"):
        wts[n] = given[n]
    x_bf = _after(token, x[0]).astype(BF16)
    wts.update(fetch(0, x_bf))

    def start_swap(group, grads):
        names, blocks = [], []
        for n, g in grads.items():
            if n == "pool_w":
                g = g.reshape(groups, N_SHARD, cs, -1).transpose(1, 0, 2, 3).reshape(N_SHARD, groups * cs, -1)
            elif g.ndim == 2:
                g = g.reshape(N_SHARD, -1, g.shape[-1])
            names.append({"w_in_t": "w_in", "w_up3": "w_up"}.get(n, n))
            blocks.append(g)
        zones = [((N_SHARD,) + _half_shape(b.shape[1], b.shape[2]), F32) for b in blocks]
        swap = _exchange_start(f"grad_swap_start_{group}", _swap_copies, blocks, zones, N_SHARD)
        return {"group": group, "names": names, "swap": swap, "token": swap[3]}

    def start_send(state, after):
        group, names = state["group"], state["names"]
        state["blocks"] = state["swap"][1]
        state["others"] = _exchange_wait(f"grad_swap_wait_{group}", _swap_copies, state["swap"], after)
        partials = [_chip_partial("chip_partial_" + n, gb, ob, core)
                    for n, gb, ob in zip(names, state["blocks"], state["others"])]
        zones = [((3,) + p.shape[1:], BF16) for p in partials]
        state["send"] = _exchange_start(f"grad_send_start_{group}", _partial_copies, partials, zones, 3)
        state["token"] = state["send"][3]

    grad, delta, new_m, new_v = {}, {}, {}, {}

    def start_join(state, after):
        group, names = state["group"], state["names"]
        received = _exchange_wait(f"grad_send_wait_{group}", _partial_copies, state["send"], after)
        halves = [_reduce_own("reduce_own_" + n, gb, ob, rb, where)
                  for n, gb, ob, rb in zip(names, state["blocks"], state["others"], received)]
        state["join"] = _join_start(f"grad_join_start_{group}", halves)
        return state["join"][2]

    def finish_reduce(state, after):
        group, names = state["group"], state["names"]
        for n, g in zip(names, _join_wait(f"grad_join_wait_{group}", state["join"], after)):
            shp = given[n].shape
            d2, m2, v2, g2 = _adamw("adamw_" + n, _as2d(n, given[n]), g, _as2d(n, given["m_" + n]),
                                    _as2d(n, given["v_" + n]))
            grad[n], delta[n], new_m[n], new_v[n] = (_from2d(n, a, shp) for a in (g2, d2, m2, v2))
        return d2

    step = _local_step(x[0], mem[0], loss_target[0], wts, x_bf)
    pending = {}
    request = next(step)
    while True:
        try:
            kind, group, payload = request
            if kind == "weights":
                request = step.send(fetch(group, payload))
            elif kind == "relay":
                request = step.send(send_on(payload))
            elif kind == "pass":
                request = step.send(pass_on(group, payload))
            elif kind == "grads":
                pending[group] = start_swap(group, payload)
                request = step.send(pending[group]["token"])
            else:
                start_send(pending[group], [payload])
                request = step.send(pending[group]["token"])
        except StopIteration as stop:
            loss_row, grad_x, g = stop.value
            break

    after = [pending[2]["token"], grad_x]
    for group in (0, 1):
        after = [start_join(pending[group], after)]
    for group in (0, 1):
        after = [finish_reduce(pending[group], after)]
    after = [finish_reduce(pending[2], [start_join(pending[2], after)])]

    small_names = ("a_log", "dt_bias", "gdn_norm_w", "pool_scale", "ln1_g", "ln1_b", "ln2_g", "ln2_b", "ln3_g", "ln3_b")
    pieces = [g["conv_w"]] + [g[n] for n in small_names] + [loss_row[:, :1]]
    shapes = [p.shape for p in pieces]
    summed = _unpack(_all_reduce_small("all_reduce_small", _pack(pieces), after[0]), shapes)
    gsmall = dict(zip(small_names, summed[1:-1]))
    gsmall["conv_w"] = lax.dynamic_slice(summed[0], (0, me * conv_cols), (kk, conv_cols))
    loss = summed[-1][0, 0]

    sshapes = [given[n].shape for n in SMALL]
    slabs = [_pack([given[p + n] for n in SMALL]) for p in ("", "m_", "v_")]
    gslab = _pack([gsmall[n] for n in SMALL])
    outs = _adamw("adamw_small", slabs[0], gslab, slabs[1], slabs[2])[:3]
    for dst, slab in zip((delta, new_m, new_v), outs):
        dst.update(zip(SMALL, _unpack(slab, sshapes)))
    for n in SMALL:
        grad[n] = gsmall[n].reshape(given[n].shape)

    return (loss, grad_x[None], *[grad[n] for n in ORDER], *[delta[n] for n in ORDER],
            *[new_m[n] for n in ORDER], *[new_v[n] for n in ORDER])
```

```python
import math

import jax
import jax.numpy as jnp
from jax import lax
from jax.experimental import pallas as pl
from jax.experimental.pallas import tpu as pltpu

F32 = jnp.float32
BF16 = jnp.bfloat16
MESH = pl.DeviceIdType.MESH

HEAD_DIM = 128
CHUNK = 64
POOL_WINDOWS = (2, 4, 8, 16)
XATTN_HEADS = 4
ALPHA = 2.0 ** 0.25
LN_EPS = 1e-5
NORM_EPS = 1e-6
ADAM_LR, ADAM_B1, ADAM_B2, ADAM_EPS, ADAM_WD, ADAM_STEP = 0.001, 0.9, 0.999, 1e-08, 0.01, 10
N_SHARD = 4
VMEM_LIMIT = 56 * 1024 * 1024
K_STEPS = (2048, 1024, 512, 256, 128)


def _params(*sem):
    return pltpu.CompilerParams(dimension_semantics=sem, vmem_limit_bytes=VMEM_LIMIT)


def _bdot(a, b, ta=False, tb=False):
    dims = (((0 if ta else 1,), (1 if tb else 0,)), ((), ()))
    return lax.dot_general(a.astype(BF16), b.astype(BF16), dims, preferred_element_type=F32)


def _sigmoid(x):
    return 1.0 / (1.0 + jnp.exp(-x))


def _matmul(name, a, b, *, ta=False, tb=False, tm, tn, tk, extra=(), outs, epilogue, b_blocks=None,
            sequential=False, n_used=None, n_outer=False):
    m, k_dim = (a.shape[1], a.shape[0]) if ta else a.shape
    if b_blocks and tb:
        n = b.shape[1]
        k_dim = b.shape[0] * b.shape[2]
        per = b.shape[2] // tk
        b_spec = pl.BlockSpec((None, tn, tk), lambda i, j, k: (k // per, j, k % per))
    elif b_blocks:
        n = b.shape[0] * b.shape[2]
        per = b.shape[2] // tn
        b_spec = pl.BlockSpec((None, tk, tn), lambda i, j, k: (j // per, k, j % per))
    elif tb:
        n = b.shape[0]
        b_spec = pl.BlockSpec((tn, tk), lambda i, j, k: (j, k))
    else:
        n = b.shape[1]
        b_spec = pl.BlockSpec((tk, tn), lambda i, j, k: (k, j))
    n = n_used or n
    assert m % tm == 0 and n % tn == 0 and k_dim % tk == 0, (name, m, n, k_dim, tm, tn, tk)
    nk = k_dim // tk
    a_spec = pl.BlockSpec((tk, tm), lambda i, j, k: (k, i)) if ta else pl.BlockSpec((tm, tk), lambda i, j, k: (i, k))
    n_extra, n_out = len(extra), len(outs)

    def wrap(index_map):
        return lambda i, j, k: index_map(i, j)

    def spec(block, index_map):
        if n_outer:
            return pl.BlockSpec(block, lambda j, i, k: index_map(i, j, k))
        return pl.BlockSpec(block, index_map)

    row_axis = 1 if n_outer else 0

    def body_one_step(*refs):
        ex = refs[2:2 + n_extra]
        out = refs[2 + n_extra:2 + n_extra + n_out]
        epilogue(_bdot(refs[0][...], refs[1][...], ta, tb), ex, out, pl.program_id(row_axis))

    def body(*refs):
        a_ref, b_ref = refs[0], refs[1]
        ex = refs[2:2 + n_extra]
        out = refs[2 + n_extra:2 + n_extra + n_out]
        acc = refs[-1]
        i, k = pl.program_id(row_axis), pl.program_id(2)
        part = _bdot(a_ref[...], b_ref[...], ta, tb)

        @pl.when(k == 0)
        def _():
            acc[...] = part

        @pl.when(jnp.logical_and(k > 0, k < nk - 1))
        def _():
            acc[...] += part

        @pl.when(k == nk - 1)
        def _():
            epilogue(acc[...] + part, ex, out, i)

    sem = ("arbitrary",) * 3 if sequential else ("parallel", "parallel", "arbitrary")
    res = pl.pallas_call(
        body_one_step if nk == 1 else body, name=name,
        grid=(n // tn, m // tm, nk) if n_outer else (m // tm, n // tn, nk),
        in_specs=[spec(a_spec.block_shape, a_spec.index_map), spec(b_spec.block_shape, b_spec.index_map)]
        + [spec(bs, wrap(im)) for _, bs, im in extra],
        out_specs=[spec(bs, wrap(im)) for _, bs, im in outs],
        out_shape=[s for s, _, _ in outs],
        scratch_shapes=[] if nk == 1 else [pltpu.VMEM((tm, tn), F32)],
        compiler_params=_params(*sem),
    )(a, b, *[x for x, _, _ in extra])
    return res


def _tile(i, j):
    return (i, j)


def _plain(name, a, b, *, ta=False, tb=False, tm, tn, tk, out_dtype, b_blocks=None, out3=None, n_used=None,
           n_outer=False, m_kept=None):
    m = a.shape[1] if ta else a.shape[0]
    if b_blocks:
        n = b.shape[1] if tb else b.shape[0] * b.shape[2]
    else:
        n = n_used or (b.shape[0] if tb else b.shape[1])

    def epi(acc, ex, out, i):
        out[0][...] = acc.astype(out_dtype)

    if out3:
        per = (n // out3) // tn
        spec = (jax.ShapeDtypeStruct((out3, m, n // out3), out_dtype), (None, tm, tn),
                lambda i, j: (j // per, i, j % per))
    else:
        spec = (jax.ShapeDtypeStruct((m_kept or m, n), out_dtype), (tm, tn), _tile)
    return _matmul(name, a, b, ta=ta, tb=tb, tm=tm, tn=tn, tk=tk, outs=[spec], epilogue=epi,
                   b_blocks=b_blocks, n_used=n_used, n_outer=n_outer)[0]


def _ln_forward(name, a, b, res, gamma, beta, *, tm, tk, want_h=True):
    m, n = res.shape

    def epi(acc, ex, out, i):
        u = ALPHA * ex[0][...] + acc
        mu = jnp.mean(u, axis=-1, keepdims=True)
        xc = u - mu
        var = jnp.mean(xc * xc, axis=-1, keepdims=True)
        rstd = lax.rsqrt(var + LN_EPS)
        xhat = xc * rstd
        out[-2][...] = xhat
        out[-1][...] = rstd
        if want_h:
            h = xhat * ex[1][...] + ex[2][...]
            out[0][...] = h
            out[1][...] = h.astype(BF16)

    row = lambda i, j: (i, 0)
    vec = lambda i, j: (0, 0)
    outs = [(jax.ShapeDtypeStruct((m, n), F32), (tm, n), row), (jax.ShapeDtypeStruct((m, n), BF16), (tm, n), row),
            (jax.ShapeDtypeStruct((m, n), F32), (tm, n), row), (jax.ShapeDtypeStruct((m, 1), F32), (tm, 1), row)]
    return _matmul(
        name, a, b, tm=tm, tn=n, tk=tk,
        extra=[(res, (tm, n), row), (gamma, (1, n), vec), (beta, (1, n), vec)],
        outs=outs if want_h else outs[2:], epilogue=epi)


def _ln_backward_math(dy, xhat, rstd, gamma):
    dxhat = dy * gamma
    m1 = jnp.mean(dxhat, axis=-1, keepdims=True)
    m2 = jnp.mean(dxhat * xhat, axis=-1, keepdims=True)
    du = rstd * (dxhat - m1 - xhat * m2)
    return du, jnp.sum(dy * xhat, axis=0, keepdims=True), jnp.sum(dy, axis=0, keepdims=True)


def _ln_backward(name, a, b, dres, xhat, rstd, gamma, *, tm, tk, b_blocks=None, tb=True):
    m, n = dres.shape

    def epi(acc, ex, out, i):
        dy = acc + ALPHA * ex[0][...]
        du, dg, db = _ln_backward_math(dy, ex[1][...], ex[2][...], ex[3][...])
        out[0][...] = du
        out[1][...] = du.astype(BF16)
        first = i == 0

        @pl.when(first)
        def _():
            out[2][...] = dg
            out[3][...] = db

        @pl.when(jnp.logical_not(first))
        def _():
            out[2][...] += dg
            out[3][...] += db

    row = lambda i, j: (i, 0)
    vec = lambda i, j: (0, 0)
    return _matmul(
        name, a, b, tb=tb, tm=tm, tn=n, tk=tk, b_blocks=b_blocks, sequential=True,
        extra=[(dres, (tm, n), row), (xhat, (tm, n), row), (rstd, (tm, 1), row), (gamma, (1, n), vec)],
        outs=[(jax.ShapeDtypeStruct((m, n), F32), (tm, n), row),
              (jax.ShapeDtypeStruct((m, n), BF16), (tm, n), row),
              (jax.ShapeDtypeStruct((1, n), F32), (1, n), vec),
              (jax.ShapeDtypeStruct((1, n), F32), (1, n), vec)],
        epilogue=epi)


def _shift_down(x, k):
    row = lax.broadcasted_iota(jnp.int32, x.shape, 0)
    return jnp.where(row >= k, pltpu.roll(x, k, axis=0), 0.0)


def _shift_up(x, k):
    t = x.shape[0]
    row = lax.broadcasted_iota(jnp.int32, x.shape, 0)
    return jnp.where(row < t - k, pltpu.roll(x, t - k, axis=0), 0.0)


def _conv_silu_norm(x, w, normalise):
    kk = w.shape[0]
    c = x * w[kk - 1:kk, :]
    for j in range(kk - 1):
        c = c + _shift_down(x, kk - 1 - j) * w[j:j + 1, :]
    sg = _sigmoid(c)
    s = c * sg
    r = lax.rsqrt(jnp.sum(s * s, axis=-1, keepdims=True) + NORM_EPS)
    y = jnp.where(normalise, s * r, s)
    return c, sg, s, r, y


def _gdn_pre(proj, conv_w, heads):
    t = proj.shape[0]
    kk = conv_w.shape[0]

    def body(x_ref, w_ref, o_ref):
        normalise = pl.program_id(0) < 2
        o_ref[...] = _conv_silu_norm(x_ref[...], w_ref[...], normalise)[4]

    col = lambda s, h: (0, s * heads + h)
    return pl.pallas_call(
        body, name="gdn_pre", grid=(3, heads),
        in_specs=[pl.BlockSpec((t, HEAD_DIM), col), pl.BlockSpec((kk, HEAD_DIM), col)],
        out_specs=pl.BlockSpec((t, HEAD_DIM), col),
        out_shape=jax.ShapeDtypeStruct((t, 3 * heads * HEAD_DIM), F32),
        compiler_params=_params("parallel", "parallel"),
    )(proj, conv_w)


def _gdn_pre_backward(proj, conv_w, dqkv, heads):
    t = proj.shape[0]
    kk = conv_w.shape[0]

    def body(x_ref, w_ref, dy_ref, dx_ref, dw_ref):
        normalise = pl.program_id(0) < 2
        x = x_ref[...]
        w = w_ref[...]
        dy = dy_ref[...]
        c, sg, s, r, y = _conv_silu_norm(x, w, normalise)
        ds_norm = r * (dy - y * jnp.sum(dy * y, axis=-1, keepdims=True))
        ds = jnp.where(normalise, ds_norm, dy)
        dc = ds * (sg * (1.0 + c * (1.0 - sg)))
        dx = dc * w[kk - 1:kk, :]
        rows = [None] * kk
        rows[kk - 1] = jnp.sum(dc * x, axis=0, keepdims=True)
        for j in range(kk - 1):
            lag = kk - 1 - j
            dx = dx + _shift_up(dc, lag) * w[j:j + 1, :]
            rows[j] = jnp.sum(dc * _shift_down(x, lag), axis=0, keepdims=True)
        dx_ref[...] = dx.astype(BF16)
        dw_ref[...] = jnp.concatenate(rows, axis=0)

    col = lambda s, h: (0, s * heads + h)
    return pl.pallas_call(
        body, name="gdn_pre_bwd", grid=(3, heads),
        in_specs=[pl.BlockSpec((t, HEAD_DIM), col), pl.BlockSpec((kk, HEAD_DIM), col),
                  pl.BlockSpec((t, HEAD_DIM), col)],
        out_specs=[pl.BlockSpec((t, HEAD_DIM), col), pl.BlockSpec((kk, HEAD_DIM), col)],
        out_shape=[jax.ShapeDtypeStruct((t, 3 * heads * HEAD_DIM), BF16),
                   jax.ShapeDtypeStruct((kk, 3 * heads * HEAD_DIM), F32)],
        compiler_params=_params("parallel", "parallel"),
    )(proj, conv_w, dqkv)


def _gate_vectors(a_log, dt_bias, heads):
    pad = lambda v: jnp.pad(v.astype(F32), ((0, 0), (heads, HEAD_DIM - 2 * heads)))
    return pad(jnp.exp(a_log.astype(F32))), pad(dt_bias)


def _softplus(x):
    return jnp.maximum(x, 0.0) + jnp.log(1.0 + jnp.exp(-jnp.abs(x)))


def _gates_epilogue(heads):
    def epi(acc, ex, out, i):
        lane = lax.broadcasted_iota(jnp.int32, acc.shape, 1)
        beta = _sigmoid(acc)
        g = -ex[0][...] * _softplus(acc + ex[1][...])
        out[0][...] = acc
        out[1][...] = jnp.where(lane < heads, beta, jnp.where(lane < 2 * heads, g, 0.0))
    return epi


def _gates_backward(ba, bg, dbg, ea, dtb, heads):
    t = ba.shape[0]

    def body(ba_ref, bg_ref, d_ref, ea_ref, dt_ref, dba_ref, dal_ref, ddt_ref):
        lane = lax.broadcasted_iota(jnp.int32, (t, HEAD_DIM), 1)
        bgv = bg_ref[...]
        d = d_ref[...]
        db = d * bgv * (1.0 - bgv)
        da = -d * ea_ref[...] * _sigmoid(ba_ref[...] + dt_ref[...])
        is_g = jnp.logical_and(lane >= heads, lane < 2 * heads)
        dba = jnp.where(lane < heads, db, jnp.where(is_g, da, 0.0))
        dba_ref[...] = dba.astype(BF16)
        dal_ref[...] = jnp.sum(jnp.where(is_g, d * bgv, 0.0), axis=0, keepdims=True)
        ddt_ref[...] = jnp.sum(jnp.where(is_g, da, 0.0), axis=0, keepdims=True)

    full = pl.BlockSpec((t, HEAD_DIM), lambda: (0, 0))
    vec = pl.BlockSpec((1, HEAD_DIM), lambda: (0, 0))
    return pl.pallas_call(
        body, name="gates_bwd", grid=(),
        in_specs=[full, full, full, vec, vec], out_specs=[full, vec, vec],
        out_shape=[jax.ShapeDtypeStruct((t, HEAD_DIM), BF16), jax.ShapeDtypeStruct((1, HEAD_DIM), F32),
                   jax.ShapeDtypeStruct((1, HEAD_DIM), F32)],
        compiler_params=pltpu.CompilerParams(vmem_limit_bytes=VMEM_LIMIT),
    )(ba, bg, dbg, ea, dtb)


class _Chunk:
    pass


def _split2(x):
    hi = x.astype(BF16)
    return hi, (x - hi.astype(F32)).astype(BF16)


def _split3(x):
    hi = x.astype(BF16)
    rest = x - hi.astype(F32)
    mid = rest.astype(BF16)
    return hi, mid, (rest - mid.astype(F32)).astype(BF16)


def _dot_mask(mask, x, ta=False):
    hi, mid, lo = _split3(x)
    return _bdot(mask, hi, ta=ta) + (_bdot(mask, mid, ta=ta) + _bdot(mask, lo, ta=ta))


def _transpose_by_identity(x):
    r = x.shape[0]
    eye = (lax.broadcasted_iota(jnp.int32, (r, r), 0) == lax.broadcasted_iota(jnp.int32, (r, r), 1)).astype(BF16)
    hi, mid, lo = _split3(x)
    return _bdot(hi, eye, ta=True) + (_bdot(mid, eye, ta=True) + _bdot(lo, eye, ta=True))


def _dot22(a, b, ta=False, tb=False):
    ah, al = _split2(a)
    bh, bl = _split2(b)
    return _bdot(ah, bh, ta, tb) + (_bdot(ah, bl, ta, tb) + _bdot(al, bh, ta, tb))


def _chunk_gates(bg, heads):
    n = CHUNK
    row = lax.broadcasted_iota(jnp.int32, (n, n), 0)
    col = lax.broadcasted_iota(jnp.int32, (n, n), 1)
    lane = lax.broadcasted_iota(jnp.int32, bg.shape, 1)
    graw = jnp.where(jnp.logical_and(lane >= heads, lane < 2 * heads), bg, 0.0)
    gc = _dot_mask((row >= col).astype(BF16), graw)
    return gc, _transpose_by_identity(gc)


def _in_lockstep(generators):
    results = [None] * len(generators)
    live = list(enumerate(generators))
    while live:
        still = []
        for i, gen in live:
            try:
                next(gen)
                still.append((i, gen))
            except StopIteration as stop:
                results[i] = stop.value
        live = still
    return results


def _chunk_local(q, k, v, beta, gc, grow, solved=None):
    c = _Chunk()
    n = CHUNK
    row = lax.broadcasted_iota(jnp.int32, (n, n), 0)
    col = lax.broadcasted_iota(jnp.int32, (n, n), 1)
    c.tri = row >= col
    c.strict = row > col
    eye = row == col
    c.gcb = jnp.broadcast_to(gc, (n, HEAD_DIM))
    c.decay = jnp.where(c.tri, jnp.exp(jnp.where(c.tri, gc - grow, 0.0)), 0.0)
    c.eg = jnp.exp(c.gcb)
    glast = c.gcb[n - 1:n, :]
    c.egl = jnp.exp(glast)
    c.ekl = jnp.exp(glast - c.gcb)
    c.beta = beta
    c.q = q * (HEAD_DIM ** -0.5)
    c.k = k
    c.v = v
    c.kb = k * beta
    c.vb = v * beta
    c.kg = c.kb * c.eg
    both = _bdot(jnp.concatenate([c.kb, c.q], axis=0), k, tb=True)
    yield
    c.L = jnp.where(c.strict, both[:n] * c.decay, 0.0)
    c.A = jnp.where(c.tri, both[n:] * c.decay, 0.0)
    if solved is None:
        x = -c.L
        tinv = eye.astype(F32) + x
        p = _dot22(x, x)
        yield
        for _ in range(int(math.log2(n)) - 2):
            both = _dot22(jnp.concatenate([p, tinv], axis=0), p)
            yield
            p, tinv = both[:n], tinv + both[n:]
        c.T = tinv + _dot22(tinv, p)
        yield
        uw = _dot22(c.T, jnp.concatenate([c.vb, c.kg], axis=1))
        yield
        c.u, c.w = uw[:, :HEAD_DIM], uw[:, HEAD_DIM:]
    else:
        c.T, c.u, c.w = solved
    c.qg = c.q * c.eg
    c.kdec = k * c.ekl
    return c


def _gdn_core(qkv, bg, heads):
    t = qkv.shape[0]
    nchunk = t // CHUNK

    gw = heads * HEAD_DIM

    def body(qkv_ref, bg_ref, o_ref, s_ref, t_ref, u_ref, w_ref, state):
        @pl.when(pl.program_id(0) == 0)
        def _():
            state[...] = jnp.zeros_like(state)

        bg_v = bg_ref[...]
        gc_all, gc_rows = _chunk_gates(bg_v, heads)
        def one_head(h):
            col = lambda s: pl.ds(s * gw + h * HEAD_DIM, HEAD_DIM)
            c = yield from _chunk_local(qkv_ref[:, col(0)], qkv_ref[:, col(1)], qkv_ref[:, col(2)], bg_v[:, h:h + 1],
                                        gc_all[:, heads + h:heads + h + 1], gc_rows[heads + h:heads + h + 1, :])
            s0 = state[h]
            v_new = c.u - _bdot(c.w, s0)
            yield
            o = _bdot(c.qg, s0) + _bdot(c.A, v_new)
            return s0, o, s0 * c.egl + _bdot(c.kdec, v_new, ta=True), c

        results = _in_lockstep([one_head(h) for h in range(heads)])
        for h, (s0, o, s1, c) in enumerate(results):
            lanes = pl.ds(h * HEAD_DIM, HEAD_DIM)
            s_ref[h, 0] = s0
            o_ref[:, lanes] = o
            t_ref[:, lanes] = jnp.concatenate([c.T, jnp.zeros((CHUNK, HEAD_DIM - CHUNK), F32)], axis=1)
            u_ref[:, lanes] = c.u
            w_ref[:, lanes] = c.w
            state[h] = s1

    return pl.pallas_call(
        body, name="gdn_core", grid=(nchunk,),
        in_specs=[pl.BlockSpec((CHUNK, 3 * gw), lambda n: (n, 0)), pl.BlockSpec((CHUNK, HEAD_DIM), lambda n: (n, 0))],
        out_specs=[pl.BlockSpec((CHUNK, gw), lambda n: (n, 0)),
                   pl.BlockSpec((heads, 1, HEAD_DIM, HEAD_DIM), lambda n: (0, n, 0, 0))]
        + [pl.BlockSpec((CHUNK, gw), lambda n: (n, 0))] * 3,
        out_shape=[jax.ShapeDtypeStruct((t, gw), F32),
                   jax.ShapeDtypeStruct((heads, nchunk, HEAD_DIM, HEAD_DIM), F32)]
        + [jax.ShapeDtypeStruct((t, gw), F32)] * 3,
        scratch_shapes=[pltpu.VMEM((heads, HEAD_DIM, HEAD_DIM), F32)],
        compiler_params=_params("arbitrary"),
    )(qkv, bg)


def _gdn_core_backward(qkv, bg, states, solved, do, heads):
    t = qkv.shape[0]
    nchunk = t // CHUNK
    n = CHUNK

    def one_head(chunk_local, s0, d_out, ds1):
        c = yield from chunk_local
        v_new = c.u - _bdot(c.w, s0)
        dqg = _bdot(d_out, s0, tb=True)
        ds0 = _bdot(c.qg, d_out, ta=True) + ds1 * c.egl
        dv_new = _bdot(c.A, d_out, ta=True) + _bdot(c.kdec, ds1)
        yield
        dA = jnp.where(c.tri, _bdot(d_out, v_new, tb=True), 0.0)
        dkdec = _bdot(v_new, ds1, tb=True)
        dgl = jnp.sum(jnp.sum(ds1 * s0, axis=1, keepdims=True), axis=0, keepdims=True) * c.egl
        dw = -_bdot(dv_new, s0, tb=True)
        ds0 = ds0 - _bdot(c.w, dv_new, ta=True)
        yield
        both = _dot22(c.T, jnp.concatenate([dv_new, dw], axis=1), ta=True)
        yield
        dvb, dkg = both[:, :HEAD_DIM], both[:, HEAD_DIM:]
        dL = jnp.where(c.strict, -(_bdot(dvb, c.u, tb=True) + _bdot(dkg, c.w, tb=True)), 0.0)
        yield
        dm1 = dL * c.decay
        dkb = _bdot(dm1, c.k) + dkg * c.eg
        dk = _bdot(dm1, c.kb, ta=True)
        dm2 = dA * c.decay
        dq = _bdot(dm2, c.k) + dqg * c.eg
        dk = dk + _bdot(dm2, c.q, ta=True) + dkdec * c.ekl + dkb * c.beta
        pm = dL * c.L + dA * c.A
        ones = jnp.ones((n, HEAD_DIM), BF16)
        pm_hi, pm_lo = _split2(pm)
        colsum = _bdot(pm_hi, ones, ta=True) + _bdot(pm_lo, ones, ta=True)
        tk_ = jnp.sum(dkdec * c.kdec, axis=1, keepdims=True)
        dgc = (jnp.sum(pm, axis=1, keepdims=True) - colsum
               + jnp.sum(dqg * c.qg, axis=1, keepdims=True)
               - tk_
               + jnp.sum(dkg * c.kg, axis=1, keepdims=True))
        dgl = dgl + jnp.sum(tk_, axis=0, keepdims=True)
        rowi = lax.broadcasted_iota(jnp.int32, (n, HEAD_DIM), 0)
        dgc = dgc + jnp.where(rowi == n - 1, dgl, 0.0)
        dbeta = jnp.sum(dkb * c.k, axis=1, keepdims=True) + jnp.sum(dvb * c.v, axis=1, keepdims=True)
        return dq * (HEAD_DIM ** -0.5), dk, dvb * c.beta, dbeta, dgc, ds0

    gw = heads * HEAD_DIM

    def body(qkv_ref, bg_ref, s_ref, t_ref, u_ref, w_ref, do_ref, dqkv_ref, dbg_ref, dstate):
        @pl.when(pl.program_id(0) == 0)
        def _():
            dstate[...] = jnp.zeros_like(dstate)

        bg_v = bg_ref[...]
        gc_all, gc_rows = _chunk_gates(bg_v, heads)
        lane = lax.broadcasted_iota(jnp.int32, (n, HEAD_DIM), 1)
        dgates = jnp.zeros((n, HEAD_DIM), F32)
        chains = []
        for h in range(heads):
            col = lambda s: pl.ds(s * gw + h * HEAD_DIM, HEAD_DIM)
            lanes = pl.ds(h * HEAD_DIM, HEAD_DIM)
            c = _chunk_local(qkv_ref[:, col(0)], qkv_ref[:, col(1)], qkv_ref[:, col(2)], bg_v[:, h:h + 1],
                             gc_all[:, heads + h:heads + h + 1], gc_rows[heads + h:heads + h + 1, :],
                             (t_ref[:, pl.ds(h * HEAD_DIM, CHUNK)], u_ref[:, lanes], w_ref[:, lanes]))
            chains.append(one_head(c, s_ref[h, 0], do_ref[:, pl.ds(h * HEAD_DIM, HEAD_DIM)], dstate[h]))
        results = _in_lockstep(chains)
        for h, (dq, dk, dv, dbeta, dgc, ds0) in enumerate(results):
            dgates = jnp.where(lane == h, dbeta, jnp.where(lane == heads + h, dgc, dgates))
        for h, (dq, dk, dv, dbeta, dgc, ds0) in enumerate(results):
            dqkv_ref[:, pl.ds(h * HEAD_DIM, HEAD_DIM)] = dq
            dqkv_ref[:, pl.ds(gw + h * HEAD_DIM, HEAD_DIM)] = dk
            dqkv_ref[:, pl.ds(2 * gw + h * HEAD_DIM, HEAD_DIM)] = dv
            dstate[h] = ds0
        row = lax.broadcasted_iota(jnp.int32, (n, n), 0)
        colm = lax.broadcasted_iota(jnp.int32, (n, n), 1)
        draw = _dot_mask((row >= colm).astype(BF16), dgates, ta=True)
        dbg_ref[...] = jnp.where(lane < heads, dgates, draw)

    last = nchunk - 1
    return pl.pallas_call(
        body, name="gdn_core_bwd", grid=(nchunk,),
        in_specs=[pl.BlockSpec((CHUNK, 3 * gw), lambda i: (last - i, 0)),
                  pl.BlockSpec((CHUNK, HEAD_DIM), lambda i: (last - i, 0)),
                  pl.BlockSpec((heads, 1, HEAD_DIM, HEAD_DIM), lambda i: (0, last - i, 0, 0))]
        + [pl.BlockSpec((CHUNK, gw), lambda i: (last - i, 0))] * 4,
        out_specs=[pl.BlockSpec((CHUNK, 3 * gw), lambda i: (last - i, 0)),
                   pl.BlockSpec((CHUNK, HEAD_DIM), lambda i: (last - i, 0))],
        out_shape=[jax.ShapeDtypeStruct((t, 3 * gw), F32), jax.ShapeDtypeStruct((t, HEAD_DIM), F32)],
        scratch_shapes=[pltpu.VMEM((heads, HEAD_DIM, HEAD_DIM), F32)],
        compiler_params=_params("arbitrary"),
    )(qkv, bg, states, *solved, do)


def _gdn_post(o, proj, z_col0, norm_w, heads, tt):
    t = o.shape[0]
    zb = z_col0 // HEAD_DIM

    def body(o_ref, z_ref, w_ref, out_ref):
        ov = o_ref[...]
        z = z_ref[...]
        rms = lax.rsqrt(jnp.mean(ov * ov, axis=-1, keepdims=True) + NORM_EPS)
        out_ref[...] = (ov * rms * w_ref[...] * (z * _sigmoid(z))).astype(BF16)

    return pl.pallas_call(
        body, name="gdn_post", grid=(t // tt, heads),
        in_specs=[pl.BlockSpec((tt, HEAD_DIM), lambda i, h: (i, h)),
                  pl.BlockSpec((tt, HEAD_DIM), lambda i, h: (i, zb + h)),
                  pl.BlockSpec((1, HEAD_DIM), lambda i, h: (0, 0))],
        out_specs=pl.BlockSpec((tt, HEAD_DIM), lambda i, h: (i, h)),
        out_shape=jax.ShapeDtypeStruct((t, heads * HEAD_DIM), BF16),
        compiler_params=_params("parallel", "parallel"),
    )(o, proj, norm_w)


def _gdn_post_backward(dcat, o, proj, z_col0, norm_w, heads, tt):
    t = o.shape[0]
    zb = z_col0 // HEAD_DIM

    def body(d_ref, o_ref, z_ref, w_ref, do_ref, dz_ref, dw_ref):
        d = d_ref[...]
        ov = o_ref[...]
        z = z_ref[...]
        w = w_ref[...]
        rms = lax.rsqrt(jnp.mean(ov * ov, axis=-1, keepdims=True) + NORM_EPS)
        ohat = ov * rms
        sg = _sigmoid(z)
        gate = z * sg
        dz_ref[...] = (d * ohat * w * (sg * (1.0 + z * (1.0 - sg)))).astype(BF16)
        don = d * gate
        dohat = don * w
        do_ref[...] = rms * (dohat - ohat * jnp.mean(dohat * ohat, axis=-1, keepdims=True))
        dw = jnp.sum(don * ohat, axis=0, keepdims=True)
        first = jnp.logical_and(pl.program_id(0) == 0, pl.program_id(1) == 0)

        @pl.when(first)
        def _():
            dw_ref[...] = dw

        @pl.when(jnp.logical_not(first))
        def _():
            dw_ref[...] += dw

    blk = pl.BlockSpec((tt, HEAD_DIM), lambda i, h: (i, h))
    return pl.pallas_call(
        body, name="gdn_post_bwd", grid=(t // tt, heads),
        in_specs=[blk, blk, pl.BlockSpec((tt, HEAD_DIM), lambda i, h: (i, zb + h)),
                  pl.BlockSpec((1, HEAD_DIM), lambda i, h: (0, 0))],
        out_specs=[blk, blk, pl.BlockSpec((1, HEAD_DIM), lambda i, h: (0, 0))],
        out_shape=[jax.ShapeDtypeStruct((t, heads * HEAD_DIM), F32),
                   jax.ShapeDtypeStruct((t, heads * HEAD_DIM), BF16),
                   jax.ShapeDtypeStruct((1, HEAD_DIM), F32)],
        compiler_params=_params("arbitrary", "arbitrary"),
    )(dcat, o, proj, norm_w)


def _pool_select(levels, group):
    out = levels[-1]
    for gi in range(len(levels) - 2, -1, -1):
        out = jnp.where(group == gi, levels[gi], out)
    return out


def _pool_counts(t, width, group):
    pos = lax.broadcasted_iota(jnp.int32, (t, width), 0)
    win = jnp.left_shift(2, group)
    return jnp.minimum(pos + 1, win).astype(F32)


def _pooled(p, group):
    levels, s, step = [], p, 1
    for _ in POOL_WINDOWS:
        s = s + _shift_down(s, step)
        levels.append(s)
        step *= 2
    cnt = _pool_counts(p.shape[0], p.shape[1], group)
    return _pool_select(levels, group) / cnt - p, cnt


def _pool_forward(proj, p_col0, pool_w, pool_scale):
    t = proj.shape[0]
    groups, cg, _ = pool_w.shape
    pb = p_col0 // cg

    def body(p_ref, w_ref, s_ref, o_ref):
        pooled, _ = _pooled(p_ref[...], pl.program_id(0))
        o_ref[...] = (_bdot(pooled, w_ref[0]) * s_ref[...]).astype(BF16)

    return pl.pallas_call(
        body, name="pool_fwd", grid=(groups,),
        in_specs=[pl.BlockSpec((t, cg), lambda g: (0, pb + g)), pl.BlockSpec((1, cg, cg), lambda g: (g, 0, 0)),
                  pl.BlockSpec((1, cg), lambda g: (0, g))],
        out_specs=pl.BlockSpec((t, cg), lambda g: (0, g)),
        out_shape=jax.ShapeDtypeStruct((t, groups * cg), BF16),
        compiler_params=_params("parallel"),
    )(proj, pool_w, pool_scale)


def _pool_backward(dcat, d_col0, proj, p_col0, pool_w, pool_scale):
    t = proj.shape[0]
    groups, cg, _ = pool_w.shape
    pb = p_col0 // cg
    db = d_col0 // cg

    def body(d_ref, p_ref, w_ref, s_ref, dp_ref, dw_ref, ds_ref):
        group = pl.program_id(0)
        pooled, cnt = _pooled(p_ref[...], group)
        w = w_ref[0]
        d = d_ref[...]
        mixed = _bdot(pooled, w)
        ds_ref[...] = jnp.sum(d * mixed, axis=0, keepdims=True)
        dmixed = d * s_ref[...]
        dw_ref[0] = _bdot(pooled, dmixed, ta=True)
        dpooled = _bdot(dmixed, w, tb=True)
        levels, s, step = [], dpooled / cnt, 1
        for _ in POOL_WINDOWS:
            s = s + _shift_up(s, step)
            levels.append(s)
            step *= 2
        dp_ref[...] = (_pool_select(levels, group) - dpooled).astype(BF16)

    return pl.pallas_call(
        body, name="pool_bwd", grid=(groups,),
        in_specs=[pl.BlockSpec((t, cg), lambda g: (0, db + g)), pl.BlockSpec((t, cg), lambda g: (0, pb + g)),
                  pl.BlockSpec((1, cg, cg), lambda g: (g, 0, 0)), pl.BlockSpec((1, cg), lambda g: (0, g))],
        out_specs=[pl.BlockSpec((t, cg), lambda g: (0, g)), pl.BlockSpec((1, cg, cg), lambda g: (g, 0, 0)),
                   pl.BlockSpec((1, cg), lambda g: (0, g))],
        out_shape=[jax.ShapeDtypeStruct((t, groups * cg), BF16), jax.ShapeDtypeStruct((groups, cg, cg), F32),
                   jax.ShapeDtypeStruct((1, groups * cg), F32)],
        compiler_params=_params("parallel"),
    )(dcat, proj, pool_w, pool_scale)


def _attention(q, k, v, tq):
    t, d = q.shape
    m = k.shape[0]
    dh = d // XATTN_HEADS
    scale = dh ** -0.5

    def body(q_ref, k_ref, v_ref, o_ref):
        s = _bdot(q_ref[...], k_ref[...], tb=True) * scale
        s = s - jnp.max(s, axis=-1, keepdims=True)
        e = jnp.exp(s)
        p = e / jnp.sum(e, axis=-1, keepdims=True)
        o_ref[...] = _bdot(p, v_ref[...]).astype(BF16)

    return pl.pallas_call(
        body, name="xattn_fwd", grid=(XATTN_HEADS, t // tq),
        in_specs=[pl.BlockSpec((tq, dh), lambda h, i: (i, h)), pl.BlockSpec((m, dh), lambda h, i: (0, h)),
                  pl.BlockSpec((m, dh), lambda h, i: (0, h))],
        out_specs=pl.BlockSpec((tq, dh), lambda h, i: (i, h)),
        out_shape=jax.ShapeDtypeStruct((t, d), BF16),
        compiler_params=_params("parallel", "parallel"),
    )(q, k, v)


def _attention_backward(q, k, v, do, tq):
    t, d = q.shape
    m = k.shape[0]
    dh = d // XATTN_HEADS
    scale = dh ** -0.5

    def body(q_ref, k_ref, v_ref, do_ref, dq_ref, dk_ref, dv_ref, dk_acc, dv_acc):
        i = pl.program_id(1)
        qv, kv, vv, dov = q_ref[...], k_ref[...], v_ref[...], do_ref[...]
        s = _bdot(qv, kv, tb=True) * scale
        s = s - jnp.max(s, axis=-1, keepdims=True)
        e = jnp.exp(s)
        p = e / jnp.sum(e, axis=-1, keepdims=True)
        dp = _bdot(dov, vv, tb=True)
        ds = p * (dp - jnp.sum(dp * p, axis=-1, keepdims=True)) * scale
        dq_ref[...] = _bdot(ds, kv).astype(BF16)
        dv_part = _bdot(p, dov, ta=True)
        dk_part = _bdot(ds, qv, ta=True)

        @pl.when(i == 0)
        def _():
            dk_acc[...] = dk_part
            dv_acc[...] = dv_part

        @pl.when(i > 0)
        def _():
            dk_acc[...] += dk_part
            dv_acc[...] += dv_part

        @pl.when(i == pl.num_programs(1) - 1)
        def _():
            dk_ref[...] = dk_acc[...].astype(BF16)
            dv_ref[...] = dv_acc[...].astype(BF16)

    qblk = pl.BlockSpec((tq, dh), lambda h, i: (i, h))
    kblk = pl.BlockSpec((m, dh), lambda h, i: (0, h))
    return pl.pallas_call(
        body, name="xattn_bwd", grid=(XATTN_HEADS, t // tq),
        in_specs=[qblk, kblk, kblk, qblk],
        out_specs=[qblk, kblk, kblk],
        out_shape=[jax.ShapeDtypeStruct((t, d), BF16), jax.ShapeDtypeStruct((m, d), BF16),
                   jax.ShapeDtypeStruct((m, d), BF16)],
        scratch_shapes=[pltpu.VMEM((m, dh), F32), pltpu.VMEM((m, dh), F32)],
        compiler_params=_params("parallel", "arbitrary"),
    )(q, k, v, do)


def _ln_backward_rows(name, dmain, dres, xhat, rstd, gamma, tm):
    t, d = xhat.shape

    def body(m_ref, r_ref, x_ref, s_ref, g_ref, du_ref, dub_ref, dg_ref, db_ref):
        du, dg, db = _ln_backward_math(m_ref[...] + ALPHA * r_ref[...], x_ref[...], s_ref[...], g_ref[...])
        du_ref[...] = du
        dub_ref[...] = du.astype(BF16)
        first = pl.program_id(0) == 0

        @pl.when(first)
        def _():
            dg_ref[...] = dg
            db_ref[...] = db

        @pl.when(jnp.logical_not(first))
        def _():
            dg_ref[...] += dg
            db_ref[...] += db

    row = pl.BlockSpec((tm, d), lambda i: (i, 0))
    vec = pl.BlockSpec((1, d), lambda i: (0, 0))
    return pl.pallas_call(
        body, name=name, grid=(t // tm,),
        in_specs=[row, row, row, pl.BlockSpec((tm, 1), lambda i: (i, 0)), vec],
        out_specs=[row, row, vec, vec],
        out_shape=[jax.ShapeDtypeStruct((t, d), F32), jax.ShapeDtypeStruct((t, d), BF16),
                   jax.ShapeDtypeStruct((1, d), F32), jax.ShapeDtypeStruct((1, d), F32)],
        compiler_params=_params("arbitrary"),
    )(dmain, dres, xhat, rstd, gamma)


def _loss_and_ln_backward(xhat, rstd, gamma, beta, target, tm):
    t, d = xhat.shape

    def body(x_ref, r_ref, g_ref, b_ref, t_ref, du_ref, dub_ref, dg_ref, db_ref, loss_ref):
        xh = x_ref[...]
        g = g_ref[...]
        diff = xh * g + b_ref[...] - t_ref[...]
        part = jnp.sum(jnp.sum(diff * diff, axis=1, keepdims=True), axis=0, keepdims=True) * (0.5 / d)
        dy = diff * (1.0 / d)
        du, dg, db = _ln_backward_math(dy, xh, r_ref[...], g)
        du_ref[...] = du
        dub_ref[...] = du.astype(BF16)
        lossrow = jnp.broadcast_to(part, (1, HEAD_DIM))
        first = pl.program_id(0) == 0

        @pl.when(first)
        def _():
            dg_ref[...] = dg
            db_ref[...] = db
            loss_ref[...] = lossrow

        @pl.when(jnp.logical_not(first))
        def _():
            dg_ref[...] += dg
            db_ref[...] += db
            loss_ref[...] += lossrow

    row = pl.BlockSpec((tm, d), lambda i: (i, 0))
    vec = pl.BlockSpec((1, d), lambda i: (0, 0))
    return pl.pallas_call(
        body, name="loss_ln3_bwd", grid=(t // tm,),
        in_specs=[row, pl.BlockSpec((tm, 1), lambda i: (i, 0)), vec, vec, row],
        out_specs=[row, row, vec, vec, pl.BlockSpec((1, HEAD_DIM), lambda i: (0, 0))],
        out_shape=[jax.ShapeDtypeStruct((t, d), F32), jax.ShapeDtypeStruct((t, d), BF16),
                   jax.ShapeDtypeStruct((1, d), F32), jax.ShapeDtypeStruct((1, d), F32),
                   jax.ShapeDtypeStruct((1, HEAD_DIM), F32)],
        compiler_params=_params("arbitrary"),
    )(xhat, rstd, gamma, beta, target)


def _after(token, a):
    return a if token is None else a + token[:1, :1].astype(a.dtype)


def _pick(n, prefs):
    for p in prefs:
        if n % p == 0:
            return p
    return n


def _local_step(x, mem, target, w, x_bf=None):
    t, d = x.shape
    heads = w["a_log"].shape[1]
    gw = heads * HEAD_DIM
    groups, cg, _ = w["pool_w"].shape
    pw = groups * cg
    n_main = 4 * gw + pw
    in_cols = n_main + 2 * heads
    s_in = w["w_in_t"].shape[0]

    tm = _pick(t, (512, 256, 128))
    tm_ln = _pick(t, (256, 128))
    tm_big = _pick(t, (1024, 512, 256, 128))
    tk = _pick(d, K_STEPS)

    w_in_t = w["w_in_t"].reshape(in_cols, d)
    w_p_t = w_in_t[4 * gw + 2 * heads:]
    w_ba_t = jnp.pad(w_in_t[4 * gw:4 * gw + 2 * heads], ((0, HEAD_DIM - 2 * heads), (0, 0)))
    x_bf = x.astype(BF16) if x_bf is None else x_bf
    mem_bf = mem.astype(BF16)

    tn_d = _pick(d, (1024, 512, 256, 128))
    proj = _plain("proj_main", x_bf, w_in_t, tb=True, n_used=4 * gw, tm=tm_big, tn=_pick(4 * gw, (1024, 512, 256, 128)),
                  tk=tk, out_dtype=F32)
    pproj = _plain("proj_pool", x_bf, w_p_t, tb=True, tm=tm_big, tn=_pick(pw, (1024, 512, 256, 128)), tk=tk, out_dtype=F32)
    ea, dtb = _gate_vectors(w["a_log"], w["dt_bias"], heads)
    vec128 = lambda i, j: (0, 0)
    ba, bg = _matmul(
        "proj_gates", x_bf, w_ba_t, tb=True, tm=tm, tn=HEAD_DIM, tk=tk,
        extra=[(ea, (1, HEAD_DIM), vec128), (dtb, (1, HEAD_DIM), vec128)],
        outs=[(jax.ShapeDtypeStruct((t, HEAD_DIM), F32), (tm, HEAD_DIM), _tile)] * 2,
        epilogue=_gates_epilogue(heads))
    qkv = _gdn_pre(proj, w["conv_w"], heads)
    o_gdn, states, *solved = _gdn_core(qkv, bg, heads)
    cat_g = _gdn_post(o_gdn, proj, 3 * gw, w["gdn_norm_w"], heads, tm_big)
    token = yield ("pass", 1, cat_g)
    cat_p = _pool_forward(pproj, 0, w["pool_w"], _after(token, w["pool_scale"]))
    cat = jnp.concatenate([cat_g, cat_p], axis=1)
    w = {**w, **(yield ("weights", 1, cat))}
    h1, h1_bf, xhat1, rstd1 = _ln_forward("mix_ln1", cat, w["w_out"], x, w["ln1_g"], w["ln1_b"], tm=tm_ln, tk=tk)

    h1_bf = _after((yield ("relay", None, h1_bf)), h1_bf)
    q = _plain("xattn_q", h1_bf, w["xq_w"], tm=tm, tn=tn_d, tk=tk, out_dtype=BF16)
    mlen = mem.shape[0]
    tm_mem = _pick(mlen, (256, 128))
    k = _plain("xattn_k", mem_bf, w["xk_w"], tm=tm_mem, tn=tn_d, tk=tk, out_dtype=BF16)
    v = _plain("xattn_v", mem_bf, w["xv_w"], tm=tm_mem, tn=tn_d, tk=tk, out_dtype=BF16)
    att = _attention(q, k, v, tm)
    token = yield ("pass", 2, att)
    h2, h2_bf, xhat2, rstd2 = _ln_forward("xo_ln2", att, w["xo_w"], h1, _after(token, w["ln2_g"]), w["ln2_b"],
                                          tm=tm_ln, tk=tk)

    w = {**w, **(yield ("weights", 2, h2_bf))}
    s_up = w["w_up3"].shape[0]
    ff = s_up * w["w_up3"].shape[2]
    tn_f = _pick(ff // s_up, (1024, 512, 256, 128))

    def up_epi(acc, ex, out, i):
        r = jnp.maximum(acc, 0.0)
        out[0][...] = (r * r).astype(BF16)
        out[1][...] = (2.0 * r).astype(BF16)

    act, act_grad = _matmul(
        "mlp_up", h2_bf, w["w_up3"], b_blocks=s_up, tm=tm_big, tn=tn_f, tk=tk,
        outs=[(jax.ShapeDtypeStruct((t, ff), BF16), (tm_big, tn_f), _tile)] * 2, epilogue=up_epi)
    w = {**w, **(yield ("weights", 3, act))}
    tk_f = _pick(ff, K_STEPS)
    xhat3, rstd3 = _ln_forward("down_ln3", act, w["w_down"], h2, w["ln3_g"], w["ln3_b"], tm=tm, tk=tk_f, want_h=False)

    grads = {}
    du3, du3_bf, grads["ln3_g"], grads["ln3_b"], loss = _loss_and_ln_backward(
        xhat3, rstd3, w["ln3_g"], w["ln3_b"], target, tm)

    def dup_epi(acc, ex, out, i):
        out[0][...] = (acc * ex[0][...].astype(F32)).astype(BF16)

    dup = _matmul(
        "mlp_down_dx", du3_bf, w["w_down"], tb=True, tm=tm_big, tn=tn_f, tk=tk,
        extra=[(act_grad, (tm_big, tn_f), _tile)],
        outs=[(jax.ShapeDtypeStruct((t, ff), BF16), (tm_big, tn_f), _tile)], epilogue=dup_epi)[0]
    tk_t = _pick(t, K_STEPS)
    tm_w = _pick(d, (512, 256, 128))
    grads["w_down"] = _plain("mlp_down_dw", act, du3_bf, ta=True, tm=_pick(ff, (512, 256, 128)), tn=d, tk=tk_t,
                             out_dtype=F32)
    grads["w_up3"] = _plain("mlp_up_dw", h2_bf, dup, ta=True, tm=tm_w, tn=ff // s_up, tk=tk_t, out_dtype=F32, out3=s_up,
                            n_outer=True)
    token = yield ("grads", 0, {n: grads.pop(n) for n in ("w_down", "w_up3")})
    dh2 = _plain("mlp_up_dx", dup, w["w_up3"], tb=True, b_blocks=s_up, tm=tm_big, tn=tn_d,
                 tk=_pick(ff // s_up, K_STEPS), out_dtype=F32)
    du2, du2_bf, grads["ln2_g"], grads["ln2_b"] = _ln_backward_rows(
        "ln2_bwd", dh2, du3, xhat2, rstd2, _after(token, w["ln2_g"]), tm)
    token = yield ("poll", 0, du2_bf)

    grads["xo_w"] = _plain("xo_dw", att, du2_bf, ta=True, tm=tm_w, tn=d, tk=tk_t, out_dtype=F32)
    datt = _plain("xo_dx", du2_bf, w["xo_w"], tb=True, tm=tm, tn=tn_d, tk=tk, out_dtype=BF16)
    dq, dk, dv = _attention_backward(q, k, v, datt, tm)
    tk_m = _pick(mlen, (256, 128))
    grads["xq_w"] = _plain("xq_dw", h1_bf, dq, ta=True, tm=tm_w, tn=d, tk=tk_t, out_dtype=F32)
    grads["xk_w"] = _plain("xk_dw", mem_bf, dk, ta=True, tm=tm_w, tn=tn_d, tk=tk_m, out_dtype=F32)
    grads["xv_w"] = _plain("xv_dw", mem_bf, dv, ta=True, tm=tm_w, tn=tn_d, tk=tk_m, out_dtype=F32)
    du1, du1_bf, grads["ln1_g"], grads["ln1_b"] = _ln_backward(
        "xq_dx_ln1", dq, w["xq_w"], du2, xhat1, rstd1, _after(token, w["ln1_g"]), tm=tm_ln, tk=tk)

    grads["w_out"] = _plain("out_dw", cat, du1_bf, ta=True, tm=tm_w, tn=d, tk=tk_t, out_dtype=F32)
    token = yield ("grads", 1, {n: grads.pop(n) for n in ("xo_w", "xq_w", "xk_w", "xv_w", "w_out")})
    dcat = _plain("out_dx", du1_bf, w["w_out"], tb=True, tm=tm, tn=tn_d, tk=tk, out_dtype=F32)
    dp, grads["pool_w"], grads["pool_scale"] = _pool_backward(dcat, gw, pproj, 0, w["pool_w"],
                                                              _after(token, w["pool_scale"]))
    do_gdn, dz, grads["gdn_norm_w"] = _gdn_post_backward(dcat, o_gdn, proj, 3 * gw, _after(token, w["gdn_norm_w"]),
                                                         heads, tm_big)
    dqkv, dbg = _gdn_core_backward(qkv, bg, states, solved, do_gdn, heads)
    token = yield ("poll", 1, dqkv)
    dqkv_pre, grads["conv_w"] = _gdn_pre_backward(proj, _after(token, w["conv_w"]), dqkv, heads)
    dba, dalog_row, ddt_row = _gates_backward(ba, bg, dbg, ea, dtb, heads)
    grads["a_log"] = dalog_row[:, heads:2 * heads]
    grads["dt_bias"] = ddt_row[:, heads:2 * heads]

    k_pad = -(-in_cols // (2 * HEAD_DIM)) * (2 * HEAD_DIM)
    dproj = jnp.concatenate([dqkv_pre, dz, dba[:, :2 * heads], dp, jnp.zeros((t, k_pad - in_cols), BF16)], axis=1)
    dw_in_t = _plain("proj_dw", dproj, x_bf, ta=True, tm=_pick(k_pad, (512, 256, 128)), tn=d, tk=tk_t, out_dtype=F32,
                     m_kept=in_cols)
    grads["w_in_t"] = dw_in_t.reshape(s_in, in_cols // s_in, d)

    def dx_epi(acc, ex, out, i):
        out[0][...] = acc + ALPHA * ex[0][...]

    token = yield ("grads", 2, {n: grads.pop(n) for n in ("w_in_t", "pool_w")})
    w_in_t_pad = jnp.concatenate([w_in_t, _after(token, jnp.zeros((k_pad - in_cols, d), BF16))], axis=0)
    grad_x = _matmul(
        "proj_dx", dproj, w_in_t_pad, tm=tm, tn=tn_d, tk=k_pad, extra=[(du1, (tm, tn_d), _tile)],
        outs=[(jax.ShapeDtypeStruct((t, d), F32), (tm, tn_d), _tile)], epilogue=dx_epi)[0]
    yield ("poll", 2, grad_x)
    return loss, grad_x, grads


def _adamw(name, w, g, m, v):
    r, c = w.shape
    if r % 8 == 0:
        tr = _pick(r, (256, 128, 64, 32, 16, 8))
        blk, steps = pl.BlockSpec((tr, c), lambda i: (i, 0)), r // tr
    else:
        tc = _pick(c, (256, 128))
        blk, steps = pl.BlockSpec((r, tc), lambda i: (0, i)), c // tc
    c1 = 1.0 - ADAM_B1 ** ADAM_STEP
    c2 = 1.0 - ADAM_B2 ** ADAM_STEP

    def body(w_ref, g_ref, m_ref, v_ref, d_ref, mo_ref, vo_ref, go_ref):
        gv = g_ref[...]
        mn = ADAM_B1 * m_ref[...] + (1.0 - ADAM_B1) * gv
        vn = ADAM_B2 * v_ref[...] + (1.0 - ADAM_B2) * (gv * gv)
        d_ref[...] = -ADAM_LR * ((mn / c1) / (jnp.sqrt(vn / c2) + ADAM_EPS) + ADAM_WD * w_ref[...])
        mo_ref[...] = mn
        vo_ref[...] = vn
        go_ref[...] = gv

    return pl.pallas_call(
        body, name=name, grid=(steps,), in_specs=[blk] * 4, out_specs=[blk] * 4,
        out_shape=[jax.ShapeDtypeStruct((r, c), F32)] * 4,
        compiler_params=_params("parallel"),
    )(w, g, m, v)


def _place():
    x, y, c = lax.axis_index("x"), lax.axis_index("y"), lax.axis_index("c")
    chips = [(1 - x, y), (x, 1 - y), (1 - x, 1 - y)]
    return x, y, c, chips


HBM = pl.BlockSpec(memory_space=pltpu.HBM)


SEM = pl.BlockSpec(memory_space=pltpu.SEMAPHORE)
ANY = pl.BlockSpec(memory_space=pl.ANY)
EFFECT = pltpu.SideEffectType.DATAFLOW_SIDE_EFFECTING


def _in_hbm(a):
    return pltpu.with_memory_space_constraint(a, pltpu.HBM)


def _remote(src, dst, send_sem, recv_sem, to):
    return pltpu.make_async_remote_copy(src_ref=src, dst_ref=dst, send_sem=send_sem, recv_sem=recv_sem,
                                        device_id=to, device_id_type=MESH)


def _by_rows(rows):
    return rows % 32 == 0


def _half_shape(rows, cols):
    return (rows // 2, cols) if _by_rows(rows) else (rows, cols // 2)


def _half(ref, which, *lead):
    rows, cols = ref.shape[-2:]
    if _by_rows(rows):
        return ref.at[(*lead, pl.ds(which * (rows // 2), rows // 2))]
    return ref.at[(*lead, slice(None), pl.ds(which * (cols // 2), cols // 2))]


def _landed(lands, i, shard_index, which):
    return _half(lands[i], which, shard_index)


def _routes():
    x, y, c, _ = _place()
    first = (jnp.where(c == 0, 1 - x, x), jnp.where(c == 0, y, 1 - y))
    second = (jnp.where(c == 0, x, 1 - x), jnp.where(c == 0, 1 - y, y))
    return first, second, (1 - x, 1 - y)


def _shard_of(chip):
    return 2 * chip[0] + chip[1]


def _gather_start(name, shards, after, relayed=()):
    n = len(shards)
    lands = [lax.empty((N_SHARD,) + s.shape, s.dtype) for s in shards]

    def body(*refs):
        ins, zones = refs[:n], refs[n:2 * n]
        ici_send, ici_recv, own_send, own_recv = refs[2 * n + 1:2 * n + 5]
        token = refs[-1]
        x, y, c, chips = _place()
        me = 2 * x + y
        first, _, _ = _routes()
        for i in range(n):
            if i in relayed:
                _remote(_half(ins[i], c), _landed(zones, i, me, c), ici_send.at[3 * i], ici_recv.at[3 * i],
                        (*first, c)).start()
                continue
            for j, chip in enumerate(chips):
                _remote(_half(ins[i], c), _landed(zones, i, me, c), ici_send.at[3 * i + j],
                        ici_recv.at[3 * i + j], (*chip, c)).start()
        for i in range(n):
            _remote(ins[i], zones[i].at[me], own_send.at[i], own_recv.at[i], (x, y, 1 - c)).start()
        token[...] = jnp.zeros_like(token)

    dma = pltpu.SemaphoreType.DMA
    outs = pl.pallas_call(
        body, name=name,
        in_specs=[HBM] * (2 * n) + [ANY],
        out_shape=(dma((3 * n,)), dma((3 * n,)), dma((n,)), dma((n,)),
                   *[pltpu.HBM(a.shape, a.dtype) for a in shards + lands], jax.ShapeDtypeStruct((8, LANES), F32)),
        out_specs=(SEM, SEM, SEM, SEM, *[HBM] * (2 * n), pl.BlockSpec(memory_space=pltpu.VMEM)),
        input_output_aliases={k: 4 + k for k in range(2 * n)},
        compiler_params=pltpu.CompilerParams(has_side_effects=EFFECT),
    )(*[_in_hbm(a) for a in shards + lands], after)
    sems = dict(zip(("ici_send", "ici_recv", "own_send", "own_recv"), outs[:4]))
    return sems, list(outs[4:4 + n]), list(outs[4 + n:4 + 2 * n]), outs[-1]


def _gather_forward(name, idx, lands, sems, after):
    n = len(idx)

    def body(*refs):
        zones = refs[:n]
        ici_recv = refs[n]
        fwd_send, fwd_recv = refs[n + 2], refs[n + 3]
        x, y, c, chips = _place()
        for k, i in enumerate(idx):
            for j, chip in enumerate(chips):
                half = _landed(zones, k, 2 * chip[0] + chip[1], c)
                _remote(half, half, fwd_send.at[3 * k + j], ici_recv.at[3 * i + j], (*chip, c)).wait_recv()
                _remote(half, half, fwd_send.at[3 * k + j], fwd_recv.at[3 * k + j], (x, y, 1 - c)).start()
        refs[-1][...] = jnp.zeros_like(refs[-1])

    dma = pltpu.SemaphoreType.DMA
    outs = pl.pallas_call(
        body, name=name,
        in_specs=[HBM] * n + [SEM, ANY],
        out_shape=(dma((3 * n,)), dma((3 * n,)), *[pltpu.HBM(a.shape, a.dtype) for a in lands],
                   jax.ShapeDtypeStruct((8, LANES), F32)),
        out_specs=(SEM, SEM, *[HBM] * n, pl.BlockSpec(memory_space=pltpu.VMEM)),
        input_output_aliases={k: 2 + k for k in range(n)},
        compiler_params=pltpu.CompilerParams(has_side_effects=EFFECT),
    )(*lands, sems["ici_recv"], after)
    return (outs[0], outs[1]), list(outs[2:2 + n]), outs[-1]


def _gather_wait(name, idx, shards, lands, sems, fwd, after):
    n = len(idx)

    def body(*refs):
        ins, zones = refs[:n], refs[n:2 * n]
        ici_send, own_send, own_recv, fwd_send, fwd_recv = refs[2 * n:2 * n + 5]
        x, y, c, chips = _place()
        me = 2 * x + y
        for k, i in enumerate(idx):
            mine = _half(ins[k], c)
            for j, chip in enumerate(chips):
                theirs = 2 * chip[0] + chip[1]
                _remote(mine, _landed(zones, k, me, c), ici_send.at[3 * i + j], fwd_recv.at[3 * k + j],
                        (*chip, c)).wait_send()
                sent = _landed(zones, k, theirs, c)
                _remote(sent, sent, fwd_send.at[3 * k + j], fwd_recv.at[3 * k + j], (x, y, 1 - c)).wait_send()
                passed = _landed(zones, k, theirs, 1 - c)
                _remote(passed, passed, fwd_send.at[3 * k + j], fwd_recv.at[3 * k + j], (x, y, 1 - c)).wait_recv()
            own = _remote(ins[k], zones[k].at[me], own_send.at[i], own_recv.at[i], (x, y, 1 - c))
            own.wait_send()
            own.wait_recv()

    outs = pl.pallas_call(
        body, name=name,
        in_specs=[HBM] * (2 * n) + [SEM] * 5 + [ANY],
        out_shape=tuple(pltpu.HBM(a.shape, a.dtype) for a in lands),
        out_specs=tuple([HBM] * n),
        input_output_aliases={n + k: k for k in range(n)},
        compiler_params=pltpu.CompilerParams(has_side_effects=EFFECT),
    )(*shards, *lands, sems["ici_send"], sems["own_send"], sems["own_recv"], fwd[0], fwd[1], after)
    return list(outs)


def _gather_relay(name, idx, shards, lands, sems, after):
    n = len(idx)

    def body(*refs):
        ins, zones, ici_recv = refs[:n], refs[n:2 * n], refs[2 * n]
        relay_send, relay_recv, pass_send, pass_recv = refs[2 * n + 2:2 * n + 6]
        x, y, c, _ = _place()
        first, second, _ = _routes()
        for k, i in enumerate(idx):
            landed = _landed(zones, k, _shard_of(first), c)
            _remote(landed, landed, pass_send.at[k], ici_recv.at[3 * i], (*first, c)).wait_recv()
            _remote(_half(ins[k], c), _landed(zones, k, 2 * x + y, c), relay_send.at[2 * k], relay_recv.at[2 * k],
                    (*second, c)).start()
            _remote(landed, landed, relay_send.at[2 * k + 1], relay_recv.at[2 * k + 1], (*second, c)).start()
            _remote(landed, landed, pass_send.at[k], pass_recv.at[k], (x, y, 1 - c)).start()
        refs[-1][...] = jnp.zeros_like(refs[-1])

    dma = pltpu.SemaphoreType.DMA
    outs = pl.pallas_call(
        body, name=name,
        in_specs=[HBM] * (2 * n) + [SEM, ANY],
        out_shape=(dma((2 * n,)), dma((2 * n,)), dma((n,)), dma((n,)), *[pltpu.HBM(a.shape, a.dtype) for a in lands],
                   jax.ShapeDtypeStruct((8, LANES), F32)),
        out_specs=(SEM, SEM, SEM, SEM, *[HBM] * n, pl.BlockSpec(memory_space=pltpu.VMEM)),
        input_output_aliases={n + k: 4 + k for k in range(n)},
        compiler_params=pltpu.CompilerParams(has_side_effects=EFFECT),
    )(*shards, *lands, sems["ici_recv"], after)
    return outs[:4], list(outs[4:4 + n]), outs[-1]


def _gather_forward_relayed(name, ks, lands, relay, after):
    n = len(ks)

    def body(*refs):
        zones, relay_recv = refs[:n], refs[n]
        fwd_send, fwd_recv = refs[n + 2], refs[n + 3]
        x, y, c, _ = _place()
        _, second, diagonal = _routes()
        for p, k in enumerate(ks):
            for j, chip in enumerate((second, diagonal)):
                landed = _landed(zones, p, _shard_of(chip), c)
                _remote(landed, landed, fwd_send.at[2 * p + j], relay_recv.at[2 * k + j], (*second, c)).wait_recv()
                _remote(landed, landed, fwd_send.at[2 * p + j], fwd_recv.at[2 * p + j], (x, y, 1 - c)).start()
        refs[-1][...] = jnp.zeros_like(refs[-1])

    dma = pltpu.SemaphoreType.DMA
    outs = pl.pallas_call(
        body, name=name,
        in_specs=[HBM] * n + [SEM, ANY],
        out_shape=(dma((2 * n,)), dma((2 * n,)), *[pltpu.HBM(a.shape, a.dtype) for a in lands],
                   jax.ShapeDtypeStruct((8, LANES), F32)),
        out_specs=(SEM, SEM, *[HBM] * n, pl.BlockSpec(memory_space=pltpu.VMEM)),
        input_output_aliases={k: 2 + k for k in range(n)},
        compiler_params=pltpu.CompilerParams(has_side_effects=EFFECT),
    )(*lands, relay[1], after)
    return (outs[0], outs[1]), list(outs[2:2 + n]), outs[-1]


def _gather_wait_relayed(name, idx, ks, shards, lands, sems, relay, fwd, after):
    n = len(idx)

    def body(*refs):
        ins, zones = refs[:n], refs[n:2 * n]
        ici_send, own_send, own_recv, relay_send, pass_send, pass_recv, fwd_send, fwd_recv = refs[2 * n:2 * n + 8]
        x, y, c, _ = _place()
        me = 2 * x + y
        sibling = (x, y, 1 - c)
        first, second, diagonal = _routes()
        for p, (i, k) in enumerate(zip(idx, ks)):
            mine, at_peer = _half(ins[p], c), _landed(zones, p, me, c)
            from_first = _landed(zones, p, _shard_of(first), c)
            _remote(mine, at_peer, ici_send.at[3 * i], pass_recv.at[k], (*first, c)).wait_send()
            _remote(mine, at_peer, relay_send.at[2 * k], pass_recv.at[k], (*second, c)).wait_send()
            _remote(from_first, from_first, relay_send.at[2 * k + 1], pass_recv.at[k], (*second, c)).wait_send()
            _remote(from_first, from_first, pass_send.at[k], pass_recv.at[k], sibling).wait_send()
            theirs = _landed(zones, p, _shard_of(second), 1 - c)
            _remote(theirs, theirs, pass_send.at[k], pass_recv.at[k], sibling).wait_recv()
            for j, (sent, got) in enumerate(((second, first), (diagonal, diagonal))):
                out_half = _landed(zones, p, _shard_of(sent), c)
                _remote(out_half, out_half, fwd_send.at[2 * p + j], fwd_recv.at[2 * p + j], sibling).wait_send()
                in_half = _landed(zones, p, _shard_of(got), 1 - c)
                _remote(in_half, in_half, fwd_send.at[2 * p + j], fwd_recv.at[2 * p + j], sibling).wait_recv()
            own = _remote(ins[p], zones[p].at[me], own_send.at[i], own_recv.at[i], sibling)
            own.wait_send()
            own.wait_recv()

    outs = pl.pallas_call(
        body, name=name,
        in_specs=[HBM] * (2 * n) + [SEM] * 8 + [ANY],
        out_shape=tuple(pltpu.HBM(a.shape, a.dtype) for a in lands),
        out_specs=tuple([HBM] * n),
        input_output_aliases={n + k: k for k in range(n)},
        compiler_params=pltpu.CompilerParams(has_side_effects=EFFECT),
    )(*shards, *lands, sems["ici_send"], sems["own_send"], sems["own_recv"], relay[0], relay[2], relay[3],
      fwd[0], fwd[1], after)
    return list(outs)


def _all_reduce_small(name, slab, after=None):
    r, width = slab.shape
    ndev = 8

    def body(x_ref, after_ref, out_ref, buf, send_sems, recv_sems):
        x, y, c, _ = _place()
        me = 4 * x + 2 * y + c
        buf[me] = x_ref[...]
        copies = []
        for k in range(1, ndev):
            peer = jnp.bitwise_xor(me, k)
            to = (peer // 4, (peer // 2) % 2, peer % 2)
            cp = pltpu.make_async_remote_copy(src_ref=x_ref, dst_ref=buf.at[me], send_sem=send_sems.at[k - 1],
                                              recv_sem=recv_sems.at[k - 1], device_id=to, device_id_type=MESH)
            cp.start()
            copies.append(cp)
        for k in range(1, ndev):
            peer = jnp.bitwise_xor(me, k)
            pltpu.make_async_remote_copy(src_ref=x_ref, dst_ref=buf.at[peer], send_sem=send_sems.at[k - 1],
                                         recv_sem=recv_sems.at[k - 1], device_id=(x, y, c),
                                         device_id_type=MESH).wait_recv()
        for cp in copies:
            cp.wait_send()
        total = buf[0]
        for d in range(1, ndev):
            total = total + buf[d]
        out_ref[...] = total

    return pl.pallas_call(
        body, name=name,
        in_specs=[pl.BlockSpec(memory_space=pltpu.VMEM), ANY], out_specs=pl.BlockSpec(memory_space=pltpu.VMEM),
        out_shape=jax.ShapeDtypeStruct((r, width), F32),
        scratch_shapes=[pltpu.VMEM((ndev, r, width), F32), pltpu.SemaphoreType.DMA((ndev - 1,)),
                        pltpu.SemaphoreType.DMA((ndev - 1,))],
        compiler_params=pltpu.CompilerParams(vmem_limit_bytes=VMEM_LIMIT),
    )(slab, slab if after is None else after)


def _half_tiling(rows, cols):
    if _by_rows(rows):
        tr = _pick(rows // 2, (256, 128, 64, 32, 16))
        nb = (rows // 2) // tr
        return (tr, cols), nb, (lambda which, b: (which * nb + b, 0)), (lambda b: (b, 0))
    tc = _pick(cols // 2, (256, 128))
    nb = (cols // 2) // tc
    return (rows, tc), nb, (lambda which, b: (0, which * nb + b)), (lambda b: (0, b))


def _chip_partial(name, grad, other, core):
    s, r, cdim = grad.shape
    blk, nb, whole, within = _half_tiling(r, cdim)

    def body(core_ref, g_ref, o_ref, out_ref):
        out_ref[...] = (g_ref[...] + o_ref[...]).astype(BF16)

    return pl.pallas_call(
        body, name=name,
        grid_spec=pltpu.PrefetchScalarGridSpec(
            num_scalar_prefetch=1, grid=(s, nb),
            in_specs=[pl.BlockSpec((None,) + blk, lambda j, b, core_ref: (j,) + whole(core_ref[0], b)),
                      pl.BlockSpec((None,) + blk, lambda j, b, core_ref: (j,) + within(b))],
            out_specs=pl.BlockSpec((None,) + blk, lambda j, b, core_ref: (j,) + within(b))),
        out_shape=jax.ShapeDtypeStruct((s,) + _half_shape(r, cdim), BF16),
        compiler_params=_params("parallel", "parallel"),
    )(core, grad, other)


def _partial_copies(ins, zones, send_sems, recv_sems):
    x, y, c, chips = _place()
    return [_remote(ins[i].at[2 * chip[0] + chip[1]], zones[i].at[j], send_sems.at[3 * i + j],
                    recv_sems.at[3 * i + j], (*chip, c))
            for i in range(len(ins)) for j, chip in enumerate(chips)]


def _swap_copies(ins, zones, send_sems, recv_sems):
    x, y, c, _ = _place()
    copies = []
    for i in range(len(ins)):
        for s in range(N_SHARD):
            copies.append(_remote(_half(ins[i], 1 - c, s), zones[i].at[s],
                                  send_sems.at[N_SHARD * i + s], recv_sems.at[N_SHARD * i + s], (x, y, 1 - c)))
    return copies


def _exchange_start(name, plan, sources, lands, per_array):
    n = len(sources)
    lands = [lax.empty(shape, dtype) for shape, dtype in lands]

    def body(*refs):
        for cp in plan(refs[:n], refs[n:2 * n], refs[2 * n], refs[2 * n + 1]):
            cp.start()
        refs[-1][...] = jnp.zeros_like(refs[-1])

    dma = pltpu.SemaphoreType.DMA
    outs = pl.pallas_call(
        body, name=name,
        in_specs=[HBM] * (2 * n),
        out_shape=(dma((per_array * n,)), dma((per_array * n,)),
                   *[pltpu.HBM(a.shape, a.dtype) for a in list(sources) + lands], jax.ShapeDtypeStruct((8, LANES), F32)),
        out_specs=(SEM, SEM, *[HBM] * (2 * n), pl.BlockSpec(memory_space=pltpu.VMEM)),
        input_output_aliases={k: 2 + k for k in range(2 * n)},
        compiler_params=pltpu.CompilerParams(has_side_effects=EFFECT),
    )(*[_in_hbm(a) for a in list(sources) + lands])
    return (outs[0], outs[1]), list(outs[2:2 + n]), list(outs[2 + n:2 + 2 * n]), outs[-1]


def _exchange_wait(name, plan, started, after):
    sems, partials, lands, _ = started
    n = len(partials)

    def body(*refs):
        for cp in plan(refs[:n], refs[n:2 * n], refs[2 * n], refs[2 * n + 1]):
            cp.wait_send()
            cp.wait_recv()

    outs = pl.pallas_call(
        body, name=name,
        in_specs=[HBM] * (2 * n) + [SEM, SEM] + [ANY] * len(after),
        out_shape=tuple(pltpu.HBM(a.shape, a.dtype) for a in lands),
        out_specs=tuple([HBM] * n),
        input_output_aliases={n + k: k for k in range(n)},
        compiler_params=pltpu.CompilerParams(has_side_effects=EFFECT),
    )(*partials, *lands, sems[0], sems[1], *after)
    return list(outs)


def _reduce_own(name, grad, other, received, where):
    s, r, cdim = grad.shape
    blk, nb, whole, within = _half_tiling(r, cdim)

    def body(where_ref, g_ref, o_ref, r_ref, out_ref):
        total = g_ref[...] + o_ref[...]
        for j in range(3):
            total = total + r_ref[j].astype(F32)
        out_ref[...] = total

    return pl.pallas_call(
        body, name=name,
        grid_spec=pltpu.PrefetchScalarGridSpec(
            num_scalar_prefetch=1, grid=(nb,),
            in_specs=[pl.BlockSpec((None,) + blk, lambda b, w_ref: (w_ref[0],) + whole(w_ref[1], b)),
                      pl.BlockSpec((None,) + blk, lambda b, w_ref: (w_ref[0],) + within(b)),
                      pl.BlockSpec((3,) + blk, lambda b, w_ref: (0,) + within(b))],
            out_specs=pl.BlockSpec(blk, lambda b, w_ref: whole(w_ref[1], b))),
        out_shape=jax.ShapeDtypeStruct((r, cdim), F32),
        compiler_params=_params("parallel"),
    )(where, grad, other, received)


def _join_start(name, halves):
    n = len(halves)

    def body(*refs):
        bufs, send_sems, recv_sems = refs[:n], refs[n], refs[n + 1]
        x, y, c, _ = _place()
        for i in range(n):
            mine = _half(bufs[i], c)
            _remote(mine, mine, send_sems.at[i], recv_sems.at[i], (x, y, 1 - c)).start()
        refs[-1][...] = jnp.zeros_like(refs[-1])

    dma = pltpu.SemaphoreType.DMA
    outs = pl.pallas_call(
        body, name=name,
        in_specs=[HBM] * n,
        out_shape=(dma((n,)), dma((n,)), *[pltpu.HBM(h.shape, F32) for h in halves], jax.ShapeDtypeStruct((8, LANES), F32)),
        out_specs=(SEM, SEM, *[HBM] * n, pl.BlockSpec(memory_space=pltpu.VMEM)),
        input_output_aliases={k: 2 + k for k in range(n)},
        compiler_params=pltpu.CompilerParams(has_side_effects=EFFECT),
    )(*[_in_hbm(h) for h in halves])
    return (outs[0], outs[1]), list(outs[2:2 + n]), outs[-1]


def _join_wait(name, started, after):
    sems, bufs, _ = started
    n = len(bufs)

    def body(*refs):
        bufs, send_sems, recv_sems = refs[:n], refs[n], refs[n + 1]
        x, y, c, _ = _place()
        for i in range(n):
            mine, theirs = _half(bufs[i], c), _half(bufs[i], 1 - c)
            _remote(mine, mine, send_sems.at[i], recv_sems.at[i], (x, y, 1 - c)).wait_send()
            _remote(theirs, theirs, send_sems.at[i], recv_sems.at[i], (x, y, 1 - c)).wait_recv()

    outs = pl.pallas_call(
        body, name=name,
        in_specs=[HBM] * n + [SEM, SEM] + [ANY] * len(after),
        out_shape=tuple(pltpu.HBM(b.shape, F32) for b in bufs),
        out_specs=tuple([HBM] * n),
        input_output_aliases={k: k for k in range(n)},
        compiler_params=pltpu.CompilerParams(has_side_effects=EFFECT),
    )(*bufs, sems[0], sems[1], *after)
    return list(outs)


BIG = ("w_in", "pool_w", "w_out", "xq_w", "xk_w", "xv_w", "xo_w", "w_up", "w_down", "conv_w")
KEPT_F32 = ("conv_w",)
GATHER_GROUPS = ((0, 1, 9), (2, 3, 4, 5, 6), (7,), (8,))
RELAYED = (7, 8)
SMALL = ("conv_w", "a_log", "dt_bias", "gdn_norm_w", "pool_scale", "ln1_g", "ln1_b", "ln2_g", "ln2_b", "ln3_g", "ln3_b")
ORDER = ("w_in", "conv_w", "a_log", "dt_bias", "gdn_norm_w", "pool_w", "pool_scale", "w_out", "ln1_g", "ln1_b",
         "xq_w", "xk_w", "xv_w", "xo_w", "ln2_g", "ln2_b", "w_up", "w_down", "ln3_g", "ln3_b")
LANES = 128


def _rows(flat_len):
    return -(-flat_len // LANES)


def _pack(pieces):
    out = []
    for p in pieces:
        flat = p.reshape(-1).astype(F32)
        out.append(jnp.pad(flat, (0, _rows(flat.shape[0]) * LANES - flat.shape[0])).reshape(-1, LANES))
    slab = jnp.concatenate(out, axis=0)
    return jnp.pad(slab, ((0, -slab.shape[0] % 8), (0, 0)))


def _unpack(slab, shapes):
    out, row = [], 0
    for shp in shapes:
        size = math.prod(shp)
        out.append(slab[row:row + _rows(size)].reshape(-1)[:size].reshape(shp))
        row += _rows(size)
    return out


TRANSPOSED = ("w_in",)


def _as2d(name, a):
    a = a[0]
    if name in TRANSPOSED:
        return jnp.swapaxes(a, 0, 1)
    return a.reshape(-1, a.shape[-1]) if a.ndim == 3 else a


def _from2d(name, a, shape):
    return (jnp.swapaxes(a, 0, 1) if name in TRANSPOSED else a).reshape(shape)


def kernel(x, mem, w_in, conv_w, a_log, dt_bias, gdn_norm_w, pool_w, pool_scale, w_out, ln1_g, ln1_b, xq_w, xk_w, xv_w, xo_w, ln2_g, ln2_b, w_up, w_down, ln3_g, ln3_b, loss_target, m_w_in, m_conv_w, m_a_log, m_dt_bias, m_gdn_norm_w, m_pool_w, m_pool_scale, m_w_out, m_ln1_g, m_ln1_b, m_xq_w, m_xk_w, m_xv_w, m_xo_w, m_ln2_g, m_ln2_b, m_w_up, m_w_down, m_ln3_g, m_ln3_b, v_w_in, v_conv_w, v_a_log, v_dt_bias, v_gdn_norm_w, v_pool_w, v_pool_scale, v_w_out, v_ln1_g, v_ln1_b, v_xq_w, v_xk_w, v_xv_w, v_xo_w, v_ln2_g, v_ln2_b, v_w_up, v_w_down, v_ln3_g, v_ln3_b):
    given = dict(locals())
    cx, cy, cc = lax.axis_index("x"), lax.axis_index("y"), lax.axis_index("c")
    me = 2 * cx + cy
    groups = pool_w.shape[1]
    cs = pool_w.shape[2]
    kk, conv_cols = conv_w.shape[1], conv_w.shape[2]
    core = cc.astype(jnp.int32).reshape(1)
    where = jnp.stack([me, cc]).astype(jnp.int32)

    started = {}
    wts = {}

    def start(name, idx, after, token=None):
        casts = [_after(token, _as2d(BIG[i], given[BIG[i]])).astype(F32 if BIG[i] in KEPT_F32 else BF16) for i in idx]
        relayed = tuple(k for k, i in enumerate(idx) if i in RELAYED)
        sems, shards, lands, token = _gather_start(name, casts, after, relayed)
        for k, i in enumerate(idx):
            started[i] = (sems, k, shards[k], lands[k])
        return token

    token = start("gather_start_first", GATHER_GROUPS[0], x)
    token = start("gather_start_rest", tuple(i for group in GATHER_GROUPS[1:] for i in group), token, token)

    relay = {}

    def send_on(after):
        members = [started[i] for i in RELAYED]
        relay["sems"], zones, token = _gather_relay("gather_relay", [m[1] for m in members], [m[2] for m in members],
                                                    [m[3] for m in members], members[0][0], after)
        relay["zones"] = dict(zip(RELAYED, zones))
        return token

    passed = {}

    def pass_on(group, after):
        members = [started[i] for i in GATHER_GROUPS[group]]
        if GATHER_GROUPS[group][0] in RELAYED:
            ks = [RELAYED.index(i) for i in GATHER_GROUPS[group]]
            zones = [relay["zones"][i] for i in GATHER_GROUPS[group]]
            fwd, zones, token = _gather_forward_relayed(f"gather_forward_{group}", ks, zones, relay["sems"], after)
        else:
            fwd, zones, token = _gather_forward(f"gather_forward_{group}", [m[1] for m in members],
                                                [m[3] for m in members], members[0][0], after)
        passed[group] = (fwd, zones)
        return token

    def fetch(group, after):
        members = [started[i] for i in GATHER_GROUPS[group]]
        sems, idx = members[0][0], [m[1] for m in members]
        shards = [m[2] for m in members]
        if group not in passed:
            pass_on(group, after)
        fwd, zones = passed[group]
        if GATHER_GROUPS[group][0] in RELAYED:
            ks = [RELAYED.index(i) for i in GATHER_GROUPS[group]]
            got = _gather_wait_relayed(f"gather_wait_{group}", idx, ks, shards, zones, sems, relay["sems"], fwd, after)
        else:
            got = _gather_wait(f"gather_wait_{group}", idx, shards, zones, sems, fwd, after)
        full = dict(zip([BIG[i] for i in GATHER_GROUPS[group]], got))
        out = {}
        for n, a in full.items():
            if n == "w_in":
                out["w_in_t"] = a
            elif n == "w_up":
                out["w_up3"] = a
            elif n == "pool_w":
                out[n] = a.reshape(N_SHARD, groups, cs, -1).transpose(1, 0, 2, 3).reshape(groups, N_SHARD * cs, -1)
            elif n == "conv_w":
                out[n] = a.transpose(1, 0, 2).reshape(kk, N_SHARD * conv_cols)
            else:
                out[n] = a.reshape(-1, a.shape[-1])
        return out

    for n in ("a_log", "dt_bias", "gdn_norm_w", "pool_scale", "ln1_g", "ln1_b", "ln2_g", "ln2_b", "ln3_g", "ln3_b"):
        wts[n] = given[n]
    x_bf = _after(token, x[0]).astype(BF16)
    wts.update(fetch(0, x_bf))

    def start_swap(group, grads):
        names, blocks = [], []
        for n, g in grads.items():
            if n == "pool_w":
                g = g.reshape(groups, N_SHARD, cs, -1).transpose(1, 0, 2, 3).reshape(N_SHARD, groups * cs, -1)
            elif g.ndim == 2:
                g = g.reshape(N_SHARD, -1, g.shape[-1])
            names.append({"w_in_t": "w_in", "w_up3": "w_up"}.get(n, n))
            blocks.append(g)
        zones = [((N_SHARD,) + _half_shape(b.shape[1], b.shape[2]), F32) for b in blocks]
        swap = _exchange_start(f"grad_swap_start_{group}", _swap_copies, blocks, zones, N_SHARD)
        return {"group": group, "names": names, "swap": swap, "token": swap[3]}

    def start_send(state, after):
        group, names = state["group"], state["names"]
        state["blocks"] = state["swap"][1]
        state["others"] = _exchange_wait(f"grad_swap_wait_{group}", _swap_copies, state["swap"], after)
        partials = [_chip_partial("chip_partial_" + n, gb, ob, core)
                    for n, gb, ob in zip(names, state["blocks"], state["others"])]
        zones = [((3,) + p.shape[1:], BF16) for p in partials]
        state["send"] = _exchange_start(f"grad_send_start_{group}", _partial_copies, partials, zones, 3)
        state["token"] = state["send"][3]

    grad, delta, new_m, new_v = {}, {}, {}, {}

    def start_join(state, after):
        group, names = state["group"], state["names"]
        received = _exchange_wait(f"grad_send_wait_{group}", _partial_copies, state["send"], after)
        halves = [_reduce_own("reduce_own_" + n, gb, ob, rb, where)
                  for n, gb, ob, rb in zip(names, state["blocks"], state["others"], received)]
        state["join"] = _join_start(f"grad_join_start_{group}", halves)
        return state["join"][2]

    def finish_reduce(state, after):
        group, names = state["group"], state["names"]
        for n, g in zip(names, _join_wait(f"grad_join_wait_{group}", state["join"], after)):
            shp = given[n].shape
            d2, m2, v2, g2 = _adamw("adamw_" + n, _as2d(n, given[n]), g, _as2d(n, given["m_" + n]),
                                    _as2d(n, given["v_" + n]))
            grad[n], delta[n], new_m[n], new_v[n] = (_from2d(n, a, shp) for a in (g2, d2, m2, v2))
        return d2

    step = _local_step(x[0], mem[0], loss_target[0], wts, x_bf)
    pending = {}
    request = next(step)
    while True:
        try:
            kind, group, payload = request
            if kind == "weights":
                request = step.send(fetch(group, payload))
            elif kind == "relay":
                request = step.send(send_on(payload))
            elif kind == "pass":
                request = step.send(pass_on(group, payload))
            elif kind == "grads":
                pending[group] = start_swap(group, payload)
                request = step.send(pending[group]["token"])
            else:
                start_send(pending[group], [payload])
                request = step.send(pending[group]["token"])
        except StopIteration as stop:
            loss_row, grad_x, g = stop.value
            break

    after = [pending[2]["token"], grad_x]
    for group in (0, 1):
        after = [start_join(pending[group], after)]
    for group in (0, 1):
        after = [finish_reduce(pending[group], after)]
    after = [finish_reduce(pending[2], [start_join(pending[2], after)])]

    small_names = ("a_log", "dt_bias", "gdn_norm_w", "pool_scale", "ln1_g", "ln1_b", "ln2_g", "ln2_b", "ln3_g", "ln3_b")
    pieces = [g["conv_w"]] + [g[n] for n in small_names] + [loss_row[:, :1]]
    shapes = [p.shape for p in pieces]
    summed = _unpack(_all_reduce_small("all_reduce_small", _pack(pieces), after[0]), shapes)
    gsmall = dict(zip(small_names, summed[1:-1]))
    gsmall["conv_w"] = lax.dynamic_slice(summed[0], (0, me * conv_cols), (kk, conv_cols))
    loss = summed[-1][0, 0]

    sshapes = [given[n].shape for n in SMALL]
    slabs = [_pack([given[p + n] for n in SMALL]) for p in ("", "m_", "v_")]
    gslab = _pack([gsmall[n] for n in SMALL])
    outs = _adamw("adamw_small", slabs[0], gslab, slabs[1], slabs[2])[:3]
    for dst, slab in zip((delta, new_m, new_v), outs):
        dst.update(zip(SMALL, _unpack(slab, sshapes)))
    for n in SMALL:
        grad[n] = gsmall[n].reshape(given[n].shape)

    return (loss, grad_x[None], *[grad[n] for n in ORDER], *[delta[n] for n in ORDER],
            *[new_m[n] for n in ORDER], *[new_v[n] for n in ORDER])
```

```python
import math

import jax
import jax.numpy as jnp
from jax import lax
from jax.experimental import pallas as pl
from jax.experimental.pallas import tpu as pltpu

F32 = jnp.float32
BF16 = jnp.bfloat16
MESH = pl.DeviceIdType.MESH

HEAD_DIM = 128
CHUNK = 64
POOL_WINDOWS = (2, 4, 8, 16)
XATTN_HEADS = 4
ALPHA = 2.0 ** 0.25
LN_EPS = 1e-5
NORM_EPS = 1e-6
ADAM_LR, ADAM_B1, ADAM_B2, ADAM_EPS, ADAM_WD, ADAM_STEP = 0.001, 0.9, 0.999, 1e-08, 0.01, 10
N_SHARD = 4
VMEM_LIMIT = 56 * 1024 * 1024
K_STEPS = (2048, 1024, 512, 256, 128)


def _params(*sem):
    return pltpu.CompilerParams(dimension_semantics=sem, vmem_limit_bytes=VMEM_LIMIT)


def _bdot(a, b, ta=False, tb=False):
    dims = (((0 if ta else 1,), (1 if tb else 0,)), ((), ()))
    return lax.dot_general(a.astype(BF16), b.astype(BF16), dims, preferred_element_type=F32)


def _sigmoid(x):
    return 1.0 / (1.0 + jnp.exp(-x))


def _matmul(name, a, b, *, ta=False, tb=False, tm, tn, tk, extra=(), outs, epilogue, b_blocks=None,
            sequential=False, n_used=None, n_outer=False):
    m, k_dim = (a.shape[1], a.shape[0]) if ta else a.shape
    if b_blocks and tb:
        n = b.shape[1]
        k_dim = b.shape[0] * b.shape[2]
        per = b.shape[2] // tk
        b_spec = pl.BlockSpec((None, tn, tk), lambda i, j, k: (k // per, j, k % per))
    elif b_blocks:
        n = b.shape[0] * b.shape[2]
        per = b.shape[2] // tn
        b_spec = pl.BlockSpec((None, tk, tn), lambda i, j, k: (j // per, k, j % per))
    elif tb:
        n = b.shape[0]
        b_spec = pl.BlockSpec((tn, tk), lambda i, j, k: (j, k))
    else:
        n = b.shape[1]
        b_spec = pl.BlockSpec((tk, tn), lambda i, j, k: (k, j))
    n = n_used or n
    assert m % tm == 0 and n % tn == 0 and k_dim % tk == 0, (name, m, n, k_dim, tm, tn, tk)
    nk = k_dim // tk
    a_spec = pl.BlockSpec((tk, tm), lambda i, j, k: (k, i)) if ta else pl.BlockSpec((tm, tk), lambda i, j, k: (i, k))
    n_extra, n_out = len(extra), len(outs)

    def wrap(index_map):
        return lambda i, j, k: index_map(i, j)

    def spec(block, index_map):
        if n_outer:
            return pl.BlockSpec(block, lambda j, i, k: index_map(i, j, k))
        return pl.BlockSpec(block, index_map)

    row_axis = 1 if n_outer else 0

    def body_one_step(*refs):
        ex = refs[2:2 + n_extra]
        out = refs[2 + n_extra:2 + n_extra + n_out]
        epilogue(_bdot(refs[0][...], refs[1][...], ta, tb), ex, out, pl.program_id(row_axis))

    def body(*refs):
        a_ref, b_ref = refs[0], refs[1]
        ex = refs[2:2 + n_extra]
        out = refs[2 + n_extra:2 + n_extra + n_out]
        acc = refs[-1]
        i, k = pl.program_id(row_axis), pl.program_id(2)
        part = _bdot(a_ref[...], b_ref[...], ta, tb)

        @pl.when(k == 0)
        def _():
            acc[...] = part

        @pl.when(jnp.logical_and(k > 0, k < nk - 1))
        def _():
            acc[...] += part

        @pl.when(k == nk - 1)
        def _():
            epilogue(acc[...] + part, ex, out, i)

    sem = ("arbitrary",) * 3 if sequential else ("parallel", "parallel", "arbitrary")
    res = pl.pallas_call(
        body_one_step if nk == 1 else body, name=name,
        grid=(n // tn, m // tm, nk) if n_outer else (m // tm, n // tn, nk),
        in_specs=[spec(a_spec.block_shape, a_spec.index_map), spec(b_spec.block_shape, b_spec.index_map)]
        + [spec(bs, wrap(im)) for _, bs, im in extra],
        out_specs=[spec(bs, wrap(im)) for _, bs, im in outs],
        out_shape=[s for s, _, _ in outs],
        scratch_shapes=[] if nk == 1 else [pltpu.VMEM((tm, tn), F32)],
        compiler_params=_params(*sem),
    )(a, b, *[x for x, _, _ in extra])
    return res


def _tile(i, j):
    return (i, j)


def _plain(name, a, b, *, ta=False, tb=False, tm, tn, tk, out_dtype, b_blocks=None, out3=None, n_used=None,
           n_outer=False, m_kept=None):
    m = a.shape[1] if ta else a.shape[0]
    if b_blocks:
        n = b.shape[1] if tb else b.shape[0] * b.shape[2]
    else:
        n = n_used or (b.shape[0] if tb else b.shape[1])

    def epi(acc, ex, out, i):
        out[0][...] = acc.astype(out_dtype)

    if out3:
        per = (n // out3) // tn
        spec = (jax.ShapeDtypeStruct((out3, m, n // out3), out_dtype), (None, tm, tn),
                lambda i, j: (j // per, i, j % per))
    else:
        spec = (jax.ShapeDtypeStruct((m_kept or m, n), out_dtype), (tm, tn), _tile)
    return _matmul(name, a, b, ta=ta, tb=tb, tm=tm, tn=tn, tk=tk, outs=[spec], epilogue=epi,
                   b_blocks=b_blocks, n_used=n_used, n_outer=n_outer)[0]


def _ln_forward(name, a, b, res, gamma, beta, *, tm, tk, want_h=True):
    m, n = res.shape

    def epi(acc, ex, out, i):
        u = ALPHA * ex[0][...] + acc
        mu = jnp.mean(u, axis=-1, keepdims=True)
        xc = u - mu
        var = jnp.mean(xc * xc, axis=-1, keepdims=True)
        rstd = lax.rsqrt(var + LN_EPS)
        xhat = xc * rstd
        out[-2][...] = xhat
        out[-1][...] = rstd
        if want_h:
            h = xhat * ex[1][...] + ex[2][...]
            out[0][...] = h
            out[1][...] = h.astype(BF16)

    row = lambda i, j: (i, 0)
    vec = lambda i, j: (0, 0)
    outs = [(jax.ShapeDtypeStruct((m, n), F32), (tm, n), row), (jax.ShapeDtypeStruct((m, n), BF16), (tm, n), row),
            (jax.ShapeDtypeStruct((m, n), F32), (tm, n), row), (jax.ShapeDtypeStruct((m, 1), F32), (tm, 1), row)]
    return _matmul(
        name, a, b, tm=tm, tn=n, tk=tk,
        extra=[(res, (tm, n), row), (gamma, (1, n), vec), (beta, (1, n), vec)],
        outs=outs if want_h else outs[2:], epilogue=epi)


def _ln_backward_math(dy, xhat, rstd, gamma):
    dxhat = dy * gamma
    m1 = jnp.mean(dxhat, axis=-1, keepdims=True)
    m2 = jnp.mean(dxhat * xhat, axis=-1, keepdims=True)
    du = rstd * (dxhat - m1 - xhat * m2)
    return du, jnp.sum(dy * xhat, axis=0, keepdims=True), jnp.sum(dy, axis=0, keepdims=True)


def _ln_backward(name, a, b, dres, xhat, rstd, gamma, *, tm, tk, b_blocks=None, tb=True):
    m, n = dres.shape

    def epi(acc, ex, out, i):
        dy = acc + ALPHA * ex[0][...]
        du, dg, db = _ln_backward_math(dy, ex[1][...], ex[2][...], ex[3][...])
        out[0][...] = du
        out[1][...] = du.astype(BF16)
        first = i == 0

        @pl.when(first)
        def _():
            out[2][...] = dg
            out[3][...] = db

        @pl.when(jnp.logical_not(first))
        def _():
            out[2][...] += dg
            out[3][...] += db

    row = lambda i, j: (i, 0)
    vec = lambda i, j: (0, 0)
    return _matmul(
        name, a, b, tb=tb, tm=tm, tn=n, tk=tk, b_blocks=b_blocks, sequential=True,
        extra=[(dres, (tm, n), row), (xhat, (tm, n), row), (rstd, (tm, 1), row), (gamma, (1, n), vec)],
        outs=[(jax.ShapeDtypeStruct((m, n), F32), (tm, n), row),
              (jax.ShapeDtypeStruct((m, n), BF16), (tm, n), row),
              (jax.ShapeDtypeStruct((1, n), F32), (1, n), vec),
              (jax.ShapeDtypeStruct((1, n), F32), (1, n), vec)],
        epilogue=epi)


def _shift_down(x, k):
    row = lax.broadcasted_iota(jnp.int32, x.shape, 0)
    return jnp.where(row >= k, pltpu.roll(x, k, axis=0), 0.0)


def _shift_up(x, k):
    t = x.shape[0]
    row = lax.broadcasted_iota(jnp.int32, x.shape, 0)
    return jnp.where(row < t - k, pltpu.roll(x, t - k, axis=0), 0.0)


def _conv_silu_norm(x, w, normalise):
    kk = w.shape[0]
    c = x * w[kk - 1:kk, :]
    for j in range(kk - 1):
        c = c + _shift_down(x, kk - 1 - j) * w[j:j + 1, :]
    sg = _sigmoid(c)
    s = c * sg
    r = lax.rsqrt(jnp.sum(s * s, axis=-1, keepdims=True) + NORM_EPS)
    y = jnp.where(normalise, s * r, s)
    return c, sg, s, r, y


def _gdn_pre(proj, conv_w, heads):
    t = proj.shape[0]
    kk = conv_w.shape[0]

    def body(x_ref, w_ref, o_ref):
        normalise = pl.program_id(0) < 2
        o_ref[...] = _conv_silu_norm(x_ref[...], w_ref[...], normalise)[4]

    col = lambda s, h: (0, s * heads + h)
    return pl.pallas_call(
        body, name="gdn_pre", grid=(3, heads),
        in_specs=[pl.BlockSpec((t, HEAD_DIM), col), pl.BlockSpec((kk, HEAD_DIM), col)],
        out_specs=pl.BlockSpec((t, HEAD_DIM), col),
        out_shape=jax.ShapeDtypeStruct((t, 3 * heads * HEAD_DIM), F32),
        compiler_params=_params("parallel", "parallel"),
    )(proj, conv_w)


def _gdn_pre_backward(proj, conv_w, dqkv, heads):
    t = proj.shape[0]
    kk = conv_w.shape[0]

    def body(x_ref, w_ref, dy_ref, dx_ref, dw_ref):
        normalise = pl.program_id(0) < 2
        x = x_ref[...]
        w = w_ref[...]
        dy = dy_ref[...]
        c, sg, s, r, y = _conv_silu_norm(x, w, normalise)
        ds_norm = r * (dy - y * jnp.sum(dy * y, axis=-1, keepdims=True))
        ds = jnp.where(normalise, ds_norm, dy)
        dc = ds * (sg * (1.0 + c * (1.0 - sg)))
        dx = dc * w[kk - 1:kk, :]
        rows = [None] * kk
        rows[kk - 1] = jnp.sum(dc * x, axis=0, keepdims=True)
        for j in range(kk - 1):
            lag = kk - 1 - j
            dx = dx + _shift_up(dc, lag) * w[j:j + 1, :]
            rows[j] = jnp.sum(dc * _shift_down(x, lag), axis=0, keepdims=True)
        dx_ref[...] = dx.astype(BF16)
        dw_ref[...] = jnp.concatenate(rows, axis=0)

    col = lambda s, h: (0, s * heads + h)
    return pl.pallas_call(
        body, name="gdn_pre_bwd", grid=(3, heads),
        in_specs=[pl.BlockSpec((t, HEAD_DIM), col), pl.BlockSpec((kk, HEAD_DIM), col),
                  pl.BlockSpec((t, HEAD_DIM), col)],
        out_specs=[pl.BlockSpec((t, HEAD_DIM), col), pl.BlockSpec((kk, HEAD_DIM), col)],
        out_shape=[jax.ShapeDtypeStruct((t, 3 * heads * HEAD_DIM), BF16),
                   jax.ShapeDtypeStruct((kk, 3 * heads * HEAD_DIM), F32)],
        compiler_params=_params("parallel", "parallel"),
    )(proj, conv_w, dqkv)


def _gate_vectors(a_log, dt_bias, heads):
    pad = lambda v: jnp.pad(v.astype(F32), ((0, 0), (heads, HEAD_DIM - 2 * heads)))
    return pad(jnp.exp(a_log.astype(F32))), pad(dt_bias)


def _softplus(x):
    return jnp.maximum(x, 0.0) + jnp.log(1.0 + jnp.exp(-jnp.abs(x)))


def _gates_epilogue(heads):
    def epi(acc, ex, out, i):
        lane = lax.broadcasted_iota(jnp.int32, acc.shape, 1)
        beta = _sigmoid(acc)
        g = -ex[0][...] * _softplus(acc + ex[1][...])
        out[0][...] = acc
        out[1][...] = jnp.where(lane < heads, beta, jnp.where(lane < 2 * heads, g, 0.0))
    return epi


def _gates_backward(ba, bg, dbg, ea, dtb, heads):
    t = ba.shape[0]

    def body(ba_ref, bg_ref, d_ref, ea_ref, dt_ref, dba_ref, dal_ref, ddt_ref):
        lane = lax.broadcasted_iota(jnp.int32, (t, HEAD_DIM), 1)
        bgv = bg_ref[...]
        d = d_ref[...]
        db = d * bgv * (1.0 - bgv)
        da = -d * ea_ref[...] * _sigmoid(ba_ref[...] + dt_ref[...])
        is_g = jnp.logical_and(lane >= heads, lane < 2 * heads)
        dba = jnp.where(lane < heads, db, jnp.where(is_g, da, 0.0))
        dba_ref[...] = dba.astype(BF16)
        dal_ref[...] = jnp.sum(jnp.where(is_g, d * bgv, 0.0), axis=0, keepdims=True)
        ddt_ref[...] = jnp.sum(jnp.where(is_g, da, 0.0), axis=0, keepdims=True)

    full = pl.BlockSpec((t, HEAD_DIM), lambda: (0, 0))
    vec = pl.BlockSpec((1, HEAD_DIM), lambda: (0, 0))
    return pl.pallas_call(
        body, name="gates_bwd", grid=(),
        in_specs=[full, full, full, vec, vec], out_specs=[full, vec, vec],
        out_shape=[jax.ShapeDtypeStruct((t, HEAD_DIM), BF16), jax.ShapeDtypeStruct((1, HEAD_DIM), F32),
                   jax.ShapeDtypeStruct((1, HEAD_DIM), F32)],
        compiler_params=pltpu.CompilerParams(vmem_limit_bytes=VMEM_LIMIT),
    )(ba, bg, dbg, ea, dtb)


class _Chunk:
    pass


def _split2(x):
    hi = x.astype(BF16)
    return hi, (x - hi.astype(F32)).astype(BF16)


def _split3(x):
    hi = x.astype(BF16)
    rest = x - hi.astype(F32)
    mid = rest.astype(BF16)
    return hi, mid, (rest - mid.astype(F32)).astype(BF16)


def _dot_mask(mask, x, ta=False):
    hi, mid, lo = _split3(x)
    return _bdot(mask, hi, ta=ta) + (_bdot(mask, mid, ta=ta) + _bdot(mask, lo, ta=ta))


def _transpose_by_identity(x):
    r = x.shape[0]
    eye = (lax.broadcasted_iota(jnp.int32, (r, r), 0) == lax.broadcasted_iota(jnp.int32, (r, r), 1)).astype(BF16)
    hi, mid, lo = _split3(x)
    return _bdot(hi, eye, ta=True) + (_bdot(mid, eye, ta=True) + _bdot(lo, eye, ta=True))


def _dot22(a, b, ta=False, tb=False):
    ah, al = _split2(a)
    bh, bl = _split2(b)
    return _bdot(ah, bh, ta, tb) + (_bdot(ah, bl, ta, tb) + _bdot(al, bh, ta, tb))


def _chunk_gates(bg, heads):
    n = CHUNK
    row = lax.broadcasted_iota(jnp.int32, (n, n), 0)
    col = lax.broadcasted_iota(jnp.int32, (n, n), 1)
    lane = lax.broadcasted_iota(jnp.int32, bg.shape, 1)
    graw = jnp.where(jnp.logical_and(lane >= heads, lane < 2 * heads), bg, 0.0)
    gc = _dot_mask((row >= col).astype(BF16), graw)
    return gc, _transpose_by_identity(gc)


def _in_lockstep(generators):
    results = [None] * len(generators)
    live = list(enumerate(generators))
    while live:
        still = []
        for i, gen in live:
            try:
                next(gen)
                still.append((i, gen))
            except StopIteration as stop:
                results[i] = stop.value
        live = still
    return results


def _chunk_local(q, k, v, beta, gc, grow, solved=None):
    c = _Chunk()
    n = CHUNK
    row = lax.broadcasted_iota(jnp.int32, (n, n), 0)
    col = lax.broadcasted_iota(jnp.int32, (n, n), 1)
    c.tri = row >= col
    c.strict = row > col
    eye = row == col
    c.gcb = jnp.broadcast_to(gc, (n, HEAD_DIM))
    c.decay = jnp.where(c.tri, jnp.exp(jnp.where(c.tri, gc - grow, 0.0)), 0.0)
    c.eg = jnp.exp(c.gcb)
    glast = c.gcb[n - 1:n, :]
    c.egl = jnp.exp(glast)
    c.ekl = jnp.exp(glast - c.gcb)
    c.beta = beta
    c.q = q * (HEAD_DIM ** -0.5)
    c.k = k
    c.v = v
    c.kb = k * beta
    c.vb = v * beta
    c.kg = c.kb * c.eg
    both = _bdot(jnp.concatenate([c.kb, c.q], axis=0), k, tb=True)
    yield
    c.L = jnp.where(c.strict, both[:n] * c.decay, 0.0)
    c.A = jnp.where(c.tri, both[n:] * c.decay, 0.0)
    if solved is None:
        x = -c.L
        tinv = eye.astype(F32) + x
        p = _dot22(x, x)
        yield
        for _ in range(int(math.log2(n)) - 2):
            both = _dot22(jnp.concatenate([p, tinv], axis=0), p)
            yield
            p, tinv = both[:n], tinv + both[n:]
        c.T = tinv + _dot22(tinv, p)
        yield
        uw = _dot22(c.T, jnp.concatenate([c.vb, c.kg], axis=1))
        yield
        c.u, c.w = uw[:, :HEAD_DIM], uw[:, HEAD_DIM:]
    else:
        c.T, c.u, c.w = solved
    c.qg = c.q * c.eg
    c.kdec = k * c.ekl
    return c


def _gdn_core(qkv, bg, heads):
    t = qkv.shape[0]
    nchunk = t // CHUNK

    gw = heads * HEAD_DIM

    def body(qkv_ref, bg_ref, o_ref, s_ref, t_ref, u_ref, w_ref, state):
        @pl.when(pl.program_id(0) == 0)
        def _():
            state[...] = jnp.zeros_like(state)

        bg_v = bg_ref[...]
        gc_all, gc_rows = _chunk_gates(bg_v, heads)
        def one_head(h):
            col = lambda s: pl.ds(s * gw + h * HEAD_DIM, HEAD_DIM)
            c = yield from _chunk_local(qkv_ref[:, col(0)], qkv_ref[:, col(1)], qkv_ref[:, col(2)], bg_v[:, h:h + 1],
                                        gc_all[:, heads + h:heads + h + 1], gc_rows[heads + h:heads + h + 1, :])
            s0 = state[h]
            v_new = c.u - _bdot(c.w, s0)
            yield
            o = _bdot(c.qg, s0) + _bdot(c.A, v_new)
            return s0, o, s0 * c.egl + _bdot(c.kdec, v_new, ta=True), c

        results = _in_lockstep([one_head(h) for h in range(heads)])
        for h, (s0, o, s1, c) in enumerate(results):
            lanes = pl.ds(h * HEAD_DIM, HEAD_DIM)
            s_ref[h, 0] = s0
            o_ref[:, lanes] = o
            t_ref[:, lanes] = jnp.concatenate([c.T, jnp.zeros((CHUNK, HEAD_DIM - CHUNK), F32)], axis=1)
            u_ref[:, lanes] = c.u
            w_ref[:, lanes] = c.w
            state[h] = s1

    return pl.pallas_call(
        body, name="gdn_core", grid=(nchunk,),
        in_specs=[pl.BlockSpec((CHUNK, 3 * gw), lambda n: (n, 0)), pl.BlockSpec((CHUNK, HEAD_DIM), lambda n: (n, 0))],
        out_specs=[pl.BlockSpec((CHUNK, gw), lambda n: (n, 0)),
                   pl.BlockSpec((heads, 1, HEAD_DIM, HEAD_DIM), lambda n: (0, n, 0, 0))]
        + [pl.BlockSpec((CHUNK, gw), lambda n: (n, 0))] * 3,
        out_shape=[jax.ShapeDtypeStruct((t, gw), F32),
                   jax.ShapeDtypeStruct((heads, nchunk, HEAD_DIM, HEAD_DIM), F32)]
        + [jax.ShapeDtypeStruct((t, gw), F32)] * 3,
        scratch_shapes=[pltpu.VMEM((heads, HEAD_DIM, HEAD_DIM), F32)],
        compiler_params=_params("arbitrary"),
    )(qkv, bg)


def _gdn_core_backward(qkv, bg, states, solved, do, heads):
    t = qkv.shape[0]
    nchunk = t // CHUNK
    n = CHUNK

    def one_head(chunk_local, s0, d_out, ds1):
        c = yield from chunk_local
        v_new = c.u - _bdot(c.w, s0)
        dqg = _bdot(d_out, s0, tb=True)
        ds0 = _bdot(c.qg, d_out, ta=True) + ds1 * c.egl
        dv_new = _bdot(c.A, d_out, ta=True) + _bdot(c.kdec, ds1)
        yield
        dA = jnp.where(c.tri, _bdot(d_out, v_new, tb=True), 0.0)
        dkdec = _bdot(v_new, ds1, tb=True)
        dgl = jnp.sum(jnp.sum(ds1 * s0, axis=1, keepdims=True), axis=0, keepdims=True) * c.egl
        dw = -_bdot(dv_new, s0, tb=True)
        ds0 = ds0 - _bdot(c.w, dv_new, ta=True)
        yield
        both = _dot22(c.T, jnp.concatenate([dv_new, dw], axis=1), ta=True)
        yield
        dvb, dkg = both[:, :HEAD_DIM], both[:, HEAD_DIM:]
        dL = jnp.where(c.strict, -(_bdot(dvb, c.u, tb=True) + _bdot(dkg, c.w, tb=True)), 0.0)
        yield
        dm1 = dL * c.decay
        dkb = _bdot(dm1, c.k) + dkg * c.eg
        dk = _bdot(dm1, c.kb, ta=True)
        dm2 = dA * c.decay
        dq = _bdot(dm2, c.k) + dqg * c.eg
        dk = dk + _bdot(dm2, c.q, ta=True) + dkdec * c.ekl + dkb * c.beta
        pm = dL * c.L + dA * c.A
        ones = jnp.ones((n, HEAD_DIM), BF16)
        pm_hi, pm_lo = _split2(pm)
        colsum = _bdot(pm_hi, ones, ta=True) + _bdot(pm_lo, ones, ta=True)
        tk_ = jnp.sum(dkdec * c.kdec, axis=1, keepdims=True)
        dgc = (jnp.sum(pm, axis=1, keepdims=True) - colsum
               + jnp.sum(dqg * c.qg, axis=1, keepdims=True)
               - tk_
               + jnp.sum(dkg * c.kg, axis=1, keepdims=True))
        dgl = dgl + jnp.sum(tk_, axis=0, keepdims=True)
        rowi = lax.broadcasted_iota(jnp.int32, (n, HEAD_DIM), 0)
        dgc = dgc + jnp.where(rowi == n - 1, dgl, 0.0)
        dbeta = jnp.sum(dkb * c.k, axis=1, keepdims=True) + jnp.sum(dvb * c.v, axis=1, keepdims=True)
        return dq * (HEAD_DIM ** -0.5), dk, dvb * c.beta, dbeta, dgc, ds0

    gw = heads * HEAD_DIM

    def body(qkv_ref, bg_ref, s_ref, t_ref, u_ref, w_ref, do_ref, dqkv_ref, dbg_ref, dstate):
        @pl.when(pl.program_id(0) == 0)
        def _():
            dstate[...] = jnp.zeros_like(dstate)

        bg_v = bg_ref[...]
        gc_all, gc_rows = _chunk_gates(bg_v, heads)
        lane = lax.broadcasted_iota(jnp.int32, (n, HEAD_DIM), 1)
        dgates = jnp.zeros((n, HEAD_DIM), F32)
        chains = []
        for h in range(heads):
            col = lambda s: pl.ds(s * gw + h * HEAD_DIM, HEAD_DIM)
            lanes = pl.ds(h * HEAD_DIM, HEAD_DIM)
            c = _chunk_local(qkv_ref[:, col(0)], qkv_ref[:, col(1)], qkv_ref[:, col(2)], bg_v[:, h:h + 1],
                             gc_all[:, heads + h:heads + h + 1], gc_rows[heads + h:heads + h + 1, :],
                             (t_ref[:, pl.ds(h * HEAD_DIM, CHUNK)], u_ref[:, lanes], w_ref[:, lanes]))
            chains.append(one_head(c, s_ref[h, 0], do_ref[:, pl.ds(h * HEAD_DIM, HEAD_DIM)], dstate[h]))
        results = _in_lockstep(chains)
        for h, (dq, dk, dv, dbeta, dgc, ds0) in enumerate(results):
            dgates = jnp.where(lane == h, dbeta, jnp.where(lane == heads + h, dgc, dgates))
        for h, (dq, dk, dv, dbeta, dgc, ds0) in enumerate(results):
            dqkv_ref[:, pl.ds(h * HEAD_DIM, HEAD_DIM)] = dq
            dqkv_ref[:, pl.ds(gw + h * HEAD_DIM, HEAD_DIM)] = dk
            dqkv_ref[:, pl.ds(2 * gw + h * HEAD_DIM, HEAD_DIM)] = dv
            dstate[h] = ds0
        row = lax.broadcasted_iota(jnp.int32, (n, n), 0)
        colm = lax.broadcasted_iota(jnp.int32, (n, n), 1)
        draw = _dot_mask((row >= colm).astype(BF16), dgates, ta=True)
        dbg_ref[...] = jnp.where(lane < heads, dgates, draw)

    last = nchunk - 1
    return pl.pallas_call(
        body, name="gdn_core_bwd", grid=(nchunk,),
        in_specs=[pl.BlockSpec((CHUNK, 3 * gw), lambda i: (last - i, 0)),
                  pl.BlockSpec((CHUNK, HEAD_DIM), lambda i: (last - i, 0)),
                  pl.BlockSpec((heads, 1, HEAD_DIM, HEAD_DIM), lambda i: (0, last - i, 0, 0))]
        + [pl.BlockSpec((CHUNK, gw), lambda i: (last - i, 0))] * 4,
        out_specs=[pl.BlockSpec((CHUNK, 3 * gw), lambda i: (last - i, 0)),
                   pl.BlockSpec((CHUNK, HEAD_DIM), lambda i: (last - i, 0))],
        out_shape=[jax.ShapeDtypeStruct((t, 3 * gw), F32), jax.ShapeDtypeStruct((t, HEAD_DIM), F32)],
        scratch_shapes=[pltpu.VMEM((heads, HEAD_DIM, HEAD_DIM), F32)],
        compiler_params=_params("arbitrary"),
    )(qkv, bg, states, *solved, do)


def _gdn_post(o, proj, z_col0, norm_w, heads, tt):
    t = o.shape[0]
    zb = z_col0 // HEAD_DIM

    def body(o_ref, z_ref, w_ref, out_ref):
        ov = o_ref[...]
        z = z_ref[...]
        rms = lax.rsqrt(jnp.mean(ov * ov, axis=-1, keepdims=True) + NORM_EPS)
        out_ref[...] = (ov * rms * w_ref[...] * (z * _sigmoid(z))).astype(BF16)

    return pl.pallas_call(
        body, name="gdn_post", grid=(t // tt, heads),
        in_specs=[pl.BlockSpec((tt, HEAD_DIM), lambda i, h: (i, h)),
                  pl.BlockSpec((tt, HEAD_DIM), lambda i, h: (i, zb + h)),
                  pl.BlockSpec((1, HEAD_DIM), lambda i, h: (0, 0))],
        out_specs=pl.BlockSpec((tt, HEAD_DIM), lambda i, h: (i, h)),
        out_shape=jax.ShapeDtypeStruct((t, heads * HEAD_DIM), BF16),
        compiler_params=_params("parallel", "parallel"),
    )(o, proj, norm_w)


def _gdn_post_backward(dcat, o, proj, z_col0, norm_w, heads, tt):
    t = o.shape[0]
    zb = z_col0 // HEAD_DIM

    def body(d_ref, o_ref, z_ref, w_ref, do_ref, dz_ref, dw_ref):
        d = d_ref[...]
        ov = o_ref[...]
        z = z_ref[...]
        w = w_ref[...]
        rms = lax.rsqrt(jnp.mean(ov * ov, axis=-1, keepdims=True) + NORM_EPS)
        ohat = ov * rms
        sg = _sigmoid(z)
        gate = z * sg
        dz_ref[...] = (d * ohat * w * (sg * (1.0 + z * (1.0 - sg)))).astype(BF16)
        don = d * gate
        dohat = don * w
        do_ref[...] = rms * (dohat - ohat * jnp.mean(dohat * ohat, axis=-1, keepdims=True))
        dw = jnp.sum(don * ohat, axis=0, keepdims=True)
        first = jnp.logical_and(pl.program_id(0) == 0, pl.program_id(1) == 0)

        @pl.when(first)
        def _():
            dw_ref[...] = dw

        @pl.when(jnp.logical_not(first))
        def _():
            dw_ref[...] += dw

    blk = pl.BlockSpec((tt, HEAD_DIM), lambda i, h: (i, h))
    return pl.pallas_call(
        body, name="gdn_post_bwd", grid=(t // tt, heads),
        in_specs=[blk, blk, pl.BlockSpec((tt, HEAD_DIM), lambda i, h: (i, zb + h)),
                  pl.BlockSpec((1, HEAD_DIM), lambda i, h: (0, 0))],
        out_specs=[blk, blk, pl.BlockSpec((1, HEAD_DIM), lambda i, h: (0, 0))],
        out_shape=[jax.ShapeDtypeStruct((t, heads * HEAD_DIM), F32),
                   jax.ShapeDtypeStruct((t, heads * HEAD_DIM), BF16),
                   jax.ShapeDtypeStruct((1, HEAD_DIM), F32)],
        compiler_params=_params("arbitrary", "arbitrary"),
    )(dcat, o, proj, norm_w)


def _pool_select(levels, group):
    out = levels[-1]
    for gi in range(len(levels) - 2, -1, -1):
        out = jnp.where(group == gi, levels[gi], out)
    return out


def _pool_counts(t, width, group):
    pos = lax.broadcasted_iota(jnp.int32, (t, width), 0)
    win = jnp.left_shift(2, group)
    return jnp.minimum(pos + 1, win).astype(F32)


def _pooled(p, group):
    levels, s, step = [], p, 1
    for _ in POOL_WINDOWS:
        s = s + _shift_down(s, step)
        levels.append(s)
        step *= 2
    cnt = _pool_counts(p.shape[0], p.shape[1], group)
    return _pool_select(levels, group) / cnt - p, cnt


def _pool_forward(proj, p_col0, pool_w, pool_scale):
    t = proj.shape[0]
    groups, cg, _ = pool_w.shape
    pb = p_col0 // cg

    def body(p_ref, w_ref, s_ref, o_ref):
        pooled, _ = _pooled(p_ref[...], pl.program_id(0))
        o_ref[...] = (_bdot(pooled, w_ref[0]) * s_ref[...]).astype(BF16)

    return pl.pallas_call(
        body, name="pool_fwd", grid=(groups,),
        in_specs=[pl.BlockSpec((t, cg), lambda g: (0, pb + g)), pl.BlockSpec((1, cg, cg), lambda g: (g, 0, 0)),
                  pl.BlockSpec((1, cg), lambda g: (0, g))],
        out_specs=pl.BlockSpec((t, cg), lambda g: (0, g)),
        out_shape=jax.ShapeDtypeStruct((t, groups * cg), BF16),
        compiler_params=_params("parallel"),
    )(proj, pool_w, pool_scale)


def _pool_backward(dcat, d_col0, proj, p_col0, pool_w, pool_scale):
    t = proj.shape[0]
    groups, cg, _ = pool_w.shape
    pb = p_col0 // cg
    db = d_col0 // cg

    def body(d_ref, p_ref, w_ref, s_ref, dp_ref, dw_ref, ds_ref):
        group = pl.program_id(0)
        pooled, cnt = _pooled(p_ref[...], group)
        w = w_ref[0]
        d = d_ref[...]
        mixed = _bdot(pooled, w)
        ds_ref[...] = jnp.sum(d * mixed, axis=0, keepdims=True)
        dmixed = d * s_ref[...]
        dw_ref[0] = _bdot(pooled, dmixed, ta=True)
        dpooled = _bdot(dmixed, w, tb=True)
        levels, s, step = [], dpooled / cnt, 1
        for _ in POOL_WINDOWS:
            s = s + _shift_up(s, step)
            levels.append(s)
            step *= 2
        dp_ref[...] = (_pool_select(levels, group) - dpooled).astype(BF16)

    return pl.pallas_call(
        body, name="pool_bwd", grid=(groups,),
        in_specs=[pl.BlockSpec((t, cg), lambda g: (0, db + g)), pl.BlockSpec((t, cg), lambda g: (0, pb + g)),
                  pl.BlockSpec((1, cg, cg), lambda g: (g, 0, 0)), pl.BlockSpec((1, cg), lambda g: (0, g))],
        out_specs=[pl.BlockSpec((t, cg), lambda g: (0, g)), pl.BlockSpec((1, cg, cg), lambda g: (g, 0, 0)),
                   pl.BlockSpec((1, cg), lambda g: (0, g))],
        out_shape=[jax.ShapeDtypeStruct((t, groups * cg), BF16), jax.ShapeDtypeStruct((groups, cg, cg), F32),
                   jax.ShapeDtypeStruct((1, groups * cg), F32)],
        compiler_params=_params("parallel"),
    )(dcat, proj, pool_w, pool_scale)


def _attention(q, k, v, tq):
    t, d = q.shape
    m = k.shape[0]
    dh = d // XATTN_HEADS
    scale = dh ** -0.5

    def body(q_ref, k_ref, v_ref, o_ref):
        s = _bdot(q_ref[...], k_ref[...], tb=True) * scale
        s = s - jnp.max(s, axis=-1, keepdims=True)
        e = jnp.exp(s)
        p = e / jnp.sum(e, axis=-1, keepdims=True)
        o_ref[...] = _bdot(p, v_ref[...]).astype(BF16)

    return pl.pallas_call(
        body, name="xattn_fwd", grid=(XATTN_HEADS, t // tq),
        in_specs=[pl.BlockSpec((tq, dh), lambda h, i: (i, h)), pl.BlockSpec((m, dh), lambda h, i: (0, h)),
                  pl.BlockSpec((m, dh), lambda h, i: (0, h))],
        out_specs=pl.BlockSpec((tq, dh), lambda h, i: (i, h)),
        out_shape=jax.ShapeDtypeStruct((t, d), BF16),
        compiler_params=_params("parallel", "parallel"),
    )(q, k, v)


def _attention_backward(q, k, v, do, tq):
    t, d = q.shape
    m = k.shape[0]
    dh = d // XATTN_HEADS
    scale = dh ** -0.5

    def body(q_ref, k_ref, v_ref, do_ref, dq_ref, dk_ref, dv_ref, dk_acc, dv_acc):
        i = pl.program_id(1)
        qv, kv, vv, dov = q_ref[...], k_ref[...], v_ref[...], do_ref[...]
        s = _bdot(qv, kv, tb=True) * scale
        s = s - jnp.max(s, axis=-1, keepdims=True)
        e = jnp.exp(s)
        p = e / jnp.sum(e, axis=-1, keepdims=True)
        dp = _bdot(dov, vv, tb=True)
        ds = p * (dp - jnp.sum(dp * p, axis=-1, keepdims=True)) * scale
        dq_ref[...] = _bdot(ds, kv).astype(BF16)
        dv_part = _bdot(p, dov, ta=True)
        dk_part = _bdot(ds, qv, ta=True)

        @pl.when(i == 0)
        def _():
            dk_acc[...] = dk_part
            dv_acc[...] = dv_part

        @pl.when(i > 0)
        def _():
            dk_acc[...] += dk_part
            dv_acc[...] += dv_part

        @pl.when(i == pl.num_programs(1) - 1)
        def _():
            dk_ref[...] = dk_acc[...].astype(BF16)
            dv_ref[...] = dv_acc[...].astype(BF16)

    qblk = pl.BlockSpec((tq, dh), lambda h, i: (i, h))
    kblk = pl.BlockSpec((m, dh), lambda h, i: (0, h))
    return pl.pallas_call(
        body, name="xattn_bwd", grid=(XATTN_HEADS, t // tq),
        in_specs=[qblk, kblk, kblk, qblk],
        out_specs=[qblk, kblk, kblk],
        out_shape=[jax.ShapeDtypeStruct((t, d), BF16), jax.ShapeDtypeStruct((m, d), BF16),
                   jax.ShapeDtypeStruct((m, d), BF16)],
        scratch_shapes=[pltpu.VMEM((m, dh), F32), pltpu.VMEM((m, dh), F32)],
        compiler_params=_params("parallel", "arbitrary"),
    )(q, k, v, do)


def _ln_backward_rows(name, dmain, dres, xhat, rstd, gamma, tm):
    t, d = xhat.shape

    def body(m_ref, r_ref, x_ref, s_ref, g_ref, du_ref, dub_ref, dg_ref, db_ref):
        du, dg, db = _ln_backward_math(m_ref[...] + ALPHA * r_ref[...], x_ref[...], s_ref[...], g_ref[...])
        du_ref[...] = du
        dub_ref[...] = du.astype(BF16)
        first = pl.program_id(0) == 0

        @pl.when(first)
        def _():
            dg_ref[...] = dg
            db_ref[...] = db

        @pl.when(jnp.logical_not(first))
        def _():
            dg_ref[...] += dg
            db_ref[...] += db

    row = pl.BlockSpec((tm, d), lambda i: (i, 0))
    vec = pl.BlockSpec((1, d), lambda i: (0, 0))
    return pl.pallas_call(
        body, name=name, grid=(t // tm,),
        in_specs=[row, row, row, pl.BlockSpec((tm, 1), lambda i: (i, 0)), vec],
        out_specs=[row, row, vec, vec],
        out_shape=[jax.ShapeDtypeStruct((t, d), F32), jax.ShapeDtypeStruct((t, d), BF16),
                   jax.ShapeDtypeStruct((1, d), F32), jax.ShapeDtypeStruct((1, d), F32)],
        compiler_params=_params("arbitrary"),
    )(dmain, dres, xhat, rstd, gamma)


def _loss_and_ln_backward(xhat, rstd, gamma, beta, target, tm):
    t, d = xhat.shape

    def body(x_ref, r_ref, g_ref, b_ref, t_ref, du_ref, dub_ref, dg_ref, db_ref, loss_ref):
        xh = x_ref[...]
        g = g_ref[...]
        diff = xh * g + b_ref[...] - t_ref[...]
        part = jnp.sum(jnp.sum(diff * diff, axis=1, keepdims=True), axis=0, keepdims=True) * (0.5 / d)
        dy = diff * (1.0 / d)
        du, dg, db = _ln_backward_math(dy, xh, r_ref[...], g)
        du_ref[...] = du
        dub_ref[...] = du.astype(BF16)
        lossrow = jnp.broadcast_to(part, (1, HEAD_DIM))
        first = pl.program_id(0) == 0

        @pl.when(first)
        def _():
            dg_ref[...] = dg
            db_ref[...] = db
            loss_ref[...] = lossrow

        @pl.when(jnp.logical_not(first))
        def _():
            dg_ref[...] += dg
            db_ref[...] += db
            loss_ref[...] += lossrow

    row = pl.BlockSpec((tm, d), lambda i: (i, 0))
    vec = pl.BlockSpec((1, d), lambda i: (0, 0))
    return pl.pallas_call(
        body, name="loss_ln3_bwd", grid=(t // tm,),
        in_specs=[row, pl.BlockSpec((tm, 1), lambda i: (i, 0)), vec, vec, row],
        out_specs=[row, row, vec, vec, pl.BlockSpec((1, HEAD_DIM), lambda i: (0, 0))],
        out_shape=[jax.ShapeDtypeStruct((t, d), F32), jax.ShapeDtypeStruct((t, d), BF16),
                   jax.ShapeDtypeStruct((1, d), F32), jax.ShapeDtypeStruct((1, d), F32),
                   jax.ShapeDtypeStruct((1, HEAD_DIM), F32)],
        compiler_params=_params("arbitrary"),
    )(xhat, rstd, gamma, beta, target)


def _after(token, a):
    return a if token is None else a + token[:1, :1].astype(a.dtype)


def _pick(n, prefs):
    for p in prefs:
        if n % p == 0:
            return p
    return n


def _local_step(x, mem, target, w, x_bf=None):
    t, d = x.shape
    heads = w["a_log"].shape[1]
    gw = heads * HEAD_DIM
    groups, cg, _ = w["pool_w"].shape
    pw = groups * cg
    n_main = 4 * gw + pw
    in_cols = n_main + 2 * heads
    s_in = w["w_in_t"].shape[0]

    tm = _pick(t, (512, 256, 128))
    tm_ln = _pick(t, (256, 128))
    tm_big = _pick(t, (1024, 512, 256, 128))
    tk = _pick(d, K_STEPS)

    w_in_t = w["w_in_t"].reshape(in_cols, d)
    w_p_t = w_in_t[4 * gw + 2 * heads:]
    w_ba_t = jnp.pad(w_in_t[4 * gw:4 * gw + 2 * heads], ((0, HEAD_DIM - 2 * heads), (0, 0)))
    x_bf = x.astype(BF16) if x_bf is None else x_bf
    mem_bf = mem.astype(BF16)

    tn_d = _pick(d, (1024, 512, 256, 128))
    proj = _plain("proj_main", x_bf, w_in_t, tb=True, n_used=4 * gw, tm=tm_big, tn=_pick(4 * gw, (1024, 512, 256, 128)),
                  tk=tk, out_dtype=F32)
    pproj = _plain("proj_pool", x_bf, w_p_t, tb=True, tm=tm_big, tn=_pick(pw, (1024, 512, 256, 128)), tk=tk, out_dtype=F32)
    ea, dtb = _gate_vectors(w["a_log"], w["dt_bias"], heads)
    vec128 = lambda i, j: (0, 0)
    ba, bg = _matmul(
        "proj_gates", x_bf, w_ba_t, tb=True, tm=tm, tn=HEAD_DIM, tk=tk,
        extra=[(ea, (1, HEAD_DIM), vec128), (dtb, (1, HEAD_DIM), vec128)],
        outs=[(jax.ShapeDtypeStruct((t, HEAD_DIM), F32), (tm, HEAD_DIM), _tile)] * 2,
        epilogue=_gates_epilogue(heads))
    qkv = _gdn_pre(proj, w["conv_w"], heads)
    o_gdn, states, *solved = _gdn_core(qkv, bg, heads)
    cat_g = _gdn_post(o_gdn, proj, 3 * gw, w["gdn_norm_w"], heads, tm_big)
    token = yield ("pass", 1, cat_g)
    cat_p = _pool_forward(pproj, 0, w["pool_w"], _after(token, w["pool_scale"]))
    cat = jnp.concatenate([cat_g, cat_p], axis=1)
    w = {**w, **(yield ("weights", 1, cat))}
    h1, h1_bf, xhat1, rstd1 = _ln_forward("mix_ln1", cat, w["w_out"], x, w["ln1_g"], w["ln1_b"], tm=tm_ln, tk=tk)

    h1_bf = _after((yield ("relay", None, h1_bf)), h1_bf)
    q = _plain("xattn_q", h1_bf, w["xq_w"], tm=tm, tn=tn_d, tk=tk, out_dtype=BF16)
    mlen = mem.shape[0]
    tm_mem = _pick(mlen, (256, 128))
    k = _plain("xattn_k", mem_bf, w["xk_w"], tm=tm_mem, tn=tn_d, tk=tk, out_dtype=BF16)
    v = _plain("xattn_v", mem_bf, w["xv_w"], tm=tm_mem, tn=tn_d, tk=tk, out_dtype=BF16)
    att = _attention(q, k, v, tm_big)
    h2, h2_bf, xhat2, rstd2 = _ln_forward("xo_ln2", att, w["xo_w"], h1, w["ln2_g"], w["ln2_b"], tm=tm_ln, tk=tk)

    w = {**w, **(yield ("weights", 2, h2_bf))}
    s_up = w["w_up3"].shape[0]
    ff = s_up * w["w_up3"].shape[2]
    tn_f = _pick(ff // s_up, (1024, 512, 256, 128))

    def up_epi(acc, ex, out, i):
        r = jnp.maximum(acc, 0.0)
        out[0][...] = (r * r).astype(BF16)
        out[1][...] = (2.0 * r).astype(BF16)

    act, act_grad = _matmul(
        "mlp_up", h2_bf, w["w_up3"], b_blocks=s_up, tm=tm_big, tn=tn_f, tk=tk,
        outs=[(jax.ShapeDtypeStruct((t, ff), BF16), (tm_big, tn_f), _tile)] * 2, epilogue=up_epi)
    w = {**w, **(yield ("weights", 3, act))}
    tk_f = _pick(ff, K_STEPS)
    xhat3, rstd3 = _ln_forward("down_ln3", act, w["w_down"], h2, w["ln3_g"], w["ln3_b"], tm=tm, tk=tk_f, want_h=False)

    grads = {}
    du3, du3_bf, grads["ln3_g"], grads["ln3_b"], loss = _loss_and_ln_backward(
        xhat3, rstd3, w["ln3_g"], w["ln3_b"], target, tm)

    def dup_epi(acc, ex, out, i):
        out[0][...] = (acc * ex[0][...].astype(F32)).astype(BF16)

    dup = _matmul(
        "mlp_down_dx", du3_bf, w["w_down"], tb=True, tm=tm_big, tn=tn_f, tk=tk,
        extra=[(act_grad, (tm_big, tn_f), _tile)],
        outs=[(jax.ShapeDtypeStruct((t, ff), BF16), (tm_big, tn_f), _tile)], epilogue=dup_epi)[0]
    tk_t = _pick(t, K_STEPS)
    tm_w = _pick(d, (512, 256, 128))
    grads["w_down"] = _plain("mlp_down_dw", act, du3_bf, ta=True, tm=_pick(ff, (512, 256, 128)), tn=d, tk=tk_t,
                             out_dtype=F32)
    grads["w_up3"] = _plain("mlp_up_dw", h2_bf, dup, ta=True, tm=tm_w, tn=ff // s_up, tk=tk_t, out_dtype=F32, out3=s_up,
                            n_outer=True)
    token = yield ("grads", 0, {n: grads.pop(n) for n in ("w_down", "w_up3")})
    dh2 = _plain("mlp_up_dx", dup, w["w_up3"], tb=True, b_blocks=s_up, tm=tm_big, tn=tn_d,
                 tk=_pick(ff // s_up, K_STEPS), out_dtype=F32)
    du2, du2_bf, grads["ln2_g"], grads["ln2_b"] = _ln_backward_rows(
        "ln2_bwd", dh2, du3, xhat2, rstd2, _after(token, w["ln2_g"]), tm)
    token = yield ("poll", 0, du2_bf)

    grads["xo_w"] = _plain("xo_dw", att, du2_bf, ta=True, tm=tm_w, tn=d, tk=tk_t, out_dtype=F32)
    datt = _plain("xo_dx", du2_bf, w["xo_w"], tb=True, tm=tm, tn=tn_d, tk=tk, out_dtype=BF16)
    dq, dk, dv = _attention_backward(q, k, v, datt, tm_big)
    tk_m = _pick(mlen, (256, 128))
    grads["xq_w"] = _plain("xq_dw", h1_bf, dq, ta=True, tm=tm_w, tn=d, tk=tk_t, out_dtype=F32)
    grads["xk_w"] = _plain("xk_dw", mem_bf, dk, ta=True, tm=tm_w, tn=tn_d, tk=tk_m, out_dtype=F32)
    grads["xv_w"] = _plain("xv_dw", mem_bf, dv, ta=True, tm=tm_w, tn=tn_d, tk=tk_m, out_dtype=F32)
    du1, du1_bf, grads["ln1_g"], grads["ln1_b"] = _ln_backward(
        "xq_dx_ln1", dq, w["xq_w"], du2, xhat1, rstd1, _after(token, w["ln1_g"]), tm=tm_ln, tk=tk)

    grads["w_out"] = _plain("out_dw", cat, du1_bf, ta=True, tm=tm_w, tn=d, tk=tk_t, out_dtype=F32)
    token = yield ("grads", 1, {n: grads.pop(n) for n in ("xo_w", "xq_w", "xk_w", "xv_w", "w_out")})
    dcat = _plain("out_dx", du1_bf, w["w_out"], tb=True, tm=tm, tn=tn_d, tk=tk, out_dtype=F32)
    dp, grads["pool_w"], grads["pool_scale"] = _pool_backward(dcat, gw, pproj, 0, w["pool_w"],
                                                              _after(token, w["pool_scale"]))
    do_gdn, dz, grads["gdn_norm_w"] = _gdn_post_backward(dcat, o_gdn, proj, 3 * gw, _after(token, w["gdn_norm_w"]),
                                                         heads, tm_big)
    dqkv, dbg = _gdn_core_backward(qkv, bg, states, solved, do_gdn, heads)
    token = yield ("poll", 1, dqkv)
    dqkv_pre, grads["conv_w"] = _gdn_pre_backward(proj, _after(token, w["conv_w"]), dqkv, heads)
    dba, dalog_row, ddt_row = _gates_backward(ba, bg, dbg, ea, dtb, heads)
    grads["a_log"] = dalog_row[:, heads:2 * heads]
    grads["dt_bias"] = ddt_row[:, heads:2 * heads]

    k_pad = -(-in_cols // (2 * HEAD_DIM)) * (2 * HEAD_DIM)
    dproj = jnp.concatenate([dqkv_pre, dz, dba[:, :2 * heads], dp, jnp.zeros((t, k_pad - in_cols), BF16)], axis=1)
    dw_in_t = _plain("proj_dw", dproj, x_bf, ta=True, tm=_pick(k_pad, (512, 256, 128)), tn=d, tk=tk_t, out_dtype=F32,
                     m_kept=in_cols)
    grads["w_in_t"] = dw_in_t.reshape(s_in, in_cols // s_in, d)

    def dx_epi(acc, ex, out, i):
        out[0][...] = acc + ALPHA * ex[0][...]

    token = yield ("grads", 2, {n: grads.pop(n) for n in ("w_in_t", "pool_w")})
    w_in_t_pad = jnp.concatenate([w_in_t, _after(token, jnp.zeros((k_pad - in_cols, d), BF16))], axis=0)
    grad_x = _matmul(
        "proj_dx", dproj, w_in_t_pad, tm=tm, tn=tn_d, tk=k_pad, extra=[(du1, (tm, tn_d), _tile)],
        outs=[(jax.ShapeDtypeStruct((t, d), F32), (tm, tn_d), _tile)], epilogue=dx_epi)[0]
    yield ("poll", 2, grad_x)
    return loss, grad_x, grads


def _adamw(name, w, g, m, v):
    r, c = w.shape
    if r % 8 == 0:
        tr = _pick(r, (256, 128, 64, 32, 16, 8))
        blk, steps = pl.BlockSpec((tr, c), lambda i: (i, 0)), r // tr
    else:
        tc = _pick(c, (256, 128))
        blk, steps = pl.BlockSpec((r, tc), lambda i: (0, i)), c // tc
    c1 = 1.0 - ADAM_B1 ** ADAM_STEP
    c2 = 1.0 - ADAM_B2 ** ADAM_STEP

    def body(w_ref, g_ref, m_ref, v_ref, d_ref, mo_ref, vo_ref, go_ref):
        gv = g_ref[...]
        mn = ADAM_B1 * m_ref[...] + (1.0 - ADAM_B1) * gv
        vn = ADAM_B2 * v_ref[...] + (1.0 - ADAM_B2) * (gv * gv)
        d_ref[...] = -ADAM_LR * ((mn / c1) / (jnp.sqrt(vn / c2) + ADAM_EPS) + ADAM_WD * w_ref[...])
        mo_ref[...] = mn
        vo_ref[...] = vn
        go_ref[...] = gv

    return pl.pallas_call(
        body, name=name, grid=(steps,), in_specs=[blk] * 4, out_specs=[blk] * 4,
        out_shape=[jax.ShapeDtypeStruct((r, c), F32)] * 4,
        compiler_params=_params("parallel"),
    )(w, g, m, v)


def _place():
    x, y, c = lax.axis_index("x"), lax.axis_index("y"), lax.axis_index("c")
    chips = [(1 - x, y), (x, 1 - y), (1 - x, 1 - y)]
    return x, y, c, chips


HBM = pl.BlockSpec(memory_space=pltpu.HBM)


SEM = pl.BlockSpec(memory_space=pltpu.SEMAPHORE)
ANY = pl.BlockSpec(memory_space=pl.ANY)
EFFECT = pltpu.SideEffectType.DATAFLOW_SIDE_EFFECTING


def _in_hbm(a):
    return pltpu.with_memory_space_constraint(a, pltpu.HBM)


def _remote(src, dst, send_sem, recv_sem, to):
    return pltpu.make_async_remote_copy(src_ref=src, dst_ref=dst, send_sem=send_sem, recv_sem=recv_sem,
                                        device_id=to, device_id_type=MESH)


def _by_rows(rows):
    return rows % 32 == 0


def _half_shape(rows, cols):
    return (rows // 2, cols) if _by_rows(rows) else (rows, cols // 2)


def _half(ref, which, *lead):
    rows, cols = ref.shape[-2:]
    if _by_rows(rows):
        return ref.at[(*lead, pl.ds(which * (rows // 2), rows // 2))]
    return ref.at[(*lead, slice(None), pl.ds(which * (cols // 2), cols // 2))]


def _landed(lands, i, shard_index, which):
    return _half(lands[i], which, shard_index)


def _routes():
    x, y, c, _ = _place()
    first = (jnp.where(c == 0, 1 - x, x), jnp.where(c == 0, y, 1 - y))
    second = (jnp.where(c == 0, x, 1 - x), jnp.where(c == 0, 1 - y, y))
    return first, second, (1 - x, 1 - y)


def _shard_of(chip):
    return 2 * chip[0] + chip[1]


def _gather_start(name, shards, after, relayed=()):
    n = len(shards)
    lands = [lax.empty((N_SHARD,) + s.shape, s.dtype) for s in shards]

    def body(*refs):
        ins, zones = refs[:n], refs[n:2 * n]
        ici_send, ici_recv, own_send, own_recv = refs[2 * n + 1:2 * n + 5]
        token = refs[-1]
        x, y, c, chips = _place()
        me = 2 * x + y
        first, _, _ = _routes()
        for i in range(n):
            if i in relayed:
                _remote(_half(ins[i], c), _landed(zones, i, me, c), ici_send.at[3 * i], ici_recv.at[3 * i],
                        (*first, c)).start()
                continue
            for j, chip in enumerate(chips):
                _remote(_half(ins[i], c), _landed(zones, i, me, c), ici_send.at[3 * i + j],
                        ici_recv.at[3 * i + j], (*chip, c)).start()
        for i in range(n):
            _remote(ins[i], zones[i].at[me], own_send.at[i], own_recv.at[i], (x, y, 1 - c)).start()
        token[...] = jnp.zeros_like(token)

    dma = pltpu.SemaphoreType.DMA
    outs = pl.pallas_call(
        body, name=name,
        in_specs=[HBM] * (2 * n) + [ANY],
        out_shape=(dma((3 * n,)), dma((3 * n,)), dma((n,)), dma((n,)),
                   *[pltpu.HBM(a.shape, a.dtype) for a in shards + lands], jax.ShapeDtypeStruct((8, LANES), F32)),
        out_specs=(SEM, SEM, SEM, SEM, *[HBM] * (2 * n), pl.BlockSpec(memory_space=pltpu.VMEM)),
        input_output_aliases={k: 4 + k for k in range(2 * n)},
        compiler_params=pltpu.CompilerParams(has_side_effects=EFFECT),
    )(*[_in_hbm(a) for a in shards + lands], after)
    sems = dict(zip(("ici_send", "ici_recv", "own_send", "own_recv"), outs[:4]))
    return sems, list(outs[4:4 + n]), list(outs[4 + n:4 + 2 * n]), outs[-1]


def _gather_forward(name, idx, lands, sems, after):
    n = len(idx)

    def body(*refs):
        zones = refs[:n]
        ici_recv = refs[n]
        fwd_send, fwd_recv = refs[n + 2], refs[n + 3]
        x, y, c, chips = _place()
        for k, i in enumerate(idx):
            for j, chip in enumerate(chips):
                half = _landed(zones, k, 2 * chip[0] + chip[1], c)
                _remote(half, half, fwd_send.at[3 * k + j], ici_recv.at[3 * i + j], (*chip, c)).wait_recv()
                _remote(half, half, fwd_send.at[3 * k + j], fwd_recv.at[3 * k + j], (x, y, 1 - c)).start()
        refs[-1][...] = jnp.zeros_like(refs[-1])

    dma = pltpu.SemaphoreType.DMA
    outs = pl.pallas_call(
        body, name=name,
        in_specs=[HBM] * n + [SEM, ANY],
        out_shape=(dma((3 * n,)), dma((3 * n,)), *[pltpu.HBM(a.shape, a.dtype) for a in lands],
                   jax.ShapeDtypeStruct((8, LANES), F32)),
        out_specs=(SEM, SEM, *[HBM] * n, pl.BlockSpec(memory_space=pltpu.VMEM)),
        input_output_aliases={k: 2 + k for k in range(n)},
        compiler_params=pltpu.CompilerParams(has_side_effects=EFFECT),
    )(*lands, sems["ici_recv"], after)
    return (outs[0], outs[1]), list(outs[2:2 + n]), outs[-1]


def _gather_wait(name, idx, shards, lands, sems, fwd, after):
    n = len(idx)

    def body(*refs):
        ins, zones = refs[:n], refs[n:2 * n]
        ici_send, own_send, own_recv, fwd_send, fwd_recv = refs[2 * n:2 * n + 5]
        x, y, c, chips = _place()
        me = 2 * x + y
        for k, i in enumerate(idx):
            mine = _half(ins[k], c)
            for j, chip in enumerate(chips):
                theirs = 2 * chip[0] + chip[1]
                _remote(mine, _landed(zones, k, me, c), ici_send.at[3 * i + j], fwd_recv.at[3 * k + j],
                        (*chip, c)).wait_send()
                sent = _landed(zones, k, theirs, c)
                _remote(sent, sent, fwd_send.at[3 * k + j], fwd_recv.at[3 * k + j], (x, y, 1 - c)).wait_send()
                passed = _landed(zones, k, theirs, 1 - c)
                _remote(passed, passed, fwd_send.at[3 * k + j], fwd_recv.at[3 * k + j], (x, y, 1 - c)).wait_recv()
            own = _remote(ins[k], zones[k].at[me], own_send.at[i], own_recv.at[i], (x, y, 1 - c))
            own.wait_send()
            own.wait_recv()

    outs = pl.pallas_call(
        body, name=name,
        in_specs=[HBM] * (2 * n) + [SEM] * 5 + [ANY],
        out_shape=tuple(pltpu.HBM(a.shape, a.dtype) for a in lands),
        out_specs=tuple([HBM] * n),
        input_output_aliases={n + k: k for k in range(n)},
        compiler_params=pltpu.CompilerParams(has_side_effects=EFFECT),
    )(*shards, *lands, sems["ici_send"], sems["own_send"], sems["own_recv"], fwd[0], fwd[1], after)
    return list(outs)


def _gather_relay(name, idx, shards, lands, sems, after):
    n = len(idx)

    def body(*refs):
        ins, zones, ici_recv = refs[:n], refs[n:2 * n], refs[2 * n]
        relay_send, relay_recv, pass_send, pass_recv = refs[2 * n + 2:2 * n + 6]
        x, y, c, _ = _place()
        first, second, _ = _routes()
        for k, i in enumerate(idx):
            landed = _landed(zones, k, _shard_of(first), c)
            _remote(landed, landed, pass_send.at[k], ici_recv.at[3 * i], (*first, c)).wait_recv()
            _remote(_half(ins[k], c), _landed(zones, k, 2 * x + y, c), relay_send.at[2 * k], relay_recv.at[2 * k],
                    (*second, c)).start()
            _remote(landed, landed, relay_send.at[2 * k + 1], relay_recv.at[2 * k + 1], (*second, c)).start()
            _remote(landed, landed, pass_send.at[k], pass_recv.at[k], (x, y, 1 - c)).start()
        refs[-1][...] = jnp.zeros_like(refs[-1])

    dma = pltpu.SemaphoreType.DMA
    outs = pl.pallas_call(
        body, name=name,
        in_specs=[HBM] * (2 * n) + [SEM, ANY],
        out_shape=(dma((2 * n,)), dma((2 * n,)), dma((n,)), dma((n,)), *[pltpu.HBM(a.shape, a.dtype) for a in lands],
                   jax.ShapeDtypeStruct((8, LANES), F32)),
        out_specs=(SEM, SEM, SEM, SEM, *[HBM] * n, pl.BlockSpec(memory_space=pltpu.VMEM)),
        input_output_aliases={n + k: 4 + k for k in range(n)},
        compiler_params=pltpu.CompilerParams(has_side_effects=EFFECT),
    )(*shards, *lands, sems["ici_recv"], after)
    return outs[:4], list(outs[4:4 + n]), outs[-1]


def _gather_forward_relayed(name, ks, lands, relay, after):
    n = len(ks)

    def body(*refs):
        zones, relay_recv = refs[:n], refs[n]
        fwd_send, fwd_recv = refs[n + 2], refs[n + 3]
        x, y, c, _ = _place()
        _, second, diagonal = _routes()
        for p, k in enumerate(ks):
            for j, chip in enumerate((second, diagonal)):
                landed = _landed(zones, p, _shard_of(chip), c)
                _remote(landed, landed, fwd_send.at[2 * p + j], relay_recv.at[2 * k + j], (*second, c)).wait_recv()
                _remote(landed, landed, fwd_send.at[2 * p + j], fwd_recv.at[2 * p + j], (x, y, 1 - c)).start()

    dma = pltpu.SemaphoreType.DMA
    outs = pl.pallas_call(
        body, name=name,
        in_specs=[HBM] * n + [SEM, ANY],
        out_shape=(dma((2 * n,)), dma((2 * n,)), *[pltpu.HBM(a.shape, a.dtype) for a in lands]),
        out_specs=(SEM, SEM, *[HBM] * n),
        input_output_aliases={k: 2 + k for k in range(n)},
        compiler_params=pltpu.CompilerParams(has_side_effects=EFFECT),
    )(*lands, relay[1], after)
    return (outs[0], outs[1]), list(outs[2:])


def _gather_wait_relayed(name, idx, ks, shards, lands, sems, relay, fwd, after):
    n = len(idx)

    def body(*refs):
        ins, zones = refs[:n], refs[n:2 * n]
        ici_send, own_send, own_recv, relay_send, pass_send, pass_recv, fwd_send, fwd_recv = refs[2 * n:2 * n + 8]
        x, y, c, _ = _place()
        me = 2 * x + y
        sibling = (x, y, 1 - c)
        first, second, diagonal = _routes()
        for p, (i, k) in enumerate(zip(idx, ks)):
            mine, at_peer = _half(ins[p], c), _landed(zones, p, me, c)
            from_first = _landed(zones, p, _shard_of(first), c)
            _remote(mine, at_peer, ici_send.at[3 * i], pass_recv.at[k], (*first, c)).wait_send()
            _remote(mine, at_peer, relay_send.at[2 * k], pass_recv.at[k], (*second, c)).wait_send()
            _remote(from_first, from_first, relay_send.at[2 * k + 1], pass_recv.at[k], (*second, c)).wait_send()
            _remote(from_first, from_first, pass_send.at[k], pass_recv.at[k], sibling).wait_send()
            theirs = _landed(zones, p, _shard_of(second), 1 - c)
            _remote(theirs, theirs, pass_send.at[k], pass_recv.at[k], sibling).wait_recv()
            for j, (sent, got) in enumerate(((second, first), (diagonal, diagonal))):
                out_half = _landed(zones, p, _shard_of(sent), c)
                _remote(out_half, out_half, fwd_send.at[2 * p + j], fwd_recv.at[2 * p + j], sibling).wait_send()
                in_half = _landed(zones, p, _shard_of(got), 1 - c)
                _remote(in_half, in_half, fwd_send.at[2 * p + j], fwd_recv.at[2 * p + j], sibling).wait_recv()
            own = _remote(ins[p], zones[p].at[me], own_send.at[i], own_recv.at[i], sibling)
            own.wait_send()
            own.wait_recv()

    outs = pl.pallas_call(
        body, name=name,
        in_specs=[HBM] * (2 * n) + [SEM] * 8 + [ANY],
        out_shape=tuple(pltpu.HBM(a.shape, a.dtype) for a in lands),
        out_specs=tuple([HBM] * n),
        input_output_aliases={n + k: k for k in range(n)},
        compiler_params=pltpu.CompilerParams(has_side_effects=EFFECT),
    )(*shards, *lands, sems["ici_send"], sems["own_send"], sems["own_recv"], relay[0], relay[2], relay[3],
      fwd[0], fwd[1], after)
    return list(outs)


def _all_reduce_small(name, slab, after=None):
    r, width = slab.shape
    ndev = 8

    def body(x_ref, after_ref, out_ref, buf, send_sems, recv_sems):
        x, y, c, _ = _place()
        me = 4 * x + 2 * y + c
        buf[me] = x_ref[...]
        copies = []
        for k in range(1, ndev):
            peer = jnp.bitwise_xor(me, k)
            to = (peer // 4, (peer // 2) % 2, peer % 2)
            cp = pltpu.make_async_remote_copy(src_ref=x_ref, dst_ref=buf.at[me], send_sem=send_sems.at[k - 1],
                                              recv_sem=recv_sems.at[k - 1], device_id=to, device_id_type=MESH)
            cp.start()
            copies.append(cp)
        for k in range(1, ndev):
            peer = jnp.bitwise_xor(me, k)
            pltpu.make_async_remote_copy(src_ref=x_ref, dst_ref=buf.at[peer], send_sem=send_sems.at[k - 1],
                                         recv_sem=recv_sems.at[k - 1], device_id=(x, y, c),
                                         device_id_type=MESH).wait_recv()
        for cp in copies:
            cp.wait_send()
        total = buf[0]
        for d in range(1, ndev):
            total = total + buf[d]
        out_ref[...] = total

    return pl.pallas_call(
        body, name=name,
        in_specs=[pl.BlockSpec(memory_space=pltpu.VMEM), ANY], out_specs=pl.BlockSpec(memory_space=pltpu.VMEM),
        out_shape=jax.ShapeDtypeStruct((r, width), F32),
        scratch_shapes=[pltpu.VMEM((ndev, r, width), F32), pltpu.SemaphoreType.DMA((ndev - 1,)),
                        pltpu.SemaphoreType.DMA((ndev - 1,))],
        compiler_params=pltpu.CompilerParams(vmem_limit_bytes=VMEM_LIMIT),
    )(slab, slab if after is None else after)


def _half_tiling(rows, cols):
    if _by_rows(rows):
        tr = _pick(rows // 2, (256, 128, 64, 32, 16))
        nb = (rows // 2) // tr
        return (tr, cols), nb, (lambda which, b: (which * nb + b, 0)), (lambda b: (b, 0))
    tc = _pick(cols // 2, (256, 128))
    nb = (cols // 2) // tc
    return (rows, tc), nb, (lambda which, b: (0, which * nb + b)), (lambda b: (0, b))


def _chip_partial(name, grad, other, core):
    s, r, cdim = grad.shape
    blk, nb, whole, within = _half_tiling(r, cdim)

    def body(core_ref, g_ref, o_ref, out_ref):
        out_ref[...] = (g_ref[...] + o_ref[...]).astype(BF16)

    return pl.pallas_call(
        body, name=name,
        grid_spec=pltpu.PrefetchScalarGridSpec(
            num_scalar_prefetch=1, grid=(s, nb),
            in_specs=[pl.BlockSpec((None,) + blk, lambda j, b, core_ref: (j,) + whole(core_ref[0], b)),
                      pl.BlockSpec((None,) + blk, lambda j, b, core_ref: (j,) + within(b))],
            out_specs=pl.BlockSpec((None,) + blk, lambda j, b, core_ref: (j,) + within(b))),
        out_shape=jax.ShapeDtypeStruct((s,) + _half_shape(r, cdim), BF16),
        compiler_params=_params("parallel", "parallel"),
    )(core, grad, other)


def _partial_copies(ins, zones, send_sems, recv_sems):
    x, y, c, chips = _place()
    return [_remote(ins[i].at[2 * chip[0] + chip[1]], zones[i].at[j], send_sems.at[3 * i + j],
                    recv_sems.at[3 * i + j], (*chip, c))
            for i in range(len(ins)) for j, chip in enumerate(chips)]


def _swap_copies(ins, zones, send_sems, recv_sems):
    x, y, c, _ = _place()
    copies = []
    for i in range(len(ins)):
        for s in range(N_SHARD):
            copies.append(_remote(_half(ins[i], 1 - c, s), zones[i].at[s],
                                  send_sems.at[N_SHARD * i + s], recv_sems.at[N_SHARD * i + s], (x, y, 1 - c)))
    return copies


def _exchange_start(name, plan, sources, lands, per_array):
    n = len(sources)
    lands = [lax.empty(shape, dtype) for shape, dtype in lands]

    def body(*refs):
        for cp in plan(refs[:n], refs[n:2 * n], refs[2 * n], refs[2 * n + 1]):
            cp.start()
        refs[-1][...] = jnp.zeros_like(refs[-1])

    dma = pltpu.SemaphoreType.DMA
    outs = pl.pallas_call(
        body, name=name,
        in_specs=[HBM] * (2 * n),
        out_shape=(dma((per_array * n,)), dma((per_array * n,)),
                   *[pltpu.HBM(a.shape, a.dtype) for a in list(sources) + lands], jax.ShapeDtypeStruct((8, LANES), F32)),
        out_specs=(SEM, SEM, *[HBM] * (2 * n), pl.BlockSpec(memory_space=pltpu.VMEM)),
        input_output_aliases={k: 2 + k for k in range(2 * n)},
        compiler_params=pltpu.CompilerParams(has_side_effects=EFFECT),
    )(*[_in_hbm(a) for a in list(sources) + lands])
    return (outs[0], outs[1]), list(outs[2:2 + n]), list(outs[2 + n:2 + 2 * n]), outs[-1]


def _exchange_wait(name, plan, started, after):
    sems, partials, lands, _ = started
    n = len(partials)

    def body(*refs):
        for cp in plan(refs[:n], refs[n:2 * n], refs[2 * n], refs[2 * n + 1]):
            cp.wait_send()
            cp.wait_recv()

    outs = pl.pallas_call(
        body, name=name,
        in_specs=[HBM] * (2 * n) + [SEM, SEM] + [ANY] * len(after),
        out_shape=tuple(pltpu.HBM(a.shape, a.dtype) for a in lands),
        out_specs=tuple([HBM] * n),
        input_output_aliases={n + k: k for k in range(n)},
        compiler_params=pltpu.CompilerParams(has_side_effects=EFFECT),
    )(*partials, *lands, sems[0], sems[1], *after)
    return list(outs)


def _reduce_own(name, grad, other, received, where):
    s, r, cdim = grad.shape
    blk, nb, whole, within = _half_tiling(r, cdim)

    def body(where_ref, g_ref, o_ref, r_ref, out_ref):
        total = g_ref[...] + o_ref[...]
        for j in range(3):
            total = total + r_ref[j].astype(F32)
        out_ref[...] = total

    return pl.pallas_call(
        body, name=name,
        grid_spec=pltpu.PrefetchScalarGridSpec(
            num_scalar_prefetch=1, grid=(nb,),
            in_specs=[pl.BlockSpec((None,) + blk, lambda b, w_ref: (w_ref[0],) + whole(w_ref[1], b)),
                      pl.BlockSpec((None,) + blk, lambda b, w_ref: (w_ref[0],) + within(b)),
                      pl.BlockSpec((3,) + blk, lambda b, w_ref: (0,) + within(b))],
            out_specs=pl.BlockSpec(blk, lambda b, w_ref: whole(w_ref[1], b))),
        out_shape=jax.ShapeDtypeStruct((r, cdim), F32),
        compiler_params=_params("parallel"),
    )(where, grad, other, received)


def _join_start(name, halves):
    n = len(halves)

    def body(*refs):
        bufs, send_sems, recv_sems = refs[:n], refs[n], refs[n + 1]
        x, y, c, _ = _place()
        for i in range(n):
            mine = _half(bufs[i], c)
            _remote(mine, mine, send_sems.at[i], recv_sems.at[i], (x, y, 1 - c)).start()
        refs[-1][...] = jnp.zeros_like(refs[-1])

    dma = pltpu.SemaphoreType.DMA
    outs = pl.pallas_call(
        body, name=name,
        in_specs=[HBM] * n,
        out_shape=(dma((n,)), dma((n,)), *[pltpu.HBM(h.shape, F32) for h in halves], jax.ShapeDtypeStruct((8, LANES), F32)),
        out_specs=(SEM, SEM, *[HBM] * n, pl.BlockSpec(memory_space=pltpu.VMEM)),
        input_output_aliases={k: 2 + k for k in range(n)},
        compiler_params=pltpu.CompilerParams(has_side_effects=EFFECT),
    )(*[_in_hbm(h) for h in halves])
    return (outs[0], outs[1]), list(outs[2:2 + n]), outs[-1]


def _join_wait(name, started, after):
    sems, bufs, _ = started
    n = len(bufs)

    def body(*refs):
        bufs, send_sems, recv_sems = refs[:n], refs[n], refs[n + 1]
        x, y, c, _ = _place()
        for i in range(n):
            mine, theirs = _half(bufs[i], c), _half(bufs[i], 1 - c)
            _remote(mine, mine, send_sems.at[i], recv_sems.at[i], (x, y, 1 - c)).wait_send()
            _remote(theirs, theirs, send_sems.at[i], recv_sems.at[i], (x, y, 1 - c)).wait_recv()

    outs = pl.pallas_call(
        body, name=name,
        in_specs=[HBM] * n + [SEM, SEM] + [ANY] * len(after),
        out_shape=tuple(pltpu.HBM(b.shape, F32) for b in bufs),
        out_specs=tuple([HBM] * n),
        input_output_aliases={k: k for k in range(n)},
        compiler_params=pltpu.CompilerParams(has_side_effects=EFFECT),
    )(*bufs, sems[0], sems[1], *after)
    return list(outs)


BIG = ("w_in", "pool_w", "w_out", "xq_w", "xk_w", "xv_w", "xo_w", "w_up", "w_down", "conv_w")
KEPT_F32 = ("conv_w",)
GATHER_GROUPS = ((0, 1, 9), (2, 3, 4, 5, 6), (7,), (8,))
RELAYED = (7, 8)
SMALL = ("conv_w", "a_log", "dt_bias", "gdn_norm_w", "pool_scale", "ln1_g", "ln1_b", "ln2_g", "ln2_b", "ln3_g", "ln3_b")
ORDER = ("w_in", "conv_w", "a_log", "dt_bias", "gdn_norm_w", "pool_w", "pool_scale", "w_out", "ln1_g", "ln1_b",
         "xq_w", "xk_w", "xv_w", "xo_w", "ln2_g", "ln2_b", "w_up", "w_down", "ln3_g", "ln3_b")
LANES = 128


def _rows(flat_len):
    return -(-flat_len // LANES)


def _pack(pieces):
    out = []
    for p in pieces:
        flat = p.reshape(-1).astype(F32)
        out.append(jnp.pad(flat, (0, _rows(flat.shape[0]) * LANES - flat.shape[0])).reshape(-1, LANES))
    slab = jnp.concatenate(out, axis=0)
    return jnp.pad(slab, ((0, -slab.shape[0] % 8), (0, 0)))


def _unpack(slab, shapes):
    out, row = [], 0
    for shp in shapes:
        size = math.prod(shp)
        out.append(slab[row:row + _rows(size)].reshape(-1)[:size].reshape(shp))
        row += _rows(size)
    return out


TRANSPOSED = ("w_in",)


def _as2d(name, a):
    a = a[0]
    if name in TRANSPOSED:
        return jnp.swapaxes(a, 0, 1)
    return a.reshape(-1, a.shape[-1]) if a.ndim == 3 else a


def _from2d(name, a, shape):
    return (jnp.swapaxes(a, 0, 1) if name in TRANSPOSED else a).reshape(shape)


def kernel(x, mem, w_in, conv_w, a_log, dt_bias, gdn_norm_w, pool_w, pool_scale, w_out, ln1_g, ln1_b, xq_w, xk_w, xv_w, xo_w, ln2_g, ln2_b, w_up, w_down, ln3_g, ln3_b, loss_target, m_w_in, m_conv_w, m_a_log, m_dt_bias, m_gdn_norm_w, m_pool_w, m_pool_scale, m_w_out, m_ln1_g, m_ln1_b, m_xq_w, m_xk_w, m_xv_w, m_xo_w, m_ln2_g, m_ln2_b, m_w_up, m_w_down, m_ln3_g, m_ln3_b, v_w_in, v_conv_w, v_a_log, v_dt_bias, v_gdn_norm_w, v_pool_w, v_pool_scale, v_w_out, v_ln1_g, v_ln1_b, v_xq_w, v_xk_w, v_xv_w, v_xo_w, v_ln2_g, v_ln2_b, v_w_up, v_w_down, v_ln3_g, v_ln3_b):
    given = dict(locals())
    cx, cy, cc = lax.axis_index("x"), lax.axis_index("y"), lax.axis_index("c")
    me = 2 * cx + cy
    groups = pool_w.shape[1]
    cs = pool_w.shape[2]
    kk, conv_cols = conv_w.shape[1], conv_w.shape[2]
    core = cc.astype(jnp.int32).reshape(1)
    where = jnp.stack([me, cc]).astype(jnp.int32)

    started = {}
    wts = {}

    def start(name, idx, after, token=None):
        casts = [_after(token, _as2d(BIG[i], given[BIG[i]])).astype(F32 if BIG[i] in KEPT_F32 else BF16) for i in idx]
        relayed = tuple(k for k, i in enumerate(idx) if i in RELAYED)
        sems, shards, lands, token = _gather_start(name, casts, after, relayed)
        for k, i in enumerate(idx):
            started[i] = (sems, k, shards[k], lands[k])
        return token

    token = start("gather_start_first", GATHER_GROUPS[0], x)
    token = start("gather_start_rest", tuple(i for group in GATHER_GROUPS[1:] for i in group), token, token)

    relay = {}

    def send_on(after):
        members = [started[i] for i in RELAYED]
        relay["sems"], zones, token = _gather_relay("gather_relay", [m[1] for m in members], [m[2] for m in members],
                                                    [m[3] for m in members], members[0][0], after)
        relay["zones"] = dict(zip(RELAYED, zones))
        return token

    passed = {}

    def pass_on(group, after):
        members = [started[i] for i in GATHER_GROUPS[group]]
        fwd, zones, token = _gather_forward(f"gather_forward_{group}", [m[1] for m in members], [m[3] for m in members],
                                            members[0][0], after)
        passed[group] = (fwd, zones)
        return token

    def fetch(group, after):
        members = [started[i] for i in GATHER_GROUPS[group]]
        sems, idx = members[0][0], [m[1] for m in members]
        shards = [m[2] for m in members]
        if GATHER_GROUPS[group][0] in RELAYED:
            ks = [RELAYED.index(i) for i in GATHER_GROUPS[group]]
            zones = [relay["zones"][i] for i in GATHER_GROUPS[group]]
            fwd, zones = _gather_forward_relayed(f"gather_forward_{group}", ks, zones, relay["sems"], after)
            got = _gather_wait_relayed(f"gather_wait_{group}", idx, ks, shards, zones, sems, relay["sems"], fwd, after)
        else:
            if group not in passed:
                pass_on(group, after)
            fwd, zones = passed[group]
            got = _gather_wait(f"gather_wait_{group}", idx, shards, zones, sems, fwd, after)
        full = dict(zip([BIG[i] for i in GATHER_GROUPS[group]], got))
        out = {}
        for n, a in full.items():
            if n == "w_in":
                out["w_in_t"] = a
            elif n == "w_up":
                out["w_up3"] = a
            elif n == "pool_w":
                out[n] = a.reshape(N_SHARD, groups, cs, -1).transpose(1, 0, 2, 3).reshape(groups, N_SHARD * cs, -1)
            elif n == "conv_w":
                out[n] = a.transpose(1, 0, 2).reshape(kk, N_SHARD * conv_cols)
            else:
                out[n] = a.reshape(-1, a.shape[-1])
        return out

    for n in ("a_log", "dt_bias", "gdn_norm_w", "pool_scale", "ln1_g", "ln1_b", "ln2_g", "ln2_b", "ln3_g", "ln3_b"):
        wts[n] = given[n]
    x_bf = _after(token, x[0]).astype(BF16)
    wts.update(fetch(0, x_bf))

    def start_swap(group, grads):
        names, blocks = [], []
        for n, g in grads.items():
            if n == "pool_w":
                g = g.reshape(groups, N_SHARD, cs, -1).transpose(1, 0, 2, 3).reshape(N_SHARD, groups * cs, -1)
            elif g.ndim == 2:
                g = g.reshape(N_SHARD, -1, g.shape[-1])
            names.append({"w_in_t": "w_in", "w_up3": "w_up"}.get(n, n))
            blocks.append(g)
        zones = [((N_SHARD,) + _half_shape(b.shape[1], b.shape[2]), F32) for b in blocks]
        swap = _exchange_start(f"grad_swap_start_{group}", _swap_copies, blocks, zones, N_SHARD)
        return {"group": group, "names": names, "swap": swap, "token": swap[3]}

    def start_send(state, after):
        group, names = state["group"], state["names"]
        state["blocks"] = state["swap"][1]
        state["others"] = _exchange_wait(f"grad_swap_wait_{group}", _swap_copies, state["swap"], after)
        partials = [_chip_partial("chip_partial_" + n, gb, ob, core)
                    for n, gb, ob in zip(names, state["blocks"], state["others"])]
        zones = [((3,) + p.shape[1:], BF16) for p in partials]
        state["send"] = _exchange_start(f"grad_send_start_{group}", _partial_copies, partials, zones, 3)
        state["token"] = state["send"][3]

    grad, delta, new_m, new_v = {}, {}, {}, {}

    def start_join(state, after):
        group, names = state["group"], state["names"]
        received = _exchange_wait(f"grad_send_wait_{group}", _partial_copies, state["send"], after)
        halves = [_reduce_own("reduce_own_" + n, gb, ob, rb, where)
                  for n, gb, ob, rb in zip(names, state["blocks"], state["others"], received)]
        state["join"] = _join_start(f"grad_join_start_{group}", halves)
        return state["join"][2]

    def finish_reduce(state, after):
        group, names = state["group"], state["names"]
        for n, g in zip(names, _join_wait(f"grad_join_wait_{group}", state["join"], after)):
            shp = given[n].shape
            d2, m2, v2, g2 = _adamw("adamw_" + n, _as2d(n, given[n]), g, _as2d(n, given["m_" + n]),
                                    _as2d(n, given["v_" + n]))
            grad[n], delta[n], new_m[n], new_v[n] = (_from2d(n, a, shp) for a in (g2, d2, m2, v2))
        return d2

    step = _local_step(x[0], mem[0], loss_target[0], wts, x_bf)
    pending = {}
    request = next(step)
    while True:
        try:
            kind, group, payload = request
            if kind == "weights":
                request = step.send(fetch(group, payload))
            elif kind == "relay":
                request = step.send(send_on(payload))
            elif kind == "pass":
                request = step.send(pass_on(group, payload))
            elif kind == "grads":
                pending[group] = start_swap(group, payload)
                request = step.send(pending[group]["token"])
            else:
                start_send(pending[group], [payload])
                request = step.send(pending[group]["token"])
        except StopIteration as stop:
            loss_row, grad_x, g = stop.value
            break

    after = [pending[2]["token"], grad_x]
    for group in (0, 1):
        after = [start_join(pending[group], after)]
    for group in (0, 1):
        after = [finish_reduce(pending[group], after)]
    after = [finish_reduce(pending[2], [start_join(pending[2], after)])]

    small_names = ("a_log", "dt_bias", "gdn_norm_w", "pool_scale", "ln1_g", "ln1_b", "ln2_g", "ln2_b", "ln3_g", "ln3_b")
    pieces = [g["conv_w"]] + [g[n] for n in small_names] + [loss_row[:, :1]]
    shapes = [p.shape for p in pieces]
    summed = _unpack(_all_reduce_small("all_reduce_small", _pack(pieces), after[0]), shapes)
    gsmall = dict(zip(small_names, summed[1:-1]))
    gsmall["conv_w"] = lax.dynamic_slice(summed[0], (0, me * conv_cols), (kk, conv_cols))
    loss = summed[-1][0, 0]

    sshapes = [given[n].shape for n in SMALL]
    slabs = [_pack([given[p + n] for n in SMALL]) for p in ("", "m_", "v_")]
    gslab = _pack([gsmall[n] for n in SMALL])
    outs = _adamw("adamw_small", slabs[0], gslab, slabs[1], slabs[2])[:3]
    for dst, slab in zip((delta, new_m, new_v), outs):
        dst.update(zip(SMALL, _unpack(slab, sshapes)))
    for n in SMALL:
        grad[n] = gsmall[n].reshape(given[n].shape)

    return (loss, grad_x[None], *[grad[n] for n in ORDER], *[delta[n] for n in ORDER],
            *[new_m[n] for n in ORDER], *[new_v[n] for n in ORDER])
```

```python
import math

import jax
import jax.numpy as jnp
from jax import lax
from jax.experimental import pallas as pl
from jax.experimental.pallas import tpu as pltpu

F32 = jnp.float32
BF16 = jnp.bfloat16
MESH = pl.DeviceIdType.MESH

HEAD_DIM = 128
CHUNK = 64
POOL_WINDOWS = (2, 4, 8, 16)
XATTN_HEADS = 4
ALPHA = 2.0 ** 0.25
LN_EPS = 1e-5
NORM_EPS = 1e-6
ADAM_LR, ADAM_B1, ADAM_B2, ADAM_EPS, ADAM_WD, ADAM_STEP = 0.001, 0.9, 0.999, 1e-08, 0.01, 10
N_SHARD = 4
VMEM_LIMIT = 56 * 1024 * 1024
K_STEPS = (2048, 1024, 512, 256, 128)


def _params(*sem):
    return pltpu.CompilerParams(dimension_semantics=sem, vmem_limit_bytes=VMEM_LIMIT)


def _bdot(a, b, ta=False, tb=False):
    dims = (((0 if ta else 1,), (1 if tb else 0,)), ((), ()))
    return lax.dot_general(a.astype(BF16), b.astype(BF16), dims, preferred_element_type=F32)


def _sigmoid(x):
    return 1.0 / (1.0 + jnp.exp(-x))


def _matmul(name, a, b, *, ta=False, tb=False, tm, tn, tk, extra=(), outs, epilogue, b_blocks=None,
            sequential=False, n_used=None, n_outer=False):
    m, k_dim = (a.shape[1], a.shape[0]) if ta else a.shape
    if b_blocks and tb:
        n = b.shape[1]
        k_dim = b.shape[0] * b.shape[2]
        per = b.shape[2] // tk
        b_spec = pl.BlockSpec((None, tn, tk), lambda i, j, k: (k // per, j, k % per))
    elif b_blocks:
        n = b.shape[0] * b.shape[2]
        per = b.shape[2] // tn
        b_spec = pl.BlockSpec((None, tk, tn), lambda i, j, k: (j // per, k, j % per))
    elif tb:
        n = b.shape[0]
        b_spec = pl.BlockSpec((tn, tk), lambda i, j, k: (j, k))
    else:
        n = b.shape[1]
        b_spec = pl.BlockSpec((tk, tn), lambda i, j, k: (k, j))
    n = n_used or n
    assert m % tm == 0 and n % tn == 0 and k_dim % tk == 0, (name, m, n, k_dim, tm, tn, tk)
    nk = k_dim // tk
    a_spec = pl.BlockSpec((tk, tm), lambda i, j, k: (k, i)) if ta else pl.BlockSpec((tm, tk), lambda i, j, k: (i, k))
    n_extra, n_out = len(extra), len(outs)

    def wrap(index_map):
        return lambda i, j, k: index_map(i, j)

    def spec(block, index_map):
        if n_outer:
            return pl.BlockSpec(block, lambda j, i, k: index_map(i, j, k))
        return pl.BlockSpec(block, index_map)

    row_axis = 1 if n_outer else 0

    def body_one_step(*refs):
        ex = refs[2:2 + n_extra]
        out = refs[2 + n_extra:2 + n_extra + n_out]
        epilogue(_bdot(refs[0][...], refs[1][...], ta, tb), ex, out, pl.program_id(row_axis))

    def body(*refs):
        a_ref, b_ref = refs[0], refs[1]
        ex = refs[2:2 + n_extra]
        out = refs[2 + n_extra:2 + n_extra + n_out]
        acc = refs[-1]
        i, k = pl.program_id(row_axis), pl.program_id(2)
        part = _bdot(a_ref[...], b_ref[...], ta, tb)

        @pl.when(k == 0)
        def _():
            acc[...] = part

        @pl.when(jnp.logical_and(k > 0, k < nk - 1))
        def _():
            acc[...] += part

        @pl.when(k == nk - 1)
        def _():
            epilogue(acc[...] + part, ex, out, i)

    sem = ("arbitrary",) * 3 if sequential else ("parallel", "parallel", "arbitrary")
    res = pl.pallas_call(
        body_one_step if nk == 1 else body, name=name,
        grid=(n // tn, m // tm, nk) if n_outer else (m // tm, n // tn, nk),
        in_specs=[spec(a_spec.block_shape, a_spec.index_map), spec(b_spec.block_shape, b_spec.index_map)]
        + [spec(bs, wrap(im)) for _, bs, im in extra],
        out_specs=[spec(bs, wrap(im)) for _, bs, im in outs],
        out_shape=[s for s, _, _ in outs],
        scratch_shapes=[] if nk == 1 else [pltpu.VMEM((tm, tn), F32)],
        compiler_params=_params(*sem),
    )(a, b, *[x for x, _, _ in extra])
    return res


def _tile(i, j):
    return (i, j)


def _plain(name, a, b, *, ta=False, tb=False, tm, tn, tk, out_dtype, b_blocks=None, out3=None, n_used=None,
           n_outer=False, m_kept=None):
    m = a.shape[1] if ta else a.shape[0]
    if b_blocks:
        n = b.shape[1] if tb else b.shape[0] * b.shape[2]
    else:
        n = n_used or (b.shape[0] if tb else b.shape[1])

    def epi(acc, ex, out, i):
        out[0][...] = acc.astype(out_dtype)

    if out3:
        per = (n // out3) // tn
        spec = (jax.ShapeDtypeStruct((out3, m, n // out3), out_dtype), (None, tm, tn),
                lambda i, j: (j // per, i, j % per))
    else:
        spec = (jax.ShapeDtypeStruct((m_kept or m, n), out_dtype), (tm, tn), _tile)
    return _matmul(name, a, b, ta=ta, tb=tb, tm=tm, tn=tn, tk=tk, outs=[spec], epilogue=epi,
                   b_blocks=b_blocks, n_used=n_used, n_outer=n_outer)[0]


def _ln_forward(name, a, b, res, gamma, beta, *, tm, tk, want_h=True):
    m, n = res.shape

    def epi(acc, ex, out, i):
        u = ALPHA * ex[0][...] + acc
        mu = jnp.mean(u, axis=-1, keepdims=True)
        xc = u - mu
        var = jnp.mean(xc * xc, axis=-1, keepdims=True)
        rstd = lax.rsqrt(var + LN_EPS)
        xhat = xc * rstd
        out[-2][...] = xhat
        out[-1][...] = rstd
        if want_h:
            h = xhat * ex[1][...] + ex[2][...]
            out[0][...] = h
            out[1][...] = h.astype(BF16)

    row = lambda i, j: (i, 0)
    vec = lambda i, j: (0, 0)
    outs = [(jax.ShapeDtypeStruct((m, n), F32), (tm, n), row), (jax.ShapeDtypeStruct((m, n), BF16), (tm, n), row),
            (jax.ShapeDtypeStruct((m, n), F32), (tm, n), row), (jax.ShapeDtypeStruct((m, 1), F32), (tm, 1), row)]
    return _matmul(
        name, a, b, tm=tm, tn=n, tk=tk,
        extra=[(res, (tm, n), row), (gamma, (1, n), vec), (beta, (1, n), vec)],
        outs=outs if want_h else outs[2:], epilogue=epi)


def _ln_backward_math(dy, xhat, rstd, gamma):
    dxhat = dy * gamma
    m1 = jnp.mean(dxhat, axis=-1, keepdims=True)
    m2 = jnp.mean(dxhat * xhat, axis=-1, keepdims=True)
    du = rstd * (dxhat - m1 - xhat * m2)
    return du, jnp.sum(dy * xhat, axis=0, keepdims=True), jnp.sum(dy, axis=0, keepdims=True)


def _ln_backward(name, a, b, dres, xhat, rstd, gamma, *, tm, tk, b_blocks=None, tb=True):
    m, n = dres.shape

    def epi(acc, ex, out, i):
        dy = acc + ALPHA * ex[0][...]
        du, dg, db = _ln_backward_math(dy, ex[1][...], ex[2][...], ex[3][...])
        out[0][...] = du
        out[1][...] = du.astype(BF16)
        first = i == 0

        @pl.when(first)
        def _():
            out[2][...] = dg
            out[3][...] = db

        @pl.when(jnp.logical_not(first))
        def _():
            out[2][...] += dg
            out[3][...] += db

    row = lambda i, j: (i, 0)
    vec = lambda i, j: (0, 0)
    return _matmul(
        name, a, b, tb=tb, tm=tm, tn=n, tk=tk, b_blocks=b_blocks, sequential=True,
        extra=[(dres, (tm, n), row), (xhat, (tm, n), row), (rstd, (tm, 1), row), (gamma, (1, n), vec)],
        outs=[(jax.ShapeDtypeStruct((m, n), F32), (tm, n), row),
              (jax.ShapeDtypeStruct((m, n), BF16), (tm, n), row),
              (jax.ShapeDtypeStruct((1, n), F32), (1, n), vec),
              (jax.ShapeDtypeStruct((1, n), F32), (1, n), vec)],
        epilogue=epi)


def _shift_down(x, k):
    row = lax.broadcasted_iota(jnp.int32, x.shape, 0)
    return jnp.where(row >= k, pltpu.roll(x, k, axis=0), 0.0)


def _shift_up(x, k):
    t = x.shape[0]
    row = lax.broadcasted_iota(jnp.int32, x.shape, 0)
    return jnp.where(row < t - k, pltpu.roll(x, t - k, axis=0), 0.0)


def _conv_silu_norm(x, w, normalise):
    kk = w.shape[0]
    c = x * w[kk - 1:kk, :]
    for j in range(kk - 1):
        c = c + _shift_down(x, kk - 1 - j) * w[j:j + 1, :]
    sg = _sigmoid(c)
    s = c * sg
    r = lax.rsqrt(jnp.sum(s * s, axis=-1, keepdims=True) + NORM_EPS)
    y = jnp.where(normalise, s * r, s)
    return c, sg, s, r, y


def _gdn_pre(proj, conv_w, heads):
    t = proj.shape[0]
    kk = conv_w.shape[0]

    def body(x_ref, w_ref, o_ref):
        normalise = pl.program_id(0) < 2
        o_ref[...] = _conv_silu_norm(x_ref[...], w_ref[...], normalise)[4]

    col = lambda s, h: (0, s * heads + h)
    return pl.pallas_call(
        body, name="gdn_pre", grid=(3, heads),
        in_specs=[pl.BlockSpec((t, HEAD_DIM), col), pl.BlockSpec((kk, HEAD_DIM), col)],
        out_specs=pl.BlockSpec((t, HEAD_DIM), col),
        out_shape=jax.ShapeDtypeStruct((t, 3 * heads * HEAD_DIM), F32),
        compiler_params=_params("parallel", "parallel"),
    )(proj, conv_w)


def _gdn_pre_backward(proj, conv_w, dqkv, heads):
    t = proj.shape[0]
    kk = conv_w.shape[0]

    def body(x_ref, w_ref, dy_ref, dx_ref, dw_ref):
        normalise = pl.program_id(0) < 2
        x = x_ref[...]
        w = w_ref[...]
        dy = dy_ref[...]
        c, sg, s, r, y = _conv_silu_norm(x, w, normalise)
        ds_norm = r * (dy - y * jnp.sum(dy * y, axis=-1, keepdims=True))
        ds = jnp.where(normalise, ds_norm, dy)
        dc = ds * (sg * (1.0 + c * (1.0 - sg)))
        dx = dc * w[kk - 1:kk, :]
        rows = [None] * kk
        rows[kk - 1] = jnp.sum(dc * x, axis=0, keepdims=True)
        for j in range(kk - 1):
            lag = kk - 1 - j
            dx = dx + _shift_up(dc, lag) * w[j:j + 1, :]
            rows[j] = jnp.sum(dc * _shift_down(x, lag), axis=0, keepdims=True)
        dx_ref[...] = dx.astype(BF16)
        dw_ref[...] = jnp.concatenate(rows, axis=0)

    col = lambda s, h: (0, s * heads + h)
    return pl.pallas_call(
        body, name="gdn_pre_bwd", grid=(3, heads),
        in_specs=[pl.BlockSpec((t, HEAD_DIM), col), pl.BlockSpec((kk, HEAD_DIM), col),
                  pl.BlockSpec((t, HEAD_DIM), col)],
        out_specs=[pl.BlockSpec((t, HEAD_DIM), col), pl.BlockSpec((kk, HEAD_DIM), col)],
        out_shape=[jax.ShapeDtypeStruct((t, 3 * heads * HEAD_DIM), BF16),
                   jax.ShapeDtypeStruct((kk, 3 * heads * HEAD_DIM), F32)],
        compiler_params=_params("parallel", "parallel"),
    )(proj, conv_w, dqkv)


def _gate_vectors(a_log, dt_bias, heads):
    pad = lambda v: jnp.pad(v.astype(F32), ((0, 0), (heads, HEAD_DIM - 2 * heads)))
    return pad(jnp.exp(a_log.astype(F32))), pad(dt_bias)


def _softplus(x):
    return jnp.maximum(x, 0.0) + jnp.log(1.0 + jnp.exp(-jnp.abs(x)))


def _gates_epilogue(heads):
    def epi(acc, ex, out, i):
        lane = lax.broadcasted_iota(jnp.int32, acc.shape, 1)
        beta = _sigmoid(acc)
        g = -ex[0][...] * _softplus(acc + ex[1][...])
        out[0][...] = acc
        out[1][...] = jnp.where(lane < heads, beta, jnp.where(lane < 2 * heads, g, 0.0))
    return epi


def _gates_backward(ba, bg, dbg, ea, dtb, heads):
    t = ba.shape[0]

    def body(ba_ref, bg_ref, d_ref, ea_ref, dt_ref, dba_ref, dal_ref, ddt_ref):
        lane = lax.broadcasted_iota(jnp.int32, (t, HEAD_DIM), 1)
        bgv = bg_ref[...]
        d = d_ref[...]
        db = d * bgv * (1.0 - bgv)
        da = -d * ea_ref[...] * _sigmoid(ba_ref[...] + dt_ref[...])
        is_g = jnp.logical_and(lane >= heads, lane < 2 * heads)
        dba = jnp.where(lane < heads, db, jnp.where(is_g, da, 0.0))
        dba_ref[...] = dba.astype(BF16)
        dal_ref[...] = jnp.sum(jnp.where(is_g, d * bgv, 0.0), axis=0, keepdims=True)
        ddt_ref[...] = jnp.sum(jnp.where(is_g, da, 0.0), axis=0, keepdims=True)

    full = pl.BlockSpec((t, HEAD_DIM), lambda: (0, 0))
    vec = pl.BlockSpec((1, HEAD_DIM), lambda: (0, 0))
    return pl.pallas_call(
        body, name="gates_bwd", grid=(),
        in_specs=[full, full, full, vec, vec], out_specs=[full, vec, vec],
        out_shape=[jax.ShapeDtypeStruct((t, HEAD_DIM), BF16), jax.ShapeDtypeStruct((1, HEAD_DIM), F32),
                   jax.ShapeDtypeStruct((1, HEAD_DIM), F32)],
        compiler_params=pltpu.CompilerParams(vmem_limit_bytes=VMEM_LIMIT),
    )(ba, bg, dbg, ea, dtb)


class _Chunk:
    pass


def _split2(x):
    hi = x.astype(BF16)
    return hi, (x - hi.astype(F32)).astype(BF16)


def _split3(x):
    hi = x.astype(BF16)
    rest = x - hi.astype(F32)
    mid = rest.astype(BF16)
    return hi, mid, (rest - mid.astype(F32)).astype(BF16)


def _dot_mask(mask, x, ta=False):
    hi, mid, lo = _split3(x)
    return _bdot(mask, hi, ta=ta) + (_bdot(mask, mid, ta=ta) + _bdot(mask, lo, ta=ta))


def _transpose_by_identity(x):
    r = x.shape[0]
    eye = (lax.broadcasted_iota(jnp.int32, (r, r), 0) == lax.broadcasted_iota(jnp.int32, (r, r), 1)).astype(BF16)
    hi, mid, lo = _split3(x)
    return _bdot(hi, eye, ta=True) + (_bdot(mid, eye, ta=True) + _bdot(lo, eye, ta=True))


def _dot22(a, b, ta=False, tb=False):
    ah, al = _split2(a)
    bh, bl = _split2(b)
    return _bdot(ah, bh, ta, tb) + (_bdot(ah, bl, ta, tb) + _bdot(al, bh, ta, tb))


def _chunk_gates(bg, heads):
    n = CHUNK
    row = lax.broadcasted_iota(jnp.int32, (n, n), 0)
    col = lax.broadcasted_iota(jnp.int32, (n, n), 1)
    lane = lax.broadcasted_iota(jnp.int32, bg.shape, 1)
    graw = jnp.where(jnp.logical_and(lane >= heads, lane < 2 * heads), bg, 0.0)
    gc = _dot_mask((row >= col).astype(BF16), graw)
    return gc, _transpose_by_identity(gc)


def _in_lockstep(generators):
    results = [None] * len(generators)
    live = list(enumerate(generators))
    while live:
        still = []
        for i, gen in live:
            try:
                next(gen)
                still.append((i, gen))
            except StopIteration as stop:
                results[i] = stop.value
        live = still
    return results


def _chunk_local(q, k, v, beta, gc, grow, solved=None):
    c = _Chunk()
    n = CHUNK
    row = lax.broadcasted_iota(jnp.int32, (n, n), 0)
    col = lax.broadcasted_iota(jnp.int32, (n, n), 1)
    c.tri = row >= col
    c.strict = row > col
    eye = row == col
    c.gcb = jnp.broadcast_to(gc, (n, HEAD_DIM))
    c.decay = jnp.where(c.tri, jnp.exp(jnp.where(c.tri, gc - grow, 0.0)), 0.0)
    c.eg = jnp.exp(c.gcb)
    glast = c.gcb[n - 1:n, :]
    c.egl = jnp.exp(glast)
    c.ekl = jnp.exp(glast - c.gcb)
    c.beta = beta
    c.q = q * (HEAD_DIM ** -0.5)
    c.k = k
    c.v = v
    c.kb = k * beta
    c.vb = v * beta
    c.kg = c.kb * c.eg
    both = _bdot(jnp.concatenate([c.kb, c.q], axis=0), k, tb=True)
    yield
    c.L = jnp.where(c.strict, both[:n] * c.decay, 0.0)
    c.A = jnp.where(c.tri, both[n:] * c.decay, 0.0)
    if solved is None:
        x = -c.L
        tinv = eye.astype(F32) + x
        p = _dot22(x, x)
        yield
        for _ in range(int(math.log2(n)) - 2):
            both = _dot22(jnp.concatenate([p, tinv], axis=0), p)
            yield
            p, tinv = both[:n], tinv + both[n:]
        c.T = tinv + _dot22(tinv, p)
        yield
        uw = _dot22(c.T, jnp.concatenate([c.vb, c.kg], axis=1))
        yield
        c.u, c.w = uw[:, :HEAD_DIM], uw[:, HEAD_DIM:]
    else:
        c.T, c.u, c.w = solved
    c.qg = c.q * c.eg
    c.kdec = k * c.ekl
    return c


def _gdn_core(qkv, bg, heads):
    t = qkv.shape[0]
    nchunk = t // CHUNK

    gw = heads * HEAD_DIM

    def body(qkv_ref, bg_ref, o_ref, s_ref, t_ref, u_ref, w_ref, state):
        @pl.when(pl.program_id(0) == 0)
        def _():
            state[...] = jnp.zeros_like(state)

        bg_v = bg_ref[...]
        gc_all, gc_rows = _chunk_gates(bg_v, heads)
        def one_head(h):
            col = lambda s: pl.ds(s * gw + h * HEAD_DIM, HEAD_DIM)
            c = yield from _chunk_local(qkv_ref[:, col(0)], qkv_ref[:, col(1)], qkv_ref[:, col(2)], bg_v[:, h:h + 1],
                                        gc_all[:, heads + h:heads + h + 1], gc_rows[heads + h:heads + h + 1, :])
            s0 = state[h]
            v_new = c.u - _bdot(c.w, s0)
            yield
            o = _bdot(c.qg, s0) + _bdot(c.A, v_new)
            return s0, o, s0 * c.egl + _bdot(c.kdec, v_new, ta=True), c

        results = _in_lockstep([one_head(h) for h in range(heads)])
        for h, (s0, o, s1, c) in enumerate(results):
            lanes = pl.ds(h * HEAD_DIM, HEAD_DIM)
            s_ref[h, 0] = s0
            o_ref[:, lanes] = o
            t_ref[:, lanes] = jnp.concatenate([c.T, jnp.zeros((CHUNK, HEAD_DIM - CHUNK), F32)], axis=1)
            u_ref[:, lanes] = c.u
            w_ref[:, lanes] = c.w
            state[h] = s1

    return pl.pallas_call(
        body, name="gdn_core", grid=(nchunk,),
        in_specs=[pl.BlockSpec((CHUNK, 3 * gw), lambda n: (n, 0)), pl.BlockSpec((CHUNK, HEAD_DIM), lambda n: (n, 0))],
        out_specs=[pl.BlockSpec((CHUNK, gw), lambda n: (n, 0)),
                   pl.BlockSpec((heads, 1, HEAD_DIM, HEAD_DIM), lambda n: (0, n, 0, 0))]
        + [pl.BlockSpec((CHUNK, gw), lambda n: (n, 0))] * 3,
        out_shape=[jax.ShapeDtypeStruct((t, gw), F32),
                   jax.ShapeDtypeStruct((heads, nchunk, HEAD_DIM, HEAD_DIM), F32)]
        + [jax.ShapeDtypeStruct((t, gw), F32)] * 3,
        scratch_shapes=[pltpu.VMEM((heads, HEAD_DIM, HEAD_DIM), F32)],
        compiler_params=_params("arbitrary"),
    )(qkv, bg)


def _gdn_core_backward(qkv, bg, states, solved, do, heads):
    t = qkv.shape[0]
    nchunk = t // CHUNK
    n = CHUNK

    def one_head(chunk_local, s0, d_out, ds1):
        c = yield from chunk_local
        v_new = c.u - _bdot(c.w, s0)
        dqg = _bdot(d_out, s0, tb=True)
        ds0 = _bdot(c.qg, d_out, ta=True) + ds1 * c.egl
        dv_new = _bdot(c.A, d_out, ta=True) + _bdot(c.kdec, ds1)
        yield
        dA = jnp.where(c.tri, _bdot(d_out, v_new, tb=True), 0.0)
        dkdec = _bdot(v_new, ds1, tb=True)
        dgl = jnp.sum(jnp.sum(ds1 * s0, axis=1, keepdims=True), axis=0, keepdims=True) * c.egl
        dw = -_bdot(dv_new, s0, tb=True)
        ds0 = ds0 - _bdot(c.w, dv_new, ta=True)
        yield
        both = _dot22(c.T, jnp.concatenate([dv_new, dw], axis=1), ta=True)
        yield
        dvb, dkg = both[:, :HEAD_DIM], both[:, HEAD_DIM:]
        dL = jnp.where(c.strict, -(_bdot(dvb, c.u, tb=True) + _bdot(dkg, c.w, tb=True)), 0.0)
        yield
        dm1 = dL * c.decay
        dkb = _bdot(dm1, c.k) + dkg * c.eg
        dk = _bdot(dm1, c.kb, ta=True)
        dm2 = dA * c.decay
        dq = _bdot(dm2, c.k) + dqg * c.eg
        dk = dk + _bdot(dm2, c.q, ta=True) + dkdec * c.ekl + dkb * c.beta
        pm = dL * c.L + dA * c.A
        ones = jnp.ones((n, HEAD_DIM), BF16)
        pm_hi, pm_lo = _split2(pm)
        colsum = _bdot(pm_hi, ones, ta=True) + _bdot(pm_lo, ones, ta=True)
        tk_ = jnp.sum(dkdec * c.kdec, axis=1, keepdims=True)
        dgc = (jnp.sum(pm, axis=1, keepdims=True) - colsum
               + jnp.sum(dqg * c.qg, axis=1, keepdims=True)
               - tk_
               + jnp.sum(dkg * c.kg, axis=1, keepdims=True))
        dgl = dgl + jnp.sum(tk_, axis=0, keepdims=True)
        rowi = lax.broadcasted_iota(jnp.int32, (n, HEAD_DIM), 0)
        dgc = dgc + jnp.where(rowi == n - 1, dgl, 0.0)
        dbeta = jnp.sum(dkb * c.k, axis=1, keepdims=True) + jnp.sum(dvb * c.v, axis=1, keepdims=True)
        return dq * (HEAD_DIM ** -0.5), dk, dvb * c.beta, dbeta, dgc, ds0

    gw = heads * HEAD_DIM

    def body(qkv_ref, bg_ref, s_ref, t_ref, u_ref, w_ref, do_ref, dqkv_ref, dbg_ref, dstate):
        @pl.when(pl.program_id(0) == 0)
        def _():
            dstate[...] = jnp.zeros_like(dstate)

        bg_v = bg_ref[...]
        gc_all, gc_rows = _chunk_gates(bg_v, heads)
        lane = lax.broadcasted_iota(jnp.int32, (n, HEAD_DIM), 1)
        dgates = jnp.zeros((n, HEAD_DIM), F32)
        chains = []
        for h in range(heads):
            col = lambda s: pl.ds(s * gw + h * HEAD_DIM, HEAD_DIM)
            lanes = pl.ds(h * HEAD_DIM, HEAD_DIM)
            c = _chunk_local(qkv_ref[:, col(0)], qkv_ref[:, col(1)], qkv_ref[:, col(2)], bg_v[:, h:h + 1],
                             gc_all[:, heads + h:heads + h + 1], gc_rows[heads + h:heads + h + 1, :],
                             (t_ref[:, pl.ds(h * HEAD_DIM, CHUNK)], u_ref[:, lanes], w_ref[:, lanes]))
            chains.append(one_head(c, s_ref[h, 0], do_ref[:, pl.ds(h * HEAD_DIM, HEAD_DIM)], dstate[h]))
        results = _in_lockstep(chains)
        for h, (dq, dk, dv, dbeta, dgc, ds0) in enumerate(results):
            dgates = jnp.where(lane == h, dbeta, jnp.where(lane == heads + h, dgc, dgates))
        for h, (dq, dk, dv, dbeta, dgc, ds0) in enumerate(results):
            dqkv_ref[:, pl.ds(h * HEAD_DIM, HEAD_DIM)] = dq
            dqkv_ref[:, pl.ds(gw + h * HEAD_DIM, HEAD_DIM)] = dk
            dqkv_ref[:, pl.ds(2 * gw + h * HEAD_DIM, HEAD_DIM)] = dv
            dstate[h] = ds0
        row = lax.broadcasted_iota(jnp.int32, (n, n), 0)
        colm = lax.broadcasted_iota(jnp.int32, (n, n), 1)
        draw = _dot_mask((row >= colm).astype(BF16), dgates, ta=True)
        dbg_ref[...] = jnp.where(lane < heads, dgates, draw)

    last = nchunk - 1
    return pl.pallas_call(
        body, name="gdn_core_bwd", grid=(nchunk,),
        in_specs=[pl.BlockSpec((CHUNK, 3 * gw), lambda i: (last - i, 0)),
                  pl.BlockSpec((CHUNK, HEAD_DIM), lambda i: (last - i, 0)),
                  pl.BlockSpec((heads, 1, HEAD_DIM, HEAD_DIM), lambda i: (0, last - i, 0, 0))]
        + [pl.BlockSpec((CHUNK, gw), lambda i: (last - i, 0))] * 4,
        out_specs=[pl.BlockSpec((CHUNK, 3 * gw), lambda i: (last - i, 0)),
                   pl.BlockSpec((CHUNK, HEAD_DIM), lambda i: (last - i, 0))],
        out_shape=[jax.ShapeDtypeStruct((t, 3 * gw), F32), jax.ShapeDtypeStruct((t, HEAD_DIM), F32)],
        scratch_shapes=[pltpu.VMEM((heads, HEAD_DIM, HEAD_DIM), F32)],
        compiler_params=_params("arbitrary"),
    )(qkv, bg, states, *solved, do)


def _gdn_post(o, proj, z_col0, norm_w, heads, tt):
    t = o.shape[0]
    zb = z_col0 // HEAD_DIM

    def body(o_ref, z_ref, w_ref, out_ref):
        ov = o_ref[...]
        z = z_ref[...]
        rms = lax.rsqrt(jnp.mean(ov * ov, axis=-1, keepdims=True) + NORM_EPS)
        out_ref[...] = (ov * rms * w_ref[...] * (z * _sigmoid(z))).astype(BF16)

    return pl.pallas_call(
        body, name="gdn_post", grid=(t // tt, heads),
        in_specs=[pl.BlockSpec((tt, HEAD_DIM), lambda i, h: (i, h)),
                  pl.BlockSpec((tt, HEAD_DIM), lambda i, h: (i, zb + h)),
                  pl.BlockSpec((1, HEAD_DIM), lambda i, h: (0, 0))],
        out_specs=pl.BlockSpec((tt, HEAD_DIM), lambda i, h: (i, h)),
        out_shape=jax.ShapeDtypeStruct((t, heads * HEAD_DIM), BF16),
        compiler_params=_params("parallel", "parallel"),
    )(o, proj, norm_w)


def _gdn_post_backward(dcat, o, proj, z_col0, norm_w, heads, tt):
    t = o.shape[0]
    zb = z_col0 // HEAD_DIM

    def body(d_ref, o_ref, z_ref, w_ref, do_ref, dz_ref, dw_ref):
        d = d_ref[...]
        ov = o_ref[...]
        z = z_ref[...]
        w = w_ref[...]
        rms = lax.rsqrt(jnp.mean(ov * ov, axis=-1, keepdims=True) + NORM_EPS)
        ohat = ov * rms
        sg = _sigmoid(z)
        gate = z * sg
        dz_ref[...] = (d * ohat * w * (sg * (1.0 + z * (1.0 - sg)))).astype(BF16)
        don = d * gate
        dohat = don * w
        do_ref[...] = rms * (dohat - ohat * jnp.mean(dohat * ohat, axis=-1, keepdims=True))
        dw = jnp.sum(don * ohat, axis=0, keepdims=True)
        first = jnp.logical_and(pl.program_id(0) == 0, pl.program_id(1) == 0)

        @pl.when(first)
        def _():
            dw_ref[...] = dw

        @pl.when(jnp.logical_not(first))
        def _():
            dw_ref[...] += dw

    blk = pl.BlockSpec((tt, HEAD_DIM), lambda i, h: (i, h))
    return pl.pallas_call(
        body, name="gdn_post_bwd", grid=(t // tt, heads),
        in_specs=[blk, blk, pl.BlockSpec((tt, HEAD_DIM), lambda i, h: (i, zb + h)),
                  pl.BlockSpec((1, HEAD_DIM), lambda i, h: (0, 0))],
        out_specs=[blk, blk, pl.BlockSpec((1, HEAD_DIM), lambda i, h: (0, 0))],
        out_shape=[jax.ShapeDtypeStruct((t, heads * HEAD_DIM), F32),
                   jax.ShapeDtypeStruct((t, heads * HEAD_DIM), BF16),
                   jax.ShapeDtypeStruct((1, HEAD_DIM), F32)],
        compiler_params=_params("arbitrary", "arbitrary"),
    )(dcat, o, proj, norm_w)


def _pool_select(levels, group):
    out = levels[-1]
    for gi in range(len(levels) - 2, -1, -1):
        out = jnp.where(group == gi, levels[gi], out)
    return out


def _pool_counts(t, width, group):
    pos = lax.broadcasted_iota(jnp.int32, (t, width), 0)
    win = jnp.left_shift(2, group)
    return jnp.minimum(pos + 1, win).astype(F32)


def _pooled(p, group):
    levels, s, step = [], p, 1
    for _ in POOL_WINDOWS:
        s = s + _shift_down(s, step)
        levels.append(s)
        step *= 2
    cnt = _pool_counts(p.shape[0], p.shape[1], group)
    return _pool_select(levels, group) / cnt - p, cnt


def _pool_forward(proj, p_col0, pool_w, pool_scale):
    t = proj.shape[0]
    groups, cg, _ = pool_w.shape
    pb = p_col0 // cg

    def body(p_ref, w_ref, s_ref, o_ref):
        pooled, _ = _pooled(p_ref[...], pl.program_id(0))
        o_ref[...] = (_bdot(pooled, w_ref[0]) * s_ref[...]).astype(BF16)

    return pl.pallas_call(
        body, name="pool_fwd", grid=(groups,),
        in_specs=[pl.BlockSpec((t, cg), lambda g: (0, pb + g)), pl.BlockSpec((1, cg, cg), lambda g: (g, 0, 0)),
                  pl.BlockSpec((1, cg), lambda g: (0, g))],
        out_specs=pl.BlockSpec((t, cg), lambda g: (0, g)),
        out_shape=jax.ShapeDtypeStruct((t, groups * cg), BF16),
        compiler_params=_params("parallel"),
    )(proj, pool_w, pool_scale)


def _pool_backward(dcat, d_col0, proj, p_col0, pool_w, pool_scale):
    t = proj.shape[0]
    groups, cg, _ = pool_w.shape
    pb = p_col0 // cg
    db = d_col0 // cg

    def body(d_ref, p_ref, w_ref, s_ref, dp_ref, dw_ref, ds_ref):
        group = pl.program_id(0)
        pooled, cnt = _pooled(p_ref[...], group)
        w = w_ref[0]
        d = d_ref[...]
        mixed = _bdot(pooled, w)
        ds_ref[...] = jnp.sum(d * mixed, axis=0, keepdims=True)
        dmixed = d * s_ref[...]
        dw_ref[0] = _bdot(pooled, dmixed, ta=True)
        dpooled = _bdot(dmixed, w, tb=True)
        levels, s, step = [], dpooled / cnt, 1
        for _ in POOL_WINDOWS:
            s = s + _shift_up(s, step)
            levels.append(s)
            step *= 2
        dp_ref[...] = (_pool_select(levels, group) - dpooled).astype(BF16)

    return pl.pallas_call(
        body, name="pool_bwd", grid=(groups,),
        in_specs=[pl.BlockSpec((t, cg), lambda g: (0, db + g)), pl.BlockSpec((t, cg), lambda g: (0, pb + g)),
                  pl.BlockSpec((1, cg, cg), lambda g: (g, 0, 0)), pl.BlockSpec((1, cg), lambda g: (0, g))],
        out_specs=[pl.BlockSpec((t, cg), lambda g: (0, g)), pl.BlockSpec((1, cg, cg), lambda g: (g, 0, 0)),
                   pl.BlockSpec((1, cg), lambda g: (0, g))],
        out_shape=[jax.ShapeDtypeStruct((t, groups * cg), BF16), jax.ShapeDtypeStruct((groups, cg, cg), F32),
                   jax.ShapeDtypeStruct((1, groups * cg), F32)],
        compiler_params=_params("parallel"),
    )(dcat, proj, pool_w, pool_scale)


def _attention(q, k, v, tq):
    t, d = q.shape
    m = k.shape[0]
    dh = d // XATTN_HEADS
    scale = dh ** -0.5

    def body(q_ref, k_ref, v_ref, o_ref):
        s = _bdot(q_ref[...], k_ref[...], tb=True) * scale
        s = s - jnp.max(s, axis=-1, keepdims=True)
        e = jnp.exp(s)
        p = e / jnp.sum(e, axis=-1, keepdims=True)
        o_ref[...] = _bdot(p, v_ref[...]).astype(BF16)

    return pl.pallas_call(
        body, name="xattn_fwd", grid=(XATTN_HEADS, t // tq),
        in_specs=[pl.BlockSpec((tq, dh), lambda h, i: (i, h)), pl.BlockSpec((m, dh), lambda h, i: (0, h)),
                  pl.BlockSpec((m, dh), lambda h, i: (0, h))],
        out_specs=pl.BlockSpec((tq, dh), lambda h, i: (i, h)),
        out_shape=jax.ShapeDtypeStruct((t, d), BF16),
        compiler_params=_params("parallel", "parallel"),
    )(q, k, v)


def _attention_backward(q, k, v, do, tq):
    t, d = q.shape
    m = k.shape[0]
    dh = d // XATTN_HEADS
    scale = dh ** -0.5

    def body(q_ref, k_ref, v_ref, do_ref, dq_ref, dk_ref, dv_ref, dk_acc, dv_acc):
        i = pl.program_id(1)
        qv, kv, vv, dov = q_ref[...], k_ref[...], v_ref[...], do_ref[...]
        s = _bdot(qv, kv, tb=True) * scale
        s = s - jnp.max(s, axis=-1, keepdims=True)
        e = jnp.exp(s)
        p = e / jnp.sum(e, axis=-1, keepdims=True)
        dp = _bdot(dov, vv, tb=True)
        ds = p * (dp - jnp.sum(dp * p, axis=-1, keepdims=True)) * scale
        dq_ref[...] = _bdot(ds, kv).astype(BF16)
        dv_part = _bdot(p, dov, ta=True)
        dk_part = _bdot(ds, qv, ta=True)

        @pl.when(i == 0)
        def _():
            dk_acc[...] = dk_part
            dv_acc[...] = dv_part

        @pl.when(i > 0)
        def _():
            dk_acc[...] += dk_part
            dv_acc[...] += dv_part

        @pl.when(i == pl.num_programs(1) - 1)
        def _():
            dk_ref[...] = dk_acc[...].astype(BF16)
            dv_ref[...] = dv_acc[...].astype(BF16)

    qblk = pl.BlockSpec((tq, dh), lambda h, i: (i, h))
    kblk = pl.BlockSpec((m, dh), lambda h, i: (0, h))
    return pl.pallas_call(
        body, name="xattn_bwd", grid=(XATTN_HEADS, t // tq),
        in_specs=[qblk, kblk, kblk, qblk],
        out_specs=[qblk, kblk, kblk],
        out_shape=[jax.ShapeDtypeStruct((t, d), BF16), jax.ShapeDtypeStruct((m, d), BF16),
                   jax.ShapeDtypeStruct((m, d), BF16)],
        scratch_shapes=[pltpu.VMEM((m, dh), F32), pltpu.VMEM((m, dh), F32)],
        compiler_params=_params("parallel", "arbitrary"),
    )(q, k, v, do)


def _ln_backward_rows(name, dmain, dres, xhat, rstd, gamma, tm):
    t, d = xhat.shape

    def body(m_ref, r_ref, x_ref, s_ref, g_ref, du_ref, dub_ref, dg_ref, db_ref):
        du, dg, db = _ln_backward_math(m_ref[...] + ALPHA * r_ref[...], x_ref[...], s_ref[...], g_ref[...])
        du_ref[...] = du
        dub_ref[...] = du.astype(BF16)
        first = pl.program_id(0) == 0

        @pl.when(first)
        def _():
            dg_ref[...] = dg
            db_ref[...] = db

        @pl.when(jnp.logical_not(first))
        def _():
            dg_ref[...] += dg
            db_ref[...] += db

    row = pl.BlockSpec((tm, d), lambda i: (i, 0))
    vec = pl.BlockSpec((1, d), lambda i: (0, 0))
    return pl.pallas_call(
        body, name=name, grid=(t // tm,),
        in_specs=[row, row, row, pl.BlockSpec((tm, 1), lambda i: (i, 0)), vec],
        out_specs=[row, row, vec, vec],
        out_shape=[jax.ShapeDtypeStruct((t, d), F32), jax.ShapeDtypeStruct((t, d), BF16),
                   jax.ShapeDtypeStruct((1, d), F32), jax.ShapeDtypeStruct((1, d), F32)],
        compiler_params=_params("arbitrary"),
    )(dmain, dres, xhat, rstd, gamma)


def _loss_and_ln_backward(xhat, rstd, gamma, beta, target, tm):
    t, d = xhat.shape

    def body(x_ref, r_ref, g_ref, b_ref, t_ref, du_ref, dub_ref, dg_ref, db_ref, loss_ref):
        xh = x_ref[...]
        g = g_ref[...]
        diff = xh * g + b_ref[...] - t_ref[...]
        part = jnp.sum(jnp.sum(diff * diff, axis=1, keepdims=True), axis=0, keepdims=True) * (0.5 / d)
        dy = diff * (1.0 / d)
        du, dg, db = _ln_backward_math(dy, xh, r_ref[...], g)
        du_ref[...] = du
        dub_ref[...] = du.astype(BF16)
        lossrow = jnp.broadcast_to(part, (1, HEAD_DIM))
        first = pl.program_id(0) == 0

        @pl.when(first)
        def _():
            dg_ref[...] = dg
            db_ref[...] = db
            loss_ref[...] = lossrow

        @pl.when(jnp.logical_not(first))
        def _():
            dg_ref[...] += dg
            db_ref[...] += db
            loss_ref[...] += lossrow

    row = pl.BlockSpec((tm, d), lambda i: (i, 0))
    vec = pl.BlockSpec((1, d), lambda i: (0, 0))
    return pl.pallas_call(
        body, name="loss_ln3_bwd", grid=(t // tm,),
        in_specs=[row, pl.BlockSpec((tm, 1), lambda i: (i, 0)), vec, vec, row],
        out_specs=[row, row, vec, vec, pl.BlockSpec((1, HEAD_DIM), lambda i: (0, 0))],
        out_shape=[jax.ShapeDtypeStruct((t, d), F32), jax.ShapeDtypeStruct((t, d), BF16),
                   jax.ShapeDtypeStruct((1, d), F32), jax.ShapeDtypeStruct((1, d), F32),
                   jax.ShapeDtypeStruct((1, HEAD_DIM), F32)],
        compiler_params=_params("arbitrary"),
    )(xhat, rstd, gamma, beta, target)


def _after(token, a):
    return a if token is None else a + token[:1, :1].astype(a.dtype)


def _pick(n, prefs):
    for p in prefs:
        if n % p == 0:
            return p
    return n


def _local_step(x, mem, target, w, x_bf=None):
    t, d = x.shape
    heads = w["a_log"].shape[1]
    gw = heads * HEAD_DIM
    groups, cg, _ = w["pool_w"].shape
    pw = groups * cg
    n_main = 4 * gw + pw
    in_cols = n_main + 2 * heads
    s_in = w["w_in_t"].shape[0]

    tm = _pick(t, (512, 256, 128))
    tm_ln = _pick(t, (256, 128))
    tm_big = _pick(t, (1024, 512, 256, 128))
    tk = _pick(d, K_STEPS)

    w_in_t = w["w_in_t"].reshape(in_cols, d)
    w_p_t = w_in_t[4 * gw + 2 * heads:]
    w_ba_t = jnp.pad(w_in_t[4 * gw:4 * gw + 2 * heads], ((0, HEAD_DIM - 2 * heads), (0, 0)))
    x_bf = x.astype(BF16) if x_bf is None else x_bf
    mem_bf = mem.astype(BF16)

    tn_d = _pick(d, (1024, 512, 256, 128))
    proj = _plain("proj_main", x_bf, w_in_t, tb=True, n_used=4 * gw, tm=tm_big, tn=_pick(4 * gw, (1024, 512, 256, 128)),
                  tk=tk, out_dtype=F32)
    pproj = _plain("proj_pool", x_bf, w_p_t, tb=True, tm=tm_big, tn=_pick(pw, (1024, 512, 256, 128)), tk=tk, out_dtype=F32)
    ea, dtb = _gate_vectors(w["a_log"], w["dt_bias"], heads)
    vec128 = lambda i, j: (0, 0)
    ba, bg = _matmul(
        "proj_gates", x_bf, w_ba_t, tb=True, tm=tm, tn=HEAD_DIM, tk=tk,
        extra=[(ea, (1, HEAD_DIM), vec128), (dtb, (1, HEAD_DIM), vec128)],
        outs=[(jax.ShapeDtypeStruct((t, HEAD_DIM), F32), (tm, HEAD_DIM), _tile)] * 2,
        epilogue=_gates_epilogue(heads))
    qkv = _gdn_pre(proj, w["conv_w"], heads)
    o_gdn, states, *solved = _gdn_core(qkv, bg, heads)
    cat_g = _gdn_post(o_gdn, proj, 3 * gw, w["gdn_norm_w"], heads, tm_big)
    token = yield ("pass", 1, cat_g)
    cat_p = _pool_forward(pproj, 0, w["pool_w"], _after(token, w["pool_scale"]))
    cat = jnp.concatenate([cat_g, cat_p], axis=1)
    w = {**w, **(yield ("weights", 1, cat))}
    h1, h1_bf, xhat1, rstd1 = _ln_forward("mix_ln1", cat, w["w_out"], x, w["ln1_g"], w["ln1_b"], tm=tm_ln, tk=tk)

    h1_bf = _after((yield ("relay", None, h1_bf)), h1_bf)
    q = _plain("xattn_q", h1_bf, w["xq_w"], tm=tm_big, tn=tn_d, tk=tk, out_dtype=BF16)
    mlen = mem.shape[0]
    tm_mem = _pick(mlen, (256, 128))
    k = _plain("xattn_k", mem_bf, w["xk_w"], tm=tm_mem, tn=tn_d, tk=tk, out_dtype=BF16)
    v = _plain("xattn_v", mem_bf, w["xv_w"], tm=tm_mem, tn=tn_d, tk=tk, out_dtype=BF16)
    att = _attention(q, k, v, tm_big)
    h2, h2_bf, xhat2, rstd2 = _ln_forward("xo_ln2", att, w["xo_w"], h1, w["ln2_g"], w["ln2_b"], tm=tm_ln, tk=tk)

    w = {**w, **(yield ("weights", 2, h2_bf))}
    s_up = w["w_up3"].shape[0]
    ff = s_up * w["w_up3"].shape[2]
    tn_f = _pick(ff // s_up, (1024, 512, 256, 128))

    def up_epi(acc, ex, out, i):
        r = jnp.maximum(acc, 0.0)
        out[0][...] = (r * r).astype(BF16)
        out[1][...] = (2.0 * r).astype(BF16)

    act, act_grad = _matmul(
        "mlp_up", h2_bf, w["w_up3"], b_blocks=s_up, tm=tm_big, tn=tn_f, tk=tk,
        outs=[(jax.ShapeDtypeStruct((t, ff), BF16), (tm_big, tn_f), _tile)] * 2, epilogue=up_epi)
    w = {**w, **(yield ("weights", 3, act))}
    tk_f = _pick(ff, K_STEPS)
    xhat3, rstd3 = _ln_forward("down_ln3", act, w["w_down"], h2, w["ln3_g"], w["ln3_b"], tm=tm, tk=tk_f, want_h=False)

    grads = {}
    du3, du3_bf, grads["ln3_g"], grads["ln3_b"], loss = _loss_and_ln_backward(
        xhat3, rstd3, w["ln3_g"], w["ln3_b"], target, tm)

    def dup_epi(acc, ex, out, i):
        out[0][...] = (acc * ex[0][...].astype(F32)).astype(BF16)

    dup = _matmul(
        "mlp_down_dx", du3_bf, w["w_down"], tb=True, tm=tm_big, tn=tn_f, tk=tk,
        extra=[(act_grad, (tm_big, tn_f), _tile)],
        outs=[(jax.ShapeDtypeStruct((t, ff), BF16), (tm_big, tn_f), _tile)], epilogue=dup_epi)[0]
    tk_t = _pick(t, K_STEPS)
    tm_w = _pick(d, (512, 256, 128))
    grads["w_down"] = _plain("mlp_down_dw", act, du3_bf, ta=True, tm=_pick(ff, (512, 256, 128)), tn=d, tk=tk_t,
                             out_dtype=F32)
    grads["w_up3"] = _plain("mlp_up_dw", h2_bf, dup, ta=True, tm=tm_w, tn=ff // s_up, tk=tk_t, out_dtype=F32, out3=s_up,
                            n_outer=True)
    token = yield ("grads", 0, {n: grads.pop(n) for n in ("w_down", "w_up3")})
    dh2 = _plain("mlp_up_dx", dup, w["w_up3"], tb=True, b_blocks=s_up, tm=tm_big, tn=tn_d,
                 tk=_pick(ff // s_up, K_STEPS), out_dtype=F32)
    du2, du2_bf, grads["ln2_g"], grads["ln2_b"] = _ln_backward_rows(
        "ln2_bwd", dh2, du3, xhat2, rstd2, _after(token, w["ln2_g"]), tm)
    token = yield ("poll", 0, du2_bf)

    grads["xo_w"] = _plain("xo_dw", att, du2_bf, ta=True, tm=tm_w, tn=d, tk=tk_t, out_dtype=F32)
    datt = _plain("xo_dx", du2_bf, w["xo_w"], tb=True, tm=tm_big, tn=tn_d, tk=tk, out_dtype=BF16)
    dq, dk, dv = _attention_backward(q, k, v, datt, tm_big)
    tk_m = _pick(mlen, (256, 128))
    grads["xq_w"] = _plain("xq_dw", h1_bf, dq, ta=True, tm=tm_w, tn=d, tk=tk_t, out_dtype=F32)
    grads["xk_w"] = _plain("xk_dw", mem_bf, dk, ta=True, tm=tm_w, tn=tn_d, tk=tk_m, out_dtype=F32)
    grads["xv_w"] = _plain("xv_dw", mem_bf, dv, ta=True, tm=tm_w, tn=tn_d, tk=tk_m, out_dtype=F32)
    du1, du1_bf, grads["ln1_g"], grads["ln1_b"] = _ln_backward(
        "xq_dx_ln1", dq, w["xq_w"], du2, xhat1, rstd1, _after(token, w["ln1_g"]), tm=tm_ln, tk=tk)

    grads["w_out"] = _plain("out_dw", cat, du1_bf, ta=True, tm=tm_w, tn=d, tk=tk_t, out_dtype=F32)
    token = yield ("grads", 1, {n: grads.pop(n) for n in ("xo_w", "xq_w", "xk_w", "xv_w", "w_out")})
    dcat = _plain("out_dx", du1_bf, w["w_out"], tb=True, tm=tm_big, tn=tn_d, tk=tk, out_dtype=F32)
    dp, grads["pool_w"], grads["pool_scale"] = _pool_backward(dcat, gw, pproj, 0, w["pool_w"],
                                                              _after(token, w["pool_scale"]))
    do_gdn, dz, grads["gdn_norm_w"] = _gdn_post_backward(dcat, o_gdn, proj, 3 * gw, _after(token, w["gdn_norm_w"]),
                                                         heads, tm_big)
    dqkv, dbg = _gdn_core_backward(qkv, bg, states, solved, do_gdn, heads)
    token = yield ("poll", 1, dqkv)
    dqkv_pre, grads["conv_w"] = _gdn_pre_backward(proj, _after(token, w["conv_w"]), dqkv, heads)
    dba, dalog_row, ddt_row = _gates_backward(ba, bg, dbg, ea, dtb, heads)
    grads["a_log"] = dalog_row[:, heads:2 * heads]
    grads["dt_bias"] = ddt_row[:, heads:2 * heads]

    k_pad = -(-in_cols // (2 * HEAD_DIM)) * (2 * HEAD_DIM)
    dproj = jnp.concatenate([dqkv_pre, dz, dba[:, :2 * heads], dp, jnp.zeros((t, k_pad - in_cols), BF16)], axis=1)
    dw_in_t = _plain("proj_dw", dproj, x_bf, ta=True, tm=_pick(k_pad, (512, 256, 128)), tn=d, tk=tk_t, out_dtype=F32,
                     m_kept=in_cols)
    grads["w_in_t"] = dw_in_t.reshape(s_in, in_cols // s_in, d)

    def dx_epi(acc, ex, out, i):
        out[0][...] = acc + ALPHA * ex[0][...]

    token = yield ("grads", 2, {n: grads.pop(n) for n in ("w_in_t", "pool_w")})
    w_in_t_pad = jnp.concatenate([w_in_t, _after(token, jnp.zeros((k_pad - in_cols, d), BF16))], axis=0)
    grad_x = _matmul(
        "proj_dx", dproj, w_in_t_pad, tm=tm, tn=tn_d, tk=k_pad, extra=[(du1, (tm, tn_d), _tile)],
        outs=[(jax.ShapeDtypeStruct((t, d), F32), (tm, tn_d), _tile)], epilogue=dx_epi)[0]
    yield ("poll", 2, grad_x)
    return loss, grad_x, grads


def _adamw(name, w, g, m, v):
    r, c = w.shape
    if r % 8 == 0:
        tr = _pick(r, (256, 128, 64, 32, 16, 8))
        blk, steps = pl.BlockSpec((tr, c), lambda i: (i, 0)), r // tr
    else:
        tc = _pick(c, (256, 128))
        blk, steps = pl.BlockSpec((r, tc), lambda i: (0, i)), c // tc
    c1 = 1.0 - ADAM_B1 ** ADAM_STEP
    c2 = 1.0 - ADAM_B2 ** ADAM_STEP

    def body(w_ref, g_ref, m_ref, v_ref, d_ref, mo_ref, vo_ref, go_ref):
        gv = g_ref[...]
        mn = ADAM_B1 * m_ref[...] + (1.0 - ADAM_B1) * gv
        vn = ADAM_B2 * v_ref[...] + (1.0 - ADAM_B2) * (gv * gv)
        d_ref[...] = -ADAM_LR * ((mn / c1) / (jnp.sqrt(vn / c2) + ADAM_EPS) + ADAM_WD * w_ref[...])
        mo_ref[...] = mn
        vo_ref[...] = vn
        go_ref[...] = gv

    return pl.pallas_call(
        body, name=name, grid=(steps,), in_specs=[blk] * 4, out_specs=[blk] * 4,
        out_shape=[jax.ShapeDtypeStruct((r, c), F32)] * 4,
        compiler_params=_params("parallel"),
    )(w, g, m, v)


def _place():
    x, y, c = lax.axis_index("x"), lax.axis_index("y"), lax.axis_index("c")
    chips = [(1 - x, y), (x, 1 - y), (1 - x, 1 - y)]
    return x, y, c, chips


HBM = pl.BlockSpec(memory_space=pltpu.HBM)


SEM = pl.BlockSpec(memory_space=pltpu.SEMAPHORE)
ANY = pl.BlockSpec(memory_space=pl.ANY)
EFFECT = pltpu.SideEffectType.DATAFLOW_SIDE_EFFECTING


def _in_hbm(a):
    return pltpu.with_memory_space_constraint(a, pltpu.HBM)


def _remote(src, dst, send_sem, recv_sem, to):
    return pltpu.make_async_remote_copy(src_ref=src, dst_ref=dst, send_sem=send_sem, recv_sem=recv_sem,
                                        device_id=to, device_id_type=MESH)


def _by_rows(rows):
    return rows % 32 == 0


def _half_shape(rows, cols):
    return (rows // 2, cols) if _by_rows(rows) else (rows, cols // 2)


def _half(ref, which, *lead):
    rows, cols = ref.shape[-2:]
    if _by_rows(rows):
        return ref.at[(*lead, pl.ds(which * (rows // 2), rows // 2))]
    return ref.at[(*lead, slice(None), pl.ds(which * (cols // 2), cols // 2))]


def _landed(lands, i, shard_index, which):
    return _half(lands[i], which, shard_index)


def _routes():
    x, y, c, _ = _place()
    first = (jnp.where(c == 0, 1 - x, x), jnp.where(c == 0, y, 1 - y))
    second = (jnp.where(c == 0, x, 1 - x), jnp.where(c == 0, 1 - y, y))
    return first, second, (1 - x, 1 - y)


def _shard_of(chip):
    return 2 * chip[0] + chip[1]


def _gather_start(name, shards, after, relayed=()):
    n = len(shards)
    lands = [lax.empty((N_SHARD,) + s.shape, s.dtype) for s in shards]

    def body(*refs):
        ins, zones = refs[:n], refs[n:2 * n]
        ici_send, ici_recv, own_send, own_recv = refs[2 * n + 1:2 * n + 5]
        token = refs[-1]
        x, y, c, chips = _place()
        me = 2 * x + y
        first, _, _ = _routes()
        for i in range(n):
            if i in relayed:
                _remote(_half(ins[i], c), _landed(zones, i, me, c), ici_send.at[3 * i], ici_recv.at[3 * i],
                        (*first, c)).start()
                continue
            for j, chip in enumerate(chips):
                _remote(_half(ins[i], c), _landed(zones, i, me, c), ici_send.at[3 * i + j],
                        ici_recv.at[3 * i + j], (*chip, c)).start()
        for i in range(n):
            _remote(ins[i], zones[i].at[me], own_send.at[i], own_recv.at[i], (x, y, 1 - c)).start()
        token[...] = jnp.zeros_like(token)

    dma = pltpu.SemaphoreType.DMA
    outs = pl.pallas_call(
        body, name=name,
        in_specs=[HBM] * (2 * n) + [ANY],
        out_shape=(dma((3 * n,)), dma((3 * n,)), dma((n,)), dma((n,)),
                   *[pltpu.HBM(a.shape, a.dtype) for a in shards + lands], jax.ShapeDtypeStruct((8, LANES), F32)),
        out_specs=(SEM, SEM, SEM, SEM, *[HBM] * (2 * n), pl.BlockSpec(memory_space=pltpu.VMEM)),
        input_output_aliases={k: 4 + k for k in range(2 * n)},
        compiler_params=pltpu.CompilerParams(has_side_effects=EFFECT),
    )(*[_in_hbm(a) for a in shards + lands], after)
    sems = dict(zip(("ici_send", "ici_recv", "own_send", "own_recv"), outs[:4]))
    return sems, list(outs[4:4 + n]), list(outs[4 + n:4 + 2 * n]), outs[-1]


def _gather_forward(name, idx, lands, sems, after):
    n = len(idx)

    def body(*refs):
        zones = refs[:n]
        ici_recv = refs[n]
        fwd_send, fwd_recv = refs[n + 2], refs[n + 3]
        x, y, c, chips = _place()
        for k, i in enumerate(idx):
            for j, chip in enumerate(chips):
                half = _landed(zones, k, 2 * chip[0] + chip[1], c)
                _remote(half, half, fwd_send.at[3 * k + j], ici_recv.at[3 * i + j], (*chip, c)).wait_recv()
                _remote(half, half, fwd_send.at[3 * k + j], fwd_recv.at[3 * k + j], (x, y, 1 - c)).start()
        refs[-1][...] = jnp.zeros_like(refs[-1])

    dma = pltpu.SemaphoreType.DMA
    outs = pl.pallas_call(
        body, name=name,
        in_specs=[HBM] * n + [SEM, ANY],
        out_shape=(dma((3 * n,)), dma((3 * n,)), *[pltpu.HBM(a.shape, a.dtype) for a in lands],
                   jax.ShapeDtypeStruct((8, LANES), F32)),
        out_specs=(SEM, SEM, *[HBM] * n, pl.BlockSpec(memory_space=pltpu.VMEM)),
        input_output_aliases={k: 2 + k for k in range(n)},
        compiler_params=pltpu.CompilerParams(has_side_effects=EFFECT),
    )(*lands, sems["ici_recv"], after)
    return (outs[0], outs[1]), list(outs[2:2 + n]), outs[-1]


def _gather_wait(name, idx, shards, lands, sems, fwd, after):
    n = len(idx)

    def body(*refs):
        ins, zones = refs[:n], refs[n:2 * n]
        ici_send, own_send, own_recv, fwd_send, fwd_recv = refs[2 * n:2 * n + 5]
        x, y, c, chips = _place()
        me = 2 * x + y
        for k, i in enumerate(idx):
            mine = _half(ins[k], c)
            for j, chip in enumerate(chips):
                theirs = 2 * chip[0] + chip[1]
                _remote(mine, _landed(zones, k, me, c), ici_send.at[3 * i + j], fwd_recv.at[3 * k + j],
                        (*chip, c)).wait_send()
                sent = _landed(zones, k, theirs, c)
                _remote(sent, sent, fwd_send.at[3 * k + j], fwd_recv.at[3 * k + j], (x, y, 1 - c)).wait_send()
                passed = _landed(zones, k, theirs, 1 - c)
                _remote(passed, passed, fwd_send.at[3 * k + j], fwd_recv.at[3 * k + j], (x, y, 1 - c)).wait_recv()
            own = _remote(ins[k], zones[k].at[me], own_send.at[i], own_recv.at[i], (x, y, 1 - c))
            own.wait_send()
            own.wait_recv()

    outs = pl.pallas_call(
        body, name=name,
        in_specs=[HBM] * (2 * n) + [SEM] * 5 + [ANY],
        out_shape=tuple(pltpu.HBM(a.shape, a.dtype) for a in lands),
        out_specs=tuple([HBM] * n),
        input_output_aliases={n + k: k for k in range(n)},
        compiler_params=pltpu.CompilerParams(has_side_effects=EFFECT),
    )(*shards, *lands, sems["ici_send"], sems["own_send"], sems["own_recv"], fwd[0], fwd[1], after)
    return list(outs)


def _gather_relay(name, idx, shards, lands, sems, after):
    n = len(idx)

    def body(*refs):
        ins, zones, ici_recv = refs[:n], refs[n:2 * n], refs[2 * n]
        relay_send, relay_recv, pass_send, pass_recv = refs[2 * n + 2:2 * n + 6]
        x, y, c, _ = _place()
        first, second, _ = _routes()
        for k, i in enumerate(idx):
            landed = _landed(zones, k, _shard_of(first), c)
            _remote(landed, landed, pass_send.at[k], ici_recv.at[3 * i], (*first, c)).wait_recv()
            _remote(_half(ins[k], c), _landed(zones, k, 2 * x + y, c), relay_send.at[2 * k], relay_recv.at[2 * k],
                    (*second, c)).start()
            _remote(landed, landed, relay_send.at[2 * k + 1], relay_recv.at[2 * k + 1], (*second, c)).start()
            _remote(landed, landed, pass_send.at[k], pass_recv.at[k], (x, y, 1 - c)).start()
        refs[-1][...] = jnp.zeros_like(refs[-1])

    dma = pltpu.SemaphoreType.DMA
    outs = pl.pallas_call(
        body, name=name,
        in_specs=[HBM] * (2 * n) + [SEM, ANY],
        out_shape=(dma((2 * n,)), dma((2 * n,)), dma((n,)), dma((n,)), *[pltpu.HBM(a.shape, a.dtype) for a in lands],
                   jax.ShapeDtypeStruct((8, LANES), F32)),
        out_specs=(SEM, SEM, SEM, SEM, *[HBM] * n, pl.BlockSpec(memory_space=pltpu.VMEM)),
        input_output_aliases={n + k: 4 + k for k in range(n)},
        compiler_params=pltpu.CompilerParams(has_side_effects=EFFECT),
    )(*shards, *lands, sems["ici_recv"], after)
    return outs[:4], list(outs[4:4 + n]), outs[-1]


def _gather_forward_relayed(name, ks, lands, relay, after):
    n = len(ks)

    def body(*refs):
        zones, relay_recv = refs[:n], refs[n]
        fwd_send, fwd_recv = refs[n + 2], refs[n + 3]
        x, y, c, _ = _place()
        _, second, diagonal = _routes()
        for p, k in enumerate(ks):
            for j, chip in enumerate((second, diagonal)):
                landed = _landed(zones, p, _shard_of(chip), c)
                _remote(landed, landed, fwd_send.at[2 * p + j], relay_recv.at[2 * k + j], (*second, c)).wait_recv()
                _remote(landed, landed, fwd_send.at[2 * p + j], fwd_recv.at[2 * p + j], (x, y, 1 - c)).start()

    dma = pltpu.SemaphoreType.DMA
    outs = pl.pallas_call(
        body, name=name,
        in_specs=[HBM] * n + [SEM, ANY],
        out_shape=(dma((2 * n,)), dma((2 * n,)), *[pltpu.HBM(a.shape, a.dtype) for a in lands]),
        out_specs=(SEM, SEM, *[HBM] * n),
        input_output_aliases={k: 2 + k for k in range(n)},
        compiler_params=pltpu.CompilerParams(has_side_effects=EFFECT),
    )(*lands, relay[1], after)
    return (outs[0], outs[1]), list(outs[2:])


def _gather_wait_relayed(name, idx, ks, shards, lands, sems, relay, fwd, after):
    n = len(idx)

    def body(*refs):
        ins, zones = refs[:n], refs[n:2 * n]
        ici_send, own_send, own_recv, relay_send, pass_send, pass_recv, fwd_send, fwd_recv = refs[2 * n:2 * n + 8]
        x, y, c, _ = _place()
        me = 2 * x + y
        sibling = (x, y, 1 - c)
        first, second, diagonal = _routes()
        for p, (i, k) in enumerate(zip(idx, ks)):
            mine, at_peer = _half(ins[p], c), _landed(zones, p, me, c)
            from_first = _landed(zones, p, _shard_of(first), c)
            _remote(mine, at_peer, ici_send.at[3 * i], pass_recv.at[k], (*first, c)).wait_send()
            _remote(mine, at_peer, relay_send.at[2 * k], pass_recv.at[k], (*second, c)).wait_send()
            _remote(from_first, from_first, relay_send.at[2 * k + 1], pass_recv.at[k], (*second, c)).wait_send()
            _remote(from_first, from_first, pass_send.at[k], pass_recv.at[k], sibling).wait_send()
            theirs = _landed(zones, p, _shard_of(second), 1 - c)
            _remote(theirs, theirs, pass_send.at[k], pass_recv.at[k], sibling).wait_recv()
            for j, (sent, got) in enumerate(((second, first), (diagonal, diagonal))):
                out_half = _landed(zones, p, _shard_of(sent), c)
                _remote(out_half, out_half, fwd_send.at[2 * p + j], fwd_recv.at[2 * p + j], sibling).wait_send()
                in_half = _landed(zones, p, _shard_of(got), 1 - c)
                _remote(in_half, in_half, fwd_send.at[2 * p + j], fwd_recv.at[2 * p + j], sibling).wait_recv()
            own = _remote(ins[p], zones[p].at[me], own_send.at[i], own_recv.at[i], sibling)
            own.wait_send()
            own.wait_recv()

    outs = pl.pallas_call(
        body, name=name,
        in_specs=[HBM] * (2 * n) + [SEM] * 8 + [ANY],
        out_shape=tuple(pltpu.HBM(a.shape, a.dtype) for a in lands),
        out_specs=tuple([HBM] * n),
        input_output_aliases={n + k: k for k in range(n)},
        compiler_params=pltpu.CompilerParams(has_side_effects=EFFECT),
    )(*shards, *lands, sems["ici_send"], sems["own_send"], sems["own_recv"], relay[0], relay[2], relay[3],
      fwd[0], fwd[1], after)
    return list(outs)


def _all_reduce_small(name, slab, after=None):
    r, width = slab.shape
    ndev = 8

    def body(x_ref, after_ref, out_ref, buf, send_sems, recv_sems):
        x, y, c, _ = _place()
        me = 4 * x + 2 * y + c
        buf[me] = x_ref[...]
        copies = []
        for k in range(1, ndev):
            peer = jnp.bitwise_xor(me, k)
            to = (peer // 4, (peer // 2) % 2, peer % 2)
            cp = pltpu.make_async_remote_copy(src_ref=x_ref, dst_ref=buf.at[me], send_sem=send_sems.at[k - 1],
                                              recv_sem=recv_sems.at[k - 1], device_id=to, device_id_type=MESH)
            cp.start()
            copies.append(cp)
        for k in range(1, ndev):
            peer = jnp.bitwise_xor(me, k)
            pltpu.make_async_remote_copy(src_ref=x_ref, dst_ref=buf.at[peer], send_sem=send_sems.at[k - 1],
                                         recv_sem=recv_sems.at[k - 1], device_id=(x, y, c),
                                         device_id_type=MESH).wait_recv()
        for cp in copies:
            cp.wait_send()
        total = buf[0]
        for d in range(1, ndev):
            total = total + buf[d]
        out_ref[...] = total

    return pl.pallas_call(
        body, name=name,
        in_specs=[pl.BlockSpec(memory_space=pltpu.VMEM), ANY], out_specs=pl.BlockSpec(memory_space=pltpu.VMEM),
        out_shape=jax.ShapeDtypeStruct((r, width), F32),
        scratch_shapes=[pltpu.VMEM((ndev, r, width), F32), pltpu.SemaphoreType.DMA((ndev - 1,)),
                        pltpu.SemaphoreType.DMA((ndev - 1,))],
        compiler_params=pltpu.CompilerParams(vmem_limit_bytes=VMEM_LIMIT),
    )(slab, slab if after is None else after)


def _half_tiling(rows, cols):
    if _by_rows(rows):
        tr = _pick(rows // 2, (256, 128, 64, 32, 16))
        nb = (rows // 2) // tr
        return (tr, cols), nb, (lambda which, b: (which * nb + b, 0)), (lambda b: (b, 0))
    tc = _pick(cols // 2, (256, 128))
    nb = (cols // 2) // tc
    return (rows, tc), nb, (lambda which, b: (0, which * nb + b)), (lambda b: (0, b))


def _chip_partial(name, grad, other, core):
    s, r, cdim = grad.shape
    blk, nb, whole, within = _half_tiling(r, cdim)

    def body(core_ref, g_ref, o_ref, out_ref):
        out_ref[...] = (g_ref[...] + o_ref[...]).astype(BF16)

    return pl.pallas_call(
        body, name=name,
        grid_spec=pltpu.PrefetchScalarGridSpec(
            num_scalar_prefetch=1, grid=(s, nb),
            in_specs=[pl.BlockSpec((None,) + blk, lambda j, b, core_ref: (j,) + whole(core_ref[0], b)),
                      pl.BlockSpec((None,) + blk, lambda j, b, core_ref: (j,) + within(b))],
            out_specs=pl.BlockSpec((None,) + blk, lambda j, b, core_ref: (j,) + within(b))),
        out_shape=jax.ShapeDtypeStruct((s,) + _half_shape(r, cdim), BF16),
        compiler_params=_params("parallel", "parallel"),
    )(core, grad, other)


def _partial_copies(ins, zones, send_sems, recv_sems):
    x, y, c, chips = _place()
    return [_remote(ins[i].at[2 * chip[0] + chip[1]], zones[i].at[j], send_sems.at[3 * i + j],
                    recv_sems.at[3 * i + j], (*chip, c))
            for i in range(len(ins)) for j, chip in enumerate(chips)]


def _swap_copies(ins, zones, send_sems, recv_sems):
    x, y, c, _ = _place()
    copies = []
    for i in range(len(ins)):
        for s in range(N_SHARD):
            copies.append(_remote(_half(ins[i], 1 - c, s), zones[i].at[s],
                                  send_sems.at[N_SHARD * i + s], recv_sems.at[N_SHARD * i + s], (x, y, 1 - c)))
    return copies


def _exchange_start(name, plan, sources, lands, per_array):
    n = len(sources)
    lands = [lax.empty(shape, dtype) for shape, dtype in lands]

    def body(*refs):
        for cp in plan(refs[:n], refs[n:2 * n], refs[2 * n], refs[2 * n + 1]):
            cp.start()
        refs[-1][...] = jnp.zeros_like(refs[-1])

    dma = pltpu.SemaphoreType.DMA
    outs = pl.pallas_call(
        body, name=name,
        in_specs=[HBM] * (2 * n),
        out_shape=(dma((per_array * n,)), dma((per_array * n,)),
                   *[pltpu.HBM(a.shape, a.dtype) for a in list(sources) + lands], jax.ShapeDtypeStruct((8, LANES), F32)),
        out_specs=(SEM, SEM, *[HBM] * (2 * n), pl.BlockSpec(memory_space=pltpu.VMEM)),
        input_output_aliases={k: 2 + k for k in range(2 * n)},
        compiler_params=pltpu.CompilerParams(has_side_effects=EFFECT),
    )(*[_in_hbm(a) for a in list(sources) + lands])
    return (outs[0], outs[1]), list(outs[2:2 + n]), list(outs[2 + n:2 + 2 * n]), outs[-1]


def _exchange_wait(name, plan, started, after):
    sems, partials, lands, _ = started
    n = len(partials)

    def body(*refs):
        for cp in plan(refs[:n], refs[n:2 * n], refs[2 * n], refs[2 * n + 1]):
            cp.wait_send()
            cp.wait_recv()

    outs = pl.pallas_call(
        body, name=name,
        in_specs=[HBM] * (2 * n) + [SEM, SEM] + [ANY] * len(after),
        out_shape=tuple(pltpu.HBM(a.shape, a.dtype) for a in lands),
        out_specs=tuple([HBM] * n),
        input_output_aliases={n + k: k for k in range(n)},
        compiler_params=pltpu.CompilerParams(has_side_effects=EFFECT),
    )(*partials, *lands, sems[0], sems[1], *after)
    return list(outs)


def _reduce_own(name, grad, other, received, where):
    s, r, cdim = grad.shape
    blk, nb, whole, within = _half_tiling(r, cdim)

    def body(where_ref, g_ref, o_ref, r_ref, out_ref):
        total = g_ref[...] + o_ref[...]
        for j in range(3):
            total = total + r_ref[j].astype(F32)
        out_ref[...] = total

    return pl.pallas_call(
        body, name=name,
        grid_spec=pltpu.PrefetchScalarGridSpec(
            num_scalar_prefetch=1, grid=(nb,),
            in_specs=[pl.BlockSpec((None,) + blk, lambda b, w_ref: (w_ref[0],) + whole(w_ref[1], b)),
                      pl.BlockSpec((None,) + blk, lambda b, w_ref: (w_ref[0],) + within(b)),
                      pl.BlockSpec((3,) + blk, lambda b, w_ref: (0,) + within(b))],
            out_specs=pl.BlockSpec(blk, lambda b, w_ref: whole(w_ref[1], b))),
        out_shape=jax.ShapeDtypeStruct((r, cdim), F32),
        compiler_params=_params("parallel"),
    )(where, grad, other, received)


def _join_start(name, halves):
    n = len(halves)

    def body(*refs):
        bufs, send_sems, recv_sems = refs[:n], refs[n], refs[n + 1]
        x, y, c, _ = _place()
        for i in range(n):
            mine = _half(bufs[i], c)
            _remote(mine, mine, send_sems.at[i], recv_sems.at[i], (x, y, 1 - c)).start()
        refs[-1][...] = jnp.zeros_like(refs[-1])

    dma = pltpu.SemaphoreType.DMA
    outs = pl.pallas_call(
        body, name=name,
        in_specs=[HBM] * n,
        out_shape=(dma((n,)), dma((n,)), *[pltpu.HBM(h.shape, F32) for h in halves], jax.ShapeDtypeStruct((8, LANES), F32)),
        out_specs=(SEM, SEM, *[HBM] * n, pl.BlockSpec(memory_space=pltpu.VMEM)),
        input_output_aliases={k: 2 + k for k in range(n)},
        compiler_params=pltpu.CompilerParams(has_side_effects=EFFECT),
    )(*[_in_hbm(h) for h in halves])
    return (outs[0], outs[1]), list(outs[2:2 + n]), outs[-1]


def _join_wait(name, started, after):
    sems, bufs, _ = started
    n = len(bufs)

    def body(*refs):
        bufs, send_sems, recv_sems = refs[:n], refs[n], refs[n + 1]
        x, y, c, _ = _place()
        for i in range(n):
            mine, theirs = _half(bufs[i], c), _half(bufs[i], 1 - c)
            _remote(mine, mine, send_sems.at[i], recv_sems.at[i], (x, y, 1 - c)).wait_send()
            _remote(theirs, theirs, send_sems.at[i], recv_sems.at[i], (x, y, 1 - c)).wait_recv()

    outs = pl.pallas_call(
        body, name=name,
        in_specs=[HBM] * n + [SEM, SEM] + [ANY] * len(after),
        out_shape=tuple(pltpu.HBM(b.shape, F32) for b in bufs),
        out_specs=tuple([HBM] * n),
        input_output_aliases={k: k for k in range(n)},
        compiler_params=pltpu.CompilerParams(has_side_effects=EFFECT),
    )(*bufs, sems[0], sems[1], *after)
    return list(outs)


BIG = ("w_in", "pool_w", "w_out", "xq_w", "xk_w", "xv_w", "xo_w", "w_up", "w_down", "conv_w")
KEPT_F32 = ("conv_w",)
GATHER_GROUPS = ((0, 1, 9), (2, 3, 4, 5, 6), (7,), (8,))
RELAYED = (7, 8)
SMALL = ("conv_w", "a_log", "dt_bias", "gdn_norm_w", "pool_scale", "ln1_g", "ln1_b", "ln2_g", "ln2_b", "ln3_g", "ln3_b")
ORDER = ("w_in", "conv_w", "a_log", "dt_bias", "gdn_norm_w", "pool_w", "pool_scale", "w_out", "ln1_g", "ln1_b",
         "xq_w", "xk_w", "xv_w", "xo_w", "ln2_g", "ln2_b", "w_up", "w_down", "ln3_g", "ln3_b")
LANES = 128


def _rows(flat_len):
    return -(-flat_len // LANES)


def _pack(pieces):
    out = []
    for p in pieces:
        flat = p.reshape(-1).astype(F32)
        out.append(jnp.pad(flat, (0, _rows(flat.shape[0]) * LANES - flat.shape[0])).reshape(-1, LANES))
    slab = jnp.concatenate(out, axis=0)
    return jnp.pad(slab, ((0, -slab.shape[0] % 8), (0, 0)))


def _unpack(slab, shapes):
    out, row = [], 0
    for shp in shapes:
        size = math.prod(shp)
        out.append(slab[row:row + _rows(size)].reshape(-1)[:size].reshape(shp))
        row += _rows(size)
    return out


TRANSPOSED = ("w_in",)


def _as2d(name, a):
    a = a[0]
    if name in TRANSPOSED:
        return jnp.swapaxes(a, 0, 1)
    return a.reshape(-1, a.shape[-1]) if a.ndim == 3 else a


def _from2d(name, a, shape):
    return (jnp.swapaxes(a, 0, 1) if name in TRANSPOSED else a).reshape(shape)


def kernel(x, mem, w_in, conv_w, a_log, dt_bias, gdn_norm_w, pool_w, pool_scale, w_out, ln1_g, ln1_b, xq_w, xk_w, xv_w, xo_w, ln2_g, ln2_b, w_up, w_down, ln3_g, ln3_b, loss_target, m_w_in, m_conv_w, m_a_log, m_dt_bias, m_gdn_norm_w, m_pool_w, m_pool_scale, m_w_out, m_ln1_g, m_ln1_b, m_xq_w, m_xk_w, m_xv_w, m_xo_w, m_ln2_g, m_ln2_b, m_w_up, m_w_down, m_ln3_g, m_ln3_b, v_w_in, v_conv_w, v_a_log, v_dt_bias, v_gdn_norm_w, v_pool_w, v_pool_scale, v_w_out, v_ln1_g, v_ln1_b, v_xq_w, v_xk_w, v_xv_w, v_xo_w, v_ln2_g, v_ln2_b, v_w_up, v_w_down, v_ln3_g, v_ln3_b):
    given = dict(locals())
    cx, cy, cc = lax.axis_index("x"), lax.axis_index("y"), lax.axis_index("c")
    me = 2 * cx + cy
    groups = pool_w.shape[1]
    cs = pool_w.shape[2]
    kk, conv_cols = conv_w.shape[1], conv_w.shape[2]
    core = cc.astype(jnp.int32).reshape(1)
    where = jnp.stack([me, cc]).astype(jnp.int32)

    started = {}
    wts = {}

    def start(name, idx, after, token=None):
        casts = [_after(token, _as2d(BIG[i], given[BIG[i]])).astype(F32 if BIG[i] in KEPT_F32 else BF16) for i in idx]
        relayed = tuple(k for k, i in enumerate(idx) if i in RELAYED)
        sems, shards, lands, token = _gather_start(name, casts, after, relayed)
        for k, i in enumerate(idx):
            started[i] = (sems, k, shards[k], lands[k])
        return token

    token = start("gather_start_first", GATHER_GROUPS[0], x)
    token = start("gather_start_rest", tuple(i for group in GATHER_GROUPS[1:] for i in group), token, token)

    relay = {}

    def send_on(after):
        members = [started[i] for i in RELAYED]
        relay["sems"], zones, token = _gather_relay("gather_relay", [m[1] for m in members], [m[2] for m in members],
                                                    [m[3] for m in members], members[0][0], after)
        relay["zones"] = dict(zip(RELAYED, zones))
        return token

    passed = {}

    def pass_on(group, after):
        members = [started[i] for i in GATHER_GROUPS[group]]
        fwd, zones, token = _gather_forward(f"gather_forward_{group}", [m[1] for m in members], [m[3] for m in members],
                                            members[0][0], after)
        passed[group] = (fwd, zones)
        return token

    def fetch(group, after):
        members = [started[i] for i in GATHER_GROUPS[group]]
        sems, idx = members[0][0], [m[1] for m in members]
        shards = [m[2] for m in members]
        if GATHER_GROUPS[group][0] in RELAYED:
            ks = [RELAYED.index(i) for i in GATHER_GROUPS[group]]
            zones = [relay["zones"][i] for i in GATHER_GROUPS[group]]
            fwd, zones = _gather_forward_relayed(f"gather_forward_{group}", ks, zones, relay["sems"], after)
            got = _gather_wait_relayed(f"gather_wait_{group}", idx, ks, shards, zones, sems, relay["sems"], fwd, after)
        else:
            if group not in passed:
                pass_on(group, after)
            fwd, zones = passed[group]
            got = _gather_wait(f"gather_wait_{group}", idx, shards, zones, sems, fwd, after)
        full = dict(zip([BIG[i] for i in GATHER_GROUPS[group]], got))
        out = {}
        for n, a in full.items():
            if n == "w_in":
                out["w_in_t"] = a
            elif n == "w_up":
                out["w_up3"] = a
            elif n == "pool_w":
                out[n] = a.reshape(N_SHARD, groups, cs, -1).transpose(1, 0, 2, 3).reshape(groups, N_SHARD * cs, -1)
            elif n == "conv_w":
                out[n] = a.transpose(1, 0, 2).reshape(kk, N_SHARD * conv_cols)
            else:
                out[n] = a.reshape(-1, a.shape[-1])
        return out

    for n in ("a_log", "dt_bias", "gdn_norm_w", "pool_scale", "ln1_g", "ln1_b", "ln2_g", "ln2_b", "ln3_g", "ln3_b"):
        wts[n] = given[n]
    x_bf = _after(token, x[0]).astype(BF16)
    wts.update(fetch(0, x_bf))

    def start_swap(group, grads):
        names, blocks = [], []
        for n, g in grads.items():
            if n == "pool_w":
                g = g.reshape(groups, N_SHARD, cs, -1).transpose(1, 0, 2, 3).reshape(N_SHARD, groups * cs, -1)
            elif g.ndim == 2:
                g = g.reshape(N_SHARD, -1, g.shape[-1])
            names.append({"w_in_t": "w_in", "w_up3": "w_up"}.get(n, n))
            blocks.append(g)
        zones = [((N_SHARD,) + _half_shape(b.shape[1], b.shape[2]), F32) for b in blocks]
        swap = _exchange_start(f"grad_swap_start_{group}", _swap_copies, blocks, zones, N_SHARD)
        return {"group": group, "names": names, "swap": swap, "token": swap[3]}

    def start_send(state, after):
        group, names = state["group"], state["names"]
        state["blocks"] = state["swap"][1]
        state["others"] = _exchange_wait(f"grad_swap_wait_{group}", _swap_copies, state["swap"], after)
        partials = [_chip_partial("chip_partial_" + n, gb, ob, core)
                    for n, gb, ob in zip(names, state["blocks"], state["others"])]
        zones = [((3,) + p.shape[1:], BF16) for p in partials]
        state["send"] = _exchange_start(f"grad_send_start_{group}", _partial_copies, partials, zones, 3)
        state["token"] = state["send"][3]

    grad, delta, new_m, new_v = {}, {}, {}, {}

    def start_join(state, after):
        group, names = state["group"], state["names"]
        received = _exchange_wait(f"grad_send_wait_{group}", _partial_copies, state["send"], after)
        halves = [_reduce_own("reduce_own_" + n, gb, ob, rb, where)
                  for n, gb, ob, rb in zip(names, state["blocks"], state["others"], received)]
        state["join"] = _join_start(f"grad_join_start_{group}", halves)
        return state["join"][2]

    def finish_reduce(state, after):
        group, names = state["group"], state["names"]
        for n, g in zip(names, _join_wait(f"grad_join_wait_{group}", state["join"], after)):
            shp = given[n].shape
            d2, m2, v2, g2 = _adamw("adamw_" + n, _as2d(n, given[n]), g, _as2d(n, given["m_" + n]),
                                    _as2d(n, given["v_" + n]))
            grad[n], delta[n], new_m[n], new_v[n] = (_from2d(n, a, shp) for a in (g2, d2, m2, v2))
        return d2

    step = _local_step(x[0], mem[0], loss_target[0], wts, x_bf)
    pending = {}
    request = next(step)
    while True:
        try:
            kind, group, payload = request
            if kind == "weights":
                request = step.send(fetch(group, payload))
            elif kind == "relay":
                request = step.send(send_on(payload))
            elif kind == "pass":
                request = step.send(pass_on(group, payload))
            elif kind == "grads":
                pending[group] = start_swap(group, payload)
                request = step.send(pending[group]["token"])
            else:
                start_send(pending[group], [payload])
                request = step.send(pending[group]["token"])
        except StopIteration as stop:
            loss_row, grad_x, g = stop.value
            break

    after = [pending[2]["token"], grad_x]
    for group in (0, 1):
        after = [start_join(pending[group], after)]
    for group in (0, 1):
        after = [finish_reduce(pending[group], after)]
    after = [finish_reduce(pending[2], [start_join(pending[2], after)])]

    small_names = ("a_log", "dt_bias", "gdn_norm_w", "pool_scale", "ln1_g", "ln1_b", "ln2_g", "ln2_b", "ln3_g", "ln3_b")
    pieces = [g["conv_w"]] + [g[n] for n in small_names] + [loss_row[:, :1]]
    shapes = [p.shape for p in pieces]
    summed = _unpack(_all_reduce_small("all_reduce_small", _pack(pieces), after[0]), shapes)
    gsmall = dict(zip(small_names, summed[1:-1]))
    gsmall["conv_w"] = lax.dynamic_slice(summed[0], (0, me * conv_cols), (kk, conv_cols))
    loss = summed[-1][0, 0]

    sshapes = [given[n].shape for n in SMALL]
    slabs = [_pack([given[p + n] for n in SMALL]) for p in ("", "m_", "v_")]
    gslab = _pack([gsmall[n] for n in SMALL])
    outs = _adamw("adamw_small", slabs[0], gslab, slabs[1], slabs[2])[:3]
    for dst, slab in zip((delta, new_m, new_v), outs):
        dst.update(zip(SMALL, _unpack(slab, sshapes)))
    for n in SMALL:
        grad[n] = gsmall[n].reshape(given[n].shape)

    return (loss, grad_x[None], *[grad[n] for n in ORDER], *[delta[n] for n in ORDER],
            *[new_m[n] for n in ORDER], *[new_v[n] for n in ORDER])
```

```python
import math

import jax
import jax.numpy as jnp
from jax import lax
from jax.experimental import pallas as pl
from jax.experimental.pallas import tpu as pltpu

F32 = jnp.float32
BF16 = jnp.bfloat16
MESH = pl.DeviceIdType.MESH

HEAD_DIM = 128
CHUNK = 64
POOL_WINDOWS = (2, 4, 8, 16)
XATTN_HEADS = 4
ALPHA = 2.0 ** 0.25
LN_EPS = 1e-5
NORM_EPS = 1e-6
ADAM_LR, ADAM_B1, ADAM_B2, ADAM_EPS, ADAM_WD, ADAM_STEP = 0.001, 0.9, 0.999, 1e-08, 0.01, 10
N_SHARD = 4
VMEM_LIMIT = 56 * 1024 * 1024
K_STEPS = (2048, 1024, 512, 256, 128)


def _params(*sem):
    return pltpu.CompilerParams(dimension_semantics=sem, vmem_limit_bytes=VMEM_LIMIT)


def _bdot(a, b, ta=False, tb=False):
    dims = (((0 if ta else 1,), (1 if tb else 0,)), ((), ()))
    return lax.dot_general(a.astype(BF16), b.astype(BF16), dims, preferred_element_type=F32)


def _sigmoid(x):
    return 1.0 / (1.0 + jnp.exp(-x))


def _matmul(name, a, b, *, ta=False, tb=False, tm, tn, tk, extra=(), outs, epilogue, b_blocks=None,
            sequential=False, n_used=None, n_outer=False):
    m, k_dim = (a.shape[1], a.shape[0]) if ta else a.shape
    if b_blocks and tb:
        n = b.shape[1]
        k_dim = b.shape[0] * b.shape[2]
        per = b.shape[2] // tk
        b_spec = pl.BlockSpec((None, tn, tk), lambda i, j, k: (k // per, j, k % per))
    elif b_blocks:
        n = b.shape[0] * b.shape[2]
        per = b.shape[2] // tn
        b_spec = pl.BlockSpec((None, tk, tn), lambda i, j, k: (j // per, k, j % per))
    elif tb:
        n = b.shape[0]
        b_spec = pl.BlockSpec((tn, tk), lambda i, j, k: (j, k))
    else:
        n = b.shape[1]
        b_spec = pl.BlockSpec((tk, tn), lambda i, j, k: (k, j))
    n = n_used or n
    assert m % tm == 0 and n % tn == 0 and k_dim % tk == 0, (name, m, n, k_dim, tm, tn, tk)
    nk = k_dim // tk
    a_spec = pl.BlockSpec((tk, tm), lambda i, j, k: (k, i)) if ta else pl.BlockSpec((tm, tk), lambda i, j, k: (i, k))
    n_extra, n_out = len(extra), len(outs)

    def wrap(index_map):
        return lambda i, j, k: index_map(i, j)

    def spec(block, index_map):
        if n_outer:
            return pl.BlockSpec(block, lambda j, i, k: index_map(i, j, k))
        return pl.BlockSpec(block, index_map)

    row_axis = 1 if n_outer else 0

    def body_one_step(*refs):
        ex = refs[2:2 + n_extra]
        out = refs[2 + n_extra:2 + n_extra + n_out]
        epilogue(_bdot(refs[0][...], refs[1][...], ta, tb), ex, out, pl.program_id(row_axis))

    def body(*refs):
        a_ref, b_ref = refs[0], refs[1]
        ex = refs[2:2 + n_extra]
        out = refs[2 + n_extra:2 + n_extra + n_out]
        acc = refs[-1]
        i, k = pl.program_id(row_axis), pl.program_id(2)
        part = _bdot(a_ref[...], b_ref[...], ta, tb)

        @pl.when(k == 0)
        def _():
            acc[...] = part

        @pl.when(jnp.logical_and(k > 0, k < nk - 1))
        def _():
            acc[...] += part

        @pl.when(k == nk - 1)
        def _():
            epilogue(acc[...] + part, ex, out, i)

    sem = ("arbitrary",) * 3 if sequential else ("parallel", "parallel", "arbitrary")
    res = pl.pallas_call(
        body_one_step if nk == 1 else body, name=name,
        grid=(n // tn, m // tm, nk) if n_outer else (m // tm, n // tn, nk),
        in_specs=[spec(a_spec.block_shape, a_spec.index_map), spec(b_spec.block_shape, b_spec.index_map)]
        + [spec(bs, wrap(im)) for _, bs, im in extra],
        out_specs=[spec(bs, wrap(im)) for _, bs, im in outs],
        out_shape=[s for s, _, _ in outs],
        scratch_shapes=[] if nk == 1 else [pltpu.VMEM((tm, tn), F32)],
        compiler_params=_params(*sem),
    )(a, b, *[x for x, _, _ in extra])
    return res


def _tile(i, j):
    return (i, j)


def _plain(name, a, b, *, ta=False, tb=False, tm, tn, tk, out_dtype, b_blocks=None, out3=None, n_used=None,
           n_outer=False, m_kept=None):
    m = a.shape[1] if ta else a.shape[0]
    if b_blocks:
        n = b.shape[1] if tb else b.shape[0] * b.shape[2]
    else:
        n = n_used or (b.shape[0] if tb else b.shape[1])

    def epi(acc, ex, out, i):
        out[0][...] = acc.astype(out_dtype)

    if out3:
        per = (n // out3) // tn
        spec = (jax.ShapeDtypeStruct((out3, m, n // out3), out_dtype), (None, tm, tn),
                lambda i, j: (j // per, i, j % per))
    else:
        spec = (jax.ShapeDtypeStruct((m_kept or m, n), out_dtype), (tm, tn), _tile)
    return _matmul(name, a, b, ta=ta, tb=tb, tm=tm, tn=tn, tk=tk, outs=[spec], epilogue=epi,
                   b_blocks=b_blocks, n_used=n_used, n_outer=n_outer)[0]


def _ln_forward(name, a, b, res, gamma, beta, *, tm, tk, want_h=True):
    m, n = res.shape

    def epi(acc, ex, out, i):
        u = ALPHA * ex[0][...] + acc
        mu = jnp.mean(u, axis=-1, keepdims=True)
        xc = u - mu
        var = jnp.mean(xc * xc, axis=-1, keepdims=True)
        rstd = lax.rsqrt(var + LN_EPS)
        xhat = xc * rstd
        out[-2][...] = xhat
        out[-1][...] = rstd
        if want_h:
            h = xhat * ex[1][...] + ex[2][...]
            out[0][...] = h
            out[1][...] = h.astype(BF16)

    row = lambda i, j: (i, 0)
    vec = lambda i, j: (0, 0)
    outs = [(jax.ShapeDtypeStruct((m, n), F32), (tm, n), row), (jax.ShapeDtypeStruct((m, n), BF16), (tm, n), row),
            (jax.ShapeDtypeStruct((m, n), F32), (tm, n), row), (jax.ShapeDtypeStruct((m, 1), F32), (tm, 1), row)]
    return _matmul(
        name, a, b, tm=tm, tn=n, tk=tk,
        extra=[(res, (tm, n), row), (gamma, (1, n), vec), (beta, (1, n), vec)],
        outs=outs if want_h else outs[2:], epilogue=epi)


def _ln_backward_math(dy, xhat, rstd, gamma):
    dxhat = dy * gamma
    m1 = jnp.mean(dxhat, axis=-1, keepdims=True)
    m2 = jnp.mean(dxhat * xhat, axis=-1, keepdims=True)
    du = rstd * (dxhat - m1 - xhat * m2)
    return du, jnp.sum(dy * xhat, axis=0, keepdims=True), jnp.sum(dy, axis=0, keepdims=True)


def _ln_backward(name, a, b, dres, xhat, rstd, gamma, *, tm, tk, b_blocks=None, tb=True):
    m, n = dres.shape

    def epi(acc, ex, out, i):
        dy = acc + ALPHA * ex[0][...]
        du, dg, db = _ln_backward_math(dy, ex[1][...], ex[2][...], ex[3][...])
        out[0][...] = du
        out[1][...] = du.astype(BF16)
        first = i == 0

        @pl.when(first)
        def _():
            out[2][...] = dg
            out[3][...] = db

        @pl.when(jnp.logical_not(first))
        def _():
            out[2][...] += dg
            out[3][...] += db

    row = lambda i, j: (i, 0)
    vec = lambda i, j: (0, 0)
    return _matmul(
        name, a, b, tb=tb, tm=tm, tn=n, tk=tk, b_blocks=b_blocks, sequential=True,
        extra=[(dres, (tm, n), row), (xhat, (tm, n), row), (rstd, (tm, 1), row), (gamma, (1, n), vec)],
        outs=[(jax.ShapeDtypeStruct((m, n), F32), (tm, n), row),
              (jax.ShapeDtypeStruct((m, n), BF16), (tm, n), row),
              (jax.ShapeDtypeStruct((1, n), F32), (1, n), vec),
              (jax.ShapeDtypeStruct((1, n), F32), (1, n), vec)],
        epilogue=epi)


def _shift_down(x, k):
    row = lax.broadcasted_iota(jnp.int32, x.shape, 0)
    return jnp.where(row >= k, pltpu.roll(x, k, axis=0), 0.0)


def _shift_up(x, k):
    t = x.shape[0]
    row = lax.broadcasted_iota(jnp.int32, x.shape, 0)
    return jnp.where(row < t - k, pltpu.roll(x, t - k, axis=0), 0.0)


def _conv_silu_norm(x, w, normalise):
    kk = w.shape[0]
    c = x * w[kk - 1:kk, :]
    for j in range(kk - 1):
        c = c + _shift_down(x, kk - 1 - j) * w[j:j + 1, :]
    sg = _sigmoid(c)
    s = c * sg
    r = lax.rsqrt(jnp.sum(s * s, axis=-1, keepdims=True) + NORM_EPS)
    y = jnp.where(normalise, s * r, s)
    return c, sg, s, r, y


def _gdn_pre(proj, conv_w, heads):
    t = proj.shape[0]
    kk = conv_w.shape[0]

    def body(x_ref, w_ref, o_ref):
        normalise = pl.program_id(0) < 2
        o_ref[...] = _conv_silu_norm(x_ref[...], w_ref[...], normalise)[4]

    col = lambda s, h: (0, s * heads + h)
    return pl.pallas_call(
        body, name="gdn_pre", grid=(3, heads),
        in_specs=[pl.BlockSpec((t, HEAD_DIM), col), pl.BlockSpec((kk, HEAD_DIM), col)],
        out_specs=pl.BlockSpec((t, HEAD_DIM), col),
        out_shape=jax.ShapeDtypeStruct((t, 3 * heads * HEAD_DIM), F32),
        compiler_params=_params("parallel", "parallel"),
    )(proj, conv_w)


def _gdn_pre_backward(proj, conv_w, dqkv, heads):
    t = proj.shape[0]
    kk = conv_w.shape[0]

    def body(x_ref, w_ref, dy_ref, dx_ref, dw_ref):
        normalise = pl.program_id(0) < 2
        x = x_ref[...]
        w = w_ref[...]
        dy = dy_ref[...]
        c, sg, s, r, y = _conv_silu_norm(x, w, normalise)
        ds_norm = r * (dy - y * jnp.sum(dy * y, axis=-1, keepdims=True))
        ds = jnp.where(normalise, ds_norm, dy)
        dc = ds * (sg * (1.0 + c * (1.0 - sg)))
        dx = dc * w[kk - 1:kk, :]
        rows = [None] * kk
        rows[kk - 1] = jnp.sum(dc * x, axis=0, keepdims=True)
        for j in range(kk - 1):
            lag = kk - 1 - j
            dx = dx + _shift_up(dc, lag) * w[j:j + 1, :]
            rows[j] = jnp.sum(dc * _shift_down(x, lag), axis=0, keepdims=True)
        dx_ref[...] = dx.astype(BF16)
        dw_ref[...] = jnp.concatenate(rows, axis=0)

    col = lambda s, h: (0, s * heads + h)
    return pl.pallas_call(
        body, name="gdn_pre_bwd", grid=(3, heads),
        in_specs=[pl.BlockSpec((t, HEAD_DIM), col), pl.BlockSpec((kk, HEAD_DIM), col),
                  pl.BlockSpec((t, HEAD_DIM), col)],
        out_specs=[pl.BlockSpec((t, HEAD_DIM), col), pl.BlockSpec((kk, HEAD_DIM), col)],
        out_shape=[jax.ShapeDtypeStruct((t, 3 * heads * HEAD_DIM), BF16),
                   jax.ShapeDtypeStruct((kk, 3 * heads * HEAD_DIM), F32)],
        compiler_params=_params("parallel", "parallel"),
    )(proj, conv_w, dqkv)


def _gate_vectors(a_log, dt_bias, heads):
    pad = lambda v: jnp.pad(v.astype(F32), ((0, 0), (heads, HEAD_DIM - 2 * heads)))
    return pad(jnp.exp(a_log.astype(F32))), pad(dt_bias)


def _softplus(x):
    return jnp.maximum(x, 0.0) + jnp.log(1.0 + jnp.exp(-jnp.abs(x)))


def _gates_epilogue(heads):
    def epi(acc, ex, out, i):
        lane = lax.broadcasted_iota(jnp.int32, acc.shape, 1)
        beta = _sigmoid(acc)
        g = -ex[0][...] * _softplus(acc + ex[1][...])
        out[0][...] = acc
        out[1][...] = jnp.where(lane < heads, beta, jnp.where(lane < 2 * heads, g, 0.0))
    return epi


def _gates_backward(ba, bg, dbg, ea, dtb, heads):
    t = ba.shape[0]

    def body(ba_ref, bg_ref, d_ref, ea_ref, dt_ref, dba_ref, dal_ref, ddt_ref):
        lane = lax.broadcasted_iota(jnp.int32, (t, HEAD_DIM), 1)
        bgv = bg_ref[...]
        d = d_ref[...]
        db = d * bgv * (1.0 - bgv)
        da = -d * ea_ref[...] * _sigmoid(ba_ref[...] + dt_ref[...])
        is_g = jnp.logical_and(lane >= heads, lane < 2 * heads)
        dba = jnp.where(lane < heads, db, jnp.where(is_g, da, 0.0))
        dba_ref[...] = dba.astype(BF16)
        dal_ref[...] = jnp.sum(jnp.where(is_g, d * bgv, 0.0), axis=0, keepdims=True)
        ddt_ref[...] = jnp.sum(jnp.where(is_g, da, 0.0), axis=0, keepdims=True)

    full = pl.BlockSpec((t, HEAD_DIM), lambda: (0, 0))
    vec = pl.BlockSpec((1, HEAD_DIM), lambda: (0, 0))
    return pl.pallas_call(
        body, name="gates_bwd", grid=(),
        in_specs=[full, full, full, vec, vec], out_specs=[full, vec, vec],
        out_shape=[jax.ShapeDtypeStruct((t, HEAD_DIM), BF16), jax.ShapeDtypeStruct((1, HEAD_DIM), F32),
                   jax.ShapeDtypeStruct((1, HEAD_DIM), F32)],
        compiler_params=pltpu.CompilerParams(vmem_limit_bytes=VMEM_LIMIT),
    )(ba, bg, dbg, ea, dtb)


class _Chunk:
    pass


def _split2(x):
    hi = x.astype(BF16)
    return hi, (x - hi.astype(F32)).astype(BF16)


def _split3(x):
    hi = x.astype(BF16)
    rest = x - hi.astype(F32)
    mid = rest.astype(BF16)
    return hi, mid, (rest - mid.astype(F32)).astype(BF16)


def _dot_mask(mask, x, ta=False):
    hi, mid, lo = _split3(x)
    return _bdot(mask, hi, ta=ta) + (_bdot(mask, mid, ta=ta) + _bdot(mask, lo, ta=ta))


def _transpose_by_identity(x):
    r = x.shape[0]
    eye = (lax.broadcasted_iota(jnp.int32, (r, r), 0) == lax.broadcasted_iota(jnp.int32, (r, r), 1)).astype(BF16)
    hi, mid, lo = _split3(x)
    return _bdot(hi, eye, ta=True) + (_bdot(mid, eye, ta=True) + _bdot(lo, eye, ta=True))


def _dot22(a, b, ta=False, tb=False):
    ah, al = _split2(a)
    bh, bl = _split2(b)
    return _bdot(ah, bh, ta, tb) + (_bdot(ah, bl, ta, tb) + _bdot(al, bh, ta, tb))


def _chunk_gates(bg, heads):
    n = CHUNK
    row = lax.broadcasted_iota(jnp.int32, (n, n), 0)
    col = lax.broadcasted_iota(jnp.int32, (n, n), 1)
    lane = lax.broadcasted_iota(jnp.int32, bg.shape, 1)
    graw = jnp.where(jnp.logical_and(lane >= heads, lane < 2 * heads), bg, 0.0)
    gc = _dot_mask((row >= col).astype(BF16), graw)
    return gc, _transpose_by_identity(gc)


def _in_lockstep(generators):
    results = [None] * len(generators)
    live = list(enumerate(generators))
    while live:
        still = []
        for i, gen in live:
            try:
                next(gen)
                still.append((i, gen))
            except StopIteration as stop:
                results[i] = stop.value
        live = still
    return results


def _chunk_local(q, k, v, beta, gc, grow, solved=None):
    c = _Chunk()
    n = CHUNK
    row = lax.broadcasted_iota(jnp.int32, (n, n), 0)
    col = lax.broadcasted_iota(jnp.int32, (n, n), 1)
    c.tri = row >= col
    c.strict = row > col
    eye = row == col
    c.gcb = jnp.broadcast_to(gc, (n, HEAD_DIM))
    c.decay = jnp.where(c.tri, jnp.exp(jnp.where(c.tri, gc - grow, 0.0)), 0.0)
    c.eg = jnp.exp(c.gcb)
    glast = c.gcb[n - 1:n, :]
    c.egl = jnp.exp(glast)
    c.ekl = jnp.exp(glast - c.gcb)
    c.beta = beta
    c.q = q * (HEAD_DIM ** -0.5)
    c.k = k
    c.v = v
    c.kb = k * beta
    c.vb = v * beta
    c.kg = c.kb * c.eg
    both = _bdot(jnp.concatenate([c.kb, c.q], axis=0), k, tb=True)
    yield
    c.L = jnp.where(c.strict, both[:n] * c.decay, 0.0)
    c.A = jnp.where(c.tri, both[n:] * c.decay, 0.0)
    if solved is None:
        x = -c.L
        tinv = eye.astype(F32) + x
        p = _dot22(x, x)
        yield
        for _ in range(int(math.log2(n)) - 2):
            both = _dot22(jnp.concatenate([p, tinv], axis=0), p)
            yield
            p, tinv = both[:n], tinv + both[n:]
        c.T = tinv + _dot22(tinv, p)
        yield
        uw = _dot22(c.T, jnp.concatenate([c.vb, c.kg], axis=1))
        yield
        c.u, c.w = uw[:, :HEAD_DIM], uw[:, HEAD_DIM:]
    else:
        c.T, c.u, c.w = solved
    c.qg = c.q * c.eg
    c.kdec = k * c.ekl
    return c


def _gdn_core(qkv, bg, heads):
    t = qkv.shape[0]
    nchunk = t // CHUNK

    gw = heads * HEAD_DIM

    def body(qkv_ref, bg_ref, o_ref, s_ref, t_ref, u_ref, w_ref, state):
        @pl.when(pl.program_id(0) == 0)
        def _():
            state[...] = jnp.zeros_like(state)

        bg_v = bg_ref[...]
        gc_all, gc_rows = _chunk_gates(bg_v, heads)
        def one_head(h):
            col = lambda s: pl.ds(s * gw + h * HEAD_DIM, HEAD_DIM)
            c = yield from _chunk_local(qkv_ref[:, col(0)], qkv_ref[:, col(1)], qkv_ref[:, col(2)], bg_v[:, h:h + 1],
                                        gc_all[:, heads + h:heads + h + 1], gc_rows[heads + h:heads + h + 1, :])
            s0 = state[h]
            v_new = c.u - _bdot(c.w, s0)
            yield
            o = _bdot(c.qg, s0) + _bdot(c.A, v_new)
            return s0, o, s0 * c.egl + _bdot(c.kdec, v_new, ta=True), c

        results = _in_lockstep([one_head(h) for h in range(heads)])
        for h, (s0, o, s1, c) in enumerate(results):
            lanes = pl.ds(h * HEAD_DIM, HEAD_DIM)
            s_ref[h, 0] = s0
            o_ref[:, lanes] = o
            t_ref[:, lanes] = jnp.concatenate([c.T, jnp.zeros((CHUNK, HEAD_DIM - CHUNK), F32)], axis=1)
            u_ref[:, lanes] = c.u
            w_ref[:, lanes] = c.w
            state[h] = s1

    return pl.pallas_call(
        body, name="gdn_core", grid=(nchunk,),
        in_specs=[pl.BlockSpec((CHUNK, 3 * gw), lambda n: (n, 0)), pl.BlockSpec((CHUNK, HEAD_DIM), lambda n: (n, 0))],
        out_specs=[pl.BlockSpec((CHUNK, gw), lambda n: (n, 0)),
                   pl.BlockSpec((heads, 1, HEAD_DIM, HEAD_DIM), lambda n: (0, n, 0, 0))]
        + [pl.BlockSpec((CHUNK, gw), lambda n: (n, 0))] * 3,
        out_shape=[jax.ShapeDtypeStruct((t, gw), F32),
                   jax.ShapeDtypeStruct((heads, nchunk, HEAD_DIM, HEAD_DIM), F32)]
        + [jax.ShapeDtypeStruct((t, gw), F32)] * 3,
        scratch_shapes=[pltpu.VMEM((heads, HEAD_DIM, HEAD_DIM), F32)],
        compiler_params=_params("arbitrary"),
    )(qkv, bg)


def _gdn_core_backward(qkv, bg, states, solved, do, heads):
    t = qkv.shape[0]
    nchunk = t // CHUNK
    n = CHUNK

    def one_head(chunk_local, s0, d_out, ds1):
        c = yield from chunk_local
        v_new = c.u - _bdot(c.w, s0)
        dqg = _bdot(d_out, s0, tb=True)
        ds0 = _bdot(c.qg, d_out, ta=True) + ds1 * c.egl
        dv_new = _bdot(c.A, d_out, ta=True) + _bdot(c.kdec, ds1)
        yield
        dA = jnp.where(c.tri, _bdot(d_out, v_new, tb=True), 0.0)
        dkdec = _bdot(v_new, ds1, tb=True)
        dgl = jnp.sum(jnp.sum(ds1 * s0, axis=1, keepdims=True), axis=0, keepdims=True) * c.egl
        dw = -_bdot(dv_new, s0, tb=True)
        ds0 = ds0 - _bdot(c.w, dv_new, ta=True)
        yield
        both = _dot22(c.T, jnp.concatenate([dv_new, dw], axis=1), ta=True)
        yield
        dvb, dkg = both[:, :HEAD_DIM], both[:, HEAD_DIM:]
        dL = jnp.where(c.strict, -(_bdot(dvb, c.u, tb=True) + _bdot(dkg, c.w, tb=True)), 0.0)
        yield
        dm1 = dL * c.decay
        dkb = _bdot(dm1, c.k) + dkg * c.eg
        dk = _bdot(dm1, c.kb, ta=True)
        dm2 = dA * c.decay
        dq = _bdot(dm2, c.k) + dqg * c.eg
        dk = dk + _bdot(dm2, c.q, ta=True) + dkdec * c.ekl + dkb * c.beta
        pm = dL * c.L + dA * c.A
        ones = jnp.ones((n, HEAD_DIM), BF16)
        pm_hi, pm_lo = _split2(pm)
        colsum = _bdot(pm_hi, ones, ta=True) + _bdot(pm_lo, ones, ta=True)
        tk_ = jnp.sum(dkdec * c.kdec, axis=1, keepdims=True)
        dgc = (jnp.sum(pm, axis=1, keepdims=True) - colsum
               + jnp.sum(dqg * c.qg, axis=1, keepdims=True)
               - tk_
               + jnp.sum(dkg * c.kg, axis=1, keepdims=True))
        dgl = dgl + jnp.sum(tk_, axis=0, keepdims=True)
        rowi = lax.broadcasted_iota(jnp.int32, (n, HEAD_DIM), 0)
        dgc = dgc + jnp.where(rowi == n - 1, dgl, 0.0)
        dbeta = jnp.sum(dkb * c.k, axis=1, keepdims=True) + jnp.sum(dvb * c.v, axis=1, keepdims=True)
        return dq * (HEAD_DIM ** -0.5), dk, dvb * c.beta, dbeta, dgc, ds0

    gw = heads * HEAD_DIM

    def body(qkv_ref, bg_ref, s_ref, t_ref, u_ref, w_ref, do_ref, dqkv_ref, dbg_ref, dstate):
        @pl.when(pl.program_id(0) == 0)
        def _():
            dstate[...] = jnp.zeros_like(dstate)

        bg_v = bg_ref[...]
        gc_all, gc_rows = _chunk_gates(bg_v, heads)
        lane = lax.broadcasted_iota(jnp.int32, (n, HEAD_DIM), 1)
        dgates = jnp.zeros((n, HEAD_DIM), F32)
        chains = []
        for h in range(heads):
            col = lambda s: pl.ds(s * gw + h * HEAD_DIM, HEAD_DIM)
            lanes = pl.ds(h * HEAD_DIM, HEAD_DIM)
            c = _chunk_local(qkv_ref[:, col(0)], qkv_ref[:, col(1)], qkv_ref[:, col(2)], bg_v[:, h:h + 1],
                             gc_all[:, heads + h:heads + h + 1], gc_rows[heads + h:heads + h + 1, :],
                             (t_ref[:, pl.ds(h * HEAD_DIM, CHUNK)], u_ref[:, lanes], w_ref[:, lanes]))
            chains.append(one_head(c, s_ref[h, 0], do_ref[:, pl.ds(h * HEAD_DIM, HEAD_DIM)], dstate[h]))
        results = _in_lockstep(chains)
        for h, (dq, dk, dv, dbeta, dgc, ds0) in enumerate(results):
            dgates = jnp.where(lane == h, dbeta, jnp.where(lane == heads + h, dgc, dgates))
        for h, (dq, dk, dv, dbeta, dgc, ds0) in enumerate(results):
            dqkv_ref[:, pl.ds(h * HEAD_DIM, HEAD_DIM)] = dq
            dqkv_ref[:, pl.ds(gw + h * HEAD_DIM, HEAD_DIM)] = dk
            dqkv_ref[:, pl.ds(2 * gw + h * HEAD_DIM, HEAD_DIM)] = dv
            dstate[h] = ds0
        row = lax.broadcasted_iota(jnp.int32, (n, n), 0)
        colm = lax.broadcasted_iota(jnp.int32, (n, n), 1)
        draw = _dot_mask((row >= colm).astype(BF16), dgates, ta=True)
        dbg_ref[...] = jnp.where(lane < heads, dgates, draw)

    last = nchunk - 1
    return pl.pallas_call(
        body, name="gdn_core_bwd", grid=(nchunk,),
        in_specs=[pl.BlockSpec((CHUNK, 3 * gw), lambda i: (last - i, 0)),
                  pl.BlockSpec((CHUNK, HEAD_DIM), lambda i: (last - i, 0)),
                  pl.BlockSpec((heads, 1, HEAD_DIM, HEAD_DIM), lambda i: (0, last - i, 0, 0))]
        + [pl.BlockSpec((CHUNK, gw), lambda i: (last - i, 0))] * 4,
        out_specs=[pl.BlockSpec((CHUNK, 3 * gw), lambda i: (last - i, 0)),
                   pl.BlockSpec((CHUNK, HEAD_DIM), lambda i: (last - i, 0))],
        out_shape=[jax.ShapeDtypeStruct((t, 3 * gw), F32), jax.ShapeDtypeStruct((t, HEAD_DIM), F32)],
        scratch_shapes=[pltpu.VMEM((heads, HEAD_DIM, HEAD_DIM), F32)],
        compiler_params=_params("arbitrary"),
    )(qkv, bg, states, *solved, do)


def _gdn_post(o, proj, z_col0, norm_w, heads, tt):
    t = o.shape[0]
    zb = z_col0 // HEAD_DIM

    def body(o_ref, z_ref, w_ref, out_ref):
        ov = o_ref[...]
        z = z_ref[...]
        rms = lax.rsqrt(jnp.mean(ov * ov, axis=-1, keepdims=True) + NORM_EPS)
        out_ref[...] = (ov * rms * w_ref[...] * (z * _sigmoid(z))).astype(BF16)

    return pl.pallas_call(
        body, name="gdn_post", grid=(t // tt, heads),
        in_specs=[pl.BlockSpec((tt, HEAD_DIM), lambda i, h: (i, h)),
                  pl.BlockSpec((tt, HEAD_DIM), lambda i, h: (i, zb + h)),
                  pl.BlockSpec((1, HEAD_DIM), lambda i, h: (0, 0))],
        out_specs=pl.BlockSpec((tt, HEAD_DIM), lambda i, h: (i, h)),
        out_shape=jax.ShapeDtypeStruct((t, heads * HEAD_DIM), BF16),
        compiler_params=_params("parallel", "parallel"),
    )(o, proj, norm_w)


def _gdn_post_backward(dcat, o, proj, z_col0, norm_w, heads, tt):
    t = o.shape[0]
    zb = z_col0 // HEAD_DIM

    def body(d_ref, o_ref, z_ref, w_ref, do_ref, dz_ref, dw_ref):
        d = d_ref[...]
        ov = o_ref[...]
        z = z_ref[...]
        w = w_ref[...]
        rms = lax.rsqrt(jnp.mean(ov * ov, axis=-1, keepdims=True) + NORM_EPS)
        ohat = ov * rms
        sg = _sigmoid(z)
        gate = z * sg
        dz_ref[...] = (d * ohat * w * (sg * (1.0 + z * (1.0 - sg)))).astype(BF16)
        don = d * gate
        dohat = don * w
        do_ref[...] = rms * (dohat - ohat * jnp.mean(dohat * ohat, axis=-1, keepdims=True))
        dw = jnp.sum(don * ohat, axis=0, keepdims=True)
        first = jnp.logical_and(pl.program_id(0) == 0, pl.program_id(1) == 0)

        @pl.when(first)
        def _():
            dw_ref[...] = dw

        @pl.when(jnp.logical_not(first))
        def _():
            dw_ref[...] += dw

    blk = pl.BlockSpec((tt, HEAD_DIM), lambda i, h: (i, h))
    return pl.pallas_call(
        body, name="gdn_post_bwd", grid=(t // tt, heads),
        in_specs=[blk, blk, pl.BlockSpec((tt, HEAD_DIM), lambda i, h: (i, zb + h)),
                  pl.BlockSpec((1, HEAD_DIM), lambda i, h: (0, 0))],
        out_specs=[blk, blk, pl.BlockSpec((1, HEAD_DIM), lambda i, h: (0, 0))],
        out_shape=[jax.ShapeDtypeStruct((t, heads * HEAD_DIM), F32),
                   jax.ShapeDtypeStruct((t, heads * HEAD_DIM), BF16),
                   jax.ShapeDtypeStruct((1, HEAD_DIM), F32)],
        compiler_params=_params("arbitrary", "arbitrary"),
    )(dcat, o, proj, norm_w)


def _pool_select(levels, group):
    out = levels[-1]
    for gi in range(len(levels) - 2, -1, -1):
        out = jnp.where(group == gi, levels[gi], out)
    return out


def _pool_counts(t, width, group):
    pos = lax.broadcasted_iota(jnp.int32, (t, width), 0)
    win = jnp.left_shift(2, group)
    return jnp.minimum(pos + 1, win).astype(F32)


def _pooled(p, group):
    levels, s, step = [], p, 1
    for _ in POOL_WINDOWS:
        s = s + _shift_down(s, step)
        levels.append(s)
        step *= 2
    cnt = _pool_counts(p.shape[0], p.shape[1], group)
    return _pool_select(levels, group) / cnt - p, cnt


def _pool_forward(proj, p_col0, pool_w, pool_scale):
    t = proj.shape[0]
    groups, cg, _ = pool_w.shape
    pb = p_col0 // cg

    def body(p_ref, w_ref, s_ref, o_ref):
        pooled, _ = _pooled(p_ref[...], pl.program_id(0))
        o_ref[...] = (_bdot(pooled, w_ref[0]) * s_ref[...]).astype(BF16)

    return pl.pallas_call(
        body, name="pool_fwd", grid=(groups,),
        in_specs=[pl.BlockSpec((t, cg), lambda g: (0, pb + g)), pl.BlockSpec((1, cg, cg), lambda g: (g, 0, 0)),
                  pl.BlockSpec((1, cg), lambda g: (0, g))],
        out_specs=pl.BlockSpec((t, cg), lambda g: (0, g)),
        out_shape=jax.ShapeDtypeStruct((t, groups * cg), BF16),
        compiler_params=_params("parallel"),
    )(proj, pool_w, pool_scale)


def _pool_backward(dcat, d_col0, proj, p_col0, pool_w, pool_scale):
    t = proj.shape[0]
    groups, cg, _ = pool_w.shape
    pb = p_col0 // cg
    db = d_col0 // cg

    def body(d_ref, p_ref, w_ref, s_ref, dp_ref, dw_ref, ds_ref):
        group = pl.program_id(0)
        pooled, cnt = _pooled(p_ref[...], group)
        w = w_ref[0]
        d = d_ref[...]
        mixed = _bdot(pooled, w)
        ds_ref[...] = jnp.sum(d * mixed, axis=0, keepdims=True)
        dmixed = d * s_ref[...]
        dw_ref[0] = _bdot(pooled, dmixed, ta=True)
        dpooled = _bdot(dmixed, w, tb=True)
        levels, s, step = [], dpooled / cnt, 1
        for _ in POOL_WINDOWS:
            s = s + _shift_up(s, step)
            levels.append(s)
            step *= 2
        dp_ref[...] = (_pool_select(levels, group) - dpooled).astype(BF16)

    return pl.pallas_call(
        body, name="pool_bwd", grid=(groups,),
        in_specs=[pl.BlockSpec((t, cg), lambda g: (0, db + g)), pl.BlockSpec((t, cg), lambda g: (0, pb + g)),
                  pl.BlockSpec((1, cg, cg), lambda g: (g, 0, 0)), pl.BlockSpec((1, cg), lambda g: (0, g))],
        out_specs=[pl.BlockSpec((t, cg), lambda g: (0, g)), pl.BlockSpec((1, cg, cg), lambda g: (g, 0, 0)),
                   pl.BlockSpec((1, cg), lambda g: (0, g))],
        out_shape=[jax.ShapeDtypeStruct((t, groups * cg), BF16), jax.ShapeDtypeStruct((groups, cg, cg), F32),
                   jax.ShapeDtypeStruct((1, groups * cg), F32)],
        compiler_params=_params("parallel"),
    )(dcat, proj, pool_w, pool_scale)


def _attention(q, k, v, tq):
    t, d = q.shape
    m = k.shape[0]
    dh = d // XATTN_HEADS
    scale = dh ** -0.5

    def body(q_ref, k_ref, v_ref, o_ref):
        s = _bdot(q_ref[...], k_ref[...], tb=True) * scale
        s = s - jnp.max(s, axis=-1, keepdims=True)
        e = jnp.exp(s)
        p = e / jnp.sum(e, axis=-1, keepdims=True)
        o_ref[...] = _bdot(p, v_ref[...]).astype(BF16)

    return pl.pallas_call(
        body, name="xattn_fwd", grid=(XATTN_HEADS, t // tq),
        in_specs=[pl.BlockSpec((tq, dh), lambda h, i: (i, h)), pl.BlockSpec((m, dh), lambda h, i: (0, h)),
                  pl.BlockSpec((m, dh), lambda h, i: (0, h))],
        out_specs=pl.BlockSpec((tq, dh), lambda h, i: (i, h)),
        out_shape=jax.ShapeDtypeStruct((t, d), BF16),
        compiler_params=_params("parallel", "parallel"),
    )(q, k, v)


def _attention_backward(q, k, v, do, tq):
    t, d = q.shape
    m = k.shape[0]
    dh = d // XATTN_HEADS
    scale = dh ** -0.5

    def body(q_ref, k_ref, v_ref, do_ref, dq_ref, dk_ref, dv_ref, dk_acc, dv_acc):
        i = pl.program_id(1)
        qv, kv, vv, dov = q_ref[...], k_ref[...], v_ref[...], do_ref[...]
        s = _bdot(qv, kv, tb=True) * scale
        s = s - jnp.max(s, axis=-1, keepdims=True)
        e = jnp.exp(s)
        p = e / jnp.sum(e, axis=-1, keepdims=True)
        dp = _bdot(dov, vv, tb=True)
        ds = p * (dp - jnp.sum(dp * p, axis=-1, keepdims=True)) * scale
        dq_ref[...] = _bdot(ds, kv).astype(BF16)
        dv_part = _bdot(p, dov, ta=True)
        dk_part = _bdot(ds, qv, ta=True)

        @pl.when(i == 0)
        def _():
            dk_acc[...] = dk_part
            dv_acc[...] = dv_part

        @pl.when(i > 0)
        def _():
            dk_acc[...] += dk_part
            dv_acc[...] += dv_part

        @pl.when(i == pl.num_programs(1) - 1)
        def _():
            dk_ref[...] = dk_acc[...].astype(BF16)
            dv_ref[...] = dv_acc[...].astype(BF16)

    qblk = pl.BlockSpec((tq, dh), lambda h, i: (i, h))
    kblk = pl.BlockSpec((m, dh), lambda h, i: (0, h))
    return pl.pallas_call(
        body, name="xattn_bwd", grid=(XATTN_HEADS, t // tq),
        in_specs=[qblk, kblk, kblk, qblk],
        out_specs=[qblk, kblk, kblk],
        out_shape=[jax.ShapeDtypeStruct((t, d), BF16), jax.ShapeDtypeStruct((m, d), BF16),
                   jax.ShapeDtypeStruct((m, d), BF16)],
        scratch_shapes=[pltpu.VMEM((m, dh), F32), pltpu.VMEM((m, dh), F32)],
        compiler_params=_params("parallel", "arbitrary"),
    )(q, k, v, do)


def _ln_backward_rows(name, dmain, dres, xhat, rstd, gamma, tm):
    t, d = xhat.shape

    def body(m_ref, r_ref, x_ref, s_ref, g_ref, du_ref, dub_ref, dg_ref, db_ref):
        du, dg, db = _ln_backward_math(m_ref[...] + ALPHA * r_ref[...], x_ref[...], s_ref[...], g_ref[...])
        du_ref[...] = du
        dub_ref[...] = du.astype(BF16)
        first = pl.program_id(0) == 0

        @pl.when(first)
        def _():
            dg_ref[...] = dg
            db_ref[...] = db

        @pl.when(jnp.logical_not(first))
        def _():
            dg_ref[...] += dg
            db_ref[...] += db

    row = pl.BlockSpec((tm, d), lambda i: (i, 0))
    vec = pl.BlockSpec((1, d), lambda i: (0, 0))
    return pl.pallas_call(
        body, name=name, grid=(t // tm,),
        in_specs=[row, row, row, pl.BlockSpec((tm, 1), lambda i: (i, 0)), vec],
        out_specs=[row, row, vec, vec],
        out_shape=[jax.ShapeDtypeStruct((t, d), F32), jax.ShapeDtypeStruct((t, d), BF16),
                   jax.ShapeDtypeStruct((1, d), F32), jax.ShapeDtypeStruct((1, d), F32)],
        compiler_params=_params("arbitrary"),
    )(dmain, dres, xhat, rstd, gamma)


def _loss_and_ln_backward(xhat, rstd, gamma, beta, target, tm):
    t, d = xhat.shape

    def body(x_ref, r_ref, g_ref, b_ref, t_ref, du_ref, dub_ref, dg_ref, db_ref, loss_ref):
        xh = x_ref[...]
        g = g_ref[...]
        diff = xh * g + b_ref[...] - t_ref[...]
        part = jnp.sum(jnp.sum(diff * diff, axis=1, keepdims=True), axis=0, keepdims=True) * (0.5 / d)
        dy = diff * (1.0 / d)
        du, dg, db = _ln_backward_math(dy, xh, r_ref[...], g)
        du_ref[...] = du
        dub_ref[...] = du.astype(BF16)
        lossrow = jnp.broadcast_to(part, (1, HEAD_DIM))
        first = pl.program_id(0) == 0

        @pl.when(first)
        def _():
            dg_ref[...] = dg
            db_ref[...] = db
            loss_ref[...] = lossrow

        @pl.when(jnp.logical_not(first))
        def _():
            dg_ref[...] += dg
            db_ref[...] += db
            loss_ref[...] += lossrow

    row = pl.BlockSpec((tm, d), lambda i: (i, 0))
    vec = pl.BlockSpec((1, d), lambda i: (0, 0))
    return pl.pallas_call(
        body, name="loss_ln3_bwd", grid=(t // tm,),
        in_specs=[row, pl.BlockSpec((tm, 1), lambda i: (i, 0)), vec, vec, row],
        out_specs=[row, row, vec, vec, pl.BlockSpec((1, HEAD_DIM), lambda i: (0, 0))],
        out_shape=[jax.ShapeDtypeStruct((t, d), F32), jax.ShapeDtypeStruct((t, d), BF16),
                   jax.ShapeDtypeStruct((1, d), F32), jax.ShapeDtypeStruct((1, d), F32),
                   jax.ShapeDtypeStruct((1, HEAD_DIM), F32)],
        compiler_params=_params("arbitrary"),
    )(xhat, rstd, gamma, beta, target)


def _after(token, a):
    return a if token is None else a + token[:1, :1].astype(a.dtype)


def _pick(n, prefs):
    for p in prefs:
        if n % p == 0:
            return p
    return n


def _local_step(x, mem, target, w, x_bf=None):
    t, d = x.shape
    heads = w["a_log"].shape[1]
    gw = heads * HEAD_DIM
    groups, cg, _ = w["pool_w"].shape
    pw = groups * cg
    n_main = 4 * gw + pw
    in_cols = n_main + 2 * heads
    s_in = w["w_in_t"].shape[0]

    tm = _pick(t, (512, 256, 128))
    tm_ln = _pick(t, (256, 128))
    tm_big = _pick(t, (1024, 512, 256, 128))
    tk = _pick(d, K_STEPS)

    w_in_t = w["w_in_t"].reshape(in_cols, d)
    w_p_t = w_in_t[4 * gw + 2 * heads:]
    w_ba_t = jnp.pad(w_in_t[4 * gw:4 * gw + 2 * heads], ((0, HEAD_DIM - 2 * heads), (0, 0)))
    x_bf = x.astype(BF16) if x_bf is None else x_bf
    mem_bf = mem.astype(BF16)

    tn_d = _pick(d, (1024, 512, 256, 128))
    proj = _plain("proj_main", x_bf, w_in_t, tb=True, n_used=4 * gw, tm=tm_big, tn=_pick(4 * gw, (1024, 512, 256, 128)),
                  tk=tk, out_dtype=F32)
    pproj = _plain("proj_pool", x_bf, w_p_t, tb=True, tm=tm_big, tn=_pick(pw, (1024, 512, 256, 128)), tk=tk, out_dtype=F32)
    ea, dtb = _gate_vectors(w["a_log"], w["dt_bias"], heads)
    vec128 = lambda i, j: (0, 0)
    ba, bg = _matmul(
        "proj_gates", x_bf, w_ba_t, tb=True, tm=tm, tn=HEAD_DIM, tk=tk,
        extra=[(ea, (1, HEAD_DIM), vec128), (dtb, (1, HEAD_DIM), vec128)],
        outs=[(jax.ShapeDtypeStruct((t, HEAD_DIM), F32), (tm, HEAD_DIM), _tile)] * 2,
        epilogue=_gates_epilogue(heads))
    qkv = _gdn_pre(proj, w["conv_w"], heads)
    o_gdn, states, *solved = _gdn_core(qkv, bg, heads)
    cat_g = _gdn_post(o_gdn, proj, 3 * gw, w["gdn_norm_w"], heads, tm_big)
    token = yield ("pass", 1, cat_g)
    cat_p = _pool_forward(pproj, 0, w["pool_w"], _after(token, w["pool_scale"]))
    cat = jnp.concatenate([cat_g, cat_p], axis=1)
    w = {**w, **(yield ("weights", 1, cat))}
    h1, h1_bf, xhat1, rstd1 = _ln_forward("mix_ln1", cat, w["w_out"], x, w["ln1_g"], w["ln1_b"], tm=tm_ln, tk=tk)

    h1_bf = _after((yield ("relay", None, h1_bf)), h1_bf)
    q = _plain("xattn_q", h1_bf, w["xq_w"], tm=tm_big, tn=tn_d, tk=tk, out_dtype=BF16)
    mlen = mem.shape[0]
    tm_mem = _pick(mlen, (256, 128))
    k = _plain("xattn_k", mem_bf, w["xk_w"], tm=tm_mem, tn=tn_d, tk=tk, out_dtype=BF16)
    v = _plain("xattn_v", mem_bf, w["xv_w"], tm=tm_mem, tn=tn_d, tk=tk, out_dtype=BF16)
    att = _attention(q, k, v, tm_big)
    h2, h2_bf, xhat2, rstd2 = _ln_forward("xo_ln2", att, w["xo_w"], h1, w["ln2_g"], w["ln2_b"], tm=tm_ln, tk=tk)

    w = {**w, **(yield ("weights", 2, h2_bf))}
    s_up = w["w_up3"].shape[0]
    ff = s_up * w["w_up3"].shape[2]
    tn_f = _pick(ff // s_up, (1024, 512, 256, 128))

    def up_epi(acc, ex, out, i):
        r = jnp.maximum(acc, 0.0)
        out[0][...] = (r * r).astype(BF16)
        out[1][...] = (2.0 * r).astype(BF16)

    act, act_grad = _matmul(
        "mlp_up", h2_bf, w["w_up3"], b_blocks=s_up, tm=tm_big, tn=tn_f, tk=tk,
        outs=[(jax.ShapeDtypeStruct((t, ff), BF16), (tm_big, tn_f), _tile)] * 2, epilogue=up_epi)
    w = {**w, **(yield ("weights", 3, act))}
    tk_f = _pick(ff, K_STEPS)
    xhat3, rstd3 = _ln_forward("down_ln3", act, w["w_down"], h2, w["ln3_g"], w["ln3_b"], tm=tm, tk=tk_f, want_h=False)

    grads = {}
    du3, du3_bf, grads["ln3_g"], grads["ln3_b"], loss = _loss_and_ln_backward(
        xhat3, rstd3, w["ln3_g"], w["ln3_b"], target, tm)

    def dup_epi(acc, ex, out, i):
        out[0][...] = (acc * ex[0][...].astype(F32)).astype(BF16)

    dup = _matmul(
        "mlp_down_dx", du3_bf, w["w_down"], tb=True, tm=tm_big, tn=tn_f, tk=tk,
        extra=[(act_grad, (tm_big, tn_f), _tile)],
        outs=[(jax.ShapeDtypeStruct((t, ff), BF16), (tm_big, tn_f), _tile)], epilogue=dup_epi)[0]
    tk_t = _pick(t, K_STEPS)
    tm_w = _pick(d, (1024, 512, 256, 128))
    grads["w_down"] = _plain("mlp_down_dw", act, du3_bf, ta=True, tm=_pick(ff, (512, 256, 128)), tn=d, tk=tk_t,
                             out_dtype=F32)
    grads["w_up3"] = _plain("mlp_up_dw", h2_bf, dup, ta=True, tm=tm_w, tn=ff // s_up, tk=tk_t, out_dtype=F32, out3=s_up,
                            n_outer=True)
    token = yield ("grads", 0, {n: grads.pop(n) for n in ("w_down", "w_up3")})
    dh2 = _plain("mlp_up_dx", dup, w["w_up3"], tb=True, b_blocks=s_up, tm=tm_big, tn=tn_d,
                 tk=_pick(ff // s_up, K_STEPS), out_dtype=F32)
    du2, du2_bf, grads["ln2_g"], grads["ln2_b"] = _ln_backward_rows(
        "ln2_bwd", dh2, du3, xhat2, rstd2, _after(token, w["ln2_g"]), tm)
    token = yield ("poll", 0, du2_bf)

    grads["xo_w"] = _plain("xo_dw", att, du2_bf, ta=True, tm=tm_w, tn=d, tk=tk_t, out_dtype=F32)
    datt = _plain("xo_dx", du2_bf, w["xo_w"], tb=True, tm=tm_big, tn=tn_d, tk=tk, out_dtype=BF16)
    dq, dk, dv = _attention_backward(q, k, v, datt, tm_big)
    tk_m = _pick(mlen, (256, 128))
    grads["xq_w"] = _plain("xq_dw", h1_bf, dq, ta=True, tm=tm_w, tn=d, tk=tk_t, out_dtype=F32)
    grads["xk_w"] = _plain("xk_dw", mem_bf, dk, ta=True, tm=tm_w, tn=tn_d, tk=tk_m, out_dtype=F32)
    grads["xv_w"] = _plain("xv_dw", mem_bf, dv, ta=True, tm=tm_w, tn=tn_d, tk=tk_m, out_dtype=F32)
    du1, du1_bf, grads["ln1_g"], grads["ln1_b"] = _ln_backward(
        "xq_dx_ln1", dq, w["xq_w"], du2, xhat1, rstd1, _after(token, w["ln1_g"]), tm=tm_ln, tk=tk)

    grads["w_out"] = _plain("out_dw", cat, du1_bf, ta=True, tm=tm_w, tn=d, tk=tk_t, out_dtype=F32)
    token = yield ("grads", 1, {n: grads.pop(n) for n in ("xo_w", "xq_w", "xk_w", "xv_w", "w_out")})
    dcat = _plain("out_dx", du1_bf, w["w_out"], tb=True, tm=tm_big, tn=tn_d, tk=tk, out_dtype=F32)
    dp, grads["pool_w"], grads["pool_scale"] = _pool_backward(dcat, gw, pproj, 0, w["pool_w"],
                                                              _after(token, w["pool_scale"]))
    do_gdn, dz, grads["gdn_norm_w"] = _gdn_post_backward(dcat, o_gdn, proj, 3 * gw, _after(token, w["gdn_norm_w"]),
                                                         heads, tm_big)
    dqkv, dbg = _gdn_core_backward(qkv, bg, states, solved, do_gdn, heads)
    token = yield ("poll", 1, dqkv)
    dqkv_pre, grads["conv_w"] = _gdn_pre_backward(proj, _after(token, w["conv_w"]), dqkv, heads)
    dba, dalog_row, ddt_row = _gates_backward(ba, bg, dbg, ea, dtb, heads)
    grads["a_log"] = dalog_row[:, heads:2 * heads]
    grads["dt_bias"] = ddt_row[:, heads:2 * heads]

    k_pad = -(-in_cols // (2 * HEAD_DIM)) * (2 * HEAD_DIM)
    dproj = jnp.concatenate([dqkv_pre, dz, dba[:, :2 * heads], dp, jnp.zeros((t, k_pad - in_cols), BF16)], axis=1)
    dw_in_t = _plain("proj_dw", dproj, x_bf, ta=True, tm=_pick(k_pad, (512, 256, 128)), tn=d, tk=tk_t, out_dtype=F32,
                     m_kept=in_cols)
    grads["w_in_t"] = dw_in_t.reshape(s_in, in_cols // s_in, d)

    def dx_epi(acc, ex, out, i):
        out[0][...] = acc + ALPHA * ex[0][...]

    token = yield ("grads", 2, {n: grads.pop(n) for n in ("w_in_t", "pool_w")})
    w_in_t_pad = jnp.concatenate([w_in_t, _after(token, jnp.zeros((k_pad - in_cols, d), BF16))], axis=0)
    grad_x = _matmul(
        "proj_dx", dproj, w_in_t_pad, tm=tm, tn=tn_d, tk=k_pad, extra=[(du1, (tm, tn_d), _tile)],
        outs=[(jax.ShapeDtypeStruct((t, d), F32), (tm, tn_d), _tile)], epilogue=dx_epi)[0]
    yield ("poll", 2, grad_x)
    return loss, grad_x, grads


def _adamw(name, w, g, m, v):
    r, c = w.shape
    if r % 8 == 0:
        tr = _pick(r, (256, 128, 64, 32, 16, 8))
        blk, steps = pl.BlockSpec((tr, c), lambda i: (i, 0)), r // tr
    else:
        tc = _pick(c, (256, 128))
        blk, steps = pl.BlockSpec((r, tc), lambda i: (0, i)), c // tc
    c1 = 1.0 - ADAM_B1 ** ADAM_STEP
    c2 = 1.0 - ADAM_B2 ** ADAM_STEP

    def body(w_ref, g_ref, m_ref, v_ref, d_ref, mo_ref, vo_ref, go_ref):
        gv = g_ref[...]
        mn = ADAM_B1 * m_ref[...] + (1.0 - ADAM_B1) * gv
        vn = ADAM_B2 * v_ref[...] + (1.0 - ADAM_B2) * (gv * gv)
        d_ref[...] = -ADAM_LR * ((mn / c1) / (jnp.sqrt(vn / c2) + ADAM_EPS) + ADAM_WD * w_ref[...])
        mo_ref[...] = mn
        vo_ref[...] = vn
        go_ref[...] = gv

    return pl.pallas_call(
        body, name=name, grid=(steps,), in_specs=[blk] * 4, out_specs=[blk] * 4,
        out_shape=[jax.ShapeDtypeStruct((r, c), F32)] * 4,
        compiler_params=_params("parallel"),
    )(w, g, m, v)


def _place():
    x, y, c = lax.axis_index("x"), lax.axis_index("y"), lax.axis_index("c")
    chips = [(1 - x, y), (x, 1 - y), (1 - x, 1 - y)]
    return x, y, c, chips


HBM = pl.BlockSpec(memory_space=pltpu.HBM)


SEM = pl.BlockSpec(memory_space=pltpu.SEMAPHORE)
ANY = pl.BlockSpec(memory_space=pl.ANY)
EFFECT = pltpu.SideEffectType.DATAFLOW_SIDE_EFFECTING


def _in_hbm(a):
    return pltpu.with_memory_space_constraint(a, pltpu.HBM)


def _remote(src, dst, send_sem, recv_sem, to):
    return pltpu.make_async_remote_copy(src_ref=src, dst_ref=dst, send_sem=send_sem, recv_sem=recv_sem,
                                        device_id=to, device_id_type=MESH)


def _by_rows(rows):
    return rows % 32 == 0


def _half_shape(rows, cols):
    return (rows // 2, cols) if _by_rows(rows) else (rows, cols // 2)


def _half(ref, which, *lead):
    rows, cols = ref.shape[-2:]
    if _by_rows(rows):
        return ref.at[(*lead, pl.ds(which * (rows // 2), rows // 2))]
    return ref.at[(*lead, slice(None), pl.ds(which * (cols // 2), cols // 2))]


def _landed(lands, i, shard_index, which):
    return _half(lands[i], which, shard_index)


def _routes():
    x, y, c, _ = _place()
    first = (jnp.where(c == 0, 1 - x, x), jnp.where(c == 0, y, 1 - y))
    second = (jnp.where(c == 0, x, 1 - x), jnp.where(c == 0, 1 - y, y))
    return first, second, (1 - x, 1 - y)


def _shard_of(chip):
    return 2 * chip[0] + chip[1]


def _gather_start(name, shards, after, relayed=()):
    n = len(shards)
    lands = [lax.empty((N_SHARD,) + s.shape, s.dtype) for s in shards]

    def body(*refs):
        ins, zones = refs[:n], refs[n:2 * n]
        ici_send, ici_recv, own_send, own_recv = refs[2 * n + 1:2 * n + 5]
        token = refs[-1]
        x, y, c, chips = _place()
        me = 2 * x + y
        first, _, _ = _routes()
        for i in range(n):
            if i in relayed:
                _remote(_half(ins[i], c), _landed(zones, i, me, c), ici_send.at[3 * i], ici_recv.at[3 * i],
                        (*first, c)).start()
                continue
            for j, chip in enumerate(chips):
                _remote(_half(ins[i], c), _landed(zones, i, me, c), ici_send.at[3 * i + j],
                        ici_recv.at[3 * i + j], (*chip, c)).start()
        for i in range(n):
            _remote(ins[i], zones[i].at[me], own_send.at[i], own_recv.at[i], (x, y, 1 - c)).start()
        token[...] = jnp.zeros_like(token)

    dma = pltpu.SemaphoreType.DMA
    outs = pl.pallas_call(
        body, name=name,
        in_specs=[HBM] * (2 * n) + [ANY],
        out_shape=(dma((3 * n,)), dma((3 * n,)), dma((n,)), dma((n,)),
                   *[pltpu.HBM(a.shape, a.dtype) for a in shards + lands], jax.ShapeDtypeStruct((8, LANES), F32)),
        out_specs=(SEM, SEM, SEM, SEM, *[HBM] * (2 * n), pl.BlockSpec(memory_space=pltpu.VMEM)),
        input_output_aliases={k: 4 + k for k in range(2 * n)},
        compiler_params=pltpu.CompilerParams(has_side_effects=EFFECT),
    )(*[_in_hbm(a) for a in shards + lands], after)
    sems = dict(zip(("ici_send", "ici_recv", "own_send", "own_recv"), outs[:4]))
    return sems, list(outs[4:4 + n]), list(outs[4 + n:4 + 2 * n]), outs[-1]


def _gather_forward(name, idx, lands, sems, after):
    n = len(idx)

    def body(*refs):
        zones = refs[:n]
        ici_recv = refs[n]
        fwd_send, fwd_recv = refs[n + 2], refs[n + 3]
        x, y, c, chips = _place()
        for k, i in enumerate(idx):
            for j, chip in enumerate(chips):
                half = _landed(zones, k, 2 * chip[0] + chip[1], c)
                _remote(half, half, fwd_send.at[3 * k + j], ici_recv.at[3 * i + j], (*chip, c)).wait_recv()
                _remote(half, half, fwd_send.at[3 * k + j], fwd_recv.at[3 * k + j], (x, y, 1 - c)).start()
        refs[-1][...] = jnp.zeros_like(refs[-1])

    dma = pltpu.SemaphoreType.DMA
    outs = pl.pallas_call(
        body, name=name,
        in_specs=[HBM] * n + [SEM, ANY],
        out_shape=(dma((3 * n,)), dma((3 * n,)), *[pltpu.HBM(a.shape, a.dtype) for a in lands],
                   jax.ShapeDtypeStruct((8, LANES), F32)),
        out_specs=(SEM, SEM, *[HBM] * n, pl.BlockSpec(memory_space=pltpu.VMEM)),
        input_output_aliases={k: 2 + k for k in range(n)},
        compiler_params=pltpu.CompilerParams(has_side_effects=EFFECT),
    )(*lands, sems["ici_recv"], after)
    return (outs[0], outs[1]), list(outs[2:2 + n]), outs[-1]


def _gather_wait(name, idx, shards, lands, sems, fwd, after):
    n = len(idx)

    def body(*refs):
        ins, zones = refs[:n], refs[n:2 * n]
        ici_send, own_send, own_recv, fwd_send, fwd_recv = refs[2 * n:2 * n + 5]
        x, y, c, chips = _place()
        me = 2 * x + y
        for k, i in enumerate(idx):
            mine = _half(ins[k], c)
            for j, chip in enumerate(chips):
                theirs = 2 * chip[0] + chip[1]
                _remote(mine, _landed(zones, k, me, c), ici_send.at[3 * i + j], fwd_recv.at[3 * k + j],
                        (*chip, c)).wait_send()
                sent = _landed(zones, k, theirs, c)
                _remote(sent, sent, fwd_send.at[3 * k + j], fwd_recv.at[3 * k + j], (x, y, 1 - c)).wait_send()
                passed = _landed(zones, k, theirs, 1 - c)
                _remote(passed, passed, fwd_send.at[3 * k + j], fwd_recv.at[3 * k + j], (x, y, 1 - c)).wait_recv()
            own = _remote(ins[k], zones[k].at[me], own_send.at[i], own_recv.at[i], (x, y, 1 - c))
            own.wait_send()
            own.wait_recv()

    outs = pl.pallas_call(
        body, name=name,
        in_specs=[HBM] * (2 * n) + [SEM] * 5 + [ANY],
        out_shape=tuple(pltpu.HBM(a.shape, a.dtype) for a in lands),
        out_specs=tuple([HBM] * n),
        input_output_aliases={n + k: k for k in range(n)},
        compiler_params=pltpu.CompilerParams(has_side_effects=EFFECT),
    )(*shards, *lands, sems["ici_send"], sems["own_send"], sems["own_recv"], fwd[0], fwd[1], after)
    return list(outs)


def _gather_relay(name, idx, shards, lands, sems, after):
    n = len(idx)

    def body(*refs):
        ins, zones, ici_recv = refs[:n], refs[n:2 * n], refs[2 * n]
        relay_send, relay_recv, pass_send, pass_recv = refs[2 * n + 2:2 * n + 6]
        x, y, c, _ = _place()
        first, second, _ = _routes()
        for k, i in enumerate(idx):
            landed = _landed(zones, k, _shard_of(first), c)
            _remote(landed, landed, pass_send.at[k], ici_recv.at[3 * i], (*first, c)).wait_recv()
            _remote(_half(ins[k], c), _landed(zones, k, 2 * x + y, c), relay_send.at[2 * k], relay_recv.at[2 * k],
                    (*second, c)).start()
            _remote(landed, landed, relay_send.at[2 * k + 1], relay_recv.at[2 * k + 1], (*second, c)).start()
            _remote(landed, landed, pass_send.at[k], pass_recv.at[k], (x, y, 1 - c)).start()
        refs[-1][...] = jnp.zeros_like(refs[-1])

    dma = pltpu.SemaphoreType.DMA
    outs = pl.pallas_call(
        body, name=name,
        in_specs=[HBM] * (2 * n) + [SEM, ANY],
        out_shape=(dma((2 * n,)), dma((2 * n,)), dma((n,)), dma((n,)), *[pltpu.HBM(a.shape, a.dtype) for a in lands],
                   jax.ShapeDtypeStruct((8, LANES), F32)),
        out_specs=(SEM, SEM, SEM, SEM, *[HBM] * n, pl.BlockSpec(memory_space=pltpu.VMEM)),
        input_output_aliases={n + k: 4 + k for k in range(n)},
        compiler_params=pltpu.CompilerParams(has_side_effects=EFFECT),
    )(*shards, *lands, sems["ici_recv"], after)
    return outs[:4], list(outs[4:4 + n]), outs[-1]


def _gather_forward_relayed(name, ks, lands, relay, after):
    n = len(ks)

    def body(*refs):
        zones, relay_recv = refs[:n], refs[n]
        fwd_send, fwd_recv = refs[n + 2], refs[n + 3]
        x, y, c, _ = _place()
        _, second, diagonal = _routes()
        for p, k in enumerate(ks):
            for j, chip in enumerate((second, diagonal)):
                landed = _landed(zones, p, _shard_of(chip), c)
                _remote(landed, landed, fwd_send.at[2 * p + j], relay_recv.at[2 * k + j], (*second, c)).wait_recv()
                _remote(landed, landed, fwd_send.at[2 * p + j], fwd_recv.at[2 * p + j], (x, y, 1 - c)).start()

    dma = pltpu.SemaphoreType.DMA
    outs = pl.pallas_call(
        body, name=name,
        in_specs=[HBM] * n + [SEM, ANY],
        out_shape=(dma((2 * n,)), dma((2 * n,)), *[pltpu.HBM(a.shape, a.dtype) for a in lands]),
        out_specs=(SEM, SEM, *[HBM] * n),
        input_output_aliases={k: 2 + k for k in range(n)},
        compiler_params=pltpu.CompilerParams(has_side_effects=EFFECT),
    )(*lands, relay[1], after)
    return (outs[0], outs[1]), list(outs[2:])


def _gather_wait_relayed(name, idx, ks, shards, lands, sems, relay, fwd, after):
    n = len(idx)

    def body(*refs):
        ins, zones = refs[:n], refs[n:2 * n]
        ici_send, own_send, own_recv, relay_send, pass_send, pass_recv, fwd_send, fwd_recv = refs[2 * n:2 * n + 8]
        x, y, c, _ = _place()
        me = 2 * x + y
        sibling = (x, y, 1 - c)
        first, second, diagonal = _routes()
        for p, (i, k) in enumerate(zip(idx, ks)):
            mine, at_peer = _half(ins[p], c), _landed(zones, p, me, c)
            from_first = _landed(zones, p, _shard_of(first), c)
            _remote(mine, at_peer, ici_send.at[3 * i], pass_recv.at[k], (*first, c)).wait_send()
            _remote(mine, at_peer, relay_send.at[2 * k], pass_recv.at[k], (*second, c)).wait_send()
            _remote(from_first, from_first, relay_send.at[2 * k + 1], pass_recv.at[k], (*second, c)).wait_send()
            _remote(from_first, from_first, pass_send.at[k], pass_recv.at[k], sibling).wait_send()
            theirs = _landed(zones, p, _shard_of(second), 1 - c)
            _remote(theirs, theirs, pass_send.at[k], pass_recv.at[k], sibling).wait_recv()
            for j, (sent, got) in enumerate(((second, first), (diagonal, diagonal))):
                out_half = _landed(zones, p, _shard_of(sent), c)
                _remote(out_half, out_half, fwd_send.at[2 * p + j], fwd_recv.at[2 * p + j], sibling).wait_send()
                in_half = _landed(zones, p, _shard_of(got), 1 - c)
                _remote(in_half, in_half, fwd_send.at[2 * p + j], fwd_recv.at[2 * p + j], sibling).wait_recv()
            own = _remote(ins[p], zones[p].at[me], own_send.at[i], own_recv.at[i], sibling)
            own.wait_send()
            own.wait_recv()

    outs = pl.pallas_call(
        body, name=name,
        in_specs=[HBM] * (2 * n) + [SEM] * 8 + [ANY],
        out_shape=tuple(pltpu.HBM(a.shape, a.dtype) for a in lands),
        out_specs=tuple([HBM] * n),
        input_output_aliases={n + k: k for k in range(n)},
        compiler_params=pltpu.CompilerParams(has_side_effects=EFFECT),
    )(*shards, *lands, sems["ici_send"], sems["own_send"], sems["own_recv"], relay[0], relay[2], relay[3],
      fwd[0], fwd[1], after)
    return list(outs)


def _all_reduce_small(name, slab, after=None):
    r, width = slab.shape
    ndev = 8

    def body(x_ref, after_ref, out_ref, buf, send_sems, recv_sems):
        x, y, c, _ = _place()
        me = 4 * x + 2 * y + c
        buf[me] = x_ref[...]
        copies = []
        for k in range(1, ndev):
            peer = jnp.bitwise_xor(me, k)
            to = (peer // 4, (peer // 2) % 2, peer % 2)
            cp = pltpu.make_async_remote_copy(src_ref=x_ref, dst_ref=buf.at[me], send_sem=send_sems.at[k - 1],
                                              recv_sem=recv_sems.at[k - 1], device_id=to, device_id_type=MESH)
            cp.start()
            copies.append(cp)
        for k in range(1, ndev):
            peer = jnp.bitwise_xor(me, k)
            pltpu.make_async_remote_copy(src_ref=x_ref, dst_ref=buf.at[peer], send_sem=send_sems.at[k - 1],
                                         recv_sem=recv_sems.at[k - 1], device_id=(x, y, c),
                                         device_id_type=MESH).wait_recv()
        for cp in copies:
            cp.wait_send()
        total = buf[0]
        for d in range(1, ndev):
            total = total + buf[d]
        out_ref[...] = total

    return pl.pallas_call(
        body, name=name,
        in_specs=[pl.BlockSpec(memory_space=pltpu.VMEM), ANY], out_specs=pl.BlockSpec(memory_space=pltpu.VMEM),
        out_shape=jax.ShapeDtypeStruct((r, width), F32),
        scratch_shapes=[pltpu.VMEM((ndev, r, width), F32), pltpu.SemaphoreType.DMA((ndev - 1,)),
                        pltpu.SemaphoreType.DMA((ndev - 1,))],
        compiler_params=pltpu.CompilerParams(vmem_limit_bytes=VMEM_LIMIT),
    )(slab, slab if after is None else after)


def _half_tiling(rows, cols):
    if _by_rows(rows):
        tr = _pick(rows // 2, (256, 128, 64, 32, 16))
        nb = (rows // 2) // tr
        return (tr, cols), nb, (lambda which, b: (which * nb + b, 0)), (lambda b: (b, 0))
    tc = _pick(cols // 2, (256, 128))
    nb = (cols // 2) // tc
    return (rows, tc), nb, (lambda which, b: (0, which * nb + b)), (lambda b: (0, b))


def _chip_partial(name, grad, other, core):
    s, r, cdim = grad.shape
    blk, nb, whole, within = _half_tiling(r, cdim)

    def body(core_ref, g_ref, o_ref, out_ref):
        out_ref[...] = (g_ref[...] + o_ref[...]).astype(BF16)

    return pl.pallas_call(
        body, name=name,
        grid_spec=pltpu.PrefetchScalarGridSpec(
            num_scalar_prefetch=1, grid=(s, nb),
            in_specs=[pl.BlockSpec((None,) + blk, lambda j, b, core_ref: (j,) + whole(core_ref[0], b)),
                      pl.BlockSpec((None,) + blk, lambda j, b, core_ref: (j,) + within(b))],
            out_specs=pl.BlockSpec((None,) + blk, lambda j, b, core_ref: (j,) + within(b))),
        out_shape=jax.ShapeDtypeStruct((s,) + _half_shape(r, cdim), BF16),
        compiler_params=_params("parallel", "parallel"),
    )(core, grad, other)


def _partial_copies(ins, zones, send_sems, recv_sems):
    x, y, c, chips = _place()
    return [_remote(ins[i].at[2 * chip[0] + chip[1]], zones[i].at[j], send_sems.at[3 * i + j],
                    recv_sems.at[3 * i + j], (*chip, c))
            for i in range(len(ins)) for j, chip in enumerate(chips)]


def _swap_copies(ins, zones, send_sems, recv_sems):
    x, y, c, _ = _place()
    copies = []
    for i in range(len(ins)):
        for s in range(N_SHARD):
            copies.append(_remote(_half(ins[i], 1 - c, s), zones[i].at[s],
                                  send_sems.at[N_SHARD * i + s], recv_sems.at[N_SHARD * i + s], (x, y, 1 - c)))
    return copies


def _exchange_start(name, plan, sources, lands, per_array):
    n = len(sources)
    lands = [lax.empty(shape, dtype) for shape, dtype in lands]

    def body(*refs):
        for cp in plan(refs[:n], refs[n:2 * n], refs[2 * n], refs[2 * n + 1]):
            cp.start()
        refs[-1][...] = jnp.zeros_like(refs[-1])

    dma = pltpu.SemaphoreType.DMA
    outs = pl.pallas_call(
        body, name=name,
        in_specs=[HBM] * (2 * n),
        out_shape=(dma((per_array * n,)), dma((per_array * n,)),
                   *[pltpu.HBM(a.shape, a.dtype) for a in list(sources) + lands], jax.ShapeDtypeStruct((8, LANES), F32)),
        out_specs=(SEM, SEM, *[HBM] * (2 * n), pl.BlockSpec(memory_space=pltpu.VMEM)),
        input_output_aliases={k: 2 + k for k in range(2 * n)},
        compiler_params=pltpu.CompilerParams(has_side_effects=EFFECT),
    )(*[_in_hbm(a) for a in list(sources) + lands])
    return (outs[0], outs[1]), list(outs[2:2 + n]), list(outs[2 + n:2 + 2 * n]), outs[-1]


def _exchange_wait(name, plan, started, after):
    sems, partials, lands, _ = started
    n = len(partials)

    def body(*refs):
        for cp in plan(refs[:n], refs[n:2 * n], refs[2 * n], refs[2 * n + 1]):
            cp.wait_send()
            cp.wait_recv()

    outs = pl.pallas_call(
        body, name=name,
        in_specs=[HBM] * (2 * n) + [SEM, SEM] + [ANY] * len(after),
        out_shape=tuple(pltpu.HBM(a.shape, a.dtype) for a in lands),
        out_specs=tuple([HBM] * n),
        input_output_aliases={n + k: k for k in range(n)},
        compiler_params=pltpu.CompilerParams(has_side_effects=EFFECT),
    )(*partials, *lands, sems[0], sems[1], *after)
    return list(outs)


def _reduce_own(name, grad, other, received, where):
    s, r, cdim = grad.shape
    blk, nb, whole, within = _half_tiling(r, cdim)

    def body(where_ref, g_ref, o_ref, r_ref, out_ref):
        total = g_ref[...] + o_ref[...]
        for j in range(3):
            total = total + r_ref[j].astype(F32)
        out_ref[...] = total

    return pl.pallas_call(
        body, name=name,
        grid_spec=pltpu.PrefetchScalarGridSpec(
            num_scalar_prefetch=1, grid=(nb,),
            in_specs=[pl.BlockSpec((None,) + blk, lambda b, w_ref: (w_ref[0],) + whole(w_ref[1], b)),
                      pl.BlockSpec((None,) + blk, lambda b, w_ref: (w_ref[0],) + within(b)),
                      pl.BlockSpec((3,) + blk, lambda b, w_ref: (0,) + within(b))],
            out_specs=pl.BlockSpec(blk, lambda b, w_ref: whole(w_ref[1], b))),
        out_shape=jax.ShapeDtypeStruct((r, cdim), F32),
        compiler_params=_params("parallel"),
    )(where, grad, other, received)


def _join_start(name, halves):
    n = len(halves)

    def body(*refs):
        bufs, send_sems, recv_sems = refs[:n], refs[n], refs[n + 1]
        x, y, c, _ = _place()
        for i in range(n):
            mine = _half(bufs[i], c)
            _remote(mine, mine, send_sems.at[i], recv_sems.at[i], (x, y, 1 - c)).start()
        refs[-1][...] = jnp.zeros_like(refs[-1])

    dma = pltpu.SemaphoreType.DMA
    outs = pl.pallas_call(
        body, name=name,
        in_specs=[HBM] * n,
        out_shape=(dma((n,)), dma((n,)), *[pltpu.HBM(h.shape, F32) for h in halves], jax.ShapeDtypeStruct((8, LANES), F32)),
        out_specs=(SEM, SEM, *[HBM] * n, pl.BlockSpec(memory_space=pltpu.VMEM)),
        input_output_aliases={k: 2 + k for k in range(n)},
        compiler_params=pltpu.CompilerParams(has_side_effects=EFFECT),
    )(*[_in_hbm(h) for h in halves])
    return (outs[0], outs[1]), list(outs[2:2 + n]), outs[-1]


def _join_wait(name, started, after):
    sems, bufs, _ = started
    n = len(bufs)

    def body(*refs):
        bufs, send_sems, recv_sems = refs[:n], refs[n], refs[n + 1]
        x, y, c, _ = _place()
        for i in range(n):
            mine, theirs = _half(bufs[i], c), _half(bufs[i], 1 - c)
            _remote(mine, mine, send_sems.at[i], recv_sems.at[i], (x, y, 1 - c)).wait_send()
            _remote(theirs, theirs, send_sems.at[i], recv_sems.at[i], (x, y, 1 - c)).wait_recv()

    outs = pl.pallas_call(
        body, name=name,
        in_specs=[HBM] * n + [SEM, SEM] + [ANY] * len(after),
        out_shape=tuple(pltpu.HBM(b.shape, F32) for b in bufs),
        out_specs=tuple([HBM] * n),
        input_output_aliases={k: k for k in range(n)},
        compiler_params=pltpu.CompilerParams(has_side_effects=EFFECT),
    )(*bufs, sems[0], sems[1], *after)
    return list(outs)


BIG = ("w_in", "pool_w", "w_out", "xq_w", "xk_w", "xv_w", "xo_w", "w_up", "w_down", "conv_w")
KEPT_F32 = ("conv_w",)
GATHER_GROUPS = ((0, 1, 9), (2, 3, 4, 5, 6), (7,), (8,))
RELAYED = (7, 8)
SMALL = ("conv_w", "a_log", "dt_bias", "gdn_norm_w", "pool_scale", "ln1_g", "ln1_b", "ln2_g", "ln2_b", "ln3_g", "ln3_b")
ORDER = ("w_in", "conv_w", "a_log", "dt_bias", "gdn_norm_w", "pool_w", "pool_scale", "w_out", "ln1_g", "ln1_b",
         "xq_w", "xk_w", "xv_w", "xo_w", "ln2_g", "ln2_b", "w_up", "w_down", "ln3_g", "ln3_b")
LANES = 128


def _rows(flat_len):
    return -(-flat_len // LANES)


def _pack(pieces):
    out = []
    for p in pieces:
        flat = p.reshape(-1).astype(F32)
        out.append(jnp.pad(flat, (0, _rows(flat.shape[0]) * LANES - flat.shape[0])).reshape(-1, LANES))
    slab = jnp.concatenate(out, axis=0)
    return jnp.pad(slab, ((0, -slab.shape[0] % 8), (0, 0)))


def _unpack(slab, shapes):
    out, row = [], 0
    for shp in shapes:
        size = math.prod(shp)
        out.append(slab[row:row + _rows(size)].reshape(-1)[:size].reshape(shp))
        row += _rows(size)
    return out


TRANSPOSED = ("w_in",)


def _as2d(name, a):
    a = a[0]
    if name in TRANSPOSED:
        return jnp.swapaxes(a, 0, 1)
    return a.reshape(-1, a.shape[-1]) if a.ndim == 3 else a


def _from2d(name, a, shape):
    return (jnp.swapaxes(a, 0, 1) if name in TRANSPOSED else a).reshape(shape)


def kernel(x, mem, w_in, conv_w, a_log, dt_bias, gdn_norm_w, pool_w, pool_scale, w_out, ln1_g, ln1_b, xq_w, xk_w, xv_w, xo_w, ln2_g, ln2_b, w_up, w_down, ln3_g, ln3_b, loss_target, m_w_in, m_conv_w, m_a_log, m_dt_bias, m_gdn_norm_w, m_pool_w, m_pool_scale, m_w_out, m_ln1_g, m_ln1_b, m_xq_w, m_xk_w, m_xv_w, m_xo_w, m_ln2_g, m_ln2_b, m_w_up, m_w_down, m_ln3_g, m_ln3_b, v_w_in, v_conv_w, v_a_log, v_dt_bias, v_gdn_norm_w, v_pool_w, v_pool_scale, v_w_out, v_ln1_g, v_ln1_b, v_xq_w, v_xk_w, v_xv_w, v_xo_w, v_ln2_g, v_ln2_b, v_w_up, v_w_down, v_ln3_g, v_ln3_b):
    given = dict(locals())
    cx, cy, cc = lax.axis_index("x"), lax.axis_index("y"), lax.axis_index("c")
    me = 2 * cx + cy
    groups = pool_w.shape[1]
    cs = pool_w.shape[2]
    kk, conv_cols = conv_w.shape[1], conv_w.shape[2]
    core = cc.astype(jnp.int32).reshape(1)
    where = jnp.stack([me, cc]).astype(jnp.int32)

    started = {}
    wts = {}

    def start(name, idx, after, token=None):
        casts = [_after(token, _as2d(BIG[i], given[BIG[i]])).astype(F32 if BIG[i] in KEPT_F32 else BF16) for i in idx]
        relayed = tuple(k for k, i in enumerate(idx) if i in RELAYED)
        sems, shards, lands, token = _gather_start(name, casts, after, relayed)
        for k, i in enumerate(idx):
            started[i] = (sems, k, shards[k], lands[k])
        return token

    token = start("gather_start_first", GATHER_GROUPS[0], x)
    token = start("gather_start_rest", tuple(i for group in GATHER_GROUPS[1:] for i in group), token, token)

    relay = {}

    def send_on(after):
        members = [started[i] for i in RELAYED]
        relay["sems"], zones, token = _gather_relay("gather_relay", [m[1] for m in members], [m[2] for m in members],
                                                    [m[3] for m in members], members[0][0], after)
        relay["zones"] = dict(zip(RELAYED, zones))
        return token

    passed = {}

    def pass_on(group, after):
        members = [started[i] for i in GATHER_GROUPS[group]]
        fwd, zones, token = _gather_forward(f"gather_forward_{group}", [m[1] for m in members], [m[3] for m in members],
                                            members[0][0], after)
        passed[group] = (fwd, zones)
        return token

    def fetch(group, after):
        members = [started[i] for i in GATHER_GROUPS[group]]
        sems, idx = members[0][0], [m[1] for m in members]
        shards = [m[2] for m in members]
        if GATHER_GROUPS[group][0] in RELAYED:
            ks = [RELAYED.index(i) for i in GATHER_GROUPS[group]]
            zones = [relay["zones"][i] for i in GATHER_GROUPS[group]]
            fwd, zones = _gather_forward_relayed(f"gather_forward_{group}", ks, zones, relay["sems"], after)
            got = _gather_wait_relayed(f"gather_wait_{group}", idx, ks, shards, zones, sems, relay["sems"], fwd, after)
        else:
            if group not in passed:
                pass_on(group, after)
            fwd, zones = passed[group]
            got = _gather_wait(f"gather_wait_{group}", idx, shards, zones, sems, fwd, after)
        full = dict(zip([BIG[i] for i in GATHER_GROUPS[group]], got))
        out = {}
        for n, a in full.items():
            if n == "w_in":
                out["w_in_t"] = a
            elif n == "w_up":
                out["w_up3"] = a
            elif n == "pool_w":
                out[n] = a.reshape(N_SHARD, groups, cs, -1).transpose(1, 0, 2, 3).reshape(groups, N_SHARD * cs, -1)
            elif n == "conv_w":
                out[n] = a.transpose(1, 0, 2).reshape(kk, N_SHARD * conv_cols)
            else:
                out[n] = a.reshape(-1, a.shape[-1])
        return out

    for n in ("a_log", "dt_bias", "gdn_norm_w", "pool_scale", "ln1_g", "ln1_b", "ln2_g", "ln2_b", "ln3_g", "ln3_b"):
        wts[n] = given[n]
    x_bf = _after(token, x[0]).astype(BF16)
    wts.update(fetch(0, x_bf))

    def start_swap(group, grads):
        names, blocks = [], []
        for n, g in grads.items():
            if n == "pool_w":
                g = g.reshape(groups, N_SHARD, cs, -1).transpose(1, 0, 2, 3).reshape(N_SHARD, groups * cs, -1)
            elif g.ndim == 2:
                g = g.reshape(N_SHARD, -1, g.shape[-1])
            names.append({"w_in_t": "w_in", "w_up3": "w_up"}.get(n, n))
            blocks.append(g)
        zones = [((N_SHARD,) + _half_shape(b.shape[1], b.shape[2]), F32) for b in blocks]
        swap = _exchange_start(f"grad_swap_start_{group}", _swap_copies, blocks, zones, N_SHARD)
        return {"group": group, "names": names, "swap": swap, "token": swap[3]}

    def start_send(state, after):
        group, names = state["group"], state["names"]
        state["blocks"] = state["swap"][1]
        state["others"] = _exchange_wait(f"grad_swap_wait_{group}", _swap_copies, state["swap"], after)
        partials = [_chip_partial("chip_partial_" + n, gb, ob, core)
                    for n, gb, ob in zip(names, state["blocks"], state["others"])]
        zones = [((3,) + p.shape[1:], BF16) for p in partials]
        state["send"] = _exchange_start(f"grad_send_start_{group}", _partial_copies, partials, zones, 3)
        state["token"] = state["send"][3]

    grad, delta, new_m, new_v = {}, {}, {}, {}

    def start_join(state, after):
        group, names = state["group"], state["names"]
        received = _exchange_wait(f"grad_send_wait_{group}", _partial_copies, state["send"], after)
        halves = [_reduce_own("reduce_own_" + n, gb, ob, rb, where)
                  for n, gb, ob, rb in zip(names, state["blocks"], state["others"], received)]
        state["join"] = _join_start(f"grad_join_start_{group}", halves)
        return state["join"][2]

    def finish_reduce(state, after):
        group, names = state["group"], state["names"]
        for n, g in zip(names, _join_wait(f"grad_join_wait_{group}", state["join"], after)):
            shp = given[n].shape
            d2, m2, v2, g2 = _adamw("adamw_" + n, _as2d(n, given[n]), g, _as2d(n, given["m_" + n]),
                                    _as2d(n, given["v_" + n]))
            grad[n], delta[n], new_m[n], new_v[n] = (_from2d(n, a, shp) for a in (g2, d2, m2, v2))
        return d2

    step = _local_step(x[0], mem[0], loss_target[0], wts, x_bf)
    pending = {}
    request = next(step)
    while True:
        try:
            kind, group, payload = request
            if kind == "weights":
                request = step.send(fetch(group, payload))
            elif kind == "relay":
                request = step.send(send_on(payload))
            elif kind == "pass":
                request = step.send(pass_on(group, payload))
            elif kind == "grads":
                pending[group] = start_swap(group, payload)
                request = step.send(pending[group]["token"])
            else:
                start_send(pending[group], [payload])
                request = step.send(pending[group]["token"])
        except StopIteration as stop:
            loss_row, grad_x, g = stop.value
            break

    after = [pending[2]["token"], grad_x]
    for group in (0, 1):
        after = [start_join(pending[group], after)]
    for group in (0, 1):
        after = [finish_reduce(pending[group], after)]
    after = [finish_reduce(pending[2], [start_join(pending[2], after)])]

    small_names = ("a_log", "dt_bias", "gdn_norm_w", "pool_scale", "ln1_g", "ln1_b", "ln2_g", "ln2_b", "ln3_g", "ln3_b")
    pieces = [g["conv_w"]] + [g[n] for n in small_names] + [loss_row[:, :1]]
    shapes = [p.shape for p in pieces]
    summed = _unpack(_all_reduce_small("all_reduce_small", _pack(pieces), after[0]), shapes)
    gsmall = dict(zip(small_names, summed[1:-1]))
    gsmall["conv_w"] = lax.dynamic_slice(summed[0], (0, me * conv_cols), (kk, conv_cols))
    loss = summed[-1][0, 0]

    sshapes = [given[n].shape for n in SMALL]
    slabs = [_pack([given[p + n] for n in SMALL]) for p in ("", "m_", "v_")]
    gslab = _pack([gsmall[n] for n in SMALL])
    outs = _adamw("adamw_small", slabs[0], gslab, slabs[1], slabs[2])[:3]
    for dst, slab in zip((delta, new_m, new_v), outs):
        dst.update(zip(SMALL, _unpack(slab, sshapes)))
    for n in SMALL:
        grad[n] = gsmall[n].reshape(given[n].shape)

    return (loss, grad_x[None], *[grad[n] for n in ORDER], *[delta[n] for n in ORDER],
            *[new_m[n] for n in ORDER], *[new_v[n] for n in ORDER])
```
